```python
import math
import jax, jax.numpy as jnp
from jax import lax
import numpy as np

D_MODEL = 1024
BATCH = 8
SEQ = 4096
DEPTH = 1

N_META = 16
MIX_WIDTH = D_MODEL
S5_GROUP_CH = 16
S5_STATE = 64
S5_WIDTH = MIX_WIDTH // 4
S5_GROUPS = S5_WIDTH // S5_GROUP_CH
RET_HEAD_DIM = 128
RET_WIDTH = MIX_WIDTH - S5_WIDTH
RET_HEADS = RET_WIDTH // RET_HEAD_DIM
CHUNK = 128
ROPE_BASE = 10000.0
D_FF = 4 * D_MODEL
LN_EPS = 1e-5
GN_EPS = 1e-5
IN_PROJ_WIDTH = S5_WIDTH + 4 * RET_WIDTH
DEEPNORM_ALPHA = (2.0 * DEPTH) ** 0.25
DEEPNORM_BETA = (8.0 * DEPTH) ** -0.25

kernel_name = "hymba_s5_retnet_deepnorm_layer"


def layer_norm(x, g, b):
    xf = x.astype(jnp.float32)
    mu = jnp.mean(xf, axis=-1, keepdims=True)
    xc = xf - mu
    var = jnp.mean(xc * xc, axis=-1, keepdims=True)
    y = xc * lax.rsqrt(var + LN_EPS)
    return (y * g.astype(jnp.float32) + b.astype(jnp.float32)).astype(x.dtype)


def _complex_linear_combine(e1, e2):
    a1r, a1i, b1r, b1i = e1
    a2r, a2i, b2r, b2i = e2
    ar = a2r * a1r - a2i * a1i
    ai = a2r * a1i + a2i * a1r
    br = a2r * b1r - a2i * b1i + b2r
    bi = a2r * b1i + a2i * b1r + b2i
    return (ar, ai, br, bi)


def s5_mixer(u, lam_re, lam_im, log_dt, b_re, b_im, c_re, c_im, d, w_glu, b_glu):
    bsz, L, _ = u.shape
    ug = u.reshape(bsz, L, S5_GROUPS, S5_GROUP_CH)
    dt = jnp.exp(log_dt)[:, None]
    mag = jnp.exp(lam_re * dt)
    lbr = mag * jnp.cos(lam_im * dt)
    lbi = mag * jnp.sin(lam_im * dt)
    den = lam_re * lam_re + lam_im * lam_im
    nr = lbr - 1.0
    qr = (nr * lam_re + lbi * lam_im) / den
    qi = (lbi * lam_re - nr * lam_im) / den
    bbr = qr[..., None] * b_re - qi[..., None] * b_im
    bbi = qr[..., None] * b_im + qi[..., None] * b_re
    bur = jnp.einsum('blgh,gph->blgp', ug, bbr)
    bui = jnp.einsum('blgh,gph->blgp', ug, bbi)
    ar = jnp.broadcast_to(lbr[None, None], (1, L, S5_GROUPS, S5_STATE))
    ai = jnp.broadcast_to(lbi[None, None], (1, L, S5_GROUPS, S5_STATE))
    _, _, xr, xi = lax.associative_scan(_complex_linear_combine, (ar, ai, bur, bui), axis=1)
    y = jnp.einsum('blgp,ghp->blgh', xr, c_re) - jnp.einsum('blgp,ghp->blgh', xi, c_im)
    y = y.reshape(bsz, L, S5_WIDTH) + d * u
    y = jax.nn.gelu(y)
    return y * jax.nn.sigmoid(y @ w_glu + b_glu)


def _rotate(x, cos, sin):
    half = x.shape[-1] // 2
    x1, x2 = x[..., :half], x[..., half:]
    return jnp.concatenate([x1 * cos - x2 * sin, x1 * sin + x2 * cos], axis=-1)


def retention_mixer(q, k, v, g, gn_g, gn_b):
    bsz, L, _ = q.shape
    dtype = q.dtype
    q = q.reshape(bsz, L, RET_HEADS, RET_HEAD_DIM)
    k = k.reshape(bsz, L, RET_HEADS, RET_HEAD_DIM)
    v = v.reshape(bsz, L, RET_HEADS, RET_HEAD_DIM)
    pos = jnp.arange(L, dtype=jnp.float32)
    inv_freq = 1.0 / (ROPE_BASE ** (jnp.arange(0, RET_HEAD_DIM, 2, dtype=jnp.float32) / RET_HEAD_DIM))
    ang = pos[:, None] * inv_freq[None, :]
    cos = jnp.cos(ang)[None, :, None, :].astype(dtype)
    sin = jnp.sin(ang)[None, :, None, :].astype(dtype)
    q = _rotate(q, cos, sin)
    k = _rotate(k, cos, sin) * (RET_HEAD_DIM ** -0.5)

    pad = CHUNK - N_META
    n_chunks = (L + pad) // CHUNK

    def to_chunks(t):
        t = jnp.pad(t, ((0, 0), (pad, 0), (0, 0), (0, 0)))
        return t.reshape(bsz, n_chunks, CHUNK, RET_HEADS, RET_HEAD_DIM).transpose(1, 0, 3, 2, 4)

    qc, kc, vc = to_chunks(q), to_chunks(k), to_chunks(v)

    log_gamma = jnp.log1p(-jnp.exp2(-5.0 - jnp.arange(RET_HEADS, dtype=jnp.float32)))
    idx = jnp.arange(CHUNK, dtype=jnp.float32)
    diff = idx[:, None] - idx[None, :]
    dmat = jnp.where(diff[None] >= 0,
                     jnp.exp(jnp.maximum(diff, 0.0)[None] * log_gamma[:, None, None]),
                     0.0).astype(dtype)
    zeta = jnp.exp((CHUNK - 1.0 - idx)[None] * log_gamma[:, None]).astype(dtype)
    xi = jnp.exp((idx + 1.0)[None] * log_gamma[:, None]).astype(dtype)
    gamma_chunk = jnp.exp(CHUNK * log_gamma).astype(dtype)

    def step(state, inp):
        qb, kb, vb = inp
        scores = jnp.einsum('bhid,bhjd->bhij', qb, kb) * dmat[None]
        inner = jnp.einsum('bhij,bhje->bhie', scores, vb)
        cross = jnp.einsum('bhid,bhde->bhie', qb, state) * xi[None, :, :, None]
        new_state = (gamma_chunk[None, :, None, None] * state
                     + jnp.einsum('bhjd,bhje->bhde', kb * zeta[None, :, :, None], vb))
        return new_state, inner + cross

    state0 = jnp.zeros((bsz, RET_HEADS, RET_HEAD_DIM, RET_HEAD_DIM), dtype=dtype)
    _, out = lax.scan(step, state0, (qc, kc, vc))
    out = out.transpose(1, 0, 3, 2, 4).reshape(bsz, n_chunks * CHUNK, RET_HEADS, RET_HEAD_DIM)[:, pad:]

    of = out.astype(jnp.float32)
    mu = jnp.mean(of, axis=-1, keepdims=True)
    oc = of - mu
    var = jnp.mean(oc * oc, axis=-1, keepdims=True)
    on = (oc * lax.rsqrt(var + GN_EPS)).reshape(bsz, L, RET_WIDTH)
    on = (on * gn_g.astype(jnp.float32) + gn_b.astype(jnp.float32)).astype(dtype)
    return jax.nn.silu(g) * on


def _fwd_setup_inputs(seed: int = 0) -> dict:
    key = jax.random.key(seed)
    ks = jax.random.split(key, 24)
    f32 = jnp.float32
    nrm = lambda k, s, sc: jax.random.normal(k, s, f32) * sc
    P, G, H = S5_STATE, S5_GROUPS, S5_GROUP_CH
    x = jax.random.normal(ks[0], (BATCH, SEQ, D_MODEL), f32)
    meta_tokens = nrm(ks[1], (N_META, D_MODEL), 1.0)
    ln_in_g = 1.0 + nrm(ks[2], (D_MODEL,), 0.02)
    ln_in_b = nrm(ks[3], (D_MODEL,), 0.02)
    w_in = nrm(ks[4], (DEPTH, D_MODEL, IN_PROJ_WIDTH), D_MODEL ** -0.5)
    s5_lambda_re = -0.5 + nrm(ks[5], (DEPTH, G, P), 0.01)
    s5_lambda_im = math.pi * jnp.broadcast_to(jnp.arange(P, dtype=f32), (DEPTH, G, P)) + nrm(ks[6], (DEPTH, G, P), 0.01)
    s5_log_dt = jax.random.uniform(ks[7], (DEPTH, G), f32, math.log(1e-3), math.log(1e-1))
    s5_b_re = nrm(ks[8], (DEPTH, G, P, H), (2.0 * H) ** -0.5)
    s5_b_im = nrm(ks[9], (DEPTH, G, P, H), (2.0 * H) ** -0.5)
    s5_c_re = nrm(ks[10], (DEPTH, G, H, P), (2.0 * P) ** -0.5)
    s5_c_im = nrm(ks[11], (DEPTH, G, H, P), (2.0 * P) ** -0.5)
    s5_d = nrm(ks[12], (DEPTH, S5_WIDTH), 1.0)
    s5_w_glu = nrm(ks[13], (DEPTH, S5_WIDTH, S5_WIDTH), S5_WIDTH ** -0.5)
    s5_b_glu = nrm(ks[14], (DEPTH, S5_WIDTH), 0.01)
    ret_gn_g = 1.0 + nrm(ks[15], (DEPTH, RET_WIDTH), 0.02)
    ret_gn_b = nrm(ks[16], (DEPTH, RET_WIDTH), 0.02)
    w_out = nrm(ks[17], (DEPTH, MIX_WIDTH, D_MODEL), MIX_WIDTH ** -0.5 * DEEPNORM_BETA)
    ln1_g = 1.0 + nrm(ks[18], (DEPTH, D_MODEL), 0.02)
    ln1_b = nrm(ks[19], (DEPTH, D_MODEL), 0.02)
    w_up = nrm(ks[20], (DEPTH, D_MODEL, D_FF), D_MODEL ** -0.5)
    w_down = nrm(ks[21], (DEPTH, D_FF, D_MODEL), D_FF ** -0.5 * DEEPNORM_BETA)
    ln2_g = 1.0 + nrm(ks[22], (DEPTH, D_MODEL), 0.02)
    ln2_b = nrm(ks[23], (DEPTH, D_MODEL), 0.02)
    return {"x": x, "meta_tokens": meta_tokens, "ln_in_g": ln_in_g, "ln_in_b": ln_in_b,
            "w_in": w_in, "s5_lambda_re": s5_lambda_re, "s5_lambda_im": s5_lambda_im,
            "s5_log_dt": s5_log_dt, "s5_b_re": s5_b_re, "s5_b_im": s5_b_im,
            "s5_c_re": s5_c_re, "s5_c_im": s5_c_im, "s5_d": s5_d, "s5_w_glu": s5_w_glu,
            "s5_b_glu": s5_b_glu, "ret_gn_g": ret_gn_g, "ret_gn_b": ret_gn_b, "w_out": w_out,
            "ln1_g": ln1_g, "ln1_b": ln1_b, "w_up": w_up, "w_down": w_down,
            "ln2_g": ln2_g, "ln2_b": ln2_b}


def _fwd_reference(x, meta_tokens, ln_in_g, ln_in_b, w_in, s5_lambda_re, s5_lambda_im, s5_log_dt,
              s5_b_re, s5_b_im, s5_c_re, s5_c_im, s5_d, s5_w_glu, s5_b_glu, ret_gn_g, ret_gn_b,
              w_out, ln1_g, ln1_b, w_up, w_down, ln2_g, ln2_b):
    bsz = x.shape[0]
    meta = jnp.broadcast_to(meta_tokens[None].astype(x.dtype), (bsz, N_META, D_MODEL))
    h = jnp.concatenate([meta, x], axis=1)
    h = layer_norm(h, ln_in_g, ln_in_b)
    splits = [S5_WIDTH, S5_WIDTH + RET_WIDTH, S5_WIDTH + 2 * RET_WIDTH, S5_WIDTH + 3 * RET_WIDTH]
    for l in range(DEPTH):
        proj = h @ w_in[l]
        u, q, k, v, g = jnp.split(proj, splits, axis=-1)
        y_s5 = s5_mixer(u, s5_lambda_re[l], s5_lambda_im[l], s5_log_dt[l], s5_b_re[l], s5_b_im[l],
                        s5_c_re[l], s5_c_im[l], s5_d[l], s5_w_glu[l], s5_b_glu[l])
        y_ret = retention_mixer(q, k, v, g, ret_gn_g[l], ret_gn_b[l])
        mixed = jnp.concatenate([y_s5, y_ret], axis=-1) @ w_out[l]
        h = layer_norm(DEEPNORM_ALPHA * h + mixed, ln1_g[l], ln1_b[l])
        ff = jnp.square(jax.nn.relu(h @ w_up[l])) @ w_down[l]
        h = layer_norm(DEEPNORM_ALPHA * h + ff, ln2_g[l], ln2_b[l])
    return h[:, N_META:]


import jax as _jax
import jax.numpy as _jnp

TWIN_FORMAT = 'train_step'
FWD_PARAMS = ['x', 'meta_tokens', 'ln_in_g', 'ln_in_b', 'w_in', 's5_lambda_re', 's5_lambda_im', 's5_log_dt', 's5_b_re', 's5_b_im', 's5_c_re', 's5_c_im', 's5_d', 's5_w_glu', 's5_b_glu', 'ret_gn_g', 'ret_gn_b', 'w_out', 'ln1_g', 'ln1_b', 'w_up', 'w_down', 'ln2_g', 'ln2_b']
TWIN_WEIGHTS = ['meta_tokens', 'ln_in_g', 'ln_in_b', 'w_in', 's5_lambda_re', 's5_lambda_im', 's5_log_dt', 's5_b_re', 's5_b_im', 's5_c_re', 's5_c_im', 's5_d', 's5_w_glu', 's5_b_glu', 'ret_gn_g', 'ret_gn_b', 'w_out', 'ln1_g', 'ln1_b', 'w_up', 'w_down', 'ln2_g', 'ln2_b']
TWIN_DIFF_INPUT = 'x'
TWIN_INPUTS = ['x', 'meta_tokens', 'ln_in_g', 'ln_in_b', 'w_in', 's5_lambda_re', 's5_lambda_im', 's5_log_dt', 's5_b_re', 's5_b_im', 's5_c_re', 's5_c_im', 's5_d', 's5_w_glu', 's5_b_glu', 'ret_gn_g', 'ret_gn_b', 'w_out', 'ln1_g', 'ln1_b', 'w_up', 'w_down', 'ln2_g', 'ln2_b', 'loss_target', 'm_meta_tokens', 'm_ln_in_g', 'm_ln_in_b', 'm_w_in', 'm_s5_lambda_re', 'm_s5_lambda_im', 'm_s5_log_dt', 'm_s5_b_re', 'm_s5_b_im', 'm_s5_c_re', 'm_s5_c_im', 'm_s5_d', 'm_s5_w_glu', 'm_s5_b_glu', 'm_ret_gn_g', 'm_ret_gn_b', 'm_w_out', 'm_ln1_g', 'm_ln1_b', 'm_w_up', 'm_w_down', 'm_ln2_g', 'm_ln2_b', 'v_meta_tokens', 'v_ln_in_g', 'v_ln_in_b', 'v_w_in', 'v_s5_lambda_re', 'v_s5_lambda_im', 'v_s5_log_dt', 'v_s5_b_re', 'v_s5_b_im', 'v_s5_c_re', 'v_s5_c_im', 'v_s5_d', 'v_s5_w_glu', 'v_s5_b_glu', 'v_ret_gn_g', 'v_ret_gn_b', 'v_w_out', 'v_ln1_g', 'v_ln1_b', 'v_w_up', 'v_w_down', 'v_ln2_g', 'v_ln2_b']
TWIN_OUTPUTS = ['loss', 'grad_x', 'grad_meta_tokens', 'grad_ln_in_g', 'grad_ln_in_b', 'grad_w_in', 'grad_s5_lambda_re', 'grad_s5_lambda_im', 'grad_s5_log_dt', 'grad_s5_b_re', 'grad_s5_b_im', 'grad_s5_c_re', 'grad_s5_c_im', 'grad_s5_d', 'grad_s5_w_glu', 'grad_s5_b_glu', 'grad_ret_gn_g', 'grad_ret_gn_b', 'grad_w_out', 'grad_ln1_g', 'grad_ln1_b', 'grad_w_up', 'grad_w_down', 'grad_ln2_g', 'grad_ln2_b', 'delta_meta_tokens', 'delta_ln_in_g', 'delta_ln_in_b', 'delta_w_in', 'delta_s5_lambda_re', 'delta_s5_lambda_im', 'delta_s5_log_dt', 'delta_s5_b_re', 'delta_s5_b_im', 'delta_s5_c_re', 'delta_s5_c_im', 'delta_s5_d', 'delta_s5_w_glu', 'delta_s5_b_glu', 'delta_ret_gn_g', 'delta_ret_gn_b', 'delta_w_out', 'delta_ln1_g', 'delta_ln1_b', 'delta_w_up', 'delta_w_down', 'delta_ln2_g', 'delta_ln2_b', 'new_m_meta_tokens', 'new_m_ln_in_g', 'new_m_ln_in_b', 'new_m_w_in', 'new_m_s5_lambda_re', 'new_m_s5_lambda_im', 'new_m_s5_log_dt', 'new_m_s5_b_re', 'new_m_s5_b_im', 'new_m_s5_c_re', 'new_m_s5_c_im', 'new_m_s5_d', 'new_m_s5_w_glu', 'new_m_s5_b_glu', 'new_m_ret_gn_g', 'new_m_ret_gn_b', 'new_m_w_out', 'new_m_ln1_g', 'new_m_ln1_b', 'new_m_w_up', 'new_m_w_down', 'new_m_ln2_g', 'new_m_ln2_b', 'new_v_meta_tokens', 'new_v_ln_in_g', 'new_v_ln_in_b', 'new_v_w_in', 'new_v_s5_lambda_re', 'new_v_s5_lambda_im', 'new_v_s5_log_dt', 'new_v_s5_b_re', 'new_v_s5_b_im', 'new_v_s5_c_re', 'new_v_s5_c_im', 'new_v_s5_d', 'new_v_s5_w_glu', 'new_v_s5_b_glu', 'new_v_ret_gn_g', 'new_v_ret_gn_b', 'new_v_w_out', 'new_v_ln1_g', 'new_v_ln1_b', 'new_v_w_up', 'new_v_w_down', 'new_v_ln2_g', 'new_v_ln2_b']
TWIN_LEAF_KINDS = {'loss': 'loss', 'grad_x': 'grad_x', 'grad_meta_tokens': 'grad_w', 'grad_ln_in_g': 'grad_w', 'grad_ln_in_b': 'grad_w', 'grad_w_in': 'grad_w', 'grad_s5_lambda_re': 'grad_w', 'grad_s5_lambda_im': 'grad_w', 'grad_s5_log_dt': 'grad_w', 'grad_s5_b_re': 'grad_w', 'grad_s5_b_im': 'grad_w', 'grad_s5_c_re': 'grad_w', 'grad_s5_c_im': 'grad_w', 'grad_s5_d': 'grad_w', 'grad_s5_w_glu': 'grad_w', 'grad_s5_b_glu': 'grad_w', 'grad_ret_gn_g': 'grad_w', 'grad_ret_gn_b': 'grad_w', 'grad_w_out': 'grad_w', 'grad_ln1_g': 'grad_w', 'grad_ln1_b': 'grad_w', 'grad_w_up': 'grad_w', 'grad_w_down': 'grad_w', 'grad_ln2_g': 'grad_w', 'grad_ln2_b': 'grad_w', 'delta_meta_tokens': 'delta_w', 'delta_ln_in_g': 'delta_w', 'delta_ln_in_b': 'delta_w', 'delta_w_in': 'delta_w', 'delta_s5_lambda_re': 'delta_w', 'delta_s5_lambda_im': 'delta_w', 'delta_s5_log_dt': 'delta_w', 'delta_s5_b_re': 'delta_w', 'delta_s5_b_im': 'delta_w', 'delta_s5_c_re': 'delta_w', 'delta_s5_c_im': 'delta_w', 'delta_s5_d': 'delta_w', 'delta_s5_w_glu': 'delta_w', 'delta_s5_b_glu': 'delta_w', 'delta_ret_gn_g': 'delta_w', 'delta_ret_gn_b': 'delta_w', 'delta_w_out': 'delta_w', 'delta_ln1_g': 'delta_w', 'delta_ln1_b': 'delta_w', 'delta_w_up': 'delta_w', 'delta_w_down': 'delta_w', 'delta_ln2_g': 'delta_w', 'delta_ln2_b': 'delta_w', 'new_m_meta_tokens': 'new_m', 'new_m_ln_in_g': 'new_m', 'new_m_ln_in_b': 'new_m', 'new_m_w_in': 'new_m', 'new_m_s5_lambda_re': 'new_m', 'new_m_s5_lambda_im': 'new_m', 'new_m_s5_log_dt': 'new_m', 'new_m_s5_b_re': 'new_m', 'new_m_s5_b_im': 'new_m', 'new_m_s5_c_re': 'new_m', 'new_m_s5_c_im': 'new_m', 'new_m_s5_d': 'new_m', 'new_m_s5_w_glu': 'new_m', 'new_m_s5_b_glu': 'new_m', 'new_m_ret_gn_g': 'new_m', 'new_m_ret_gn_b': 'new_m', 'new_m_w_out': 'new_m', 'new_m_ln1_g': 'new_m', 'new_m_ln1_b': 'new_m', 'new_m_w_up': 'new_m', 'new_m_w_down': 'new_m', 'new_m_ln2_g': 'new_m', 'new_m_ln2_b': 'new_m', 'new_v_meta_tokens': 'new_v', 'new_v_ln_in_g': 'new_v', 'new_v_ln_in_b': 'new_v', 'new_v_w_in': 'new_v', 'new_v_s5_lambda_re': 'new_v', 'new_v_s5_lambda_im': 'new_v', 'new_v_s5_log_dt': 'new_v', 'new_v_s5_b_re': 'new_v', 'new_v_s5_b_im': 'new_v', 'new_v_s5_c_re': 'new_v', 'new_v_s5_c_im': 'new_v', 'new_v_s5_d': 'new_v', 'new_v_s5_w_glu': 'new_v', 'new_v_s5_b_glu': 'new_v', 'new_v_ret_gn_g': 'new_v', 'new_v_ret_gn_b': 'new_v', 'new_v_w_out': 'new_v', 'new_v_ln1_g': 'new_v', 'new_v_ln1_b': 'new_v', 'new_v_w_up': 'new_v', 'new_v_w_down': 'new_v', 'new_v_ln2_g': 'new_v', 'new_v_ln2_b': 'new_v'}


def _forward(args):
    return _fwd_reference(*[args[k] for k in FWD_PARAMS])


def _output_shape():
    out = _jax.eval_shape(lambda: _forward(_fwd_setup_inputs(0)))
    return out.shape, out.dtype

N_MICROBATCH = 1
ADAM_LR = 0.001
ADAM_B1 = 0.9
ADAM_B2 = 0.999
ADAM_EPS = 1e-08
ADAM_WD = 0.01
ADAM_STEP = 10
PER_EXAMPLE_BATCH_AXIS = {'x': 0, 'loss_target': 0}
SHARED_INPUTS = []
_WEIGHT_DTYPES = {'meta_tokens': _jnp.float32, 'ln_in_g': _jnp.float32, 'ln_in_b': _jnp.float32, 'w_in': _jnp.float32, 's5_lambda_re': _jnp.float32, 's5_lambda_im': _jnp.float32, 's5_log_dt': _jnp.float32, 's5_b_re': _jnp.float32, 's5_b_im': _jnp.float32, 's5_c_re': _jnp.float32, 's5_c_im': _jnp.float32, 's5_d': _jnp.float32, 's5_w_glu': _jnp.float32, 's5_b_glu': _jnp.float32, 'ret_gn_g': _jnp.float32, 'ret_gn_b': _jnp.float32, 'w_out': _jnp.float32, 'ln1_g': _jnp.float32, 'ln1_b': _jnp.float32, 'w_up': _jnp.float32, 'w_down': _jnp.float32, 'ln2_g': _jnp.float32, 'ln2_b': _jnp.float32}
MOMENT_SCALE = {'meta_tokens': 5.410152e-03, 'ln_in_g': 6.107941e-01, 'ln_in_b': 4.229788e-01, 'w_in': 5.281936e-02, 's5_lambda_re': 1.193988e-03, 's5_lambda_im': 1.628862e-03, 's5_log_dt': 1.588777e+00, 's5_b_re': 1.028346e-03, 's5_b_im': 1.026924e-03, 's5_c_re': 2.048356e-03, 's5_c_im': 1.947466e-03, 's5_d': 4.401831e-02, 's5_w_glu': 9.261555e-03, 's5_b_glu': 1.394917e-02, 'ret_gn_g': 5.402433e-02, 'ret_gn_b': 6.853250e-02, 'w_out': 8.334100e-02, 'ln1_g': 7.004830e-01, 'ln1_b': 4.189188e-01, 'w_up': 5.358506e-02, 'w_down': 2.069237e-01, 'ln2_g': 3.207363e+01, 'ln2_b': 6.951057e+00}


def _to_microbatches(a, axis):
    t = _jnp.moveaxis(a, axis, 0)
    t = t.reshape((N_MICROBATCH, t.shape[0] // N_MICROBATCH) + t.shape[1:])
    return _jnp.moveaxis(t, 1, axis + 1)


def setup_inputs(seed: int = 0) -> dict:
    inp = _fwd_setup_inputs(seed)
    key = _jax.random.fold_in(_jax.random.key(seed), 7919)
    shape, _ = _output_shape()
    out = dict(inp)
    out["loss_target"] = _jax.random.normal(_jax.random.fold_in(key, 0), shape, _jnp.float32)
    for i, name in enumerate(TWIN_WEIGHTS):
        w = inp[name].astype(_jnp.float32)
        if MOMENT_SCALE is None:
            s = _jnp.sqrt(_jnp.mean(_jnp.square(w)) + 1e-30)
        else:
            s = MOMENT_SCALE[name]
        km, kv = _jax.random.split(_jax.random.fold_in(key, i + 1))
        out[name] = w
        out["m_" + name] = s * _jax.random.normal(km, w.shape, _jnp.float32)
        out["v_" + name] = (s * s) * _jax.random.uniform(kv, w.shape, _jnp.float32, 0.5, 1.5)
    if N_MICROBATCH > 1:
        for name, axis in PER_EXAMPLE_BATCH_AXIS.items():
            out[name] = _to_microbatches(out[name], axis)
    return {'x': out['x'], 'meta_tokens': out['meta_tokens'], 'ln_in_g': out['ln_in_g'], 'ln_in_b': out['ln_in_b'], 'w_in': out['w_in'], 's5_lambda_re': out['s5_lambda_re'], 's5_lambda_im': out['s5_lambda_im'], 's5_log_dt': out['s5_log_dt'], 's5_b_re': out['s5_b_re'], 's5_b_im': out['s5_b_im'], 's5_c_re': out['s5_c_re'], 's5_c_im': out['s5_c_im'], 's5_d': out['s5_d'], 's5_w_glu': out['s5_w_glu'], 's5_b_glu': out['s5_b_glu'], 'ret_gn_g': out['ret_gn_g'], 'ret_gn_b': out['ret_gn_b'], 'w_out': out['w_out'], 'ln1_g': out['ln1_g'], 'ln1_b': out['ln1_b'], 'w_up': out['w_up'], 'w_down': out['w_down'], 'ln2_g': out['ln2_g'], 'ln2_b': out['ln2_b'], 'loss_target': out['loss_target'], 'm_meta_tokens': out['m_meta_tokens'], 'm_ln_in_g': out['m_ln_in_g'], 'm_ln_in_b': out['m_ln_in_b'], 'm_w_in': out['m_w_in'], 'm_s5_lambda_re': out['m_s5_lambda_re'], 'm_s5_lambda_im': out['m_s5_lambda_im'], 'm_s5_log_dt': out['m_s5_log_dt'], 'm_s5_b_re': out['m_s5_b_re'], 'm_s5_b_im': out['m_s5_b_im'], 'm_s5_c_re': out['m_s5_c_re'], 'm_s5_c_im': out['m_s5_c_im'], 'm_s5_d': out['m_s5_d'], 'm_s5_w_glu': out['m_s5_w_glu'], 'm_s5_b_glu': out['m_s5_b_glu'], 'm_ret_gn_g': out['m_ret_gn_g'], 'm_ret_gn_b': out['m_ret_gn_b'], 'm_w_out': out['m_w_out'], 'm_ln1_g': out['m_ln1_g'], 'm_ln1_b': out['m_ln1_b'], 'm_w_up': out['m_w_up'], 'm_w_down': out['m_w_down'], 'm_ln2_g': out['m_ln2_g'], 'm_ln2_b': out['m_ln2_b'], 'v_meta_tokens': out['v_meta_tokens'], 'v_ln_in_g': out['v_ln_in_g'], 'v_ln_in_b': out['v_ln_in_b'], 'v_w_in': out['v_w_in'], 'v_s5_lambda_re': out['v_s5_lambda_re'], 'v_s5_lambda_im': out['v_s5_lambda_im'], 'v_s5_log_dt': out['v_s5_log_dt'], 'v_s5_b_re': out['v_s5_b_re'], 'v_s5_b_im': out['v_s5_b_im'], 'v_s5_c_re': out['v_s5_c_re'], 'v_s5_c_im': out['v_s5_c_im'], 'v_s5_d': out['v_s5_d'], 'v_s5_w_glu': out['v_s5_w_glu'], 'v_s5_b_glu': out['v_s5_b_glu'], 'v_ret_gn_g': out['v_ret_gn_g'], 'v_ret_gn_b': out['v_ret_gn_b'], 'v_w_out': out['v_w_out'], 'v_ln1_g': out['v_ln1_g'], 'v_ln1_b': out['v_ln1_b'], 'v_w_up': out['v_w_up'], 'v_w_down': out['v_w_down'], 'v_ln2_g': out['v_ln2_g'], 'v_ln2_b': out['v_ln2_b']}


def _loss(weights, diff, rest, loss_target):
    with _jax.named_scope("forward"):
        args = {**rest, TWIN_DIFF_INPUT: diff, **{k: w.astype(_WEIGHT_DTYPES[k]) for k, w in weights.items()}}
        y = _forward(args)
    with _jax.named_scope("loss_head"):
        err = _jnp.square(y.astype(_jnp.float32) - loss_target)
        return 0.5 * _jnp.sum(_jnp.mean(err, axis=-1)) if err.ndim else 0.5 * err


def _adamw(w, g, m, v):
    m = ADAM_B1 * m + (1.0 - ADAM_B1) * g
    v = ADAM_B2 * v + (1.0 - ADAM_B2) * _jnp.square(g)
    m_hat = m / (1.0 - ADAM_B1 ** ADAM_STEP)
    v_hat = v / (1.0 - ADAM_B2 ** ADAM_STEP)
    delta = -ADAM_LR * (m_hat / (_jnp.sqrt(v_hat) + ADAM_EPS) + ADAM_WD * w)
    return delta, m, v


def reference(x, meta_tokens, ln_in_g, ln_in_b, w_in, s5_lambda_re, s5_lambda_im, s5_log_dt, s5_b_re, s5_b_im, s5_c_re, s5_c_im, s5_d, s5_w_glu, s5_b_glu, ret_gn_g, ret_gn_b, w_out, ln1_g, ln1_b, w_up, w_down, ln2_g, ln2_b, loss_target, m_meta_tokens, m_ln_in_g, m_ln_in_b, m_w_in, m_s5_lambda_re, m_s5_lambda_im, m_s5_log_dt, m_s5_b_re, m_s5_b_im, m_s5_c_re, m_s5_c_im, m_s5_d, m_s5_w_glu, m_s5_b_glu, m_ret_gn_g, m_ret_gn_b, m_w_out, m_ln1_g, m_ln1_b, m_w_up, m_w_down, m_ln2_g, m_ln2_b, v_meta_tokens, v_ln_in_g, v_ln_in_b, v_w_in, v_s5_lambda_re, v_s5_lambda_im, v_s5_log_dt, v_s5_b_re, v_s5_b_im, v_s5_c_re, v_s5_c_im, v_s5_d, v_s5_w_glu, v_s5_b_glu, v_ret_gn_g, v_ret_gn_b, v_w_out, v_ln1_g, v_ln1_b, v_w_up, v_w_down, v_ln2_g, v_ln2_b):
    given = dict(x=x, meta_tokens=meta_tokens, ln_in_g=ln_in_g, ln_in_b=ln_in_b, w_in=w_in, s5_lambda_re=s5_lambda_re, s5_lambda_im=s5_lambda_im, s5_log_dt=s5_log_dt, s5_b_re=s5_b_re, s5_b_im=s5_b_im, s5_c_re=s5_c_re, s5_c_im=s5_c_im, s5_d=s5_d, s5_w_glu=s5_w_glu, s5_b_glu=s5_b_glu, ret_gn_g=ret_gn_g, ret_gn_b=ret_gn_b, w_out=w_out, ln1_g=ln1_g, ln1_b=ln1_b, w_up=w_up, w_down=w_down, ln2_g=ln2_g, ln2_b=ln2_b, loss_target=loss_target, m_meta_tokens=m_meta_tokens, m_ln_in_g=m_ln_in_g, m_ln_in_b=m_ln_in_b, m_w_in=m_w_in, m_s5_lambda_re=m_s5_lambda_re, m_s5_lambda_im=m_s5_lambda_im, m_s5_log_dt=m_s5_log_dt, m_s5_b_re=m_s5_b_re, m_s5_b_im=m_s5_b_im, m_s5_c_re=m_s5_c_re, m_s5_c_im=m_s5_c_im, m_s5_d=m_s5_d, m_s5_w_glu=m_s5_w_glu, m_s5_b_glu=m_s5_b_glu, m_ret_gn_g=m_ret_gn_g, m_ret_gn_b=m_ret_gn_b, m_w_out=m_w_out, m_ln1_g=m_ln1_g, m_ln1_b=m_ln1_b, m_w_up=m_w_up, m_w_down=m_w_down, m_ln2_g=m_ln2_g, m_ln2_b=m_ln2_b, v_meta_tokens=v_meta_tokens, v_ln_in_g=v_ln_in_g, v_ln_in_b=v_ln_in_b, v_w_in=v_w_in, v_s5_lambda_re=v_s5_lambda_re, v_s5_lambda_im=v_s5_lambda_im, v_s5_log_dt=v_s5_log_dt, v_s5_b_re=v_s5_b_re, v_s5_b_im=v_s5_b_im, v_s5_c_re=v_s5_c_re, v_s5_c_im=v_s5_c_im, v_s5_d=v_s5_d, v_s5_w_glu=v_s5_w_glu, v_s5_b_glu=v_s5_b_glu, v_ret_gn_g=v_ret_gn_g, v_ret_gn_b=v_ret_gn_b, v_w_out=v_w_out, v_ln1_g=v_ln1_g, v_ln1_b=v_ln1_b, v_w_up=v_w_up, v_w_down=v_w_down, v_ln2_g=v_ln2_g, v_ln2_b=v_ln2_b)
    weights = {n: given[n] for n in TWIN_WEIGHTS}
    shared = {n: given[n] for n in SHARED_INPUTS}
    per_example = {n: given[n] for n in ['x']}
    grad_fn = _jax.value_and_grad(_loss, argnums=(0, 1))

    def one_microbatch(ex, loss_target):
        ex = dict(ex)
        diff = ex.pop(TWIN_DIFF_INPUT)
        return grad_fn(weights, diff, {**shared, **ex}, loss_target)

    if N_MICROBATCH == 1:
        loss, (grad_w, grad_x) = one_microbatch(per_example, given["loss_target"])
    else:
        def body(carry, xs):
            loss_sum, grad_sum = carry
            l_k, (gw_k, gx_k) = one_microbatch(xs[0], xs[1])
            with _jax.named_scope("update"):
                return (loss_sum + l_k, _jax.tree.map(_jnp.add, grad_sum, gw_k)), gx_k

        init = (_jnp.zeros((), _jnp.float32), _jax.tree.map(_jnp.zeros_like, weights))
        (loss, grad_w), grad_x = _jax.lax.scan(body, init, (per_example, given["loss_target"]))
    with _jax.named_scope("update"):
        delta_w, new_m, new_v = {}, {}, {}
        for n in TWIN_WEIGHTS:
            delta_w[n], new_m[n], new_v[n] = _adamw(weights[n], grad_w[n], given["m_" + n], given["v_" + n])
    return (loss, grad_x, *[grad_w[n] for n in TWIN_WEIGHTS], *[delta_w[n] for n in TWIN_WEIGHTS],
            *[new_m[n] for n in TWIN_WEIGHTS], *[new_v[n] for n in TWIN_WEIGHTS])
```

```python
import math

import jax
import jax.numpy as jnp
from jax import lax
from jax.experimental import pallas as pl
from jax.experimental.pallas import tpu as pltpu

F32 = jnp.float32
MM = jnp.bfloat16

D_MODEL = 1024
N_META = 16
CHUNK = 128
PAD = CHUNK - N_META
S5_W, S5_G, S5_H, S5_P = 256, 16, 16, 64
S5_N = S5_G * S5_P
RET_W, RET_H, HEAD = 768, 6, 128
D_FF = 4096
PROJ_W = S5_W + 4 * RET_W
N_DEV = 8
FF_BLK = D_FF // N_DEV
ROW_BLK = 384
ALPHA = 2.0 ** 0.25
LN_EPS = 1e-5
GN_EPS = 1e-5
ROPE_BASE = 10000.0
GELU_C = math.sqrt(2.0 / math.pi)
GELU_A = 0.044715
ADAM_LR, ADAM_B1, ADAM_B2, ADAM_EPS, ADAM_WD, ADAM_STEP = 0.001, 0.9, 0.999, 1e-08, 0.01, 10
VMEM_LIMIT = 60 * 1024 * 1024

_VMEM = pl.BlockSpec(memory_space=pltpu.VMEM)
_ANY = pl.BlockSpec(memory_space=pl.ANY)
_MESH = pl.DeviceIdType.MESH


def _params(sem=None):
    return pltpu.CompilerParams(dimension_semantics=sem, vmem_limit_bytes=VMEM_LIMIT)


def _dot(a, b):
    return jnp.dot(a.astype(MM), b.astype(MM), preferred_element_type=F32)


def _dot_nt(a, b):
    return lax.dot_general(a.astype(MM), b.astype(MM), (((1,), (1,)), ((), ())), preferred_element_type=F32)


def _dot_tn(a, b):
    return lax.dot_general(a.astype(MM), b.astype(MM), (((0,), (0,)), ((), ())), preferred_element_type=F32)


def _split3(a):
    hi = a.astype(jnp.bfloat16)
    r1 = a - hi.astype(F32)
    mid = r1.astype(jnp.bfloat16)
    lo = (r1 - mid.astype(F32)).astype(jnp.bfloat16)
    return hi, mid, lo


def _dot_sel_rhs(a, sel):
    s = sel.astype(jnp.bfloat16)
    return sum(jnp.dot(p, s, preferred_element_type=F32) for p in _split3(a))


def _dot_sel_lhs(sel, b):
    s = sel.astype(jnp.bfloat16)
    return sum(jnp.dot(s, p, preferred_element_type=F32) for p in _split3(b))


def _ln_fwd(r, eps):
    mu = jnp.mean(r, axis=-1, keepdims=True)
    xc = r - mu
    var = jnp.mean(xc * xc, axis=-1, keepdims=True)
    rstd = lax.rsqrt(var + eps)
    return xc * rstd, rstd


def _ln_bwd(dxhat, xhat, rstd):
    m1 = jnp.mean(dxhat, axis=-1, keepdims=True)
    m2 = jnp.mean(dxhat * xhat, axis=-1, keepdims=True)
    return rstd * (dxhat - m1 - xhat * m2)


def _colsum(a):
    return jnp.sum(a, axis=0, keepdims=True)


def _shift3(n_in):
    return [pl.BlockSpec((CHUNK, D_MODEL), (lambda i, j=j: (jnp.clip(3 * i - 1 + j, 0, n_in - 1), 0))) for j in range(3)]


def _in_proj(x2d, meta_full, ln_g, ln_b, w_int, cos2, sin2):
    seq = x2d.shape[0]
    tp = seq + CHUNK
    R = ROW_BLK

    def body(xa, xb, xc, meta_ref, g_ref, b_ref, w_ref, cos_ref, sin_ref,
             xhat_ref, rstd_ref, u_ref, q_ref, k_ref, v_ref, gate_ref, raw_ref):
        i = pl.program_id(0)
        raw_ref[0:CHUNK, :] = xa[...]
        raw_ref[CHUNK:2 * CHUNK, :] = xb[...]
        raw_ref[2 * CHUNK:3 * CHUNK, :] = xc[...]

        @pl.when(i == 0)
        def _():
            raw_ref[0:PAD, :] = jnp.zeros((PAD, D_MODEL), F32)
            raw_ref[PAD:CHUNK, :] = meta_ref[...]

        xhat, rstd = _ln_fwd(raw_ref[...], LN_EPS)
        xhat_ref[...] = xhat
        rstd_ref[...] = rstd
        hb = (xhat * g_ref[...] + b_ref[...]).astype(MM)
        valid = (i * R + lax.broadcasted_iota(jnp.int32, (R, 1), 0)) >= PAD

        def seg(lo, hi):
            return jnp.where(valid, _dot_nt(hb, w_ref[lo:hi, :]), 0.0)

        u_ref[...] = seg(0, S5_W)
        cos = cos_ref[...]
        sin = sin_ref[...]
        q = seg(S5_W, S5_W + RET_W)
        k = seg(S5_W + RET_W, S5_W + 2 * RET_W)
        for h in range(RET_H):
            sl = slice(h * HEAD, (h + 1) * HEAD)
            qh = q[:, sl]
            kh = k[:, sl]
            q_ref[:, sl] = (qh * cos + pltpu.roll(qh, HEAD // 2, 1) * sin).astype(q_ref.dtype)
            k_ref[:, sl] = ((kh * cos + pltpu.roll(kh, HEAD // 2, 1) * sin) * (HEAD ** -0.5)).astype(k_ref.dtype)
        v_ref[...] = seg(S5_W + 2 * RET_W, S5_W + 3 * RET_W).astype(v_ref.dtype)
        gate_ref[...] = seg(S5_W + 3 * RET_W, PROJ_W)

    def rows(w, dt):
        return pl.BlockSpec((R, w), lambda i: (i, 0)), jax.ShapeDtypeStruct((tp, w), dt)

    outs = [rows(D_MODEL, F32), rows(1, F32), rows(S5_W, F32), rows(RET_W, MM), rows(RET_W, MM),
            rows(RET_W, MM), rows(RET_W, F32)]
    full = lambda s: pl.BlockSpec(s, lambda i: (0,) * len(s))
    return pl.pallas_call(
        body, name="in_proj", grid=(tp // R,),
        in_specs=_shift3(seq // CHUNK) + [full((N_META, D_MODEL)), full((1, D_MODEL)), full((1, D_MODEL)), _VMEM,
                                          pl.BlockSpec((R, HEAD), lambda i: (i, 0)), pl.BlockSpec((R, HEAD), lambda i: (i, 0))],
        out_specs=[o[0] for o in outs], out_shape=[o[1] for o in outs],
        scratch_shapes=[pltpu.VMEM((R, D_MODEL), F32)],
        compiler_params=_params(("arbitrary",)),
    )(x2d, x2d, x2d, meta_full, ln_g, ln_b, w_int, cos2, sin2)


def _s5_disc(lre, lim, ldt, bre_t, bim_t):
    dt = jnp.exp(ldt)
    mag = jnp.exp(lre * dt)
    ang = lim * dt
    lbr = mag * jnp.cos(ang)
    lbi = mag * jnp.sin(ang)
    den = lre * lre + lim * lim
    nr = lbr - 1.0
    qr = (nr * lre + lbi * lim) / den
    qi = (lbi * lre - nr * lim) / den
    return lbr, lbi, qr * bre_t - qi * bim_t, qr * bim_t + qi * bre_t


def _s5_tables(lbr, lbi, reverse):
    if reverse:
        lbi = -lbi
    pw = [(lbr, lbi)]
    for _ in range(7):
        r, i = pw[-1]
        pw.append((r * lbr - i * lbi, r * lbi + i * lbr))
    row = lax.broadcasted_iota(jnp.int32, (8, S5_N), 0)
    tabs = []
    for k in range(3):
        sh = 2 ** k
        mask = (row < 8 - sh) if reverse else (row >= sh)
        ar, ai = pw[sh - 1]
        tabs.append((jnp.where(mask, ar, 0.0), jnp.where(mask, ai, 0.0)))
    pr = jnp.zeros((8, S5_N), F32)
    pi = jnp.zeros((8, S5_N), F32)
    for i in range(8):
        ar, ai = pw[7 - i] if reverse else pw[i]
        pr = jnp.where(row == i, ar, pr)
        pi = jnp.where(row == i, ai, pi)
    tabs.append((pr, pi))
    return tabs


def _store_tables(tab_ref, tabs):
    for k, (r, i) in enumerate(tabs):
        tab_ref[2 * k] = r
        tab_ref[2 * k + 1] = i


def _bd_mask():
    r = lax.broadcasted_iota(jnp.int32, (S5_W, S5_N), 0)
    c = lax.broadcasted_iota(jnp.int32, (S5_W, S5_N), 1)
    return jnp.right_shift(r, 4) == jnp.right_shift(c, 6)


def _s5_block_diag(bbr_t, bbi_t, cre_w, cim_w):
    mask = _bd_mask()
    bd = lambda t: jnp.where(mask, t, 0.0)
    return (bd(jnp.tile(bbr_t, (S5_G, 1))), bd(jnp.tile(bbi_t, (S5_G, 1))),
            bd(jnp.tile(cre_w, (1, S5_N // HEAD))), bd(jnp.tile(cim_w, (1, S5_N // HEAD))))


def _scan8(xr, xi, tab_ref, lanes, reverse):
    for k in range(3):
        sh = (8 - 2 ** k) if reverse else 2 ** k
        sr = pltpu.roll(xr, sh, 0)
        si = pltpu.roll(xi, sh, 0)
        mr = tab_ref[2 * k, :, lanes]
        mi = tab_ref[2 * k + 1, :, lanes]
        xr, xi = xr + (mr * sr - mi * si), xi + (mr * si + mi * sr)
    return xr, xi


S5_LANES = 256


def _gelu(y):
    t = jnp.tanh(GELU_C * (y + GELU_A * y * y * y))
    return 0.5 * y * (1.0 + t), t


def _s5_fwd(u, lre, lim, ldt, bre_t, bim_t, cre_w, cim_w, d_row, w_glu, b_glu):
    tp = u.shape[0]
    R = ROW_BLK

    def body(u_ref, lre_ref, lim_ref, ldt_ref, bre_ref, bim_ref, cre_ref, cim_ref, d_ref, wg_ref, bg_ref,
             y_ref, xr_ref, xi_ref, bbd_r, bbd_i, cbd_r, cbd_i, tab_ref, car_r, car_i):
        @pl.when(pl.program_id(0) == 0)
        def _():
            lbr, lbi, bbr, bbi = _s5_disc(lre_ref[...], lim_ref[...], ldt_ref[...], bre_ref[...], bim_ref[...])
            br, bi, cr, ci = _s5_block_diag(bbr, bbi, cre_ref[...], cim_ref[...])
            bbd_r[...] = br.astype(MM)
            bbd_i[...] = bi.astype(MM)
            cbd_r[...] = cr.astype(MM)
            cbd_i[...] = ci.astype(MM)
            _store_tables(tab_ref, _s5_tables(lbr, lbi, False))
            car_r[...] = jnp.zeros_like(car_r)
            car_i[...] = jnp.zeros_like(car_i)

        u = u_ref[...]
        ub = u.astype(MM)
        xr_ref[...] = jnp.dot(ub, bbd_r[...], preferred_element_type=F32)
        xi_ref[...] = jnp.dot(ub, bbd_i[...], preferred_element_type=F32)
        for j in range(S5_N // S5_LANES):
            lanes = pl.ds(j * S5_LANES, S5_LANES)
            pr = tab_ref[6, :, lanes]
            pi = tab_ref[7, :, lanes]

            def step(g, carry):
                cr, ci = carry
                rows = pl.ds(pl.multiple_of(g * 8, 8), 8)
                xr, xi = _scan8(xr_ref[rows, lanes], xi_ref[rows, lanes], tab_ref, lanes, False)
                br = jnp.broadcast_to(cr[7:8, :], cr.shape)
                bi = jnp.broadcast_to(ci[7:8, :], ci.shape)
                xr = xr + (pr * br - pi * bi)
                xi = xi + (pr * bi + pi * br)
                xr_ref[rows, lanes] = xr
                xi_ref[rows, lanes] = xi
                return xr, xi

            cr, ci = lax.fori_loop(0, R // 8, step, (car_r[:, lanes], car_i[:, lanes]), unroll=2)
            car_r[:, lanes] = cr
            car_i[:, lanes] = ci
        y = _dot_nt(xr_ref[...], cbd_r[...]) - _dot_nt(xi_ref[...], cbd_i[...]) + d_ref[...] * u
        yg, _ = _gelu(y)
        z = _dot(yg, wg_ref[...]) + bg_ref[...]
        y_ref[...] = yg * jax.nn.sigmoid(z)

    full = lambda a: pl.BlockSpec(a.shape, lambda i: (0,) * a.ndim)
    small = [lre, lim, ldt, bre_t, bim_t, cre_w, cim_w, d_row, w_glu, b_glu]
    return pl.pallas_call(
        body, name="s5_fwd", grid=(tp // R,),
        in_specs=[pl.BlockSpec((R, S5_W), lambda i: (i, 0))] + [full(a) for a in small],
        out_specs=[pl.BlockSpec((R, S5_W), lambda i: (i, 0)), pl.BlockSpec((R, S5_N), lambda i: (i, 0)),
                   pl.BlockSpec((R, S5_N), lambda i: (i, 0))],
        out_shape=[jax.ShapeDtypeStruct((tp, S5_W), F32), jax.ShapeDtypeStruct((tp, S5_N), F32),
                   jax.ShapeDtypeStruct((tp, S5_N), F32)],
        scratch_shapes=[pltpu.VMEM((S5_W, S5_N), MM)] * 4 + [pltpu.VMEM((8, 8, S5_N), F32),
                                                            pltpu.VMEM((8, S5_N), F32), pltpu.VMEM((8, S5_N), F32)],
        compiler_params=_params(("arbitrary",)),
    )(u, *small)


def _s5_bwd(dy_out, u, xr, xi, lre, lim, ldt, bre_t, bim_t, cre_w, cim_w, d_row, w_glu, b_glu):
    tp = u.shape[0]
    R = ROW_BLK
    nb = tp // R

    def body(dyo_ref, u_ref, xr_ref, xi_ref, xpr_ref, xpi_ref,
             lre_ref, lim_ref, ldt_ref, bre_ref, bim_ref, cre_ref, cim_ref, d_ref, wg_ref, bg_ref,
             du_ref, dlre_ref, dlim_ref, dldt_ref, dbre_ref, dbim_ref, dcre_ref, dcim_ref, dd_ref, dwg_ref, dbg_ref,
             bbd_r, bbd_i, cbd_r, cbd_i, tab_ref, car_r, car_i, gr_ref, gi_ref, xer_ref, xei_ref,
             abr, abi, acr, aci, adr, adi):
        i = pl.program_id(0)

        @pl.when(i == 0)
        def _():
            lbr, lbi, bbr, bbi = _s5_disc(lre_ref[...], lim_ref[...], ldt_ref[...], bre_ref[...], bim_ref[...])
            br, bi, cr, ci = _s5_block_diag(bbr, bbi, cre_ref[...], cim_ref[...])
            bbd_r[...] = br.astype(MM)
            bbd_i[...] = bi.astype(MM)
            cbd_r[...] = cr.astype(MM)
            cbd_i[...] = ci.astype(MM)
            _store_tables(tab_ref, _s5_tables(lbr, lbi, True))
            for ref in (car_r, car_i, abr, abi, acr, aci, adr, adi, dd_ref, dwg_ref, dbg_ref):
                ref[...] = jnp.zeros_like(ref)

        u = u_ref[...]
        xrv = xr_ref[...]
        xiv = xi_ref[...]
        y = _dot_nt(xrv, cbd_r[...]) - _dot_nt(xiv, cbd_i[...]) + d_ref[...] * u
        yg, t = _gelu(y)
        z = _dot(yg, wg_ref[...]) + bg_ref[...]
        s = jax.nn.sigmoid(z)
        dout = dyo_ref[...]
        dz = dout * yg * s * (1.0 - s)
        dyg = dout * s + _dot_nt(dz, wg_ref[...])
        dwg_ref[...] += _dot_tn(yg, dz)
        dbg_ref[...] += _colsum(dz)
        dy = dyg * (0.5 * (1.0 + t) + 0.5 * y * (1.0 - t * t) * GELU_C * (1.0 + 3.0 * GELU_A * y * y))
        dd_ref[...] += _colsum(dy * u)
        acr[...] += _dot_tn(dy, xrv)
        aci[...] -= _dot_tn(dy, xiv)
        gr_ref[...] = _dot(dy, cbd_r[...])
        gi_ref[...] = -_dot(dy, cbd_i[...])
        has_prev = (i < nb - 1).astype(F32)
        xer_ref[0:8, :] = xpr_ref[...] * has_prev
        xei_ref[0:8, :] = xpi_ref[...] * has_prev
        xer_ref[8:R + 8, :] = xrv
        xei_ref[8:R + 8, :] = xiv
        row = lax.broadcasted_iota(jnp.int32, (8, S5_LANES), 0)
        for j in range(S5_N // S5_LANES):
            lanes = pl.ds(j * S5_LANES, S5_LANES)
            pr = tab_ref[6, :, lanes]
            pi = tab_ref[7, :, lanes]

            def step(n, carry):
                cr, ci, sar, sai = carry
                g = R // 8 - 1 - n
                r0 = pl.multiple_of(g * 8, 8)
                rows = pl.ds(r0, 8)
                gr, gi = _scan8(gr_ref[rows, lanes], gi_ref[rows, lanes], tab_ref, lanes, True)
                br = jnp.broadcast_to(cr[0:1, :], cr.shape)
                bi = jnp.broadcast_to(ci[0:1, :], ci.shape)
                gr = gr + (pr * br - pi * bi)
                gi = gi + (pr * bi + pi * br)
                gr_ref[rows, lanes] = gr
                gi_ref[rows, lanes] = gi
                last = row == 7
                xpr = pltpu.roll(jnp.where(last, xer_ref[rows, lanes], xer_ref[pl.ds(r0 + 8, 8), lanes]), 1, 0)
                xpi = pltpu.roll(jnp.where(last, xei_ref[rows, lanes], xei_ref[pl.ds(r0 + 8, 8), lanes]), 1, 0)
                return gr, gi, sar + (gr * xpr + gi * xpi), sai + (gi * xpr - gr * xpi)

            cr, ci, sar, sai = lax.fori_loop(
                0, R // 8, step, (car_r[:, lanes], car_i[:, lanes], adr[:, lanes], adi[:, lanes]), unroll=2)
            car_r[:, lanes] = cr
            car_i[:, lanes] = ci
            adr[:, lanes] = sar
            adi[:, lanes] = sai
        grv = gr_ref[...]
        giv = gi_ref[...]
        du_ref[...] = dy * d_ref[...] + _dot_nt(grv, bbd_r[...]) + _dot_nt(giv, bbd_i[...])
        abr[...] += _dot_tn(u, grv)
        abi[...] += _dot_tn(u, giv)

        @pl.when(i == nb - 1)
        def _():
            mask = _bd_mask()
            r16 = lax.broadcasted_iota(jnp.int32, (S5_H, S5_W), 1)
            h16 = lax.broadcasted_iota(jnp.int32, (S5_H, S5_W), 0)
            fold_b = jnp.bitwise_and(r16, S5_H - 1) == h16
            c64 = lax.broadcasted_iota(jnp.int32, (S5_N, S5_P), 0)
            p64 = lax.broadcasted_iota(jnp.int32, (S5_N, S5_P), 1)
            fold_c = jnp.bitwise_and(c64, S5_P - 1) == p64
            dbbr = _dot_sel_lhs(fold_b, jnp.where(mask, abr[...], 0.0))
            dbbi = _dot_sel_lhs(fold_b, jnp.where(mask, abi[...], 0.0))
            dcre_ref[...] = _dot_sel_rhs(jnp.where(mask, acr[...], 0.0), fold_c)
            dcim_ref[...] = _dot_sel_rhs(jnp.where(mask, aci[...], 0.0), fold_c)
            dlbr = _colsum(adr[...])
            dlbi = _colsum(adi[...])
            _, vjp = jax.vjp(_s5_disc, lre_ref[...], lim_ref[...], ldt_ref[...], bre_ref[...], bim_ref[...])
            dlre, dlim, dldt, dbre, dbim = vjp((dlbr, dlbi, dbbr, dbbi))
            dlre_ref[...] = dlre
            dlim_ref[...] = dlim
            dbre_ref[...] = dbre
            dbim_ref[...] = dbim
            gsel = jnp.right_shift(lax.broadcasted_iota(jnp.int32, (S5_N, HEAD), 0), 6) == \
                lax.broadcasted_iota(jnp.int32, (S5_N, HEAD), 1)
            dldt_ref[...] = _dot_sel_rhs(dldt, gsel)

    full = lambda a: pl.BlockSpec(a.shape, lambda i: (0,) * a.ndim)
    rev = lambda w: pl.BlockSpec((R, w), lambda i: (nb - 1 - i, 0))
    prev8 = pl.BlockSpec((8, S5_N), lambda i: (jnp.maximum((nb - 1 - i) * (R // 8) - 1, 0), 0))
    small = [lre, lim, ldt, bre_t, bim_t, cre_w, cim_w, d_row, w_glu, b_glu]
    outs = [((tp, S5_W), rev(S5_W))] + [
        (s, pl.BlockSpec(s, lambda i: (0, 0))) for s in
        [(1, S5_N), (1, S5_N), (1, HEAD), (S5_H, S5_N), (S5_H, S5_N), (S5_W, S5_P), (S5_W, S5_P),
         (1, S5_W), (S5_W, S5_W), (1, S5_W)]]
    return pl.pallas_call(
        body, name="s5_bwd", grid=(nb,),
        in_specs=[rev(S5_W), rev(S5_W), rev(S5_N), rev(S5_N), prev8, prev8] + [full(a) for a in small],
        out_specs=[o[1] for o in outs], out_shape=[jax.ShapeDtypeStruct(o[0], F32) for o in outs],
        scratch_shapes=[pltpu.VMEM((S5_W, S5_N), MM)] * 4 + [
            pltpu.VMEM((8, 8, S5_N), F32), pltpu.VMEM((8, S5_N), F32), pltpu.VMEM((8, S5_N), F32),
            pltpu.VMEM((R, S5_N), F32), pltpu.VMEM((R, S5_N), F32),
            pltpu.VMEM((R + 8, S5_N), F32), pltpu.VMEM((R + 8, S5_N), F32)] + [pltpu.VMEM((S5_W, S5_N), F32)] * 4 + [
            pltpu.VMEM((8, S5_N), F32), pltpu.VMEM((8, S5_N), F32)],
        compiler_params=_params(("arbitrary",)),
    )(dy_out, u, xr, xi, xr, xi, *small)


def _ret_fwd(q, k, v, dmat, zeta_b, xi_b, gam_b):
    tp = q.shape[0]
    nc = tp // CHUNK

    def body(q_ref, k_ref, v_ref, dm_ref, ze_ref, xi_ref, ga_ref, o_ref, st_ref, s_ref):
        @pl.when(pl.program_id(0) == 0)
        def _():
            s_ref[...] = jnp.zeros_like(s_ref)

        for h in range(RET_H):
            sl = slice(h * HEAD, (h + 1) * HEAD)
            qh, kh, vh = q_ref[:, sl], k_ref[:, sl], v_ref[:, sl]
            sh = s_ref[h]
            st_ref[0, sl, :] = sh
            scores = _dot_nt(qh, kh) * dm_ref[h]
            o_ref[:, sl] = _dot(scores, vh) + _dot(qh, sh) * xi_ref[h]
            s_ref[h] = ga_ref[h] * sh + _dot_tn(kh.astype(F32) * ze_ref[h], vh)

    blk = pl.BlockSpec((CHUNK, RET_W), lambda c: (c, 0))
    cst = pl.BlockSpec((RET_H, HEAD, HEAD), lambda c: (0, 0, 0))
    return pl.pallas_call(
        body, name="ret_fwd", grid=(nc,),
        in_specs=[blk, blk, blk, cst, cst, cst, cst],
        out_specs=[blk, pl.BlockSpec((1, RET_W, HEAD), lambda c: (c, 0, 0))],
        out_shape=[jax.ShapeDtypeStruct((tp, RET_W), F32), jax.ShapeDtypeStruct((nc, RET_W, HEAD), F32)],
        scratch_shapes=[pltpu.VMEM((RET_H, HEAD, HEAD), F32)],
        compiler_params=_params(("arbitrary",)),
    )(q, k, v, dmat, zeta_b, xi_b, gam_b)


def _ret_bwd(q, k, v, do, states, cos2, sin2, dmat, zeta_b, xi_b, gam_b):
    tp = q.shape[0]
    nc = tp // CHUNK

    def body(q_ref, k_ref, v_ref, do_ref, st_ref, cos_ref, sin_ref, dm_ref, ze_ref, xi_ref, ga_ref,
             dq_ref, dk_ref, dv_ref, ds_ref):
        @pl.when(pl.program_id(0) == 0)
        def _():
            ds_ref[...] = jnp.zeros_like(ds_ref)

        cos = cos_ref[...]
        sin = sin_ref[...]
        for h in range(RET_H):
            sl = slice(h * HEAD, (h + 1) * HEAD)
            qh, kh, vh = q_ref[:, sl], k_ref[:, sl], v_ref[:, sl]
            dmh = dm_ref[h]
            sh = st_ref[0, sl, :]
            dsn = ds_ref[h]
            doh = do_ref[:, sl]
            dox = doh * xi_ref[h]
            a = _dot_nt(qh, kh) * dmh
            dqk = _dot_nt(doh, vh) * dmh
            kz = kh.astype(F32) * ze_ref[h]
            dv_ref[:, sl] = _dot_tn(a, doh) + _dot(kz, dsn)
            dqr = _dot(dqk, kh) + _dot_nt(dox, sh)
            dkr = _dot_tn(dqk, qh) + ze_ref[h] * _dot_nt(vh, dsn)
            ds_ref[h] = ga_ref[h] * dsn + _dot_tn(qh, dox)
            dq_ref[:, sl] = dqr * cos - pltpu.roll(dqr, HEAD // 2, 1) * sin
            dk_ref[:, sl] = (dkr * cos - pltpu.roll(dkr, HEAD // 2, 1) * sin) * (HEAD ** -0.5)

    blk = pl.BlockSpec((CHUNK, RET_W), lambda c: (nc - 1 - c, 0))
    tab = pl.BlockSpec((CHUNK, HEAD), lambda c: (nc - 1 - c, 0))
    cst = pl.BlockSpec((RET_H, HEAD, HEAD), lambda c: (0, 0, 0))
    return pl.pallas_call(
        body, name="ret_bwd", grid=(nc,),
        in_specs=[blk, blk, blk, blk, pl.BlockSpec((1, RET_W, HEAD), lambda c: (nc - 1 - c, 0, 0)), tab, tab,
                  cst, cst, cst, cst],
        out_specs=[blk, blk, blk],
        out_shape=[jax.ShapeDtypeStruct((tp, RET_W), F32)] * 3,
        scratch_shapes=[pltpu.VMEM((RET_H, HEAD, HEAD), F32)],
        compiler_params=_params(("arbitrary",)),
    )(q, k, v, do, states, cos2, sin2, dmat, zeta_b, xi_b, gam_b)


def _gn_gate(o, gate, gn_g, gn_b):
    xhat, rstd = _ln_fwd(o, GN_EPS)
    on = xhat * gn_g + gn_b
    s = jax.nn.sigmoid(gate)
    return gate * s * on, xhat, rstd, on, s


def _post_fwd(o, gate, ys5, xhat0, tgt, gn_g, gn_b, li_g, li_b, l1_g, l1_b, l2_g, l2_b, w_out, w_up, w_down):
    tp = o.shape[0]
    seq = tgt.shape[0]
    R = ROW_BLK

    def body(o_ref, g_ref, ys_ref, xh0_ref, ta, tb, tc, gng, gnb, lig, lib, l1g, l1b, l2g, l2b, wo_ref, wu_ref, wd_ref,
             ycat_ref, xh1_ref, rstd1_ref, h1b_ref, dr2_ref, dffb_ref, loss_ref, dl2g_ref, dl2b_ref, tgt_ref):
        i = pl.program_id(0)

        @pl.when(i == 0)
        def _():
            for ref in (loss_ref, dl2g_ref, dl2b_ref):
                ref[...] = jnp.zeros_like(ref)

        tgt_ref[0:CHUNK, :] = ta[...]
        tgt_ref[CHUNK:2 * CHUNK, :] = tb[...]
        tgt_ref[2 * CHUNK:3 * CHUNK, :] = tc[...]
        ycat_ref[:, 0:S5_W] = ys_ref[...].astype(ycat_ref.dtype)
        for h in range(RET_H):
            sl = slice(h * HEAD, (h + 1) * HEAD)
            yret = _gn_gate(o_ref[:, sl], g_ref[:, sl], gng[:, sl], gnb[:, sl])[0]
            ycat_ref[:, S5_W + h * HEAD:S5_W + (h + 1) * HEAD] = yret.astype(ycat_ref.dtype)
        mixed = _dot(ycat_ref[...], wo_ref[...])
        h0 = xh0_ref[...] * lig[...] + lib[...]
        xh1, rstd1 = _ln_fwd(ALPHA * h0 + mixed, LN_EPS)
        xh1_ref[...] = xh1
        rstd1_ref[...] = rstd1
        h1 = xh1 * l1g[...] + l1b[...]
        h1b = h1.astype(MM)
        h1b_ref[...] = h1b
        ff = jnp.zeros((R, D_MODEL), F32)
        for d in range(N_DEV):
            pre = jnp.maximum(_dot(h1b, wu_ref[d]), 0.0)
            ff = ff + _dot(pre * pre, wd_ref[d * FF_BLK:(d + 1) * FF_BLK, :])
        xh2, rstd2 = _ln_fwd(ALPHA * h1 + ff, LN_EPS)
        h2 = xh2 * l2g[...] + l2b[...]
        valid = (i * R + lax.broadcasted_iota(jnp.int32, (R, 1), 0)) >= CHUNK
        err = jnp.where(valid, h2 - tgt_ref[...], 0.0)
        loss_ref[...] += 0.5 * jnp.sum(err * err) / D_MODEL
        dh2 = err * (1.0 / D_MODEL)
        dl2g_ref[...] += _colsum(dh2 * xh2)
        dl2b_ref[...] += _colsum(dh2)
        dr2 = _ln_bwd(dh2 * l2g[...], xh2, rstd2)
        dr2_ref[...] = dr2
        dffb_ref[...] = dr2.astype(MM)

    row = lambda w: pl.BlockSpec((R, w), lambda i: (i, 0))
    full = lambda a: pl.BlockSpec(a.shape, lambda i: (0,) * a.ndim)
    vecs = [gn_g, gn_b, li_g, li_b, l1_g, l1_b, l2_g, l2_b]
    acc = lambda s: (pl.BlockSpec(s, lambda i: (0, 0)), jax.ShapeDtypeStruct(s, F32))
    outs = [(row(D_MODEL), jax.ShapeDtypeStruct((tp, D_MODEL), MM)),
            (row(D_MODEL), jax.ShapeDtypeStruct((tp, D_MODEL), F32)),
            (row(1), jax.ShapeDtypeStruct((tp, 1), F32)),
            (row(D_MODEL), jax.ShapeDtypeStruct((tp, D_MODEL), MM)),
            (row(D_MODEL), jax.ShapeDtypeStruct((tp, D_MODEL), F32)),
            (row(D_MODEL), jax.ShapeDtypeStruct((tp, D_MODEL), MM)),
            acc((8, HEAD)), acc((1, D_MODEL)), acc((1, D_MODEL))]
    return pl.pallas_call(
        body, name="post_fwd", grid=(tp // R,),
        in_specs=[row(RET_W), row(RET_W), row(S5_W), row(D_MODEL)] + _shift3(seq // CHUNK) + [full(a) for a in vecs]
        + [_VMEM, _VMEM, _VMEM],
        out_specs=[o[0] for o in outs], out_shape=[o[1] for o in outs],
        scratch_shapes=[pltpu.VMEM((R, D_MODEL), F32)],
        compiler_params=_params(("arbitrary",)),
    )(o, gate, ys5, xhat0, tgt, tgt, tgt, *vecs, w_out, w_up, w_down)


def _mlp_bwd(h1b, dffb, w_up, w_down):
    tp = h1b.shape[0]
    R = ROW_BLK
    nr = tp // R

    def body(h_ref, df_ref, wu_ref, wd_ref, gup_ref, gdn_ref, dh1_ref, aup, adn):
        d = pl.program_id(0)
        r = pl.program_id(1)

        @pl.when(r == 0)
        def _():
            aup[...] = jnp.zeros_like(aup)
            adn[...] = jnp.zeros_like(adn)

        h = h_ref[...]
        df = df_ref[...]
        wu = wu_ref[0]
        wd = wd_ref[0]
        pre = jnp.maximum(_dot(h, wu), 0.0)
        dpre = (_dot_nt(df, wd) * (2.0 * pre)).astype(MM)
        aup[...] += _dot_tn(h, dpre)
        adn[...] += _dot_tn(pre * pre, df)
        contrib = _dot_nt(dpre, wu)
        rows = pl.ds(pl.multiple_of(r * R, CHUNK), R)

        @pl.when(d == 0)
        def _():
            dh1_ref[rows, :] = contrib

        @pl.when(d > 0)
        def _():
            dh1_ref[rows, :] += contrib

        @pl.when(r == nr - 1)
        def _():
            gup_ref[0] = aup[...].astype(gup_ref.dtype)
            gdn_ref[0] = adn[...].astype(gdn_ref.dtype)

    return pl.pallas_call(
        body, name="mlp_bwd", grid=(N_DEV, nr),
        in_specs=[pl.BlockSpec((R, D_MODEL), lambda d, r: (r, 0)), pl.BlockSpec((R, D_MODEL), lambda d, r: (r, 0)),
                  pl.BlockSpec((1, D_MODEL, FF_BLK), lambda d, r: (d, 0, 0)),
                  pl.BlockSpec((1, FF_BLK, D_MODEL), lambda d, r: (d, 0, 0))],
        out_specs=[pl.BlockSpec((1, D_MODEL, FF_BLK), lambda d, r: (d, 0, 0)),
                   pl.BlockSpec((1, FF_BLK, D_MODEL), lambda d, r: (d, 0, 0)), _VMEM],
        out_shape=[jax.ShapeDtypeStruct((N_DEV, D_MODEL, FF_BLK), MM), jax.ShapeDtypeStruct((N_DEV, FF_BLK, D_MODEL), MM),
                   jax.ShapeDtypeStruct((tp, D_MODEL), F32)],
        scratch_shapes=[pltpu.VMEM((D_MODEL, FF_BLK), F32), pltpu.VMEM((FF_BLK, D_MODEL), F32)],
        compiler_params=_params(("arbitrary", "arbitrary")),
    )(h1b, dffb, w_up, w_down.reshape(N_DEV, FF_BLK, D_MODEL))


def _post_bwd(dh1m, dr2, xhat1, rstd1, ycat, o, gate, gn_g, gn_b, l1_g, w_out):
    tp = o.shape[0]
    R = ROW_BLK
    nb = tp // R

    def body(dm_ref, dr2_ref, xh1_ref, rs1_ref, yc_ref, o_ref, g_ref, gng, gnb, l1g, wo_ref,
             do_ref, dg_ref, dys_ref, dh0_ref, gwo_ref, dl1g_ref, dl1b_ref, dgng_ref, dgnb_ref, awo):
        i = pl.program_id(0)

        @pl.when(i == 0)
        def _():
            for ref in (awo, dl1g_ref, dl1b_ref, dgng_ref, dgnb_ref):
                ref[...] = jnp.zeros_like(ref)

        dh1 = dm_ref[...] + ALPHA * dr2_ref[...]
        xh1 = xh1_ref[...]
        dl1g_ref[...] += _colsum(dh1 * xh1)
        dl1b_ref[...] += _colsum(dh1)
        dr1 = _ln_bwd(dh1 * l1g[...], xh1, rs1_ref[...])
        dh0_ref[...] = ALPHA * dr1
        dmix = dr1.astype(MM)
        awo[...] += _dot_tn(yc_ref[...], dmix)
        dyc = _dot_nt(dmix, wo_ref[...])
        dys_ref[...] = dyc[:, 0:S5_W]
        for h in range(RET_H):
            sl = slice(h * HEAD, (h + 1) * HEAD)
            gt = g_ref[:, sl]
            _, xhat, rstd, on, s = _gn_gate(o_ref[:, sl], gt, gng[:, sl], gnb[:, sl])
            dyr = dyc[:, S5_W + h * HEAD:S5_W + (h + 1) * HEAD]
            dg_ref[:, sl] = dyr * on * (s * (1.0 + gt * (1.0 - s)))
            don = dyr * gt * s
            dgng_ref[:, sl] += _colsum(don * xhat)
            dgnb_ref[:, sl] += _colsum(don)
            do_ref[:, sl] = _ln_bwd(don * gng[:, sl], xhat, rstd)

        @pl.when(i == nb - 1)
        def _():
            gwo_ref[...] = awo[...].astype(gwo_ref.dtype)

    row = lambda w: pl.BlockSpec((R, w), lambda i: (i, 0))
    full = lambda a: pl.BlockSpec(a.shape, lambda i: (0,) * a.ndim)
    acc = lambda s, dt=F32: (pl.BlockSpec(s, lambda i: (0, 0)), jax.ShapeDtypeStruct(s, dt))
    outs = [(row(RET_W), jax.ShapeDtypeStruct((tp, RET_W), F32)), (row(RET_W), jax.ShapeDtypeStruct((tp, RET_W), F32)),
            (row(S5_W), jax.ShapeDtypeStruct((tp, S5_W), F32)), (row(D_MODEL), jax.ShapeDtypeStruct((tp, D_MODEL), F32)),
            acc((D_MODEL, D_MODEL), MM), acc((1, D_MODEL)), acc((1, D_MODEL)), acc((1, RET_W)), acc((1, RET_W))]
    return pl.pallas_call(
        body, name="post_bwd", grid=(nb,),
        in_specs=[row(D_MODEL), row(D_MODEL), row(D_MODEL), row(1), row(D_MODEL), row(RET_W), row(RET_W),
                  full(gn_g), full(gn_b), full(l1_g), _VMEM],
        out_specs=[o[0] for o in outs], out_shape=[o[1] for o in outs],
        scratch_shapes=[pltpu.VMEM((D_MODEL, D_MODEL), F32)],
        compiler_params=_params(("arbitrary",)),
    )(dh1m, dr2, xhat1, rstd1, ycat, o, gate, gn_g, gn_b, l1_g, w_out)


def _in_bwd(du, dq, dk, dv, dg, dh0r, xhat0, rstd0, li_g, li_b, w_int):
    tp = du.shape[0]
    R = ROW_BLK
    nb = tp // R
    segs = [(0, S5_W)] + [(S5_W + n * RET_W, S5_W + (n + 1) * RET_W) for n in range(4)]

    def body(du_ref, dq_ref, dk_ref, dv_ref, dg_ref, dh0r_ref, xh_ref, rs_ref, lig, lib, w_ref,
             draw_ref, gw_ref, dlg_ref, dlb_ref, aw):
        i = pl.program_id(0)

        @pl.when(i == 0)
        def _():
            for ref in (aw, dlg_ref, dlb_ref):
                ref[...] = jnp.zeros_like(ref)

        valid = (i * R + lax.broadcasted_iota(jnp.int32, (R, 1), 0)) >= PAD
        xh = xh_ref[...]
        hb = (xh * lig[...] + lib[...]).astype(MM)
        dh0 = dh0r_ref[...]
        for (lo, hi), ref in zip(segs, (du_ref, dq_ref, dk_ref, dv_ref, dg_ref)):
            dseg = jnp.where(valid, ref[...], 0.0).astype(MM)
            dh0 = dh0 + _dot(dseg, w_ref[lo:hi, :])
            aw[lo:hi, :] += _dot_tn(dseg, hb)
        dlg_ref[...] += _colsum(dh0 * xh)
        dlb_ref[...] += _colsum(dh0)
        draw_ref[...] = _ln_bwd(dh0 * lig[...], xh, rs_ref[...])

        @pl.when(i == nb - 1)
        def _():
            gw_ref[...] = aw[...].astype(gw_ref.dtype)

    row = lambda w: pl.BlockSpec((R, w), lambda i: (i, 0))
    full = lambda a: pl.BlockSpec(a.shape, lambda i: (0,) * a.ndim)
    acc = lambda s, dt=F32: (pl.BlockSpec(s, lambda i: (0, 0)), jax.ShapeDtypeStruct(s, dt))
    outs = [(row(D_MODEL), jax.ShapeDtypeStruct((tp, D_MODEL), F32)), acc((PROJ_W, D_MODEL), MM),
            acc((1, D_MODEL)), acc((1, D_MODEL))]
    return pl.pallas_call(
        body, name="in_bwd", grid=(nb,),
        in_specs=[row(S5_W), row(RET_W), row(RET_W), row(RET_W), row(RET_W), row(D_MODEL), row(D_MODEL), row(1),
                  full(li_g), full(li_b), _VMEM],
        out_specs=[o[0] for o in outs], out_shape=[o[1] for o in outs],
        scratch_shapes=[pltpu.VMEM((PROJ_W, D_MODEL), F32)],
        compiler_params=_params(("arbitrary",)),
    )(du, dq, dk, dv, dg, dh0r, xhat0, rstd0, li_g, li_b, w_int)


def _place():
    return lax.axis_index("x"), lax.axis_index("y"), lax.axis_index("c")


def _all_gather(shards, name):
    n = len(shards)

    def body(*refs):
        ins, outs = refs[:n], refs[n:2 * n]
        send_sems, recv_sems, local_sems = refs[2 * n:]
        x, y, c = _place()
        me, sib = (x, y, c), (x, y, 1 - c)
        chips = [(1 - x, y), (x, 1 - y), (1 - x, 1 - y)]

        def slot(a, dev):
            return outs[a].at[4 * dev[0] + 2 * dev[1] + dev[2]]

        def copy(a, k, block, to, src=None):
            return pltpu.make_async_remote_copy(
                src_ref=slot(a, block) if src is None else src, dst_ref=slot(a, block),
                send_sem=send_sems.at[7 * a + k], recv_sem=recv_sems.at[7 * a + k], device_id=to, device_id_type=_MESH)

        mine = [pltpu.make_async_copy(ins[a], slot(a, me), local_sems.at[a]) for a in range(n)]
        for cp in mine:
            cp.start()
        first = []
        for a in range(n):
            first.append(copy(a, 0, me, sib, src=ins[a]))
            first += [copy(a, 1 + j, me, (*chip, c), src=ins[a]) for j, chip in enumerate(chips)]
        for cp in first:
            cp.start()
        passed = []
        for j, chip in enumerate(chips):
            for a in range(n):
                copy(a, 1 + j, (*chip, c), me).wait_recv()
                cp = copy(a, 4 + j, (*chip, c), sib)
                cp.start()
                passed.append(cp)
        for a in range(n):
            copy(a, 0, sib, me).wait_recv()
            for j, chip in enumerate(chips):
                copy(a, 4 + j, (*chip, 1 - c), me).wait_recv()
        for cp in first + passed:
            cp.wait_send()
        for cp in mine:
            cp.wait()

    return pl.pallas_call(
        body, name=name,
        out_shape=[jax.ShapeDtypeStruct((N_DEV,) + s.shape, s.dtype) for s in shards],
        in_specs=[_ANY] * n, out_specs=[_ANY] * n,
        scratch_shapes=[pltpu.SemaphoreType.DMA((7 * n,)), pltpu.SemaphoreType.DMA((7 * n,)),
                        pltpu.SemaphoreType.DMA((n,))],
    )(*shards)


def _exchange_pair(gs, name):
    n = len(gs)

    def body(*refs):
        ins, outs = refs[:n], refs[n:2 * n]
        send_sems, recv_sems = refs[2 * n:]
        x, y, c = _place()
        cps = []
        for a in range(n):
            for j in range(4):
                cps.append(pltpu.make_async_remote_copy(
                    src_ref=ins[a].at[2 * j + (1 - c)], dst_ref=outs[a].at[j],
                    send_sem=send_sems.at[4 * a + j], recv_sem=recv_sems.at[4 * a + j],
                    device_id=(x, y, 1 - c), device_id_type=_MESH))
        for cp in cps:
            cp.start()
        for cp in cps:
            cp.wait()

    return pl.pallas_call(
        body, name=name,
        out_shape=[jax.ShapeDtypeStruct((4,) + g.shape[1:], g.dtype) for g in gs],
        in_specs=[_ANY] * n, out_specs=[_ANY] * n,
        scratch_shapes=[pltpu.SemaphoreType.DMA((4 * n,)), pltpu.SemaphoreType.DMA((4 * n,))],
    )(*gs)


def _pair_sum(gs, r1s, c_arr):
    n = len(gs)

    def body(c_ref, *refs):
        for a in range(n):
            refs[2 * n + a][...] = (refs[a][...].astype(F32) + refs[n + a][...].astype(F32)).astype(refs[2 * n + a].dtype)

    def blk(g, own):
        s = g.shape[1:]
        if own:
            return pl.BlockSpec((1,) + s, lambda j, c_ref: (2 * j + c_ref[0],) + (0,) * len(s))
        return pl.BlockSpec((1,) + s, lambda j, c_ref: (j,) + (0,) * len(s))

    return pl.pallas_call(
        body, name="pair_sum",
        grid_spec=pltpu.PrefetchScalarGridSpec(
            num_scalar_prefetch=1, grid=(4,),
            in_specs=[blk(g, True) for g in gs] + [blk(g, False) for g in gs],
            out_specs=[blk(g, False) for g in gs]),
        out_shape=[jax.ShapeDtypeStruct((4,) + g.shape[1:], g.dtype) for g in gs],
        compiler_params=_params(("arbitrary",)),
    )(c_arr, *gs, *r1s)


def _exchange_chips(ps, name):
    n = len(ps)

    def body(*refs):
        ins, outs = refs[:n], refs[n:2 * n]
        send_sems, recv_sems = refs[2 * n:]
        x, y, c = _place()
        chips = [(1 - x, y), (x, 1 - y), (1 - x, 1 - y)]
        cps = []
        for a in range(n):
            for k, chip in enumerate(chips):
                cps.append(pltpu.make_async_remote_copy(
                    src_ref=ins[a].at[2 * chip[0] + chip[1]], dst_ref=outs[a].at[k],
                    send_sem=send_sems.at[3 * a + k], recv_sem=recv_sems.at[3 * a + k],
                    device_id=(*chip, c), device_id_type=_MESH))
        for cp in cps:
            cp.start()
        for cp in cps:
            cp.wait()

    return pl.pallas_call(
        body, name=name,
        out_shape=[jax.ShapeDtypeStruct((3,) + p.shape[1:], p.dtype) for p in ps],
        in_specs=[_ANY] * n, out_specs=[_ANY] * n,
        scratch_shapes=[pltpu.SemaphoreType.DMA((3 * n,)), pltpu.SemaphoreType.DMA((3 * n,))],
    )(*ps)


def _chip_sum(ps, r2s, j_arr):
    n = len(ps)

    def body(j_ref, *refs):
        for a in range(n):
            r2 = refs[n + a]
            refs[2 * n + a][...] = ((refs[a][0].astype(F32) + r2[0].astype(F32)) + r2[1].astype(F32)) + r2[2].astype(F32)

    def own(p):
        s = p.shape[1:]
        return pl.BlockSpec((1,) + s, lambda i, j_ref: (j_ref[0],) + (0,) * len(s))

    def whole(p):
        return pl.BlockSpec(p.shape, lambda i, j_ref: (0,) * p.ndim)

    return pl.pallas_call(
        body, name="chip_sum",
        grid_spec=pltpu.PrefetchScalarGridSpec(
            num_scalar_prefetch=1, grid=(1,),
            in_specs=[own(p) for p in ps] + [whole(r) for r in r2s],
            out_specs=[pl.BlockSpec(p.shape[1:], lambda i, j_ref: (0,) * (p.ndim - 1)) for p in ps]),
        out_shape=[jax.ShapeDtypeStruct(p.shape[1:], F32) for p in ps],
        compiler_params=_params(("arbitrary",)),
    )(j_arr, *ps, *r2s)


def _adamw_math(w, g, m, v):
    m = ADAM_B1 * m + (1.0 - ADAM_B1) * g
    v = ADAM_B2 * v + (1.0 - ADAM_B2) * (g * g)
    m_hat = m / (1.0 - ADAM_B1 ** ADAM_STEP)
    v_hat = v / (1.0 - ADAM_B2 ** ADAM_STEP)
    return -ADAM_LR * (m_hat / (jnp.sqrt(v_hat) + ADAM_EPS) + ADAM_WD * w), m, v


def _adamw(items, name, steps):
    n = len(items)

    def body(*refs):
        for a in range(n):
            g, w, m, v = (refs[4 * a + t][...] for t in range(4))
            d, m2, v2 = _adamw_math(w, g, m, v)
            refs[4 * n + 3 * a][...] = d
            refs[4 * n + 3 * a + 1][...] = m2
            refs[4 * n + 3 * a + 2][...] = v2

    def blk(arr):
        r, c = arr.shape
        return pl.BlockSpec((r // steps, c), lambda i: (i, 0))

    flat = [t for it in items for t in it]
    return pl.pallas_call(
        body, name=name, grid=(steps,),
        in_specs=[blk(t) for t in flat],
        out_specs=[blk(it[1]) for it in items for _ in range(3)],
        out_shape=[jax.ShapeDtypeStruct(it[1].shape, F32) for it in items for _ in range(3)],
        compiler_params=_params(("arbitrary",)),
    )(*flat)


def _adamw_small(gathered, w, m, v):
    def body(gs_ref, w_ref, m_ref, v_ref, g_ref, d_ref, m2_ref, v2_ref):
        g = gs_ref[0]
        for s in range(1, N_DEV):
            g = g + gs_ref[s]
        g_ref[...] = g
        d_ref[...], m2_ref[...], v2_ref[...] = _adamw_math(w_ref[...], g, m_ref[...], v_ref[...])

    return pl.pallas_call(
        body, name="adamw_small", out_shape=[jax.ShapeDtypeStruct(w.shape, F32)] * 4,
        in_specs=[_VMEM] * 4, out_specs=[_VMEM] * 4, compiler_params=_params(),
    )(gathered, w, m, v)


SMALL = ["ln_in_g", "ln_in_b", "s5_lambda_re", "s5_lambda_im", "s5_log_dt", "s5_b_re", "s5_b_im", "s5_c_re", "s5_c_im",
         "s5_d", "s5_b_glu", "ret_gn_g", "ret_gn_b", "ln1_g", "ln1_b", "ln2_g", "ln2_b"]
LANE = 128


def _pack(arrs):
    parts = []
    for a in arrs:
        f = a.reshape(-1)
        parts.append(jnp.pad(f, (0, (-f.shape[0]) % LANE)))
    flat = jnp.concatenate(parts)
    rows = -(-flat.shape[0] // LANE)
    flat = jnp.pad(flat, (0, (-rows % 8) * LANE + rows * LANE - flat.shape[0]))
    return flat.reshape(-1, LANE)


def _unpack(packed, shapes):
    flat = packed.reshape(-1)
    out, off = [], 0
    for s in shapes:
        n = math.prod(s)
        out.append(flat[off:off + n].reshape(s))
        off += n + (-n) % LANE
    return out


def _rope_tables(tp):
    pos = jnp.arange(tp, dtype=F32) - float(PAD)
    inv_freq = 1.0 / (ROPE_BASE ** (jnp.arange(0, HEAD, 2, dtype=F32) / HEAD))
    ang = pos[:, None] * inv_freq[None, :]
    cos, sin = jnp.cos(ang), jnp.sin(ang)
    return jnp.concatenate([cos, cos], axis=1), jnp.concatenate([-sin, sin], axis=1)


def _decay_tables():
    log_gamma = jnp.log1p(-jnp.exp2(-5.0 - jnp.arange(RET_H, dtype=F32)))
    idx = jnp.arange(CHUNK, dtype=F32)
    diff = idx[:, None] - idx[None, :]
    dmat = jnp.where(diff[None] >= 0, jnp.exp(jnp.maximum(diff, 0.0)[None] * log_gamma[:, None, None]), 0.0)
    zeta = jnp.exp((CHUNK - 1.0 - idx)[None] * log_gamma[:, None])
    xi = jnp.exp((idx + 1.0)[None] * log_gamma[:, None])
    gam = jnp.exp(CHUNK * log_gamma)
    wide = lambda t: jnp.broadcast_to(t[:, :, None], (RET_H, CHUNK, HEAD))
    return dmat, wide(zeta), wide(xi), jnp.broadcast_to(gam[:, None, None], (RET_H, CHUNK, HEAD))


def _local_step(x2d, tgt, meta_full, w_int, w_out, w_up, w_down, w_glu, sp):
    tp = x2d.shape[0] + CHUNK
    row = lambda a: a.reshape(1, -1)
    cos2, sin2 = _rope_tables(tp)
    dmat, zeta_b, xi_b, gam_b = _decay_tables()
    li_g, li_b = row(sp["ln_in_g"]), row(sp["ln_in_b"])
    l1_g, l1_b, l2_g, l2_b = row(sp["ln1_g"]), row(sp["ln1_b"]), row(sp["ln2_g"]), row(sp["ln2_b"])
    gn_g, gn_b = row(sp["ret_gn_g"]), row(sp["ret_gn_b"])
    lre, lim = row(sp["s5_lambda_re"]), row(sp["s5_lambda_im"])
    ldt = row(jnp.repeat(sp["s5_log_dt"].reshape(-1), S5_P))
    to_t = lambda b: b.reshape(S5_G, S5_P, S5_H).transpose(2, 0, 1).reshape(S5_H, S5_N)
    bre_t, bim_t = to_t(sp["s5_b_re"]), to_t(sp["s5_b_im"])
    to_w = lambda c: jnp.tile(c.reshape(S5_W, S5_P), (1, 2))
    cre_w, cim_w = to_w(sp["s5_c_re"]), to_w(sp["s5_c_im"])
    s5_small = (lre, lim, ldt, bre_t, bim_t, cre_w, cim_w, row(sp["s5_d"]), w_glu, row(sp["s5_b_glu"]))

    xhat0, rstd0, u, q, k, v, gate = _in_proj(x2d, meta_full, li_g, li_b, w_int, cos2, sin2)
    ys5, xr, xi = _s5_fwd(u, *s5_small)
    o, states = _ret_fwd(q, k, v, dmat, zeta_b, xi_b, gam_b)
    ycat, xhat1, rstd1, h1b, dr2, dffb, loss8, dl2g, dl2b = _post_fwd(
        o, gate, ys5, xhat0, tgt, gn_g, gn_b, li_g, li_b, l1_g, l1_b, l2_g, l2_b, w_out, w_up, w_down)
    g_up, g_down, dh1m = _mlp_bwd(h1b, dffb, w_up, w_down)
    do, dgate, dys5, dh0r, g_out, dl1g, dl1b, dgng, dgnb = _post_bwd(
        dh1m, dr2, xhat1, rstd1, ycat, o, gate, gn_g, gn_b, l1_g, w_out)
    du, dlre, dlim, dldt, dbre_t, dbim_t, dcre, dcim, dd, dwglu, dbglu = _s5_bwd(dys5, u, xr, xi, *s5_small)
    dq, dk, dv = _ret_bwd(q, k, v, do, states, cos2, sin2, dmat, zeta_b, xi_b, gam_b)
    draw, g_int, dlig, dlib = _in_bwd(du, dq, dk, dv, dgate, dh0r, xhat0, rstd0, li_g, li_b, w_int)

    from_t = lambda t: t.reshape(S5_H, S5_G, S5_P).transpose(1, 2, 0)
    small = {
        "ln_in_g": dlig, "ln_in_b": dlib, "s5_lambda_re": dlre, "s5_lambda_im": dlim, "s5_log_dt": dldt[:, :S5_G],
        "s5_b_re": from_t(dbre_t), "s5_b_im": from_t(dbim_t), "s5_c_re": dcre, "s5_c_im": dcim, "s5_d": dd,
        "s5_b_glu": dbglu, "ret_gn_g": dgng, "ret_gn_b": dgnb, "ln1_g": dl1g, "ln1_b": dl1b, "ln2_g": dl2g, "ln2_b": dl2b,
        "meta_tokens": draw[PAD:CHUNK], "s5_w_glu": dwglu}
    return loss8[0, 0], draw, (g_int, g_out, g_up, g_down), small


def kernel(x, meta_tokens, ln_in_g, ln_in_b, w_in, s5_lambda_re, s5_lambda_im, s5_log_dt, s5_b_re, s5_b_im, s5_c_re, s5_c_im, s5_d, s5_w_glu, s5_b_glu, ret_gn_g, ret_gn_b, w_out, ln1_g, ln1_b, w_up, w_down, ln2_g, ln2_b, loss_target, m_meta_tokens, m_ln_in_g, m_ln_in_b, m_w_in, m_s5_lambda_re, m_s5_lambda_im, m_s5_log_dt, m_s5_b_re, m_s5_b_im, m_s5_c_re, m_s5_c_im, m_s5_d, m_s5_w_glu, m_s5_b_glu, m_ret_gn_g, m_ret_gn_b, m_w_out, m_ln1_g, m_ln1_b, m_w_up, m_w_down, m_ln2_g, m_ln2_b, v_meta_tokens, v_ln_in_g, v_ln_in_b, v_w_in, v_s5_lambda_re, v_s5_lambda_im, v_s5_log_dt, v_s5_b_re, v_s5_b_im, v_s5_c_re, v_s5_c_im, v_s5_d, v_s5_w_glu, v_s5_b_glu, v_ret_gn_g, v_ret_gn_b, v_w_out, v_ln1_g, v_ln1_b, v_w_up, v_w_down, v_ln2_g, v_ln2_b):
    args = dict(locals())
    names = ["meta_tokens", "ln_in_g", "ln_in_b", "w_in", "s5_lambda_re", "s5_lambda_im", "s5_log_dt", "s5_b_re", "s5_b_im",
             "s5_c_re", "s5_c_im", "s5_d", "s5_w_glu", "s5_b_glu", "ret_gn_g", "ret_gn_b", "w_out", "ln1_g", "ln1_b",
             "w_up", "w_down", "ln2_g", "ln2_b"]
    ax, ay, ac = _place()
    me = 4 * ax + 2 * ay + ac

    big = [w_in[0].T.astype(MM), w_out[0].astype(MM), w_up[0].astype(MM), w_down[0].astype(MM), s5_w_glu[0].astype(MM),
           meta_tokens]
    a_int, a_out, a_up, a_down, a_glu, a_meta = _all_gather(big, "gather_weights")
    w_int = a_int.reshape(PROJ_W, D_MODEL)
    w_out_f = a_out.reshape(D_MODEL, D_MODEL)
    w_down_f = a_down.reshape(D_FF, D_MODEL)
    w_glu_f = a_glu.reshape(S5_W, S5_W)
    meta_full = a_meta.transpose(1, 0, 2).reshape(N_META, D_MODEL)

    sp = {n: args[n] for n in SMALL}
    loss_local, draw, grads_big, small = _local_step(x[0], loss_target[0], meta_full, w_int, w_out_f, a_up, w_down_f,
                                                     w_glu_f, sp)
    loss = lax.psum(loss_local, ("x", "y", "c"))

    gs = [grads_big[0].reshape(N_DEV, PROJ_W // N_DEV, D_MODEL), grads_big[1].reshape(N_DEV, D_MODEL // N_DEV, D_MODEL),
          grads_big[2], grads_big[3]]
    r1 = _exchange_pair(gs, "exchange_pair")
    ps = _pair_sum(gs, r1, jnp.reshape(ac, (1,)).astype(jnp.int32))
    r2 = _exchange_chips(ps, "exchange_chips")
    g_int, g_out, g_up, g_down = _chip_sum(ps, r2, jnp.reshape(2 * ax + ay, (1,)).astype(jnp.int32))
    big_grads = {"w_in": g_int.T[None], "w_out": g_out[None], "w_up": g_up[None], "w_down": g_down[None]}

    order = SMALL + ["meta_tokens", "s5_w_glu"]
    shapes = [args[n].shape for n in SMALL] + [(N_META, D_MODEL), (S5_W, S5_W)]
    g_pack = _pack([small[n] for n in order])
    (g_all,) = _all_gather([g_pack], "gather_small_grads")
    zeros = [jnp.zeros((N_META, D_MODEL), F32), jnp.zeros((S5_W, S5_W), F32)]
    packs = [_pack([args[p + n] for n in SMALL] + zeros) for p in ("", "m_", "v_")]
    g_sum, d_pack, m_pack, v_pack = _adamw_small(g_all, *packs)
    g_small = dict(zip(order, _unpack(g_sum, shapes)))
    d_small = dict(zip(order, _unpack(d_pack, shapes)))
    m_small = dict(zip(order, _unpack(m_pack, shapes)))
    v_small = dict(zip(order, _unpack(v_pack, shapes)))

    g_meta = lax.dynamic_slice(g_small["meta_tokens"], (0, me * (D_MODEL // N_DEV)), (N_META, D_MODEL // N_DEV))
    g_glu = lax.dynamic_slice(g_small["s5_w_glu"], (me * (S5_W // N_DEV), 0), (S5_W // N_DEV, S5_W))
    shard_grads = dict(big_grads, meta_tokens=g_meta, s5_w_glu=g_glu[None])
    sharded = ["w_in", "w_out", "w_up", "w_down"]
    two_d = lambda a: a.reshape(a.shape[-2:])
    res = _adamw([tuple(two_d(t) for t in (shard_grads[n], args[n], args["m_" + n], args["v_" + n])) for n in sharded],
                 "adamw_big", 8)
    res2 = _adamw([tuple(two_d(t) for t in (shard_grads[n], args[n], args["m_" + n], args["v_" + n]))
                   for n in ("meta_tokens", "s5_w_glu")], "adamw_shard_small", 1)
    upd = {}
    for idx, n in enumerate(sharded):
        upd[n] = [r.reshape(args[n].shape) for r in res[3 * idx:3 * idx + 3]]
    for idx, n in enumerate(("meta_tokens", "s5_w_glu")):
        upd[n] = [r.reshape(args[n].shape) for r in res2[3 * idx:3 * idx + 3]]

    grads, deltas, new_m, new_v = [], [], [], []
    for n in names:
        if n in upd:
            grads.append(shard_grads[n].reshape(args[n].shape))
            d, m2, v2 = upd[n]
        else:
            grads.append(g_small[n])
            d, m2, v2 = d_small[n], m_small[n], v_small[n]
        deltas.append(d)
        new_m.append(m2)
        new_v.append(v2)
    grad_x = draw[CHUNK:][None]
    return (loss, grad_x, *grads, *deltas, *new_m, *new_v)
```

```python
import math

import jax
import jax.numpy as jnp
from jax import lax
from jax.experimental import pallas as pl
from jax.experimental.pallas import tpu as pltpu

F32 = jnp.float32
MM = jnp.bfloat16

D_MODEL = 1024
N_META = 16
CHUNK = 128
PAD = CHUNK - N_META
S5_W, S5_G, S5_H, S5_P = 256, 16, 16, 64
S5_N = S5_G * S5_P
RET_W, RET_H, HEAD = 768, 6, 128
D_FF = 4096
PROJ_W = S5_W + 4 * RET_W
N_DEV = 8
FF_BLK = D_FF // N_DEV
ROW_BLK = 384
ALPHA = 2.0 ** 0.25
LN_EPS = 1e-5
GN_EPS = 1e-5
ROPE_BASE = 10000.0
GELU_C = math.sqrt(2.0 / math.pi)
GELU_A = 0.044715
ADAM_LR, ADAM_B1, ADAM_B2, ADAM_EPS, ADAM_WD, ADAM_STEP = 0.001, 0.9, 0.999, 1e-08, 0.01, 10
VMEM_LIMIT = 60 * 1024 * 1024

_VMEM = pl.BlockSpec(memory_space=pltpu.VMEM)
_ANY = pl.BlockSpec(memory_space=pl.ANY)
_MESH = pl.DeviceIdType.MESH


def _params(sem=None):
    return pltpu.CompilerParams(dimension_semantics=sem, vmem_limit_bytes=VMEM_LIMIT)


def _dot(a, b):
    return jnp.dot(a.astype(MM), b.astype(MM), preferred_element_type=F32)


def _dot_nt(a, b):
    return lax.dot_general(a.astype(MM), b.astype(MM), (((1,), (1,)), ((), ())), preferred_element_type=F32)


def _dot_tn(a, b):
    return lax.dot_general(a.astype(MM), b.astype(MM), (((0,), (0,)), ((), ())), preferred_element_type=F32)


def _split3(a):
    hi = a.astype(jnp.bfloat16)
    r1 = a - hi.astype(F32)
    mid = r1.astype(jnp.bfloat16)
    lo = (r1 - mid.astype(F32)).astype(jnp.bfloat16)
    return hi, mid, lo


def _dot_sel_rhs(a, sel):
    s = sel.astype(jnp.bfloat16)
    return sum(jnp.dot(p, s, preferred_element_type=F32) for p in _split3(a))


def _dot_sel_lhs(sel, b):
    s = sel.astype(jnp.bfloat16)
    return sum(jnp.dot(s, p, preferred_element_type=F32) for p in _split3(b))


def _ln_fwd(r, eps):
    mu = jnp.mean(r, axis=-1, keepdims=True)
    xc = r - mu
    var = jnp.mean(xc * xc, axis=-1, keepdims=True)
    rstd = lax.rsqrt(var + eps)
    return xc * rstd, rstd


def _ln_bwd(dxhat, xhat, rstd):
    m1 = jnp.mean(dxhat, axis=-1, keepdims=True)
    m2 = jnp.mean(dxhat * xhat, axis=-1, keepdims=True)
    return rstd * (dxhat - m1 - xhat * m2)


def _colsum(a):
    return jnp.sum(a, axis=0, keepdims=True)


def _shift3(n_in):
    return [pl.BlockSpec((CHUNK, D_MODEL), (lambda i, j=j: (jnp.clip(3 * i - 1 + j, 0, n_in - 1), 0))) for j in range(3)]


def _in_proj(x2d, meta_full, ln_g, ln_b, w_int, cos2, sin2, jobs=()):
    seq = x2d.shape[0]
    tp = seq + CHUNK
    R = ROW_BLK

    def body(xa, xb, xc, meta_ref, g_ref, b_ref, w_ref, cos_ref, sin_ref,
             xhat_ref, rstd_ref, u_ref, q_ref, k_ref, v_ref, gate_ref, raw_ref):
        i = pl.program_id(0)
        raw_ref[0:CHUNK, :] = xa[...]
        raw_ref[CHUNK:2 * CHUNK, :] = xb[...]
        raw_ref[2 * CHUNK:3 * CHUNK, :] = xc[...]

        @pl.when(i == 0)
        def _():
            raw_ref[0:PAD, :] = jnp.zeros((PAD, D_MODEL), F32)
            raw_ref[PAD:CHUNK, :] = meta_ref[...]

        xhat, rstd = _ln_fwd(raw_ref[...], LN_EPS)
        xhat_ref[...] = xhat
        rstd_ref[...] = rstd
        hb = (xhat * g_ref[...] + b_ref[...]).astype(MM)
        valid = (i * R + lax.broadcasted_iota(jnp.int32, (R, 1), 0)) >= PAD

        def seg(lo, hi):
            return jnp.where(valid, _dot_nt(hb, w_ref[lo:hi, :]), 0.0)

        u_ref[...] = seg(0, S5_W)
        cos = cos_ref[...]
        sin = sin_ref[...]
        q = seg(S5_W, S5_W + RET_W)
        k = seg(S5_W + RET_W, S5_W + 2 * RET_W)
        for h in range(RET_H):
            sl = slice(h * HEAD, (h + 1) * HEAD)
            qh = q[:, sl]
            kh = k[:, sl]
            q_ref[:, sl] = (qh * cos + pltpu.roll(qh, HEAD // 2, 1) * sin).astype(q_ref.dtype)
            k_ref[:, sl] = ((kh * cos + pltpu.roll(kh, HEAD // 2, 1) * sin) * (HEAD ** -0.5)).astype(k_ref.dtype)
        v_ref[...] = seg(S5_W + 2 * RET_W, S5_W + 3 * RET_W).astype(v_ref.dtype)
        gate_ref[...] = seg(S5_W + 3 * RET_W, PROJ_W)

    def rows(w, dt):
        return pl.BlockSpec((R, w), lambda i: (i, 0)), jax.ShapeDtypeStruct((tp, w), dt)

    outs = [rows(D_MODEL, F32), rows(1, F32), rows(S5_W, F32), rows(RET_W, MM), rows(RET_W, MM),
            rows(RET_W, MM), rows(RET_W, F32)]
    full = lambda s: pl.BlockSpec(s, lambda i: (0,) * len(s))
    return _call(
        body, "in_proj", (tp // R,),
        _shift3(seq // CHUNK) + [full((N_META, D_MODEL)), full((1, D_MODEL)), full((1, D_MODEL)), _VMEM,
                                 pl.BlockSpec((R, HEAD), lambda i: (i, 0)), pl.BlockSpec((R, HEAD), lambda i: (i, 0))],
        [o[0] for o in outs], [o[1] for o in outs], [pltpu.VMEM((R, D_MODEL), F32)],
        (x2d, x2d, x2d, meta_full, ln_g, ln_b, w_int, cos2, sin2), jobs)


def _s5_disc(lre, lim, ldt, bre_t, bim_t):
    dt = jnp.exp(ldt)
    mag = jnp.exp(lre * dt)
    ang = lim * dt
    lbr = mag * jnp.cos(ang)
    lbi = mag * jnp.sin(ang)
    den = lre * lre + lim * lim
    nr = lbr - 1.0
    qr = (nr * lre + lbi * lim) / den
    qi = (lbi * lre - nr * lim) / den
    return lbr, lbi, qr * bre_t - qi * bim_t, qr * bim_t + qi * bre_t


def _s5_tables(lbr, lbi, reverse):
    if reverse:
        lbi = -lbi
    pw = [(lbr, lbi)]
    for _ in range(7):
        r, i = pw[-1]
        pw.append((r * lbr - i * lbi, r * lbi + i * lbr))
    row = lax.broadcasted_iota(jnp.int32, (8, S5_N), 0)
    tabs = []
    for k in range(3):
        sh = 2 ** k
        mask = (row < 8 - sh) if reverse else (row >= sh)
        ar, ai = pw[sh - 1]
        tabs.append((jnp.where(mask, ar, 0.0), jnp.where(mask, ai, 0.0)))
    pr = jnp.zeros((8, S5_N), F32)
    pi = jnp.zeros((8, S5_N), F32)
    for i in range(8):
        ar, ai = pw[7 - i] if reverse else pw[i]
        pr = jnp.where(row == i, ar, pr)
        pi = jnp.where(row == i, ai, pi)
    tabs.append((pr, pi))
    return tabs


def _store_tables(tab_ref, tabs):
    for k, (r, i) in enumerate(tabs):
        tab_ref[2 * k] = r
        tab_ref[2 * k + 1] = i


def _bd_mask():
    r = lax.broadcasted_iota(jnp.int32, (S5_W, S5_N), 0)
    c = lax.broadcasted_iota(jnp.int32, (S5_W, S5_N), 1)
    return jnp.right_shift(r, 4) == jnp.right_shift(c, 6)


def _s5_block_diag(bbr_t, bbi_t, cre_w, cim_w):
    mask = _bd_mask()
    bd = lambda t: jnp.where(mask, t, 0.0)
    return (bd(jnp.tile(bbr_t, (S5_G, 1))), bd(jnp.tile(bbi_t, (S5_G, 1))),
            bd(jnp.tile(cre_w, (1, S5_N // HEAD))), bd(jnp.tile(cim_w, (1, S5_N // HEAD))))


def _scan8(xr, xi, tab_ref, lanes, reverse):
    for k in range(3):
        sh = (8 - 2 ** k) if reverse else 2 ** k
        sr = pltpu.roll(xr, sh, 0)
        si = pltpu.roll(xi, sh, 0)
        mr = tab_ref[2 * k, :, lanes]
        mi = tab_ref[2 * k + 1, :, lanes]
        xr, xi = xr + (mr * sr - mi * si), xi + (mr * si + mi * sr)
    return xr, xi


S5_LANES = 256


def _gelu(y):
    t = jnp.tanh(GELU_C * (y + GELU_A * y * y * y))
    return 0.5 * y * (1.0 + t), t


def _s5_fwd(u, lre, lim, ldt, bre_t, bim_t, cre_w, cim_w, d_row, w_glu, b_glu, jobs=()):
    tp = u.shape[0]
    R = ROW_BLK

    def body(u_ref, lre_ref, lim_ref, ldt_ref, bre_ref, bim_ref, cre_ref, cim_ref, d_ref, wg_ref, bg_ref,
             y_ref, xr_ref, xi_ref, bbd_r, bbd_i, cbd_r, cbd_i, tab_ref, car_r, car_i):
        @pl.when(pl.program_id(0) == 0)
        def _():
            lbr, lbi, bbr, bbi = _s5_disc(lre_ref[...], lim_ref[...], ldt_ref[...], bre_ref[...], bim_ref[...])
            br, bi, cr, ci = _s5_block_diag(bbr, bbi, cre_ref[...], cim_ref[...])
            bbd_r[...] = br.astype(MM)
            bbd_i[...] = bi.astype(MM)
            cbd_r[...] = cr.astype(MM)
            cbd_i[...] = ci.astype(MM)
            _store_tables(tab_ref, _s5_tables(lbr, lbi, False))
            car_r[...] = jnp.zeros_like(car_r)
            car_i[...] = jnp.zeros_like(car_i)

        u = u_ref[...]
        ub = u.astype(MM)
        xr_ref[...] = jnp.dot(ub, bbd_r[...], preferred_element_type=F32)
        xi_ref[...] = jnp.dot(ub, bbd_i[...], preferred_element_type=F32)
        for j in range(S5_N // S5_LANES):
            lanes = pl.ds(j * S5_LANES, S5_LANES)
            pr = tab_ref[6, :, lanes]
            pi = tab_ref[7, :, lanes]

            def step(g, carry):
                cr, ci = carry
                rows = pl.ds(pl.multiple_of(g * 8, 8), 8)
                xr, xi = _scan8(xr_ref[rows, lanes], xi_ref[rows, lanes], tab_ref, lanes, False)
                br = jnp.broadcast_to(cr[7:8, :], cr.shape)
                bi = jnp.broadcast_to(ci[7:8, :], ci.shape)
                xr = xr + (pr * br - pi * bi)
                xi = xi + (pr * bi + pi * br)
                xr_ref[rows, lanes] = xr
                xi_ref[rows, lanes] = xi
                return xr, xi

            cr, ci = lax.fori_loop(0, R // 8, step, (car_r[:, lanes], car_i[:, lanes]), unroll=2)
            car_r[:, lanes] = cr
            car_i[:, lanes] = ci
        y = _dot_nt(xr_ref[...], cbd_r[...]) - _dot_nt(xi_ref[...], cbd_i[...]) + d_ref[...] * u
        yg, _ = _gelu(y)
        z = _dot(yg, wg_ref[...]) + bg_ref[...]
        y_ref[...] = yg * jax.nn.sigmoid(z)

    full = lambda a: pl.BlockSpec(a.shape, lambda i: (0,) * a.ndim)
    small = [lre, lim, ldt, bre_t, bim_t, cre_w, cim_w, d_row, w_glu, b_glu]
    return _call(
        body, "s5_fwd", (tp // R,),
        [pl.BlockSpec((R, S5_W), lambda i: (i, 0))] + [full(a) for a in small],
        [pl.BlockSpec((R, S5_W), lambda i: (i, 0)), pl.BlockSpec((R, S5_N), lambda i: (i, 0)),
         pl.BlockSpec((R, S5_N), lambda i: (i, 0))],
        [jax.ShapeDtypeStruct((tp, S5_W), F32), jax.ShapeDtypeStruct((tp, S5_N), F32),
         jax.ShapeDtypeStruct((tp, S5_N), F32)],
        [pltpu.VMEM((S5_W, S5_N), MM)] * 4 + [pltpu.VMEM((8, 8, S5_N), F32), pltpu.VMEM((8, S5_N), F32),
                                              pltpu.VMEM((8, S5_N), F32)],
        (u, *small), jobs)


def _s5_bwd(dy_out, u, xr, xi, lre, lim, ldt, bre_t, bim_t, cre_w, cim_w, d_row, w_glu, b_glu, jobs=()):
    tp = u.shape[0]
    R = ROW_BLK
    nb = tp // R

    def body(dyo_ref, u_ref, xr_ref, xi_ref, xpr_ref, xpi_ref,
             lre_ref, lim_ref, ldt_ref, bre_ref, bim_ref, cre_ref, cim_ref, d_ref, wg_ref, bg_ref,
             du_ref, dlre_ref, dlim_ref, dldt_ref, dbre_ref, dbim_ref, dcre_ref, dcim_ref, dd_ref, dwg_ref, dbg_ref,
             bbd_r, bbd_i, cbd_r, cbd_i, tab_ref, car_r, car_i, gr_ref, gi_ref, xer_ref, xei_ref,
             abr, abi, acr, aci, adr, adi):
        i = pl.program_id(0)

        @pl.when(i == 0)
        def _():
            lbr, lbi, bbr, bbi = _s5_disc(lre_ref[...], lim_ref[...], ldt_ref[...], bre_ref[...], bim_ref[...])
            br, bi, cr, ci = _s5_block_diag(bbr, bbi, cre_ref[...], cim_ref[...])
            bbd_r[...] = br.astype(MM)
            bbd_i[...] = bi.astype(MM)
            cbd_r[...] = cr.astype(MM)
            cbd_i[...] = ci.astype(MM)
            _store_tables(tab_ref, _s5_tables(lbr, lbi, True))
            for ref in (car_r, car_i, abr, abi, acr, aci, adr, adi, dd_ref, dwg_ref, dbg_ref):
                ref[...] = jnp.zeros_like(ref)

        u = u_ref[...]
        xrv = xr_ref[...]
        xiv = xi_ref[...]
        y = _dot_nt(xrv, cbd_r[...]) - _dot_nt(xiv, cbd_i[...]) + d_ref[...] * u
        yg, t = _gelu(y)
        z = _dot(yg, wg_ref[...]) + bg_ref[...]
        s = jax.nn.sigmoid(z)
        dout = dyo_ref[...]
        dz = dout * yg * s * (1.0 - s)
        dyg = dout * s + _dot_nt(dz, wg_ref[...])
        dwg_ref[...] += _dot_tn(yg, dz)
        dbg_ref[...] += _colsum(dz)
        dy = dyg * (0.5 * (1.0 + t) + 0.5 * y * (1.0 - t * t) * GELU_C * (1.0 + 3.0 * GELU_A * y * y))
        dd_ref[...] += _colsum(dy * u)
        acr[...] += _dot_tn(dy, xrv)
        aci[...] -= _dot_tn(dy, xiv)
        gr_ref[...] = _dot(dy, cbd_r[...])
        gi_ref[...] = -_dot(dy, cbd_i[...])
        has_prev = (i < nb - 1).astype(F32)
        xer_ref[0:8, :] = xpr_ref[...] * has_prev
        xei_ref[0:8, :] = xpi_ref[...] * has_prev
        xer_ref[8:R + 8, :] = xrv
        xei_ref[8:R + 8, :] = xiv
        row = lax.broadcasted_iota(jnp.int32, (8, S5_LANES), 0)
        for j in range(S5_N // S5_LANES):
            lanes = pl.ds(j * S5_LANES, S5_LANES)
            pr = tab_ref[6, :, lanes]
            pi = tab_ref[7, :, lanes]

            def step(n, carry):
                cr, ci, sar, sai = carry
                g = R // 8 - 1 - n
                r0 = pl.multiple_of(g * 8, 8)
                rows = pl.ds(r0, 8)
                gr, gi = _scan8(gr_ref[rows, lanes], gi_ref[rows, lanes], tab_ref, lanes, True)
                br = jnp.broadcast_to(cr[0:1, :], cr.shape)
                bi = jnp.broadcast_to(ci[0:1, :], ci.shape)
                gr = gr + (pr * br - pi * bi)
                gi = gi + (pr * bi + pi * br)
                gr_ref[rows, lanes] = gr
                gi_ref[rows, lanes] = gi
                last = row == 7
                xpr = pltpu.roll(jnp.where(last, xer_ref[rows, lanes], xer_ref[pl.ds(r0 + 8, 8), lanes]), 1, 0)
                xpi = pltpu.roll(jnp.where(last, xei_ref[rows, lanes], xei_ref[pl.ds(r0 + 8, 8), lanes]), 1, 0)
                return gr, gi, sar + (gr * xpr + gi * xpi), sai + (gi * xpr - gr * xpi)

            cr, ci, sar, sai = lax.fori_loop(
                0, R // 8, step, (car_r[:, lanes], car_i[:, lanes], adr[:, lanes], adi[:, lanes]), unroll=2)
            car_r[:, lanes] = cr
            car_i[:, lanes] = ci
            adr[:, lanes] = sar
            adi[:, lanes] = sai
        grv = gr_ref[...]
        giv = gi_ref[...]
        du_ref[...] = dy * d_ref[...] + _dot_nt(grv, bbd_r[...]) + _dot_nt(giv, bbd_i[...])
        abr[...] += _dot_tn(u, grv)
        abi[...] += _dot_tn(u, giv)

        @pl.when(i == nb - 1)
        def _():
            mask = _bd_mask()
            r16 = lax.broadcasted_iota(jnp.int32, (S5_H, S5_W), 1)
            h16 = lax.broadcasted_iota(jnp.int32, (S5_H, S5_W), 0)
            fold_b = jnp.bitwise_and(r16, S5_H - 1) == h16
            c64 = lax.broadcasted_iota(jnp.int32, (S5_N, S5_P), 0)
            p64 = lax.broadcasted_iota(jnp.int32, (S5_N, S5_P), 1)
            fold_c = jnp.bitwise_and(c64, S5_P - 1) == p64
            dbbr = _dot_sel_lhs(fold_b, jnp.where(mask, abr[...], 0.0))
            dbbi = _dot_sel_lhs(fold_b, jnp.where(mask, abi[...], 0.0))
            dcre_ref[...] = _dot_sel_rhs(jnp.where(mask, acr[...], 0.0), fold_c)
            dcim_ref[...] = _dot_sel_rhs(jnp.where(mask, aci[...], 0.0), fold_c)
            dlbr = _colsum(adr[...])
            dlbi = _colsum(adi[...])
            _, vjp = jax.vjp(_s5_disc, lre_ref[...], lim_ref[...], ldt_ref[...], bre_ref[...], bim_ref[...])
            dlre, dlim, dldt, dbre, dbim = vjp((dlbr, dlbi, dbbr, dbbi))
            dlre_ref[...] = dlre
            dlim_ref[...] = dlim
            dbre_ref[...] = dbre
            dbim_ref[...] = dbim
            gsel = jnp.right_shift(lax.broadcasted_iota(jnp.int32, (S5_N, HEAD), 0), 6) == \
                lax.broadcasted_iota(jnp.int32, (S5_N, HEAD), 1)
            dldt_ref[...] = _dot_sel_rhs(dldt, gsel)

    full = lambda a: pl.BlockSpec(a.shape, lambda i: (0,) * a.ndim)
    rev = lambda w: pl.BlockSpec((R, w), lambda i: (nb - 1 - i, 0))
    prev8 = pl.BlockSpec((8, S5_N), lambda i: (jnp.maximum((nb - 1 - i) * (R // 8) - 1, 0), 0))
    small = [lre, lim, ldt, bre_t, bim_t, cre_w, cim_w, d_row, w_glu, b_glu]
    outs = [((tp, S5_W), rev(S5_W))] + [
        (s, pl.BlockSpec(s, lambda i: (0, 0))) for s in
        [(1, S5_N), (1, S5_N), (1, HEAD), (S5_H, S5_N), (S5_H, S5_N), (S5_W, S5_P), (S5_W, S5_P),
         (1, S5_W), (S5_W, S5_W), (1, S5_W)]]
    return _call(
        body, "s5_bwd", (nb,),
        [rev(S5_W), rev(S5_W), rev(S5_N), rev(S5_N), prev8, prev8] + [full(a) for a in small],
        [o[1] for o in outs], [jax.ShapeDtypeStruct(o[0], F32) for o in outs],
        [pltpu.VMEM((S5_W, S5_N), MM)] * 4 + [
            pltpu.VMEM((8, 8, S5_N), F32), pltpu.VMEM((8, S5_N), F32), pltpu.VMEM((8, S5_N), F32),
            pltpu.VMEM((R, S5_N), F32), pltpu.VMEM((R, S5_N), F32),
            pltpu.VMEM((R + 8, S5_N), F32), pltpu.VMEM((R + 8, S5_N), F32)] + [pltpu.VMEM((S5_W, S5_N), F32)] * 4 + [
            pltpu.VMEM((8, S5_N), F32), pltpu.VMEM((8, S5_N), F32)],
        (dy_out, u, xr, xi, xr, xi, *small), jobs)


def _ret_fwd(q, k, v, dmat, zeta_b, xi_b, gam_b, jobs=()):
    tp = q.shape[0]
    nc = tp // CHUNK

    def body(q_ref, k_ref, v_ref, dm_ref, ze_ref, xi_ref, ga_ref, o_ref, st_ref, s_ref):
        @pl.when(pl.program_id(0) == 0)
        def _():
            s_ref[...] = jnp.zeros_like(s_ref)

        for h in range(RET_H):
            sl = slice(h * HEAD, (h + 1) * HEAD)
            qh, kh, vh = q_ref[:, sl], k_ref[:, sl], v_ref[:, sl]
            sh = s_ref[h]
            st_ref[0, sl, :] = sh
            scores = _dot_nt(qh, kh) * dm_ref[h]
            o_ref[:, sl] = _dot(scores, vh) + _dot(qh, sh) * xi_ref[h]
            s_ref[h] = ga_ref[h] * sh + _dot_tn(kh.astype(F32) * ze_ref[h], vh)

    blk = pl.BlockSpec((CHUNK, RET_W), lambda c: (c, 0))
    cst = pl.BlockSpec((RET_H, HEAD, HEAD), lambda c: (0, 0, 0))
    return _call(
        body, "ret_fwd", (nc,), [blk, blk, blk, cst, cst, cst, cst],
        [blk, pl.BlockSpec((1, RET_W, HEAD), lambda c: (c, 0, 0))],
        [jax.ShapeDtypeStruct((tp, RET_W), F32), jax.ShapeDtypeStruct((nc, RET_W, HEAD), F32)],
        [pltpu.VMEM((RET_H, HEAD, HEAD), F32)], (q, k, v, dmat, zeta_b, xi_b, gam_b), jobs)


def _ret_bwd(q, k, v, do, states, cos2, sin2, dmat, zeta_b, xi_b, gam_b, jobs=()):
    tp = q.shape[0]
    nc = tp // CHUNK

    def body(q_ref, k_ref, v_ref, do_ref, st_ref, cos_ref, sin_ref, dm_ref, ze_ref, xi_ref, ga_ref,
             dq_ref, dk_ref, dv_ref, ds_ref):
        @pl.when(pl.program_id(0) == 0)
        def _():
            ds_ref[...] = jnp.zeros_like(ds_ref)

        cos = cos_ref[...]
        sin = sin_ref[...]
        for h in range(RET_H):
            sl = slice(h * HEAD, (h + 1) * HEAD)
            qh, kh, vh = q_ref[:, sl], k_ref[:, sl], v_ref[:, sl]
            dmh = dm_ref[h]
            sh = st_ref[0, sl, :]
            dsn = ds_ref[h]
            doh = do_ref[:, sl]
            dox = doh * xi_ref[h]
            a = _dot_nt(qh, kh) * dmh
            dqk = _dot_nt(doh, vh) * dmh
            kz = kh.astype(F32) * ze_ref[h]
            dv_ref[:, sl] = _dot_tn(a, doh) + _dot(kz, dsn)
            dqr = _dot(dqk, kh) + _dot_nt(dox, sh)
            dkr = _dot_tn(dqk, qh) + ze_ref[h] * _dot_nt(vh, dsn)
            ds_ref[h] = ga_ref[h] * dsn + _dot_tn(qh, dox)
            dq_ref[:, sl] = dqr * cos - pltpu.roll(dqr, HEAD // 2, 1) * sin
            dk_ref[:, sl] = (dkr * cos - pltpu.roll(dkr, HEAD // 2, 1) * sin) * (HEAD ** -0.5)

    blk = pl.BlockSpec((CHUNK, RET_W), lambda c: (nc - 1 - c, 0))
    tab = pl.BlockSpec((CHUNK, HEAD), lambda c: (nc - 1 - c, 0))
    cst = pl.BlockSpec((RET_H, HEAD, HEAD), lambda c: (0, 0, 0))
    return _call(
        body, "ret_bwd", (nc,),
        [blk, blk, blk, blk, pl.BlockSpec((1, RET_W, HEAD), lambda c: (nc - 1 - c, 0, 0)), tab, tab, cst, cst, cst, cst],
        [blk, blk, blk], [jax.ShapeDtypeStruct((tp, RET_W), F32)] * 3, [pltpu.VMEM((RET_H, HEAD, HEAD), F32)],
        (q, k, v, do, states, cos2, sin2, dmat, zeta_b, xi_b, gam_b), jobs)


def _gn_gate(o, gate, gn_g, gn_b):
    xhat, rstd = _ln_fwd(o, GN_EPS)
    on = xhat * gn_g + gn_b
    s = jax.nn.sigmoid(gate)
    return gate * s * on, xhat, rstd, on, s


def _post_fwd(o, gate, ys5, xhat0, tgt, gn_g, gn_b, li_g, li_b, l1_g, l1_b, l2_g, l2_b, w_out, w_up, w_down):
    tp = o.shape[0]
    seq = tgt.shape[0]
    R = ROW_BLK

    def body(o_ref, g_ref, ys_ref, xh0_ref, ta, tb, tc, gng, gnb, lig, lib, l1g, l1b, l2g, l2b, wo_ref, wu_ref, wd_ref,
             ycat_ref, xh1_ref, rstd1_ref, h1b_ref, dr2_ref, dffb_ref, loss_ref, dl2g_ref, dl2b_ref, tgt_ref):
        i = pl.program_id(0)

        @pl.when(i == 0)
        def _():
            for ref in (loss_ref, dl2g_ref, dl2b_ref):
                ref[...] = jnp.zeros_like(ref)

        tgt_ref[0:CHUNK, :] = ta[...]
        tgt_ref[CHUNK:2 * CHUNK, :] = tb[...]
        tgt_ref[2 * CHUNK:3 * CHUNK, :] = tc[...]
        ycat_ref[:, 0:S5_W] = ys_ref[...].astype(ycat_ref.dtype)
        for h in range(RET_H):
            sl = slice(h * HEAD, (h + 1) * HEAD)
            yret = _gn_gate(o_ref[:, sl], g_ref[:, sl], gng[:, sl], gnb[:, sl])[0]
            ycat_ref[:, S5_W + h * HEAD:S5_W + (h + 1) * HEAD] = yret.astype(ycat_ref.dtype)
        mixed = _dot(ycat_ref[...], wo_ref[...])
        h0 = xh0_ref[...] * lig[...] + lib[...]
        xh1, rstd1 = _ln_fwd(ALPHA * h0 + mixed, LN_EPS)
        xh1_ref[...] = xh1
        rstd1_ref[...] = rstd1
        h1 = xh1 * l1g[...] + l1b[...]
        h1b = h1.astype(MM)
        h1b_ref[...] = h1b
        ff = jnp.zeros((R, D_MODEL), F32)
        for d in range(N_DEV):
            pre = jnp.maximum(_dot(h1b, wu_ref[d]), 0.0)
            ff = ff + _dot(pre * pre, wd_ref[d * FF_BLK:(d + 1) * FF_BLK, :])
        xh2, rstd2 = _ln_fwd(ALPHA * h1 + ff, LN_EPS)
        h2 = xh2 * l2g[...] + l2b[...]
        valid = (i * R + lax.broadcasted_iota(jnp.int32, (R, 1), 0)) >= CHUNK
        err = jnp.where(valid, h2 - tgt_ref[...], 0.0)
        loss_ref[...] += 0.5 * jnp.sum(err * err) / D_MODEL
        dh2 = err * (1.0 / D_MODEL)
        dl2g_ref[...] += _colsum(dh2 * xh2)
        dl2b_ref[...] += _colsum(dh2)
        dr2 = _ln_bwd(dh2 * l2g[...], xh2, rstd2)
        dr2_ref[...] = dr2
        dffb_ref[...] = dr2.astype(MM)

    row = lambda w: pl.BlockSpec((R, w), lambda i: (i, 0))
    full = lambda a: pl.BlockSpec(a.shape, lambda i: (0,) * a.ndim)
    vecs = [gn_g, gn_b, li_g, li_b, l1_g, l1_b, l2_g, l2_b]
    acc = lambda s: (pl.BlockSpec(s, lambda i: (0, 0)), jax.ShapeDtypeStruct(s, F32))
    outs = [(row(D_MODEL), jax.ShapeDtypeStruct((tp, D_MODEL), MM)),
            (row(D_MODEL), jax.ShapeDtypeStruct((tp, D_MODEL), F32)),
            (row(1), jax.ShapeDtypeStruct((tp, 1), F32)),
            (row(D_MODEL), jax.ShapeDtypeStruct((tp, D_MODEL), MM)),
            (row(D_MODEL), jax.ShapeDtypeStruct((tp, D_MODEL), F32)),
            (row(D_MODEL), jax.ShapeDtypeStruct((tp, D_MODEL), MM)),
            acc((8, HEAD)), acc((1, D_MODEL)), acc((1, D_MODEL))]
    return pl.pallas_call(
        body, name="post_fwd", grid=(tp // R,),
        in_specs=[row(RET_W), row(RET_W), row(S5_W), row(D_MODEL)] + _shift3(seq // CHUNK) + [full(a) for a in vecs]
        + [_VMEM, _VMEM, _VMEM],
        out_specs=[o[0] for o in outs], out_shape=[o[1] for o in outs],
        scratch_shapes=[pltpu.VMEM((R, D_MODEL), F32)],
        compiler_params=_params(("arbitrary",)),
    )(o, gate, ys5, xhat0, tgt, tgt, tgt, *vecs, w_out, w_up, w_down)


def _mlp_bwd(h1b, dffb, w_up, w_down):
    tp = h1b.shape[0]
    R = ROW_BLK
    nr = tp // R

    def body(h_ref, df_ref, wu_ref, wd_ref, gup_ref, gdn_ref, dh1_ref, aup, adn):
        d = pl.program_id(0)
        r = pl.program_id(1)

        @pl.when(r == 0)
        def _():
            aup[...] = jnp.zeros_like(aup)
            adn[...] = jnp.zeros_like(adn)

        h = h_ref[...]
        df = df_ref[...]
        wu = wu_ref[0]
        wd = wd_ref[0]
        pre = jnp.maximum(_dot(h, wu), 0.0)
        dpre = (_dot_nt(df, wd) * (2.0 * pre)).astype(MM)
        aup[...] += _dot_tn(h, dpre)
        adn[...] += _dot_tn(pre * pre, df)
        contrib = _dot_nt(dpre, wu)
        rows = pl.ds(pl.multiple_of(r * R, CHUNK), R)

        @pl.when(d == 0)
        def _():
            dh1_ref[rows, :] = contrib

        @pl.when(d > 0)
        def _():
            dh1_ref[rows, :] += contrib

        @pl.when(r == nr - 1)
        def _():
            gup_ref[0] = aup[...].astype(gup_ref.dtype)
            gdn_ref[0] = adn[...].astype(gdn_ref.dtype)

    return pl.pallas_call(
        body, name="mlp_bwd", grid=(N_DEV, nr),
        in_specs=[pl.BlockSpec((R, D_MODEL), lambda d, r: (r, 0)), pl.BlockSpec((R, D_MODEL), lambda d, r: (r, 0)),
                  pl.BlockSpec((1, D_MODEL, FF_BLK), lambda d, r: (d, 0, 0)),
                  pl.BlockSpec((1, FF_BLK, D_MODEL), lambda d, r: (d, 0, 0))],
        out_specs=[pl.BlockSpec((1, D_MODEL, FF_BLK), lambda d, r: (d, 0, 0)),
                   pl.BlockSpec((1, FF_BLK, D_MODEL), lambda d, r: (d, 0, 0)), _VMEM],
        out_shape=[jax.ShapeDtypeStruct((N_DEV, D_MODEL, FF_BLK), MM), jax.ShapeDtypeStruct((N_DEV, FF_BLK, D_MODEL), MM),
                   jax.ShapeDtypeStruct((tp, D_MODEL), F32)],
        scratch_shapes=[pltpu.VMEM((D_MODEL, FF_BLK), F32), pltpu.VMEM((FF_BLK, D_MODEL), F32)],
        compiler_params=_params(("arbitrary", "arbitrary")),
    )(h1b, dffb, w_up, w_down.reshape(N_DEV, FF_BLK, D_MODEL))


def _post_bwd(dh1m, dr2, xhat1, rstd1, ycat, o, gate, gn_g, gn_b, l1_g, w_out, jobs=()):
    tp = o.shape[0]
    R = ROW_BLK
    nb = tp // R

    def body(dm_ref, dr2_ref, xh1_ref, rs1_ref, yc_ref, o_ref, g_ref, gng, gnb, l1g, wo_ref,
             do_ref, dg_ref, dys_ref, dh0_ref, gwo_ref, dl1g_ref, dl1b_ref, dgng_ref, dgnb_ref, awo):
        i = pl.program_id(0)

        @pl.when(i == 0)
        def _():
            for ref in (awo, dl1g_ref, dl1b_ref, dgng_ref, dgnb_ref):
                ref[...] = jnp.zeros_like(ref)

        dh1 = dm_ref[...] + ALPHA * dr2_ref[...]
        xh1 = xh1_ref[...]
        dl1g_ref[...] += _colsum(dh1 * xh1)
        dl1b_ref[...] += _colsum(dh1)
        dr1 = _ln_bwd(dh1 * l1g[...], xh1, rs1_ref[...])
        dh0_ref[...] = ALPHA * dr1
        dmix = dr1.astype(MM)
        awo[...] += _dot_tn(yc_ref[...], dmix)
        dyc = _dot_nt(dmix, wo_ref[...])
        dys_ref[...] = dyc[:, 0:S5_W]
        for h in range(RET_H):
            sl = slice(h * HEAD, (h + 1) * HEAD)
            gt = g_ref[:, sl]
            _, xhat, rstd, on, s = _gn_gate(o_ref[:, sl], gt, gng[:, sl], gnb[:, sl])
            dyr = dyc[:, S5_W + h * HEAD:S5_W + (h + 1) * HEAD]
            dg_ref[:, sl] = dyr * on * (s * (1.0 + gt * (1.0 - s)))
            don = dyr * gt * s
            dgng_ref[:, sl] += _colsum(don * xhat)
            dgnb_ref[:, sl] += _colsum(don)
            do_ref[:, sl] = _ln_bwd(don * gng[:, sl], xhat, rstd)

        @pl.when(i == nb - 1)
        def _():
            gwo_ref[...] = awo[...].astype(gwo_ref.dtype)

    row = lambda w: pl.BlockSpec((R, w), lambda i: (i, 0))
    full = lambda a: pl.BlockSpec(a.shape, lambda i: (0,) * a.ndim)
    acc = lambda s, dt=F32: (pl.BlockSpec(s, lambda i: (0, 0)), jax.ShapeDtypeStruct(s, dt))
    outs = [(row(RET_W), jax.ShapeDtypeStruct((tp, RET_W), F32)), (row(RET_W), jax.ShapeDtypeStruct((tp, RET_W), F32)),
            (row(S5_W), jax.ShapeDtypeStruct((tp, S5_W), F32)), (row(D_MODEL), jax.ShapeDtypeStruct((tp, D_MODEL), F32)),
            acc((D_MODEL, D_MODEL), MM), acc((1, D_MODEL)), acc((1, D_MODEL)), acc((1, RET_W)), acc((1, RET_W))]
    return _call(
        body, "post_bwd", (nb,),
        [row(D_MODEL), row(D_MODEL), row(D_MODEL), row(1), row(D_MODEL), row(RET_W), row(RET_W),
         full(gn_g), full(gn_b), full(l1_g), _VMEM],
        [o[0] for o in outs], [o[1] for o in outs], [pltpu.VMEM((D_MODEL, D_MODEL), F32)],
        (dh1m, dr2, xhat1, rstd1, ycat, o, gate, gn_g, gn_b, l1_g, w_out), jobs)


def _in_bwd(du, dq, dk, dv, dg, dh0r, xhat0, rstd0, li_g, li_b, w_int):
    tp = du.shape[0]
    R = ROW_BLK
    nb = tp // R
    segs = [(0, S5_W)] + [(S5_W + n * RET_W, S5_W + (n + 1) * RET_W) for n in range(4)]

    def body(du_ref, dq_ref, dk_ref, dv_ref, dg_ref, dh0r_ref, xh_ref, rs_ref, lig, lib, w_ref,
             draw_ref, gw_ref, dlg_ref, dlb_ref, aw):
        i = pl.program_id(0)

        @pl.when(i == 0)
        def _():
            for ref in (aw, dlg_ref, dlb_ref):
                ref[...] = jnp.zeros_like(ref)

        valid = (i * R + lax.broadcasted_iota(jnp.int32, (R, 1), 0)) >= PAD
        xh = xh_ref[...]
        hb = (xh * lig[...] + lib[...]).astype(MM)
        dh0 = dh0r_ref[...]
        for (lo, hi), ref in zip(segs, (du_ref, dq_ref, dk_ref, dv_ref, dg_ref)):
            dseg = jnp.where(valid, ref[...], 0.0).astype(MM)
            dh0 = dh0 + _dot(dseg, w_ref[lo:hi, :])
            aw[lo:hi, :] += _dot_tn(dseg, hb)
        dlg_ref[...] += _colsum(dh0 * xh)
        dlb_ref[...] += _colsum(dh0)
        draw_ref[...] = _ln_bwd(dh0 * lig[...], xh, rs_ref[...])

        @pl.when(i == nb - 1)
        def _():
            gw_ref[...] = aw[...].astype(gw_ref.dtype)

    row = lambda w: pl.BlockSpec((R, w), lambda i: (i, 0))
    full = lambda a: pl.BlockSpec(a.shape, lambda i: (0,) * a.ndim)
    acc = lambda s, dt=F32: (pl.BlockSpec(s, lambda i: (0, 0)), jax.ShapeDtypeStruct(s, dt))
    outs = [(row(D_MODEL), jax.ShapeDtypeStruct((tp, D_MODEL), F32)), acc((PROJ_W, D_MODEL), MM),
            acc((1, D_MODEL)), acc((1, D_MODEL))]
    return pl.pallas_call(
        body, name="in_bwd", grid=(nb,),
        in_specs=[row(S5_W), row(RET_W), row(RET_W), row(RET_W), row(RET_W), row(D_MODEL), row(D_MODEL), row(1),
                  full(li_g), full(li_b), _VMEM],
        out_specs=[o[0] for o in outs], out_shape=[o[1] for o in outs],
        scratch_shapes=[pltpu.VMEM((PROJ_W, D_MODEL), F32)],
        compiler_params=_params(("arbitrary",)),
    )(du, dq, dk, dv, dg, dh0r, xhat0, rstd0, li_g, li_b, w_int)


def _place():
    return lax.axis_index("x"), lax.axis_index("y"), lax.axis_index("c")


def _dma_sems(n):
    return pltpu.SemaphoreType.DMA((n,))


def _job_gather(shard):
    def parts(ins, outs, sems):
        (src,), (out,), (send_sems, recv_sems, local_sem) = ins, outs, sems
        x, y, c = _place()
        me, sib = (x, y, c), (x, y, 1 - c)
        chips = [(1 - x, y), (x, 1 - y), (1 - x, 1 - y)]

        def slot(dev):
            return out.at[4 * dev[0] + 2 * dev[1] + dev[2]]

        def copy(k, block, to, from_input=False):
            return pltpu.make_async_remote_copy(
                src_ref=src if from_input else slot(block), dst_ref=slot(block),
                send_sem=send_sems.at[k], recv_sem=recv_sems.at[k], device_id=to, device_id_type=_MESH)

        mine = pltpu.make_async_copy(src, slot(me), local_sem.at[0])
        first = [copy(0, me, sib, True)] + [copy(1 + j, me, (*chip, c), True) for j, chip in enumerate(chips)]
        return me, sib, chips, copy, mine, first

    def start(ins, outs, sems):
        _, _, _, _, mine, first = parts(ins, outs, sems)
        mine.start()
        for cp in first:
            cp.start()

    def finish(ins, outs, sems):
        me, sib, chips, copy, mine, first = parts(ins, outs, sems)
        c = me[2]
        passed = []
        for j, chip in enumerate(chips):
            copy(1 + j, (*chip, c), me).wait_recv()
            cp = copy(4 + j, (*chip, c), sib)
            cp.start()
            passed.append(cp)
        copy(0, sib, me).wait_recv()
        for j, chip in enumerate(chips):
            copy(4 + j, (*chip, 1 - c), me).wait_recv()
        for cp in first + passed:
            cp.wait_send()
        mine.wait()

    return dict(ins=[shard], outs=[jax.ShapeDtypeStruct((N_DEV,) + shard.shape, shard.dtype)],
                sems=[_dma_sems(7), _dma_sems(7), _dma_sems(1)], start=start, finish=finish)


def _job_pair(g):
    def copies(ins, outs, sems):
        x, y, c = _place()
        return [pltpu.make_async_remote_copy(
            src_ref=ins[0].at[2 * j + (1 - c)], dst_ref=outs[0].at[j], send_sem=sems[0].at[j], recv_sem=sems[1].at[j],
            device_id=(x, y, 1 - c), device_id_type=_MESH) for j in range(4)]

    def start(ins, outs, sems):
        for cp in copies(ins, outs, sems):
            cp.start()

    def finish(ins, outs, sems):
        for cp in copies(ins, outs, sems):
            cp.wait()

    return dict(ins=[g], outs=[jax.ShapeDtypeStruct((4,) + g.shape[1:], g.dtype)], sems=[_dma_sems(4), _dma_sems(4)],
                start=start, finish=finish)


def _job_chips(p):
    def copies(ins, outs, sems):
        x, y, c = _place()
        chips = [(1 - x, y), (x, 1 - y), (1 - x, 1 - y)]
        return [pltpu.make_async_remote_copy(
            src_ref=ins[0].at[2 * chip[0] + chip[1]], dst_ref=outs[0].at[k], send_sem=sems[0].at[k],
            recv_sem=sems[1].at[k], device_id=(*chip, c), device_id_type=_MESH) for k, chip in enumerate(chips)]

    def start(ins, outs, sems):
        for cp in copies(ins, outs, sems):
            cp.start()

    def finish(ins, outs, sems):
        for cp in copies(ins, outs, sems):
            cp.wait()

    return dict(ins=[p], outs=[jax.ShapeDtypeStruct((3,) + p.shape[1:], p.dtype)], sems=[_dma_sems(3), _dma_sems(3)],
                start=start, finish=finish)


def _split_job_refs(jobs, ins, outs, sems):
    res, a, b, c = [], 0, 0, 0
    for job in jobs:
        na, nb, nc = len(job["ins"]), len(job["outs"]), len(job["sems"])
        res.append((ins[a:a + na], outs[b:b + nb], sems[c:c + nc]))
        a, b, c = a + na, b + nb, c + nc
    return res


def _call(body, name, grid, in_specs, out_specs, out_shape, scratch, args, jobs=()):
    jobs = list(jobs)
    n_in, n_out, n_scr = len(in_specs), len(out_specs), len(scratch)
    j_in = [a for job in jobs for a in job["ins"]]
    j_out = [o for job in jobs for o in job["outs"]]
    j_scr = [s for job in jobs for s in job["sems"]]
    nsteps = grid[0]

    def wrapped(*refs):
        ins, jins = refs[:n_in], refs[n_in:n_in + len(j_in)]
        refs = refs[n_in + len(j_in):]
        outs, jouts = refs[:n_out], refs[n_out:n_out + len(j_out)]
        refs = refs[n_out + len(j_out):]
        scr, jscr = refs[:n_scr], refs[n_scr:]
        per_job = _split_job_refs(jobs, jins, jouts, jscr)

        @pl.when(pl.program_id(0) == 0)
        def _():
            for job, r in zip(jobs, per_job):
                job["start"](*r)

        body(*ins, *outs, *scr)

        @pl.when(pl.program_id(0) == nsteps - 1)
        def _():
            for job, r in zip(jobs, per_job):
                job["finish"](*r)

    res = pl.pallas_call(
        wrapped if jobs else body, name=name, grid=grid,
        in_specs=list(in_specs) + [_ANY] * len(j_in), out_specs=list(out_specs) + [_ANY] * len(j_out),
        out_shape=list(out_shape) + j_out, scratch_shapes=list(scratch) + j_scr,
        compiler_params=_params(("arbitrary",) * len(grid)),
    )(*args, *j_in)
    return list(res[:n_out]), list(res[n_out:])


def _exchange(jobs, name):
    j_in = [a for job in jobs for a in job["ins"]]
    j_out = [o for job in jobs for o in job["outs"]]
    j_scr = [s for job in jobs for s in job["sems"]]

    def body(*refs):
        per_job = _split_job_refs(jobs, refs[:len(j_in)], refs[len(j_in):len(j_in) + len(j_out)],
                                  refs[len(j_in) + len(j_out):])
        for job, r in zip(jobs, per_job):
            job["start"](*r)
        for job, r in zip(jobs, per_job):
            job["finish"](*r)

    return pl.pallas_call(body, name=name, out_shape=j_out, in_specs=[_ANY] * len(j_in), out_specs=[_ANY] * len(j_out),
                          scratch_shapes=j_scr)(*j_in)


def _pair_sum(gs, r1s, c_arr, name):
    n = len(gs)

    def body(c_ref, *refs):
        for a in range(n):
            refs[2 * n + a][...] = (refs[a][...].astype(F32) + refs[n + a][...].astype(F32)).astype(refs[2 * n + a].dtype)

    def blk(g, own):
        s = g.shape[1:]
        if own:
            return pl.BlockSpec((1,) + s, lambda j, c_ref: (2 * j + c_ref[0],) + (0,) * len(s))
        return pl.BlockSpec((1,) + s, lambda j, c_ref: (j,) + (0,) * len(s))

    return pl.pallas_call(
        body, name=name,
        grid_spec=pltpu.PrefetchScalarGridSpec(
            num_scalar_prefetch=1, grid=(4,),
            in_specs=[blk(g, True) for g in gs] + [blk(g, False) for g in gs],
            out_specs=[blk(g, False) for g in gs]),
        out_shape=[jax.ShapeDtypeStruct((4,) + g.shape[1:], g.dtype) for g in gs],
        compiler_params=_params(("arbitrary",)),
    )(c_arr, *gs, *r1s)


def _chip_sum(ps, r2s, j_arr):
    n = len(ps)

    def body(j_ref, *refs):
        for a in range(n):
            r2 = refs[n + a]
            refs[2 * n + a][...] = ((refs[a][0].astype(F32) + r2[0].astype(F32)) + r2[1].astype(F32)) + r2[2].astype(F32)

    def own(p):
        s = p.shape[1:]
        return pl.BlockSpec((1,) + s, lambda i, j_ref: (j_ref[0],) + (0,) * len(s))

    def whole(p):
        return pl.BlockSpec(p.shape, lambda i, j_ref: (0,) * p.ndim)

    return pl.pallas_call(
        body, name="chip_sum",
        grid_spec=pltpu.PrefetchScalarGridSpec(
            num_scalar_prefetch=1, grid=(1,),
            in_specs=[own(p) for p in ps] + [whole(r) for r in r2s],
            out_specs=[pl.BlockSpec(p.shape[1:], lambda i, j_ref: (0,) * (p.ndim - 1)) for p in ps]),
        out_shape=[jax.ShapeDtypeStruct(p.shape[1:], F32) for p in ps],
        compiler_params=_params(("arbitrary",)),
    )(j_arr, *ps, *r2s)


def _adamw_math(w, g, m, v):
    m = ADAM_B1 * m + (1.0 - ADAM_B1) * g
    v = ADAM_B2 * v + (1.0 - ADAM_B2) * (g * g)
    m_hat = m / (1.0 - ADAM_B1 ** ADAM_STEP)
    v_hat = v / (1.0 - ADAM_B2 ** ADAM_STEP)
    return -ADAM_LR * (m_hat / (jnp.sqrt(v_hat) + ADAM_EPS) + ADAM_WD * w), m, v


def _adamw(items, name, steps):
    n = len(items)

    def body(*refs):
        for a in range(n):
            g, w, m, v = (refs[4 * a + t][...] for t in range(4))
            d, m2, v2 = _adamw_math(w, g, m, v)
            refs[4 * n + 3 * a][...] = d
            refs[4 * n + 3 * a + 1][...] = m2
            refs[4 * n + 3 * a + 2][...] = v2

    def blk(arr):
        r, c = arr.shape
        return pl.BlockSpec((r // steps, c), lambda i: (i, 0))

    flat = [t for it in items for t in it]
    return pl.pallas_call(
        body, name=name, grid=(steps,),
        in_specs=[blk(t) for t in flat],
        out_specs=[blk(it[1]) for it in items for _ in range(3)],
        out_shape=[jax.ShapeDtypeStruct(it[1].shape, F32) for it in items for _ in range(3)],
        compiler_params=_params(("arbitrary",)),
    )(*flat)


def _adamw_small(gathered, w, m, v):
    def body(gs_ref, w_ref, m_ref, v_ref, g_ref, d_ref, m2_ref, v2_ref):
        g = gs_ref[0]
        for s in range(1, N_DEV):
            g = g + gs_ref[s]
        g_ref[...] = g
        d_ref[...], m2_ref[...], v2_ref[...] = _adamw_math(w_ref[...], g, m_ref[...], v_ref[...])

    return pl.pallas_call(
        body, name="adamw_small", out_shape=[jax.ShapeDtypeStruct(w.shape, F32)] * 4,
        in_specs=[_VMEM] * 4, out_specs=[_VMEM] * 4, compiler_params=_params(),
    )(gathered, w, m, v)


SMALL = ["ln_in_g", "ln_in_b", "s5_lambda_re", "s5_lambda_im", "s5_log_dt", "s5_b_re", "s5_b_im", "s5_c_re", "s5_c_im",
         "s5_d", "s5_b_glu", "ret_gn_g", "ret_gn_b", "ln1_g", "ln1_b", "ln2_g", "ln2_b"]
LANE = 128


def _pack(arrs):
    parts = []
    for a in arrs:
        f = a.reshape(-1)
        parts.append(jnp.pad(f, (0, (-f.shape[0]) % LANE)))
    flat = jnp.concatenate(parts)
    rows = -(-flat.shape[0] // LANE)
    flat = jnp.pad(flat, (0, (-rows % 8) * LANE + rows * LANE - flat.shape[0]))
    return flat.reshape(-1, LANE)


def _unpack(packed, shapes):
    flat = packed.reshape(-1)
    out, off = [], 0
    for s in shapes:
        n = math.prod(s)
        out.append(flat[off:off + n].reshape(s))
        off += n + (-n) % LANE
    return out


def _rope_tables(tp):
    pos = jnp.arange(tp, dtype=F32) - float(PAD)
    inv_freq = 1.0 / (ROPE_BASE ** (jnp.arange(0, HEAD, 2, dtype=F32) / HEAD))
    ang = pos[:, None] * inv_freq[None, :]
    cos, sin = jnp.cos(ang), jnp.sin(ang)
    return jnp.concatenate([cos, cos], axis=1), jnp.concatenate([-sin, sin], axis=1)


def _decay_tables():
    log_gamma = jnp.log1p(-jnp.exp2(-5.0 - jnp.arange(RET_H, dtype=F32)))
    idx = jnp.arange(CHUNK, dtype=F32)
    diff = idx[:, None] - idx[None, :]
    dmat = jnp.where(diff[None] >= 0, jnp.exp(jnp.maximum(diff, 0.0)[None] * log_gamma[:, None, None]), 0.0)
    zeta = jnp.exp((CHUNK - 1.0 - idx)[None] * log_gamma[:, None])
    xi = jnp.exp((idx + 1.0)[None] * log_gamma[:, None])
    gam = jnp.exp(CHUNK * log_gamma)
    wide = lambda t: jnp.broadcast_to(t[:, :, None], (RET_H, CHUNK, HEAD))
    return dmat, wide(zeta), wide(xi), jnp.broadcast_to(gam[:, None, None], (RET_H, CHUNK, HEAD))


def _local_step(x2d, tgt, meta_full, w_int, w_out, w_up, w_down, w_glu, sp, distributed):
    tp = x2d.shape[0] + CHUNK
    row = lambda a: a.reshape(1, -1)
    cos2, sin2 = _rope_tables(tp)
    dmat, zeta_b, xi_b, gam_b = _decay_tables()
    li_g, li_b = row(sp["ln_in_g"]), row(sp["ln_in_b"])
    l1_g, l1_b, l2_g, l2_b = row(sp["ln1_g"]), row(sp["ln1_b"]), row(sp["ln2_g"]), row(sp["ln2_b"])
    gn_g, gn_b = row(sp["ret_gn_g"]), row(sp["ret_gn_b"])
    lre, lim = row(sp["s5_lambda_re"]), row(sp["s5_lambda_im"])
    ldt = row(jnp.repeat(sp["s5_log_dt"].reshape(-1), S5_P))
    to_t = lambda b: b.reshape(S5_G, S5_P, S5_H).transpose(2, 0, 1).reshape(S5_H, S5_N)
    bre_t, bim_t = to_t(sp["s5_b_re"]), to_t(sp["s5_b_im"])
    to_w = lambda c: jnp.tile(c.reshape(S5_W, S5_P), (1, 2))
    cre_w, cim_w = to_w(sp["s5_c_re"]), to_w(sp["s5_c_im"])
    s5_small = (lre, lim, ldt, bre_t, bim_t, cre_w, cim_w, row(sp["s5_d"]), w_glu, row(sp["s5_b_glu"]))

    jobs = (lambda *j: list(j)) if distributed else (lambda *j: [])
    c_arr = jnp.reshape(lax.axis_index("c"), (1,)).astype(jnp.int32) if distributed else None
    (xhat0, rstd0, u, q, k, v, gate), bg = _in_proj(x2d, meta_full, li_g, li_b, w_int, cos2, sin2,
                                                    jobs(_job_gather(w_out) if distributed else None))
    if distributed:
        w_out = bg[0].reshape(D_MODEL, D_MODEL)
    (ys5, xr, xi), bg = _s5_fwd(u, *s5_small, jobs=jobs(_job_gather(w_up) if distributed else None))
    if distributed:
        w_up = bg[0]
    (o, states), bg = _ret_fwd(q, k, v, dmat, zeta_b, xi_b, gam_b, jobs(_job_gather(w_down) if distributed else None))
    if distributed:
        w_down = bg[0].reshape(D_FF, D_MODEL)
    ycat, xhat1, rstd1, h1b, dr2, dffb, loss8, dl2g, dl2b = _post_fwd(
        o, gate, ys5, xhat0, tgt, gn_g, gn_b, li_g, li_b, l1_g, l1_b, l2_g, l2_b, w_out, w_up, w_down)
    g_up, g_down, dh1m = _mlp_bwd(h1b, dffb, w_up, w_down)
    (do, dgate, dys5, dh0r, g_out, dl1g, dl1b, dgng, dgnb), bg = _post_bwd(
        dh1m, dr2, xhat1, rstd1, ycat, o, gate, gn_g, gn_b, l1_g, w_out,
        jobs(*([_job_pair(g_up), _job_pair(g_down)] if distributed else [])))
    g_out = g_out.reshape(N_DEV, D_MODEL // N_DEV, D_MODEL)
    if distributed:
        p_up, p_down = _pair_sum([g_up, g_down], bg, c_arr, "pair_sum_mlp")
    (du, dlre, dlim, dldt, dbre_t, dbim_t, dcre, dcim, dd, dwglu, dbglu), bg = _s5_bwd(
        dys5, u, xr, xi, *s5_small,
        jobs=jobs(*([_job_chips(p_up), _job_chips(p_down), _job_pair(g_out)] if distributed else [])))
    if distributed:
        r_up, r_down = bg[0], bg[1]
        (p_out,) = _pair_sum([g_out], bg[2:], c_arr, "pair_sum_out")
    (dq, dk, dv), bg = _ret_bwd(q, k, v, do, states, cos2, sin2, dmat, zeta_b, xi_b, gam_b,
                                jobs(_job_chips(p_out) if distributed else None))
    draw, g_int, dlig, dlib = _in_bwd(du, dq, dk, dv, dgate, dh0r, xhat0, rstd0, li_g, li_b, w_int)
    g_int = g_int.reshape(N_DEV, PROJ_W // N_DEV, D_MODEL)
    if distributed:
        r_out = bg[0]
        (r1_in,) = _exchange([_job_pair(g_int)], "exchange_pair_in")
        (p_in,) = _pair_sum([g_int], [r1_in], c_arr, "pair_sum_in")
        big = dict(chip_sums=[p_in, p_out, p_up, p_down], received=[None, r_out, r_up, r_down])
    else:
        big = dict(partials=[g_int, g_out, g_up, g_down])

    from_t = lambda t: t.reshape(S5_H, S5_G, S5_P).transpose(1, 2, 0)
    small = {
        "ln_in_g": dlig, "ln_in_b": dlib, "s5_lambda_re": dlre, "s5_lambda_im": dlim, "s5_log_dt": dldt[:, :S5_G],
        "s5_b_re": from_t(dbre_t), "s5_b_im": from_t(dbim_t), "s5_c_re": dcre, "s5_c_im": dcim, "s5_d": dd,
        "s5_b_glu": dbglu, "ret_gn_g": dgng, "ret_gn_b": dgnb, "ln1_g": dl1g, "ln1_b": dl1b, "ln2_g": dl2g, "ln2_b": dl2b,
        "meta_tokens": draw[PAD:CHUNK], "s5_w_glu": dwglu, "loss": loss8[0:1, 0:1]}
    return draw, big, small


def kernel(x, meta_tokens, ln_in_g, ln_in_b, w_in, s5_lambda_re, s5_lambda_im, s5_log_dt, s5_b_re, s5_b_im, s5_c_re, s5_c_im, s5_d, s5_w_glu, s5_b_glu, ret_gn_g, ret_gn_b, w_out, ln1_g, ln1_b, w_up, w_down, ln2_g, ln2_b, loss_target, m_meta_tokens, m_ln_in_g, m_ln_in_b, m_w_in, m_s5_lambda_re, m_s5_lambda_im, m_s5_log_dt, m_s5_b_re, m_s5_b_im, m_s5_c_re, m_s5_c_im, m_s5_d, m_s5_w_glu, m_s5_b_glu, m_ret_gn_g, m_ret_gn_b, m_w_out, m_ln1_g, m_ln1_b, m_w_up, m_w_down, m_ln2_g, m_ln2_b, v_meta_tokens, v_ln_in_g, v_ln_in_b, v_w_in, v_s5_lambda_re, v_s5_lambda_im, v_s5_log_dt, v_s5_b_re, v_s5_b_im, v_s5_c_re, v_s5_c_im, v_s5_d, v_s5_w_glu, v_s5_b_glu, v_ret_gn_g, v_ret_gn_b, v_w_out, v_ln1_g, v_ln1_b, v_w_up, v_w_down, v_ln2_g, v_ln2_b):
    args = dict(locals())
    names = ["meta_tokens", "ln_in_g", "ln_in_b", "w_in", "s5_lambda_re", "s5_lambda_im", "s5_log_dt", "s5_b_re", "s5_b_im",
             "s5_c_re", "s5_c_im", "s5_d", "s5_w_glu", "s5_b_glu", "ret_gn_g", "ret_gn_b", "w_out", "ln1_g", "ln1_b",
             "w_up", "w_down", "ln2_g", "ln2_b"]
    ax, ay, ac = _place()
    me = 4 * ax + 2 * ay + ac

    a_int, a_glu, a_meta = _exchange([_job_gather(w_in[0].T.astype(MM)), _job_gather(s5_w_glu[0].astype(MM)),
                                      _job_gather(meta_tokens)], "gather_first")
    w_int = a_int.reshape(PROJ_W, D_MODEL)
    w_glu_f = a_glu.reshape(S5_W, S5_W)
    meta_full = a_meta.transpose(1, 0, 2).reshape(N_META, D_MODEL)

    sp = {n: args[n] for n in SMALL}
    draw, big, small = _local_step(x[0], loss_target[0], meta_full, w_int, w_out[0].astype(MM), w_up[0].astype(MM),
                                   w_down[0].astype(MM), w_glu_f, sp, True)

    order = SMALL + ["meta_tokens", "s5_w_glu", "loss"]
    shapes = [args[n].shape for n in SMALL] + [(N_META, D_MODEL), (S5_W, S5_W), (1,)]
    g_pack = _pack([small[n] for n in order])
    r_in, g_all = _exchange([_job_chips(big["chip_sums"][0]), _job_gather(g_pack)], "exchange_tail")
    g_int, g_out, g_up, g_down = _chip_sum(big["chip_sums"], [r_in] + big["received"][1:],
                                           jnp.reshape(2 * ax + ay, (1,)).astype(jnp.int32))
    big_grads = {"w_in": g_int.T[None], "w_out": g_out[None], "w_up": g_up[None], "w_down": g_down[None]}
    zeros = [jnp.zeros((N_META, D_MODEL), F32), jnp.zeros((S5_W, S5_W), F32), jnp.zeros((1,), F32)]
    packs = [_pack([args[p + n] for n in SMALL] + zeros) for p in ("", "m_", "v_")]
    g_sum, d_pack, m_pack, v_pack = _adamw_small(g_all, *packs)
    g_small = dict(zip(order, _unpack(g_sum, shapes)))
    loss = g_small["loss"].reshape(())
    d_small = dict(zip(order, _unpack(d_pack, shapes)))
    m_small = dict(zip(order, _unpack(m_pack, shapes)))
    v_small = dict(zip(order, _unpack(v_pack, shapes)))

    g_meta = lax.dynamic_slice(g_small["meta_tokens"], (0, me * (D_MODEL // N_DEV)), (N_META, D_MODEL // N_DEV))
    g_glu = lax.dynamic_slice(g_small["s5_w_glu"], (me * (S5_W // N_DEV), 0), (S5_W // N_DEV, S5_W))
    shard_grads = dict(big_grads, meta_tokens=g_meta, s5_w_glu=g_glu[None])
    sharded = ["w_in", "w_out", "w_up", "w_down"]
    two_d = lambda a: a.reshape(a.shape[-2:])
    res = _adamw([tuple(two_d(t) for t in (shard_grads[n], args[n], args["m_" + n], args["v_" + n])) for n in sharded],
                 "adamw_big", 8)
    res2 = _adamw([tuple(two_d(t) for t in (shard_grads[n], args[n], args["m_" + n], args["v_" + n]))
                   for n in ("meta_tokens", "s5_w_glu")], "adamw_shard_small", 1)
    upd = {}
    for idx, n in enumerate(sharded):
        upd[n] = [r.reshape(args[n].shape) for r in res[3 * idx:3 * idx + 3]]
    for idx, n in enumerate(("meta_tokens", "s5_w_glu")):
        upd[n] = [r.reshape(args[n].shape) for r in res2[3 * idx:3 * idx + 3]]

    grads, deltas, new_m, new_v = [], [], [], []
    for n in names:
        if n in upd:
            grads.append(shard_grads[n].reshape(args[n].shape))
            d, m2, v2 = upd[n]
        else:
            grads.append(g_small[n])
            d, m2, v2 = d_small[n], m_small[n], v_small[n]
        deltas.append(d)
        new_m.append(m2)
        new_v.append(v2)
    grad_x = draw[CHUNK:][None]
    return (loss, grad_x, *grads, *deltas, *new_m, *new_v)
```

```python
import math

import jax
import jax.numpy as jnp
from jax import lax
from jax.experimental import pallas as pl
from jax.experimental.pallas import tpu as pltpu

F32 = jnp.float32
MM = jnp.bfloat16

D_MODEL = 1024
N_META = 16
CHUNK = 128
PAD = CHUNK - N_META
S5_W, S5_G, S5_H, S5_P = 256, 16, 16, 64
S5_N = S5_G * S5_P
RET_W, RET_H, HEAD = 768, 6, 128
D_FF = 4096
PROJ_W = S5_W + 4 * RET_W
N_DEV = 8
FF_BLK = D_FF // N_DEV
ROW_BLK = 384
ALPHA = 2.0 ** 0.25
LN_EPS = 1e-5
GN_EPS = 1e-5
ROPE_BASE = 10000.0
GELU_C = math.sqrt(2.0 / math.pi)
GELU_A = 0.044715
ADAM_LR, ADAM_B1, ADAM_B2, ADAM_EPS, ADAM_WD, ADAM_STEP = 0.001, 0.9, 0.999, 1e-08, 0.01, 10
VMEM_LIMIT = 60 * 1024 * 1024

_VMEM = pl.BlockSpec(memory_space=pltpu.VMEM)
_ANY = pl.BlockSpec(memory_space=pl.ANY)
_MESH = pl.DeviceIdType.MESH


def _params(sem=None):
    return pltpu.CompilerParams(dimension_semantics=sem, vmem_limit_bytes=VMEM_LIMIT)


def _dot(a, b):
    return jnp.dot(a.astype(MM), b.astype(MM), preferred_element_type=F32)


def _dot_nt(a, b):
    return lax.dot_general(a.astype(MM), b.astype(MM), (((1,), (1,)), ((), ())), preferred_element_type=F32)


def _dot_tn(a, b):
    return lax.dot_general(a.astype(MM), b.astype(MM), (((0,), (0,)), ((), ())), preferred_element_type=F32)


def _split3(a):
    hi = a.astype(jnp.bfloat16)
    r1 = a - hi.astype(F32)
    mid = r1.astype(jnp.bfloat16)
    lo = (r1 - mid.astype(F32)).astype(jnp.bfloat16)
    return hi, mid, lo


def _dot_sel_rhs(a, sel):
    s = sel.astype(jnp.bfloat16)
    return sum(jnp.dot(p, s, preferred_element_type=F32) for p in _split3(a))


def _dot_sel_lhs(sel, b):
    s = sel.astype(jnp.bfloat16)
    return sum(jnp.dot(s, p, preferred_element_type=F32) for p in _split3(b))


def _ln_fwd(r, eps):
    mu = jnp.mean(r, axis=-1, keepdims=True)
    xc = r - mu
    var = jnp.mean(xc * xc, axis=-1, keepdims=True)
    rstd = lax.rsqrt(var + eps)
    return xc * rstd, rstd


def _ln_bwd(dxhat, xhat, rstd):
    m1 = jnp.mean(dxhat, axis=-1, keepdims=True)
    m2 = jnp.mean(dxhat * xhat, axis=-1, keepdims=True)
    return rstd * (dxhat - m1 - xhat * m2)


def _colsum(a):
    return jnp.sum(a, axis=0, keepdims=True)


def _pair_rows(r, nr, stores, accumulate):
    half = r % 2
    rows = pl.ds(pl.multiple_of(half * ROW_BLK, CHUNK), ROW_BLK)
    for ref, val in stores:
        ref[rows, :] = val

    @pl.when(half == 1)
    def _():
        accumulate(lambda ref: ref[...])

    if nr % 2:
        @pl.when(r == nr - 1)
        def _():
            accumulate(lambda ref: ref[0:ROW_BLK, :])


def _shift3(n_in):
    return [pl.BlockSpec((CHUNK, D_MODEL), (lambda i, j=j: (jnp.clip(3 * i - 1 + j, 0, n_in - 1), 0))) for j in range(3)]


def _in_proj(x2d, meta_full, ln_g, ln_b, w_int, cos2, sin2, jobs=()):
    seq = x2d.shape[0]
    tp = seq + CHUNK
    R = ROW_BLK

    def body(xa, xb, xc, meta_ref, g_ref, b_ref, w_ref, cos_ref, sin_ref,
             xhat_ref, rstd_ref, u_ref, q_ref, k_ref, v_ref, gate_ref, raw_ref):
        i = pl.program_id(0)
        raw_ref[0:CHUNK, :] = xa[...]
        raw_ref[CHUNK:2 * CHUNK, :] = xb[...]
        raw_ref[2 * CHUNK:3 * CHUNK, :] = xc[...]

        @pl.when(i == 0)
        def _():
            raw_ref[0:PAD, :] = jnp.zeros((PAD, D_MODEL), F32)
            raw_ref[PAD:CHUNK, :] = meta_ref[...]

        xhat, rstd = _ln_fwd(raw_ref[...], LN_EPS)
        xhat_ref[...] = xhat
        rstd_ref[...] = rstd
        hb = (xhat * g_ref[...] + b_ref[...]).astype(MM)
        valid = (i * R + lax.broadcasted_iota(jnp.int32, (R, 1), 0)) >= PAD

        def seg(lo, hi):
            return jnp.where(valid, _dot_nt(hb, w_ref[lo:hi, :]), 0.0)

        u_ref[...] = seg(0, S5_W)
        cos = cos_ref[...]
        sin = sin_ref[...]
        q = seg(S5_W, S5_W + RET_W)
        k = seg(S5_W + RET_W, S5_W + 2 * RET_W)
        for h in range(RET_H):
            sl = slice(h * HEAD, (h + 1) * HEAD)
            qh = q[:, sl]
            kh = k[:, sl]
            q_ref[:, sl] = (qh * cos + pltpu.roll(qh, HEAD // 2, 1) * sin).astype(q_ref.dtype)
            k_ref[:, sl] = ((kh * cos + pltpu.roll(kh, HEAD // 2, 1) * sin) * (HEAD ** -0.5)).astype(k_ref.dtype)
        v_ref[...] = seg(S5_W + 2 * RET_W, S5_W + 3 * RET_W).astype(v_ref.dtype)
        gate_ref[...] = seg(S5_W + 3 * RET_W, PROJ_W)

    def rows(w, dt):
        return pl.BlockSpec((R, w), lambda i: (i, 0)), jax.ShapeDtypeStruct((tp, w), dt)

    outs = [rows(D_MODEL, F32), rows(1, F32), rows(S5_W, F32), rows(RET_W, MM), rows(RET_W, MM),
            rows(RET_W, MM), rows(RET_W, F32)]
    full = lambda s: pl.BlockSpec(s, lambda i: (0,) * len(s))
    return _call(
        body, "in_proj", (tp // R,),
        _shift3(seq // CHUNK) + [full((N_META, D_MODEL)), full((1, D_MODEL)), full((1, D_MODEL)), _VMEM,
                                 pl.BlockSpec((R, HEAD), lambda i: (i, 0)), pl.BlockSpec((R, HEAD), lambda i: (i, 0))],
        [o[0] for o in outs], [o[1] for o in outs], [pltpu.VMEM((R, D_MODEL), F32)],
        (x2d, x2d, x2d, meta_full, ln_g, ln_b, w_int, cos2, sin2), jobs)


def _s5_disc(lre, lim, ldt, bre_t, bim_t):
    dt = jnp.exp(ldt)
    mag = jnp.exp(lre * dt)
    ang = lim * dt
    lbr = mag * jnp.cos(ang)
    lbi = mag * jnp.sin(ang)
    den = lre * lre + lim * lim
    nr = lbr - 1.0
    qr = (nr * lre + lbi * lim) / den
    qi = (lbi * lre - nr * lim) / den
    return lbr, lbi, qr * bre_t - qi * bim_t, qr * bim_t + qi * bre_t


def _s5_tables(lbr, lbi, reverse):
    if reverse:
        lbi = -lbi
    pw = [(lbr, lbi)]
    for _ in range(7):
        r, i = pw[-1]
        pw.append((r * lbr - i * lbi, r * lbi + i * lbr))
    row = lax.broadcasted_iota(jnp.int32, (8, S5_N), 0)
    tabs = []
    for k in range(3):
        sh = 2 ** k
        mask = (row < 8 - sh) if reverse else (row >= sh)
        ar, ai = pw[sh - 1]
        tabs.append((jnp.where(mask, ar, 0.0), jnp.where(mask, ai, 0.0)))
    pr = jnp.zeros((8, S5_N), F32)
    pi = jnp.zeros((8, S5_N), F32)
    for i in range(8):
        ar, ai = pw[7 - i] if reverse else pw[i]
        pr = jnp.where(row == i, ar, pr)
        pi = jnp.where(row == i, ai, pi)
    tabs.append((pr, pi))
    return tabs


def _store_tables(tab_ref, tabs):
    for k, (r, i) in enumerate(tabs):
        tab_ref[2 * k] = r
        tab_ref[2 * k + 1] = i


def _bd_mask():
    r = lax.broadcasted_iota(jnp.int32, (S5_W, S5_N), 0)
    c = lax.broadcasted_iota(jnp.int32, (S5_W, S5_N), 1)
    return jnp.right_shift(r, 4) == jnp.right_shift(c, 6)


def _s5_block_diag(bbr_t, bbi_t, cre_w, cim_w):
    mask = _bd_mask()
    bd = lambda t: jnp.where(mask, t, 0.0)
    return (bd(jnp.tile(bbr_t, (S5_G, 1))), bd(jnp.tile(bbi_t, (S5_G, 1))),
            bd(jnp.tile(cre_w, (1, S5_N // HEAD))), bd(jnp.tile(cim_w, (1, S5_N // HEAD))))


def _scan8(xr, xi, tab_ref, lanes, reverse):
    for k in range(3):
        sh = (8 - 2 ** k) if reverse else 2 ** k
        sr = pltpu.roll(xr, sh, 0)
        si = pltpu.roll(xi, sh, 0)
        mr = tab_ref[2 * k, :, lanes]
        mi = tab_ref[2 * k + 1, :, lanes]
        xr, xi = xr + (mr * sr - mi * si), xi + (mr * si + mi * sr)
    return xr, xi


S5_LANES = 256


def _gelu(y):
    t = jnp.tanh(GELU_C * (y + GELU_A * y * y * y))
    return 0.5 * y * (1.0 + t), t


def _s5_fwd(u, lre, lim, ldt, bre_t, bim_t, cre_w, cim_w, d_row, w_glu, b_glu, jobs=()):
    tp = u.shape[0]
    R = ROW_BLK

    def body(u_ref, lre_ref, lim_ref, ldt_ref, bre_ref, bim_ref, cre_ref, cim_ref, d_ref, wg_ref, bg_ref,
             y_ref, xr_ref, xi_ref, bbd_r, bbd_i, cbd_r, cbd_i, tab_ref, car_r, car_i):
        @pl.when(pl.program_id(0) == 0)
        def _():
            lbr, lbi, bbr, bbi = _s5_disc(lre_ref[...], lim_ref[...], ldt_ref[...], bre_ref[...], bim_ref[...])
            br, bi, cr, ci = _s5_block_diag(bbr, bbi, cre_ref[...], cim_ref[...])
            bbd_r[...] = br.astype(MM)
            bbd_i[...] = bi.astype(MM)
            cbd_r[...] = cr.astype(MM)
            cbd_i[...] = ci.astype(MM)
            _store_tables(tab_ref, _s5_tables(lbr, lbi, False))
            car_r[...] = jnp.zeros_like(car_r)
            car_i[...] = jnp.zeros_like(car_i)

        u = u_ref[...]
        ub = u.astype(MM)
        xr_ref[...] = jnp.dot(ub, bbd_r[...], preferred_element_type=F32)
        xi_ref[...] = jnp.dot(ub, bbd_i[...], preferred_element_type=F32)
        for j in range(S5_N // S5_LANES):
            lanes = pl.ds(j * S5_LANES, S5_LANES)
            pr = tab_ref[6, :, lanes]
            pi = tab_ref[7, :, lanes]

            def step(g, carry):
                cr, ci = carry
                rows = pl.ds(pl.multiple_of(g * 8, 8), 8)
                xr, xi = _scan8(xr_ref[rows, lanes], xi_ref[rows, lanes], tab_ref, lanes, False)
                br = jnp.broadcast_to(cr[7:8, :], cr.shape)
                bi = jnp.broadcast_to(ci[7:8, :], ci.shape)
                xr = xr + (pr * br - pi * bi)
                xi = xi + (pr * bi + pi * br)
                xr_ref[rows, lanes] = xr
                xi_ref[rows, lanes] = xi
                return xr, xi

            cr, ci = lax.fori_loop(0, R // 8, step, (car_r[:, lanes], car_i[:, lanes]), unroll=2)
            car_r[:, lanes] = cr
            car_i[:, lanes] = ci
        y = _dot_nt(xr_ref[...], cbd_r[...]) - _dot_nt(xi_ref[...], cbd_i[...]) + d_ref[...] * u
        yg, _ = _gelu(y)
        z = _dot(yg, wg_ref[...]) + bg_ref[...]
        y_ref[...] = yg * jax.nn.sigmoid(z)

    full = lambda a: pl.BlockSpec(a.shape, lambda i: (0,) * a.ndim)
    small = [lre, lim, ldt, bre_t, bim_t, cre_w, cim_w, d_row, w_glu, b_glu]
    return _call(
        body, "s5_fwd", (tp // R,),
        [pl.BlockSpec((R, S5_W), lambda i: (i, 0))] + [full(a) for a in small],
        [pl.BlockSpec((R, S5_W), lambda i: (i, 0)), pl.BlockSpec((R, S5_N), lambda i: (i, 0)),
         pl.BlockSpec((R, S5_N), lambda i: (i, 0))],
        [jax.ShapeDtypeStruct((tp, S5_W), F32), jax.ShapeDtypeStruct((tp, S5_N), F32),
         jax.ShapeDtypeStruct((tp, S5_N), F32)],
        [pltpu.VMEM((S5_W, S5_N), MM)] * 4 + [pltpu.VMEM((8, 8, S5_N), F32), pltpu.VMEM((8, S5_N), F32),
                                              pltpu.VMEM((8, S5_N), F32)],
        (u, *small), jobs)


def _s5_bwd(dy_out, u, xr, xi, lre, lim, ldt, bre_t, bim_t, cre_w, cim_w, d_row, w_glu, b_glu, jobs=()):
    tp = u.shape[0]
    R = ROW_BLK
    nb = tp // R

    def body(dyo_ref, u_ref, xr_ref, xi_ref, xpr_ref, xpi_ref,
             lre_ref, lim_ref, ldt_ref, bre_ref, bim_ref, cre_ref, cim_ref, d_ref, wg_ref, bg_ref,
             du_ref, dlre_ref, dlim_ref, dldt_ref, dbre_ref, dbim_ref, dcre_ref, dcim_ref, dd_ref, dwg_ref, dbg_ref,
             bbd_r, bbd_i, cbd_r, cbd_i, tab_ref, car_r, car_i, gr_ref, gi_ref, xer_ref, xei_ref,
             abr, abi, acr, aci, adr, adi):
        i = pl.program_id(0)

        @pl.when(i == 0)
        def _():
            lbr, lbi, bbr, bbi = _s5_disc(lre_ref[...], lim_ref[...], ldt_ref[...], bre_ref[...], bim_ref[...])
            br, bi, cr, ci = _s5_block_diag(bbr, bbi, cre_ref[...], cim_ref[...])
            bbd_r[...] = br.astype(MM)
            bbd_i[...] = bi.astype(MM)
            cbd_r[...] = cr.astype(MM)
            cbd_i[...] = ci.astype(MM)
            _store_tables(tab_ref, _s5_tables(lbr, lbi, True))
            for ref in (car_r, car_i, abr, abi, acr, aci, adr, adi, dd_ref, dwg_ref, dbg_ref):
                ref[...] = jnp.zeros_like(ref)

        u = u_ref[...]
        xrv = xr_ref[...]
        xiv = xi_ref[...]
        y = _dot_nt(xrv, cbd_r[...]) - _dot_nt(xiv, cbd_i[...]) + d_ref[...] * u
        yg, t = _gelu(y)
        z = _dot(yg, wg_ref[...]) + bg_ref[...]
        s = jax.nn.sigmoid(z)
        dout = dyo_ref[...]
        dz = dout * yg * s * (1.0 - s)
        dyg = dout * s + _dot_nt(dz, wg_ref[...])
        dwg_ref[...] += _dot_tn(yg, dz)
        dbg_ref[...] += _colsum(dz)
        dy = dyg * (0.5 * (1.0 + t) + 0.5 * y * (1.0 - t * t) * GELU_C * (1.0 + 3.0 * GELU_A * y * y))
        dd_ref[...] += _colsum(dy * u)
        acr[...] += _dot_tn(dy, xrv)
        aci[...] -= _dot_tn(dy, xiv)
        gr_ref[...] = _dot(dy, cbd_r[...])
        gi_ref[...] = -_dot(dy, cbd_i[...])
        has_prev = (i < nb - 1).astype(F32)
        xer_ref[0:8, :] = xpr_ref[...] * has_prev
        xei_ref[0:8, :] = xpi_ref[...] * has_prev
        xer_ref[8:R + 8, :] = xrv
        xei_ref[8:R + 8, :] = xiv
        row = lax.broadcasted_iota(jnp.int32, (8, S5_LANES), 0)
        for j in range(S5_N // S5_LANES):
            lanes = pl.ds(j * S5_LANES, S5_LANES)
            pr = tab_ref[6, :, lanes]
            pi = tab_ref[7, :, lanes]

            def step(n, carry):
                cr, ci, sar, sai = carry
                g = R // 8 - 1 - n
                r0 = pl.multiple_of(g * 8, 8)
                rows = pl.ds(r0, 8)
                gr, gi = _scan8(gr_ref[rows, lanes], gi_ref[rows, lanes], tab_ref, lanes, True)
                br = jnp.broadcast_to(cr[0:1, :], cr.shape)
                bi = jnp.broadcast_to(ci[0:1, :], ci.shape)
                gr = gr + (pr * br - pi * bi)
                gi = gi + (pr * bi + pi * br)
                gr_ref[rows, lanes] = gr
                gi_ref[rows, lanes] = gi
                last = row == 7
                xpr = pltpu.roll(jnp.where(last, xer_ref[rows, lanes], xer_ref[pl.ds(r0 + 8, 8), lanes]), 1, 0)
                xpi = pltpu.roll(jnp.where(last, xei_ref[rows, lanes], xei_ref[pl.ds(r0 + 8, 8), lanes]), 1, 0)
                return gr, gi, sar + (gr * xpr + gi * xpi), sai + (gi * xpr - gr * xpi)

            cr, ci, sar, sai = lax.fori_loop(
                0, R // 8, step, (car_r[:, lanes], car_i[:, lanes], adr[:, lanes], adi[:, lanes]), unroll=2)
            car_r[:, lanes] = cr
            car_i[:, lanes] = ci
            adr[:, lanes] = sar
            adi[:, lanes] = sai
        grv = gr_ref[...]
        giv = gi_ref[...]
        du_ref[...] = dy * d_ref[...] + _dot_nt(grv, bbd_r[...]) + _dot_nt(giv, bbd_i[...])
        abr[...] += _dot_tn(u, grv)
        abi[...] += _dot_tn(u, giv)

        @pl.when(i == nb - 1)
        def _():
            mask = _bd_mask()
            r16 = lax.broadcasted_iota(jnp.int32, (S5_H, S5_W), 1)
            h16 = lax.broadcasted_iota(jnp.int32, (S5_H, S5_W), 0)
            fold_b = jnp.bitwise_and(r16, S5_H - 1) == h16
            c64 = lax.broadcasted_iota(jnp.int32, (S5_N, S5_P), 0)
            p64 = lax.broadcasted_iota(jnp.int32, (S5_N, S5_P), 1)
            fold_c = jnp.bitwise_and(c64, S5_P - 1) == p64
            dbbr = _dot_sel_lhs(fold_b, jnp.where(mask, abr[...], 0.0))
            dbbi = _dot_sel_lhs(fold_b, jnp.where(mask, abi[...], 0.0))
            dcre_ref[...] = _dot_sel_rhs(jnp.where(mask, acr[...], 0.0), fold_c)
            dcim_ref[...] = _dot_sel_rhs(jnp.where(mask, aci[...], 0.0), fold_c)
            dlbr = _colsum(adr[...])
            dlbi = _colsum(adi[...])
            _, vjp = jax.vjp(_s5_disc, lre_ref[...], lim_ref[...], ldt_ref[...], bre_ref[...], bim_ref[...])
            dlre, dlim, dldt, dbre, dbim = vjp((dlbr, dlbi, dbbr, dbbi))
            dlre_ref[...] = dlre
            dlim_ref[...] = dlim
            dbre_ref[...] = dbre
            dbim_ref[...] = dbim
            gsel = jnp.right_shift(lax.broadcasted_iota(jnp.int32, (S5_N, HEAD), 0), 6) == \
                lax.broadcasted_iota(jnp.int32, (S5_N, HEAD), 1)
            dldt_ref[...] = _dot_sel_rhs(dldt, gsel)

    full = lambda a: pl.BlockSpec(a.shape, lambda i: (0,) * a.ndim)
    rev = lambda w: pl.BlockSpec((R, w), lambda i: (nb - 1 - i, 0))
    prev8 = pl.BlockSpec((8, S5_N), lambda i: (jnp.maximum((nb - 1 - i) * (R // 8) - 1, 0), 0))
    small = [lre, lim, ldt, bre_t, bim_t, cre_w, cim_w, d_row, w_glu, b_glu]
    outs = [((tp, S5_W), rev(S5_W))] + [
        (s, pl.BlockSpec(s, lambda i: (0, 0))) for s in
        [(1, S5_N), (1, S5_N), (1, HEAD), (S5_H, S5_N), (S5_H, S5_N), (S5_W, S5_P), (S5_W, S5_P),
         (1, S5_W), (S5_W, S5_W), (1, S5_W)]]
    return _call(
        body, "s5_bwd", (nb,),
        [rev(S5_W), rev(S5_W), rev(S5_N), rev(S5_N), prev8, prev8] + [full(a) for a in small],
        [o[1] for o in outs], [jax.ShapeDtypeStruct(o[0], F32) for o in outs],
        [pltpu.VMEM((S5_W, S5_N), MM)] * 4 + [
            pltpu.VMEM((8, 8, S5_N), F32), pltpu.VMEM((8, S5_N), F32), pltpu.VMEM((8, S5_N), F32),
            pltpu.VMEM((R, S5_N), F32), pltpu.VMEM((R, S5_N), F32),
            pltpu.VMEM((R + 8, S5_N), F32), pltpu.VMEM((R + 8, S5_N), F32)] + [pltpu.VMEM((S5_W, S5_N), F32)] * 4 + [
            pltpu.VMEM((8, S5_N), F32), pltpu.VMEM((8, S5_N), F32)],
        (dy_out, u, xr, xi, xr, xi, *small), jobs)


def _ret_fwd(q, k, v, dmat, zeta_b, xi_b, gam_b, jobs=()):
    tp = q.shape[0]
    nc = tp // CHUNK

    def body(q_ref, k_ref, v_ref, dm_ref, ze_ref, xi_ref, ga_ref, o_ref, st_ref, s_ref):
        @pl.when(pl.program_id(0) == 0)
        def _():
            s_ref[...] = jnp.zeros_like(s_ref)

        for h in range(RET_H):
            sl = slice(h * HEAD, (h + 1) * HEAD)
            qh, kh, vh = q_ref[:, sl], k_ref[:, sl], v_ref[:, sl]
            sh = s_ref[h]
            st_ref[0, sl, :] = sh
            scores = _dot_nt(qh, kh) * dm_ref[h]
            o_ref[:, sl] = _dot(scores, vh) + _dot(qh, sh) * xi_ref[h]
            s_ref[h] = ga_ref[h] * sh + _dot_tn(kh.astype(F32) * ze_ref[h], vh)

    blk = pl.BlockSpec((CHUNK, RET_W), lambda c: (c, 0))
    cst = pl.BlockSpec((RET_H, HEAD, HEAD), lambda c: (0, 0, 0))
    return _call(
        body, "ret_fwd", (nc,), [blk, blk, blk, cst, cst, cst, cst],
        [blk, pl.BlockSpec((1, RET_W, HEAD), lambda c: (c, 0, 0))],
        [jax.ShapeDtypeStruct((tp, RET_W), F32), jax.ShapeDtypeStruct((nc, RET_W, HEAD), F32)],
        [pltpu.VMEM((RET_H, HEAD, HEAD), F32)], (q, k, v, dmat, zeta_b, xi_b, gam_b), jobs)


def _ret_bwd(q, k, v, do, states, cos2, sin2, dmat, zeta_b, xi_b, gam_b, jobs=()):
    tp = q.shape[0]
    nc = tp // CHUNK

    def body(q_ref, k_ref, v_ref, do_ref, st_ref, cos_ref, sin_ref, dm_ref, ze_ref, xi_ref, ga_ref,
             dq_ref, dk_ref, dv_ref, ds_ref):
        @pl.when(pl.program_id(0) == 0)
        def _():
            ds_ref[...] = jnp.zeros_like(ds_ref)

        cos = cos_ref[...]
        sin = sin_ref[...]
        for h in range(RET_H):
            sl = slice(h * HEAD, (h + 1) * HEAD)
            qh, kh, vh = q_ref[:, sl], k_ref[:, sl], v_ref[:, sl]
            dmh = dm_ref[h]
            sh = st_ref[0, sl, :]
            dsn = ds_ref[h]
            doh = do_ref[:, sl]
            dox = doh * xi_ref[h]
            a = _dot_nt(qh, kh) * dmh
            dqk = _dot_nt(doh, vh) * dmh
            kz = kh.astype(F32) * ze_ref[h]
            dv_ref[:, sl] = _dot_tn(a, doh) + _dot(kz, dsn)
            dqr = _dot(dqk, kh) + _dot_nt(dox, sh)
            dkr = _dot_tn(dqk, qh) + ze_ref[h] * _dot_nt(vh, dsn)
            ds_ref[h] = ga_ref[h] * dsn + _dot_tn(qh, dox)
            dq_ref[:, sl] = dqr * cos - pltpu.roll(dqr, HEAD // 2, 1) * sin
            dk_ref[:, sl] = (dkr * cos - pltpu.roll(dkr, HEAD // 2, 1) * sin) * (HEAD ** -0.5)

    blk = pl.BlockSpec((CHUNK, RET_W), lambda c: (nc - 1 - c, 0))
    tab = pl.BlockSpec((CHUNK, HEAD), lambda c: (nc - 1 - c, 0))
    cst = pl.BlockSpec((RET_H, HEAD, HEAD), lambda c: (0, 0, 0))
    return _call(
        body, "ret_bwd", (nc,),
        [blk, blk, blk, blk, pl.BlockSpec((1, RET_W, HEAD), lambda c: (nc - 1 - c, 0, 0)), tab, tab, cst, cst, cst, cst],
        [blk, blk, blk], [jax.ShapeDtypeStruct((tp, RET_W), F32)] * 3, [pltpu.VMEM((RET_H, HEAD, HEAD), F32)],
        (q, k, v, do, states, cos2, sin2, dmat, zeta_b, xi_b, gam_b), jobs)


def _gn_gate(o, gate, gn_g, gn_b):
    xhat, rstd = _ln_fwd(o, GN_EPS)
    on = xhat * gn_g + gn_b
    s = jax.nn.sigmoid(gate)
    return gate * s * on, xhat, rstd, on, s


def _post_fwd(o, gate, ys5, xhat0, tgt, gn_g, gn_b, li_g, li_b, l1_g, l1_b, l2_g, l2_b, w_out, w_up, w_down):
    tp = o.shape[0]
    seq = tgt.shape[0]
    R = ROW_BLK

    def body(o_ref, g_ref, ys_ref, xh0_ref, ta, tb, tc, gng, gnb, lig, lib, l1g, l1b, l2g, l2b, wo_ref, wu_ref, wd_ref,
             ycat_ref, xh1_ref, rstd1_ref, h1b_ref, dr2_ref, dffb_ref, loss_ref, dl2g_ref, dl2b_ref, pre_ref, tgt_ref):
        i = pl.program_id(0)

        @pl.when(i == 0)
        def _():
            for ref in (loss_ref, dl2g_ref, dl2b_ref):
                ref[...] = jnp.zeros_like(ref)

        tgt_ref[0:CHUNK, :] = ta[...]
        tgt_ref[CHUNK:2 * CHUNK, :] = tb[...]
        tgt_ref[2 * CHUNK:3 * CHUNK, :] = tc[...]
        ycat_ref[:, 0:S5_W] = ys_ref[...].astype(ycat_ref.dtype)
        for h in range(RET_H):
            sl = slice(h * HEAD, (h + 1) * HEAD)
            yret = _gn_gate(o_ref[:, sl], g_ref[:, sl], gng[:, sl], gnb[:, sl])[0]
            ycat_ref[:, S5_W + h * HEAD:S5_W + (h + 1) * HEAD] = yret.astype(ycat_ref.dtype)
        mixed = _dot(ycat_ref[...], wo_ref[...])
        h0 = xh0_ref[...] * lig[...] + lib[...]
        xh1, rstd1 = _ln_fwd(ALPHA * h0 + mixed, LN_EPS)
        xh1_ref[...] = xh1
        rstd1_ref[...] = rstd1
        h1 = xh1 * l1g[...] + l1b[...]
        h1b = h1.astype(MM)
        h1b_ref[...] = h1b
        ff = jnp.zeros((R, D_MODEL), F32)
        for d in range(N_DEV):
            pre = jnp.maximum(_dot(h1b, wu_ref[d]), 0.0)
            pre_ref[:, d * FF_BLK:(d + 1) * FF_BLK] = pre
            ff = ff + _dot(pre * pre, wd_ref[d * FF_BLK:(d + 1) * FF_BLK, :])
        xh2, rstd2 = _ln_fwd(ALPHA * h1 + ff, LN_EPS)
        h2 = xh2 * l2g[...] + l2b[...]
        valid = (i * R + lax.broadcasted_iota(jnp.int32, (R, 1), 0)) >= CHUNK
        err = jnp.where(valid, h2 - tgt_ref[...], 0.0)
        loss_ref[...] += 0.5 * jnp.sum(err * err) / D_MODEL
        dh2 = err * (1.0 / D_MODEL)
        dl2g_ref[...] += _colsum(dh2 * xh2)
        dl2b_ref[...] += _colsum(dh2)
        dr2 = _ln_bwd(dh2 * l2g[...], xh2, rstd2)
        dr2_ref[...] = dr2
        dffb_ref[...] = dr2.astype(MM)

    row = lambda w: pl.BlockSpec((R, w), lambda i: (i, 0))
    full = lambda a: pl.BlockSpec(a.shape, lambda i: (0,) * a.ndim)
    vecs = [gn_g, gn_b, li_g, li_b, l1_g, l1_b, l2_g, l2_b]
    acc = lambda s: (pl.BlockSpec(s, lambda i: (0, 0)), jax.ShapeDtypeStruct(s, F32))
    outs = [(row(D_MODEL), jax.ShapeDtypeStruct((tp, D_MODEL), MM)),
            (row(D_MODEL), jax.ShapeDtypeStruct((tp, D_MODEL), F32)),
            (row(1), jax.ShapeDtypeStruct((tp, 1), F32)),
            (row(D_MODEL), jax.ShapeDtypeStruct((tp, D_MODEL), MM)),
            (row(D_MODEL), jax.ShapeDtypeStruct((tp, D_MODEL), F32)),
            (row(D_MODEL), jax.ShapeDtypeStruct((tp, D_MODEL), MM)),
            acc((8, HEAD)), acc((1, D_MODEL)), acc((1, D_MODEL)),
            (row(D_FF), jax.ShapeDtypeStruct((tp, D_FF), F32))]
    return pl.pallas_call(
        body, name="post_fwd", grid=(tp // R,),
        in_specs=[row(RET_W), row(RET_W), row(S5_W), row(D_MODEL)] + _shift3(seq // CHUNK) + [full(a) for a in vecs]
        + [_VMEM, _VMEM, _VMEM],
        out_specs=[o[0] for o in outs], out_shape=[o[1] for o in outs],
        scratch_shapes=[pltpu.VMEM((R, D_MODEL), F32)],
        compiler_params=_params(("arbitrary",)),
    )(o, gate, ys5, xhat0, tgt, tgt, tgt, *vecs, w_out, w_up, w_down)


def _mlp_bwd(h1b, dffb, pre, w_up, w_down):
    tp = h1b.shape[0]
    R = ROW_BLK
    nr = tp // R

    def body(h_ref, df_ref, pre_ref, wu_ref, wd_ref, gup_ref, gdn_ref, dh1_ref, aup, adn, h2, df2, dp2, ac2):
        d = pl.program_id(0)
        r = pl.program_id(1)

        @pl.when(r == 0)
        def _():
            aup[...] = jnp.zeros_like(aup)
            adn[...] = jnp.zeros_like(adn)

        h = h_ref[...]
        df = df_ref[...]
        wu = wu_ref[0]
        wd = wd_ref[0]
        pre = pre_ref[...]
        dpre = (_dot_nt(df, wd) * (2.0 * pre)).astype(MM)

        def accumulate(get):
            aup[...] += _dot_tn(get(h2), get(dp2))
            adn[...] += _dot_tn(get(ac2), get(df2))

        _pair_rows(r, nr, [(h2, h), (df2, df), (dp2, dpre), (ac2, (pre * pre).astype(MM))], accumulate)
        contrib = _dot_nt(dpre, wu)
        rows = pl.ds(pl.multiple_of(r * R, CHUNK), R)

        @pl.when(d == 0)
        def _():
            dh1_ref[rows, :] = contrib

        @pl.when(d > 0)
        def _():
            dh1_ref[rows, :] += contrib

        @pl.when(r == nr - 1)
        def _():
            gup_ref[0] = aup[...].astype(gup_ref.dtype)
            gdn_ref[0] = adn[...].astype(gdn_ref.dtype)

    return pl.pallas_call(
        body, name="mlp_bwd", grid=(N_DEV, nr),
        in_specs=[pl.BlockSpec((R, D_MODEL), lambda d, r: (r, 0)), pl.BlockSpec((R, D_MODEL), lambda d, r: (r, 0)),
                  pl.BlockSpec((R, FF_BLK), lambda d, r: (r, d)),
                  pl.BlockSpec((1, D_MODEL, FF_BLK), lambda d, r: (d, 0, 0)),
                  pl.BlockSpec((1, FF_BLK, D_MODEL), lambda d, r: (d, 0, 0))],
        out_specs=[pl.BlockSpec((1, D_MODEL, FF_BLK), lambda d, r: (d, 0, 0)),
                   pl.BlockSpec((1, FF_BLK, D_MODEL), lambda d, r: (d, 0, 0)), _VMEM],
        out_shape=[jax.ShapeDtypeStruct((N_DEV, D_MODEL, FF_BLK), MM), jax.ShapeDtypeStruct((N_DEV, FF_BLK, D_MODEL), MM),
                   jax.ShapeDtypeStruct((tp, D_MODEL), F32)],
        scratch_shapes=[pltpu.VMEM((D_MODEL, FF_BLK), F32), pltpu.VMEM((FF_BLK, D_MODEL), F32),
                        pltpu.VMEM((2 * R, D_MODEL), MM), pltpu.VMEM((2 * R, D_MODEL), MM),
                        pltpu.VMEM((2 * R, FF_BLK), MM), pltpu.VMEM((2 * R, FF_BLK), MM)],
        compiler_params=_params(("arbitrary", "arbitrary")),
    )(h1b, dffb, pre, w_up, w_down.reshape(N_DEV, FF_BLK, D_MODEL))


def _post_bwd(dh1m, dr2, xhat1, rstd1, ycat, o, gate, gn_g, gn_b, l1_g, w_out, jobs=()):
    tp = o.shape[0]
    R = ROW_BLK
    nb = tp // R

    def body(dm_ref, dr2_ref, xh1_ref, rs1_ref, yc_ref, o_ref, g_ref, gng, gnb, l1g, wo_ref,
             do_ref, dg_ref, dys_ref, dh0_ref, gwo_ref, dl1g_ref, dl1b_ref, dgng_ref, dgnb_ref, awo, yc2, dm2):
        i = pl.program_id(0)

        @pl.when(i == 0)
        def _():
            for ref in (awo, dl1g_ref, dl1b_ref, dgng_ref, dgnb_ref):
                ref[...] = jnp.zeros_like(ref)

        dh1 = dm_ref[...] + ALPHA * dr2_ref[...]
        xh1 = xh1_ref[...]
        dl1g_ref[...] += _colsum(dh1 * xh1)
        dl1b_ref[...] += _colsum(dh1)
        dr1 = _ln_bwd(dh1 * l1g[...], xh1, rs1_ref[...])
        dh0_ref[...] = ALPHA * dr1
        dmix = dr1.astype(MM)

        def accumulate(get):
            awo[...] += _dot_tn(get(yc2), get(dm2))

        _pair_rows(i, nb, [(yc2, yc_ref[...]), (dm2, dmix)], accumulate)
        dyc = _dot_nt(dmix, wo_ref[...])
        dys_ref[...] = dyc[:, 0:S5_W]
        for h in range(RET_H):
            sl = slice(h * HEAD, (h + 1) * HEAD)
            gt = g_ref[:, sl]
            _, xhat, rstd, on, s = _gn_gate(o_ref[:, sl], gt, gng[:, sl], gnb[:, sl])
            dyr = dyc[:, S5_W + h * HEAD:S5_W + (h + 1) * HEAD]
            dg_ref[:, sl] = dyr * on * (s * (1.0 + gt * (1.0 - s)))
            don = dyr * gt * s
            dgng_ref[:, sl] += _colsum(don * xhat)
            dgnb_ref[:, sl] += _colsum(don)
            do_ref[:, sl] = _ln_bwd(don * gng[:, sl], xhat, rstd)

        @pl.when(i == nb - 1)
        def _():
            gwo_ref[...] = awo[...].astype(gwo_ref.dtype)

    row = lambda w: pl.BlockSpec((R, w), lambda i: (i, 0))
    full = lambda a: pl.BlockSpec(a.shape, lambda i: (0,) * a.ndim)
    acc = lambda s, dt=F32: (pl.BlockSpec(s, lambda i: (0, 0)), jax.ShapeDtypeStruct(s, dt))
    outs = [(row(RET_W), jax.ShapeDtypeStruct((tp, RET_W), F32)), (row(RET_W), jax.ShapeDtypeStruct((tp, RET_W), F32)),
            (row(S5_W), jax.ShapeDtypeStruct((tp, S5_W), F32)), (row(D_MODEL), jax.ShapeDtypeStruct((tp, D_MODEL), F32)),
            acc((D_MODEL, D_MODEL), MM), acc((1, D_MODEL)), acc((1, D_MODEL)), acc((1, RET_W)), acc((1, RET_W))]
    return _call(
        body, "post_bwd", (nb,),
        [row(D_MODEL), row(D_MODEL), row(D_MODEL), row(1), row(D_MODEL), row(RET_W), row(RET_W),
         full(gn_g), full(gn_b), full(l1_g), _VMEM],
        [o[0] for o in outs], [o[1] for o in outs],
        [pltpu.VMEM((D_MODEL, D_MODEL), F32), pltpu.VMEM((2 * R, D_MODEL), MM), pltpu.VMEM((2 * R, D_MODEL), MM)],
        (dh1m, dr2, xhat1, rstd1, ycat, o, gate, gn_g, gn_b, l1_g, w_out), jobs)


def _in_bwd(du, dq, dk, dv, dg, dh0r, xhat0, rstd0, li_g, li_b, w_int):
    tp = du.shape[0]
    R = ROW_BLK
    nb = tp // R
    segs = [(0, S5_W)] + [(S5_W + n * RET_W, S5_W + (n + 1) * RET_W) for n in range(4)]

    def body(du_ref, dq_ref, dk_ref, dv_ref, dg_ref, dh0r_ref, xh_ref, rs_ref, lig, lib, w_ref,
             draw_ref, gw_ref, dlg_ref, dlb_ref, aw, ds2, hb2):
        i = pl.program_id(0)

        @pl.when(i == 0)
        def _():
            for ref in (aw, dlg_ref, dlb_ref):
                ref[...] = jnp.zeros_like(ref)

        valid = (i * R + lax.broadcasted_iota(jnp.int32, (R, 1), 0)) >= PAD
        xh = xh_ref[...]
        hb = (xh * lig[...] + lib[...]).astype(MM)
        dh0 = dh0r_ref[...]
        stores = [(hb2, hb)]
        for (lo, hi), ref in zip(segs, (du_ref, dq_ref, dk_ref, dv_ref, dg_ref)):
            dseg = jnp.where(valid, ref[...], 0.0).astype(MM)
            dh0 = dh0 + _dot(dseg, w_ref[lo:hi, :])
            stores.append((ds2.at[:, lo:hi], dseg))

        def accumulate(get):
            for lo, hi in segs:
                aw[lo:hi, :] += _dot_tn(get(ds2.at[:, lo:hi]), get(hb2))

        _pair_rows(i, nb, stores, accumulate)
        dlg_ref[...] += _colsum(dh0 * xh)
        dlb_ref[...] += _colsum(dh0)
        draw_ref[...] = _ln_bwd(dh0 * lig[...], xh, rs_ref[...])

        @pl.when(i == nb - 1)
        def _():
            gw_ref[...] = aw[...].astype(gw_ref.dtype)

    row = lambda w: pl.BlockSpec((R, w), lambda i: (i, 0))
    full = lambda a: pl.BlockSpec(a.shape, lambda i: (0,) * a.ndim)
    acc = lambda s, dt=F32: (pl.BlockSpec(s, lambda i: (0, 0)), jax.ShapeDtypeStruct(s, dt))
    outs = [(row(D_MODEL), jax.ShapeDtypeStruct((tp, D_MODEL), F32)), acc((PROJ_W, D_MODEL), MM),
            acc((1, D_MODEL)), acc((1, D_MODEL))]
    return pl.pallas_call(
        body, name="in_bwd", grid=(nb,),
        in_specs=[row(S5_W), row(RET_W), row(RET_W), row(RET_W), row(RET_W), row(D_MODEL), row(D_MODEL), row(1),
                  full(li_g), full(li_b), _VMEM],
        out_specs=[o[0] for o in outs], out_shape=[o[1] for o in outs],
        scratch_shapes=[pltpu.VMEM((PROJ_W, D_MODEL), F32), pltpu.VMEM((2 * R, PROJ_W), MM),
                        pltpu.VMEM((2 * R, D_MODEL), MM)],
        compiler_params=_params(("arbitrary",)),
    )(du, dq, dk, dv, dg, dh0r, xhat0, rstd0, li_g, li_b, w_int)


def _place():
    return lax.axis_index("x"), lax.axis_index("y"), lax.axis_index("c")


def _dma_sems(n):
    return pltpu.SemaphoreType.DMA((n,))


def _job_gather(shard):
    def parts(ins, outs, sems):
        (src,), (out,), (send_sems, recv_sems, local_sem) = ins, outs, sems
        x, y, c = _place()
        me, sib = (x, y, c), (x, y, 1 - c)
        chips = [(1 - x, y), (x, 1 - y), (1 - x, 1 - y)]

        def slot(dev):
            return out.at[4 * dev[0] + 2 * dev[1] + dev[2]]

        def copy(k, block, to, from_input=False):
            return pltpu.make_async_remote_copy(
                src_ref=src if from_input else slot(block), dst_ref=slot(block),
                send_sem=send_sems.at[k], recv_sem=recv_sems.at[k], device_id=to, device_id_type=_MESH)

        mine = pltpu.make_async_copy(src, slot(me), local_sem.at[0])
        first = [copy(0, me, sib, True)] + [copy(1 + j, me, (*chip, c), True) for j, chip in enumerate(chips)]
        return me, sib, chips, copy, mine, first

    def start(ins, outs, sems):
        _, _, _, _, mine, first = parts(ins, outs, sems)
        mine.start()
        for cp in first:
            cp.start()

    def finish(ins, outs, sems):
        me, sib, chips, copy, mine, first = parts(ins, outs, sems)
        c = me[2]
        passed = []
        for j, chip in enumerate(chips):
            copy(1 + j, (*chip, c), me).wait_recv()
            cp = copy(4 + j, (*chip, c), sib)
            cp.start()
            passed.append(cp)
        copy(0, sib, me).wait_recv()
        for j, chip in enumerate(chips):
            copy(4 + j, (*chip, 1 - c), me).wait_recv()
        for cp in first + passed:
            cp.wait_send()
        mine.wait()

    return dict(ins=[shard], outs=[jax.ShapeDtypeStruct((N_DEV,) + shard.shape, shard.dtype)],
                sems=[_dma_sems(7), _dma_sems(7), _dma_sems(1)], start=start, finish=finish)


def _job_pair(g):
    def copies(ins, outs, sems):
        x, y, c = _place()
        return [pltpu.make_async_remote_copy(
            src_ref=ins[0].at[2 * j + (1 - c)], dst_ref=outs[0].at[j], send_sem=sems[0].at[j], recv_sem=sems[1].at[j],
            device_id=(x, y, 1 - c), device_id_type=_MESH) for j in range(4)]

    def start(ins, outs, sems):
        for cp in copies(ins, outs, sems):
            cp.start()

    def finish(ins, outs, sems):
        for cp in copies(ins, outs, sems):
            cp.wait()

    return dict(ins=[g], outs=[jax.ShapeDtypeStruct((4,) + g.shape[1:], g.dtype)], sems=[_dma_sems(4), _dma_sems(4)],
                start=start, finish=finish)


def _job_chips(p):
    def copies(ins, outs, sems):
        x, y, c = _place()
        chips = [(1 - x, y), (x, 1 - y), (1 - x, 1 - y)]
        return [pltpu.make_async_remote_copy(
            src_ref=ins[0].at[2 * chip[0] + chip[1]], dst_ref=outs[0].at[k], send_sem=sems[0].at[k],
            recv_sem=sems[1].at[k], device_id=(*chip, c), device_id_type=_MESH) for k, chip in enumerate(chips)]

    def start(ins, outs, sems):
        for cp in copies(ins, outs, sems):
            cp.start()

    def finish(ins, outs, sems):
        for cp in copies(ins, outs, sems):
            cp.wait()

    return dict(ins=[p], outs=[jax.ShapeDtypeStruct((3,) + p.shape[1:], p.dtype)], sems=[_dma_sems(3), _dma_sems(3)],
                start=start, finish=finish)


def _split_job_refs(jobs, ins, outs, sems):
    res, a, b, c = [], 0, 0, 0
    for job in jobs:
        na, nb, nc = len(job["ins"]), len(job["outs"]), len(job["sems"])
        res.append((ins[a:a + na], outs[b:b + nb], sems[c:c + nc]))
        a, b, c = a + na, b + nb, c + nc
    return res


def _call(body, name, grid, in_specs, out_specs, out_shape, scratch, args, jobs=()):
    jobs = list(jobs)
    n_in, n_out, n_scr = len(in_specs), len(out_specs), len(scratch)
    j_in = [a for job in jobs for a in job["ins"]]
    j_out = [o for job in jobs for o in job["outs"]]
    j_scr = [s for job in jobs for s in job["sems"]]
    nsteps = grid[0]

    def wrapped(*refs):
        ins, jins = refs[:n_in], refs[n_in:n_in + len(j_in)]
        refs = refs[n_in + len(j_in):]
        outs, jouts = refs[:n_out], refs[n_out:n_out + len(j_out)]
        refs = refs[n_out + len(j_out):]
        scr, jscr = refs[:n_scr], refs[n_scr:]
        per_job = _split_job_refs(jobs, jins, jouts, jscr)

        @pl.when(pl.program_id(0) == 0)
        def _():
            for job, r in zip(jobs, per_job):
                job["start"](*r)

        body(*ins, *outs, *scr)

        @pl.when(pl.program_id(0) == nsteps - 1)
        def _():
            for job, r in zip(jobs, per_job):
                job["finish"](*r)

    res = pl.pallas_call(
        wrapped if jobs else body, name=name, grid=grid,
        in_specs=list(in_specs) + [_ANY] * len(j_in), out_specs=list(out_specs) + [_ANY] * len(j_out),
        out_shape=list(out_shape) + j_out, scratch_shapes=list(scratch) + j_scr,
        compiler_params=_params(("arbitrary",) * len(grid)),
    )(*args, *j_in)
    return list(res[:n_out]), list(res[n_out:])


def _exchange(jobs, name):
    j_in = [a for job in jobs for a in job["ins"]]
    j_out = [o for job in jobs for o in job["outs"]]
    j_scr = [s for job in jobs for s in job["sems"]]

    def body(*refs):
        per_job = _split_job_refs(jobs, refs[:len(j_in)], refs[len(j_in):len(j_in) + len(j_out)],
                                  refs[len(j_in) + len(j_out):])
        for job, r in zip(jobs, per_job):
            job["start"](*r)
        for job, r in zip(jobs, per_job):
            job["finish"](*r)

    return pl.pallas_call(body, name=name, out_shape=j_out, in_specs=[_ANY] * len(j_in), out_specs=[_ANY] * len(j_out),
                          scratch_shapes=j_scr)(*j_in)


def _pair_sum(gs, r1s, c_arr, name):
    n = len(gs)

    def body(c_ref, *refs):
        for a in range(n):
            refs[2 * n + a][...] = (refs[a][...].astype(F32) + refs[n + a][...].astype(F32)).astype(refs[2 * n + a].dtype)

    def blk(g, own):
        s = g.shape[1:]
        if own:
            return pl.BlockSpec((1,) + s, lambda j, c_ref: (2 * j + c_ref[0],) + (0,) * len(s))
        return pl.BlockSpec((1,) + s, lambda j, c_ref: (j,) + (0,) * len(s))

    return pl.pallas_call(
        body, name=name,
        grid_spec=pltpu.PrefetchScalarGridSpec(
            num_scalar_prefetch=1, grid=(4,),
            in_specs=[blk(g, True) for g in gs] + [blk(g, False) for g in gs],
            out_specs=[blk(g, False) for g in gs]),
        out_shape=[jax.ShapeDtypeStruct((4,) + g.shape[1:], g.dtype) for g in gs],
        compiler_params=_params(("arbitrary",)),
    )(c_arr, *gs, *r1s)


def _chip_sum(ps, r2s, j_arr):
    n = len(ps)

    def body(j_ref, *refs):
        for a in range(n):
            r2 = refs[n + a]
            refs[2 * n + a][...] = ((refs[a][0].astype(F32) + r2[0].astype(F32)) + r2[1].astype(F32)) + r2[2].astype(F32)

    def own(p):
        s = p.shape[1:]
        return pl.BlockSpec((1,) + s, lambda i, j_ref: (j_ref[0],) + (0,) * len(s))

    def whole(p):
        return pl.BlockSpec(p.shape, lambda i, j_ref: (0,) * p.ndim)

    return pl.pallas_call(
        body, name="chip_sum",
        grid_spec=pltpu.PrefetchScalarGridSpec(
            num_scalar_prefetch=1, grid=(1,),
            in_specs=[own(p) for p in ps] + [whole(r) for r in r2s],
            out_specs=[pl.BlockSpec(p.shape[1:], lambda i, j_ref: (0,) * (p.ndim - 1)) for p in ps]),
        out_shape=[jax.ShapeDtypeStruct(p.shape[1:], F32) for p in ps],
        compiler_params=_params(("arbitrary",)),
    )(j_arr, *ps, *r2s)


def _adamw_math(w, g, m, v):
    m = ADAM_B1 * m + (1.0 - ADAM_B1) * g
    v = ADAM_B2 * v + (1.0 - ADAM_B2) * (g * g)
    m_hat = m / (1.0 - ADAM_B1 ** ADAM_STEP)
    v_hat = v / (1.0 - ADAM_B2 ** ADAM_STEP)
    return -ADAM_LR * (m_hat / (jnp.sqrt(v_hat) + ADAM_EPS) + ADAM_WD * w), m, v


def _adamw(items, name, steps):
    n = len(items)

    def body(*refs):
        for a in range(n):
            g, w, m, v = (refs[4 * a + t][...] for t in range(4))
            d, m2, v2 = _adamw_math(w, g, m, v)
            refs[4 * n + 3 * a][...] = d
            refs[4 * n + 3 * a + 1][...] = m2
            refs[4 * n + 3 * a + 2][...] = v2

    def blk(arr):
        r, c = arr.shape
        return pl.BlockSpec((r // steps, c), lambda i: (i, 0))

    flat = [t for it in items for t in it]
    return pl.pallas_call(
        body, name=name, grid=(steps,),
        in_specs=[blk(t) for t in flat],
        out_specs=[blk(it[1]) for it in items for _ in range(3)],
        out_shape=[jax.ShapeDtypeStruct(it[1].shape, F32) for it in items for _ in range(3)],
        compiler_params=_params(("arbitrary",)),
    )(*flat)


def _adamw_small(gathered, w, m, v):
    def body(gs_ref, w_ref, m_ref, v_ref, g_ref, d_ref, m2_ref, v2_ref):
        g = gs_ref[0]
        for s in range(1, N_DEV):
            g = g + gs_ref[s]
        g_ref[...] = g
        d_ref[...], m2_ref[...], v2_ref[...] = _adamw_math(w_ref[...], g, m_ref[...], v_ref[...])

    return pl.pallas_call(
        body, name="adamw_small", out_shape=[jax.ShapeDtypeStruct(w.shape, F32)] * 4,
        in_specs=[_VMEM] * 4, out_specs=[_VMEM] * 4, compiler_params=_params(),
    )(gathered, w, m, v)


SMALL = ["ln_in_g", "ln_in_b", "s5_lambda_re", "s5_lambda_im", "s5_log_dt", "s5_b_re", "s5_b_im", "s5_c_re", "s5_c_im",
         "s5_d", "s5_b_glu", "ret_gn_g", "ret_gn_b", "ln1_g", "ln1_b", "ln2_g", "ln2_b"]
LANE = 128


def _pack(arrs):
    parts = []
    for a in arrs:
        f = a.reshape(-1)
        parts.append(jnp.pad(f, (0, (-f.shape[0]) % LANE)))
    flat = jnp.concatenate(parts)
    rows = -(-flat.shape[0] // LANE)
    flat = jnp.pad(flat, (0, (-rows % 8) * LANE + rows * LANE - flat.shape[0]))
    return flat.reshape(-1, LANE)


def _unpack(packed, shapes):
    flat = packed.reshape(-1)
    out, off = [], 0
    for s in shapes:
        n = math.prod(s)
        out.append(flat[off:off + n].reshape(s))
        off += n + (-n) % LANE
    return out


def _rope_tables(tp):
    pos = jnp.arange(tp, dtype=F32) - float(PAD)
    inv_freq = 1.0 / (ROPE_BASE ** (jnp.arange(0, HEAD, 2, dtype=F32) / HEAD))
    ang = pos[:, None] * inv_freq[None, :]
    cos, sin = jnp.cos(ang), jnp.sin(ang)
    return jnp.concatenate([cos, cos], axis=1), jnp.concatenate([-sin, sin], axis=1)


def _decay_tables():
    log_gamma = jnp.log1p(-jnp.exp2(-5.0 - jnp.arange(RET_H, dtype=F32)))
    idx = jnp.arange(CHUNK, dtype=F32)
    diff = idx[:, None] - idx[None, :]
    dmat = jnp.where(diff[None] >= 0, jnp.exp(jnp.maximum(diff, 0.0)[None] * log_gamma[:, None, None]), 0.0)
    zeta = jnp.exp((CHUNK - 1.0 - idx)[None] * log_gamma[:, None])
    xi = jnp.exp((idx + 1.0)[None] * log_gamma[:, None])
    gam = jnp.exp(CHUNK * log_gamma)
    wide = lambda t: jnp.broadcast_to(t[:, :, None], (RET_H, CHUNK, HEAD))
    return dmat, wide(zeta), wide(xi), jnp.broadcast_to(gam[:, None, None], (RET_H, CHUNK, HEAD))


def _local_step(x2d, tgt, meta_full, w_int, w_out, w_up, w_down, w_glu, sp, distributed):
    tp = x2d.shape[0] + CHUNK
    row = lambda a: a.reshape(1, -1)
    cos2, sin2 = _rope_tables(tp)
    dmat, zeta_b, xi_b, gam_b = _decay_tables()
    li_g, li_b = row(sp["ln_in_g"]), row(sp["ln_in_b"])
    l1_g, l1_b, l2_g, l2_b = row(sp["ln1_g"]), row(sp["ln1_b"]), row(sp["ln2_g"]), row(sp["ln2_b"])
    gn_g, gn_b = row(sp["ret_gn_g"]), row(sp["ret_gn_b"])
    lre, lim = row(sp["s5_lambda_re"]), row(sp["s5_lambda_im"])
    ldt = row(jnp.repeat(sp["s5_log_dt"].reshape(-1), S5_P))
    to_t = lambda b: b.reshape(S5_G, S5_P, S5_H).transpose(2, 0, 1).reshape(S5_H, S5_N)
    bre_t, bim_t = to_t(sp["s5_b_re"]), to_t(sp["s5_b_im"])
    to_w = lambda c: jnp.tile(c.reshape(S5_W, S5_P), (1, 2))
    cre_w, cim_w = to_w(sp["s5_c_re"]), to_w(sp["s5_c_im"])
    s5_small = (lre, lim, ldt, bre_t, bim_t, cre_w, cim_w, row(sp["s5_d"]), w_glu, row(sp["s5_b_glu"]))

    jobs = (lambda *j: list(j)) if distributed else (lambda *j: [])
    c_arr = jnp.reshape(lax.axis_index("c"), (1,)).astype(jnp.int32) if distributed else None
    (xhat0, rstd0, u, q, k, v, gate), bg = _in_proj(x2d, meta_full, li_g, li_b, w_int, cos2, sin2,
                                                    jobs(_job_gather(w_out) if distributed else None))
    if distributed:
        w_out = bg[0].reshape(D_MODEL, D_MODEL)
    (ys5, xr, xi), bg = _s5_fwd(u, *s5_small, jobs=jobs(_job_gather(w_up) if distributed else None))
    if distributed:
        w_up = bg[0]
    (o, states), bg = _ret_fwd(q, k, v, dmat, zeta_b, xi_b, gam_b, jobs(_job_gather(w_down) if distributed else None))
    if distributed:
        w_down = bg[0].reshape(D_FF, D_MODEL)
    ycat, xhat1, rstd1, h1b, dr2, dffb, loss8, dl2g, dl2b, pre = _post_fwd(
        o, gate, ys5, xhat0, tgt, gn_g, gn_b, li_g, li_b, l1_g, l1_b, l2_g, l2_b, w_out, w_up, w_down)
    g_up, g_down, dh1m = _mlp_bwd(h1b, dffb, pre, w_up, w_down)
    (do, dgate, dys5, dh0r, g_out, dl1g, dl1b, dgng, dgnb), bg = _post_bwd(
        dh1m, dr2, xhat1, rstd1, ycat, o, gate, gn_g, gn_b, l1_g, w_out,
        jobs(*([_job_pair(g_up), _job_pair(g_down)] if distributed else [])))
    g_out = g_out.reshape(N_DEV, D_MODEL // N_DEV, D_MODEL)
    if distributed:
        p_up, p_down = _pair_sum([g_up, g_down], bg, c_arr, "pair_sum_mlp")
    (du, dlre, dlim, dldt, dbre_t, dbim_t, dcre, dcim, dd, dwglu, dbglu), bg = _s5_bwd(
        dys5, u, xr, xi, *s5_small,
        jobs=jobs(*([_job_chips(p_up), _job_chips(p_down), _job_pair(g_out)] if distributed else [])))
    if distributed:
        r_up, r_down = bg[0], bg[1]
        (p_out,) = _pair_sum([g_out], bg[2:], c_arr, "pair_sum_out")
    (dq, dk, dv), bg = _ret_bwd(q, k, v, do, states, cos2, sin2, dmat, zeta_b, xi_b, gam_b,
                                jobs(_job_chips(p_out) if distributed else None))
    draw, g_int, dlig, dlib = _in_bwd(du, dq, dk, dv, dgate, dh0r, xhat0, rstd0, li_g, li_b, w_int)
    g_int = g_int.reshape(N_DEV, PROJ_W // N_DEV, D_MODEL)
    if distributed:
        r_out = bg[0]
        (r1_in,) = _exchange([_job_pair(g_int)], "exchange_pair_in")
        (p_in,) = _pair_sum([g_int], [r1_in], c_arr, "pair_sum_in")
        big = dict(chip_sums=[p_in, p_out, p_up, p_down], received=[None, r_out, r_up, r_down])
    else:
        big = dict(partials=[g_int, g_out, g_up, g_down])

    from_t = lambda t: t.reshape(S5_H, S5_G, S5_P).transpose(1, 2, 0)
    small = {
        "ln_in_g": dlig, "ln_in_b": dlib, "s5_lambda_re": dlre, "s5_lambda_im": dlim, "s5_log_dt": dldt[:, :S5_G],
        "s5_b_re": from_t(dbre_t), "s5_b_im": from_t(dbim_t), "s5_c_re": dcre, "s5_c_im": dcim, "s5_d": dd,
        "s5_b_glu": dbglu, "ret_gn_g": dgng, "ret_gn_b": dgnb, "ln1_g": dl1g, "ln1_b": dl1b, "ln2_g": dl2g, "ln2_b": dl2b,
        "meta_tokens": draw[PAD:CHUNK], "s5_w_glu": dwglu, "loss": loss8[0:1, 0:1]}
    return draw, big, small


def kernel(x, meta_tokens, ln_in_g, ln_in_b, w_in, s5_lambda_re, s5_lambda_im, s5_log_dt, s5_b_re, s5_b_im, s5_c_re, s5_c_im, s5_d, s5_w_glu, s5_b_glu, ret_gn_g, ret_gn_b, w_out, ln1_g, ln1_b, w_up, w_down, ln2_g, ln2_b, loss_target, m_meta_tokens, m_ln_in_g, m_ln_in_b, m_w_in, m_s5_lambda_re, m_s5_lambda_im, m_s5_log_dt, m_s5_b_re, m_s5_b_im, m_s5_c_re, m_s5_c_im, m_s5_d, m_s5_w_glu, m_s5_b_glu, m_ret_gn_g, m_ret_gn_b, m_w_out, m_ln1_g, m_ln1_b, m_w_up, m_w_down, m_ln2_g, m_ln2_b, v_meta_tokens, v_ln_in_g, v_ln_in_b, v_w_in, v_s5_lambda_re, v_s5_lambda_im, v_s5_log_dt, v_s5_b_re, v_s5_b_im, v_s5_c_re, v_s5_c_im, v_s5_d, v_s5_w_glu, v_s5_b_glu, v_ret_gn_g, v_ret_gn_b, v_w_out, v_ln1_g, v_ln1_b, v_w_up, v_w_down, v_ln2_g, v_ln2_b):
    args = dict(locals())
    names = ["meta_tokens", "ln_in_g", "ln_in_b", "w_in", "s5_lambda_re", "s5_lambda_im", "s5_log_dt", "s5_b_re", "s5_b_im",
             "s5_c_re", "s5_c_im", "s5_d", "s5_w_glu", "s5_b_glu", "ret_gn_g", "ret_gn_b", "w_out", "ln1_g", "ln1_b",
             "w_up", "w_down", "ln2_g", "ln2_b"]
    ax, ay, ac = _place()
    me = 4 * ax + 2 * ay + ac

    a_int, a_glu, a_meta = _exchange([_job_gather(w_in[0].T.astype(MM)), _job_gather(s5_w_glu[0].astype(MM)),
                                      _job_gather(meta_tokens)], "gather_first")
    w_int = a_int.reshape(PROJ_W, D_MODEL)
    w_glu_f = a_glu.reshape(S5_W, S5_W)
    meta_full = a_meta.transpose(1, 0, 2).reshape(N_META, D_MODEL)

    sp = {n: args[n] for n in SMALL}
    draw, big, small = _local_step(x[0], loss_target[0], meta_full, w_int, w_out[0].astype(MM), w_up[0].astype(MM),
                                   w_down[0].astype(MM), w_glu_f, sp, True)

    order = SMALL + ["meta_tokens", "s5_w_glu", "loss"]
    shapes = [args[n].shape for n in SMALL] + [(N_META, D_MODEL), (S5_W, S5_W), (1,)]
    g_pack = _pack([small[n] for n in order])
    r_in, g_all = _exchange([_job_chips(big["chip_sums"][0]), _job_gather(g_pack)], "exchange_tail")
    g_int, g_out, g_up, g_down = _chip_sum(big["chip_sums"], [r_in] + big["received"][1:],
                                           jnp.reshape(2 * ax + ay, (1,)).astype(jnp.int32))
    big_grads = {"w_in": g_int.T[None], "w_out": g_out[None], "w_up": g_up[None], "w_down": g_down[None]}
    zeros = [jnp.zeros((N_META, D_MODEL), F32), jnp.zeros((S5_W, S5_W), F32), jnp.zeros((1,), F32)]
    packs = [_pack([args[p + n] for n in SMALL] + zeros) for p in ("", "m_", "v_")]
    g_sum, d_pack, m_pack, v_pack = _adamw_small(g_all, *packs)
    g_small = dict(zip(order, _unpack(g_sum, shapes)))
    loss = g_small["loss"].reshape(())
    d_small = dict(zip(order, _unpack(d_pack, shapes)))
    m_small = dict(zip(order, _unpack(m_pack, shapes)))
    v_small = dict(zip(order, _unpack(v_pack, shapes)))

    g_meta = lax.dynamic_slice(g_small["meta_tokens"], (0, me * (D_MODEL // N_DEV)), (N_META, D_MODEL // N_DEV))
    g_glu = lax.dynamic_slice(g_small["s5_w_glu"], (me * (S5_W // N_DEV), 0), (S5_W // N_DEV, S5_W))
    shard_grads = dict(big_grads, meta_tokens=g_meta, s5_w_glu=g_glu[None])
    sharded = ["w_in", "w_out", "w_up", "w_down"]
    two_d = lambda a: a.reshape(a.shape[-2:])
    res = _adamw([tuple(two_d(t) for t in (shard_grads[n], args[n], args["m_" + n], args["v_" + n])) for n in sharded],
                 "adamw_big", 8)
    res2 = _adamw([tuple(two_d(t) for t in (shard_grads[n], args[n], args["m_" + n], args["v_" + n]))
                   for n in ("meta_tokens", "s5_w_glu")], "adamw_shard_small", 1)
    upd = {}
    for idx, n in enumerate(sharded):
        upd[n] = [r.reshape(args[n].shape) for r in res[3 * idx:3 * idx + 3]]
    for idx, n in enumerate(("meta_tokens", "s5_w_glu")):
        upd[n] = [r.reshape(args[n].shape) for r in res2[3 * idx:3 * idx + 3]]

    grads, deltas, new_m, new_v = [], [], [], []
    for n in names:
        if n in upd:
            grads.append(shard_grads[n].reshape(args[n].shape))
            d, m2, v2 = upd[n]
        else:
            grads.append(g_small[n])
            d, m2, v2 = d_small[n], m_small[n], v_small[n]
        deltas.append(d)
        new_m.append(m2)
        new_v.append(v2)
    grad_x = draw[CHUNK:][None]
    return (loss, grad_x, *grads, *deltas, *new_m, *new_v)
```

```python
import math

import jax
import jax.numpy as jnp
from jax import lax
from jax.experimental import pallas as pl
from jax.experimental.pallas import tpu as pltpu

F32 = jnp.float32
MM = jnp.bfloat16

D_MODEL = 1024
N_META = 16
CHUNK = 128
PAD = CHUNK - N_META
S5_W, S5_G, S5_H, S5_P = 256, 16, 16, 64
S5_N = S5_G * S5_P
RET_W, RET_H, HEAD = 768, 6, 128
D_FF = 4096
PROJ_W = S5_W + 4 * RET_W
N_DEV = 8
FF_BLK = D_FF // N_DEV
ROW_BLK = 384
MLP_ROWS = 1408
ALPHA = 2.0 ** 0.25
LN_EPS = 1e-5
GN_EPS = 1e-5
ROPE_BASE = 10000.0
GELU_C = math.sqrt(2.0 / math.pi)
GELU_A = 0.044715
ADAM_LR, ADAM_B1, ADAM_B2, ADAM_EPS, ADAM_WD, ADAM_STEP = 0.001, 0.9, 0.999, 1e-08, 0.01, 10
VMEM_LIMIT = 60 * 1024 * 1024

_VMEM = pl.BlockSpec(memory_space=pltpu.VMEM)
_ANY = pl.BlockSpec(memory_space=pl.ANY)
_MESH = pl.DeviceIdType.MESH


def _params(sem=None):
    return pltpu.CompilerParams(dimension_semantics=sem, vmem_limit_bytes=VMEM_LIMIT)


def _dot(a, b):
    return jnp.dot(a.astype(MM), b.astype(MM), preferred_element_type=F32)


def _dot_nt(a, b):
    return lax.dot_general(a.astype(MM), b.astype(MM), (((1,), (1,)), ((), ())), preferred_element_type=F32)


def _dot_tn(a, b):
    return lax.dot_general(a.astype(MM), b.astype(MM), (((0,), (0,)), ((), ())), preferred_element_type=F32)


def _split3(a):
    hi = a.astype(jnp.bfloat16)
    r1 = a - hi.astype(F32)
    mid = r1.astype(jnp.bfloat16)
    lo = (r1 - mid.astype(F32)).astype(jnp.bfloat16)
    return hi, mid, lo


def _dot_sel_rhs(a, sel):
    s = sel.astype(jnp.bfloat16)
    return sum(jnp.dot(p, s, preferred_element_type=F32) for p in _split3(a))


def _dot_sel_lhs(sel, b):
    s = sel.astype(jnp.bfloat16)
    return sum(jnp.dot(s, p, preferred_element_type=F32) for p in _split3(b))


def _ln_fwd(r, eps):
    mu = jnp.mean(r, axis=-1, keepdims=True)
    xc = r - mu
    var = jnp.mean(xc * xc, axis=-1, keepdims=True)
    rstd = lax.rsqrt(var + eps)
    return xc * rstd, rstd


def _ln_bwd(dxhat, xhat, rstd):
    m1 = jnp.mean(dxhat, axis=-1, keepdims=True)
    m2 = jnp.mean(dxhat * xhat, axis=-1, keepdims=True)
    return rstd * (dxhat - m1 - xhat * m2)


def _colsum(a):
    return jnp.sum(a, axis=0, keepdims=True)


def _shift3(n_in):
    return [pl.BlockSpec((CHUNK, D_MODEL), (lambda i, j=j: (jnp.clip(3 * i - 1 + j, 0, n_in - 1), 0))) for j in range(3)]


def _in_proj(x2d, meta_full, ln_g, ln_b, w_int, cos2, sin2, jobs=()):
    seq = x2d.shape[0]
    tp = seq + CHUNK
    R = ROW_BLK

    def body(xa, xb, xc, meta_ref, g_ref, b_ref, w_ref, cos_ref, sin_ref,
             xhat_ref, rstd_ref, u_ref, q_ref, k_ref, v_ref, gate_ref, raw_ref):
        i = pl.program_id(0)
        raw_ref[0:CHUNK, :] = xa[...]
        raw_ref[CHUNK:2 * CHUNK, :] = xb[...]
        raw_ref[2 * CHUNK:3 * CHUNK, :] = xc[...]

        @pl.when(i == 0)
        def _():
            raw_ref[0:PAD, :] = jnp.zeros((PAD, D_MODEL), F32)
            raw_ref[PAD:CHUNK, :] = meta_ref[...]

        xhat, rstd = _ln_fwd(raw_ref[...], LN_EPS)
        xhat_ref[...] = xhat
        rstd_ref[...] = rstd
        hb = (xhat * g_ref[...] + b_ref[...]).astype(MM)
        valid = (i * R + lax.broadcasted_iota(jnp.int32, (R, 1), 0)) >= PAD

        def seg(lo, hi):
            return jnp.where(valid, _dot_nt(hb, w_ref[lo:hi, :]), 0.0)

        u_ref[...] = seg(0, S5_W)
        cos = cos_ref[...]
        sin = sin_ref[...]
        q = seg(S5_W, S5_W + RET_W)
        k = seg(S5_W + RET_W, S5_W + 2 * RET_W)
        for h in range(RET_H):
            sl = slice(h * HEAD, (h + 1) * HEAD)
            qh = q[:, sl]
            kh = k[:, sl]
            q_ref[:, sl] = (qh * cos + pltpu.roll(qh, HEAD // 2, 1) * sin).astype(q_ref.dtype)
            k_ref[:, sl] = ((kh * cos + pltpu.roll(kh, HEAD // 2, 1) * sin) * (HEAD ** -0.5)).astype(k_ref.dtype)
        v_ref[...] = seg(S5_W + 2 * RET_W, S5_W + 3 * RET_W).astype(v_ref.dtype)
        gate_ref[...] = seg(S5_W + 3 * RET_W, PROJ_W)

    def rows(w, dt):
        return pl.BlockSpec((R, w), lambda i: (i, 0)), jax.ShapeDtypeStruct((tp, w), dt)

    outs = [rows(D_MODEL, F32), rows(1, F32), rows(S5_W, F32), rows(RET_W, MM), rows(RET_W, MM),
            rows(RET_W, MM), rows(RET_W, F32)]
    full = lambda s: pl.BlockSpec(s, lambda i: (0,) * len(s))
    return _call(
        body, "in_proj", (tp // R,),
        _shift3(seq // CHUNK) + [full((N_META, D_MODEL)), full((1, D_MODEL)), full((1, D_MODEL)), _VMEM,
                                 pl.BlockSpec((R, HEAD), lambda i: (i, 0)), pl.BlockSpec((R, HEAD), lambda i: (i, 0))],
        [o[0] for o in outs], [o[1] for o in outs], [pltpu.VMEM((R, D_MODEL), F32)],
        (x2d, x2d, x2d, meta_full, ln_g, ln_b, w_int, cos2, sin2), jobs)


def _s5_disc(lre, lim, ldt, bre_t, bim_t):
    dt = jnp.exp(ldt)
    mag = jnp.exp(lre * dt)
    ang = lim * dt
    lbr = mag * jnp.cos(ang)
    lbi = mag * jnp.sin(ang)
    den = lre * lre + lim * lim
    nr = lbr - 1.0
    qr = (nr * lre + lbi * lim) / den
    qi = (lbi * lre - nr * lim) / den
    return lbr, lbi, qr * bre_t - qi * bim_t, qr * bim_t + qi * bre_t


def _s5_tables(lbr, lbi, reverse):
    if reverse:
        lbi = -lbi
    pw = [(lbr, lbi)]
    for _ in range(7):
        r, i = pw[-1]
        pw.append((r * lbr - i * lbi, r * lbi + i * lbr))
    row = lax.broadcasted_iota(jnp.int32, (8, S5_N), 0)
    tabs = []
    for k in range(3):
        sh = 2 ** k
        mask = (row < 8 - sh) if reverse else (row >= sh)
        ar, ai = pw[sh - 1]
        tabs.append((jnp.where(mask, ar, 0.0), jnp.where(mask, ai, 0.0)))
    pr = jnp.zeros((8, S5_N), F32)
    pi = jnp.zeros((8, S5_N), F32)
    for i in range(8):
        ar, ai = pw[7 - i] if reverse else pw[i]
        pr = jnp.where(row == i, ar, pr)
        pi = jnp.where(row == i, ai, pi)
    tabs.append((pr, pi))
    return tabs


def _store_tables(tab_ref, tabs):
    for k, (r, i) in enumerate(tabs):
        tab_ref[2 * k] = r
        tab_ref[2 * k + 1] = i


def _bd_mask():
    r = lax.broadcasted_iota(jnp.int32, (S5_W, S5_N), 0)
    c = lax.broadcasted_iota(jnp.int32, (S5_W, S5_N), 1)
    return jnp.right_shift(r, 4) == jnp.right_shift(c, 6)


def _s5_block_diag(bbr_t, bbi_t, cre_w, cim_w):
    mask = _bd_mask()
    bd = lambda t: jnp.where(mask, t, 0.0)
    return (bd(jnp.tile(bbr_t, (S5_G, 1))), bd(jnp.tile(bbi_t, (S5_G, 1))),
            bd(jnp.tile(cre_w, (1, S5_N // HEAD))), bd(jnp.tile(cim_w, (1, S5_N // HEAD))))


def _scan8(xr, xi, tab_ref, lanes, reverse):
    for k in range(3):
        sh = (8 - 2 ** k) if reverse else 2 ** k
        sr = pltpu.roll(xr, sh, 0)
        si = pltpu.roll(xi, sh, 0)
        mr = tab_ref[2 * k, :, lanes]
        mi = tab_ref[2 * k + 1, :, lanes]
        xr, xi = xr + (mr * sr - mi * si), xi + (mr * si + mi * sr)
    return xr, xi


S5_LANES = 256


def _gelu(y):
    t = jnp.tanh(GELU_C * (y + GELU_A * y * y * y))
    return 0.5 * y * (1.0 + t), t


def _s5_fwd(u, lre, lim, ldt, bre_t, bim_t, cre_w, cim_w, d_row, w_glu, b_glu, jobs=()):
    tp = u.shape[0]
    R = ROW_BLK

    def body(u_ref, lre_ref, lim_ref, ldt_ref, bre_ref, bim_ref, cre_ref, cim_ref, d_ref, wg_ref, bg_ref,
             y_ref, xr_ref, xi_ref, bbd_r, bbd_i, cbd_r, cbd_i, tab_ref, car_r, car_i):
        @pl.when(pl.program_id(0) == 0)
        def _():
            lbr, lbi, bbr, bbi = _s5_disc(lre_ref[...], lim_ref[...], ldt_ref[...], bre_ref[...], bim_ref[...])
            br, bi, cr, ci = _s5_block_diag(bbr, bbi, cre_ref[...], cim_ref[...])
            bbd_r[...] = br.astype(MM)
            bbd_i[...] = bi.astype(MM)
            cbd_r[...] = cr.astype(MM)
            cbd_i[...] = ci.astype(MM)
            _store_tables(tab_ref, _s5_tables(lbr, lbi, False))
            car_r[...] = jnp.zeros_like(car_r)
            car_i[...] = jnp.zeros_like(car_i)

        u = u_ref[...]
        ub = u.astype(MM)
        xr_ref[...] = jnp.dot(ub, bbd_r[...], preferred_element_type=F32)
        xi_ref[...] = jnp.dot(ub, bbd_i[...], preferred_element_type=F32)
        for j in range(S5_N // S5_LANES):
            lanes = pl.ds(j * S5_LANES, S5_LANES)
            pr = tab_ref[6, :, lanes]
            pi = tab_ref[7, :, lanes]

            def step(g, carry):
                cr, ci = carry
                rows = pl.ds(pl.multiple_of(g * 8, 8), 8)
                xr, xi = _scan8(xr_ref[rows, lanes], xi_ref[rows, lanes], tab_ref, lanes, False)
                br = jnp.broadcast_to(cr[7:8, :], cr.shape)
                bi = jnp.broadcast_to(ci[7:8, :], ci.shape)
                xr = xr + (pr * br - pi * bi)
                xi = xi + (pr * bi + pi * br)
                xr_ref[rows, lanes] = xr
                xi_ref[rows, lanes] = xi
                return xr, xi

            cr, ci = lax.fori_loop(0, R // 8, step, (car_r[:, lanes], car_i[:, lanes]), unroll=2)
            car_r[:, lanes] = cr
            car_i[:, lanes] = ci
        y = _dot_nt(xr_ref[...], cbd_r[...]) - _dot_nt(xi_ref[...], cbd_i[...]) + d_ref[...] * u
        yg, _ = _gelu(y)
        z = _dot(yg, wg_ref[...]) + bg_ref[...]
        y_ref[...] = yg * jax.nn.sigmoid(z)

    full = lambda a: pl.BlockSpec(a.shape, lambda i: (0,) * a.ndim)
    small = [lre, lim, ldt, bre_t, bim_t, cre_w, cim_w, d_row, w_glu, b_glu]
    return _call(
        body, "s5_fwd", (tp // R,),
        [pl.BlockSpec((R, S5_W), lambda i: (i, 0))] + [full(a) for a in small],
        [pl.BlockSpec((R, S5_W), lambda i: (i, 0)), pl.BlockSpec((R, S5_N), lambda i: (i, 0)),
         pl.BlockSpec((R, S5_N), lambda i: (i, 0))],
        [jax.ShapeDtypeStruct((tp, S5_W), F32), jax.ShapeDtypeStruct((tp, S5_N), F32),
         jax.ShapeDtypeStruct((tp, S5_N), F32)],
        [pltpu.VMEM((S5_W, S5_N), MM)] * 4 + [pltpu.VMEM((8, 8, S5_N), F32), pltpu.VMEM((8, S5_N), F32),
                                              pltpu.VMEM((8, S5_N), F32)],
        (u, *small), jobs)


def _s5_bwd(dy_out, u, xr, xi, lre, lim, ldt, bre_t, bim_t, cre_w, cim_w, d_row, w_glu, b_glu, jobs=()):
    tp = u.shape[0]
    R = ROW_BLK
    nb = tp // R

    def body(dyo_ref, u_ref, xr_ref, xi_ref, xpr_ref, xpi_ref,
             lre_ref, lim_ref, ldt_ref, bre_ref, bim_ref, cre_ref, cim_ref, d_ref, wg_ref, bg_ref,
             du_ref, dlre_ref, dlim_ref, dldt_ref, dbre_ref, dbim_ref, dcre_ref, dcim_ref, dd_ref, dwg_ref, dbg_ref,
             bbd_r, bbd_i, cbd_r, cbd_i, tab_ref, car_r, car_i, gr_ref, gi_ref, xer_ref, xei_ref,
             abr, abi, acr, aci, adr, adi):
        i = pl.program_id(0)

        @pl.when(i == 0)
        def _():
            lbr, lbi, bbr, bbi = _s5_disc(lre_ref[...], lim_ref[...], ldt_ref[...], bre_ref[...], bim_ref[...])
            br, bi, cr, ci = _s5_block_diag(bbr, bbi, cre_ref[...], cim_ref[...])
            bbd_r[...] = br.astype(MM)
            bbd_i[...] = bi.astype(MM)
            cbd_r[...] = cr.astype(MM)
            cbd_i[...] = ci.astype(MM)
            _store_tables(tab_ref, _s5_tables(lbr, lbi, True))
            for ref in (car_r, car_i, abr, abi, acr, aci, adr, adi, dd_ref, dwg_ref, dbg_ref):
                ref[...] = jnp.zeros_like(ref)

        u = u_ref[...]
        xrv = xr_ref[...]
        xiv = xi_ref[...]
        y = _dot_nt(xrv, cbd_r[...]) - _dot_nt(xiv, cbd_i[...]) + d_ref[...] * u
        yg, t = _gelu(y)
        z = _dot(yg, wg_ref[...]) + bg_ref[...]
        s = jax.nn.sigmoid(z)
        dout = dyo_ref[...]
        dz = dout * yg * s * (1.0 - s)
        dyg = dout * s + _dot_nt(dz, wg_ref[...])
        dwg_ref[...] += _dot_tn(yg, dz)
        dbg_ref[...] += _colsum(dz)
        dy = dyg * (0.5 * (1.0 + t) + 0.5 * y * (1.0 - t * t) * GELU_C * (1.0 + 3.0 * GELU_A * y * y))
        dd_ref[...] += _colsum(dy * u)
        acr[...] += _dot_tn(dy, xrv)
        aci[...] -= _dot_tn(dy, xiv)
        gr_ref[...] = _dot(dy, cbd_r[...])
        gi_ref[...] = -_dot(dy, cbd_i[...])
        has_prev = (i < nb - 1).astype(F32)
        xer_ref[0:8, :] = xpr_ref[...] * has_prev
        xei_ref[0:8, :] = xpi_ref[...] * has_prev
        xer_ref[8:R + 8, :] = xrv
        xei_ref[8:R + 8, :] = xiv
        row = lax.broadcasted_iota(jnp.int32, (8, S5_LANES), 0)
        for j in range(S5_N // S5_LANES):
            lanes = pl.ds(j * S5_LANES, S5_LANES)
            pr = tab_ref[6, :, lanes]
            pi = tab_ref[7, :, lanes]

            def step(n, carry):
                cr, ci, sar, sai = carry
                g = R // 8 - 1 - n
                r0 = pl.multiple_of(g * 8, 8)
                rows = pl.ds(r0, 8)
                gr, gi = _scan8(gr_ref[rows, lanes], gi_ref[rows, lanes], tab_ref, lanes, True)
                br = jnp.broadcast_to(cr[0:1, :], cr.shape)
                bi = jnp.broadcast_to(ci[0:1, :], ci.shape)
                gr = gr + (pr * br - pi * bi)
                gi = gi + (pr * bi + pi * br)
                gr_ref[rows, lanes] = gr
                gi_ref[rows, lanes] = gi
                last = row == 7
                xpr = pltpu.roll(jnp.where(last, xer_ref[rows, lanes], xer_ref[pl.ds(r0 + 8, 8), lanes]), 1, 0)
                xpi = pltpu.roll(jnp.where(last, xei_ref[rows, lanes], xei_ref[pl.ds(r0 + 8, 8), lanes]), 1, 0)
                return gr, gi, sar + (gr * xpr + gi * xpi), sai + (gi * xpr - gr * xpi)

            cr, ci, sar, sai = lax.fori_loop(
                0, R // 8, step, (car_r[:, lanes], car_i[:, lanes], adr[:, lanes], adi[:, lanes]), unroll=2)
            car_r[:, lanes] = cr
            car_i[:, lanes] = ci
            adr[:, lanes] = sar
            adi[:, lanes] = sai
        grv = gr_ref[...]
        giv = gi_ref[...]
        du_ref[...] = dy * d_ref[...] + _dot_nt(grv, bbd_r[...]) + _dot_nt(giv, bbd_i[...])
        abr[...] += _dot_tn(u, grv)
        abi[...] += _dot_tn(u, giv)

        @pl.when(i == nb - 1)
        def _():
            mask = _bd_mask()
            r16 = lax.broadcasted_iota(jnp.int32, (S5_H, S5_W), 1)
            h16 = lax.broadcasted_iota(jnp.int32, (S5_H, S5_W), 0)
            fold_b = jnp.bitwise_and(r16, S5_H - 1) == h16
            c64 = lax.broadcasted_iota(jnp.int32, (S5_N, S5_P), 0)
            p64 = lax.broadcasted_iota(jnp.int32, (S5_N, S5_P), 1)
            fold_c = jnp.bitwise_and(c64, S5_P - 1) == p64
            dbbr = _dot_sel_lhs(fold_b, jnp.where(mask, abr[...], 0.0))
            dbbi = _dot_sel_lhs(fold_b, jnp.where(mask, abi[...], 0.0))
            dcre_ref[...] = _dot_sel_rhs(jnp.where(mask, acr[...], 0.0), fold_c)
            dcim_ref[...] = _dot_sel_rhs(jnp.where(mask, aci[...], 0.0), fold_c)
            dlbr = _colsum(adr[...])
            dlbi = _colsum(adi[...])
            _, vjp = jax.vjp(_s5_disc, lre_ref[...], lim_ref[...], ldt_ref[...], bre_ref[...], bim_ref[...])
            dlre, dlim, dldt, dbre, dbim = vjp((dlbr, dlbi, dbbr, dbbi))
            dlre_ref[...] = dlre
            dlim_ref[...] = dlim
            dbre_ref[...] = dbre
            dbim_ref[...] = dbim
            gsel = jnp.right_shift(lax.broadcasted_iota(jnp.int32, (S5_N, HEAD), 0), 6) == \
                lax.broadcasted_iota(jnp.int32, (S5_N, HEAD), 1)
            dldt_ref[...] = _dot_sel_rhs(dldt, gsel)

    full = lambda a: pl.BlockSpec(a.shape, lambda i: (0,) * a.ndim)
    rev = lambda w: pl.BlockSpec((R, w), lambda i: (nb - 1 - i, 0))
    prev8 = pl.BlockSpec((8, S5_N), lambda i: (jnp.maximum((nb - 1 - i) * (R // 8) - 1, 0), 0))
    small = [lre, lim, ldt, bre_t, bim_t, cre_w, cim_w, d_row, w_glu, b_glu]
    outs = [((tp, S5_W), rev(S5_W))] + [
        (s, pl.BlockSpec(s, lambda i: (0, 0))) for s in
        [(1, S5_N), (1, S5_N), (1, HEAD), (S5_H, S5_N), (S5_H, S5_N), (S5_W, S5_P), (S5_W, S5_P),
         (1, S5_W), (S5_W, S5_W), (1, S5_W)]]
    return _call(
        body, "s5_bwd", (nb,),
        [rev(S5_W), rev(S5_W), rev(S5_N), rev(S5_N), prev8, prev8] + [full(a) for a in small],
        [o[1] for o in outs], [jax.ShapeDtypeStruct(o[0], F32) for o in outs],
        [pltpu.VMEM((S5_W, S5_N), MM)] * 4 + [
            pltpu.VMEM((8, 8, S5_N), F32), pltpu.VMEM((8, S5_N), F32), pltpu.VMEM((8, S5_N), F32),
            pltpu.VMEM((R, S5_N), F32), pltpu.VMEM((R, S5_N), F32),
            pltpu.VMEM((R + 8, S5_N), F32), pltpu.VMEM((R + 8, S5_N), F32)] + [pltpu.VMEM((S5_W, S5_N), F32)] * 4 + [
            pltpu.VMEM((8, S5_N), F32), pltpu.VMEM((8, S5_N), F32)],
        (dy_out, u, xr, xi, xr, xi, *small), jobs)


def _ret_fwd(q, k, v, dmat, zeta_b, xi_b, gam_b, jobs=()):
    tp = q.shape[0]
    nc = tp // CHUNK

    def body(q_ref, k_ref, v_ref, dm_ref, ze_ref, xi_ref, ga_ref, o_ref, st_ref, s_ref):
        @pl.when(pl.program_id(0) == 0)
        def _():
            s_ref[...] = jnp.zeros_like(s_ref)

        for h in range(RET_H):
            sl = slice(h * HEAD, (h + 1) * HEAD)
            qh, kh, vh = q_ref[:, sl], k_ref[:, sl], v_ref[:, sl]
            sh = s_ref[h]
            st_ref[0, sl, :] = sh
            scores = _dot_nt(qh, kh) * dm_ref[h]
            o_ref[:, sl] = _dot(scores, vh) + _dot(qh, sh) * xi_ref[h]
            s_ref[h] = ga_ref[h] * sh + _dot_tn(kh.astype(F32) * ze_ref[h], vh)

    blk = pl.BlockSpec((CHUNK, RET_W), lambda c: (c, 0))
    cst = pl.BlockSpec((RET_H, HEAD, HEAD), lambda c: (0, 0, 0))
    return _call(
        body, "ret_fwd", (nc,), [blk, blk, blk, cst, cst, cst, cst],
        [blk, pl.BlockSpec((1, RET_W, HEAD), lambda c: (c, 0, 0))],
        [jax.ShapeDtypeStruct((tp, RET_W), F32), jax.ShapeDtypeStruct((nc, RET_W, HEAD), F32)],
        [pltpu.VMEM((RET_H, HEAD, HEAD), F32)], (q, k, v, dmat, zeta_b, xi_b, gam_b), jobs)


def _ret_bwd(q, k, v, do, states, cos2, sin2, dmat, zeta_b, xi_b, gam_b, jobs=()):
    tp = q.shape[0]
    nc = tp // CHUNK

    def body(q_ref, k_ref, v_ref, do_ref, st_ref, cos_ref, sin_ref, dm_ref, ze_ref, xi_ref, ga_ref,
             dq_ref, dk_ref, dv_ref, ds_ref):
        @pl.when(pl.program_id(0) == 0)
        def _():
            ds_ref[...] = jnp.zeros_like(ds_ref)

        cos = cos_ref[...]
        sin = sin_ref[...]
        for h in range(RET_H):
            sl = slice(h * HEAD, (h + 1) * HEAD)
            qh, kh, vh = q_ref[:, sl], k_ref[:, sl], v_ref[:, sl]
            dmh = dm_ref[h]
            sh = st_ref[0, sl, :]
            dsn = ds_ref[h]
            doh = do_ref[:, sl]
            dox = doh * xi_ref[h]
            a = _dot_nt(qh, kh) * dmh
            dqk = _dot_nt(doh, vh) * dmh
            kz = kh.astype(F32) * ze_ref[h]
            dv_ref[:, sl] = _dot_tn(a, doh) + _dot(kz, dsn)
            dqr = _dot(dqk, kh) + _dot_nt(dox, sh)
            dkr = _dot_tn(dqk, qh) + ze_ref[h] * _dot_nt(vh, dsn)
            ds_ref[h] = ga_ref[h] * dsn + _dot_tn(qh, dox)
            dq_ref[:, sl] = dqr * cos - pltpu.roll(dqr, HEAD // 2, 1) * sin
            dk_ref[:, sl] = (dkr * cos - pltpu.roll(dkr, HEAD // 2, 1) * sin) * (HEAD ** -0.5)

    blk = pl.BlockSpec((CHUNK, RET_W), lambda c: (nc - 1 - c, 0))
    tab = pl.BlockSpec((CHUNK, HEAD), lambda c: (nc - 1 - c, 0))
    cst = pl.BlockSpec((RET_H, HEAD, HEAD), lambda c: (0, 0, 0))
    return _call(
        body, "ret_bwd", (nc,),
        [blk, blk, blk, blk, pl.BlockSpec((1, RET_W, HEAD), lambda c: (nc - 1 - c, 0, 0)), tab, tab, cst, cst, cst, cst],
        [blk, blk, blk], [jax.ShapeDtypeStruct((tp, RET_W), F32)] * 3, [pltpu.VMEM((RET_H, HEAD, HEAD), F32)],
        (q, k, v, do, states, cos2, sin2, dmat, zeta_b, xi_b, gam_b), jobs)


def _gn_gate(o, gate, gn_g, gn_b):
    xhat, rstd = _ln_fwd(o, GN_EPS)
    on = xhat * gn_g + gn_b
    s = jax.nn.sigmoid(gate)
    return gate * s * on, xhat, rstd, on, s


def _post_fwd(o, gate, ys5, xhat0, tgt, gn_g, gn_b, li_g, li_b, l1_g, l1_b, l2_g, l2_b, w_out, w_up, w_down):
    tp = o.shape[0]
    seq = tgt.shape[0]
    R = ROW_BLK

    def body(o_ref, g_ref, ys_ref, xh0_ref, ta, tb, tc, gng, gnb, lig, lib, l1g, l1b, l2g, l2b, wo_ref, wu_ref, wd_ref,
             ycat_ref, xh1_ref, rstd1_ref, h1b_ref, dr2_ref, dffb_ref, loss_ref, dl2g_ref, dl2b_ref, pre_ref, tgt_ref):
        i = pl.program_id(0)

        @pl.when(i == 0)
        def _():
            for ref in (loss_ref, dl2g_ref, dl2b_ref):
                ref[...] = jnp.zeros_like(ref)

        tgt_ref[0:CHUNK, :] = ta[...]
        tgt_ref[CHUNK:2 * CHUNK, :] = tb[...]
        tgt_ref[2 * CHUNK:3 * CHUNK, :] = tc[...]
        ycat_ref[:, 0:S5_W] = ys_ref[...].astype(ycat_ref.dtype)
        for h in range(RET_H):
            sl = slice(h * HEAD, (h + 1) * HEAD)
            yret = _gn_gate(o_ref[:, sl], g_ref[:, sl], gng[:, sl], gnb[:, sl])[0]
            ycat_ref[:, S5_W + h * HEAD:S5_W + (h + 1) * HEAD] = yret.astype(ycat_ref.dtype)
        mixed = _dot(ycat_ref[...], wo_ref[...])
        h0 = xh0_ref[...] * lig[...] + lib[...]
        xh1, rstd1 = _ln_fwd(ALPHA * h0 + mixed, LN_EPS)
        xh1_ref[...] = xh1
        rstd1_ref[...] = rstd1
        h1 = xh1 * l1g[...] + l1b[...]
        h1b = h1.astype(MM)
        h1b_ref[...] = h1b
        ff = jnp.zeros((R, D_MODEL), F32)
        for d in range(N_DEV):
            pre = jnp.maximum(_dot(h1b, wu_ref[d]), 0.0)
            pre_ref[:, d * FF_BLK:(d + 1) * FF_BLK] = pre
            ff = ff + _dot(pre * pre, wd_ref[d * FF_BLK:(d + 1) * FF_BLK, :])
        xh2, rstd2 = _ln_fwd(ALPHA * h1 + ff, LN_EPS)
        h2 = xh2 * l2g[...] + l2b[...]
        valid = (i * R + lax.broadcasted_iota(jnp.int32, (R, 1), 0)) >= CHUNK
        err = jnp.where(valid, h2 - tgt_ref[...], 0.0)
        loss_ref[...] += 0.5 * jnp.sum(err * err) / D_MODEL
        dh2 = err * (1.0 / D_MODEL)
        dl2g_ref[...] += _colsum(dh2 * xh2)
        dl2b_ref[...] += _colsum(dh2)
        dr2 = _ln_bwd(dh2 * l2g[...], xh2, rstd2)
        dr2_ref[...] = dr2
        dffb_ref[...] = dr2.astype(MM)

    row = lambda w: pl.BlockSpec((R, w), lambda i: (i, 0))
    full = lambda a: pl.BlockSpec(a.shape, lambda i: (0,) * a.ndim)
    vecs = [gn_g, gn_b, li_g, li_b, l1_g, l1_b, l2_g, l2_b]
    acc = lambda s: (pl.BlockSpec(s, lambda i: (0, 0)), jax.ShapeDtypeStruct(s, F32))
    outs = [(row(D_MODEL), jax.ShapeDtypeStruct((tp, D_MODEL), MM)),
            (row(D_MODEL), jax.ShapeDtypeStruct((tp, D_MODEL), F32)),
            (row(1), jax.ShapeDtypeStruct((tp, 1), F32)),
            (row(D_MODEL), jax.ShapeDtypeStruct((tp, D_MODEL), MM)),
            (row(D_MODEL), jax.ShapeDtypeStruct((tp, D_MODEL), F32)),
            (row(D_MODEL), jax.ShapeDtypeStruct((tp, D_MODEL), MM)),
            acc((8, HEAD)), acc((1, D_MODEL)), acc((1, D_MODEL)),
            (row(D_FF), jax.ShapeDtypeStruct((tp, D_FF), F32))]
    return pl.pallas_call(
        body, name="post_fwd", grid=(tp // R,),
        in_specs=[row(RET_W), row(RET_W), row(S5_W), row(D_MODEL)] + _shift3(seq // CHUNK) + [full(a) for a in vecs]
        + [_VMEM, _VMEM, _VMEM],
        out_specs=[o[0] for o in outs], out_shape=[o[1] for o in outs],
        scratch_shapes=[pltpu.VMEM((R, D_MODEL), F32)],
        compiler_params=_params(("arbitrary",)),
    )(o, gate, ys5, xhat0, tgt, tgt, tgt, *vecs, w_out, w_up, w_down)


def _mlp_bwd(h1b, dffb, pre, w_up, w_down):
    tp = h1b.shape[0]
    R = MLP_ROWS if tp % MLP_ROWS == 0 else ROW_BLK
    nr = tp // R

    def body(h_ref, df_ref, pre_ref, wu_ref, wd_ref, gup_ref, gdn_ref, dh1_ref, aup, adn):
        d = pl.program_id(0)
        r = pl.program_id(1)

        @pl.when(r == 0)
        def _():
            aup[...] = jnp.zeros_like(aup)
            adn[...] = jnp.zeros_like(adn)

        h = h_ref[...]
        df = df_ref[...]
        wu = wu_ref[0]
        wd = wd_ref[0]
        pre = pre_ref[...]
        dpre = (_dot_nt(df, wd) * (2.0 * pre)).astype(MM)

        aup[...] += _dot_tn(h, dpre)
        adn[...] += _dot_tn(pre * pre, df)
        contrib = _dot_nt(dpre, wu)
        rows = pl.ds(pl.multiple_of(r * R, 64), R)

        @pl.when(d == 0)
        def _():
            dh1_ref[rows, :] = contrib

        @pl.when(d > 0)
        def _():
            dh1_ref[rows, :] += contrib

        @pl.when(r == nr - 1)
        def _():
            gup_ref[0] = aup[...].astype(gup_ref.dtype)
            gdn_ref[0] = adn[...].astype(gdn_ref.dtype)

    return pl.pallas_call(
        body, name="mlp_bwd", grid=(N_DEV, nr),
        in_specs=[pl.BlockSpec((R, D_MODEL), lambda d, r: (r, 0)), pl.BlockSpec((R, D_MODEL), lambda d, r: (r, 0)),
                  pl.BlockSpec((R, FF_BLK), lambda d, r: (r, d)),
                  pl.BlockSpec((1, D_MODEL, FF_BLK), lambda d, r: (d, 0, 0)),
                  pl.BlockSpec((1, FF_BLK, D_MODEL), lambda d, r: (d, 0, 0))],
        out_specs=[pl.BlockSpec((1, D_MODEL, FF_BLK), lambda d, r: (d, 0, 0)),
                   pl.BlockSpec((1, FF_BLK, D_MODEL), lambda d, r: (d, 0, 0)), _VMEM],
        out_shape=[jax.ShapeDtypeStruct((N_DEV, D_MODEL, FF_BLK), MM), jax.ShapeDtypeStruct((N_DEV, FF_BLK, D_MODEL), MM),
                   jax.ShapeDtypeStruct((tp, D_MODEL), F32)],
        scratch_shapes=[pltpu.VMEM((D_MODEL, FF_BLK), F32), pltpu.VMEM((FF_BLK, D_MODEL), F32)],
        compiler_params=_params(("arbitrary", "arbitrary")),
    )(h1b, dffb, pre, w_up, w_down.reshape(N_DEV, FF_BLK, D_MODEL))


def _post_bwd(dh1m, dr2, xhat1, rstd1, ycat, o, gate, gn_g, gn_b, l1_g, w_out, jobs=()):
    tp = o.shape[0]
    R = ROW_BLK
    nb = tp // R

    def body(dm_ref, dr2_ref, xh1_ref, rs1_ref, yc_ref, o_ref, g_ref, gng, gnb, l1g, wo_ref,
             do_ref, dg_ref, dys_ref, dh0_ref, gwo_ref, dl1g_ref, dl1b_ref, dgng_ref, dgnb_ref, awo):
        i = pl.program_id(0)

        @pl.when(i == 0)
        def _():
            for ref in (awo, dl1g_ref, dl1b_ref, dgng_ref, dgnb_ref):
                ref[...] = jnp.zeros_like(ref)

        dh1 = dm_ref[...] + ALPHA * dr2_ref[...]
        xh1 = xh1_ref[...]
        dl1g_ref[...] += _colsum(dh1 * xh1)
        dl1b_ref[...] += _colsum(dh1)
        dr1 = _ln_bwd(dh1 * l1g[...], xh1, rs1_ref[...])
        dh0_ref[...] = ALPHA * dr1
        dmix = dr1.astype(MM)
        awo[...] += _dot_tn(yc_ref[...], dmix)
        dyc = _dot_nt(dmix, wo_ref[...])
        dys_ref[...] = dyc[:, 0:S5_W]
        for h in range(RET_H):
            sl = slice(h * HEAD, (h + 1) * HEAD)
            gt = g_ref[:, sl]
            _, xhat, rstd, on, s = _gn_gate(o_ref[:, sl], gt, gng[:, sl], gnb[:, sl])
            dyr = dyc[:, S5_W + h * HEAD:S5_W + (h + 1) * HEAD]
            dg_ref[:, sl] = dyr * on * (s * (1.0 + gt * (1.0 - s)))
            don = dyr * gt * s
            dgng_ref[:, sl] += _colsum(don * xhat)
            dgnb_ref[:, sl] += _colsum(don)
            do_ref[:, sl] = _ln_bwd(don * gng[:, sl], xhat, rstd)

        @pl.when(i == nb - 1)
        def _():
            gwo_ref[...] = awo[...].astype(gwo_ref.dtype)

    row = lambda w: pl.BlockSpec((R, w), lambda i: (i, 0))
    full = lambda a: pl.BlockSpec(a.shape, lambda i: (0,) * a.ndim)
    acc = lambda s, dt=F32: (pl.BlockSpec(s, lambda i: (0, 0)), jax.ShapeDtypeStruct(s, dt))
    outs = [(row(RET_W), jax.ShapeDtypeStruct((tp, RET_W), F32)), (row(RET_W), jax.ShapeDtypeStruct((tp, RET_W), F32)),
            (row(S5_W), jax.ShapeDtypeStruct((tp, S5_W), F32)), (row(D_MODEL), jax.ShapeDtypeStruct((tp, D_MODEL), F32)),
            acc((D_MODEL, D_MODEL), MM), acc((1, D_MODEL)), acc((1, D_MODEL)), acc((1, RET_W)), acc((1, RET_W))]
    return _call(
        body, "post_bwd", (nb,),
        [row(D_MODEL), row(D_MODEL), row(D_MODEL), row(1), row(D_MODEL), row(RET_W), row(RET_W),
         full(gn_g), full(gn_b), full(l1_g), _VMEM],
        [o[0] for o in outs], [o[1] for o in outs],
        [pltpu.VMEM((D_MODEL, D_MODEL), F32)],
        (dh1m, dr2, xhat1, rstd1, ycat, o, gate, gn_g, gn_b, l1_g, w_out), jobs)


def _in_bwd(du, dq, dk, dv, dg, dh0r, xhat0, rstd0, li_g, li_b, w_int, jobs=()):
    tp = du.shape[0]
    R = ROW_BLK
    nb = tp // R
    segs = [(0, S5_W)] + [(S5_W + n * RET_W, S5_W + (n + 1) * RET_W) for n in range(4)]

    def body(du_ref, dq_ref, dk_ref, dv_ref, dg_ref, dh0r_ref, xh_ref, rs_ref, lig, lib, w_ref,
             draw_ref, gw_ref, dlg_ref, dlb_ref, aw):
        i = pl.program_id(0)

        @pl.when(i == 0)
        def _():
            for ref in (aw, dlg_ref, dlb_ref):
                ref[...] = jnp.zeros_like(ref)

        valid = (i * R + lax.broadcasted_iota(jnp.int32, (R, 1), 0)) >= PAD
        xh = xh_ref[...]
        hb = (xh * lig[...] + lib[...]).astype(MM)
        dh0 = dh0r_ref[...]
        for (lo, hi), ref in zip(segs, (du_ref, dq_ref, dk_ref, dv_ref, dg_ref)):
            dseg = jnp.where(valid, ref[...], 0.0).astype(MM)
            dh0 = dh0 + _dot(dseg, w_ref[lo:hi, :])
            aw[lo:hi, :] += _dot_tn(dseg, hb)
        dlg_ref[...] += _colsum(dh0 * xh)
        dlb_ref[...] += _colsum(dh0)
        draw_ref[...] = _ln_bwd(dh0 * lig[...], xh, rs_ref[...])

        @pl.when(i == nb - 1)
        def _():
            gw_ref[...] = aw[...].astype(gw_ref.dtype)

    row = lambda w: pl.BlockSpec((R, w), lambda i: (i, 0))
    full = lambda a: pl.BlockSpec(a.shape, lambda i: (0,) * a.ndim)
    acc = lambda s, dt=F32: (pl.BlockSpec(s, lambda i: (0, 0)), jax.ShapeDtypeStruct(s, dt))
    outs = [(row(D_MODEL), jax.ShapeDtypeStruct((tp, D_MODEL), F32)), acc((PROJ_W, D_MODEL), MM),
            acc((1, D_MODEL)), acc((1, D_MODEL))]
    return _call(
        body, "in_bwd", (nb,),
        [row(S5_W), row(RET_W), row(RET_W), row(RET_W), row(RET_W), row(D_MODEL), row(D_MODEL), row(1),
         full(li_g), full(li_b), _VMEM],
        [o[0] for o in outs], [o[1] for o in outs], [pltpu.VMEM((PROJ_W, D_MODEL), F32)],
        (du, dq, dk, dv, dg, dh0r, xhat0, rstd0, li_g, li_b, w_int), jobs)


def _place():
    return lax.axis_index("x"), lax.axis_index("y"), lax.axis_index("c")


def _dma_sems(n):
    return pltpu.SemaphoreType.DMA((n,))


def _job_gather(shard):
    def parts(ins, outs, sems):
        (src,), (out,), (send_sems, recv_sems, local_sem) = ins, outs, sems
        x, y, c = _place()
        me, sib = (x, y, c), (x, y, 1 - c)
        chips = [(1 - x, y), (x, 1 - y), (1 - x, 1 - y)]

        def slot(dev):
            return out.at[4 * dev[0] + 2 * dev[1] + dev[2]]

        def copy(k, block, to, from_input=False):
            return pltpu.make_async_remote_copy(
                src_ref=src if from_input else slot(block), dst_ref=slot(block),
                send_sem=send_sems.at[k], recv_sem=recv_sems.at[k], device_id=to, device_id_type=_MESH)

        mine = pltpu.make_async_copy(src, slot(me), local_sem.at[0])
        first = [copy(0, me, sib, True)] + [copy(1 + j, me, (*chip, c), True) for j, chip in enumerate(chips)]
        return me, sib, chips, copy, mine, first

    def start(ins, outs, sems):
        _, _, _, _, mine, first = parts(ins, outs, sems)
        mine.start()
        for cp in first:
            cp.start()

    def finish(ins, outs, sems):
        me, sib, chips, copy, mine, first = parts(ins, outs, sems)
        c = me[2]
        passed = []
        for j, chip in enumerate(chips):
            copy(1 + j, (*chip, c), me).wait_recv()
            cp = copy(4 + j, (*chip, c), sib)
            cp.start()
            passed.append(cp)
        copy(0, sib, me).wait_recv()
        for j, chip in enumerate(chips):
            copy(4 + j, (*chip, 1 - c), me).wait_recv()
        for cp in first + passed:
            cp.wait_send()
        mine.wait()

    return dict(ins=[shard], outs=[jax.ShapeDtypeStruct((N_DEV,) + shard.shape, shard.dtype)],
                sems=[_dma_sems(7), _dma_sems(7), _dma_sems(1)], start=start, finish=finish)


def _job_pair(g):
    def copies(ins, outs, sems):
        x, y, c = _place()
        return [pltpu.make_async_remote_copy(
            src_ref=ins[0].at[2 * j + (1 - c)], dst_ref=outs[0].at[j], send_sem=sems[0].at[j], recv_sem=sems[1].at[j],
            device_id=(x, y, 1 - c), device_id_type=_MESH) for j in range(4)]

    def start(ins, outs, sems):
        for cp in copies(ins, outs, sems):
            cp.start()

    def finish(ins, outs, sems):
        for cp in copies(ins, outs, sems):
            cp.wait()

    return dict(ins=[g], outs=[jax.ShapeDtypeStruct((4,) + g.shape[1:], g.dtype)], sems=[_dma_sems(4), _dma_sems(4)],
                start=start, finish=finish)


def _job_chips(p):
    def copies(ins, outs, sems):
        x, y, c = _place()
        chips = [(1 - x, y), (x, 1 - y), (1 - x, 1 - y)]
        return [pltpu.make_async_remote_copy(
            src_ref=ins[0].at[2 * chip[0] + chip[1]], dst_ref=outs[0].at[k], send_sem=sems[0].at[k],
            recv_sem=sems[1].at[k], device_id=(*chip, c), device_id_type=_MESH) for k, chip in enumerate(chips)]

    def start(ins, outs, sems):
        for cp in copies(ins, outs, sems):
            cp.start()

    def finish(ins, outs, sems):
        for cp in copies(ins, outs, sems):
            cp.wait()

    return dict(ins=[p], outs=[jax.ShapeDtypeStruct((3,) + p.shape[1:], p.dtype)], sems=[_dma_sems(3), _dma_sems(3)],
                start=start, finish=finish)


def _split_job_refs(jobs, ins, outs, sems):
    res, a, b, c = [], 0, 0, 0
    for job in jobs:
        na, nb, nc = len(job["ins"]), len(job["outs"]), len(job["sems"])
        res.append((ins[a:a + na], outs[b:b + nb], sems[c:c + nc]))
        a, b, c = a + na, b + nb, c + nc
    return res


def _call(body, name, grid, in_specs, out_specs, out_shape, scratch, args, jobs=()):
    jobs = list(jobs)
    n_in, n_out, n_scr = len(in_specs), len(out_specs), len(scratch)
    j_in = [a for job in jobs for a in job["ins"]]
    j_out = [o for job in jobs for o in job["outs"]]
    j_scr = [s for job in jobs for s in job["sems"]]
    nsteps = grid[0]

    def wrapped(*refs):
        ins, jins = refs[:n_in], refs[n_in:n_in + len(j_in)]
        refs = refs[n_in + len(j_in):]
        outs, jouts = refs[:n_out], refs[n_out:n_out + len(j_out)]
        refs = refs[n_out + len(j_out):]
        scr, jscr = refs[:n_scr], refs[n_scr:]
        per_job = _split_job_refs(jobs, jins, jouts, jscr)

        @pl.when(pl.program_id(0) == 0)
        def _():
            for job, r in zip(jobs, per_job):
                job["start"](*r)

        body(*ins, *outs, *scr)

        @pl.when(pl.program_id(0) == nsteps - 1)
        def _():
            for job, r in zip(jobs, per_job):
                job["finish"](*r)

    res = pl.pallas_call(
        wrapped if jobs else body, name=name, grid=grid,
        in_specs=list(in_specs) + [_ANY] * len(j_in), out_specs=list(out_specs) + [_ANY] * len(j_out),
        out_shape=list(out_shape) + j_out, scratch_shapes=list(scratch) + j_scr,
        compiler_params=_params(("arbitrary",) * len(grid)),
    )(*args, *j_in)
    return list(res[:n_out]), list(res[n_out:])


def _exchange(jobs, name):
    j_in = [a for job in jobs for a in job["ins"]]
    j_out = [o for job in jobs for o in job["outs"]]
    j_scr = [s for job in jobs for s in job["sems"]]

    def body(*refs):
        per_job = _split_job_refs(jobs, refs[:len(j_in)], refs[len(j_in):len(j_in) + len(j_out)],
                                  refs[len(j_in) + len(j_out):])
        for job, r in zip(jobs, per_job):
            job["start"](*r)
        for job, r in zip(jobs, per_job):
            job["finish"](*r)

    return pl.pallas_call(body, name=name, out_shape=j_out, in_specs=[_ANY] * len(j_in), out_specs=[_ANY] * len(j_out),
                          scratch_shapes=j_scr)(*j_in)


def _pair_sum(gs, r1s, c_arr, name):
    n = len(gs)

    def body(c_ref, *refs):
        for a in range(n):
            refs[2 * n + a][...] = (refs[a][...].astype(F32) + refs[n + a][...].astype(F32)).astype(refs[2 * n + a].dtype)

    def blk(g, own):
        s = g.shape[1:]
        if own:
            return pl.BlockSpec((1,) + s, lambda j, c_ref: (2 * j + c_ref[0],) + (0,) * len(s))
        return pl.BlockSpec((1,) + s, lambda j, c_ref: (j,) + (0,) * len(s))

    return pl.pallas_call(
        body, name=name,
        grid_spec=pltpu.PrefetchScalarGridSpec(
            num_scalar_prefetch=1, grid=(4,),
            in_specs=[blk(g, True) for g in gs] + [blk(g, False) for g in gs],
            out_specs=[blk(g, False) for g in gs]),
        out_shape=[jax.ShapeDtypeStruct((4,) + g.shape[1:], g.dtype) for g in gs],
        compiler_params=_params(("arbitrary",)),
    )(c_arr, *gs, *r1s)


def _chip_sum(ps, r2s, j_arr, name):
    n = len(ps)

    def body(j_ref, *refs):
        for a in range(n):
            r2 = refs[n + a]
            refs[2 * n + a][...] = ((refs[a][0].astype(F32) + r2[0].astype(F32)) + r2[1].astype(F32)) + r2[2].astype(F32)

    def own(p):
        s = p.shape[1:]
        return pl.BlockSpec((1,) + s, lambda i, j_ref: (j_ref[0],) + (0,) * len(s))

    def whole(p):
        return pl.BlockSpec(p.shape, lambda i, j_ref: (0,) * p.ndim)

    return pl.pallas_call(
        body, name=name,
        grid_spec=pltpu.PrefetchScalarGridSpec(
            num_scalar_prefetch=1, grid=(1,),
            in_specs=[own(p) for p in ps] + [whole(r) for r in r2s],
            out_specs=[pl.BlockSpec(p.shape[1:], lambda i, j_ref: (0,) * (p.ndim - 1)) for p in ps]),
        out_shape=[jax.ShapeDtypeStruct(p.shape[1:], F32) for p in ps],
        compiler_params=_params(("arbitrary",)),
    )(j_arr, *ps, *r2s)


def _adamw_math(w, g, m, v):
    m = ADAM_B1 * m + (1.0 - ADAM_B1) * g
    v = ADAM_B2 * v + (1.0 - ADAM_B2) * (g * g)
    m_hat = m / (1.0 - ADAM_B1 ** ADAM_STEP)
    v_hat = v / (1.0 - ADAM_B2 ** ADAM_STEP)
    return -ADAM_LR * (m_hat / (jnp.sqrt(v_hat) + ADAM_EPS) + ADAM_WD * w), m, v


def _adamw(items, name, steps, jobs=()):
    n = len(items)

    def body(*refs):
        for a in range(n):
            g, w, m, v = (refs[4 * a + t][...] for t in range(4))
            d, m2, v2 = _adamw_math(w, g, m, v)
            refs[4 * n + 3 * a][...] = d
            refs[4 * n + 3 * a + 1][...] = m2
            refs[4 * n + 3 * a + 2][...] = v2

    def blk(arr):
        r, c = arr.shape
        return pl.BlockSpec((r // steps, c), lambda i: (i, 0))

    flat = [t for it in items for t in it]
    return _call(body, name, (steps,), [blk(t) for t in flat], [blk(it[1]) for it in items for _ in range(3)],
                 [jax.ShapeDtypeStruct(it[1].shape, F32) for it in items for _ in range(3)], [], flat, jobs)


def _adamw_small(gathered, w, m, v, name):
    def body(gs_ref, w_ref, m_ref, v_ref, g_ref, d_ref, m2_ref, v2_ref):
        g = gs_ref[0]
        for s in range(1, N_DEV):
            g = g + gs_ref[s]
        g_ref[...] = g
        d_ref[...], m2_ref[...], v2_ref[...] = _adamw_math(w_ref[...], g, m_ref[...], v_ref[...])

    return pl.pallas_call(
        body, name=name, out_shape=[jax.ShapeDtypeStruct(w.shape, F32)] * 4,
        in_specs=[_VMEM] * 4, out_specs=[_VMEM] * 4, compiler_params=_params(),
    )(gathered, w, m, v)


SMALL = ["ln_in_g", "ln_in_b", "s5_lambda_re", "s5_lambda_im", "s5_log_dt", "s5_b_re", "s5_b_im", "s5_c_re", "s5_c_im",
         "s5_d", "s5_b_glu", "ret_gn_g", "ret_gn_b", "ln1_g", "ln1_b", "ln2_g", "ln2_b"]
LATE = ["ln_in_g", "ln_in_b", "meta_tokens"]
EARLY = [n for n in SMALL if n not in LATE] + ["s5_w_glu", "loss"]
LANE = 128


def _pack(arrs):
    parts = []
    for a in arrs:
        f = a.reshape(-1)
        parts.append(jnp.pad(f, (0, (-f.shape[0]) % LANE)))
    flat = jnp.concatenate(parts)
    rows = -(-flat.shape[0] // LANE)
    flat = jnp.pad(flat, (0, (-rows % 8) * LANE + rows * LANE - flat.shape[0]))
    return flat.reshape(-1, LANE)


def _unpack(packed, shapes):
    flat = packed.reshape(-1)
    out, off = [], 0
    for s in shapes:
        n = math.prod(s)
        out.append(flat[off:off + n].reshape(s))
        off += n + (-n) % LANE
    return out


def _rope_tables(tp):
    pos = jnp.arange(tp, dtype=F32) - float(PAD)
    inv_freq = 1.0 / (ROPE_BASE ** (jnp.arange(0, HEAD, 2, dtype=F32) / HEAD))
    ang = pos[:, None] * inv_freq[None, :]
    cos, sin = jnp.cos(ang), jnp.sin(ang)
    return jnp.concatenate([cos, cos], axis=1), jnp.concatenate([-sin, sin], axis=1)


def _decay_tables():
    log_gamma = jnp.log1p(-jnp.exp2(-5.0 - jnp.arange(RET_H, dtype=F32)))
    idx = jnp.arange(CHUNK, dtype=F32)
    diff = idx[:, None] - idx[None, :]
    dmat = jnp.where(diff[None] >= 0, jnp.exp(jnp.maximum(diff, 0.0)[None] * log_gamma[:, None, None]), 0.0)
    zeta = jnp.exp((CHUNK - 1.0 - idx)[None] * log_gamma[:, None])
    xi = jnp.exp((idx + 1.0)[None] * log_gamma[:, None])
    gam = jnp.exp(CHUNK * log_gamma)
    wide = lambda t: jnp.broadcast_to(t[:, :, None], (RET_H, CHUNK, HEAD))
    return dmat, wide(zeta), wide(xi), jnp.broadcast_to(gam[:, None, None], (RET_H, CHUNK, HEAD))


def _local_step(x2d, tgt, meta_full, w_int, w_out, w_up, w_down, w_glu, sp, distributed):
    tp = x2d.shape[0] + CHUNK
    row = lambda a: a.reshape(1, -1)
    cos2, sin2 = _rope_tables(tp)
    dmat, zeta_b, xi_b, gam_b = _decay_tables()
    li_g, li_b = row(sp["ln_in_g"]), row(sp["ln_in_b"])
    l1_g, l1_b, l2_g, l2_b = row(sp["ln1_g"]), row(sp["ln1_b"]), row(sp["ln2_g"]), row(sp["ln2_b"])
    gn_g, gn_b = row(sp["ret_gn_g"]), row(sp["ret_gn_b"])
    lre, lim = row(sp["s5_lambda_re"]), row(sp["s5_lambda_im"])
    ldt = row(jnp.repeat(sp["s5_log_dt"].reshape(-1), S5_P))
    to_t = lambda b: b.reshape(S5_G, S5_P, S5_H).transpose(2, 0, 1).reshape(S5_H, S5_N)
    bre_t, bim_t = to_t(sp["s5_b_re"]), to_t(sp["s5_b_im"])
    to_w = lambda c: jnp.tile(c.reshape(S5_W, S5_P), (1, 2))
    cre_w, cim_w = to_w(sp["s5_c_re"]), to_w(sp["s5_c_im"])
    s5_small = (lre, lim, ldt, bre_t, bim_t, cre_w, cim_w, row(sp["s5_d"]), w_glu, row(sp["s5_b_glu"]))

    jobs = (lambda *j: list(j)) if distributed else (lambda *j: [])
    c_arr = jnp.reshape(lax.axis_index("c"), (1,)).astype(jnp.int32) if distributed else None
    (xhat0, rstd0, u, q, k, v, gate), bg = _in_proj(x2d, meta_full, li_g, li_b, w_int, cos2, sin2,
                                                    jobs(_job_gather(w_out) if distributed else None))
    if distributed:
        w_out = bg[0].reshape(D_MODEL, D_MODEL)
    (ys5, xr, xi), bg = _s5_fwd(u, *s5_small, jobs=jobs(_job_gather(w_up) if distributed else None))
    if distributed:
        w_up = bg[0]
    (o, states), bg = _ret_fwd(q, k, v, dmat, zeta_b, xi_b, gam_b, jobs(_job_gather(w_down) if distributed else None))
    if distributed:
        w_down = bg[0].reshape(D_FF, D_MODEL)
    ycat, xhat1, rstd1, h1b, dr2, dffb, loss8, dl2g, dl2b, pre = _post_fwd(
        o, gate, ys5, xhat0, tgt, gn_g, gn_b, li_g, li_b, l1_g, l1_b, l2_g, l2_b, w_out, w_up, w_down)
    g_up, g_down, dh1m = _mlp_bwd(h1b, dffb, pre, w_up, w_down)
    (do, dgate, dys5, dh0r, g_out, dl1g, dl1b, dgng, dgnb), bg = _post_bwd(
        dh1m, dr2, xhat1, rstd1, ycat, o, gate, gn_g, gn_b, l1_g, w_out,
        jobs(*([_job_pair(g_up), _job_pair(g_down)] if distributed else [])))
    g_out = g_out.reshape(N_DEV, D_MODEL // N_DEV, D_MODEL)
    if distributed:
        p_up, p_down = _pair_sum([g_up, g_down], bg, c_arr, "pair_sum_mlp")
    (du, dlre, dlim, dldt, dbre_t, dbim_t, dcre, dcim, dd, dwglu, dbglu), bg = _s5_bwd(
        dys5, u, xr, xi, *s5_small,
        jobs=jobs(*([_job_chips(p_up), _job_chips(p_down), _job_pair(g_out)] if distributed else [])))
    if distributed:
        r_up, r_down = bg[0], bg[1]
        (p_out,) = _pair_sum([g_out], bg[2:], c_arr, "pair_sum_out")
    (dq, dk, dv), bg = _ret_bwd(q, k, v, do, states, cos2, sin2, dmat, zeta_b, xi_b, gam_b,
                                jobs(_job_chips(p_out) if distributed else None))
    r_out = bg[0] if distributed else None
    from_t = lambda t: t.reshape(S5_H, S5_G, S5_P).transpose(1, 2, 0)
    small = {
        "s5_lambda_re": dlre, "s5_lambda_im": dlim, "s5_log_dt": dldt[:, :S5_G],
        "s5_b_re": from_t(dbre_t), "s5_b_im": from_t(dbim_t), "s5_c_re": dcre, "s5_c_im": dcim, "s5_d": dd,
        "s5_b_glu": dbglu, "ret_gn_g": dgng, "ret_gn_b": dgnb, "ln1_g": dl1g, "ln1_b": dl1b, "ln2_g": dl2g, "ln2_b": dl2b,
        "s5_w_glu": dwglu, "loss": loss8[0:1, 0:1]}
    early_pack = _pack([small[n] for n in EARLY])
    (draw, g_int, dlig, dlib), bg = _in_bwd(du, dq, dk, dv, dgate, dh0r, xhat0, rstd0, li_g, li_b, w_int,
                                            jobs(_job_gather(early_pack) if distributed else None))
    small.update(ln_in_g=dlig, ln_in_b=dlib, meta_tokens=draw[PAD:CHUNK])
    g_int = g_int.reshape(N_DEV, PROJ_W // N_DEV, D_MODEL)
    if distributed:
        (r1_in,) = _exchange([_job_pair(g_int)], "exchange_pair_in")
        (p_in,) = _pair_sum([g_int], [r1_in], c_arr, "pair_sum_in")
        big = dict(chip_sums=[p_in, p_out, p_up, p_down], received=[None, r_out, r_up, r_down], early=bg[0])
    else:
        big = dict(partials=[g_int, g_out, g_up, g_down])
    return draw, big, small


def kernel(x, meta_tokens, ln_in_g, ln_in_b, w_in, s5_lambda_re, s5_lambda_im, s5_log_dt, s5_b_re, s5_b_im, s5_c_re, s5_c_im, s5_d, s5_w_glu, s5_b_glu, ret_gn_g, ret_gn_b, w_out, ln1_g, ln1_b, w_up, w_down, ln2_g, ln2_b, loss_target, m_meta_tokens, m_ln_in_g, m_ln_in_b, m_w_in, m_s5_lambda_re, m_s5_lambda_im, m_s5_log_dt, m_s5_b_re, m_s5_b_im, m_s5_c_re, m_s5_c_im, m_s5_d, m_s5_w_glu, m_s5_b_glu, m_ret_gn_g, m_ret_gn_b, m_w_out, m_ln1_g, m_ln1_b, m_w_up, m_w_down, m_ln2_g, m_ln2_b, v_meta_tokens, v_ln_in_g, v_ln_in_b, v_w_in, v_s5_lambda_re, v_s5_lambda_im, v_s5_log_dt, v_s5_b_re, v_s5_b_im, v_s5_c_re, v_s5_c_im, v_s5_d, v_s5_w_glu, v_s5_b_glu, v_ret_gn_g, v_ret_gn_b, v_w_out, v_ln1_g, v_ln1_b, v_w_up, v_w_down, v_ln2_g, v_ln2_b):
    args = dict(locals())
    names = ["meta_tokens", "ln_in_g", "ln_in_b", "w_in", "s5_lambda_re", "s5_lambda_im", "s5_log_dt", "s5_b_re", "s5_b_im",
             "s5_c_re", "s5_c_im", "s5_d", "s5_w_glu", "s5_b_glu", "ret_gn_g", "ret_gn_b", "w_out", "ln1_g", "ln1_b",
             "w_up", "w_down", "ln2_g", "ln2_b"]
    ax, ay, ac = _place()
    me = 4 * ax + 2 * ay + ac

    a_int, a_glu, a_meta = _exchange([_job_gather(w_in[0].T.astype(MM)), _job_gather(s5_w_glu[0].astype(MM)),
                                      _job_gather(meta_tokens)], "gather_first")
    w_int = a_int.reshape(PROJ_W, D_MODEL)
    w_glu_f = a_glu.reshape(S5_W, S5_W)
    meta_full = a_meta.transpose(1, 0, 2).reshape(N_META, D_MODEL)

    sp = {n: args[n] for n in SMALL}
    draw, big, small = _local_step(x[0], loss_target[0], meta_full, w_int, w_out[0].astype(MM), w_up[0].astype(MM),
                                   w_down[0].astype(MM), w_glu_f, sp, True)

    j_arr = jnp.reshape(2 * ax + ay, (1,)).astype(jnp.int32)
    two_d = lambda a: a.reshape(a.shape[-2:])
    item = lambda n, g: tuple(two_d(t) for t in (g, args[n], args["m_" + n], args["v_" + n]))
    g_out, g_up, g_down = _chip_sum(big["chip_sums"][1:], big["received"][1:], j_arr, "chip_sum_mlp")
    shard_grads = {"w_out": g_out[None], "w_up": g_up[None], "w_down": g_down[None]}
    late_pack = _pack([small[n] for n in LATE])
    res, (r_in, late_all) = _adamw([item(n, shard_grads[n]) for n in ("w_out", "w_up", "w_down")], "adamw_mlp", 8,
                                   [_job_chips(big["chip_sums"][0]), _job_gather(late_pack)])
    (g_int,) = _chip_sum(big["chip_sums"][:1], [r_in], j_arr, "chip_sum_in")
    shard_grads["w_in"] = g_int.T[None]

    def small_update(order, shapes, gathered, placeholders, name):
        packs = [_pack([jnp.zeros(s, F32) if n in placeholders else args[p + n] for n, s in zip(order, shapes)])
                 for p in ("", "m_", "v_")]
        outs = _adamw_small(gathered, *packs, name)
        return [dict(zip(order, _unpack(o, shapes))) for o in outs]

    early_shapes = [args[n].shape for n in EARLY[:-2]] + [(S5_W, S5_W), (1,)]
    late_shapes = [args["ln_in_g"].shape, args["ln_in_b"].shape, (N_META, D_MODEL)]
    parts = [small_update(EARLY, early_shapes, big["early"], ("s5_w_glu", "loss"), "adamw_small_early"),
             small_update(LATE, late_shapes, late_all, ("meta_tokens",), "adamw_small_late")]
    g_small, d_small, m_small, v_small = ({**parts[0][t], **parts[1][t]} for t in range(4))
    loss = g_small["loss"].reshape(())

    shard_grads["meta_tokens"] = lax.dynamic_slice(g_small["meta_tokens"], (0, me * (D_MODEL // N_DEV)),
                                                   (N_META, D_MODEL // N_DEV))
    shard_grads["s5_w_glu"] = lax.dynamic_slice(g_small["s5_w_glu"], (me * (S5_W // N_DEV), 0),
                                                (S5_W // N_DEV, S5_W))[None]
    res_in, _ = _adamw([item("w_in", shard_grads["w_in"])], "adamw_in", 8)
    res2, _ = _adamw([item(n, shard_grads[n]) for n in ("meta_tokens", "s5_w_glu")], "adamw_shard_small", 1)
    upd = {"w_in": [r.reshape(args["w_in"].shape) for r in res_in]}
    for idx, n in enumerate(("w_out", "w_up", "w_down")):
        upd[n] = [r.reshape(args[n].shape) for r in res[3 * idx:3 * idx + 3]]
    for idx, n in enumerate(("meta_tokens", "s5_w_glu")):
        upd[n] = [r.reshape(args[n].shape) for r in res2[3 * idx:3 * idx + 3]]

    grads, deltas, new_m, new_v = [], [], [], []
    for n in names:
        if n in upd:
            grads.append(shard_grads[n].reshape(args[n].shape))
            d, m2, v2 = upd[n]
        else:
            grads.append(g_small[n])
            d, m2, v2 = d_small[n], m_small[n], v_small[n]
        deltas.append(d)
        new_m.append(m2)
        new_v.append(v2)
    grad_x = draw[CHUNK:][None]
    return (loss, grad_x, *grads, *deltas, *new_m, *new_v)
```

```python
import math

import jax
import jax.numpy as jnp
from jax import lax
from jax.experimental import pallas as pl
from jax.experimental.pallas import tpu as pltpu

F32 = jnp.float32
MM = jnp.bfloat16

D_MODEL = 1024
N_META = 16
CHUNK = 128
PAD = CHUNK - N_META
S5_W, S5_G, S5_H, S5_P = 256, 16, 16, 64
S5_N = S5_G * S5_P
RET_W, RET_H, HEAD = 768, 6, 128
D_FF = 4096
PROJ_W = S5_W + 4 * RET_W
N_DEV = 8
FF_BLK = D_FF // N_DEV
ROW_BLK = 384
MLP_ROWS = 1408
PROJ_ROWS = 704
ALPHA = 2.0 ** 0.25
LN_EPS = 1e-5
GN_EPS = 1e-5
ROPE_BASE = 10000.0
GELU_C = math.sqrt(2.0 / math.pi)
GELU_A = 0.044715
ADAM_LR, ADAM_B1, ADAM_B2, ADAM_EPS, ADAM_WD, ADAM_STEP = 0.001, 0.9, 0.999, 1e-08, 0.01, 10
VMEM_LIMIT = 60 * 1024 * 1024

_VMEM = pl.BlockSpec(memory_space=pltpu.VMEM)
_ANY = pl.BlockSpec(memory_space=pl.ANY)
_MESH = pl.DeviceIdType.MESH


def _params(sem=None):
    return pltpu.CompilerParams(dimension_semantics=sem, vmem_limit_bytes=VMEM_LIMIT)


def _dot(a, b):
    return jnp.dot(a.astype(MM), b.astype(MM), preferred_element_type=F32)


def _dot_nt(a, b):
    return lax.dot_general(a.astype(MM), b.astype(MM), (((1,), (1,)), ((), ())), preferred_element_type=F32)


def _dot_tn(a, b):
    return lax.dot_general(a.astype(MM), b.astype(MM), (((0,), (0,)), ((), ())), preferred_element_type=F32)


def _split3(a):
    hi = a.astype(jnp.bfloat16)
    r1 = a - hi.astype(F32)
    mid = r1.astype(jnp.bfloat16)
    lo = (r1 - mid.astype(F32)).astype(jnp.bfloat16)
    return hi, mid, lo


def _dot_sel_rhs(a, sel):
    s = sel.astype(jnp.bfloat16)
    return sum(jnp.dot(p, s, preferred_element_type=F32) for p in _split3(a))


def _dot_sel_lhs(sel, b):
    s = sel.astype(jnp.bfloat16)
    return sum(jnp.dot(s, p, preferred_element_type=F32) for p in _split3(b))


def _ln_fwd(r, eps):
    mu = jnp.mean(r, axis=-1, keepdims=True)
    xc = r - mu
    var = jnp.mean(xc * xc, axis=-1, keepdims=True)
    rstd = lax.rsqrt(var + eps)
    return xc * rstd, rstd


def _ln_bwd(dxhat, xhat, rstd):
    m1 = jnp.mean(dxhat, axis=-1, keepdims=True)
    m2 = jnp.mean(dxhat * xhat, axis=-1, keepdims=True)
    return rstd * (dxhat - m1 - xhat * m2)


def _colsum(a):
    return jnp.sum(a, axis=0, keepdims=True)


def _shift3(n_in):
    return [pl.BlockSpec((CHUNK, D_MODEL), (lambda i, j=j: (jnp.clip(3 * i - 1 + j, 0, n_in - 1), 0))) for j in range(3)]


def _ln_in(x2d, meta_full, jobs=()):
    seq = x2d.shape[0]
    tp = seq + CHUNK
    R = ROW_BLK

    def body(xa, xb, xc, meta_ref, xhat_ref, rstd_ref, raw_ref):
        raw_ref[0:CHUNK, :] = xa[...]
        raw_ref[CHUNK:2 * CHUNK, :] = xb[...]
        raw_ref[2 * CHUNK:3 * CHUNK, :] = xc[...]

        @pl.when(pl.program_id(0) == 0)
        def _():
            raw_ref[0:PAD, :] = jnp.zeros((PAD, D_MODEL), F32)
            raw_ref[PAD:CHUNK, :] = meta_ref[...]

        xhat_ref[...], rstd_ref[...] = _ln_fwd(raw_ref[...], LN_EPS)

    row = lambda w: pl.BlockSpec((R, w), lambda i: (i, 0))
    return _call(
        body, "ln_in", (tp // R,),
        _shift3(seq // CHUNK) + [pl.BlockSpec((N_META, D_MODEL), lambda i: (0, 0))],
        [row(D_MODEL), row(1)], [jax.ShapeDtypeStruct((tp, D_MODEL), F32), jax.ShapeDtypeStruct((tp, 1), F32)],
        [pltpu.VMEM((R, D_MODEL), F32)], (x2d, x2d, x2d, meta_full), jobs)


def _in_proj(xhat0, ln_g, ln_b, w_int, cos2, sin2, jobs=()):
    tp = xhat0.shape[0]
    R = PROJ_ROWS if tp % PROJ_ROWS == 0 else ROW_BLK

    def body(xh_ref, g_ref, b_ref, w_ref, cos_ref, sin_ref, u_ref, q_ref, k_ref, v_ref, gate_ref):
        hb = (xh_ref[...] * g_ref[...] + b_ref[...]).astype(MM)
        valid = (pl.program_id(0) * R + lax.broadcasted_iota(jnp.int32, (R, 1), 0)) >= PAD

        def seg(lo, hi):
            return jnp.where(valid, _dot_nt(hb, w_ref[lo:hi, :]), 0.0)

        u_ref[...] = seg(0, S5_W)
        cos = cos_ref[...]
        sin = sin_ref[...]
        q = seg(S5_W, S5_W + RET_W)
        k = seg(S5_W + RET_W, S5_W + 2 * RET_W)
        for h in range(RET_H):
            sl = slice(h * HEAD, (h + 1) * HEAD)
            qh = q[:, sl]
            kh = k[:, sl]
            q_ref[:, sl] = (qh * cos + pltpu.roll(qh, HEAD // 2, 1) * sin).astype(q_ref.dtype)
            k_ref[:, sl] = ((kh * cos + pltpu.roll(kh, HEAD // 2, 1) * sin) * (HEAD ** -0.5)).astype(k_ref.dtype)
        v_ref[...] = seg(S5_W + 2 * RET_W, S5_W + 3 * RET_W).astype(v_ref.dtype)
        gate_ref[...] = seg(S5_W + 3 * RET_W, PROJ_W)

    def rows(w, dt):
        return pl.BlockSpec((R, w), lambda i: (i, 0)), jax.ShapeDtypeStruct((tp, w), dt)

    outs = [rows(S5_W, F32), rows(RET_W, MM), rows(RET_W, MM), rows(RET_W, MM), rows(RET_W, F32)]
    full = lambda s: pl.BlockSpec(s, lambda i: (0,) * len(s))
    return _call(
        body, "in_proj", (tp // R,),
        [pl.BlockSpec((R, D_MODEL), lambda i: (i, 0)), full((1, D_MODEL)), full((1, D_MODEL)), _VMEM,
         pl.BlockSpec((R, HEAD), lambda i: (i, 0)), pl.BlockSpec((R, HEAD), lambda i: (i, 0))],
        [o[0] for o in outs], [o[1] for o in outs], [], (xhat0, ln_g, ln_b, w_int, cos2, sin2), jobs)


def _s5_disc(lre, lim, ldt, bre_t, bim_t):
    dt = jnp.exp(ldt)
    mag = jnp.exp(lre * dt)
    ang = lim * dt
    lbr = mag * jnp.cos(ang)
    lbi = mag * jnp.sin(ang)
    den = lre * lre + lim * lim
    nr = lbr - 1.0
    qr = (nr * lre + lbi * lim) / den
    qi = (lbi * lre - nr * lim) / den
    return lbr, lbi, qr * bre_t - qi * bim_t, qr * bim_t + qi * bre_t


def _s5_tables(lbr, lbi, reverse):
    if reverse:
        lbi = -lbi
    pw = [(lbr, lbi)]
    for _ in range(7):
        r, i = pw[-1]
        pw.append((r * lbr - i * lbi, r * lbi + i * lbr))
    row = lax.broadcasted_iota(jnp.int32, (8, S5_N), 0)
    tabs = []
    for k in range(3):
        sh = 2 ** k
        mask = (row < 8 - sh) if reverse else (row >= sh)
        ar, ai = pw[sh - 1]
        tabs.append((jnp.where(mask, ar, 0.0), jnp.where(mask, ai, 0.0)))
    pr = jnp.zeros((8, S5_N), F32)
    pi = jnp.zeros((8, S5_N), F32)
    for i in range(8):
        ar, ai = pw[7 - i] if reverse else pw[i]
        pr = jnp.where(row == i, ar, pr)
        pi = jnp.where(row == i, ai, pi)
    tabs.append((pr, pi))
    return tabs


def _store_tables(tab_ref, tabs):
    for k, (r, i) in enumerate(tabs):
        tab_ref[2 * k] = r
        tab_ref[2 * k + 1] = i


def _bd_mask():
    r = lax.broadcasted_iota(jnp.int32, (S5_W, S5_N), 0)
    c = lax.broadcasted_iota(jnp.int32, (S5_W, S5_N), 1)
    return jnp.right_shift(r, 4) == jnp.right_shift(c, 6)


def _s5_block_diag(bbr_t, bbi_t, cre_w, cim_w):
    mask = _bd_mask()
    bd = lambda t: jnp.where(mask, t, 0.0)
    return (bd(jnp.tile(bbr_t, (S5_G, 1))), bd(jnp.tile(bbi_t, (S5_G, 1))),
            bd(jnp.tile(cre_w, (1, S5_N // HEAD))), bd(jnp.tile(cim_w, (1, S5_N // HEAD))))


def _scan8(xr, xi, tab_ref, lanes, reverse):
    for k in range(3):
        sh = (8 - 2 ** k) if reverse else 2 ** k
        sr = pltpu.roll(xr, sh, 0)
        si = pltpu.roll(xi, sh, 0)
        mr = tab_ref[2 * k, :, lanes]
        mi = tab_ref[2 * k + 1, :, lanes]
        xr, xi = xr + (mr * sr - mi * si), xi + (mr * si + mi * sr)
    return xr, xi


S5_LANES = 256


def _gelu(y):
    t = jnp.tanh(GELU_C * (y + GELU_A * y * y * y))
    return 0.5 * y * (1.0 + t), t


def _s5_fwd(u, lre, lim, ldt, bre_t, bim_t, cre_w, cim_w, d_row, w_glu, b_glu, jobs=()):
    tp = u.shape[0]
    R = ROW_BLK

    def body(u_ref, lre_ref, lim_ref, ldt_ref, bre_ref, bim_ref, cre_ref, cim_ref, d_ref, wg_ref, bg_ref,
             y_ref, xr_ref, xi_ref, bbd_r, bbd_i, cbd_r, cbd_i, tab_ref, car_r, car_i):
        @pl.when(pl.program_id(0) == 0)
        def _():
            lbr, lbi, bbr, bbi = _s5_disc(lre_ref[...], lim_ref[...], ldt_ref[...], bre_ref[...], bim_ref[...])
            br, bi, cr, ci = _s5_block_diag(bbr, bbi, cre_ref[...], cim_ref[...])
            bbd_r[...] = br.astype(MM)
            bbd_i[...] = bi.astype(MM)
            cbd_r[...] = cr.astype(MM)
            cbd_i[...] = ci.astype(MM)
            _store_tables(tab_ref, _s5_tables(lbr, lbi, False))
            car_r[...] = jnp.zeros_like(car_r)
            car_i[...] = jnp.zeros_like(car_i)

        u = u_ref[...]
        ub = u.astype(MM)
        xr_ref[...] = jnp.dot(ub, bbd_r[...], preferred_element_type=F32)
        xi_ref[...] = jnp.dot(ub, bbd_i[...], preferred_element_type=F32)
        for j in range(S5_N // S5_LANES):
            lanes = pl.ds(j * S5_LANES, S5_LANES)
            pr = tab_ref[6, :, lanes]
            pi = tab_ref[7, :, lanes]

            def step(g, carry):
                cr, ci = carry
                rows = pl.ds(pl.multiple_of(g * 8, 8), 8)
                xr, xi = _scan8(xr_ref[rows, lanes], xi_ref[rows, lanes], tab_ref, lanes, False)
                br = jnp.broadcast_to(cr[7:8, :], cr.shape)
                bi = jnp.broadcast_to(ci[7:8, :], ci.shape)
                xr = xr + (pr * br - pi * bi)
                xi = xi + (pr * bi + pi * br)
                xr_ref[rows, lanes] = xr
                xi_ref[rows, lanes] = xi
                return xr, xi

            cr, ci = lax.fori_loop(0, R // 8, step, (car_r[:, lanes], car_i[:, lanes]), unroll=2)
            car_r[:, lanes] = cr
            car_i[:, lanes] = ci
        y = _dot_nt(xr_ref[...], cbd_r[...]) - _dot_nt(xi_ref[...], cbd_i[...]) + d_ref[...] * u
        yg, _ = _gelu(y)
        z = _dot(yg, wg_ref[...]) + bg_ref[...]
        y_ref[...] = yg * jax.nn.sigmoid(z)

    full = lambda a: pl.BlockSpec(a.shape, lambda i: (0,) * a.ndim)
    small = [lre, lim, ldt, bre_t, bim_t, cre_w, cim_w, d_row, w_glu, b_glu]
    return _call(
        body, "s5_fwd", (tp // R,),
        [pl.BlockSpec((R, S5_W), lambda i: (i, 0))] + [full(a) for a in small],
        [pl.BlockSpec((R, S5_W), lambda i: (i, 0)), pl.BlockSpec((R, S5_N), lambda i: (i, 0)),
         pl.BlockSpec((R, S5_N), lambda i: (i, 0))],
        [jax.ShapeDtypeStruct((tp, S5_W), F32), jax.ShapeDtypeStruct((tp, S5_N), F32),
         jax.ShapeDtypeStruct((tp, S5_N), F32)],
        [pltpu.VMEM((S5_W, S5_N), MM)] * 4 + [pltpu.VMEM((8, 8, S5_N), F32), pltpu.VMEM((8, S5_N), F32),
                                              pltpu.VMEM((8, S5_N), F32)],
        (u, *small), jobs)


def _s5_bwd(dy_out, u, xr, xi, lre, lim, ldt, bre_t, bim_t, cre_w, cim_w, d_row, w_glu, b_glu, jobs=()):
    tp = u.shape[0]
    R = ROW_BLK
    nb = tp // R

    def body(dyo_ref, u_ref, xr_ref, xi_ref, xpr_ref, xpi_ref,
             lre_ref, lim_ref, ldt_ref, bre_ref, bim_ref, cre_ref, cim_ref, d_ref, wg_ref, bg_ref,
             du_ref, dlre_ref, dlim_ref, dldt_ref, dbre_ref, dbim_ref, dcre_ref, dcim_ref, dd_ref, dwg_ref, dbg_ref,
             bbd_r, bbd_i, cbd_r, cbd_i, tab_ref, car_r, car_i, gr_ref, gi_ref, xer_ref, xei_ref,
             abr, abi, acr, aci, adr, adi):
        i = pl.program_id(0)

        @pl.when(i == 0)
        def _():
            lbr, lbi, bbr, bbi = _s5_disc(lre_ref[...], lim_ref[...], ldt_ref[...], bre_ref[...], bim_ref[...])
            br, bi, cr, ci = _s5_block_diag(bbr, bbi, cre_ref[...], cim_ref[...])
            bbd_r[...] = br.astype(MM)
            bbd_i[...] = bi.astype(MM)
            cbd_r[...] = cr.astype(MM)
            cbd_i[...] = ci.astype(MM)
            _store_tables(tab_ref, _s5_tables(lbr, lbi, True))
            for ref in (car_r, car_i, abr, abi, acr, aci, adr, adi, dd_ref, dwg_ref, dbg_ref):
                ref[...] = jnp.zeros_like(ref)

        u = u_ref[...]
        xrv = xr_ref[...]
        xiv = xi_ref[...]
        y = _dot_nt(xrv, cbd_r[...]) - _dot_nt(xiv, cbd_i[...]) + d_ref[...] * u
        yg, t = _gelu(y)
        z = _dot(yg, wg_ref[...]) + bg_ref[...]
        s = jax.nn.sigmoid(z)
        dout = dyo_ref[...]
        dz = dout * yg * s * (1.0 - s)
        dyg = dout * s + _dot_nt(dz, wg_ref[...])
        dwg_ref[...] += _dot_tn(yg, dz)
        dbg_ref[...] += _colsum(dz)
        dy = dyg * (0.5 * (1.0 + t) + 0.5 * y * (1.0 - t * t) * GELU_C * (1.0 + 3.0 * GELU_A * y * y))
        dd_ref[...] += _colsum(dy * u)
        acr[...] += _dot_tn(dy, xrv)
        aci[...] -= _dot_tn(dy, xiv)
        gr_ref[...] = _dot(dy, cbd_r[...])
        gi_ref[...] = -_dot(dy, cbd_i[...])
        has_prev = (i < nb - 1).astype(F32)
        xer_ref[0:8, :] = xpr_ref[...] * has_prev
        xei_ref[0:8, :] = xpi_ref[...] * has_prev
        xer_ref[8:R + 8, :] = xrv
        xei_ref[8:R + 8, :] = xiv
        row = lax.broadcasted_iota(jnp.int32, (8, S5_LANES), 0)
        for j in range(S5_N // S5_LANES):
            lanes = pl.ds(j * S5_LANES, S5_LANES)
            pr = tab_ref[6, :, lanes]
            pi = tab_ref[7, :, lanes]

            def step(n, carry):
                cr, ci, sar, sai = carry
                g = R // 8 - 1 - n
                r0 = pl.multiple_of(g * 8, 8)
                rows = pl.ds(r0, 8)
                gr, gi = _scan8(gr_ref[rows, lanes], gi_ref[rows, lanes], tab_ref, lanes, True)
                br = jnp.broadcast_to(cr[0:1, :], cr.shape)
                bi = jnp.broadcast_to(ci[0:1, :], ci.shape)
                gr = gr + (pr * br - pi * bi)
                gi = gi + (pr * bi + pi * br)
                gr_ref[rows, lanes] = gr
                gi_ref[rows, lanes] = gi
                last = row == 7
                xpr = pltpu.roll(jnp.where(last, xer_ref[rows, lanes], xer_ref[pl.ds(r0 + 8, 8), lanes]), 1, 0)
                xpi = pltpu.roll(jnp.where(last, xei_ref[rows, lanes], xei_ref[pl.ds(r0 + 8, 8), lanes]), 1, 0)
                return gr, gi, sar + (gr * xpr + gi * xpi), sai + (gi * xpr - gr * xpi)

            cr, ci, sar, sai = lax.fori_loop(
                0, R // 8, step, (car_r[:, lanes], car_i[:, lanes], adr[:, lanes], adi[:, lanes]), unroll=2)
            car_r[:, lanes] = cr
            car_i[:, lanes] = ci
            adr[:, lanes] = sar
            adi[:, lanes] = sai
        grv = gr_ref[...]
        giv = gi_ref[...]
        du_ref[...] = (dy * d_ref[...] + _dot_nt(grv, bbd_r[...]) + _dot_nt(giv, bbd_i[...])).astype(du_ref.dtype)
        abr[...] += _dot_tn(u, grv)
        abi[...] += _dot_tn(u, giv)

        @pl.when(i == nb - 1)
        def _():
            mask = _bd_mask()
            r16 = lax.broadcasted_iota(jnp.int32, (S5_H, S5_W), 1)
            h16 = lax.broadcasted_iota(jnp.int32, (S5_H, S5_W), 0)
            fold_b = jnp.bitwise_and(r16, S5_H - 1) == h16
            c64 = lax.broadcasted_iota(jnp.int32, (S5_N, S5_P), 0)
            p64 = lax.broadcasted_iota(jnp.int32, (S5_N, S5_P), 1)
            fold_c = jnp.bitwise_and(c64, S5_P - 1) == p64
            dbbr = _dot_sel_lhs(fold_b, jnp.where(mask, abr[...], 0.0))
            dbbi = _dot_sel_lhs(fold_b, jnp.where(mask, abi[...], 0.0))
            dcre_ref[...] = _dot_sel_rhs(jnp.where(mask, acr[...], 0.0), fold_c)
            dcim_ref[...] = _dot_sel_rhs(jnp.where(mask, aci[...], 0.0), fold_c)
            dlbr = _colsum(adr[...])
            dlbi = _colsum(adi[...])
            _, vjp = jax.vjp(_s5_disc, lre_ref[...], lim_ref[...], ldt_ref[...], bre_ref[...], bim_ref[...])
            dlre, dlim, dldt, dbre, dbim = vjp((dlbr, dlbi, dbbr, dbbi))
            dlre_ref[...] = dlre
            dlim_ref[...] = dlim
            dbre_ref[...] = dbre
            dbim_ref[...] = dbim
            gsel = jnp.right_shift(lax.broadcasted_iota(jnp.int32, (S5_N, HEAD), 0), 6) == \
                lax.broadcasted_iota(jnp.int32, (S5_N, HEAD), 1)
            dldt_ref[...] = _dot_sel_rhs(dldt, gsel)

    full = lambda a: pl.BlockSpec(a.shape, lambda i: (0,) * a.ndim)
    rev = lambda w: pl.BlockSpec((R, w), lambda i: (nb - 1 - i, 0))
    prev8 = pl.BlockSpec((8, S5_N), lambda i: (jnp.maximum((nb - 1 - i) * (R // 8) - 1, 0), 0))
    small = [lre, lim, ldt, bre_t, bim_t, cre_w, cim_w, d_row, w_glu, b_glu]
    outs = [((tp, S5_W), rev(S5_W))] + [
        (s, pl.BlockSpec(s, lambda i: (0, 0))) for s in
        [(1, S5_N), (1, S5_N), (1, HEAD), (S5_H, S5_N), (S5_H, S5_N), (S5_W, S5_P), (S5_W, S5_P),
         (1, S5_W), (S5_W, S5_W), (1, S5_W)]]
    return _call(
        body, "s5_bwd", (nb,),
        [rev(S5_W), rev(S5_W), rev(S5_N), rev(S5_N), prev8, prev8] + [full(a) for a in small],
        [o[1] for o in outs], [jax.ShapeDtypeStruct(o[0], MM if n == 0 else F32) for n, o in enumerate(outs)],
        [pltpu.VMEM((S5_W, S5_N), MM)] * 4 + [
            pltpu.VMEM((8, 8, S5_N), F32), pltpu.VMEM((8, S5_N), F32), pltpu.VMEM((8, S5_N), F32),
            pltpu.VMEM((R, S5_N), F32), pltpu.VMEM((R, S5_N), F32),
            pltpu.VMEM((R + 8, S5_N), F32), pltpu.VMEM((R + 8, S5_N), F32)] + [pltpu.VMEM((S5_W, S5_N), F32)] * 4 + [
            pltpu.VMEM((8, S5_N), F32), pltpu.VMEM((8, S5_N), F32)],
        (dy_out, u, xr, xi, xr, xi, *small), jobs)


def _ret_fwd(q, k, v, dmat, zeta_b, xi_b, gam_b, jobs=()):
    tp = q.shape[0]
    C = dmat.shape[1]
    nc = tp // C

    def body(q_ref, k_ref, v_ref, dm_ref, ze_ref, xi_ref, ga_ref, o_ref, st_ref, s_ref):
        @pl.when(pl.program_id(0) == 0)
        def _():
            s_ref[...] = jnp.zeros_like(s_ref)

        for h in range(RET_H):
            sl = slice(h * HEAD, (h + 1) * HEAD)
            qh, kh, vh = q_ref[:, sl], k_ref[:, sl], v_ref[:, sl]
            sh = s_ref[h]
            st_ref[0, sl, :] = sh
            scores = _dot_nt(qh, kh) * dm_ref[h]
            o_ref[:, sl] = _dot(scores, vh) + _dot(qh, sh) * xi_ref[h]
            s_ref[h] = ga_ref[h] * sh + _dot_tn(kh.astype(F32) * ze_ref[h], vh)

    blk = pl.BlockSpec((C, RET_W), lambda c: (c, 0))
    cst = lambda a: pl.BlockSpec(a.shape, lambda c: (0, 0, 0))
    return _call(
        body, "ret_fwd", (nc,), [blk, blk, blk, cst(dmat), cst(zeta_b), cst(xi_b), cst(gam_b)],
        [blk, pl.BlockSpec((1, RET_W, HEAD), lambda c: (c, 0, 0))],
        [jax.ShapeDtypeStruct((tp, RET_W), F32), jax.ShapeDtypeStruct((nc, RET_W, HEAD), F32)],
        [pltpu.VMEM((RET_H, HEAD, HEAD), F32)], (q, k, v, dmat, zeta_b, xi_b, gam_b), jobs)


def _ret_bwd(q, k, v, do, states, cos2, sin2, dmat, zeta_b, xi_b, gam_b, jobs=()):
    tp = q.shape[0]
    C = dmat.shape[1]
    nc = tp // C

    def body(q_ref, k_ref, v_ref, do_ref, st_ref, cos_ref, sin_ref, dm_ref, ze_ref, xi_ref, ga_ref,
             dq_ref, dk_ref, dv_ref, ds_ref):
        @pl.when(pl.program_id(0) == 0)
        def _():
            ds_ref[...] = jnp.zeros_like(ds_ref)

        cos = cos_ref[...]
        sin = sin_ref[...]
        for h in range(RET_H):
            sl = slice(h * HEAD, (h + 1) * HEAD)
            qh, kh, vh = q_ref[:, sl], k_ref[:, sl], v_ref[:, sl]
            dmh = dm_ref[h]
            sh = st_ref[0, sl, :]
            dsn = ds_ref[h]
            doh = do_ref[:, sl]
            dox = doh * xi_ref[h]
            a = _dot_nt(qh, kh) * dmh
            dqk = _dot_nt(doh, vh) * dmh
            kz = kh.astype(F32) * ze_ref[h]
            dv_ref[:, sl] = (_dot_tn(a, doh) + _dot(kz, dsn)).astype(dv_ref.dtype)
            dqr = _dot(dqk, kh) + _dot_nt(dox, sh)
            dkr = _dot_tn(dqk, qh) + ze_ref[h] * _dot_nt(vh, dsn)
            ds_ref[h] = ga_ref[h] * dsn + _dot_tn(qh, dox)
            dq_ref[:, sl] = (dqr * cos - pltpu.roll(dqr, HEAD // 2, 1) * sin).astype(dq_ref.dtype)
            dk_ref[:, sl] = ((dkr * cos - pltpu.roll(dkr, HEAD // 2, 1) * sin) * (HEAD ** -0.5)).astype(dk_ref.dtype)

    blk = pl.BlockSpec((C, RET_W), lambda c: (nc - 1 - c, 0))
    tab = pl.BlockSpec((C, HEAD), lambda c: (nc - 1 - c, 0))
    cst = lambda a: pl.BlockSpec(a.shape, lambda c: (0, 0, 0))
    return _call(
        body, "ret_bwd", (nc,),
        [blk, blk, blk, blk, pl.BlockSpec((1, RET_W, HEAD), lambda c: (nc - 1 - c, 0, 0)), tab, tab,
         cst(dmat), cst(zeta_b), cst(xi_b), cst(gam_b)],
        [blk, blk, blk], [jax.ShapeDtypeStruct((tp, RET_W), MM)] * 3, [pltpu.VMEM((RET_H, HEAD, HEAD), F32)],
        (q, k, v, do, states, cos2, sin2, dmat, zeta_b, xi_b, gam_b), jobs)


def _gn_gate(o, gate, gn_g, gn_b):
    xhat, rstd = _ln_fwd(o, GN_EPS)
    on = xhat * gn_g + gn_b
    s = jax.nn.sigmoid(gate)
    return gate * s * on, xhat, rstd, on, s


def _post_fwd(o, gate, ys5, xhat0, tgt, gn_g, gn_b, li_g, li_b, l1_g, l1_b, l2_g, l2_b, w_out, w_up, w_down):
    tp = o.shape[0]
    seq = tgt.shape[0]
    R = ROW_BLK

    def body(o_ref, g_ref, ys_ref, xh0_ref, ta, tb, tc, gng, gnb, lig, lib, l1g, l1b, l2g, l2b, wo_ref, wu_ref, wd_ref,
             ycat_ref, xh1_ref, rstd1_ref, h1b_ref, dr2_ref, dffb_ref, loss_ref, dl2g_ref, dl2b_ref, pre_ref, tgt_ref):
        i = pl.program_id(0)

        @pl.when(i == 0)
        def _():
            for ref in (loss_ref, dl2g_ref, dl2b_ref):
                ref[...] = jnp.zeros_like(ref)

        tgt_ref[0:CHUNK, :] = ta[...]
        tgt_ref[CHUNK:2 * CHUNK, :] = tb[...]
        tgt_ref[2 * CHUNK:3 * CHUNK, :] = tc[...]
        ycat_ref[:, 0:S5_W] = ys_ref[...].astype(ycat_ref.dtype)
        for h in range(RET_H):
            sl = slice(h * HEAD, (h + 1) * HEAD)
            yret = _gn_gate(o_ref[:, sl], g_ref[:, sl], gng[:, sl], gnb[:, sl])[0]
            ycat_ref[:, S5_W + h * HEAD:S5_W + (h + 1) * HEAD] = yret.astype(ycat_ref.dtype)
        mixed = _dot(ycat_ref[...], wo_ref[...])
        h0 = xh0_ref[...] * lig[...] + lib[...]
        xh1, rstd1 = _ln_fwd(ALPHA * h0 + mixed, LN_EPS)
        xh1_ref[...] = xh1
        rstd1_ref[...] = rstd1
        h1 = xh1 * l1g[...] + l1b[...]
        h1b = h1.astype(MM)
        h1b_ref[...] = h1b
        ff = jnp.zeros((R, D_MODEL), F32)
        for d in range(N_DEV):
            pre = jnp.maximum(_dot(h1b, wu_ref[d]), 0.0)
            pre_ref[:, d * FF_BLK:(d + 1) * FF_BLK] = pre
            ff = ff + _dot(pre * pre, wd_ref[d * FF_BLK:(d + 1) * FF_BLK, :])
        xh2, rstd2 = _ln_fwd(ALPHA * h1 + ff, LN_EPS)
        h2 = xh2 * l2g[...] + l2b[...]
        valid = (i * R + lax.broadcasted_iota(jnp.int32, (R, 1), 0)) >= CHUNK
        err = jnp.where(valid, h2 - tgt_ref[...], 0.0)
        loss_ref[...] += 0.5 * jnp.sum(err * err) / D_MODEL
        dh2 = err * (1.0 / D_MODEL)
        dl2g_ref[...] += _colsum(dh2 * xh2)
        dl2b_ref[...] += _colsum(dh2)
        dr2 = _ln_bwd(dh2 * l2g[...], xh2, rstd2)
        dr2_ref[...] = dr2
        dffb_ref[...] = dr2.astype(MM)

    row = lambda w: pl.BlockSpec((R, w), lambda i: (i, 0))
    full = lambda a: pl.BlockSpec(a.shape, lambda i: (0,) * a.ndim)
    vecs = [gn_g, gn_b, li_g, li_b, l1_g, l1_b, l2_g, l2_b]
    acc = lambda s: (pl.BlockSpec(s, lambda i: (0, 0)), jax.ShapeDtypeStruct(s, F32))
    outs = [(row(D_MODEL), jax.ShapeDtypeStruct((tp, D_MODEL), MM)),
            (row(D_MODEL), jax.ShapeDtypeStruct((tp, D_MODEL), F32)),
            (row(1), jax.ShapeDtypeStruct((tp, 1), F32)),
            (row(D_MODEL), jax.ShapeDtypeStruct((tp, D_MODEL), MM)),
            (row(D_MODEL), jax.ShapeDtypeStruct((tp, D_MODEL), F32)),
            (row(D_MODEL), jax.ShapeDtypeStruct((tp, D_MODEL), MM)),
            acc((8, HEAD)), acc((1, D_MODEL)), acc((1, D_MODEL)),
            (row(D_FF), jax.ShapeDtypeStruct((tp, D_FF), F32))]
    return pl.pallas_call(
        body, name="post_fwd", grid=(tp // R,),
        in_specs=[row(RET_W), row(RET_W), row(S5_W), row(D_MODEL)] + _shift3(seq // CHUNK) + [full(a) for a in vecs]
        + [_VMEM, _VMEM, _VMEM],
        out_specs=[o[0] for o in outs], out_shape=[o[1] for o in outs],
        scratch_shapes=[pltpu.VMEM((R, D_MODEL), F32)],
        compiler_params=_params(("arbitrary",)),
    )(o, gate, ys5, xhat0, tgt, tgt, tgt, *vecs, w_out, w_up, w_down)


def _mlp_bwd(h1b, dffb, pre, w_up, w_down):
    tp = h1b.shape[0]
    R = MLP_ROWS if tp % MLP_ROWS == 0 else ROW_BLK
    nr = tp // R

    def body(h_ref, df_ref, pre_ref, wu_ref, wd_ref, gup_ref, gdn_ref, dh1_ref, aup, adn):
        d = pl.program_id(0)
        r = pl.program_id(1)

        @pl.when(r == 0)
        def _():
            aup[...] = jnp.zeros_like(aup)
            adn[...] = jnp.zeros_like(adn)

        h = h_ref[...]
        df = df_ref[...]
        wu = wu_ref[0]
        wd = wd_ref[0]
        pre = pre_ref[...]
        dpre = (_dot_nt(df, wd) * (2.0 * pre)).astype(MM)

        aup[...] += _dot_tn(h, dpre)
        adn[...] += _dot_tn(pre * pre, df)
        contrib = _dot_nt(dpre, wu)
        rows = pl.ds(pl.multiple_of(r * R, 64), R)

        @pl.when(d == 0)
        def _():
            dh1_ref[rows, :] = contrib

        @pl.when(d > 0)
        def _():
            dh1_ref[rows, :] += contrib

        @pl.when(r == nr - 1)
        def _():
            gup_ref[0] = aup[...].astype(gup_ref.dtype)
            gdn_ref[0] = adn[...].astype(gdn_ref.dtype)

    return pl.pallas_call(
        body, name="mlp_bwd", grid=(N_DEV, nr),
        in_specs=[pl.BlockSpec((R, D_MODEL), lambda d, r: (r, 0)), pl.BlockSpec((R, D_MODEL), lambda d, r: (r, 0)),
                  pl.BlockSpec((R, FF_BLK), lambda d, r: (r, d)),
                  pl.BlockSpec((1, D_MODEL, FF_BLK), lambda d, r: (d, 0, 0)),
                  pl.BlockSpec((1, FF_BLK, D_MODEL), lambda d, r: (d, 0, 0))],
        out_specs=[pl.BlockSpec((1, D_MODEL, FF_BLK), lambda d, r: (d, 0, 0)),
                   pl.BlockSpec((1, FF_BLK, D_MODEL), lambda d, r: (d, 0, 0)), _VMEM],
        out_shape=[jax.ShapeDtypeStruct((N_DEV, D_MODEL, FF_BLK), MM), jax.ShapeDtypeStruct((N_DEV, FF_BLK, D_MODEL), MM),
                   jax.ShapeDtypeStruct((tp, D_MODEL), F32)],
        scratch_shapes=[pltpu.VMEM((D_MODEL, FF_BLK), F32), pltpu.VMEM((FF_BLK, D_MODEL), F32)],
        compiler_params=_params(("arbitrary", "arbitrary")),
    )(h1b, dffb, pre, w_up, w_down.reshape(N_DEV, FF_BLK, D_MODEL))


def _post_bwd(dh1m, dr2, xhat1, rstd1, ycat, o, gate, gn_g, gn_b, l1_g, w_out, jobs=()):
    tp = o.shape[0]
    R = ROW_BLK
    nb = tp // R

    def body(dm_ref, dr2_ref, xh1_ref, rs1_ref, yc_ref, o_ref, g_ref, gng, gnb, l1g, wo_ref,
             do_ref, dg_ref, dys_ref, dh0_ref, gwo_ref, dl1g_ref, dl1b_ref, dgng_ref, dgnb_ref, awo):
        i = pl.program_id(0)

        @pl.when(i == 0)
        def _():
            for ref in (awo, dl1g_ref, dl1b_ref, dgng_ref, dgnb_ref):
                ref[...] = jnp.zeros_like(ref)

        dh1 = dm_ref[...] + ALPHA * dr2_ref[...]
        xh1 = xh1_ref[...]
        dl1g_ref[...] += _colsum(dh1 * xh1)
        dl1b_ref[...] += _colsum(dh1)
        dr1 = _ln_bwd(dh1 * l1g[...], xh1, rs1_ref[...])
        dh0_ref[...] = ALPHA * dr1
        dmix = dr1.astype(MM)
        awo[...] += _dot_tn(yc_ref[...], dmix)
        dyc = _dot_nt(dmix, wo_ref[...])
        dys_ref[...] = dyc[:, 0:S5_W]
        for h in range(RET_H):
            sl = slice(h * HEAD, (h + 1) * HEAD)
            gt = g_ref[:, sl]
            _, xhat, rstd, on, s = _gn_gate(o_ref[:, sl], gt, gng[:, sl], gnb[:, sl])
            dyr = dyc[:, S5_W + h * HEAD:S5_W + (h + 1) * HEAD]
            dg_ref[:, sl] = (dyr * on * (s * (1.0 + gt * (1.0 - s)))).astype(dg_ref.dtype)
            don = dyr * gt * s
            dgng_ref[:, sl] += _colsum(don * xhat)
            dgnb_ref[:, sl] += _colsum(don)
            do_ref[:, sl] = _ln_bwd(don * gng[:, sl], xhat, rstd)

        @pl.when(i == nb - 1)
        def _():
            gwo_ref[...] = awo[...].astype(gwo_ref.dtype)

    row = lambda w: pl.BlockSpec((R, w), lambda i: (i, 0))
    full = lambda a: pl.BlockSpec(a.shape, lambda i: (0,) * a.ndim)
    acc = lambda s, dt=F32: (pl.BlockSpec(s, lambda i: (0, 0)), jax.ShapeDtypeStruct(s, dt))
    outs = [(row(RET_W), jax.ShapeDtypeStruct((tp, RET_W), F32)), (row(RET_W), jax.ShapeDtypeStruct((tp, RET_W), MM)),
            (row(S5_W), jax.ShapeDtypeStruct((tp, S5_W), F32)), (row(D_MODEL), jax.ShapeDtypeStruct((tp, D_MODEL), F32)),
            acc((D_MODEL, D_MODEL), MM), acc((1, D_MODEL)), acc((1, D_MODEL)), acc((1, RET_W)), acc((1, RET_W))]
    return _call(
        body, "post_bwd", (nb,),
        [row(D_MODEL), row(D_MODEL), row(D_MODEL), row(1), row(D_MODEL), row(RET_W), row(RET_W),
         full(gn_g), full(gn_b), full(l1_g), _VMEM],
        [o[0] for o in outs], [o[1] for o in outs],
        [pltpu.VMEM((D_MODEL, D_MODEL), F32)],
        (dh1m, dr2, xhat1, rstd1, ycat, o, gate, gn_g, gn_b, l1_g, w_out), jobs)


def _in_bwd(du, dq, dk, dv, dg, dh0r, xhat0, rstd0, li_g, li_b, w_int, jobs=()):
    tp = du.shape[0]
    R = PROJ_ROWS if tp % PROJ_ROWS == 0 else ROW_BLK
    nb = tp // R
    segs = [(0, S5_W)] + [(S5_W + n * RET_W, S5_W + (n + 1) * RET_W) for n in range(4)]

    def body(du_ref, dq_ref, dk_ref, dv_ref, dg_ref, dh0r_ref, xh_ref, rs_ref, lig, lib, w_ref,
             draw_ref, gw_ref, dlg_ref, dlb_ref, aw):
        i = pl.program_id(0)

        @pl.when(i == 0)
        def _():
            for ref in (aw, dlg_ref, dlb_ref):
                ref[...] = jnp.zeros_like(ref)

        valid = (i * R + lax.broadcasted_iota(jnp.int32, (R, 1), 0)) >= PAD
        xh = xh_ref[...]
        hb = (xh * lig[...] + lib[...]).astype(MM)
        dh0 = dh0r_ref[...]
        for (lo, hi), ref in zip(segs, (du_ref, dq_ref, dk_ref, dv_ref, dg_ref)):
            dseg = jnp.where(valid, ref[...], 0.0).astype(MM)
            dh0 = dh0 + _dot(dseg, w_ref[lo:hi, :])
            aw[lo:hi, :] += _dot_tn(dseg, hb)
        dlg_ref[...] += _colsum(dh0 * xh)
        dlb_ref[...] += _colsum(dh0)
        draw_ref[...] = _ln_bwd(dh0 * lig[...], xh, rs_ref[...])

        @pl.when(i == nb - 1)
        def _():
            gw_ref[...] = aw[...].astype(gw_ref.dtype)

    row = lambda w: pl.BlockSpec((R, w), lambda i: (i, 0))
    full = lambda a: pl.BlockSpec(a.shape, lambda i: (0,) * a.ndim)
    acc = lambda s, dt=F32: (pl.BlockSpec(s, lambda i: (0, 0)), jax.ShapeDtypeStruct(s, dt))
    outs = [(row(D_MODEL), jax.ShapeDtypeStruct((tp, D_MODEL), F32)), acc((PROJ_W, D_MODEL), MM),
            acc((1, D_MODEL)), acc((1, D_MODEL))]
    return _call(
        body, "in_bwd", (nb,),
        [row(S5_W), row(RET_W), row(RET_W), row(RET_W), row(RET_W), row(D_MODEL), row(D_MODEL), row(1),
         full(li_g), full(li_b), _VMEM],
        [o[0] for o in outs], [o[1] for o in outs], [pltpu.VMEM((PROJ_W, D_MODEL), F32)],
        (du, dq, dk, dv, dg, dh0r, xhat0, rstd0, li_g, li_b, w_int), jobs)


def _place():
    return lax.axis_index("x"), lax.axis_index("y"), lax.axis_index("c")


def _dma_sems(n):
    return pltpu.SemaphoreType.DMA((n,))


def _job_gather(shard):
    def parts(ins, outs, sems):
        (src,), (out,), (send_sems, recv_sems, local_sem) = ins, outs, sems
        x, y, c = _place()
        me, sib = (x, y, c), (x, y, 1 - c)
        chips = [(1 - x, y), (x, 1 - y), (1 - x, 1 - y)]

        def slot(dev):
            return out.at[4 * dev[0] + 2 * dev[1] + dev[2]]

        def copy(k, block, to, from_input=False):
            return pltpu.make_async_remote_copy(
                src_ref=src if from_input else slot(block), dst_ref=slot(block),
                send_sem=send_sems.at[k], recv_sem=recv_sems.at[k], device_id=to, device_id_type=_MESH)

        mine = pltpu.make_async_copy(src, slot(me), local_sem.at[0])
        first = [copy(0, me, sib, True)] + [copy(1 + j, me, (*chip, c), True) for j, chip in enumerate(chips)]
        return me, sib, chips, copy, mine, first

    def start(ins, outs, sems):
        _, _, _, _, mine, first = parts(ins, outs, sems)
        mine.start()
        for cp in first:
            cp.start()

    def finish(ins, outs, sems):
        me, sib, chips, copy, mine, first = parts(ins, outs, sems)
        c = me[2]
        passed = []
        for j, chip in enumerate(chips):
            copy(1 + j, (*chip, c), me).wait_recv()
            cp = copy(4 + j, (*chip, c), sib)
            cp.start()
            passed.append(cp)
        copy(0, sib, me).wait_recv()
        for j, chip in enumerate(chips):
            copy(4 + j, (*chip, 1 - c), me).wait_recv()
        for cp in first + passed:
            cp.wait_send()
        mine.wait()

    return dict(ins=[shard], outs=[jax.ShapeDtypeStruct((N_DEV,) + shard.shape, shard.dtype)],
                sems=[_dma_sems(7), _dma_sems(7), _dma_sems(1)], start=start, finish=finish)


def _job_pair(g):
    def copies(ins, outs, sems):
        x, y, c = _place()
        return [pltpu.make_async_remote_copy(
            src_ref=ins[0].at[2 * j + (1 - c)], dst_ref=outs[0].at[j], send_sem=sems[0].at[j], recv_sem=sems[1].at[j],
            device_id=(x, y, 1 - c), device_id_type=_MESH) for j in range(4)]

    def start(ins, outs, sems):
        for cp in copies(ins, outs, sems):
            cp.start()

    def finish(ins, outs, sems):
        for cp in copies(ins, outs, sems):
            cp.wait()

    return dict(ins=[g], outs=[jax.ShapeDtypeStruct((4,) + g.shape[1:], g.dtype)], sems=[_dma_sems(4), _dma_sems(4)],
                start=start, finish=finish)


def _job_chips(p):
    def copies(ins, outs, sems):
        x, y, c = _place()
        chips = [(1 - x, y), (x, 1 - y), (1 - x, 1 - y)]
        return [pltpu.make_async_remote_copy(
            src_ref=ins[0].at[2 * chip[0] + chip[1]], dst_ref=outs[0].at[k], send_sem=sems[0].at[k],
            recv_sem=sems[1].at[k], device_id=(*chip, c), device_id_type=_MESH) for k, chip in enumerate(chips)]

    def start(ins, outs, sems):
        for cp in copies(ins, outs, sems):
            cp.start()

    def finish(ins, outs, sems):
        for cp in copies(ins, outs, sems):
            cp.wait()

    return dict(ins=[p], outs=[jax.ShapeDtypeStruct((3,) + p.shape[1:], p.dtype)], sems=[_dma_sems(3), _dma_sems(3)],
                start=start, finish=finish)


def _split_job_refs(jobs, ins, outs, sems):
    res, a, b, c = [], 0, 0, 0
    for job in jobs:
        na, nb, nc = len(job["ins"]), len(job["outs"]), len(job["sems"])
        res.append((ins[a:a + na], outs[b:b + nb], sems[c:c + nc]))
        a, b, c = a + na, b + nb, c + nc
    return res


def _call(body, name, grid, in_specs, out_specs, out_shape, scratch, args, jobs=()):
    jobs = list(jobs)
    n_in, n_out, n_scr = len(in_specs), len(out_specs), len(scratch)
    j_in = [a for job in jobs for a in job["ins"]]
    j_out = [o for job in jobs for o in job["outs"]]
    j_scr = [s for job in jobs for s in job["sems"]]
    nsteps = grid[0]

    def wrapped(*refs):
        ins, jins = refs[:n_in], refs[n_in:n_in + len(j_in)]
        refs = refs[n_in + len(j_in):]
        outs, jouts = refs[:n_out], refs[n_out:n_out + len(j_out)]
        refs = refs[n_out + len(j_out):]
        scr, jscr = refs[:n_scr], refs[n_scr:]
        per_job = _split_job_refs(jobs, jins, jouts, jscr)

        @pl.when(pl.program_id(0) == 0)
        def _():
            for job, r in zip(jobs, per_job):
                job["start"](*r)

        body(*ins, *outs, *scr)

        @pl.when(pl.program_id(0) == nsteps - 1)
        def _():
            for job, r in zip(jobs, per_job):
                job["finish"](*r)

    res = pl.pallas_call(
        wrapped if jobs else body, name=name, grid=grid,
        in_specs=list(in_specs) + [_ANY] * len(j_in), out_specs=list(out_specs) + [_ANY] * len(j_out),
        out_shape=list(out_shape) + j_out, scratch_shapes=list(scratch) + j_scr,
        compiler_params=_params(("arbitrary",) * len(grid)),
    )(*args, *j_in)
    return list(res[:n_out]), list(res[n_out:])


def _exchange(jobs, name):
    j_in = [a for job in jobs for a in job["ins"]]
    j_out = [o for job in jobs for o in job["outs"]]
    j_scr = [s for job in jobs for s in job["sems"]]

    def body(*refs):
        per_job = _split_job_refs(jobs, refs[:len(j_in)], refs[len(j_in):len(j_in) + len(j_out)],
                                  refs[len(j_in) + len(j_out):])
        for job, r in zip(jobs, per_job):
            job["start"](*r)
        for job, r in zip(jobs, per_job):
            job["finish"](*r)

    return pl.pallas_call(body, name=name, out_shape=j_out, in_specs=[_ANY] * len(j_in), out_specs=[_ANY] * len(j_out),
                          scratch_shapes=j_scr)(*j_in)


def _pair_sum(gs, r1s, c_arr, name):
    n = len(gs)

    def body(c_ref, *refs):
        for a in range(n):
            refs[2 * n + a][...] = (refs[a][...].astype(F32) + refs[n + a][...].astype(F32)).astype(refs[2 * n + a].dtype)

    def blk(g, own):
        s = g.shape[1:]
        if own:
            return pl.BlockSpec((1,) + s, lambda j, c_ref: (2 * j + c_ref[0],) + (0,) * len(s))
        return pl.BlockSpec((1,) + s, lambda j, c_ref: (j,) + (0,) * len(s))

    return pl.pallas_call(
        body, name=name,
        grid_spec=pltpu.PrefetchScalarGridSpec(
            num_scalar_prefetch=1, grid=(4,),
            in_specs=[blk(g, True) for g in gs] + [blk(g, False) for g in gs],
            out_specs=[blk(g, False) for g in gs]),
        out_shape=[jax.ShapeDtypeStruct((4,) + g.shape[1:], g.dtype) for g in gs],
        compiler_params=_params(("arbitrary",)),
    )(c_arr, *gs, *r1s)


def _chip_sum(ps, r2s, j_arr, name):
    n = len(ps)

    def body(j_ref, *refs):
        for a in range(n):
            r2 = refs[n + a]
            refs[2 * n + a][...] = ((refs[a][0].astype(F32) + r2[0].astype(F32)) + r2[1].astype(F32)) + r2[2].astype(F32)

    def own(p):
        s = p.shape[1:]
        return pl.BlockSpec((1,) + s, lambda i, j_ref: (j_ref[0],) + (0,) * len(s))

    def whole(p):
        return pl.BlockSpec(p.shape, lambda i, j_ref: (0,) * p.ndim)

    return pl.pallas_call(
        body, name=name,
        grid_spec=pltpu.PrefetchScalarGridSpec(
            num_scalar_prefetch=1, grid=(1,),
            in_specs=[own(p) for p in ps] + [whole(r) for r in r2s],
            out_specs=[pl.BlockSpec(p.shape[1:], lambda i, j_ref: (0,) * (p.ndim - 1)) for p in ps]),
        out_shape=[jax.ShapeDtypeStruct(p.shape[1:], F32) for p in ps],
        compiler_params=_params(("arbitrary",)),
    )(j_arr, *ps, *r2s)


def _adamw_math(w, g, m, v):
    m = ADAM_B1 * m + (1.0 - ADAM_B1) * g
    v = ADAM_B2 * v + (1.0 - ADAM_B2) * (g * g)
    m_hat = m / (1.0 - ADAM_B1 ** ADAM_STEP)
    v_hat = v / (1.0 - ADAM_B2 ** ADAM_STEP)
    return -ADAM_LR * (m_hat / (jnp.sqrt(v_hat) + ADAM_EPS) + ADAM_WD * w), m, v


def _adamw(items, name, steps, jobs=()):
    n = len(items)

    def body(*refs):
        for a in range(n):
            g, w, m, v = (refs[4 * a + t][...] for t in range(4))
            d, m2, v2 = _adamw_math(w, g, m, v)
            refs[4 * n + 3 * a][...] = d
            refs[4 * n + 3 * a + 1][...] = m2
            refs[4 * n + 3 * a + 2][...] = v2

    def blk(arr):
        r, c = arr.shape
        return pl.BlockSpec((r // steps, c), lambda i: (i, 0))

    flat = [t for it in items for t in it]
    return _call(body, name, (steps,), [blk(t) for t in flat], [blk(it[1]) for it in items for _ in range(3)],
                 [jax.ShapeDtypeStruct(it[1].shape, F32) for it in items for _ in range(3)], [], flat, jobs)


def _adamw_small(gathered, w, m, v, name):
    def body(gs_ref, w_ref, m_ref, v_ref, g_ref, d_ref, m2_ref, v2_ref):
        g = gs_ref[0]
        for s in range(1, N_DEV):
            g = g + gs_ref[s]
        g_ref[...] = g
        d_ref[...], m2_ref[...], v2_ref[...] = _adamw_math(w_ref[...], g, m_ref[...], v_ref[...])

    return pl.pallas_call(
        body, name=name, out_shape=[jax.ShapeDtypeStruct(w.shape, F32)] * 4,
        in_specs=[_VMEM] * 4, out_specs=[_VMEM] * 4, compiler_params=_params(),
    )(gathered, w, m, v)


SMALL = ["ln_in_g", "ln_in_b", "s5_lambda_re", "s5_lambda_im", "s5_log_dt", "s5_b_re", "s5_b_im", "s5_c_re", "s5_c_im",
         "s5_d", "s5_b_glu", "ret_gn_g", "ret_gn_b", "ln1_g", "ln1_b", "ln2_g", "ln2_b"]
LATE = ["ln_in_g", "ln_in_b", "meta_tokens"]
EARLY = [n for n in SMALL if n not in LATE] + ["s5_w_glu", "loss"]
LANE = 128


def _pack(arrs):
    parts = []
    for a in arrs:
        f = a.reshape(-1)
        parts.append(jnp.pad(f, (0, (-f.shape[0]) % LANE)))
    flat = jnp.concatenate(parts)
    rows = -(-flat.shape[0] // LANE)
    flat = jnp.pad(flat, (0, (-rows % 8) * LANE + rows * LANE - flat.shape[0]))
    return flat.reshape(-1, LANE)


def _unpack(packed, shapes):
    flat = packed.reshape(-1)
    out, off = [], 0
    for s in shapes:
        n = math.prod(s)
        out.append(flat[off:off + n].reshape(s))
        off += n + (-n) % LANE
    return out


def _rope_tables(tp):
    pos = jnp.arange(tp, dtype=F32) - float(PAD)
    inv_freq = 1.0 / (ROPE_BASE ** (jnp.arange(0, HEAD, 2, dtype=F32) / HEAD))
    ang = pos[:, None] * inv_freq[None, :]
    cos, sin = jnp.cos(ang), jnp.sin(ang)
    return jnp.concatenate([cos, cos], axis=1), jnp.concatenate([-sin, sin], axis=1)


RET_CHUNK = ROW_BLK


def _decay_tables():
    log_gamma = jnp.log1p(-jnp.exp2(-5.0 - jnp.arange(RET_H, dtype=F32)))
    idx = jnp.arange(RET_CHUNK, dtype=F32)
    diff = idx[:, None] - idx[None, :]
    dmat = jnp.where(diff[None] >= 0, jnp.exp(jnp.maximum(diff, 0.0)[None] * log_gamma[:, None, None]), 0.0)
    zeta = jnp.exp((RET_CHUNK - 1.0 - idx)[None] * log_gamma[:, None])
    xi = jnp.exp((idx + 1.0)[None] * log_gamma[:, None])
    gam = jnp.exp(RET_CHUNK * log_gamma)
    wide = lambda t: jnp.broadcast_to(t[:, :, None], (RET_H, RET_CHUNK, HEAD))
    return dmat, wide(zeta), wide(xi), jnp.broadcast_to(gam[:, None, None], (RET_H, HEAD, HEAD))


def _local_step(x2d, tgt, meta_full, w_int, w_out, w_up, w_down, w_glu, sp, distributed):
    tp = x2d.shape[0] + CHUNK
    row = lambda a: a.reshape(1, -1)
    cos2, sin2 = _rope_tables(tp)
    dmat, zeta_b, xi_b, gam_b = _decay_tables()
    li_g, li_b = row(sp["ln_in_g"]), row(sp["ln_in_b"])
    l1_g, l1_b, l2_g, l2_b = row(sp["ln1_g"]), row(sp["ln1_b"]), row(sp["ln2_g"]), row(sp["ln2_b"])
    gn_g, gn_b = row(sp["ret_gn_g"]), row(sp["ret_gn_b"])
    lre, lim = row(sp["s5_lambda_re"]), row(sp["s5_lambda_im"])
    ldt = row(jnp.repeat(sp["s5_log_dt"].reshape(-1), S5_P))
    to_t = lambda b: b.reshape(S5_G, S5_P, S5_H).transpose(2, 0, 1).reshape(S5_H, S5_N)
    bre_t, bim_t = to_t(sp["s5_b_re"]), to_t(sp["s5_b_im"])
    to_w = lambda c: jnp.tile(c.reshape(S5_W, S5_P), (1, 2))
    cre_w, cim_w = to_w(sp["s5_c_re"]), to_w(sp["s5_c_im"])

    jobs = (lambda *j: list(j)) if distributed else (lambda *j: [])
    c_arr = jnp.reshape(lax.axis_index("c"), (1,)).astype(jnp.int32) if distributed else None
    (xhat0, rstd0), bg = _ln_in(x2d, meta_full, jobs(*([_job_gather(w_int), _job_gather(w_glu)] if distributed else [])))
    if distributed:
        w_int, w_glu = bg[0].reshape(PROJ_W, D_MODEL), bg[1].reshape(S5_W, S5_W)
    s5_small = (lre, lim, ldt, bre_t, bim_t, cre_w, cim_w, row(sp["s5_d"]), w_glu, row(sp["s5_b_glu"]))
    (u, q, k, v, gate), bg = _in_proj(
        xhat0, li_g, li_b, w_int, cos2, sin2,
        jobs(*([_job_gather(w_out), _job_gather(w_up)] if distributed else [])))
    if distributed:
        w_out, w_up = bg[0].reshape(D_MODEL, D_MODEL), bg[1]
    (ys5, xr, xi), bg = _s5_fwd(u, *s5_small, jobs=jobs(_job_gather(w_down) if distributed else None))
    if distributed:
        w_down = bg[0].reshape(D_FF, D_MODEL)
    (o, states), _ = _ret_fwd(q, k, v, dmat, zeta_b, xi_b, gam_b)
    ycat, xhat1, rstd1, h1b, dr2, dffb, loss8, dl2g, dl2b, pre = _post_fwd(
        o, gate, ys5, xhat0, tgt, gn_g, gn_b, li_g, li_b, l1_g, l1_b, l2_g, l2_b, w_out, w_up, w_down)
    g_up, g_down, dh1m = _mlp_bwd(h1b, dffb, pre, w_up, w_down)
    (do, dgate, dys5, dh0r, g_out, dl1g, dl1b, dgng, dgnb), bg = _post_bwd(
        dh1m, dr2, xhat1, rstd1, ycat, o, gate, gn_g, gn_b, l1_g, w_out,
        jobs(*([_job_pair(g_up), _job_pair(g_down)] if distributed else [])))
    g_out = g_out.reshape(N_DEV, D_MODEL // N_DEV, D_MODEL)
    if distributed:
        p_up, p_down = _pair_sum([g_up, g_down], bg, c_arr, "pair_sum_mlp")
    (du, dlre, dlim, dldt, dbre_t, dbim_t, dcre, dcim, dd, dwglu, dbglu), bg = _s5_bwd(
        dys5, u, xr, xi, *s5_small,
        jobs=jobs(*([_job_chips(p_up), _job_chips(p_down), _job_pair(g_out)] if distributed else [])))
    if distributed:
        r_up, r_down = bg[0], bg[1]
        (p_out,) = _pair_sum([g_out], bg[2:], c_arr, "pair_sum_out")
    (dq, dk, dv), bg = _ret_bwd(q, k, v, do, states, cos2, sin2, dmat, zeta_b, xi_b, gam_b,
                                jobs(_job_chips(p_out) if distributed else None))
    r_out = bg[0] if distributed else None
    from_t = lambda t: t.reshape(S5_H, S5_G, S5_P).transpose(1, 2, 0)
    small = {
        "s5_lambda_re": dlre, "s5_lambda_im": dlim, "s5_log_dt": dldt[:, :S5_G],
        "s5_b_re": from_t(dbre_t), "s5_b_im": from_t(dbim_t), "s5_c_re": dcre, "s5_c_im": dcim, "s5_d": dd,
        "s5_b_glu": dbglu, "ret_gn_g": dgng, "ret_gn_b": dgnb, "ln1_g": dl1g, "ln1_b": dl1b, "ln2_g": dl2g, "ln2_b": dl2b,
        "s5_w_glu": dwglu, "loss": loss8[0:1, 0:1]}
    early_pack = _pack([small[n] for n in EARLY])
    (draw, g_int, dlig, dlib), bg = _in_bwd(du, dq, dk, dv, dgate, dh0r, xhat0, rstd0, li_g, li_b, w_int,
                                            jobs(_job_gather(early_pack) if distributed else None))
    small.update(ln_in_g=dlig, ln_in_b=dlib, meta_tokens=draw[PAD:CHUNK])
    g_int = g_int.reshape(N_DEV, PROJ_W // N_DEV, D_MODEL)
    if distributed:
        (r1_in,) = _exchange([_job_pair(g_int)], "exchange_pair_in")
        (p_in,) = _pair_sum([g_int], [r1_in], c_arr, "pair_sum_in")
        big = dict(chip_sums=[p_in, p_out, p_up, p_down], received=[None, r_out, r_up, r_down], early=bg[0])
    else:
        big = dict(partials=[g_int, g_out, g_up, g_down])
    return draw, big, small


def kernel(x, meta_tokens, ln_in_g, ln_in_b, w_in, s5_lambda_re, s5_lambda_im, s5_log_dt, s5_b_re, s5_b_im, s5_c_re, s5_c_im, s5_d, s5_w_glu, s5_b_glu, ret_gn_g, ret_gn_b, w_out, ln1_g, ln1_b, w_up, w_down, ln2_g, ln2_b, loss_target, m_meta_tokens, m_ln_in_g, m_ln_in_b, m_w_in, m_s5_lambda_re, m_s5_lambda_im, m_s5_log_dt, m_s5_b_re, m_s5_b_im, m_s5_c_re, m_s5_c_im, m_s5_d, m_s5_w_glu, m_s5_b_glu, m_ret_gn_g, m_ret_gn_b, m_w_out, m_ln1_g, m_ln1_b, m_w_up, m_w_down, m_ln2_g, m_ln2_b, v_meta_tokens, v_ln_in_g, v_ln_in_b, v_w_in, v_s5_lambda_re, v_s5_lambda_im, v_s5_log_dt, v_s5_b_re, v_s5_b_im, v_s5_c_re, v_s5_c_im, v_s5_d, v_s5_w_glu, v_s5_b_glu, v_ret_gn_g, v_ret_gn_b, v_w_out, v_ln1_g, v_ln1_b, v_w_up, v_w_down, v_ln2_g, v_ln2_b):
    args = dict(locals())
    names = ["meta_tokens", "ln_in_g", "ln_in_b", "w_in", "s5_lambda_re", "s5_lambda_im", "s5_log_dt", "s5_b_re", "s5_b_im",
             "s5_c_re", "s5_c_im", "s5_d", "s5_w_glu", "s5_b_glu", "ret_gn_g", "ret_gn_b", "w_out", "ln1_g", "ln1_b",
             "w_up", "w_down", "ln2_g", "ln2_b"]
    ax, ay, ac = _place()
    me = 4 * ax + 2 * ay + ac

    (a_meta,) = _exchange([_job_gather(meta_tokens)], "gather_meta")
    meta_full = a_meta.transpose(1, 0, 2).reshape(N_META, D_MODEL)

    sp = {n: args[n] for n in SMALL}
    draw, big, small = _local_step(x[0], loss_target[0], meta_full, w_in[0].T.astype(MM), w_out[0].astype(MM),
                                   w_up[0].astype(MM), w_down[0].astype(MM), s5_w_glu[0].astype(MM), sp, True)

    j_arr = jnp.reshape(2 * ax + ay, (1,)).astype(jnp.int32)
    two_d = lambda a: a.reshape(a.shape[-2:])
    item = lambda n, g: tuple(two_d(t) for t in (g, args[n], args["m_" + n], args["v_" + n]))
    g_out, g_up, g_down = _chip_sum(big["chip_sums"][1:], big["received"][1:], j_arr, "chip_sum_mlp")
    shard_grads = {"w_out": g_out[None], "w_up": g_up[None], "w_down": g_down[None]}
    late_pack = _pack([small[n] for n in LATE])
    res, (r_in, late_all) = _adamw([item(n, shard_grads[n]) for n in ("w_out", "w_up", "w_down")], "adamw_mlp", 8,
                                   [_job_chips(big["chip_sums"][0]), _job_gather(late_pack)])
    (g_int,) = _chip_sum(big["chip_sums"][:1], [r_in], j_arr, "chip_sum_in")
    shard_grads["w_in"] = g_int.T[None]

    def small_update(order, shapes, gathered, placeholders, name):
        packs = [_pack([jnp.zeros(s, F32) if n in placeholders else args[p + n] for n, s in zip(order, shapes)])
                 for p in ("", "m_", "v_")]
        outs = _adamw_small(gathered, *packs, name)
        return [dict(zip(order, _unpack(o, shapes))) for o in outs]

    early_shapes = [args[n].shape for n in EARLY[:-2]] + [(S5_W, S5_W), (1,)]
    late_shapes = [args["ln_in_g"].shape, args["ln_in_b"].shape, (N_META, D_MODEL)]
    parts = [small_update(EARLY, early_shapes, big["early"], ("s5_w_glu", "loss"), "adamw_small_early"),
             small_update(LATE, late_shapes, late_all, ("meta_tokens",), "adamw_small_late")]
    g_small, d_small, m_small, v_small = ({**parts[0][t], **parts[1][t]} for t in range(4))
    loss = g_small["loss"].reshape(())

    shard_grads["meta_tokens"] = lax.dynamic_slice(g_small["meta_tokens"], (0, me * (D_MODEL // N_DEV)),
                                                   (N_META, D_MODEL // N_DEV))
    shard_grads["s5_w_glu"] = lax.dynamic_slice(g_small["s5_w_glu"], (me * (S5_W // N_DEV), 0),
                                                (S5_W // N_DEV, S5_W))[None]
    res_in, _ = _adamw([item("w_in", shard_grads["w_in"])], "adamw_in", 8)
    res2, _ = _adamw([item(n, shard_grads[n]) for n in ("meta_tokens", "s5_w_glu")], "adamw_shard_small", 1)
    upd = {"w_in": [r.reshape(args["w_in"].shape) for r in res_in]}
    for idx, n in enumerate(("w_out", "w_up", "w_down")):
        upd[n] = [r.reshape(args[n].shape) for r in res[3 * idx:3 * idx + 3]]
    for idx, n in enumerate(("meta_tokens", "s5_w_glu")):
        upd[n] = [r.reshape(args[n].shape) for r in res2[3 * idx:3 * idx + 3]]

    grads, deltas, new_m, new_v = [], [], [], []
    for n in names:
        if n in upd:
            grads.append(shard_grads[n].reshape(args[n].shape))
            d, m2, v2 = upd[n]
        else:
            grads.append(g_small[n])
            d, m2, v2 = d_small[n], m_small[n], v_small[n]
        deltas.append(d)
        new_m.append(m2)
        new_v.append(v2)
    grad_x = draw[CHUNK:][None]
    return (loss, grad_x, *grads, *deltas, *new_m, *new_v)
```

```python
import math

import jax
import jax.numpy as jnp
from jax import lax
from jax.experimental import pallas as pl
from jax.experimental.pallas import tpu as pltpu

F32 = jnp.float32
MM = jnp.bfloat16

D_MODEL = 1024
N_META = 16
CHUNK = 128
PAD = CHUNK - N_META
S5_W, S5_G, S5_H, S5_P = 256, 16, 16, 64
S5_N = S5_G * S5_P
RET_W, RET_H, HEAD = 768, 6, 128
D_FF = 4096
PROJ_W = S5_W + 4 * RET_W
N_DEV = 8
FF_BLK = D_FF // N_DEV
ROW_BLK = 384
MLP_ROWS = 1408
PROJ_ROWS = 704
ALPHA = 2.0 ** 0.25
LN_EPS = 1e-5
GN_EPS = 1e-5
ROPE_BASE = 10000.0
GELU_C = math.sqrt(2.0 / math.pi)
GELU_A = 0.044715
ADAM_LR, ADAM_B1, ADAM_B2, ADAM_EPS, ADAM_WD, ADAM_STEP = 0.001, 0.9, 0.999, 1e-08, 0.01, 10
VMEM_LIMIT = 60 * 1024 * 1024

_VMEM = pl.BlockSpec(memory_space=pltpu.VMEM)
_ANY = pl.BlockSpec(memory_space=pl.ANY)
_MESH = pl.DeviceIdType.MESH


def _params(sem=None):
    return pltpu.CompilerParams(dimension_semantics=sem, vmem_limit_bytes=VMEM_LIMIT)


def _dot(a, b):
    return jnp.dot(a.astype(MM), b.astype(MM), preferred_element_type=F32)


def _dot_nt(a, b):
    return lax.dot_general(a.astype(MM), b.astype(MM), (((1,), (1,)), ((), ())), preferred_element_type=F32)


def _dot_tn(a, b):
    return lax.dot_general(a.astype(MM), b.astype(MM), (((0,), (0,)), ((), ())), preferred_element_type=F32)


def _split3(a):
    hi = a.astype(jnp.bfloat16)
    r1 = a - hi.astype(F32)
    mid = r1.astype(jnp.bfloat16)
    lo = (r1 - mid.astype(F32)).astype(jnp.bfloat16)
    return hi, mid, lo


def _dot_sel_rhs(a, sel):
    s = sel.astype(jnp.bfloat16)
    return sum(jnp.dot(p, s, preferred_element_type=F32) for p in _split3(a))


def _dot_sel_lhs(sel, b):
    s = sel.astype(jnp.bfloat16)
    return sum(jnp.dot(s, p, preferred_element_type=F32) for p in _split3(b))


def _ln_fwd(r, eps):
    mu = jnp.mean(r, axis=-1, keepdims=True)
    xc = r - mu
    var = jnp.mean(xc * xc, axis=-1, keepdims=True)
    rstd = lax.rsqrt(var + eps)
    return xc * rstd, rstd


def _ln_bwd(dxhat, xhat, rstd):
    m1 = jnp.mean(dxhat, axis=-1, keepdims=True)
    m2 = jnp.mean(dxhat * xhat, axis=-1, keepdims=True)
    return rstd * (dxhat - m1 - xhat * m2)


def _colsum(a):
    return jnp.sum(a, axis=0, keepdims=True)


def _shift3(n_in):
    return [pl.BlockSpec((CHUNK, D_MODEL), (lambda i, j=j: (jnp.clip(3 * i - 1 + j, 0, n_in - 1), 0))) for j in range(3)]


def _ln_in(x2d, meta_full, jobs=()):
    seq = x2d.shape[0]
    tp = seq + CHUNK
    R = ROW_BLK

    def body(xa, xb, xc, meta_ref, xhat_ref, rstd_ref, raw_ref):
        raw_ref[0:CHUNK, :] = xa[...]
        raw_ref[CHUNK:2 * CHUNK, :] = xb[...]
        raw_ref[2 * CHUNK:3 * CHUNK, :] = xc[...]

        @pl.when(pl.program_id(0) == 0)
        def _():
            raw_ref[0:PAD, :] = jnp.zeros((PAD, D_MODEL), F32)
            raw_ref[PAD:CHUNK, :] = meta_ref[...]

        xhat_ref[...], rstd_ref[...] = _ln_fwd(raw_ref[...], LN_EPS)

    row = lambda w: pl.BlockSpec((R, w), lambda i: (i, 0))
    return _call(
        body, "ln_in", (tp // R,),
        _shift3(seq // CHUNK) + [pl.BlockSpec((N_META, D_MODEL), lambda i: (0, 0))],
        [row(D_MODEL), row(1)], [jax.ShapeDtypeStruct((tp, D_MODEL), F32), jax.ShapeDtypeStruct((tp, 1), F32)],
        [pltpu.VMEM((R, D_MODEL), F32)], (x2d, x2d, x2d, meta_full), jobs)


def _in_proj(xhat0, ln_g, ln_b, w_int, cos2, sin2, jobs=()):
    tp = xhat0.shape[0]
    R = PROJ_ROWS if tp % PROJ_ROWS == 0 else ROW_BLK

    def body(xh_ref, g_ref, b_ref, w_ref, cos_ref, sin_ref, u_ref, q_ref, k_ref, v_ref, gate_ref):
        hb = (xh_ref[...] * g_ref[...] + b_ref[...]).astype(MM)
        valid = (pl.program_id(0) * R + lax.broadcasted_iota(jnp.int32, (R, 1), 0)) >= PAD

        def seg(lo, hi):
            return jnp.where(valid, _dot_nt(hb, w_ref[lo:hi, :]), 0.0)

        u_ref[...] = seg(0, S5_W)
        cos = cos_ref[...]
        sin = sin_ref[...]
        q = seg(S5_W, S5_W + RET_W)
        k = seg(S5_W + RET_W, S5_W + 2 * RET_W)
        for h in range(RET_H):
            sl = slice(h * HEAD, (h + 1) * HEAD)
            qh = q[:, sl]
            kh = k[:, sl]
            q_ref[:, sl] = (qh * cos + pltpu.roll(qh, HEAD // 2, 1) * sin).astype(q_ref.dtype)
            k_ref[:, sl] = ((kh * cos + pltpu.roll(kh, HEAD // 2, 1) * sin) * (HEAD ** -0.5)).astype(k_ref.dtype)
        v_ref[...] = seg(S5_W + 2 * RET_W, S5_W + 3 * RET_W).astype(v_ref.dtype)
        gate_ref[...] = seg(S5_W + 3 * RET_W, PROJ_W)

    def rows(w, dt):
        return pl.BlockSpec((R, w), lambda i: (i, 0)), jax.ShapeDtypeStruct((tp, w), dt)

    outs = [rows(S5_W, F32), rows(RET_W, MM), rows(RET_W, MM), rows(RET_W, MM), rows(RET_W, F32)]
    full = lambda s: pl.BlockSpec(s, lambda i: (0,) * len(s))
    return _call(
        body, "in_proj", (tp // R,),
        [pl.BlockSpec((R, D_MODEL), lambda i: (i, 0)), full((1, D_MODEL)), full((1, D_MODEL)), _VMEM,
         pl.BlockSpec((R, HEAD), lambda i: (i, 0)), pl.BlockSpec((R, HEAD), lambda i: (i, 0))],
        [o[0] for o in outs], [o[1] for o in outs], [], (xhat0, ln_g, ln_b, w_int, cos2, sin2), jobs)


def _s5_disc(lre, lim, ldt, bre_t, bim_t):
    dt = jnp.exp(ldt)
    mag = jnp.exp(lre * dt)
    ang = lim * dt
    lbr = mag * jnp.cos(ang)
    lbi = mag * jnp.sin(ang)
    den = lre * lre + lim * lim
    nr = lbr - 1.0
    qr = (nr * lre + lbi * lim) / den
    qi = (lbi * lre - nr * lim) / den
    return lbr, lbi, qr * bre_t - qi * bim_t, qr * bim_t + qi * bre_t


def _s5_tables(lbr, lbi, reverse):
    if reverse:
        lbi = -lbi
    pw = [(lbr, lbi)]
    for _ in range(7):
        r, i = pw[-1]
        pw.append((r * lbr - i * lbi, r * lbi + i * lbr))
    row = lax.broadcasted_iota(jnp.int32, (8, S5_N), 0)
    tabs = []
    for k in range(3):
        sh = 2 ** k
        mask = (row < 8 - sh) if reverse else (row >= sh)
        ar, ai = pw[sh - 1]
        tabs.append((jnp.where(mask, ar, 0.0), jnp.where(mask, ai, 0.0)))
    pr = jnp.zeros((8, S5_N), F32)
    pi = jnp.zeros((8, S5_N), F32)
    for i in range(8):
        ar, ai = pw[7 - i] if reverse else pw[i]
        pr = jnp.where(row == i, ar, pr)
        pi = jnp.where(row == i, ai, pi)
    tabs.append((pr, pi))
    return tabs


def _store_tables(tab_ref, tabs):
    for k, (r, i) in enumerate(tabs):
        tab_ref[2 * k] = r
        tab_ref[2 * k + 1] = i


def _bd_mask():
    r = lax.broadcasted_iota(jnp.int32, (S5_W, S5_N), 0)
    c = lax.broadcasted_iota(jnp.int32, (S5_W, S5_N), 1)
    return jnp.right_shift(r, 4) == jnp.right_shift(c, 6)


def _s5_block_diag(bbr_t, bbi_t, cre_w, cim_w):
    mask = _bd_mask()
    bd = lambda t: jnp.where(mask, t, 0.0)
    return (bd(jnp.tile(bbr_t, (S5_G, 1))), bd(jnp.tile(bbi_t, (S5_G, 1))),
            bd(jnp.tile(cre_w, (1, S5_N // HEAD))), bd(jnp.tile(cim_w, (1, S5_N // HEAD))))


def _scan8(xr, xi, tab_ref, lanes, reverse):
    for k in range(3):
        sh = (8 - 2 ** k) if reverse else 2 ** k
        sr = pltpu.roll(xr, sh, 0)
        si = pltpu.roll(xi, sh, 0)
        mr = tab_ref[2 * k, :, lanes]
        mi = tab_ref[2 * k + 1, :, lanes]
        xr, xi = xr + (mr * sr - mi * si), xi + (mr * si + mi * sr)
    return xr, xi


S5_LANES = 256


def _gelu(y):
    t = jnp.tanh(GELU_C * (y + GELU_A * y * y * y))
    return 0.5 * y * (1.0 + t), t


def _s5_fwd(u, lre, lim, ldt, bre_t, bim_t, cre_w, cim_w, d_row, w_glu, b_glu, jobs=()):
    tp = u.shape[0]
    R = ROW_BLK

    def body(u_ref, lre_ref, lim_ref, ldt_ref, bre_ref, bim_ref, cre_ref, cim_ref, d_ref, wg_ref, bg_ref,
             y_ref, xr_ref, xi_ref, bbd_r, bbd_i, cbd_r, cbd_i, tab_ref, car_r, car_i):
        @pl.when(pl.program_id(0) == 0)
        def _():
            lbr, lbi, bbr, bbi = _s5_disc(lre_ref[...], lim_ref[...], ldt_ref[...], bre_ref[...], bim_ref[...])
            br, bi, cr, ci = _s5_block_diag(bbr, bbi, cre_ref[...], cim_ref[...])
            bbd_r[...] = br.astype(MM)
            bbd_i[...] = bi.astype(MM)
            cbd_r[...] = cr.astype(MM)
            cbd_i[...] = ci.astype(MM)
            _store_tables(tab_ref, _s5_tables(lbr, lbi, False))
            car_r[...] = jnp.zeros_like(car_r)
            car_i[...] = jnp.zeros_like(car_i)

        u = u_ref[...]
        ub = u.astype(MM)
        xr_ref[...] = jnp.dot(ub, bbd_r[...], preferred_element_type=F32)
        xi_ref[...] = jnp.dot(ub, bbd_i[...], preferred_element_type=F32)
        for j in range(S5_N // S5_LANES):
            lanes = pl.ds(j * S5_LANES, S5_LANES)
            pr = tab_ref[6, :, lanes]
            pi = tab_ref[7, :, lanes]

            def step(g, carry):
                cr, ci = carry
                rows = pl.ds(pl.multiple_of(g * 8, 8), 8)
                xr, xi = _scan8(xr_ref[rows, lanes], xi_ref[rows, lanes], tab_ref, lanes, False)
                br = jnp.broadcast_to(cr[7:8, :], cr.shape)
                bi = jnp.broadcast_to(ci[7:8, :], ci.shape)
                xr = xr + (pr * br - pi * bi)
                xi = xi + (pr * bi + pi * br)
                xr_ref[rows, lanes] = xr
                xi_ref[rows, lanes] = xi
                return xr, xi

            cr, ci = lax.fori_loop(0, R // 8, step, (car_r[:, lanes], car_i[:, lanes]), unroll=2)
            car_r[:, lanes] = cr
            car_i[:, lanes] = ci
        y = _dot_nt(xr_ref[...], cbd_r[...]) - _dot_nt(xi_ref[...], cbd_i[...]) + d_ref[...] * u
        yg, _ = _gelu(y)
        z = _dot(yg, wg_ref[...]) + bg_ref[...]
        y_ref[...] = yg * jax.nn.sigmoid(z)

    full = lambda a: pl.BlockSpec(a.shape, lambda i: (0,) * a.ndim)
    small = [lre, lim, ldt, bre_t, bim_t, cre_w, cim_w, d_row, w_glu, b_glu]
    return _call(
        body, "s5_fwd", (tp // R,),
        [pl.BlockSpec((R, S5_W), lambda i: (i, 0))] + [full(a) for a in small],
        [pl.BlockSpec((R, S5_W), lambda i: (i, 0)), pl.BlockSpec((R, S5_N), lambda i: (i, 0)),
         pl.BlockSpec((R, S5_N), lambda i: (i, 0))],
        [jax.ShapeDtypeStruct((tp, S5_W), F32), jax.ShapeDtypeStruct((tp, S5_N), F32),
         jax.ShapeDtypeStruct((tp, S5_N), F32)],
        [pltpu.VMEM((S5_W, S5_N), MM)] * 4 + [pltpu.VMEM((8, 8, S5_N), F32), pltpu.VMEM((8, S5_N), F32),
                                              pltpu.VMEM((8, S5_N), F32)],
        (u, *small), jobs)


def _s5_bwd(dy_out, u, xr, xi, lre, lim, ldt, bre_t, bim_t, cre_w, cim_w, d_row, w_glu, b_glu, jobs=()):
    tp = u.shape[0]
    R = ROW_BLK
    nb = tp // R

    def body(dyo_ref, u_ref, xr_ref, xi_ref, xpr_ref, xpi_ref,
             lre_ref, lim_ref, ldt_ref, bre_ref, bim_ref, cre_ref, cim_ref, d_ref, wg_ref, bg_ref,
             du_ref, dlre_ref, dlim_ref, dldt_ref, dbre_ref, dbim_ref, dcre_ref, dcim_ref, dd_ref, dwg_ref, dbg_ref,
             bbd_r, bbd_i, cbd_r, cbd_i, tab_ref, car_r, car_i, gr_ref, gi_ref, xer_ref, xei_ref,
             abr, abi, acr, aci, adr, adi):
        i = pl.program_id(0)

        @pl.when(i == 0)
        def _():
            lbr, lbi, bbr, bbi = _s5_disc(lre_ref[...], lim_ref[...], ldt_ref[...], bre_ref[...], bim_ref[...])
            br, bi, cr, ci = _s5_block_diag(bbr, bbi, cre_ref[...], cim_ref[...])
            bbd_r[...] = br.astype(MM)
            bbd_i[...] = bi.astype(MM)
            cbd_r[...] = cr.astype(MM)
            cbd_i[...] = ci.astype(MM)
            _store_tables(tab_ref, _s5_tables(lbr, lbi, True))
            for ref in (car_r, car_i, abr, abi, acr, aci, adr, adi, dd_ref, dwg_ref, dbg_ref):
                ref[...] = jnp.zeros_like(ref)

        u = u_ref[...]
        xrv = xr_ref[...]
        xiv = xi_ref[...]
        y = _dot_nt(xrv, cbd_r[...]) - _dot_nt(xiv, cbd_i[...]) + d_ref[...] * u
        yg, t = _gelu(y)
        z = _dot(yg, wg_ref[...]) + bg_ref[...]
        s = jax.nn.sigmoid(z)
        dout = dyo_ref[...]
        dz = dout * yg * s * (1.0 - s)
        dyg = dout * s + _dot_nt(dz, wg_ref[...])
        dwg_ref[...] += _dot_tn(yg, dz)
        dbg_ref[...] += _colsum(dz)
        dy = dyg * (0.5 * (1.0 + t) + 0.5 * y * (1.0 - t * t) * GELU_C * (1.0 + 3.0 * GELU_A * y * y))
        dd_ref[...] += _colsum(dy * u)
        acr[...] += _dot_tn(dy, xrv)
        aci[...] -= _dot_tn(dy, xiv)
        gr_ref[...] = _dot(dy, cbd_r[...])
        gi_ref[...] = -_dot(dy, cbd_i[...])
        has_prev = (i < nb - 1).astype(F32)
        xer_ref[0:8, :] = xpr_ref[...] * has_prev
        xei_ref[0:8, :] = xpi_ref[...] * has_prev
        xer_ref[8:R + 8, :] = xrv
        xei_ref[8:R + 8, :] = xiv
        row = lax.broadcasted_iota(jnp.int32, (8, S5_LANES), 0)
        for j in range(S5_N // S5_LANES):
            lanes = pl.ds(j * S5_LANES, S5_LANES)
            pr = tab_ref[6, :, lanes]
            pi = tab_ref[7, :, lanes]

            def step(n, carry):
                cr, ci, sar, sai = carry
                g = R // 8 - 1 - n
                r0 = pl.multiple_of(g * 8, 8)
                rows = pl.ds(r0, 8)
                gr, gi = _scan8(gr_ref[rows, lanes], gi_ref[rows, lanes], tab_ref, lanes, True)
                br = jnp.broadcast_to(cr[0:1, :], cr.shape)
                bi = jnp.broadcast_to(ci[0:1, :], ci.shape)
                gr = gr + (pr * br - pi * bi)
                gi = gi + (pr * bi + pi * br)
                gr_ref[rows, lanes] = gr
                gi_ref[rows, lanes] = gi
                last = row == 7
                xpr = pltpu.roll(jnp.where(last, xer_ref[rows, lanes], xer_ref[pl.ds(r0 + 8, 8), lanes]), 1, 0)
                xpi = pltpu.roll(jnp.where(last, xei_ref[rows, lanes], xei_ref[pl.ds(r0 + 8, 8), lanes]), 1, 0)
                return gr, gi, sar + (gr * xpr + gi * xpi), sai + (gi * xpr - gr * xpi)

            cr, ci, sar, sai = lax.fori_loop(
                0, R // 8, step, (car_r[:, lanes], car_i[:, lanes], adr[:, lanes], adi[:, lanes]), unroll=2)
            car_r[:, lanes] = cr
            car_i[:, lanes] = ci
            adr[:, lanes] = sar
            adi[:, lanes] = sai
        grv = gr_ref[...]
        giv = gi_ref[...]
        du_ref[...] = (dy * d_ref[...] + _dot_nt(grv, bbd_r[...]) + _dot_nt(giv, bbd_i[...])).astype(du_ref.dtype)
        abr[...] += _dot_tn(u, grv)
        abi[...] += _dot_tn(u, giv)

        @pl.when(i == nb - 1)
        def _():
            mask = _bd_mask()
            r16 = lax.broadcasted_iota(jnp.int32, (S5_H, S5_W), 1)
            h16 = lax.broadcasted_iota(jnp.int32, (S5_H, S5_W), 0)
            fold_b = jnp.bitwise_and(r16, S5_H - 1) == h16
            c64 = lax.broadcasted_iota(jnp.int32, (S5_N, S5_P), 0)
            p64 = lax.broadcasted_iota(jnp.int32, (S5_N, S5_P), 1)
            fold_c = jnp.bitwise_and(c64, S5_P - 1) == p64
            dbbr = _dot_sel_lhs(fold_b, jnp.where(mask, abr[...], 0.0))
            dbbi = _dot_sel_lhs(fold_b, jnp.where(mask, abi[...], 0.0))
            dcre_ref[...] = _dot_sel_rhs(jnp.where(mask, acr[...], 0.0), fold_c)
            dcim_ref[...] = _dot_sel_rhs(jnp.where(mask, aci[...], 0.0), fold_c)
            dlbr = _colsum(adr[...])
            dlbi = _colsum(adi[...])
            _, vjp = jax.vjp(_s5_disc, lre_ref[...], lim_ref[...], ldt_ref[...], bre_ref[...], bim_ref[...])
            dlre, dlim, dldt, dbre, dbim = vjp((dlbr, dlbi, dbbr, dbbi))
            dlre_ref[...] = dlre
            dlim_ref[...] = dlim
            dbre_ref[...] = dbre
            dbim_ref[...] = dbim
            gsel = jnp.right_shift(lax.broadcasted_iota(jnp.int32, (S5_N, HEAD), 0), 6) == \
                lax.broadcasted_iota(jnp.int32, (S5_N, HEAD), 1)
            dldt_ref[...] = _dot_sel_rhs(dldt, gsel)

    full = lambda a: pl.BlockSpec(a.shape, lambda i: (0,) * a.ndim)
    rev = lambda w: pl.BlockSpec((R, w), lambda i: (nb - 1 - i, 0))
    prev8 = pl.BlockSpec((8, S5_N), lambda i: (jnp.maximum((nb - 1 - i) * (R // 8) - 1, 0), 0))
    small = [lre, lim, ldt, bre_t, bim_t, cre_w, cim_w, d_row, w_glu, b_glu]
    outs = [((tp, S5_W), rev(S5_W))] + [
        (s, pl.BlockSpec(s, lambda i: (0, 0))) for s in
        [(1, S5_N), (1, S5_N), (1, HEAD), (S5_H, S5_N), (S5_H, S5_N), (S5_W, S5_P), (S5_W, S5_P),
         (1, S5_W), (S5_W, S5_W), (1, S5_W)]]
    return _call(
        body, "s5_bwd", (nb,),
        [rev(S5_W), rev(S5_W), rev(S5_N), rev(S5_N), prev8, prev8] + [full(a) for a in small],
        [o[1] for o in outs], [jax.ShapeDtypeStruct(o[0], MM if n == 0 else F32) for n, o in enumerate(outs)],
        [pltpu.VMEM((S5_W, S5_N), MM)] * 4 + [
            pltpu.VMEM((8, 8, S5_N), F32), pltpu.VMEM((8, S5_N), F32), pltpu.VMEM((8, S5_N), F32),
            pltpu.VMEM((R, S5_N), F32), pltpu.VMEM((R, S5_N), F32),
            pltpu.VMEM((R + 8, S5_N), F32), pltpu.VMEM((R + 8, S5_N), F32)] + [pltpu.VMEM((S5_W, S5_N), F32)] * 4 + [
            pltpu.VMEM((8, S5_N), F32), pltpu.VMEM((8, S5_N), F32)],
        (dy_out, u, xr, xi, xr, xi, *small), jobs)


def _ret_fwd(q, k, v, dmat, zeta_b, xi_b, gam_b, jobs=()):
    tp = q.shape[0]
    C = dmat.shape[1]
    nc = tp // C

    def body(q_ref, k_ref, v_ref, dm_ref, ze_ref, xi_ref, ga_ref, o_ref, st_ref, s_ref):
        @pl.when(pl.program_id(0) == 0)
        def _():
            s_ref[...] = jnp.zeros_like(s_ref)

        for h in range(RET_H):
            sl = slice(h * HEAD, (h + 1) * HEAD)
            qh, kh, vh = q_ref[:, sl], k_ref[:, sl], v_ref[:, sl]
            sh = s_ref[h]
            st_ref[0, sl, :] = sh
            scores = _dot_nt(qh, kh) * dm_ref[h]
            o_ref[:, sl] = _dot(scores, vh) + _dot(qh, sh) * xi_ref[h]
            s_ref[h] = ga_ref[h] * sh + _dot_tn(kh.astype(F32) * ze_ref[h], vh)

    blk = pl.BlockSpec((C, RET_W), lambda c: (c, 0))
    cst = lambda a: pl.BlockSpec(a.shape, lambda c: (0, 0, 0))
    return _call(
        body, "ret_fwd", (nc,), [blk, blk, blk, cst(dmat), cst(zeta_b), cst(xi_b), cst(gam_b)],
        [blk, pl.BlockSpec((1, RET_W, HEAD), lambda c: (c, 0, 0))],
        [jax.ShapeDtypeStruct((tp, RET_W), F32), jax.ShapeDtypeStruct((nc, RET_W, HEAD), F32)],
        [pltpu.VMEM((RET_H, HEAD, HEAD), F32)], (q, k, v, dmat, zeta_b, xi_b, gam_b), jobs)


def _ret_bwd(q, k, v, do, states, cos2, sin2, dmat, zeta_b, xi_b, gam_b, jobs=()):
    tp = q.shape[0]
    C = dmat.shape[1]
    nc = tp // C

    def body(q_ref, k_ref, v_ref, do_ref, st_ref, cos_ref, sin_ref, dm_ref, ze_ref, xi_ref, ga_ref,
             dq_ref, dk_ref, dv_ref, ds_ref):
        @pl.when(pl.program_id(0) == 0)
        def _():
            ds_ref[...] = jnp.zeros_like(ds_ref)

        cos = cos_ref[...]
        sin = sin_ref[...]
        for h in range(RET_H):
            sl = slice(h * HEAD, (h + 1) * HEAD)
            qh, kh, vh = q_ref[:, sl], k_ref[:, sl], v_ref[:, sl]
            dmh = dm_ref[h]
            sh = st_ref[0, sl, :]
            dsn = ds_ref[h]
            doh = do_ref[:, sl]
            dox = doh * xi_ref[h]
            a = _dot_nt(qh, kh) * dmh
            dqk = _dot_nt(doh, vh) * dmh
            kz = kh.astype(F32) * ze_ref[h]
            dv_ref[:, sl] = (_dot_tn(a, doh) + _dot(kz, dsn)).astype(dv_ref.dtype)
            dqr = _dot(dqk, kh) + _dot_nt(dox, sh)
            dkr = _dot_tn(dqk, qh) + ze_ref[h] * _dot_nt(vh, dsn)
            ds_ref[h] = ga_ref[h] * dsn + _dot_tn(qh, dox)
            dq_ref[:, sl] = (dqr * cos - pltpu.roll(dqr, HEAD // 2, 1) * sin).astype(dq_ref.dtype)
            dk_ref[:, sl] = ((dkr * cos - pltpu.roll(dkr, HEAD // 2, 1) * sin) * (HEAD ** -0.5)).astype(dk_ref.dtype)

    blk = pl.BlockSpec((C, RET_W), lambda c: (nc - 1 - c, 0))
    tab = pl.BlockSpec((C, HEAD), lambda c: (nc - 1 - c, 0))
    cst = lambda a: pl.BlockSpec(a.shape, lambda c: (0, 0, 0))
    return _call(
        body, "ret_bwd", (nc,),
        [blk, blk, blk, blk, pl.BlockSpec((1, RET_W, HEAD), lambda c: (nc - 1 - c, 0, 0)), tab, tab,
         cst(dmat), cst(zeta_b), cst(xi_b), cst(gam_b)],
        [blk, blk, blk], [jax.ShapeDtypeStruct((tp, RET_W), MM)] * 3, [pltpu.VMEM((RET_H, HEAD, HEAD), F32)],
        (q, k, v, do, states, cos2, sin2, dmat, zeta_b, xi_b, gam_b), jobs)


def _gn_gate(o, gate, gn_g, gn_b):
    xhat, rstd = _ln_fwd(o, GN_EPS)
    on = xhat * gn_g + gn_b
    s = jax.nn.sigmoid(gate)
    return gate * s * on, xhat, rstd, on, s


def _post_up(o, gate, ys5, xhat0, gn_g, gn_b, li_g, li_b, l1_g, l1_b, w_out, w_up, jobs=()):
    tp = o.shape[0]
    R = ROW_BLK

    def body(o_ref, g_ref, ys_ref, xh0_ref, gng, gnb, lig, lib, l1g, l1b, wo_ref, wu_ref,
             ycat_ref, xh1_ref, rstd1_ref, h1b_ref, pre_ref):
        ycat_ref[:, 0:S5_W] = ys_ref[...].astype(ycat_ref.dtype)
        for h in range(RET_H):
            sl = slice(h * HEAD, (h + 1) * HEAD)
            yret = _gn_gate(o_ref[:, sl], g_ref[:, sl], gng[:, sl], gnb[:, sl])[0]
            ycat_ref[:, S5_W + h * HEAD:S5_W + (h + 1) * HEAD] = yret.astype(ycat_ref.dtype)
        mixed = _dot(ycat_ref[...], wo_ref[...])
        h0 = xh0_ref[...] * lig[...] + lib[...]
        xh1, rstd1 = _ln_fwd(ALPHA * h0 + mixed, LN_EPS)
        xh1_ref[...] = xh1
        rstd1_ref[...] = rstd1
        h1b = (xh1 * l1g[...] + l1b[...]).astype(MM)
        h1b_ref[...] = h1b
        for d in range(N_DEV):
            pre_ref[:, d * FF_BLK:(d + 1) * FF_BLK] = jnp.maximum(_dot(h1b, wu_ref[d]), 0.0)

    row = lambda w: pl.BlockSpec((R, w), lambda i: (i, 0))
    full = lambda a: pl.BlockSpec(a.shape, lambda i: (0,) * a.ndim)
    vecs = [gn_g, gn_b, li_g, li_b, l1_g, l1_b]
    outs = [(row(D_MODEL), jax.ShapeDtypeStruct((tp, D_MODEL), MM)), (row(D_MODEL), jax.ShapeDtypeStruct((tp, D_MODEL), F32)),
            (row(1), jax.ShapeDtypeStruct((tp, 1), F32)), (row(D_MODEL), jax.ShapeDtypeStruct((tp, D_MODEL), MM)),
            (row(D_FF), jax.ShapeDtypeStruct((tp, D_FF), F32))]
    return _call(
        body, "post_up", (tp // R,),
        [row(RET_W), row(RET_W), row(S5_W), row(D_MODEL)] + [full(a) for a in vecs] + [_VMEM, _VMEM],
        [o[0] for o in outs], [o[1] for o in outs], [], (o, gate, ys5, xhat0, *vecs, w_out, w_up), jobs)


def _post_down(pre, xhat1, tgt, l1_g, l1_b, l2_g, l2_b, w_down):
    tp = pre.shape[0]
    seq = tgt.shape[0]
    R = ROW_BLK

    def body(pre_ref, xh1_ref, ta, tb, tc, l1g, l1b, l2g, l2b, wd_ref,
             dr2_ref, dffb_ref, loss_ref, dl2g_ref, dl2b_ref, tgt_ref):
        i = pl.program_id(0)

        @pl.when(i == 0)
        def _():
            for ref in (loss_ref, dl2g_ref, dl2b_ref):
                ref[...] = jnp.zeros_like(ref)

        tgt_ref[0:CHUNK, :] = ta[...]
        tgt_ref[CHUNK:2 * CHUNK, :] = tb[...]
        tgt_ref[2 * CHUNK:3 * CHUNK, :] = tc[...]
        ff = jnp.zeros((R, D_MODEL), F32)
        for d in range(N_DEV):
            pre = pre_ref[:, d * FF_BLK:(d + 1) * FF_BLK]
            ff = ff + _dot(pre * pre, wd_ref[d * FF_BLK:(d + 1) * FF_BLK, :])
        h1 = xh1_ref[...] * l1g[...] + l1b[...]
        xh2, rstd2 = _ln_fwd(ALPHA * h1 + ff, LN_EPS)
        h2 = xh2 * l2g[...] + l2b[...]
        valid = (i * R + lax.broadcasted_iota(jnp.int32, (R, 1), 0)) >= CHUNK
        err = jnp.where(valid, h2 - tgt_ref[...], 0.0)
        loss_ref[...] += 0.5 * jnp.sum(err * err) / D_MODEL
        dh2 = err * (1.0 / D_MODEL)
        dl2g_ref[...] += _colsum(dh2 * xh2)
        dl2b_ref[...] += _colsum(dh2)
        dr2 = _ln_bwd(dh2 * l2g[...], xh2, rstd2)
        dr2_ref[...] = dr2
        dffb_ref[...] = dr2.astype(MM)

    row = lambda w: pl.BlockSpec((R, w), lambda i: (i, 0))
    full = lambda a: pl.BlockSpec(a.shape, lambda i: (0,) * a.ndim)
    vecs = [l1_g, l1_b, l2_g, l2_b]
    acc = lambda s: (pl.BlockSpec(s, lambda i: (0, 0)), jax.ShapeDtypeStruct(s, F32))
    outs = [(row(D_MODEL), jax.ShapeDtypeStruct((tp, D_MODEL), F32)), (row(D_MODEL), jax.ShapeDtypeStruct((tp, D_MODEL), MM)),
            acc((8, HEAD)), acc((1, D_MODEL)), acc((1, D_MODEL))]
    return pl.pallas_call(
        body, name="post_down", grid=(tp // R,),
        in_specs=[row(D_FF), row(D_MODEL)] + _shift3(seq // CHUNK) + [full(a) for a in vecs] + [_VMEM],
        out_specs=[o[0] for o in outs], out_shape=[o[1] for o in outs],
        scratch_shapes=[pltpu.VMEM((R, D_MODEL), F32)],
        compiler_params=_params(("arbitrary",)),
    )(pre, xhat1, tgt, tgt, tgt, *vecs, w_down)


def _mlp_bwd(h1b, dffb, pre, w_up, w_down):
    tp = h1b.shape[0]
    R = MLP_ROWS if tp % MLP_ROWS == 0 else ROW_BLK
    nr = tp // R

    def body(h_ref, df_ref, pre_ref, wu_ref, wd_ref, gup_ref, gdn_ref, dh1_ref, aup, adn):
        d = pl.program_id(0)
        r = pl.program_id(1)

        @pl.when(r == 0)
        def _():
            aup[...] = jnp.zeros_like(aup)
            adn[...] = jnp.zeros_like(adn)

        h = h_ref[...]
        df = df_ref[...]
        wu = wu_ref[0]
        wd = wd_ref[0]
        pre = pre_ref[...]
        dpre = (_dot_nt(df, wd) * (2.0 * pre)).astype(MM)

        aup[...] += _dot_tn(h, dpre)
        adn[...] += _dot_tn(pre * pre, df)
        contrib = _dot_nt(dpre, wu)
        rows = pl.ds(pl.multiple_of(r * R, 64), R)

        @pl.when(d == 0)
        def _():
            dh1_ref[rows, :] = contrib

        @pl.when(d > 0)
        def _():
            dh1_ref[rows, :] += contrib

        @pl.when(r == nr - 1)
        def _():
            gup_ref[0] = aup[...].astype(gup_ref.dtype)
            gdn_ref[0] = adn[...].astype(gdn_ref.dtype)

    return pl.pallas_call(
        body, name="mlp_bwd", grid=(N_DEV, nr),
        in_specs=[pl.BlockSpec((R, D_MODEL), lambda d, r: (r, 0)), pl.BlockSpec((R, D_MODEL), lambda d, r: (r, 0)),
                  pl.BlockSpec((R, FF_BLK), lambda d, r: (r, d)),
                  pl.BlockSpec((1, D_MODEL, FF_BLK), lambda d, r: (d, 0, 0)),
                  pl.BlockSpec((1, FF_BLK, D_MODEL), lambda d, r: (d, 0, 0))],
        out_specs=[pl.BlockSpec((1, D_MODEL, FF_BLK), lambda d, r: (d, 0, 0)),
                   pl.BlockSpec((1, FF_BLK, D_MODEL), lambda d, r: (d, 0, 0)), _VMEM],
        out_shape=[jax.ShapeDtypeStruct((N_DEV, D_MODEL, FF_BLK), MM), jax.ShapeDtypeStruct((N_DEV, FF_BLK, D_MODEL), MM),
                   jax.ShapeDtypeStruct((tp, D_MODEL), F32)],
        scratch_shapes=[pltpu.VMEM((D_MODEL, FF_BLK), F32), pltpu.VMEM((FF_BLK, D_MODEL), F32)],
        compiler_params=_params(("arbitrary", "arbitrary")),
    )(h1b, dffb, pre, w_up, w_down.reshape(N_DEV, FF_BLK, D_MODEL))


def _post_bwd(dh1m, dr2, xhat1, rstd1, ycat, o, gate, gn_g, gn_b, l1_g, w_out, jobs=()):
    tp = o.shape[0]
    R = ROW_BLK
    nb = tp // R

    def body(dm_ref, dr2_ref, xh1_ref, rs1_ref, yc_ref, o_ref, g_ref, gng, gnb, l1g, wo_ref,
             do_ref, dg_ref, dys_ref, dh0_ref, gwo_ref, dl1g_ref, dl1b_ref, dgng_ref, dgnb_ref, awo):
        i = pl.program_id(0)

        @pl.when(i == 0)
        def _():
            for ref in (awo, dl1g_ref, dl1b_ref, dgng_ref, dgnb_ref):
                ref[...] = jnp.zeros_like(ref)

        dh1 = dm_ref[...] + ALPHA * dr2_ref[...]
        xh1 = xh1_ref[...]
        dl1g_ref[...] += _colsum(dh1 * xh1)
        dl1b_ref[...] += _colsum(dh1)
        dr1 = _ln_bwd(dh1 * l1g[...], xh1, rs1_ref[...])
        dh0_ref[...] = ALPHA * dr1
        dmix = dr1.astype(MM)
        awo[...] += _dot_tn(yc_ref[...], dmix)
        dyc = _dot_nt(dmix, wo_ref[...])
        dys_ref[...] = dyc[:, 0:S5_W]
        for h in range(RET_H):
            sl = slice(h * HEAD, (h + 1) * HEAD)
            gt = g_ref[:, sl]
            _, xhat, rstd, on, s = _gn_gate(o_ref[:, sl], gt, gng[:, sl], gnb[:, sl])
            dyr = dyc[:, S5_W + h * HEAD:S5_W + (h + 1) * HEAD]
            dg_ref[:, sl] = (dyr * on * (s * (1.0 + gt * (1.0 - s)))).astype(dg_ref.dtype)
            don = dyr * gt * s
            dgng_ref[:, sl] += _colsum(don * xhat)
            dgnb_ref[:, sl] += _colsum(don)
            do_ref[:, sl] = _ln_bwd(don * gng[:, sl], xhat, rstd)

        @pl.when(i == nb - 1)
        def _():
            gwo_ref[...] = awo[...].astype(gwo_ref.dtype)

    row = lambda w: pl.BlockSpec((R, w), lambda i: (i, 0))
    full = lambda a: pl.BlockSpec(a.shape, lambda i: (0,) * a.ndim)
    acc = lambda s, dt=F32: (pl.BlockSpec(s, lambda i: (0, 0)), jax.ShapeDtypeStruct(s, dt))
    outs = [(row(RET_W), jax.ShapeDtypeStruct((tp, RET_W), F32)), (row(RET_W), jax.ShapeDtypeStruct((tp, RET_W), MM)),
            (row(S5_W), jax.ShapeDtypeStruct((tp, S5_W), F32)), (row(D_MODEL), jax.ShapeDtypeStruct((tp, D_MODEL), F32)),
            acc((D_MODEL, D_MODEL), MM), acc((1, D_MODEL)), acc((1, D_MODEL)), acc((1, RET_W)), acc((1, RET_W))]
    return _call(
        body, "post_bwd", (nb,),
        [row(D_MODEL), row(D_MODEL), row(D_MODEL), row(1), row(D_MODEL), row(RET_W), row(RET_W),
         full(gn_g), full(gn_b), full(l1_g), _VMEM],
        [o[0] for o in outs], [o[1] for o in outs],
        [pltpu.VMEM((D_MODEL, D_MODEL), F32)],
        (dh1m, dr2, xhat1, rstd1, ycat, o, gate, gn_g, gn_b, l1_g, w_out), jobs)


def _in_bwd(du, dq, dk, dv, dg, dh0r, xhat0, rstd0, li_g, li_b, w_int, jobs=()):
    tp = du.shape[0]
    R = PROJ_ROWS if tp % PROJ_ROWS == 0 else ROW_BLK
    nb = tp // R
    segs = [(0, S5_W)] + [(S5_W + n * RET_W, S5_W + (n + 1) * RET_W) for n in range(4)]

    def body(du_ref, dq_ref, dk_ref, dv_ref, dg_ref, dh0r_ref, xh_ref, rs_ref, lig, lib, w_ref,
             draw_ref, gw_ref, dlg_ref, dlb_ref, aw):
        i = pl.program_id(0)

        @pl.when(i == 0)
        def _():
            for ref in (aw, dlg_ref, dlb_ref):
                ref[...] = jnp.zeros_like(ref)

        valid = (i * R + lax.broadcasted_iota(jnp.int32, (R, 1), 0)) >= PAD
        xh = xh_ref[...]
        hb = (xh * lig[...] + lib[...]).astype(MM)
        dh0 = dh0r_ref[...]
        for (lo, hi), ref in zip(segs, (du_ref, dq_ref, dk_ref, dv_ref, dg_ref)):
            dseg = jnp.where(valid, ref[...], 0.0).astype(MM)
            dh0 = dh0 + _dot(dseg, w_ref[lo:hi, :])
            aw[lo:hi, :] += _dot_tn(dseg, hb)
        dlg_ref[...] += _colsum(dh0 * xh)
        dlb_ref[...] += _colsum(dh0)
        draw_ref[...] = _ln_bwd(dh0 * lig[...], xh, rs_ref[...])

        @pl.when(i == nb - 1)
        def _():
            gw_ref[...] = aw[...].astype(gw_ref.dtype)

    row = lambda w: pl.BlockSpec((R, w), lambda i: (i, 0))
    full = lambda a: pl.BlockSpec(a.shape, lambda i: (0,) * a.ndim)
    acc = lambda s, dt=F32: (pl.BlockSpec(s, lambda i: (0, 0)), jax.ShapeDtypeStruct(s, dt))
    outs = [(row(D_MODEL), jax.ShapeDtypeStruct((tp, D_MODEL), F32)), acc((PROJ_W, D_MODEL), MM),
            acc((1, D_MODEL)), acc((1, D_MODEL))]
    return _call(
        body, "in_bwd", (nb,),
        [row(S5_W), row(RET_W), row(RET_W), row(RET_W), row(RET_W), row(D_MODEL), row(D_MODEL), row(1),
         full(li_g), full(li_b), _VMEM],
        [o[0] for o in outs], [o[1] for o in outs], [pltpu.VMEM((PROJ_W, D_MODEL), F32)],
        (du, dq, dk, dv, dg, dh0r, xhat0, rstd0, li_g, li_b, w_int), jobs)


def _place():
    return lax.axis_index("x"), lax.axis_index("y"), lax.axis_index("c")


def _dma_sems(n):
    return pltpu.SemaphoreType.DMA((n,))


def _job_gather(shard):
    def parts(ins, outs, sems):
        (src,), (out,), (send_sems, recv_sems, local_sem) = ins, outs, sems
        x, y, c = _place()
        me, sib = (x, y, c), (x, y, 1 - c)
        chips = [(1 - x, y), (x, 1 - y), (1 - x, 1 - y)]

        def slot(dev):
            return out.at[4 * dev[0] + 2 * dev[1] + dev[2]]

        def copy(k, block, to, from_input=False):
            return pltpu.make_async_remote_copy(
                src_ref=src if from_input else slot(block), dst_ref=slot(block),
                send_sem=send_sems.at[k], recv_sem=recv_sems.at[k], device_id=to, device_id_type=_MESH)

        mine = pltpu.make_async_copy(src, slot(me), local_sem.at[0])
        first = [copy(0, me, sib, True)] + [copy(1 + j, me, (*chip, c), True) for j, chip in enumerate(chips)]
        return me, sib, chips, copy, mine, first

    def start(ins, outs, sems):
        _, _, _, _, mine, first = parts(ins, outs, sems)
        mine.start()
        for cp in first:
            cp.start()

    def finish(ins, outs, sems):
        me, sib, chips, copy, mine, first = parts(ins, outs, sems)
        c = me[2]
        passed = []
        for j, chip in enumerate(chips):
            copy(1 + j, (*chip, c), me).wait_recv()
            cp = copy(4 + j, (*chip, c), sib)
            cp.start()
            passed.append(cp)
        copy(0, sib, me).wait_recv()
        for j, chip in enumerate(chips):
            copy(4 + j, (*chip, 1 - c), me).wait_recv()
        for cp in first + passed:
            cp.wait_send()
        mine.wait()

    return dict(ins=[shard], outs=[jax.ShapeDtypeStruct((N_DEV,) + shard.shape, shard.dtype)],
                sems=[_dma_sems(7), _dma_sems(7), _dma_sems(1)], start=start, finish=finish)


def _job_pair(g):
    def copies(ins, outs, sems):
        x, y, c = _place()
        return [pltpu.make_async_remote_copy(
            src_ref=ins[0].at[2 * j + (1 - c)], dst_ref=outs[0].at[j], send_sem=sems[0].at[j], recv_sem=sems[1].at[j],
            device_id=(x, y, 1 - c), device_id_type=_MESH) for j in range(4)]

    def start(ins, outs, sems):
        for cp in copies(ins, outs, sems):
            cp.start()

    def finish(ins, outs, sems):
        for cp in copies(ins, outs, sems):
            cp.wait()

    return dict(ins=[g], outs=[jax.ShapeDtypeStruct((4,) + g.shape[1:], g.dtype)], sems=[_dma_sems(4), _dma_sems(4)],
                start=start, finish=finish)


def _job_chips(p):
    def copies(ins, outs, sems):
        x, y, c = _place()
        chips = [(1 - x, y), (x, 1 - y), (1 - x, 1 - y)]
        return [pltpu.make_async_remote_copy(
            src_ref=ins[0].at[2 * chip[0] + chip[1]], dst_ref=outs[0].at[k], send_sem=sems[0].at[k],
            recv_sem=sems[1].at[k], device_id=(*chip, c), device_id_type=_MESH) for k, chip in enumerate(chips)]

    def start(ins, outs, sems):
        for cp in copies(ins, outs, sems):
            cp.start()

    def finish(ins, outs, sems):
        for cp in copies(ins, outs, sems):
            cp.wait()

    return dict(ins=[p], outs=[jax.ShapeDtypeStruct((3,) + p.shape[1:], p.dtype)], sems=[_dma_sems(3), _dma_sems(3)],
                start=start, finish=finish)


def _split_job_refs(jobs, ins, outs, sems):
    res, a, b, c = [], 0, 0, 0
    for job in jobs:
        na, nb, nc = len(job["ins"]), len(job["outs"]), len(job["sems"])
        res.append((ins[a:a + na], outs[b:b + nb], sems[c:c + nc]))
        a, b, c = a + na, b + nb, c + nc
    return res


def _call(body, name, grid, in_specs, out_specs, out_shape, scratch, args, jobs=()):
    jobs = list(jobs)
    n_in, n_out, n_scr = len(in_specs), len(out_specs), len(scratch)
    j_in = [a for job in jobs for a in job["ins"]]
    j_out = [o for job in jobs for o in job["outs"]]
    j_scr = [s for job in jobs for s in job["sems"]]
    nsteps = grid[0]

    def wrapped(*refs):
        ins, jins = refs[:n_in], refs[n_in:n_in + len(j_in)]
        refs = refs[n_in + len(j_in):]
        outs, jouts = refs[:n_out], refs[n_out:n_out + len(j_out)]
        refs = refs[n_out + len(j_out):]
        scr, jscr = refs[:n_scr], refs[n_scr:]
        per_job = _split_job_refs(jobs, jins, jouts, jscr)

        @pl.when(pl.program_id(0) == 0)
        def _():
            for job, r in zip(jobs, per_job):
                job["start"](*r)

        body(*ins, *outs, *scr)

        @pl.when(pl.program_id(0) == nsteps - 1)
        def _():
            for job, r in zip(jobs, per_job):
                job["finish"](*r)

    res = pl.pallas_call(
        wrapped if jobs else body, name=name, grid=grid,
        in_specs=list(in_specs) + [_ANY] * len(j_in), out_specs=list(out_specs) + [_ANY] * len(j_out),
        out_shape=list(out_shape) + j_out, scratch_shapes=list(scratch) + j_scr,
        compiler_params=_params(("arbitrary",) * len(grid)),
    )(*args, *j_in)
    return list(res[:n_out]), list(res[n_out:])


def _exchange(jobs, name):
    j_in = [a for job in jobs for a in job["ins"]]
    j_out = [o for job in jobs for o in job["outs"]]
    j_scr = [s for job in jobs for s in job["sems"]]

    def body(*refs):
        per_job = _split_job_refs(jobs, refs[:len(j_in)], refs[len(j_in):len(j_in) + len(j_out)],
                                  refs[len(j_in) + len(j_out):])
        for job, r in zip(jobs, per_job):
            job["start"](*r)
        for job, r in zip(jobs, per_job):
            job["finish"](*r)

    return pl.pallas_call(body, name=name, out_shape=j_out, in_specs=[_ANY] * len(j_in), out_specs=[_ANY] * len(j_out),
                          scratch_shapes=j_scr)(*j_in)


def _pair_sum(gs, r1s, c_arr, name):
    n = len(gs)

    def body(c_ref, *refs):
        for a in range(n):
            refs[2 * n + a][...] = (refs[a][...].astype(F32) + refs[n + a][...].astype(F32)).astype(refs[2 * n + a].dtype)

    def blk(g, own):
        s = g.shape[1:]
        if own:
            return pl.BlockSpec((1,) + s, lambda j, c_ref: (2 * j + c_ref[0],) + (0,) * len(s))
        return pl.BlockSpec((1,) + s, lambda j, c_ref: (j,) + (0,) * len(s))

    return pl.pallas_call(
        body, name=name,
        grid_spec=pltpu.PrefetchScalarGridSpec(
            num_scalar_prefetch=1, grid=(4,),
            in_specs=[blk(g, True) for g in gs] + [blk(g, False) for g in gs],
            out_specs=[blk(g, False) for g in gs]),
        out_shape=[jax.ShapeDtypeStruct((4,) + g.shape[1:], g.dtype) for g in gs],
        compiler_params=_params(("arbitrary",)),
    )(c_arr, *gs, *r1s)


def _chip_sum(ps, r2s, j_arr, name):
    n = len(ps)

    def body(j_ref, *refs):
        for a in range(n):
            r2 = refs[n + a]
            refs[2 * n + a][...] = ((refs[a][0].astype(F32) + r2[0].astype(F32)) + r2[1].astype(F32)) + r2[2].astype(F32)

    def own(p):
        s = p.shape[1:]
        return pl.BlockSpec((1,) + s, lambda i, j_ref: (j_ref[0],) + (0,) * len(s))

    def whole(p):
        return pl.BlockSpec(p.shape, lambda i, j_ref: (0,) * p.ndim)

    return pl.pallas_call(
        body, name=name,
        grid_spec=pltpu.PrefetchScalarGridSpec(
            num_scalar_prefetch=1, grid=(1,),
            in_specs=[own(p) for p in ps] + [whole(r) for r in r2s],
            out_specs=[pl.BlockSpec(p.shape[1:], lambda i, j_ref: (0,) * (p.ndim - 1)) for p in ps]),
        out_shape=[jax.ShapeDtypeStruct(p.shape[1:], F32) for p in ps],
        compiler_params=_params(("arbitrary",)),
    )(j_arr, *ps, *r2s)


def _adamw_math(w, g, m, v):
    m = ADAM_B1 * m + (1.0 - ADAM_B1) * g
    v = ADAM_B2 * v + (1.0 - ADAM_B2) * (g * g)
    m_hat = m / (1.0 - ADAM_B1 ** ADAM_STEP)
    v_hat = v / (1.0 - ADAM_B2 ** ADAM_STEP)
    return -ADAM_LR * (m_hat / (jnp.sqrt(v_hat) + ADAM_EPS) + ADAM_WD * w), m, v


def _adamw(items, name, steps, jobs=()):
    n = len(items)

    def body(*refs):
        for a in range(n):
            g, w, m, v = (refs[4 * a + t][...] for t in range(4))
            d, m2, v2 = _adamw_math(w, g, m, v)
            refs[4 * n + 3 * a][...] = d
            refs[4 * n + 3 * a + 1][...] = m2
            refs[4 * n + 3 * a + 2][...] = v2

    def blk(arr):
        r, c = arr.shape
        return pl.BlockSpec((r // steps, c), lambda i: (i, 0))

    flat = [t for it in items for t in it]
    return _call(body, name, (steps,), [blk(t) for t in flat], [blk(it[1]) for it in items for _ in range(3)],
                 [jax.ShapeDtypeStruct(it[1].shape, F32) for it in items for _ in range(3)], [], flat, jobs)


def _adamw_small(gathered, w, m, v, name):
    def body(gs_ref, w_ref, m_ref, v_ref, g_ref, d_ref, m2_ref, v2_ref):
        g = gs_ref[0]
        for s in range(1, N_DEV):
            g = g + gs_ref[s]
        g_ref[...] = g
        d_ref[...], m2_ref[...], v2_ref[...] = _adamw_math(w_ref[...], g, m_ref[...], v_ref[...])

    return pl.pallas_call(
        body, name=name, out_shape=[jax.ShapeDtypeStruct(w.shape, F32)] * 4,
        in_specs=[_VMEM] * 4, out_specs=[_VMEM] * 4, compiler_params=_params(),
    )(gathered, w, m, v)


SMALL = ["ln_in_g", "ln_in_b", "s5_lambda_re", "s5_lambda_im", "s5_log_dt", "s5_b_re", "s5_b_im", "s5_c_re", "s5_c_im",
         "s5_d", "s5_b_glu", "ret_gn_g", "ret_gn_b", "ln1_g", "ln1_b", "ln2_g", "ln2_b"]
LATE = ["ln_in_g", "ln_in_b", "meta_tokens"]
EARLY = [n for n in SMALL if n not in LATE] + ["s5_w_glu", "loss"]
LANE = 128


def _pack(arrs):
    parts = []
    for a in arrs:
        f = a.reshape(-1)
        parts.append(jnp.pad(f, (0, (-f.shape[0]) % LANE)))
    flat = jnp.concatenate(parts)
    rows = -(-flat.shape[0] // LANE)
    flat = jnp.pad(flat, (0, (-rows % 8) * LANE + rows * LANE - flat.shape[0]))
    return flat.reshape(-1, LANE)


def _unpack(packed, shapes):
    flat = packed.reshape(-1)
    out, off = [], 0
    for s in shapes:
        n = math.prod(s)
        out.append(flat[off:off + n].reshape(s))
        off += n + (-n) % LANE
    return out


def _rope_tables(tp):
    pos = jnp.arange(tp, dtype=F32) - float(PAD)
    inv_freq = 1.0 / (ROPE_BASE ** (jnp.arange(0, HEAD, 2, dtype=F32) / HEAD))
    ang = pos[:, None] * inv_freq[None, :]
    cos, sin = jnp.cos(ang), jnp.sin(ang)
    return jnp.concatenate([cos, cos], axis=1), jnp.concatenate([-sin, sin], axis=1)


RET_CHUNK = ROW_BLK


def _decay_tables():
    log_gamma = jnp.log1p(-jnp.exp2(-5.0 - jnp.arange(RET_H, dtype=F32)))
    idx = jnp.arange(RET_CHUNK, dtype=F32)
    diff = idx[:, None] - idx[None, :]
    dmat = jnp.where(diff[None] >= 0, jnp.exp(jnp.maximum(diff, 0.0)[None] * log_gamma[:, None, None]), 0.0)
    zeta = jnp.exp((RET_CHUNK - 1.0 - idx)[None] * log_gamma[:, None])
    xi = jnp.exp((idx + 1.0)[None] * log_gamma[:, None])
    gam = jnp.exp(RET_CHUNK * log_gamma)
    wide = lambda t: jnp.broadcast_to(t[:, :, None], (RET_H, RET_CHUNK, HEAD))
    return dmat, wide(zeta), wide(xi), jnp.broadcast_to(gam[:, None, None], (RET_H, HEAD, HEAD))


def _local_step(x2d, tgt, meta_full, w_int, w_out, w_up, w_down, w_glu, sp, distributed):
    tp = x2d.shape[0] + CHUNK
    row = lambda a: a.reshape(1, -1)
    cos2, sin2 = _rope_tables(tp)
    dmat, zeta_b, xi_b, gam_b = _decay_tables()
    li_g, li_b = row(sp["ln_in_g"]), row(sp["ln_in_b"])
    l1_g, l1_b, l2_g, l2_b = row(sp["ln1_g"]), row(sp["ln1_b"]), row(sp["ln2_g"]), row(sp["ln2_b"])
    gn_g, gn_b = row(sp["ret_gn_g"]), row(sp["ret_gn_b"])
    lre, lim = row(sp["s5_lambda_re"]), row(sp["s5_lambda_im"])
    ldt = row(jnp.repeat(sp["s5_log_dt"].reshape(-1), S5_P))
    to_t = lambda b: b.reshape(S5_G, S5_P, S5_H).transpose(2, 0, 1).reshape(S5_H, S5_N)
    bre_t, bim_t = to_t(sp["s5_b_re"]), to_t(sp["s5_b_im"])
    to_w = lambda c: jnp.tile(c.reshape(S5_W, S5_P), (1, 2))
    cre_w, cim_w = to_w(sp["s5_c_re"]), to_w(sp["s5_c_im"])

    jobs = (lambda *j: list(j)) if distributed else (lambda *j: [])
    c_arr = jnp.reshape(lax.axis_index("c"), (1,)).astype(jnp.int32) if distributed else None
    (xhat0, rstd0), bg = _ln_in(x2d, meta_full, jobs(*([_job_gather(w_int), _job_gather(w_glu)] if distributed else [])))
    if distributed:
        w_int, w_glu = bg[0].reshape(PROJ_W, D_MODEL), bg[1].reshape(S5_W, S5_W)
    s5_small = (lre, lim, ldt, bre_t, bim_t, cre_w, cim_w, row(sp["s5_d"]), w_glu, row(sp["s5_b_glu"]))
    (u, q, k, v, gate), bg = _in_proj(xhat0, li_g, li_b, w_int, cos2, sin2,
                                      jobs(_job_gather(w_out) if distributed else None))
    if distributed:
        w_out = bg[0].reshape(D_MODEL, D_MODEL)
    (ys5, xr, xi), bg = _s5_fwd(u, *s5_small, jobs=jobs(_job_gather(w_up) if distributed else None))
    if distributed:
        w_up = bg[0]
    (o, states), _ = _ret_fwd(q, k, v, dmat, zeta_b, xi_b, gam_b)
    (ycat, xhat1, rstd1, h1b, pre), bg = _post_up(o, gate, ys5, xhat0, gn_g, gn_b, li_g, li_b, l1_g, l1_b, w_out, w_up,
                                                  jobs(_job_gather(w_down) if distributed else None))
    if distributed:
        w_down = bg[0].reshape(D_FF, D_MODEL)
    dr2, dffb, loss8, dl2g, dl2b = _post_down(pre, xhat1, tgt, l1_g, l1_b, l2_g, l2_b, w_down)
    g_up, g_down, dh1m = _mlp_bwd(h1b, dffb, pre, w_up, w_down)
    (do, dgate, dys5, dh0r, g_out, dl1g, dl1b, dgng, dgnb), bg = _post_bwd(
        dh1m, dr2, xhat1, rstd1, ycat, o, gate, gn_g, gn_b, l1_g, w_out,
        jobs(*([_job_pair(g_up), _job_pair(g_down)] if distributed else [])))
    g_out = g_out.reshape(N_DEV, D_MODEL // N_DEV, D_MODEL)
    if distributed:
        p_up, p_down = _pair_sum([g_up, g_down], bg, c_arr, "pair_sum_mlp")
    (du, dlre, dlim, dldt, dbre_t, dbim_t, dcre, dcim, dd, dwglu, dbglu), bg = _s5_bwd(
        dys5, u, xr, xi, *s5_small,
        jobs=jobs(*([_job_chips(p_up), _job_pair(g_out)] if distributed else [])))
    if distributed:
        r_up = bg[0]
        (p_out,) = _pair_sum([g_out], bg[1:], c_arr, "pair_sum_out")
    (dq, dk, dv), bg = _ret_bwd(q, k, v, do, states, cos2, sin2, dmat, zeta_b, xi_b, gam_b,
                                jobs(_job_chips(p_down) if distributed else None))
    r_down = bg[0] if distributed else None
    from_t = lambda t: t.reshape(S5_H, S5_G, S5_P).transpose(1, 2, 0)
    small = {
        "s5_lambda_re": dlre, "s5_lambda_im": dlim, "s5_log_dt": dldt[:, :S5_G],
        "s5_b_re": from_t(dbre_t), "s5_b_im": from_t(dbim_t), "s5_c_re": dcre, "s5_c_im": dcim, "s5_d": dd,
        "s5_b_glu": dbglu, "ret_gn_g": dgng, "ret_gn_b": dgnb, "ln1_g": dl1g, "ln1_b": dl1b, "ln2_g": dl2g, "ln2_b": dl2b,
        "s5_w_glu": dwglu, "loss": loss8[0:1, 0:1]}
    early_pack = _pack([small[n] for n in EARLY])
    (draw, g_int, dlig, dlib), bg = _in_bwd(du, dq, dk, dv, dgate, dh0r, xhat0, rstd0, li_g, li_b, w_int,
                                            jobs(*([_job_chips(p_out), _job_gather(early_pack)] if distributed else [])))
    small.update(ln_in_g=dlig, ln_in_b=dlib, meta_tokens=draw[PAD:CHUNK])
    g_int = g_int.reshape(N_DEV, PROJ_W // N_DEV, D_MODEL)
    if distributed:
        (r1_in,) = _exchange([_job_pair(g_int)], "exchange_pair_in")
        (p_in,) = _pair_sum([g_int], [r1_in], c_arr, "pair_sum_in")
        big = dict(chip_sums=[p_in, p_out, p_up, p_down], received=[None, bg[0], r_up, r_down], early=bg[1])
    else:
        big = dict(partials=[g_int, g_out, g_up, g_down])
    return draw, big, small


def kernel(x, meta_tokens, ln_in_g, ln_in_b, w_in, s5_lambda_re, s5_lambda_im, s5_log_dt, s5_b_re, s5_b_im, s5_c_re, s5_c_im, s5_d, s5_w_glu, s5_b_glu, ret_gn_g, ret_gn_b, w_out, ln1_g, ln1_b, w_up, w_down, ln2_g, ln2_b, loss_target, m_meta_tokens, m_ln_in_g, m_ln_in_b, m_w_in, m_s5_lambda_re, m_s5_lambda_im, m_s5_log_dt, m_s5_b_re, m_s5_b_im, m_s5_c_re, m_s5_c_im, m_s5_d, m_s5_w_glu, m_s5_b_glu, m_ret_gn_g, m_ret_gn_b, m_w_out, m_ln1_g, m_ln1_b, m_w_up, m_w_down, m_ln2_g, m_ln2_b, v_meta_tokens, v_ln_in_g, v_ln_in_b, v_w_in, v_s5_lambda_re, v_s5_lambda_im, v_s5_log_dt, v_s5_b_re, v_s5_b_im, v_s5_c_re, v_s5_c_im, v_s5_d, v_s5_w_glu, v_s5_b_glu, v_ret_gn_g, v_ret_gn_b, v_w_out, v_ln1_g, v_ln1_b, v_w_up, v_w_down, v_ln2_g, v_ln2_b):
    args = dict(locals())
    names = ["meta_tokens", "ln_in_g", "ln_in_b", "w_in", "s5_lambda_re", "s5_lambda_im", "s5_log_dt", "s5_b_re", "s5_b_im",
             "s5_c_re", "s5_c_im", "s5_d", "s5_w_glu", "s5_b_glu", "ret_gn_g", "ret_gn_b", "w_out", "ln1_g", "ln1_b",
             "w_up", "w_down", "ln2_g", "ln2_b"]
    ax, ay, ac = _place()
    me = 4 * ax + 2 * ay + ac

    (a_meta,) = _exchange([_job_gather(meta_tokens)], "gather_meta")
    meta_full = a_meta.transpose(1, 0, 2).reshape(N_META, D_MODEL)

    sp = {n: args[n] for n in SMALL}
    draw, big, small = _local_step(x[0], loss_target[0], meta_full, w_in[0].T.astype(MM), w_out[0].astype(MM),
                                   w_up[0].astype(MM), w_down[0].astype(MM), s5_w_glu[0].astype(MM), sp, True)

    j_arr = jnp.reshape(2 * ax + ay, (1,)).astype(jnp.int32)
    two_d = lambda a: a.reshape(a.shape[-2:])
    item = lambda n, g: tuple(two_d(t) for t in (g, args[n], args["m_" + n], args["v_" + n]))
    g_out, g_up, g_down = _chip_sum(big["chip_sums"][1:], big["received"][1:], j_arr, "chip_sum_mlp")
    shard_grads = {"w_out": g_out[None], "w_up": g_up[None], "w_down": g_down[None]}
    late_pack = _pack([small[n] for n in LATE])
    res, (r_in, late_all) = _adamw([item(n, shard_grads[n]) for n in ("w_out", "w_up", "w_down")], "adamw_mlp", 8,
                                   [_job_chips(big["chip_sums"][0]), _job_gather(late_pack)])
    (g_int,) = _chip_sum(big["chip_sums"][:1], [r_in], j_arr, "chip_sum_in")
    shard_grads["w_in"] = g_int.T[None]

    def small_update(order, shapes, gathered, placeholders, name):
        packs = [_pack([jnp.zeros(s, F32) if n in placeholders else args[p + n] for n, s in zip(order, shapes)])
                 for p in ("", "m_", "v_")]
        outs = _adamw_small(gathered, *packs, name)
        return [dict(zip(order, _unpack(o, shapes))) for o in outs]

    early_shapes = [args[n].shape for n in EARLY[:-2]] + [(S5_W, S5_W), (1,)]
    late_shapes = [args["ln_in_g"].shape, args["ln_in_b"].shape, (N_META, D_MODEL)]
    parts = [small_update(EARLY, early_shapes, big["early"], ("s5_w_glu", "loss"), "adamw_small_early"),
             small_update(LATE, late_shapes, late_all, ("meta_tokens",), "adamw_small_late")]
    g_small, d_small, m_small, v_small = ({**parts[0][t], **parts[1][t]} for t in range(4))
    loss = g_small["loss"].reshape(())

    shard_grads["meta_tokens"] = lax.dynamic_slice(g_small["meta_tokens"], (0, me * (D_MODEL // N_DEV)),
                                                   (N_META, D_MODEL // N_DEV))
    shard_grads["s5_w_glu"] = lax.dynamic_slice(g_small["s5_w_glu"], (me * (S5_W // N_DEV), 0),
                                                (S5_W // N_DEV, S5_W))[None]
    res_in, _ = _adamw([item("w_in", shard_grads["w_in"])], "adamw_in", 8)
    res2, _ = _adamw([item(n, shard_grads[n]) for n in ("meta_tokens", "s5_w_glu")], "adamw_shard_small", 1)
    upd = {"w_in": [r.reshape(args["w_in"].shape) for r in res_in]}
    for idx, n in enumerate(("w_out", "w_up", "w_down")):
        upd[n] = [r.reshape(args[n].shape) for r in res[3 * idx:3 * idx + 3]]
    for idx, n in enumerate(("meta_tokens", "s5_w_glu")):
        upd[n] = [r.reshape(args[n].shape) for r in res2[3 * idx:3 * idx + 3]]

    grads, deltas, new_m, new_v = [], [], [], []
    for n in names:
        if n in upd:
            grads.append(shard_grads[n].reshape(args[n].shape))
            d, m2, v2 = upd[n]
        else:
            grads.append(g_small[n])
            d, m2, v2 = d_small[n], m_small[n], v_small[n]
        deltas.append(d)
        new_m.append(m2)
        new_v.append(v2)
    grad_x = draw[CHUNK:][None]
    return (loss, grad_x, *grads, *deltas, *new_m, *new_v)
```

```python
import math

import jax
import jax.numpy as jnp
from jax import lax
from jax.experimental import pallas as pl
from jax.experimental.pallas import tpu as pltpu

F32 = jnp.float32
MM = jnp.bfloat16

D_MODEL = 1024
N_META = 16
CHUNK = 128
PAD = CHUNK - N_META
S5_W, S5_G, S5_H, S5_P = 256, 16, 16, 64
S5_N = S5_G * S5_P
RET_W, RET_H, HEAD = 768, 6, 128
D_FF = 4096
PROJ_W = S5_W + 4 * RET_W
N_DEV = 8
FF_BLK = D_FF // N_DEV
ROW_BLK = 384
MLP_ROWS = 1408
PROJ_ROWS = 704
ALPHA = 2.0 ** 0.25
LN_EPS = 1e-5
GN_EPS = 1e-5
ROPE_BASE = 10000.0
GELU_C = math.sqrt(2.0 / math.pi)
GELU_A = 0.044715
ADAM_LR, ADAM_B1, ADAM_B2, ADAM_EPS, ADAM_WD, ADAM_STEP = 0.001, 0.9, 0.999, 1e-08, 0.01, 10
VMEM_LIMIT = 60 * 1024 * 1024

_VMEM = pl.BlockSpec(memory_space=pltpu.VMEM)
_ANY = pl.BlockSpec(memory_space=pl.ANY)
_MESH = pl.DeviceIdType.MESH


def _params(sem=None):
    return pltpu.CompilerParams(dimension_semantics=sem, vmem_limit_bytes=VMEM_LIMIT)


def _dot(a, b):
    return jnp.dot(a.astype(MM), b.astype(MM), preferred_element_type=F32)


def _dot_nt(a, b):
    return lax.dot_general(a.astype(MM), b.astype(MM), (((1,), (1,)), ((), ())), preferred_element_type=F32)


def _dot_tn(a, b):
    return lax.dot_general(a.astype(MM), b.astype(MM), (((0,), (0,)), ((), ())), preferred_element_type=F32)


def _split3(a):
    hi = a.astype(jnp.bfloat16)
    r1 = a - hi.astype(F32)
    mid = r1.astype(jnp.bfloat16)
    lo = (r1 - mid.astype(F32)).astype(jnp.bfloat16)
    return hi, mid, lo


def _dot_sel_rhs(a, sel):
    s = sel.astype(jnp.bfloat16)
    return sum(jnp.dot(p, s, preferred_element_type=F32) for p in _split3(a))


def _dot_sel_lhs(sel, b):
    s = sel.astype(jnp.bfloat16)
    return sum(jnp.dot(s, p, preferred_element_type=F32) for p in _split3(b))


def _ln_fwd(r, eps):
    mu = jnp.mean(r, axis=-1, keepdims=True)
    xc = r - mu
    var = jnp.mean(xc * xc, axis=-1, keepdims=True)
    rstd = lax.rsqrt(var + eps)
    return xc * rstd, rstd


def _ln_bwd(dxhat, xhat, rstd):
    m1 = jnp.mean(dxhat, axis=-1, keepdims=True)
    m2 = jnp.mean(dxhat * xhat, axis=-1, keepdims=True)
    return rstd * (dxhat - m1 - xhat * m2)


def _colsum(a):
    return jnp.sum(a, axis=0, keepdims=True)


def _shift3(n_in):
    return [pl.BlockSpec((CHUNK, D_MODEL), (lambda i, j=j: (jnp.clip(3 * i - 1 + j, 0, n_in - 1), 0))) for j in range(3)]


def _ln_in(x2d, meta_full, jobs=()):
    seq = x2d.shape[0]
    tp = seq + CHUNK
    R = ROW_BLK

    def body(xa, xb, xc, meta_ref, xhat_ref, rstd_ref, raw_ref):
        raw_ref[0:CHUNK, :] = xa[...]
        raw_ref[CHUNK:2 * CHUNK, :] = xb[...]
        raw_ref[2 * CHUNK:3 * CHUNK, :] = xc[...]

        @pl.when(pl.program_id(0) == 0)
        def _():
            raw_ref[0:PAD, :] = jnp.zeros((PAD, D_MODEL), F32)
            raw_ref[PAD:CHUNK, :] = meta_ref[...]

        xhat_ref[...], rstd_ref[...] = _ln_fwd(raw_ref[...], LN_EPS)

    row = lambda w: pl.BlockSpec((R, w), lambda i: (i, 0))
    return _call(
        body, "ln_in", (tp // R,),
        _shift3(seq // CHUNK) + [pl.BlockSpec((N_META, D_MODEL), lambda i: (0, 0))],
        [row(D_MODEL), row(1)], [jax.ShapeDtypeStruct((tp, D_MODEL), F32), jax.ShapeDtypeStruct((tp, 1), F32)],
        [pltpu.VMEM((R, D_MODEL), F32)], (x2d, x2d, x2d, meta_full), jobs)


def _in_proj(xhat0, ln_g, ln_b, w_int, cos2, sin2, jobs=()):
    tp = xhat0.shape[0]
    R = PROJ_ROWS if tp % PROJ_ROWS == 0 else ROW_BLK

    def body(xh_ref, g_ref, b_ref, w_ref, cos_ref, sin_ref, u_ref, q_ref, k_ref, v_ref, gate_ref):
        hb = (xh_ref[...] * g_ref[...] + b_ref[...]).astype(MM)
        valid = (pl.program_id(0) * R + lax.broadcasted_iota(jnp.int32, (R, 1), 0)) >= PAD

        def seg(lo, hi):
            return jnp.where(valid, _dot_nt(hb, w_ref[lo:hi, :]), 0.0)

        u_ref[...] = seg(0, S5_W)
        cos = cos_ref[...]
        sin = sin_ref[...]
        q = seg(S5_W, S5_W + RET_W)
        k = seg(S5_W + RET_W, S5_W + 2 * RET_W)
        for h in range(RET_H):
            sl = slice(h * HEAD, (h + 1) * HEAD)
            qh = q[:, sl]
            kh = k[:, sl]
            q_ref[:, sl] = (qh * cos + pltpu.roll(qh, HEAD // 2, 1) * sin).astype(q_ref.dtype)
            k_ref[:, sl] = ((kh * cos + pltpu.roll(kh, HEAD // 2, 1) * sin) * (HEAD ** -0.5)).astype(k_ref.dtype)
        v_ref[...] = seg(S5_W + 2 * RET_W, S5_W + 3 * RET_W).astype(v_ref.dtype)
        gate_ref[...] = seg(S5_W + 3 * RET_W, PROJ_W)

    def rows(w, dt):
        return pl.BlockSpec((R, w), lambda i: (i, 0)), jax.ShapeDtypeStruct((tp, w), dt)

    outs = [rows(S5_W, F32), rows(RET_W, MM), rows(RET_W, MM), rows(RET_W, MM), rows(RET_W, F32)]
    full = lambda s: pl.BlockSpec(s, lambda i: (0,) * len(s))
    return _call(
        body, "in_proj", (tp // R,),
        [pl.BlockSpec((R, D_MODEL), lambda i: (i, 0)), full((1, D_MODEL)), full((1, D_MODEL)), _VMEM,
         pl.BlockSpec((R, HEAD), lambda i: (i, 0)), pl.BlockSpec((R, HEAD), lambda i: (i, 0))],
        [o[0] for o in outs], [o[1] for o in outs], [], (xhat0, ln_g, ln_b, w_int, cos2, sin2), jobs)


def _s5_disc(lre, lim, ldt, bre_t, bim_t):
    dt = jnp.exp(ldt)
    mag = jnp.exp(lre * dt)
    ang = lim * dt
    lbr = mag * jnp.cos(ang)
    lbi = mag * jnp.sin(ang)
    den = lre * lre + lim * lim
    nr = lbr - 1.0
    qr = (nr * lre + lbi * lim) / den
    qi = (lbi * lre - nr * lim) / den
    return lbr, lbi, qr * bre_t - qi * bim_t, qr * bim_t + qi * bre_t


def _s5_tables(lbr, lbi, reverse):
    if reverse:
        lbi = -lbi
    pw = [(lbr, lbi)]
    for _ in range(7):
        r, i = pw[-1]
        pw.append((r * lbr - i * lbi, r * lbi + i * lbr))
    row = lax.broadcasted_iota(jnp.int32, (8, S5_N), 0)
    tabs = []
    for k in range(3):
        sh = 2 ** k
        mask = (row < 8 - sh) if reverse else (row >= sh)
        ar, ai = pw[sh - 1]
        tabs.append((jnp.where(mask, ar, 0.0), jnp.where(mask, ai, 0.0)))
    pr = jnp.zeros((8, S5_N), F32)
    pi = jnp.zeros((8, S5_N), F32)
    for i in range(8):
        ar, ai = pw[7 - i] if reverse else pw[i]
        pr = jnp.where(row == i, ar, pr)
        pi = jnp.where(row == i, ai, pi)
    tabs.append((pr, pi))
    return tabs


def _store_tables(tab_ref, tabs):
    for k, (r, i) in enumerate(tabs):
        tab_ref[2 * k] = r
        tab_ref[2 * k + 1] = i


def _bd_mask():
    r = lax.broadcasted_iota(jnp.int32, (S5_W, S5_N), 0)
    c = lax.broadcasted_iota(jnp.int32, (S5_W, S5_N), 1)
    return jnp.right_shift(r, 4) == jnp.right_shift(c, 6)


def _s5_block_diag(bbr_t, bbi_t, cre_w, cim_w):
    mask = _bd_mask()
    bd = lambda t: jnp.where(mask, t, 0.0)
    return (bd(jnp.tile(bbr_t, (S5_G, 1))), bd(jnp.tile(bbi_t, (S5_G, 1))),
            bd(jnp.tile(cre_w, (1, S5_N // HEAD))), bd(jnp.tile(cim_w, (1, S5_N // HEAD))))


def _scan8(xr, xi, tab_ref, lanes, reverse):
    for k in range(3):
        sh = (8 - 2 ** k) if reverse else 2 ** k
        sr = pltpu.roll(xr, sh, 0)
        si = pltpu.roll(xi, sh, 0)
        mr = tab_ref[2 * k, :, lanes]
        mi = tab_ref[2 * k + 1, :, lanes]
        xr, xi = xr + (mr * sr - mi * si), xi + (mr * si + mi * sr)
    return xr, xi


S5_LANES = 256


def _gelu(y):
    t = jnp.tanh(GELU_C * (y + GELU_A * y * y * y))
    return 0.5 * y * (1.0 + t), t


def _s5_fwd(u, lre, lim, ldt, bre_t, bim_t, cre_w, cim_w, d_row, w_glu, b_glu, jobs=()):
    tp = u.shape[0]
    R = ROW_BLK

    def body(u_ref, lre_ref, lim_ref, ldt_ref, bre_ref, bim_ref, cre_ref, cim_ref, d_ref, wg_ref, bg_ref,
             y_ref, xr_ref, xi_ref, bbd_r, bbd_i, cbd_r, cbd_i, tab_ref, car_r, car_i):
        @pl.when(pl.program_id(0) == 0)
        def _():
            lbr, lbi, bbr, bbi = _s5_disc(lre_ref[...], lim_ref[...], ldt_ref[...], bre_ref[...], bim_ref[...])
            br, bi, cr, ci = _s5_block_diag(bbr, bbi, cre_ref[...], cim_ref[...])
            bbd_r[...] = br.astype(MM)
            bbd_i[...] = bi.astype(MM)
            cbd_r[...] = cr.astype(MM)
            cbd_i[...] = ci.astype(MM)
            _store_tables(tab_ref, _s5_tables(lbr, lbi, False))
            car_r[...] = jnp.zeros_like(car_r)
            car_i[...] = jnp.zeros_like(car_i)

        u = u_ref[...]
        ub = u.astype(MM)
        xr_ref[...] = jnp.dot(ub, bbd_r[...], preferred_element_type=F32)
        xi_ref[...] = jnp.dot(ub, bbd_i[...], preferred_element_type=F32)
        for j in range(S5_N // S5_LANES):
            lanes = pl.ds(j * S5_LANES, S5_LANES)
            pr = tab_ref[6, :, lanes]
            pi = tab_ref[7, :, lanes]

            def step(g, carry):
                cr, ci = carry
                rows = pl.ds(pl.multiple_of(g * 8, 8), 8)
                xr, xi = _scan8(xr_ref[rows, lanes], xi_ref[rows, lanes], tab_ref, lanes, False)
                br = jnp.broadcast_to(cr[7:8, :], cr.shape)
                bi = jnp.broadcast_to(ci[7:8, :], ci.shape)
                xr = xr + (pr * br - pi * bi)
                xi = xi + (pr * bi + pi * br)
                xr_ref[rows, lanes] = xr
                xi_ref[rows, lanes] = xi
                return xr, xi

            cr, ci = lax.fori_loop(0, R // 8, step, (car_r[:, lanes], car_i[:, lanes]), unroll=2)
            car_r[:, lanes] = cr
            car_i[:, lanes] = ci
        y = _dot_nt(xr_ref[...], cbd_r[...]) - _dot_nt(xi_ref[...], cbd_i[...]) + d_ref[...] * u
        yg, _ = _gelu(y)
        z = _dot(yg, wg_ref[...]) + bg_ref[...]
        y_ref[...] = yg * jax.nn.sigmoid(z)

    full = lambda a: pl.BlockSpec(a.shape, lambda i: (0,) * a.ndim)
    small = [lre, lim, ldt, bre_t, bim_t, cre_w, cim_w, d_row, w_glu, b_glu]
    return _call(
        body, "s5_fwd", (tp // R,),
        [pl.BlockSpec((R, S5_W), lambda i: (i, 0))] + [full(a) for a in small],
        [pl.BlockSpec((R, S5_W), lambda i: (i, 0)), pl.BlockSpec((R, S5_N), lambda i: (i, 0)),
         pl.BlockSpec((R, S5_N), lambda i: (i, 0))],
        [jax.ShapeDtypeStruct((tp, S5_W), F32), jax.ShapeDtypeStruct((tp, S5_N), F32),
         jax.ShapeDtypeStruct((tp, S5_N), F32)],
        [pltpu.VMEM((S5_W, S5_N), MM)] * 4 + [pltpu.VMEM((8, 8, S5_N), F32), pltpu.VMEM((8, S5_N), F32),
                                              pltpu.VMEM((8, S5_N), F32)],
        (u, *small), jobs)


def _s5_bwd(dy_out, u, xr, xi, lre, lim, ldt, bre_t, bim_t, cre_w, cim_w, d_row, w_glu, b_glu, jobs=()):
    tp = u.shape[0]
    R = ROW_BLK
    nb = tp // R

    def body(dyo_ref, u_ref, xr_ref, xi_ref, xpr_ref, xpi_ref,
             lre_ref, lim_ref, ldt_ref, bre_ref, bim_ref, cre_ref, cim_ref, d_ref, wg_ref, bg_ref,
             du_ref, dlre_ref, dlim_ref, dldt_ref, dbre_ref, dbim_ref, dcre_ref, dcim_ref, dd_ref, dwg_ref, dbg_ref,
             bbd_r, bbd_i, cbd_r, cbd_i, tab_ref, car_r, car_i, gr_ref, gi_ref, xer_ref, xei_ref,
             abr, abi, acr, aci, adr, adi):
        i = pl.program_id(0)

        @pl.when(i == 0)
        def _():
            lbr, lbi, bbr, bbi = _s5_disc(lre_ref[...], lim_ref[...], ldt_ref[...], bre_ref[...], bim_ref[...])
            br, bi, cr, ci = _s5_block_diag(bbr, bbi, cre_ref[...], cim_ref[...])
            bbd_r[...] = br.astype(MM)
            bbd_i[...] = bi.astype(MM)
            cbd_r[...] = cr.astype(MM)
            cbd_i[...] = ci.astype(MM)
            _store_tables(tab_ref, _s5_tables(lbr, lbi, True))
            for ref in (car_r, car_i, abr, abi, acr, aci, adr, adi, dd_ref, dwg_ref, dbg_ref):
                ref[...] = jnp.zeros_like(ref)

        u = u_ref[...]
        xrv = xr_ref[...]
        xiv = xi_ref[...]
        y = _dot_nt(xrv, cbd_r[...]) - _dot_nt(xiv, cbd_i[...]) + d_ref[...] * u
        yg, t = _gelu(y)
        z = _dot(yg, wg_ref[...]) + bg_ref[...]
        s = jax.nn.sigmoid(z)
        dout = dyo_ref[...]
        dz = dout * yg * s * (1.0 - s)
        dyg = dout * s + _dot_nt(dz, wg_ref[...])
        dwg_ref[...] += _dot_tn(yg, dz)
        dbg_ref[...] += _colsum(dz)
        dy = dyg * (0.5 * (1.0 + t) + 0.5 * y * (1.0 - t * t) * GELU_C * (1.0 + 3.0 * GELU_A * y * y))
        dd_ref[...] += _colsum(dy * u)
        acr[...] += _dot_tn(dy, xrv)
        aci[...] -= _dot_tn(dy, xiv)
        gr_ref[...] = _dot(dy, cbd_r[...])
        gi_ref[...] = -_dot(dy, cbd_i[...])
        has_prev = (i < nb - 1).astype(F32)
        xer_ref[0:8, :] = xpr_ref[...] * has_prev
        xei_ref[0:8, :] = xpi_ref[...] * has_prev
        xer_ref[8:R + 8, :] = xrv
        xei_ref[8:R + 8, :] = xiv
        row = lax.broadcasted_iota(jnp.int32, (8, S5_LANES), 0)
        for j in range(S5_N // S5_LANES):
            lanes = pl.ds(j * S5_LANES, S5_LANES)
            pr = tab_ref[6, :, lanes]
            pi = tab_ref[7, :, lanes]

            def step(n, carry):
                cr, ci, sar, sai = carry
                g = R // 8 - 1 - n
                r0 = pl.multiple_of(g * 8, 8)
                rows = pl.ds(r0, 8)
                gr, gi = _scan8(gr_ref[rows, lanes], gi_ref[rows, lanes], tab_ref, lanes, True)
                br = jnp.broadcast_to(cr[0:1, :], cr.shape)
                bi = jnp.broadcast_to(ci[0:1, :], ci.shape)
                gr = gr + (pr * br - pi * bi)
                gi = gi + (pr * bi + pi * br)
                gr_ref[rows, lanes] = gr
                gi_ref[rows, lanes] = gi
                last = row == 7
                xpr = pltpu.roll(jnp.where(last, xer_ref[rows, lanes], xer_ref[pl.ds(r0 + 8, 8), lanes]), 1, 0)
                xpi = pltpu.roll(jnp.where(last, xei_ref[rows, lanes], xei_ref[pl.ds(r0 + 8, 8), lanes]), 1, 0)
                return gr, gi, sar + (gr * xpr + gi * xpi), sai + (gi * xpr - gr * xpi)

            cr, ci, sar, sai = lax.fori_loop(
                0, R // 8, step, (car_r[:, lanes], car_i[:, lanes], adr[:, lanes], adi[:, lanes]), unroll=2)
            car_r[:, lanes] = cr
            car_i[:, lanes] = ci
            adr[:, lanes] = sar
            adi[:, lanes] = sai
        grv = gr_ref[...]
        giv = gi_ref[...]
        du_ref[...] = (dy * d_ref[...] + _dot_nt(grv, bbd_r[...]) + _dot_nt(giv, bbd_i[...])).astype(du_ref.dtype)
        abr[...] += _dot_tn(u, grv)
        abi[...] += _dot_tn(u, giv)

        @pl.when(i == nb - 1)
        def _():
            mask = _bd_mask()
            r16 = lax.broadcasted_iota(jnp.int32, (S5_H, S5_W), 1)
            h16 = lax.broadcasted_iota(jnp.int32, (S5_H, S5_W), 0)
            fold_b = jnp.bitwise_and(r16, S5_H - 1) == h16
            c64 = lax.broadcasted_iota(jnp.int32, (S5_N, S5_P), 0)
            p64 = lax.broadcasted_iota(jnp.int32, (S5_N, S5_P), 1)
            fold_c = jnp.bitwise_and(c64, S5_P - 1) == p64
            dbbr = _dot_sel_lhs(fold_b, jnp.where(mask, abr[...], 0.0))
            dbbi = _dot_sel_lhs(fold_b, jnp.where(mask, abi[...], 0.0))
            dcre_ref[...] = _dot_sel_rhs(jnp.where(mask, acr[...], 0.0), fold_c)
            dcim_ref[...] = _dot_sel_rhs(jnp.where(mask, aci[...], 0.0), fold_c)
            dlbr = _colsum(adr[...])
            dlbi = _colsum(adi[...])
            _, vjp = jax.vjp(_s5_disc, lre_ref[...], lim_ref[...], ldt_ref[...], bre_ref[...], bim_ref[...])
            dlre, dlim, dldt, dbre, dbim = vjp((dlbr, dlbi, dbbr, dbbi))
            dlre_ref[...] = dlre
            dlim_ref[...] = dlim
            dbre_ref[...] = dbre
            dbim_ref[...] = dbim
            gsel = jnp.right_shift(lax.broadcasted_iota(jnp.int32, (S5_N, HEAD), 0), 6) == \
                lax.broadcasted_iota(jnp.int32, (S5_N, HEAD), 1)
            dldt_ref[...] = _dot_sel_rhs(dldt, gsel)

    full = lambda a: pl.BlockSpec(a.shape, lambda i: (0,) * a.ndim)
    rev = lambda w: pl.BlockSpec((R, w), lambda i: (nb - 1 - i, 0))
    prev8 = pl.BlockSpec((8, S5_N), lambda i: (jnp.maximum((nb - 1 - i) * (R // 8) - 1, 0), 0))
    small = [lre, lim, ldt, bre_t, bim_t, cre_w, cim_w, d_row, w_glu, b_glu]
    outs = [((tp, S5_W), rev(S5_W))] + [
        (s, pl.BlockSpec(s, lambda i: (0, 0))) for s in
        [(1, S5_N), (1, S5_N), (1, HEAD), (S5_H, S5_N), (S5_H, S5_N), (S5_W, S5_P), (S5_W, S5_P),
         (1, S5_W), (S5_W, S5_W), (1, S5_W)]]
    return _call(
        body, "s5_bwd", (nb,),
        [rev(S5_W), rev(S5_W), rev(S5_N), rev(S5_N), prev8, prev8] + [full(a) for a in small],
        [o[1] for o in outs], [jax.ShapeDtypeStruct(o[0], MM if n == 0 else F32) for n, o in enumerate(outs)],
        [pltpu.VMEM((S5_W, S5_N), MM)] * 4 + [
            pltpu.VMEM((8, 8, S5_N), F32), pltpu.VMEM((8, S5_N), F32), pltpu.VMEM((8, S5_N), F32),
            pltpu.VMEM((R, S5_N), F32), pltpu.VMEM((R, S5_N), F32),
            pltpu.VMEM((R + 8, S5_N), F32), pltpu.VMEM((R + 8, S5_N), F32)] + [pltpu.VMEM((S5_W, S5_N), F32)] * 4 + [
            pltpu.VMEM((8, S5_N), F32), pltpu.VMEM((8, S5_N), F32)],
        (dy_out, u, xr, xi, xr, xi, *small), jobs)


def _ret_fwd(q, k, v, dmat, zeta_b, xi_b, gam_b, jobs=()):
    tp = q.shape[0]
    C = dmat.shape[1]
    nc = tp // C

    def body(q_ref, k_ref, v_ref, dm_ref, ze_ref, xi_ref, ga_ref, o_ref, st_ref, s_ref):
        @pl.when(pl.program_id(0) == 0)
        def _():
            s_ref[...] = jnp.zeros_like(s_ref)

        for h in range(RET_H):
            sl = slice(h * HEAD, (h + 1) * HEAD)
            qh, kh, vh = q_ref[:, sl], k_ref[:, sl], v_ref[:, sl]
            sh = s_ref[h]
            st_ref[0, sl, :] = sh
            scores = _dot_nt(qh, kh) * dm_ref[h]
            o_ref[:, sl] = _dot(scores, vh) + _dot(qh, sh) * xi_ref[h]
            s_ref[h] = ga_ref[h] * sh + _dot_tn(kh.astype(F32) * ze_ref[h], vh)

    blk = pl.BlockSpec((C, RET_W), lambda c: (c, 0))
    cst = lambda a: pl.BlockSpec(a.shape, lambda c: (0, 0, 0))
    return _call(
        body, "ret_fwd", (nc,), [blk, blk, blk, cst(dmat), cst(zeta_b), cst(xi_b), cst(gam_b)],
        [blk, pl.BlockSpec((1, RET_W, HEAD), lambda c: (c, 0, 0))],
        [jax.ShapeDtypeStruct((tp, RET_W), F32), jax.ShapeDtypeStruct((nc, RET_W, HEAD), F32)],
        [pltpu.VMEM((RET_H, HEAD, HEAD), F32)], (q, k, v, dmat, zeta_b, xi_b, gam_b), jobs)


def _ret_bwd(q, k, v, do, states, cos2, sin2, dmat, zeta_b, xi_b, gam_b, jobs=()):
    tp = q.shape[0]
    C = dmat.shape[1]
    nc = tp // C

    def body(q_ref, k_ref, v_ref, do_ref, st_ref, cos_ref, sin_ref, dm_ref, ze_ref, xi_ref, ga_ref,
             dq_ref, dk_ref, dv_ref, ds_ref):
        @pl.when(pl.program_id(0) == 0)
        def _():
            ds_ref[...] = jnp.zeros_like(ds_ref)

        cos = cos_ref[...]
        sin = sin_ref[...]
        for h in range(RET_H):
            sl = slice(h * HEAD, (h + 1) * HEAD)
            qh, kh, vh = q_ref[:, sl], k_ref[:, sl], v_ref[:, sl]
            dmh = dm_ref[h]
            sh = st_ref[0, sl, :]
            dsn = ds_ref[h]
            doh = do_ref[:, sl]
            dox = doh * xi_ref[h]
            a = _dot_nt(qh, kh) * dmh
            dqk = _dot_nt(doh, vh) * dmh
            kz = kh.astype(F32) * ze_ref[h]
            dv_ref[:, sl] = (_dot_tn(a, doh) + _dot(kz, dsn)).astype(dv_ref.dtype)
            dqr = _dot(dqk, kh) + _dot_nt(dox, sh)
            dkr = _dot_tn(dqk, qh) + ze_ref[h] * _dot_nt(vh, dsn)
            ds_ref[h] = ga_ref[h] * dsn + _dot_tn(qh, dox)
            dq_ref[:, sl] = (dqr * cos - pltpu.roll(dqr, HEAD // 2, 1) * sin).astype(dq_ref.dtype)
            dk_ref[:, sl] = ((dkr * cos - pltpu.roll(dkr, HEAD // 2, 1) * sin) * (HEAD ** -0.5)).astype(dk_ref.dtype)

    blk = pl.BlockSpec((C, RET_W), lambda c: (nc - 1 - c, 0))
    tab = pl.BlockSpec((C, HEAD), lambda c: (nc - 1 - c, 0))
    cst = lambda a: pl.BlockSpec(a.shape, lambda c: (0, 0, 0))
    return _call(
        body, "ret_bwd", (nc,),
        [blk, blk, blk, blk, pl.BlockSpec((1, RET_W, HEAD), lambda c: (nc - 1 - c, 0, 0)), tab, tab,
         cst(dmat), cst(zeta_b), cst(xi_b), cst(gam_b)],
        [blk, blk, blk], [jax.ShapeDtypeStruct((tp, RET_W), MM)] * 3, [pltpu.VMEM((RET_H, HEAD, HEAD), F32)],
        (q, k, v, do, states, cos2, sin2, dmat, zeta_b, xi_b, gam_b), jobs)


def _gn_gate(o, gate, gn_g, gn_b):
    xhat, rstd = _ln_fwd(o, GN_EPS)
    on = xhat * gn_g + gn_b
    s = jax.nn.sigmoid(gate)
    return gate * s * on, xhat, rstd, on, s


def _post_up(o, gate, ys5, xhat0, gn_g, gn_b, li_g, li_b, l1_g, l1_b, w_out, w_up, jobs=()):
    tp = o.shape[0]
    R = ROW_BLK

    def body(o_ref, g_ref, ys_ref, xh0_ref, gng, gnb, lig, lib, l1g, l1b, wo_ref, wu_ref,
             ycat_ref, xh1_ref, rstd1_ref, h1b_ref, pre_ref):
        ycat_ref[:, 0:S5_W] = ys_ref[...].astype(ycat_ref.dtype)
        for h in range(RET_H):
            sl = slice(h * HEAD, (h + 1) * HEAD)
            yret = _gn_gate(o_ref[:, sl], g_ref[:, sl], gng[:, sl], gnb[:, sl])[0]
            ycat_ref[:, S5_W + h * HEAD:S5_W + (h + 1) * HEAD] = yret.astype(ycat_ref.dtype)
        mixed = _dot(ycat_ref[...], wo_ref[...])
        h0 = xh0_ref[...] * lig[...] + lib[...]
        xh1, rstd1 = _ln_fwd(ALPHA * h0 + mixed, LN_EPS)
        xh1_ref[...] = xh1
        rstd1_ref[...] = rstd1
        h1b = (xh1 * l1g[...] + l1b[...]).astype(MM)
        h1b_ref[...] = h1b
        for d in range(N_DEV):
            pre_ref[:, d * FF_BLK:(d + 1) * FF_BLK] = jnp.maximum(_dot(h1b, wu_ref[d]), 0.0)

    row = lambda w: pl.BlockSpec((R, w), lambda i: (i, 0))
    full = lambda a: pl.BlockSpec(a.shape, lambda i: (0,) * a.ndim)
    vecs = [gn_g, gn_b, li_g, li_b, l1_g, l1_b]
    outs = [(row(D_MODEL), jax.ShapeDtypeStruct((tp, D_MODEL), MM)), (row(D_MODEL), jax.ShapeDtypeStruct((tp, D_MODEL), F32)),
            (row(1), jax.ShapeDtypeStruct((tp, 1), F32)), (row(D_MODEL), jax.ShapeDtypeStruct((tp, D_MODEL), MM)),
            (row(D_FF), jax.ShapeDtypeStruct((tp, D_FF), F32))]
    return _call(
        body, "post_up", (tp // R,),
        [row(RET_W), row(RET_W), row(S5_W), row(D_MODEL)] + [full(a) for a in vecs] + [_VMEM, _VMEM],
        [o[0] for o in outs], [o[1] for o in outs], [], (o, gate, ys5, xhat0, *vecs, w_out, w_up), jobs)


def _post_down(pre, xhat1, tgt, l1_g, l1_b, l2_g, l2_b, w_down):
    tp = pre.shape[0]
    seq = tgt.shape[0]
    R = ROW_BLK

    def body(pre_ref, xh1_ref, ta, tb, tc, l1g, l1b, l2g, l2b, wd_ref,
             dr2_ref, dffb_ref, loss_ref, dl2g_ref, dl2b_ref, tgt_ref):
        i = pl.program_id(0)

        @pl.when(i == 0)
        def _():
            for ref in (loss_ref, dl2g_ref, dl2b_ref):
                ref[...] = jnp.zeros_like(ref)

        tgt_ref[0:CHUNK, :] = ta[...]
        tgt_ref[CHUNK:2 * CHUNK, :] = tb[...]
        tgt_ref[2 * CHUNK:3 * CHUNK, :] = tc[...]
        ff = jnp.zeros((R, D_MODEL), F32)
        for d in range(N_DEV):
            pre = pre_ref[:, d * FF_BLK:(d + 1) * FF_BLK]
            ff = ff + _dot(pre * pre, wd_ref[d * FF_BLK:(d + 1) * FF_BLK, :])
        h1 = xh1_ref[...] * l1g[...] + l1b[...]
        xh2, rstd2 = _ln_fwd(ALPHA * h1 + ff, LN_EPS)
        h2 = xh2 * l2g[...] + l2b[...]
        valid = (i * R + lax.broadcasted_iota(jnp.int32, (R, 1), 0)) >= CHUNK
        err = jnp.where(valid, h2 - tgt_ref[...], 0.0)
        loss_ref[...] += 0.5 * jnp.sum(err * err) / D_MODEL
        dh2 = err * (1.0 / D_MODEL)
        dl2g_ref[...] += _colsum(dh2 * xh2)
        dl2b_ref[...] += _colsum(dh2)
        dr2 = _ln_bwd(dh2 * l2g[...], xh2, rstd2)
        dr2_ref[...] = dr2
        dffb_ref[...] = dr2.astype(MM)

    row = lambda w: pl.BlockSpec((R, w), lambda i: (i, 0))
    full = lambda a: pl.BlockSpec(a.shape, lambda i: (0,) * a.ndim)
    vecs = [l1_g, l1_b, l2_g, l2_b]
    acc = lambda s: (pl.BlockSpec(s, lambda i: (0, 0)), jax.ShapeDtypeStruct(s, F32))
    outs = [(row(D_MODEL), jax.ShapeDtypeStruct((tp, D_MODEL), F32)), (row(D_MODEL), jax.ShapeDtypeStruct((tp, D_MODEL), MM)),
            acc((8, HEAD)), acc((1, D_MODEL)), acc((1, D_MODEL))]
    return pl.pallas_call(
        body, name="post_down", grid=(tp // R,),
        in_specs=[row(D_FF), row(D_MODEL)] + _shift3(seq // CHUNK) + [full(a) for a in vecs] + [_VMEM],
        out_specs=[o[0] for o in outs], out_shape=[o[1] for o in outs],
        scratch_shapes=[pltpu.VMEM((R, D_MODEL), F32)],
        compiler_params=_params(("arbitrary",)),
    )(pre, xhat1, tgt, tgt, tgt, *vecs, w_down)


def _mlp_bwd(h1b, dffb, pre, w_up, w_down):
    tp = h1b.shape[0]
    R = MLP_ROWS if tp % MLP_ROWS == 0 else ROW_BLK
    nr = tp // R

    def body(h_ref, df_ref, pre_ref, wu_ref, wd_ref, gup_ref, gdn_ref, dh1_ref, aup, adn):
        d = pl.program_id(0)
        r = pl.program_id(1)

        @pl.when(r == 0)
        def _():
            aup[...] = jnp.zeros_like(aup)
            adn[...] = jnp.zeros_like(adn)

        h = h_ref[...]
        df = df_ref[...]
        wu = wu_ref[0]
        wd = wd_ref[0]
        pre = pre_ref[...]
        dpre = (_dot_nt(df, wd) * (2.0 * pre)).astype(MM)

        aup[...] += _dot_tn(h, dpre)
        adn[...] += _dot_tn(pre * pre, df)
        contrib = _dot_nt(dpre, wu)
        rows = pl.ds(pl.multiple_of(r * R, 64), R)

        @pl.when(d == 0)
        def _():
            dh1_ref[rows, :] = contrib

        @pl.when(d > 0)
        def _():
            dh1_ref[rows, :] += contrib

        @pl.when(r == nr - 1)
        def _():
            gup_ref[0] = aup[...].astype(gup_ref.dtype)
            gdn_ref[0] = adn[...].astype(gdn_ref.dtype)

    return pl.pallas_call(
        body, name="mlp_bwd", grid=(N_DEV, nr),
        in_specs=[pl.BlockSpec((R, D_MODEL), lambda d, r: (r, 0)), pl.BlockSpec((R, D_MODEL), lambda d, r: (r, 0)),
                  pl.BlockSpec((R, FF_BLK), lambda d, r: (r, d)),
                  pl.BlockSpec((1, D_MODEL, FF_BLK), lambda d, r: (d, 0, 0)),
                  pl.BlockSpec((1, FF_BLK, D_MODEL), lambda d, r: (d, 0, 0))],
        out_specs=[pl.BlockSpec((1, D_MODEL, FF_BLK), lambda d, r: (d, 0, 0)),
                   pl.BlockSpec((1, FF_BLK, D_MODEL), lambda d, r: (d, 0, 0)), _VMEM],
        out_shape=[jax.ShapeDtypeStruct((N_DEV, D_MODEL, FF_BLK), MM), jax.ShapeDtypeStruct((N_DEV, FF_BLK, D_MODEL), MM),
                   jax.ShapeDtypeStruct((tp, D_MODEL), F32)],
        scratch_shapes=[pltpu.VMEM((D_MODEL, FF_BLK), F32), pltpu.VMEM((FF_BLK, D_MODEL), F32)],
        compiler_params=_params(("arbitrary", "arbitrary")),
    )(h1b, dffb, pre, w_up, w_down.reshape(N_DEV, FF_BLK, D_MODEL))


def _post_bwd(dh1m, dr2, xhat1, rstd1, ycat, o, gate, gn_g, gn_b, l1_g, w_out, jobs=()):
    tp = o.shape[0]
    R = ROW_BLK
    nb = tp // R

    def body(dm_ref, dr2_ref, xh1_ref, rs1_ref, yc_ref, o_ref, g_ref, gng, gnb, l1g, wo_ref,
             do_ref, dg_ref, dys_ref, dh0_ref, gwo_ref, dl1g_ref, dl1b_ref, dgng_ref, dgnb_ref, awo):
        i = pl.program_id(0)

        @pl.when(i == 0)
        def _():
            for ref in (awo, dl1g_ref, dl1b_ref, dgng_ref, dgnb_ref):
                ref[...] = jnp.zeros_like(ref)

        dh1 = dm_ref[...] + ALPHA * dr2_ref[...]
        xh1 = xh1_ref[...]
        dl1g_ref[...] += _colsum(dh1 * xh1)
        dl1b_ref[...] += _colsum(dh1)
        dr1 = _ln_bwd(dh1 * l1g[...], xh1, rs1_ref[...])
        dh0_ref[...] = ALPHA * dr1
        dmix = dr1.astype(MM)
        awo[...] += _dot_tn(yc_ref[...], dmix)
        dyc = _dot_nt(dmix, wo_ref[...])
        dys_ref[...] = dyc[:, 0:S5_W]
        for h in range(RET_H):
            sl = slice(h * HEAD, (h + 1) * HEAD)
            gt = g_ref[:, sl]
            _, xhat, rstd, on, s = _gn_gate(o_ref[:, sl], gt, gng[:, sl], gnb[:, sl])
            dyr = dyc[:, S5_W + h * HEAD:S5_W + (h + 1) * HEAD]
            dg_ref[:, sl] = (dyr * on * (s * (1.0 + gt * (1.0 - s)))).astype(dg_ref.dtype)
            don = dyr * gt * s
            dgng_ref[:, sl] += _colsum(don * xhat)
            dgnb_ref[:, sl] += _colsum(don)
            do_ref[:, sl] = _ln_bwd(don * gng[:, sl], xhat, rstd)

        @pl.when(i == nb - 1)
        def _():
            gwo_ref[...] = awo[...].astype(gwo_ref.dtype)

    row = lambda w: pl.BlockSpec((R, w), lambda i: (i, 0))
    full = lambda a: pl.BlockSpec(a.shape, lambda i: (0,) * a.ndim)
    acc = lambda s, dt=F32: (pl.BlockSpec(s, lambda i: (0, 0)), jax.ShapeDtypeStruct(s, dt))
    outs = [(row(RET_W), jax.ShapeDtypeStruct((tp, RET_W), F32)), (row(RET_W), jax.ShapeDtypeStruct((tp, RET_W), MM)),
            (row(S5_W), jax.ShapeDtypeStruct((tp, S5_W), F32)), (row(D_MODEL), jax.ShapeDtypeStruct((tp, D_MODEL), F32)),
            acc((D_MODEL, D_MODEL), MM), acc((1, D_MODEL)), acc((1, D_MODEL)), acc((1, RET_W)), acc((1, RET_W))]
    return _call(
        body, "post_bwd", (nb,),
        [row(D_MODEL), row(D_MODEL), row(D_MODEL), row(1), row(D_MODEL), row(RET_W), row(RET_W),
         full(gn_g), full(gn_b), full(l1_g), _VMEM],
        [o[0] for o in outs], [o[1] for o in outs],
        [pltpu.VMEM((D_MODEL, D_MODEL), F32)],
        (dh1m, dr2, xhat1, rstd1, ycat, o, gate, gn_g, gn_b, l1_g, w_out), jobs)


def _in_bwd(du, dq, dk, dv, dg, dh0r, xhat0, rstd0, li_g, li_b, w_int, jobs=()):
    tp = du.shape[0]
    R = PROJ_ROWS if tp % PROJ_ROWS == 0 else ROW_BLK
    nb = tp // R
    segs = [(0, S5_W)] + [(S5_W + n * RET_W, S5_W + (n + 1) * RET_W) for n in range(4)]

    def body(du_ref, dq_ref, dk_ref, dv_ref, dg_ref, dh0r_ref, xh_ref, rs_ref, lig, lib, w_ref,
             draw_ref, gw_ref, dlg_ref, dlb_ref, aw):
        i = pl.program_id(0)

        @pl.when(i == 0)
        def _():
            for ref in (aw, dlg_ref, dlb_ref):
                ref[...] = jnp.zeros_like(ref)

        valid = (i * R + lax.broadcasted_iota(jnp.int32, (R, 1), 0)) >= PAD
        xh = xh_ref[...]
        hb = (xh * lig[...] + lib[...]).astype(MM)
        dh0 = dh0r_ref[...]
        for (lo, hi), ref in zip(segs, (du_ref, dq_ref, dk_ref, dv_ref, dg_ref)):
            dseg = jnp.where(valid, ref[...], 0.0).astype(MM)
            dh0 = dh0 + _dot(dseg, w_ref[lo:hi, :])
            aw[lo:hi, :] += _dot_tn(dseg, hb)
        dlg_ref[...] += _colsum(dh0 * xh)
        dlb_ref[...] += _colsum(dh0)
        draw_ref[...] = _ln_bwd(dh0 * lig[...], xh, rs_ref[...])

        @pl.when(i == nb - 1)
        def _():
            gw_ref[...] = aw[...].astype(gw_ref.dtype)

    row = lambda w: pl.BlockSpec((R, w), lambda i: (i, 0))
    full = lambda a: pl.BlockSpec(a.shape, lambda i: (0,) * a.ndim)
    acc = lambda s, dt=F32: (pl.BlockSpec(s, lambda i: (0, 0)), jax.ShapeDtypeStruct(s, dt))
    outs = [(row(D_MODEL), jax.ShapeDtypeStruct((tp, D_MODEL), F32)), acc((PROJ_W, D_MODEL), MM),
            acc((1, D_MODEL)), acc((1, D_MODEL))]
    return _call(
        body, "in_bwd", (nb,),
        [row(S5_W), row(RET_W), row(RET_W), row(RET_W), row(RET_W), row(D_MODEL), row(D_MODEL), row(1),
         full(li_g), full(li_b), _VMEM],
        [o[0] for o in outs], [o[1] for o in outs], [pltpu.VMEM((PROJ_W, D_MODEL), F32)],
        (du, dq, dk, dv, dg, dh0r, xhat0, rstd0, li_g, li_b, w_int), jobs)


def _place():
    return lax.axis_index("x"), lax.axis_index("y"), lax.axis_index("c")


def _dma_sems(n):
    return pltpu.SemaphoreType.DMA((n,))


def _job_gather(shard):
    def parts(ins, outs, sems):
        (src,), (out,), (send_sems, recv_sems, local_sem) = ins, outs, sems
        x, y, c = _place()
        north = c == 1
        me, sib = (x, y, c), (x, y, 1 - c)
        xn, yn, dg = (1 - x, y, c), (x, 1 - y, c), (1 - x, 1 - y, c)
        relay_from = (jnp.where(north, 1 - x, x), jnp.where(north, y, 1 - y), c)
        relay_to = (jnp.where(north, x, 1 - x), jnp.where(north, 1 - y, y), c)

        def slot(dev):
            return out.at[4 * dev[0] + 2 * dev[1] + dev[2]]

        def copy(k, block, to, from_input=False):
            return pltpu.make_async_remote_copy(
                src_ref=src if from_input else slot(block), dst_ref=slot(block),
                send_sem=send_sems.at[k], recv_sem=recv_sems.at[k], device_id=to, device_id_type=_MESH)

        mine = lambda: pltpu.make_async_copy(src, slot(me), local_sem.at[0])
        first = lambda: [copy(0, me, sib, True), copy(1, me, xn, True), copy(2, me, yn, True)]
        relayed = lambda: [copy(3, relay_from, relay_to), copy(4, xn, sib), copy(5, yn, sib)]
        return me, sib, xn, yn, dg, copy, mine, first, relayed

    def start(ins, outs, sems):
        mine, first = parts(ins, outs, sems)[6:8]
        mine().start()
        for cp in first():
            cp.start()

    def relay(ins, outs, sems):
        me, sib, xn, yn, dg, copy, mine, first, relayed = parts(ins, outs, sems)
        copy(1, xn, me).wait_recv()
        copy(2, yn, me).wait_recv()
        for cp in relayed():
            cp.start()

    def finish(ins, outs, sems):
        me, sib, xn, yn, dg, copy, mine, first, relayed = parts(ins, outs, sems)
        other = 1 - me[2]
        copy(3, dg, me).wait_recv()
        last = copy(6, dg, sib)
        last.start()
        copy(0, sib, me).wait_recv()
        for k, chip in ((4, xn), (5, yn), (6, dg)):
            copy(k, (chip[0], chip[1], other), me).wait_recv()
        for cp in first() + relayed() + [last]:
            cp.wait_send()
        mine().wait()

    return dict(ins=[shard], outs=[jax.ShapeDtypeStruct((N_DEV,) + shard.shape, shard.dtype)],
                sems=[_dma_sems(7), _dma_sems(7), _dma_sems(1)], start=start, middle=relay, finish=finish)


def _job_pair(g):
    def copies(ins, outs, sems):
        x, y, c = _place()
        return [pltpu.make_async_remote_copy(
            src_ref=ins[0].at[2 * j + (1 - c)], dst_ref=outs[0].at[j], send_sem=sems[0].at[j], recv_sem=sems[1].at[j],
            device_id=(x, y, 1 - c), device_id_type=_MESH) for j in range(4)]

    def start(ins, outs, sems):
        for cp in copies(ins, outs, sems):
            cp.start()

    def finish(ins, outs, sems):
        for cp in copies(ins, outs, sems):
            cp.wait()

    return dict(ins=[g], outs=[jax.ShapeDtypeStruct((4,) + g.shape[1:], g.dtype)], sems=[_dma_sems(4), _dma_sems(4)],
                start=start, finish=finish)


def _job_chips(p):
    def copies(ins, outs, sems):
        x, y, c = _place()
        chips = [(1 - x, y), (x, 1 - y), (1 - x, 1 - y)]
        return [pltpu.make_async_remote_copy(
            src_ref=ins[0].at[2 * chip[0] + chip[1]], dst_ref=outs[0].at[k], send_sem=sems[0].at[k],
            recv_sem=sems[1].at[k], device_id=(*chip, c), device_id_type=_MESH) for k, chip in enumerate(chips)]

    def start(ins, outs, sems):
        for cp in copies(ins, outs, sems):
            cp.start()

    def finish(ins, outs, sems):
        for cp in copies(ins, outs, sems):
            cp.wait()

    return dict(ins=[p], outs=[jax.ShapeDtypeStruct((3,) + p.shape[1:], p.dtype)], sems=[_dma_sems(3), _dma_sems(3)],
                start=start, finish=finish)


def _split_job_refs(jobs, ins, outs, sems):
    res, a, b, c = [], 0, 0, 0
    for job in jobs:
        na, nb, nc = len(job["ins"]), len(job["outs"]), len(job["sems"])
        res.append((ins[a:a + na], outs[b:b + nb], sems[c:c + nc]))
        a, b, c = a + na, b + nb, c + nc
    return res


def _call(body, name, grid, in_specs, out_specs, out_shape, scratch, args, jobs=()):
    jobs = list(jobs)
    n_in, n_out, n_scr = len(in_specs), len(out_specs), len(scratch)
    j_in = [a for job in jobs for a in job["ins"]]
    j_out = [o for job in jobs for o in job["outs"]]
    j_scr = [s for job in jobs for s in job["sems"]]
    nsteps = grid[0]

    def wrapped(*refs):
        ins, jins = refs[:n_in], refs[n_in:n_in + len(j_in)]
        refs = refs[n_in + len(j_in):]
        outs, jouts = refs[:n_out], refs[n_out:n_out + len(j_out)]
        refs = refs[n_out + len(j_out):]
        scr, jscr = refs[:n_scr], refs[n_scr:]
        per_job = _split_job_refs(jobs, jins, jouts, jscr)

        def middle():
            for job, r in zip(jobs, per_job):
                if "middle" in job:
                    job["middle"](*r)

        @pl.when(pl.program_id(0) == 0)
        def _():
            for job, r in zip(jobs, per_job):
                job["start"](*r)

        if nsteps >= 3:
            pl.when(pl.program_id(0) == nsteps // 2)(middle)

        body(*ins, *outs, *scr)

        @pl.when(pl.program_id(0) == nsteps - 1)
        def _():
            if nsteps < 3:
                middle()
            for job, r in zip(jobs, per_job):
                job["finish"](*r)

    res = pl.pallas_call(
        wrapped if jobs else body, name=name, grid=grid,
        in_specs=list(in_specs) + [_ANY] * len(j_in), out_specs=list(out_specs) + [_ANY] * len(j_out),
        out_shape=list(out_shape) + j_out, scratch_shapes=list(scratch) + j_scr,
        compiler_params=_params(("arbitrary",) * len(grid)),
    )(*args, *j_in)
    return list(res[:n_out]), list(res[n_out:])


def _exchange(jobs, name):
    j_in = [a for job in jobs for a in job["ins"]]
    j_out = [o for job in jobs for o in job["outs"]]
    j_scr = [s for job in jobs for s in job["sems"]]

    def body(*refs):
        per_job = _split_job_refs(jobs, refs[:len(j_in)], refs[len(j_in):len(j_in) + len(j_out)],
                                  refs[len(j_in) + len(j_out):])
        for phase in ("start", "middle", "finish"):
            for job, r in zip(jobs, per_job):
                if phase in job:
                    job[phase](*r)

    return pl.pallas_call(body, name=name, out_shape=j_out, in_specs=[_ANY] * len(j_in), out_specs=[_ANY] * len(j_out),
                          scratch_shapes=j_scr)(*j_in)


def _pair_sum(gs, r1s, c_arr, name):
    n = len(gs)

    def body(c_ref, *refs):
        for a in range(n):
            refs[2 * n + a][...] = (refs[a][...].astype(F32) + refs[n + a][...].astype(F32)).astype(refs[2 * n + a].dtype)

    def blk(g, own):
        s = g.shape[1:]
        if own:
            return pl.BlockSpec((1,) + s, lambda j, c_ref: (2 * j + c_ref[0],) + (0,) * len(s))
        return pl.BlockSpec((1,) + s, lambda j, c_ref: (j,) + (0,) * len(s))

    return pl.pallas_call(
        body, name=name,
        grid_spec=pltpu.PrefetchScalarGridSpec(
            num_scalar_prefetch=1, grid=(4,),
            in_specs=[blk(g, True) for g in gs] + [blk(g, False) for g in gs],
            out_specs=[blk(g, False) for g in gs]),
        out_shape=[jax.ShapeDtypeStruct((4,) + g.shape[1:], g.dtype) for g in gs],
        compiler_params=_params(("arbitrary",)),
    )(c_arr, *gs, *r1s)


def _chip_sum(ps, r2s, j_arr, name):
    n = len(ps)

    def body(j_ref, *refs):
        for a in range(n):
            r2 = refs[n + a]
            refs[2 * n + a][...] = ((refs[a][0].astype(F32) + r2[0].astype(F32)) + r2[1].astype(F32)) + r2[2].astype(F32)

    def own(p):
        s = p.shape[1:]
        return pl.BlockSpec((1,) + s, lambda i, j_ref: (j_ref[0],) + (0,) * len(s))

    def whole(p):
        return pl.BlockSpec(p.shape, lambda i, j_ref: (0,) * p.ndim)

    return pl.pallas_call(
        body, name=name,
        grid_spec=pltpu.PrefetchScalarGridSpec(
            num_scalar_prefetch=1, grid=(1,),
            in_specs=[own(p) for p in ps] + [whole(r) for r in r2s],
            out_specs=[pl.BlockSpec(p.shape[1:], lambda i, j_ref: (0,) * (p.ndim - 1)) for p in ps]),
        out_shape=[jax.ShapeDtypeStruct(p.shape[1:], F32) for p in ps],
        compiler_params=_params(("arbitrary",)),
    )(j_arr, *ps, *r2s)


def _adamw_math(w, g, m, v):
    m = ADAM_B1 * m + (1.0 - ADAM_B1) * g
    v = ADAM_B2 * v + (1.0 - ADAM_B2) * (g * g)
    m_hat = m / (1.0 - ADAM_B1 ** ADAM_STEP)
    v_hat = v / (1.0 - ADAM_B2 ** ADAM_STEP)
    return -ADAM_LR * (m_hat / (jnp.sqrt(v_hat) + ADAM_EPS) + ADAM_WD * w), m, v


def _adamw(items, name, steps, jobs=()):
    n = len(items)

    def body(*refs):
        for a in range(n):
            g, w, m, v = (refs[4 * a + t][...] for t in range(4))
            d, m2, v2 = _adamw_math(w, g, m, v)
            refs[4 * n + 3 * a][...] = d
            refs[4 * n + 3 * a + 1][...] = m2
            refs[4 * n + 3 * a + 2][...] = v2

    def blk(arr):
        r, c = arr.shape
        return pl.BlockSpec((r // steps, c), lambda i: (i, 0))

    flat = [t for it in items for t in it]
    return _call(body, name, (steps,), [blk(t) for t in flat], [blk(it[1]) for it in items for _ in range(3)],
                 [jax.ShapeDtypeStruct(it[1].shape, F32) for it in items for _ in range(3)], [], flat, jobs)


def _sum_devices(gathered, name):
    def body(gs_ref, g_ref):
        g = gs_ref[0]
        for s in range(1, N_DEV):
            g = g + gs_ref[s]
        g_ref[...] = g

    return pl.pallas_call(body, name=name, out_shape=jax.ShapeDtypeStruct(gathered.shape[1:], F32),
                          in_specs=[_VMEM], out_specs=_VMEM, compiler_params=_params())(gathered)


def _adamw_native(items, name):
    n = len(items)

    def body(*refs):
        for a in range(n):
            g, w, m, v = (refs[4 * a + t][...] for t in range(4))
            refs[4 * n + 3 * a][...], refs[4 * n + 3 * a + 1][...], refs[4 * n + 3 * a + 2][...] = _adamw_math(w, g, m, v)

    return pl.pallas_call(
        body, name=name, out_shape=[jax.ShapeDtypeStruct(it[1].shape, F32) for it in items for _ in range(3)],
        in_specs=[_VMEM] * (4 * n), out_specs=[_VMEM] * (3 * n), compiler_params=_params(),
    )(*[t for it in items for t in it])


SMALL = ["ln_in_g", "ln_in_b", "s5_lambda_re", "s5_lambda_im", "s5_log_dt", "s5_b_re", "s5_b_im", "s5_c_re", "s5_c_im",
         "s5_d", "s5_b_glu", "ret_gn_g", "ret_gn_b", "ln1_g", "ln1_b", "ln2_g", "ln2_b"]
LATE = ["ln_in_g", "ln_in_b", "meta_tokens"]
EARLY = [n for n in SMALL if n not in LATE] + ["s5_w_glu", "loss"]
LANE = 128


def _pack(arrs):
    parts = []
    for a in arrs:
        f = a.reshape(-1)
        parts.append(jnp.pad(f, (0, (-f.shape[0]) % LANE)))
    flat = jnp.concatenate(parts)
    rows = -(-flat.shape[0] // LANE)
    flat = jnp.pad(flat, (0, (-rows % 8) * LANE + rows * LANE - flat.shape[0]))
    return flat.reshape(-1, LANE)


def _unpack(packed, shapes):
    flat = packed.reshape(-1)
    out, off = [], 0
    for s in shapes:
        n = math.prod(s)
        out.append(flat[off:off + n].reshape(s))
        off += n + (-n) % LANE
    return out


def _rope_tables(tp):
    pos = jnp.arange(tp, dtype=F32) - float(PAD)
    inv_freq = 1.0 / (ROPE_BASE ** (jnp.arange(0, HEAD, 2, dtype=F32) / HEAD))
    ang = pos[:, None] * inv_freq[None, :]
    cos, sin = jnp.cos(ang), jnp.sin(ang)
    return jnp.concatenate([cos, cos], axis=1), jnp.concatenate([-sin, sin], axis=1)


RET_CHUNK = ROW_BLK


def _decay_tables():
    log_gamma = jnp.log1p(-jnp.exp2(-5.0 - jnp.arange(RET_H, dtype=F32)))
    idx = jnp.arange(RET_CHUNK, dtype=F32)
    diff = idx[:, None] - idx[None, :]
    dmat = jnp.where(diff[None] >= 0, jnp.exp(jnp.maximum(diff, 0.0)[None] * log_gamma[:, None, None]), 0.0)
    zeta = jnp.exp((RET_CHUNK - 1.0 - idx)[None] * log_gamma[:, None])
    xi = jnp.exp((idx + 1.0)[None] * log_gamma[:, None])
    gam = jnp.exp(RET_CHUNK * log_gamma)
    wide = lambda t: jnp.broadcast_to(t[:, :, None], (RET_H, RET_CHUNK, HEAD))
    return dmat, wide(zeta), wide(xi), jnp.broadcast_to(gam[:, None, None], (RET_H, HEAD, HEAD))


def _local_step(x2d, tgt, meta_full, w_int, w_out, w_up, w_down, w_glu, sp, distributed):
    tp = x2d.shape[0] + CHUNK
    row = lambda a: a.reshape(1, -1)
    cos2, sin2 = _rope_tables(tp)
    dmat, zeta_b, xi_b, gam_b = _decay_tables()
    li_g, li_b = row(sp["ln_in_g"]), row(sp["ln_in_b"])
    l1_g, l1_b, l2_g, l2_b = row(sp["ln1_g"]), row(sp["ln1_b"]), row(sp["ln2_g"]), row(sp["ln2_b"])
    gn_g, gn_b = row(sp["ret_gn_g"]), row(sp["ret_gn_b"])
    lre, lim = row(sp["s5_lambda_re"]), row(sp["s5_lambda_im"])
    ldt = row(jnp.repeat(sp["s5_log_dt"].reshape(-1), S5_P))
    to_t = lambda b: b.reshape(S5_G, S5_P, S5_H).transpose(2, 0, 1).reshape(S5_H, S5_N)
    bre_t, bim_t = to_t(sp["s5_b_re"]), to_t(sp["s5_b_im"])
    to_w = lambda c: jnp.tile(c.reshape(S5_W, S5_P), (1, 2))
    cre_w, cim_w = to_w(sp["s5_c_re"]), to_w(sp["s5_c_im"])

    jobs = (lambda *j: list(j)) if distributed else (lambda *j: [])
    c_arr = jnp.reshape(lax.axis_index("c"), (1,)).astype(jnp.int32) if distributed else None
    (xhat0, rstd0), bg = _ln_in(x2d, meta_full, jobs(*([_job_gather(w_int), _job_gather(w_glu)] if distributed else [])))
    if distributed:
        w_int, w_glu = bg[0].reshape(PROJ_W, D_MODEL), bg[1].reshape(S5_W, S5_W)
    s5_small = (lre, lim, ldt, bre_t, bim_t, cre_w, cim_w, row(sp["s5_d"]), w_glu, row(sp["s5_b_glu"]))
    (u, q, k, v, gate), bg = _in_proj(xhat0, li_g, li_b, w_int, cos2, sin2,
                                      jobs(_job_gather(w_out) if distributed else None))
    if distributed:
        w_out = bg[0].reshape(D_MODEL, D_MODEL)
    (ys5, xr, xi), bg = _s5_fwd(u, *s5_small, jobs=jobs(_job_gather(w_up) if distributed else None))
    if distributed:
        w_up = bg[0]
    (o, states), _ = _ret_fwd(q, k, v, dmat, zeta_b, xi_b, gam_b)
    (ycat, xhat1, rstd1, h1b, pre), bg = _post_up(o, gate, ys5, xhat0, gn_g, gn_b, li_g, li_b, l1_g, l1_b, w_out, w_up,
                                                  jobs(_job_gather(w_down) if distributed else None))
    if distributed:
        w_down = bg[0].reshape(D_FF, D_MODEL)
    dr2, dffb, loss8, dl2g, dl2b = _post_down(pre, xhat1, tgt, l1_g, l1_b, l2_g, l2_b, w_down)
    g_up, g_down, dh1m = _mlp_bwd(h1b, dffb, pre, w_up, w_down)
    (do, dgate, dys5, dh0r, g_out, dl1g, dl1b, dgng, dgnb), bg = _post_bwd(
        dh1m, dr2, xhat1, rstd1, ycat, o, gate, gn_g, gn_b, l1_g, w_out,
        jobs(*([_job_pair(g_up), _job_pair(g_down)] if distributed else [])))
    g_out = g_out.reshape(N_DEV, D_MODEL // N_DEV, D_MODEL)
    if distributed:
        p_up, p_down = _pair_sum([g_up, g_down], bg, c_arr, "pair_sum_mlp")
    (du, dlre, dlim, dldt, dbre_t, dbim_t, dcre, dcim, dd, dwglu, dbglu), bg = _s5_bwd(
        dys5, u, xr, xi, *s5_small,
        jobs=jobs(*([_job_chips(p_up), _job_pair(g_out)] if distributed else [])))
    if distributed:
        r_up = bg[0]
        (p_out,) = _pair_sum([g_out], bg[1:], c_arr, "pair_sum_out")
    (dq, dk, dv), bg = _ret_bwd(q, k, v, do, states, cos2, sin2, dmat, zeta_b, xi_b, gam_b,
                                jobs(_job_chips(p_down) if distributed else None))
    r_down = bg[0] if distributed else None
    from_t = lambda t: t.reshape(S5_H, S5_G, S5_P).transpose(1, 2, 0)
    small = {
        "s5_lambda_re": dlre, "s5_lambda_im": dlim, "s5_log_dt": dldt[:, :S5_G],
        "s5_b_re": from_t(dbre_t), "s5_b_im": from_t(dbim_t), "s5_c_re": dcre, "s5_c_im": dcim, "s5_d": dd,
        "s5_b_glu": dbglu, "ret_gn_g": dgng, "ret_gn_b": dgnb, "ln1_g": dl1g, "ln1_b": dl1b, "ln2_g": dl2g, "ln2_b": dl2b,
        "s5_w_glu": dwglu, "loss": loss8[0:1, 0:1]}
    early_pack = _pack([small[n] for n in EARLY])
    (draw, g_int, dlig, dlib), bg = _in_bwd(du, dq, dk, dv, dgate, dh0r, xhat0, rstd0, li_g, li_b, w_int,
                                            jobs(*([_job_chips(p_out), _job_gather(early_pack)] if distributed else [])))
    small.update(ln_in_g=dlig, ln_in_b=dlib, meta_tokens=draw[PAD:CHUNK])
    g_int = g_int.reshape(N_DEV, PROJ_W // N_DEV, D_MODEL)
    if distributed:
        (r1_in,) = _exchange([_job_pair(g_int)], "exchange_pair_in")
        (p_in,) = _pair_sum([g_int], [r1_in], c_arr, "pair_sum_in")
        big = dict(chip_sums=[p_in, p_out, p_up, p_down], received=[None, bg[0], r_up, r_down], early=bg[1])
    else:
        big = dict(partials=[g_int, g_out, g_up, g_down])
    return draw, big, small


def kernel(x, meta_tokens, ln_in_g, ln_in_b, w_in, s5_lambda_re, s5_lambda_im, s5_log_dt, s5_b_re, s5_b_im, s5_c_re, s5_c_im, s5_d, s5_w_glu, s5_b_glu, ret_gn_g, ret_gn_b, w_out, ln1_g, ln1_b, w_up, w_down, ln2_g, ln2_b, loss_target, m_meta_tokens, m_ln_in_g, m_ln_in_b, m_w_in, m_s5_lambda_re, m_s5_lambda_im, m_s5_log_dt, m_s5_b_re, m_s5_b_im, m_s5_c_re, m_s5_c_im, m_s5_d, m_s5_w_glu, m_s5_b_glu, m_ret_gn_g, m_ret_gn_b, m_w_out, m_ln1_g, m_ln1_b, m_w_up, m_w_down, m_ln2_g, m_ln2_b, v_meta_tokens, v_ln_in_g, v_ln_in_b, v_w_in, v_s5_lambda_re, v_s5_lambda_im, v_s5_log_dt, v_s5_b_re, v_s5_b_im, v_s5_c_re, v_s5_c_im, v_s5_d, v_s5_w_glu, v_s5_b_glu, v_ret_gn_g, v_ret_gn_b, v_w_out, v_ln1_g, v_ln1_b, v_w_up, v_w_down, v_ln2_g, v_ln2_b):
    args = dict(locals())
    names = ["meta_tokens", "ln_in_g", "ln_in_b", "w_in", "s5_lambda_re", "s5_lambda_im", "s5_log_dt", "s5_b_re", "s5_b_im",
             "s5_c_re", "s5_c_im", "s5_d", "s5_w_glu", "s5_b_glu", "ret_gn_g", "ret_gn_b", "w_out", "ln1_g", "ln1_b",
             "w_up", "w_down", "ln2_g", "ln2_b"]
    ax, ay, ac = _place()
    me = 4 * ax + 2 * ay + ac

    (a_meta,) = _exchange([_job_gather(meta_tokens)], "gather_meta")
    meta_full = a_meta.transpose(1, 0, 2).reshape(N_META, D_MODEL)

    sp = {n: args[n] for n in SMALL}
    draw, big, small = _local_step(x[0], loss_target[0], meta_full, w_in[0].T.astype(MM), w_out[0].astype(MM),
                                   w_up[0].astype(MM), w_down[0].astype(MM), s5_w_glu[0].astype(MM), sp, True)

    j_arr = jnp.reshape(2 * ax + ay, (1,)).astype(jnp.int32)
    two_d = lambda a: a.reshape(a.shape[-2:])
    item = lambda n, g: tuple(two_d(t) for t in (g, args[n], args["m_" + n], args["v_" + n]))
    g_out, g_up, g_down = _chip_sum(big["chip_sums"][1:], big["received"][1:], j_arr, "chip_sum_mlp")
    shard_grads = {"w_out": g_out[None], "w_up": g_up[None], "w_down": g_down[None]}
    late_pack = _pack([small[n] for n in LATE])
    res, (r_in, late_all) = _adamw([item(n, shard_grads[n]) for n in ("w_out", "w_up", "w_down")], "adamw_mlp", 8,
                                   [_job_chips(big["chip_sums"][0]), _job_gather(late_pack)])
    (g_int,) = _chip_sum(big["chip_sums"][:1], [r_in], j_arr, "chip_sum_in")
    shard_grads["w_in"] = g_int.T[None]

    early_shapes = [args[n].shape for n in EARLY[:-2]] + [(S5_W, S5_W), (1,)]
    late_shapes = [args["ln_in_g"].shape, args["ln_in_b"].shape, (N_META, D_MODEL)]
    g_small = dict(zip(EARLY, _unpack(_sum_devices(big["early"], "sum_small_early"), early_shapes)))
    g_small.update(zip(LATE, _unpack(_sum_devices(late_all, "sum_small_late"), late_shapes)))
    loss = g_small["loss"].reshape(())

    shard_grads["meta_tokens"] = lax.dynamic_slice(g_small["meta_tokens"], (0, me * (D_MODEL // N_DEV)),
                                                   (N_META, D_MODEL // N_DEV))
    shard_grads["s5_w_glu"] = lax.dynamic_slice(g_small["s5_w_glu"], (me * (S5_W // N_DEV), 0),
                                                (S5_W // N_DEV, S5_W))[None]
    res_in, _ = _adamw([item("w_in", shard_grads["w_in"])], "adamw_in", 8)
    natives = SMALL + ["meta_tokens", "s5_w_glu"]
    res2 = _adamw_native([(shard_grads[n] if n in shard_grads else g_small[n], args[n], args["m_" + n], args["v_" + n])
                          for n in natives], "adamw_small")
    upd = {"w_in": [r.reshape(args["w_in"].shape) for r in res_in]}
    for idx, n in enumerate(("w_out", "w_up", "w_down")):
        upd[n] = [r.reshape(args[n].shape) for r in res[3 * idx:3 * idx + 3]]
    for idx, n in enumerate(natives):
        upd[n] = res2[3 * idx:3 * idx + 3]

    grads, deltas, new_m, new_v = [], [], [], []
    for n in names:
        grads.append((shard_grads[n] if n in shard_grads else g_small[n]).reshape(args[n].shape))
        d, m2, v2 = upd[n]
        deltas.append(d)
        new_m.append(m2)
        new_v.append(v2)
    grad_x = draw[CHUNK:][None]
    return (loss, grad_x, *grads, *deltas, *new_m, *new_v)
```

```python
import math

import jax
import jax.numpy as jnp
from jax import lax
from jax.experimental import pallas as pl
from jax.experimental.pallas import tpu as pltpu

F32 = jnp.float32
MM = jnp.bfloat16

D_MODEL = 1024
N_META = 16
CHUNK = 128
PAD = CHUNK - N_META
S5_W, S5_G, S5_H, S5_P = 256, 16, 16, 64
S5_N = S5_G * S5_P
RET_W, RET_H, HEAD = 768, 6, 128
D_FF = 4096
PROJ_W = S5_W + 4 * RET_W
N_DEV = 8
FF_BLK = D_FF // N_DEV
ROW_BLK = 384
MLP_ROWS = 1408
PROJ_ROWS = 704
ALPHA = 2.0 ** 0.25
LN_EPS = 1e-5
GN_EPS = 1e-5
ROPE_BASE = 10000.0
GELU_C = math.sqrt(2.0 / math.pi)
GELU_A = 0.044715
ADAM_LR, ADAM_B1, ADAM_B2, ADAM_EPS, ADAM_WD, ADAM_STEP = 0.001, 0.9, 0.999, 1e-08, 0.01, 10
VMEM_LIMIT = 60 * 1024 * 1024

_VMEM = pl.BlockSpec(memory_space=pltpu.VMEM)
_ANY = pl.BlockSpec(memory_space=pl.ANY)
_MESH = pl.DeviceIdType.MESH


def _params(sem=None):
    return pltpu.CompilerParams(dimension_semantics=sem, vmem_limit_bytes=VMEM_LIMIT)


def _dot(a, b):
    return jnp.dot(a.astype(MM), b.astype(MM), preferred_element_type=F32)


def _dot_nt(a, b):
    return lax.dot_general(a.astype(MM), b.astype(MM), (((1,), (1,)), ((), ())), preferred_element_type=F32)


def _dot_tn(a, b):
    return lax.dot_general(a.astype(MM), b.astype(MM), (((0,), (0,)), ((), ())), preferred_element_type=F32)


def _split3(a):
    hi = a.astype(jnp.bfloat16)
    r1 = a - hi.astype(F32)
    mid = r1.astype(jnp.bfloat16)
    lo = (r1 - mid.astype(F32)).astype(jnp.bfloat16)
    return hi, mid, lo


def _dot_sel_rhs(a, sel):
    s = sel.astype(jnp.bfloat16)
    return sum(jnp.dot(p, s, preferred_element_type=F32) for p in _split3(a))


def _dot_sel_lhs(sel, b):
    s = sel.astype(jnp.bfloat16)
    return sum(jnp.dot(s, p, preferred_element_type=F32) for p in _split3(b))


def _ln_fwd(r, eps):
    mu = jnp.mean(r, axis=-1, keepdims=True)
    xc = r - mu
    var = jnp.mean(xc * xc, axis=-1, keepdims=True)
    rstd = lax.rsqrt(var + eps)
    return xc * rstd, rstd


def _ln_bwd(dxhat, xhat, rstd):
    m1 = jnp.mean(dxhat, axis=-1, keepdims=True)
    m2 = jnp.mean(dxhat * xhat, axis=-1, keepdims=True)
    return rstd * (dxhat - m1 - xhat * m2)


def _colsum(a):
    return jnp.sum(a, axis=0, keepdims=True)


def _shift3(n_in):
    return [pl.BlockSpec((CHUNK, D_MODEL), (lambda i, j=j: (jnp.clip(3 * i - 1 + j, 0, n_in - 1), 0))) for j in range(3)]


def _ln_in(x2d, meta_full, jobs=()):
    seq = x2d.shape[0]
    tp = seq + CHUNK
    R = ROW_BLK

    def body(xa, xb, xc, meta_ref, xhat_ref, rstd_ref, raw_ref):
        raw_ref[0:CHUNK, :] = xa[...]
        raw_ref[CHUNK:2 * CHUNK, :] = xb[...]
        raw_ref[2 * CHUNK:3 * CHUNK, :] = xc[...]

        @pl.when(pl.program_id(0) == 0)
        def _():
            raw_ref[0:PAD, :] = jnp.zeros((PAD, D_MODEL), F32)
            raw_ref[PAD:CHUNK, :] = meta_ref[...]

        xhat_ref[...], rstd_ref[...] = _ln_fwd(raw_ref[...], LN_EPS)

    row = lambda w: pl.BlockSpec((R, w), lambda i: (i, 0))
    return _call(
        body, "ln_in", (tp // R,),
        _shift3(seq // CHUNK) + [pl.BlockSpec((N_META, D_MODEL), lambda i: (0, 0))],
        [row(D_MODEL), row(1)], [jax.ShapeDtypeStruct((tp, D_MODEL), F32), jax.ShapeDtypeStruct((tp, 1), F32)],
        [pltpu.VMEM((R, D_MODEL), F32)], (x2d, x2d, x2d, meta_full), jobs)


def _in_proj(xhat0, ln_g, ln_b, w_int, cos2, sin2, jobs=()):
    tp = xhat0.shape[0]
    R = PROJ_ROWS if tp % PROJ_ROWS == 0 else ROW_BLK

    def body(xh_ref, g_ref, b_ref, w_ref, cos_ref, sin_ref, u_ref, q_ref, k_ref, v_ref, gate_ref):
        hb = (xh_ref[...] * g_ref[...] + b_ref[...]).astype(MM)
        valid = (pl.program_id(0) * R + lax.broadcasted_iota(jnp.int32, (R, 1), 0)) >= PAD

        def seg(lo, hi):
            return jnp.where(valid, _dot_nt(hb, w_ref[lo:hi, :]), 0.0)

        u_ref[...] = seg(0, S5_W)
        cos = cos_ref[...]
        sin = sin_ref[...]
        q = seg(S5_W, S5_W + RET_W)
        k = seg(S5_W + RET_W, S5_W + 2 * RET_W)
        for h in range(RET_H):
            sl = slice(h * HEAD, (h + 1) * HEAD)
            qh = q[:, sl]
            kh = k[:, sl]
            q_ref[:, sl] = (qh * cos + pltpu.roll(qh, HEAD // 2, 1) * sin).astype(q_ref.dtype)
            k_ref[:, sl] = ((kh * cos + pltpu.roll(kh, HEAD // 2, 1) * sin) * (HEAD ** -0.5)).astype(k_ref.dtype)
        v_ref[...] = seg(S5_W + 2 * RET_W, S5_W + 3 * RET_W).astype(v_ref.dtype)
        gate_ref[...] = seg(S5_W + 3 * RET_W, PROJ_W)

    def rows(w, dt):
        return pl.BlockSpec((R, w), lambda i: (i, 0)), jax.ShapeDtypeStruct((tp, w), dt)

    outs = [rows(S5_W, F32), rows(RET_W, MM), rows(RET_W, MM), rows(RET_W, MM), rows(RET_W, F32)]
    full = lambda s: pl.BlockSpec(s, lambda i: (0,) * len(s))
    return _call(
        body, "in_proj", (tp // R,),
        [pl.BlockSpec((R, D_MODEL), lambda i: (i, 0)), full((1, D_MODEL)), full((1, D_MODEL)), _VMEM,
         pl.BlockSpec((R, HEAD), lambda i: (i, 0)), pl.BlockSpec((R, HEAD), lambda i: (i, 0))],
        [o[0] for o in outs], [o[1] for o in outs], [], (xhat0, ln_g, ln_b, w_int, cos2, sin2), jobs)


def _s5_disc(lre, lim, ldt, bre_t, bim_t):
    dt = jnp.exp(ldt)
    mag = jnp.exp(lre * dt)
    ang = lim * dt
    lbr = mag * jnp.cos(ang)
    lbi = mag * jnp.sin(ang)
    den = lre * lre + lim * lim
    nr = lbr - 1.0
    qr = (nr * lre + lbi * lim) / den
    qi = (lbi * lre - nr * lim) / den
    return lbr, lbi, qr * bre_t - qi * bim_t, qr * bim_t + qi * bre_t


def _s5_tables(lbr, lbi, reverse):
    if reverse:
        lbi = -lbi
    pw = [(lbr, lbi)]
    for _ in range(7):
        r, i = pw[-1]
        pw.append((r * lbr - i * lbi, r * lbi + i * lbr))
    row = lax.broadcasted_iota(jnp.int32, (8, S5_N), 0)
    tabs = []
    for k in range(3):
        sh = 2 ** k
        mask = (row < 8 - sh) if reverse else (row >= sh)
        ar, ai = pw[sh - 1]
        tabs.append((jnp.where(mask, ar, 0.0), jnp.where(mask, ai, 0.0)))
    pr = jnp.zeros((8, S5_N), F32)
    pi = jnp.zeros((8, S5_N), F32)
    for i in range(8):
        ar, ai = pw[7 - i] if reverse else pw[i]
        pr = jnp.where(row == i, ar, pr)
        pi = jnp.where(row == i, ai, pi)
    tabs.append((pr, pi))
    return tabs


def _store_tables(tab_ref, tabs):
    for k, (r, i) in enumerate(tabs):
        tab_ref[2 * k] = r
        tab_ref[2 * k + 1] = i


def _bd_mask():
    r = lax.broadcasted_iota(jnp.int32, (S5_W, S5_N), 0)
    c = lax.broadcasted_iota(jnp.int32, (S5_W, S5_N), 1)
    return jnp.right_shift(r, 4) == jnp.right_shift(c, 6)


def _s5_block_diag(bbr_t, bbi_t, cre_w, cim_w):
    mask = _bd_mask()
    bd = lambda t: jnp.where(mask, t, 0.0)
    return (bd(jnp.tile(bbr_t, (S5_G, 1))), bd(jnp.tile(bbi_t, (S5_G, 1))),
            bd(jnp.tile(cre_w, (1, S5_N // HEAD))), bd(jnp.tile(cim_w, (1, S5_N // HEAD))))


def _scan8(xr, xi, tab_ref, lanes, reverse):
    for k in range(3):
        sh = (8 - 2 ** k) if reverse else 2 ** k
        sr = pltpu.roll(xr, sh, 0)
        si = pltpu.roll(xi, sh, 0)
        mr = tab_ref[2 * k, :, lanes]
        mi = tab_ref[2 * k + 1, :, lanes]
        xr, xi = xr + (mr * sr - mi * si), xi + (mr * si + mi * sr)
    return xr, xi


S5_LANES = 256


def _gelu(y):
    t = jnp.tanh(GELU_C * (y + GELU_A * y * y * y))
    return 0.5 * y * (1.0 + t), t


def _s5_fwd(u, lre, lim, ldt, bre_t, bim_t, cre_w, cim_w, d_row, w_glu, b_glu, jobs=()):
    tp = u.shape[0]
    R = ROW_BLK

    def body(u_ref, lre_ref, lim_ref, ldt_ref, bre_ref, bim_ref, cre_ref, cim_ref, d_ref, wg_ref, bg_ref,
             y_ref, xr_ref, xi_ref, bbd_r, bbd_i, cbd_r, cbd_i, tab_ref, car_r, car_i):
        @pl.when(pl.program_id(0) == 0)
        def _():
            lbr, lbi, bbr, bbi = _s5_disc(lre_ref[...], lim_ref[...], ldt_ref[...], bre_ref[...], bim_ref[...])
            br, bi, cr, ci = _s5_block_diag(bbr, bbi, cre_ref[...], cim_ref[...])
            bbd_r[...] = br.astype(MM)
            bbd_i[...] = bi.astype(MM)
            cbd_r[...] = cr.astype(MM)
            cbd_i[...] = ci.astype(MM)
            _store_tables(tab_ref, _s5_tables(lbr, lbi, False))
            car_r[...] = jnp.zeros_like(car_r)
            car_i[...] = jnp.zeros_like(car_i)

        u = u_ref[...]
        ub = u.astype(MM)
        xr_ref[...] = jnp.dot(ub, bbd_r[...], preferred_element_type=F32)
        xi_ref[...] = jnp.dot(ub, bbd_i[...], preferred_element_type=F32)
        for j in range(S5_N // S5_LANES):
            lanes = pl.ds(j * S5_LANES, S5_LANES)
            pr = tab_ref[6, :, lanes]
            pi = tab_ref[7, :, lanes]

            def step(g, carry):
                cr, ci = carry
                rows = pl.ds(pl.multiple_of(g * 8, 8), 8)
                xr, xi = _scan8(xr_ref[rows, lanes], xi_ref[rows, lanes], tab_ref, lanes, False)
                br = jnp.broadcast_to(cr[7:8, :], cr.shape)
                bi = jnp.broadcast_to(ci[7:8, :], ci.shape)
                xr = xr + (pr * br - pi * bi)
                xi = xi + (pr * bi + pi * br)
                xr_ref[rows, lanes] = xr
                xi_ref[rows, lanes] = xi
                return xr, xi

            cr, ci = lax.fori_loop(0, R // 8, step, (car_r[:, lanes], car_i[:, lanes]), unroll=2)
            car_r[:, lanes] = cr
            car_i[:, lanes] = ci
        y = _dot_nt(xr_ref[...], cbd_r[...]) - _dot_nt(xi_ref[...], cbd_i[...]) + d_ref[...] * u
        yg, _ = _gelu(y)
        z = _dot(yg, wg_ref[...]) + bg_ref[...]
        y_ref[...] = yg * jax.nn.sigmoid(z)

    full = lambda a: pl.BlockSpec(a.shape, lambda i: (0,) * a.ndim)
    small = [lre, lim, ldt, bre_t, bim_t, cre_w, cim_w, d_row, w_glu, b_glu]
    return _call(
        body, "s5_fwd", (tp // R,),
        [pl.BlockSpec((R, S5_W), lambda i: (i, 0))] + [full(a) for a in small],
        [pl.BlockSpec((R, S5_W), lambda i: (i, 0)), pl.BlockSpec((R, S5_N), lambda i: (i, 0)),
         pl.BlockSpec((R, S5_N), lambda i: (i, 0))],
        [jax.ShapeDtypeStruct((tp, S5_W), F32), jax.ShapeDtypeStruct((tp, S5_N), F32),
         jax.ShapeDtypeStruct((tp, S5_N), F32)],
        [pltpu.VMEM((S5_W, S5_N), MM)] * 4 + [pltpu.VMEM((8, 8, S5_N), F32), pltpu.VMEM((8, S5_N), F32),
                                              pltpu.VMEM((8, S5_N), F32)],
        (u, *small), jobs)


def _s5_bwd(dy_out, u, xr, xi, lre, lim, ldt, bre_t, bim_t, cre_w, cim_w, d_row, w_glu, b_glu, jobs=()):
    tp = u.shape[0]
    R = ROW_BLK
    nb = tp // R

    def body(dyo_ref, u_ref, xr_ref, xi_ref, xpr_ref, xpi_ref,
             lre_ref, lim_ref, ldt_ref, bre_ref, bim_ref, cre_ref, cim_ref, d_ref, wg_ref, bg_ref,
             du_ref, dlre_ref, dlim_ref, dldt_ref, dbre_ref, dbim_ref, dcre_ref, dcim_ref, dd_ref, dwg_ref, dbg_ref,
             bbd_r, bbd_i, cbd_r, cbd_i, tab_ref, car_r, car_i, gr_ref, gi_ref, xer_ref, xei_ref,
             abr, abi, acr, aci, adr, adi):
        i = pl.program_id(0)

        @pl.when(i == 0)
        def _():
            lbr, lbi, bbr, bbi = _s5_disc(lre_ref[...], lim_ref[...], ldt_ref[...], bre_ref[...], bim_ref[...])
            br, bi, cr, ci = _s5_block_diag(bbr, bbi, cre_ref[...], cim_ref[...])
            bbd_r[...] = br.astype(MM)
            bbd_i[...] = bi.astype(MM)
            cbd_r[...] = cr.astype(MM)
            cbd_i[...] = ci.astype(MM)
            _store_tables(tab_ref, _s5_tables(lbr, lbi, True))
            for ref in (car_r, car_i, abr, abi, acr, aci, adr, adi, dd_ref, dwg_ref, dbg_ref):
                ref[...] = jnp.zeros_like(ref)

        u = u_ref[...]
        xrv = xr_ref[...]
        xiv = xi_ref[...]
        y = _dot_nt(xrv, cbd_r[...]) - _dot_nt(xiv, cbd_i[...]) + d_ref[...] * u
        yg, t = _gelu(y)
        z = _dot(yg, wg_ref[...]) + bg_ref[...]
        s = jax.nn.sigmoid(z)
        dout = dyo_ref[...]
        dz = dout * yg * s * (1.0 - s)
        dyg = dout * s + _dot_nt(dz, wg_ref[...])
        dwg_ref[...] += _dot_tn(yg, dz)
        dbg_ref[...] += _colsum(dz)
        dy = dyg * (0.5 * (1.0 + t) + 0.5 * y * (1.0 - t * t) * GELU_C * (1.0 + 3.0 * GELU_A * y * y))
        dd_ref[...] += _colsum(dy * u)
        acr[...] += _dot_tn(dy, xrv)
        aci[...] -= _dot_tn(dy, xiv)
        gr_ref[...] = _dot(dy, cbd_r[...])
        gi_ref[...] = -_dot(dy, cbd_i[...])
        has_prev = (i < nb - 1).astype(F32)
        xer_ref[0:8, :] = xpr_ref[...] * has_prev
        xei_ref[0:8, :] = xpi_ref[...] * has_prev
        xer_ref[8:R + 8, :] = xrv
        xei_ref[8:R + 8, :] = xiv
        row = lax.broadcasted_iota(jnp.int32, (8, S5_LANES), 0)
        for j in range(S5_N // S5_LANES):
            lanes = pl.ds(j * S5_LANES, S5_LANES)
            pr = tab_ref[6, :, lanes]
            pi = tab_ref[7, :, lanes]

            def step(n, carry):
                cr, ci, sar, sai = carry
                g = R // 8 - 1 - n
                r0 = pl.multiple_of(g * 8, 8)
                rows = pl.ds(r0, 8)
                gr, gi = _scan8(gr_ref[rows, lanes], gi_ref[rows, lanes], tab_ref, lanes, True)
                br = jnp.broadcast_to(cr[0:1, :], cr.shape)
                bi = jnp.broadcast_to(ci[0:1, :], ci.shape)
                gr = gr + (pr * br - pi * bi)
                gi = gi + (pr * bi + pi * br)
                gr_ref[rows, lanes] = gr
                gi_ref[rows, lanes] = gi
                last = row == 7
                xpr = pltpu.roll(jnp.where(last, xer_ref[rows, lanes], xer_ref[pl.ds(r0 + 8, 8), lanes]), 1, 0)
                xpi = pltpu.roll(jnp.where(last, xei_ref[rows, lanes], xei_ref[pl.ds(r0 + 8, 8), lanes]), 1, 0)
                return gr, gi, sar + (gr * xpr + gi * xpi), sai + (gi * xpr - gr * xpi)

            cr, ci, sar, sai = lax.fori_loop(
                0, R // 8, step, (car_r[:, lanes], car_i[:, lanes], adr[:, lanes], adi[:, lanes]), unroll=2)
            car_r[:, lanes] = cr
            car_i[:, lanes] = ci
            adr[:, lanes] = sar
            adi[:, lanes] = sai
        grv = gr_ref[...]
        giv = gi_ref[...]
        du_ref[...] = (dy * d_ref[...] + _dot_nt(grv, bbd_r[...]) + _dot_nt(giv, bbd_i[...])).astype(du_ref.dtype)
        abr[...] += _dot_tn(u, grv)
        abi[...] += _dot_tn(u, giv)

        @pl.when(i == nb - 1)
        def _():
            mask = _bd_mask()
            r16 = lax.broadcasted_iota(jnp.int32, (S5_H, S5_W), 1)
            h16 = lax.broadcasted_iota(jnp.int32, (S5_H, S5_W), 0)
            fold_b = jnp.bitwise_and(r16, S5_H - 1) == h16
            c64 = lax.broadcasted_iota(jnp.int32, (S5_N, S5_P), 0)
            p64 = lax.broadcasted_iota(jnp.int32, (S5_N, S5_P), 1)
            fold_c = jnp.bitwise_and(c64, S5_P - 1) == p64
            dbbr = _dot_sel_lhs(fold_b, jnp.where(mask, abr[...], 0.0))
            dbbi = _dot_sel_lhs(fold_b, jnp.where(mask, abi[...], 0.0))
            dcre_ref[...] = _dot_sel_rhs(jnp.where(mask, acr[...], 0.0), fold_c)
            dcim_ref[...] = _dot_sel_rhs(jnp.where(mask, aci[...], 0.0), fold_c)
            dlbr = _colsum(adr[...])
            dlbi = _colsum(adi[...])
            _, vjp = jax.vjp(_s5_disc, lre_ref[...], lim_ref[...], ldt_ref[...], bre_ref[...], bim_ref[...])
            dlre, dlim, dldt, dbre, dbim = vjp((dlbr, dlbi, dbbr, dbbi))
            dlre_ref[...] = dlre
            dlim_ref[...] = dlim
            dbre_ref[...] = dbre
            dbim_ref[...] = dbim
            gsel = jnp.right_shift(lax.broadcasted_iota(jnp.int32, (S5_N, HEAD), 0), 6) == \
                lax.broadcasted_iota(jnp.int32, (S5_N, HEAD), 1)
            dldt_ref[...] = _dot_sel_rhs(dldt, gsel)

    full = lambda a: pl.BlockSpec(a.shape, lambda i: (0,) * a.ndim)
    rev = lambda w: pl.BlockSpec((R, w), lambda i: (nb - 1 - i, 0))
    prev8 = pl.BlockSpec((8, S5_N), lambda i: (jnp.maximum((nb - 1 - i) * (R // 8) - 1, 0), 0))
    small = [lre, lim, ldt, bre_t, bim_t, cre_w, cim_w, d_row, w_glu, b_glu]
    outs = [((tp, S5_W), rev(S5_W))] + [
        (s, pl.BlockSpec(s, lambda i: (0, 0))) for s in
        [(1, S5_N), (1, S5_N), (1, HEAD), (S5_H, S5_N), (S5_H, S5_N), (S5_W, S5_P), (S5_W, S5_P),
         (1, S5_W), (S5_W, S5_W), (1, S5_W)]]
    return _call(
        body, "s5_bwd", (nb,),
        [rev(S5_W), rev(S5_W), rev(S5_N), rev(S5_N), prev8, prev8] + [full(a) for a in small],
        [o[1] for o in outs], [jax.ShapeDtypeStruct(o[0], MM if n == 0 else F32) for n, o in enumerate(outs)],
        [pltpu.VMEM((S5_W, S5_N), MM)] * 4 + [
            pltpu.VMEM((8, 8, S5_N), F32), pltpu.VMEM((8, S5_N), F32), pltpu.VMEM((8, S5_N), F32),
            pltpu.VMEM((R, S5_N), F32), pltpu.VMEM((R, S5_N), F32),
            pltpu.VMEM((R + 8, S5_N), F32), pltpu.VMEM((R + 8, S5_N), F32)] + [pltpu.VMEM((S5_W, S5_N), F32)] * 4 + [
            pltpu.VMEM((8, S5_N), F32), pltpu.VMEM((8, S5_N), F32)],
        (dy_out, u, xr, xi, xr, xi, *small), jobs)


def _ret_fwd(q, k, v, dmat, zeta_b, xi_b, gam_b, jobs=()):
    tp = q.shape[0]
    C = dmat.shape[1]
    nc = tp // C

    def body(q_ref, k_ref, v_ref, dm_ref, ze_ref, xi_ref, ga_ref, o_ref, st_ref, s_ref):
        @pl.when(pl.program_id(0) == 0)
        def _():
            s_ref[...] = jnp.zeros_like(s_ref)

        for h in range(RET_H):
            sl = slice(h * HEAD, (h + 1) * HEAD)
            qh, kh, vh = q_ref[:, sl], k_ref[:, sl], v_ref[:, sl]
            sh = s_ref[h]
            st_ref[0, sl, :] = sh
            scores = _dot_nt(qh, kh) * dm_ref[h]
            o_ref[:, sl] = _dot(scores, vh) + _dot(qh, sh) * xi_ref[h]
            s_ref[h] = ga_ref[h] * sh + _dot_tn(kh.astype(F32) * ze_ref[h], vh)

    blk = pl.BlockSpec((C, RET_W), lambda c: (c, 0))
    cst = lambda a: pl.BlockSpec(a.shape, lambda c: (0, 0, 0))
    return _call(
        body, "ret_fwd", (nc,), [blk, blk, blk, cst(dmat), cst(zeta_b), cst(xi_b), cst(gam_b)],
        [blk, pl.BlockSpec((1, RET_W, HEAD), lambda c: (c, 0, 0))],
        [jax.ShapeDtypeStruct((tp, RET_W), F32), jax.ShapeDtypeStruct((nc, RET_W, HEAD), F32)],
        [pltpu.VMEM((RET_H, HEAD, HEAD), F32)], (q, k, v, dmat, zeta_b, xi_b, gam_b), jobs)


def _ret_bwd(q, k, v, do, states, cos2, sin2, dmat, zeta_b, xi_b, gam_b, jobs=()):
    tp = q.shape[0]
    C = dmat.shape[1]
    nc = tp // C

    def body(q_ref, k_ref, v_ref, do_ref, st_ref, cos_ref, sin_ref, dm_ref, ze_ref, xi_ref, ga_ref,
             dq_ref, dk_ref, dv_ref, ds_ref):
        @pl.when(pl.program_id(0) == 0)
        def _():
            ds_ref[...] = jnp.zeros_like(ds_ref)

        cos = cos_ref[...]
        sin = sin_ref[...]
        for h in range(RET_H):
            sl = slice(h * HEAD, (h + 1) * HEAD)
            qh, kh, vh = q_ref[:, sl], k_ref[:, sl], v_ref[:, sl]
            dmh = dm_ref[h]
            sh = st_ref[0, sl, :]
            dsn = ds_ref[h]
            doh = do_ref[:, sl]
            dox = doh * xi_ref[h]
            a = _dot_nt(qh, kh) * dmh
            dqk = _dot_nt(doh, vh) * dmh
            kz = kh.astype(F32) * ze_ref[h]
            dv_ref[:, sl] = (_dot_tn(a, doh) + _dot(kz, dsn)).astype(dv_ref.dtype)
            dqr = _dot(dqk, kh) + _dot_nt(dox, sh)
            dkr = _dot_tn(dqk, qh) + ze_ref[h] * _dot_nt(vh, dsn)
            ds_ref[h] = ga_ref[h] * dsn + _dot_tn(qh, dox)
            dq_ref[:, sl] = (dqr * cos - pltpu.roll(dqr, HEAD // 2, 1) * sin).astype(dq_ref.dtype)
            dk_ref[:, sl] = ((dkr * cos - pltpu.roll(dkr, HEAD // 2, 1) * sin) * (HEAD ** -0.5)).astype(dk_ref.dtype)

    blk = pl.BlockSpec((C, RET_W), lambda c: (nc - 1 - c, 0))
    tab = pl.BlockSpec((C, HEAD), lambda c: (nc - 1 - c, 0))
    cst = lambda a: pl.BlockSpec(a.shape, lambda c: (0, 0, 0))
    return _call(
        body, "ret_bwd", (nc,),
        [blk, blk, blk, blk, pl.BlockSpec((1, RET_W, HEAD), lambda c: (nc - 1 - c, 0, 0)), tab, tab,
         cst(dmat), cst(zeta_b), cst(xi_b), cst(gam_b)],
        [blk, blk, blk], [jax.ShapeDtypeStruct((tp, RET_W), MM)] * 3, [pltpu.VMEM((RET_H, HEAD, HEAD), F32)],
        (q, k, v, do, states, cos2, sin2, dmat, zeta_b, xi_b, gam_b), jobs)


def _gn_gate(o, gate, gn_g, gn_b):
    xhat, rstd = _ln_fwd(o, GN_EPS)
    on = xhat * gn_g + gn_b
    s = jax.nn.sigmoid(gate)
    return gate * s * on, xhat, rstd, on, s


def _post_up(o, gate, ys5, xhat0, gn_g, gn_b, li_g, li_b, l1_g, l1_b, w_out, w_up, jobs=()):
    tp = o.shape[0]
    R = ROW_BLK

    def body(o_ref, g_ref, ys_ref, xh0_ref, gng, gnb, lig, lib, l1g, l1b, wo_ref, wu_ref,
             ycat_ref, xh1_ref, rstd1_ref, h1b_ref, pre_ref):
        ycat_ref[:, 0:S5_W] = ys_ref[...].astype(ycat_ref.dtype)
        for h in range(RET_H):
            sl = slice(h * HEAD, (h + 1) * HEAD)
            yret = _gn_gate(o_ref[:, sl], g_ref[:, sl], gng[:, sl], gnb[:, sl])[0]
            ycat_ref[:, S5_W + h * HEAD:S5_W + (h + 1) * HEAD] = yret.astype(ycat_ref.dtype)
        mixed = _dot(ycat_ref[...], wo_ref[...])
        h0 = xh0_ref[...] * lig[...] + lib[...]
        xh1, rstd1 = _ln_fwd(ALPHA * h0 + mixed, LN_EPS)
        xh1_ref[...] = xh1
        rstd1_ref[...] = rstd1
        h1b = (xh1 * l1g[...] + l1b[...]).astype(MM)
        h1b_ref[...] = h1b
        for d in range(N_DEV):
            pre_ref[:, d * FF_BLK:(d + 1) * FF_BLK] = jnp.maximum(_dot(h1b, wu_ref[d]), 0.0)

    row = lambda w: pl.BlockSpec((R, w), lambda i: (i, 0))
    full = lambda a: pl.BlockSpec(a.shape, lambda i: (0,) * a.ndim)
    vecs = [gn_g, gn_b, li_g, li_b, l1_g, l1_b]
    outs = [(row(D_MODEL), jax.ShapeDtypeStruct((tp, D_MODEL), MM)), (row(D_MODEL), jax.ShapeDtypeStruct((tp, D_MODEL), F32)),
            (row(1), jax.ShapeDtypeStruct((tp, 1), F32)), (row(D_MODEL), jax.ShapeDtypeStruct((tp, D_MODEL), MM)),
            (row(D_FF), jax.ShapeDtypeStruct((tp, D_FF), F32))]
    return _call(
        body, "post_up", (tp // R,),
        [row(RET_W), row(RET_W), row(S5_W), row(D_MODEL)] + [full(a) for a in vecs] + [_VMEM, _VMEM],
        [o[0] for o in outs], [o[1] for o in outs], [], (o, gate, ys5, xhat0, *vecs, w_out, w_up), jobs)


def _post_down(pre, xhat1, tgt, l1_g, l1_b, l2_g, l2_b, w_down):
    tp = pre.shape[0]
    seq = tgt.shape[0]
    R = ROW_BLK

    def body(pre_ref, xh1_ref, ta, tb, tc, l1g, l1b, l2g, l2b, wd_ref,
             dr2_ref, dffb_ref, loss_ref, dl2g_ref, dl2b_ref, tgt_ref):
        i = pl.program_id(0)

        @pl.when(i == 0)
        def _():
            for ref in (loss_ref, dl2g_ref, dl2b_ref):
                ref[...] = jnp.zeros_like(ref)

        tgt_ref[0:CHUNK, :] = ta[...]
        tgt_ref[CHUNK:2 * CHUNK, :] = tb[...]
        tgt_ref[2 * CHUNK:3 * CHUNK, :] = tc[...]
        ff = jnp.zeros((R, D_MODEL), F32)
        for d in range(N_DEV):
            pre = pre_ref[:, d * FF_BLK:(d + 1) * FF_BLK]
            ff = ff + _dot(pre * pre, wd_ref[d * FF_BLK:(d + 1) * FF_BLK, :])
        h1 = xh1_ref[...] * l1g[...] + l1b[...]
        xh2, rstd2 = _ln_fwd(ALPHA * h1 + ff, LN_EPS)
        h2 = xh2 * l2g[...] + l2b[...]
        valid = (i * R + lax.broadcasted_iota(jnp.int32, (R, 1), 0)) >= CHUNK
        err = jnp.where(valid, h2 - tgt_ref[...], 0.0)
        loss_ref[...] += 0.5 * jnp.sum(err * err) / D_MODEL
        dh2 = err * (1.0 / D_MODEL)
        dl2g_ref[...] += _colsum(dh2 * xh2)
        dl2b_ref[...] += _colsum(dh2)
        dr2 = _ln_bwd(dh2 * l2g[...], xh2, rstd2)
        dr2_ref[...] = dr2
        dffb_ref[...] = dr2.astype(MM)

    row = lambda w: pl.BlockSpec((R, w), lambda i: (i, 0))
    full = lambda a: pl.BlockSpec(a.shape, lambda i: (0,) * a.ndim)
    vecs = [l1_g, l1_b, l2_g, l2_b]
    acc = lambda s: (pl.BlockSpec(s, lambda i: (0, 0)), jax.ShapeDtypeStruct(s, F32))
    outs = [(row(D_MODEL), jax.ShapeDtypeStruct((tp, D_MODEL), F32)), (row(D_MODEL), jax.ShapeDtypeStruct((tp, D_MODEL), MM)),
            acc((8, HEAD)), acc((1, D_MODEL)), acc((1, D_MODEL))]
    return pl.pallas_call(
        body, name="post_down", grid=(tp // R,),
        in_specs=[row(D_FF), row(D_MODEL)] + _shift3(seq // CHUNK) + [full(a) for a in vecs] + [_VMEM],
        out_specs=[o[0] for o in outs], out_shape=[o[1] for o in outs],
        scratch_shapes=[pltpu.VMEM((R, D_MODEL), F32)],
        compiler_params=_params(("arbitrary",)),
    )(pre, xhat1, tgt, tgt, tgt, *vecs, w_down)


def _mlp_bwd(h1b, dffb, pre, w_up, w_down):
    tp = h1b.shape[0]
    R = MLP_ROWS if tp % MLP_ROWS == 0 else ROW_BLK
    nr = tp // R

    def body(h_ref, df_ref, pre_ref, wu_ref, wd_ref, gup_ref, gdn_ref, dh1_ref, aup, adn):
        d = pl.program_id(0)
        r = pl.program_id(1)

        @pl.when(r == 0)
        def _():
            aup[...] = jnp.zeros_like(aup)
            adn[...] = jnp.zeros_like(adn)

        h = h_ref[...]
        df = df_ref[...]
        wu = wu_ref[0]
        wd = wd_ref[0]
        pre = pre_ref[...]
        dpre = (_dot_nt(df, wd) * (2.0 * pre)).astype(MM)

        aup[...] += _dot_tn(h, dpre)
        adn[...] += _dot_tn(pre * pre, df)
        contrib = _dot_nt(dpre, wu)
        rows = pl.ds(pl.multiple_of(r * R, 64), R)

        @pl.when(d == 0)
        def _():
            dh1_ref[rows, :] = contrib

        @pl.when(d > 0)
        def _():
            dh1_ref[rows, :] += contrib

        @pl.when(r == nr - 1)
        def _():
            gup_ref[0] = aup[...].astype(gup_ref.dtype)
            gdn_ref[0] = adn[...].astype(gdn_ref.dtype)

    return pl.pallas_call(
        body, name="mlp_bwd", grid=(N_DEV, nr),
        in_specs=[pl.BlockSpec((R, D_MODEL), lambda d, r: (r, 0)), pl.BlockSpec((R, D_MODEL), lambda d, r: (r, 0)),
                  pl.BlockSpec((R, FF_BLK), lambda d, r: (r, d)),
                  pl.BlockSpec((1, D_MODEL, FF_BLK), lambda d, r: (d, 0, 0)),
                  pl.BlockSpec((1, FF_BLK, D_MODEL), lambda d, r: (d, 0, 0))],
        out_specs=[pl.BlockSpec((1, D_MODEL, FF_BLK), lambda d, r: (d, 0, 0)),
                   pl.BlockSpec((1, FF_BLK, D_MODEL), lambda d, r: (d, 0, 0)), _VMEM],
        out_shape=[jax.ShapeDtypeStruct((N_DEV, D_MODEL, FF_BLK), MM), jax.ShapeDtypeStruct((N_DEV, FF_BLK, D_MODEL), MM),
                   jax.ShapeDtypeStruct((tp, D_MODEL), F32)],
        scratch_shapes=[pltpu.VMEM((D_MODEL, FF_BLK), F32), pltpu.VMEM((FF_BLK, D_MODEL), F32)],
        compiler_params=_params(("arbitrary", "arbitrary")),
    )(h1b, dffb, pre, w_up, w_down.reshape(N_DEV, FF_BLK, D_MODEL))


def _post_bwd(dh1m, dr2, xhat1, rstd1, ycat, o, gate, gn_g, gn_b, l1_g, w_out, jobs=()):
    tp = o.shape[0]
    R = ROW_BLK
    nb = tp // R

    def body(dm_ref, dr2_ref, xh1_ref, rs1_ref, yc_ref, o_ref, g_ref, gng, gnb, l1g, wo_ref,
             do_ref, dg_ref, dys_ref, dh0_ref, gwo_ref, dl1g_ref, dl1b_ref, dgng_ref, dgnb_ref, awo):
        i = pl.program_id(0)

        @pl.when(i == 0)
        def _():
            for ref in (awo, dl1g_ref, dl1b_ref, dgng_ref, dgnb_ref):
                ref[...] = jnp.zeros_like(ref)

        dh1 = dm_ref[...] + ALPHA * dr2_ref[...]
        xh1 = xh1_ref[...]
        dl1g_ref[...] += _colsum(dh1 * xh1)
        dl1b_ref[...] += _colsum(dh1)
        dr1 = _ln_bwd(dh1 * l1g[...], xh1, rs1_ref[...])
        dh0_ref[...] = ALPHA * dr1
        dmix = dr1.astype(MM)
        awo[...] += _dot_tn(yc_ref[...], dmix)
        dyc = _dot_nt(dmix, wo_ref[...])
        dys_ref[...] = dyc[:, 0:S5_W]
        for h in range(RET_H):
            sl = slice(h * HEAD, (h + 1) * HEAD)
            gt = g_ref[:, sl]
            _, xhat, rstd, on, s = _gn_gate(o_ref[:, sl], gt, gng[:, sl], gnb[:, sl])
            dyr = dyc[:, S5_W + h * HEAD:S5_W + (h + 1) * HEAD]
            dg_ref[:, sl] = (dyr * on * (s * (1.0 + gt * (1.0 - s)))).astype(dg_ref.dtype)
            don = dyr * gt * s
            dgng_ref[:, sl] += _colsum(don * xhat)
            dgnb_ref[:, sl] += _colsum(don)
            do_ref[:, sl] = _ln_bwd(don * gng[:, sl], xhat, rstd)

        @pl.when(i == nb - 1)
        def _():
            gwo_ref[...] = awo[...].astype(gwo_ref.dtype)

    row = lambda w: pl.BlockSpec((R, w), lambda i: (i, 0))
    full = lambda a: pl.BlockSpec(a.shape, lambda i: (0,) * a.ndim)
    acc = lambda s, dt=F32: (pl.BlockSpec(s, lambda i: (0, 0)), jax.ShapeDtypeStruct(s, dt))
    outs = [(row(RET_W), jax.ShapeDtypeStruct((tp, RET_W), F32)), (row(RET_W), jax.ShapeDtypeStruct((tp, RET_W), MM)),
            (row(S5_W), jax.ShapeDtypeStruct((tp, S5_W), F32)), (row(D_MODEL), jax.ShapeDtypeStruct((tp, D_MODEL), F32)),
            acc((D_MODEL, D_MODEL), MM), acc((1, D_MODEL)), acc((1, D_MODEL)), acc((1, RET_W)), acc((1, RET_W))]
    return _call(
        body, "post_bwd", (nb,),
        [row(D_MODEL), row(D_MODEL), row(D_MODEL), row(1), row(D_MODEL), row(RET_W), row(RET_W),
         full(gn_g), full(gn_b), full(l1_g), _VMEM],
        [o[0] for o in outs], [o[1] for o in outs],
        [pltpu.VMEM((D_MODEL, D_MODEL), F32)],
        (dh1m, dr2, xhat1, rstd1, ycat, o, gate, gn_g, gn_b, l1_g, w_out), jobs)


def _in_bwd(du, dq, dk, dv, dg, dh0r, xhat0, rstd0, li_g, li_b, w_int, jobs=()):
    tp = du.shape[0]
    R = PROJ_ROWS if tp % PROJ_ROWS == 0 else ROW_BLK
    nb = tp // R
    segs = [(0, S5_W)] + [(S5_W + n * RET_W, S5_W + (n + 1) * RET_W) for n in range(4)]

    def body(du_ref, dq_ref, dk_ref, dv_ref, dg_ref, dh0r_ref, xh_ref, rs_ref, lig, lib, w_ref,
             gx_ref, dmeta_ref, gw_ref, dlg_ref, dlb_ref, aw, stage, out_sems):
        i = pl.program_id(0)
        slot = i % 2

        def to_gx(step_slot, first):
            if first:
                return pltpu.make_async_copy(stage.at[0, CHUNK:R, :], gx_ref.at[0:R - CHUNK, :], out_sems.at[0])
            return pltpu.make_async_copy(stage.at[step_slot], gx_ref.at[pl.ds(i * R - CHUNK, R), :], out_sems.at[step_slot])

        @pl.when(i == 0)
        def _():
            for ref in (aw, dlg_ref, dlb_ref):
                ref[...] = jnp.zeros_like(ref)

        @pl.when(i >= 3)
        def _():
            to_gx(slot, False).wait()

        valid = (i * R + lax.broadcasted_iota(jnp.int32, (R, 1), 0)) >= PAD
        xh = xh_ref[...]
        hb = (xh * lig[...] + lib[...]).astype(MM)
        dh0 = dh0r_ref[...]
        for (lo, hi), ref in zip(segs, (du_ref, dq_ref, dk_ref, dv_ref, dg_ref)):
            dseg = jnp.where(valid, ref[...], 0.0).astype(MM)
            dh0 = dh0 + _dot(dseg, w_ref[lo:hi, :])
            aw[lo:hi, :] += _dot_tn(dseg, hb)
        dlg_ref[...] += _colsum(dh0 * xh)
        dlb_ref[...] += _colsum(dh0)
        draw = _ln_bwd(dh0 * lig[...], xh, rs_ref[...])
        stage[slot] = draw

        @pl.when(i == 0)
        def _():
            dmeta_ref[...] = draw[PAD:CHUNK, :]
            first = to_gx(0, True)
            first.start()
            first.wait()

        @pl.when(i > 0)
        def _():
            to_gx(slot, False).start()

        @pl.when(i == nb - 1)
        def _():
            gw_ref[...] = aw[...].astype(gw_ref.dtype)
            for back in (1, 0):
                if nb - 1 - back >= 1:
                    to_gx((nb - 1 - back) % 2, False).wait()

    row = lambda w: pl.BlockSpec((R, w), lambda i: (i, 0))
    full = lambda a: pl.BlockSpec(a.shape, lambda i: (0,) * a.ndim)
    acc = lambda s, dt=F32: (pl.BlockSpec(s, lambda i: (0, 0)), jax.ShapeDtypeStruct(s, dt))
    outs = [(_ANY, jax.ShapeDtypeStruct((tp - CHUNK, D_MODEL), F32)), acc((N_META, D_MODEL)), acc((PROJ_W, D_MODEL), MM),
            acc((1, D_MODEL)), acc((1, D_MODEL))]
    return _call(
        body, "in_bwd", (nb,),
        [row(S5_W), row(RET_W), row(RET_W), row(RET_W), row(RET_W), row(D_MODEL), row(D_MODEL), row(1),
         full(li_g), full(li_b), _VMEM],
        [o[0] for o in outs], [o[1] for o in outs],
        [pltpu.VMEM((PROJ_W, D_MODEL), F32), pltpu.VMEM((2, R, D_MODEL), F32), pltpu.SemaphoreType.DMA((2,))],
        (du, dq, dk, dv, dg, dh0r, xhat0, rstd0, li_g, li_b, w_int), jobs)


def _place():
    return lax.axis_index("x"), lax.axis_index("y"), lax.axis_index("c")


def _dma_sems(n):
    return pltpu.SemaphoreType.DMA((n,))


def _job_gather(shard):
    def parts(ins, outs, sems):
        (src,), (out,), (send_sems, recv_sems, local_sem) = ins, outs, sems
        x, y, c = _place()
        north = c == 1
        me, sib = (x, y, c), (x, y, 1 - c)
        xn, yn, dg = (1 - x, y, c), (x, 1 - y, c), (1 - x, 1 - y, c)
        relay_from = (jnp.where(north, 1 - x, x), jnp.where(north, y, 1 - y), c)
        relay_to = (jnp.where(north, x, 1 - x), jnp.where(north, 1 - y, y), c)

        def slot(dev):
            return out.at[4 * dev[0] + 2 * dev[1] + dev[2]]

        def copy(k, block, to, from_input=False):
            return pltpu.make_async_remote_copy(
                src_ref=src if from_input else slot(block), dst_ref=slot(block),
                send_sem=send_sems.at[k], recv_sem=recv_sems.at[k], device_id=to, device_id_type=_MESH)

        mine = lambda: pltpu.make_async_copy(src, slot(me), local_sem.at[0])
        first = lambda: [copy(0, me, sib, True), copy(1, me, xn, True), copy(2, me, yn, True)]
        relayed = lambda: [copy(3, relay_from, relay_to), copy(4, xn, sib), copy(5, yn, sib)]
        return me, sib, xn, yn, dg, copy, mine, first, relayed

    def start(ins, outs, sems):
        mine, first = parts(ins, outs, sems)[6:8]
        mine().start()
        for cp in first():
            cp.start()

    def relay(ins, outs, sems):
        me, sib, xn, yn, dg, copy, mine, first, relayed = parts(ins, outs, sems)
        copy(1, xn, me).wait_recv()
        copy(2, yn, me).wait_recv()
        for cp in relayed():
            cp.start()

    def finish(ins, outs, sems):
        me, sib, xn, yn, dg, copy, mine, first, relayed = parts(ins, outs, sems)
        other = 1 - me[2]
        copy(3, dg, me).wait_recv()
        last = copy(6, dg, sib)
        last.start()
        copy(0, sib, me).wait_recv()
        for k, chip in ((4, xn), (5, yn), (6, dg)):
            copy(k, (chip[0], chip[1], other), me).wait_recv()
        for cp in first() + relayed() + [last]:
            cp.wait_send()
        mine().wait()

    return dict(ins=[shard], outs=[jax.ShapeDtypeStruct((N_DEV,) + shard.shape, shard.dtype)],
                sems=[_dma_sems(7), _dma_sems(7), _dma_sems(1)], start=start, middle=relay, finish=finish)


def _job_pair(g):
    def copies(ins, outs, sems):
        x, y, c = _place()
        return [pltpu.make_async_remote_copy(
            src_ref=ins[0].at[2 * j + (1 - c)], dst_ref=outs[0].at[j], send_sem=sems[0].at[j], recv_sem=sems[1].at[j],
            device_id=(x, y, 1 - c), device_id_type=_MESH) for j in range(4)]

    def start(ins, outs, sems):
        for cp in copies(ins, outs, sems):
            cp.start()

    def finish(ins, outs, sems):
        for cp in copies(ins, outs, sems):
            cp.wait()

    return dict(ins=[g], outs=[jax.ShapeDtypeStruct((4,) + g.shape[1:], g.dtype)], sems=[_dma_sems(4), _dma_sems(4)],
                start=start, finish=finish)


def _job_chips(p):
    def copies(ins, outs, sems):
        x, y, c = _place()
        chips = [(1 - x, y), (x, 1 - y), (1 - x, 1 - y)]
        return [pltpu.make_async_remote_copy(
            src_ref=ins[0].at[2 * chip[0] + chip[1]], dst_ref=outs[0].at[k], send_sem=sems[0].at[k],
            recv_sem=sems[1].at[k], device_id=(*chip, c), device_id_type=_MESH) for k, chip in enumerate(chips)]

    def start(ins, outs, sems):
        for cp in copies(ins, outs, sems):
            cp.start()

    def finish(ins, outs, sems):
        for cp in copies(ins, outs, sems):
            cp.wait()

    return dict(ins=[p], outs=[jax.ShapeDtypeStruct((3,) + p.shape[1:], p.dtype)], sems=[_dma_sems(3), _dma_sems(3)],
                start=start, finish=finish)


def _split_job_refs(jobs, ins, outs, sems):
    res, a, b, c = [], 0, 0, 0
    for job in jobs:
        na, nb, nc = len(job["ins"]), len(job["outs"]), len(job["sems"])
        res.append((ins[a:a + na], outs[b:b + nb], sems[c:c + nc]))
        a, b, c = a + na, b + nb, c + nc
    return res


def _call(body, name, grid, in_specs, out_specs, out_shape, scratch, args, jobs=(), prefetch=None):
    jobs = list(jobs)
    n_in, n_out, n_scr = len(in_specs), len(out_specs), len(scratch)
    j_in = [a for job in jobs for a in job["ins"]]
    j_out = [o for job in jobs for o in job["outs"]]
    j_scr = [s for job in jobs for s in job["sems"]]
    nsteps = grid[0]
    n_pre = 0 if prefetch is None else 1

    def wrapped(*refs):
        pre, refs = refs[:n_pre], refs[n_pre:]
        ins, jins = refs[:n_in], refs[n_in:n_in + len(j_in)]
        refs = refs[n_in + len(j_in):]
        outs, jouts = refs[:n_out], refs[n_out:n_out + len(j_out)]
        refs = refs[n_out + len(j_out):]
        scr, jscr = refs[:n_scr], refs[n_scr:]
        per_job = _split_job_refs(jobs, jins, jouts, jscr)

        def middle():
            for job, r in zip(jobs, per_job):
                if "middle" in job:
                    job["middle"](*r)

        @pl.when(pl.program_id(0) == 0)
        def _():
            for job, r in zip(jobs, per_job):
                job["start"](*r)

        if nsteps >= 3:
            pl.when(pl.program_id(0) == nsteps // 2)(middle)

        body(*pre, *ins, *outs, *scr)

        @pl.when(pl.program_id(0) == nsteps - 1)
        def _():
            if nsteps < 3:
                middle()
            for job, r in zip(jobs, per_job):
                job["finish"](*r)

    specs = dict(in_specs=list(in_specs) + [_ANY] * len(j_in), out_specs=list(out_specs) + [_ANY] * len(j_out),
                 scratch_shapes=list(scratch) + j_scr)
    if n_pre:
        specs = dict(grid_spec=pltpu.PrefetchScalarGridSpec(num_scalar_prefetch=1, grid=grid, **specs))
    else:
        specs["grid"] = grid
    res = pl.pallas_call(
        wrapped if jobs else body, name=name, out_shape=list(out_shape) + j_out,
        compiler_params=_params(("arbitrary",) * len(grid)), **specs,
    )(*([prefetch] if n_pre else []), *args, *j_in)
    return list(res[:n_out]), list(res[n_out:])


def _exchange(jobs, name):
    j_in = [a for job in jobs for a in job["ins"]]
    j_out = [o for job in jobs for o in job["outs"]]
    j_scr = [s for job in jobs for s in job["sems"]]

    def body(*refs):
        per_job = _split_job_refs(jobs, refs[:len(j_in)], refs[len(j_in):len(j_in) + len(j_out)],
                                  refs[len(j_in) + len(j_out):])
        for phase in ("start", "middle", "finish"):
            for job, r in zip(jobs, per_job):
                if phase in job:
                    job[phase](*r)

    return pl.pallas_call(body, name=name, out_shape=j_out, in_specs=[_ANY] * len(j_in), out_specs=[_ANY] * len(j_out),
                          scratch_shapes=j_scr)(*j_in)


def _pair_sum(gs, r1s, c_arr, name):
    n = len(gs)

    def body(c_ref, *refs):
        for a in range(n):
            refs[2 * n + a][...] = (refs[a][...].astype(F32) + refs[n + a][...].astype(F32)).astype(refs[2 * n + a].dtype)

    def blk(g, own):
        s = g.shape[1:]
        if own:
            return pl.BlockSpec((1,) + s, lambda j, c_ref: (2 * j + c_ref[0],) + (0,) * len(s))
        return pl.BlockSpec((1,) + s, lambda j, c_ref: (j,) + (0,) * len(s))

    return pl.pallas_call(
        body, name=name,
        grid_spec=pltpu.PrefetchScalarGridSpec(
            num_scalar_prefetch=1, grid=(4,),
            in_specs=[blk(g, True) for g in gs] + [blk(g, False) for g in gs],
            out_specs=[blk(g, False) for g in gs]),
        out_shape=[jax.ShapeDtypeStruct((4,) + g.shape[1:], g.dtype) for g in gs],
        compiler_params=_params(("arbitrary",)),
    )(c_arr, *gs, *r1s)


def _adamw_math(w, g, m, v):
    m = ADAM_B1 * m + (1.0 - ADAM_B1) * g
    v = ADAM_B2 * v + (1.0 - ADAM_B2) * (g * g)
    m_hat = m / (1.0 - ADAM_B1 ** ADAM_STEP)
    v_hat = v / (1.0 - ADAM_B2 ** ADAM_STEP)
    return -ADAM_LR * (m_hat / (jnp.sqrt(v_hat) + ADAM_EPS) + ADAM_WD * w), m, v


def _transpose_exact(a):
    n = a.shape[0]
    eye = (lax.broadcasted_iota(jnp.int32, (n, n), 0) == lax.broadcasted_iota(jnp.int32, (n, n), 1)).astype(jnp.bfloat16)
    return sum(lax.dot_general(p, eye, (((0,), (0,)), ((), ())), preferred_element_type=F32) for p in _split3(a))


def _adamw_shards(items, name, steps, chip, jobs=()):
    n = len(items)

    def body(chip_ref, *refs):
        for a, it in enumerate(items):
            p_ref, r_ref, w_ref, m_ref, v_ref = refs[5 * a:5 * a + 5]
            g = ((p_ref[0].astype(F32) + r_ref[0].astype(F32)) + r_ref[1].astype(F32)) + r_ref[2].astype(F32)
            if it[5]:
                g = _transpose_exact(g)
            outs = refs[5 * n + 4 * a:5 * n + 4 * a + 4]
            outs[0][...] = g
            outs[1][...], outs[2][...], outs[3][...] = _adamw_math(w_ref[...], g, m_ref[...], v_ref[...])

    in_specs, out_specs, out_shape, flat = [], [], [], []
    for p, r, w, m, v, transposed in items:
        rows, cols = w.shape
        rb = rows // steps
        if transposed:
            in_specs += [pl.BlockSpec((1, cols, rb), lambda i, c: (c[0], 0, i)), pl.BlockSpec((3, cols, rb), lambda i, c: (0, 0, i))]
        else:
            in_specs += [pl.BlockSpec((1, rb, cols), lambda i, c: (c[0], i, 0)), pl.BlockSpec((3, rb, cols), lambda i, c: (0, i, 0))]
        wblk = pl.BlockSpec((rb, cols), lambda i, c: (i, 0))
        in_specs += [wblk] * 3
        out_specs += [wblk] * 4
        out_shape += [jax.ShapeDtypeStruct(w.shape, F32)] * 4
        flat += [p, r, w, m, v]
    return _call(body, name, (steps,), in_specs, out_specs, out_shape, [], flat, jobs, prefetch=chip)


def _sum_devices(gathered, name):
    def body(gs_ref, g_ref):
        g = gs_ref[0]
        for s in range(1, N_DEV):
            g = g + gs_ref[s]
        g_ref[...] = g

    return pl.pallas_call(body, name=name, out_shape=jax.ShapeDtypeStruct(gathered.shape[1:], F32),
                          in_specs=[_VMEM], out_specs=_VMEM, compiler_params=_params())(gathered)


def _adamw_native(items, name):
    n = len(items)

    def body(*refs):
        for a in range(n):
            g, w, m, v = (refs[4 * a + t][...] for t in range(4))
            refs[4 * n + 3 * a][...], refs[4 * n + 3 * a + 1][...], refs[4 * n + 3 * a + 2][...] = _adamw_math(w, g, m, v)

    return pl.pallas_call(
        body, name=name, out_shape=[jax.ShapeDtypeStruct(it[1].shape, F32) for it in items for _ in range(3)],
        in_specs=[_VMEM] * (4 * n), out_specs=[_VMEM] * (3 * n), compiler_params=_params(),
    )(*[t for it in items for t in it])


SMALL = ["ln_in_g", "ln_in_b", "s5_lambda_re", "s5_lambda_im", "s5_log_dt", "s5_b_re", "s5_b_im", "s5_c_re", "s5_c_im",
         "s5_d", "s5_b_glu", "ret_gn_g", "ret_gn_b", "ln1_g", "ln1_b", "ln2_g", "ln2_b"]
LATE = ["ln_in_g", "ln_in_b", "meta_tokens"]
EARLY = [n for n in SMALL if n not in LATE] + ["s5_w_glu", "loss"]
LANE = 128


def _pack(arrs):
    parts = []
    for a in arrs:
        f = a.reshape(-1)
        parts.append(jnp.pad(f, (0, (-f.shape[0]) % LANE)))
    flat = jnp.concatenate(parts)
    rows = -(-flat.shape[0] // LANE)
    flat = jnp.pad(flat, (0, (-rows % 8) * LANE + rows * LANE - flat.shape[0]))
    return flat.reshape(-1, LANE)


def _unpack(packed, shapes):
    flat = packed.reshape(-1)
    out, off = [], 0
    for s in shapes:
        n = math.prod(s)
        out.append(flat[off:off + n].reshape(s))
        off += n + (-n) % LANE
    return out


def _rope_tables(tp):
    pos = jnp.arange(tp, dtype=F32) - float(PAD)
    inv_freq = 1.0 / (ROPE_BASE ** (jnp.arange(0, HEAD, 2, dtype=F32) / HEAD))
    ang = pos[:, None] * inv_freq[None, :]
    cos, sin = jnp.cos(ang), jnp.sin(ang)
    return jnp.concatenate([cos, cos], axis=1), jnp.concatenate([-sin, sin], axis=1)


RET_CHUNK = ROW_BLK


def _decay_tables():
    log_gamma = jnp.log1p(-jnp.exp2(-5.0 - jnp.arange(RET_H, dtype=F32)))
    idx = jnp.arange(RET_CHUNK, dtype=F32)
    diff = idx[:, None] - idx[None, :]
    dmat = jnp.where(diff[None] >= 0, jnp.exp(jnp.maximum(diff, 0.0)[None] * log_gamma[:, None, None]), 0.0)
    zeta = jnp.exp((RET_CHUNK - 1.0 - idx)[None] * log_gamma[:, None])
    xi = jnp.exp((idx + 1.0)[None] * log_gamma[:, None])
    gam = jnp.exp(RET_CHUNK * log_gamma)
    wide = lambda t: jnp.broadcast_to(t[:, :, None], (RET_H, RET_CHUNK, HEAD))
    return dmat, wide(zeta), wide(xi), jnp.broadcast_to(gam[:, None, None], (RET_H, HEAD, HEAD))


def _local_step(x2d, tgt, meta_full, w_int, w_out, w_up, w_down, w_glu, sp, distributed):
    tp = x2d.shape[0] + CHUNK
    row = lambda a: a.reshape(1, -1)
    cos2, sin2 = _rope_tables(tp)
    dmat, zeta_b, xi_b, gam_b = _decay_tables()
    li_g, li_b = row(sp["ln_in_g"]), row(sp["ln_in_b"])
    l1_g, l1_b, l2_g, l2_b = row(sp["ln1_g"]), row(sp["ln1_b"]), row(sp["ln2_g"]), row(sp["ln2_b"])
    gn_g, gn_b = row(sp["ret_gn_g"]), row(sp["ret_gn_b"])
    lre, lim = row(sp["s5_lambda_re"]), row(sp["s5_lambda_im"])
    ldt = row(jnp.repeat(sp["s5_log_dt"].reshape(-1), S5_P))
    to_t = lambda b: b.reshape(S5_G, S5_P, S5_H).transpose(2, 0, 1).reshape(S5_H, S5_N)
    bre_t, bim_t = to_t(sp["s5_b_re"]), to_t(sp["s5_b_im"])
    to_w = lambda c: jnp.tile(c.reshape(S5_W, S5_P), (1, 2))
    cre_w, cim_w = to_w(sp["s5_c_re"]), to_w(sp["s5_c_im"])

    jobs = (lambda *j: list(j)) if distributed else (lambda *j: [])
    c_arr = jnp.reshape(lax.axis_index("c"), (1,)).astype(jnp.int32) if distributed else None
    (xhat0, rstd0), bg = _ln_in(x2d, meta_full, jobs(*([_job_gather(w_int), _job_gather(w_glu)] if distributed else [])))
    if distributed:
        w_int, w_glu = bg[0].reshape(PROJ_W, D_MODEL), bg[1].reshape(S5_W, S5_W)
    s5_small = (lre, lim, ldt, bre_t, bim_t, cre_w, cim_w, row(sp["s5_d"]), w_glu, row(sp["s5_b_glu"]))
    (u, q, k, v, gate), bg = _in_proj(xhat0, li_g, li_b, w_int, cos2, sin2,
                                      jobs(_job_gather(w_out) if distributed else None))
    if distributed:
        w_out = bg[0].reshape(D_MODEL, D_MODEL)
    (ys5, xr, xi), bg = _s5_fwd(u, *s5_small, jobs=jobs(_job_gather(w_up) if distributed else None))
    if distributed:
        w_up = bg[0]
    (o, states), _ = _ret_fwd(q, k, v, dmat, zeta_b, xi_b, gam_b)
    (ycat, xhat1, rstd1, h1b, pre), bg = _post_up(o, gate, ys5, xhat0, gn_g, gn_b, li_g, li_b, l1_g, l1_b, w_out, w_up,
                                                  jobs(_job_gather(w_down) if distributed else None))
    if distributed:
        w_down = bg[0].reshape(D_FF, D_MODEL)
    dr2, dffb, loss8, dl2g, dl2b = _post_down(pre, xhat1, tgt, l1_g, l1_b, l2_g, l2_b, w_down)
    g_up, g_down, dh1m = _mlp_bwd(h1b, dffb, pre, w_up, w_down)
    (do, dgate, dys5, dh0r, g_out, dl1g, dl1b, dgng, dgnb), bg = _post_bwd(
        dh1m, dr2, xhat1, rstd1, ycat, o, gate, gn_g, gn_b, l1_g, w_out,
        jobs(*([_job_pair(g_up), _job_pair(g_down)] if distributed else [])))
    g_out = g_out.reshape(N_DEV, D_MODEL // N_DEV, D_MODEL)
    if distributed:
        p_up, p_down = _pair_sum([g_up, g_down], bg, c_arr, "pair_sum_mlp")
    (du, dlre, dlim, dldt, dbre_t, dbim_t, dcre, dcim, dd, dwglu, dbglu), bg = _s5_bwd(
        dys5, u, xr, xi, *s5_small,
        jobs=jobs(*([_job_chips(p_up), _job_pair(g_out)] if distributed else [])))
    if distributed:
        r_up = bg[0]
        (p_out,) = _pair_sum([g_out], bg[1:], c_arr, "pair_sum_out")
    (dq, dk, dv), bg = _ret_bwd(q, k, v, do, states, cos2, sin2, dmat, zeta_b, xi_b, gam_b,
                                jobs(_job_chips(p_down) if distributed else None))
    r_down = bg[0] if distributed else None
    from_t = lambda t: t.reshape(S5_H, S5_G, S5_P).transpose(1, 2, 0)
    small = {
        "s5_lambda_re": dlre, "s5_lambda_im": dlim, "s5_log_dt": dldt[:, :S5_G],
        "s5_b_re": from_t(dbre_t), "s5_b_im": from_t(dbim_t), "s5_c_re": dcre, "s5_c_im": dcim, "s5_d": dd,
        "s5_b_glu": dbglu, "ret_gn_g": dgng, "ret_gn_b": dgnb, "ln1_g": dl1g, "ln1_b": dl1b, "ln2_g": dl2g, "ln2_b": dl2b,
        "s5_w_glu": dwglu, "loss": loss8[0:1, 0:1]}
    early_pack = _pack([small[n] for n in EARLY])
    (grad_x, dmeta, g_int, dlig, dlib), bg = _in_bwd(du, dq, dk, dv, dgate, dh0r, xhat0, rstd0, li_g, li_b, w_int,
                                            jobs(*([_job_chips(p_out), _job_gather(early_pack)] if distributed else [])))
    small.update(ln_in_g=dlig, ln_in_b=dlib, meta_tokens=dmeta)
    g_int = g_int.reshape(N_DEV, PROJ_W // N_DEV, D_MODEL)
    if distributed:
        (r1_in,) = _exchange([_job_pair(g_int)], "exchange_pair_in")
        (p_in,) = _pair_sum([g_int], [r1_in], c_arr, "pair_sum_in")
        big = dict(chip_sums=[p_in, p_out, p_up, p_down], received=[None, bg[0], r_up, r_down], early=bg[1])
    else:
        big = dict(partials=[g_int, g_out, g_up, g_down])
    return grad_x, big, small


def kernel(x, meta_tokens, ln_in_g, ln_in_b, w_in, s5_lambda_re, s5_lambda_im, s5_log_dt, s5_b_re, s5_b_im, s5_c_re, s5_c_im, s5_d, s5_w_glu, s5_b_glu, ret_gn_g, ret_gn_b, w_out, ln1_g, ln1_b, w_up, w_down, ln2_g, ln2_b, loss_target, m_meta_tokens, m_ln_in_g, m_ln_in_b, m_w_in, m_s5_lambda_re, m_s5_lambda_im, m_s5_log_dt, m_s5_b_re, m_s5_b_im, m_s5_c_re, m_s5_c_im, m_s5_d, m_s5_w_glu, m_s5_b_glu, m_ret_gn_g, m_ret_gn_b, m_w_out, m_ln1_g, m_ln1_b, m_w_up, m_w_down, m_ln2_g, m_ln2_b, v_meta_tokens, v_ln_in_g, v_ln_in_b, v_w_in, v_s5_lambda_re, v_s5_lambda_im, v_s5_log_dt, v_s5_b_re, v_s5_b_im, v_s5_c_re, v_s5_c_im, v_s5_d, v_s5_w_glu, v_s5_b_glu, v_ret_gn_g, v_ret_gn_b, v_w_out, v_ln1_g, v_ln1_b, v_w_up, v_w_down, v_ln2_g, v_ln2_b):
    args = dict(locals())
    names = ["meta_tokens", "ln_in_g", "ln_in_b", "w_in", "s5_lambda_re", "s5_lambda_im", "s5_log_dt", "s5_b_re", "s5_b_im",
             "s5_c_re", "s5_c_im", "s5_d", "s5_w_glu", "s5_b_glu", "ret_gn_g", "ret_gn_b", "w_out", "ln1_g", "ln1_b",
             "w_up", "w_down", "ln2_g", "ln2_b"]
    ax, ay, ac = _place()
    me = 4 * ax + 2 * ay + ac

    (a_meta,) = _exchange([_job_gather(meta_tokens)], "gather_meta")
    meta_full = a_meta.transpose(1, 0, 2).reshape(N_META, D_MODEL)

    sp = {n: args[n] for n in SMALL}
    grad_x, big, small = _local_step(x[0], loss_target[0], meta_full, w_in[0].T.astype(MM), w_out[0].astype(MM),
                                   w_up[0].astype(MM), w_down[0].astype(MM), s5_w_glu[0].astype(MM), sp, True)

    j_arr = jnp.reshape(2 * ax + ay, (1,)).astype(jnp.int32)
    two_d = lambda a: a.reshape(a.shape[-2:])
    item = lambda n, p, r, t: (p, r, *(two_d(a) for a in (args[n], args["m_" + n], args["v_" + n])), t)
    late_pack = _pack([small[n] for n in LATE])
    mlp = ("w_out", "w_up", "w_down")
    res, (r_in, late_all) = _adamw_shards(
        [item(n, p, r, False) for n, p, r in zip(mlp, big["chip_sums"][1:], big["received"][1:])], "adamw_mlp", 8, j_arr,
        [_job_chips(big["chip_sums"][0]), _job_gather(late_pack)])
    res_in, _ = _adamw_shards([item("w_in", big["chip_sums"][0], r_in, True)], "adamw_in", 8, j_arr)
    upd = {"w_in": res_in}
    for idx, n in enumerate(mlp):
        upd[n] = res[4 * idx:4 * idx + 4]
    shard_grads = {n: upd[n][0] for n in upd}

    early_shapes = [args[n].shape for n in EARLY[:-2]] + [(S5_W, S5_W), (1,)]
    late_shapes = [args["ln_in_g"].shape, args["ln_in_b"].shape, (N_META, D_MODEL)]
    g_small = dict(zip(EARLY, _unpack(_sum_devices(big["early"], "sum_small_early"), early_shapes)))
    g_small.update(zip(LATE, _unpack(_sum_devices(late_all, "sum_small_late"), late_shapes)))
    loss = g_small["loss"].reshape(())

    shard_grads["meta_tokens"] = lax.dynamic_slice(g_small["meta_tokens"], (0, me * (D_MODEL // N_DEV)),
                                                   (N_META, D_MODEL // N_DEV))
    shard_grads["s5_w_glu"] = lax.dynamic_slice(g_small["s5_w_glu"], (me * (S5_W // N_DEV), 0),
                                                (S5_W // N_DEV, S5_W))[None]
    natives = SMALL + ["meta_tokens", "s5_w_glu"]
    res2 = _adamw_native([(shard_grads[n] if n in shard_grads else g_small[n], args[n], args["m_" + n], args["v_" + n])
                          for n in natives], "adamw_small")
    for idx, n in enumerate(natives):
        upd[n] = [shard_grads[n] if n in shard_grads else g_small[n]] + list(res2[3 * idx:3 * idx + 3])

    grads, deltas, new_m, new_v = ([upd[n][t].reshape(args[n].shape) for n in names] for t in range(4))
    return (loss, grad_x[None], *grads, *deltas, *new_m, *new_v)
```

```python
import math

import jax
import jax.numpy as jnp
from jax import lax
from jax.experimental import pallas as pl
from jax.experimental.pallas import tpu as pltpu

F32 = jnp.float32
MM = jnp.bfloat16

D_MODEL = 1024
N_META = 16
CHUNK = 128
PAD = CHUNK - N_META
S5_W, S5_G, S5_H, S5_P = 256, 16, 16, 64
S5_N = S5_G * S5_P
RET_W, RET_H, HEAD = 768, 6, 128
D_FF = 4096
PROJ_W = S5_W + 4 * RET_W
N_DEV = 8
FF_BLK = D_FF // N_DEV
ROW_BLK = 384
MLP_ROWS = 1408
PROJ_ROWS = 704
ALPHA = 2.0 ** 0.25
LN_EPS = 1e-5
GN_EPS = 1e-5
ROPE_BASE = 10000.0
GELU_C = math.sqrt(2.0 / math.pi)
GELU_A = 0.044715
ADAM_LR, ADAM_B1, ADAM_B2, ADAM_EPS, ADAM_WD, ADAM_STEP = 0.001, 0.9, 0.999, 1e-08, 0.01, 10
VMEM_LIMIT = 60 * 1024 * 1024

_VMEM = pl.BlockSpec(memory_space=pltpu.VMEM)
_ANY = pl.BlockSpec(memory_space=pl.ANY)
_MESH = pl.DeviceIdType.MESH


def _params(sem=None):
    return pltpu.CompilerParams(dimension_semantics=sem, vmem_limit_bytes=VMEM_LIMIT)


def _dot(a, b):
    return jnp.dot(a.astype(MM), b.astype(MM), preferred_element_type=F32)


def _dot_nt(a, b):
    return lax.dot_general(a.astype(MM), b.astype(MM), (((1,), (1,)), ((), ())), preferred_element_type=F32)


def _dot_tn(a, b):
    return lax.dot_general(a.astype(MM), b.astype(MM), (((0,), (0,)), ((), ())), preferred_element_type=F32)


def _split3(a):
    hi = a.astype(jnp.bfloat16)
    r1 = a - hi.astype(F32)
    mid = r1.astype(jnp.bfloat16)
    lo = (r1 - mid.astype(F32)).astype(jnp.bfloat16)
    return hi, mid, lo


def _dot_sel_rhs(a, sel):
    s = sel.astype(jnp.bfloat16)
    return sum(jnp.dot(p, s, preferred_element_type=F32) for p in _split3(a))


def _dot_sel_lhs(sel, b):
    s = sel.astype(jnp.bfloat16)
    return sum(jnp.dot(s, p, preferred_element_type=F32) for p in _split3(b))


def _ln_fwd(r, eps):
    mu = jnp.mean(r, axis=-1, keepdims=True)
    xc = r - mu
    var = jnp.mean(xc * xc, axis=-1, keepdims=True)
    rstd = lax.rsqrt(var + eps)
    return xc * rstd, rstd


def _ln_bwd(dxhat, xhat, rstd):
    m1 = jnp.mean(dxhat, axis=-1, keepdims=True)
    m2 = jnp.mean(dxhat * xhat, axis=-1, keepdims=True)
    return rstd * (dxhat - m1 - xhat * m2)


def _colsum(a):
    return jnp.sum(a, axis=0, keepdims=True)


def _shift3(n_in):
    return [pl.BlockSpec((CHUNK, D_MODEL), (lambda i, j=j: (jnp.clip(3 * i - 1 + j, 0, n_in - 1), 0))) for j in range(3)]


def _ln_in(x2d, meta_full, jobs=()):
    seq = x2d.shape[0]
    tp = seq + CHUNK
    R = ROW_BLK

    def body(xa, xb, xc, meta_ref, xhat_ref, rstd_ref, raw_ref):
        raw_ref[0:CHUNK, :] = xa[...]
        raw_ref[CHUNK:2 * CHUNK, :] = xb[...]
        raw_ref[2 * CHUNK:3 * CHUNK, :] = xc[...]

        @pl.when(pl.program_id(0) == 0)
        def _():
            raw_ref[0:PAD, :] = jnp.zeros((PAD, D_MODEL), F32)
            raw_ref[PAD:CHUNK, :] = meta_ref[...]

        xhat_ref[...], rstd_ref[...] = _ln_fwd(raw_ref[...], LN_EPS)

    row = lambda w: pl.BlockSpec((R, w), lambda i: (i, 0))
    return _call(
        body, "ln_in", (tp // R,),
        _shift3(seq // CHUNK) + [pl.BlockSpec((N_META, D_MODEL), lambda i: (0, 0))],
        [row(D_MODEL), row(1)], [jax.ShapeDtypeStruct((tp, D_MODEL), F32), jax.ShapeDtypeStruct((tp, 1), F32)],
        [pltpu.VMEM((R, D_MODEL), F32)], (x2d, x2d, x2d, meta_full), jobs)


def _in_proj(xhat0, ln_g, ln_b, w_int, cos2, sin2, jobs=()):
    tp = xhat0.shape[0]
    R = PROJ_ROWS if tp % PROJ_ROWS == 0 else ROW_BLK

    def body(xh_ref, g_ref, b_ref, w_ref, cos_ref, sin_ref, u_ref, q_ref, k_ref, v_ref, gate_ref):
        hb = (xh_ref[...] * g_ref[...] + b_ref[...]).astype(MM)
        valid = (pl.program_id(0) * R + lax.broadcasted_iota(jnp.int32, (R, 1), 0)) >= PAD

        def seg(lo, hi):
            return jnp.where(valid, _dot_nt(hb, w_ref[lo:hi, :]), 0.0)

        u_ref[...] = seg(0, S5_W)
        cos = cos_ref[...]
        sin = sin_ref[...]
        q = seg(S5_W, S5_W + RET_W)
        k = seg(S5_W + RET_W, S5_W + 2 * RET_W)
        for h in range(RET_H):
            sl = slice(h * HEAD, (h + 1) * HEAD)
            qh = q[:, sl]
            kh = k[:, sl]
            q_ref[:, sl] = (qh * cos + pltpu.roll(qh, HEAD // 2, 1) * sin).astype(q_ref.dtype)
            k_ref[:, sl] = ((kh * cos + pltpu.roll(kh, HEAD // 2, 1) * sin) * (HEAD ** -0.5)).astype(k_ref.dtype)
        v_ref[...] = seg(S5_W + 2 * RET_W, S5_W + 3 * RET_W).astype(v_ref.dtype)
        gate_ref[...] = seg(S5_W + 3 * RET_W, PROJ_W)

    def rows(w, dt):
        return pl.BlockSpec((R, w), lambda i: (i, 0)), jax.ShapeDtypeStruct((tp, w), dt)

    outs = [rows(S5_W, F32), rows(RET_W, MM), rows(RET_W, MM), rows(RET_W, MM), rows(RET_W, F32)]
    full = lambda s: pl.BlockSpec(s, lambda i: (0,) * len(s))
    return _call(
        body, "in_proj", (tp // R,),
        [pl.BlockSpec((R, D_MODEL), lambda i: (i, 0)), full((1, D_MODEL)), full((1, D_MODEL)), _VMEM,
         pl.BlockSpec((R, HEAD), lambda i: (i, 0)), pl.BlockSpec((R, HEAD), lambda i: (i, 0))],
        [o[0] for o in outs], [o[1] for o in outs], [], (xhat0, ln_g, ln_b, w_int, cos2, sin2), jobs)


def _s5_disc(lre, lim, ldt, bre_t, bim_t):
    dt = jnp.exp(ldt)
    mag = jnp.exp(lre * dt)
    ang = lim * dt
    lbr = mag * jnp.cos(ang)
    lbi = mag * jnp.sin(ang)
    den = lre * lre + lim * lim
    nr = lbr - 1.0
    qr = (nr * lre + lbi * lim) / den
    qi = (lbi * lre - nr * lim) / den
    return lbr, lbi, qr * bre_t - qi * bim_t, qr * bim_t + qi * bre_t


def _s5_tables(lbr, lbi, reverse):
    if reverse:
        lbi = -lbi
    pw = [(lbr, lbi)]
    for _ in range(7):
        r, i = pw[-1]
        pw.append((r * lbr - i * lbi, r * lbi + i * lbr))
    row = lax.broadcasted_iota(jnp.int32, (8, S5_N), 0)
    tabs = []
    for k in range(3):
        sh = 2 ** k
        mask = (row < 8 - sh) if reverse else (row >= sh)
        ar, ai = pw[sh - 1]
        tabs.append((jnp.where(mask, ar, 0.0), jnp.where(mask, ai, 0.0)))
    pr = jnp.zeros((8, S5_N), F32)
    pi = jnp.zeros((8, S5_N), F32)
    for i in range(8):
        ar, ai = pw[7 - i] if reverse else pw[i]
        pr = jnp.where(row == i, ar, pr)
        pi = jnp.where(row == i, ai, pi)
    tabs.append((pr, pi))
    return tabs


def _store_tables(tab_ref, tabs):
    for k, (r, i) in enumerate(tabs):
        tab_ref[2 * k] = r
        tab_ref[2 * k + 1] = i


def _bd_mask():
    r = lax.broadcasted_iota(jnp.int32, (S5_W, S5_N), 0)
    c = lax.broadcasted_iota(jnp.int32, (S5_W, S5_N), 1)
    return jnp.right_shift(r, 4) == jnp.right_shift(c, 6)


def _s5_block_diag(bbr_t, bbi_t, cre_w, cim_w):
    mask = _bd_mask()
    bd = lambda t: jnp.where(mask, t, 0.0)
    return (bd(jnp.tile(bbr_t, (S5_G, 1))), bd(jnp.tile(bbi_t, (S5_G, 1))),
            bd(jnp.tile(cre_w, (1, S5_N // HEAD))), bd(jnp.tile(cim_w, (1, S5_N // HEAD))))


def _scan8(xr, xi, tab_ref, lanes, reverse):
    for k in range(3):
        sh = (8 - 2 ** k) if reverse else 2 ** k
        sr = pltpu.roll(xr, sh, 0)
        si = pltpu.roll(xi, sh, 0)
        mr = tab_ref[2 * k, :, lanes]
        mi = tab_ref[2 * k + 1, :, lanes]
        xr, xi = xr + (mr * sr - mi * si), xi + (mr * si + mi * sr)
    return xr, xi


S5_LANES = 256


def _gelu(y):
    t = jnp.tanh(GELU_C * (y + GELU_A * y * y * y))
    return 0.5 * y * (1.0 + t), t


def _s5_fwd(u, lre, lim, ldt, bre_t, bim_t, cre_w, cim_w, d_row, w_glu, b_glu, jobs=()):
    tp = u.shape[0]
    R = ROW_BLK

    def body(u_ref, lre_ref, lim_ref, ldt_ref, bre_ref, bim_ref, cre_ref, cim_ref, d_ref, wg_ref, bg_ref,
             y_ref, xr_ref, xi_ref, bbd_r, bbd_i, cbd_r, cbd_i, tab_ref, car_r, car_i):
        @pl.when(pl.program_id(0) == 0)
        def _():
            lbr, lbi, bbr, bbi = _s5_disc(lre_ref[...], lim_ref[...], ldt_ref[...], bre_ref[...], bim_ref[...])
            br, bi, cr, ci = _s5_block_diag(bbr, bbi, cre_ref[...], cim_ref[...])
            bbd_r[...] = br.astype(MM)
            bbd_i[...] = bi.astype(MM)
            cbd_r[...] = cr.astype(MM)
            cbd_i[...] = ci.astype(MM)
            _store_tables(tab_ref, _s5_tables(lbr, lbi, False))
            car_r[...] = jnp.zeros_like(car_r)
            car_i[...] = jnp.zeros_like(car_i)

        u = u_ref[...]
        ub = u.astype(MM)
        xr_ref[...] = jnp.dot(ub, bbd_r[...], preferred_element_type=F32)
        xi_ref[...] = jnp.dot(ub, bbd_i[...], preferred_element_type=F32)
        for j in range(S5_N // S5_LANES):
            lanes = pl.ds(j * S5_LANES, S5_LANES)
            pr = tab_ref[6, :, lanes]
            pi = tab_ref[7, :, lanes]

            def step(g, carry):
                cr, ci = carry
                rows = pl.ds(pl.multiple_of(g * 8, 8), 8)
                xr, xi = _scan8(xr_ref[rows, lanes], xi_ref[rows, lanes], tab_ref, lanes, False)
                br = jnp.broadcast_to(cr[7:8, :], cr.shape)
                bi = jnp.broadcast_to(ci[7:8, :], ci.shape)
                xr = xr + (pr * br - pi * bi)
                xi = xi + (pr * bi + pi * br)
                xr_ref[rows, lanes] = xr
                xi_ref[rows, lanes] = xi
                return xr, xi

            cr, ci = lax.fori_loop(0, R // 8, step, (car_r[:, lanes], car_i[:, lanes]), unroll=2)
            car_r[:, lanes] = cr
            car_i[:, lanes] = ci
        y = _dot_nt(xr_ref[...], cbd_r[...]) - _dot_nt(xi_ref[...], cbd_i[...]) + d_ref[...] * u
        yg, _ = _gelu(y)
        z = _dot(yg, wg_ref[...]) + bg_ref[...]
        y_ref[...] = yg * jax.nn.sigmoid(z)

    full = lambda a: pl.BlockSpec(a.shape, lambda i: (0,) * a.ndim)
    small = [lre, lim, ldt, bre_t, bim_t, cre_w, cim_w, d_row, w_glu, b_glu]
    return _call(
        body, "s5_fwd", (tp // R,),
        [pl.BlockSpec((R, S5_W), lambda i: (i, 0))] + [full(a) for a in small],
        [pl.BlockSpec((R, S5_W), lambda i: (i, 0)), pl.BlockSpec((R, S5_N), lambda i: (i, 0)),
         pl.BlockSpec((R, S5_N), lambda i: (i, 0))],
        [jax.ShapeDtypeStruct((tp, S5_W), F32), jax.ShapeDtypeStruct((tp, S5_N), F32),
         jax.ShapeDtypeStruct((tp, S5_N), F32)],
        [pltpu.VMEM((S5_W, S5_N), MM)] * 4 + [pltpu.VMEM((8, 8, S5_N), F32), pltpu.VMEM((8, S5_N), F32),
                                              pltpu.VMEM((8, S5_N), F32)],
        (u, *small), jobs)


def _s5_bwd(dy_out, u, xr, xi, lre, lim, ldt, bre_t, bim_t, cre_w, cim_w, d_row, w_glu, b_glu, jobs=()):
    tp = u.shape[0]
    R = ROW_BLK
    nb = tp // R

    def body(dyo_ref, u_ref, xr_ref, xi_ref, xpr_ref, xpi_ref,
             lre_ref, lim_ref, ldt_ref, bre_ref, bim_ref, cre_ref, cim_ref, d_ref, wg_ref, bg_ref,
             du_ref, dlre_ref, dlim_ref, dldt_ref, dbre_ref, dbim_ref, dcre_ref, dcim_ref, dd_ref, dwg_ref, dbg_ref,
             bbd_r, bbd_i, cbd_r, cbd_i, tab_ref, car_r, car_i, gr_ref, gi_ref, xer_ref, xei_ref,
             abr, abi, acr, aci, adr, adi):
        i = pl.program_id(0)

        @pl.when(i == 0)
        def _():
            lbr, lbi, bbr, bbi = _s5_disc(lre_ref[...], lim_ref[...], ldt_ref[...], bre_ref[...], bim_ref[...])
            br, bi, cr, ci = _s5_block_diag(bbr, bbi, cre_ref[...], cim_ref[...])
            bbd_r[...] = br.astype(MM)
            bbd_i[...] = bi.astype(MM)
            cbd_r[...] = cr.astype(MM)
            cbd_i[...] = ci.astype(MM)
            _store_tables(tab_ref, _s5_tables(lbr, lbi, True))
            for ref in (car_r, car_i, abr, abi, acr, aci, adr, adi, dd_ref, dwg_ref, dbg_ref):
                ref[...] = jnp.zeros_like(ref)

        u = u_ref[...]
        xrv = xr_ref[...]
        xiv = xi_ref[...]
        y = _dot_nt(xrv, cbd_r[...]) - _dot_nt(xiv, cbd_i[...]) + d_ref[...] * u
        yg, t = _gelu(y)
        z = _dot(yg, wg_ref[...]) + bg_ref[...]
        s = jax.nn.sigmoid(z)
        dout = dyo_ref[...]
        dz = dout * yg * s * (1.0 - s)
        dyg = dout * s + _dot_nt(dz, wg_ref[...])
        dwg_ref[...] += _dot_tn(yg, dz)
        dbg_ref[...] += _colsum(dz)
        dy = dyg * (0.5 * (1.0 + t) + 0.5 * y * (1.0 - t * t) * GELU_C * (1.0 + 3.0 * GELU_A * y * y))
        dd_ref[...] += _colsum(dy * u)
        acr[...] += _dot_tn(dy, xrv)
        aci[...] -= _dot_tn(dy, xiv)
        gr_ref[...] = _dot(dy, cbd_r[...])
        gi_ref[...] = -_dot(dy, cbd_i[...])
        has_prev = (i < nb - 1).astype(F32)
        xer_ref[0:8, :] = xpr_ref[...] * has_prev
        xei_ref[0:8, :] = xpi_ref[...] * has_prev
        xer_ref[8:R + 8, :] = xrv
        xei_ref[8:R + 8, :] = xiv
        row = lax.broadcasted_iota(jnp.int32, (8, S5_LANES), 0)
        for j in range(S5_N // S5_LANES):
            lanes = pl.ds(j * S5_LANES, S5_LANES)
            pr = tab_ref[6, :, lanes]
            pi = tab_ref[7, :, lanes]

            def step(n, carry):
                cr, ci, sar, sai = carry
                g = R // 8 - 1 - n
                r0 = pl.multiple_of(g * 8, 8)
                rows = pl.ds(r0, 8)
                gr, gi = _scan8(gr_ref[rows, lanes], gi_ref[rows, lanes], tab_ref, lanes, True)
                br = jnp.broadcast_to(cr[0:1, :], cr.shape)
                bi = jnp.broadcast_to(ci[0:1, :], ci.shape)
                gr = gr + (pr * br - pi * bi)
                gi = gi + (pr * bi + pi * br)
                gr_ref[rows, lanes] = gr
                gi_ref[rows, lanes] = gi
                last = row == 7
                xpr = pltpu.roll(jnp.where(last, xer_ref[rows, lanes], xer_ref[pl.ds(r0 + 8, 8), lanes]), 1, 0)
                xpi = pltpu.roll(jnp.where(last, xei_ref[rows, lanes], xei_ref[pl.ds(r0 + 8, 8), lanes]), 1, 0)
                return gr, gi, sar + (gr * xpr + gi * xpi), sai + (gi * xpr - gr * xpi)

            cr, ci, sar, sai = lax.fori_loop(
                0, R // 8, step, (car_r[:, lanes], car_i[:, lanes], adr[:, lanes], adi[:, lanes]), unroll=2)
            car_r[:, lanes] = cr
            car_i[:, lanes] = ci
            adr[:, lanes] = sar
            adi[:, lanes] = sai
        grv = gr_ref[...]
        giv = gi_ref[...]
        du_ref[...] = (dy * d_ref[...] + _dot_nt(grv, bbd_r[...]) + _dot_nt(giv, bbd_i[...])).astype(du_ref.dtype)
        abr[...] += _dot_tn(u, grv)
        abi[...] += _dot_tn(u, giv)

        @pl.when(i == nb - 1)
        def _():
            mask = _bd_mask()
            r16 = lax.broadcasted_iota(jnp.int32, (S5_H, S5_W), 1)
            h16 = lax.broadcasted_iota(jnp.int32, (S5_H, S5_W), 0)
            fold_b = jnp.bitwise_and(r16, S5_H - 1) == h16
            c64 = lax.broadcasted_iota(jnp.int32, (S5_N, S5_P), 0)
            p64 = lax.broadcasted_iota(jnp.int32, (S5_N, S5_P), 1)
            fold_c = jnp.bitwise_and(c64, S5_P - 1) == p64
            dbbr = _dot_sel_lhs(fold_b, jnp.where(mask, abr[...], 0.0))
            dbbi = _dot_sel_lhs(fold_b, jnp.where(mask, abi[...], 0.0))
            dcre_ref[...] = _dot_sel_rhs(jnp.where(mask, acr[...], 0.0), fold_c)
            dcim_ref[...] = _dot_sel_rhs(jnp.where(mask, aci[...], 0.0), fold_c)
            dlbr = _colsum(adr[...])
            dlbi = _colsum(adi[...])
            _, vjp = jax.vjp(_s5_disc, lre_ref[...], lim_ref[...], ldt_ref[...], bre_ref[...], bim_ref[...])
            dlre, dlim, dldt, dbre, dbim = vjp((dlbr, dlbi, dbbr, dbbi))
            dlre_ref[...] = dlre
            dlim_ref[...] = dlim
            dbre_ref[...] = dbre
            dbim_ref[...] = dbim
            gsel = jnp.right_shift(lax.broadcasted_iota(jnp.int32, (S5_N, HEAD), 0), 6) == \
                lax.broadcasted_iota(jnp.int32, (S5_N, HEAD), 1)
            dldt_ref[...] = _dot_sel_rhs(dldt, gsel)

    full = lambda a: pl.BlockSpec(a.shape, lambda i: (0,) * a.ndim)
    rev = lambda w: pl.BlockSpec((R, w), lambda i: (nb - 1 - i, 0))
    prev8 = pl.BlockSpec((8, S5_N), lambda i: (jnp.maximum((nb - 1 - i) * (R // 8) - 1, 0), 0))
    small = [lre, lim, ldt, bre_t, bim_t, cre_w, cim_w, d_row, w_glu, b_glu]
    outs = [((tp, S5_W), rev(S5_W))] + [
        (s, pl.BlockSpec(s, lambda i: (0, 0))) for s in
        [(1, S5_N), (1, S5_N), (1, HEAD), (S5_H, S5_N), (S5_H, S5_N), (S5_W, S5_P), (S5_W, S5_P),
         (1, S5_W), (S5_W, S5_W), (1, S5_W)]]
    return _call(
        body, "s5_bwd", (nb,),
        [rev(S5_W), rev(S5_W), rev(S5_N), rev(S5_N), prev8, prev8] + [full(a) for a in small],
        [o[1] for o in outs], [jax.ShapeDtypeStruct(o[0], MM if n == 0 else F32) for n, o in enumerate(outs)],
        [pltpu.VMEM((S5_W, S5_N), MM)] * 4 + [
            pltpu.VMEM((8, 8, S5_N), F32), pltpu.VMEM((8, S5_N), F32), pltpu.VMEM((8, S5_N), F32),
            pltpu.VMEM((R, S5_N), F32), pltpu.VMEM((R, S5_N), F32),
            pltpu.VMEM((R + 8, S5_N), F32), pltpu.VMEM((R + 8, S5_N), F32)] + [pltpu.VMEM((S5_W, S5_N), F32)] * 4 + [
            pltpu.VMEM((8, S5_N), F32), pltpu.VMEM((8, S5_N), F32)],
        (dy_out, u, xr, xi, xr, xi, *small), jobs)


def _ret_fwd(q, k, v, dmat, zeta_b, xi_b, gam_b, jobs=()):
    tp = q.shape[0]
    C = dmat.shape[1]
    nc = tp // C

    def body(q_ref, k_ref, v_ref, dm_ref, ze_ref, xi_ref, ga_ref, o_ref, st_ref, s_ref):
        @pl.when(pl.program_id(0) == 0)
        def _():
            s_ref[...] = jnp.zeros_like(s_ref)

        for h in range(RET_H):
            sl = slice(h * HEAD, (h + 1) * HEAD)
            qh, kh, vh = q_ref[:, sl], k_ref[:, sl], v_ref[:, sl]
            sh = s_ref[h]
            st_ref[0, sl, :] = sh
            scores = _dot_nt(qh, kh) * dm_ref[h]
            o_ref[:, sl] = _dot(scores, vh) + _dot(qh, sh) * xi_ref[h]
            s_ref[h] = ga_ref[h] * sh + _dot_tn(kh.astype(F32) * ze_ref[h], vh)

    blk = pl.BlockSpec((C, RET_W), lambda c: (c, 0))
    cst = lambda a: pl.BlockSpec(a.shape, lambda c: (0, 0, 0))
    return _call(
        body, "ret_fwd", (nc,), [blk, blk, blk, cst(dmat), cst(zeta_b), cst(xi_b), cst(gam_b)],
        [blk, pl.BlockSpec((1, RET_W, HEAD), lambda c: (c, 0, 0))],
        [jax.ShapeDtypeStruct((tp, RET_W), F32), jax.ShapeDtypeStruct((nc, RET_W, HEAD), F32)],
        [pltpu.VMEM((RET_H, HEAD, HEAD), F32)], (q, k, v, dmat, zeta_b, xi_b, gam_b), jobs)


def _ret_bwd(q, k, v, do, states, cos2, sin2, dmat, zeta_b, xi_b, gam_b, jobs=()):
    tp = q.shape[0]
    C = dmat.shape[1]
    nc = tp // C

    def body(q_ref, k_ref, v_ref, do_ref, st_ref, cos_ref, sin_ref, dm_ref, ze_ref, xi_ref, ga_ref,
             dq_ref, dk_ref, dv_ref, ds_ref):
        @pl.when(pl.program_id(0) == 0)
        def _():
            ds_ref[...] = jnp.zeros_like(ds_ref)

        cos = cos_ref[...]
        sin = sin_ref[...]
        for h in range(RET_H):
            sl = slice(h * HEAD, (h + 1) * HEAD)
            qh, kh, vh = q_ref[:, sl], k_ref[:, sl], v_ref[:, sl]
            dmh = dm_ref[h]
            sh = st_ref[0, sl, :]
            dsn = ds_ref[h]
            doh = do_ref[:, sl]
            dox = doh * xi_ref[h]
            a = _dot_nt(qh, kh) * dmh
            dqk = _dot_nt(doh, vh) * dmh
            kz = kh.astype(F32) * ze_ref[h]
            dv_ref[:, sl] = (_dot_tn(a, doh) + _dot(kz, dsn)).astype(dv_ref.dtype)
            dqr = _dot(dqk, kh) + _dot_nt(dox, sh)
            dkr = _dot_tn(dqk, qh) + ze_ref[h] * _dot_nt(vh, dsn)
            ds_ref[h] = ga_ref[h] * dsn + _dot_tn(qh, dox)
            dq_ref[:, sl] = (dqr * cos - pltpu.roll(dqr, HEAD // 2, 1) * sin).astype(dq_ref.dtype)
            dk_ref[:, sl] = ((dkr * cos - pltpu.roll(dkr, HEAD // 2, 1) * sin) * (HEAD ** -0.5)).astype(dk_ref.dtype)

    blk = pl.BlockSpec((C, RET_W), lambda c: (nc - 1 - c, 0))
    tab = pl.BlockSpec((C, HEAD), lambda c: (nc - 1 - c, 0))
    cst = lambda a: pl.BlockSpec(a.shape, lambda c: (0, 0, 0))
    return _call(
        body, "ret_bwd", (nc,),
        [blk, blk, blk, blk, pl.BlockSpec((1, RET_W, HEAD), lambda c: (nc - 1 - c, 0, 0)), tab, tab,
         cst(dmat), cst(zeta_b), cst(xi_b), cst(gam_b)],
        [blk, blk, blk], [jax.ShapeDtypeStruct((tp, RET_W), MM)] * 3, [pltpu.VMEM((RET_H, HEAD, HEAD), F32)],
        (q, k, v, do, states, cos2, sin2, dmat, zeta_b, xi_b, gam_b), jobs)


def _gn_gate(o, gate, gn_g, gn_b):
    xhat, rstd = _ln_fwd(o, GN_EPS)
    on = xhat * gn_g + gn_b
    s = jax.nn.sigmoid(gate)
    return gate * s * on, xhat, rstd, on, s


def _post_up(o, gate, ys5, xhat0, gn_g, gn_b, li_g, li_b, l1_g, l1_b, w_out, w_up, jobs=()):
    tp = o.shape[0]
    R = ROW_BLK

    def body(o_ref, g_ref, ys_ref, xh0_ref, gng, gnb, lig, lib, l1g, l1b, wo_ref, wu_ref,
             ycat_ref, xh1_ref, rstd1_ref, h1b_ref, pre_ref):
        ycat_ref[:, 0:S5_W] = ys_ref[...].astype(ycat_ref.dtype)
        for h in range(RET_H):
            sl = slice(h * HEAD, (h + 1) * HEAD)
            yret = _gn_gate(o_ref[:, sl], g_ref[:, sl], gng[:, sl], gnb[:, sl])[0]
            ycat_ref[:, S5_W + h * HEAD:S5_W + (h + 1) * HEAD] = yret.astype(ycat_ref.dtype)
        mixed = _dot(ycat_ref[...], wo_ref[...])
        h0 = xh0_ref[...] * lig[...] + lib[...]
        xh1, rstd1 = _ln_fwd(ALPHA * h0 + mixed, LN_EPS)
        xh1_ref[...] = xh1
        rstd1_ref[...] = rstd1
        h1b = (xh1 * l1g[...] + l1b[...]).astype(MM)
        h1b_ref[...] = h1b
        for d in range(N_DEV):
            pre_ref[:, d * FF_BLK:(d + 1) * FF_BLK] = jnp.maximum(_dot(h1b, wu_ref[d]), 0.0)

    row = lambda w: pl.BlockSpec((R, w), lambda i: (i, 0))
    full = lambda a: pl.BlockSpec(a.shape, lambda i: (0,) * a.ndim)
    vecs = [gn_g, gn_b, li_g, li_b, l1_g, l1_b]
    outs = [(row(D_MODEL), jax.ShapeDtypeStruct((tp, D_MODEL), MM)), (row(D_MODEL), jax.ShapeDtypeStruct((tp, D_MODEL), F32)),
            (row(1), jax.ShapeDtypeStruct((tp, 1), F32)), (row(D_MODEL), jax.ShapeDtypeStruct((tp, D_MODEL), MM)),
            (row(D_FF), jax.ShapeDtypeStruct((tp, D_FF), F32))]
    return _call(
        body, "post_up", (tp // R,),
        [row(RET_W), row(RET_W), row(S5_W), row(D_MODEL)] + [full(a) for a in vecs] + [_VMEM, _VMEM],
        [o[0] for o in outs], [o[1] for o in outs], [], (o, gate, ys5, xhat0, *vecs, w_out, w_up), jobs)


def _post_down(pre, xhat1, tgt, l1_g, l1_b, l2_g, l2_b, w_down):
    tp = pre.shape[0]
    seq = tgt.shape[0]
    R = ROW_BLK

    def body(pre_ref, xh1_ref, ta, tb, tc, l1g, l1b, l2g, l2b, wd_ref,
             dr2_ref, dffb_ref, loss_ref, dl2g_ref, dl2b_ref, tgt_ref):
        i = pl.program_id(0)

        @pl.when(i == 0)
        def _():
            for ref in (loss_ref, dl2g_ref, dl2b_ref):
                ref[...] = jnp.zeros_like(ref)

        tgt_ref[0:CHUNK, :] = ta[...]
        tgt_ref[CHUNK:2 * CHUNK, :] = tb[...]
        tgt_ref[2 * CHUNK:3 * CHUNK, :] = tc[...]
        ff = jnp.zeros((R, D_MODEL), F32)
        for d in range(N_DEV):
            pre = pre_ref[:, d * FF_BLK:(d + 1) * FF_BLK]
            ff = ff + _dot(pre * pre, wd_ref[d * FF_BLK:(d + 1) * FF_BLK, :])
        h1 = xh1_ref[...] * l1g[...] + l1b[...]
        xh2, rstd2 = _ln_fwd(ALPHA * h1 + ff, LN_EPS)
        h2 = xh2 * l2g[...] + l2b[...]
        valid = (i * R + lax.broadcasted_iota(jnp.int32, (R, 1), 0)) >= CHUNK
        err = jnp.where(valid, h2 - tgt_ref[...], 0.0)
        loss_ref[...] += 0.5 * jnp.sum(err * err) / D_MODEL
        dh2 = err * (1.0 / D_MODEL)
        dl2g_ref[...] += _colsum(dh2 * xh2)
        dl2b_ref[...] += _colsum(dh2)
        dr2 = _ln_bwd(dh2 * l2g[...], xh2, rstd2)
        dr2_ref[...] = dr2
        dffb_ref[...] = dr2.astype(MM)

    row = lambda w: pl.BlockSpec((R, w), lambda i: (i, 0))
    full = lambda a: pl.BlockSpec(a.shape, lambda i: (0,) * a.ndim)
    vecs = [l1_g, l1_b, l2_g, l2_b]
    acc = lambda s: (pl.BlockSpec(s, lambda i: (0, 0)), jax.ShapeDtypeStruct(s, F32))
    outs = [(row(D_MODEL), jax.ShapeDtypeStruct((tp, D_MODEL), F32)), (row(D_MODEL), jax.ShapeDtypeStruct((tp, D_MODEL), MM)),
            acc((8, HEAD)), acc((1, D_MODEL)), acc((1, D_MODEL))]
    return pl.pallas_call(
        body, name="post_down", grid=(tp // R,),
        in_specs=[row(D_FF), row(D_MODEL)] + _shift3(seq // CHUNK) + [full(a) for a in vecs] + [_VMEM],
        out_specs=[o[0] for o in outs], out_shape=[o[1] for o in outs],
        scratch_shapes=[pltpu.VMEM((R, D_MODEL), F32)],
        compiler_params=_params(("arbitrary",)),
    )(pre, xhat1, tgt, tgt, tgt, *vecs, w_down)


def _mlp_bwd(h1b, dffb, pre, w_up, w_down):
    tp = h1b.shape[0]
    R = MLP_ROWS if tp % MLP_ROWS == 0 else ROW_BLK
    nr = tp // R

    def body(h_ref, df_ref, pre_ref, wu_ref, wd_ref, gup_ref, gdn_ref, dh1_ref, aup, adn):
        d = pl.program_id(0)
        r = pl.program_id(1)

        @pl.when(r == 0)
        def _():
            aup[...] = jnp.zeros_like(aup)
            adn[...] = jnp.zeros_like(adn)

        h = h_ref[...]
        df = df_ref[...]
        wu = wu_ref[0]
        wd = wd_ref[0]
        pre = pre_ref[...]
        dpre = (_dot_nt(df, wd) * (2.0 * pre)).astype(MM)

        aup[...] += _dot_tn(h, dpre)
        adn[...] += _dot_tn(pre * pre, df)
        contrib = _dot_nt(dpre, wu)
        rows = pl.ds(pl.multiple_of(r * R, 64), R)

        @pl.when(d == 0)
        def _():
            dh1_ref[rows, :] = contrib

        @pl.when(d > 0)
        def _():
            dh1_ref[rows, :] += contrib

        @pl.when(r == nr - 1)
        def _():
            gup_ref[0] = aup[...].astype(gup_ref.dtype)
            gdn_ref[0] = adn[...].astype(gdn_ref.dtype)

    return pl.pallas_call(
        body, name="mlp_bwd", grid=(N_DEV, nr),
        in_specs=[pl.BlockSpec((R, D_MODEL), lambda d, r: (r, 0)), pl.BlockSpec((R, D_MODEL), lambda d, r: (r, 0)),
                  pl.BlockSpec((R, FF_BLK), lambda d, r: (r, d)),
                  pl.BlockSpec((1, D_MODEL, FF_BLK), lambda d, r: (d, 0, 0)),
                  pl.BlockSpec((1, FF_BLK, D_MODEL), lambda d, r: (d, 0, 0))],
        out_specs=[pl.BlockSpec((1, D_MODEL, FF_BLK), lambda d, r: (d, 0, 0)),
                   pl.BlockSpec((1, FF_BLK, D_MODEL), lambda d, r: (d, 0, 0)), _VMEM],
        out_shape=[jax.ShapeDtypeStruct((N_DEV, D_MODEL, FF_BLK), MM), jax.ShapeDtypeStruct((N_DEV, FF_BLK, D_MODEL), MM),
                   jax.ShapeDtypeStruct((tp, D_MODEL), F32)],
        scratch_shapes=[pltpu.VMEM((D_MODEL, FF_BLK), F32), pltpu.VMEM((FF_BLK, D_MODEL), F32)],
        compiler_params=_params(("arbitrary", "arbitrary")),
    )(h1b, dffb, pre, w_up, w_down.reshape(N_DEV, FF_BLK, D_MODEL))


def _post_bwd(dh1m, dr2, xhat1, rstd1, ycat, o, gate, gn_g, gn_b, l1_g, w_out, jobs=()):
    tp = o.shape[0]
    R = ROW_BLK
    nb = tp // R

    def body(dm_ref, dr2_ref, xh1_ref, rs1_ref, yc_ref, o_ref, g_ref, gng, gnb, l1g, wo_ref,
             do_ref, dg_ref, dys_ref, dh0_ref, gwo_ref, dl1g_ref, dl1b_ref, dgng_ref, dgnb_ref, awo):
        i = pl.program_id(0)

        @pl.when(i == 0)
        def _():
            for ref in (awo, dl1g_ref, dl1b_ref, dgng_ref, dgnb_ref):
                ref[...] = jnp.zeros_like(ref)

        dh1 = dm_ref[...] + ALPHA * dr2_ref[...]
        xh1 = xh1_ref[...]
        dl1g_ref[...] += _colsum(dh1 * xh1)
        dl1b_ref[...] += _colsum(dh1)
        dr1 = _ln_bwd(dh1 * l1g[...], xh1, rs1_ref[...])
        dh0_ref[...] = ALPHA * dr1
        dmix = dr1.astype(MM)
        awo[...] += _dot_tn(yc_ref[...], dmix)
        dyc = _dot_nt(dmix, wo_ref[...])
        dys_ref[...] = dyc[:, 0:S5_W]
        for h in range(RET_H):
            sl = slice(h * HEAD, (h + 1) * HEAD)
            gt = g_ref[:, sl]
            _, xhat, rstd, on, s = _gn_gate(o_ref[:, sl], gt, gng[:, sl], gnb[:, sl])
            dyr = dyc[:, S5_W + h * HEAD:S5_W + (h + 1) * HEAD]
            dg_ref[:, sl] = (dyr * on * (s * (1.0 + gt * (1.0 - s)))).astype(dg_ref.dtype)
            don = dyr * gt * s
            dgng_ref[:, sl] += _colsum(don * xhat)
            dgnb_ref[:, sl] += _colsum(don)
            do_ref[:, sl] = _ln_bwd(don * gng[:, sl], xhat, rstd)

        @pl.when(i == nb - 1)
        def _():
            gwo_ref[...] = awo[...].astype(gwo_ref.dtype)

    row = lambda w: pl.BlockSpec((R, w), lambda i: (i, 0))
    full = lambda a: pl.BlockSpec(a.shape, lambda i: (0,) * a.ndim)
    acc = lambda s, dt=F32: (pl.BlockSpec(s, lambda i: (0, 0)), jax.ShapeDtypeStruct(s, dt))
    outs = [(row(RET_W), jax.ShapeDtypeStruct((tp, RET_W), F32)), (row(RET_W), jax.ShapeDtypeStruct((tp, RET_W), MM)),
            (row(S5_W), jax.ShapeDtypeStruct((tp, S5_W), F32)), (row(D_MODEL), jax.ShapeDtypeStruct((tp, D_MODEL), F32)),
            acc((D_MODEL, D_MODEL), MM), acc((1, D_MODEL)), acc((1, D_MODEL)), acc((1, RET_W)), acc((1, RET_W))]
    return _call(
        body, "post_bwd", (nb,),
        [row(D_MODEL), row(D_MODEL), row(D_MODEL), row(1), row(D_MODEL), row(RET_W), row(RET_W),
         full(gn_g), full(gn_b), full(l1_g), _VMEM],
        [o[0] for o in outs], [o[1] for o in outs],
        [pltpu.VMEM((D_MODEL, D_MODEL), F32)],
        (dh1m, dr2, xhat1, rstd1, ycat, o, gate, gn_g, gn_b, l1_g, w_out), jobs)


def _in_bwd(du, dq, dk, dv, dg, dh0r, xhat0, rstd0, li_g, li_b, w_int, jobs=()):
    tp = du.shape[0]
    R = PROJ_ROWS if tp % PROJ_ROWS == 0 else ROW_BLK
    nb = tp // R
    segs = [(0, S5_W)] + [(S5_W + n * RET_W, S5_W + (n + 1) * RET_W) for n in range(4)]

    def body(du_ref, dq_ref, dk_ref, dv_ref, dg_ref, dh0r_ref, xh_ref, rs_ref, lig, lib, w_ref,
             gx_ref, dmeta_ref, gw_ref, dlg_ref, dlb_ref, aw, stage, out_sems):
        i = pl.program_id(0)
        slot = i % 2

        def to_gx(step_slot, first):
            if first:
                return pltpu.make_async_copy(stage.at[0, CHUNK:R, :], gx_ref.at[0:R - CHUNK, :], out_sems.at[0])
            return pltpu.make_async_copy(stage.at[step_slot], gx_ref.at[pl.ds(i * R - CHUNK, R), :], out_sems.at[step_slot])

        @pl.when(i == 0)
        def _():
            for ref in (aw, dlg_ref, dlb_ref):
                ref[...] = jnp.zeros_like(ref)

        @pl.when(i >= 3)
        def _():
            to_gx(slot, False).wait()

        valid = (i * R + lax.broadcasted_iota(jnp.int32, (R, 1), 0)) >= PAD
        xh = xh_ref[...]
        hb = (xh * lig[...] + lib[...]).astype(MM)
        dh0 = dh0r_ref[...]
        for (lo, hi), ref in zip(segs, (du_ref, dq_ref, dk_ref, dv_ref, dg_ref)):
            dseg = jnp.where(valid, ref[...], 0.0).astype(MM)
            dh0 = dh0 + _dot(dseg, w_ref[lo:hi, :])
            aw[lo:hi, :] += _dot_tn(dseg, hb)
        dlg_ref[...] += _colsum(dh0 * xh)
        dlb_ref[...] += _colsum(dh0)
        draw = _ln_bwd(dh0 * lig[...], xh, rs_ref[...])
        stage[slot] = draw

        @pl.when(i == 0)
        def _():
            dmeta_ref[...] = draw[PAD:CHUNK, :]
            first = to_gx(0, True)
            first.start()
            first.wait()

        @pl.when(i > 0)
        def _():
            to_gx(slot, False).start()

        @pl.when(i == nb - 1)
        def _():
            gw_ref[...] = aw[...].astype(gw_ref.dtype)
            for back in (1, 0):
                if nb - 1 - back >= 1:
                    to_gx((nb - 1 - back) % 2, False).wait()

    row = lambda w: pl.BlockSpec((R, w), lambda i: (i, 0))
    full = lambda a: pl.BlockSpec(a.shape, lambda i: (0,) * a.ndim)
    acc = lambda s, dt=F32: (pl.BlockSpec(s, lambda i: (0, 0)), jax.ShapeDtypeStruct(s, dt))
    outs = [(_ANY, jax.ShapeDtypeStruct((tp - CHUNK, D_MODEL), F32)), acc((N_META, D_MODEL)), acc((PROJ_W, D_MODEL), MM),
            acc((1, D_MODEL)), acc((1, D_MODEL))]
    return _call(
        body, "in_bwd", (nb,),
        [row(S5_W), row(RET_W), row(RET_W), row(RET_W), row(RET_W), row(D_MODEL), row(D_MODEL), row(1),
         full(li_g), full(li_b), _VMEM],
        [o[0] for o in outs], [o[1] for o in outs],
        [pltpu.VMEM((PROJ_W, D_MODEL), F32), pltpu.VMEM((2, R, D_MODEL), F32), pltpu.SemaphoreType.DMA((2,))],
        (du, dq, dk, dv, dg, dh0r, xhat0, rstd0, li_g, li_b, w_int), jobs)


def _place():
    return lax.axis_index("x"), lax.axis_index("y"), lax.axis_index("c")


def _dma_sems(n):
    return pltpu.SemaphoreType.DMA((n,))


def _job_gather(shard):
    def parts(ins, outs, sems):
        (src,), (out,), (send_sems, recv_sems, local_sem) = ins, outs, sems
        x, y, c = _place()
        north = c == 1
        me, sib = (x, y, c), (x, y, 1 - c)
        xn, yn, dg = (1 - x, y, c), (x, 1 - y, c), (1 - x, 1 - y, c)
        relay_from = (jnp.where(north, 1 - x, x), jnp.where(north, y, 1 - y), c)
        relay_to = (jnp.where(north, x, 1 - x), jnp.where(north, 1 - y, y), c)

        def slot(dev):
            return out.at[4 * dev[0] + 2 * dev[1] + dev[2]]

        def copy(k, block, to, from_input=False):
            return pltpu.make_async_remote_copy(
                src_ref=src if from_input else slot(block), dst_ref=slot(block),
                send_sem=send_sems.at[k], recv_sem=recv_sems.at[k], device_id=to, device_id_type=_MESH)

        mine = lambda: pltpu.make_async_copy(src, slot(me), local_sem.at[0])
        first = lambda: [copy(0, me, sib, True), copy(1, me, xn, True), copy(2, me, yn, True)]
        relayed = lambda: [copy(3, relay_from, relay_to), copy(4, xn, sib), copy(5, yn, sib)]
        return me, sib, xn, yn, dg, copy, mine, first, relayed

    def start(ins, outs, sems):
        mine, first = parts(ins, outs, sems)[6:8]
        mine().start()
        for cp in first():
            cp.start()

    def relay(ins, outs, sems):
        me, sib, xn, yn, dg, copy, mine, first, relayed = parts(ins, outs, sems)
        copy(1, xn, me).wait_recv()
        copy(2, yn, me).wait_recv()
        for cp in relayed():
            cp.start()

    def finish(ins, outs, sems):
        me, sib, xn, yn, dg, copy, mine, first, relayed = parts(ins, outs, sems)
        other = 1 - me[2]
        copy(3, dg, me).wait_recv()
        last = copy(6, dg, sib)
        last.start()
        copy(0, sib, me).wait_recv()
        for k, chip in ((4, xn), (5, yn), (6, dg)):
            copy(k, (chip[0], chip[1], other), me).wait_recv()
        for cp in first() + relayed() + [last]:
            cp.wait_send()
        mine().wait()

    return dict(ins=[shard], outs=[jax.ShapeDtypeStruct((N_DEV,) + shard.shape, shard.dtype)],
                sems=[_dma_sems(7), _dma_sems(7), _dma_sems(1)], start=start, middle=relay, finish=finish)


def _job_pair(g):
    def copies(ins, outs, sems):
        x, y, c = _place()
        return [pltpu.make_async_remote_copy(
            src_ref=ins[0].at[2 * j + (1 - c)], dst_ref=outs[0].at[j], send_sem=sems[0].at[j], recv_sem=sems[1].at[j],
            device_id=(x, y, 1 - c), device_id_type=_MESH) for j in range(4)]

    def start(ins, outs, sems):
        for cp in copies(ins, outs, sems):
            cp.start()

    def finish(ins, outs, sems):
        for cp in copies(ins, outs, sems):
            cp.wait()

    return dict(ins=[g], outs=[jax.ShapeDtypeStruct((4,) + g.shape[1:], g.dtype)], sems=[_dma_sems(4), _dma_sems(4)],
                start=start, finish=finish)


def _job_chips(p):
    def copies(ins, outs, sems):
        x, y, c = _place()
        chips = [(1 - x, y), (x, 1 - y), (1 - x, 1 - y)]
        return [pltpu.make_async_remote_copy(
            src_ref=ins[0].at[2 * chip[0] + chip[1]], dst_ref=outs[0].at[k], send_sem=sems[0].at[k],
            recv_sem=sems[1].at[k], device_id=(*chip, c), device_id_type=_MESH) for k, chip in enumerate(chips)]

    def start(ins, outs, sems):
        for cp in copies(ins, outs, sems):
            cp.start()

    def finish(ins, outs, sems):
        for cp in copies(ins, outs, sems):
            cp.wait()

    return dict(ins=[p], outs=[jax.ShapeDtypeStruct((3,) + p.shape[1:], p.dtype)], sems=[_dma_sems(3), _dma_sems(3)],
                start=start, finish=finish)


def _split_job_refs(jobs, ins, outs, sems):
    res, a, b, c = [], 0, 0, 0
    for job in jobs:
        na, nb, nc = len(job["ins"]), len(job["outs"]), len(job["sems"])
        res.append((ins[a:a + na], outs[b:b + nb], sems[c:c + nc]))
        a, b, c = a + na, b + nb, c + nc
    return res


def _call(body, name, grid, in_specs, out_specs, out_shape, scratch, args, jobs=(), prefetch=None):
    jobs = list(jobs)
    n_in, n_out, n_scr = len(in_specs), len(out_specs), len(scratch)
    j_in = [a for job in jobs for a in job["ins"]]
    j_out = [o for job in jobs for o in job["outs"]]
    j_scr = [s for job in jobs for s in job["sems"]]
    nsteps = grid[0]
    n_pre = 0 if prefetch is None else 1

    def wrapped(*refs):
        pre, refs = refs[:n_pre], refs[n_pre:]
        ins, jins = refs[:n_in], refs[n_in:n_in + len(j_in)]
        refs = refs[n_in + len(j_in):]
        outs, jouts = refs[:n_out], refs[n_out:n_out + len(j_out)]
        refs = refs[n_out + len(j_out):]
        scr, jscr = refs[:n_scr], refs[n_scr:]
        per_job = _split_job_refs(jobs, jins, jouts, jscr)

        def middle():
            for job, r in zip(jobs, per_job):
                if "middle" in job:
                    job["middle"](*r)

        @pl.when(pl.program_id(0) == 0)
        def _():
            for job, r in zip(jobs, per_job):
                job["start"](*r)

        if nsteps >= 3:
            pl.when(pl.program_id(0) == nsteps // 2)(middle)

        body(*pre, *ins, *outs, *scr)

        @pl.when(pl.program_id(0) == nsteps - 1)
        def _():
            if nsteps < 3:
                middle()
            for job, r in zip(jobs, per_job):
                job["finish"](*r)

    specs = dict(in_specs=list(in_specs) + [_ANY] * len(j_in), out_specs=list(out_specs) + [_ANY] * len(j_out),
                 scratch_shapes=list(scratch) + j_scr)
    if n_pre:
        specs = dict(grid_spec=pltpu.PrefetchScalarGridSpec(num_scalar_prefetch=1, grid=grid, **specs))
    else:
        specs["grid"] = grid
    res = pl.pallas_call(
        wrapped if jobs else body, name=name, out_shape=list(out_shape) + j_out,
        compiler_params=_params(("arbitrary",) * len(grid)), **specs,
    )(*([prefetch] if n_pre else []), *args, *j_in)
    return list(res[:n_out]), list(res[n_out:])


def _exchange(jobs, name):
    j_in = [a for job in jobs for a in job["ins"]]
    j_out = [o for job in jobs for o in job["outs"]]
    j_scr = [s for job in jobs for s in job["sems"]]

    def body(*refs):
        per_job = _split_job_refs(jobs, refs[:len(j_in)], refs[len(j_in):len(j_in) + len(j_out)],
                                  refs[len(j_in) + len(j_out):])
        for phase in ("start", "middle", "finish"):
            for job, r in zip(jobs, per_job):
                if phase in job:
                    job[phase](*r)

    return pl.pallas_call(body, name=name, out_shape=j_out, in_specs=[_ANY] * len(j_in), out_specs=[_ANY] * len(j_out),
                          scratch_shapes=j_scr)(*j_in)


def _pair_sum(gs, r1s, c_arr, name):
    n = len(gs)

    def body(c_ref, *refs):
        for a in range(n):
            refs[2 * n + a][...] = (refs[a][...].astype(F32) + refs[n + a][...].astype(F32)).astype(refs[2 * n + a].dtype)

    def blk(g, own):
        s = g.shape[1:]
        if own:
            return pl.BlockSpec((1,) + s, lambda j, c_ref: (2 * j + c_ref[0],) + (0,) * len(s))
        return pl.BlockSpec((1,) + s, lambda j, c_ref: (j,) + (0,) * len(s))

    return pl.pallas_call(
        body, name=name,
        grid_spec=pltpu.PrefetchScalarGridSpec(
            num_scalar_prefetch=1, grid=(4,),
            in_specs=[blk(g, True) for g in gs] + [blk(g, False) for g in gs],
            out_specs=[blk(g, False) for g in gs]),
        out_shape=[jax.ShapeDtypeStruct((4,) + g.shape[1:], g.dtype) for g in gs],
        compiler_params=_params(("arbitrary",)),
    )(c_arr, *gs, *r1s)


def _adamw_math(w, g, m, v):
    m = ADAM_B1 * m + (1.0 - ADAM_B1) * g
    v = ADAM_B2 * v + (1.0 - ADAM_B2) * (g * g)
    m_hat = m / (1.0 - ADAM_B1 ** ADAM_STEP)
    v_hat = v / (1.0 - ADAM_B2 ** ADAM_STEP)
    return -ADAM_LR * (m_hat / (jnp.sqrt(v_hat) + ADAM_EPS) + ADAM_WD * w), m, v


def _transpose_exact(a):
    n = a.shape[0]
    eye = (lax.broadcasted_iota(jnp.int32, (n, n), 0) == lax.broadcasted_iota(jnp.int32, (n, n), 1)).astype(jnp.bfloat16)
    return sum(lax.dot_general(p, eye, (((0,), (0,)), ((), ())), preferred_element_type=F32) for p in _split3(a))


def _adamw_shards(items, name, steps, chip, jobs=()):
    n = len(items)

    def body(chip_ref, *refs):
        for a, it in enumerate(items):
            p_ref, r_ref, w_ref, m_ref, v_ref = refs[5 * a:5 * a + 5]
            g = ((p_ref[0].astype(F32) + r_ref[0].astype(F32)) + r_ref[1].astype(F32)) + r_ref[2].astype(F32)
            if it[5]:
                g = _transpose_exact(g)
            outs = refs[5 * n + 4 * a:5 * n + 4 * a + 4]
            outs[0][...] = g
            outs[1][...], outs[2][...], outs[3][...] = _adamw_math(w_ref[...], g, m_ref[...], v_ref[...])

    in_specs, out_specs, out_shape, flat = [], [], [], []
    for p, r, w, m, v, transposed in items:
        rows, cols = w.shape
        rb = rows // steps
        if transposed:
            in_specs += [pl.BlockSpec((1, cols, rb), lambda i, c: (c[0], 0, i)), pl.BlockSpec((3, cols, rb), lambda i, c: (0, 0, i))]
        else:
            in_specs += [pl.BlockSpec((1, rb, cols), lambda i, c: (c[0], i, 0)), pl.BlockSpec((3, rb, cols), lambda i, c: (0, i, 0))]
        wblk = pl.BlockSpec((rb, cols), lambda i, c: (i, 0))
        in_specs += [wblk] * 3
        out_specs += [wblk] * 4
        out_shape += [jax.ShapeDtypeStruct(w.shape, F32)] * 4
        flat += [p, r, w, m, v]
    return _call(body, name, (steps,), in_specs, out_specs, out_shape, [], flat, jobs, prefetch=chip)


def _sum_devices(gathered, name):
    def body(gs_ref, g_ref):
        g = gs_ref[0]
        for s in range(1, N_DEV):
            g = g + gs_ref[s]
        g_ref[...] = g

    return pl.pallas_call(body, name=name, out_shape=jax.ShapeDtypeStruct(gathered.shape[1:], F32),
                          in_specs=[_VMEM], out_specs=_VMEM, compiler_params=_params())(gathered)


def _adamw_native(items, name):
    n = len(items)

    def body(*refs):
        for a in range(n):
            g, w, m, v = (refs[4 * a + t][...] for t in range(4))
            refs[4 * n + 3 * a][...], refs[4 * n + 3 * a + 1][...], refs[4 * n + 3 * a + 2][...] = _adamw_math(w, g, m, v)

    return pl.pallas_call(
        body, name=name, out_shape=[jax.ShapeDtypeStruct(it[1].shape, F32) for it in items for _ in range(3)],
        in_specs=[_VMEM] * (4 * n), out_specs=[_VMEM] * (3 * n), compiler_params=_params(),
    )(*[t for it in items for t in it])


SMALL = ["ln_in_g", "ln_in_b", "s5_lambda_re", "s5_lambda_im", "s5_log_dt", "s5_b_re", "s5_b_im", "s5_c_re", "s5_c_im",
         "s5_d", "s5_b_glu", "ret_gn_g", "ret_gn_b", "ln1_g", "ln1_b", "ln2_g", "ln2_b"]
LATE = ["ln_in_g", "ln_in_b", "meta_tokens"]
EARLY = [n for n in SMALL if n not in LATE] + ["s5_w_glu", "loss"]
LANE = 128


def _pack(arrs):
    parts = []
    for a in arrs:
        f = a.reshape(-1)
        parts.append(jnp.pad(f, (0, (-f.shape[0]) % LANE)))
    flat = jnp.concatenate(parts)
    rows = -(-flat.shape[0] // LANE)
    flat = jnp.pad(flat, (0, (-rows % 8) * LANE + rows * LANE - flat.shape[0]))
    return flat.reshape(-1, LANE)


def _unpack(packed, shapes):
    flat = packed.reshape(-1)
    out, off = [], 0
    for s in shapes:
        n = math.prod(s)
        out.append(flat[off:off + n].reshape(s))
        off += n + (-n) % LANE
    return out


def _rope_tables(tp):
    pos = jnp.arange(tp, dtype=F32) - float(PAD)
    inv_freq = 1.0 / (ROPE_BASE ** (jnp.arange(0, HEAD, 2, dtype=F32) / HEAD))
    ang = pos[:, None] * inv_freq[None, :]
    cos, sin = jnp.cos(ang), jnp.sin(ang)
    return jnp.concatenate([cos, cos], axis=1), jnp.concatenate([-sin, sin], axis=1)


RET_CHUNK = ROW_BLK


def _decay_tables():
    log_gamma = jnp.log1p(-jnp.exp2(-5.0 - jnp.arange(RET_H, dtype=F32)))
    idx = jnp.arange(RET_CHUNK, dtype=F32)
    diff = idx[:, None] - idx[None, :]
    dmat = jnp.where(diff[None] >= 0, jnp.exp(jnp.maximum(diff, 0.0)[None] * log_gamma[:, None, None]), 0.0)
    zeta = jnp.exp((RET_CHUNK - 1.0 - idx)[None] * log_gamma[:, None])
    xi = jnp.exp((idx + 1.0)[None] * log_gamma[:, None])
    gam = jnp.exp(RET_CHUNK * log_gamma)
    wide = lambda t: jnp.broadcast_to(t[:, :, None], (RET_H, RET_CHUNK, HEAD))
    return dmat, wide(zeta), wide(xi), jnp.broadcast_to(gam[:, None, None], (RET_H, HEAD, HEAD))


def _local_step(x2d, tgt, meta_full, w_int, w_out, w_up, w_down, w_glu, sp, distributed):
    tp = x2d.shape[0] + CHUNK
    row = lambda a: a.reshape(1, -1)
    cos2, sin2 = _rope_tables(tp)
    dmat, zeta_b, xi_b, gam_b = _decay_tables()
    li_g, li_b = row(sp["ln_in_g"]), row(sp["ln_in_b"])
    l1_g, l1_b, l2_g, l2_b = row(sp["ln1_g"]), row(sp["ln1_b"]), row(sp["ln2_g"]), row(sp["ln2_b"])
    gn_g, gn_b = row(sp["ret_gn_g"]), row(sp["ret_gn_b"])
    lre, lim = row(sp["s5_lambda_re"]), row(sp["s5_lambda_im"])
    ldt = row(jnp.repeat(sp["s5_log_dt"].reshape(-1), S5_P))
    to_t = lambda b: b.reshape(S5_G, S5_P, S5_H).transpose(2, 0, 1).reshape(S5_H, S5_N)
    bre_t, bim_t = to_t(sp["s5_b_re"]), to_t(sp["s5_b_im"])
    to_w = lambda c: jnp.tile(c.reshape(S5_W, S5_P), (1, 2))
    cre_w, cim_w = to_w(sp["s5_c_re"]), to_w(sp["s5_c_im"])

    jobs = (lambda *j: list(j)) if distributed else (lambda *j: [])
    c_arr = jnp.reshape(lax.axis_index("c"), (1,)).astype(jnp.int32) if distributed else None
    (xhat0, rstd0), bg = _ln_in(x2d, meta_full, jobs(*([_job_gather(w_int), _job_gather(w_glu)] if distributed else [])))
    if distributed:
        w_int, w_glu = bg[0].reshape(PROJ_W, D_MODEL), bg[1].reshape(S5_W, S5_W)
    s5_small = (lre, lim, ldt, bre_t, bim_t, cre_w, cim_w, row(sp["s5_d"]), w_glu, row(sp["s5_b_glu"]))
    (u, q, k, v, gate), bg = _in_proj(xhat0, li_g, li_b, w_int, cos2, sin2,
                                      jobs(_job_gather(w_out) if distributed else None))
    if distributed:
        w_out = bg[0].reshape(D_MODEL, D_MODEL)
    (ys5, xr, xi), bg = _s5_fwd(u, *s5_small, jobs=jobs(_job_gather(w_up) if distributed else None))
    if distributed:
        w_up = bg[0]
    (o, states), _ = _ret_fwd(q, k, v, dmat, zeta_b, xi_b, gam_b)
    (ycat, xhat1, rstd1, h1b, pre), bg = _post_up(o, gate, ys5, xhat0, gn_g, gn_b, li_g, li_b, l1_g, l1_b, w_out, w_up,
                                                  jobs(_job_gather(w_down) if distributed else None))
    if distributed:
        w_down = bg[0].reshape(D_FF, D_MODEL)
    dr2, dffb, loss8, dl2g, dl2b = _post_down(pre, xhat1, tgt, l1_g, l1_b, l2_g, l2_b, w_down)
    g_up, g_down, dh1m = _mlp_bwd(h1b, dffb, pre, w_up, w_down)
    (do, dgate, dys5, dh0r, g_out, dl1g, dl1b, dgng, dgnb), bg = _post_bwd(
        dh1m, dr2, xhat1, rstd1, ycat, o, gate, gn_g, gn_b, l1_g, w_out,
        jobs(*([_job_pair(g_up), _job_pair(g_down)] if distributed else [])))
    g_out = g_out.reshape(N_DEV, D_MODEL // N_DEV, D_MODEL)
    if distributed:
        p_up, p_down = _pair_sum([g_up, g_down], bg, c_arr, "pair_sum_mlp")
    (du, dlre, dlim, dldt, dbre_t, dbim_t, dcre, dcim, dd, dwglu, dbglu), bg = _s5_bwd(
        dys5, u, xr, xi, *s5_small,
        jobs=jobs(*([_job_chips(p_up), _job_chips(p_down), _job_pair(g_out)] if distributed else [])))
    if distributed:
        r_up, r_down = bg[0], bg[1]
        (p_out,) = _pair_sum([g_out], bg[2:], c_arr, "pair_sum_out")
    from_t = lambda t: t.reshape(S5_H, S5_G, S5_P).transpose(1, 2, 0)
    small = {
        "s5_lambda_re": dlre, "s5_lambda_im": dlim, "s5_log_dt": dldt[:, :S5_G],
        "s5_b_re": from_t(dbre_t), "s5_b_im": from_t(dbim_t), "s5_c_re": dcre, "s5_c_im": dcim, "s5_d": dd,
        "s5_b_glu": dbglu, "ret_gn_g": dgng, "ret_gn_b": dgnb, "ln1_g": dl1g, "ln1_b": dl1b, "ln2_g": dl2g, "ln2_b": dl2b,
        "s5_w_glu": dwglu, "loss": loss8[0:1, 0:1]}
    early_pack = _pack([small[n] for n in EARLY])
    (dq, dk, dv), bg = _ret_bwd(q, k, v, do, states, cos2, sin2, dmat, zeta_b, xi_b, gam_b,
                                jobs(*([_job_chips(p_out), _job_gather(early_pack)] if distributed else [])))
    (grad_x, dmeta, g_int, dlig, dlib), _ = _in_bwd(du, dq, dk, dv, dgate, dh0r, xhat0, rstd0, li_g, li_b, w_int)
    small.update(ln_in_g=dlig, ln_in_b=dlib, meta_tokens=dmeta)
    g_int = g_int.reshape(N_DEV, PROJ_W // N_DEV, D_MODEL)
    if distributed:
        (r1_in,) = _exchange([_job_pair(g_int)], "exchange_pair_in")
        (p_in,) = _pair_sum([g_int], [r1_in], c_arr, "pair_sum_in")
        big = dict(chip_sums=[p_in, p_out, p_up, p_down], received=[None, bg[0], r_up, r_down], early=bg[1])
    else:
        big = dict(partials=[g_int, g_out, g_up, g_down])
    return grad_x, big, small


def kernel(x, meta_tokens, ln_in_g, ln_in_b, w_in, s5_lambda_re, s5_lambda_im, s5_log_dt, s5_b_re, s5_b_im, s5_c_re, s5_c_im, s5_d, s5_w_glu, s5_b_glu, ret_gn_g, ret_gn_b, w_out, ln1_g, ln1_b, w_up, w_down, ln2_g, ln2_b, loss_target, m_meta_tokens, m_ln_in_g, m_ln_in_b, m_w_in, m_s5_lambda_re, m_s5_lambda_im, m_s5_log_dt, m_s5_b_re, m_s5_b_im, m_s5_c_re, m_s5_c_im, m_s5_d, m_s5_w_glu, m_s5_b_glu, m_ret_gn_g, m_ret_gn_b, m_w_out, m_ln1_g, m_ln1_b, m_w_up, m_w_down, m_ln2_g, m_ln2_b, v_meta_tokens, v_ln_in_g, v_ln_in_b, v_w_in, v_s5_lambda_re, v_s5_lambda_im, v_s5_log_dt, v_s5_b_re, v_s5_b_im, v_s5_c_re, v_s5_c_im, v_s5_d, v_s5_w_glu, v_s5_b_glu, v_ret_gn_g, v_ret_gn_b, v_w_out, v_ln1_g, v_ln1_b, v_w_up, v_w_down, v_ln2_g, v_ln2_b):
    args = dict(locals())
    names = ["meta_tokens", "ln_in_g", "ln_in_b", "w_in", "s5_lambda_re", "s5_lambda_im", "s5_log_dt", "s5_b_re", "s5_b_im",
             "s5_c_re", "s5_c_im", "s5_d", "s5_w_glu", "s5_b_glu", "ret_gn_g", "ret_gn_b", "w_out", "ln1_g", "ln1_b",
             "w_up", "w_down", "ln2_g", "ln2_b"]
    ax, ay, ac = _place()
    me = 4 * ax + 2 * ay + ac

    (a_meta,) = _exchange([_job_gather(meta_tokens)], "gather_meta")
    meta_full = a_meta.transpose(1, 0, 2).reshape(N_META, D_MODEL)

    sp = {n: args[n] for n in SMALL}
    grad_x, big, small = _local_step(x[0], loss_target[0], meta_full, w_in[0].T.astype(MM), w_out[0].astype(MM),
                                   w_up[0].astype(MM), w_down[0].astype(MM), s5_w_glu[0].astype(MM), sp, True)

    j_arr = jnp.reshape(2 * ax + ay, (1,)).astype(jnp.int32)
    two_d = lambda a: a.reshape(a.shape[-2:])
    item = lambda n, p, r, t: (p, r, *(two_d(a) for a in (args[n], args["m_" + n], args["v_" + n])), t)
    late_pack = _pack([small[n] for n in LATE])
    mlp = ("w_out", "w_up", "w_down")
    res, (r_in, late_all) = _adamw_shards(
        [item(n, p, r, False) for n, p, r in zip(mlp, big["chip_sums"][1:], big["received"][1:])], "adamw_mlp", 8, j_arr,
        [_job_chips(big["chip_sums"][0]), _job_gather(late_pack)])
    res_in, _ = _adamw_shards([item("w_in", big["chip_sums"][0], r_in, True)], "adamw_in", 8, j_arr)
    upd = {"w_in": res_in}
    for idx, n in enumerate(mlp):
        upd[n] = res[4 * idx:4 * idx + 4]
    shard_grads = {n: upd[n][0] for n in upd}

    early_shapes = [args[n].shape for n in EARLY[:-2]] + [(S5_W, S5_W), (1,)]
    late_shapes = [args["ln_in_g"].shape, args["ln_in_b"].shape, (N_META, D_MODEL)]
    g_small = dict(zip(EARLY, _unpack(_sum_devices(big["early"], "sum_small_early"), early_shapes)))
    g_small.update(zip(LATE, _unpack(_sum_devices(late_all, "sum_small_late"), late_shapes)))
    loss = g_small["loss"].reshape(())

    shard_grads["meta_tokens"] = lax.dynamic_slice(g_small["meta_tokens"], (0, me * (D_MODEL // N_DEV)),
                                                   (N_META, D_MODEL // N_DEV))
    shard_grads["s5_w_glu"] = lax.dynamic_slice(g_small["s5_w_glu"], (me * (S5_W // N_DEV), 0),
                                                (S5_W // N_DEV, S5_W))[None]
    natives = SMALL + ["meta_tokens", "s5_w_glu"]
    res2 = _adamw_native([(shard_grads[n] if n in shard_grads else g_small[n], args[n], args["m_" + n], args["v_" + n])
                          for n in natives], "adamw_small")
    for idx, n in enumerate(natives):
        upd[n] = [shard_grads[n] if n in shard_grads else g_small[n]] + list(res2[3 * idx:3 * idx + 3])

    grads, deltas, new_m, new_v = ([upd[n][t].reshape(args[n].shape) for n in names] for t in range(4))
    return (loss, grad_x[None], *grads, *deltas, *new_m, *new_v)
```

```python
import math

import jax
import jax.numpy as jnp
from jax import lax
from jax.experimental import pallas as pl
from jax.experimental.pallas import tpu as pltpu

F32 = jnp.float32
MM = jnp.bfloat16

D_MODEL = 1024
N_META = 16
CHUNK = 128
PAD = CHUNK - N_META
S5_W, S5_G, S5_H, S5_P = 256, 16, 16, 64
S5_N = S5_G * S5_P
RET_W, RET_H, HEAD = 768, 6, 128
D_FF = 4096
PROJ_W = S5_W + 4 * RET_W
N_DEV = 8
FF_BLK = D_FF // N_DEV
ROW_BLK = 384
MLP_ROWS = 1408
PROJ_ROWS = 704
ALPHA = 2.0 ** 0.25
LN_EPS = 1e-5
GN_EPS = 1e-5
ROPE_BASE = 10000.0
GELU_C = math.sqrt(2.0 / math.pi)
GELU_A = 0.044715
ADAM_LR, ADAM_B1, ADAM_B2, ADAM_EPS, ADAM_WD, ADAM_STEP = 0.001, 0.9, 0.999, 1e-08, 0.01, 10
VMEM_LIMIT = 60 * 1024 * 1024

_VMEM = pl.BlockSpec(memory_space=pltpu.VMEM)
_ANY = pl.BlockSpec(memory_space=pl.ANY)
_MESH = pl.DeviceIdType.MESH


def _params(sem=None):
    return pltpu.CompilerParams(dimension_semantics=sem, vmem_limit_bytes=VMEM_LIMIT)


def _dot(a, b):
    return jnp.dot(a.astype(MM), b.astype(MM), preferred_element_type=F32)


def _dot_nt(a, b):
    return lax.dot_general(a.astype(MM), b.astype(MM), (((1,), (1,)), ((), ())), preferred_element_type=F32)


def _dot_tn(a, b):
    return lax.dot_general(a.astype(MM), b.astype(MM), (((0,), (0,)), ((), ())), preferred_element_type=F32)


def _split3(a):
    hi = a.astype(jnp.bfloat16)
    r1 = a - hi.astype(F32)
    mid = r1.astype(jnp.bfloat16)
    lo = (r1 - mid.astype(F32)).astype(jnp.bfloat16)
    return hi, mid, lo


def _dot_sel_rhs(a, sel):
    s = sel.astype(jnp.bfloat16)
    return sum(jnp.dot(p, s, preferred_element_type=F32) for p in _split3(a))


def _dot_sel_lhs(sel, b):
    s = sel.astype(jnp.bfloat16)
    return sum(jnp.dot(s, p, preferred_element_type=F32) for p in _split3(b))


def _ln_fwd(r, eps):
    mu = jnp.mean(r, axis=-1, keepdims=True)
    xc = r - mu
    var = jnp.mean(xc * xc, axis=-1, keepdims=True)
    rstd = lax.rsqrt(var + eps)
    return xc * rstd, rstd


def _ln_bwd(dxhat, xhat, rstd):
    m1 = jnp.mean(dxhat, axis=-1, keepdims=True)
    m2 = jnp.mean(dxhat * xhat, axis=-1, keepdims=True)
    return rstd * (dxhat - m1 - xhat * m2)


def _colsum(a):
    return jnp.sum(a, axis=0, keepdims=True)


def _shift3(n_in):
    return [pl.BlockSpec((CHUNK, D_MODEL), (lambda i, j=j: (jnp.clip(3 * i - 1 + j, 0, n_in - 1), 0))) for j in range(3)]


def _ln_in(x2d, meta_full, jobs=()):
    seq = x2d.shape[0]
    tp = seq + CHUNK
    R = ROW_BLK

    def body(xa, xb, xc, meta_ref, xhat_ref, rstd_ref, raw_ref):
        raw_ref[0:CHUNK, :] = xa[...]
        raw_ref[CHUNK:2 * CHUNK, :] = xb[...]
        raw_ref[2 * CHUNK:3 * CHUNK, :] = xc[...]

        @pl.when(pl.program_id(0) == 0)
        def _():
            raw_ref[0:PAD, :] = jnp.zeros((PAD, D_MODEL), F32)
            raw_ref[PAD:CHUNK, :] = meta_ref[...]

        xhat_ref[...], rstd_ref[...] = _ln_fwd(raw_ref[...], LN_EPS)

    row = lambda w: pl.BlockSpec((R, w), lambda i: (i, 0))
    return _call(
        body, "ln_in", (tp // R,),
        _shift3(seq // CHUNK) + [pl.BlockSpec((N_META, D_MODEL), lambda i: (0, 0))],
        [row(D_MODEL), row(1)], [jax.ShapeDtypeStruct((tp, D_MODEL), F32), jax.ShapeDtypeStruct((tp, 1), F32)],
        [pltpu.VMEM((R, D_MODEL), F32)], (x2d, x2d, x2d, meta_full), jobs)


def _in_proj(xhat0, ln_g, ln_b, w_int, cos2, sin2, jobs=()):
    tp = xhat0.shape[0]
    R = PROJ_ROWS if tp % PROJ_ROWS == 0 else ROW_BLK

    def body(xh_ref, g_ref, b_ref, w_ref, cos_ref, sin_ref, u_ref, q_ref, k_ref, v_ref, gate_ref):
        hb = (xh_ref[...] * g_ref[...] + b_ref[...]).astype(MM)
        valid = (pl.program_id(0) * R + lax.broadcasted_iota(jnp.int32, (R, 1), 0)) >= PAD

        def seg(lo, hi):
            return jnp.where(valid, _dot_nt(hb, w_ref[lo:hi, :]), 0.0)

        u_ref[...] = seg(0, S5_W)
        cos = cos_ref[...]
        sin = sin_ref[...]
        q = seg(S5_W, S5_W + RET_W)
        k = seg(S5_W + RET_W, S5_W + 2 * RET_W)
        for h in range(RET_H):
            sl = slice(h * HEAD, (h + 1) * HEAD)
            qh = q[:, sl]
            kh = k[:, sl]
            q_ref[:, sl] = (qh * cos + pltpu.roll(qh, HEAD // 2, 1) * sin).astype(q_ref.dtype)
            k_ref[:, sl] = ((kh * cos + pltpu.roll(kh, HEAD // 2, 1) * sin) * (HEAD ** -0.5)).astype(k_ref.dtype)
        v_ref[...] = seg(S5_W + 2 * RET_W, S5_W + 3 * RET_W).astype(v_ref.dtype)
        gate_ref[...] = seg(S5_W + 3 * RET_W, PROJ_W)

    def rows(w, dt):
        return pl.BlockSpec((R, w), lambda i: (i, 0)), jax.ShapeDtypeStruct((tp, w), dt)

    outs = [rows(S5_W, F32), rows(RET_W, MM), rows(RET_W, MM), rows(RET_W, MM), rows(RET_W, F32)]
    full = lambda s: pl.BlockSpec(s, lambda i: (0,) * len(s))
    return _call(
        body, "in_proj", (tp // R,),
        [pl.BlockSpec((R, D_MODEL), lambda i: (i, 0)), full((1, D_MODEL)), full((1, D_MODEL)), _VMEM,
         pl.BlockSpec((R, HEAD), lambda i: (i, 0)), pl.BlockSpec((R, HEAD), lambda i: (i, 0))],
        [o[0] for o in outs], [o[1] for o in outs], [], (xhat0, ln_g, ln_b, w_int, cos2, sin2), jobs)


def _s5_disc(lre, lim, ldt, bre_t, bim_t):
    dt = jnp.exp(ldt)
    mag = jnp.exp(lre * dt)
    ang = lim * dt
    lbr = mag * jnp.cos(ang)
    lbi = mag * jnp.sin(ang)
    den = lre * lre + lim * lim
    nr = lbr - 1.0
    qr = (nr * lre + lbi * lim) / den
    qi = (lbi * lre - nr * lim) / den
    return lbr, lbi, qr * bre_t - qi * bim_t, qr * bim_t + qi * bre_t


def _s5_tables(lbr, lbi, reverse):
    if reverse:
        lbi = -lbi
    pw = [(lbr, lbi)]
    for _ in range(7):
        r, i = pw[-1]
        pw.append((r * lbr - i * lbi, r * lbi + i * lbr))
    row = lax.broadcasted_iota(jnp.int32, (8, S5_N), 0)
    tabs = []
    for k in range(3):
        sh = 2 ** k
        mask = (row < 8 - sh) if reverse else (row >= sh)
        ar, ai = pw[sh - 1]
        tabs.append((jnp.where(mask, ar, 0.0), jnp.where(mask, ai, 0.0)))
    pr = jnp.zeros((8, S5_N), F32)
    pi = jnp.zeros((8, S5_N), F32)
    for i in range(8):
        ar, ai = pw[7 - i] if reverse else pw[i]
        pr = jnp.where(row == i, ar, pr)
        pi = jnp.where(row == i, ai, pi)
    tabs.append((pr, pi))
    return tabs


def _store_tables(tab_ref, tabs):
    for k, (r, i) in enumerate(tabs):
        tab_ref[2 * k] = r
        tab_ref[2 * k + 1] = i


def _bd_mask():
    r = lax.broadcasted_iota(jnp.int32, (S5_W, S5_N), 0)
    c = lax.broadcasted_iota(jnp.int32, (S5_W, S5_N), 1)
    return jnp.right_shift(r, 4) == jnp.right_shift(c, 6)


def _s5_block_diag(bbr_t, bbi_t, cre_w, cim_w):
    mask = _bd_mask()
    bd = lambda t: jnp.where(mask, t, 0.0)
    return (bd(jnp.tile(bbr_t, (S5_G, 1))), bd(jnp.tile(bbi_t, (S5_G, 1))),
            bd(jnp.tile(cre_w, (1, S5_N // HEAD))), bd(jnp.tile(cim_w, (1, S5_N // HEAD))))


def _scan8(xr, xi, tab_ref, lanes, reverse):
    for k in range(3):
        sh = (8 - 2 ** k) if reverse else 2 ** k
        sr = pltpu.roll(xr, sh, 0)
        si = pltpu.roll(xi, sh, 0)
        mr = tab_ref[2 * k, :, lanes]
        mi = tab_ref[2 * k + 1, :, lanes]
        xr, xi = xr + (mr * sr - mi * si), xi + (mr * si + mi * sr)
    return xr, xi


S5_LANES = 256


def _gelu(y):
    t = jnp.tanh(GELU_C * (y + GELU_A * y * y * y))
    return 0.5 * y * (1.0 + t), t


def _s5_fwd(u, lre, lim, ldt, bre_t, bim_t, cre_w, cim_w, d_row, w_glu, b_glu, jobs=()):
    tp = u.shape[0]
    R = ROW_BLK

    def body(u_ref, lre_ref, lim_ref, ldt_ref, bre_ref, bim_ref, cre_ref, cim_ref, d_ref, wg_ref, bg_ref,
             y_ref, xr_ref, xi_ref, bbd_r, bbd_i, cbd_r, cbd_i, tab_ref, car_r, car_i):
        @pl.when(pl.program_id(0) == 0)
        def _():
            lbr, lbi, bbr, bbi = _s5_disc(lre_ref[...], lim_ref[...], ldt_ref[...], bre_ref[...], bim_ref[...])
            br, bi, cr, ci = _s5_block_diag(bbr, bbi, cre_ref[...], cim_ref[...])
            bbd_r[...] = br.astype(MM)
            bbd_i[...] = bi.astype(MM)
            cbd_r[...] = cr.astype(MM)
            cbd_i[...] = ci.astype(MM)
            _store_tables(tab_ref, _s5_tables(lbr, lbi, False))
            car_r[...] = jnp.zeros_like(car_r)
            car_i[...] = jnp.zeros_like(car_i)

        u = u_ref[...]
        ub = u.astype(MM)
        xr_ref[...] = jnp.dot(ub, bbd_r[...], preferred_element_type=F32)
        xi_ref[...] = jnp.dot(ub, bbd_i[...], preferred_element_type=F32)
        for j in range(S5_N // S5_LANES):
            lanes = pl.ds(j * S5_LANES, S5_LANES)
            pr = tab_ref[6, :, lanes]
            pi = tab_ref[7, :, lanes]

            def step(g, carry):
                cr, ci = carry
                rows = pl.ds(pl.multiple_of(g * 8, 8), 8)
                xr, xi = _scan8(xr_ref[rows, lanes], xi_ref[rows, lanes], tab_ref, lanes, False)
                br = jnp.broadcast_to(cr[7:8, :], cr.shape)
                bi = jnp.broadcast_to(ci[7:8, :], ci.shape)
                xr = xr + (pr * br - pi * bi)
                xi = xi + (pr * bi + pi * br)
                xr_ref[rows, lanes] = xr
                xi_ref[rows, lanes] = xi
                return xr, xi

            cr, ci = lax.fori_loop(0, R // 8, step, (car_r[:, lanes], car_i[:, lanes]), unroll=2)
            car_r[:, lanes] = cr
            car_i[:, lanes] = ci
        y = _dot_nt(xr_ref[...], cbd_r[...]) - _dot_nt(xi_ref[...], cbd_i[...]) + d_ref[...] * u
        yg, _ = _gelu(y)
        z = _dot(yg, wg_ref[...]) + bg_ref[...]
        y_ref[...] = yg * jax.nn.sigmoid(z)

    full = lambda a: pl.BlockSpec(a.shape, lambda i: (0,) * a.ndim)
    small = [lre, lim, ldt, bre_t, bim_t, cre_w, cim_w, d_row, w_glu, b_glu]
    return _call(
        body, "s5_fwd", (tp // R,),
        [pl.BlockSpec((R, S5_W), lambda i: (i, 0))] + [full(a) for a in small],
        [pl.BlockSpec((R, S5_W), lambda i: (i, 0)), pl.BlockSpec((R, S5_N), lambda i: (i, 0)),
         pl.BlockSpec((R, S5_N), lambda i: (i, 0))],
        [jax.ShapeDtypeStruct((tp, S5_W), F32), jax.ShapeDtypeStruct((tp, S5_N), F32),
         jax.ShapeDtypeStruct((tp, S5_N), F32)],
        [pltpu.VMEM((S5_W, S5_N), MM)] * 4 + [pltpu.VMEM((8, 8, S5_N), F32), pltpu.VMEM((8, S5_N), F32),
                                              pltpu.VMEM((8, S5_N), F32)],
        (u, *small), jobs)


def _s5_bwd(dy_out, u, xr, xi, lre, lim, ldt, bre_t, bim_t, cre_w, cim_w, d_row, w_glu, b_glu, jobs=()):
    tp = u.shape[0]
    R = ROW_BLK
    nb = tp // R

    def body(dyo_ref, u_ref, xr_ref, xi_ref, xpr_ref, xpi_ref,
             lre_ref, lim_ref, ldt_ref, bre_ref, bim_ref, cre_ref, cim_ref, d_ref, wg_ref, bg_ref,
             du_ref, dlre_ref, dlim_ref, dldt_ref, dbre_ref, dbim_ref, dcre_ref, dcim_ref, dd_ref, dwg_ref, dbg_ref,
             bbd_r, bbd_i, cbd_r, cbd_i, tab_ref, car_r, car_i, gr_ref, gi_ref, xer_ref, xei_ref,
             abr, abi, acr, aci, adr, adi):
        i = pl.program_id(0)

        @pl.when(i == 0)
        def _():
            lbr, lbi, bbr, bbi = _s5_disc(lre_ref[...], lim_ref[...], ldt_ref[...], bre_ref[...], bim_ref[...])
            br, bi, cr, ci = _s5_block_diag(bbr, bbi, cre_ref[...], cim_ref[...])
            bbd_r[...] = br.astype(MM)
            bbd_i[...] = bi.astype(MM)
            cbd_r[...] = cr.astype(MM)
            cbd_i[...] = ci.astype(MM)
            _store_tables(tab_ref, _s5_tables(lbr, lbi, True))
            for ref in (car_r, car_i, abr, abi, acr, aci, adr, adi, dd_ref, dwg_ref, dbg_ref):
                ref[...] = jnp.zeros_like(ref)

        u = u_ref[...]
        xrv = xr_ref[...]
        xiv = xi_ref[...]
        y = _dot_nt(xrv, cbd_r[...]) - _dot_nt(xiv, cbd_i[...]) + d_ref[...] * u
        yg, t = _gelu(y)
        z = _dot(yg, wg_ref[...]) + bg_ref[...]
        s = jax.nn.sigmoid(z)
        dout = dyo_ref[...]
        dz = dout * yg * s * (1.0 - s)
        dyg = dout * s + _dot_nt(dz, wg_ref[...])
        dwg_ref[...] += _dot_tn(yg, dz)
        dbg_ref[...] += _colsum(dz)
        dy = dyg * (0.5 * (1.0 + t) + 0.5 * y * (1.0 - t * t) * GELU_C * (1.0 + 3.0 * GELU_A * y * y))
        dd_ref[...] += _colsum(dy * u)
        acr[...] += _dot_tn(dy, xrv)
        aci[...] -= _dot_tn(dy, xiv)
        gr_ref[...] = _dot(dy, cbd_r[...])
        gi_ref[...] = -_dot(dy, cbd_i[...])
        has_prev = (i < nb - 1).astype(F32)
        xer_ref[0:8, :] = xpr_ref[...] * has_prev
        xei_ref[0:8, :] = xpi_ref[...] * has_prev
        xer_ref[8:R + 8, :] = xrv
        xei_ref[8:R + 8, :] = xiv
        row = lax.broadcasted_iota(jnp.int32, (8, S5_LANES), 0)
        for j in range(S5_N // S5_LANES):
            lanes = pl.ds(j * S5_LANES, S5_LANES)
            pr = tab_ref[6, :, lanes]
            pi = tab_ref[7, :, lanes]

            def step(n, carry):
                cr, ci, sar, sai = carry
                g = R // 8 - 1 - n
                r0 = pl.multiple_of(g * 8, 8)
                rows = pl.ds(r0, 8)
                gr, gi = _scan8(gr_ref[rows, lanes], gi_ref[rows, lanes], tab_ref, lanes, True)
                br = jnp.broadcast_to(cr[0:1, :], cr.shape)
                bi = jnp.broadcast_to(ci[0:1, :], ci.shape)
                gr = gr + (pr * br - pi * bi)
                gi = gi + (pr * bi + pi * br)
                gr_ref[rows, lanes] = gr
                gi_ref[rows, lanes] = gi
                last = row == 7
                xpr = pltpu.roll(jnp.where(last, xer_ref[rows, lanes], xer_ref[pl.ds(r0 + 8, 8), lanes]), 1, 0)
                xpi = pltpu.roll(jnp.where(last, xei_ref[rows, lanes], xei_ref[pl.ds(r0 + 8, 8), lanes]), 1, 0)
                return gr, gi, sar + (gr * xpr + gi * xpi), sai + (gi * xpr - gr * xpi)

            cr, ci, sar, sai = lax.fori_loop(
                0, R // 8, step, (car_r[:, lanes], car_i[:, lanes], adr[:, lanes], adi[:, lanes]), unroll=2)
            car_r[:, lanes] = cr
            car_i[:, lanes] = ci
            adr[:, lanes] = sar
            adi[:, lanes] = sai
        grv = gr_ref[...]
        giv = gi_ref[...]
        du_ref[...] = (dy * d_ref[...] + _dot_nt(grv, bbd_r[...]) + _dot_nt(giv, bbd_i[...])).astype(du_ref.dtype)
        abr[...] += _dot_tn(u, grv)
        abi[...] += _dot_tn(u, giv)

        @pl.when(i == nb - 1)
        def _():
            mask = _bd_mask()
            r16 = lax.broadcasted_iota(jnp.int32, (S5_H, S5_W), 1)
            h16 = lax.broadcasted_iota(jnp.int32, (S5_H, S5_W), 0)
            fold_b = jnp.bitwise_and(r16, S5_H - 1) == h16
            c64 = lax.broadcasted_iota(jnp.int32, (S5_N, S5_P), 0)
            p64 = lax.broadcasted_iota(jnp.int32, (S5_N, S5_P), 1)
            fold_c = jnp.bitwise_and(c64, S5_P - 1) == p64
            dbbr = _dot_sel_lhs(fold_b, jnp.where(mask, abr[...], 0.0))
            dbbi = _dot_sel_lhs(fold_b, jnp.where(mask, abi[...], 0.0))
            dcre_ref[...] = _dot_sel_rhs(jnp.where(mask, acr[...], 0.0), fold_c)
            dcim_ref[...] = _dot_sel_rhs(jnp.where(mask, aci[...], 0.0), fold_c)
            dlbr = _colsum(adr[...])
            dlbi = _colsum(adi[...])
            _, vjp = jax.vjp(_s5_disc, lre_ref[...], lim_ref[...], ldt_ref[...], bre_ref[...], bim_ref[...])
            dlre, dlim, dldt, dbre, dbim = vjp((dlbr, dlbi, dbbr, dbbi))
            dlre_ref[...] = dlre
            dlim_ref[...] = dlim
            dbre_ref[...] = dbre
            dbim_ref[...] = dbim
            gsel = jnp.right_shift(lax.broadcasted_iota(jnp.int32, (S5_N, HEAD), 0), 6) == \
                lax.broadcasted_iota(jnp.int32, (S5_N, HEAD), 1)
            dldt_ref[...] = _dot_sel_rhs(dldt, gsel)

    full = lambda a: pl.BlockSpec(a.shape, lambda i: (0,) * a.ndim)
    rev = lambda w: pl.BlockSpec((R, w), lambda i: (nb - 1 - i, 0))
    prev8 = pl.BlockSpec((8, S5_N), lambda i: (jnp.maximum((nb - 1 - i) * (R // 8) - 1, 0), 0))
    small = [lre, lim, ldt, bre_t, bim_t, cre_w, cim_w, d_row, w_glu, b_glu]
    outs = [((tp, S5_W), rev(S5_W))] + [
        (s, pl.BlockSpec(s, lambda i: (0, 0))) for s in
        [(1, S5_N), (1, S5_N), (1, HEAD), (S5_H, S5_N), (S5_H, S5_N), (S5_W, S5_P), (S5_W, S5_P),
         (1, S5_W), (S5_W, S5_W), (1, S5_W)]]
    return _call(
        body, "s5_bwd", (nb,),
        [rev(S5_W), rev(S5_W), rev(S5_N), rev(S5_N), prev8, prev8] + [full(a) for a in small],
        [o[1] for o in outs], [jax.ShapeDtypeStruct(o[0], MM if n == 0 else F32) for n, o in enumerate(outs)],
        [pltpu.VMEM((S5_W, S5_N), MM)] * 4 + [
            pltpu.VMEM((8, 8, S5_N), F32), pltpu.VMEM((8, S5_N), F32), pltpu.VMEM((8, S5_N), F32),
            pltpu.VMEM((R, S5_N), F32), pltpu.VMEM((R, S5_N), F32),
            pltpu.VMEM((R + 8, S5_N), F32), pltpu.VMEM((R + 8, S5_N), F32)] + [pltpu.VMEM((S5_W, S5_N), F32)] * 4 + [
            pltpu.VMEM((8, S5_N), F32), pltpu.VMEM((8, S5_N), F32)],
        (dy_out, u, xr, xi, xr, xi, *small), jobs)


def _ret_fwd(q, k, v, dmat, zeta_b, xi_b, gam_b, jobs=()):
    tp = q.shape[0]
    C = dmat.shape[1]
    nc = tp // C

    def body(q_ref, k_ref, v_ref, dm_ref, ze_ref, xi_ref, ga_ref, o_ref, st_ref, s_ref):
        @pl.when(pl.program_id(0) == 0)
        def _():
            s_ref[...] = jnp.zeros_like(s_ref)

        for h in range(RET_H):
            sl = slice(h * HEAD, (h + 1) * HEAD)
            qh, kh, vh = q_ref[:, sl], k_ref[:, sl], v_ref[:, sl]
            sh = s_ref[h]
            st_ref[0, sl, :] = sh
            scores = _dot_nt(qh, kh) * dm_ref[h]
            o_ref[:, sl] = _dot(scores, vh) + _dot(qh, sh) * xi_ref[h]
            s_ref[h] = ga_ref[h] * sh + _dot_tn(kh.astype(F32) * ze_ref[h], vh)

    blk = pl.BlockSpec((C, RET_W), lambda c: (c, 0))
    cst = lambda a: pl.BlockSpec(a.shape, lambda c: (0, 0, 0))
    return _call(
        body, "ret_fwd", (nc,), [blk, blk, blk, cst(dmat), cst(zeta_b), cst(xi_b), cst(gam_b)],
        [blk, pl.BlockSpec((1, RET_W, HEAD), lambda c: (c, 0, 0))],
        [jax.ShapeDtypeStruct((tp, RET_W), F32), jax.ShapeDtypeStruct((nc, RET_W, HEAD), F32)],
        [pltpu.VMEM((RET_H, HEAD, HEAD), F32)], (q, k, v, dmat, zeta_b, xi_b, gam_b), jobs)


def _ret_bwd(q, k, v, do, states, cos2, sin2, dmat, zeta_b, xi_b, gam_b, jobs=()):
    tp = q.shape[0]
    C = dmat.shape[1]
    nc = tp // C

    def body(q_ref, k_ref, v_ref, do_ref, st_ref, cos_ref, sin_ref, dm_ref, ze_ref, xi_ref, ga_ref,
             dq_ref, dk_ref, dv_ref, ds_ref):
        @pl.when(pl.program_id(0) == 0)
        def _():
            ds_ref[...] = jnp.zeros_like(ds_ref)

        cos = cos_ref[...]
        sin = sin_ref[...]
        for h in range(RET_H):
            sl = slice(h * HEAD, (h + 1) * HEAD)
            qh, kh, vh = q_ref[:, sl], k_ref[:, sl], v_ref[:, sl]
            dmh = dm_ref[h]
            sh = st_ref[0, sl, :]
            dsn = ds_ref[h]
            doh = do_ref[:, sl]
            dox = doh * xi_ref[h]
            a = _dot_nt(qh, kh) * dmh
            dqk = _dot_nt(doh, vh) * dmh
            kz = kh.astype(F32) * ze_ref[h]
            dv_ref[:, sl] = (_dot_tn(a, doh) + _dot(kz, dsn)).astype(dv_ref.dtype)
            dqr = _dot(dqk, kh) + _dot_nt(dox, sh)
            dkr = _dot_tn(dqk, qh) + ze_ref[h] * _dot_nt(vh, dsn)
            ds_ref[h] = ga_ref[h] * dsn + _dot_tn(qh, dox)
            dq_ref[:, sl] = (dqr * cos - pltpu.roll(dqr, HEAD // 2, 1) * sin).astype(dq_ref.dtype)
            dk_ref[:, sl] = ((dkr * cos - pltpu.roll(dkr, HEAD // 2, 1) * sin) * (HEAD ** -0.5)).astype(dk_ref.dtype)

    blk = pl.BlockSpec((C, RET_W), lambda c: (nc - 1 - c, 0))
    tab = pl.BlockSpec((C, HEAD), lambda c: (nc - 1 - c, 0))
    cst = lambda a: pl.BlockSpec(a.shape, lambda c: (0, 0, 0))
    return _call(
        body, "ret_bwd", (nc,),
        [blk, blk, blk, blk, pl.BlockSpec((1, RET_W, HEAD), lambda c: (nc - 1 - c, 0, 0)), tab, tab,
         cst(dmat), cst(zeta_b), cst(xi_b), cst(gam_b)],
        [blk, blk, blk], [jax.ShapeDtypeStruct((tp, RET_W), MM)] * 3, [pltpu.VMEM((RET_H, HEAD, HEAD), F32)],
        (q, k, v, do, states, cos2, sin2, dmat, zeta_b, xi_b, gam_b), jobs)


def _gn_gate(o, gate, gn_g, gn_b):
    xhat, rstd = _ln_fwd(o, GN_EPS)
    on = xhat * gn_g + gn_b
    s = jax.nn.sigmoid(gate)
    return gate * s * on, xhat, rstd, on, s


def _post_up(o, gate, ys5, xhat0, gn_g, gn_b, li_g, li_b, l1_g, l1_b, w_out, w_up, jobs=()):
    tp = o.shape[0]
    R = ROW_BLK

    def body(o_ref, g_ref, ys_ref, xh0_ref, gng, gnb, lig, lib, l1g, l1b, wo_ref, wu_ref,
             ycat_ref, xh1_ref, rstd1_ref, h1b_ref, pre_ref):
        ycat_ref[:, 0:S5_W] = ys_ref[...].astype(ycat_ref.dtype)
        for h in range(RET_H):
            sl = slice(h * HEAD, (h + 1) * HEAD)
            yret = _gn_gate(o_ref[:, sl], g_ref[:, sl], gng[:, sl], gnb[:, sl])[0]
            ycat_ref[:, S5_W + h * HEAD:S5_W + (h + 1) * HEAD] = yret.astype(ycat_ref.dtype)
        mixed = _dot(ycat_ref[...], wo_ref[...])
        h0 = xh0_ref[...] * lig[...] + lib[...]
        xh1, rstd1 = _ln_fwd(ALPHA * h0 + mixed, LN_EPS)
        xh1_ref[...] = xh1
        rstd1_ref[...] = rstd1
        h1b = (xh1 * l1g[...] + l1b[...]).astype(MM)
        h1b_ref[...] = h1b
        for d in range(N_DEV):
            pre_ref[:, d * FF_BLK:(d + 1) * FF_BLK] = jnp.maximum(_dot(h1b, wu_ref[d]), 0.0)

    row = lambda w: pl.BlockSpec((R, w), lambda i: (i, 0))
    full = lambda a: pl.BlockSpec(a.shape, lambda i: (0,) * a.ndim)
    vecs = [gn_g, gn_b, li_g, li_b, l1_g, l1_b]
    outs = [(row(D_MODEL), jax.ShapeDtypeStruct((tp, D_MODEL), MM)), (row(D_MODEL), jax.ShapeDtypeStruct((tp, D_MODEL), F32)),
            (row(1), jax.ShapeDtypeStruct((tp, 1), F32)), (row(D_MODEL), jax.ShapeDtypeStruct((tp, D_MODEL), MM)),
            (row(D_FF), jax.ShapeDtypeStruct((tp, D_FF), F32))]
    return _call(
        body, "post_up", (tp // R,),
        [row(RET_W), row(RET_W), row(S5_W), row(D_MODEL)] + [full(a) for a in vecs] + [_VMEM, _VMEM],
        [o[0] for o in outs], [o[1] for o in outs], [], (o, gate, ys5, xhat0, *vecs, w_out, w_up), jobs)


def _post_down(pre, xhat1, tgt, l1_g, l1_b, l2_g, l2_b, w_down):
    tp = pre.shape[0]
    seq = tgt.shape[0]
    R = ROW_BLK

    def body(pre_ref, xh1_ref, ta, tb, tc, l1g, l1b, l2g, l2b, wd_ref,
             dr2_ref, dffb_ref, loss_ref, dl2g_ref, dl2b_ref, tgt_ref):
        i = pl.program_id(0)

        @pl.when(i == 0)
        def _():
            for ref in (loss_ref, dl2g_ref, dl2b_ref):
                ref[...] = jnp.zeros_like(ref)

        tgt_ref[0:CHUNK, :] = ta[...]
        tgt_ref[CHUNK:2 * CHUNK, :] = tb[...]
        tgt_ref[2 * CHUNK:3 * CHUNK, :] = tc[...]
        ff = jnp.zeros((R, D_MODEL), F32)
        for d in range(N_DEV):
            pre = pre_ref[:, d * FF_BLK:(d + 1) * FF_BLK]
            ff = ff + _dot(pre * pre, wd_ref[d * FF_BLK:(d + 1) * FF_BLK, :])
        h1 = xh1_ref[...] * l1g[...] + l1b[...]
        xh2, rstd2 = _ln_fwd(ALPHA * h1 + ff, LN_EPS)
        h2 = xh2 * l2g[...] + l2b[...]
        valid = (i * R + lax.broadcasted_iota(jnp.int32, (R, 1), 0)) >= CHUNK
        err = jnp.where(valid, h2 - tgt_ref[...], 0.0)
        loss_ref[...] += 0.5 * jnp.sum(err * err) / D_MODEL
        dh2 = err * (1.0 / D_MODEL)
        dl2g_ref[...] += _colsum(dh2 * xh2)
        dl2b_ref[...] += _colsum(dh2)
        dr2 = _ln_bwd(dh2 * l2g[...], xh2, rstd2)
        dr2_ref[...] = dr2
        dffb_ref[...] = dr2.astype(MM)

    row = lambda w: pl.BlockSpec((R, w), lambda i: (i, 0))
    full = lambda a: pl.BlockSpec(a.shape, lambda i: (0,) * a.ndim)
    vecs = [l1_g, l1_b, l2_g, l2_b]
    acc = lambda s: (pl.BlockSpec(s, lambda i: (0, 0)), jax.ShapeDtypeStruct(s, F32))
    outs = [(row(D_MODEL), jax.ShapeDtypeStruct((tp, D_MODEL), F32)), (row(D_MODEL), jax.ShapeDtypeStruct((tp, D_MODEL), MM)),
            acc((8, HEAD)), acc((1, D_MODEL)), acc((1, D_MODEL))]
    return pl.pallas_call(
        body, name="post_down", grid=(tp // R,),
        in_specs=[row(D_FF), row(D_MODEL)] + _shift3(seq // CHUNK) + [full(a) for a in vecs] + [_VMEM],
        out_specs=[o[0] for o in outs], out_shape=[o[1] for o in outs],
        scratch_shapes=[pltpu.VMEM((R, D_MODEL), F32)],
        compiler_params=_params(("arbitrary",)),
    )(pre, xhat1, tgt, tgt, tgt, *vecs, w_down)


def _mlp_bwd(h1b, dffb, pre, w_up, w_down):
    tp = h1b.shape[0]
    R = MLP_ROWS if tp % MLP_ROWS == 0 else ROW_BLK
    nr = tp // R

    def body(h_ref, df_ref, pre_ref, wu_ref, wd_ref, gup_ref, gdn_ref, dh1_ref, aup, adn):
        d = pl.program_id(0)
        r = pl.program_id(1)

        @pl.when(r == 0)
        def _():
            aup[...] = jnp.zeros_like(aup)
            adn[...] = jnp.zeros_like(adn)

        h = h_ref[...]
        df = df_ref[...]
        wu = wu_ref[0]
        wd = wd_ref[0]
        pre = pre_ref[...]
        dpre = (_dot_nt(df, wd) * (2.0 * pre)).astype(MM)

        aup[...] += _dot_tn(h, dpre)
        adn[...] += _dot_tn(pre * pre, df)
        contrib = _dot_nt(dpre, wu)
        rows = pl.ds(pl.multiple_of(r * R, 64), R)

        @pl.when(d == 0)
        def _():
            dh1_ref[rows, :] = contrib

        @pl.when(d > 0)
        def _():
            dh1_ref[rows, :] += contrib

        @pl.when(r == nr - 1)
        def _():
            gup_ref[0] = aup[...].astype(gup_ref.dtype)
            gdn_ref[0] = adn[...].astype(gdn_ref.dtype)

    return pl.pallas_call(
        body, name="mlp_bwd", grid=(N_DEV, nr),
        in_specs=[pl.BlockSpec((R, D_MODEL), lambda d, r: (r, 0)), pl.BlockSpec((R, D_MODEL), lambda d, r: (r, 0)),
                  pl.BlockSpec((R, FF_BLK), lambda d, r: (r, d)),
                  pl.BlockSpec((1, D_MODEL, FF_BLK), lambda d, r: (d, 0, 0)),
                  pl.BlockSpec((1, FF_BLK, D_MODEL), lambda d, r: (d, 0, 0))],
        out_specs=[pl.BlockSpec((1, D_MODEL, FF_BLK), lambda d, r: (d, 0, 0)),
                   pl.BlockSpec((1, FF_BLK, D_MODEL), lambda d, r: (d, 0, 0)), _VMEM],
        out_shape=[jax.ShapeDtypeStruct((N_DEV, D_MODEL, FF_BLK), MM), jax.ShapeDtypeStruct((N_DEV, FF_BLK, D_MODEL), MM),
                   jax.ShapeDtypeStruct((tp, D_MODEL), F32)],
        scratch_shapes=[pltpu.VMEM((D_MODEL, FF_BLK), F32), pltpu.VMEM((FF_BLK, D_MODEL), F32)],
        compiler_params=_params(("arbitrary", "arbitrary")),
    )(h1b, dffb, pre, w_up, w_down.reshape(N_DEV, FF_BLK, D_MODEL))


def _post_bwd(dh1m, dr2, xhat1, rstd1, ycat, o, gate, gn_g, gn_b, l1_g, w_out, jobs=()):
    tp = o.shape[0]
    R = ROW_BLK
    nb = tp // R

    def body(dm_ref, dr2_ref, xh1_ref, rs1_ref, yc_ref, o_ref, g_ref, gng, gnb, l1g, wo_ref,
             do_ref, dg_ref, dys_ref, dh0_ref, gwo_ref, dl1g_ref, dl1b_ref, dgng_ref, dgnb_ref, awo):
        i = pl.program_id(0)

        @pl.when(i == 0)
        def _():
            for ref in (awo, dl1g_ref, dl1b_ref, dgng_ref, dgnb_ref):
                ref[...] = jnp.zeros_like(ref)

        dh1 = dm_ref[...] + ALPHA * dr2_ref[...]
        xh1 = xh1_ref[...]
        dl1g_ref[...] += _colsum(dh1 * xh1)
        dl1b_ref[...] += _colsum(dh1)
        dr1 = _ln_bwd(dh1 * l1g[...], xh1, rs1_ref[...])
        dh0_ref[...] = ALPHA * dr1
        dmix = dr1.astype(MM)
        awo[...] += _dot_tn(yc_ref[...], dmix)
        dyc = _dot_nt(dmix, wo_ref[...])
        dys_ref[...] = dyc[:, 0:S5_W]
        for h in range(RET_H):
            sl = slice(h * HEAD, (h + 1) * HEAD)
            gt = g_ref[:, sl]
            _, xhat, rstd, on, s = _gn_gate(o_ref[:, sl], gt, gng[:, sl], gnb[:, sl])
            dyr = dyc[:, S5_W + h * HEAD:S5_W + (h + 1) * HEAD]
            dg_ref[:, sl] = (dyr * on * (s * (1.0 + gt * (1.0 - s)))).astype(dg_ref.dtype)
            don = dyr * gt * s
            dgng_ref[:, sl] += _colsum(don * xhat)
            dgnb_ref[:, sl] += _colsum(don)
            do_ref[:, sl] = _ln_bwd(don * gng[:, sl], xhat, rstd)

        @pl.when(i == nb - 1)
        def _():
            gwo_ref[...] = awo[...].astype(gwo_ref.dtype)

    row = lambda w: pl.BlockSpec((R, w), lambda i: (i, 0))
    full = lambda a: pl.BlockSpec(a.shape, lambda i: (0,) * a.ndim)
    acc = lambda s, dt=F32: (pl.BlockSpec(s, lambda i: (0, 0)), jax.ShapeDtypeStruct(s, dt))
    outs = [(row(RET_W), jax.ShapeDtypeStruct((tp, RET_W), F32)), (row(RET_W), jax.ShapeDtypeStruct((tp, RET_W), MM)),
            (row(S5_W), jax.ShapeDtypeStruct((tp, S5_W), F32)), (row(D_MODEL), jax.ShapeDtypeStruct((tp, D_MODEL), F32)),
            acc((D_MODEL, D_MODEL), MM), acc((1, D_MODEL)), acc((1, D_MODEL)), acc((1, RET_W)), acc((1, RET_W))]
    return _call(
        body, "post_bwd", (nb,),
        [row(D_MODEL), row(D_MODEL), row(D_MODEL), row(1), row(D_MODEL), row(RET_W), row(RET_W),
         full(gn_g), full(gn_b), full(l1_g), _VMEM],
        [o[0] for o in outs], [o[1] for o in outs],
        [pltpu.VMEM((D_MODEL, D_MODEL), F32)],
        (dh1m, dr2, xhat1, rstd1, ycat, o, gate, gn_g, gn_b, l1_g, w_out), jobs)


def _in_bwd(du, dq, dk, dv, dg, dh0r, xhat0, rstd0, li_g, li_b, w_int, jobs=()):
    tp = du.shape[0]
    R = PROJ_ROWS if tp % PROJ_ROWS == 0 else ROW_BLK
    nb = tp // R
    segs = [(0, S5_W)] + [(S5_W + n * RET_W, S5_W + (n + 1) * RET_W) for n in range(4)]

    def body(du_ref, dq_ref, dk_ref, dv_ref, dg_ref, dh0r_ref, xh_ref, rs_ref, lig, lib, w_ref,
             gx_ref, dmeta_ref, gw_ref, dlg_ref, dlb_ref, aw, stage, out_sems):
        i = pl.program_id(0)
        slot = i % 2

        def to_gx(step_slot, first):
            if first:
                return pltpu.make_async_copy(stage.at[0, CHUNK:R, :], gx_ref.at[0:R - CHUNK, :], out_sems.at[0])
            return pltpu.make_async_copy(stage.at[step_slot], gx_ref.at[pl.ds(i * R - CHUNK, R), :], out_sems.at[step_slot])

        @pl.when(i == 0)
        def _():
            for ref in (aw, dlg_ref, dlb_ref):
                ref[...] = jnp.zeros_like(ref)

        @pl.when(i >= 3)
        def _():
            to_gx(slot, False).wait()

        valid = (i * R + lax.broadcasted_iota(jnp.int32, (R, 1), 0)) >= PAD
        xh = xh_ref[...]
        hb = (xh * lig[...] + lib[...]).astype(MM)
        dh0 = dh0r_ref[...]
        for (lo, hi), ref in zip(segs, (du_ref, dq_ref, dk_ref, dv_ref, dg_ref)):
            dseg = jnp.where(valid, ref[...], 0.0).astype(MM)
            dh0 = dh0 + _dot(dseg, w_ref[lo:hi, :])
            aw[lo:hi, :] += _dot_tn(dseg, hb)
        dlg_ref[...] += _colsum(dh0 * xh)
        dlb_ref[...] += _colsum(dh0)
        draw = _ln_bwd(dh0 * lig[...], xh, rs_ref[...])
        stage[slot] = draw

        @pl.when(i == 0)
        def _():
            dmeta_ref[...] = draw[PAD:CHUNK, :]
            first = to_gx(0, True)
            first.start()
            first.wait()

        @pl.when(i > 0)
        def _():
            to_gx(slot, False).start()

        @pl.when(i == nb - 1)
        def _():
            gw_ref[...] = aw[...].astype(gw_ref.dtype)
            for back in (1, 0):
                if nb - 1 - back >= 1:
                    to_gx((nb - 1 - back) % 2, False).wait()

    row = lambda w: pl.BlockSpec((R, w), lambda i: (i, 0))
    full = lambda a: pl.BlockSpec(a.shape, lambda i: (0,) * a.ndim)
    acc = lambda s, dt=F32: (pl.BlockSpec(s, lambda i: (0, 0)), jax.ShapeDtypeStruct(s, dt))
    outs = [(_ANY, jax.ShapeDtypeStruct((tp - CHUNK, D_MODEL), F32)), acc((N_META, D_MODEL)), acc((PROJ_W, D_MODEL), MM),
            acc((1, D_MODEL)), acc((1, D_MODEL))]
    return _call(
        body, "in_bwd", (nb,),
        [row(S5_W), row(RET_W), row(RET_W), row(RET_W), row(RET_W), row(D_MODEL), row(D_MODEL), row(1),
         full(li_g), full(li_b), _VMEM],
        [o[0] for o in outs], [o[1] for o in outs],
        [pltpu.VMEM((PROJ_W, D_MODEL), F32), pltpu.VMEM((2, R, D_MODEL), F32), pltpu.SemaphoreType.DMA((2,))],
        (du, dq, dk, dv, dg, dh0r, xhat0, rstd0, li_g, li_b, w_int), jobs)


def _place():
    return lax.axis_index("x"), lax.axis_index("y"), lax.axis_index("c")


def _dma_sems(n):
    return pltpu.SemaphoreType.DMA((n,))


def _job_gather(shard):
    def parts(ins, outs, sems):
        (src,), (out,), (send_sems, recv_sems, local_sem) = ins, outs, sems
        x, y, c = _place()
        north = c == 1
        me, sib = (x, y, c), (x, y, 1 - c)
        xn, yn, dg = (1 - x, y, c), (x, 1 - y, c), (1 - x, 1 - y, c)
        relay_from = (jnp.where(north, 1 - x, x), jnp.where(north, y, 1 - y), c)
        relay_to = (jnp.where(north, x, 1 - x), jnp.where(north, 1 - y, y), c)

        def slot(dev):
            return out.at[4 * dev[0] + 2 * dev[1] + dev[2]]

        def copy(k, block, to, from_input=False):
            return pltpu.make_async_remote_copy(
                src_ref=src if from_input else slot(block), dst_ref=slot(block),
                send_sem=send_sems.at[k], recv_sem=recv_sems.at[k], device_id=to, device_id_type=_MESH)

        mine = lambda: pltpu.make_async_copy(src, slot(me), local_sem.at[0])
        first = lambda: [copy(0, me, sib, True), copy(1, me, xn, True), copy(2, me, yn, True)]
        relayed = lambda: [copy(3, relay_from, relay_to), copy(4, xn, sib), copy(5, yn, sib)]
        return me, sib, xn, yn, dg, copy, mine, first, relayed

    def start(ins, outs, sems):
        mine, first = parts(ins, outs, sems)[6:8]
        mine().start()
        for cp in first():
            cp.start()

    def relay(ins, outs, sems):
        me, sib, xn, yn, dg, copy, mine, first, relayed = parts(ins, outs, sems)
        copy(1, xn, me).wait_recv()
        copy(2, yn, me).wait_recv()
        for cp in relayed():
            cp.start()

    def finish(ins, outs, sems):
        me, sib, xn, yn, dg, copy, mine, first, relayed = parts(ins, outs, sems)
        other = 1 - me[2]
        copy(3, dg, me).wait_recv()
        last = copy(6, dg, sib)
        last.start()
        copy(0, sib, me).wait_recv()
        for k, chip in ((4, xn), (5, yn), (6, dg)):
            copy(k, (chip[0], chip[1], other), me).wait_recv()
        for cp in first() + relayed() + [last]:
            cp.wait_send()
        mine().wait()

    return dict(ins=[shard], outs=[jax.ShapeDtypeStruct((N_DEV,) + shard.shape, shard.dtype)],
                sems=[_dma_sems(7), _dma_sems(7), _dma_sems(1)], start=start, middle=relay, finish=finish)


def _job_pair(g):
    def copies(ins, outs, sems):
        x, y, c = _place()
        return [pltpu.make_async_remote_copy(
            src_ref=ins[0].at[2 * j + (1 - c)], dst_ref=outs[0].at[j], send_sem=sems[0].at[j], recv_sem=sems[1].at[j],
            device_id=(x, y, 1 - c), device_id_type=_MESH) for j in range(4)]

    def start(ins, outs, sems):
        for cp in copies(ins, outs, sems):
            cp.start()

    def finish(ins, outs, sems):
        for cp in copies(ins, outs, sems):
            cp.wait()

    return dict(ins=[g], outs=[jax.ShapeDtypeStruct((4,) + g.shape[1:], g.dtype)], sems=[_dma_sems(4), _dma_sems(4)],
                start=start, finish=finish)


def _job_chips(p):
    def copies(ins, outs, sems):
        x, y, c = _place()
        chips = [(1 - x, y), (x, 1 - y), (1 - x, 1 - y)]
        return [pltpu.make_async_remote_copy(
            src_ref=ins[0].at[2 * chip[0] + chip[1]], dst_ref=outs[0].at[k], send_sem=sems[0].at[k],
            recv_sem=sems[1].at[k], device_id=(*chip, c), device_id_type=_MESH) for k, chip in enumerate(chips)]

    def start(ins, outs, sems):
        for cp in copies(ins, outs, sems):
            cp.start()

    def finish(ins, outs, sems):
        for cp in copies(ins, outs, sems):
            cp.wait()

    return dict(ins=[p], outs=[jax.ShapeDtypeStruct((3,) + p.shape[1:], p.dtype)], sems=[_dma_sems(3), _dma_sems(3)],
                start=start, finish=finish)


def _split_job_refs(jobs, ins, outs, sems):
    res, a, b, c = [], 0, 0, 0
    for job in jobs:
        na, nb, nc = len(job["ins"]), len(job["outs"]), len(job["sems"])
        res.append((ins[a:a + na], outs[b:b + nb], sems[c:c + nc]))
        a, b, c = a + na, b + nb, c + nc
    return res


def _call(body, name, grid, in_specs, out_specs, out_shape, scratch, args, jobs=(), prefetch=None):
    jobs = list(jobs)
    n_in, n_out, n_scr = len(in_specs), len(out_specs), len(scratch)
    j_in = [a for job in jobs for a in job["ins"]]
    j_out = [o for job in jobs for o in job["outs"]]
    j_scr = [s for job in jobs for s in job["sems"]]
    nsteps = grid[0]
    n_pre = 0 if prefetch is None else 1

    def wrapped(*refs):
        pre, refs = refs[:n_pre], refs[n_pre:]
        ins, jins = refs[:n_in], refs[n_in:n_in + len(j_in)]
        refs = refs[n_in + len(j_in):]
        outs, jouts = refs[:n_out], refs[n_out:n_out + len(j_out)]
        refs = refs[n_out + len(j_out):]
        scr, jscr = refs[:n_scr], refs[n_scr:]
        per_job = _split_job_refs(jobs, jins, jouts, jscr)

        def middle():
            for job, r in zip(jobs, per_job):
                if "middle" in job:
                    job["middle"](*r)

        @pl.when(pl.program_id(0) == 0)
        def _():
            for job, r in zip(jobs, per_job):
                job["start"](*r)

        if nsteps >= 3:
            pl.when(pl.program_id(0) == nsteps // 2)(middle)

        body(*pre, *ins, *outs, *scr)

        @pl.when(pl.program_id(0) == nsteps - 1)
        def _():
            if nsteps < 3:
                middle()
            for job, r in zip(jobs, per_job):
                job["finish"](*r)

    specs = dict(in_specs=list(in_specs) + [_ANY] * len(j_in), out_specs=list(out_specs) + [_ANY] * len(j_out),
                 scratch_shapes=list(scratch) + j_scr)
    if n_pre:
        specs = dict(grid_spec=pltpu.PrefetchScalarGridSpec(num_scalar_prefetch=1, grid=grid, **specs))
    else:
        specs["grid"] = grid
    res = pl.pallas_call(
        wrapped if jobs else body, name=name, out_shape=list(out_shape) + j_out,
        compiler_params=_params(("arbitrary",) * len(grid)), **specs,
    )(*([prefetch] if n_pre else []), *args, *j_in)
    return list(res[:n_out]), list(res[n_out:])


def _exchange(jobs, name):
    j_in = [a for job in jobs for a in job["ins"]]
    j_out = [o for job in jobs for o in job["outs"]]
    j_scr = [s for job in jobs for s in job["sems"]]

    def body(*refs):
        per_job = _split_job_refs(jobs, refs[:len(j_in)], refs[len(j_in):len(j_in) + len(j_out)],
                                  refs[len(j_in) + len(j_out):])
        for phase in ("start", "middle", "finish"):
            for job, r in zip(jobs, per_job):
                if phase in job:
                    job[phase](*r)

    return pl.pallas_call(body, name=name, out_shape=j_out, in_specs=[_ANY] * len(j_in), out_specs=[_ANY] * len(j_out),
                          scratch_shapes=j_scr)(*j_in)


def _pair_sum(gs, r1s, c_arr, name):
    n = len(gs)

    def body(c_ref, *refs):
        for a in range(n):
            refs[2 * n + a][...] = (refs[a][...].astype(F32) + refs[n + a][...].astype(F32)).astype(refs[2 * n + a].dtype)

    def blk(g, own):
        s = g.shape[1:]
        if own:
            return pl.BlockSpec((1,) + s, lambda j, c_ref: (2 * j + c_ref[0],) + (0,) * len(s))
        return pl.BlockSpec((1,) + s, lambda j, c_ref: (j,) + (0,) * len(s))

    return pl.pallas_call(
        body, name=name,
        grid_spec=pltpu.PrefetchScalarGridSpec(
            num_scalar_prefetch=1, grid=(4,),
            in_specs=[blk(g, True) for g in gs] + [blk(g, False) for g in gs],
            out_specs=[blk(g, False) for g in gs]),
        out_shape=[jax.ShapeDtypeStruct((4,) + g.shape[1:], g.dtype) for g in gs],
        compiler_params=_params(("arbitrary",)),
    )(c_arr, *gs, *r1s)


def _adamw_math(w, g, m, v):
    m = ADAM_B1 * m + (1.0 - ADAM_B1) * g
    v = ADAM_B2 * v + (1.0 - ADAM_B2) * (g * g)
    m_hat = m / (1.0 - ADAM_B1 ** ADAM_STEP)
    v_hat = v / (1.0 - ADAM_B2 ** ADAM_STEP)
    return -ADAM_LR * (m_hat / (jnp.sqrt(v_hat) + ADAM_EPS) + ADAM_WD * w), m, v


def _view(name, a):
    return jnp.swapaxes(a, -1, -2) if name in ("w_in", "s5_b_re", "s5_b_im") else a


def _adamw_shards(items, name, steps, chip, jobs=()):
    n = len(items)

    def body(chip_ref, *refs):
        for a in range(n):
            p_ref, r_ref, w_ref, m_ref, v_ref = refs[5 * a:5 * a + 5]
            g = ((p_ref[0].astype(F32) + r_ref[0].astype(F32)) + r_ref[1].astype(F32)) + r_ref[2].astype(F32)
            outs = refs[5 * n + 4 * a:5 * n + 4 * a + 4]
            outs[0][...] = g
            outs[1][...], outs[2][...], outs[3][...] = _adamw_math(w_ref[...], g, m_ref[...], v_ref[...])

    in_specs, out_specs, out_shape, flat = [], [], [], []
    for p, r, w, m, v in items:
        rows, cols = w.shape
        rb = rows // steps
        in_specs += [pl.BlockSpec((1, rb, cols), lambda i, c: (c[0], i, 0)), pl.BlockSpec((3, rb, cols), lambda i, c: (0, i, 0))]
        wblk = pl.BlockSpec((rb, cols), lambda i, c: (i, 0))
        in_specs += [wblk] * 3
        out_specs += [wblk] * 4
        out_shape += [jax.ShapeDtypeStruct(w.shape, F32)] * 4
        flat += [p, r, w, m, v]
    return _call(body, name, (steps,), in_specs, out_specs, out_shape, [], flat, jobs, prefetch=chip)


def _sum_devices(gathered, name):
    def body(gs_ref, g_ref):
        g = gs_ref[0]
        for s in range(1, N_DEV):
            g = g + gs_ref[s]
        g_ref[...] = g

    return pl.pallas_call(body, name=name, out_shape=jax.ShapeDtypeStruct(gathered.shape[1:], F32),
                          in_specs=[_VMEM], out_specs=_VMEM, compiler_params=_params())(gathered)


def _adamw_native(items, name):
    n = len(items)

    def body(*refs):
        for a in range(n):
            g, w, m, v = (refs[4 * a + t][...] for t in range(4))
            refs[4 * n + 3 * a][...], refs[4 * n + 3 * a + 1][...], refs[4 * n + 3 * a + 2][...] = _adamw_math(w, g, m, v)

    return pl.pallas_call(
        body, name=name, out_shape=[jax.ShapeDtypeStruct(it[1].shape, F32) for it in items for _ in range(3)],
        in_specs=[_VMEM] * (4 * n), out_specs=[_VMEM] * (3 * n), compiler_params=_params(),
    )(*[t for it in items for t in it])


SMALL = ["ln_in_g", "ln_in_b", "s5_lambda_re", "s5_lambda_im", "s5_log_dt", "s5_b_re", "s5_b_im", "s5_c_re", "s5_c_im",
         "s5_d", "s5_b_glu", "ret_gn_g", "ret_gn_b", "ln1_g", "ln1_b", "ln2_g", "ln2_b"]
LATE = ["ln_in_g", "ln_in_b", "meta_tokens"]
EARLY = [n for n in SMALL if n not in LATE] + ["s5_w_glu", "loss"]
LANE = 128


def _pack(arrs):
    parts = []
    for a in arrs:
        f = a.reshape(-1)
        parts.append(jnp.pad(f, (0, (-f.shape[0]) % LANE)))
    flat = jnp.concatenate(parts)
    rows = -(-flat.shape[0] // LANE)
    flat = jnp.pad(flat, (0, (-rows % 8) * LANE + rows * LANE - flat.shape[0]))
    return flat.reshape(-1, LANE)


def _unpack(packed, shapes):
    flat = packed.reshape(-1)
    out, off = [], 0
    for s in shapes:
        n = math.prod(s)
        out.append(flat[off:off + n].reshape(s))
        off += n + (-n) % LANE
    return out


def _rope_tables(tp):
    pos = jnp.arange(tp, dtype=F32) - float(PAD)
    inv_freq = 1.0 / (ROPE_BASE ** (jnp.arange(0, HEAD, 2, dtype=F32) / HEAD))
    ang = pos[:, None] * inv_freq[None, :]
    cos, sin = jnp.cos(ang), jnp.sin(ang)
    return jnp.concatenate([cos, cos], axis=1), jnp.concatenate([-sin, sin], axis=1)


RET_CHUNK = ROW_BLK


def _decay_tables():
    log_gamma = jnp.log1p(-jnp.exp2(-5.0 - jnp.arange(RET_H, dtype=F32)))
    idx = jnp.arange(RET_CHUNK, dtype=F32)
    diff = idx[:, None] - idx[None, :]
    dmat = jnp.where(diff[None] >= 0, jnp.exp(jnp.maximum(diff, 0.0)[None] * log_gamma[:, None, None]), 0.0)
    zeta = jnp.exp((RET_CHUNK - 1.0 - idx)[None] * log_gamma[:, None])
    xi = jnp.exp((idx + 1.0)[None] * log_gamma[:, None])
    gam = jnp.exp(RET_CHUNK * log_gamma)
    wide = lambda t: jnp.broadcast_to(t[:, :, None], (RET_H, RET_CHUNK, HEAD))
    return dmat, wide(zeta), wide(xi), jnp.broadcast_to(gam[:, None, None], (RET_H, HEAD, HEAD))


def _local_step(x2d, tgt, meta_full, w_int, w_out, w_up, w_down, w_glu, sp, distributed):
    tp = x2d.shape[0] + CHUNK
    row = lambda a: a.reshape(1, -1)
    cos2, sin2 = _rope_tables(tp)
    dmat, zeta_b, xi_b, gam_b = _decay_tables()
    li_g, li_b = row(sp["ln_in_g"]), row(sp["ln_in_b"])
    l1_g, l1_b, l2_g, l2_b = row(sp["ln1_g"]), row(sp["ln1_b"]), row(sp["ln2_g"]), row(sp["ln2_b"])
    gn_g, gn_b = row(sp["ret_gn_g"]), row(sp["ret_gn_b"])
    lre, lim = row(sp["s5_lambda_re"]), row(sp["s5_lambda_im"])
    ldt = row(jnp.repeat(sp["s5_log_dt"].reshape(-1), S5_P))
    to_t = lambda b: b.reshape(S5_G, S5_P, S5_H).transpose(2, 0, 1).reshape(S5_H, S5_N)
    bre_t, bim_t = to_t(sp["s5_b_re"]), to_t(sp["s5_b_im"])
    to_w = lambda c: jnp.tile(c.reshape(S5_W, S5_P), (1, 2))
    cre_w, cim_w = to_w(sp["s5_c_re"]), to_w(sp["s5_c_im"])

    jobs = (lambda *j: list(j)) if distributed else (lambda *j: [])
    c_arr = jnp.reshape(lax.axis_index("c"), (1,)).astype(jnp.int32) if distributed else None
    (xhat0, rstd0), bg = _ln_in(x2d, meta_full, jobs(*([_job_gather(w_int), _job_gather(w_glu)] if distributed else [])))
    if distributed:
        w_int, w_glu = bg[0].reshape(PROJ_W, D_MODEL), bg[1].reshape(S5_W, S5_W)
    s5_small = (lre, lim, ldt, bre_t, bim_t, cre_w, cim_w, row(sp["s5_d"]), w_glu, row(sp["s5_b_glu"]))
    (u, q, k, v, gate), bg = _in_proj(xhat0, li_g, li_b, w_int, cos2, sin2,
                                      jobs(_job_gather(w_out) if distributed else None))
    if distributed:
        w_out = bg[0].reshape(D_MODEL, D_MODEL)
    (ys5, xr, xi), bg = _s5_fwd(u, *s5_small, jobs=jobs(_job_gather(w_up) if distributed else None))
    if distributed:
        w_up = bg[0]
    (o, states), _ = _ret_fwd(q, k, v, dmat, zeta_b, xi_b, gam_b)
    (ycat, xhat1, rstd1, h1b, pre), bg = _post_up(o, gate, ys5, xhat0, gn_g, gn_b, li_g, li_b, l1_g, l1_b, w_out, w_up,
                                                  jobs(_job_gather(w_down) if distributed else None))
    if distributed:
        w_down = bg[0].reshape(D_FF, D_MODEL)
    dr2, dffb, loss8, dl2g, dl2b = _post_down(pre, xhat1, tgt, l1_g, l1_b, l2_g, l2_b, w_down)
    g_up, g_down, dh1m = _mlp_bwd(h1b, dffb, pre, w_up, w_down)
    (do, dgate, dys5, dh0r, g_out, dl1g, dl1b, dgng, dgnb), bg = _post_bwd(
        dh1m, dr2, xhat1, rstd1, ycat, o, gate, gn_g, gn_b, l1_g, w_out,
        jobs(*([_job_pair(g_up), _job_pair(g_down)] if distributed else [])))
    g_out = g_out.reshape(N_DEV, D_MODEL // N_DEV, D_MODEL)
    if distributed:
        p_up, p_down = _pair_sum([g_up, g_down], bg, c_arr, "pair_sum_mlp")
    (du, dlre, dlim, dldt, dbre_t, dbim_t, dcre, dcim, dd, dwglu, dbglu), bg = _s5_bwd(
        dys5, u, xr, xi, *s5_small,
        jobs=jobs(*([_job_chips(p_up), _job_chips(p_down), _job_pair(g_out)] if distributed else [])))
    if distributed:
        r_up, r_down = bg[0], bg[1]
        (p_out,) = _pair_sum([g_out], bg[2:], c_arr, "pair_sum_out")
    from_t = lambda t: t.reshape(S5_H, S5_G, S5_P).transpose(1, 0, 2)
    small = {
        "s5_lambda_re": dlre, "s5_lambda_im": dlim, "s5_log_dt": dldt[:, :S5_G],
        "s5_b_re": from_t(dbre_t), "s5_b_im": from_t(dbim_t), "s5_c_re": dcre, "s5_c_im": dcim, "s5_d": dd,
        "s5_b_glu": dbglu, "ret_gn_g": dgng, "ret_gn_b": dgnb, "ln1_g": dl1g, "ln1_b": dl1b, "ln2_g": dl2g, "ln2_b": dl2b,
        "s5_w_glu": dwglu, "loss": loss8[0:1, 0:1]}
    early_pack = _pack([small[n] for n in EARLY])
    (dq, dk, dv), bg = _ret_bwd(q, k, v, do, states, cos2, sin2, dmat, zeta_b, xi_b, gam_b,
                                jobs(*([_job_chips(p_out), _job_gather(early_pack)] if distributed else [])))
    (grad_x, dmeta, g_int, dlig, dlib), _ = _in_bwd(du, dq, dk, dv, dgate, dh0r, xhat0, rstd0, li_g, li_b, w_int)
    small.update(ln_in_g=dlig, ln_in_b=dlib, meta_tokens=dmeta)
    g_int = g_int.reshape(N_DEV, PROJ_W // N_DEV, D_MODEL)
    if distributed:
        (r1_in,) = _exchange([_job_pair(g_int)], "exchange_pair_in")
        (p_in,) = _pair_sum([g_int], [r1_in], c_arr, "pair_sum_in")
        big = dict(chip_sums=[p_in, p_out, p_up, p_down], received=[None, bg[0], r_up, r_down], early=bg[1])
    else:
        big = dict(partials=[g_int, g_out, g_up, g_down])
    return grad_x, big, small


def kernel(x, meta_tokens, ln_in_g, ln_in_b, w_in, s5_lambda_re, s5_lambda_im, s5_log_dt, s5_b_re, s5_b_im, s5_c_re, s5_c_im, s5_d, s5_w_glu, s5_b_glu, ret_gn_g, ret_gn_b, w_out, ln1_g, ln1_b, w_up, w_down, ln2_g, ln2_b, loss_target, m_meta_tokens, m_ln_in_g, m_ln_in_b, m_w_in, m_s5_lambda_re, m_s5_lambda_im, m_s5_log_dt, m_s5_b_re, m_s5_b_im, m_s5_c_re, m_s5_c_im, m_s5_d, m_s5_w_glu, m_s5_b_glu, m_ret_gn_g, m_ret_gn_b, m_w_out, m_ln1_g, m_ln1_b, m_w_up, m_w_down, m_ln2_g, m_ln2_b, v_meta_tokens, v_ln_in_g, v_ln_in_b, v_w_in, v_s5_lambda_re, v_s5_lambda_im, v_s5_log_dt, v_s5_b_re, v_s5_b_im, v_s5_c_re, v_s5_c_im, v_s5_d, v_s5_w_glu, v_s5_b_glu, v_ret_gn_g, v_ret_gn_b, v_w_out, v_ln1_g, v_ln1_b, v_w_up, v_w_down, v_ln2_g, v_ln2_b):
    args = dict(locals())
    names = ["meta_tokens", "ln_in_g", "ln_in_b", "w_in", "s5_lambda_re", "s5_lambda_im", "s5_log_dt", "s5_b_re", "s5_b_im",
             "s5_c_re", "s5_c_im", "s5_d", "s5_w_glu", "s5_b_glu", "ret_gn_g", "ret_gn_b", "w_out", "ln1_g", "ln1_b",
             "w_up", "w_down", "ln2_g", "ln2_b"]
    ax, ay, ac = _place()
    me = 4 * ax + 2 * ay + ac

    (a_meta,) = _exchange([_job_gather(meta_tokens)], "gather_meta")
    meta_full = a_meta.transpose(1, 0, 2).reshape(N_META, D_MODEL)

    sp = {n: args[n] for n in SMALL}
    grad_x, big, small = _local_step(x[0], loss_target[0], meta_full, w_in[0].T.astype(MM), w_out[0].astype(MM),
                                   w_up[0].astype(MM), w_down[0].astype(MM), s5_w_glu[0].astype(MM), sp, True)

    j_arr = jnp.reshape(2 * ax + ay, (1,)).astype(jnp.int32)
    two_d = lambda a: a.reshape(a.shape[-2:])
    item = lambda n, p, r: (p, r, *(two_d(_view(n, a)) for a in (args[n], args["m_" + n], args["v_" + n])))
    late_pack = _pack([small[n] for n in LATE])
    mlp = ("w_out", "w_up", "w_down")
    res, (r_in, late_all) = _adamw_shards(
        [item(n, p, r) for n, p, r in zip(mlp, big["chip_sums"][1:], big["received"][1:])], "adamw_mlp", 8, j_arr,
        [_job_chips(big["chip_sums"][0]), _job_gather(late_pack)])
    res_in, _ = _adamw_shards([item("w_in", big["chip_sums"][0], r_in)], "adamw_in", 2, j_arr)
    upd = {"w_in": res_in}
    for idx, n in enumerate(mlp):
        upd[n] = res[4 * idx:4 * idx + 4]
    shard_grads = {n: upd[n][0] for n in upd}

    early_shapes = [_view(n, args[n]).shape for n in EARLY[:-2]] + [(S5_W, S5_W), (1,)]
    late_shapes = [args["ln_in_g"].shape, args["ln_in_b"].shape, (N_META, D_MODEL)]
    g_small = dict(zip(EARLY, _unpack(_sum_devices(big["early"], "sum_small_early"), early_shapes)))
    g_small.update(zip(LATE, _unpack(_sum_devices(late_all, "sum_small_late"), late_shapes)))
    loss = g_small["loss"].reshape(())

    shard_grads["meta_tokens"] = lax.dynamic_slice(g_small["meta_tokens"], (0, me * (D_MODEL // N_DEV)),
                                                   (N_META, D_MODEL // N_DEV))
    shard_grads["s5_w_glu"] = lax.dynamic_slice(g_small["s5_w_glu"], (me * (S5_W // N_DEV), 0),
                                                (S5_W // N_DEV, S5_W))[None]
    natives = SMALL + ["meta_tokens", "s5_w_glu"]
    res2 = _adamw_native([(shard_grads[n] if n in shard_grads else g_small[n], *(_view(n, args[p + n]) for p in ("", "m_", "v_")))
                          for n in natives], "adamw_small")
    for idx, n in enumerate(natives):
        upd[n] = [shard_grads[n] if n in shard_grads else g_small[n]] + list(res2[3 * idx:3 * idx + 3])

    grads, deltas, new_m, new_v = ([_view(n, upd[n][t]).reshape(args[n].shape) for n in names] for t in range(4))
    return (loss, grad_x[None], *grads, *deltas, *new_m, *new_v)
```

```python
import math

import jax
import jax.numpy as jnp
from jax import lax
from jax.experimental import pallas as pl
from jax.experimental.pallas import tpu as pltpu

F32 = jnp.float32
MM = jnp.bfloat16

D_MODEL = 1024
N_META = 16
CHUNK = 128
PAD = CHUNK - N_META
S5_W, S5_G, S5_H, S5_P = 256, 16, 16, 64
S5_N = S5_G * S5_P
RET_W, RET_H, HEAD = 768, 6, 128
D_FF = 4096
PROJ_W = S5_W + 4 * RET_W
N_DEV = 8
FF_BLK = D_FF // N_DEV
ROW_BLK = 384
MLP_ROWS = 1408
PROJ_ROWS = 704
ALPHA = 2.0 ** 0.25
LN_EPS = 1e-5
GN_EPS = 1e-5
ROPE_BASE = 10000.0
GELU_C = math.sqrt(2.0 / math.pi)
GELU_A = 0.044715
ADAM_LR, ADAM_B1, ADAM_B2, ADAM_EPS, ADAM_WD, ADAM_STEP = 0.001, 0.9, 0.999, 1e-08, 0.01, 10
VMEM_LIMIT = 60 * 1024 * 1024

_VMEM = pl.BlockSpec(memory_space=pltpu.VMEM)
_ANY = pl.BlockSpec(memory_space=pl.ANY)
_MESH = pl.DeviceIdType.MESH


def _params(sem=None):
    return pltpu.CompilerParams(dimension_semantics=sem, vmem_limit_bytes=VMEM_LIMIT)


def _dot(a, b):
    return jnp.dot(a.astype(MM), b.astype(MM), preferred_element_type=F32)


def _dot_nt(a, b):
    return lax.dot_general(a.astype(MM), b.astype(MM), (((1,), (1,)), ((), ())), preferred_element_type=F32)


def _dot_tn(a, b):
    return lax.dot_general(a.astype(MM), b.astype(MM), (((0,), (0,)), ((), ())), preferred_element_type=F32)


def _split3(a):
    hi = a.astype(jnp.bfloat16)
    r1 = a - hi.astype(F32)
    mid = r1.astype(jnp.bfloat16)
    lo = (r1 - mid.astype(F32)).astype(jnp.bfloat16)
    return hi, mid, lo


def _dot_sel_rhs(a, sel):
    s = sel.astype(jnp.bfloat16)
    return sum(jnp.dot(p, s, preferred_element_type=F32) for p in _split3(a))


def _dot_sel_lhs(sel, b):
    s = sel.astype(jnp.bfloat16)
    return sum(jnp.dot(s, p, preferred_element_type=F32) for p in _split3(b))


def _ln_fwd(r, eps):
    mu = jnp.mean(r, axis=-1, keepdims=True)
    xc = r - mu
    var = jnp.mean(xc * xc, axis=-1, keepdims=True)
    rstd = lax.rsqrt(var + eps)
    return xc * rstd, rstd


def _ln_bwd(dxhat, xhat, rstd):
    m1 = jnp.mean(dxhat, axis=-1, keepdims=True)
    m2 = jnp.mean(dxhat * xhat, axis=-1, keepdims=True)
    return rstd * (dxhat - m1 - xhat * m2)


def _colsum(a):
    return jnp.sum(a, axis=0, keepdims=True)


def _shift3(n_in, block=lambda i: i):
    return [pl.BlockSpec((CHUNK, D_MODEL), (lambda i, j=j: (jnp.clip(3 * block(i) - 1 + j, 0, n_in - 1), 0)))
            for j in range(3)]


def _ln_in(x2d, meta, jobs=(), gather_meta=False):
    seq = x2d.shape[0]
    tp = seq + CHUNK
    R = ROW_BLK
    nb = tp // R
    shard_w = D_MODEL // N_DEV

    def body(xa, xb, xc, meta_ref, xhat_ref, rstd_ref, raw_ref, *gathered):
        raw_ref[0:CHUNK, :] = xa[...]
        raw_ref[CHUNK:2 * CHUNK, :] = xb[...]
        raw_ref[2 * CHUNK:3 * CHUNK, :] = xc[...]

        @pl.when(pl.program_id(0) == nb - 1)
        def _():
            raw_ref[0:PAD, :] = jnp.zeros((PAD, D_MODEL), F32)
            if gather_meta:
                for d in range(N_DEV):
                    pltpu.sync_copy(gathered[0].at[d], raw_ref.at[PAD:CHUNK, d * shard_w:(d + 1) * shard_w])
            else:
                raw_ref[PAD:CHUNK, :] = meta_ref[...]

        xhat_ref[...], rstd_ref[...] = _ln_fwd(raw_ref[...], LN_EPS)

    row = lambda w: pl.BlockSpec((R, w), lambda i: (nb - 1 - i, 0))
    jobs = ([_job_gather(meta)] if gather_meta else []) + list(jobs)
    return _call(
        body, "ln_in", (nb,),
        _shift3(seq // CHUNK, lambda i: nb - 1 - i) + [pl.BlockSpec(meta.shape, lambda i: (0, 0))],
        [row(D_MODEL), row(1)], [jax.ShapeDtypeStruct((tp, D_MODEL), F32), jax.ShapeDtypeStruct((tp, 1), F32)],
        [pltpu.VMEM((R, D_MODEL), F32)], (x2d, x2d, x2d, meta), jobs, early=1 if gather_meta else 0)


def _in_proj(xhat0, ln_g, ln_b, w_int, cos2, sin2, jobs=()):
    tp = xhat0.shape[0]
    R = PROJ_ROWS if tp % PROJ_ROWS == 0 else ROW_BLK

    def body(xh_ref, g_ref, b_ref, w_ref, cos_ref, sin_ref, u_ref, q_ref, k_ref, v_ref, gate_ref):
        hb = (xh_ref[...] * g_ref[...] + b_ref[...]).astype(MM)
        valid = (pl.program_id(0) * R + lax.broadcasted_iota(jnp.int32, (R, 1), 0)) >= PAD

        def seg(lo, hi):
            return jnp.where(valid, _dot_nt(hb, w_ref[lo:hi, :]), 0.0)

        u_ref[...] = seg(0, S5_W)
        cos = cos_ref[...]
        sin = sin_ref[...]
        q = seg(S5_W, S5_W + RET_W)
        k = seg(S5_W + RET_W, S5_W + 2 * RET_W)
        for h in range(RET_H):
            sl = slice(h * HEAD, (h + 1) * HEAD)
            qh = q[:, sl]
            kh = k[:, sl]
            q_ref[:, sl] = (qh * cos + pltpu.roll(qh, HEAD // 2, 1) * sin).astype(q_ref.dtype)
            k_ref[:, sl] = ((kh * cos + pltpu.roll(kh, HEAD // 2, 1) * sin) * (HEAD ** -0.5)).astype(k_ref.dtype)
        v_ref[...] = seg(S5_W + 2 * RET_W, S5_W + 3 * RET_W).astype(v_ref.dtype)
        gate_ref[...] = seg(S5_W + 3 * RET_W, PROJ_W)

    def rows(w, dt):
        return pl.BlockSpec((R, w), lambda i: (i, 0)), jax.ShapeDtypeStruct((tp, w), dt)

    outs = [rows(S5_W, F32), rows(RET_W, MM), rows(RET_W, MM), rows(RET_W, MM), rows(RET_W, F32)]
    full = lambda s: pl.BlockSpec(s, lambda i: (0,) * len(s))
    return _call(
        body, "in_proj", (tp // R,),
        [pl.BlockSpec((R, D_MODEL), lambda i: (i, 0)), full((1, D_MODEL)), full((1, D_MODEL)), _VMEM,
         pl.BlockSpec((R, HEAD), lambda i: (i, 0)), pl.BlockSpec((R, HEAD), lambda i: (i, 0))],
        [o[0] for o in outs], [o[1] for o in outs], [], (xhat0, ln_g, ln_b, w_int, cos2, sin2), jobs)


def _s5_disc(lre, lim, ldt, bre_t, bim_t):
    dt = jnp.exp(ldt)
    mag = jnp.exp(lre * dt)
    ang = lim * dt
    lbr = mag * jnp.cos(ang)
    lbi = mag * jnp.sin(ang)
    den = lre * lre + lim * lim
    nr = lbr - 1.0
    qr = (nr * lre + lbi * lim) / den
    qi = (lbi * lre - nr * lim) / den
    return lbr, lbi, qr * bre_t - qi * bim_t, qr * bim_t + qi * bre_t


def _s5_tables(lbr, lbi, reverse):
    if reverse:
        lbi = -lbi
    pw = [(lbr, lbi)]
    for _ in range(7):
        r, i = pw[-1]
        pw.append((r * lbr - i * lbi, r * lbi + i * lbr))
    row = lax.broadcasted_iota(jnp.int32, (8, S5_N), 0)
    tabs = []
    for k in range(3):
        sh = 2 ** k
        mask = (row < 8 - sh) if reverse else (row >= sh)
        ar, ai = pw[sh - 1]
        tabs.append((jnp.where(mask, ar, 0.0), jnp.where(mask, ai, 0.0)))
    pr = jnp.zeros((8, S5_N), F32)
    pi = jnp.zeros((8, S5_N), F32)
    for i in range(8):
        ar, ai = pw[7 - i] if reverse else pw[i]
        pr = jnp.where(row == i, ar, pr)
        pi = jnp.where(row == i, ai, pi)
    tabs.append((pr, pi))
    return tabs


def _store_tables(tab_ref, tabs):
    for k, (r, i) in enumerate(tabs):
        tab_ref[2 * k] = r
        tab_ref[2 * k + 1] = i


def _bd_mask():
    r = lax.broadcasted_iota(jnp.int32, (S5_W, S5_N), 0)
    c = lax.broadcasted_iota(jnp.int32, (S5_W, S5_N), 1)
    return jnp.right_shift(r, 4) == jnp.right_shift(c, 6)


def _s5_block_diag(bbr_t, bbi_t, cre_w, cim_w):
    mask = _bd_mask()
    bd = lambda t: jnp.where(mask, t, 0.0)
    return (bd(jnp.tile(bbr_t, (S5_G, 1))), bd(jnp.tile(bbi_t, (S5_G, 1))),
            bd(jnp.tile(cre_w, (1, S5_N // HEAD))), bd(jnp.tile(cim_w, (1, S5_N // HEAD))))


def _scan8(xr, xi, tab_ref, lanes, reverse):
    for k in range(3):
        sh = (8 - 2 ** k) if reverse else 2 ** k
        sr = pltpu.roll(xr, sh, 0)
        si = pltpu.roll(xi, sh, 0)
        mr = tab_ref[2 * k, :, lanes]
        mi = tab_ref[2 * k + 1, :, lanes]
        xr, xi = xr + (mr * sr - mi * si), xi + (mr * si + mi * sr)
    return xr, xi


S5_LANES = 256


def _gelu(y):
    t = jnp.tanh(GELU_C * (y + GELU_A * y * y * y))
    return 0.5 * y * (1.0 + t), t


def _s5_fwd(u, lre, lim, ldt, bre_t, bim_t, cre_w, cim_w, d_row, w_glu, b_glu, jobs=()):
    tp = u.shape[0]
    R = ROW_BLK

    def body(u_ref, lre_ref, lim_ref, ldt_ref, bre_ref, bim_ref, cre_ref, cim_ref, d_ref, wg_ref, bg_ref,
             y_ref, xr_ref, xi_ref, bbd_r, bbd_i, cbd_r, cbd_i, tab_ref, car_r, car_i):
        @pl.when(pl.program_id(0) == 0)
        def _():
            lbr, lbi, bbr, bbi = _s5_disc(lre_ref[...], lim_ref[...], ldt_ref[...], bre_ref[...], bim_ref[...])
            br, bi, cr, ci = _s5_block_diag(bbr, bbi, cre_ref[...], cim_ref[...])
            bbd_r[...] = br.astype(MM)
            bbd_i[...] = bi.astype(MM)
            cbd_r[...] = cr.astype(MM)
            cbd_i[...] = ci.astype(MM)
            _store_tables(tab_ref, _s5_tables(lbr, lbi, False))
            car_r[...] = jnp.zeros_like(car_r)
            car_i[...] = jnp.zeros_like(car_i)

        u = u_ref[...]
        ub = u.astype(MM)
        xr_ref[...] = jnp.dot(ub, bbd_r[...], preferred_element_type=F32)
        xi_ref[...] = jnp.dot(ub, bbd_i[...], preferred_element_type=F32)
        for j in range(S5_N // S5_LANES):
            lanes = pl.ds(j * S5_LANES, S5_LANES)
            pr = tab_ref[6, :, lanes]
            pi = tab_ref[7, :, lanes]

            def step(g, carry):
                cr, ci = carry
                rows = pl.ds(pl.multiple_of(g * 8, 8), 8)
                xr, xi = _scan8(xr_ref[rows, lanes], xi_ref[rows, lanes], tab_ref, lanes, False)
                br = jnp.broadcast_to(cr[7:8, :], cr.shape)
                bi = jnp.broadcast_to(ci[7:8, :], ci.shape)
                xr = xr + (pr * br - pi * bi)
                xi = xi + (pr * bi + pi * br)
                xr_ref[rows, lanes] = xr
                xi_ref[rows, lanes] = xi
                return xr, xi

            cr, ci = lax.fori_loop(0, R // 8, step, (car_r[:, lanes], car_i[:, lanes]), unroll=2)
            car_r[:, lanes] = cr
            car_i[:, lanes] = ci
        y = _dot_nt(xr_ref[...], cbd_r[...]) - _dot_nt(xi_ref[...], cbd_i[...]) + d_ref[...] * u
        yg, _ = _gelu(y)
        z = _dot(yg, wg_ref[...]) + bg_ref[...]
        y_ref[...] = yg * jax.nn.sigmoid(z)

    full = lambda a: pl.BlockSpec(a.shape, lambda i: (0,) * a.ndim)
    small = [lre, lim, ldt, bre_t, bim_t, cre_w, cim_w, d_row, w_glu, b_glu]
    return _call(
        body, "s5_fwd", (tp // R,),
        [pl.BlockSpec((R, S5_W), lambda i: (i, 0))] + [full(a) for a in small],
        [pl.BlockSpec((R, S5_W), lambda i: (i, 0)), pl.BlockSpec((R, S5_N), lambda i: (i, 0)),
         pl.BlockSpec((R, S5_N), lambda i: (i, 0))],
        [jax.ShapeDtypeStruct((tp, S5_W), F32), jax.ShapeDtypeStruct((tp, S5_N), F32),
         jax.ShapeDtypeStruct((tp, S5_N), F32)],
        [pltpu.VMEM((S5_W, S5_N), MM)] * 4 + [pltpu.VMEM((8, 8, S5_N), F32), pltpu.VMEM((8, S5_N), F32),
                                              pltpu.VMEM((8, S5_N), F32)],
        (u, *small), jobs)


def _s5_bwd(dy_out, u, xr, xi, lre, lim, ldt, bre_t, bim_t, cre_w, cim_w, d_row, w_glu, b_glu, jobs=()):
    tp = u.shape[0]
    R = ROW_BLK
    nb = tp // R

    def body(dyo_ref, u_ref, xr_ref, xi_ref, xpr_ref, xpi_ref,
             lre_ref, lim_ref, ldt_ref, bre_ref, bim_ref, cre_ref, cim_ref, d_ref, wg_ref, bg_ref,
             du_ref, dlre_ref, dlim_ref, dldt_ref, dbre_ref, dbim_ref, dcre_ref, dcim_ref, dd_ref, dwg_ref, dbg_ref,
             bbd_r, bbd_i, cbd_r, cbd_i, tab_ref, car_r, car_i, gr_ref, gi_ref, xer_ref, xei_ref,
             abr, abi, acr, aci, adr, adi):
        i = pl.program_id(0)

        @pl.when(i == 0)
        def _():
            lbr, lbi, bbr, bbi = _s5_disc(lre_ref[...], lim_ref[...], ldt_ref[...], bre_ref[...], bim_ref[...])
            br, bi, cr, ci = _s5_block_diag(bbr, bbi, cre_ref[...], cim_ref[...])
            bbd_r[...] = br.astype(MM)
            bbd_i[...] = bi.astype(MM)
            cbd_r[...] = cr.astype(MM)
            cbd_i[...] = ci.astype(MM)
            _store_tables(tab_ref, _s5_tables(lbr, lbi, True))
            for ref in (car_r, car_i, abr, abi, acr, aci, adr, adi, dd_ref, dwg_ref, dbg_ref):
                ref[...] = jnp.zeros_like(ref)

        u = u_ref[...]
        xrv = xr_ref[...]
        xiv = xi_ref[...]
        y = _dot_nt(xrv, cbd_r[...]) - _dot_nt(xiv, cbd_i[...]) + d_ref[...] * u
        yg, t = _gelu(y)
        z = _dot(yg, wg_ref[...]) + bg_ref[...]
        s = jax.nn.sigmoid(z)
        dout = dyo_ref[...]
        dz = dout * yg * s * (1.0 - s)
        dyg = dout * s + _dot_nt(dz, wg_ref[...])
        dwg_ref[...] += _dot_tn(yg, dz)
        dbg_ref[...] += _colsum(dz)
        dy = dyg * (0.5 * (1.0 + t) + 0.5 * y * (1.0 - t * t) * GELU_C * (1.0 + 3.0 * GELU_A * y * y))
        dd_ref[...] += _colsum(dy * u)
        acr[...] += _dot_tn(dy, xrv)
        aci[...] -= _dot_tn(dy, xiv)
        gr_ref[...] = _dot(dy, cbd_r[...])
        gi_ref[...] = -_dot(dy, cbd_i[...])
        has_prev = (i < nb - 1).astype(F32)
        xer_ref[0:8, :] = xpr_ref[...] * has_prev
        xei_ref[0:8, :] = xpi_ref[...] * has_prev
        xer_ref[8:R + 8, :] = xrv
        xei_ref[8:R + 8, :] = xiv
        row = lax.broadcasted_iota(jnp.int32, (8, S5_LANES), 0)
        for j in range(S5_N // S5_LANES):
            lanes = pl.ds(j * S5_LANES, S5_LANES)
            pr = tab_ref[6, :, lanes]
            pi = tab_ref[7, :, lanes]

            def step(n, carry):
                cr, ci, sar, sai = carry
                g = R // 8 - 1 - n
                r0 = pl.multiple_of(g * 8, 8)
                rows = pl.ds(r0, 8)
                gr, gi = _scan8(gr_ref[rows, lanes], gi_ref[rows, lanes], tab_ref, lanes, True)
                br = jnp.broadcast_to(cr[0:1, :], cr.shape)
                bi = jnp.broadcast_to(ci[0:1, :], ci.shape)
                gr = gr + (pr * br - pi * bi)
                gi = gi + (pr * bi + pi * br)
                gr_ref[rows, lanes] = gr
                gi_ref[rows, lanes] = gi
                last = row == 7
                xpr = pltpu.roll(jnp.where(last, xer_ref[rows, lanes], xer_ref[pl.ds(r0 + 8, 8), lanes]), 1, 0)
                xpi = pltpu.roll(jnp.where(last, xei_ref[rows, lanes], xei_ref[pl.ds(r0 + 8, 8), lanes]), 1, 0)
                return gr, gi, sar + (gr * xpr + gi * xpi), sai + (gi * xpr - gr * xpi)

            cr, ci, sar, sai = lax.fori_loop(
                0, R // 8, step, (car_r[:, lanes], car_i[:, lanes], adr[:, lanes], adi[:, lanes]), unroll=2)
            car_r[:, lanes] = cr
            car_i[:, lanes] = ci
            adr[:, lanes] = sar
            adi[:, lanes] = sai
        grv = gr_ref[...]
        giv = gi_ref[...]
        du_ref[...] = (dy * d_ref[...] + _dot_nt(grv, bbd_r[...]) + _dot_nt(giv, bbd_i[...])).astype(du_ref.dtype)
        abr[...] += _dot_tn(u, grv)
        abi[...] += _dot_tn(u, giv)

        @pl.when(i == nb - 1)
        def _():
            mask = _bd_mask()
            r16 = lax.broadcasted_iota(jnp.int32, (S5_H, S5_W), 1)
            h16 = lax.broadcasted_iota(jnp.int32, (S5_H, S5_W), 0)
            fold_b = jnp.bitwise_and(r16, S5_H - 1) == h16
            c64 = lax.broadcasted_iota(jnp.int32, (S5_N, S5_P), 0)
            p64 = lax.broadcasted_iota(jnp.int32, (S5_N, S5_P), 1)
            fold_c = jnp.bitwise_and(c64, S5_P - 1) == p64
            dbbr = _dot_sel_lhs(fold_b, jnp.where(mask, abr[...], 0.0))
            dbbi = _dot_sel_lhs(fold_b, jnp.where(mask, abi[...], 0.0))
            dcre_ref[...] = _dot_sel_rhs(jnp.where(mask, acr[...], 0.0), fold_c)
            dcim_ref[...] = _dot_sel_rhs(jnp.where(mask, aci[...], 0.0), fold_c)
            dlbr = _colsum(adr[...])
            dlbi = _colsum(adi[...])
            _, vjp = jax.vjp(_s5_disc, lre_ref[...], lim_ref[...], ldt_ref[...], bre_ref[...], bim_ref[...])
            dlre, dlim, dldt, dbre, dbim = vjp((dlbr, dlbi, dbbr, dbbi))
            dlre_ref[...] = dlre
            dlim_ref[...] = dlim
            dbre_ref[...] = dbre
            dbim_ref[...] = dbim
            gsel = jnp.right_shift(lax.broadcasted_iota(jnp.int32, (S5_N, HEAD), 0), 6) == \
                lax.broadcasted_iota(jnp.int32, (S5_N, HEAD), 1)
            dldt_ref[...] = _dot_sel_rhs(dldt, gsel)

    full = lambda a: pl.BlockSpec(a.shape, lambda i: (0,) * a.ndim)
    rev = lambda w: pl.BlockSpec((R, w), lambda i: (nb - 1 - i, 0))
    prev8 = pl.BlockSpec((8, S5_N), lambda i: (jnp.maximum((nb - 1 - i) * (R // 8) - 1, 0), 0))
    small = [lre, lim, ldt, bre_t, bim_t, cre_w, cim_w, d_row, w_glu, b_glu]
    outs = [((tp, S5_W), rev(S5_W))] + [
        (s, pl.BlockSpec(s, lambda i: (0, 0))) for s in
        [(1, S5_N), (1, S5_N), (1, HEAD), (S5_H, S5_N), (S5_H, S5_N), (S5_W, S5_P), (S5_W, S5_P),
         (1, S5_W), (S5_W, S5_W), (1, S5_W)]]
    return _call(
        body, "s5_bwd", (nb,),
        [rev(S5_W), rev(S5_W), rev(S5_N), rev(S5_N), prev8, prev8] + [full(a) for a in small],
        [o[1] for o in outs], [jax.ShapeDtypeStruct(o[0], MM if n == 0 else F32) for n, o in enumerate(outs)],
        [pltpu.VMEM((S5_W, S5_N), MM)] * 4 + [
            pltpu.VMEM((8, 8, S5_N), F32), pltpu.VMEM((8, S5_N), F32), pltpu.VMEM((8, S5_N), F32),
            pltpu.VMEM((R, S5_N), F32), pltpu.VMEM((R, S5_N), F32),
            pltpu.VMEM((R + 8, S5_N), F32), pltpu.VMEM((R + 8, S5_N), F32)] + [pltpu.VMEM((S5_W, S5_N), F32)] * 4 + [
            pltpu.VMEM((8, S5_N), F32), pltpu.VMEM((8, S5_N), F32)],
        (dy_out, u, xr, xi, xr, xi, *small), jobs)


def _ret_fwd(q, k, v, dmat, zeta_b, xi_b, gam_b, jobs=()):
    tp = q.shape[0]
    C = dmat.shape[1]
    nc = tp // C

    def body(q_ref, k_ref, v_ref, dm_ref, ze_ref, xi_ref, ga_ref, o_ref, st_ref, s_ref):
        @pl.when(pl.program_id(0) == 0)
        def _():
            s_ref[...] = jnp.zeros_like(s_ref)

        for h in range(RET_H):
            sl = slice(h * HEAD, (h + 1) * HEAD)
            qh, kh, vh = q_ref[:, sl], k_ref[:, sl], v_ref[:, sl]
            sh = s_ref[h]
            st_ref[0, sl, :] = sh
            scores = _dot_nt(qh, kh) * dm_ref[h]
            o_ref[:, sl] = _dot(scores, vh) + _dot(qh, sh) * xi_ref[h]
            s_ref[h] = ga_ref[h] * sh + _dot_tn(kh.astype(F32) * ze_ref[h], vh)

    blk = pl.BlockSpec((C, RET_W), lambda c: (c, 0))
    cst = lambda a: pl.BlockSpec(a.shape, lambda c: (0, 0, 0))
    return _call(
        body, "ret_fwd", (nc,), [blk, blk, blk, cst(dmat), cst(zeta_b), cst(xi_b), cst(gam_b)],
        [blk, pl.BlockSpec((1, RET_W, HEAD), lambda c: (c, 0, 0))],
        [jax.ShapeDtypeStruct((tp, RET_W), F32), jax.ShapeDtypeStruct((nc, RET_W, HEAD), F32)],
        [pltpu.VMEM((RET_H, HEAD, HEAD), F32)], (q, k, v, dmat, zeta_b, xi_b, gam_b), jobs)


def _ret_bwd(q, k, v, do, states, cos2, sin2, dmat, zeta_b, xi_b, gam_b, jobs=()):
    tp = q.shape[0]
    C = dmat.shape[1]
    nc = tp // C

    def body(q_ref, k_ref, v_ref, do_ref, st_ref, cos_ref, sin_ref, dm_ref, ze_ref, xi_ref, ga_ref,
             dq_ref, dk_ref, dv_ref, ds_ref):
        @pl.when(pl.program_id(0) == 0)
        def _():
            ds_ref[...] = jnp.zeros_like(ds_ref)

        cos = cos_ref[...]
        sin = sin_ref[...]
        for h in range(RET_H):
            sl = slice(h * HEAD, (h + 1) * HEAD)
            qh, kh, vh = q_ref[:, sl], k_ref[:, sl], v_ref[:, sl]
            dmh = dm_ref[h]
            sh = st_ref[0, sl, :]
            dsn = ds_ref[h]
            doh = do_ref[:, sl]
            dox = doh * xi_ref[h]
            a = _dot_nt(qh, kh) * dmh
            dqk = _dot_nt(doh, vh) * dmh
            kz = kh.astype(F32) * ze_ref[h]
            dv_ref[:, sl] = (_dot_tn(a, doh) + _dot(kz, dsn)).astype(dv_ref.dtype)
            dqr = _dot(dqk, kh) + _dot_nt(dox, sh)
            dkr = _dot_tn(dqk, qh) + ze_ref[h] * _dot_nt(vh, dsn)
            ds_ref[h] = ga_ref[h] * dsn + _dot_tn(qh, dox)
            dq_ref[:, sl] = (dqr * cos - pltpu.roll(dqr, HEAD // 2, 1) * sin).astype(dq_ref.dtype)
            dk_ref[:, sl] = ((dkr * cos - pltpu.roll(dkr, HEAD // 2, 1) * sin) * (HEAD ** -0.5)).astype(dk_ref.dtype)

    blk = pl.BlockSpec((C, RET_W), lambda c: (nc - 1 - c, 0))
    tab = pl.BlockSpec((C, HEAD), lambda c: (nc - 1 - c, 0))
    cst = lambda a: pl.BlockSpec(a.shape, lambda c: (0, 0, 0))
    return _call(
        body, "ret_bwd", (nc,),
        [blk, blk, blk, blk, pl.BlockSpec((1, RET_W, HEAD), lambda c: (nc - 1 - c, 0, 0)), tab, tab,
         cst(dmat), cst(zeta_b), cst(xi_b), cst(gam_b)],
        [blk, blk, blk], [jax.ShapeDtypeStruct((tp, RET_W), MM)] * 3, [pltpu.VMEM((RET_H, HEAD, HEAD), F32)],
        (q, k, v, do, states, cos2, sin2, dmat, zeta_b, xi_b, gam_b), jobs)


def _gn_gate(o, gate, gn_g, gn_b):
    xhat, rstd = _ln_fwd(o, GN_EPS)
    on = xhat * gn_g + gn_b
    s = jax.nn.sigmoid(gate)
    return gate * s * on, xhat, rstd, on, s


def _post_up(o, gate, ys5, xhat0, gn_g, gn_b, li_g, li_b, l1_g, l1_b, w_out, w_up, jobs=()):
    tp = o.shape[0]
    R = ROW_BLK

    def body(o_ref, g_ref, ys_ref, xh0_ref, gng, gnb, lig, lib, l1g, l1b, wo_ref, wu_ref,
             ycat_ref, xh1_ref, rstd1_ref, h1b_ref, pre_ref):
        ycat_ref[:, 0:S5_W] = ys_ref[...].astype(ycat_ref.dtype)
        for h in range(RET_H):
            sl = slice(h * HEAD, (h + 1) * HEAD)
            yret = _gn_gate(o_ref[:, sl], g_ref[:, sl], gng[:, sl], gnb[:, sl])[0]
            ycat_ref[:, S5_W + h * HEAD:S5_W + (h + 1) * HEAD] = yret.astype(ycat_ref.dtype)
        mixed = _dot(ycat_ref[...], wo_ref[...])
        h0 = xh0_ref[...] * lig[...] + lib[...]
        xh1, rstd1 = _ln_fwd(ALPHA * h0 + mixed, LN_EPS)
        xh1_ref[...] = xh1
        rstd1_ref[...] = rstd1
        h1b = (xh1 * l1g[...] + l1b[...]).astype(MM)
        h1b_ref[...] = h1b
        for d in range(N_DEV):
            pre_ref[:, d * FF_BLK:(d + 1) * FF_BLK] = jnp.maximum(_dot(h1b, wu_ref[d]), 0.0)

    row = lambda w: pl.BlockSpec((R, w), lambda i: (i, 0))
    full = lambda a: pl.BlockSpec(a.shape, lambda i: (0,) * a.ndim)
    vecs = [gn_g, gn_b, li_g, li_b, l1_g, l1_b]
    outs = [(row(D_MODEL), jax.ShapeDtypeStruct((tp, D_MODEL), MM)), (row(D_MODEL), jax.ShapeDtypeStruct((tp, D_MODEL), F32)),
            (row(1), jax.ShapeDtypeStruct((tp, 1), F32)), (row(D_MODEL), jax.ShapeDtypeStruct((tp, D_MODEL), MM)),
            (row(D_FF), jax.ShapeDtypeStruct((tp, D_FF), F32))]
    return _call(
        body, "post_up", (tp // R,),
        [row(RET_W), row(RET_W), row(S5_W), row(D_MODEL)] + [full(a) for a in vecs] + [_VMEM, _VMEM],
        [o[0] for o in outs], [o[1] for o in outs], [], (o, gate, ys5, xhat0, *vecs, w_out, w_up), jobs)


def _post_down(pre, xhat1, tgt, l1_g, l1_b, l2_g, l2_b, w_down):
    tp = pre.shape[0]
    seq = tgt.shape[0]
    R = ROW_BLK

    def body(pre_ref, xh1_ref, ta, tb, tc, l1g, l1b, l2g, l2b, wd_ref,
             dr2_ref, dffb_ref, loss_ref, dl2g_ref, dl2b_ref, tgt_ref):
        i = pl.program_id(0)

        @pl.when(i == 0)
        def _():
            for ref in (loss_ref, dl2g_ref, dl2b_ref):
                ref[...] = jnp.zeros_like(ref)

        tgt_ref[0:CHUNK, :] = ta[...]
        tgt_ref[CHUNK:2 * CHUNK, :] = tb[...]
        tgt_ref[2 * CHUNK:3 * CHUNK, :] = tc[...]
        ff = jnp.zeros((R, D_MODEL), F32)
        for d in range(N_DEV):
            pre = pre_ref[:, d * FF_BLK:(d + 1) * FF_BLK]
            ff = ff + _dot(pre * pre, wd_ref[d * FF_BLK:(d + 1) * FF_BLK, :])
        h1 = xh1_ref[...] * l1g[...] + l1b[...]
        xh2, rstd2 = _ln_fwd(ALPHA * h1 + ff, LN_EPS)
        h2 = xh2 * l2g[...] + l2b[...]
        valid = (i * R + lax.broadcasted_iota(jnp.int32, (R, 1), 0)) >= CHUNK
        err = jnp.where(valid, h2 - tgt_ref[...], 0.0)
        loss_ref[...] += 0.5 * jnp.sum(err * err) / D_MODEL
        dh2 = err * (1.0 / D_MODEL)
        dl2g_ref[...] += _colsum(dh2 * xh2)
        dl2b_ref[...] += _colsum(dh2)
        dr2 = _ln_bwd(dh2 * l2g[...], xh2, rstd2)
        dr2_ref[...] = dr2
        dffb_ref[...] = dr2.astype(MM)

    row = lambda w: pl.BlockSpec((R, w), lambda i: (i, 0))
    full = lambda a: pl.BlockSpec(a.shape, lambda i: (0,) * a.ndim)
    vecs = [l1_g, l1_b, l2_g, l2_b]
    acc = lambda s: (pl.BlockSpec(s, lambda i: (0, 0)), jax.ShapeDtypeStruct(s, F32))
    outs = [(row(D_MODEL), jax.ShapeDtypeStruct((tp, D_MODEL), F32)), (row(D_MODEL), jax.ShapeDtypeStruct((tp, D_MODEL), MM)),
            acc((8, HEAD)), acc((1, D_MODEL)), acc((1, D_MODEL))]
    return pl.pallas_call(
        body, name="post_down", grid=(tp // R,),
        in_specs=[row(D_FF), row(D_MODEL)] + _shift3(seq // CHUNK) + [full(a) for a in vecs] + [_VMEM],
        out_specs=[o[0] for o in outs], out_shape=[o[1] for o in outs],
        scratch_shapes=[pltpu.VMEM((R, D_MODEL), F32)],
        compiler_params=_params(("arbitrary",)),
    )(pre, xhat1, tgt, tgt, tgt, *vecs, w_down)


def _mlp_bwd(h1b, dffb, pre, w_up, w_down):
    tp = h1b.shape[0]
    R = MLP_ROWS if tp % MLP_ROWS == 0 else ROW_BLK
    nr = tp // R

    def body(h_ref, df_ref, pre_ref, wu_ref, wd_ref, gup_ref, gdn_ref, dh1_ref, aup, adn):
        d = pl.program_id(0)
        r = pl.program_id(1)

        @pl.when(r == 0)
        def _():
            aup[...] = jnp.zeros_like(aup)
            adn[...] = jnp.zeros_like(adn)

        h = h_ref[...]
        df = df_ref[...]
        wu = wu_ref[0]
        wd = wd_ref[0]
        pre = pre_ref[...]
        dpre = (_dot_nt(df, wd) * (2.0 * pre)).astype(MM)

        aup[...] += _dot_tn(h, dpre)
        adn[...] += _dot_tn(pre * pre, df)
        contrib = _dot_nt(dpre, wu)
        rows = pl.ds(pl.multiple_of(r * R, 64), R)

        @pl.when(d == 0)
        def _():
            dh1_ref[rows, :] = contrib

        @pl.when(d > 0)
        def _():
            dh1_ref[rows, :] += contrib

        @pl.when(r == nr - 1)
        def _():
            gup_ref[0] = aup[...].astype(gup_ref.dtype)
            gdn_ref[0] = adn[...].astype(gdn_ref.dtype)

    return pl.pallas_call(
        body, name="mlp_bwd", grid=(N_DEV, nr),
        in_specs=[pl.BlockSpec((R, D_MODEL), lambda d, r: (r, 0)), pl.BlockSpec((R, D_MODEL), lambda d, r: (r, 0)),
                  pl.BlockSpec((R, FF_BLK), lambda d, r: (r, d)),
                  pl.BlockSpec((1, D_MODEL, FF_BLK), lambda d, r: (d, 0, 0)),
                  pl.BlockSpec((1, FF_BLK, D_MODEL), lambda d, r: (d, 0, 0))],
        out_specs=[pl.BlockSpec((1, D_MODEL, FF_BLK), lambda d, r: (d, 0, 0)),
                   pl.BlockSpec((1, FF_BLK, D_MODEL), lambda d, r: (d, 0, 0)), _VMEM],
        out_shape=[jax.ShapeDtypeStruct((N_DEV, D_MODEL, FF_BLK), MM), jax.ShapeDtypeStruct((N_DEV, FF_BLK, D_MODEL), MM),
                   jax.ShapeDtypeStruct((tp, D_MODEL), F32)],
        scratch_shapes=[pltpu.VMEM((D_MODEL, FF_BLK), F32), pltpu.VMEM((FF_BLK, D_MODEL), F32)],
        compiler_params=_params(("arbitrary", "arbitrary")),
    )(h1b, dffb, pre, w_up, w_down.reshape(N_DEV, FF_BLK, D_MODEL))


def _post_bwd(dh1m, dr2, xhat1, rstd1, ycat, o, gate, gn_g, gn_b, l1_g, w_out, jobs=()):
    tp = o.shape[0]
    R = ROW_BLK
    nb = tp // R

    def body(dm_ref, dr2_ref, xh1_ref, rs1_ref, yc_ref, o_ref, g_ref, gng, gnb, l1g, wo_ref,
             do_ref, dg_ref, dys_ref, dh0_ref, gwo_ref, dl1g_ref, dl1b_ref, dgng_ref, dgnb_ref, awo):
        i = pl.program_id(0)

        @pl.when(i == 0)
        def _():
            for ref in (awo, dl1g_ref, dl1b_ref, dgng_ref, dgnb_ref):
                ref[...] = jnp.zeros_like(ref)

        dh1 = dm_ref[...] + ALPHA * dr2_ref[...]
        xh1 = xh1_ref[...]
        dl1g_ref[...] += _colsum(dh1 * xh1)
        dl1b_ref[...] += _colsum(dh1)
        dr1 = _ln_bwd(dh1 * l1g[...], xh1, rs1_ref[...])
        dh0_ref[...] = ALPHA * dr1
        dmix = dr1.astype(MM)
        awo[...] += _dot_tn(yc_ref[...], dmix)
        dyc = _dot_nt(dmix, wo_ref[...])
        dys_ref[...] = dyc[:, 0:S5_W]
        for h in range(RET_H):
            sl = slice(h * HEAD, (h + 1) * HEAD)
            gt = g_ref[:, sl]
            _, xhat, rstd, on, s = _gn_gate(o_ref[:, sl], gt, gng[:, sl], gnb[:, sl])
            dyr = dyc[:, S5_W + h * HEAD:S5_W + (h + 1) * HEAD]
            dg_ref[:, sl] = (dyr * on * (s * (1.0 + gt * (1.0 - s)))).astype(dg_ref.dtype)
            don = dyr * gt * s
            dgng_ref[:, sl] += _colsum(don * xhat)
            dgnb_ref[:, sl] += _colsum(don)
            do_ref[:, sl] = _ln_bwd(don * gng[:, sl], xhat, rstd)

        @pl.when(i == nb - 1)
        def _():
            gwo_ref[...] = awo[...].astype(gwo_ref.dtype)

    row = lambda w: pl.BlockSpec((R, w), lambda i: (i, 0))
    full = lambda a: pl.BlockSpec(a.shape, lambda i: (0,) * a.ndim)
    acc = lambda s, dt=F32: (pl.BlockSpec(s, lambda i: (0, 0)), jax.ShapeDtypeStruct(s, dt))
    outs = [(row(RET_W), jax.ShapeDtypeStruct((tp, RET_W), F32)), (row(RET_W), jax.ShapeDtypeStruct((tp, RET_W), MM)),
            (row(S5_W), jax.ShapeDtypeStruct((tp, S5_W), F32)), (row(D_MODEL), jax.ShapeDtypeStruct((tp, D_MODEL), F32)),
            acc((D_MODEL, D_MODEL), MM), acc((1, D_MODEL)), acc((1, D_MODEL)), acc((1, RET_W)), acc((1, RET_W))]
    return _call(
        body, "post_bwd", (nb,),
        [row(D_MODEL), row(D_MODEL), row(D_MODEL), row(1), row(D_MODEL), row(RET_W), row(RET_W),
         full(gn_g), full(gn_b), full(l1_g), _VMEM],
        [o[0] for o in outs], [o[1] for o in outs],
        [pltpu.VMEM((D_MODEL, D_MODEL), F32)],
        (dh1m, dr2, xhat1, rstd1, ycat, o, gate, gn_g, gn_b, l1_g, w_out), jobs)


def _in_bwd(du, dq, dk, dv, dg, dh0r, xhat0, rstd0, li_g, li_b, w_int, jobs=()):
    tp = du.shape[0]
    R = PROJ_ROWS if tp % PROJ_ROWS == 0 else ROW_BLK
    nb = tp // R
    segs = [(0, S5_W)] + [(S5_W + n * RET_W, S5_W + (n + 1) * RET_W) for n in range(4)]

    def body(du_ref, dq_ref, dk_ref, dv_ref, dg_ref, dh0r_ref, xh_ref, rs_ref, lig, lib, w_ref,
             gx_ref, dmeta_ref, gw_ref, dlg_ref, dlb_ref, aw, stage, out_sems):
        i = pl.program_id(0)
        slot = i % 2

        def to_gx(step_slot, first):
            if first:
                return pltpu.make_async_copy(stage.at[0, CHUNK:R, :], gx_ref.at[0:R - CHUNK, :], out_sems.at[0])
            return pltpu.make_async_copy(stage.at[step_slot], gx_ref.at[pl.ds(i * R - CHUNK, R), :], out_sems.at[step_slot])

        @pl.when(i == 0)
        def _():
            for ref in (aw, dlg_ref, dlb_ref):
                ref[...] = jnp.zeros_like(ref)

        @pl.when(i >= 3)
        def _():
            to_gx(slot, False).wait()

        valid = (i * R + lax.broadcasted_iota(jnp.int32, (R, 1), 0)) >= PAD
        xh = xh_ref[...]
        hb = (xh * lig[...] + lib[...]).astype(MM)
        dh0 = dh0r_ref[...]
        for (lo, hi), ref in zip(segs, (du_ref, dq_ref, dk_ref, dv_ref, dg_ref)):
            dseg = jnp.where(valid, ref[...], 0.0).astype(MM)
            dh0 = dh0 + _dot(dseg, w_ref[lo:hi, :])
            aw[lo:hi, :] += _dot_tn(dseg, hb)
        dlg_ref[...] += _colsum(dh0 * xh)
        dlb_ref[...] += _colsum(dh0)
        draw = _ln_bwd(dh0 * lig[...], xh, rs_ref[...])
        stage[slot] = draw

        @pl.when(i == 0)
        def _():
            dmeta_ref[...] = draw[PAD:CHUNK, :]
            first = to_gx(0, True)
            first.start()
            first.wait()

        @pl.when(i > 0)
        def _():
            to_gx(slot, False).start()

        @pl.when(i == nb - 1)
        def _():
            gw_ref[...] = aw[...].astype(gw_ref.dtype)
            for back in (1, 0):
                if nb - 1 - back >= 1:
                    to_gx((nb - 1 - back) % 2, False).wait()

    row = lambda w: pl.BlockSpec((R, w), lambda i: (i, 0))
    full = lambda a: pl.BlockSpec(a.shape, lambda i: (0,) * a.ndim)
    acc = lambda s, dt=F32: (pl.BlockSpec(s, lambda i: (0, 0)), jax.ShapeDtypeStruct(s, dt))
    outs = [(_ANY, jax.ShapeDtypeStruct((tp - CHUNK, D_MODEL), F32)), acc((N_META, D_MODEL)), acc((PROJ_W, D_MODEL), MM),
            acc((1, D_MODEL)), acc((1, D_MODEL))]
    return _call(
        body, "in_bwd", (nb,),
        [row(S5_W), row(RET_W), row(RET_W), row(RET_W), row(RET_W), row(D_MODEL), row(D_MODEL), row(1),
         full(li_g), full(li_b), _VMEM],
        [o[0] for o in outs], [o[1] for o in outs],
        [pltpu.VMEM((PROJ_W, D_MODEL), F32), pltpu.VMEM((2, R, D_MODEL), F32), pltpu.SemaphoreType.DMA((2,))],
        (du, dq, dk, dv, dg, dh0r, xhat0, rstd0, li_g, li_b, w_int), jobs)


def _place():
    return lax.axis_index("x"), lax.axis_index("y"), lax.axis_index("c")


def _dma_sems(n):
    return pltpu.SemaphoreType.DMA((n,))


def _job_gather(shard):
    def parts(ins, outs, sems):
        (src,), (out,), (send_sems, recv_sems, local_sem) = ins, outs, sems
        x, y, c = _place()
        north = c == 1
        me, sib = (x, y, c), (x, y, 1 - c)
        xn, yn, dg = (1 - x, y, c), (x, 1 - y, c), (1 - x, 1 - y, c)
        relay_from = (jnp.where(north, 1 - x, x), jnp.where(north, y, 1 - y), c)
        relay_to = (jnp.where(north, x, 1 - x), jnp.where(north, 1 - y, y), c)

        def slot(dev):
            return out.at[4 * dev[0] + 2 * dev[1] + dev[2]]

        def copy(k, block, to, from_input=False):
            return pltpu.make_async_remote_copy(
                src_ref=src if from_input else slot(block), dst_ref=slot(block),
                send_sem=send_sems.at[k], recv_sem=recv_sems.at[k], device_id=to, device_id_type=_MESH)

        mine = lambda: pltpu.make_async_copy(src, slot(me), local_sem.at[0])
        first = lambda: [copy(0, me, sib, True), copy(1, me, xn, True), copy(2, me, yn, True)]
        relayed = lambda: [copy(3, relay_from, relay_to), copy(4, xn, sib), copy(5, yn, sib)]
        return me, sib, xn, yn, dg, copy, mine, first, relayed

    def start(ins, outs, sems):
        mine, first = parts(ins, outs, sems)[6:8]
        mine().start()
        for cp in first():
            cp.start()

    def relay(ins, outs, sems):
        me, sib, xn, yn, dg, copy, mine, first, relayed = parts(ins, outs, sems)
        copy(1, xn, me).wait_recv()
        copy(2, yn, me).wait_recv()
        for cp in relayed():
            cp.start()

    def finish(ins, outs, sems):
        me, sib, xn, yn, dg, copy, mine, first, relayed = parts(ins, outs, sems)
        other = 1 - me[2]
        copy(3, dg, me).wait_recv()
        last = copy(6, dg, sib)
        last.start()
        copy(0, sib, me).wait_recv()
        for k, chip in ((4, xn), (5, yn), (6, dg)):
            copy(k, (chip[0], chip[1], other), me).wait_recv()
        for cp in first() + relayed() + [last]:
            cp.wait_send()
        mine().wait()

    return dict(ins=[shard], outs=[jax.ShapeDtypeStruct((N_DEV,) + shard.shape, shard.dtype)],
                sems=[_dma_sems(7), _dma_sems(7), _dma_sems(1)], start=start, middle=relay, finish=finish)


def _job_pair(g):
    def copies(ins, outs, sems):
        x, y, c = _place()
        return [pltpu.make_async_remote_copy(
            src_ref=ins[0].at[2 * j + (1 - c)], dst_ref=outs[0].at[j], send_sem=sems[0].at[j], recv_sem=sems[1].at[j],
            device_id=(x, y, 1 - c), device_id_type=_MESH) for j in range(4)]

    def start(ins, outs, sems):
        for cp in copies(ins, outs, sems):
            cp.start()

    def finish(ins, outs, sems):
        for cp in copies(ins, outs, sems):
            cp.wait()

    return dict(ins=[g], outs=[jax.ShapeDtypeStruct((4,) + g.shape[1:], g.dtype)], sems=[_dma_sems(4), _dma_sems(4)],
                start=start, finish=finish)


def _job_chips(p):
    def copies(ins, outs, sems):
        x, y, c = _place()
        chips = [(1 - x, y), (x, 1 - y), (1 - x, 1 - y)]
        return [pltpu.make_async_remote_copy(
            src_ref=ins[0].at[2 * chip[0] + chip[1]], dst_ref=outs[0].at[k], send_sem=sems[0].at[k],
            recv_sem=sems[1].at[k], device_id=(*chip, c), device_id_type=_MESH) for k, chip in enumerate(chips)]

    def start(ins, outs, sems):
        for cp in copies(ins, outs, sems):
            cp.start()

    def finish(ins, outs, sems):
        for cp in copies(ins, outs, sems):
            cp.wait()

    return dict(ins=[p], outs=[jax.ShapeDtypeStruct((3,) + p.shape[1:], p.dtype)], sems=[_dma_sems(3), _dma_sems(3)],
                start=start, finish=finish)


def _split_job_refs(jobs, ins, outs, sems):
    res, a, b, c = [], 0, 0, 0
    for job in jobs:
        na, nb, nc = len(job["ins"]), len(job["outs"]), len(job["sems"])
        res.append((ins[a:a + na], outs[b:b + nb], sems[c:c + nc]))
        a, b, c = a + na, b + nb, c + nc
    return res


def _call(body, name, grid, in_specs, out_specs, out_shape, scratch, args, jobs=(), prefetch=None, early=0):
    jobs = list(jobs)
    n_in, n_out, n_scr = len(in_specs), len(out_specs), len(scratch)
    j_in = [a for job in jobs for a in job["ins"]]
    j_out = [o for job in jobs for o in job["outs"]]
    j_scr = [s for job in jobs for s in job["sems"]]
    nsteps = grid[0]
    n_pre = 0 if prefetch is None else 1

    def wrapped(*refs):
        pre, refs = refs[:n_pre], refs[n_pre:]
        ins, jins = refs[:n_in], refs[n_in:n_in + len(j_in)]
        refs = refs[n_in + len(j_in):]
        outs, jouts = refs[:n_out], refs[n_out:n_out + len(j_out)]
        refs = refs[n_out + len(j_out):]
        scr, jscr = refs[:n_scr], refs[n_scr:]
        per_job = _split_job_refs(jobs, jins, jouts, jscr)

        def middle():
            for job, r in zip(jobs, per_job):
                if "middle" in job:
                    job["middle"](*r)

        @pl.when(pl.program_id(0) == 0)
        def _():
            for job, r in zip(jobs, per_job):
                job["start"](*r)

        if nsteps >= 3:
            pl.when(pl.program_id(0) == nsteps // 2)(middle)

        if early:
            @pl.when(pl.program_id(0) == nsteps - 1)
            def _():
                for job, r in zip(jobs[:early], per_job[:early]):
                    job["finish"](*r)

        body(*pre, *ins, *outs, *scr, *[o for r in per_job[:early] for o in r[1]])

        @pl.when(pl.program_id(0) == nsteps - 1)
        def _():
            if nsteps < 3:
                middle()
            for job, r in zip(jobs[early:], per_job[early:]):
                job["finish"](*r)

    specs = dict(in_specs=list(in_specs) + [_ANY] * len(j_in), out_specs=list(out_specs) + [_ANY] * len(j_out),
                 scratch_shapes=list(scratch) + j_scr)
    if n_pre:
        specs = dict(grid_spec=pltpu.PrefetchScalarGridSpec(num_scalar_prefetch=1, grid=grid, **specs))
    else:
        specs["grid"] = grid
    res = pl.pallas_call(
        wrapped if jobs else body, name=name, out_shape=list(out_shape) + j_out,
        compiler_params=_params(("arbitrary",) * len(grid)), **specs,
    )(*([prefetch] if n_pre else []), *args, *j_in)
    return list(res[:n_out]), list(res[n_out:])


def _exchange(jobs, name):
    j_in = [a for job in jobs for a in job["ins"]]
    j_out = [o for job in jobs for o in job["outs"]]
    j_scr = [s for job in jobs for s in job["sems"]]

    def body(*refs):
        per_job = _split_job_refs(jobs, refs[:len(j_in)], refs[len(j_in):len(j_in) + len(j_out)],
                                  refs[len(j_in) + len(j_out):])
        for phase in ("start", "middle", "finish"):
            for job, r in zip(jobs, per_job):
                if phase in job:
                    job[phase](*r)

    return pl.pallas_call(body, name=name, out_shape=j_out, in_specs=[_ANY] * len(j_in), out_specs=[_ANY] * len(j_out),
                          scratch_shapes=j_scr)(*j_in)


def _pair_sum(gs, r1s, c_arr, name):
    n = len(gs)

    def body(c_ref, *refs):
        for a in range(n):
            refs[2 * n + a][...] = (refs[a][...].astype(F32) + refs[n + a][...].astype(F32)).astype(refs[2 * n + a].dtype)

    def blk(g, own):
        s = g.shape[1:]
        if own:
            return pl.BlockSpec((1,) + s, lambda j, c_ref: (2 * j + c_ref[0],) + (0,) * len(s))
        return pl.BlockSpec((1,) + s, lambda j, c_ref: (j,) + (0,) * len(s))

    return pl.pallas_call(
        body, name=name,
        grid_spec=pltpu.PrefetchScalarGridSpec(
            num_scalar_prefetch=1, grid=(4,),
            in_specs=[blk(g, True) for g in gs] + [blk(g, False) for g in gs],
            out_specs=[blk(g, False) for g in gs]),
        out_shape=[jax.ShapeDtypeStruct((4,) + g.shape[1:], g.dtype) for g in gs],
        compiler_params=_params(("arbitrary",)),
    )(c_arr, *gs, *r1s)


def _adamw_math(w, g, m, v):
    m = ADAM_B1 * m + (1.0 - ADAM_B1) * g
    v = ADAM_B2 * v + (1.0 - ADAM_B2) * (g * g)
    m_hat = m / (1.0 - ADAM_B1 ** ADAM_STEP)
    v_hat = v / (1.0 - ADAM_B2 ** ADAM_STEP)
    return -ADAM_LR * (m_hat / (jnp.sqrt(v_hat) + ADAM_EPS) + ADAM_WD * w), m, v


def _view(name, a):
    return jnp.swapaxes(a, -1, -2) if name in ("w_in", "s5_b_re", "s5_b_im") else a


def _adamw_shards(items, name, steps, chip, jobs=()):
    n = len(items)

    def body(chip_ref, *refs):
        for a in range(n):
            p_ref, r_ref, w_ref, m_ref, v_ref = refs[5 * a:5 * a + 5]
            g = ((p_ref[0].astype(F32) + r_ref[0].astype(F32)) + r_ref[1].astype(F32)) + r_ref[2].astype(F32)
            outs = refs[5 * n + 4 * a:5 * n + 4 * a + 4]
            outs[0][...] = g
            outs[1][...], outs[2][...], outs[3][...] = _adamw_math(w_ref[...], g, m_ref[...], v_ref[...])

    in_specs, out_specs, out_shape, flat = [], [], [], []
    for p, r, w, m, v in items:
        rows, cols = w.shape
        rb = rows // steps
        in_specs += [pl.BlockSpec((1, rb, cols), lambda i, c: (c[0], i, 0)), pl.BlockSpec((3, rb, cols), lambda i, c: (0, i, 0))]
        wblk = pl.BlockSpec((rb, cols), lambda i, c: (i, 0))
        in_specs += [wblk] * 3
        out_specs += [wblk] * 4
        out_shape += [jax.ShapeDtypeStruct(w.shape, F32)] * 4
        flat += [p, r, w, m, v]
    return _call(body, name, (steps,), in_specs, out_specs, out_shape, [], flat, jobs, prefetch=chip)


def _sum_devices(gathered, name):
    def body(gs_ref, g_ref):
        g = gs_ref[0]
        for s in range(1, N_DEV):
            g = g + gs_ref[s]
        g_ref[...] = g

    return pl.pallas_call(body, name=name, out_shape=jax.ShapeDtypeStruct(gathered.shape[1:], F32),
                          in_specs=[_VMEM], out_specs=_VMEM, compiler_params=_params())(gathered)


def _adamw_native(items, name):
    n = len(items)

    def body(*refs):
        for a in range(n):
            g, w, m, v = (refs[4 * a + t][...] for t in range(4))
            refs[4 * n + 3 * a][...], refs[4 * n + 3 * a + 1][...], refs[4 * n + 3 * a + 2][...] = _adamw_math(w, g, m, v)

    return pl.pallas_call(
        body, name=name, out_shape=[jax.ShapeDtypeStruct(it[1].shape, F32) for it in items for _ in range(3)],
        in_specs=[_VMEM] * (4 * n), out_specs=[_VMEM] * (3 * n), compiler_params=_params(),
    )(*[t for it in items for t in it])


SMALL = ["ln_in_g", "ln_in_b", "s5_lambda_re", "s5_lambda_im", "s5_log_dt", "s5_b_re", "s5_b_im", "s5_c_re", "s5_c_im",
         "s5_d", "s5_b_glu", "ret_gn_g", "ret_gn_b", "ln1_g", "ln1_b", "ln2_g", "ln2_b"]
LATE = ["ln_in_g", "ln_in_b", "meta_tokens"]
EARLY = [n for n in SMALL if n not in LATE] + ["s5_w_glu", "loss"]
LANE = 128


def _pack(arrs):
    parts = []
    for a in arrs:
        f = a.reshape(-1)
        parts.append(jnp.pad(f, (0, (-f.shape[0]) % LANE)))
    flat = jnp.concatenate(parts)
    rows = -(-flat.shape[0] // LANE)
    flat = jnp.pad(flat, (0, (-rows % 8) * LANE + rows * LANE - flat.shape[0]))
    return flat.reshape(-1, LANE)


def _unpack(packed, shapes):
    flat = packed.reshape(-1)
    out, off = [], 0
    for s in shapes:
        n = math.prod(s)
        out.append(flat[off:off + n].reshape(s))
        off += n + (-n) % LANE
    return out


def _rope_tables(tp):
    inv_freq = 1.0 / (ROPE_BASE ** (jnp.arange(0, HEAD, 2, dtype=F32) / HEAD))
    blk = (jnp.arange(tp // ROW_BLK, dtype=F32) * ROW_BLK)[:, None, None] * inv_freq
    off = (jnp.arange(ROW_BLK, dtype=F32) - float(PAD))[None, :, None] * inv_freq
    cos = (jnp.cos(blk) * jnp.cos(off) - jnp.sin(blk) * jnp.sin(off)).reshape(tp, HEAD // 2)
    sin = (jnp.sin(blk) * jnp.cos(off) + jnp.cos(blk) * jnp.sin(off)).reshape(tp, HEAD // 2)
    return jnp.concatenate([cos, cos], axis=1), jnp.concatenate([-sin, sin], axis=1)


RET_CHUNK = ROW_BLK


def _decay_tables():
    log_gamma = jnp.log1p(-jnp.exp2(-5.0 - jnp.arange(RET_H, dtype=F32)))
    idx = jnp.arange(RET_CHUNK, dtype=F32)
    diff = idx[:, None] - idx[None, :]
    dmat = jnp.where(diff[None] >= 0, jnp.exp(jnp.maximum(diff, 0.0)[None] * log_gamma[:, None, None]), 0.0)
    zeta = jnp.exp((RET_CHUNK - 1.0 - idx)[None] * log_gamma[:, None])
    xi = jnp.exp((idx + 1.0)[None] * log_gamma[:, None])
    gam = jnp.exp(RET_CHUNK * log_gamma)
    wide = lambda t: jnp.broadcast_to(t[:, :, None], (RET_H, RET_CHUNK, HEAD))
    return dmat, wide(zeta), wide(xi), jnp.broadcast_to(gam[:, None, None], (RET_H, HEAD, HEAD))


def _local_step(x2d, tgt, meta, w_int, w_out, w_up, w_down, w_glu, sp, distributed):
    tp = x2d.shape[0] + CHUNK
    row = lambda a: a.reshape(1, -1)
    cos2, sin2 = _rope_tables(tp)
    dmat, zeta_b, xi_b, gam_b = _decay_tables()
    li_g, li_b = row(sp["ln_in_g"]), row(sp["ln_in_b"])
    l1_g, l1_b, l2_g, l2_b = row(sp["ln1_g"]), row(sp["ln1_b"]), row(sp["ln2_g"]), row(sp["ln2_b"])
    gn_g, gn_b = row(sp["ret_gn_g"]), row(sp["ret_gn_b"])
    lre, lim = row(sp["s5_lambda_re"]), row(sp["s5_lambda_im"])
    ldt = row(jnp.repeat(sp["s5_log_dt"].reshape(-1), S5_P))
    to_t = lambda b: b.reshape(S5_G, S5_P, S5_H).transpose(2, 0, 1).reshape(S5_H, S5_N)
    bre_t, bim_t = to_t(sp["s5_b_re"]), to_t(sp["s5_b_im"])
    to_w = lambda c: jnp.tile(c.reshape(S5_W, S5_P), (1, 2))
    cre_w, cim_w = to_w(sp["s5_c_re"]), to_w(sp["s5_c_im"])

    jobs = (lambda *j: list(j)) if distributed else (lambda *j: [])
    c_arr = jnp.reshape(lax.axis_index("c"), (1,)).astype(jnp.int32) if distributed else None
    (xhat0, rstd0), bg = _ln_in(x2d, meta, jobs(*([_job_gather(w_int), _job_gather(w_glu)] if distributed else [])),
                                gather_meta=distributed)
    if distributed:
        w_int, w_glu = bg[1].reshape(PROJ_W, D_MODEL), bg[2].reshape(S5_W, S5_W)
    s5_small = (lre, lim, ldt, bre_t, bim_t, cre_w, cim_w, row(sp["s5_d"]), w_glu, row(sp["s5_b_glu"]))
    (u, q, k, v, gate), bg = _in_proj(xhat0, li_g, li_b, w_int, cos2, sin2,
                                      jobs(_job_gather(w_out) if distributed else None))
    if distributed:
        w_out = bg[0].reshape(D_MODEL, D_MODEL)
    (ys5, xr, xi), bg = _s5_fwd(u, *s5_small, jobs=jobs(_job_gather(w_up) if distributed else None))
    if distributed:
        w_up = bg[0]
    (o, states), _ = _ret_fwd(q, k, v, dmat, zeta_b, xi_b, gam_b)
    (ycat, xhat1, rstd1, h1b, pre), bg = _post_up(o, gate, ys5, xhat0, gn_g, gn_b, li_g, li_b, l1_g, l1_b, w_out, w_up,
                                                  jobs(_job_gather(w_down) if distributed else None))
    if distributed:
        w_down = bg[0].reshape(D_FF, D_MODEL)
    dr2, dffb, loss8, dl2g, dl2b = _post_down(pre, xhat1, tgt, l1_g, l1_b, l2_g, l2_b, w_down)
    g_up, g_down, dh1m = _mlp_bwd(h1b, dffb, pre, w_up, w_down)
    (do, dgate, dys5, dh0r, g_out, dl1g, dl1b, dgng, dgnb), bg = _post_bwd(
        dh1m, dr2, xhat1, rstd1, ycat, o, gate, gn_g, gn_b, l1_g, w_out,
        jobs(*([_job_pair(g_up), _job_pair(g_down)] if distributed else [])))
    g_out = g_out.reshape(N_DEV, D_MODEL // N_DEV, D_MODEL)
    if distributed:
        p_up, p_down = _pair_sum([g_up, g_down], bg, c_arr, "pair_sum_mlp")
    (du, dlre, dlim, dldt, dbre_t, dbim_t, dcre, dcim, dd, dwglu, dbglu), bg = _s5_bwd(
        dys5, u, xr, xi, *s5_small,
        jobs=jobs(*([_job_chips(p_up), _job_chips(p_down), _job_pair(g_out)] if distributed else [])))
    if distributed:
        r_up, r_down = bg[0], bg[1]
        (p_out,) = _pair_sum([g_out], bg[2:], c_arr, "pair_sum_out")
    from_t = lambda t: t.reshape(S5_H, S5_G, S5_P).transpose(1, 0, 2)
    small = {
        "s5_lambda_re": dlre, "s5_lambda_im": dlim, "s5_log_dt": dldt[:, :S5_G],
        "s5_b_re": from_t(dbre_t), "s5_b_im": from_t(dbim_t), "s5_c_re": dcre, "s5_c_im": dcim, "s5_d": dd,
        "s5_b_glu": dbglu, "ret_gn_g": dgng, "ret_gn_b": dgnb, "ln1_g": dl1g, "ln1_b": dl1b, "ln2_g": dl2g, "ln2_b": dl2b,
        "s5_w_glu": dwglu, "loss": loss8[0:1, 0:1]}
    early_pack = _pack([small[n] for n in EARLY])
    (dq, dk, dv), bg = _ret_bwd(q, k, v, do, states, cos2, sin2, dmat, zeta_b, xi_b, gam_b,
                                jobs(*([_job_chips(p_out), _job_gather(early_pack)] if distributed else [])))
    (grad_x, dmeta, g_int, dlig, dlib), _ = _in_bwd(du, dq, dk, dv, dgate, dh0r, xhat0, rstd0, li_g, li_b, w_int)
    small.update(ln_in_g=dlig, ln_in_b=dlib, meta_tokens=dmeta)
    g_int = g_int.reshape(N_DEV, PROJ_W // N_DEV, D_MODEL)
    if distributed:
        (r1_in,) = _exchange([_job_pair(g_int)], "exchange_pair_in")
        (p_in,) = _pair_sum([g_int], [r1_in], c_arr, "pair_sum_in")
        big = dict(chip_sums=[p_in, p_out, p_up, p_down], received=[None, bg[0], r_up, r_down], early=bg[1])
    else:
        big = dict(partials=[g_int, g_out, g_up, g_down])
    return grad_x, big, small


def kernel(x, meta_tokens, ln_in_g, ln_in_b, w_in, s5_lambda_re, s5_lambda_im, s5_log_dt, s5_b_re, s5_b_im, s5_c_re, s5_c_im, s5_d, s5_w_glu, s5_b_glu, ret_gn_g, ret_gn_b, w_out, ln1_g, ln1_b, w_up, w_down, ln2_g, ln2_b, loss_target, m_meta_tokens, m_ln_in_g, m_ln_in_b, m_w_in, m_s5_lambda_re, m_s5_lambda_im, m_s5_log_dt, m_s5_b_re, m_s5_b_im, m_s5_c_re, m_s5_c_im, m_s5_d, m_s5_w_glu, m_s5_b_glu, m_ret_gn_g, m_ret_gn_b, m_w_out, m_ln1_g, m_ln1_b, m_w_up, m_w_down, m_ln2_g, m_ln2_b, v_meta_tokens, v_ln_in_g, v_ln_in_b, v_w_in, v_s5_lambda_re, v_s5_lambda_im, v_s5_log_dt, v_s5_b_re, v_s5_b_im, v_s5_c_re, v_s5_c_im, v_s5_d, v_s5_w_glu, v_s5_b_glu, v_ret_gn_g, v_ret_gn_b, v_w_out, v_ln1_g, v_ln1_b, v_w_up, v_w_down, v_ln2_g, v_ln2_b):
    args = dict(locals())
    names = ["meta_tokens", "ln_in_g", "ln_in_b", "w_in", "s5_lambda_re", "s5_lambda_im", "s5_log_dt", "s5_b_re", "s5_b_im",
             "s5_c_re", "s5_c_im", "s5_d", "s5_w_glu", "s5_b_glu", "ret_gn_g", "ret_gn_b", "w_out", "ln1_g", "ln1_b",
             "w_up", "w_down", "ln2_g", "ln2_b"]
    ax, ay, ac = _place()
    me = 4 * ax + 2 * ay + ac

    sp = {n: args[n] for n in SMALL}
    grad_x, big, small = _local_step(x[0], loss_target[0], meta_tokens, w_in[0].T.astype(MM), w_out[0].astype(MM),
                                   w_up[0].astype(MM), w_down[0].astype(MM), s5_w_glu[0].astype(MM), sp, True)

    j_arr = jnp.reshape(2 * ax + ay, (1,)).astype(jnp.int32)
    two_d = lambda a: a.reshape(a.shape[-2:])
    item = lambda n, p, r: (p, r, *(two_d(_view(n, a)) for a in (args[n], args["m_" + n], args["v_" + n])))
    late_pack = _pack([small[n] for n in LATE])
    mlp = ("w_out", "w_up", "w_down")
    res, (r_in, late_all) = _adamw_shards(
        [item(n, p, r) for n, p, r in zip(mlp, big["chip_sums"][1:], big["received"][1:])], "adamw_mlp", 8, j_arr,
        [_job_chips(big["chip_sums"][0]), _job_gather(late_pack)])
    res_in, _ = _adamw_shards([item("w_in", big["chip_sums"][0], r_in)], "adamw_in", 2, j_arr)
    upd = {"w_in": res_in}
    for idx, n in enumerate(mlp):
        upd[n] = res[4 * idx:4 * idx + 4]
    shard_grads = {n: upd[n][0] for n in upd}

    early_shapes = [_view(n, args[n]).shape for n in EARLY[:-2]] + [(S5_W, S5_W), (1,)]
    late_shapes = [args["ln_in_g"].shape, args["ln_in_b"].shape, (N_META, D_MODEL)]
    g_small = dict(zip(EARLY, _unpack(_sum_devices(big["early"], "sum_small_early"), early_shapes)))
    g_small.update(zip(LATE, _unpack(_sum_devices(late_all, "sum_small_late"), late_shapes)))
    loss = g_small["loss"].reshape(())

    shard_grads["meta_tokens"] = lax.dynamic_slice(g_small["meta_tokens"], (0, me * (D_MODEL // N_DEV)),
                                                   (N_META, D_MODEL // N_DEV))
    shard_grads["s5_w_glu"] = lax.dynamic_slice(g_small["s5_w_glu"], (me * (S5_W // N_DEV), 0),
                                                (S5_W // N_DEV, S5_W))[None]
    natives = SMALL + ["meta_tokens", "s5_w_glu"]
    res2 = _adamw_native([(shard_grads[n] if n in shard_grads else g_small[n], *(_view(n, args[p + n]) for p in ("", "m_", "v_")))
                          for n in natives], "adamw_small")
    for idx, n in enumerate(natives):
        upd[n] = [shard_grads[n] if n in shard_grads else g_small[n]] + list(res2[3 * idx:3 * idx + 3])

    grads, deltas, new_m, new_v = ([_view(n, upd[n][t]).reshape(args[n].shape) for n in names] for t in range(4))
    return (loss, grad_x[None], *grads, *deltas, *new_m, *new_v)
```

```python
import math

import jax
import jax.numpy as jnp
from jax import lax
from jax.experimental import pallas as pl
from jax.experimental.pallas import tpu as pltpu

F32 = jnp.float32
MM = jnp.bfloat16

D_MODEL = 1024
N_META = 16
CHUNK = 128
PAD = CHUNK - N_META
S5_W, S5_G, S5_H, S5_P = 256, 16, 16, 64
S5_N = S5_G * S5_P
RET_W, RET_H, HEAD = 768, 6, 128
D_FF = 4096
PROJ_W = S5_W + 4 * RET_W
N_DEV = 8
FF_BLK = D_FF // N_DEV
ROW_BLK = 384
MLP_ROWS = 1408
PROJ_ROWS = 704
ALPHA = 2.0 ** 0.25
LN_EPS = 1e-5
GN_EPS = 1e-5
ROPE_BASE = 10000.0
GELU_C = math.sqrt(2.0 / math.pi)
GELU_A = 0.044715
ADAM_LR, ADAM_B1, ADAM_B2, ADAM_EPS, ADAM_WD, ADAM_STEP = 0.001, 0.9, 0.999, 1e-08, 0.01, 10
VMEM_LIMIT = 60 * 1024 * 1024

_VMEM = pl.BlockSpec(memory_space=pltpu.VMEM)
_ANY = pl.BlockSpec(memory_space=pl.ANY)
_MESH = pl.DeviceIdType.MESH


def _params(sem=None):
    return pltpu.CompilerParams(dimension_semantics=sem, vmem_limit_bytes=VMEM_LIMIT)


def _dot(a, b):
    return jnp.dot(a.astype(MM), b.astype(MM), preferred_element_type=F32)


def _dot_nt(a, b):
    return lax.dot_general(a.astype(MM), b.astype(MM), (((1,), (1,)), ((), ())), preferred_element_type=F32)


def _dot_tn(a, b):
    return lax.dot_general(a.astype(MM), b.astype(MM), (((0,), (0,)), ((), ())), preferred_element_type=F32)


def _split3(a):
    hi = a.astype(jnp.bfloat16)
    r1 = a - hi.astype(F32)
    mid = r1.astype(jnp.bfloat16)
    lo = (r1 - mid.astype(F32)).astype(jnp.bfloat16)
    return hi, mid, lo


def _dot_sel_rhs(a, sel):
    s = sel.astype(jnp.bfloat16)
    return sum(jnp.dot(p, s, preferred_element_type=F32) for p in _split3(a))


def _dot_sel_lhs(sel, b):
    s = sel.astype(jnp.bfloat16)
    return sum(jnp.dot(s, p, preferred_element_type=F32) for p in _split3(b))


def _ln_fwd(r, eps):
    mu = jnp.mean(r, axis=-1, keepdims=True)
    xc = r - mu
    var = jnp.mean(xc * xc, axis=-1, keepdims=True)
    rstd = lax.rsqrt(var + eps)
    return xc * rstd, rstd


def _ln_bwd(dxhat, xhat, rstd):
    m1 = jnp.mean(dxhat, axis=-1, keepdims=True)
    m2 = jnp.mean(dxhat * xhat, axis=-1, keepdims=True)
    return rstd * (dxhat - m1 - xhat * m2)


def _colsum(a):
    return jnp.sum(a, axis=0, keepdims=True)


def _shift3(n_in, block=lambda i: i):
    return [pl.BlockSpec((CHUNK, D_MODEL), (lambda i, j=j: (jnp.clip(3 * block(i) - 1 + j, 0, n_in - 1), 0)))
            for j in range(3)]


def _ln_in(x2d, meta, jobs=(), gather_meta=False):
    seq = x2d.shape[0]
    tp = seq + CHUNK
    R = ROW_BLK
    nb = tp // R
    shard_w = D_MODEL // N_DEV

    def body(xa, xb, xc, meta_ref, xhat_ref, rstd_ref, raw_ref, *gathered):
        raw_ref[0:CHUNK, :] = xa[...]
        raw_ref[CHUNK:2 * CHUNK, :] = xb[...]
        raw_ref[2 * CHUNK:3 * CHUNK, :] = xc[...]

        @pl.when(pl.program_id(0) == nb - 1)
        def _():
            raw_ref[0:PAD, :] = jnp.zeros((PAD, D_MODEL), F32)
            if gather_meta:
                for d in range(N_DEV):
                    pltpu.sync_copy(gathered[0].at[d], raw_ref.at[PAD:CHUNK, d * shard_w:(d + 1) * shard_w])
            else:
                raw_ref[PAD:CHUNK, :] = meta_ref[...]

        xhat_ref[...], rstd_ref[...] = _ln_fwd(raw_ref[...], LN_EPS)

    row = lambda w: pl.BlockSpec((R, w), lambda i: (nb - 1 - i, 0))
    jobs = ([_job_gather(meta)] if gather_meta else []) + list(jobs)
    return _call(
        body, "ln_in", (nb,),
        _shift3(seq // CHUNK, lambda i: nb - 1 - i) + [pl.BlockSpec(meta.shape, lambda i: (0, 0))],
        [row(D_MODEL), row(1)], [jax.ShapeDtypeStruct((tp, D_MODEL), F32), jax.ShapeDtypeStruct((tp, 1), F32)],
        [pltpu.VMEM((R, D_MODEL), F32)], (x2d, x2d, x2d, meta), jobs, early=1 if gather_meta else 0)


def _in_proj(xhat0, ln_g, ln_b, w_int, cos2, sin2, jobs=()):
    tp = xhat0.shape[0]
    R = PROJ_ROWS if tp % PROJ_ROWS == 0 else ROW_BLK

    def body(xh_ref, g_ref, b_ref, w_ref, cos_ref, sin_ref, u_ref, q_ref, k_ref, v_ref, gate_ref):
        hb = (xh_ref[...] * g_ref[...] + b_ref[...]).astype(MM)
        valid = (pl.program_id(0) * R + lax.broadcasted_iota(jnp.int32, (R, 1), 0)) >= PAD

        def seg(lo, hi):
            return jnp.where(valid, _dot_nt(hb, w_ref[lo:hi, :]), 0.0)

        u_ref[...] = seg(0, S5_W)
        cos = cos_ref[...]
        sin = sin_ref[...]
        q = seg(S5_W, S5_W + RET_W)
        k = seg(S5_W + RET_W, S5_W + 2 * RET_W)
        for h in range(RET_H):
            sl = slice(h * HEAD, (h + 1) * HEAD)
            qh = q[:, sl]
            kh = k[:, sl]
            q_ref[:, sl] = (qh * cos + pltpu.roll(qh, HEAD // 2, 1) * sin).astype(q_ref.dtype)
            k_ref[:, sl] = ((kh * cos + pltpu.roll(kh, HEAD // 2, 1) * sin) * (HEAD ** -0.5)).astype(k_ref.dtype)
        v_ref[...] = seg(S5_W + 2 * RET_W, S5_W + 3 * RET_W).astype(v_ref.dtype)
        gate_ref[...] = seg(S5_W + 3 * RET_W, PROJ_W)

    def rows(w, dt):
        return pl.BlockSpec((R, w), lambda i: (i, 0)), jax.ShapeDtypeStruct((tp, w), dt)

    outs = [rows(S5_W, F32), rows(RET_W, MM), rows(RET_W, MM), rows(RET_W, MM), rows(RET_W, F32)]
    full = lambda s: pl.BlockSpec(s, lambda i: (0,) * len(s))
    return _call(
        body, "in_proj", (tp // R,),
        [pl.BlockSpec((R, D_MODEL), lambda i: (i, 0)), full((1, D_MODEL)), full((1, D_MODEL)), _VMEM,
         pl.BlockSpec((R, HEAD), lambda i: (i, 0)), pl.BlockSpec((R, HEAD), lambda i: (i, 0))],
        [o[0] for o in outs], [o[1] for o in outs], [], (xhat0, ln_g, ln_b, w_int, cos2, sin2), jobs)


def _s5_disc(lre, lim, ldt, bre_t, bim_t):
    dt = jnp.exp(ldt)
    mag = jnp.exp(lre * dt)
    ang = lim * dt
    lbr = mag * jnp.cos(ang)
    lbi = mag * jnp.sin(ang)
    den = lre * lre + lim * lim
    nr = lbr - 1.0
    qr = (nr * lre + lbi * lim) / den
    qi = (lbi * lre - nr * lim) / den
    return lbr, lbi, qr * bre_t - qi * bim_t, qr * bim_t + qi * bre_t


def _s5_tables(lbr, lbi, reverse):
    if reverse:
        lbi = -lbi
    pw = [(lbr, lbi)]
    for _ in range(7):
        r, i = pw[-1]
        pw.append((r * lbr - i * lbi, r * lbi + i * lbr))
    row = lax.broadcasted_iota(jnp.int32, (8, S5_N), 0)
    tabs = []
    for k in range(3):
        sh = 2 ** k
        mask = (row < 8 - sh) if reverse else (row >= sh)
        ar, ai = pw[sh - 1]
        tabs.append((jnp.where(mask, ar, 0.0), jnp.where(mask, ai, 0.0)))
    pr = jnp.zeros((8, S5_N), F32)
    pi = jnp.zeros((8, S5_N), F32)
    for i in range(8):
        ar, ai = pw[7 - i] if reverse else pw[i]
        pr = jnp.where(row == i, ar, pr)
        pi = jnp.where(row == i, ai, pi)
    tabs.append((pr, pi))
    return tabs


def _store_tables(tab_ref, tabs):
    for k, (r, i) in enumerate(tabs):
        tab_ref[2 * k] = r
        tab_ref[2 * k + 1] = i


def _bd_mask():
    r = lax.broadcasted_iota(jnp.int32, (S5_W, S5_N), 0)
    c = lax.broadcasted_iota(jnp.int32, (S5_W, S5_N), 1)
    return jnp.right_shift(r, 4) == jnp.right_shift(c, 6)


def _s5_block_diag(bbr_t, bbi_t, cre_w, cim_w):
    mask = _bd_mask()
    bd = lambda t: jnp.where(mask, t, 0.0)
    return (bd(jnp.tile(bbr_t, (S5_G, 1))), bd(jnp.tile(bbi_t, (S5_G, 1))),
            bd(jnp.tile(cre_w, (1, S5_N // HEAD))), bd(jnp.tile(cim_w, (1, S5_N // HEAD))))


def _scan8(xr, xi, tab_ref, lanes, reverse):
    for k in range(3):
        sh = (8 - 2 ** k) if reverse else 2 ** k
        sr = pltpu.roll(xr, sh, 0)
        si = pltpu.roll(xi, sh, 0)
        mr = tab_ref[2 * k, :, lanes]
        mi = tab_ref[2 * k + 1, :, lanes]
        xr, xi = xr + (mr * sr - mi * si), xi + (mr * si + mi * sr)
    return xr, xi


S5_LANES = 256


def _gelu(y):
    t = jnp.tanh(GELU_C * (y + GELU_A * y * y * y))
    return 0.5 * y * (1.0 + t), t


def _s5_fwd(u, lre, lim, ldt, bre_t, bim_t, cre_w, cim_w, d_row, w_glu, b_glu, jobs=()):
    tp = u.shape[0]
    R = ROW_BLK

    def body(u_ref, lre_ref, lim_ref, ldt_ref, bre_ref, bim_ref, cre_ref, cim_ref, d_ref, wg_ref, bg_ref,
             y_ref, xr_ref, xi_ref, bbd_r, bbd_i, cbd_r, cbd_i, tab_ref, car_r, car_i):
        @pl.when(pl.program_id(0) == 0)
        def _():
            lbr, lbi, bbr, bbi = _s5_disc(lre_ref[...], lim_ref[...], ldt_ref[...], bre_ref[...], bim_ref[...])
            br, bi, cr, ci = _s5_block_diag(bbr, bbi, cre_ref[...], cim_ref[...])
            bbd_r[...] = br.astype(MM)
            bbd_i[...] = bi.astype(MM)
            cbd_r[...] = cr.astype(MM)
            cbd_i[...] = ci.astype(MM)
            _store_tables(tab_ref, _s5_tables(lbr, lbi, False))
            car_r[...] = jnp.zeros_like(car_r)
            car_i[...] = jnp.zeros_like(car_i)

        u = u_ref[...]
        ub = u.astype(MM)
        xr_ref[...] = jnp.dot(ub, bbd_r[...], preferred_element_type=F32)
        xi_ref[...] = jnp.dot(ub, bbd_i[...], preferred_element_type=F32)
        for j in range(S5_N // S5_LANES):
            lanes = pl.ds(j * S5_LANES, S5_LANES)
            pr = tab_ref[6, :, lanes]
            pi = tab_ref[7, :, lanes]

            def step(g, carry):
                cr, ci = carry
                rows = pl.ds(pl.multiple_of(g * 8, 8), 8)
                xr, xi = _scan8(xr_ref[rows, lanes], xi_ref[rows, lanes], tab_ref, lanes, False)
                br = jnp.broadcast_to(cr[7:8, :], cr.shape)
                bi = jnp.broadcast_to(ci[7:8, :], ci.shape)
                xr = xr + (pr * br - pi * bi)
                xi = xi + (pr * bi + pi * br)
                xr_ref[rows, lanes] = xr
                xi_ref[rows, lanes] = xi
                return xr, xi

            cr, ci = lax.fori_loop(0, R // 8, step, (car_r[:, lanes], car_i[:, lanes]), unroll=2)
            car_r[:, lanes] = cr
            car_i[:, lanes] = ci
        y = _dot_nt(xr_ref[...], cbd_r[...]) - _dot_nt(xi_ref[...], cbd_i[...]) + d_ref[...] * u
        yg, _ = _gelu(y)
        z = _dot(yg, wg_ref[...]) + bg_ref[...]
        y_ref[...] = yg * jax.nn.sigmoid(z)

    full = lambda a: pl.BlockSpec(a.shape, lambda i: (0,) * a.ndim)
    small = [lre, lim, ldt, bre_t, bim_t, cre_w, cim_w, d_row, w_glu, b_glu]
    return _call(
        body, "s5_fwd", (tp // R,),
        [pl.BlockSpec((R, S5_W), lambda i: (i, 0))] + [full(a) for a in small],
        [pl.BlockSpec((R, S5_W), lambda i: (i, 0)), pl.BlockSpec((R, S5_N), lambda i: (i, 0)),
         pl.BlockSpec((R, S5_N), lambda i: (i, 0))],
        [jax.ShapeDtypeStruct((tp, S5_W), F32), jax.ShapeDtypeStruct((tp, S5_N), F32),
         jax.ShapeDtypeStruct((tp, S5_N), F32)],
        [pltpu.VMEM((S5_W, S5_N), MM)] * 4 + [pltpu.VMEM((8, 8, S5_N), F32), pltpu.VMEM((8, S5_N), F32),
                                              pltpu.VMEM((8, S5_N), F32)],
        (u, *small), jobs)


def _s5_bwd(dy_out, u, xr, xi, lre, lim, ldt, bre_t, bim_t, cre_w, cim_w, d_row, w_glu, b_glu, jobs=()):
    tp = u.shape[0]
    R = ROW_BLK
    nb = tp // R

    def body(dyo_ref, u_ref, xr_ref, xi_ref, xpr_ref, xpi_ref,
             lre_ref, lim_ref, ldt_ref, bre_ref, bim_ref, cre_ref, cim_ref, d_ref, wg_ref, bg_ref,
             du_ref, dlre_ref, dlim_ref, dldt_ref, dbre_ref, dbim_ref, dcre_ref, dcim_ref, dd_ref, dwg_ref, dbg_ref,
             bbd_r, bbd_i, cbd_r, cbd_i, tab_ref, car_r, car_i, gr_ref, gi_ref, xer_ref, xei_ref,
             abr, abi, acr, aci, adr, adi):
        i = pl.program_id(0)

        @pl.when(i == 0)
        def _():
            lbr, lbi, bbr, bbi = _s5_disc(lre_ref[...], lim_ref[...], ldt_ref[...], bre_ref[...], bim_ref[...])
            br, bi, cr, ci = _s5_block_diag(bbr, bbi, cre_ref[...], cim_ref[...])
            bbd_r[...] = br.astype(MM)
            bbd_i[...] = bi.astype(MM)
            cbd_r[...] = cr.astype(MM)
            cbd_i[...] = ci.astype(MM)
            _store_tables(tab_ref, _s5_tables(lbr, lbi, True))
            for ref in (car_r, car_i, abr, abi, acr, aci, adr, adi, dd_ref, dwg_ref, dbg_ref):
                ref[...] = jnp.zeros_like(ref)

        u = u_ref[...]
        xrv = xr_ref[...]
        xiv = xi_ref[...]
        y = _dot_nt(xrv, cbd_r[...]) - _dot_nt(xiv, cbd_i[...]) + d_ref[...] * u
        yg, t = _gelu(y)
        z = _dot(yg, wg_ref[...]) + bg_ref[...]
        s = jax.nn.sigmoid(z)
        dout = dyo_ref[...]
        dz = dout * yg * s * (1.0 - s)
        dyg = dout * s + _dot_nt(dz, wg_ref[...])
        dwg_ref[...] += _dot_tn(yg, dz)
        dbg_ref[...] += _colsum(dz)
        dy = dyg * (0.5 * (1.0 + t) + 0.5 * y * (1.0 - t * t) * GELU_C * (1.0 + 3.0 * GELU_A * y * y))
        dd_ref[...] += _colsum(dy * u)
        acr[...] += _dot_tn(dy, xrv)
        aci[...] -= _dot_tn(dy, xiv)
        gr_ref[...] = _dot(dy, cbd_r[...])
        gi_ref[...] = -_dot(dy, cbd_i[...])
        has_prev = (i < nb - 1).astype(F32)
        xer_ref[0:8, :] = xpr_ref[...] * has_prev
        xei_ref[0:8, :] = xpi_ref[...] * has_prev
        xer_ref[8:R + 8, :] = xrv
        xei_ref[8:R + 8, :] = xiv
        row = lax.broadcasted_iota(jnp.int32, (8, S5_LANES), 0)
        for j in range(S5_N // S5_LANES):
            lanes = pl.ds(j * S5_LANES, S5_LANES)
            pr = tab_ref[6, :, lanes]
            pi = tab_ref[7, :, lanes]

            def step(n, carry):
                cr, ci, sar, sai = carry
                g = R // 8 - 1 - n
                r0 = pl.multiple_of(g * 8, 8)
                rows = pl.ds(r0, 8)
                gr, gi = _scan8(gr_ref[rows, lanes], gi_ref[rows, lanes], tab_ref, lanes, True)
                br = jnp.broadcast_to(cr[0:1, :], cr.shape)
                bi = jnp.broadcast_to(ci[0:1, :], ci.shape)
                gr = gr + (pr * br - pi * bi)
                gi = gi + (pr * bi + pi * br)
                gr_ref[rows, lanes] = gr
                gi_ref[rows, lanes] = gi
                last = row == 7
                xpr = pltpu.roll(jnp.where(last, xer_ref[rows, lanes], xer_ref[pl.ds(r0 + 8, 8), lanes]), 1, 0)
                xpi = pltpu.roll(jnp.where(last, xei_ref[rows, lanes], xei_ref[pl.ds(r0 + 8, 8), lanes]), 1, 0)
                return gr, gi, sar + (gr * xpr + gi * xpi), sai + (gi * xpr - gr * xpi)

            cr, ci, sar, sai = lax.fori_loop(
                0, R // 8, step, (car_r[:, lanes], car_i[:, lanes], adr[:, lanes], adi[:, lanes]), unroll=2)
            car_r[:, lanes] = cr
            car_i[:, lanes] = ci
            adr[:, lanes] = sar
            adi[:, lanes] = sai
        grv = gr_ref[...]
        giv = gi_ref[...]
        du_ref[...] = (dy * d_ref[...] + _dot_nt(grv, bbd_r[...]) + _dot_nt(giv, bbd_i[...])).astype(du_ref.dtype)
        abr[...] += _dot_tn(u, grv)
        abi[...] += _dot_tn(u, giv)

        @pl.when(i == nb - 1)
        def _():
            mask = _bd_mask()
            r16 = lax.broadcasted_iota(jnp.int32, (S5_H, S5_W), 1)
            h16 = lax.broadcasted_iota(jnp.int32, (S5_H, S5_W), 0)
            fold_b = jnp.bitwise_and(r16, S5_H - 1) == h16
            c64 = lax.broadcasted_iota(jnp.int32, (S5_N, S5_P), 0)
            p64 = lax.broadcasted_iota(jnp.int32, (S5_N, S5_P), 1)
            fold_c = jnp.bitwise_and(c64, S5_P - 1) == p64
            dbbr = _dot_sel_lhs(fold_b, jnp.where(mask, abr[...], 0.0))
            dbbi = _dot_sel_lhs(fold_b, jnp.where(mask, abi[...], 0.0))
            dcre_ref[...] = _dot_sel_rhs(jnp.where(mask, acr[...], 0.0), fold_c)
            dcim_ref[...] = _dot_sel_rhs(jnp.where(mask, aci[...], 0.0), fold_c)
            dlbr = _colsum(adr[...])
            dlbi = _colsum(adi[...])
            _, vjp = jax.vjp(_s5_disc, lre_ref[...], lim_ref[...], ldt_ref[...], bre_ref[...], bim_ref[...])
            dlre, dlim, dldt, dbre, dbim = vjp((dlbr, dlbi, dbbr, dbbi))
            dlre_ref[...] = dlre
            dlim_ref[...] = dlim
            dbre_ref[...] = dbre
            dbim_ref[...] = dbim
            gsel = jnp.right_shift(lax.broadcasted_iota(jnp.int32, (S5_N, HEAD), 0), 6) == \
                lax.broadcasted_iota(jnp.int32, (S5_N, HEAD), 1)
            dldt_ref[...] = _dot_sel_rhs(dldt, gsel)

    full = lambda a: pl.BlockSpec(a.shape, lambda i: (0,) * a.ndim)
    rev = lambda w: pl.BlockSpec((R, w), lambda i: (nb - 1 - i, 0))
    prev8 = pl.BlockSpec((8, S5_N), lambda i: (jnp.maximum((nb - 1 - i) * (R // 8) - 1, 0), 0))
    small = [lre, lim, ldt, bre_t, bim_t, cre_w, cim_w, d_row, w_glu, b_glu]
    outs = [((tp, S5_W), rev(S5_W))] + [
        (s, pl.BlockSpec(s, lambda i: (0, 0))) for s in
        [(1, S5_N), (1, S5_N), (1, HEAD), (S5_H, S5_N), (S5_H, S5_N), (S5_W, S5_P), (S5_W, S5_P),
         (1, S5_W), (S5_W, S5_W), (1, S5_W)]]
    return _call(
        body, "s5_bwd", (nb,),
        [rev(S5_W), rev(S5_W), rev(S5_N), rev(S5_N), prev8, prev8] + [full(a) for a in small],
        [o[1] for o in outs], [jax.ShapeDtypeStruct(o[0], MM if n == 0 else F32) for n, o in enumerate(outs)],
        [pltpu.VMEM((S5_W, S5_N), MM)] * 4 + [
            pltpu.VMEM((8, 8, S5_N), F32), pltpu.VMEM((8, S5_N), F32), pltpu.VMEM((8, S5_N), F32),
            pltpu.VMEM((R, S5_N), F32), pltpu.VMEM((R, S5_N), F32),
            pltpu.VMEM((R + 8, S5_N), F32), pltpu.VMEM((R + 8, S5_N), F32)] + [pltpu.VMEM((S5_W, S5_N), F32)] * 4 + [
            pltpu.VMEM((8, S5_N), F32), pltpu.VMEM((8, S5_N), F32)],
        (dy_out, u, xr, xi, xr, xi, *small), jobs)


def _ret_fwd(q, k, v, dmat, zeta_b, xi_b, gam_b, jobs=()):
    tp = q.shape[0]
    C = dmat.shape[1]
    nc = tp // C

    def body(q_ref, k_ref, v_ref, dm_ref, ze_ref, xi_ref, ga_ref, o_ref, st_ref, s_ref):
        @pl.when(pl.program_id(0) == 0)
        def _():
            s_ref[...] = jnp.zeros_like(s_ref)

        for h in range(RET_H):
            sl = slice(h * HEAD, (h + 1) * HEAD)
            qh, kh, vh = q_ref[:, sl], k_ref[:, sl], v_ref[:, sl]
            sh = s_ref[h]
            st_ref[0, sl, :] = sh
            scores = _dot_nt(qh, kh) * dm_ref[h]
            o_ref[:, sl] = _dot(scores, vh) + _dot(qh, sh) * xi_ref[h]
            s_ref[h] = ga_ref[h] * sh + _dot_tn(kh.astype(F32) * ze_ref[h], vh)

    blk = pl.BlockSpec((C, RET_W), lambda c: (c, 0))
    cst = lambda a: pl.BlockSpec(a.shape, lambda c: (0, 0, 0))
    return _call(
        body, "ret_fwd", (nc,), [blk, blk, blk, cst(dmat), cst(zeta_b), cst(xi_b), cst(gam_b)],
        [blk, pl.BlockSpec((1, RET_W, HEAD), lambda c: (c, 0, 0))],
        [jax.ShapeDtypeStruct((tp, RET_W), F32), jax.ShapeDtypeStruct((nc, RET_W, HEAD), F32)],
        [pltpu.VMEM((RET_H, HEAD, HEAD), F32)], (q, k, v, dmat, zeta_b, xi_b, gam_b), jobs)


def _ret_bwd(q, k, v, do, states, cos2, sin2, dmat, zeta_b, xi_b, gam_b, jobs=()):
    tp = q.shape[0]
    C = dmat.shape[1]
    nc = tp // C

    def body(q_ref, k_ref, v_ref, do_ref, st_ref, cos_ref, sin_ref, dm_ref, ze_ref, xi_ref, ga_ref,
             dq_ref, dk_ref, dv_ref, ds_ref):
        @pl.when(pl.program_id(0) == 0)
        def _():
            ds_ref[...] = jnp.zeros_like(ds_ref)

        cos = cos_ref[...]
        sin = sin_ref[...]
        for h in range(RET_H):
            sl = slice(h * HEAD, (h + 1) * HEAD)
            qh, kh, vh = q_ref[:, sl], k_ref[:, sl], v_ref[:, sl]
            dmh = dm_ref[h]
            sh = st_ref[0, sl, :]
            dsn = ds_ref[h]
            doh = do_ref[:, sl]
            dox = doh * xi_ref[h]
            a = _dot_nt(qh, kh) * dmh
            dqk = _dot_nt(doh, vh) * dmh
            kz = kh.astype(F32) * ze_ref[h]
            dv_ref[:, sl] = (_dot_tn(a, doh) + _dot(kz, dsn)).astype(dv_ref.dtype)
            dqr = _dot(dqk, kh) + _dot_nt(dox, sh)
            dkr = _dot_tn(dqk, qh) + ze_ref[h] * _dot_nt(vh, dsn)
            ds_ref[h] = ga_ref[h] * dsn + _dot_tn(qh, dox)
            dq_ref[:, sl] = (dqr * cos - pltpu.roll(dqr, HEAD // 2, 1) * sin).astype(dq_ref.dtype)
            dk_ref[:, sl] = ((dkr * cos - pltpu.roll(dkr, HEAD // 2, 1) * sin) * (HEAD ** -0.5)).astype(dk_ref.dtype)

    blk = pl.BlockSpec((C, RET_W), lambda c: (nc - 1 - c, 0))
    tab = pl.BlockSpec((C, HEAD), lambda c: (nc - 1 - c, 0))
    cst = lambda a: pl.BlockSpec(a.shape, lambda c: (0, 0, 0))
    return _call(
        body, "ret_bwd", (nc,),
        [blk, blk, blk, blk, pl.BlockSpec((1, RET_W, HEAD), lambda c: (nc - 1 - c, 0, 0)), tab, tab,
         cst(dmat), cst(zeta_b), cst(xi_b), cst(gam_b)],
        [blk, blk, blk], [jax.ShapeDtypeStruct((tp, RET_W), MM)] * 3, [pltpu.VMEM((RET_H, HEAD, HEAD), F32)],
        (q, k, v, do, states, cos2, sin2, dmat, zeta_b, xi_b, gam_b), jobs)


def _gn_gate(o, gate, gn_g, gn_b):
    xhat, rstd = _ln_fwd(o, GN_EPS)
    on = xhat * gn_g + gn_b
    s = jax.nn.sigmoid(gate)
    return gate * s * on, xhat, rstd, on, s


def _post_up(o, gate, ys5, xhat0, gn_g, gn_b, li_g, li_b, l1_g, l1_b, w_out, w_up, jobs=()):
    tp = o.shape[0]
    R = ROW_BLK

    def body(o_ref, g_ref, ys_ref, xh0_ref, gng, gnb, lig, lib, l1g, l1b, wo_ref, wu_ref,
             ycat_ref, xh1_ref, rstd1_ref, h1b_ref, pre_ref):
        ycat_ref[:, 0:S5_W] = ys_ref[...].astype(ycat_ref.dtype)
        for h in range(RET_H):
            sl = slice(h * HEAD, (h + 1) * HEAD)
            yret = _gn_gate(o_ref[:, sl], g_ref[:, sl], gng[:, sl], gnb[:, sl])[0]
            ycat_ref[:, S5_W + h * HEAD:S5_W + (h + 1) * HEAD] = yret.astype(ycat_ref.dtype)
        mixed = _dot(ycat_ref[...], wo_ref[...])
        h0 = xh0_ref[...] * lig[...] + lib[...]
        xh1, rstd1 = _ln_fwd(ALPHA * h0 + mixed, LN_EPS)
        xh1_ref[...] = xh1
        rstd1_ref[...] = rstd1
        h1b = (xh1 * l1g[...] + l1b[...]).astype(MM)
        h1b_ref[...] = h1b
        for d in range(N_DEV):
            pre_ref[:, d * FF_BLK:(d + 1) * FF_BLK] = jnp.maximum(_dot(h1b, wu_ref[d]), 0.0)

    row = lambda w: pl.BlockSpec((R, w), lambda i: (i, 0))
    full = lambda a: pl.BlockSpec(a.shape, lambda i: (0,) * a.ndim)
    vecs = [gn_g, gn_b, li_g, li_b, l1_g, l1_b]
    outs = [(row(D_MODEL), jax.ShapeDtypeStruct((tp, D_MODEL), MM)), (row(D_MODEL), jax.ShapeDtypeStruct((tp, D_MODEL), F32)),
            (row(1), jax.ShapeDtypeStruct((tp, 1), F32)), (row(D_MODEL), jax.ShapeDtypeStruct((tp, D_MODEL), MM)),
            (row(D_FF), jax.ShapeDtypeStruct((tp, D_FF), F32))]
    return _call(
        body, "post_up", (tp // R,),
        [row(RET_W), row(RET_W), row(S5_W), row(D_MODEL)] + [full(a) for a in vecs] + [_VMEM, _VMEM],
        [o[0] for o in outs], [o[1] for o in outs], [], (o, gate, ys5, xhat0, *vecs, w_out, w_up), jobs)


def _post_down(pre, xhat1, tgt, l1_g, l1_b, l2_g, l2_b, w_down):
    tp = pre.shape[0]
    seq = tgt.shape[0]
    R = ROW_BLK

    def body(pre_ref, xh1_ref, ta, tb, tc, l1g, l1b, l2g, l2b, wd_ref,
             dr2_ref, dffb_ref, loss_ref, dl2g_ref, dl2b_ref, tgt_ref):
        i = pl.program_id(0)

        @pl.when(i == 0)
        def _():
            for ref in (loss_ref, dl2g_ref, dl2b_ref):
                ref[...] = jnp.zeros_like(ref)

        tgt_ref[0:CHUNK, :] = ta[...]
        tgt_ref[CHUNK:2 * CHUNK, :] = tb[...]
        tgt_ref[2 * CHUNK:3 * CHUNK, :] = tc[...]
        ff = jnp.zeros((R, D_MODEL), F32)
        for d in range(N_DEV):
            pre = pre_ref[:, d * FF_BLK:(d + 1) * FF_BLK]
            ff = ff + _dot(pre * pre, wd_ref[d * FF_BLK:(d + 1) * FF_BLK, :])
        h1 = xh1_ref[...] * l1g[...] + l1b[...]
        xh2, rstd2 = _ln_fwd(ALPHA * h1 + ff, LN_EPS)
        h2 = xh2 * l2g[...] + l2b[...]
        valid = (i * R + lax.broadcasted_iota(jnp.int32, (R, 1), 0)) >= CHUNK
        err = jnp.where(valid, h2 - tgt_ref[...], 0.0)
        loss_ref[...] += 0.5 * jnp.sum(err * err) / D_MODEL
        dh2 = err * (1.0 / D_MODEL)
        dl2g_ref[...] += _colsum(dh2 * xh2)
        dl2b_ref[...] += _colsum(dh2)
        dr2 = _ln_bwd(dh2 * l2g[...], xh2, rstd2)
        dr2_ref[...] = dr2
        dffb_ref[...] = dr2.astype(MM)

    row = lambda w: pl.BlockSpec((R, w), lambda i: (i, 0))
    full = lambda a: pl.BlockSpec(a.shape, lambda i: (0,) * a.ndim)
    vecs = [l1_g, l1_b, l2_g, l2_b]
    acc = lambda s: (pl.BlockSpec(s, lambda i: (0, 0)), jax.ShapeDtypeStruct(s, F32))
    outs = [(row(D_MODEL), jax.ShapeDtypeStruct((tp, D_MODEL), F32)), (row(D_MODEL), jax.ShapeDtypeStruct((tp, D_MODEL), MM)),
            acc((8, HEAD)), acc((1, D_MODEL)), acc((1, D_MODEL))]
    return pl.pallas_call(
        body, name="post_down", grid=(tp // R,),
        in_specs=[row(D_FF), row(D_MODEL)] + _shift3(seq // CHUNK) + [full(a) for a in vecs] + [_VMEM],
        out_specs=[o[0] for o in outs], out_shape=[o[1] for o in outs],
        scratch_shapes=[pltpu.VMEM((R, D_MODEL), F32)],
        compiler_params=_params(("arbitrary",)),
    )(pre, xhat1, tgt, tgt, tgt, *vecs, w_down)


def _mlp_bwd(h1b, dffb, pre, w_up, w_down):
    tp = h1b.shape[0]
    R = MLP_ROWS if tp % MLP_ROWS == 0 else ROW_BLK
    nr = tp // R

    def body(h_ref, df_ref, pre_ref, wu_ref, wd_ref, gup_ref, gdn_ref, dh1_ref, aup, adn):
        d = pl.program_id(0)
        r = pl.program_id(1)

        @pl.when(r == 0)
        def _():
            aup[...] = jnp.zeros_like(aup)
            adn[...] = jnp.zeros_like(adn)

        h = h_ref[...]
        df = df_ref[...]
        wu = wu_ref[0]
        wd = wd_ref[0]
        pre = pre_ref[...]
        dpre = (_dot_nt(df, wd) * (2.0 * pre)).astype(MM)

        aup[...] += _dot_tn(h, dpre)
        adn[...] += _dot_tn(pre * pre, df)
        contrib = _dot_nt(dpre, wu)
        rows = pl.ds(pl.multiple_of(r * R, 64), R)

        @pl.when(d == 0)
        def _():
            dh1_ref[rows, :] = contrib

        @pl.when(d > 0)
        def _():
            dh1_ref[rows, :] += contrib

        @pl.when(r == nr - 1)
        def _():
            gup_ref[0] = aup[...].astype(gup_ref.dtype)
            gdn_ref[0] = adn[...].astype(gdn_ref.dtype)

    return pl.pallas_call(
        body, name="mlp_bwd", grid=(N_DEV, nr),
        in_specs=[pl.BlockSpec((R, D_MODEL), lambda d, r: (r, 0)), pl.BlockSpec((R, D_MODEL), lambda d, r: (r, 0)),
                  pl.BlockSpec((R, FF_BLK), lambda d, r: (r, d)),
                  pl.BlockSpec((1, D_MODEL, FF_BLK), lambda d, r: (d, 0, 0)),
                  pl.BlockSpec((1, FF_BLK, D_MODEL), lambda d, r: (d, 0, 0))],
        out_specs=[pl.BlockSpec((1, D_MODEL, FF_BLK), lambda d, r: (d, 0, 0)),
                   pl.BlockSpec((1, FF_BLK, D_MODEL), lambda d, r: (d, 0, 0)), _VMEM],
        out_shape=[jax.ShapeDtypeStruct((N_DEV, D_MODEL, FF_BLK), MM), jax.ShapeDtypeStruct((N_DEV, FF_BLK, D_MODEL), MM),
                   jax.ShapeDtypeStruct((tp, D_MODEL), F32)],
        scratch_shapes=[pltpu.VMEM((D_MODEL, FF_BLK), F32), pltpu.VMEM((FF_BLK, D_MODEL), F32)],
        compiler_params=_params(("arbitrary", "arbitrary")),
    )(h1b, dffb, pre, w_up, w_down.reshape(N_DEV, FF_BLK, D_MODEL))


def _post_bwd(dh1m, dr2, xhat1, rstd1, ycat, o, gate, gn_g, gn_b, l1_g, w_out, jobs=()):
    tp = o.shape[0]
    R = ROW_BLK
    nb = tp // R

    def body(dm_ref, dr2_ref, xh1_ref, rs1_ref, yc_ref, o_ref, g_ref, gng, gnb, l1g, wo_ref,
             do_ref, dg_ref, dys_ref, dh0_ref, gwo_ref, dl1g_ref, dl1b_ref, dgng_ref, dgnb_ref, awo):
        i = pl.program_id(0)

        @pl.when(i == 0)
        def _():
            for ref in (awo, dl1g_ref, dl1b_ref, dgng_ref, dgnb_ref):
                ref[...] = jnp.zeros_like(ref)

        dh1 = dm_ref[...] + ALPHA * dr2_ref[...]
        xh1 = xh1_ref[...]
        dl1g_ref[...] += _colsum(dh1 * xh1)
        dl1b_ref[...] += _colsum(dh1)
        dr1 = _ln_bwd(dh1 * l1g[...], xh1, rs1_ref[...])
        dh0_ref[...] = ALPHA * dr1
        dmix = dr1.astype(MM)
        awo[...] += _dot_tn(yc_ref[...], dmix)
        dyc = _dot_nt(dmix, wo_ref[...])
        dys_ref[...] = dyc[:, 0:S5_W]
        for h in range(RET_H):
            sl = slice(h * HEAD, (h + 1) * HEAD)
            gt = g_ref[:, sl]
            _, xhat, rstd, on, s = _gn_gate(o_ref[:, sl], gt, gng[:, sl], gnb[:, sl])
            dyr = dyc[:, S5_W + h * HEAD:S5_W + (h + 1) * HEAD]
            dg_ref[:, sl] = (dyr * on * (s * (1.0 + gt * (1.0 - s)))).astype(dg_ref.dtype)
            don = dyr * gt * s
            dgng_ref[:, sl] += _colsum(don * xhat)
            dgnb_ref[:, sl] += _colsum(don)
            do_ref[:, sl] = _ln_bwd(don * gng[:, sl], xhat, rstd)

        @pl.when(i == nb - 1)
        def _():
            gwo_ref[...] = awo[...].astype(gwo_ref.dtype)

    row = lambda w: pl.BlockSpec((R, w), lambda i: (i, 0))
    full = lambda a: pl.BlockSpec(a.shape, lambda i: (0,) * a.ndim)
    acc = lambda s, dt=F32: (pl.BlockSpec(s, lambda i: (0, 0)), jax.ShapeDtypeStruct(s, dt))
    outs = [(row(RET_W), jax.ShapeDtypeStruct((tp, RET_W), F32)), (row(RET_W), jax.ShapeDtypeStruct((tp, RET_W), MM)),
            (row(S5_W), jax.ShapeDtypeStruct((tp, S5_W), F32)), (row(D_MODEL), jax.ShapeDtypeStruct((tp, D_MODEL), F32)),
            acc((D_MODEL, D_MODEL), MM), acc((1, D_MODEL)), acc((1, D_MODEL)), acc((1, RET_W)), acc((1, RET_W))]
    return _call(
        body, "post_bwd", (nb,),
        [row(D_MODEL), row(D_MODEL), row(D_MODEL), row(1), row(D_MODEL), row(RET_W), row(RET_W),
         full(gn_g), full(gn_b), full(l1_g), _VMEM],
        [o[0] for o in outs], [o[1] for o in outs],
        [pltpu.VMEM((D_MODEL, D_MODEL), F32)],
        (dh1m, dr2, xhat1, rstd1, ycat, o, gate, gn_g, gn_b, l1_g, w_out), jobs)


def _in_bwd(du, dq, dk, dv, dg, dh0r, xhat0, rstd0, li_g, li_b, w_int, jobs=()):
    tp = du.shape[0]
    R = PROJ_ROWS if tp % PROJ_ROWS == 0 else ROW_BLK
    nb = tp // R
    segs = [(0, S5_W)] + [(S5_W + n * RET_W, S5_W + (n + 1) * RET_W) for n in range(4)]

    def body(du_ref, dq_ref, dk_ref, dv_ref, dg_ref, dh0r_ref, xh_ref, rs_ref, lig, lib, w_ref,
             gx_ref, dmeta_ref, gw_ref, dlg_ref, dlb_ref, aw, stage, out_sems):
        i = pl.program_id(0)
        slot = i % 2

        def to_gx(step_slot, first):
            if first:
                return pltpu.make_async_copy(stage.at[0, CHUNK:R, :], gx_ref.at[0:R - CHUNK, :], out_sems.at[0])
            return pltpu.make_async_copy(stage.at[step_slot], gx_ref.at[pl.ds(i * R - CHUNK, R), :], out_sems.at[step_slot])

        @pl.when(i == 0)
        def _():
            for ref in (aw, dlg_ref, dlb_ref):
                ref[...] = jnp.zeros_like(ref)

        @pl.when(i >= 3)
        def _():
            to_gx(slot, False).wait()

        valid = (i * R + lax.broadcasted_iota(jnp.int32, (R, 1), 0)) >= PAD
        xh = xh_ref[...]
        hb = (xh * lig[...] + lib[...]).astype(MM)
        dh0 = dh0r_ref[...]
        for (lo, hi), ref in zip(segs, (du_ref, dq_ref, dk_ref, dv_ref, dg_ref)):
            dseg = jnp.where(valid, ref[...], 0.0).astype(MM)
            dh0 = dh0 + _dot(dseg, w_ref[lo:hi, :])
            aw[lo:hi, :] += _dot_tn(dseg, hb)
        dlg_ref[...] += _colsum(dh0 * xh)
        dlb_ref[...] += _colsum(dh0)
        draw = _ln_bwd(dh0 * lig[...], xh, rs_ref[...])
        stage[slot] = draw

        @pl.when(i == 0)
        def _():
            dmeta_ref[...] = draw[PAD:CHUNK, :]
            first = to_gx(0, True)
            first.start()
            first.wait()

        @pl.when(i > 0)
        def _():
            to_gx(slot, False).start()

        @pl.when(i == nb - 1)
        def _():
            gw_ref[...] = aw[...].astype(gw_ref.dtype)
            for back in (1, 0):
                if nb - 1 - back >= 1:
                    to_gx((nb - 1 - back) % 2, False).wait()

    row = lambda w: pl.BlockSpec((R, w), lambda i: (i, 0))
    full = lambda a: pl.BlockSpec(a.shape, lambda i: (0,) * a.ndim)
    acc = lambda s, dt=F32: (pl.BlockSpec(s, lambda i: (0, 0)), jax.ShapeDtypeStruct(s, dt))
    outs = [(_ANY, jax.ShapeDtypeStruct((tp - CHUNK, D_MODEL), F32)), acc((N_META, D_MODEL)), acc((PROJ_W, D_MODEL), MM),
            acc((1, D_MODEL)), acc((1, D_MODEL))]
    return _call(
        body, "in_bwd", (nb,),
        [row(S5_W), row(RET_W), row(RET_W), row(RET_W), row(RET_W), row(D_MODEL), row(D_MODEL), row(1),
         full(li_g), full(li_b), _VMEM],
        [o[0] for o in outs], [o[1] for o in outs],
        [pltpu.VMEM((PROJ_W, D_MODEL), F32), pltpu.VMEM((2, R, D_MODEL), F32), pltpu.SemaphoreType.DMA((2,))],
        (du, dq, dk, dv, dg, dh0r, xhat0, rstd0, li_g, li_b, w_int), jobs)


def _place():
    return lax.axis_index("x"), lax.axis_index("y"), lax.axis_index("c")


def _dma_sems(n):
    return pltpu.SemaphoreType.DMA((n,))


def _job_gather(shard):
    halves = 2 if shard.shape[0] % 32 == 0 else 1
    rows = shard.shape[0] // halves

    def parts(ins, outs, sems):
        (src,), (out,), (send_sems, recv_sems, local_sem) = ins, outs, sems
        x, y, c = _place()
        north = c == 1
        me, sib = (x, y, c), (x, y, 1 - c)
        xn, yn, dg = (1 - x, y, c), (x, 1 - y, c), (1 - x, 1 - y, c)
        relay_from = (jnp.where(north, 1 - x, x), jnp.where(north, y, 1 - y), c)
        relay_to = (jnp.where(north, x, 1 - x), jnp.where(north, 1 - y, y), c)

        def slot(dev, h=None):
            block = out.at[4 * dev[0] + 2 * dev[1] + dev[2]]
            return block if h is None else block.at[pl.ds(h * rows, rows)]

        def copy(k, h, block, to, from_input=False):
            return pltpu.make_async_remote_copy(
                src_ref=src.at[pl.ds(h * rows, rows)] if from_input else slot(block, h), dst_ref=slot(block, h),
                send_sem=send_sems.at[halves * k + h], recv_sem=recv_sems.at[halves * k + h], device_id=to,
                device_id_type=_MESH)

        mine = lambda: pltpu.make_async_copy(src, slot(me), local_sem.at[0])
        first = lambda h: [copy(0, h, me, sib, True), copy(1, h, me, xn, True), copy(2, h, me, yn, True)]
        relayed = lambda h: [copy(3, h, relay_from, relay_to), copy(4, h, xn, sib), copy(5, h, yn, sib)]
        return me, sib, xn, yn, dg, copy, mine, first, relayed

    def start(ins, outs, sems):
        mine, first = parts(ins, outs, sems)[6:8]
        mine().start()
        for h in range(halves):
            for cp in first(h):
                cp.start()

    def relay(ins, outs, sems):
        me, sib, xn, yn, dg, copy, mine, first, relayed = parts(ins, outs, sems)
        for h in range(halves):
            copy(1, h, xn, me).wait_recv()
            copy(2, h, yn, me).wait_recv()
            for cp in relayed(h):
                cp.start()

    def finish(ins, outs, sems):
        me, sib, xn, yn, dg, copy, mine, first, relayed = parts(ins, outs, sems)
        other = 1 - me[2]
        last = []
        for h in range(halves):
            copy(3, h, dg, me).wait_recv()
            last.append(copy(6, h, dg, sib))
            last[-1].start()
        for h in range(halves):
            copy(0, h, sib, me).wait_recv()
            for k, chip in ((4, xn), (5, yn), (6, dg)):
                copy(k, h, (chip[0], chip[1], other), me).wait_recv()
            for cp in first(h) + relayed(h) + [last[h]]:
                cp.wait_send()
        mine().wait()

    return dict(ins=[shard], outs=[jax.ShapeDtypeStruct((N_DEV,) + shard.shape, shard.dtype)],
                sems=[_dma_sems(7 * halves), _dma_sems(7 * halves), _dma_sems(1)], start=start, middle=relay,
                finish=finish)


def _job_pair(g):
    def copies(ins, outs, sems):
        x, y, c = _place()
        return [pltpu.make_async_remote_copy(
            src_ref=ins[0].at[2 * j + (1 - c)], dst_ref=outs[0].at[j], send_sem=sems[0].at[j], recv_sem=sems[1].at[j],
            device_id=(x, y, 1 - c), device_id_type=_MESH) for j in range(4)]

    def start(ins, outs, sems):
        for cp in copies(ins, outs, sems):
            cp.start()

    def finish(ins, outs, sems):
        for cp in copies(ins, outs, sems):
            cp.wait()

    return dict(ins=[g], outs=[jax.ShapeDtypeStruct((4,) + g.shape[1:], g.dtype)], sems=[_dma_sems(4), _dma_sems(4)],
                start=start, finish=finish)


def _job_chips(p):
    def copies(ins, outs, sems):
        x, y, c = _place()
        chips = [(1 - x, y), (x, 1 - y), (1 - x, 1 - y)]
        return [pltpu.make_async_remote_copy(
            src_ref=ins[0].at[2 * chip[0] + chip[1]], dst_ref=outs[0].at[k], send_sem=sems[0].at[k],
            recv_sem=sems[1].at[k], device_id=(*chip, c), device_id_type=_MESH) for k, chip in enumerate(chips)]

    def start(ins, outs, sems):
        for cp in copies(ins, outs, sems):
            cp.start()

    def finish(ins, outs, sems):
        for cp in copies(ins, outs, sems):
            cp.wait()

    return dict(ins=[p], outs=[jax.ShapeDtypeStruct((3,) + p.shape[1:], p.dtype)], sems=[_dma_sems(3), _dma_sems(3)],
                start=start, finish=finish)


def _split_job_refs(jobs, ins, outs, sems):
    res, a, b, c = [], 0, 0, 0
    for job in jobs:
        na, nb, nc = len(job["ins"]), len(job["outs"]), len(job["sems"])
        res.append((ins[a:a + na], outs[b:b + nb], sems[c:c + nc]))
        a, b, c = a + na, b + nb, c + nc
    return res


def _call(body, name, grid, in_specs, out_specs, out_shape, scratch, args, jobs=(), prefetch=None, early=0):
    jobs = list(jobs)
    n_in, n_out, n_scr = len(in_specs), len(out_specs), len(scratch)
    j_in = [a for job in jobs for a in job["ins"]]
    j_out = [o for job in jobs for o in job["outs"]]
    j_scr = [s for job in jobs for s in job["sems"]]
    nsteps = grid[0]
    n_pre = 0 if prefetch is None else 1

    def wrapped(*refs):
        pre, refs = refs[:n_pre], refs[n_pre:]
        ins, jins = refs[:n_in], refs[n_in:n_in + len(j_in)]
        refs = refs[n_in + len(j_in):]
        outs, jouts = refs[:n_out], refs[n_out:n_out + len(j_out)]
        refs = refs[n_out + len(j_out):]
        scr, jscr = refs[:n_scr], refs[n_scr:]
        per_job = _split_job_refs(jobs, jins, jouts, jscr)

        def middle():
            for job, r in zip(jobs, per_job):
                if "middle" in job:
                    job["middle"](*r)

        @pl.when(pl.program_id(0) == 0)
        def _():
            for job, r in zip(jobs, per_job):
                job["start"](*r)

        if nsteps >= 3:
            pl.when(pl.program_id(0) == nsteps // 2)(middle)

        if early:
            @pl.when(pl.program_id(0) == nsteps - 1)
            def _():
                for job, r in zip(jobs[:early], per_job[:early]):
                    job["finish"](*r)

        body(*pre, *ins, *outs, *scr, *[o for r in per_job[:early] for o in r[1]])

        @pl.when(pl.program_id(0) == nsteps - 1)
        def _():
            if nsteps < 3:
                middle()
            for job, r in zip(jobs[early:], per_job[early:]):
                job["finish"](*r)

    specs = dict(in_specs=list(in_specs) + [_ANY] * len(j_in), out_specs=list(out_specs) + [_ANY] * len(j_out),
                 scratch_shapes=list(scratch) + j_scr)
    if n_pre:
        specs = dict(grid_spec=pltpu.PrefetchScalarGridSpec(num_scalar_prefetch=1, grid=grid, **specs))
    else:
        specs["grid"] = grid
    res = pl.pallas_call(
        wrapped if jobs else body, name=name, out_shape=list(out_shape) + j_out,
        compiler_params=_params(("arbitrary",) * len(grid)), **specs,
    )(*([prefetch] if n_pre else []), *args, *j_in)
    return list(res[:n_out]), list(res[n_out:])


def _exchange(jobs, name):
    j_in = [a for job in jobs for a in job["ins"]]
    j_out = [o for job in jobs for o in job["outs"]]
    j_scr = [s for job in jobs for s in job["sems"]]

    def body(*refs):
        per_job = _split_job_refs(jobs, refs[:len(j_in)], refs[len(j_in):len(j_in) + len(j_out)],
                                  refs[len(j_in) + len(j_out):])
        for phase in ("start", "middle", "finish"):
            for job, r in zip(jobs, per_job):
                if phase in job:
                    job[phase](*r)

    return pl.pallas_call(body, name=name, out_shape=j_out, in_specs=[_ANY] * len(j_in), out_specs=[_ANY] * len(j_out),
                          scratch_shapes=j_scr)(*j_in)


def _pair_sum(gs, r1s, c_arr, name):
    n = len(gs)

    def body(c_ref, *refs):
        for a in range(n):
            refs[2 * n + a][...] = (refs[a][...].astype(F32) + refs[n + a][...].astype(F32)).astype(refs[2 * n + a].dtype)

    def blk(g, own):
        s = g.shape[1:]
        if own:
            return pl.BlockSpec((1,) + s, lambda j, c_ref: (2 * j + c_ref[0],) + (0,) * len(s))
        return pl.BlockSpec((1,) + s, lambda j, c_ref: (j,) + (0,) * len(s))

    return pl.pallas_call(
        body, name=name,
        grid_spec=pltpu.PrefetchScalarGridSpec(
            num_scalar_prefetch=1, grid=(4,),
            in_specs=[blk(g, True) for g in gs] + [blk(g, False) for g in gs],
            out_specs=[blk(g, False) for g in gs]),
        out_shape=[jax.ShapeDtypeStruct((4,) + g.shape[1:], g.dtype) for g in gs],
        compiler_params=_params(("arbitrary",)),
    )(c_arr, *gs, *r1s)


def _adamw_math(w, g, m, v):
    m = ADAM_B1 * m + (1.0 - ADAM_B1) * g
    v = ADAM_B2 * v + (1.0 - ADAM_B2) * (g * g)
    m_hat = m / (1.0 - ADAM_B1 ** ADAM_STEP)
    v_hat = v / (1.0 - ADAM_B2 ** ADAM_STEP)
    return -ADAM_LR * (m_hat / (jnp.sqrt(v_hat) + ADAM_EPS) + ADAM_WD * w), m, v


def _view(name, a):
    return jnp.swapaxes(a, -1, -2) if name in ("w_in", "s5_b_re", "s5_b_im") else a


def _adamw_shards(items, name, steps, chip, jobs=()):
    n = len(items)

    def body(chip_ref, *refs):
        for a in range(n):
            p_ref, r_ref, w_ref, m_ref, v_ref = refs[5 * a:5 * a + 5]
            g = ((p_ref[0].astype(F32) + r_ref[0].astype(F32)) + r_ref[1].astype(F32)) + r_ref[2].astype(F32)
            outs = refs[5 * n + 4 * a:5 * n + 4 * a + 4]
            outs[0][...] = g
            outs[1][...], outs[2][...], outs[3][...] = _adamw_math(w_ref[...], g, m_ref[...], v_ref[...])

    in_specs, out_specs, out_shape, flat = [], [], [], []
    for p, r, w, m, v in items:
        rows, cols = w.shape
        rb = rows // steps
        in_specs += [pl.BlockSpec((1, rb, cols), lambda i, c: (c[0], i, 0)), pl.BlockSpec((3, rb, cols), lambda i, c: (0, i, 0))]
        wblk = pl.BlockSpec((rb, cols), lambda i, c: (i, 0))
        in_specs += [wblk] * 3
        out_specs += [wblk] * 4
        out_shape += [jax.ShapeDtypeStruct(w.shape, F32)] * 4
        flat += [p, r, w, m, v]
    return _call(body, name, (steps,), in_specs, out_specs, out_shape, [], flat, jobs, prefetch=chip)


def _sum_devices(gathered, name):
    def body(gs_ref, g_ref):
        g = gs_ref[0]
        for s in range(1, N_DEV):
            g = g + gs_ref[s]
        g_ref[...] = g

    return pl.pallas_call(body, name=name, out_shape=jax.ShapeDtypeStruct(gathered.shape[1:], F32),
                          in_specs=[_VMEM], out_specs=_VMEM, compiler_params=_params())(gathered)


def _adamw_native(items, name):
    n = len(items)

    def body(*refs):
        for a in range(n):
            g, w, m, v = (refs[4 * a + t][...] for t in range(4))
            refs[4 * n + 3 * a][...], refs[4 * n + 3 * a + 1][...], refs[4 * n + 3 * a + 2][...] = _adamw_math(w, g, m, v)

    return pl.pallas_call(
        body, name=name, out_shape=[jax.ShapeDtypeStruct(it[1].shape, F32) for it in items for _ in range(3)],
        in_specs=[_VMEM] * (4 * n), out_specs=[_VMEM] * (3 * n), compiler_params=_params(),
    )(*[t for it in items for t in it])


SMALL = ["ln_in_g", "ln_in_b", "s5_lambda_re", "s5_lambda_im", "s5_log_dt", "s5_b_re", "s5_b_im", "s5_c_re", "s5_c_im",
         "s5_d", "s5_b_glu", "ret_gn_g", "ret_gn_b", "ln1_g", "ln1_b", "ln2_g", "ln2_b"]
LATE = ["ln_in_g", "ln_in_b", "meta_tokens"]
EARLY = [n for n in SMALL if n not in LATE] + ["s5_w_glu", "loss"]
LANE = 128


def _pack(arrs):
    parts = []
    for a in arrs:
        f = a.reshape(-1)
        parts.append(jnp.pad(f, (0, (-f.shape[0]) % LANE)))
    flat = jnp.concatenate(parts)
    rows = -(-flat.shape[0] // LANE)
    flat = jnp.pad(flat, (0, (-rows % 32) * LANE + rows * LANE - flat.shape[0]))
    return flat.reshape(-1, LANE)


def _unpack(packed, shapes):
    flat = packed.reshape(-1)
    out, off = [], 0
    for s in shapes:
        n = math.prod(s)
        out.append(flat[off:off + n].reshape(s))
        off += n + (-n) % LANE
    return out


def _rope_tables(tp):
    inv_freq = 1.0 / (ROPE_BASE ** (jnp.arange(0, HEAD, 2, dtype=F32) / HEAD))
    blk = (jnp.arange(tp // ROW_BLK, dtype=F32) * ROW_BLK)[:, None, None] * inv_freq
    off = (jnp.arange(ROW_BLK, dtype=F32) - float(PAD))[None, :, None] * inv_freq
    cos = (jnp.cos(blk) * jnp.cos(off) - jnp.sin(blk) * jnp.sin(off)).reshape(tp, HEAD // 2)
    sin = (jnp.sin(blk) * jnp.cos(off) + jnp.cos(blk) * jnp.sin(off)).reshape(tp, HEAD // 2)
    return jnp.concatenate([cos, cos], axis=1), jnp.concatenate([-sin, sin], axis=1)


RET_CHUNK = ROW_BLK


def _decay_tables():
    log_gamma = jnp.log1p(-jnp.exp2(-5.0 - jnp.arange(RET_H, dtype=F32)))
    idx = jnp.arange(RET_CHUNK, dtype=F32)
    diff = idx[:, None] - idx[None, :]
    dmat = jnp.where(diff[None] >= 0, jnp.exp(jnp.maximum(diff, 0.0)[None] * log_gamma[:, None, None]), 0.0)
    zeta = jnp.exp((RET_CHUNK - 1.0 - idx)[None] * log_gamma[:, None])
    xi = jnp.exp((idx + 1.0)[None] * log_gamma[:, None])
    gam = jnp.exp(RET_CHUNK * log_gamma)
    wide = lambda t: jnp.broadcast_to(t[:, :, None], (RET_H, RET_CHUNK, HEAD))
    return dmat, wide(zeta), wide(xi), jnp.broadcast_to(gam[:, None, None], (RET_H, HEAD, HEAD))


def _local_step(x2d, tgt, meta, w_int, w_out, w_up, w_down, w_glu, sp, distributed):
    tp = x2d.shape[0] + CHUNK
    row = lambda a: a.reshape(1, -1)
    cos2, sin2 = _rope_tables(tp)
    dmat, zeta_b, xi_b, gam_b = _decay_tables()
    li_g, li_b = row(sp["ln_in_g"]), row(sp["ln_in_b"])
    l1_g, l1_b, l2_g, l2_b = row(sp["ln1_g"]), row(sp["ln1_b"]), row(sp["ln2_g"]), row(sp["ln2_b"])
    gn_g, gn_b = row(sp["ret_gn_g"]), row(sp["ret_gn_b"])
    lre, lim = row(sp["s5_lambda_re"]), row(sp["s5_lambda_im"])
    ldt = row(jnp.repeat(sp["s5_log_dt"].reshape(-1), S5_P))
    to_t = lambda b: b.reshape(S5_G, S5_P, S5_H).transpose(2, 0, 1).reshape(S5_H, S5_N)
    bre_t, bim_t = to_t(sp["s5_b_re"]), to_t(sp["s5_b_im"])
    to_w = lambda c: jnp.tile(c.reshape(S5_W, S5_P), (1, 2))
    cre_w, cim_w = to_w(sp["s5_c_re"]), to_w(sp["s5_c_im"])

    jobs = (lambda *j: list(j)) if distributed else (lambda *j: [])
    c_arr = jnp.reshape(lax.axis_index("c"), (1,)).astype(jnp.int32) if distributed else None
    (xhat0, rstd0), bg = _ln_in(x2d, meta, jobs(*([_job_gather(w_int), _job_gather(w_glu)] if distributed else [])),
                                gather_meta=distributed)
    if distributed:
        w_int, w_glu = bg[1].reshape(PROJ_W, D_MODEL), bg[2].reshape(S5_W, S5_W)
    s5_small = (lre, lim, ldt, bre_t, bim_t, cre_w, cim_w, row(sp["s5_d"]), w_glu, row(sp["s5_b_glu"]))
    (u, q, k, v, gate), bg = _in_proj(xhat0, li_g, li_b, w_int, cos2, sin2,
                                      jobs(_job_gather(w_out) if distributed else None))
    if distributed:
        w_out = bg[0].reshape(D_MODEL, D_MODEL)
    (ys5, xr, xi), bg = _s5_fwd(u, *s5_small, jobs=jobs(_job_gather(w_up) if distributed else None))
    if distributed:
        w_up = bg[0]
    (o, states), _ = _ret_fwd(q, k, v, dmat, zeta_b, xi_b, gam_b)
    (ycat, xhat1, rstd1, h1b, pre), bg = _post_up(o, gate, ys5, xhat0, gn_g, gn_b, li_g, li_b, l1_g, l1_b, w_out, w_up,
                                                  jobs(_job_gather(w_down) if distributed else None))
    if distributed:
        w_down = bg[0].reshape(D_FF, D_MODEL)
    dr2, dffb, loss8, dl2g, dl2b = _post_down(pre, xhat1, tgt, l1_g, l1_b, l2_g, l2_b, w_down)
    g_up, g_down, dh1m = _mlp_bwd(h1b, dffb, pre, w_up, w_down)
    (do, dgate, dys5, dh0r, g_out, dl1g, dl1b, dgng, dgnb), bg = _post_bwd(
        dh1m, dr2, xhat1, rstd1, ycat, o, gate, gn_g, gn_b, l1_g, w_out,
        jobs(*([_job_pair(g_up), _job_pair(g_down)] if distributed else [])))
    g_out = g_out.reshape(N_DEV, D_MODEL // N_DEV, D_MODEL)
    if distributed:
        p_up, p_down = _pair_sum([g_up, g_down], bg, c_arr, "pair_sum_mlp")
    (du, dlre, dlim, dldt, dbre_t, dbim_t, dcre, dcim, dd, dwglu, dbglu), bg = _s5_bwd(
        dys5, u, xr, xi, *s5_small,
        jobs=jobs(*([_job_chips(p_up), _job_chips(p_down), _job_pair(g_out)] if distributed else [])))
    if distributed:
        r_up, r_down = bg[0], bg[1]
        (p_out,) = _pair_sum([g_out], bg[2:], c_arr, "pair_sum_out")
    from_t = lambda t: t.reshape(S5_H, S5_G, S5_P).transpose(1, 0, 2)
    small = {
        "s5_lambda_re": dlre, "s5_lambda_im": dlim, "s5_log_dt": dldt[:, :S5_G],
        "s5_b_re": from_t(dbre_t), "s5_b_im": from_t(dbim_t), "s5_c_re": dcre, "s5_c_im": dcim, "s5_d": dd,
        "s5_b_glu": dbglu, "ret_gn_g": dgng, "ret_gn_b": dgnb, "ln1_g": dl1g, "ln1_b": dl1b, "ln2_g": dl2g, "ln2_b": dl2b,
        "s5_w_glu": dwglu, "loss": loss8[0:1, 0:1]}
    early_pack = _pack([small[n] for n in EARLY])
    (dq, dk, dv), bg = _ret_bwd(q, k, v, do, states, cos2, sin2, dmat, zeta_b, xi_b, gam_b,
                                jobs(*([_job_chips(p_out), _job_gather(early_pack)] if distributed else [])))
    (grad_x, dmeta, g_int, dlig, dlib), _ = _in_bwd(du, dq, dk, dv, dgate, dh0r, xhat0, rstd0, li_g, li_b, w_int)
    small.update(ln_in_g=dlig, ln_in_b=dlib, meta_tokens=dmeta)
    g_int = g_int.reshape(N_DEV, PROJ_W // N_DEV, D_MODEL)
    if distributed:
        (r1_in,) = _exchange([_job_pair(g_int)], "exchange_pair_in")
        (p_in,) = _pair_sum([g_int], [r1_in], c_arr, "pair_sum_in")
        big = dict(chip_sums=[p_in, p_out, p_up, p_down], received=[None, bg[0], r_up, r_down], early=bg[1])
    else:
        big = dict(partials=[g_int, g_out, g_up, g_down])
    return grad_x, big, small


def kernel(x, meta_tokens, ln_in_g, ln_in_b, w_in, s5_lambda_re, s5_lambda_im, s5_log_dt, s5_b_re, s5_b_im, s5_c_re, s5_c_im, s5_d, s5_w_glu, s5_b_glu, ret_gn_g, ret_gn_b, w_out, ln1_g, ln1_b, w_up, w_down, ln2_g, ln2_b, loss_target, m_meta_tokens, m_ln_in_g, m_ln_in_b, m_w_in, m_s5_lambda_re, m_s5_lambda_im, m_s5_log_dt, m_s5_b_re, m_s5_b_im, m_s5_c_re, m_s5_c_im, m_s5_d, m_s5_w_glu, m_s5_b_glu, m_ret_gn_g, m_ret_gn_b, m_w_out, m_ln1_g, m_ln1_b, m_w_up, m_w_down, m_ln2_g, m_ln2_b, v_meta_tokens, v_ln_in_g, v_ln_in_b, v_w_in, v_s5_lambda_re, v_s5_lambda_im, v_s5_log_dt, v_s5_b_re, v_s5_b_im, v_s5_c_re, v_s5_c_im, v_s5_d, v_s5_w_glu, v_s5_b_glu, v_ret_gn_g, v_ret_gn_b, v_w_out, v_ln1_g, v_ln1_b, v_w_up, v_w_down, v_ln2_g, v_ln2_b):
    args = dict(locals())
    names = ["meta_tokens", "ln_in_g", "ln_in_b", "w_in", "s5_lambda_re", "s5_lambda_im", "s5_log_dt", "s5_b_re", "s5_b_im",
             "s5_c_re", "s5_c_im", "s5_d", "s5_w_glu", "s5_b_glu", "ret_gn_g", "ret_gn_b", "w_out", "ln1_g", "ln1_b",
             "w_up", "w_down", "ln2_g", "ln2_b"]
    ax, ay, ac = _place()
    me = 4 * ax + 2 * ay + ac

    sp = {n: args[n] for n in SMALL}
    grad_x, big, small = _local_step(x[0], loss_target[0], meta_tokens, w_in[0].T.astype(MM), w_out[0].astype(MM),
                                   w_up[0].astype(MM), w_down[0].astype(MM), s5_w_glu[0].astype(MM), sp, True)

    j_arr = jnp.reshape(2 * ax + ay, (1,)).astype(jnp.int32)
    two_d = lambda a: a.reshape(a.shape[-2:])
    item = lambda n, p, r: (p, r, *(two_d(_view(n, a)) for a in (args[n], args["m_" + n], args["v_" + n])))
    late_pack = _pack([small[n] for n in LATE])
    mlp = ("w_out", "w_up", "w_down")
    res, (r_in, late_all) = _adamw_shards(
        [item(n, p, r) for n, p, r in zip(mlp, big["chip_sums"][1:], big["received"][1:])], "adamw_mlp", 8, j_arr,
        [_job_chips(big["chip_sums"][0]), _job_gather(late_pack)])
    res_in, _ = _adamw_shards([item("w_in", big["chip_sums"][0], r_in)], "adamw_in", 2, j_arr)
    upd = {"w_in": res_in}
    for idx, n in enumerate(mlp):
        upd[n] = res[4 * idx:4 * idx + 4]
    shard_grads = {n: upd[n][0] for n in upd}

    early_shapes = [_view(n, args[n]).shape for n in EARLY[:-2]] + [(S5_W, S5_W), (1,)]
    late_shapes = [args["ln_in_g"].shape, args["ln_in_b"].shape, (N_META, D_MODEL)]
    g_small = dict(zip(EARLY, _unpack(_sum_devices(big["early"], "sum_small_early"), early_shapes)))
    g_small.update(zip(LATE, _unpack(_sum_devices(late_all, "sum_small_late"), late_shapes)))
    loss = g_small["loss"].reshape(())

    shard_grads["meta_tokens"] = lax.dynamic_slice(g_small["meta_tokens"], (0, me * (D_MODEL // N_DEV)),
                                                   (N_META, D_MODEL // N_DEV))
    shard_grads["s5_w_glu"] = lax.dynamic_slice(g_small["s5_w_glu"], (me * (S5_W // N_DEV), 0),
                                                (S5_W // N_DEV, S5_W))[None]
    natives = SMALL + ["meta_tokens", "s5_w_glu"]
    res2 = _adamw_native([(shard_grads[n] if n in shard_grads else g_small[n], *(_view(n, args[p + n]) for p in ("", "m_", "v_")))
                          for n in natives], "adamw_small")
    for idx, n in enumerate(natives):
        upd[n] = [shard_grads[n] if n in shard_grads else g_small[n]] + list(res2[3 * idx:3 * idx + 3])

    grads, deltas, new_m, new_v = ([_view(n, upd[n][t]).reshape(args[n].shape) for n in names] for t in range(4))
    return (loss, grad_x[None], *grads, *deltas, *new_m, *new_v)
```

```python
import math

import jax
import jax.numpy as jnp
from jax import lax
from jax.experimental import pallas as pl
from jax.experimental.pallas import tpu as pltpu

F32 = jnp.float32
MM = jnp.bfloat16

D_MODEL = 1024
N_META = 16
CHUNK = 128
PAD = CHUNK - N_META
S5_W, S5_G, S5_H, S5_P = 256, 16, 16, 64
S5_N = S5_G * S5_P
RET_W, RET_H, HEAD = 768, 6, 128
D_FF = 4096
PROJ_W = S5_W + 4 * RET_W
N_DEV = 8
FF_BLK = D_FF // N_DEV
ROW_BLK = 384
MLP_ROWS = 1408
PROJ_ROWS = 704
ALPHA = 2.0 ** 0.25
LN_EPS = 1e-5
GN_EPS = 1e-5
ROPE_BASE = 10000.0
GELU_C = math.sqrt(2.0 / math.pi)
GELU_A = 0.044715
ADAM_LR, ADAM_B1, ADAM_B2, ADAM_EPS, ADAM_WD, ADAM_STEP = 0.001, 0.9, 0.999, 1e-08, 0.01, 10
VMEM_LIMIT = 60 * 1024 * 1024

_VMEM = pl.BlockSpec(memory_space=pltpu.VMEM)
_ANY = pl.BlockSpec(memory_space=pl.ANY)
_MESH = pl.DeviceIdType.MESH


def _params(sem=None):
    return pltpu.CompilerParams(dimension_semantics=sem, vmem_limit_bytes=VMEM_LIMIT)


def _dot(a, b):
    return jnp.dot(a.astype(MM), b.astype(MM), preferred_element_type=F32)


def _dot_nt(a, b):
    return lax.dot_general(a.astype(MM), b.astype(MM), (((1,), (1,)), ((), ())), preferred_element_type=F32)


def _dot_tn(a, b):
    return lax.dot_general(a.astype(MM), b.astype(MM), (((0,), (0,)), ((), ())), preferred_element_type=F32)


def _split3(a):
    hi = a.astype(jnp.bfloat16)
    r1 = a - hi.astype(F32)
    mid = r1.astype(jnp.bfloat16)
    lo = (r1 - mid.astype(F32)).astype(jnp.bfloat16)
    return hi, mid, lo


def _dot_sel_rhs(a, sel):
    s = sel.astype(jnp.bfloat16)
    return sum(jnp.dot(p, s, preferred_element_type=F32) for p in _split3(a))


def _dot_sel_lhs(sel, b):
    s = sel.astype(jnp.bfloat16)
    return sum(jnp.dot(s, p, preferred_element_type=F32) for p in _split3(b))


def _ln_fwd(r, eps):
    mu = jnp.mean(r, axis=-1, keepdims=True)
    xc = r - mu
    var = jnp.mean(xc * xc, axis=-1, keepdims=True)
    rstd = lax.rsqrt(var + eps)
    return xc * rstd, rstd


def _ln_bwd(dxhat, xhat, rstd):
    m1 = jnp.mean(dxhat, axis=-1, keepdims=True)
    m2 = jnp.mean(dxhat * xhat, axis=-1, keepdims=True)
    return rstd * (dxhat - m1 - xhat * m2)


def _colsum(a):
    return jnp.sum(a, axis=0, keepdims=True)


def _shift3(n_in, block=lambda i: i):
    return [pl.BlockSpec((CHUNK, D_MODEL), (lambda i, j=j: (jnp.clip(3 * block(i) - 1 + j, 0, n_in - 1), 0)))
            for j in range(3)]


def _ln_in(x2d, meta, jobs=(), gather_meta=False):
    seq = x2d.shape[0]
    tp = seq + CHUNK
    R = ROW_BLK
    nb = tp // R
    shard_w = D_MODEL // N_DEV

    def body(xa, xb, xc, meta_ref, xhat_ref, rstd_ref, raw_ref, *gathered):
        raw_ref[0:CHUNK, :] = xa[...]
        raw_ref[CHUNK:2 * CHUNK, :] = xb[...]
        raw_ref[2 * CHUNK:3 * CHUNK, :] = xc[...]

        @pl.when(pl.program_id(0) == nb - 1)
        def _():
            raw_ref[0:PAD, :] = jnp.zeros((PAD, D_MODEL), F32)
            if gather_meta:
                for d in range(N_DEV):
                    pltpu.sync_copy(gathered[0].at[d], raw_ref.at[PAD:CHUNK, d * shard_w:(d + 1) * shard_w])
            else:
                raw_ref[PAD:CHUNK, :] = meta_ref[...]

        xhat_ref[...], rstd_ref[...] = _ln_fwd(raw_ref[...], LN_EPS)

    row = lambda w: pl.BlockSpec((R, w), lambda i: (nb - 1 - i, 0))
    jobs = ([_job_gather(meta)] if gather_meta else []) + list(jobs)
    return _call(
        body, "ln_in", (nb,),
        _shift3(seq // CHUNK, lambda i: nb - 1 - i) + [pl.BlockSpec(meta.shape, lambda i: (0, 0))],
        [row(D_MODEL), row(1)], [jax.ShapeDtypeStruct((tp, D_MODEL), F32), jax.ShapeDtypeStruct((tp, 1), F32)],
        [pltpu.VMEM((R, D_MODEL), F32)], (x2d, x2d, x2d, meta), jobs, early=1 if gather_meta else 0)


def _in_proj(xhat0, ln_g, ln_b, w_int, cos2, sin2, jobs=()):
    tp = xhat0.shape[0]
    R = PROJ_ROWS if tp % PROJ_ROWS == 0 else ROW_BLK

    def body(xh_ref, g_ref, b_ref, w_ref, cos_ref, sin_ref, u_ref, q_ref, k_ref, v_ref, gate_ref):
        hb = (xh_ref[...] * g_ref[...] + b_ref[...]).astype(MM)
        valid = (pl.program_id(0) * R + lax.broadcasted_iota(jnp.int32, (R, 1), 0)) >= PAD

        def seg(lo, hi):
            return jnp.where(valid, _dot_nt(hb, w_ref[lo:hi, :]), 0.0)

        u_ref[...] = seg(0, S5_W)
        cos = cos_ref[...]
        sin = sin_ref[...]
        q = seg(S5_W, S5_W + RET_W)
        k = seg(S5_W + RET_W, S5_W + 2 * RET_W)
        for h in range(RET_H):
            sl = slice(h * HEAD, (h + 1) * HEAD)
            qh = q[:, sl]
            kh = k[:, sl]
            q_ref[:, sl] = (qh * cos + pltpu.roll(qh, HEAD // 2, 1) * sin).astype(q_ref.dtype)
            k_ref[:, sl] = ((kh * cos + pltpu.roll(kh, HEAD // 2, 1) * sin) * (HEAD ** -0.5)).astype(k_ref.dtype)
        v_ref[...] = seg(S5_W + 2 * RET_W, S5_W + 3 * RET_W).astype(v_ref.dtype)
        gate_ref[...] = seg(S5_W + 3 * RET_W, PROJ_W)

    def rows(w, dt):
        return pl.BlockSpec((R, w), lambda i: (i, 0)), jax.ShapeDtypeStruct((tp, w), dt)

    outs = [rows(S5_W, F32), rows(RET_W, MM), rows(RET_W, MM), rows(RET_W, MM), rows(RET_W, F32)]
    full = lambda s: pl.BlockSpec(s, lambda i: (0,) * len(s))
    return _call(
        body, "in_proj", (tp // R,),
        [pl.BlockSpec((R, D_MODEL), lambda i: (i, 0)), full((1, D_MODEL)), full((1, D_MODEL)), _VMEM,
         pl.BlockSpec((R, HEAD), lambda i: (i, 0)), pl.BlockSpec((R, HEAD), lambda i: (i, 0))],
        [o[0] for o in outs], [o[1] for o in outs], [], (xhat0, ln_g, ln_b, w_int, cos2, sin2), jobs)


def _s5_disc(lre, lim, ldt, bre_t, bim_t):
    dt = jnp.exp(ldt)
    mag = jnp.exp(lre * dt)
    ang = lim * dt
    lbr = mag * jnp.cos(ang)
    lbi = mag * jnp.sin(ang)
    den = lre * lre + lim * lim
    nr = lbr - 1.0
    qr = (nr * lre + lbi * lim) / den
    qi = (lbi * lre - nr * lim) / den
    return lbr, lbi, qr * bre_t - qi * bim_t, qr * bim_t + qi * bre_t


def _s5_tables(lbr, lbi, reverse):
    if reverse:
        lbi = -lbi
    pw = [(lbr, lbi)]
    for _ in range(7):
        r, i = pw[-1]
        pw.append((r * lbr - i * lbi, r * lbi + i * lbr))
    row = lax.broadcasted_iota(jnp.int32, (8, S5_N), 0)
    tabs = []
    for k in range(3):
        sh = 2 ** k
        mask = (row < 8 - sh) if reverse else (row >= sh)
        ar, ai = pw[sh - 1]
        tabs.append((jnp.where(mask, ar, 0.0), jnp.where(mask, ai, 0.0)))
    pr = jnp.zeros((8, S5_N), F32)
    pi = jnp.zeros((8, S5_N), F32)
    for i in range(8):
        ar, ai = pw[7 - i] if reverse else pw[i]
        pr = jnp.where(row == i, ar, pr)
        pi = jnp.where(row == i, ai, pi)
    tabs.append((pr, pi))
    return tabs


def _store_tables(tab_ref, tabs):
    for k, (r, i) in enumerate(tabs):
        tab_ref[2 * k] = r
        tab_ref[2 * k + 1] = i


def _bd_mask():
    r = lax.broadcasted_iota(jnp.int32, (S5_W, S5_N), 0)
    c = lax.broadcasted_iota(jnp.int32, (S5_W, S5_N), 1)
    return jnp.right_shift(r, 4) == jnp.right_shift(c, 6)


def _s5_block_diag(bbr_t, bbi_t, cre_w, cim_w):
    mask = _bd_mask()
    bd = lambda t: jnp.where(mask, t, 0.0)
    return (bd(jnp.tile(bbr_t, (S5_G, 1))), bd(jnp.tile(bbi_t, (S5_G, 1))),
            bd(jnp.tile(cre_w, (1, S5_N // HEAD))), bd(jnp.tile(cim_w, (1, S5_N // HEAD))))


def _scan8(xr, xi, tab_ref, lanes, reverse):
    for k in range(3):
        sh = (8 - 2 ** k) if reverse else 2 ** k
        sr = pltpu.roll(xr, sh, 0)
        si = pltpu.roll(xi, sh, 0)
        mr = tab_ref[2 * k, :, lanes]
        mi = tab_ref[2 * k + 1, :, lanes]
        xr, xi = xr + (mr * sr - mi * si), xi + (mr * si + mi * sr)
    return xr, xi


S5_LANES = 256


def _gelu(y):
    t = jnp.tanh(GELU_C * (y + GELU_A * y * y * y))
    return 0.5 * y * (1.0 + t), t


def _s5_fwd(u, lre, lim, ldt, bre_t, bim_t, cre_w, cim_w, d_row, w_glu, b_glu, jobs=()):
    tp = u.shape[0]
    R = ROW_BLK

    def body(u_ref, lre_ref, lim_ref, ldt_ref, bre_ref, bim_ref, cre_ref, cim_ref, d_ref, wg_ref, bg_ref,
             y_ref, xr_ref, xi_ref, bbd_r, bbd_i, cbd_r, cbd_i, tab_ref, car_r, car_i):
        @pl.when(pl.program_id(0) == 0)
        def _():
            lbr, lbi, bbr, bbi = _s5_disc(lre_ref[...], lim_ref[...], ldt_ref[...], bre_ref[...], bim_ref[...])
            br, bi, cr, ci = _s5_block_diag(bbr, bbi, cre_ref[...], cim_ref[...])
            bbd_r[...] = br.astype(MM)
            bbd_i[...] = bi.astype(MM)
            cbd_r[...] = cr.astype(MM)
            cbd_i[...] = ci.astype(MM)
            _store_tables(tab_ref, _s5_tables(lbr, lbi, False))
            car_r[...] = jnp.zeros_like(car_r)
            car_i[...] = jnp.zeros_like(car_i)

        u = u_ref[...]
        ub = u.astype(MM)
        xr_ref[...] = jnp.dot(ub, bbd_r[...], preferred_element_type=F32)
        xi_ref[...] = jnp.dot(ub, bbd_i[...], preferred_element_type=F32)
        for j in range(S5_N // S5_LANES):
            lanes = pl.ds(j * S5_LANES, S5_LANES)
            pr = tab_ref[6, :, lanes]
            pi = tab_ref[7, :, lanes]

            def step(g, carry):
                cr, ci = carry
                rows = pl.ds(pl.multiple_of(g * 8, 8), 8)
                xr, xi = _scan8(xr_ref[rows, lanes], xi_ref[rows, lanes], tab_ref, lanes, False)
                br = jnp.broadcast_to(cr[7:8, :], cr.shape)
                bi = jnp.broadcast_to(ci[7:8, :], ci.shape)
                xr = xr + (pr * br - pi * bi)
                xi = xi + (pr * bi + pi * br)
                xr_ref[rows, lanes] = xr
                xi_ref[rows, lanes] = xi
                return xr, xi

            cr, ci = lax.fori_loop(0, R // 8, step, (car_r[:, lanes], car_i[:, lanes]), unroll=2)
            car_r[:, lanes] = cr
            car_i[:, lanes] = ci
        y = _dot_nt(xr_ref[...], cbd_r[...]) - _dot_nt(xi_ref[...], cbd_i[...]) + d_ref[...] * u
        yg, _ = _gelu(y)
        z = _dot(yg, wg_ref[...]) + bg_ref[...]
        y_ref[...] = yg * jax.nn.sigmoid(z)

    full = lambda a: pl.BlockSpec(a.shape, lambda i: (0,) * a.ndim)
    small = [lre, lim, ldt, bre_t, bim_t, cre_w, cim_w, d_row, w_glu, b_glu]
    return _call(
        body, "s5_fwd", (tp // R,),
        [pl.BlockSpec((R, S5_W), lambda i: (i, 0))] + [full(a) for a in small],
        [pl.BlockSpec((R, S5_W), lambda i: (i, 0)), pl.BlockSpec((R, S5_N), lambda i: (i, 0)),
         pl.BlockSpec((R, S5_N), lambda i: (i, 0))],
        [jax.ShapeDtypeStruct((tp, S5_W), F32), jax.ShapeDtypeStruct((tp, S5_N), F32),
         jax.ShapeDtypeStruct((tp, S5_N), F32)],
        [pltpu.VMEM((S5_W, S5_N), MM)] * 4 + [pltpu.VMEM((8, 8, S5_N), F32), pltpu.VMEM((8, S5_N), F32),
                                              pltpu.VMEM((8, S5_N), F32)],
        (u, *small), jobs)


def _s5_bwd(dy_out, u, xr, xi, lre, lim, ldt, bre_t, bim_t, cre_w, cim_w, d_row, w_glu, b_glu, jobs=()):
    tp = u.shape[0]
    R = ROW_BLK
    nb = tp // R

    def body(dyo_ref, u_ref, xr_ref, xi_ref, xpr_ref, xpi_ref,
             lre_ref, lim_ref, ldt_ref, bre_ref, bim_ref, cre_ref, cim_ref, d_ref, wg_ref, bg_ref,
             du_ref, dlre_ref, dlim_ref, dldt_ref, dbre_ref, dbim_ref, dcre_ref, dcim_ref, dd_ref, dwg_ref, dbg_ref,
             bbd_r, bbd_i, cbd_r, cbd_i, tab_ref, car_r, car_i, gr_ref, gi_ref, xer_ref, xei_ref,
             abr, abi, acr, aci, adr, adi):
        i = pl.program_id(0)

        @pl.when(i == 0)
        def _():
            lbr, lbi, bbr, bbi = _s5_disc(lre_ref[...], lim_ref[...], ldt_ref[...], bre_ref[...], bim_ref[...])
            br, bi, cr, ci = _s5_block_diag(bbr, bbi, cre_ref[...], cim_ref[...])
            bbd_r[...] = br.astype(MM)
            bbd_i[...] = bi.astype(MM)
            cbd_r[...] = cr.astype(MM)
            cbd_i[...] = ci.astype(MM)
            _store_tables(tab_ref, _s5_tables(lbr, lbi, True))
            for ref in (car_r, car_i, abr, abi, acr, aci, adr, adi, dd_ref, dwg_ref, dbg_ref):
                ref[...] = jnp.zeros_like(ref)

        u = u_ref[...]
        xrv = xr_ref[...]
        xiv = xi_ref[...]
        y = _dot_nt(xrv, cbd_r[...]) - _dot_nt(xiv, cbd_i[...]) + d_ref[...] * u
        yg, t = _gelu(y)
        z = _dot(yg, wg_ref[...]) + bg_ref[...]
        s = jax.nn.sigmoid(z)
        dout = dyo_ref[...]
        dz = dout * yg * s * (1.0 - s)
        dyg = dout * s + _dot_nt(dz, wg_ref[...])
        dwg_ref[...] += _dot_tn(yg, dz)
        dbg_ref[...] += _colsum(dz)
        dy = dyg * (0.5 * (1.0 + t) + 0.5 * y * (1.0 - t * t) * GELU_C * (1.0 + 3.0 * GELU_A * y * y))
        dd_ref[...] += _colsum(dy * u)
        acr[...] += _dot_tn(dy, xrv)
        aci[...] -= _dot_tn(dy, xiv)
        gr_ref[...] = _dot(dy, cbd_r[...])
        gi_ref[...] = -_dot(dy, cbd_i[...])
        has_prev = (i < nb - 1).astype(F32)
        xer_ref[0:8, :] = xpr_ref[...] * has_prev
        xei_ref[0:8, :] = xpi_ref[...] * has_prev
        xer_ref[8:R + 8, :] = xrv
        xei_ref[8:R + 8, :] = xiv
        row = lax.broadcasted_iota(jnp.int32, (8, S5_LANES), 0)
        for j in range(S5_N // S5_LANES):
            lanes = pl.ds(j * S5_LANES, S5_LANES)
            pr = tab_ref[6, :, lanes]
            pi = tab_ref[7, :, lanes]

            def step(n, carry):
                cr, ci, sar, sai = carry
                g = R // 8 - 1 - n
                r0 = pl.multiple_of(g * 8, 8)
                rows = pl.ds(r0, 8)
                gr, gi = _scan8(gr_ref[rows, lanes], gi_ref[rows, lanes], tab_ref, lanes, True)
                br = jnp.broadcast_to(cr[0:1, :], cr.shape)
                bi = jnp.broadcast_to(ci[0:1, :], ci.shape)
                gr = gr + (pr * br - pi * bi)
                gi = gi + (pr * bi + pi * br)
                gr_ref[rows, lanes] = gr
                gi_ref[rows, lanes] = gi
                last = row == 7
                xpr = pltpu.roll(jnp.where(last, xer_ref[rows, lanes], xer_ref[pl.ds(r0 + 8, 8), lanes]), 1, 0)
                xpi = pltpu.roll(jnp.where(last, xei_ref[rows, lanes], xei_ref[pl.ds(r0 + 8, 8), lanes]), 1, 0)
                return gr, gi, sar + (gr * xpr + gi * xpi), sai + (gi * xpr - gr * xpi)

            cr, ci, sar, sai = lax.fori_loop(
                0, R // 8, step, (car_r[:, lanes], car_i[:, lanes], adr[:, lanes], adi[:, lanes]), unroll=2)
            car_r[:, lanes] = cr
            car_i[:, lanes] = ci
            adr[:, lanes] = sar
            adi[:, lanes] = sai
        grv = gr_ref[...]
        giv = gi_ref[...]
        du_ref[...] = (dy * d_ref[...] + _dot_nt(grv, bbd_r[...]) + _dot_nt(giv, bbd_i[...])).astype(du_ref.dtype)
        abr[...] += _dot_tn(u, grv)
        abi[...] += _dot_tn(u, giv)

        @pl.when(i == nb - 1)
        def _():
            mask = _bd_mask()
            r16 = lax.broadcasted_iota(jnp.int32, (S5_H, S5_W), 1)
            h16 = lax.broadcasted_iota(jnp.int32, (S5_H, S5_W), 0)
            fold_b = jnp.bitwise_and(r16, S5_H - 1) == h16
            c64 = lax.broadcasted_iota(jnp.int32, (S5_N, S5_P), 0)
            p64 = lax.broadcasted_iota(jnp.int32, (S5_N, S5_P), 1)
            fold_c = jnp.bitwise_and(c64, S5_P - 1) == p64
            dbbr = _dot_sel_lhs(fold_b, jnp.where(mask, abr[...], 0.0))
            dbbi = _dot_sel_lhs(fold_b, jnp.where(mask, abi[...], 0.0))
            dcre_ref[...] = _dot_sel_rhs(jnp.where(mask, acr[...], 0.0), fold_c)
            dcim_ref[...] = _dot_sel_rhs(jnp.where(mask, aci[...], 0.0), fold_c)
            dlbr = _colsum(adr[...])
            dlbi = _colsum(adi[...])
            _, vjp = jax.vjp(_s5_disc, lre_ref[...], lim_ref[...], ldt_ref[...], bre_ref[...], bim_ref[...])
            dlre, dlim, dldt, dbre, dbim = vjp((dlbr, dlbi, dbbr, dbbi))
            dlre_ref[...] = dlre
            dlim_ref[...] = dlim
            dbre_ref[...] = dbre
            dbim_ref[...] = dbim
            gsel = jnp.right_shift(lax.broadcasted_iota(jnp.int32, (S5_N, HEAD), 0), 6) == \
                lax.broadcasted_iota(jnp.int32, (S5_N, HEAD), 1)
            dldt_ref[...] = _dot_sel_rhs(dldt, gsel)

    full = lambda a: pl.BlockSpec(a.shape, lambda i: (0,) * a.ndim)
    rev = lambda w: pl.BlockSpec((R, w), lambda i: (nb - 1 - i, 0))
    prev8 = pl.BlockSpec((8, S5_N), lambda i: (jnp.maximum((nb - 1 - i) * (R // 8) - 1, 0), 0))
    small = [lre, lim, ldt, bre_t, bim_t, cre_w, cim_w, d_row, w_glu, b_glu]
    outs = [((tp, S5_W), rev(S5_W))] + [
        (s, pl.BlockSpec(s, lambda i: (0, 0))) for s in
        [(1, S5_N), (1, S5_N), (1, HEAD), (S5_H, S5_N), (S5_H, S5_N), (S5_W, S5_P), (S5_W, S5_P),
         (1, S5_W), (S5_W, S5_W), (1, S5_W)]]
    return _call(
        body, "s5_bwd", (nb,),
        [rev(S5_W), rev(S5_W), rev(S5_N), rev(S5_N), prev8, prev8] + [full(a) for a in small],
        [o[1] for o in outs], [jax.ShapeDtypeStruct(o[0], MM if n == 0 else F32) for n, o in enumerate(outs)],
        [pltpu.VMEM((S5_W, S5_N), MM)] * 4 + [
            pltpu.VMEM((8, 8, S5_N), F32), pltpu.VMEM((8, S5_N), F32), pltpu.VMEM((8, S5_N), F32),
            pltpu.VMEM((R, S5_N), F32), pltpu.VMEM((R, S5_N), F32),
            pltpu.VMEM((R + 8, S5_N), F32), pltpu.VMEM((R + 8, S5_N), F32)] + [pltpu.VMEM((S5_W, S5_N), F32)] * 4 + [
            pltpu.VMEM((8, S5_N), F32), pltpu.VMEM((8, S5_N), F32)],
        (dy_out, u, xr, xi, xr, xi, *small), jobs)


RET_CHUNK = ROW_BLK
LOG_GAMMA = [math.log1p(-2.0 ** (-5 - h)) for h in range(RET_H)]
GAMMA_CHUNK = [math.exp(RET_CHUNK * lg) for lg in LOG_GAMMA]
_DECAY_SCRATCH = [pltpu.VMEM((RET_H, RET_CHUNK, RET_CHUNK), F32), pltpu.VMEM((RET_H, RET_CHUNK, HEAD), F32),
                  pltpu.VMEM((RET_H, RET_CHUNK, HEAD), F32)]


def _fill_decay(dm_ref, ze_ref, xi_ref):
    C = RET_CHUNK
    diff = (lax.broadcasted_iota(jnp.int32, (C, C), 0) - lax.broadcasted_iota(jnp.int32, (C, C), 1)).astype(F32)
    r = lax.broadcasted_iota(jnp.int32, (C, HEAD), 0).astype(F32)
    for h, lg in enumerate(LOG_GAMMA):
        dm_ref[h] = jnp.where(diff >= 0.0, jnp.exp(jnp.maximum(diff, 0.0) * lg), 0.0)
        ze_ref[h] = jnp.exp((C - 1.0 - r) * lg)
        xi_ref[h] = jnp.exp((r + 1.0) * lg)


def _ret_fwd(q, k, v, jobs=()):
    tp = q.shape[0]
    C = RET_CHUNK
    nc = tp // C

    def body(q_ref, k_ref, v_ref, o_ref, st_ref, s_ref, dm_ref, ze_ref, xi_ref):
        @pl.when(pl.program_id(0) == 0)
        def _():
            s_ref[...] = jnp.zeros_like(s_ref)
            _fill_decay(dm_ref, ze_ref, xi_ref)

        for h in range(RET_H):
            sl = slice(h * HEAD, (h + 1) * HEAD)
            qh, kh, vh = q_ref[:, sl], k_ref[:, sl], v_ref[:, sl]
            sh = s_ref[h]
            st_ref[0, sl, :] = sh
            scores = _dot_nt(qh, kh) * dm_ref[h]
            o_ref[:, sl] = _dot(scores, vh) + _dot(qh, sh) * xi_ref[h]
            s_ref[h] = GAMMA_CHUNK[h] * sh + _dot_tn(kh.astype(F32) * ze_ref[h], vh)

    blk = pl.BlockSpec((C, RET_W), lambda c: (c, 0))
    return _call(
        body, "ret_fwd", (nc,), [blk, blk, blk], [blk, pl.BlockSpec((1, RET_W, HEAD), lambda c: (c, 0, 0))],
        [jax.ShapeDtypeStruct((tp, RET_W), F32), jax.ShapeDtypeStruct((nc, RET_W, HEAD), F32)],
        [pltpu.VMEM((RET_H, HEAD, HEAD), F32)] + _DECAY_SCRATCH, (q, k, v), jobs)


def _ret_bwd(q, k, v, do, states, cos2, sin2, jobs=()):
    tp = q.shape[0]
    C = RET_CHUNK
    nc = tp // C

    def body(q_ref, k_ref, v_ref, do_ref, st_ref, cos_ref, sin_ref,
             dq_ref, dk_ref, dv_ref, ds_ref, dm_ref, ze_ref, xi_ref):
        @pl.when(pl.program_id(0) == 0)
        def _():
            ds_ref[...] = jnp.zeros_like(ds_ref)
            _fill_decay(dm_ref, ze_ref, xi_ref)

        cos = cos_ref[...]
        sin = sin_ref[...]
        for h in range(RET_H):
            sl = slice(h * HEAD, (h + 1) * HEAD)
            qh, kh, vh = q_ref[:, sl], k_ref[:, sl], v_ref[:, sl]
            dmh = dm_ref[h]
            sh = st_ref[0, sl, :]
            dsn = ds_ref[h]
            doh = do_ref[:, sl]
            dox = doh * xi_ref[h]
            a = _dot_nt(qh, kh) * dmh
            dqk = _dot_nt(doh, vh) * dmh
            kz = kh.astype(F32) * ze_ref[h]
            dv_ref[:, sl] = (_dot_tn(a, doh) + _dot(kz, dsn)).astype(dv_ref.dtype)
            dqr = _dot(dqk, kh) + _dot_nt(dox, sh)
            dkr = _dot_tn(dqk, qh) + ze_ref[h] * _dot_nt(vh, dsn)
            ds_ref[h] = GAMMA_CHUNK[h] * dsn + _dot_tn(qh, dox)
            dq_ref[:, sl] = (dqr * cos - pltpu.roll(dqr, HEAD // 2, 1) * sin).astype(dq_ref.dtype)
            dk_ref[:, sl] = ((dkr * cos - pltpu.roll(dkr, HEAD // 2, 1) * sin) * (HEAD ** -0.5)).astype(dk_ref.dtype)

    blk = pl.BlockSpec((C, RET_W), lambda c: (nc - 1 - c, 0))
    tab = pl.BlockSpec((C, HEAD), lambda c: (nc - 1 - c, 0))
    return _call(
        body, "ret_bwd", (nc,),
        [blk, blk, blk, blk, pl.BlockSpec((1, RET_W, HEAD), lambda c: (nc - 1 - c, 0, 0)), tab, tab],
        [blk, blk, blk], [jax.ShapeDtypeStruct((tp, RET_W), MM)] * 3,
        [pltpu.VMEM((RET_H, HEAD, HEAD), F32)] + _DECAY_SCRATCH, (q, k, v, do, states, cos2, sin2), jobs)


def _gn_gate(o, gate, gn_g, gn_b):
    xhat, rstd = _ln_fwd(o, GN_EPS)
    on = xhat * gn_g + gn_b
    s = jax.nn.sigmoid(gate)
    return gate * s * on, xhat, rstd, on, s


def _post_up(o, gate, ys5, xhat0, gn_g, gn_b, li_g, li_b, l1_g, l1_b, w_out, w_up, jobs=()):
    tp = o.shape[0]
    R = ROW_BLK

    def body(o_ref, g_ref, ys_ref, xh0_ref, gng, gnb, lig, lib, l1g, l1b, wo_ref, wu_ref,
             ycat_ref, xh1_ref, rstd1_ref, h1b_ref, pre_ref):
        ycat_ref[:, 0:S5_W] = ys_ref[...].astype(ycat_ref.dtype)
        for h in range(RET_H):
            sl = slice(h * HEAD, (h + 1) * HEAD)
            yret = _gn_gate(o_ref[:, sl], g_ref[:, sl], gng[:, sl], gnb[:, sl])[0]
            ycat_ref[:, S5_W + h * HEAD:S5_W + (h + 1) * HEAD] = yret.astype(ycat_ref.dtype)
        mixed = _dot(ycat_ref[...], wo_ref[...])
        h0 = xh0_ref[...] * lig[...] + lib[...]
        xh1, rstd1 = _ln_fwd(ALPHA * h0 + mixed, LN_EPS)
        xh1_ref[...] = xh1
        rstd1_ref[...] = rstd1
        h1b = (xh1 * l1g[...] + l1b[...]).astype(MM)
        h1b_ref[...] = h1b
        for d in range(N_DEV):
            pre_ref[:, d * FF_BLK:(d + 1) * FF_BLK] = jnp.maximum(_dot(h1b, wu_ref[d]), 0.0)

    row = lambda w: pl.BlockSpec((R, w), lambda i: (i, 0))
    full = lambda a: pl.BlockSpec(a.shape, lambda i: (0,) * a.ndim)
    vecs = [gn_g, gn_b, li_g, li_b, l1_g, l1_b]
    outs = [(row(D_MODEL), jax.ShapeDtypeStruct((tp, D_MODEL), MM)), (row(D_MODEL), jax.ShapeDtypeStruct((tp, D_MODEL), F32)),
            (row(1), jax.ShapeDtypeStruct((tp, 1), F32)), (row(D_MODEL), jax.ShapeDtypeStruct((tp, D_MODEL), MM)),
            (row(D_FF), jax.ShapeDtypeStruct((tp, D_FF), F32))]
    return _call(
        body, "post_up", (tp // R,),
        [row(RET_W), row(RET_W), row(S5_W), row(D_MODEL)] + [full(a) for a in vecs] + [_VMEM, _VMEM],
        [o[0] for o in outs], [o[1] for o in outs], [], (o, gate, ys5, xhat0, *vecs, w_out, w_up), jobs)


def _post_down(pre, xhat1, tgt, l1_g, l1_b, l2_g, l2_b, w_down):
    tp = pre.shape[0]
    seq = tgt.shape[0]
    R = ROW_BLK

    def body(pre_ref, xh1_ref, ta, tb, tc, l1g, l1b, l2g, l2b, wd_ref,
             dr2_ref, dffb_ref, loss_ref, dl2g_ref, dl2b_ref, tgt_ref):
        i = pl.program_id(0)

        @pl.when(i == 0)
        def _():
            for ref in (loss_ref, dl2g_ref, dl2b_ref):
                ref[...] = jnp.zeros_like(ref)

        tgt_ref[0:CHUNK, :] = ta[...]
        tgt_ref[CHUNK:2 * CHUNK, :] = tb[...]
        tgt_ref[2 * CHUNK:3 * CHUNK, :] = tc[...]
        ff = jnp.zeros((R, D_MODEL), F32)
        for d in range(N_DEV):
            pre = pre_ref[:, d * FF_BLK:(d + 1) * FF_BLK]
            ff = ff + _dot(pre * pre, wd_ref[d * FF_BLK:(d + 1) * FF_BLK, :])
        h1 = xh1_ref[...] * l1g[...] + l1b[...]
        xh2, rstd2 = _ln_fwd(ALPHA * h1 + ff, LN_EPS)
        h2 = xh2 * l2g[...] + l2b[...]
        valid = (i * R + lax.broadcasted_iota(jnp.int32, (R, 1), 0)) >= CHUNK
        err = jnp.where(valid, h2 - tgt_ref[...], 0.0)
        loss_ref[...] += 0.5 * jnp.sum(err * err) / D_MODEL
        dh2 = err * (1.0 / D_MODEL)
        dl2g_ref[...] += _colsum(dh2 * xh2)
        dl2b_ref[...] += _colsum(dh2)
        dr2 = _ln_bwd(dh2 * l2g[...], xh2, rstd2)
        dr2_ref[...] = dr2
        dffb_ref[...] = dr2.astype(MM)

    row = lambda w: pl.BlockSpec((R, w), lambda i: (i, 0))
    full = lambda a: pl.BlockSpec(a.shape, lambda i: (0,) * a.ndim)
    vecs = [l1_g, l1_b, l2_g, l2_b]
    acc = lambda s: (pl.BlockSpec(s, lambda i: (0, 0)), jax.ShapeDtypeStruct(s, F32))
    outs = [(row(D_MODEL), jax.ShapeDtypeStruct((tp, D_MODEL), F32)), (row(D_MODEL), jax.ShapeDtypeStruct((tp, D_MODEL), MM)),
            acc((8, HEAD)), acc((1, D_MODEL)), acc((1, D_MODEL))]
    return pl.pallas_call(
        body, name="post_down", grid=(tp // R,),
        in_specs=[row(D_FF), row(D_MODEL)] + _shift3(seq // CHUNK) + [full(a) for a in vecs] + [_VMEM],
        out_specs=[o[0] for o in outs], out_shape=[o[1] for o in outs],
        scratch_shapes=[pltpu.VMEM((R, D_MODEL), F32)],
        compiler_params=_params(("arbitrary",)),
    )(pre, xhat1, tgt, tgt, tgt, *vecs, w_down)


def _mlp_bwd(h1b, dffb, pre, w_up, w_down):
    tp = h1b.shape[0]
    R = MLP_ROWS if tp % MLP_ROWS == 0 else ROW_BLK
    nr = tp // R

    def body(h_ref, df_ref, pre_ref, wu_ref, wd_ref, gup_ref, gdn_ref, dh1_ref, aup, adn):
        d = pl.program_id(0)
        r = pl.program_id(1)

        @pl.when(r == 0)
        def _():
            aup[...] = jnp.zeros_like(aup)
            adn[...] = jnp.zeros_like(adn)

        h = h_ref[...]
        df = df_ref[...]
        wu = wu_ref[0]
        wd = wd_ref[0]
        pre = pre_ref[...]
        dpre = (_dot_nt(df, wd) * (2.0 * pre)).astype(MM)

        aup[...] += _dot_tn(h, dpre)
        adn[...] += _dot_tn(pre * pre, df)
        contrib = _dot_nt(dpre, wu)
        rows = pl.ds(pl.multiple_of(r * R, 64), R)

        @pl.when(d == 0)
        def _():
            dh1_ref[rows, :] = contrib

        @pl.when(d > 0)
        def _():
            dh1_ref[rows, :] += contrib

        @pl.when(r == nr - 1)
        def _():
            gup_ref[0] = aup[...].astype(gup_ref.dtype)
            gdn_ref[0] = adn[...].astype(gdn_ref.dtype)

    return pl.pallas_call(
        body, name="mlp_bwd", grid=(N_DEV, nr),
        in_specs=[pl.BlockSpec((R, D_MODEL), lambda d, r: (r, 0)), pl.BlockSpec((R, D_MODEL), lambda d, r: (r, 0)),
                  pl.BlockSpec((R, FF_BLK), lambda d, r: (r, d)),
                  pl.BlockSpec((1, D_MODEL, FF_BLK), lambda d, r: (d, 0, 0)),
                  pl.BlockSpec((1, FF_BLK, D_MODEL), lambda d, r: (d, 0, 0))],
        out_specs=[pl.BlockSpec((1, D_MODEL, FF_BLK), lambda d, r: (d, 0, 0)),
                   pl.BlockSpec((1, FF_BLK, D_MODEL), lambda d, r: (d, 0, 0)), _VMEM],
        out_shape=[jax.ShapeDtypeStruct((N_DEV, D_MODEL, FF_BLK), MM), jax.ShapeDtypeStruct((N_DEV, FF_BLK, D_MODEL), MM),
                   jax.ShapeDtypeStruct((tp, D_MODEL), F32)],
        scratch_shapes=[pltpu.VMEM((D_MODEL, FF_BLK), F32), pltpu.VMEM((FF_BLK, D_MODEL), F32)],
        compiler_params=_params(("arbitrary", "arbitrary")),
    )(h1b, dffb, pre, w_up, w_down.reshape(N_DEV, FF_BLK, D_MODEL))


def _post_bwd(dh1m, dr2, xhat1, rstd1, ycat, o, gate, gn_g, gn_b, l1_g, w_out, jobs=()):
    tp = o.shape[0]
    R = ROW_BLK
    nb = tp // R

    def body(dm_ref, dr2_ref, xh1_ref, rs1_ref, yc_ref, o_ref, g_ref, gng, gnb, l1g, wo_ref,
             do_ref, dg_ref, dys_ref, dh0_ref, gwo_ref, dl1g_ref, dl1b_ref, dgng_ref, dgnb_ref, awo):
        i = pl.program_id(0)

        @pl.when(i == 0)
        def _():
            for ref in (awo, dl1g_ref, dl1b_ref, dgng_ref, dgnb_ref):
                ref[...] = jnp.zeros_like(ref)

        dh1 = dm_ref[...] + ALPHA * dr2_ref[...]
        xh1 = xh1_ref[...]
        dl1g_ref[...] += _colsum(dh1 * xh1)
        dl1b_ref[...] += _colsum(dh1)
        dr1 = _ln_bwd(dh1 * l1g[...], xh1, rs1_ref[...])
        dh0_ref[...] = ALPHA * dr1
        dmix = dr1.astype(MM)
        awo[...] += _dot_tn(yc_ref[...], dmix)
        dyc = _dot_nt(dmix, wo_ref[...])
        dys_ref[...] = dyc[:, 0:S5_W]
        for h in range(RET_H):
            sl = slice(h * HEAD, (h + 1) * HEAD)
            gt = g_ref[:, sl]
            _, xhat, rstd, on, s = _gn_gate(o_ref[:, sl], gt, gng[:, sl], gnb[:, sl])
            dyr = dyc[:, S5_W + h * HEAD:S5_W + (h + 1) * HEAD]
            dg_ref[:, sl] = (dyr * on * (s * (1.0 + gt * (1.0 - s)))).astype(dg_ref.dtype)
            don = dyr * gt * s
            dgng_ref[:, sl] += _colsum(don * xhat)
            dgnb_ref[:, sl] += _colsum(don)
            do_ref[:, sl] = _ln_bwd(don * gng[:, sl], xhat, rstd)

        @pl.when(i == nb - 1)
        def _():
            gwo_ref[...] = awo[...].astype(gwo_ref.dtype)

    row = lambda w: pl.BlockSpec((R, w), lambda i: (i, 0))
    full = lambda a: pl.BlockSpec(a.shape, lambda i: (0,) * a.ndim)
    acc = lambda s, dt=F32: (pl.BlockSpec(s, lambda i: (0, 0)), jax.ShapeDtypeStruct(s, dt))
    outs = [(row(RET_W), jax.ShapeDtypeStruct((tp, RET_W), F32)), (row(RET_W), jax.ShapeDtypeStruct((tp, RET_W), MM)),
            (row(S5_W), jax.ShapeDtypeStruct((tp, S5_W), F32)), (row(D_MODEL), jax.ShapeDtypeStruct((tp, D_MODEL), F32)),
            acc((D_MODEL, D_MODEL), MM), acc((1, D_MODEL)), acc((1, D_MODEL)), acc((1, RET_W)), acc((1, RET_W))]
    return _call(
        body, "post_bwd", (nb,),
        [row(D_MODEL), row(D_MODEL), row(D_MODEL), row(1), row(D_MODEL), row(RET_W), row(RET_W),
         full(gn_g), full(gn_b), full(l1_g), _VMEM],
        [o[0] for o in outs], [o[1] for o in outs],
        [pltpu.VMEM((D_MODEL, D_MODEL), F32)],
        (dh1m, dr2, xhat1, rstd1, ycat, o, gate, gn_g, gn_b, l1_g, w_out), jobs)


def _in_bwd(du, dq, dk, dv, dg, dh0r, xhat0, rstd0, li_g, li_b, w_int, jobs=()):
    tp = du.shape[0]
    R = PROJ_ROWS if tp % PROJ_ROWS == 0 else ROW_BLK
    nb = tp // R
    segs = [(0, S5_W)] + [(S5_W + n * RET_W, S5_W + (n + 1) * RET_W) for n in range(4)]

    def body(du_ref, dq_ref, dk_ref, dv_ref, dg_ref, dh0r_ref, xh_ref, rs_ref, lig, lib, w_ref,
             gx_ref, dmeta_ref, gw_ref, dlg_ref, dlb_ref, aw, stage, out_sems):
        i = pl.program_id(0)
        slot = i % 2

        def to_gx(step_slot, first):
            if first:
                return pltpu.make_async_copy(stage.at[0, CHUNK:R, :], gx_ref.at[0:R - CHUNK, :], out_sems.at[0])
            return pltpu.make_async_copy(stage.at[step_slot], gx_ref.at[pl.ds(i * R - CHUNK, R), :], out_sems.at[step_slot])

        @pl.when(i == 0)
        def _():
            for ref in (aw, dlg_ref, dlb_ref):
                ref[...] = jnp.zeros_like(ref)

        @pl.when(i >= 3)
        def _():
            to_gx(slot, False).wait()

        valid = (i * R + lax.broadcasted_iota(jnp.int32, (R, 1), 0)) >= PAD
        xh = xh_ref[...]
        hb = (xh * lig[...] + lib[...]).astype(MM)
        dh0 = dh0r_ref[...]
        for (lo, hi), ref in zip(segs, (du_ref, dq_ref, dk_ref, dv_ref, dg_ref)):
            dseg = jnp.where(valid, ref[...], 0.0).astype(MM)
            dh0 = dh0 + _dot(dseg, w_ref[lo:hi, :])
            aw[lo:hi, :] += _dot_tn(dseg, hb)
        dlg_ref[...] += _colsum(dh0 * xh)
        dlb_ref[...] += _colsum(dh0)
        draw = _ln_bwd(dh0 * lig[...], xh, rs_ref[...])
        stage[slot] = draw

        @pl.when(i == 0)
        def _():
            dmeta_ref[...] = draw[PAD:CHUNK, :]
            first = to_gx(0, True)
            first.start()
            first.wait()

        @pl.when(i > 0)
        def _():
            to_gx(slot, False).start()

        @pl.when(i == nb - 1)
        def _():
            gw_ref[...] = aw[...].astype(gw_ref.dtype)
            for back in (1, 0):
                if nb - 1 - back >= 1:
                    to_gx((nb - 1 - back) % 2, False).wait()

    row = lambda w: pl.BlockSpec((R, w), lambda i: (i, 0))
    full = lambda a: pl.BlockSpec(a.shape, lambda i: (0,) * a.ndim)
    acc = lambda s, dt=F32: (pl.BlockSpec(s, lambda i: (0, 0)), jax.ShapeDtypeStruct(s, dt))
    outs = [(_ANY, jax.ShapeDtypeStruct((tp - CHUNK, D_MODEL), F32)), acc((N_META, D_MODEL)), acc((PROJ_W, D_MODEL), MM),
            acc((1, D_MODEL)), acc((1, D_MODEL))]
    return _call(
        body, "in_bwd", (nb,),
        [row(S5_W), row(RET_W), row(RET_W), row(RET_W), row(RET_W), row(D_MODEL), row(D_MODEL), row(1),
         full(li_g), full(li_b), _VMEM],
        [o[0] for o in outs], [o[1] for o in outs],
        [pltpu.VMEM((PROJ_W, D_MODEL), F32), pltpu.VMEM((2, R, D_MODEL), F32), pltpu.SemaphoreType.DMA((2,))],
        (du, dq, dk, dv, dg, dh0r, xhat0, rstd0, li_g, li_b, w_int), jobs)


def _place():
    return lax.axis_index("x"), lax.axis_index("y"), lax.axis_index("c")


def _dma_sems(n):
    return pltpu.SemaphoreType.DMA((n,))


def _job_gather(shard):
    def parts(ins, outs, sems):
        (src,), (out,), (send_sems, recv_sems, local_sem) = ins, outs, sems
        x, y, c = _place()
        north = c == 1
        me, sib = (x, y, c), (x, y, 1 - c)
        xn, yn, dg = (1 - x, y, c), (x, 1 - y, c), (1 - x, 1 - y, c)
        relay_from = (jnp.where(north, 1 - x, x), jnp.where(north, y, 1 - y), c)
        relay_to = (jnp.where(north, x, 1 - x), jnp.where(north, 1 - y, y), c)

        def slot(dev):
            return out.at[4 * dev[0] + 2 * dev[1] + dev[2]]

        def copy(k, block, to, from_input=False):
            return pltpu.make_async_remote_copy(
                src_ref=src if from_input else slot(block), dst_ref=slot(block),
                send_sem=send_sems.at[k], recv_sem=recv_sems.at[k], device_id=to, device_id_type=_MESH)

        mine = lambda: pltpu.make_async_copy(src, slot(me), local_sem.at[0])
        first = lambda: [copy(0, me, sib, True), copy(1, me, xn, True), copy(2, me, yn, True)]
        relayed = lambda: [copy(3, relay_from, relay_to), copy(4, xn, sib), copy(5, yn, sib)]
        return me, sib, xn, yn, dg, copy, mine, first, relayed

    def start(ins, outs, sems):
        mine, first = parts(ins, outs, sems)[6:8]
        mine().start()
        for cp in first():
            cp.start()

    def relay(ins, outs, sems):
        me, sib, xn, yn, dg, copy, mine, first, relayed = parts(ins, outs, sems)
        copy(1, xn, me).wait_recv()
        copy(2, yn, me).wait_recv()
        for cp in relayed():
            cp.start()

    def finish(ins, outs, sems):
        me, sib, xn, yn, dg, copy, mine, first, relayed = parts(ins, outs, sems)
        other = 1 - me[2]
        copy(3, dg, me).wait_recv()
        last = copy(6, dg, sib)
        last.start()
        copy(0, sib, me).wait_recv()
        for k, chip in ((4, xn), (5, yn), (6, dg)):
            copy(k, (chip[0], chip[1], other), me).wait_recv()
        for cp in first() + relayed() + [last]:
            cp.wait_send()
        mine().wait()

    return dict(ins=[shard], outs=[jax.ShapeDtypeStruct((N_DEV,) + shard.shape, shard.dtype)],
                sems=[_dma_sems(7), _dma_sems(7), _dma_sems(1)], start=start, middle=relay, finish=finish)


def _job_pair(g):
    def copies(ins, outs, sems):
        x, y, c = _place()
        return [pltpu.make_async_remote_copy(
            src_ref=ins[0].at[2 * j + (1 - c)], dst_ref=outs[0].at[j], send_sem=sems[0].at[j], recv_sem=sems[1].at[j],
            device_id=(x, y, 1 - c), device_id_type=_MESH) for j in range(4)]

    def start(ins, outs, sems):
        for cp in copies(ins, outs, sems):
            cp.start()

    def finish(ins, outs, sems):
        for cp in copies(ins, outs, sems):
            cp.wait()

    return dict(ins=[g], outs=[jax.ShapeDtypeStruct((4,) + g.shape[1:], g.dtype)], sems=[_dma_sems(4), _dma_sems(4)],
                start=start, finish=finish)


def _job_chips(p):
    def copies(ins, outs, sems):
        x, y, c = _place()
        chips = [(1 - x, y), (x, 1 - y), (1 - x, 1 - y)]
        return [pltpu.make_async_remote_copy(
            src_ref=ins[0].at[2 * chip[0] + chip[1]], dst_ref=outs[0].at[k], send_sem=sems[0].at[k],
            recv_sem=sems[1].at[k], device_id=(*chip, c), device_id_type=_MESH) for k, chip in enumerate(chips)]

    def start(ins, outs, sems):
        for cp in copies(ins, outs, sems):
            cp.start()

    def finish(ins, outs, sems):
        for cp in copies(ins, outs, sems):
            cp.wait()

    return dict(ins=[p], outs=[jax.ShapeDtypeStruct((3,) + p.shape[1:], p.dtype)], sems=[_dma_sems(3), _dma_sems(3)],
                start=start, finish=finish)


def _split_job_refs(jobs, ins, outs, sems):
    res, a, b, c = [], 0, 0, 0
    for job in jobs:
        na, nb, nc = len(job["ins"]), len(job["outs"]), len(job["sems"])
        res.append((ins[a:a + na], outs[b:b + nb], sems[c:c + nc]))
        a, b, c = a + na, b + nb, c + nc
    return res


def _call(body, name, grid, in_specs, out_specs, out_shape, scratch, args, jobs=(), prefetch=None, early=0):
    jobs = list(jobs)
    n_in, n_out, n_scr = len(in_specs), len(out_specs), len(scratch)
    j_in = [a for job in jobs for a in job["ins"]]
    j_out = [o for job in jobs for o in job["outs"]]
    j_scr = [s for job in jobs for s in job["sems"]]
    nsteps = grid[0]
    n_pre = 0 if prefetch is None else 1

    def wrapped(*refs):
        pre, refs = refs[:n_pre], refs[n_pre:]
        ins, jins = refs[:n_in], refs[n_in:n_in + len(j_in)]
        refs = refs[n_in + len(j_in):]
        outs, jouts = refs[:n_out], refs[n_out:n_out + len(j_out)]
        refs = refs[n_out + len(j_out):]
        scr, jscr = refs[:n_scr], refs[n_scr:]
        per_job = _split_job_refs(jobs, jins, jouts, jscr)

        def middle():
            for job, r in zip(jobs, per_job):
                if "middle" in job:
                    job["middle"](*r)

        @pl.when(pl.program_id(0) == 0)
        def _():
            for job, r in zip(jobs, per_job):
                job["start"](*r)

        if nsteps >= 3:
            pl.when(pl.program_id(0) == nsteps // 2)(middle)

        if early:
            @pl.when(pl.program_id(0) == nsteps - 1)
            def _():
                for job, r in zip(jobs[:early], per_job[:early]):
                    job["finish"](*r)

        body(*pre, *ins, *outs, *scr, *[o for r in per_job[:early] for o in r[1]])

        @pl.when(pl.program_id(0) == nsteps - 1)
        def _():
            if nsteps < 3:
                middle()
            for job, r in zip(jobs[early:], per_job[early:]):
                job["finish"](*r)

    specs = dict(in_specs=list(in_specs) + [_ANY] * len(j_in), out_specs=list(out_specs) + [_ANY] * len(j_out),
                 scratch_shapes=list(scratch) + j_scr)
    if n_pre:
        specs = dict(grid_spec=pltpu.PrefetchScalarGridSpec(num_scalar_prefetch=1, grid=grid, **specs))
    else:
        specs["grid"] = grid
    res = pl.pallas_call(
        wrapped if jobs else body, name=name, out_shape=list(out_shape) + j_out,
        compiler_params=_params(("arbitrary",) * len(grid)), **specs,
    )(*([prefetch] if n_pre else []), *args, *j_in)
    return list(res[:n_out]), list(res[n_out:])


def _exchange(jobs, name):
    j_in = [a for job in jobs for a in job["ins"]]
    j_out = [o for job in jobs for o in job["outs"]]
    j_scr = [s for job in jobs for s in job["sems"]]

    def body(*refs):
        per_job = _split_job_refs(jobs, refs[:len(j_in)], refs[len(j_in):len(j_in) + len(j_out)],
                                  refs[len(j_in) + len(j_out):])
        for phase in ("start", "middle", "finish"):
            for job, r in zip(jobs, per_job):
                if phase in job:
                    job[phase](*r)

    return pl.pallas_call(body, name=name, out_shape=j_out, in_specs=[_ANY] * len(j_in), out_specs=[_ANY] * len(j_out),
                          scratch_shapes=j_scr)(*j_in)


def _pair_sum(gs, r1s, c_arr, name):
    n = len(gs)

    def body(c_ref, *refs):
        for a in range(n):
            refs[2 * n + a][...] = (refs[a][...].astype(F32) + refs[n + a][...].astype(F32)).astype(refs[2 * n + a].dtype)

    def blk(g, own):
        s = g.shape[1:]
        if own:
            return pl.BlockSpec((1,) + s, lambda j, c_ref: (2 * j + c_ref[0],) + (0,) * len(s))
        return pl.BlockSpec((1,) + s, lambda j, c_ref: (j,) + (0,) * len(s))

    return pl.pallas_call(
        body, name=name,
        grid_spec=pltpu.PrefetchScalarGridSpec(
            num_scalar_prefetch=1, grid=(4,),
            in_specs=[blk(g, True) for g in gs] + [blk(g, False) for g in gs],
            out_specs=[blk(g, False) for g in gs]),
        out_shape=[jax.ShapeDtypeStruct((4,) + g.shape[1:], g.dtype) for g in gs],
        compiler_params=_params(("arbitrary",)),
    )(c_arr, *gs, *r1s)


def _adamw_math(w, g, m, v):
    m = ADAM_B1 * m + (1.0 - ADAM_B1) * g
    v = ADAM_B2 * v + (1.0 - ADAM_B2) * (g * g)
    m_hat = m / (1.0 - ADAM_B1 ** ADAM_STEP)
    v_hat = v / (1.0 - ADAM_B2 ** ADAM_STEP)
    return -ADAM_LR * (m_hat / (jnp.sqrt(v_hat) + ADAM_EPS) + ADAM_WD * w), m, v


def _view(name, a):
    return jnp.swapaxes(a, -1, -2) if name in ("w_in", "s5_b_re", "s5_b_im") else a


def _adamw_shards(items, name, steps, chip, jobs=()):
    n = len(items)

    def body(chip_ref, *refs):
        for a in range(n):
            p_ref, r_ref, w_ref, m_ref, v_ref = refs[5 * a:5 * a + 5]
            g = ((p_ref[0].astype(F32) + r_ref[0].astype(F32)) + r_ref[1].astype(F32)) + r_ref[2].astype(F32)
            outs = refs[5 * n + 4 * a:5 * n + 4 * a + 4]
            outs[0][...] = g
            outs[1][...], outs[2][...], outs[3][...] = _adamw_math(w_ref[...], g, m_ref[...], v_ref[...])

    in_specs, out_specs, out_shape, flat = [], [], [], []
    for p, r, w, m, v in items:
        rows, cols = w.shape
        rb = rows // steps
        in_specs += [pl.BlockSpec((1, rb, cols), lambda i, c: (c[0], i, 0)), pl.BlockSpec((3, rb, cols), lambda i, c: (0, i, 0))]
        wblk = pl.BlockSpec((rb, cols), lambda i, c: (i, 0))
        in_specs += [wblk] * 3
        out_specs += [wblk] * 4
        out_shape += [jax.ShapeDtypeStruct(w.shape, F32)] * 4
        flat += [p, r, w, m, v]
    return _call(body, name, (steps,), in_specs, out_specs, out_shape, [], flat, jobs, prefetch=chip)


def _sum_devices(gathered, name):
    def body(gs_ref, g_ref):
        g = gs_ref[0]
        for s in range(1, N_DEV):
            g = g + gs_ref[s]
        g_ref[...] = g

    return pl.pallas_call(body, name=name, out_shape=jax.ShapeDtypeStruct(gathered.shape[1:], F32),
                          in_specs=[_VMEM], out_specs=_VMEM, compiler_params=_params())(gathered)


def _adamw_native(items, name):
    n = len(items)

    def body(*refs):
        for a in range(n):
            g, w, m, v = (refs[4 * a + t][...] for t in range(4))
            refs[4 * n + 3 * a][...], refs[4 * n + 3 * a + 1][...], refs[4 * n + 3 * a + 2][...] = _adamw_math(w, g, m, v)

    return pl.pallas_call(
        body, name=name, out_shape=[jax.ShapeDtypeStruct(it[1].shape, F32) for it in items for _ in range(3)],
        in_specs=[_VMEM] * (4 * n), out_specs=[_VMEM] * (3 * n), compiler_params=_params(),
    )(*[t for it in items for t in it])


SMALL = ["ln_in_g", "ln_in_b", "s5_lambda_re", "s5_lambda_im", "s5_log_dt", "s5_b_re", "s5_b_im", "s5_c_re", "s5_c_im",
         "s5_d", "s5_b_glu", "ret_gn_g", "ret_gn_b", "ln1_g", "ln1_b", "ln2_g", "ln2_b"]
LATE = ["ln_in_g", "ln_in_b", "meta_tokens"]
EARLY = [n for n in SMALL if n not in LATE] + ["s5_w_glu", "loss"]
LANE = 128


def _pack(arrs):
    parts = []
    for a in arrs:
        f = a.reshape(-1)
        parts.append(jnp.pad(f, (0, (-f.shape[0]) % LANE)))
    flat = jnp.concatenate(parts)
    rows = -(-flat.shape[0] // LANE)
    flat = jnp.pad(flat, (0, (-rows % 8) * LANE + rows * LANE - flat.shape[0]))
    return flat.reshape(-1, LANE)


def _unpack(packed, shapes):
    flat = packed.reshape(-1)
    out, off = [], 0
    for s in shapes:
        n = math.prod(s)
        out.append(flat[off:off + n].reshape(s))
        off += n + (-n) % LANE
    return out


def _rope_tables(tp):
    inv_freq = 1.0 / (ROPE_BASE ** (jnp.arange(0, HEAD, 2, dtype=F32) / HEAD))
    blk = (jnp.arange(tp // ROW_BLK, dtype=F32) * ROW_BLK)[:, None, None] * inv_freq
    off = (jnp.arange(ROW_BLK, dtype=F32) - float(PAD))[None, :, None] * inv_freq
    cos = (jnp.cos(blk) * jnp.cos(off) - jnp.sin(blk) * jnp.sin(off)).reshape(tp, HEAD // 2)
    sin = (jnp.sin(blk) * jnp.cos(off) + jnp.cos(blk) * jnp.sin(off)).reshape(tp, HEAD // 2)
    return jnp.concatenate([cos, cos], axis=1), jnp.concatenate([-sin, sin], axis=1)


def _local_step(x2d, tgt, meta, w_int, w_out, w_up, w_down, w_glu, sp, distributed):
    tp = x2d.shape[0] + CHUNK
    row = lambda a: a.reshape(1, -1)
    cos2, sin2 = _rope_tables(tp)
    li_g, li_b = row(sp["ln_in_g"]), row(sp["ln_in_b"])
    l1_g, l1_b, l2_g, l2_b = row(sp["ln1_g"]), row(sp["ln1_b"]), row(sp["ln2_g"]), row(sp["ln2_b"])
    gn_g, gn_b = row(sp["ret_gn_g"]), row(sp["ret_gn_b"])
    lre, lim = row(sp["s5_lambda_re"]), row(sp["s5_lambda_im"])
    ldt = row(jnp.repeat(sp["s5_log_dt"].reshape(-1), S5_P))
    to_t = lambda b: b.reshape(S5_G, S5_P, S5_H).transpose(2, 0, 1).reshape(S5_H, S5_N)
    bre_t, bim_t = to_t(sp["s5_b_re"]), to_t(sp["s5_b_im"])
    to_w = lambda c: jnp.tile(c.reshape(S5_W, S5_P), (1, 2))
    cre_w, cim_w = to_w(sp["s5_c_re"]), to_w(sp["s5_c_im"])

    jobs = (lambda *j: list(j)) if distributed else (lambda *j: [])
    c_arr = jnp.reshape(lax.axis_index("c"), (1,)).astype(jnp.int32) if distributed else None
    (xhat0, rstd0), bg = _ln_in(x2d, meta, jobs(*([_job_gather(w_int), _job_gather(w_glu)] if distributed else [])),
                                gather_meta=distributed)
    if distributed:
        w_int, w_glu = bg[1].reshape(PROJ_W, D_MODEL), bg[2].reshape(S5_W, S5_W)
    s5_small = (lre, lim, ldt, bre_t, bim_t, cre_w, cim_w, row(sp["s5_d"]), w_glu, row(sp["s5_b_glu"]))
    (u, q, k, v, gate), bg = _in_proj(xhat0, li_g, li_b, w_int, cos2, sin2,
                                      jobs(_job_gather(w_out) if distributed else None))
    if distributed:
        w_out = bg[0].reshape(D_MODEL, D_MODEL)
    (ys5, xr, xi), bg = _s5_fwd(u, *s5_small, jobs=jobs(_job_gather(w_up) if distributed else None))
    if distributed:
        w_up = bg[0]
    (o, states), _ = _ret_fwd(q, k, v)
    (ycat, xhat1, rstd1, h1b, pre), bg = _post_up(o, gate, ys5, xhat0, gn_g, gn_b, li_g, li_b, l1_g, l1_b, w_out, w_up,
                                                  jobs(_job_gather(w_down) if distributed else None))
    if distributed:
        w_down = bg[0].reshape(D_FF, D_MODEL)
    dr2, dffb, loss8, dl2g, dl2b = _post_down(pre, xhat1, tgt, l1_g, l1_b, l2_g, l2_b, w_down)
    g_up, g_down, dh1m = _mlp_bwd(h1b, dffb, pre, w_up, w_down)
    (do, dgate, dys5, dh0r, g_out, dl1g, dl1b, dgng, dgnb), bg = _post_bwd(
        dh1m, dr2, xhat1, rstd1, ycat, o, gate, gn_g, gn_b, l1_g, w_out,
        jobs(*([_job_pair(g_up), _job_pair(g_down)] if distributed else [])))
    g_out = g_out.reshape(N_DEV, D_MODEL // N_DEV, D_MODEL)
    if distributed:
        p_up, p_down = _pair_sum([g_up, g_down], bg, c_arr, "pair_sum_mlp")
    (du, dlre, dlim, dldt, dbre_t, dbim_t, dcre, dcim, dd, dwglu, dbglu), bg = _s5_bwd(
        dys5, u, xr, xi, *s5_small,
        jobs=jobs(*([_job_chips(p_up), _job_chips(p_down), _job_pair(g_out)] if distributed else [])))
    if distributed:
        r_up, r_down = bg[0], bg[1]
        (p_out,) = _pair_sum([g_out], bg[2:], c_arr, "pair_sum_out")
    from_t = lambda t: t.reshape(S5_H, S5_G, S5_P).transpose(1, 0, 2)
    small = {
        "s5_lambda_re": dlre, "s5_lambda_im": dlim, "s5_log_dt": dldt[:, :S5_G],
        "s5_b_re": from_t(dbre_t), "s5_b_im": from_t(dbim_t), "s5_c_re": dcre, "s5_c_im": dcim, "s5_d": dd,
        "s5_b_glu": dbglu, "ret_gn_g": dgng, "ret_gn_b": dgnb, "ln1_g": dl1g, "ln1_b": dl1b, "ln2_g": dl2g, "ln2_b": dl2b,
        "s5_w_glu": dwglu, "loss": loss8[0:1, 0:1]}
    early_pack = _pack([small[n] for n in EARLY])
    (dq, dk, dv), bg = _ret_bwd(q, k, v, do, states, cos2, sin2,
                                jobs(*([_job_chips(p_out), _job_gather(early_pack)] if distributed else [])))
    (grad_x, dmeta, g_int, dlig, dlib), _ = _in_bwd(du, dq, dk, dv, dgate, dh0r, xhat0, rstd0, li_g, li_b, w_int)
    small.update(ln_in_g=dlig, ln_in_b=dlib, meta_tokens=dmeta)
    g_int = g_int.reshape(N_DEV, PROJ_W // N_DEV, D_MODEL)
    if distributed:
        (r1_in,) = _exchange([_job_pair(g_int)], "exchange_pair_in")
        (p_in,) = _pair_sum([g_int], [r1_in], c_arr, "pair_sum_in")
        big = dict(chip_sums=[p_in, p_out, p_up, p_down], received=[None, bg[0], r_up, r_down], early=bg[1])
    else:
        big = dict(partials=[g_int, g_out, g_up, g_down])
    return grad_x, big, small


def kernel(x, meta_tokens, ln_in_g, ln_in_b, w_in, s5_lambda_re, s5_lambda_im, s5_log_dt, s5_b_re, s5_b_im, s5_c_re, s5_c_im, s5_d, s5_w_glu, s5_b_glu, ret_gn_g, ret_gn_b, w_out, ln1_g, ln1_b, w_up, w_down, ln2_g, ln2_b, loss_target, m_meta_tokens, m_ln_in_g, m_ln_in_b, m_w_in, m_s5_lambda_re, m_s5_lambda_im, m_s5_log_dt, m_s5_b_re, m_s5_b_im, m_s5_c_re, m_s5_c_im, m_s5_d, m_s5_w_glu, m_s5_b_glu, m_ret_gn_g, m_ret_gn_b, m_w_out, m_ln1_g, m_ln1_b, m_w_up, m_w_down, m_ln2_g, m_ln2_b, v_meta_tokens, v_ln_in_g, v_ln_in_b, v_w_in, v_s5_lambda_re, v_s5_lambda_im, v_s5_log_dt, v_s5_b_re, v_s5_b_im, v_s5_c_re, v_s5_c_im, v_s5_d, v_s5_w_glu, v_s5_b_glu, v_ret_gn_g, v_ret_gn_b, v_w_out, v_ln1_g, v_ln1_b, v_w_up, v_w_down, v_ln2_g, v_ln2_b):
    args = dict(locals())
    names = ["meta_tokens", "ln_in_g", "ln_in_b", "w_in", "s5_lambda_re", "s5_lambda_im", "s5_log_dt", "s5_b_re", "s5_b_im",
             "s5_c_re", "s5_c_im", "s5_d", "s5_w_glu", "s5_b_glu", "ret_gn_g", "ret_gn_b", "w_out", "ln1_g", "ln1_b",
             "w_up", "w_down", "ln2_g", "ln2_b"]
    ax, ay, ac = _place()
    me = 4 * ax + 2 * ay + ac

    sp = {n: args[n] for n in SMALL}
    grad_x, big, small = _local_step(x[0], loss_target[0], meta_tokens, w_in[0].T.astype(MM), w_out[0].astype(MM),
                                   w_up[0].astype(MM), w_down[0].astype(MM), s5_w_glu[0].astype(MM), sp, True)

    j_arr = jnp.reshape(2 * ax + ay, (1,)).astype(jnp.int32)
    two_d = lambda a: a.reshape(a.shape[-2:])
    item = lambda n, p, r: (p, r, *(two_d(_view(n, a)) for a in (args[n], args["m_" + n], args["v_" + n])))
    late_pack = _pack([small[n] for n in LATE])
    mlp = ("w_out", "w_up", "w_down")
    res, (r_in, late_all) = _adamw_shards(
        [item(n, p, r) for n, p, r in zip(mlp, big["chip_sums"][1:], big["received"][1:])], "adamw_mlp", 8, j_arr,
        [_job_chips(big["chip_sums"][0]), _job_gather(late_pack)])
    res_in, _ = _adamw_shards([item("w_in", big["chip_sums"][0], r_in)], "adamw_in", 2, j_arr)
    upd = {"w_in": res_in}
    for idx, n in enumerate(mlp):
        upd[n] = res[4 * idx:4 * idx + 4]
    shard_grads = {n: upd[n][0] for n in upd}

    early_shapes = [_view(n, args[n]).shape for n in EARLY[:-2]] + [(S5_W, S5_W), (1,)]
    late_shapes = [args["ln_in_g"].shape, args["ln_in_b"].shape, (N_META, D_MODEL)]
    g_small = dict(zip(EARLY, _unpack(_sum_devices(big["early"], "sum_small_early"), early_shapes)))
    g_small.update(zip(LATE, _unpack(_sum_devices(late_all, "sum_small_late"), late_shapes)))
    loss = g_small["loss"].reshape(())

    shard_grads["meta_tokens"] = lax.dynamic_slice(g_small["meta_tokens"], (0, me * (D_MODEL // N_DEV)),
                                                   (N_META, D_MODEL // N_DEV))
    shard_grads["s5_w_glu"] = lax.dynamic_slice(g_small["s5_w_glu"], (me * (S5_W // N_DEV), 0),
                                                (S5_W // N_DEV, S5_W))[None]
    natives = SMALL + ["meta_tokens", "s5_w_glu"]
    res2 = _adamw_native([(shard_grads[n] if n in shard_grads else g_small[n], *(_view(n, args[p + n]) for p in ("", "m_", "v_")))
                          for n in natives], "adamw_small")
    for idx, n in enumerate(natives):
        upd[n] = [shard_grads[n] if n in shard_grads else g_small[n]] + list(res2[3 * idx:3 * idx + 3])

    grads, deltas, new_m, new_v = ([_view(n, upd[n][t]).reshape(args[n].shape) for n in names] for t in range(4))
    return (loss, grad_x[None], *grads, *deltas, *new_m, *new_v)
```

```python
import math

import jax
import jax.numpy as jnp
from jax import lax
from jax.experimental import pallas as pl
from jax.experimental.pallas import tpu as pltpu

F32 = jnp.float32
MM = jnp.bfloat16

D_MODEL = 1024
N_META = 16
CHUNK = 128
PAD = CHUNK - N_META
S5_W, S5_G, S5_H, S5_P = 256, 16, 16, 64
S5_N = S5_G * S5_P
RET_W, RET_H, HEAD = 768, 6, 128
D_FF = 4096
PROJ_W = S5_W + 4 * RET_W
N_DEV = 8
FF_BLK = D_FF // N_DEV
ROW_BLK = 384
MLP_ROWS = 1408
PROJ_ROWS = 704
ALPHA = 2.0 ** 0.25
LN_EPS = 1e-5
GN_EPS = 1e-5
ROPE_BASE = 10000.0
GELU_C = math.sqrt(2.0 / math.pi)
GELU_A = 0.044715
ADAM_LR, ADAM_B1, ADAM_B2, ADAM_EPS, ADAM_WD, ADAM_STEP = 0.001, 0.9, 0.999, 1e-08, 0.01, 10
VMEM_LIMIT = 60 * 1024 * 1024

_VMEM = pl.BlockSpec(memory_space=pltpu.VMEM)
_ANY = pl.BlockSpec(memory_space=pl.ANY)
_MESH = pl.DeviceIdType.MESH


def _params(sem=None):
    return pltpu.CompilerParams(dimension_semantics=sem, vmem_limit_bytes=VMEM_LIMIT)


def _dot(a, b):
    return jnp.dot(a.astype(MM), b.astype(MM), preferred_element_type=F32)


def _dot_nt(a, b):
    return lax.dot_general(a.astype(MM), b.astype(MM), (((1,), (1,)), ((), ())), preferred_element_type=F32)


def _dot_tn(a, b):
    return lax.dot_general(a.astype(MM), b.astype(MM), (((0,), (0,)), ((), ())), preferred_element_type=F32)


def _split3(a):
    hi = a.astype(jnp.bfloat16)
    r1 = a - hi.astype(F32)
    mid = r1.astype(jnp.bfloat16)
    lo = (r1 - mid.astype(F32)).astype(jnp.bfloat16)
    return hi, mid, lo


def _dot_sel_rhs(a, sel):
    s = sel.astype(jnp.bfloat16)
    return sum(jnp.dot(p, s, preferred_element_type=F32) for p in _split3(a))


def _dot_sel_lhs(sel, b):
    s = sel.astype(jnp.bfloat16)
    return sum(jnp.dot(s, p, preferred_element_type=F32) for p in _split3(b))


def _ln_fwd(r, eps):
    mu = jnp.mean(r, axis=-1, keepdims=True)
    xc = r - mu
    var = jnp.mean(xc * xc, axis=-1, keepdims=True)
    rstd = lax.rsqrt(var + eps)
    return xc * rstd, rstd


def _ln_bwd(dxhat, xhat, rstd):
    m1 = jnp.mean(dxhat, axis=-1, keepdims=True)
    m2 = jnp.mean(dxhat * xhat, axis=-1, keepdims=True)
    return rstd * (dxhat - m1 - xhat * m2)


def _colsum(a):
    return jnp.sum(a, axis=0, keepdims=True)


def _shift3(n_in, block=lambda i: i):
    return [pl.BlockSpec((CHUNK, D_MODEL), (lambda i, j=j: (jnp.clip(3 * block(i) - 1 + j, 0, n_in - 1), 0)))
            for j in range(3)]


def _ln_in(x2d, meta, jobs=(), gather_meta=False):
    seq = x2d.shape[0]
    tp = seq + CHUNK
    R = ROW_BLK
    nb = tp // R
    shard_w = D_MODEL // N_DEV

    def body(xa, xb, xc, meta_ref, xhat_ref, rstd_ref, raw_ref, *gathered):
        raw_ref[0:CHUNK, :] = xa[...]
        raw_ref[CHUNK:2 * CHUNK, :] = xb[...]
        raw_ref[2 * CHUNK:3 * CHUNK, :] = xc[...]

        @pl.when(pl.program_id(0) == nb - 1)
        def _():
            raw_ref[0:PAD, :] = jnp.zeros((PAD, D_MODEL), F32)
            if gather_meta:
                for d in range(N_DEV):
                    pltpu.sync_copy(gathered[0].at[d], raw_ref.at[PAD:CHUNK, d * shard_w:(d + 1) * shard_w])
            else:
                raw_ref[PAD:CHUNK, :] = meta_ref[...]

        xhat_ref[...], rstd_ref[...] = _ln_fwd(raw_ref[...], LN_EPS)

    row = lambda w: pl.BlockSpec((R, w), lambda i: (nb - 1 - i, 0))
    jobs = ([_job_gather(meta)] if gather_meta else []) + list(jobs)
    return _call(
        body, "ln_in", (nb,),
        _shift3(seq // CHUNK, lambda i: nb - 1 - i) + [pl.BlockSpec(meta.shape, lambda i: (0, 0))],
        [row(D_MODEL), row(1)], [jax.ShapeDtypeStruct((tp, D_MODEL), F32), jax.ShapeDtypeStruct((tp, 1), F32)],
        [pltpu.VMEM((R, D_MODEL), F32)], (x2d, x2d, x2d, meta), jobs, early=1 if gather_meta else 0)


def _in_proj(xhat0, ln_g, ln_b, w_int, cos2, sin2, jobs=()):
    tp = xhat0.shape[0]
    R = PROJ_ROWS if tp % PROJ_ROWS == 0 else ROW_BLK

    def body(xh_ref, g_ref, b_ref, w_ref, cos_ref, sin_ref, u_ref, q_ref, k_ref, v_ref, gate_ref):
        hb = (xh_ref[...] * g_ref[...] + b_ref[...]).astype(MM)
        valid = (pl.program_id(0) * R + lax.broadcasted_iota(jnp.int32, (R, 1), 0)) >= PAD

        def seg(lo, hi):
            return jnp.where(valid, _dot_nt(hb, w_ref[lo:hi, :]), 0.0)

        u_ref[...] = seg(0, S5_W)
        cos = cos_ref[...]
        sin = sin_ref[...]
        q = seg(S5_W, S5_W + RET_W)
        k = seg(S5_W + RET_W, S5_W + 2 * RET_W)
        for h in range(RET_H):
            sl = slice(h * HEAD, (h + 1) * HEAD)
            qh = q[:, sl]
            kh = k[:, sl]
            q_ref[:, sl] = (qh * cos + pltpu.roll(qh, HEAD // 2, 1) * sin).astype(q_ref.dtype)
            k_ref[:, sl] = ((kh * cos + pltpu.roll(kh, HEAD // 2, 1) * sin) * (HEAD ** -0.5)).astype(k_ref.dtype)
        v_ref[...] = seg(S5_W + 2 * RET_W, S5_W + 3 * RET_W).astype(v_ref.dtype)
        gate_ref[...] = seg(S5_W + 3 * RET_W, PROJ_W)

    def rows(w, dt):
        return pl.BlockSpec((R, w), lambda i: (i, 0)), jax.ShapeDtypeStruct((tp, w), dt)

    outs = [rows(S5_W, F32), rows(RET_W, MM), rows(RET_W, MM), rows(RET_W, MM), rows(RET_W, F32)]
    full = lambda s: pl.BlockSpec(s, lambda i: (0,) * len(s))
    return _call(
        body, "in_proj", (tp // R,),
        [pl.BlockSpec((R, D_MODEL), lambda i: (i, 0)), full((1, D_MODEL)), full((1, D_MODEL)), _VMEM,
         pl.BlockSpec((R, HEAD), lambda i: (i, 0)), pl.BlockSpec((R, HEAD), lambda i: (i, 0))],
        [o[0] for o in outs], [o[1] for o in outs], [], (xhat0, ln_g, ln_b, w_int, cos2, sin2), jobs)


def _s5_disc(lre, lim, ldt, bre_t, bim_t):
    dt = jnp.exp(ldt)
    mag = jnp.exp(lre * dt)
    ang = lim * dt
    lbr = mag * jnp.cos(ang)
    lbi = mag * jnp.sin(ang)
    den = lre * lre + lim * lim
    nr = lbr - 1.0
    qr = (nr * lre + lbi * lim) / den
    qi = (lbi * lre - nr * lim) / den
    return lbr, lbi, qr * bre_t - qi * bim_t, qr * bim_t + qi * bre_t


def _s5_tables(lbr, lbi, reverse):
    if reverse:
        lbi = -lbi
    pw = [(lbr, lbi)]
    for _ in range(7):
        r, i = pw[-1]
        pw.append((r * lbr - i * lbi, r * lbi + i * lbr))
    row = lax.broadcasted_iota(jnp.int32, (8, S5_N), 0)
    tabs = []
    for k in range(3):
        sh = 2 ** k
        mask = (row < 8 - sh) if reverse else (row >= sh)
        ar, ai = pw[sh - 1]
        tabs.append((jnp.where(mask, ar, 0.0), jnp.where(mask, ai, 0.0)))
    pr = jnp.zeros((8, S5_N), F32)
    pi = jnp.zeros((8, S5_N), F32)
    for i in range(8):
        ar, ai = pw[7 - i] if reverse else pw[i]
        pr = jnp.where(row == i, ar, pr)
        pi = jnp.where(row == i, ai, pi)
    tabs.append((pr, pi))
    return tabs


def _store_tables(tab_ref, tabs):
    for k, (r, i) in enumerate(tabs):
        tab_ref[2 * k] = r
        tab_ref[2 * k + 1] = i


def _bd_mask():
    r = lax.broadcasted_iota(jnp.int32, (S5_W, S5_N), 0)
    c = lax.broadcasted_iota(jnp.int32, (S5_W, S5_N), 1)
    return jnp.right_shift(r, 4) == jnp.right_shift(c, 6)


def _s5_block_diag(bbr_t, bbi_t, cre_w, cim_w):
    mask = _bd_mask()
    bd = lambda t: jnp.where(mask, t, 0.0)
    return (bd(jnp.tile(bbr_t, (S5_G, 1))), bd(jnp.tile(bbi_t, (S5_G, 1))),
            bd(jnp.tile(cre_w, (1, S5_N // HEAD))), bd(jnp.tile(cim_w, (1, S5_N // HEAD))))


def _scan8(xr, xi, tab_ref, lanes, reverse):
    for k in range(3):
        sh = (8 - 2 ** k) if reverse else 2 ** k
        sr = pltpu.roll(xr, sh, 0)
        si = pltpu.roll(xi, sh, 0)
        mr = tab_ref[2 * k, :, lanes]
        mi = tab_ref[2 * k + 1, :, lanes]
        xr, xi = xr + (mr * sr - mi * si), xi + (mr * si + mi * sr)
    return xr, xi


S5_LANES = 256


def _gelu(y):
    t = jnp.tanh(GELU_C * (y + GELU_A * y * y * y))
    return 0.5 * y * (1.0 + t), t


def _s5_fwd(u, lre, lim, ldt, bre_t, bim_t, cre_w, cim_w, d_row, w_glu, b_glu, jobs=()):
    tp = u.shape[0]
    R = ROW_BLK

    def body(u_ref, lre_ref, lim_ref, ldt_ref, bre_ref, bim_ref, cre_ref, cim_ref, d_ref, wg_ref, bg_ref,
             y_ref, xr_ref, xi_ref, bbd_r, bbd_i, cbd_r, cbd_i, tab_ref, car_r, car_i):
        @pl.when(pl.program_id(0) == 0)
        def _():
            lbr, lbi, bbr, bbi = _s5_disc(lre_ref[...], lim_ref[...], ldt_ref[...], bre_ref[...], bim_ref[...])
            br, bi, cr, ci = _s5_block_diag(bbr, bbi, cre_ref[...], cim_ref[...])
            bbd_r[...] = br.astype(MM)
            bbd_i[...] = bi.astype(MM)
            cbd_r[...] = cr.astype(MM)
            cbd_i[...] = ci.astype(MM)
            _store_tables(tab_ref, _s5_tables(lbr, lbi, False))
            car_r[...] = jnp.zeros_like(car_r)
            car_i[...] = jnp.zeros_like(car_i)

        u = u_ref[...]
        ub = u.astype(MM)
        xr_ref[...] = jnp.dot(ub, bbd_r[...], preferred_element_type=F32)
        xi_ref[...] = jnp.dot(ub, bbd_i[...], preferred_element_type=F32)
        for j in range(S5_N // S5_LANES):
            lanes = pl.ds(j * S5_LANES, S5_LANES)
            pr = tab_ref[6, :, lanes]
            pi = tab_ref[7, :, lanes]

            def step(g, carry):
                cr, ci = carry
                rows = pl.ds(pl.multiple_of(g * 8, 8), 8)
                xr, xi = _scan8(xr_ref[rows, lanes], xi_ref[rows, lanes], tab_ref, lanes, False)
                br = jnp.broadcast_to(cr[7:8, :], cr.shape)
                bi = jnp.broadcast_to(ci[7:8, :], ci.shape)
                xr = xr + (pr * br - pi * bi)
                xi = xi + (pr * bi + pi * br)
                xr_ref[rows, lanes] = xr
                xi_ref[rows, lanes] = xi
                return xr, xi

            cr, ci = lax.fori_loop(0, R // 8, step, (car_r[:, lanes], car_i[:, lanes]), unroll=2)
            car_r[:, lanes] = cr
            car_i[:, lanes] = ci
        y = _dot_nt(xr_ref[...], cbd_r[...]) - _dot_nt(xi_ref[...], cbd_i[...]) + d_ref[...] * u
        yg, _ = _gelu(y)
        z = _dot(yg, wg_ref[...]) + bg_ref[...]
        y_ref[...] = yg * jax.nn.sigmoid(z)

    full = lambda a: pl.BlockSpec(a.shape, lambda i: (0,) * a.ndim)
    small = [lre, lim, ldt, bre_t, bim_t, cre_w, cim_w, d_row, w_glu, b_glu]
    return _call(
        body, "s5_fwd", (tp // R,),
        [pl.BlockSpec((R, S5_W), lambda i: (i, 0))] + [full(a) for a in small],
        [pl.BlockSpec((R, S5_W), lambda i: (i, 0)), pl.BlockSpec((R, S5_N), lambda i: (i, 0)),
         pl.BlockSpec((R, S5_N), lambda i: (i, 0))],
        [jax.ShapeDtypeStruct((tp, S5_W), F32), jax.ShapeDtypeStruct((tp, S5_N), F32),
         jax.ShapeDtypeStruct((tp, S5_N), F32)],
        [pltpu.VMEM((S5_W, S5_N), MM)] * 4 + [pltpu.VMEM((8, 8, S5_N), F32), pltpu.VMEM((8, S5_N), F32),
                                              pltpu.VMEM((8, S5_N), F32)],
        (u, *small), jobs)


def _s5_bwd(dy_out, u, xr, xi, lre, lim, ldt, bre_t, bim_t, cre_w, cim_w, d_row, w_glu, b_glu, jobs=()):
    tp = u.shape[0]
    R = ROW_BLK
    nb = tp // R

    def body(dyo_ref, u_ref, xr_ref, xi_ref, xpr_ref, xpi_ref,
             lre_ref, lim_ref, ldt_ref, bre_ref, bim_ref, cre_ref, cim_ref, d_ref, wg_ref, bg_ref,
             du_ref, dlre_ref, dlim_ref, dldt_ref, dbre_ref, dbim_ref, dcre_ref, dcim_ref, dd_ref, dwg_ref, dbg_ref,
             bbd_r, bbd_i, cbd_r, cbd_i, tab_ref, car_r, car_i, gr_ref, gi_ref, xer_ref, xei_ref,
             abr, abi, acr, aci, adr, adi):
        i = pl.program_id(0)

        @pl.when(i == 0)
        def _():
            lbr, lbi, bbr, bbi = _s5_disc(lre_ref[...], lim_ref[...], ldt_ref[...], bre_ref[...], bim_ref[...])
            br, bi, cr, ci = _s5_block_diag(bbr, bbi, cre_ref[...], cim_ref[...])
            bbd_r[...] = br.astype(MM)
            bbd_i[...] = bi.astype(MM)
            cbd_r[...] = cr.astype(MM)
            cbd_i[...] = ci.astype(MM)
            _store_tables(tab_ref, _s5_tables(lbr, lbi, True))
            for ref in (car_r, car_i, abr, abi, acr, aci, adr, adi, dd_ref, dwg_ref, dbg_ref):
                ref[...] = jnp.zeros_like(ref)

        u = u_ref[...]
        xrv = xr_ref[...]
        xiv = xi_ref[...]
        y = _dot_nt(xrv, cbd_r[...]) - _dot_nt(xiv, cbd_i[...]) + d_ref[...] * u
        yg, t = _gelu(y)
        z = _dot(yg, wg_ref[...]) + bg_ref[...]
        s = jax.nn.sigmoid(z)
        dout = dyo_ref[...]
        dz = dout * yg * s * (1.0 - s)
        dyg = dout * s + _dot_nt(dz, wg_ref[...])
        dwg_ref[...] += _dot_tn(yg, dz)
        dbg_ref[...] += _colsum(dz)
        dy = dyg * (0.5 * (1.0 + t) + 0.5 * y * (1.0 - t * t) * GELU_C * (1.0 + 3.0 * GELU_A * y * y))
        dd_ref[...] += _colsum(dy * u)
        acr[...] += _dot_tn(dy, xrv)
        aci[...] -= _dot_tn(dy, xiv)
        gr_ref[...] = _dot(dy, cbd_r[...])
        gi_ref[...] = -_dot(dy, cbd_i[...])
        has_prev = (i < nb - 1).astype(F32)
        xer_ref[0:8, :] = xpr_ref[...] * has_prev
        xei_ref[0:8, :] = xpi_ref[...] * has_prev
        xer_ref[8:R + 8, :] = xrv
        xei_ref[8:R + 8, :] = xiv
        row = lax.broadcasted_iota(jnp.int32, (8, S5_LANES), 0)
        for j in range(S5_N // S5_LANES):
            lanes = pl.ds(j * S5_LANES, S5_LANES)
            pr = tab_ref[6, :, lanes]
            pi = tab_ref[7, :, lanes]

            def step(n, carry):
                cr, ci, sar, sai = carry
                g = R // 8 - 1 - n
                r0 = pl.multiple_of(g * 8, 8)
                rows = pl.ds(r0, 8)
                gr, gi = _scan8(gr_ref[rows, lanes], gi_ref[rows, lanes], tab_ref, lanes, True)
                br = jnp.broadcast_to(cr[0:1, :], cr.shape)
                bi = jnp.broadcast_to(ci[0:1, :], ci.shape)
                gr = gr + (pr * br - pi * bi)
                gi = gi + (pr * bi + pi * br)
                gr_ref[rows, lanes] = gr
                gi_ref[rows, lanes] = gi
                last = row == 7
                xpr = pltpu.roll(jnp.where(last, xer_ref[rows, lanes], xer_ref[pl.ds(r0 + 8, 8), lanes]), 1, 0)
                xpi = pltpu.roll(jnp.where(last, xei_ref[rows, lanes], xei_ref[pl.ds(r0 + 8, 8), lanes]), 1, 0)
                return gr, gi, sar + (gr * xpr + gi * xpi), sai + (gi * xpr - gr * xpi)

            cr, ci, sar, sai = lax.fori_loop(
                0, R // 8, step, (car_r[:, lanes], car_i[:, lanes], adr[:, lanes], adi[:, lanes]), unroll=2)
            car_r[:, lanes] = cr
            car_i[:, lanes] = ci
            adr[:, lanes] = sar
            adi[:, lanes] = sai
        grv = gr_ref[...]
        giv = gi_ref[...]
        du_ref[...] = (dy * d_ref[...] + _dot_nt(grv, bbd_r[...]) + _dot_nt(giv, bbd_i[...])).astype(du_ref.dtype)
        abr[...] += _dot_tn(u, grv)
        abi[...] += _dot_tn(u, giv)

        @pl.when(i == nb - 1)
        def _():
            mask = _bd_mask()
            r16 = lax.broadcasted_iota(jnp.int32, (S5_H, S5_W), 1)
            h16 = lax.broadcasted_iota(jnp.int32, (S5_H, S5_W), 0)
            fold_b = jnp.bitwise_and(r16, S5_H - 1) == h16
            c64 = lax.broadcasted_iota(jnp.int32, (S5_N, S5_P), 0)
            p64 = lax.broadcasted_iota(jnp.int32, (S5_N, S5_P), 1)
            fold_c = jnp.bitwise_and(c64, S5_P - 1) == p64
            dbbr = _dot_sel_lhs(fold_b, jnp.where(mask, abr[...], 0.0))
            dbbi = _dot_sel_lhs(fold_b, jnp.where(mask, abi[...], 0.0))
            dcre_ref[...] = _dot_sel_rhs(jnp.where(mask, acr[...], 0.0), fold_c)
            dcim_ref[...] = _dot_sel_rhs(jnp.where(mask, aci[...], 0.0), fold_c)
            dlbr = _colsum(adr[...])
            dlbi = _colsum(adi[...])
            _, vjp = jax.vjp(_s5_disc, lre_ref[...], lim_ref[...], ldt_ref[...], bre_ref[...], bim_ref[...])
            dlre, dlim, dldt, dbre, dbim = vjp((dlbr, dlbi, dbbr, dbbi))
            dlre_ref[...] = dlre
            dlim_ref[...] = dlim
            dbre_ref[...] = dbre
            dbim_ref[...] = dbim
            gsel = jnp.right_shift(lax.broadcasted_iota(jnp.int32, (S5_N, HEAD), 0), 6) == \
                lax.broadcasted_iota(jnp.int32, (S5_N, HEAD), 1)
            dldt_ref[...] = _dot_sel_rhs(dldt, gsel)

    full = lambda a: pl.BlockSpec(a.shape, lambda i: (0,) * a.ndim)
    rev = lambda w: pl.BlockSpec((R, w), lambda i: (nb - 1 - i, 0))
    prev8 = pl.BlockSpec((8, S5_N), lambda i: (jnp.maximum((nb - 1 - i) * (R // 8) - 1, 0), 0))
    small = [lre, lim, ldt, bre_t, bim_t, cre_w, cim_w, d_row, w_glu, b_glu]
    outs = [((tp, S5_W), rev(S5_W))] + [
        (s, pl.BlockSpec(s, lambda i: (0, 0))) for s in
        [(1, S5_N), (1, S5_N), (1, HEAD), (S5_H, S5_N), (S5_H, S5_N), (S5_W, S5_P), (S5_W, S5_P),
         (1, S5_W), (S5_W, S5_W), (1, S5_W)]]
    return _call(
        body, "s5_bwd", (nb,),
        [rev(S5_W), rev(S5_W), rev(S5_N), rev(S5_N), prev8, prev8] + [full(a) for a in small],
        [o[1] for o in outs], [jax.ShapeDtypeStruct(o[0], MM if n == 0 else F32) for n, o in enumerate(outs)],
        [pltpu.VMEM((S5_W, S5_N), MM)] * 4 + [
            pltpu.VMEM((8, 8, S5_N), F32), pltpu.VMEM((8, S5_N), F32), pltpu.VMEM((8, S5_N), F32),
            pltpu.VMEM((R, S5_N), F32), pltpu.VMEM((R, S5_N), F32),
            pltpu.VMEM((R + 8, S5_N), F32), pltpu.VMEM((R + 8, S5_N), F32)] + [pltpu.VMEM((S5_W, S5_N), F32)] * 4 + [
            pltpu.VMEM((8, S5_N), F32), pltpu.VMEM((8, S5_N), F32)],
        (dy_out, u, xr, xi, xr, xi, *small), jobs)


RET_CHUNK = ROW_BLK
LOG_GAMMA = [math.log1p(-2.0 ** (-5 - h)) for h in range(RET_H)]
GAMMA_CHUNK = [math.exp(RET_CHUNK * lg) for lg in LOG_GAMMA]
_DECAY_SCRATCH = [pltpu.VMEM((RET_H, RET_CHUNK, RET_CHUNK), F32), pltpu.VMEM((RET_H, RET_CHUNK, HEAD), F32),
                  pltpu.VMEM((RET_H, RET_CHUNK, HEAD), F32)]


def _fill_decay(dm_ref, ze_ref, xi_ref):
    C = RET_CHUNK
    diff = (lax.broadcasted_iota(jnp.int32, (C, C), 0) - lax.broadcasted_iota(jnp.int32, (C, C), 1)).astype(F32)
    r = lax.broadcasted_iota(jnp.int32, (C, HEAD), 0).astype(F32)
    for h, lg in enumerate(LOG_GAMMA):
        dm_ref[h] = jnp.where(diff >= 0.0, jnp.exp(jnp.maximum(diff, 0.0) * lg), 0.0)
        ze_ref[h] = jnp.exp((C - 1.0 - r) * lg)
        xi_ref[h] = jnp.exp((r + 1.0) * lg)


def _ret_fwd(q, k, v, jobs=()):
    tp = q.shape[0]
    C = RET_CHUNK
    nc = tp // C

    def body(q_ref, k_ref, v_ref, o_ref, st_ref, s_ref, dm_ref, ze_ref, xi_ref):
        @pl.when(pl.program_id(0) == 0)
        def _():
            s_ref[...] = jnp.zeros_like(s_ref)
            _fill_decay(dm_ref, ze_ref, xi_ref)

        for h in range(RET_H):
            sl = slice(h * HEAD, (h + 1) * HEAD)
            qh, kh, vh = q_ref[:, sl], k_ref[:, sl], v_ref[:, sl]
            sh = s_ref[h]
            st_ref[0, sl, :] = sh
            scores = _dot_nt(qh, kh) * dm_ref[h]
            o_ref[:, sl] = _dot(scores, vh) + _dot(qh, sh) * xi_ref[h]
            s_ref[h] = GAMMA_CHUNK[h] * sh + _dot_tn(kh.astype(F32) * ze_ref[h], vh)

    blk = pl.BlockSpec((C, RET_W), lambda c: (c, 0))
    return _call(
        body, "ret_fwd", (nc,), [blk, blk, blk], [blk, pl.BlockSpec((1, RET_W, HEAD), lambda c: (c, 0, 0))],
        [jax.ShapeDtypeStruct((tp, RET_W), F32), jax.ShapeDtypeStruct((nc, RET_W, HEAD), F32)],
        [pltpu.VMEM((RET_H, HEAD, HEAD), F32)] + _DECAY_SCRATCH, (q, k, v), jobs)


def _ret_bwd(q, k, v, do, states, cos2, sin2, jobs=()):
    tp = q.shape[0]
    C = RET_CHUNK
    nc = tp // C

    def body(q_ref, k_ref, v_ref, do_ref, st_ref, cos_ref, sin_ref,
             dq_ref, dk_ref, dv_ref, ds_ref, dm_ref, ze_ref, xi_ref):
        @pl.when(pl.program_id(0) == 0)
        def _():
            ds_ref[...] = jnp.zeros_like(ds_ref)
            _fill_decay(dm_ref, ze_ref, xi_ref)

        cos = cos_ref[...]
        sin = sin_ref[...]
        for h in range(RET_H):
            sl = slice(h * HEAD, (h + 1) * HEAD)
            qh, kh, vh = q_ref[:, sl], k_ref[:, sl], v_ref[:, sl]
            dmh = dm_ref[h]
            sh = st_ref[0, sl, :]
            dsn = ds_ref[h]
            doh = do_ref[:, sl]
            dox = doh * xi_ref[h]
            a = _dot_nt(qh, kh) * dmh
            dqk = _dot_nt(doh, vh) * dmh
            kz = kh.astype(F32) * ze_ref[h]
            dv_ref[:, sl] = (_dot_tn(a, doh) + _dot(kz, dsn)).astype(dv_ref.dtype)
            dqr = _dot(dqk, kh) + _dot_nt(dox, sh)
            dkr = _dot_tn(dqk, qh) + ze_ref[h] * _dot_nt(vh, dsn)
            ds_ref[h] = GAMMA_CHUNK[h] * dsn + _dot_tn(qh, dox)
            dq_ref[:, sl] = (dqr * cos - pltpu.roll(dqr, HEAD // 2, 1) * sin).astype(dq_ref.dtype)
            dk_ref[:, sl] = ((dkr * cos - pltpu.roll(dkr, HEAD // 2, 1) * sin) * (HEAD ** -0.5)).astype(dk_ref.dtype)

    blk = pl.BlockSpec((C, RET_W), lambda c: (nc - 1 - c, 0))
    tab = pl.BlockSpec((C, HEAD), lambda c: (nc - 1 - c, 0))
    return _call(
        body, "ret_bwd", (nc,),
        [blk, blk, blk, blk, pl.BlockSpec((1, RET_W, HEAD), lambda c: (nc - 1 - c, 0, 0)), tab, tab],
        [blk, blk, blk], [jax.ShapeDtypeStruct((tp, RET_W), MM)] * 3,
        [pltpu.VMEM((RET_H, HEAD, HEAD), F32)] + _DECAY_SCRATCH, (q, k, v, do, states, cos2, sin2), jobs)


def _gn_gate(o, gate, gn_g, gn_b):
    xhat, rstd = _ln_fwd(o, GN_EPS)
    on = xhat * gn_g + gn_b
    s = jax.nn.sigmoid(gate)
    return gate * s * on, xhat, rstd, on, s


def _post_up(o, gate, ys5, xhat0, gn_g, gn_b, li_g, li_b, l1_g, l1_b, w_out, w_up, jobs=()):
    tp = o.shape[0]
    R = ROW_BLK

    def body(o_ref, g_ref, ys_ref, xh0_ref, gng, gnb, lig, lib, l1g, l1b, wo_ref, wu_ref,
             ycat_ref, xh1_ref, rstd1_ref, h1b_ref, pre_ref):
        ycat_ref[:, 0:S5_W] = ys_ref[...].astype(ycat_ref.dtype)
        for h in range(RET_H):
            sl = slice(h * HEAD, (h + 1) * HEAD)
            yret = _gn_gate(o_ref[:, sl], g_ref[:, sl], gng[:, sl], gnb[:, sl])[0]
            ycat_ref[:, S5_W + h * HEAD:S5_W + (h + 1) * HEAD] = yret.astype(ycat_ref.dtype)
        mixed = _dot(ycat_ref[...], wo_ref[...])
        h0 = xh0_ref[...] * lig[...] + lib[...]
        xh1, rstd1 = _ln_fwd(ALPHA * h0 + mixed, LN_EPS)
        xh1_ref[...] = xh1
        rstd1_ref[...] = rstd1
        h1b = (xh1 * l1g[...] + l1b[...]).astype(MM)
        h1b_ref[...] = h1b
        for d in range(N_DEV):
            pre_ref[:, d * FF_BLK:(d + 1) * FF_BLK] = jnp.maximum(_dot(h1b, wu_ref[d]), 0.0)

    row = lambda w: pl.BlockSpec((R, w), lambda i: (i, 0))
    full = lambda a: pl.BlockSpec(a.shape, lambda i: (0,) * a.ndim)
    vecs = [gn_g, gn_b, li_g, li_b, l1_g, l1_b]
    outs = [(row(D_MODEL), jax.ShapeDtypeStruct((tp, D_MODEL), MM)), (row(D_MODEL), jax.ShapeDtypeStruct((tp, D_MODEL), F32)),
            (row(1), jax.ShapeDtypeStruct((tp, 1), F32)), (row(D_MODEL), jax.ShapeDtypeStruct((tp, D_MODEL), MM)),
            (row(D_FF), jax.ShapeDtypeStruct((tp, D_FF), F32))]
    return _call(
        body, "post_up", (tp // R,),
        [row(RET_W), row(RET_W), row(S5_W), row(D_MODEL)] + [full(a) for a in vecs] + [_VMEM, _VMEM],
        [o[0] for o in outs], [o[1] for o in outs], [], (o, gate, ys5, xhat0, *vecs, w_out, w_up), jobs)


def _post_down(pre, xhat1, tgt, l1_g, l1_b, l2_g, l2_b, w_down):
    tp = pre.shape[0]
    seq = tgt.shape[0]
    R = ROW_BLK

    def body(pre_ref, xh1_ref, ta, tb, tc, l1g, l1b, l2g, l2b, wd_ref,
             dr2_ref, dffb_ref, loss_ref, dl2g_ref, dl2b_ref, tgt_ref):
        i = pl.program_id(0)

        @pl.when(i == 0)
        def _():
            for ref in (loss_ref, dl2g_ref, dl2b_ref):
                ref[...] = jnp.zeros_like(ref)

        tgt_ref[0:CHUNK, :] = ta[...]
        tgt_ref[CHUNK:2 * CHUNK, :] = tb[...]
        tgt_ref[2 * CHUNK:3 * CHUNK, :] = tc[...]
        ff = jnp.zeros((R, D_MODEL), F32)
        for d in range(N_DEV):
            pre = pre_ref[:, d * FF_BLK:(d + 1) * FF_BLK]
            ff = ff + _dot(pre * pre, wd_ref[d * FF_BLK:(d + 1) * FF_BLK, :])
        h1 = xh1_ref[...] * l1g[...] + l1b[...]
        xh2, rstd2 = _ln_fwd(ALPHA * h1 + ff, LN_EPS)
        h2 = xh2 * l2g[...] + l2b[...]
        valid = (i * R + lax.broadcasted_iota(jnp.int32, (R, 1), 0)) >= CHUNK
        err = jnp.where(valid, h2 - tgt_ref[...], 0.0)
        loss_ref[...] += 0.5 * jnp.sum(err * err) / D_MODEL
        dh2 = err * (1.0 / D_MODEL)
        dl2g_ref[...] += _colsum(dh2 * xh2)
        dl2b_ref[...] += _colsum(dh2)
        dr2 = _ln_bwd(dh2 * l2g[...], xh2, rstd2)
        dr2_ref[...] = dr2
        dffb_ref[...] = dr2.astype(MM)

    row = lambda w: pl.BlockSpec((R, w), lambda i: (i, 0))
    full = lambda a: pl.BlockSpec(a.shape, lambda i: (0,) * a.ndim)
    vecs = [l1_g, l1_b, l2_g, l2_b]
    acc = lambda s: (pl.BlockSpec(s, lambda i: (0, 0)), jax.ShapeDtypeStruct(s, F32))
    outs = [(row(D_MODEL), jax.ShapeDtypeStruct((tp, D_MODEL), F32)), (row(D_MODEL), jax.ShapeDtypeStruct((tp, D_MODEL), MM)),
            acc((8, HEAD)), acc((1, D_MODEL)), acc((1, D_MODEL))]
    return pl.pallas_call(
        body, name="post_down", grid=(tp // R,),
        in_specs=[row(D_FF), row(D_MODEL)] + _shift3(seq // CHUNK) + [full(a) for a in vecs] + [_VMEM],
        out_specs=[o[0] for o in outs], out_shape=[o[1] for o in outs],
        scratch_shapes=[pltpu.VMEM((R, D_MODEL), F32)],
        compiler_params=_params(("arbitrary",)),
    )(pre, xhat1, tgt, tgt, tgt, *vecs, w_down)


def _mlp_bwd(h1b, dffb, pre, w_up, w_down, send_pairs=False):
    tp = h1b.shape[0]
    R = MLP_ROWS if tp % MLP_ROWS == 0 else ROW_BLK
    nr = tp // R

    def body(h_ref, df_ref, pre_ref, wu_ref, wd_ref, gup_ref, gdn_ref, dh1_ref, *rest):
        if send_pairs:
            from_sib_up, from_sib_dn, aup, adn, stage_up, stage_dn, send_sems, recv_sems = rest
        else:
            aup, adn = rest
        d = pl.program_id(0)
        r = pl.program_id(1)

        def to_sibling(a, j):
            x, y, c = _place()
            return pltpu.make_async_remote_copy(
                src_ref=(stage_up, stage_dn)[a].at[j % 2], dst_ref=(from_sib_up, from_sib_dn)[a].at[j],
                send_sem=send_sems.at[4 * a + j], recv_sem=recv_sems.at[4 * a + j], device_id=(x, y, 1 - c),
                device_id_type=_MESH)

        if send_pairs:
            sibling_owns = lambda blk: blk % 2 == 1 - lax.axis_index("c")

            @pl.when((r == 0) & (d > 0) & sibling_owns(d - 1))
            def _():
                for a in range(2):
                    to_sibling(a, (d - 1) // 2).wait_send()

        @pl.when(r == 0)
        def _():
            aup[...] = jnp.zeros_like(aup)
            adn[...] = jnp.zeros_like(adn)

        h = h_ref[...]
        df = df_ref[...]
        wu = wu_ref[0]
        wd = wd_ref[0]
        pre = pre_ref[...]
        dpre = (_dot_nt(df, wd) * (2.0 * pre)).astype(MM)

        aup[...] += _dot_tn(h, dpre)
        adn[...] += _dot_tn(pre * pre, df)
        contrib = _dot_nt(dpre, wu)
        rows = pl.ds(pl.multiple_of(r * R, 64), R)

        @pl.when(d == 0)
        def _():
            dh1_ref[rows, :] = contrib

        @pl.when(d > 0)
        def _():
            dh1_ref[rows, :] += contrib

        @pl.when(r == nr - 1)
        def _():
            gup_ref[0] = aup[...].astype(gup_ref.dtype)
            gdn_ref[0] = adn[...].astype(gdn_ref.dtype)

        if send_pairs:
            @pl.when((r == nr - 1) & sibling_owns(d))
            def _():
                stage_up[(d // 2) % 2] = aup[...].astype(stage_up.dtype)
                stage_dn[(d // 2) % 2] = adn[...].astype(stage_dn.dtype)
                for a in range(2):
                    to_sibling(a, d // 2).start()

            @pl.when((r == nr - 1) & (d == N_DEV - 1))
            def _():
                @pl.when(sibling_owns(d))
                def _():
                    for a in range(2):
                        to_sibling(a, d // 2).wait_send()

                for a in range(2):
                    for j in range(4):
                        to_sibling(a, j).wait_recv()

    extra_out = ([_ANY, _ANY], [jax.ShapeDtypeStruct((4, D_MODEL, FF_BLK), MM),
                                jax.ShapeDtypeStruct((4, FF_BLK, D_MODEL), MM)]) if send_pairs else ([], [])
    return pl.pallas_call(
        body, name="mlp_bwd", grid=(N_DEV, nr),
        in_specs=[pl.BlockSpec((R, D_MODEL), lambda d, r: (r, 0)), pl.BlockSpec((R, D_MODEL), lambda d, r: (r, 0)),
                  pl.BlockSpec((R, FF_BLK), lambda d, r: (r, d)),
                  pl.BlockSpec((1, D_MODEL, FF_BLK), lambda d, r: (d, 0, 0)),
                  pl.BlockSpec((1, FF_BLK, D_MODEL), lambda d, r: (d, 0, 0))],
        out_specs=[pl.BlockSpec((1, D_MODEL, FF_BLK), lambda d, r: (d, 0, 0)),
                   pl.BlockSpec((1, FF_BLK, D_MODEL), lambda d, r: (d, 0, 0)), _VMEM] + extra_out[0],
        out_shape=[jax.ShapeDtypeStruct((N_DEV, D_MODEL, FF_BLK), MM), jax.ShapeDtypeStruct((N_DEV, FF_BLK, D_MODEL), MM),
                   jax.ShapeDtypeStruct((tp, D_MODEL), F32)] + extra_out[1],
        scratch_shapes=[pltpu.VMEM((D_MODEL, FF_BLK), F32), pltpu.VMEM((FF_BLK, D_MODEL), F32)]
        + ([pltpu.VMEM((2, D_MODEL, FF_BLK), MM), pltpu.VMEM((2, FF_BLK, D_MODEL), MM), _dma_sems(8), _dma_sems(8)]
           if send_pairs else []),
        compiler_params=_params(("arbitrary", "arbitrary")),
    )(h1b, dffb, pre, w_up, w_down.reshape(N_DEV, FF_BLK, D_MODEL))


def _post_bwd(dh1m, dr2, xhat1, rstd1, ycat, o, gate, gn_g, gn_b, l1_g, w_out, jobs=()):
    tp = o.shape[0]
    R = ROW_BLK
    nb = tp // R

    def body(dm_ref, dr2_ref, xh1_ref, rs1_ref, yc_ref, o_ref, g_ref, gng, gnb, l1g, wo_ref,
             do_ref, dg_ref, dys_ref, dh0_ref, gwo_ref, dl1g_ref, dl1b_ref, dgng_ref, dgnb_ref, awo):
        i = pl.program_id(0)

        @pl.when(i == 0)
        def _():
            for ref in (awo, dl1g_ref, dl1b_ref, dgng_ref, dgnb_ref):
                ref[...] = jnp.zeros_like(ref)

        dh1 = dm_ref[...] + ALPHA * dr2_ref[...]
        xh1 = xh1_ref[...]
        dl1g_ref[...] += _colsum(dh1 * xh1)
        dl1b_ref[...] += _colsum(dh1)
        dr1 = _ln_bwd(dh1 * l1g[...], xh1, rs1_ref[...])
        dh0_ref[...] = ALPHA * dr1
        dmix = dr1.astype(MM)
        awo[...] += _dot_tn(yc_ref[...], dmix)
        dyc = _dot_nt(dmix, wo_ref[...])
        dys_ref[...] = dyc[:, 0:S5_W]
        for h in range(RET_H):
            sl = slice(h * HEAD, (h + 1) * HEAD)
            gt = g_ref[:, sl]
            _, xhat, rstd, on, s = _gn_gate(o_ref[:, sl], gt, gng[:, sl], gnb[:, sl])
            dyr = dyc[:, S5_W + h * HEAD:S5_W + (h + 1) * HEAD]
            dg_ref[:, sl] = (dyr * on * (s * (1.0 + gt * (1.0 - s)))).astype(dg_ref.dtype)
            don = dyr * gt * s
            dgng_ref[:, sl] += _colsum(don * xhat)
            dgnb_ref[:, sl] += _colsum(don)
            do_ref[:, sl] = _ln_bwd(don * gng[:, sl], xhat, rstd)

        @pl.when(i == nb - 1)
        def _():
            gwo_ref[...] = awo[...].astype(gwo_ref.dtype)

    row = lambda w: pl.BlockSpec((R, w), lambda i: (i, 0))
    full = lambda a: pl.BlockSpec(a.shape, lambda i: (0,) * a.ndim)
    acc = lambda s, dt=F32: (pl.BlockSpec(s, lambda i: (0, 0)), jax.ShapeDtypeStruct(s, dt))
    outs = [(row(RET_W), jax.ShapeDtypeStruct((tp, RET_W), F32)), (row(RET_W), jax.ShapeDtypeStruct((tp, RET_W), MM)),
            (row(S5_W), jax.ShapeDtypeStruct((tp, S5_W), F32)), (row(D_MODEL), jax.ShapeDtypeStruct((tp, D_MODEL), F32)),
            acc((D_MODEL, D_MODEL), MM), acc((1, D_MODEL)), acc((1, D_MODEL)), acc((1, RET_W)), acc((1, RET_W))]
    return _call(
        body, "post_bwd", (nb,),
        [row(D_MODEL), row(D_MODEL), row(D_MODEL), row(1), row(D_MODEL), row(RET_W), row(RET_W),
         full(gn_g), full(gn_b), full(l1_g), _VMEM],
        [o[0] for o in outs], [o[1] for o in outs],
        [pltpu.VMEM((D_MODEL, D_MODEL), F32)],
        (dh1m, dr2, xhat1, rstd1, ycat, o, gate, gn_g, gn_b, l1_g, w_out), jobs)


def _in_bwd(du, dq, dk, dv, dg, dh0r, xhat0, rstd0, li_g, li_b, w_int, jobs=()):
    tp = du.shape[0]
    R = PROJ_ROWS if tp % PROJ_ROWS == 0 else ROW_BLK
    nb = tp // R
    segs = [(0, S5_W)] + [(S5_W + n * RET_W, S5_W + (n + 1) * RET_W) for n in range(4)]

    def body(du_ref, dq_ref, dk_ref, dv_ref, dg_ref, dh0r_ref, xh_ref, rs_ref, lig, lib, w_ref,
             gx_ref, dmeta_ref, gw_ref, dlg_ref, dlb_ref, aw, stage, out_sems):
        i = pl.program_id(0)
        slot = i % 2

        def to_gx(step_slot, first):
            if first:
                return pltpu.make_async_copy(stage.at[0, CHUNK:R, :], gx_ref.at[0:R - CHUNK, :], out_sems.at[0])
            return pltpu.make_async_copy(stage.at[step_slot], gx_ref.at[pl.ds(i * R - CHUNK, R), :], out_sems.at[step_slot])

        @pl.when(i == 0)
        def _():
            for ref in (aw, dlg_ref, dlb_ref):
                ref[...] = jnp.zeros_like(ref)

        @pl.when(i >= 3)
        def _():
            to_gx(slot, False).wait()

        valid = (i * R + lax.broadcasted_iota(jnp.int32, (R, 1), 0)) >= PAD
        xh = xh_ref[...]
        hb = (xh * lig[...] + lib[...]).astype(MM)
        dh0 = dh0r_ref[...]
        for (lo, hi), ref in zip(segs, (du_ref, dq_ref, dk_ref, dv_ref, dg_ref)):
            dseg = jnp.where(valid, ref[...], 0.0).astype(MM)
            dh0 = dh0 + _dot(dseg, w_ref[lo:hi, :])
            aw[lo:hi, :] += _dot_tn(dseg, hb)
        dlg_ref[...] += _colsum(dh0 * xh)
        dlb_ref[...] += _colsum(dh0)
        draw = _ln_bwd(dh0 * lig[...], xh, rs_ref[...])
        stage[slot] = draw

        @pl.when(i == 0)
        def _():
            dmeta_ref[...] = draw[PAD:CHUNK, :]
            first = to_gx(0, True)
            first.start()
            first.wait()

        @pl.when(i > 0)
        def _():
            to_gx(slot, False).start()

        @pl.when(i == nb - 1)
        def _():
            gw_ref[...] = aw[...].astype(gw_ref.dtype)
            for back in (1, 0):
                if nb - 1 - back >= 1:
                    to_gx((nb - 1 - back) % 2, False).wait()

    row = lambda w: pl.BlockSpec((R, w), lambda i: (i, 0))
    full = lambda a: pl.BlockSpec(a.shape, lambda i: (0,) * a.ndim)
    acc = lambda s, dt=F32: (pl.BlockSpec(s, lambda i: (0, 0)), jax.ShapeDtypeStruct(s, dt))
    outs = [(_ANY, jax.ShapeDtypeStruct((tp - CHUNK, D_MODEL), F32)), acc((N_META, D_MODEL)), acc((PROJ_W, D_MODEL), MM),
            acc((1, D_MODEL)), acc((1, D_MODEL))]
    return _call(
        body, "in_bwd", (nb,),
        [row(S5_W), row(RET_W), row(RET_W), row(RET_W), row(RET_W), row(D_MODEL), row(D_MODEL), row(1),
         full(li_g), full(li_b), _VMEM],
        [o[0] for o in outs], [o[1] for o in outs],
        [pltpu.VMEM((PROJ_W, D_MODEL), F32), pltpu.VMEM((2, R, D_MODEL), F32), pltpu.SemaphoreType.DMA((2,))],
        (du, dq, dk, dv, dg, dh0r, xhat0, rstd0, li_g, li_b, w_int), jobs)


def _place():
    return lax.axis_index("x"), lax.axis_index("y"), lax.axis_index("c")


def _dma_sems(n):
    return pltpu.SemaphoreType.DMA((n,))


def _job_gather(shard):
    def parts(ins, outs, sems):
        (src,), (out,), (send_sems, recv_sems, local_sem) = ins, outs, sems
        x, y, c = _place()
        north = c == 1
        me, sib = (x, y, c), (x, y, 1 - c)
        xn, yn, dg = (1 - x, y, c), (x, 1 - y, c), (1 - x, 1 - y, c)
        relay_from = (jnp.where(north, 1 - x, x), jnp.where(north, y, 1 - y), c)
        relay_to = (jnp.where(north, x, 1 - x), jnp.where(north, 1 - y, y), c)

        def slot(dev):
            return out.at[4 * dev[0] + 2 * dev[1] + dev[2]]

        def copy(k, block, to, from_input=False):
            return pltpu.make_async_remote_copy(
                src_ref=src if from_input else slot(block), dst_ref=slot(block),
                send_sem=send_sems.at[k], recv_sem=recv_sems.at[k], device_id=to, device_id_type=_MESH)

        mine = lambda: pltpu.make_async_copy(src, slot(me), local_sem.at[0])
        first = lambda: [copy(0, me, sib, True), copy(1, me, xn, True), copy(2, me, yn, True)]
        relayed = lambda: [copy(3, relay_from, relay_to), copy(4, xn, sib), copy(5, yn, sib)]
        return me, sib, xn, yn, dg, copy, mine, first, relayed

    def start(ins, outs, sems):
        mine, first = parts(ins, outs, sems)[6:8]
        mine().start()
        for cp in first():
            cp.start()

    def relay(ins, outs, sems):
        me, sib, xn, yn, dg, copy, mine, first, relayed = parts(ins, outs, sems)
        copy(1, xn, me).wait_recv()
        copy(2, yn, me).wait_recv()
        for cp in relayed():
            cp.start()

    def finish(ins, outs, sems):
        me, sib, xn, yn, dg, copy, mine, first, relayed = parts(ins, outs, sems)
        other = 1 - me[2]
        copy(3, dg, me).wait_recv()
        last = copy(6, dg, sib)
        last.start()
        copy(0, sib, me).wait_recv()
        for k, chip in ((4, xn), (5, yn), (6, dg)):
            copy(k, (chip[0], chip[1], other), me).wait_recv()
        for cp in first() + relayed() + [last]:
            cp.wait_send()
        mine().wait()

    return dict(ins=[shard], outs=[jax.ShapeDtypeStruct((N_DEV,) + shard.shape, shard.dtype)],
                sems=[_dma_sems(7), _dma_sems(7), _dma_sems(1)], start=start, middle=relay, finish=finish)


def _job_pair(g):
    def copies(ins, outs, sems):
        x, y, c = _place()
        return [pltpu.make_async_remote_copy(
            src_ref=ins[0].at[2 * j + (1 - c)], dst_ref=outs[0].at[j], send_sem=sems[0].at[j], recv_sem=sems[1].at[j],
            device_id=(x, y, 1 - c), device_id_type=_MESH) for j in range(4)]

    def start(ins, outs, sems):
        for cp in copies(ins, outs, sems):
            cp.start()

    def finish(ins, outs, sems):
        for cp in copies(ins, outs, sems):
            cp.wait()

    return dict(ins=[g], outs=[jax.ShapeDtypeStruct((4,) + g.shape[1:], g.dtype)], sems=[_dma_sems(4), _dma_sems(4)],
                start=start, finish=finish)


def _job_chips(p):
    def copies(ins, outs, sems):
        x, y, c = _place()
        chips = [(1 - x, y), (x, 1 - y), (1 - x, 1 - y)]
        return [pltpu.make_async_remote_copy(
            src_ref=ins[0].at[2 * chip[0] + chip[1]], dst_ref=outs[0].at[k], send_sem=sems[0].at[k],
            recv_sem=sems[1].at[k], device_id=(*chip, c), device_id_type=_MESH) for k, chip in enumerate(chips)]

    def start(ins, outs, sems):
        for cp in copies(ins, outs, sems):
            cp.start()

    def finish(ins, outs, sems):
        for cp in copies(ins, outs, sems):
            cp.wait()

    return dict(ins=[p], outs=[jax.ShapeDtypeStruct((3,) + p.shape[1:], p.dtype)], sems=[_dma_sems(3), _dma_sems(3)],
                start=start, finish=finish)


def _split_job_refs(jobs, ins, outs, sems):
    res, a, b, c = [], 0, 0, 0
    for job in jobs:
        na, nb, nc = len(job["ins"]), len(job["outs"]), len(job["sems"])
        res.append((ins[a:a + na], outs[b:b + nb], sems[c:c + nc]))
        a, b, c = a + na, b + nb, c + nc
    return res


def _call(body, name, grid, in_specs, out_specs, out_shape, scratch, args, jobs=(), prefetch=None, early=0):
    jobs = list(jobs)
    n_in, n_out, n_scr = len(in_specs), len(out_specs), len(scratch)
    j_in = [a for job in jobs for a in job["ins"]]
    j_out = [o for job in jobs for o in job["outs"]]
    j_scr = [s for job in jobs for s in job["sems"]]
    nsteps = grid[0]
    n_pre = 0 if prefetch is None else 1

    def wrapped(*refs):
        pre, refs = refs[:n_pre], refs[n_pre:]
        ins, jins = refs[:n_in], refs[n_in:n_in + len(j_in)]
        refs = refs[n_in + len(j_in):]
        outs, jouts = refs[:n_out], refs[n_out:n_out + len(j_out)]
        refs = refs[n_out + len(j_out):]
        scr, jscr = refs[:n_scr], refs[n_scr:]
        per_job = _split_job_refs(jobs, jins, jouts, jscr)

        def middle():
            for job, r in zip(jobs, per_job):
                if "middle" in job:
                    job["middle"](*r)

        @pl.when(pl.program_id(0) == 0)
        def _():
            for job, r in zip(jobs, per_job):
                job["start"](*r)

        if nsteps >= 3:
            pl.when(pl.program_id(0) == nsteps // 2)(middle)

        if early:
            @pl.when(pl.program_id(0) == nsteps - 1)
            def _():
                for job, r in zip(jobs[:early], per_job[:early]):
                    job["finish"](*r)

        body(*pre, *ins, *outs, *scr, *[o for r in per_job[:early] for o in r[1]])

        @pl.when(pl.program_id(0) == nsteps - 1)
        def _():
            if nsteps < 3:
                middle()
            for job, r in zip(jobs[early:], per_job[early:]):
                job["finish"](*r)

    specs = dict(in_specs=list(in_specs) + [_ANY] * len(j_in), out_specs=list(out_specs) + [_ANY] * len(j_out),
                 scratch_shapes=list(scratch) + j_scr)
    if n_pre:
        specs = dict(grid_spec=pltpu.PrefetchScalarGridSpec(num_scalar_prefetch=1, grid=grid, **specs))
    else:
        specs["grid"] = grid
    res = pl.pallas_call(
        wrapped if jobs else body, name=name, out_shape=list(out_shape) + j_out,
        compiler_params=_params(("arbitrary",) * len(grid)), **specs,
    )(*([prefetch] if n_pre else []), *args, *j_in)
    return list(res[:n_out]), list(res[n_out:])


def _exchange(jobs, name):
    j_in = [a for job in jobs for a in job["ins"]]
    j_out = [o for job in jobs for o in job["outs"]]
    j_scr = [s for job in jobs for s in job["sems"]]

    def body(*refs):
        per_job = _split_job_refs(jobs, refs[:len(j_in)], refs[len(j_in):len(j_in) + len(j_out)],
                                  refs[len(j_in) + len(j_out):])
        for phase in ("start", "middle", "finish"):
            for job, r in zip(jobs, per_job):
                if phase in job:
                    job[phase](*r)

    return pl.pallas_call(body, name=name, out_shape=j_out, in_specs=[_ANY] * len(j_in), out_specs=[_ANY] * len(j_out),
                          scratch_shapes=j_scr)(*j_in)


def _pair_sum(gs, r1s, c_arr, name):
    n = len(gs)

    def body(c_ref, *refs):
        for a in range(n):
            refs[2 * n + a][...] = (refs[a][...].astype(F32) + refs[n + a][...].astype(F32)).astype(refs[2 * n + a].dtype)

    def blk(g, own):
        s = g.shape[1:]
        if own:
            return pl.BlockSpec((1,) + s, lambda j, c_ref: (2 * j + c_ref[0],) + (0,) * len(s))
        return pl.BlockSpec((1,) + s, lambda j, c_ref: (j,) + (0,) * len(s))

    return pl.pallas_call(
        body, name=name,
        grid_spec=pltpu.PrefetchScalarGridSpec(
            num_scalar_prefetch=1, grid=(4,),
            in_specs=[blk(g, True) for g in gs] + [blk(g, False) for g in gs],
            out_specs=[blk(g, False) for g in gs]),
        out_shape=[jax.ShapeDtypeStruct((4,) + g.shape[1:], g.dtype) for g in gs],
        compiler_params=_params(("arbitrary",)),
    )(c_arr, *gs, *r1s)


def _adamw_math(w, g, m, v):
    m = ADAM_B1 * m + (1.0 - ADAM_B1) * g
    v = ADAM_B2 * v + (1.0 - ADAM_B2) * (g * g)
    m_hat = m / (1.0 - ADAM_B1 ** ADAM_STEP)
    v_hat = v / (1.0 - ADAM_B2 ** ADAM_STEP)
    return -ADAM_LR * (m_hat / (jnp.sqrt(v_hat) + ADAM_EPS) + ADAM_WD * w), m, v


def _view(name, a):
    return jnp.swapaxes(a, -1, -2) if name in ("w_in", "s5_b_re", "s5_b_im") else a


def _adamw_shards(items, name, steps, chip, jobs=()):
    n = len(items)

    def body(chip_ref, *refs):
        for a in range(n):
            p_ref, r_ref, w_ref, m_ref, v_ref = refs[5 * a:5 * a + 5]
            g = ((p_ref[0].astype(F32) + r_ref[0].astype(F32)) + r_ref[1].astype(F32)) + r_ref[2].astype(F32)
            outs = refs[5 * n + 4 * a:5 * n + 4 * a + 4]
            outs[0][...] = g
            outs[1][...], outs[2][...], outs[3][...] = _adamw_math(w_ref[...], g, m_ref[...], v_ref[...])

    in_specs, out_specs, out_shape, flat = [], [], [], []
    for p, r, w, m, v in items:
        rows, cols = w.shape
        rb = rows // steps
        in_specs += [pl.BlockSpec((1, rb, cols), lambda i, c: (c[0], i, 0)), pl.BlockSpec((3, rb, cols), lambda i, c: (0, i, 0))]
        wblk = pl.BlockSpec((rb, cols), lambda i, c: (i, 0))
        in_specs += [wblk] * 3
        out_specs += [wblk] * 4
        out_shape += [jax.ShapeDtypeStruct(w.shape, F32)] * 4
        flat += [p, r, w, m, v]
    return _call(body, name, (steps,), in_specs, out_specs, out_shape, [], flat, jobs, prefetch=chip)


def _sum_devices(gathered, name):
    def body(gs_ref, g_ref):
        g = gs_ref[0]
        for s in range(1, N_DEV):
            g = g + gs_ref[s]
        g_ref[...] = g

    return pl.pallas_call(body, name=name, out_shape=jax.ShapeDtypeStruct(gathered.shape[1:], F32),
                          in_specs=[_VMEM], out_specs=_VMEM, compiler_params=_params())(gathered)


def _adamw_native(items, name):
    n = len(items)

    def body(*refs):
        for a in range(n):
            g, w, m, v = (refs[4 * a + t][...] for t in range(4))
            refs[4 * n + 3 * a][...], refs[4 * n + 3 * a + 1][...], refs[4 * n + 3 * a + 2][...] = _adamw_math(w, g, m, v)

    return pl.pallas_call(
        body, name=name, out_shape=[jax.ShapeDtypeStruct(it[1].shape, F32) for it in items for _ in range(3)],
        in_specs=[_VMEM] * (4 * n), out_specs=[_VMEM] * (3 * n), compiler_params=_params(),
    )(*[t for it in items for t in it])


SMALL = ["ln_in_g", "ln_in_b", "s5_lambda_re", "s5_lambda_im", "s5_log_dt", "s5_b_re", "s5_b_im", "s5_c_re", "s5_c_im",
         "s5_d", "s5_b_glu", "ret_gn_g", "ret_gn_b", "ln1_g", "ln1_b", "ln2_g", "ln2_b"]
LATE = ["ln_in_g", "ln_in_b", "meta_tokens"]
EARLY = [n for n in SMALL if n not in LATE] + ["s5_w_glu", "loss"]
LANE = 128


def _pack(arrs):
    parts = []
    for a in arrs:
        f = a.reshape(-1)
        parts.append(jnp.pad(f, (0, (-f.shape[0]) % LANE)))
    flat = jnp.concatenate(parts)
    rows = -(-flat.shape[0] // LANE)
    flat = jnp.pad(flat, (0, (-rows % 8) * LANE + rows * LANE - flat.shape[0]))
    return flat.reshape(-1, LANE)


def _unpack(packed, shapes):
    flat = packed.reshape(-1)
    out, off = [], 0
    for s in shapes:
        n = math.prod(s)
        out.append(flat[off:off + n].reshape(s))
        off += n + (-n) % LANE
    return out


def _rope_tables(tp):
    inv_freq = 1.0 / (ROPE_BASE ** (jnp.arange(0, HEAD, 2, dtype=F32) / HEAD))
    blk = (jnp.arange(tp // ROW_BLK, dtype=F32) * ROW_BLK)[:, None, None] * inv_freq
    off = (jnp.arange(ROW_BLK, dtype=F32) - float(PAD))[None, :, None] * inv_freq
    cos = (jnp.cos(blk) * jnp.cos(off) - jnp.sin(blk) * jnp.sin(off)).reshape(tp, HEAD // 2)
    sin = (jnp.sin(blk) * jnp.cos(off) + jnp.cos(blk) * jnp.sin(off)).reshape(tp, HEAD // 2)
    return jnp.concatenate([cos, cos], axis=1), jnp.concatenate([-sin, sin], axis=1)


def _local_step(x2d, tgt, meta, w_int, w_out, w_up, w_down, w_glu, sp, distributed):
    tp = x2d.shape[0] + CHUNK
    row = lambda a: a.reshape(1, -1)
    cos2, sin2 = _rope_tables(tp)
    li_g, li_b = row(sp["ln_in_g"]), row(sp["ln_in_b"])
    l1_g, l1_b, l2_g, l2_b = row(sp["ln1_g"]), row(sp["ln1_b"]), row(sp["ln2_g"]), row(sp["ln2_b"])
    gn_g, gn_b = row(sp["ret_gn_g"]), row(sp["ret_gn_b"])
    lre, lim = row(sp["s5_lambda_re"]), row(sp["s5_lambda_im"])
    ldt = row(jnp.repeat(sp["s5_log_dt"].reshape(-1), S5_P))
    to_t = lambda b: b.reshape(S5_G, S5_P, S5_H).transpose(2, 0, 1).reshape(S5_H, S5_N)
    bre_t, bim_t = to_t(sp["s5_b_re"]), to_t(sp["s5_b_im"])
    to_w = lambda c: jnp.tile(c.reshape(S5_W, S5_P), (1, 2))
    cre_w, cim_w = to_w(sp["s5_c_re"]), to_w(sp["s5_c_im"])

    jobs = (lambda *j: list(j)) if distributed else (lambda *j: [])
    c_arr = jnp.reshape(lax.axis_index("c"), (1,)).astype(jnp.int32) if distributed else None
    (xhat0, rstd0), bg = _ln_in(x2d, meta, jobs(*([_job_gather(w_int), _job_gather(w_glu)] if distributed else [])),
                                gather_meta=distributed)
    if distributed:
        w_int, w_glu = bg[1].reshape(PROJ_W, D_MODEL), bg[2].reshape(S5_W, S5_W)
    s5_small = (lre, lim, ldt, bre_t, bim_t, cre_w, cim_w, row(sp["s5_d"]), w_glu, row(sp["s5_b_glu"]))
    (u, q, k, v, gate), bg = _in_proj(xhat0, li_g, li_b, w_int, cos2, sin2,
                                      jobs(_job_gather(w_out) if distributed else None))
    if distributed:
        w_out = bg[0].reshape(D_MODEL, D_MODEL)
    (ys5, xr, xi), bg = _s5_fwd(u, *s5_small, jobs=jobs(_job_gather(w_up) if distributed else None))
    if distributed:
        w_up = bg[0]
    (o, states), _ = _ret_fwd(q, k, v)
    (ycat, xhat1, rstd1, h1b, pre), bg = _post_up(o, gate, ys5, xhat0, gn_g, gn_b, li_g, li_b, l1_g, l1_b, w_out, w_up,
                                                  jobs(_job_gather(w_down) if distributed else None))
    if distributed:
        w_down = bg[0].reshape(D_FF, D_MODEL)
    dr2, dffb, loss8, dl2g, dl2b = _post_down(pre, xhat1, tgt, l1_g, l1_b, l2_g, l2_b, w_down)
    g_up, g_down, dh1m, *from_sib = _mlp_bwd(h1b, dffb, pre, w_up, w_down, send_pairs=distributed)
    if distributed:
        p_up, p_down = _pair_sum([g_up, g_down], from_sib, c_arr, "pair_sum_mlp")
    (do, dgate, dys5, dh0r, g_out, dl1g, dl1b, dgng, dgnb), bg = _post_bwd(
        dh1m, dr2, xhat1, rstd1, ycat, o, gate, gn_g, gn_b, l1_g, w_out,
        jobs(_job_chips(p_up) if distributed else None))
    g_out = g_out.reshape(N_DEV, D_MODEL // N_DEV, D_MODEL)
    r_up = bg[0] if distributed else None
    (du, dlre, dlim, dldt, dbre_t, dbim_t, dcre, dcim, dd, dwglu, dbglu), bg = _s5_bwd(
        dys5, u, xr, xi, *s5_small,
        jobs=jobs(*([_job_chips(p_down), _job_pair(g_out)] if distributed else [])))
    if distributed:
        r_down = bg[0]
        (p_out,) = _pair_sum([g_out], bg[1:], c_arr, "pair_sum_out")
    from_t = lambda t: t.reshape(S5_H, S5_G, S5_P).transpose(1, 0, 2)
    small = {
        "s5_lambda_re": dlre, "s5_lambda_im": dlim, "s5_log_dt": dldt[:, :S5_G],
        "s5_b_re": from_t(dbre_t), "s5_b_im": from_t(dbim_t), "s5_c_re": dcre, "s5_c_im": dcim, "s5_d": dd,
        "s5_b_glu": dbglu, "ret_gn_g": dgng, "ret_gn_b": dgnb, "ln1_g": dl1g, "ln1_b": dl1b, "ln2_g": dl2g, "ln2_b": dl2b,
        "s5_w_glu": dwglu, "loss": loss8[0:1, 0:1]}
    early_pack = _pack([small[n] for n in EARLY])
    (dq, dk, dv), bg = _ret_bwd(q, k, v, do, states, cos2, sin2,
                                jobs(*([_job_chips(p_out), _job_gather(early_pack)] if distributed else [])))
    (grad_x, dmeta, g_int, dlig, dlib), _ = _in_bwd(du, dq, dk, dv, dgate, dh0r, xhat0, rstd0, li_g, li_b, w_int)
    small.update(ln_in_g=dlig, ln_in_b=dlib, meta_tokens=dmeta)
    g_int = g_int.reshape(N_DEV, PROJ_W // N_DEV, D_MODEL)
    if distributed:
        (r1_in,) = _exchange([_job_pair(g_int)], "exchange_pair_in")
        (p_in,) = _pair_sum([g_int], [r1_in], c_arr, "pair_sum_in")
        big = dict(chip_sums=[p_in, p_out, p_up, p_down], received=[None, bg[0], r_up, r_down], early=bg[1])
    else:
        big = dict(partials=[g_int, g_out, g_up, g_down])
    return grad_x, big, small


def kernel(x, meta_tokens, ln_in_g, ln_in_b, w_in, s5_lambda_re, s5_lambda_im, s5_log_dt, s5_b_re, s5_b_im, s5_c_re, s5_c_im, s5_d, s5_w_glu, s5_b_glu, ret_gn_g, ret_gn_b, w_out, ln1_g, ln1_b, w_up, w_down, ln2_g, ln2_b, loss_target, m_meta_tokens, m_ln_in_g, m_ln_in_b, m_w_in, m_s5_lambda_re, m_s5_lambda_im, m_s5_log_dt, m_s5_b_re, m_s5_b_im, m_s5_c_re, m_s5_c_im, m_s5_d, m_s5_w_glu, m_s5_b_glu, m_ret_gn_g, m_ret_gn_b, m_w_out, m_ln1_g, m_ln1_b, m_w_up, m_w_down, m_ln2_g, m_ln2_b, v_meta_tokens, v_ln_in_g, v_ln_in_b, v_w_in, v_s5_lambda_re, v_s5_lambda_im, v_s5_log_dt, v_s5_b_re, v_s5_b_im, v_s5_c_re, v_s5_c_im, v_s5_d, v_s5_w_glu, v_s5_b_glu, v_ret_gn_g, v_ret_gn_b, v_w_out, v_ln1_g, v_ln1_b, v_w_up, v_w_down, v_ln2_g, v_ln2_b):
    args = dict(locals())
    names = ["meta_tokens", "ln_in_g", "ln_in_b", "w_in", "s5_lambda_re", "s5_lambda_im", "s5_log_dt", "s5_b_re", "s5_b_im",
             "s5_c_re", "s5_c_im", "s5_d", "s5_w_glu", "s5_b_glu", "ret_gn_g", "ret_gn_b", "w_out", "ln1_g", "ln1_b",
             "w_up", "w_down", "ln2_g", "ln2_b"]
    ax, ay, ac = _place()
    me = 4 * ax + 2 * ay + ac

    sp = {n: args[n] for n in SMALL}
    grad_x, big, small = _local_step(x[0], loss_target[0], meta_tokens, w_in[0].T.astype(MM), w_out[0].astype(MM),
                                   w_up[0].astype(MM), w_down[0].astype(MM), s5_w_glu[0].astype(MM), sp, True)

    j_arr = jnp.reshape(2 * ax + ay, (1,)).astype(jnp.int32)
    two_d = lambda a: a.reshape(a.shape[-2:])
    item = lambda n, p, r: (p, r, *(two_d(_view(n, a)) for a in (args[n], args["m_" + n], args["v_" + n])))
    late_pack = _pack([small[n] for n in LATE])
    mlp = ("w_out", "w_up", "w_down")
    res, (r_in, late_all) = _adamw_shards(
        [item(n, p, r) for n, p, r in zip(mlp, big["chip_sums"][1:], big["received"][1:])], "adamw_mlp", 8, j_arr,
        [_job_chips(big["chip_sums"][0]), _job_gather(late_pack)])
    res_in, _ = _adamw_shards([item("w_in", big["chip_sums"][0], r_in)], "adamw_in", 2, j_arr)
    upd = {"w_in": res_in}
    for idx, n in enumerate(mlp):
        upd[n] = res[4 * idx:4 * idx + 4]
    shard_grads = {n: upd[n][0] for n in upd}

    early_shapes = [_view(n, args[n]).shape for n in EARLY[:-2]] + [(S5_W, S5_W), (1,)]
    late_shapes = [args["ln_in_g"].shape, args["ln_in_b"].shape, (N_META, D_MODEL)]
    g_small = dict(zip(EARLY, _unpack(_sum_devices(big["early"], "sum_small_early"), early_shapes)))
    g_small.update(zip(LATE, _unpack(_sum_devices(late_all, "sum_small_late"), late_shapes)))
    loss = g_small["loss"].reshape(())

    shard_grads["meta_tokens"] = lax.dynamic_slice(g_small["meta_tokens"], (0, me * (D_MODEL // N_DEV)),
                                                   (N_META, D_MODEL // N_DEV))
    shard_grads["s5_w_glu"] = lax.dynamic_slice(g_small["s5_w_glu"], (me * (S5_W // N_DEV), 0),
                                                (S5_W // N_DEV, S5_W))[None]
    natives = SMALL + ["meta_tokens", "s5_w_glu"]
    res2 = _adamw_native([(shard_grads[n] if n in shard_grads else g_small[n], *(_view(n, args[p + n]) for p in ("", "m_", "v_")))
                          for n in natives], "adamw_small")
    for idx, n in enumerate(natives):
        upd[n] = [shard_grads[n] if n in shard_grads else g_small[n]] + list(res2[3 * idx:3 * idx + 3])

    grads, deltas, new_m, new_v = ([_view(n, upd[n][t]).reshape(args[n].shape) for n in names] for t in range(4))
    return (loss, grad_x[None], *grads, *deltas, *new_m, *new_v)
```

```python
import math

import jax
import jax.numpy as jnp
from jax import lax
from jax.experimental import pallas as pl
from jax.experimental.pallas import tpu as pltpu

F32 = jnp.float32
MM = jnp.bfloat16

D_MODEL = 1024
N_META = 16
CHUNK = 128
PAD = CHUNK - N_META
S5_W, S5_G, S5_H, S5_P = 256, 16, 16, 64
S5_N = S5_G * S5_P
RET_W, RET_H, HEAD = 768, 6, 128
D_FF = 4096
PROJ_W = S5_W + 4 * RET_W
N_DEV = 8
FF_BLK = D_FF // N_DEV
ROW_BLK = 384
MLP_ROWS = 1408
PROJ_ROWS = 704
ALPHA = 2.0 ** 0.25
LN_EPS = 1e-5
GN_EPS = 1e-5
ROPE_BASE = 10000.0
GELU_C = math.sqrt(2.0 / math.pi)
GELU_A = 0.044715
ADAM_LR, ADAM_B1, ADAM_B2, ADAM_EPS, ADAM_WD, ADAM_STEP = 0.001, 0.9, 0.999, 1e-08, 0.01, 10
VMEM_LIMIT = 60 * 1024 * 1024

_VMEM = pl.BlockSpec(memory_space=pltpu.VMEM)
_ANY = pl.BlockSpec(memory_space=pl.ANY)
_MESH = pl.DeviceIdType.MESH


def _params(sem=None):
    return pltpu.CompilerParams(dimension_semantics=sem, vmem_limit_bytes=VMEM_LIMIT)


def _dot(a, b):
    return jnp.dot(a.astype(MM), b.astype(MM), preferred_element_type=F32)


def _dot_nt(a, b):
    return lax.dot_general(a.astype(MM), b.astype(MM), (((1,), (1,)), ((), ())), preferred_element_type=F32)


def _dot_tn(a, b):
    return lax.dot_general(a.astype(MM), b.astype(MM), (((0,), (0,)), ((), ())), preferred_element_type=F32)


def _split3(a):
    hi = a.astype(jnp.bfloat16)
    r1 = a - hi.astype(F32)
    mid = r1.astype(jnp.bfloat16)
    lo = (r1 - mid.astype(F32)).astype(jnp.bfloat16)
    return hi, mid, lo


def _dot_sel_rhs(a, sel):
    s = sel.astype(jnp.bfloat16)
    return sum(jnp.dot(p, s, preferred_element_type=F32) for p in _split3(a))


def _dot_sel_lhs(sel, b):
    s = sel.astype(jnp.bfloat16)
    return sum(jnp.dot(s, p, preferred_element_type=F32) for p in _split3(b))


def _ln_fwd(r, eps):
    mu = jnp.mean(r, axis=-1, keepdims=True)
    xc = r - mu
    var = jnp.mean(xc * xc, axis=-1, keepdims=True)
    rstd = lax.rsqrt(var + eps)
    return xc * rstd, rstd


def _ln_bwd(dxhat, xhat, rstd):
    m1 = jnp.mean(dxhat, axis=-1, keepdims=True)
    m2 = jnp.mean(dxhat * xhat, axis=-1, keepdims=True)
    return rstd * (dxhat - m1 - xhat * m2)


def _colsum(a):
    return jnp.sum(a, axis=0, keepdims=True)


def _shift3(n_in, block=lambda i: i):
    return [pl.BlockSpec((CHUNK, D_MODEL), (lambda i, j=j: (jnp.clip(3 * block(i) - 1 + j, 0, n_in - 1), 0)))
            for j in range(3)]


def _ln_in(x2d, meta, jobs=(), gather_meta=False):
    seq = x2d.shape[0]
    tp = seq + CHUNK
    R = ROW_BLK
    nb = tp // R
    shard_w = D_MODEL // N_DEV

    def body(xa, xb, xc, meta_ref, xhat_ref, rstd_ref, raw_ref, *gathered):
        raw_ref[0:CHUNK, :] = xa[...]
        raw_ref[CHUNK:2 * CHUNK, :] = xb[...]
        raw_ref[2 * CHUNK:3 * CHUNK, :] = xc[...]

        @pl.when(pl.program_id(0) == nb - 1)
        def _():
            raw_ref[0:PAD, :] = jnp.zeros((PAD, D_MODEL), F32)
            if gather_meta:
                for d in range(N_DEV):
                    pltpu.sync_copy(gathered[0].at[d], raw_ref.at[PAD:CHUNK, d * shard_w:(d + 1) * shard_w])
            else:
                raw_ref[PAD:CHUNK, :] = meta_ref[...]

        xhat_ref[...], rstd_ref[...] = _ln_fwd(raw_ref[...], LN_EPS)

    row = lambda w: pl.BlockSpec((R, w), lambda i: (nb - 1 - i, 0))
    jobs = ([_job_gather(meta)] if gather_meta else []) + list(jobs)
    return _call(
        body, "ln_in", (nb,),
        _shift3(seq // CHUNK, lambda i: nb - 1 - i) + [pl.BlockSpec(meta.shape, lambda i: (0, 0))],
        [row(D_MODEL), row(1)], [jax.ShapeDtypeStruct((tp, D_MODEL), F32), jax.ShapeDtypeStruct((tp, 1), F32)],
        [pltpu.VMEM((R, D_MODEL), F32)], (x2d, x2d, x2d, meta), jobs, early=1 if gather_meta else 0)


def _in_proj(xhat0, ln_g, ln_b, w_int, cos2, sin2, jobs=()):
    tp = xhat0.shape[0]
    R = PROJ_ROWS if tp % PROJ_ROWS == 0 else ROW_BLK

    def body(xh_ref, g_ref, b_ref, w_ref, cos_ref, sin_ref, u_ref, q_ref, k_ref, v_ref, gate_ref):
        hb = (xh_ref[...] * g_ref[...] + b_ref[...]).astype(MM)
        valid = (pl.program_id(0) * R + lax.broadcasted_iota(jnp.int32, (R, 1), 0)) >= PAD

        def seg(lo, hi):
            return jnp.where(valid, _dot_nt(hb, w_ref[lo:hi, :]), 0.0)

        u_ref[...] = seg(0, S5_W)
        cos = cos_ref[...]
        sin = sin_ref[...]
        q = seg(S5_W, S5_W + RET_W)
        k = seg(S5_W + RET_W, S5_W + 2 * RET_W)
        for h in range(RET_H):
            sl = slice(h * HEAD, (h + 1) * HEAD)
            qh = q[:, sl]
            kh = k[:, sl]
            q_ref[:, sl] = (qh * cos + pltpu.roll(qh, HEAD // 2, 1) * sin).astype(q_ref.dtype)
            k_ref[:, sl] = ((kh * cos + pltpu.roll(kh, HEAD // 2, 1) * sin) * (HEAD ** -0.5)).astype(k_ref.dtype)
        v_ref[...] = seg(S5_W + 2 * RET_W, S5_W + 3 * RET_W).astype(v_ref.dtype)
        gate_ref[...] = seg(S5_W + 3 * RET_W, PROJ_W)

    def rows(w, dt):
        return pl.BlockSpec((R, w), lambda i: (i, 0)), jax.ShapeDtypeStruct((tp, w), dt)

    outs = [rows(S5_W, F32), rows(RET_W, MM), rows(RET_W, MM), rows(RET_W, MM), rows(RET_W, F32)]
    full = lambda s: pl.BlockSpec(s, lambda i: (0,) * len(s))
    return _call(
        body, "in_proj", (tp // R,),
        [pl.BlockSpec((R, D_MODEL), lambda i: (i, 0)), full((1, D_MODEL)), full((1, D_MODEL)), _VMEM,
         pl.BlockSpec((R, HEAD), lambda i: (i, 0)), pl.BlockSpec((R, HEAD), lambda i: (i, 0))],
        [o[0] for o in outs], [o[1] for o in outs], [], (xhat0, ln_g, ln_b, w_int, cos2, sin2), jobs)


def _s5_disc(lre, lim, ldt, bre_t, bim_t):
    dt = jnp.exp(ldt)
    mag = jnp.exp(lre * dt)
    ang = lim * dt
    lbr = mag * jnp.cos(ang)
    lbi = mag * jnp.sin(ang)
    den = lre * lre + lim * lim
    nr = lbr - 1.0
    qr = (nr * lre + lbi * lim) / den
    qi = (lbi * lre - nr * lim) / den
    return lbr, lbi, qr * bre_t - qi * bim_t, qr * bim_t + qi * bre_t


def _s5_tables(lbr, lbi, reverse):
    if reverse:
        lbi = -lbi
    pw = [(lbr, lbi)]
    for _ in range(7):
        r, i = pw[-1]
        pw.append((r * lbr - i * lbi, r * lbi + i * lbr))
    row = lax.broadcasted_iota(jnp.int32, (8, S5_N), 0)
    tabs = []
    for k in range(3):
        sh = 2 ** k
        mask = (row < 8 - sh) if reverse else (row >= sh)
        ar, ai = pw[sh - 1]
        tabs.append((jnp.where(mask, ar, 0.0), jnp.where(mask, ai, 0.0)))
    pr = jnp.zeros((8, S5_N), F32)
    pi = jnp.zeros((8, S5_N), F32)
    for i in range(8):
        ar, ai = pw[7 - i] if reverse else pw[i]
        pr = jnp.where(row == i, ar, pr)
        pi = jnp.where(row == i, ai, pi)
    tabs.append((pr, pi))
    return tabs


def _store_tables(tab_ref, tabs):
    for k, (r, i) in enumerate(tabs):
        tab_ref[2 * k] = r
        tab_ref[2 * k + 1] = i


def _bd_mask():
    r = lax.broadcasted_iota(jnp.int32, (S5_W, S5_N), 0)
    c = lax.broadcasted_iota(jnp.int32, (S5_W, S5_N), 1)
    return jnp.right_shift(r, 4) == jnp.right_shift(c, 6)


def _s5_block_diag(bbr_t, bbi_t, cre_w, cim_w):
    mask = _bd_mask()
    bd = lambda t: jnp.where(mask, t, 0.0)
    return (bd(jnp.tile(bbr_t, (S5_G, 1))), bd(jnp.tile(bbi_t, (S5_G, 1))),
            bd(jnp.tile(cre_w, (1, S5_N // HEAD))), bd(jnp.tile(cim_w, (1, S5_N // HEAD))))


def _scan8(xr, xi, tab_ref, lanes, reverse):
    for k in range(3):
        sh = (8 - 2 ** k) if reverse else 2 ** k
        sr = pltpu.roll(xr, sh, 0)
        si = pltpu.roll(xi, sh, 0)
        mr = tab_ref[2 * k, :, lanes]
        mi = tab_ref[2 * k + 1, :, lanes]
        xr, xi = xr + (mr * sr - mi * si), xi + (mr * si + mi * sr)
    return xr, xi


S5_LANES = 512


def _gelu(y):
    t = jnp.tanh(GELU_C * (y + GELU_A * y * y * y))
    return 0.5 * y * (1.0 + t), t


def _s5_fwd(u, lre, lim, ldt, bre_t, bim_t, cre_w, cim_w, d_row, w_glu, b_glu, jobs=()):
    tp = u.shape[0]
    R = ROW_BLK

    def body(u_ref, lre_ref, lim_ref, ldt_ref, bre_ref, bim_ref, cre_ref, cim_ref, d_ref, wg_ref, bg_ref,
             y_ref, xr_ref, xi_ref, bbd_r, bbd_i, cbd_r, cbd_i, tab_ref, car_r, car_i):
        @pl.when(pl.program_id(0) == 0)
        def _():
            lbr, lbi, bbr, bbi = _s5_disc(lre_ref[...], lim_ref[...], ldt_ref[...], bre_ref[...], bim_ref[...])
            br, bi, cr, ci = _s5_block_diag(bbr, bbi, cre_ref[...], cim_ref[...])
            bbd_r[...] = br.astype(MM)
            bbd_i[...] = bi.astype(MM)
            cbd_r[...] = cr.astype(MM)
            cbd_i[...] = ci.astype(MM)
            _store_tables(tab_ref, _s5_tables(lbr, lbi, False))
            car_r[...] = jnp.zeros_like(car_r)
            car_i[...] = jnp.zeros_like(car_i)

        u = u_ref[...]
        ub = u.astype(MM)
        xr_ref[...] = jnp.dot(ub, bbd_r[...], preferred_element_type=F32)
        xi_ref[...] = jnp.dot(ub, bbd_i[...], preferred_element_type=F32)
        for j in range(S5_N // S5_LANES):
            lanes = pl.ds(j * S5_LANES, S5_LANES)
            pr = tab_ref[6, :, lanes]
            pi = tab_ref[7, :, lanes]

            def step(g, carry):
                cr, ci = carry
                rows = pl.ds(pl.multiple_of(g * 8, 8), 8)
                xr, xi = _scan8(xr_ref[rows, lanes], xi_ref[rows, lanes], tab_ref, lanes, False)
                br = jnp.broadcast_to(cr[7:8, :], cr.shape)
                bi = jnp.broadcast_to(ci[7:8, :], ci.shape)
                xr = xr + (pr * br - pi * bi)
                xi = xi + (pr * bi + pi * br)
                xr_ref[rows, lanes] = xr
                xi_ref[rows, lanes] = xi
                return xr, xi

            cr, ci = lax.fori_loop(0, R // 8, step, (car_r[:, lanes], car_i[:, lanes]), unroll=2)
            car_r[:, lanes] = cr
            car_i[:, lanes] = ci
        y = _dot_nt(xr_ref[...], cbd_r[...]) - _dot_nt(xi_ref[...], cbd_i[...]) + d_ref[...] * u
        yg, _ = _gelu(y)
        z = _dot(yg, wg_ref[...]) + bg_ref[...]
        y_ref[...] = yg * jax.nn.sigmoid(z)

    full = lambda a: pl.BlockSpec(a.shape, lambda i: (0,) * a.ndim)
    small = [lre, lim, ldt, bre_t, bim_t, cre_w, cim_w, d_row, w_glu, b_glu]
    return _call(
        body, "s5_fwd", (tp // R,),
        [pl.BlockSpec((R, S5_W), lambda i: (i, 0))] + [full(a) for a in small],
        [pl.BlockSpec((R, S5_W), lambda i: (i, 0)), pl.BlockSpec((R, S5_N), lambda i: (i, 0)),
         pl.BlockSpec((R, S5_N), lambda i: (i, 0))],
        [jax.ShapeDtypeStruct((tp, S5_W), F32), jax.ShapeDtypeStruct((tp, S5_N), F32),
         jax.ShapeDtypeStruct((tp, S5_N), F32)],
        [pltpu.VMEM((S5_W, S5_N), MM)] * 4 + [pltpu.VMEM((8, 8, S5_N), F32), pltpu.VMEM((8, S5_N), F32),
                                              pltpu.VMEM((8, S5_N), F32)],
        (u, *small), jobs)


def _s5_bwd(dy_out, u, xr, xi, lre, lim, ldt, bre_t, bim_t, cre_w, cim_w, d_row, w_glu, b_glu, jobs=()):
    tp = u.shape[0]
    R = ROW_BLK
    nb = tp // R

    def body(dyo_ref, u_ref, xr_ref, xi_ref, xpr_ref, xpi_ref,
             lre_ref, lim_ref, ldt_ref, bre_ref, bim_ref, cre_ref, cim_ref, d_ref, wg_ref, bg_ref,
             du_ref, dlre_ref, dlim_ref, dldt_ref, dbre_ref, dbim_ref, dcre_ref, dcim_ref, dd_ref, dwg_ref, dbg_ref,
             bbd_r, bbd_i, cbd_r, cbd_i, tab_ref, car_r, car_i, gr_ref, gi_ref, xer_ref, xei_ref,
             abr, abi, acr, aci, adr, adi):
        i = pl.program_id(0)

        @pl.when(i == 0)
        def _():
            lbr, lbi, bbr, bbi = _s5_disc(lre_ref[...], lim_ref[...], ldt_ref[...], bre_ref[...], bim_ref[...])
            br, bi, cr, ci = _s5_block_diag(bbr, bbi, cre_ref[...], cim_ref[...])
            bbd_r[...] = br.astype(MM)
            bbd_i[...] = bi.astype(MM)
            cbd_r[...] = cr.astype(MM)
            cbd_i[...] = ci.astype(MM)
            _store_tables(tab_ref, _s5_tables(lbr, lbi, True))
            for ref in (car_r, car_i, abr, abi, acr, aci, adr, adi, dd_ref, dwg_ref, dbg_ref):
                ref[...] = jnp.zeros_like(ref)

        u = u_ref[...]
        xrv = xr_ref[...]
        xiv = xi_ref[...]
        y = _dot_nt(xrv, cbd_r[...]) - _dot_nt(xiv, cbd_i[...]) + d_ref[...] * u
        yg, t = _gelu(y)
        z = _dot(yg, wg_ref[...]) + bg_ref[...]
        s = jax.nn.sigmoid(z)
        dout = dyo_ref[...]
        dz = dout * yg * s * (1.0 - s)
        dyg = dout * s + _dot_nt(dz, wg_ref[...])
        dwg_ref[...] += _dot_tn(yg, dz)
        dbg_ref[...] += _colsum(dz)
        dy = dyg * (0.5 * (1.0 + t) + 0.5 * y * (1.0 - t * t) * GELU_C * (1.0 + 3.0 * GELU_A * y * y))
        dd_ref[...] += _colsum(dy * u)
        acr[...] += _dot_tn(dy, xrv)
        aci[...] -= _dot_tn(dy, xiv)
        gr_ref[...] = _dot(dy, cbd_r[...])
        gi_ref[...] = -_dot(dy, cbd_i[...])
        has_prev = (i < nb - 1).astype(F32)
        xer_ref[0:8, :] = xpr_ref[...] * has_prev
        xei_ref[0:8, :] = xpi_ref[...] * has_prev
        xer_ref[8:R + 8, :] = xrv
        xei_ref[8:R + 8, :] = xiv
        row = lax.broadcasted_iota(jnp.int32, (8, S5_LANES), 0)
        for j in range(S5_N // S5_LANES):
            lanes = pl.ds(j * S5_LANES, S5_LANES)
            pr = tab_ref[6, :, lanes]
            pi = tab_ref[7, :, lanes]

            def step(n, carry):
                cr, ci, sar, sai = carry
                g = R // 8 - 1 - n
                r0 = pl.multiple_of(g * 8, 8)
                rows = pl.ds(r0, 8)
                gr, gi = _scan8(gr_ref[rows, lanes], gi_ref[rows, lanes], tab_ref, lanes, True)
                br = jnp.broadcast_to(cr[0:1, :], cr.shape)
                bi = jnp.broadcast_to(ci[0:1, :], ci.shape)
                gr = gr + (pr * br - pi * bi)
                gi = gi + (pr * bi + pi * br)
                gr_ref[rows, lanes] = gr
                gi_ref[rows, lanes] = gi
                last = row == 7
                xpr = pltpu.roll(jnp.where(last, xer_ref[rows, lanes], xer_ref[pl.ds(r0 + 8, 8), lanes]), 1, 0)
                xpi = pltpu.roll(jnp.where(last, xei_ref[rows, lanes], xei_ref[pl.ds(r0 + 8, 8), lanes]), 1, 0)
                return gr, gi, sar + (gr * xpr + gi * xpi), sai + (gi * xpr - gr * xpi)

            cr, ci, sar, sai = lax.fori_loop(
                0, R // 8, step, (car_r[:, lanes], car_i[:, lanes], adr[:, lanes], adi[:, lanes]), unroll=2)
            car_r[:, lanes] = cr
            car_i[:, lanes] = ci
            adr[:, lanes] = sar
            adi[:, lanes] = sai
        grv = gr_ref[...]
        giv = gi_ref[...]
        du_ref[...] = (dy * d_ref[...] + _dot_nt(grv, bbd_r[...]) + _dot_nt(giv, bbd_i[...])).astype(du_ref.dtype)
        abr[...] += _dot_tn(u, grv)
        abi[...] += _dot_tn(u, giv)

        @pl.when(i == nb - 1)
        def _():
            mask = _bd_mask()
            r16 = lax.broadcasted_iota(jnp.int32, (S5_H, S5_W), 1)
            h16 = lax.broadcasted_iota(jnp.int32, (S5_H, S5_W), 0)
            fold_b = jnp.bitwise_and(r16, S5_H - 1) == h16
            c64 = lax.broadcasted_iota(jnp.int32, (S5_N, S5_P), 0)
            p64 = lax.broadcasted_iota(jnp.int32, (S5_N, S5_P), 1)
            fold_c = jnp.bitwise_and(c64, S5_P - 1) == p64
            dbbr = _dot_sel_lhs(fold_b, jnp.where(mask, abr[...], 0.0))
            dbbi = _dot_sel_lhs(fold_b, jnp.where(mask, abi[...], 0.0))
            dcre_ref[...] = _dot_sel_rhs(jnp.where(mask, acr[...], 0.0), fold_c)
            dcim_ref[...] = _dot_sel_rhs(jnp.where(mask, aci[...], 0.0), fold_c)
            dlbr = _colsum(adr[...])
            dlbi = _colsum(adi[...])
            _, vjp = jax.vjp(_s5_disc, lre_ref[...], lim_ref[...], ldt_ref[...], bre_ref[...], bim_ref[...])
            dlre, dlim, dldt, dbre, dbim = vjp((dlbr, dlbi, dbbr, dbbi))
            dlre_ref[...] = dlre
            dlim_ref[...] = dlim
            dbre_ref[...] = dbre
            dbim_ref[...] = dbim
            gsel = jnp.right_shift(lax.broadcasted_iota(jnp.int32, (S5_N, HEAD), 0), 6) == \
                lax.broadcasted_iota(jnp.int32, (S5_N, HEAD), 1)
            dldt_ref[...] = _dot_sel_rhs(dldt, gsel)

    full = lambda a: pl.BlockSpec(a.shape, lambda i: (0,) * a.ndim)
    rev = lambda w: pl.BlockSpec((R, w), lambda i: (nb - 1 - i, 0))
    prev8 = pl.BlockSpec((8, S5_N), lambda i: (jnp.maximum((nb - 1 - i) * (R // 8) - 1, 0), 0))
    small = [lre, lim, ldt, bre_t, bim_t, cre_w, cim_w, d_row, w_glu, b_glu]
    outs = [((tp, S5_W), rev(S5_W))] + [
        (s, pl.BlockSpec(s, lambda i: (0, 0))) for s in
        [(1, S5_N), (1, S5_N), (1, HEAD), (S5_H, S5_N), (S5_H, S5_N), (S5_W, S5_P), (S5_W, S5_P),
         (1, S5_W), (S5_W, S5_W), (1, S5_W)]]
    return _call(
        body, "s5_bwd", (nb,),
        [rev(S5_W), rev(S5_W), rev(S5_N), rev(S5_N), prev8, prev8] + [full(a) for a in small],
        [o[1] for o in outs], [jax.ShapeDtypeStruct(o[0], MM if n == 0 else F32) for n, o in enumerate(outs)],
        [pltpu.VMEM((S5_W, S5_N), MM)] * 4 + [
            pltpu.VMEM((8, 8, S5_N), F32), pltpu.VMEM((8, S5_N), F32), pltpu.VMEM((8, S5_N), F32),
            pltpu.VMEM((R, S5_N), F32), pltpu.VMEM((R, S5_N), F32),
            pltpu.VMEM((R + 8, S5_N), F32), pltpu.VMEM((R + 8, S5_N), F32)] + [pltpu.VMEM((S5_W, S5_N), F32)] * 4 + [
            pltpu.VMEM((8, S5_N), F32), pltpu.VMEM((8, S5_N), F32)],
        (dy_out, u, xr, xi, xr, xi, *small), jobs)


RET_CHUNK = ROW_BLK
LOG_GAMMA = [math.log1p(-2.0 ** (-5 - h)) for h in range(RET_H)]
GAMMA_CHUNK = [math.exp(RET_CHUNK * lg) for lg in LOG_GAMMA]
_DECAY_SCRATCH = [pltpu.VMEM((RET_H, RET_CHUNK, RET_CHUNK), F32), pltpu.VMEM((RET_H, RET_CHUNK, HEAD), F32),
                  pltpu.VMEM((RET_H, RET_CHUNK, HEAD), F32)]


def _fill_decay(dm_ref, ze_ref, xi_ref):
    C = RET_CHUNK
    diff = (lax.broadcasted_iota(jnp.int32, (C, C), 0) - lax.broadcasted_iota(jnp.int32, (C, C), 1)).astype(F32)
    r = lax.broadcasted_iota(jnp.int32, (C, HEAD), 0).astype(F32)
    for h, lg in enumerate(LOG_GAMMA):
        dm_ref[h] = jnp.where(diff >= 0.0, jnp.exp(jnp.maximum(diff, 0.0) * lg), 0.0)
        ze_ref[h] = jnp.exp((C - 1.0 - r) * lg)
        xi_ref[h] = jnp.exp((r + 1.0) * lg)


def _ret_fwd(q, k, v, jobs=()):
    tp = q.shape[0]
    C = RET_CHUNK
    nc = tp // C

    def body(q_ref, k_ref, v_ref, o_ref, st_ref, s_ref, dm_ref, ze_ref, xi_ref):
        @pl.when(pl.program_id(0) == 0)
        def _():
            s_ref[...] = jnp.zeros_like(s_ref)
            _fill_decay(dm_ref, ze_ref, xi_ref)

        for h in range(RET_H):
            sl = slice(h * HEAD, (h + 1) * HEAD)
            qh, kh, vh = q_ref[:, sl], k_ref[:, sl], v_ref[:, sl]
            sh = s_ref[h]
            st_ref[0, sl, :] = sh
            scores = _dot_nt(qh, kh) * dm_ref[h]
            o_ref[:, sl] = _dot(scores, vh) + _dot(qh, sh) * xi_ref[h]
            s_ref[h] = GAMMA_CHUNK[h] * sh + _dot_tn(kh.astype(F32) * ze_ref[h], vh)

    blk = pl.BlockSpec((C, RET_W), lambda c: (c, 0))
    return _call(
        body, "ret_fwd", (nc,), [blk, blk, blk], [blk, pl.BlockSpec((1, RET_W, HEAD), lambda c: (c, 0, 0))],
        [jax.ShapeDtypeStruct((tp, RET_W), F32), jax.ShapeDtypeStruct((nc, RET_W, HEAD), F32)],
        [pltpu.VMEM((RET_H, HEAD, HEAD), F32)] + _DECAY_SCRATCH, (q, k, v), jobs)


def _ret_bwd(q, k, v, do, states, cos2, sin2, jobs=()):
    tp = q.shape[0]
    C = RET_CHUNK
    nc = tp // C

    def body(q_ref, k_ref, v_ref, do_ref, st_ref, cos_ref, sin_ref,
             dq_ref, dk_ref, dv_ref, ds_ref, dm_ref, ze_ref, xi_ref):
        @pl.when(pl.program_id(0) == 0)
        def _():
            ds_ref[...] = jnp.zeros_like(ds_ref)
            _fill_decay(dm_ref, ze_ref, xi_ref)

        cos = cos_ref[...]
        sin = sin_ref[...]
        for h in range(RET_H):
            sl = slice(h * HEAD, (h + 1) * HEAD)
            qh, kh, vh = q_ref[:, sl], k_ref[:, sl], v_ref[:, sl]
            dmh = dm_ref[h]
            sh = st_ref[0, sl, :]
            dsn = ds_ref[h]
            doh = do_ref[:, sl]
            dox = doh * xi_ref[h]
            a = _dot_nt(qh, kh) * dmh
            dqk = _dot_nt(doh, vh) * dmh
            kz = kh.astype(F32) * ze_ref[h]
            dv_ref[:, sl] = (_dot_tn(a, doh) + _dot(kz, dsn)).astype(dv_ref.dtype)
            dqr = _dot(dqk, kh) + _dot_nt(dox, sh)
            dkr = _dot_tn(dqk, qh) + ze_ref[h] * _dot_nt(vh, dsn)
            ds_ref[h] = GAMMA_CHUNK[h] * dsn + _dot_tn(qh, dox)
            dq_ref[:, sl] = (dqr * cos - pltpu.roll(dqr, HEAD // 2, 1) * sin).astype(dq_ref.dtype)
            dk_ref[:, sl] = ((dkr * cos - pltpu.roll(dkr, HEAD // 2, 1) * sin) * (HEAD ** -0.5)).astype(dk_ref.dtype)

    blk = pl.BlockSpec((C, RET_W), lambda c: (nc - 1 - c, 0))
    tab = pl.BlockSpec((C, HEAD), lambda c: (nc - 1 - c, 0))
    return _call(
        body, "ret_bwd", (nc,),
        [blk, blk, blk, blk, pl.BlockSpec((1, RET_W, HEAD), lambda c: (nc - 1 - c, 0, 0)), tab, tab],
        [blk, blk, blk], [jax.ShapeDtypeStruct((tp, RET_W), MM)] * 3,
        [pltpu.VMEM((RET_H, HEAD, HEAD), F32)] + _DECAY_SCRATCH, (q, k, v, do, states, cos2, sin2), jobs)


def _gn_gate(o, gate, gn_g, gn_b):
    xhat, rstd = _ln_fwd(o, GN_EPS)
    on = xhat * gn_g + gn_b
    s = jax.nn.sigmoid(gate)
    return gate * s * on, xhat, rstd, on, s


def _post_up(o, gate, ys5, xhat0, gn_g, gn_b, li_g, li_b, l1_g, l1_b, w_out, w_up, jobs=()):
    tp = o.shape[0]
    R = ROW_BLK

    def body(o_ref, g_ref, ys_ref, xh0_ref, gng, gnb, lig, lib, l1g, l1b, wo_ref, wu_ref,
             ycat_ref, xh1_ref, rstd1_ref, h1b_ref, pre_ref):
        ycat_ref[:, 0:S5_W] = ys_ref[...].astype(ycat_ref.dtype)
        for h in range(RET_H):
            sl = slice(h * HEAD, (h + 1) * HEAD)
            yret = _gn_gate(o_ref[:, sl], g_ref[:, sl], gng[:, sl], gnb[:, sl])[0]
            ycat_ref[:, S5_W + h * HEAD:S5_W + (h + 1) * HEAD] = yret.astype(ycat_ref.dtype)
        mixed = _dot(ycat_ref[...], wo_ref[...])
        h0 = xh0_ref[...] * lig[...] + lib[...]
        xh1, rstd1 = _ln_fwd(ALPHA * h0 + mixed, LN_EPS)
        xh1_ref[...] = xh1
        rstd1_ref[...] = rstd1
        h1b = (xh1 * l1g[...] + l1b[...]).astype(MM)
        h1b_ref[...] = h1b
        for d in range(N_DEV):
            pre_ref[:, d * FF_BLK:(d + 1) * FF_BLK] = jnp.maximum(_dot(h1b, wu_ref[d]), 0.0)

    row = lambda w: pl.BlockSpec((R, w), lambda i: (i, 0))
    full = lambda a: pl.BlockSpec(a.shape, lambda i: (0,) * a.ndim)
    vecs = [gn_g, gn_b, li_g, li_b, l1_g, l1_b]
    outs = [(row(D_MODEL), jax.ShapeDtypeStruct((tp, D_MODEL), MM)), (row(D_MODEL), jax.ShapeDtypeStruct((tp, D_MODEL), F32)),
            (row(1), jax.ShapeDtypeStruct((tp, 1), F32)), (row(D_MODEL), jax.ShapeDtypeStruct((tp, D_MODEL), MM)),
            (row(D_FF), jax.ShapeDtypeStruct((tp, D_FF), F32))]
    return _call(
        body, "post_up", (tp // R,),
        [row(RET_W), row(RET_W), row(S5_W), row(D_MODEL)] + [full(a) for a in vecs] + [_VMEM, _VMEM],
        [o[0] for o in outs], [o[1] for o in outs], [], (o, gate, ys5, xhat0, *vecs, w_out, w_up), jobs)


def _post_down(pre, xhat1, tgt, l1_g, l1_b, l2_g, l2_b, w_down):
    tp = pre.shape[0]
    seq = tgt.shape[0]
    R = ROW_BLK

    def body(pre_ref, xh1_ref, ta, tb, tc, l1g, l1b, l2g, l2b, wd_ref,
             dr2_ref, dffb_ref, loss_ref, dl2g_ref, dl2b_ref, tgt_ref):
        i = pl.program_id(0)

        @pl.when(i == 0)
        def _():
            for ref in (loss_ref, dl2g_ref, dl2b_ref):
                ref[...] = jnp.zeros_like(ref)

        tgt_ref[0:CHUNK, :] = ta[...]
        tgt_ref[CHUNK:2 * CHUNK, :] = tb[...]
        tgt_ref[2 * CHUNK:3 * CHUNK, :] = tc[...]
        ff = jnp.zeros((R, D_MODEL), F32)
        for d in range(N_DEV):
            pre = pre_ref[:, d * FF_BLK:(d + 1) * FF_BLK]
            ff = ff + _dot(pre * pre, wd_ref[d * FF_BLK:(d + 1) * FF_BLK, :])
        h1 = xh1_ref[...] * l1g[...] + l1b[...]
        xh2, rstd2 = _ln_fwd(ALPHA * h1 + ff, LN_EPS)
        h2 = xh2 * l2g[...] + l2b[...]
        valid = (i * R + lax.broadcasted_iota(jnp.int32, (R, 1), 0)) >= CHUNK
        err = jnp.where(valid, h2 - tgt_ref[...], 0.0)
        loss_ref[...] += 0.5 * jnp.sum(err * err) / D_MODEL
        dh2 = err * (1.0 / D_MODEL)
        dl2g_ref[...] += _colsum(dh2 * xh2)
        dl2b_ref[...] += _colsum(dh2)
        dr2 = _ln_bwd(dh2 * l2g[...], xh2, rstd2)
        dr2_ref[...] = dr2
        dffb_ref[...] = dr2.astype(MM)

    row = lambda w: pl.BlockSpec((R, w), lambda i: (i, 0))
    full = lambda a: pl.BlockSpec(a.shape, lambda i: (0,) * a.ndim)
    vecs = [l1_g, l1_b, l2_g, l2_b]
    acc = lambda s: (pl.BlockSpec(s, lambda i: (0, 0)), jax.ShapeDtypeStruct(s, F32))
    outs = [(row(D_MODEL), jax.ShapeDtypeStruct((tp, D_MODEL), F32)), (row(D_MODEL), jax.ShapeDtypeStruct((tp, D_MODEL), MM)),
            acc((8, HEAD)), acc((1, D_MODEL)), acc((1, D_MODEL))]
    return pl.pallas_call(
        body, name="post_down", grid=(tp // R,),
        in_specs=[row(D_FF), row(D_MODEL)] + _shift3(seq // CHUNK) + [full(a) for a in vecs] + [_VMEM],
        out_specs=[o[0] for o in outs], out_shape=[o[1] for o in outs],
        scratch_shapes=[pltpu.VMEM((R, D_MODEL), F32)],
        compiler_params=_params(("arbitrary",)),
    )(pre, xhat1, tgt, tgt, tgt, *vecs, w_down)


def _mlp_bwd(h1b, dffb, pre, w_up, w_down):
    tp = h1b.shape[0]
    R = MLP_ROWS if tp % MLP_ROWS == 0 else ROW_BLK
    nr = tp // R

    def body(h_ref, df_ref, pre_ref, wu_ref, wd_ref, gup_ref, gdn_ref, dh1_ref, aup, adn):
        d = pl.program_id(0)
        r = pl.program_id(1)

        @pl.when(r == 0)
        def _():
            aup[...] = jnp.zeros_like(aup)
            adn[...] = jnp.zeros_like(adn)

        h = h_ref[...]
        df = df_ref[...]
        wu = wu_ref[0]
        wd = wd_ref[0]
        pre = pre_ref[...]
        dpre = (_dot_nt(df, wd) * (2.0 * pre)).astype(MM)

        aup[...] += _dot_tn(h, dpre)
        adn[...] += _dot_tn(pre * pre, df)
        contrib = _dot_nt(dpre, wu)
        rows = pl.ds(pl.multiple_of(r * R, 64), R)

        @pl.when(d == 0)
        def _():
            dh1_ref[rows, :] = contrib

        @pl.when(d > 0)
        def _():
            dh1_ref[rows, :] += contrib

        @pl.when(r == nr - 1)
        def _():
            gup_ref[0] = aup[...].astype(gup_ref.dtype)
            gdn_ref[0] = adn[...].astype(gdn_ref.dtype)

    return pl.pallas_call(
        body, name="mlp_bwd", grid=(N_DEV, nr),
        in_specs=[pl.BlockSpec((R, D_MODEL), lambda d, r: (r, 0)), pl.BlockSpec((R, D_MODEL), lambda d, r: (r, 0)),
                  pl.BlockSpec((R, FF_BLK), lambda d, r: (r, d)),
                  pl.BlockSpec((1, D_MODEL, FF_BLK), lambda d, r: (d, 0, 0)),
                  pl.BlockSpec((1, FF_BLK, D_MODEL), lambda d, r: (d, 0, 0))],
        out_specs=[pl.BlockSpec((1, D_MODEL, FF_BLK), lambda d, r: (d, 0, 0)),
                   pl.BlockSpec((1, FF_BLK, D_MODEL), lambda d, r: (d, 0, 0)), _VMEM],
        out_shape=[jax.ShapeDtypeStruct((N_DEV, D_MODEL, FF_BLK), MM), jax.ShapeDtypeStruct((N_DEV, FF_BLK, D_MODEL), MM),
                   jax.ShapeDtypeStruct((tp, D_MODEL), F32)],
        scratch_shapes=[pltpu.VMEM((D_MODEL, FF_BLK), F32), pltpu.VMEM((FF_BLK, D_MODEL), F32)],
        compiler_params=_params(("arbitrary", "arbitrary")),
    )(h1b, dffb, pre, w_up, w_down.reshape(N_DEV, FF_BLK, D_MODEL))


def _post_bwd(dh1m, dr2, xhat1, rstd1, ycat, o, gate, gn_g, gn_b, l1_g, w_out, jobs=()):
    tp = o.shape[0]
    R = ROW_BLK
    nb = tp // R

    def body(dm_ref, dr2_ref, xh1_ref, rs1_ref, yc_ref, o_ref, g_ref, gng, gnb, l1g, wo_ref,
             do_ref, dg_ref, dys_ref, dh0_ref, gwo_ref, dl1g_ref, dl1b_ref, dgng_ref, dgnb_ref, awo):
        i = pl.program_id(0)

        @pl.when(i == 0)
        def _():
            for ref in (awo, dl1g_ref, dl1b_ref, dgng_ref, dgnb_ref):
                ref[...] = jnp.zeros_like(ref)

        dh1 = dm_ref[...] + ALPHA * dr2_ref[...]
        xh1 = xh1_ref[...]
        dl1g_ref[...] += _colsum(dh1 * xh1)
        dl1b_ref[...] += _colsum(dh1)
        dr1 = _ln_bwd(dh1 * l1g[...], xh1, rs1_ref[...])
        dh0_ref[...] = ALPHA * dr1
        dmix = dr1.astype(MM)
        awo[...] += _dot_tn(yc_ref[...], dmix)
        dyc = _dot_nt(dmix, wo_ref[...])
        dys_ref[...] = dyc[:, 0:S5_W]
        for h in range(RET_H):
            sl = slice(h * HEAD, (h + 1) * HEAD)
            gt = g_ref[:, sl]
            _, xhat, rstd, on, s = _gn_gate(o_ref[:, sl], gt, gng[:, sl], gnb[:, sl])
            dyr = dyc[:, S5_W + h * HEAD:S5_W + (h + 1) * HEAD]
            dg_ref[:, sl] = (dyr * on * (s * (1.0 + gt * (1.0 - s)))).astype(dg_ref.dtype)
            don = dyr * gt * s
            dgng_ref[:, sl] += _colsum(don * xhat)
            dgnb_ref[:, sl] += _colsum(don)
            do_ref[:, sl] = _ln_bwd(don * gng[:, sl], xhat, rstd)

        @pl.when(i == nb - 1)
        def _():
            gwo_ref[...] = awo[...].astype(gwo_ref.dtype)

    row = lambda w: pl.BlockSpec((R, w), lambda i: (i, 0))
    full = lambda a: pl.BlockSpec(a.shape, lambda i: (0,) * a.ndim)
    acc = lambda s, dt=F32: (pl.BlockSpec(s, lambda i: (0, 0)), jax.ShapeDtypeStruct(s, dt))
    outs = [(row(RET_W), jax.ShapeDtypeStruct((tp, RET_W), F32)), (row(RET_W), jax.ShapeDtypeStruct((tp, RET_W), MM)),
            (row(S5_W), jax.ShapeDtypeStruct((tp, S5_W), F32)), (row(D_MODEL), jax.ShapeDtypeStruct((tp, D_MODEL), F32)),
            acc((D_MODEL, D_MODEL), MM), acc((1, D_MODEL)), acc((1, D_MODEL)), acc((1, RET_W)), acc((1, RET_W))]
    return _call(
        body, "post_bwd", (nb,),
        [row(D_MODEL), row(D_MODEL), row(D_MODEL), row(1), row(D_MODEL), row(RET_W), row(RET_W),
         full(gn_g), full(gn_b), full(l1_g), _VMEM],
        [o[0] for o in outs], [o[1] for o in outs],
        [pltpu.VMEM((D_MODEL, D_MODEL), F32)],
        (dh1m, dr2, xhat1, rstd1, ycat, o, gate, gn_g, gn_b, l1_g, w_out), jobs)


def _in_bwd(du, dq, dk, dv, dg, dh0r, xhat0, rstd0, li_g, li_b, w_int, jobs=()):
    tp = du.shape[0]
    R = PROJ_ROWS if tp % PROJ_ROWS == 0 else ROW_BLK
    nb = tp // R
    segs = [(0, S5_W)] + [(S5_W + n * RET_W, S5_W + (n + 1) * RET_W) for n in range(4)]

    def body(du_ref, dq_ref, dk_ref, dv_ref, dg_ref, dh0r_ref, xh_ref, rs_ref, lig, lib, w_ref,
             gx_ref, dmeta_ref, gw_ref, dlg_ref, dlb_ref, aw, stage, out_sems):
        i = pl.program_id(0)
        slot = i % 2

        def to_gx(step_slot, first):
            if first:
                return pltpu.make_async_copy(stage.at[0, CHUNK:R, :], gx_ref.at[0:R - CHUNK, :], out_sems.at[0])
            return pltpu.make_async_copy(stage.at[step_slot], gx_ref.at[pl.ds(i * R - CHUNK, R), :], out_sems.at[step_slot])

        @pl.when(i == 0)
        def _():
            for ref in (aw, dlg_ref, dlb_ref):
                ref[...] = jnp.zeros_like(ref)

        @pl.when(i >= 3)
        def _():
            to_gx(slot, False).wait()

        valid = (i * R + lax.broadcasted_iota(jnp.int32, (R, 1), 0)) >= PAD
        xh = xh_ref[...]
        hb = (xh * lig[...] + lib[...]).astype(MM)
        dh0 = dh0r_ref[...]
        for (lo, hi), ref in zip(segs, (du_ref, dq_ref, dk_ref, dv_ref, dg_ref)):
            dseg = jnp.where(valid, ref[...], 0.0).astype(MM)
            dh0 = dh0 + _dot(dseg, w_ref[lo:hi, :])
            aw[lo:hi, :] += _dot_tn(dseg, hb)
        dlg_ref[...] += _colsum(dh0 * xh)
        dlb_ref[...] += _colsum(dh0)
        draw = _ln_bwd(dh0 * lig[...], xh, rs_ref[...])
        stage[slot] = draw

        @pl.when(i == 0)
        def _():
            dmeta_ref[...] = draw[PAD:CHUNK, :]
            first = to_gx(0, True)
            first.start()
            first.wait()

        @pl.when(i > 0)
        def _():
            to_gx(slot, False).start()

        @pl.when(i == nb - 1)
        def _():
            gw_ref[...] = aw[...].astype(gw_ref.dtype)
            for back in (1, 0):
                if nb - 1 - back >= 1:
                    to_gx((nb - 1 - back) % 2, False).wait()

    row = lambda w: pl.BlockSpec((R, w), lambda i: (i, 0))
    full = lambda a: pl.BlockSpec(a.shape, lambda i: (0,) * a.ndim)
    acc = lambda s, dt=F32: (pl.BlockSpec(s, lambda i: (0, 0)), jax.ShapeDtypeStruct(s, dt))
    outs = [(_ANY, jax.ShapeDtypeStruct((tp - CHUNK, D_MODEL), F32)), acc((N_META, D_MODEL)), acc((PROJ_W, D_MODEL), MM),
            acc((1, D_MODEL)), acc((1, D_MODEL))]
    return _call(
        body, "in_bwd", (nb,),
        [row(S5_W), row(RET_W), row(RET_W), row(RET_W), row(RET_W), row(D_MODEL), row(D_MODEL), row(1),
         full(li_g), full(li_b), _VMEM],
        [o[0] for o in outs], [o[1] for o in outs],
        [pltpu.VMEM((PROJ_W, D_MODEL), F32), pltpu.VMEM((2, R, D_MODEL), F32), pltpu.SemaphoreType.DMA((2,))],
        (du, dq, dk, dv, dg, dh0r, xhat0, rstd0, li_g, li_b, w_int), jobs)


def _place():
    return lax.axis_index("x"), lax.axis_index("y"), lax.axis_index("c")


def _dma_sems(n):
    return pltpu.SemaphoreType.DMA((n,))


def _job_gather(shard):
    def parts(ins, outs, sems):
        (src,), (out,), (send_sems, recv_sems, local_sem) = ins, outs, sems
        x, y, c = _place()
        north = c == 1
        me, sib = (x, y, c), (x, y, 1 - c)
        xn, yn, dg = (1 - x, y, c), (x, 1 - y, c), (1 - x, 1 - y, c)
        relay_from = (jnp.where(north, 1 - x, x), jnp.where(north, y, 1 - y), c)
        relay_to = (jnp.where(north, x, 1 - x), jnp.where(north, 1 - y, y), c)

        def slot(dev):
            return out.at[4 * dev[0] + 2 * dev[1] + dev[2]]

        def copy(k, block, to, from_input=False):
            return pltpu.make_async_remote_copy(
                src_ref=src if from_input else slot(block), dst_ref=slot(block),
                send_sem=send_sems.at[k], recv_sem=recv_sems.at[k], device_id=to, device_id_type=_MESH)

        mine = lambda: pltpu.make_async_copy(src, slot(me), local_sem.at[0])
        first = lambda: [copy(0, me, sib, True), copy(1, me, xn, True), copy(2, me, yn, True)]
        relayed = lambda: [copy(3, relay_from, relay_to), copy(4, xn, sib), copy(5, yn, sib)]
        return me, sib, xn, yn, dg, copy, mine, first, relayed

    def start(ins, outs, sems):
        mine, first = parts(ins, outs, sems)[6:8]
        mine().start()
        for cp in first():
            cp.start()

    def relay(ins, outs, sems):
        me, sib, xn, yn, dg, copy, mine, first, relayed = parts(ins, outs, sems)
        copy(1, xn, me).wait_recv()
        copy(2, yn, me).wait_recv()
        for cp in relayed():
            cp.start()

    def finish(ins, outs, sems):
        me, sib, xn, yn, dg, copy, mine, first, relayed = parts(ins, outs, sems)
        other = 1 - me[2]
        copy(3, dg, me).wait_recv()
        last = copy(6, dg, sib)
        last.start()
        copy(0, sib, me).wait_recv()
        for k, chip in ((4, xn), (5, yn), (6, dg)):
            copy(k, (chip[0], chip[1], other), me).wait_recv()
        for cp in first() + relayed() + [last]:
            cp.wait_send()
        mine().wait()

    return dict(ins=[shard], outs=[jax.ShapeDtypeStruct((N_DEV,) + shard.shape, shard.dtype)],
                sems=[_dma_sems(7), _dma_sems(7), _dma_sems(1)], start=start, middle=relay, finish=finish)


def _job_pair(g):
    def copies(ins, outs, sems):
        x, y, c = _place()
        return [pltpu.make_async_remote_copy(
            src_ref=ins[0].at[2 * j + (1 - c)], dst_ref=outs[0].at[j], send_sem=sems[0].at[j], recv_sem=sems[1].at[j],
            device_id=(x, y, 1 - c), device_id_type=_MESH) for j in range(4)]

    def start(ins, outs, sems):
        for cp in copies(ins, outs, sems):
            cp.start()

    def finish(ins, outs, sems):
        for cp in copies(ins, outs, sems):
            cp.wait()

    return dict(ins=[g], outs=[jax.ShapeDtypeStruct((4,) + g.shape[1:], g.dtype)], sems=[_dma_sems(4), _dma_sems(4)],
                start=start, finish=finish)


def _job_chips(p):
    def copies(ins, outs, sems):
        x, y, c = _place()
        chips = [(1 - x, y), (x, 1 - y), (1 - x, 1 - y)]
        return [pltpu.make_async_remote_copy(
            src_ref=ins[0].at[2 * chip[0] + chip[1]], dst_ref=outs[0].at[k], send_sem=sems[0].at[k],
            recv_sem=sems[1].at[k], device_id=(*chip, c), device_id_type=_MESH) for k, chip in enumerate(chips)]

    def start(ins, outs, sems):
        for cp in copies(ins, outs, sems):
            cp.start()

    def finish(ins, outs, sems):
        for cp in copies(ins, outs, sems):
            cp.wait()

    return dict(ins=[p], outs=[jax.ShapeDtypeStruct((3,) + p.shape[1:], p.dtype)], sems=[_dma_sems(3), _dma_sems(3)],
                start=start, finish=finish)


def _split_job_refs(jobs, ins, outs, sems):
    res, a, b, c = [], 0, 0, 0
    for job in jobs:
        na, nb, nc = len(job["ins"]), len(job["outs"]), len(job["sems"])
        res.append((ins[a:a + na], outs[b:b + nb], sems[c:c + nc]))
        a, b, c = a + na, b + nb, c + nc
    return res


def _call(body, name, grid, in_specs, out_specs, out_shape, scratch, args, jobs=(), prefetch=None, early=0):
    jobs = list(jobs)
    n_in, n_out, n_scr = len(in_specs), len(out_specs), len(scratch)
    j_in = [a for job in jobs for a in job["ins"]]
    j_out = [o for job in jobs for o in job["outs"]]
    j_scr = [s for job in jobs for s in job["sems"]]
    nsteps = grid[0]
    n_pre = 0 if prefetch is None else 1

    def wrapped(*refs):
        pre, refs = refs[:n_pre], refs[n_pre:]
        ins, jins = refs[:n_in], refs[n_in:n_in + len(j_in)]
        refs = refs[n_in + len(j_in):]
        outs, jouts = refs[:n_out], refs[n_out:n_out + len(j_out)]
        refs = refs[n_out + len(j_out):]
        scr, jscr = refs[:n_scr], refs[n_scr:]
        per_job = _split_job_refs(jobs, jins, jouts, jscr)

        def middle():
            for job, r in zip(jobs, per_job):
                if "middle" in job:
                    job["middle"](*r)

        @pl.when(pl.program_id(0) == 0)
        def _():
            for job, r in zip(jobs, per_job):
                job["start"](*r)

        if nsteps >= 3:
            pl.when(pl.program_id(0) == nsteps // 2)(middle)

        if early:
            @pl.when(pl.program_id(0) == nsteps - 1)
            def _():
                for job, r in zip(jobs[:early], per_job[:early]):
                    job["finish"](*r)

        body(*pre, *ins, *outs, *scr, *[o for r in per_job[:early] for o in r[1]])

        @pl.when(pl.program_id(0) == nsteps - 1)
        def _():
            if nsteps < 3:
                middle()
            for job, r in zip(jobs[early:], per_job[early:]):
                job["finish"](*r)

    specs = dict(in_specs=list(in_specs) + [_ANY] * len(j_in), out_specs=list(out_specs) + [_ANY] * len(j_out),
                 scratch_shapes=list(scratch) + j_scr)
    if n_pre:
        specs = dict(grid_spec=pltpu.PrefetchScalarGridSpec(num_scalar_prefetch=1, grid=grid, **specs))
    else:
        specs["grid"] = grid
    res = pl.pallas_call(
        wrapped if jobs else body, name=name, out_shape=list(out_shape) + j_out,
        compiler_params=_params(("arbitrary",) * len(grid)), **specs,
    )(*([prefetch] if n_pre else []), *args, *j_in)
    return list(res[:n_out]), list(res[n_out:])


def _exchange(jobs, name):
    j_in = [a for job in jobs for a in job["ins"]]
    j_out = [o for job in jobs for o in job["outs"]]
    j_scr = [s for job in jobs for s in job["sems"]]

    def body(*refs):
        per_job = _split_job_refs(jobs, refs[:len(j_in)], refs[len(j_in):len(j_in) + len(j_out)],
                                  refs[len(j_in) + len(j_out):])
        for phase in ("start", "middle", "finish"):
            for job, r in zip(jobs, per_job):
                if phase in job:
                    job[phase](*r)

    return pl.pallas_call(body, name=name, out_shape=j_out, in_specs=[_ANY] * len(j_in), out_specs=[_ANY] * len(j_out),
                          scratch_shapes=j_scr)(*j_in)


def _pair_sum(gs, r1s, c_arr, name):
    n = len(gs)

    def body(c_ref, *refs):
        for a in range(n):
            refs[2 * n + a][...] = (refs[a][...].astype(F32) + refs[n + a][...].astype(F32)).astype(refs[2 * n + a].dtype)

    def blk(g, own):
        s = g.shape[1:]
        if own:
            return pl.BlockSpec((1,) + s, lambda j, c_ref: (2 * j + c_ref[0],) + (0,) * len(s))
        return pl.BlockSpec((1,) + s, lambda j, c_ref: (j,) + (0,) * len(s))

    return pl.pallas_call(
        body, name=name,
        grid_spec=pltpu.PrefetchScalarGridSpec(
            num_scalar_prefetch=1, grid=(4,),
            in_specs=[blk(g, True) for g in gs] + [blk(g, False) for g in gs],
            out_specs=[blk(g, False) for g in gs]),
        out_shape=[jax.ShapeDtypeStruct((4,) + g.shape[1:], g.dtype) for g in gs],
        compiler_params=_params(("arbitrary",)),
    )(c_arr, *gs, *r1s)


def _adamw_math(w, g, m, v):
    m = ADAM_B1 * m + (1.0 - ADAM_B1) * g
    v = ADAM_B2 * v + (1.0 - ADAM_B2) * (g * g)
    m_hat = m / (1.0 - ADAM_B1 ** ADAM_STEP)
    v_hat = v / (1.0 - ADAM_B2 ** ADAM_STEP)
    return -ADAM_LR * (m_hat / (jnp.sqrt(v_hat) + ADAM_EPS) + ADAM_WD * w), m, v


def _view(name, a):
    return jnp.swapaxes(a, -1, -2) if name in ("w_in", "s5_b_re", "s5_b_im") else a


def _adamw_shards(items, name, steps, chip, jobs=()):
    n = len(items)

    def body(chip_ref, *refs):
        for a in range(n):
            p_ref, r_ref, w_ref, m_ref, v_ref = refs[5 * a:5 * a + 5]
            g = ((p_ref[0].astype(F32) + r_ref[0].astype(F32)) + r_ref[1].astype(F32)) + r_ref[2].astype(F32)
            outs = refs[5 * n + 4 * a:5 * n + 4 * a + 4]
            outs[0][...] = g
            outs[1][...], outs[2][...], outs[3][...] = _adamw_math(w_ref[...], g, m_ref[...], v_ref[...])

    in_specs, out_specs, out_shape, flat = [], [], [], []
    for p, r, w, m, v in items:
        rows, cols = w.shape
        rb = rows // steps
        in_specs += [pl.BlockSpec((1, rb, cols), lambda i, c: (c[0], i, 0)), pl.BlockSpec((3, rb, cols), lambda i, c: (0, i, 0))]
        wblk = pl.BlockSpec((rb, cols), lambda i, c: (i, 0))
        in_specs += [wblk] * 3
        out_specs += [wblk] * 4
        out_shape += [jax.ShapeDtypeStruct(w.shape, F32)] * 4
        flat += [p, r, w, m, v]
    return _call(body, name, (steps,), in_specs, out_specs, out_shape, [], flat, jobs, prefetch=chip)


def _sum_devices(gathered, name):
    def body(gs_ref, g_ref):
        g = gs_ref[0]
        for s in range(1, N_DEV):
            g = g + gs_ref[s]
        g_ref[...] = g

    return pl.pallas_call(body, name=name, out_shape=jax.ShapeDtypeStruct(gathered.shape[1:], F32),
                          in_specs=[_VMEM], out_specs=_VMEM, compiler_params=_params())(gathered)


def _adamw_native(items, name):
    n = len(items)

    def body(*refs):
        for a in range(n):
            g, w, m, v = (refs[4 * a + t][...] for t in range(4))
            refs[4 * n + 3 * a][...], refs[4 * n + 3 * a + 1][...], refs[4 * n + 3 * a + 2][...] = _adamw_math(w, g, m, v)

    return pl.pallas_call(
        body, name=name, out_shape=[jax.ShapeDtypeStruct(it[1].shape, F32) for it in items for _ in range(3)],
        in_specs=[_VMEM] * (4 * n), out_specs=[_VMEM] * (3 * n), compiler_params=_params(),
    )(*[t for it in items for t in it])


SMALL = ["ln_in_g", "ln_in_b", "s5_lambda_re", "s5_lambda_im", "s5_log_dt", "s5_b_re", "s5_b_im", "s5_c_re", "s5_c_im",
         "s5_d", "s5_b_glu", "ret_gn_g", "ret_gn_b", "ln1_g", "ln1_b", "ln2_g", "ln2_b"]
LATE = ["ln_in_g", "ln_in_b", "meta_tokens"]
EARLY = [n for n in SMALL if n not in LATE] + ["s5_w_glu", "loss"]
LANE = 128


def _pack(arrs):
    parts = []
    for a in arrs:
        f = a.reshape(-1)
        parts.append(jnp.pad(f, (0, (-f.shape[0]) % LANE)))
    flat = jnp.concatenate(parts)
    rows = -(-flat.shape[0] // LANE)
    flat = jnp.pad(flat, (0, (-rows % 8) * LANE + rows * LANE - flat.shape[0]))
    return flat.reshape(-1, LANE)


def _unpack(packed, shapes):
    flat = packed.reshape(-1)
    out, off = [], 0
    for s in shapes:
        n = math.prod(s)
        out.append(flat[off:off + n].reshape(s))
        off += n + (-n) % LANE
    return out


def _rope_tables(tp):
    inv_freq = 1.0 / (ROPE_BASE ** (jnp.arange(0, HEAD, 2, dtype=F32) / HEAD))
    blk = (jnp.arange(tp // ROW_BLK, dtype=F32) * ROW_BLK)[:, None, None] * inv_freq
    off = (jnp.arange(ROW_BLK, dtype=F32) - float(PAD))[None, :, None] * inv_freq
    cos = (jnp.cos(blk) * jnp.cos(off) - jnp.sin(blk) * jnp.sin(off)).reshape(tp, HEAD // 2)
    sin = (jnp.sin(blk) * jnp.cos(off) + jnp.cos(blk) * jnp.sin(off)).reshape(tp, HEAD // 2)
    return jnp.concatenate([cos, cos], axis=1), jnp.concatenate([-sin, sin], axis=1)


def _local_step(x2d, tgt, meta, w_int, w_out, w_up, w_down, w_glu, sp, distributed):
    tp = x2d.shape[0] + CHUNK
    row = lambda a: a.reshape(1, -1)
    cos2, sin2 = _rope_tables(tp)
    li_g, li_b = row(sp["ln_in_g"]), row(sp["ln_in_b"])
    l1_g, l1_b, l2_g, l2_b = row(sp["ln1_g"]), row(sp["ln1_b"]), row(sp["ln2_g"]), row(sp["ln2_b"])
    gn_g, gn_b = row(sp["ret_gn_g"]), row(sp["ret_gn_b"])
    lre, lim = row(sp["s5_lambda_re"]), row(sp["s5_lambda_im"])
    ldt = row(jnp.repeat(sp["s5_log_dt"].reshape(-1), S5_P))
    to_t = lambda b: b.reshape(S5_G, S5_P, S5_H).transpose(2, 0, 1).reshape(S5_H, S5_N)
    bre_t, bim_t = to_t(sp["s5_b_re"]), to_t(sp["s5_b_im"])
    to_w = lambda c: jnp.tile(c.reshape(S5_W, S5_P), (1, 2))
    cre_w, cim_w = to_w(sp["s5_c_re"]), to_w(sp["s5_c_im"])

    jobs = (lambda *j: list(j)) if distributed else (lambda *j: [])
    c_arr = jnp.reshape(lax.axis_index("c"), (1,)).astype(jnp.int32) if distributed else None
    (xhat0, rstd0), bg = _ln_in(x2d, meta, jobs(*([_job_gather(w_int), _job_gather(w_glu)] if distributed else [])),
                                gather_meta=distributed)
    if distributed:
        w_int, w_glu = bg[1].reshape(PROJ_W, D_MODEL), bg[2].reshape(S5_W, S5_W)
    s5_small = (lre, lim, ldt, bre_t, bim_t, cre_w, cim_w, row(sp["s5_d"]), w_glu, row(sp["s5_b_glu"]))
    (u, q, k, v, gate), bg = _in_proj(xhat0, li_g, li_b, w_int, cos2, sin2,
                                      jobs(_job_gather(w_out) if distributed else None))
    if distributed:
        w_out = bg[0].reshape(D_MODEL, D_MODEL)
    (ys5, xr, xi), bg = _s5_fwd(u, *s5_small, jobs=jobs(_job_gather(w_up) if distributed else None))
    if distributed:
        w_up = bg[0]
    (o, states), _ = _ret_fwd(q, k, v)
    (ycat, xhat1, rstd1, h1b, pre), bg = _post_up(o, gate, ys5, xhat0, gn_g, gn_b, li_g, li_b, l1_g, l1_b, w_out, w_up,
                                                  jobs(_job_gather(w_down) if distributed else None))
    if distributed:
        w_down = bg[0].reshape(D_FF, D_MODEL)
    dr2, dffb, loss8, dl2g, dl2b = _post_down(pre, xhat1, tgt, l1_g, l1_b, l2_g, l2_b, w_down)
    g_up, g_down, dh1m = _mlp_bwd(h1b, dffb, pre, w_up, w_down)
    (do, dgate, dys5, dh0r, g_out, dl1g, dl1b, dgng, dgnb), bg = _post_bwd(
        dh1m, dr2, xhat1, rstd1, ycat, o, gate, gn_g, gn_b, l1_g, w_out,
        jobs(*([_job_pair(g_up), _job_pair(g_down)] if distributed else [])))
    g_out = g_out.reshape(N_DEV, D_MODEL // N_DEV, D_MODEL)
    if distributed:
        p_up, p_down = _pair_sum([g_up, g_down], bg, c_arr, "pair_sum_mlp")
    (du, dlre, dlim, dldt, dbre_t, dbim_t, dcre, dcim, dd, dwglu, dbglu), bg = _s5_bwd(
        dys5, u, xr, xi, *s5_small,
        jobs=jobs(*([_job_chips(p_up), _job_chips(p_down), _job_pair(g_out)] if distributed else [])))
    if distributed:
        r_up, r_down = bg[0], bg[1]
        (p_out,) = _pair_sum([g_out], bg[2:], c_arr, "pair_sum_out")
    from_t = lambda t: t.reshape(S5_H, S5_G, S5_P).transpose(1, 0, 2)
    small = {
        "s5_lambda_re": dlre, "s5_lambda_im": dlim, "s5_log_dt": dldt[:, :S5_G],
        "s5_b_re": from_t(dbre_t), "s5_b_im": from_t(dbim_t), "s5_c_re": dcre, "s5_c_im": dcim, "s5_d": dd,
        "s5_b_glu": dbglu, "ret_gn_g": dgng, "ret_gn_b": dgnb, "ln1_g": dl1g, "ln1_b": dl1b, "ln2_g": dl2g, "ln2_b": dl2b,
        "s5_w_glu": dwglu, "loss": loss8[0:1, 0:1]}
    early_pack = _pack([small[n] for n in EARLY])
    (dq, dk, dv), bg = _ret_bwd(q, k, v, do, states, cos2, sin2,
                                jobs(*([_job_chips(p_out), _job_gather(early_pack)] if distributed else [])))
    (grad_x, dmeta, g_int, dlig, dlib), _ = _in_bwd(du, dq, dk, dv, dgate, dh0r, xhat0, rstd0, li_g, li_b, w_int)
    small.update(ln_in_g=dlig, ln_in_b=dlib, meta_tokens=dmeta)
    g_int = g_int.reshape(N_DEV, PROJ_W // N_DEV, D_MODEL)
    if distributed:
        (r1_in,) = _exchange([_job_pair(g_int)], "exchange_pair_in")
        (p_in,) = _pair_sum([g_int], [r1_in], c_arr, "pair_sum_in")
        big = dict(chip_sums=[p_in, p_out, p_up, p_down], received=[None, bg[0], r_up, r_down], early=bg[1])
    else:
        big = dict(partials=[g_int, g_out, g_up, g_down])
    return grad_x, big, small


def kernel(x, meta_tokens, ln_in_g, ln_in_b, w_in, s5_lambda_re, s5_lambda_im, s5_log_dt, s5_b_re, s5_b_im, s5_c_re, s5_c_im, s5_d, s5_w_glu, s5_b_glu, ret_gn_g, ret_gn_b, w_out, ln1_g, ln1_b, w_up, w_down, ln2_g, ln2_b, loss_target, m_meta_tokens, m_ln_in_g, m_ln_in_b, m_w_in, m_s5_lambda_re, m_s5_lambda_im, m_s5_log_dt, m_s5_b_re, m_s5_b_im, m_s5_c_re, m_s5_c_im, m_s5_d, m_s5_w_glu, m_s5_b_glu, m_ret_gn_g, m_ret_gn_b, m_w_out, m_ln1_g, m_ln1_b, m_w_up, m_w_down, m_ln2_g, m_ln2_b, v_meta_tokens, v_ln_in_g, v_ln_in_b, v_w_in, v_s5_lambda_re, v_s5_lambda_im, v_s5_log_dt, v_s5_b_re, v_s5_b_im, v_s5_c_re, v_s5_c_im, v_s5_d, v_s5_w_glu, v_s5_b_glu, v_ret_gn_g, v_ret_gn_b, v_w_out, v_ln1_g, v_ln1_b, v_w_up, v_w_down, v_ln2_g, v_ln2_b):
    args = dict(locals())
    names = ["meta_tokens", "ln_in_g", "ln_in_b", "w_in", "s5_lambda_re", "s5_lambda_im", "s5_log_dt", "s5_b_re", "s5_b_im",
             "s5_c_re", "s5_c_im", "s5_d", "s5_w_glu", "s5_b_glu", "ret_gn_g", "ret_gn_b", "w_out", "ln1_g", "ln1_b",
             "w_up", "w_down", "ln2_g", "ln2_b"]
    ax, ay, ac = _place()
    me = 4 * ax + 2 * ay + ac

    sp = {n: args[n] for n in SMALL}
    grad_x, big, small = _local_step(x[0], loss_target[0], meta_tokens, w_in[0].T.astype(MM), w_out[0].astype(MM),
                                   w_up[0].astype(MM), w_down[0].astype(MM), s5_w_glu[0].astype(MM), sp, True)

    j_arr = jnp.reshape(2 * ax + ay, (1,)).astype(jnp.int32)
    two_d = lambda a: a.reshape(a.shape[-2:])
    item = lambda n, p, r: (p, r, *(two_d(_view(n, a)) for a in (args[n], args["m_" + n], args["v_" + n])))
    late_pack = _pack([small[n] for n in LATE])
    mlp = ("w_out", "w_up", "w_down")
    res, (r_in, late_all) = _adamw_shards(
        [item(n, p, r) for n, p, r in zip(mlp, big["chip_sums"][1:], big["received"][1:])], "adamw_mlp", 8, j_arr,
        [_job_chips(big["chip_sums"][0]), _job_gather(late_pack)])
    res_in, _ = _adamw_shards([item("w_in", big["chip_sums"][0], r_in)], "adamw_in", 2, j_arr)
    upd = {"w_in": res_in}
    for idx, n in enumerate(mlp):
        upd[n] = res[4 * idx:4 * idx + 4]
    shard_grads = {n: upd[n][0] for n in upd}

    early_shapes = [_view(n, args[n]).shape for n in EARLY[:-2]] + [(S5_W, S5_W), (1,)]
    late_shapes = [args["ln_in_g"].shape, args["ln_in_b"].shape, (N_META, D_MODEL)]
    g_small = dict(zip(EARLY, _unpack(_sum_devices(big["early"], "sum_small_early"), early_shapes)))
    g_small.update(zip(LATE, _unpack(_sum_devices(late_all, "sum_small_late"), late_shapes)))
    loss = g_small["loss"].reshape(())

    shard_grads["meta_tokens"] = lax.dynamic_slice(g_small["meta_tokens"], (0, me * (D_MODEL // N_DEV)),
                                                   (N_META, D_MODEL // N_DEV))
    shard_grads["s5_w_glu"] = lax.dynamic_slice(g_small["s5_w_glu"], (me * (S5_W // N_DEV), 0),
                                                (S5_W // N_DEV, S5_W))[None]
    natives = SMALL + ["meta_tokens", "s5_w_glu"]
    res2 = _adamw_native([(shard_grads[n] if n in shard_grads else g_small[n], *(_view(n, args[p + n]) for p in ("", "m_", "v_")))
                          for n in natives], "adamw_small")
    for idx, n in enumerate(natives):
        upd[n] = [shard_grads[n] if n in shard_grads else g_small[n]] + list(res2[3 * idx:3 * idx + 3])

    grads, deltas, new_m, new_v = ([_view(n, upd[n][t]).reshape(args[n].shape) for n in names] for t in range(4))
    return (loss, grad_x[None], *grads, *deltas, *new_m, *new_v)
```

```python
import math

import jax
import jax.numpy as jnp
from jax import lax
from jax.experimental import pallas as pl
from jax.experimental.pallas import tpu as pltpu

F32 = jnp.float32
MM = jnp.bfloat16

D_MODEL = 1024
N_META = 16
CHUNK = 128
PAD = CHUNK - N_META
S5_W, S5_G, S5_H, S5_P = 256, 16, 16, 64
S5_N = S5_G * S5_P
RET_W, RET_H, HEAD = 768, 6, 128
D_FF = 4096
PROJ_W = S5_W + 4 * RET_W
N_DEV = 8
FF_BLK = D_FF // N_DEV
ROW_BLK = 384
MLP_ROWS = 1408
PROJ_ROWS = 704
ALPHA = 2.0 ** 0.25
LN_EPS = 1e-5
GN_EPS = 1e-5
ROPE_BASE = 10000.0
GELU_C = math.sqrt(2.0 / math.pi)
GELU_A = 0.044715
ADAM_LR, ADAM_B1, ADAM_B2, ADAM_EPS, ADAM_WD, ADAM_STEP = 0.001, 0.9, 0.999, 1e-08, 0.01, 10
VMEM_LIMIT = 60 * 1024 * 1024

_VMEM = pl.BlockSpec(memory_space=pltpu.VMEM)
_ANY = pl.BlockSpec(memory_space=pl.ANY)
_MESH = pl.DeviceIdType.MESH


def _params(sem=None):
    return pltpu.CompilerParams(dimension_semantics=sem, vmem_limit_bytes=VMEM_LIMIT)


def _dot(a, b):
    return jnp.dot(a.astype(MM), b.astype(MM), preferred_element_type=F32)


def _dot_nt(a, b):
    return lax.dot_general(a.astype(MM), b.astype(MM), (((1,), (1,)), ((), ())), preferred_element_type=F32)


def _dot_tn(a, b):
    return lax.dot_general(a.astype(MM), b.astype(MM), (((0,), (0,)), ((), ())), preferred_element_type=F32)


def _split3(a):
    hi = a.astype(jnp.bfloat16)
    r1 = a - hi.astype(F32)
    mid = r1.astype(jnp.bfloat16)
    lo = (r1 - mid.astype(F32)).astype(jnp.bfloat16)
    return hi, mid, lo


def _dot_sel_rhs(a, sel):
    s = sel.astype(jnp.bfloat16)
    return sum(jnp.dot(p, s, preferred_element_type=F32) for p in _split3(a))


def _dot_sel_lhs(sel, b):
    s = sel.astype(jnp.bfloat16)
    return sum(jnp.dot(s, p, preferred_element_type=F32) for p in _split3(b))


def _ln_fwd(r, eps):
    mu = jnp.mean(r, axis=-1, keepdims=True)
    xc = r - mu
    var = jnp.mean(xc * xc, axis=-1, keepdims=True)
    rstd = lax.rsqrt(var + eps)
    return xc * rstd, rstd


def _ln_bwd(dxhat, xhat, rstd):
    m1 = jnp.mean(dxhat, axis=-1, keepdims=True)
    m2 = jnp.mean(dxhat * xhat, axis=-1, keepdims=True)
    return rstd * (dxhat - m1 - xhat * m2)


def _colsum(a):
    return jnp.sum(a, axis=0, keepdims=True)


def _shift3(n_in, block=lambda i: i):
    return [pl.BlockSpec((CHUNK, D_MODEL), (lambda i, j=j: (jnp.clip(3 * block(i) - 1 + j, 0, n_in - 1), 0)))
            for j in range(3)]


def _ln_in(x2d, meta, jobs=(), gather_meta=False):
    seq = x2d.shape[0]
    tp = seq + CHUNK
    R = ROW_BLK
    nb = tp // R
    shard_w = D_MODEL // N_DEV

    def body(xa, xb, xc, meta_ref, xhat_ref, rstd_ref, raw_ref, *gathered):
        raw_ref[0:CHUNK, :] = xa[...]
        raw_ref[CHUNK:2 * CHUNK, :] = xb[...]
        raw_ref[2 * CHUNK:3 * CHUNK, :] = xc[...]

        @pl.when(pl.program_id(0) == nb - 1)
        def _():
            raw_ref[0:PAD, :] = jnp.zeros((PAD, D_MODEL), F32)
            if gather_meta:
                for d in range(N_DEV):
                    pltpu.sync_copy(gathered[0].at[d], raw_ref.at[PAD:CHUNK, d * shard_w:(d + 1) * shard_w])
            else:
                raw_ref[PAD:CHUNK, :] = meta_ref[...]

        xhat_ref[...], rstd_ref[...] = _ln_fwd(raw_ref[...], LN_EPS)

    row = lambda w: pl.BlockSpec((R, w), lambda i: (nb - 1 - i, 0))
    jobs = ([_job_gather(meta)] if gather_meta else []) + list(jobs)
    return _call(
        body, "ln_in", (nb,),
        _shift3(seq // CHUNK, lambda i: nb - 1 - i) + [pl.BlockSpec(meta.shape, lambda i: (0, 0))],
        [row(D_MODEL), row(1)], [jax.ShapeDtypeStruct((tp, D_MODEL), F32), jax.ShapeDtypeStruct((tp, 1), F32)],
        [pltpu.VMEM((R, D_MODEL), F32)], (x2d, x2d, x2d, meta), jobs, early=1 if gather_meta else 0)


def _in_proj(xhat0, ln_g, ln_b, w_int, cos2, sin2, jobs=()):
    tp = xhat0.shape[0]
    R = PROJ_ROWS if tp % PROJ_ROWS == 0 else ROW_BLK

    def body(xh_ref, g_ref, b_ref, w_ref, cos_ref, sin_ref, u_ref, q_ref, k_ref, v_ref, gate_ref):
        hb = (xh_ref[...] * g_ref[...] + b_ref[...]).astype(MM)
        valid = (pl.program_id(0) * R + lax.broadcasted_iota(jnp.int32, (R, 1), 0)) >= PAD

        def seg(lo, hi):
            return jnp.where(valid, _dot_nt(hb, w_ref[lo:hi, :]), 0.0)

        u_ref[...] = seg(0, S5_W)
        cos = cos_ref[...]
        sin = sin_ref[...]
        q = seg(S5_W, S5_W + RET_W)
        k = seg(S5_W + RET_W, S5_W + 2 * RET_W)
        for h in range(RET_H):
            sl = slice(h * HEAD, (h + 1) * HEAD)
            qh = q[:, sl]
            kh = k[:, sl]
            q_ref[:, sl] = (qh * cos + pltpu.roll(qh, HEAD // 2, 1) * sin).astype(q_ref.dtype)
            k_ref[:, sl] = ((kh * cos + pltpu.roll(kh, HEAD // 2, 1) * sin) * (HEAD ** -0.5)).astype(k_ref.dtype)
        v_ref[...] = seg(S5_W + 2 * RET_W, S5_W + 3 * RET_W).astype(v_ref.dtype)
        gate_ref[...] = seg(S5_W + 3 * RET_W, PROJ_W)

    def rows(w, dt):
        return pl.BlockSpec((R, w), lambda i: (i, 0)), jax.ShapeDtypeStruct((tp, w), dt)

    outs = [rows(S5_W, F32), rows(RET_W, MM), rows(RET_W, MM), rows(RET_W, MM), rows(RET_W, F32)]
    full = lambda s: pl.BlockSpec(s, lambda i: (0,) * len(s))
    return _call(
        body, "in_proj", (tp // R,),
        [pl.BlockSpec((R, D_MODEL), lambda i: (i, 0)), full((1, D_MODEL)), full((1, D_MODEL)), _VMEM,
         pl.BlockSpec((R, HEAD), lambda i: (i, 0)), pl.BlockSpec((R, HEAD), lambda i: (i, 0))],
        [o[0] for o in outs], [o[1] for o in outs], [], (xhat0, ln_g, ln_b, w_int, cos2, sin2), jobs)


def _s5_disc(lre, lim, ldt, bre_t, bim_t):
    dt = jnp.exp(ldt)
    mag = jnp.exp(lre * dt)
    ang = lim * dt
    lbr = mag * jnp.cos(ang)
    lbi = mag * jnp.sin(ang)
    den = lre * lre + lim * lim
    nr = lbr - 1.0
    qr = (nr * lre + lbi * lim) / den
    qi = (lbi * lre - nr * lim) / den
    return lbr, lbi, qr * bre_t - qi * bim_t, qr * bim_t + qi * bre_t


def _s5_tables(lbr, lbi, reverse):
    if reverse:
        lbi = -lbi
    pw = [(lbr, lbi)]
    for _ in range(7):
        r, i = pw[-1]
        pw.append((r * lbr - i * lbi, r * lbi + i * lbr))
    row = lax.broadcasted_iota(jnp.int32, (8, S5_N), 0)
    tabs = []
    for k in range(3):
        sh = 2 ** k
        mask = (row < 8 - sh) if reverse else (row >= sh)
        ar, ai = pw[sh - 1]
        tabs.append((jnp.where(mask, ar, 0.0), jnp.where(mask, ai, 0.0)))
    pr = jnp.zeros((8, S5_N), F32)
    pi = jnp.zeros((8, S5_N), F32)
    for i in range(8):
        ar, ai = pw[7 - i] if reverse else pw[i]
        pr = jnp.where(row == i, ar, pr)
        pi = jnp.where(row == i, ai, pi)
    tabs.append((pr, pi))
    return tabs


def _store_tables(tab_ref, tabs):
    for k, (r, i) in enumerate(tabs):
        tab_ref[2 * k] = r
        tab_ref[2 * k + 1] = i


def _bd_mask():
    r = lax.broadcasted_iota(jnp.int32, (S5_W, S5_N), 0)
    c = lax.broadcasted_iota(jnp.int32, (S5_W, S5_N), 1)
    return jnp.right_shift(r, 4) == jnp.right_shift(c, 6)


def _s5_block_diag(bbr_t, bbi_t, cre_w, cim_w):
    mask = _bd_mask()
    bd = lambda t: jnp.where(mask, t, 0.0)
    return (bd(jnp.tile(bbr_t, (S5_G, 1))), bd(jnp.tile(bbi_t, (S5_G, 1))),
            bd(jnp.tile(cre_w, (1, S5_N // HEAD))), bd(jnp.tile(cim_w, (1, S5_N // HEAD))))


def _scan8(xr, xi, tab_ref, lanes, reverse):
    for k in range(3):
        sh = (8 - 2 ** k) if reverse else 2 ** k
        sr = pltpu.roll(xr, sh, 0)
        si = pltpu.roll(xi, sh, 0)
        mr = tab_ref[2 * k, :, lanes]
        mi = tab_ref[2 * k + 1, :, lanes]
        xr, xi = xr + (mr * sr - mi * si), xi + (mr * si + mi * sr)
    return xr, xi


S5_LANES = 512


def _gelu(y):
    t = jnp.tanh(GELU_C * (y + GELU_A * y * y * y))
    return 0.5 * y * (1.0 + t), t


def _s5_fwd(u, lre, lim, ldt, bre_t, bim_t, cre_w, cim_w, d_row, w_glu, b_glu, jobs=()):
    tp = u.shape[0]
    R = ROW_BLK

    def body(u_ref, lre_ref, lim_ref, ldt_ref, bre_ref, bim_ref, cre_ref, cim_ref, d_ref, wg_ref, bg_ref,
             y_ref, xr_ref, xi_ref, bbd_r, bbd_i, cbd_r, cbd_i, tab_ref, car_r, car_i):
        @pl.when(pl.program_id(0) == 0)
        def _():
            lbr, lbi, bbr, bbi = _s5_disc(lre_ref[...], lim_ref[...], ldt_ref[...], bre_ref[...], bim_ref[...])
            br, bi, cr, ci = _s5_block_diag(bbr, bbi, cre_ref[...], cim_ref[...])
            bbd_r[...] = br.astype(MM)
            bbd_i[...] = bi.astype(MM)
            cbd_r[...] = cr.astype(MM)
            cbd_i[...] = ci.astype(MM)
            _store_tables(tab_ref, _s5_tables(lbr, lbi, False))
            car_r[...] = jnp.zeros_like(car_r)
            car_i[...] = jnp.zeros_like(car_i)

        u = u_ref[...]
        ub = u.astype(MM)
        xr_ref[...] = jnp.dot(ub, bbd_r[...], preferred_element_type=F32)
        xi_ref[...] = jnp.dot(ub, bbd_i[...], preferred_element_type=F32)
        for j in range(S5_N // S5_LANES):
            lanes = pl.ds(j * S5_LANES, S5_LANES)
            pr = tab_ref[6, :, lanes]
            pi = tab_ref[7, :, lanes]

            def step(g, carry):
                cr, ci = carry
                rows = pl.ds(pl.multiple_of(g * 8, 8), 8)
                xr, xi = _scan8(xr_ref[rows, lanes], xi_ref[rows, lanes], tab_ref, lanes, False)
                br = jnp.broadcast_to(cr[7:8, :], cr.shape)
                bi = jnp.broadcast_to(ci[7:8, :], ci.shape)
                xr = xr + (pr * br - pi * bi)
                xi = xi + (pr * bi + pi * br)
                xr_ref[rows, lanes] = xr
                xi_ref[rows, lanes] = xi
                return xr, xi

            cr, ci = lax.fori_loop(0, R // 8, step, (car_r[:, lanes], car_i[:, lanes]), unroll=2)
            car_r[:, lanes] = cr
            car_i[:, lanes] = ci
        y = _dot_nt(xr_ref[...], cbd_r[...]) - _dot_nt(xi_ref[...], cbd_i[...]) + d_ref[...] * u
        yg, _ = _gelu(y)
        z = _dot(yg, wg_ref[...]) + bg_ref[...]
        y_ref[...] = yg * jax.nn.sigmoid(z)

    full = lambda a: pl.BlockSpec(a.shape, lambda i: (0,) * a.ndim)
    small = [lre, lim, ldt, bre_t, bim_t, cre_w, cim_w, d_row, w_glu, b_glu]
    return _call(
        body, "s5_fwd", (tp // R,),
        [pl.BlockSpec((R, S5_W), lambda i: (i, 0))] + [full(a) for a in small],
        [pl.BlockSpec((R, S5_W), lambda i: (i, 0)), pl.BlockSpec((R, S5_N), lambda i: (i, 0)),
         pl.BlockSpec((R, S5_N), lambda i: (i, 0))],
        [jax.ShapeDtypeStruct((tp, S5_W), F32), jax.ShapeDtypeStruct((tp, S5_N), F32),
         jax.ShapeDtypeStruct((tp, S5_N), F32)],
        [pltpu.VMEM((S5_W, S5_N), MM)] * 4 + [pltpu.VMEM((8, 8, S5_N), F32), pltpu.VMEM((8, S5_N), F32),
                                              pltpu.VMEM((8, S5_N), F32)],
        (u, *small), jobs)


def _s5_bwd(dy_out, u, xr, xi, lre, lim, ldt, bre_t, bim_t, cre_w, cim_w, d_row, w_glu, b_glu, jobs=()):
    tp = u.shape[0]
    R = ROW_BLK
    nb = tp // R

    def body(dyo_ref, u_ref, xr_ref, xi_ref, xpr_ref, xpi_ref,
             lre_ref, lim_ref, ldt_ref, bre_ref, bim_ref, cre_ref, cim_ref, d_ref, wg_ref, bg_ref,
             du_ref, dlre_ref, dlim_ref, dldt_ref, dbre_ref, dbim_ref, dcre_ref, dcim_ref, dd_ref, dwg_ref, dbg_ref,
             bbd_r, bbd_i, cbd_r, cbd_i, tab_ref, car_r, car_i, gr_ref, gi_ref, xer_ref, xei_ref,
             abr, abi, acr, aci, adr, adi):
        i = pl.program_id(0)

        @pl.when(i == 0)
        def _():
            lbr, lbi, bbr, bbi = _s5_disc(lre_ref[...], lim_ref[...], ldt_ref[...], bre_ref[...], bim_ref[...])
            br, bi, cr, ci = _s5_block_diag(bbr, bbi, cre_ref[...], cim_ref[...])
            bbd_r[...] = br.astype(MM)
            bbd_i[...] = bi.astype(MM)
            cbd_r[...] = cr.astype(MM)
            cbd_i[...] = ci.astype(MM)
            _store_tables(tab_ref, _s5_tables(lbr, lbi, True))
            for ref in (car_r, car_i, abr, abi, acr, aci, adr, adi, dd_ref, dwg_ref, dbg_ref):
                ref[...] = jnp.zeros_like(ref)

        u = u_ref[...]
        xrv = xr_ref[...]
        xiv = xi_ref[...]
        y = _dot_nt(xrv, cbd_r[...]) - _dot_nt(xiv, cbd_i[...]) + d_ref[...] * u
        yg, t = _gelu(y)
        z = _dot(yg, wg_ref[...]) + bg_ref[...]
        s = jax.nn.sigmoid(z)
        dout = dyo_ref[...]
        dz = dout * yg * s * (1.0 - s)
        dyg = dout * s + _dot_nt(dz, wg_ref[...])
        dwg_ref[...] += _dot_tn(yg, dz)
        dbg_ref[...] += _colsum(dz)
        dy = dyg * (0.5 * (1.0 + t) + 0.5 * y * (1.0 - t * t) * GELU_C * (1.0 + 3.0 * GELU_A * y * y))
        dd_ref[...] += _colsum(dy * u)
        acr[...] += _dot_tn(dy, xrv)
        aci[...] -= _dot_tn(dy, xiv)
        gr_ref[...] = _dot(dy, cbd_r[...])
        gi_ref[...] = -_dot(dy, cbd_i[...])
        has_prev = (i < nb - 1).astype(F32)
        xer_ref[0:8, :] = xpr_ref[...] * has_prev
        xei_ref[0:8, :] = xpi_ref[...] * has_prev
        xer_ref[8:R + 8, :] = xrv
        xei_ref[8:R + 8, :] = xiv
        row = lax.broadcasted_iota(jnp.int32, (8, S5_LANES), 0)
        for j in range(S5_N // S5_LANES):
            lanes = pl.ds(j * S5_LANES, S5_LANES)
            pr = tab_ref[6, :, lanes]
            pi = tab_ref[7, :, lanes]

            def step(n, carry):
                cr, ci, sar, sai = carry
                g = R // 8 - 1 - n
                r0 = pl.multiple_of(g * 8, 8)
                rows = pl.ds(r0, 8)
                gr, gi = _scan8(gr_ref[rows, lanes], gi_ref[rows, lanes], tab_ref, lanes, True)
                br = jnp.broadcast_to(cr[0:1, :], cr.shape)
                bi = jnp.broadcast_to(ci[0:1, :], ci.shape)
                gr = gr + (pr * br - pi * bi)
                gi = gi + (pr * bi + pi * br)
                gr_ref[rows, lanes] = gr
                gi_ref[rows, lanes] = gi
                last = row == 7
                xpr = pltpu.roll(jnp.where(last, xer_ref[rows, lanes], xer_ref[pl.ds(r0 + 8, 8), lanes]), 1, 0)
                xpi = pltpu.roll(jnp.where(last, xei_ref[rows, lanes], xei_ref[pl.ds(r0 + 8, 8), lanes]), 1, 0)
                return gr, gi, sar + (gr * xpr + gi * xpi), sai + (gi * xpr - gr * xpi)

            cr, ci, sar, sai = lax.fori_loop(
                0, R // 8, step, (car_r[:, lanes], car_i[:, lanes], adr[:, lanes], adi[:, lanes]), unroll=2)
            car_r[:, lanes] = cr
            car_i[:, lanes] = ci
            adr[:, lanes] = sar
            adi[:, lanes] = sai
        grv = gr_ref[...]
        giv = gi_ref[...]
        du_ref[...] = (dy * d_ref[...] + _dot_nt(grv, bbd_r[...]) + _dot_nt(giv, bbd_i[...])).astype(du_ref.dtype)
        abr[...] += _dot_tn(u, grv)
        abi[...] += _dot_tn(u, giv)

        @pl.when(i == nb - 1)
        def _():
            mask = _bd_mask()
            r16 = lax.broadcasted_iota(jnp.int32, (S5_H, S5_W), 1)
            h16 = lax.broadcasted_iota(jnp.int32, (S5_H, S5_W), 0)
            fold_b = jnp.bitwise_and(r16, S5_H - 1) == h16
            c64 = lax.broadcasted_iota(jnp.int32, (S5_N, S5_P), 0)
            p64 = lax.broadcasted_iota(jnp.int32, (S5_N, S5_P), 1)
            fold_c = jnp.bitwise_and(c64, S5_P - 1) == p64
            dbbr = _dot_sel_lhs(fold_b, jnp.where(mask, abr[...], 0.0))
            dbbi = _dot_sel_lhs(fold_b, jnp.where(mask, abi[...], 0.0))
            dcre_ref[...] = _dot_sel_rhs(jnp.where(mask, acr[...], 0.0), fold_c)
            dcim_ref[...] = _dot_sel_rhs(jnp.where(mask, aci[...], 0.0), fold_c)
            dlbr = _colsum(adr[...])
            dlbi = _colsum(adi[...])
            _, vjp = jax.vjp(_s5_disc, lre_ref[...], lim_ref[...], ldt_ref[...], bre_ref[...], bim_ref[...])
            dlre, dlim, dldt, dbre, dbim = vjp((dlbr, dlbi, dbbr, dbbi))
            dlre_ref[...] = dlre
            dlim_ref[...] = dlim
            dbre_ref[...] = dbre
            dbim_ref[...] = dbim
            gsel = jnp.right_shift(lax.broadcasted_iota(jnp.int32, (S5_N, HEAD), 0), 6) == \
                lax.broadcasted_iota(jnp.int32, (S5_N, HEAD), 1)
            dldt_ref[...] = _dot_sel_rhs(dldt, gsel)

    full = lambda a: pl.BlockSpec(a.shape, lambda i: (0,) * a.ndim)
    rev = lambda w: pl.BlockSpec((R, w), lambda i: (nb - 1 - i, 0))
    prev8 = pl.BlockSpec((8, S5_N), lambda i: (jnp.maximum((nb - 1 - i) * (R // 8) - 1, 0), 0))
    small = [lre, lim, ldt, bre_t, bim_t, cre_w, cim_w, d_row, w_glu, b_glu]
    outs = [((tp, S5_W), rev(S5_W))] + [
        (s, pl.BlockSpec(s, lambda i: (0, 0))) for s in
        [(1, S5_N), (1, S5_N), (1, HEAD), (S5_H, S5_N), (S5_H, S5_N), (S5_W, S5_P), (S5_W, S5_P),
         (1, S5_W), (S5_W, S5_W), (1, S5_W)]]
    return _call(
        body, "s5_bwd", (nb,),
        [rev(S5_W), rev(S5_W), rev(S5_N), rev(S5_N), prev8, prev8] + [full(a) for a in small],
        [o[1] for o in outs], [jax.ShapeDtypeStruct(o[0], MM if n == 0 else F32) for n, o in enumerate(outs)],
        [pltpu.VMEM((S5_W, S5_N), MM)] * 4 + [
            pltpu.VMEM((8, 8, S5_N), F32), pltpu.VMEM((8, S5_N), F32), pltpu.VMEM((8, S5_N), F32),
            pltpu.VMEM((R, S5_N), F32), pltpu.VMEM((R, S5_N), F32),
            pltpu.VMEM((R + 8, S5_N), F32), pltpu.VMEM((R + 8, S5_N), F32)] + [pltpu.VMEM((S5_W, S5_N), F32)] * 4 + [
            pltpu.VMEM((8, S5_N), F32), pltpu.VMEM((8, S5_N), F32)],
        (dy_out, u, xr, xi, xr, xi, *small), jobs)


RET_CHUNK = ROW_BLK
LOG_GAMMA = [math.log1p(-2.0 ** (-5 - h)) for h in range(RET_H)]
GAMMA_CHUNK = [math.exp(RET_CHUNK * lg) for lg in LOG_GAMMA]
_DECAY_SCRATCH = [pltpu.VMEM((RET_H, RET_CHUNK, RET_CHUNK), F32), pltpu.VMEM((RET_H, RET_CHUNK, HEAD), F32),
                  pltpu.VMEM((RET_H, RET_CHUNK, HEAD), F32)]


def _fill_decay(dm_ref, ze_ref, xi_ref):
    C = RET_CHUNK
    diff = (lax.broadcasted_iota(jnp.int32, (C, C), 0) - lax.broadcasted_iota(jnp.int32, (C, C), 1)).astype(F32)
    r = lax.broadcasted_iota(jnp.int32, (C, HEAD), 0).astype(F32)
    for h, lg in enumerate(LOG_GAMMA):
        dm_ref[h] = jnp.where(diff >= 0.0, jnp.exp(jnp.maximum(diff, 0.0) * lg), 0.0)
        ze_ref[h] = jnp.exp((C - 1.0 - r) * lg)
        xi_ref[h] = jnp.exp((r + 1.0) * lg)


def _ret_fwd(q, k, v, jobs=()):
    tp = q.shape[0]
    C = RET_CHUNK
    nc = tp // C

    def body(q_ref, k_ref, v_ref, o_ref, st_ref, s_ref, dm_ref, ze_ref, xi_ref):
        @pl.when(pl.program_id(0) == 0)
        def _():
            s_ref[...] = jnp.zeros_like(s_ref)
            _fill_decay(dm_ref, ze_ref, xi_ref)

        for h in range(RET_H):
            sl = slice(h * HEAD, (h + 1) * HEAD)
            qh, kh, vh = q_ref[:, sl], k_ref[:, sl], v_ref[:, sl]
            sh = s_ref[h]
            st_ref[0, sl, :] = sh
            scores = _dot_nt(qh, kh) * dm_ref[h]
            o_ref[:, sl] = _dot(scores, vh) + _dot(qh, sh) * xi_ref[h]
            s_ref[h] = GAMMA_CHUNK[h] * sh + _dot_tn(kh.astype(F32) * ze_ref[h], vh)

    blk = pl.BlockSpec((C, RET_W), lambda c: (c, 0))
    return _call(
        body, "ret_fwd", (nc,), [blk, blk, blk], [blk, pl.BlockSpec((1, RET_W, HEAD), lambda c: (c, 0, 0))],
        [jax.ShapeDtypeStruct((tp, RET_W), F32), jax.ShapeDtypeStruct((nc, RET_W, HEAD), F32)],
        [pltpu.VMEM((RET_H, HEAD, HEAD), F32)] + _DECAY_SCRATCH, (q, k, v), jobs)


def _ret_bwd(q, k, v, do, states, cos2, sin2, jobs=()):
    tp = q.shape[0]
    C = RET_CHUNK
    nc = tp // C

    def body(q_ref, k_ref, v_ref, do_ref, st_ref, cos_ref, sin_ref,
             dq_ref, dk_ref, dv_ref, ds_ref, dm_ref, ze_ref, xi_ref):
        @pl.when(pl.program_id(0) == 0)
        def _():
            ds_ref[...] = jnp.zeros_like(ds_ref)
            _fill_decay(dm_ref, ze_ref, xi_ref)

        cos = cos_ref[...]
        sin = sin_ref[...]
        for h in range(RET_H):
            sl = slice(h * HEAD, (h + 1) * HEAD)
            qh, kh, vh = q_ref[:, sl], k_ref[:, sl], v_ref[:, sl]
            dmh = dm_ref[h]
            sh = st_ref[0, sl, :]
            dsn = ds_ref[h]
            doh = do_ref[:, sl]
            dox = doh * xi_ref[h]
            a = _dot_nt(qh, kh) * dmh
            dqk = _dot_nt(doh, vh) * dmh
            kz = kh.astype(F32) * ze_ref[h]
            dv_ref[:, sl] = (_dot_tn(a, doh) + _dot(kz, dsn)).astype(dv_ref.dtype)
            dqr = _dot(dqk, kh) + _dot_nt(dox, sh)
            dkr = _dot_tn(dqk, qh) + ze_ref[h] * _dot_nt(vh, dsn)
            ds_ref[h] = GAMMA_CHUNK[h] * dsn + _dot_tn(qh, dox)
            dq_ref[:, sl] = (dqr * cos - pltpu.roll(dqr, HEAD // 2, 1) * sin).astype(dq_ref.dtype)
            dk_ref[:, sl] = ((dkr * cos - pltpu.roll(dkr, HEAD // 2, 1) * sin) * (HEAD ** -0.5)).astype(dk_ref.dtype)

    blk = pl.BlockSpec((C, RET_W), lambda c: (nc - 1 - c, 0))
    tab = pl.BlockSpec((C, HEAD), lambda c: (nc - 1 - c, 0))
    return _call(
        body, "ret_bwd", (nc,),
        [blk, blk, blk, blk, pl.BlockSpec((1, RET_W, HEAD), lambda c: (nc - 1 - c, 0, 0)), tab, tab],
        [blk, blk, blk], [jax.ShapeDtypeStruct((tp, RET_W), MM)] * 3,
        [pltpu.VMEM((RET_H, HEAD, HEAD), F32)] + _DECAY_SCRATCH, (q, k, v, do, states, cos2, sin2), jobs)


def _gn_gate(o, gate, gn_g, gn_b):
    xhat, rstd = _ln_fwd(o, GN_EPS)
    on = xhat * gn_g + gn_b
    s = jax.nn.sigmoid(gate)
    return gate * s * on, xhat, rstd, on, s


def _post_up(o, gate, ys5, xhat0, gn_g, gn_b, li_g, li_b, l1_g, l1_b, w_out, w_up, jobs=()):
    tp = o.shape[0]
    R = ROW_BLK

    def body(o_ref, g_ref, ys_ref, xh0_ref, gng, gnb, lig, lib, l1g, l1b, wo_ref, wu_ref,
             ycat_ref, xh1_ref, rstd1_ref, h1b_ref, pre_ref):
        ycat_ref[:, 0:S5_W] = ys_ref[...].astype(ycat_ref.dtype)
        for h in range(RET_H):
            sl = slice(h * HEAD, (h + 1) * HEAD)
            yret = _gn_gate(o_ref[:, sl], g_ref[:, sl], gng[:, sl], gnb[:, sl])[0]
            ycat_ref[:, S5_W + h * HEAD:S5_W + (h + 1) * HEAD] = yret.astype(ycat_ref.dtype)
        mixed = _dot(ycat_ref[...], wo_ref[...])
        h0 = xh0_ref[...] * lig[...] + lib[...]
        xh1, rstd1 = _ln_fwd(ALPHA * h0 + mixed, LN_EPS)
        xh1_ref[...] = xh1
        rstd1_ref[...] = rstd1
        h1b = (xh1 * l1g[...] + l1b[...]).astype(MM)
        h1b_ref[...] = h1b
        for d in range(N_DEV):
            pre_ref[:, d * FF_BLK:(d + 1) * FF_BLK] = jnp.maximum(_dot(h1b, wu_ref[d]), 0.0)

    row = lambda w: pl.BlockSpec((R, w), lambda i: (i, 0))
    full = lambda a: pl.BlockSpec(a.shape, lambda i: (0,) * a.ndim)
    vecs = [gn_g, gn_b, li_g, li_b, l1_g, l1_b]
    outs = [(row(D_MODEL), jax.ShapeDtypeStruct((tp, D_MODEL), MM)), (row(D_MODEL), jax.ShapeDtypeStruct((tp, D_MODEL), F32)),
            (row(1), jax.ShapeDtypeStruct((tp, 1), F32)), (row(D_MODEL), jax.ShapeDtypeStruct((tp, D_MODEL), MM)),
            (row(D_FF), jax.ShapeDtypeStruct((tp, D_FF), F32))]
    return _call(
        body, "post_up", (tp // R,),
        [row(RET_W), row(RET_W), row(S5_W), row(D_MODEL)] + [full(a) for a in vecs] + [_VMEM, _VMEM],
        [o[0] for o in outs], [o[1] for o in outs], [], (o, gate, ys5, xhat0, *vecs, w_out, w_up), jobs)


def _post_down(pre, xhat1, tgt, l1_g, l1_b, l2_g, l2_b, w_down):
    tp = pre.shape[0]
    seq = tgt.shape[0]
    R = ROW_BLK

    def body(pre_ref, xh1_ref, ta, tb, tc, l1g, l1b, l2g, l2b, wd_ref,
             dr2_ref, dffb_ref, loss_ref, dl2g_ref, dl2b_ref, tgt_ref):
        i = pl.program_id(0)

        @pl.when(i == 0)
        def _():
            for ref in (loss_ref, dl2g_ref, dl2b_ref):
                ref[...] = jnp.zeros_like(ref)

        tgt_ref[0:CHUNK, :] = ta[...]
        tgt_ref[CHUNK:2 * CHUNK, :] = tb[...]
        tgt_ref[2 * CHUNK:3 * CHUNK, :] = tc[...]
        ff = jnp.zeros((R, D_MODEL), F32)
        for d in range(N_DEV):
            pre = pre_ref[:, d * FF_BLK:(d + 1) * FF_BLK]
            ff = ff + _dot(pre * pre, wd_ref[d * FF_BLK:(d + 1) * FF_BLK, :])
        h1 = xh1_ref[...] * l1g[...] + l1b[...]
        xh2, rstd2 = _ln_fwd(ALPHA * h1 + ff, LN_EPS)
        h2 = xh2 * l2g[...] + l2b[...]
        valid = (i * R + lax.broadcasted_iota(jnp.int32, (R, 1), 0)) >= CHUNK
        err = jnp.where(valid, h2 - tgt_ref[...], 0.0)
        loss_ref[...] += 0.5 * jnp.sum(err * err) / D_MODEL
        dh2 = err * (1.0 / D_MODEL)
        dl2g_ref[...] += _colsum(dh2 * xh2)
        dl2b_ref[...] += _colsum(dh2)
        dr2 = _ln_bwd(dh2 * l2g[...], xh2, rstd2)
        dr2_ref[...] = dr2
        dffb_ref[...] = dr2.astype(MM)

    row = lambda w: pl.BlockSpec((R, w), lambda i: (i, 0))
    full = lambda a: pl.BlockSpec(a.shape, lambda i: (0,) * a.ndim)
    vecs = [l1_g, l1_b, l2_g, l2_b]
    acc = lambda s: (pl.BlockSpec(s, lambda i: (0, 0)), jax.ShapeDtypeStruct(s, F32))
    outs = [(row(D_MODEL), jax.ShapeDtypeStruct((tp, D_MODEL), F32)), (row(D_MODEL), jax.ShapeDtypeStruct((tp, D_MODEL), MM)),
            acc((8, HEAD)), acc((1, D_MODEL)), acc((1, D_MODEL))]
    return pl.pallas_call(
        body, name="post_down", grid=(tp // R,),
        in_specs=[row(D_FF), row(D_MODEL)] + _shift3(seq // CHUNK) + [full(a) for a in vecs] + [_VMEM],
        out_specs=[o[0] for o in outs], out_shape=[o[1] for o in outs],
        scratch_shapes=[pltpu.VMEM((R, D_MODEL), F32)],
        compiler_params=_params(("arbitrary",)),
    )(pre, xhat1, tgt, tgt, tgt, *vecs, w_down)


def _mlp_bwd(h1b, dffb, pre, w_up, w_down):
    tp = h1b.shape[0]
    R = MLP_ROWS if tp % MLP_ROWS == 0 else ROW_BLK
    nr = tp // R

    def body(h_ref, df_ref, pre_ref, wu_ref, wd_ref, gup_ref, gdn_ref, dh1_ref, aup, adn):
        d = pl.program_id(0)
        r = pl.program_id(1)

        @pl.when(r == 0)
        def _():
            aup[...] = jnp.zeros_like(aup)
            adn[...] = jnp.zeros_like(adn)

        h = h_ref[...]
        df = df_ref[...]
        wu = wu_ref[0]
        wd = wd_ref[0]
        pre = pre_ref[...]
        dpre = (_dot_nt(df, wd) * (2.0 * pre)).astype(MM)

        aup[...] += _dot_tn(h, dpre)
        adn[...] += _dot_tn(pre * pre, df)
        contrib = _dot_nt(dpre, wu)
        rows = pl.ds(pl.multiple_of(r * R, 64), R)

        @pl.when(d == 0)
        def _():
            dh1_ref[rows, :] = contrib

        @pl.when(d > 0)
        def _():
            dh1_ref[rows, :] += contrib

        @pl.when(r == nr - 1)
        def _():
            gup_ref[0] = aup[...].astype(gup_ref.dtype)
            gdn_ref[0] = adn[...].astype(gdn_ref.dtype)

    return pl.pallas_call(
        body, name="mlp_bwd", grid=(N_DEV, nr),
        in_specs=[pl.BlockSpec((R, D_MODEL), lambda d, r: (r, 0)), pl.BlockSpec((R, D_MODEL), lambda d, r: (r, 0)),
                  pl.BlockSpec((R, FF_BLK), lambda d, r: (r, d)),
                  pl.BlockSpec((1, D_MODEL, FF_BLK), lambda d, r: (d, 0, 0)),
                  pl.BlockSpec((1, FF_BLK, D_MODEL), lambda d, r: (d, 0, 0))],
        out_specs=[pl.BlockSpec((1, D_MODEL, FF_BLK), lambda d, r: (d, 0, 0)),
                   pl.BlockSpec((1, FF_BLK, D_MODEL), lambda d, r: (d, 0, 0)), _VMEM],
        out_shape=[jax.ShapeDtypeStruct((N_DEV, D_MODEL, FF_BLK), MM), jax.ShapeDtypeStruct((N_DEV, FF_BLK, D_MODEL), MM),
                   jax.ShapeDtypeStruct((tp, D_MODEL), F32)],
        scratch_shapes=[pltpu.VMEM((D_MODEL, FF_BLK), F32), pltpu.VMEM((FF_BLK, D_MODEL), F32)],
        compiler_params=_params(("arbitrary", "arbitrary")),
    )(h1b, dffb, pre, w_up, w_down.reshape(N_DEV, FF_BLK, D_MODEL))


def _post_bwd(dh1m, dr2, xhat1, rstd1, ycat, o, gate, gn_g, gn_b, l1_g, w_out, jobs=()):
    tp = o.shape[0]
    R = PROJ_ROWS if tp % PROJ_ROWS == 0 else ROW_BLK
    nb = tp // R

    def body(dm_ref, dr2_ref, xh1_ref, rs1_ref, yc_ref, o_ref, g_ref, gng, gnb, l1g, wo_ref,
             do_ref, dg_ref, dys_ref, dh0_ref, gwo_ref, dl1g_ref, dl1b_ref, dgng_ref, dgnb_ref, awo):
        i = pl.program_id(0)

        @pl.when(i == 0)
        def _():
            for ref in (awo, dl1g_ref, dl1b_ref, dgng_ref, dgnb_ref):
                ref[...] = jnp.zeros_like(ref)

        dh1 = dm_ref[...] + ALPHA * dr2_ref[...]
        xh1 = xh1_ref[...]
        dl1g_ref[...] += _colsum(dh1 * xh1)
        dl1b_ref[...] += _colsum(dh1)
        dr1 = _ln_bwd(dh1 * l1g[...], xh1, rs1_ref[...])
        dh0_ref[...] = ALPHA * dr1
        dmix = dr1.astype(MM)
        awo[...] += _dot_tn(yc_ref[...], dmix)
        dyc = _dot_nt(dmix, wo_ref[...])
        dys_ref[...] = dyc[:, 0:S5_W]
        for h in range(RET_H):
            sl = slice(h * HEAD, (h + 1) * HEAD)
            gt = g_ref[:, sl]
            _, xhat, rstd, on, s = _gn_gate(o_ref[:, sl], gt, gng[:, sl], gnb[:, sl])
            dyr = dyc[:, S5_W + h * HEAD:S5_W + (h + 1) * HEAD]
            dg_ref[:, sl] = (dyr * on * (s * (1.0 + gt * (1.0 - s)))).astype(dg_ref.dtype)
            don = dyr * gt * s
            dgng_ref[:, sl] += _colsum(don * xhat)
            dgnb_ref[:, sl] += _colsum(don)
            do_ref[:, sl] = _ln_bwd(don * gng[:, sl], xhat, rstd)

        @pl.when(i == nb - 1)
        def _():
            gwo_ref[...] = awo[...].astype(gwo_ref.dtype)

    row = lambda w: pl.BlockSpec((R, w), lambda i: (i, 0))
    full = lambda a: pl.BlockSpec(a.shape, lambda i: (0,) * a.ndim)
    acc = lambda s, dt=F32: (pl.BlockSpec(s, lambda i: (0, 0)), jax.ShapeDtypeStruct(s, dt))
    outs = [(row(RET_W), jax.ShapeDtypeStruct((tp, RET_W), F32)), (row(RET_W), jax.ShapeDtypeStruct((tp, RET_W), MM)),
            (row(S5_W), jax.ShapeDtypeStruct((tp, S5_W), F32)), (row(D_MODEL), jax.ShapeDtypeStruct((tp, D_MODEL), F32)),
            acc((D_MODEL, D_MODEL), MM), acc((1, D_MODEL)), acc((1, D_MODEL)), acc((1, RET_W)), acc((1, RET_W))]
    return _call(
        body, "post_bwd", (nb,),
        [row(D_MODEL), row(D_MODEL), row(D_MODEL), row(1), row(D_MODEL), row(RET_W), row(RET_W),
         full(gn_g), full(gn_b), full(l1_g), _VMEM],
        [o[0] for o in outs], [o[1] for o in outs],
        [pltpu.VMEM((D_MODEL, D_MODEL), F32)],
        (dh1m, dr2, xhat1, rstd1, ycat, o, gate, gn_g, gn_b, l1_g, w_out), jobs)


def _in_bwd(du, dq, dk, dv, dg, dh0r, xhat0, rstd0, li_g, li_b, w_int, jobs=()):
    tp = du.shape[0]
    R = PROJ_ROWS if tp % PROJ_ROWS == 0 else ROW_BLK
    nb = tp // R
    segs = [(0, S5_W)] + [(S5_W + n * RET_W, S5_W + (n + 1) * RET_W) for n in range(4)]

    def body(du_ref, dq_ref, dk_ref, dv_ref, dg_ref, dh0r_ref, xh_ref, rs_ref, lig, lib, w_ref,
             gx_ref, dmeta_ref, gw_ref, dlg_ref, dlb_ref, aw, stage, out_sems):
        i = pl.program_id(0)
        slot = i % 2

        def to_gx(step_slot, first):
            if first:
                return pltpu.make_async_copy(stage.at[0, CHUNK:R, :], gx_ref.at[0:R - CHUNK, :], out_sems.at[0])
            return pltpu.make_async_copy(stage.at[step_slot], gx_ref.at[pl.ds(i * R - CHUNK, R), :], out_sems.at[step_slot])

        @pl.when(i == 0)
        def _():
            for ref in (aw, dlg_ref, dlb_ref):
                ref[...] = jnp.zeros_like(ref)

        @pl.when(i >= 3)
        def _():
            to_gx(slot, False).wait()

        valid = (i * R + lax.broadcasted_iota(jnp.int32, (R, 1), 0)) >= PAD
        xh = xh_ref[...]
        hb = (xh * lig[...] + lib[...]).astype(MM)
        dh0 = dh0r_ref[...]
        for (lo, hi), ref in zip(segs, (du_ref, dq_ref, dk_ref, dv_ref, dg_ref)):
            dseg = jnp.where(valid, ref[...], 0.0).astype(MM)
            dh0 = dh0 + _dot(dseg, w_ref[lo:hi, :])
            aw[lo:hi, :] += _dot_tn(dseg, hb)
        dlg_ref[...] += _colsum(dh0 * xh)
        dlb_ref[...] += _colsum(dh0)
        draw = _ln_bwd(dh0 * lig[...], xh, rs_ref[...])
        stage[slot] = draw

        @pl.when(i == 0)
        def _():
            dmeta_ref[...] = draw[PAD:CHUNK, :]
            first = to_gx(0, True)
            first.start()
            first.wait()

        @pl.when(i > 0)
        def _():
            to_gx(slot, False).start()

        @pl.when(i == nb - 1)
        def _():
            gw_ref[...] = aw[...].astype(gw_ref.dtype)
            for back in (1, 0):
                if nb - 1 - back >= 1:
                    to_gx((nb - 1 - back) % 2, False).wait()

    row = lambda w: pl.BlockSpec((R, w), lambda i: (i, 0))
    full = lambda a: pl.BlockSpec(a.shape, lambda i: (0,) * a.ndim)
    acc = lambda s, dt=F32: (pl.BlockSpec(s, lambda i: (0, 0)), jax.ShapeDtypeStruct(s, dt))
    outs = [(_ANY, jax.ShapeDtypeStruct((tp - CHUNK, D_MODEL), F32)), acc((N_META, D_MODEL)), acc((PROJ_W, D_MODEL), MM),
            acc((1, D_MODEL)), acc((1, D_MODEL))]
    return _call(
        body, "in_bwd", (nb,),
        [row(S5_W), row(RET_W), row(RET_W), row(RET_W), row(RET_W), row(D_MODEL), row(D_MODEL), row(1),
         full(li_g), full(li_b), _VMEM],
        [o[0] for o in outs], [o[1] for o in outs],
        [pltpu.VMEM((PROJ_W, D_MODEL), F32), pltpu.VMEM((2, R, D_MODEL), F32), pltpu.SemaphoreType.DMA((2,))],
        (du, dq, dk, dv, dg, dh0r, xhat0, rstd0, li_g, li_b, w_int), jobs)


def _place():
    return lax.axis_index("x"), lax.axis_index("y"), lax.axis_index("c")


def _dma_sems(n):
    return pltpu.SemaphoreType.DMA((n,))


def _job_gather(shard):
    def parts(ins, outs, sems):
        (src,), (out,), (send_sems, recv_sems, local_sem) = ins, outs, sems
        x, y, c = _place()
        north = c == 1
        me, sib = (x, y, c), (x, y, 1 - c)
        xn, yn, dg = (1 - x, y, c), (x, 1 - y, c), (1 - x, 1 - y, c)
        relay_from = (jnp.where(north, 1 - x, x), jnp.where(north, y, 1 - y), c)
        relay_to = (jnp.where(north, x, 1 - x), jnp.where(north, 1 - y, y), c)

        def slot(dev):
            return out.at[4 * dev[0] + 2 * dev[1] + dev[2]]

        def copy(k, block, to, from_input=False):
            return pltpu.make_async_remote_copy(
                src_ref=src if from_input else slot(block), dst_ref=slot(block),
                send_sem=send_sems.at[k], recv_sem=recv_sems.at[k], device_id=to, device_id_type=_MESH)

        mine = lambda: pltpu.make_async_copy(src, slot(me), local_sem.at[0])
        first = lambda: [copy(0, me, sib, True), copy(1, me, xn, True), copy(2, me, yn, True)]
        relayed = lambda: [copy(3, relay_from, relay_to), copy(4, xn, sib), copy(5, yn, sib)]
        return me, sib, xn, yn, dg, copy, mine, first, relayed

    def start(ins, outs, sems):
        mine, first = parts(ins, outs, sems)[6:8]
        mine().start()
        for cp in first():
            cp.start()

    def relay(ins, outs, sems):
        me, sib, xn, yn, dg, copy, mine, first, relayed = parts(ins, outs, sems)
        copy(1, xn, me).wait_recv()
        copy(2, yn, me).wait_recv()
        for cp in relayed():
            cp.start()

    def finish(ins, outs, sems):
        me, sib, xn, yn, dg, copy, mine, first, relayed = parts(ins, outs, sems)
        other = 1 - me[2]
        copy(3, dg, me).wait_recv()
        last = copy(6, dg, sib)
        last.start()
        copy(0, sib, me).wait_recv()
        for k, chip in ((4, xn), (5, yn), (6, dg)):
            copy(k, (chip[0], chip[1], other), me).wait_recv()
        for cp in first() + relayed() + [last]:
            cp.wait_send()
        mine().wait()

    return dict(ins=[shard], outs=[jax.ShapeDtypeStruct((N_DEV,) + shard.shape, shard.dtype)],
                sems=[_dma_sems(7), _dma_sems(7), _dma_sems(1)], start=start, middle=relay, finish=finish)


def _job_pair(g):
    def copies(ins, outs, sems):
        x, y, c = _place()
        return [pltpu.make_async_remote_copy(
            src_ref=ins[0].at[2 * j + (1 - c)], dst_ref=outs[0].at[j], send_sem=sems[0].at[j], recv_sem=sems[1].at[j],
            device_id=(x, y, 1 - c), device_id_type=_MESH) for j in range(4)]

    def start(ins, outs, sems):
        for cp in copies(ins, outs, sems):
            cp.start()

    def finish(ins, outs, sems):
        for cp in copies(ins, outs, sems):
            cp.wait()

    return dict(ins=[g], outs=[jax.ShapeDtypeStruct((4,) + g.shape[1:], g.dtype)], sems=[_dma_sems(4), _dma_sems(4)],
                start=start, finish=finish)


def _job_chips(p):
    def copies(ins, outs, sems):
        x, y, c = _place()
        chips = [(1 - x, y), (x, 1 - y), (1 - x, 1 - y)]
        return [pltpu.make_async_remote_copy(
            src_ref=ins[0].at[2 * chip[0] + chip[1]], dst_ref=outs[0].at[k], send_sem=sems[0].at[k],
            recv_sem=sems[1].at[k], device_id=(*chip, c), device_id_type=_MESH) for k, chip in enumerate(chips)]

    def start(ins, outs, sems):
        for cp in copies(ins, outs, sems):
            cp.start()

    def finish(ins, outs, sems):
        for cp in copies(ins, outs, sems):
            cp.wait()

    return dict(ins=[p], outs=[jax.ShapeDtypeStruct((3,) + p.shape[1:], p.dtype)], sems=[_dma_sems(3), _dma_sems(3)],
                start=start, finish=finish)


def _split_job_refs(jobs, ins, outs, sems):
    res, a, b, c = [], 0, 0, 0
    for job in jobs:
        na, nb, nc = len(job["ins"]), len(job["outs"]), len(job["sems"])
        res.append((ins[a:a + na], outs[b:b + nb], sems[c:c + nc]))
        a, b, c = a + na, b + nb, c + nc
    return res


def _call(body, name, grid, in_specs, out_specs, out_shape, scratch, args, jobs=(), prefetch=None, early=0):
    jobs = list(jobs)
    n_in, n_out, n_scr = len(in_specs), len(out_specs), len(scratch)
    j_in = [a for job in jobs for a in job["ins"]]
    j_out = [o for job in jobs for o in job["outs"]]
    j_scr = [s for job in jobs for s in job["sems"]]
    nsteps = grid[0]
    n_pre = 0 if prefetch is None else 1

    def wrapped(*refs):
        pre, refs = refs[:n_pre], refs[n_pre:]
        ins, jins = refs[:n_in], refs[n_in:n_in + len(j_in)]
        refs = refs[n_in + len(j_in):]
        outs, jouts = refs[:n_out], refs[n_out:n_out + len(j_out)]
        refs = refs[n_out + len(j_out):]
        scr, jscr = refs[:n_scr], refs[n_scr:]
        per_job = _split_job_refs(jobs, jins, jouts, jscr)

        def middle():
            for job, r in zip(jobs, per_job):
                if "middle" in job:
                    job["middle"](*r)

        @pl.when(pl.program_id(0) == 0)
        def _():
            for job, r in zip(jobs, per_job):
                job["start"](*r)

        if nsteps >= 3:
            pl.when(pl.program_id(0) == nsteps // 2)(middle)

        if early:
            @pl.when(pl.program_id(0) == nsteps - 1)
            def _():
                for job, r in zip(jobs[:early], per_job[:early]):
                    job["finish"](*r)

        body(*pre, *ins, *outs, *scr, *[o for r in per_job[:early] for o in r[1]])

        @pl.when(pl.program_id(0) == nsteps - 1)
        def _():
            if nsteps < 3:
                middle()
            for job, r in zip(jobs[early:], per_job[early:]):
                job["finish"](*r)

    specs = dict(in_specs=list(in_specs) + [_ANY] * len(j_in), out_specs=list(out_specs) + [_ANY] * len(j_out),
                 scratch_shapes=list(scratch) + j_scr)
    if n_pre:
        specs = dict(grid_spec=pltpu.PrefetchScalarGridSpec(num_scalar_prefetch=1, grid=grid, **specs))
    else:
        specs["grid"] = grid
    res = pl.pallas_call(
        wrapped if jobs else body, name=name, out_shape=list(out_shape) + j_out,
        compiler_params=_params(("arbitrary",) * len(grid)), **specs,
    )(*([prefetch] if n_pre else []), *args, *j_in)
    return list(res[:n_out]), list(res[n_out:])


def _exchange(jobs, name):
    j_in = [a for job in jobs for a in job["ins"]]
    j_out = [o for job in jobs for o in job["outs"]]
    j_scr = [s for job in jobs for s in job["sems"]]

    def body(*refs):
        per_job = _split_job_refs(jobs, refs[:len(j_in)], refs[len(j_in):len(j_in) + len(j_out)],
                                  refs[len(j_in) + len(j_out):])
        for phase in ("start", "middle", "finish"):
            for job, r in zip(jobs, per_job):
                if phase in job:
                    job[phase](*r)

    return pl.pallas_call(body, name=name, out_shape=j_out, in_specs=[_ANY] * len(j_in), out_specs=[_ANY] * len(j_out),
                          scratch_shapes=j_scr)(*j_in)


def _pair_sum(gs, r1s, c_arr, name):
    n = len(gs)

    def body(c_ref, *refs):
        for a in range(n):
            refs[2 * n + a][...] = (refs[a][...].astype(F32) + refs[n + a][...].astype(F32)).astype(refs[2 * n + a].dtype)

    def blk(g, own):
        s = g.shape[1:]
        if own:
            return pl.BlockSpec((1,) + s, lambda j, c_ref: (2 * j + c_ref[0],) + (0,) * len(s))
        return pl.BlockSpec((1,) + s, lambda j, c_ref: (j,) + (0,) * len(s))

    return pl.pallas_call(
        body, name=name,
        grid_spec=pltpu.PrefetchScalarGridSpec(
            num_scalar_prefetch=1, grid=(4,),
            in_specs=[blk(g, True) for g in gs] + [blk(g, False) for g in gs],
            out_specs=[blk(g, False) for g in gs]),
        out_shape=[jax.ShapeDtypeStruct((4,) + g.shape[1:], g.dtype) for g in gs],
        compiler_params=_params(("arbitrary",)),
    )(c_arr, *gs, *r1s)


def _adamw_math(w, g, m, v):
    m = ADAM_B1 * m + (1.0 - ADAM_B1) * g
    v = ADAM_B2 * v + (1.0 - ADAM_B2) * (g * g)
    m_hat = m / (1.0 - ADAM_B1 ** ADAM_STEP)
    v_hat = v / (1.0 - ADAM_B2 ** ADAM_STEP)
    return -ADAM_LR * (m_hat / (jnp.sqrt(v_hat) + ADAM_EPS) + ADAM_WD * w), m, v


def _view(name, a):
    return jnp.swapaxes(a, -1, -2) if name in ("w_in", "s5_b_re", "s5_b_im") else a


def _adamw_shards(items, name, steps, chip, jobs=()):
    n = len(items)

    def body(chip_ref, *refs):
        for a in range(n):
            p_ref, r_ref, w_ref, m_ref, v_ref = refs[5 * a:5 * a + 5]
            g = ((p_ref[0].astype(F32) + r_ref[0].astype(F32)) + r_ref[1].astype(F32)) + r_ref[2].astype(F32)
            outs = refs[5 * n + 4 * a:5 * n + 4 * a + 4]
            outs[0][...] = g
            outs[1][...], outs[2][...], outs[3][...] = _adamw_math(w_ref[...], g, m_ref[...], v_ref[...])

    in_specs, out_specs, out_shape, flat = [], [], [], []
    for p, r, w, m, v in items:
        rows, cols = w.shape
        rb = rows // steps
        in_specs += [pl.BlockSpec((1, rb, cols), lambda i, c: (c[0], i, 0)), pl.BlockSpec((3, rb, cols), lambda i, c: (0, i, 0))]
        wblk = pl.BlockSpec((rb, cols), lambda i, c: (i, 0))
        in_specs += [wblk] * 3
        out_specs += [wblk] * 4
        out_shape += [jax.ShapeDtypeStruct(w.shape, F32)] * 4
        flat += [p, r, w, m, v]
    return _call(body, name, (steps,), in_specs, out_specs, out_shape, [], flat, jobs, prefetch=chip)


def _sum_devices(gathered, name):
    def body(gs_ref, g_ref):
        g = gs_ref[0]
        for s in range(1, N_DEV):
            g = g + gs_ref[s]
        g_ref[...] = g

    return pl.pallas_call(body, name=name, out_shape=jax.ShapeDtypeStruct(gathered.shape[1:], F32),
                          in_specs=[_VMEM], out_specs=_VMEM, compiler_params=_params())(gathered)


def _adamw_native(items, name):
    n = len(items)

    def body(*refs):
        for a in range(n):
            g, w, m, v = (refs[4 * a + t][...] for t in range(4))
            refs[4 * n + 3 * a][...], refs[4 * n + 3 * a + 1][...], refs[4 * n + 3 * a + 2][...] = _adamw_math(w, g, m, v)

    return pl.pallas_call(
        body, name=name, out_shape=[jax.ShapeDtypeStruct(it[1].shape, F32) for it in items for _ in range(3)],
        in_specs=[_VMEM] * (4 * n), out_specs=[_VMEM] * (3 * n), compiler_params=_params(),
    )(*[t for it in items for t in it])


SMALL = ["ln_in_g", "ln_in_b", "s5_lambda_re", "s5_lambda_im", "s5_log_dt", "s5_b_re", "s5_b_im", "s5_c_re", "s5_c_im",
         "s5_d", "s5_b_glu", "ret_gn_g", "ret_gn_b", "ln1_g", "ln1_b", "ln2_g", "ln2_b"]
LATE = ["ln_in_g", "ln_in_b", "meta_tokens"]
EARLY = [n for n in SMALL if n not in LATE] + ["s5_w_glu", "loss"]
LANE = 128


def _pack(arrs):
    parts = []
    for a in arrs:
        f = a.reshape(-1)
        parts.append(jnp.pad(f, (0, (-f.shape[0]) % LANE)))
    flat = jnp.concatenate(parts)
    rows = -(-flat.shape[0] // LANE)
    flat = jnp.pad(flat, (0, (-rows % 8) * LANE + rows * LANE - flat.shape[0]))
    return flat.reshape(-1, LANE)


def _unpack(packed, shapes):
    flat = packed.reshape(-1)
    out, off = [], 0
    for s in shapes:
        n = math.prod(s)
        out.append(flat[off:off + n].reshape(s))
        off += n + (-n) % LANE
    return out


def _rope_tables(tp):
    inv_freq = 1.0 / (ROPE_BASE ** (jnp.arange(0, HEAD, 2, dtype=F32) / HEAD))
    blk = (jnp.arange(tp // ROW_BLK, dtype=F32) * ROW_BLK)[:, None, None] * inv_freq
    off = (jnp.arange(ROW_BLK, dtype=F32) - float(PAD))[None, :, None] * inv_freq
    cos = (jnp.cos(blk) * jnp.cos(off) - jnp.sin(blk) * jnp.sin(off)).reshape(tp, HEAD // 2)
    sin = (jnp.sin(blk) * jnp.cos(off) + jnp.cos(blk) * jnp.sin(off)).reshape(tp, HEAD // 2)
    return jnp.concatenate([cos, cos], axis=1), jnp.concatenate([-sin, sin], axis=1)


def _local_step(x2d, tgt, meta, w_int, w_out, w_up, w_down, w_glu, sp, distributed):
    tp = x2d.shape[0] + CHUNK
    row = lambda a: a.reshape(1, -1)
    cos2, sin2 = _rope_tables(tp)
    li_g, li_b = row(sp["ln_in_g"]), row(sp["ln_in_b"])
    l1_g, l1_b, l2_g, l2_b = row(sp["ln1_g"]), row(sp["ln1_b"]), row(sp["ln2_g"]), row(sp["ln2_b"])
    gn_g, gn_b = row(sp["ret_gn_g"]), row(sp["ret_gn_b"])
    lre, lim = row(sp["s5_lambda_re"]), row(sp["s5_lambda_im"])
    ldt = row(jnp.repeat(sp["s5_log_dt"].reshape(-1), S5_P))
    to_t = lambda b: b.reshape(S5_G, S5_P, S5_H).transpose(2, 0, 1).reshape(S5_H, S5_N)
    bre_t, bim_t = to_t(sp["s5_b_re"]), to_t(sp["s5_b_im"])
    to_w = lambda c: jnp.tile(c.reshape(S5_W, S5_P), (1, 2))
    cre_w, cim_w = to_w(sp["s5_c_re"]), to_w(sp["s5_c_im"])

    jobs = (lambda *j: list(j)) if distributed else (lambda *j: [])
    c_arr = jnp.reshape(lax.axis_index("c"), (1,)).astype(jnp.int32) if distributed else None
    (xhat0, rstd0), bg = _ln_in(x2d, meta, jobs(*([_job_gather(w_int), _job_gather(w_glu)] if distributed else [])),
                                gather_meta=distributed)
    if distributed:
        w_int, w_glu = bg[1].reshape(PROJ_W, D_MODEL), bg[2].reshape(S5_W, S5_W)
    s5_small = (lre, lim, ldt, bre_t, bim_t, cre_w, cim_w, row(sp["s5_d"]), w_glu, row(sp["s5_b_glu"]))
    (u, q, k, v, gate), bg = _in_proj(xhat0, li_g, li_b, w_int, cos2, sin2,
                                      jobs(_job_gather(w_out) if distributed else None))
    if distributed:
        w_out = bg[0].reshape(D_MODEL, D_MODEL)
    (ys5, xr, xi), bg = _s5_fwd(u, *s5_small, jobs=jobs(_job_gather(w_up) if distributed else None))
    if distributed:
        w_up = bg[0]
    (o, states), _ = _ret_fwd(q, k, v)
    (ycat, xhat1, rstd1, h1b, pre), bg = _post_up(o, gate, ys5, xhat0, gn_g, gn_b, li_g, li_b, l1_g, l1_b, w_out, w_up,
                                                  jobs(_job_gather(w_down) if distributed else None))
    if distributed:
        w_down = bg[0].reshape(D_FF, D_MODEL)
    dr2, dffb, loss8, dl2g, dl2b = _post_down(pre, xhat1, tgt, l1_g, l1_b, l2_g, l2_b, w_down)
    g_up, g_down, dh1m = _mlp_bwd(h1b, dffb, pre, w_up, w_down)
    (do, dgate, dys5, dh0r, g_out, dl1g, dl1b, dgng, dgnb), bg = _post_bwd(
        dh1m, dr2, xhat1, rstd1, ycat, o, gate, gn_g, gn_b, l1_g, w_out,
        jobs(*([_job_pair(g_up), _job_pair(g_down)] if distributed else [])))
    g_out = g_out.reshape(N_DEV, D_MODEL // N_DEV, D_MODEL)
    if distributed:
        p_up, p_down = _pair_sum([g_up, g_down], bg, c_arr, "pair_sum_mlp")
    (du, dlre, dlim, dldt, dbre_t, dbim_t, dcre, dcim, dd, dwglu, dbglu), bg = _s5_bwd(
        dys5, u, xr, xi, *s5_small,
        jobs=jobs(*([_job_chips(p_up), _job_chips(p_down), _job_pair(g_out)] if distributed else [])))
    if distributed:
        r_up, r_down = bg[0], bg[1]
        (p_out,) = _pair_sum([g_out], bg[2:], c_arr, "pair_sum_out")
    from_t = lambda t: t.reshape(S5_H, S5_G, S5_P).transpose(1, 0, 2)
    small = {
        "s5_lambda_re": dlre, "s5_lambda_im": dlim, "s5_log_dt": dldt[:, :S5_G],
        "s5_b_re": from_t(dbre_t), "s5_b_im": from_t(dbim_t), "s5_c_re": dcre, "s5_c_im": dcim, "s5_d": dd,
        "s5_b_glu": dbglu, "ret_gn_g": dgng, "ret_gn_b": dgnb, "ln1_g": dl1g, "ln1_b": dl1b, "ln2_g": dl2g, "ln2_b": dl2b,
        "s5_w_glu": dwglu, "loss": loss8[0:1, 0:1]}
    early_pack = _pack([small[n] for n in EARLY])
    (dq, dk, dv), bg = _ret_bwd(q, k, v, do, states, cos2, sin2,
                                jobs(*([_job_chips(p_out), _job_gather(early_pack)] if distributed else [])))
    (grad_x, dmeta, g_int, dlig, dlib), _ = _in_bwd(du, dq, dk, dv, dgate, dh0r, xhat0, rstd0, li_g, li_b, w_int)
    small.update(ln_in_g=dlig, ln_in_b=dlib, meta_tokens=dmeta)
    g_int = g_int.reshape(N_DEV, PROJ_W // N_DEV, D_MODEL)
    if distributed:
        (r1_in,) = _exchange([_job_pair(g_int)], "exchange_pair_in")
        (p_in,) = _pair_sum([g_int], [r1_in], c_arr, "pair_sum_in")
        big = dict(chip_sums=[p_in, p_out, p_up, p_down], received=[None, bg[0], r_up, r_down], early=bg[1])
    else:
        big = dict(partials=[g_int, g_out, g_up, g_down])
    return grad_x, big, small


def kernel(x, meta_tokens, ln_in_g, ln_in_b, w_in, s5_lambda_re, s5_lambda_im, s5_log_dt, s5_b_re, s5_b_im, s5_c_re, s5_c_im, s5_d, s5_w_glu, s5_b_glu, ret_gn_g, ret_gn_b, w_out, ln1_g, ln1_b, w_up, w_down, ln2_g, ln2_b, loss_target, m_meta_tokens, m_ln_in_g, m_ln_in_b, m_w_in, m_s5_lambda_re, m_s5_lambda_im, m_s5_log_dt, m_s5_b_re, m_s5_b_im, m_s5_c_re, m_s5_c_im, m_s5_d, m_s5_w_glu, m_s5_b_glu, m_ret_gn_g, m_ret_gn_b, m_w_out, m_ln1_g, m_ln1_b, m_w_up, m_w_down, m_ln2_g, m_ln2_b, v_meta_tokens, v_ln_in_g, v_ln_in_b, v_w_in, v_s5_lambda_re, v_s5_lambda_im, v_s5_log_dt, v_s5_b_re, v_s5_b_im, v_s5_c_re, v_s5_c_im, v_s5_d, v_s5_w_glu, v_s5_b_glu, v_ret_gn_g, v_ret_gn_b, v_w_out, v_ln1_g, v_ln1_b, v_w_up, v_w_down, v_ln2_g, v_ln2_b):
    args = dict(locals())
    names = ["meta_tokens", "ln_in_g", "ln_in_b", "w_in", "s5_lambda_re", "s5_lambda_im", "s5_log_dt", "s5_b_re", "s5_b_im",
             "s5_c_re", "s5_c_im", "s5_d", "s5_w_glu", "s5_b_glu", "ret_gn_g", "ret_gn_b", "w_out", "ln1_g", "ln1_b",
             "w_up", "w_down", "ln2_g", "ln2_b"]
    ax, ay, ac = _place()
    me = 4 * ax + 2 * ay + ac

    sp = {n: args[n] for n in SMALL}
    grad_x, big, small = _local_step(x[0], loss_target[0], meta_tokens, w_in[0].T.astype(MM), w_out[0].astype(MM),
                                   w_up[0].astype(MM), w_down[0].astype(MM), s5_w_glu[0].astype(MM), sp, True)

    j_arr = jnp.reshape(2 * ax + ay, (1,)).astype(jnp.int32)
    two_d = lambda a: a.reshape(a.shape[-2:])
    item = lambda n, p, r: (p, r, *(two_d(_view(n, a)) for a in (args[n], args["m_" + n], args["v_" + n])))
    late_pack = _pack([small[n] for n in LATE])
    mlp = ("w_out", "w_up", "w_down")
    res, (r_in, late_all) = _adamw_shards(
        [item(n, p, r) for n, p, r in zip(mlp, big["chip_sums"][1:], big["received"][1:])], "adamw_mlp", 8, j_arr,
        [_job_chips(big["chip_sums"][0]), _job_gather(late_pack)])
    res_in, _ = _adamw_shards([item("w_in", big["chip_sums"][0], r_in)], "adamw_in", 2, j_arr)
    upd = {"w_in": res_in}
    for idx, n in enumerate(mlp):
        upd[n] = res[4 * idx:4 * idx + 4]
    shard_grads = {n: upd[n][0] for n in upd}

    early_shapes = [_view(n, args[n]).shape for n in EARLY[:-2]] + [(S5_W, S5_W), (1,)]
    late_shapes = [args["ln_in_g"].shape, args["ln_in_b"].shape, (N_META, D_MODEL)]
    g_small = dict(zip(EARLY, _unpack(_sum_devices(big["early"], "sum_small_early"), early_shapes)))
    g_small.update(zip(LATE, _unpack(_sum_devices(late_all, "sum_small_late"), late_shapes)))
    loss = g_small["loss"].reshape(())

    shard_grads["meta_tokens"] = lax.dynamic_slice(g_small["meta_tokens"], (0, me * (D_MODEL // N_DEV)),
                                                   (N_META, D_MODEL // N_DEV))
    shard_grads["s5_w_glu"] = lax.dynamic_slice(g_small["s5_w_glu"], (me * (S5_W // N_DEV), 0),
                                                (S5_W // N_DEV, S5_W))[None]
    natives = SMALL + ["meta_tokens", "s5_w_glu"]
    res2 = _adamw_native([(shard_grads[n] if n in shard_grads else g_small[n], *(_view(n, args[p + n]) for p in ("", "m_", "v_")))
                          for n in natives], "adamw_small")
    for idx, n in enumerate(natives):
        upd[n] = [shard_grads[n] if n in shard_grads else g_small[n]] + list(res2[3 * idx:3 * idx + 3])

    grads, deltas, new_m, new_v = ([_view(n, upd[n][t]).reshape(args[n].shape) for n in names] for t in range(4))
    return (loss, grad_x[None], *grads, *deltas, *new_m, *new_v)
```

```python
import math

import jax
import jax.numpy as jnp
from jax import lax
from jax.experimental import pallas as pl
from jax.experimental.pallas import tpu as pltpu

F32 = jnp.float32
MM = jnp.bfloat16

D_MODEL = 1024
N_META = 16
CHUNK = 128
PAD = CHUNK - N_META
S5_W, S5_G, S5_H, S5_P = 256, 16, 16, 64
S5_N = S5_G * S5_P
RET_W, RET_H, HEAD = 768, 6, 128
D_FF = 4096
PROJ_W = S5_W + 4 * RET_W
N_DEV = 8
FF_BLK = D_FF // N_DEV
ROW_BLK = 384
MLP_ROWS = 1408
PROJ_ROWS = 704
ALPHA = 2.0 ** 0.25
LN_EPS = 1e-5
GN_EPS = 1e-5
ROPE_BASE = 10000.0
GELU_C = math.sqrt(2.0 / math.pi)
GELU_A = 0.044715
ADAM_LR, ADAM_B1, ADAM_B2, ADAM_EPS, ADAM_WD, ADAM_STEP = 0.001, 0.9, 0.999, 1e-08, 0.01, 10
VMEM_LIMIT = 60 * 1024 * 1024

_VMEM = pl.BlockSpec(memory_space=pltpu.VMEM)
_ANY = pl.BlockSpec(memory_space=pl.ANY)
_MESH = pl.DeviceIdType.MESH


def _params(sem=None):
    return pltpu.CompilerParams(dimension_semantics=sem, vmem_limit_bytes=VMEM_LIMIT)


def _dot(a, b):
    return jnp.dot(a.astype(MM), b.astype(MM), preferred_element_type=F32)


def _dot_nt(a, b):
    return lax.dot_general(a.astype(MM), b.astype(MM), (((1,), (1,)), ((), ())), preferred_element_type=F32)


def _dot_tn(a, b):
    return lax.dot_general(a.astype(MM), b.astype(MM), (((0,), (0,)), ((), ())), preferred_element_type=F32)


def _split3(a):
    hi = a.astype(jnp.bfloat16)
    r1 = a - hi.astype(F32)
    mid = r1.astype(jnp.bfloat16)
    lo = (r1 - mid.astype(F32)).astype(jnp.bfloat16)
    return hi, mid, lo


def _dot_sel_rhs(a, sel):
    s = sel.astype(jnp.bfloat16)
    return sum(jnp.dot(p, s, preferred_element_type=F32) for p in _split3(a))


def _dot_sel_lhs(sel, b):
    s = sel.astype(jnp.bfloat16)
    return sum(jnp.dot(s, p, preferred_element_type=F32) for p in _split3(b))


def _ln_fwd(r, eps):
    mu = jnp.mean(r, axis=-1, keepdims=True)
    xc = r - mu
    var = jnp.mean(xc * xc, axis=-1, keepdims=True)
    rstd = lax.rsqrt(var + eps)
    return xc * rstd, rstd


def _ln_bwd(dxhat, xhat, rstd):
    m1 = jnp.mean(dxhat, axis=-1, keepdims=True)
    m2 = jnp.mean(dxhat * xhat, axis=-1, keepdims=True)
    return rstd * (dxhat - m1 - xhat * m2)


def _colsum(a):
    return jnp.sum(a, axis=0, keepdims=True)


def _shift3(n_in, block=lambda i: i):
    return [pl.BlockSpec((CHUNK, D_MODEL), (lambda i, j=j: (jnp.clip(3 * block(i) - 1 + j, 0, n_in - 1), 0)))
            for j in range(3)]


def _ln_in(x2d, meta, jobs=(), gather_meta=False):
    seq = x2d.shape[0]
    tp = seq + CHUNK
    R = ROW_BLK
    nb = tp // R
    shard_w = D_MODEL // N_DEV

    def body(xa, xb, xc, meta_ref, xhat_ref, rstd_ref, raw_ref, *gathered):
        raw_ref[0:CHUNK, :] = xa[...]
        raw_ref[CHUNK:2 * CHUNK, :] = xb[...]
        raw_ref[2 * CHUNK:3 * CHUNK, :] = xc[...]

        @pl.when(pl.program_id(0) == nb - 1)
        def _():
            raw_ref[0:PAD, :] = jnp.zeros((PAD, D_MODEL), F32)
            if gather_meta:
                for d in range(N_DEV):
                    pltpu.sync_copy(gathered[0].at[d], raw_ref.at[PAD:CHUNK, d * shard_w:(d + 1) * shard_w])
            else:
                raw_ref[PAD:CHUNK, :] = meta_ref[...]

        xhat_ref[...], rstd_ref[...] = _ln_fwd(raw_ref[...], LN_EPS)

    row = lambda w: pl.BlockSpec((R, w), lambda i: (nb - 1 - i, 0))
    jobs = ([_job_gather(meta)] if gather_meta else []) + list(jobs)
    return _call(
        body, "ln_in", (nb,),
        _shift3(seq // CHUNK, lambda i: nb - 1 - i) + [pl.BlockSpec(meta.shape, lambda i: (0, 0))],
        [row(D_MODEL), row(1)], [jax.ShapeDtypeStruct((tp, D_MODEL), F32), jax.ShapeDtypeStruct((tp, 1), F32)],
        [pltpu.VMEM((R, D_MODEL), F32)], (x2d, x2d, x2d, meta), jobs, early=1 if gather_meta else 0)


def _in_proj(xhat0, ln_g, ln_b, w_int, cos2, sin2, jobs=()):
    tp = xhat0.shape[0]
    R = PROJ_ROWS if tp % PROJ_ROWS == 0 else ROW_BLK

    def body(xh_ref, g_ref, b_ref, w_ref, cos_ref, sin_ref, u_ref, q_ref, k_ref, v_ref, gate_ref):
        hb = (xh_ref[...] * g_ref[...] + b_ref[...]).astype(MM)
        valid = (pl.program_id(0) * R + lax.broadcasted_iota(jnp.int32, (R, 1), 0)) >= PAD

        def seg(lo, hi):
            return jnp.where(valid, _dot_nt(hb, w_ref[lo:hi, :]), 0.0)

        u_ref[...] = seg(0, S5_W)
        cos = cos_ref[...]
        sin = sin_ref[...]
        q = seg(S5_W, S5_W + RET_W)
        k = seg(S5_W + RET_W, S5_W + 2 * RET_W)
        for h in range(RET_H):
            sl = slice(h * HEAD, (h + 1) * HEAD)
            qh = q[:, sl]
            kh = k[:, sl]
            q_ref[:, sl] = (qh * cos + pltpu.roll(qh, HEAD // 2, 1) * sin).astype(q_ref.dtype)
            k_ref[:, sl] = ((kh * cos + pltpu.roll(kh, HEAD // 2, 1) * sin) * (HEAD ** -0.5)).astype(k_ref.dtype)
        v_ref[...] = seg(S5_W + 2 * RET_W, S5_W + 3 * RET_W).astype(v_ref.dtype)
        gate_ref[...] = seg(S5_W + 3 * RET_W, PROJ_W)

    def rows(w, dt):
        return pl.BlockSpec((R, w), lambda i: (i, 0)), jax.ShapeDtypeStruct((tp, w), dt)

    outs = [rows(S5_W, F32), rows(RET_W, MM), rows(RET_W, MM), rows(RET_W, MM), rows(RET_W, F32)]
    full = lambda s: pl.BlockSpec(s, lambda i: (0,) * len(s))
    return _call(
        body, "in_proj", (tp // R,),
        [pl.BlockSpec((R, D_MODEL), lambda i: (i, 0)), full((1, D_MODEL)), full((1, D_MODEL)), _VMEM,
         pl.BlockSpec((R, HEAD), lambda i: (i, 0)), pl.BlockSpec((R, HEAD), lambda i: (i, 0))],
        [o[0] for o in outs], [o[1] for o in outs], [], (xhat0, ln_g, ln_b, w_int, cos2, sin2), jobs)


def _s5_disc(lre, lim, ldt, bre_t, bim_t):
    dt = jnp.exp(ldt)
    mag = jnp.exp(lre * dt)
    ang = lim * dt
    lbr = mag * jnp.cos(ang)
    lbi = mag * jnp.sin(ang)
    den = lre * lre + lim * lim
    nr = lbr - 1.0
    qr = (nr * lre + lbi * lim) / den
    qi = (lbi * lre - nr * lim) / den
    return lbr, lbi, qr * bre_t - qi * bim_t, qr * bim_t + qi * bre_t


def _s5_tables(lbr, lbi, reverse):
    if reverse:
        lbi = -lbi
    pw = [(lbr, lbi)]
    for _ in range(7):
        r, i = pw[-1]
        pw.append((r * lbr - i * lbi, r * lbi + i * lbr))
    row = lax.broadcasted_iota(jnp.int32, (8, S5_N), 0)
    tabs = []
    for k in range(3):
        sh = 2 ** k
        mask = (row < 8 - sh) if reverse else (row >= sh)
        ar, ai = pw[sh - 1]
        tabs.append((jnp.where(mask, ar, 0.0), jnp.where(mask, ai, 0.0)))
    pr = jnp.zeros((8, S5_N), F32)
    pi = jnp.zeros((8, S5_N), F32)
    for i in range(8):
        ar, ai = pw[7 - i] if reverse else pw[i]
        pr = jnp.where(row == i, ar, pr)
        pi = jnp.where(row == i, ai, pi)
    tabs.append((pr, pi))
    return tabs


def _store_tables(tab_ref, tabs):
    for k, (r, i) in enumerate(tabs):
        tab_ref[2 * k] = r
        tab_ref[2 * k + 1] = i


def _bd_mask():
    r = lax.broadcasted_iota(jnp.int32, (S5_W, S5_N), 0)
    c = lax.broadcasted_iota(jnp.int32, (S5_W, S5_N), 1)
    return jnp.right_shift(r, 4) == jnp.right_shift(c, 6)


def _s5_block_diag(bbr_t, bbi_t, cre_w, cim_w):
    mask = _bd_mask()
    bd = lambda t: jnp.where(mask, t, 0.0)
    return (bd(jnp.tile(bbr_t, (S5_G, 1))), bd(jnp.tile(bbi_t, (S5_G, 1))),
            bd(jnp.tile(cre_w, (1, S5_N // HEAD))), bd(jnp.tile(cim_w, (1, S5_N // HEAD))))


def _scan8(xr, xi, tab_ref, lanes, reverse):
    for k in range(3):
        sh = (8 - 2 ** k) if reverse else 2 ** k
        sr = pltpu.roll(xr, sh, 0)
        si = pltpu.roll(xi, sh, 0)
        mr = tab_ref[2 * k, :, lanes]
        mi = tab_ref[2 * k + 1, :, lanes]
        xr, xi = xr + (mr * sr - mi * si), xi + (mr * si + mi * sr)
    return xr, xi


S5_LANES = 512


def _gelu(y):
    t = jnp.tanh(GELU_C * (y + GELU_A * y * y * y))
    return 0.5 * y * (1.0 + t), t


def _s5_fwd(u, lre, lim, ldt, bre_t, bim_t, cre_w, cim_w, d_row, w_glu, b_glu, jobs=()):
    tp = u.shape[0]
    R = ROW_BLK

    def body(u_ref, lre_ref, lim_ref, ldt_ref, bre_ref, bim_ref, cre_ref, cim_ref, d_ref, wg_ref, bg_ref,
             y_ref, xr_ref, xi_ref, bbd_r, bbd_i, cbd_r, cbd_i, tab_ref, car_r, car_i):
        @pl.when(pl.program_id(0) == 0)
        def _():
            lbr, lbi, bbr, bbi = _s5_disc(lre_ref[...], lim_ref[...], ldt_ref[...], bre_ref[...], bim_ref[...])
            br, bi, cr, ci = _s5_block_diag(bbr, bbi, cre_ref[...], cim_ref[...])
            bbd_r[...] = br.astype(MM)
            bbd_i[...] = bi.astype(MM)
            cbd_r[...] = cr.astype(MM)
            cbd_i[...] = ci.astype(MM)
            _store_tables(tab_ref, _s5_tables(lbr, lbi, False))
            car_r[...] = jnp.zeros_like(car_r)
            car_i[...] = jnp.zeros_like(car_i)

        u = u_ref[...]
        ub = u.astype(MM)
        xr_ref[...] = jnp.dot(ub, bbd_r[...], preferred_element_type=F32)
        xi_ref[...] = jnp.dot(ub, bbd_i[...], preferred_element_type=F32)
        for j in range(S5_N // S5_LANES):
            lanes = pl.ds(j * S5_LANES, S5_LANES)
            pr = tab_ref[6, :, lanes]
            pi = tab_ref[7, :, lanes]

            def step(g, carry):
                cr, ci = carry
                rows = pl.ds(pl.multiple_of(g * 8, 8), 8)
                xr, xi = _scan8(xr_ref[rows, lanes], xi_ref[rows, lanes], tab_ref, lanes, False)
                br = jnp.broadcast_to(cr[7:8, :], cr.shape)
                bi = jnp.broadcast_to(ci[7:8, :], ci.shape)
                xr = xr + (pr * br - pi * bi)
                xi = xi + (pr * bi + pi * br)
                xr_ref[rows, lanes] = xr
                xi_ref[rows, lanes] = xi
                return xr, xi

            cr, ci = lax.fori_loop(0, R // 8, step, (car_r[:, lanes], car_i[:, lanes]), unroll=2)
            car_r[:, lanes] = cr
            car_i[:, lanes] = ci
        y = _dot_nt(xr_ref[...], cbd_r[...]) - _dot_nt(xi_ref[...], cbd_i[...]) + d_ref[...] * u
        yg, _ = _gelu(y)
        z = _dot(yg, wg_ref[...]) + bg_ref[...]
        y_ref[...] = yg * jax.nn.sigmoid(z)

    full = lambda a: pl.BlockSpec(a.shape, lambda i: (0,) * a.ndim)
    small = [lre, lim, ldt, bre_t, bim_t, cre_w, cim_w, d_row, w_glu, b_glu]
    return _call(
        body, "s5_fwd", (tp // R,),
        [pl.BlockSpec((R, S5_W), lambda i: (i, 0))] + [full(a) for a in small],
        [pl.BlockSpec((R, S5_W), lambda i: (i, 0)), pl.BlockSpec((R, S5_N), lambda i: (i, 0)),
         pl.BlockSpec((R, S5_N), lambda i: (i, 0))],
        [jax.ShapeDtypeStruct((tp, S5_W), F32), jax.ShapeDtypeStruct((tp, S5_N), F32),
         jax.ShapeDtypeStruct((tp, S5_N), F32)],
        [pltpu.VMEM((S5_W, S5_N), MM)] * 4 + [pltpu.VMEM((8, 8, S5_N), F32), pltpu.VMEM((8, S5_N), F32),
                                              pltpu.VMEM((8, S5_N), F32)],
        (u, *small), jobs)


def _s5_bwd(dy_out, u, xr, xi, lre, lim, ldt, bre_t, bim_t, cre_w, cim_w, d_row, w_glu, b_glu, jobs=()):
    tp = u.shape[0]
    R = ROW_BLK
    nb = tp // R

    def body(dyo_ref, u_ref, xr_ref, xi_ref, xpr_ref, xpi_ref,
             lre_ref, lim_ref, ldt_ref, bre_ref, bim_ref, cre_ref, cim_ref, d_ref, wg_ref, bg_ref,
             du_ref, dlre_ref, dlim_ref, dldt_ref, dbre_ref, dbim_ref, dcre_ref, dcim_ref, dd_ref, dwg_ref, dbg_ref,
             bbd_r, bbd_i, cbd_r, cbd_i, tab_ref, car_r, car_i, gr_ref, gi_ref, xer_ref, xei_ref,
             abr, abi, acr, aci, adr, adi):
        i = pl.program_id(0)

        @pl.when(i == 0)
        def _():
            lbr, lbi, bbr, bbi = _s5_disc(lre_ref[...], lim_ref[...], ldt_ref[...], bre_ref[...], bim_ref[...])
            br, bi, cr, ci = _s5_block_diag(bbr, bbi, cre_ref[...], cim_ref[...])
            bbd_r[...] = br.astype(MM)
            bbd_i[...] = bi.astype(MM)
            cbd_r[...] = cr.astype(MM)
            cbd_i[...] = ci.astype(MM)
            _store_tables(tab_ref, _s5_tables(lbr, lbi, True))
            for ref in (car_r, car_i, abr, abi, acr, aci, adr, adi, dd_ref, dwg_ref, dbg_ref):
                ref[...] = jnp.zeros_like(ref)

        u = u_ref[...]
        xrv = xr_ref[...]
        xiv = xi_ref[...]
        y = _dot_nt(xrv, cbd_r[...]) - _dot_nt(xiv, cbd_i[...]) + d_ref[...] * u
        yg, t = _gelu(y)
        z = _dot(yg, wg_ref[...]) + bg_ref[...]
        s = jax.nn.sigmoid(z)
        dout = dyo_ref[...]
        dz = dout * yg * s * (1.0 - s)
        dyg = dout * s + _dot_nt(dz, wg_ref[...])
        dwg_ref[...] += _dot_tn(yg, dz)
        dbg_ref[...] += _colsum(dz)
        dy = dyg * (0.5 * (1.0 + t) + 0.5 * y * (1.0 - t * t) * GELU_C * (1.0 + 3.0 * GELU_A * y * y))
        dd_ref[...] += _colsum(dy * u)
        acr[...] += _dot_tn(dy, xrv)
        aci[...] -= _dot_tn(dy, xiv)
        gr_ref[...] = _dot(dy, cbd_r[...])
        gi_ref[...] = -_dot(dy, cbd_i[...])
        has_prev = (i < nb - 1).astype(F32)
        xer_ref[0:8, :] = xpr_ref[...] * has_prev
        xei_ref[0:8, :] = xpi_ref[...] * has_prev
        xer_ref[8:R + 8, :] = xrv
        xei_ref[8:R + 8, :] = xiv
        row = lax.broadcasted_iota(jnp.int32, (8, S5_LANES), 0)
        for j in range(S5_N // S5_LANES):
            lanes = pl.ds(j * S5_LANES, S5_LANES)
            pr = tab_ref[6, :, lanes]
            pi = tab_ref[7, :, lanes]

            def step(n, carry):
                cr, ci, sar, sai = carry
                g = R // 8 - 1 - n
                r0 = pl.multiple_of(g * 8, 8)
                rows = pl.ds(r0, 8)
                gr, gi = _scan8(gr_ref[rows, lanes], gi_ref[rows, lanes], tab_ref, lanes, True)
                br = jnp.broadcast_to(cr[0:1, :], cr.shape)
                bi = jnp.broadcast_to(ci[0:1, :], ci.shape)
                gr = gr + (pr * br - pi * bi)
                gi = gi + (pr * bi + pi * br)
                gr_ref[rows, lanes] = gr
                gi_ref[rows, lanes] = gi
                last = row == 7
                xpr = pltpu.roll(jnp.where(last, xer_ref[rows, lanes], xer_ref[pl.ds(r0 + 8, 8), lanes]), 1, 0)
                xpi = pltpu.roll(jnp.where(last, xei_ref[rows, lanes], xei_ref[pl.ds(r0 + 8, 8), lanes]), 1, 0)
                return gr, gi, sar + (gr * xpr + gi * xpi), sai + (gi * xpr - gr * xpi)

            cr, ci, sar, sai = lax.fori_loop(
                0, R // 8, step, (car_r[:, lanes], car_i[:, lanes], adr[:, lanes], adi[:, lanes]), unroll=2)
            car_r[:, lanes] = cr
            car_i[:, lanes] = ci
            adr[:, lanes] = sar
            adi[:, lanes] = sai
        grv = gr_ref[...]
        giv = gi_ref[...]
        du_ref[...] = (dy * d_ref[...] + _dot_nt(grv, bbd_r[...]) + _dot_nt(giv, bbd_i[...])).astype(du_ref.dtype)
        abr[...] += _dot_tn(u, grv)
        abi[...] += _dot_tn(u, giv)

        @pl.when(i == nb - 1)
        def _():
            mask = _bd_mask()
            r16 = lax.broadcasted_iota(jnp.int32, (S5_H, S5_W), 1)
            h16 = lax.broadcasted_iota(jnp.int32, (S5_H, S5_W), 0)
            fold_b = jnp.bitwise_and(r16, S5_H - 1) == h16
            c64 = lax.broadcasted_iota(jnp.int32, (S5_N, S5_P), 0)
            p64 = lax.broadcasted_iota(jnp.int32, (S5_N, S5_P), 1)
            fold_c = jnp.bitwise_and(c64, S5_P - 1) == p64
            dbbr = _dot_sel_lhs(fold_b, jnp.where(mask, abr[...], 0.0))
            dbbi = _dot_sel_lhs(fold_b, jnp.where(mask, abi[...], 0.0))
            dcre_ref[...] = _dot_sel_rhs(jnp.where(mask, acr[...], 0.0), fold_c)
            dcim_ref[...] = _dot_sel_rhs(jnp.where(mask, aci[...], 0.0), fold_c)
            dlbr = _colsum(adr[...])
            dlbi = _colsum(adi[...])
            _, vjp = jax.vjp(_s5_disc, lre_ref[...], lim_ref[...], ldt_ref[...], bre_ref[...], bim_ref[...])
            dlre, dlim, dldt, dbre, dbim = vjp((dlbr, dlbi, dbbr, dbbi))
            dlre_ref[...] = dlre
            dlim_ref[...] = dlim
            dbre_ref[...] = dbre
            dbim_ref[...] = dbim
            gsel = jnp.right_shift(lax.broadcasted_iota(jnp.int32, (S5_N, HEAD), 0), 6) == \
                lax.broadcasted_iota(jnp.int32, (S5_N, HEAD), 1)
            dldt_ref[...] = _dot_sel_rhs(dldt, gsel)

    full = lambda a: pl.BlockSpec(a.shape, lambda i: (0,) * a.ndim)
    rev = lambda w: pl.BlockSpec((R, w), lambda i: (nb - 1 - i, 0))
    prev8 = pl.BlockSpec((8, S5_N), lambda i: (jnp.maximum((nb - 1 - i) * (R // 8) - 1, 0), 0))
    small = [lre, lim, ldt, bre_t, bim_t, cre_w, cim_w, d_row, w_glu, b_glu]
    outs = [((tp, S5_W), rev(S5_W))] + [
        (s, pl.BlockSpec(s, lambda i: (0, 0))) for s in
        [(1, S5_N), (1, S5_N), (1, HEAD), (S5_H, S5_N), (S5_H, S5_N), (S5_W, S5_P), (S5_W, S5_P),
         (1, S5_W), (S5_W, S5_W), (1, S5_W)]]
    return _call(
        body, "s5_bwd", (nb,),
        [rev(S5_W), rev(S5_W), rev(S5_N), rev(S5_N), prev8, prev8] + [full(a) for a in small],
        [o[1] for o in outs], [jax.ShapeDtypeStruct(o[0], MM if n == 0 else F32) for n, o in enumerate(outs)],
        [pltpu.VMEM((S5_W, S5_N), MM)] * 4 + [
            pltpu.VMEM((8, 8, S5_N), F32), pltpu.VMEM((8, S5_N), F32), pltpu.VMEM((8, S5_N), F32),
            pltpu.VMEM((R, S5_N), F32), pltpu.VMEM((R, S5_N), F32),
            pltpu.VMEM((R + 8, S5_N), F32), pltpu.VMEM((R + 8, S5_N), F32)] + [pltpu.VMEM((S5_W, S5_N), F32)] * 4 + [
            pltpu.VMEM((8, S5_N), F32), pltpu.VMEM((8, S5_N), F32)],
        (dy_out, u, xr, xi, xr, xi, *small), jobs)


RET_CHUNK = ROW_BLK
LOG_GAMMA = [math.log1p(-2.0 ** (-5 - h)) for h in range(RET_H)]
GAMMA_CHUNK = [math.exp(RET_CHUNK * lg) for lg in LOG_GAMMA]
_DECAY_SCRATCH = [pltpu.VMEM((RET_H, RET_CHUNK, RET_CHUNK), F32), pltpu.VMEM((RET_H, RET_CHUNK, HEAD), F32),
                  pltpu.VMEM((RET_H, RET_CHUNK, HEAD), F32)]


def _fill_decay(dm_ref, ze_ref, xi_ref):
    C = RET_CHUNK
    diff = (lax.broadcasted_iota(jnp.int32, (C, C), 0) - lax.broadcasted_iota(jnp.int32, (C, C), 1)).astype(F32)
    r = lax.broadcasted_iota(jnp.int32, (C, HEAD), 0).astype(F32)
    for h, lg in enumerate(LOG_GAMMA):
        dm_ref[h] = jnp.where(diff >= 0.0, jnp.exp(jnp.maximum(diff, 0.0) * lg), 0.0)
        ze_ref[h] = jnp.exp((C - 1.0 - r) * lg)
        xi_ref[h] = jnp.exp((r + 1.0) * lg)


def _ret_fwd(q, k, v, jobs=()):
    tp = q.shape[0]
    C = RET_CHUNK
    nc = tp // C

    def body(q_ref, k_ref, v_ref, o_ref, st_ref, s_ref, dm_ref, ze_ref, xi_ref):
        @pl.when(pl.program_id(0) == 0)
        def _():
            s_ref[...] = jnp.zeros_like(s_ref)
            _fill_decay(dm_ref, ze_ref, xi_ref)

        for h in range(RET_H):
            sl = slice(h * HEAD, (h + 1) * HEAD)
            qh, kh, vh = q_ref[:, sl], k_ref[:, sl], v_ref[:, sl]
            sh = s_ref[h]
            st_ref[0, sl, :] = sh
            scores = _dot_nt(qh, kh) * dm_ref[h]
            o_ref[:, sl] = _dot(scores, vh) + _dot(qh, sh) * xi_ref[h]
            s_ref[h] = GAMMA_CHUNK[h] * sh + _dot_tn(kh.astype(F32) * ze_ref[h], vh)

    blk = pl.BlockSpec((C, RET_W), lambda c: (c, 0))
    return _call(
        body, "ret_fwd", (nc,), [blk, blk, blk], [blk, pl.BlockSpec((1, RET_W, HEAD), lambda c: (c, 0, 0))],
        [jax.ShapeDtypeStruct((tp, RET_W), F32), jax.ShapeDtypeStruct((nc, RET_W, HEAD), F32)],
        [pltpu.VMEM((RET_H, HEAD, HEAD), F32)] + _DECAY_SCRATCH, (q, k, v), jobs)


def _ret_bwd(q, k, v, do, states, cos2, sin2, jobs=()):
    tp = q.shape[0]
    C = RET_CHUNK
    nc = tp // C

    def body(q_ref, k_ref, v_ref, do_ref, st_ref, cos_ref, sin_ref,
             dq_ref, dk_ref, dv_ref, ds_ref, dm_ref, ze_ref, xi_ref):
        @pl.when(pl.program_id(0) == 0)
        def _():
            ds_ref[...] = jnp.zeros_like(ds_ref)
            _fill_decay(dm_ref, ze_ref, xi_ref)

        cos = cos_ref[...]
        sin = sin_ref[...]
        for h in range(RET_H):
            sl = slice(h * HEAD, (h + 1) * HEAD)
            qh, kh, vh = q_ref[:, sl], k_ref[:, sl], v_ref[:, sl]
            dmh = dm_ref[h]
            sh = st_ref[0, sl, :]
            dsn = ds_ref[h]
            doh = do_ref[:, sl]
            dox = doh * xi_ref[h]
            a = _dot_nt(qh, kh) * dmh
            dqk = _dot_nt(doh, vh) * dmh
            kz = kh.astype(F32) * ze_ref[h]
            dv_ref[:, sl] = (_dot_tn(a, doh) + _dot(kz, dsn)).astype(dv_ref.dtype)
            dqr = _dot(dqk, kh) + _dot_nt(dox, sh)
            dkr = _dot_tn(dqk, qh) + ze_ref[h] * _dot_nt(vh, dsn)
            ds_ref[h] = GAMMA_CHUNK[h] * dsn + _dot_tn(qh, dox)
            dq_ref[:, sl] = (dqr * cos - pltpu.roll(dqr, HEAD // 2, 1) * sin).astype(dq_ref.dtype)
            dk_ref[:, sl] = ((dkr * cos - pltpu.roll(dkr, HEAD // 2, 1) * sin) * (HEAD ** -0.5)).astype(dk_ref.dtype)

    blk = pl.BlockSpec((C, RET_W), lambda c: (nc - 1 - c, 0))
    tab = pl.BlockSpec((C, HEAD), lambda c: (nc - 1 - c, 0))
    return _call(
        body, "ret_bwd", (nc,),
        [blk, blk, blk, blk, pl.BlockSpec((1, RET_W, HEAD), lambda c: (nc - 1 - c, 0, 0)), tab, tab],
        [blk, blk, blk], [jax.ShapeDtypeStruct((tp, RET_W), MM)] * 3,
        [pltpu.VMEM((RET_H, HEAD, HEAD), F32)] + _DECAY_SCRATCH, (q, k, v, do, states, cos2, sin2), jobs)


def _gn_gate(o, gate, gn_g, gn_b):
    xhat, rstd = _ln_fwd(o, GN_EPS)
    on = xhat * gn_g + gn_b
    s = jax.nn.sigmoid(gate)
    return gate * s * on, xhat, rstd, on, s


def _post_up(o, gate, ys5, xhat0, gn_g, gn_b, li_g, li_b, l1_g, l1_b, w_out, w_up, jobs=()):
    tp = o.shape[0]
    R = ROW_BLK

    def body(o_ref, g_ref, ys_ref, xh0_ref, gng, gnb, lig, lib, l1g, l1b, wo_ref, wu_ref,
             ycat_ref, xh1_ref, rstd1_ref, h1b_ref, pre_ref):
        ycat_ref[:, 0:S5_W] = ys_ref[...].astype(ycat_ref.dtype)
        for h in range(RET_H):
            sl = slice(h * HEAD, (h + 1) * HEAD)
            yret = _gn_gate(o_ref[:, sl], g_ref[:, sl], gng[:, sl], gnb[:, sl])[0]
            ycat_ref[:, S5_W + h * HEAD:S5_W + (h + 1) * HEAD] = yret.astype(ycat_ref.dtype)
        mixed = _dot(ycat_ref[...], wo_ref[...])
        h0 = xh0_ref[...] * lig[...] + lib[...]
        xh1, rstd1 = _ln_fwd(ALPHA * h0 + mixed, LN_EPS)
        xh1_ref[...] = xh1
        rstd1_ref[...] = rstd1
        h1b = (xh1 * l1g[...] + l1b[...]).astype(MM)
        h1b_ref[...] = h1b
        for d in range(N_DEV):
            pre_ref[:, d * FF_BLK:(d + 1) * FF_BLK] = jnp.maximum(_dot(h1b, wu_ref[d]), 0.0)

    row = lambda w: pl.BlockSpec((R, w), lambda i: (i, 0))
    full = lambda a: pl.BlockSpec(a.shape, lambda i: (0,) * a.ndim)
    vecs = [gn_g, gn_b, li_g, li_b, l1_g, l1_b]
    outs = [(row(D_MODEL), jax.ShapeDtypeStruct((tp, D_MODEL), MM)), (row(D_MODEL), jax.ShapeDtypeStruct((tp, D_MODEL), F32)),
            (row(1), jax.ShapeDtypeStruct((tp, 1), F32)), (row(D_MODEL), jax.ShapeDtypeStruct((tp, D_MODEL), MM)),
            (row(D_FF), jax.ShapeDtypeStruct((tp, D_FF), F32))]
    return _call(
        body, "post_up", (tp // R,),
        [row(RET_W), row(RET_W), row(S5_W), row(D_MODEL)] + [full(a) for a in vecs] + [_VMEM, _VMEM],
        [o[0] for o in outs], [o[1] for o in outs], [], (o, gate, ys5, xhat0, *vecs, w_out, w_up), jobs)


def _post_down(pre, xhat1, tgt, l1_g, l1_b, l2_g, l2_b, w_down):
    tp = pre.shape[0]
    seq = tgt.shape[0]
    R = ROW_BLK

    def body(pre_ref, xh1_ref, ta, tb, tc, l1g, l1b, l2g, l2b, wd_ref,
             dr2_ref, dffb_ref, loss_ref, dl2g_ref, dl2b_ref, tgt_ref):
        i = pl.program_id(0)

        @pl.when(i == 0)
        def _():
            for ref in (loss_ref, dl2g_ref, dl2b_ref):
                ref[...] = jnp.zeros_like(ref)

        tgt_ref[0:CHUNK, :] = ta[...]
        tgt_ref[CHUNK:2 * CHUNK, :] = tb[...]
        tgt_ref[2 * CHUNK:3 * CHUNK, :] = tc[...]
        ff = jnp.zeros((R, D_MODEL), F32)
        for d in range(N_DEV):
            pre = pre_ref[:, d * FF_BLK:(d + 1) * FF_BLK]
            ff = ff + _dot(pre * pre, wd_ref[d * FF_BLK:(d + 1) * FF_BLK, :])
        h1 = xh1_ref[...] * l1g[...] + l1b[...]
        xh2, rstd2 = _ln_fwd(ALPHA * h1 + ff, LN_EPS)
        h2 = xh2 * l2g[...] + l2b[...]
        valid = (i * R + lax.broadcasted_iota(jnp.int32, (R, 1), 0)) >= CHUNK
        err = jnp.where(valid, h2 - tgt_ref[...], 0.0)
        loss_ref[...] += 0.5 * jnp.sum(err * err) / D_MODEL
        dh2 = err * (1.0 / D_MODEL)
        dl2g_ref[...] += _colsum(dh2 * xh2)
        dl2b_ref[...] += _colsum(dh2)
        dr2 = _ln_bwd(dh2 * l2g[...], xh2, rstd2)
        dr2_ref[...] = dr2
        dffb_ref[...] = dr2.astype(MM)

    row = lambda w: pl.BlockSpec((R, w), lambda i: (i, 0))
    full = lambda a: pl.BlockSpec(a.shape, lambda i: (0,) * a.ndim)
    vecs = [l1_g, l1_b, l2_g, l2_b]
    acc = lambda s: (pl.BlockSpec(s, lambda i: (0, 0)), jax.ShapeDtypeStruct(s, F32))
    outs = [(row(D_MODEL), jax.ShapeDtypeStruct((tp, D_MODEL), F32)), (row(D_MODEL), jax.ShapeDtypeStruct((tp, D_MODEL), MM)),
            acc((8, HEAD)), acc((1, D_MODEL)), acc((1, D_MODEL))]
    return pl.pallas_call(
        body, name="post_down", grid=(tp // R,),
        in_specs=[row(D_FF), row(D_MODEL)] + _shift3(seq // CHUNK) + [full(a) for a in vecs] + [_VMEM],
        out_specs=[o[0] for o in outs], out_shape=[o[1] for o in outs],
        scratch_shapes=[pltpu.VMEM((R, D_MODEL), F32)],
        compiler_params=_params(("arbitrary",)),
    )(pre, xhat1, tgt, tgt, tgt, *vecs, w_down)


def _mlp_bwd(h1b, dffb, pre, w_up, w_down):
    tp = h1b.shape[0]
    R = MLP_ROWS if tp % MLP_ROWS == 0 else ROW_BLK
    nr = tp // R

    def body(h_ref, df_ref, pre_ref, wu_ref, wd_ref, gup_ref, gdn_ref, dh1_ref, aup, adn):
        d = pl.program_id(0)
        r = pl.program_id(1)

        @pl.when(r == 0)
        def _():
            aup[...] = jnp.zeros_like(aup)
            adn[...] = jnp.zeros_like(adn)

        h = h_ref[...]
        df = df_ref[...]
        wu = wu_ref[0]
        wd = wd_ref[0]
        pre = pre_ref[...]
        dpre = (_dot_nt(df, wd) * (2.0 * pre)).astype(MM)

        aup[...] += _dot_tn(h, dpre)
        adn[...] += _dot_tn(pre * pre, df)
        contrib = _dot_nt(dpre, wu)
        rows = pl.ds(pl.multiple_of(r * R, 64), R)

        @pl.when(d == 0)
        def _():
            dh1_ref[rows, :] = contrib

        @pl.when(d > 0)
        def _():
            dh1_ref[rows, :] += contrib

        @pl.when(r == nr - 1)
        def _():
            gup_ref[0] = aup[...].astype(gup_ref.dtype)
            gdn_ref[0] = adn[...].astype(gdn_ref.dtype)

    return pl.pallas_call(
        body, name="mlp_bwd", grid=(N_DEV, nr),
        in_specs=[pl.BlockSpec((R, D_MODEL), lambda d, r: (r, 0)), pl.BlockSpec((R, D_MODEL), lambda d, r: (r, 0)),
                  pl.BlockSpec((R, FF_BLK), lambda d, r: (r, d)),
                  pl.BlockSpec((1, D_MODEL, FF_BLK), lambda d, r: (d, 0, 0)),
                  pl.BlockSpec((1, FF_BLK, D_MODEL), lambda d, r: (d, 0, 0))],
        out_specs=[pl.BlockSpec((1, D_MODEL, FF_BLK), lambda d, r: (d, 0, 0)),
                   pl.BlockSpec((1, FF_BLK, D_MODEL), lambda d, r: (d, 0, 0)), _VMEM],
        out_shape=[jax.ShapeDtypeStruct((N_DEV, D_MODEL, FF_BLK), MM), jax.ShapeDtypeStruct((N_DEV, FF_BLK, D_MODEL), MM),
                   jax.ShapeDtypeStruct((tp, D_MODEL), F32)],
        scratch_shapes=[pltpu.VMEM((D_MODEL, FF_BLK), F32), pltpu.VMEM((FF_BLK, D_MODEL), F32)],
        compiler_params=_params(("arbitrary", "arbitrary")),
    )(h1b, dffb, pre, w_up, w_down.reshape(N_DEV, FF_BLK, D_MODEL))


def _post_bwd(dh1m, dr2, xhat1, rstd1, ycat, o, gate, gn_g, gn_b, l1_g, w_out, jobs=()):
    tp = o.shape[0]
    R = ROW_BLK
    nb = tp // R

    def body(dm_ref, dr2_ref, xh1_ref, rs1_ref, yc_ref, o_ref, g_ref, gng, gnb, l1g, wo_ref,
             do_ref, dg_ref, dys_ref, dh0_ref, gwo_ref, dl1g_ref, dl1b_ref, dgng_ref, dgnb_ref, awo):
        i = pl.program_id(0)

        @pl.when(i == 0)
        def _():
            for ref in (awo, dl1g_ref, dl1b_ref, dgng_ref, dgnb_ref):
                ref[...] = jnp.zeros_like(ref)

        dh1 = dm_ref[...] + ALPHA * dr2_ref[...]
        xh1 = xh1_ref[...]
        dl1g_ref[...] += _colsum(dh1 * xh1)
        dl1b_ref[...] += _colsum(dh1)
        dr1 = _ln_bwd(dh1 * l1g[...], xh1, rs1_ref[...])
        dh0_ref[...] = ALPHA * dr1
        dmix = dr1.astype(MM)
        awo[...] += _dot_tn(yc_ref[...], dmix)
        dyc = _dot_nt(dmix, wo_ref[...])
        dys_ref[...] = dyc[:, 0:S5_W]
        for h in range(RET_H):
            sl = slice(h * HEAD, (h + 1) * HEAD)
            gt = g_ref[:, sl]
            _, xhat, rstd, on, s = _gn_gate(o_ref[:, sl], gt, gng[:, sl], gnb[:, sl])
            dyr = dyc[:, S5_W + h * HEAD:S5_W + (h + 1) * HEAD]
            dg_ref[:, sl] = (dyr * on * (s * (1.0 + gt * (1.0 - s)))).astype(dg_ref.dtype)
            don = dyr * gt * s
            dgng_ref[:, sl] += _colsum(don * xhat)
            dgnb_ref[:, sl] += _colsum(don)
            do_ref[:, sl] = _ln_bwd(don * gng[:, sl], xhat, rstd)

        @pl.when(i == nb - 1)
        def _():
            gwo_ref[...] = awo[...].astype(gwo_ref.dtype)

    row = lambda w: pl.BlockSpec((R, w), lambda i: (i, 0))
    full = lambda a: pl.BlockSpec(a.shape, lambda i: (0,) * a.ndim)
    acc = lambda s, dt=F32: (pl.BlockSpec(s, lambda i: (0, 0)), jax.ShapeDtypeStruct(s, dt))
    outs = [(row(RET_W), jax.ShapeDtypeStruct((tp, RET_W), F32)), (row(RET_W), jax.ShapeDtypeStruct((tp, RET_W), MM)),
            (row(S5_W), jax.ShapeDtypeStruct((tp, S5_W), F32)), (row(D_MODEL), jax.ShapeDtypeStruct((tp, D_MODEL), F32)),
            acc((D_MODEL, D_MODEL), MM), acc((1, D_MODEL)), acc((1, D_MODEL)), acc((1, RET_W)), acc((1, RET_W))]
    return _call(
        body, "post_bwd", (nb,),
        [row(D_MODEL), row(D_MODEL), row(D_MODEL), row(1), row(D_MODEL), row(RET_W), row(RET_W),
         full(gn_g), full(gn_b), full(l1_g), _VMEM],
        [o[0] for o in outs], [o[1] for o in outs],
        [pltpu.VMEM((D_MODEL, D_MODEL), F32)],
        (dh1m, dr2, xhat1, rstd1, ycat, o, gate, gn_g, gn_b, l1_g, w_out), jobs)


_PROJ_SEGS = [(0, S5_W)] + [(S5_W + n * RET_W, S5_W + (n + 1) * RET_W) for n in range(4)]


def _in_w_grad(du, dq, dk, dv, dg, xhat0, li_g, li_b):
    tp = du.shape[0]
    R = PROJ_ROWS if tp % PROJ_ROWS == 0 else ROW_BLK
    nb = tp // R

    def body(du_ref, dq_ref, dk_ref, dv_ref, dg_ref, xh_ref, lig, lib, gw_ref, aw):
        i = pl.program_id(0)

        @pl.when(i == 0)
        def _():
            aw[...] = jnp.zeros_like(aw)

        valid = (i * R + lax.broadcasted_iota(jnp.int32, (R, 1), 0)) >= PAD
        hb = (xh_ref[...] * lig[...] + lib[...]).astype(MM)
        for (lo, hi), ref in zip(_PROJ_SEGS, (du_ref, dq_ref, dk_ref, dv_ref, dg_ref)):
            aw[lo:hi, :] += _dot_tn(jnp.where(valid, ref[...], 0.0).astype(MM), hb)

        @pl.when(i == nb - 1)
        def _():
            gw_ref[...] = aw[...].astype(gw_ref.dtype)

    row = lambda w: pl.BlockSpec((R, w), lambda i: (i, 0))
    full = lambda a: pl.BlockSpec(a.shape, lambda i: (0,) * a.ndim)
    (gw,), _ = _call(
        body, "in_w_grad", (nb,),
        [row(S5_W), row(RET_W), row(RET_W), row(RET_W), row(RET_W), row(D_MODEL), full(li_g), full(li_b)],
        [pl.BlockSpec((PROJ_W, D_MODEL), lambda i: (0, 0))], [jax.ShapeDtypeStruct((PROJ_W, D_MODEL), MM)],
        [pltpu.VMEM((PROJ_W, D_MODEL), F32)], (du, dq, dk, dv, dg, xhat0, li_g, li_b))
    return gw


def _in_bwd(du, dq, dk, dv, dg, dh0r, xhat0, rstd0, li_g, w_int, after):
    tp = du.shape[0]
    R = PROJ_ROWS if tp % PROJ_ROWS == 0 else ROW_BLK
    nb = tp // R
    segs = _PROJ_SEGS

    def body(du_ref, dq_ref, dk_ref, dv_ref, dg_ref, dh0r_ref, xh_ref, rs_ref, lig, w_ref, after_ref,
             gx_ref, dmeta_ref, dlg_ref, dlb_ref, stage, out_sems):
        i = pl.program_id(0)
        slot = i % 2

        def to_gx(step_slot, first):
            if first:
                return pltpu.make_async_copy(stage.at[0, CHUNK:R, :], gx_ref.at[0:R - CHUNK, :], out_sems.at[0])
            return pltpu.make_async_copy(stage.at[step_slot], gx_ref.at[pl.ds(i * R - CHUNK, R), :], out_sems.at[step_slot])

        @pl.when(i == 0)
        def _():
            for ref in (dlg_ref, dlb_ref):
                ref[...] = jnp.zeros_like(ref)

        @pl.when(i >= 3)
        def _():
            to_gx(slot, False).wait()

        valid = (i * R + lax.broadcasted_iota(jnp.int32, (R, 1), 0)) >= PAD
        xh = xh_ref[...]
        dh0 = dh0r_ref[...]
        for (lo, hi), ref in zip(segs, (du_ref, dq_ref, dk_ref, dv_ref, dg_ref)):
            dh0 = dh0 + _dot(jnp.where(valid, ref[...], 0.0).astype(MM), w_ref[lo:hi, :])
        dlg_ref[...] += _colsum(dh0 * xh)
        dlb_ref[...] += _colsum(dh0)
        draw = _ln_bwd(dh0 * lig[...], xh, rs_ref[...])
        stage[slot] = draw

        @pl.when(i == 0)
        def _():
            dmeta_ref[...] = draw[PAD:CHUNK, :]
            first = to_gx(0, True)
            first.start()
            first.wait()

        @pl.when(i > 0)
        def _():
            to_gx(slot, False).start()

        @pl.when(i == nb - 1)
        def _():
            for back in (1, 0):
                if nb - 1 - back >= 1:
                    to_gx((nb - 1 - back) % 2, False).wait()

    row = lambda w: pl.BlockSpec((R, w), lambda i: (i, 0))
    full = lambda a: pl.BlockSpec(a.shape, lambda i: (0,) * a.ndim)
    acc = lambda s, dt=F32: (pl.BlockSpec(s, lambda i: (0, 0)), jax.ShapeDtypeStruct(s, dt))
    outs = [(_ANY, jax.ShapeDtypeStruct((tp - CHUNK, D_MODEL), F32)), acc((N_META, D_MODEL)),
            acc((1, D_MODEL)), acc((1, D_MODEL))]
    return _call(
        body, "in_bwd", (nb,),
        [row(S5_W), row(RET_W), row(RET_W), row(RET_W), row(RET_W), row(D_MODEL), row(D_MODEL), row(1),
         full(li_g), _VMEM, full(after)],
        [o[0] for o in outs], [o[1] for o in outs],
        [pltpu.VMEM((2, R, D_MODEL), F32), pltpu.SemaphoreType.DMA((2,))],
        (du, dq, dk, dv, dg, dh0r, xhat0, rstd0, li_g, w_int, after))[0]


def _place():
    return lax.axis_index("x"), lax.axis_index("y"), lax.axis_index("c")


def _dma_sems(n):
    return pltpu.SemaphoreType.DMA((n,))


def _job_gather(shard):
    def parts(ins, outs, sems):
        (src,), (out,), (send_sems, recv_sems, local_sem) = ins, outs, sems
        x, y, c = _place()
        north = c == 1
        me, sib = (x, y, c), (x, y, 1 - c)
        xn, yn, dg = (1 - x, y, c), (x, 1 - y, c), (1 - x, 1 - y, c)
        relay_from = (jnp.where(north, 1 - x, x), jnp.where(north, y, 1 - y), c)
        relay_to = (jnp.where(north, x, 1 - x), jnp.where(north, 1 - y, y), c)

        def slot(dev):
            return out.at[4 * dev[0] + 2 * dev[1] + dev[2]]

        def copy(k, block, to, from_input=False):
            return pltpu.make_async_remote_copy(
                src_ref=src if from_input else slot(block), dst_ref=slot(block),
                send_sem=send_sems.at[k], recv_sem=recv_sems.at[k], device_id=to, device_id_type=_MESH)

        mine = lambda: pltpu.make_async_copy(src, slot(me), local_sem.at[0])
        first = lambda: [copy(0, me, sib, True), copy(1, me, xn, True), copy(2, me, yn, True)]
        relayed = lambda: [copy(3, relay_from, relay_to), copy(4, xn, sib), copy(5, yn, sib)]
        return me, sib, xn, yn, dg, copy, mine, first, relayed

    def start(ins, outs, sems):
        mine, first = parts(ins, outs, sems)[6:8]
        mine().start()
        for cp in first():
            cp.start()

    def relay(ins, outs, sems):
        me, sib, xn, yn, dg, copy, mine, first, relayed = parts(ins, outs, sems)
        copy(1, xn, me).wait_recv()
        copy(2, yn, me).wait_recv()
        for cp in relayed():
            cp.start()

    def finish(ins, outs, sems):
        me, sib, xn, yn, dg, copy, mine, first, relayed = parts(ins, outs, sems)
        other = 1 - me[2]
        copy(3, dg, me).wait_recv()
        last = copy(6, dg, sib)
        last.start()
        copy(0, sib, me).wait_recv()
        for k, chip in ((4, xn), (5, yn), (6, dg)):
            copy(k, (chip[0], chip[1], other), me).wait_recv()
        for cp in first() + relayed() + [last]:
            cp.wait_send()
        mine().wait()

    return dict(ins=[shard], outs=[jax.ShapeDtypeStruct((N_DEV,) + shard.shape, shard.dtype)],
                sems=[_dma_sems(7), _dma_sems(7), _dma_sems(1)], start=start, middle=relay, finish=finish)


def _job_pair(g):
    def copies(ins, outs, sems):
        x, y, c = _place()
        return [pltpu.make_async_remote_copy(
            src_ref=ins[0].at[2 * j + (1 - c)], dst_ref=outs[0].at[j], send_sem=sems[0].at[j], recv_sem=sems[1].at[j],
            device_id=(x, y, 1 - c), device_id_type=_MESH) for j in range(4)]

    def start(ins, outs, sems):
        for cp in copies(ins, outs, sems):
            cp.start()

    def finish(ins, outs, sems):
        for cp in copies(ins, outs, sems):
            cp.wait()

    return dict(ins=[g], outs=[jax.ShapeDtypeStruct((4,) + g.shape[1:], g.dtype)], sems=[_dma_sems(4), _dma_sems(4)],
                start=start, finish=finish)


def _job_chips(p):
    def copies(ins, outs, sems):
        x, y, c = _place()
        chips = [(1 - x, y), (x, 1 - y), (1 - x, 1 - y)]
        return [pltpu.make_async_remote_copy(
            src_ref=ins[0].at[2 * chip[0] + chip[1]], dst_ref=outs[0].at[k], send_sem=sems[0].at[k],
            recv_sem=sems[1].at[k], device_id=(*chip, c), device_id_type=_MESH) for k, chip in enumerate(chips)]

    def start(ins, outs, sems):
        for cp in copies(ins, outs, sems):
            cp.start()

    def finish(ins, outs, sems):
        for cp in copies(ins, outs, sems):
            cp.wait()

    return dict(ins=[p], outs=[jax.ShapeDtypeStruct((3,) + p.shape[1:], p.dtype)], sems=[_dma_sems(3), _dma_sems(3)],
                start=start, finish=finish)


_HBM = pl.BlockSpec(memory_space=pltpu.HBM)
_SEM = pl.BlockSpec(memory_space=pltpu.SEMAPHORE)
_ORDERED = pltpu.CompilerParams(has_side_effects=pltpu.SideEffectType.DATAFLOW_SIDE_EFFECTING)


def _chip_copies(p_ref, land_ref, sems):
    x, y, c = _place()
    chips = [(1 - x, y), (x, 1 - y), (1 - x, 1 - y)]
    return [pltpu.make_async_remote_copy(
        src_ref=p_ref.at[2 * chip[0] + chip[1]], dst_ref=land_ref.at[k], send_sem=sems[k], recv_sem=sems[3 + k],
        device_id=(*chip, c), device_id_type=_MESH) for k, chip in enumerate(chips)]


def _chips_start(p, name):
    def body(p_ref, land_ref, *rest):
        for cp in _chip_copies(p_ref, land_ref, rest[:6]):
            cp.start()
        rest[8][...] = jnp.zeros_like(rest[8])

    land = (3,) + p.shape[1:]
    outs = pl.pallas_call(
        body, name=name,
        out_shape=(*[pltpu.SemaphoreType.DMA(())] * 6, pltpu.HBM(p.shape, p.dtype), pltpu.HBM(land, p.dtype),
                   jax.ShapeDtypeStruct((8, LANE), F32)),
        in_specs=(_HBM, _HBM), out_specs=(*[_SEM] * 6, _HBM, _HBM, _VMEM), input_output_aliases={0: 6, 1: 7},
        compiler_params=_ORDERED,
    )(pltpu.with_memory_space_constraint(p, pltpu.HBM), pltpu.with_memory_space_constraint(lax.empty(land, p.dtype), pltpu.HBM))
    return list(outs[:6]), outs[6], outs[7], outs[8]


def _chips_wait(sems, p_thru, land_thru, after, name):
    def body(p_ref, land_ref, *rest):
        for cp in _chip_copies(p_ref, land_ref, rest[:6]):
            cp.wait_send()
            cp.wait_recv()

    return pl.pallas_call(
        body, name=name, out_shape=(pltpu.HBM(p_thru.shape, p_thru.dtype), pltpu.HBM(land_thru.shape, land_thru.dtype)),
        in_specs=(_HBM, _HBM, *[_SEM] * 6, _ANY), out_specs=(_HBM, _HBM), input_output_aliases={0: 0, 1: 1},
        compiler_params=_ORDERED,
    )(p_thru, land_thru, *sems, after)


def _split_job_refs(jobs, ins, outs, sems):
    res, a, b, c = [], 0, 0, 0
    for job in jobs:
        na, nb, nc = len(job["ins"]), len(job["outs"]), len(job["sems"])
        res.append((ins[a:a + na], outs[b:b + nb], sems[c:c + nc]))
        a, b, c = a + na, b + nb, c + nc
    return res


def _call(body, name, grid, in_specs, out_specs, out_shape, scratch, args, jobs=(), prefetch=None, early=0):
    jobs = list(jobs)
    n_in, n_out, n_scr = len(in_specs), len(out_specs), len(scratch)
    j_in = [a for job in jobs for a in job["ins"]]
    j_out = [o for job in jobs for o in job["outs"]]
    j_scr = [s for job in jobs for s in job["sems"]]
    nsteps = grid[0]
    n_pre = 0 if prefetch is None else 1

    def wrapped(*refs):
        pre, refs = refs[:n_pre], refs[n_pre:]
        ins, jins = refs[:n_in], refs[n_in:n_in + len(j_in)]
        refs = refs[n_in + len(j_in):]
        outs, jouts = refs[:n_out], refs[n_out:n_out + len(j_out)]
        refs = refs[n_out + len(j_out):]
        scr, jscr = refs[:n_scr], refs[n_scr:]
        per_job = _split_job_refs(jobs, jins, jouts, jscr)

        def middle():
            for job, r in zip(jobs, per_job):
                if "middle" in job:
                    job["middle"](*r)

        @pl.when(pl.program_id(0) == 0)
        def _():
            for job, r in zip(jobs, per_job):
                job["start"](*r)

        if nsteps >= 3:
            pl.when(pl.program_id(0) == nsteps // 2)(middle)

        if early:
            @pl.when(pl.program_id(0) == nsteps - 1)
            def _():
                for job, r in zip(jobs[:early], per_job[:early]):
                    job["finish"](*r)

        body(*pre, *ins, *outs, *scr, *[o for r in per_job[:early] for o in r[1]])

        @pl.when(pl.program_id(0) == nsteps - 1)
        def _():
            if nsteps < 3:
                middle()
            for job, r in zip(jobs[early:], per_job[early:]):
                job["finish"](*r)

    specs = dict(in_specs=list(in_specs) + [_ANY] * len(j_in), out_specs=list(out_specs) + [_ANY] * len(j_out),
                 scratch_shapes=list(scratch) + j_scr)
    if n_pre:
        specs = dict(grid_spec=pltpu.PrefetchScalarGridSpec(num_scalar_prefetch=1, grid=grid, **specs))
    else:
        specs["grid"] = grid
    res = pl.pallas_call(
        wrapped if jobs else body, name=name, out_shape=list(out_shape) + j_out,
        compiler_params=_params(("arbitrary",) * len(grid)), **specs,
    )(*([prefetch] if n_pre else []), *args, *j_in)
    return list(res[:n_out]), list(res[n_out:])


def _exchange(jobs, name):
    j_in = [a for job in jobs for a in job["ins"]]
    j_out = [o for job in jobs for o in job["outs"]]
    j_scr = [s for job in jobs for s in job["sems"]]

    def body(*refs):
        per_job = _split_job_refs(jobs, refs[:len(j_in)], refs[len(j_in):len(j_in) + len(j_out)],
                                  refs[len(j_in) + len(j_out):])
        for phase in ("start", "middle", "finish"):
            for job, r in zip(jobs, per_job):
                if phase in job:
                    job[phase](*r)

    return pl.pallas_call(body, name=name, out_shape=j_out, in_specs=[_ANY] * len(j_in), out_specs=[_ANY] * len(j_out),
                          scratch_shapes=j_scr)(*j_in)


def _pair_sum(gs, r1s, c_arr, name):
    n = len(gs)

    def body(c_ref, *refs):
        for a in range(n):
            refs[2 * n + a][...] = (refs[a][...].astype(F32) + refs[n + a][...].astype(F32)).astype(refs[2 * n + a].dtype)

    def blk(g, own):
        s = g.shape[1:]
        if own:
            return pl.BlockSpec((1,) + s, lambda j, c_ref: (2 * j + c_ref[0],) + (0,) * len(s))
        return pl.BlockSpec((1,) + s, lambda j, c_ref: (j,) + (0,) * len(s))

    return pl.pallas_call(
        body, name=name,
        grid_spec=pltpu.PrefetchScalarGridSpec(
            num_scalar_prefetch=1, grid=(4,),
            in_specs=[blk(g, True) for g in gs] + [blk(g, False) for g in gs],
            out_specs=[blk(g, False) for g in gs]),
        out_shape=[jax.ShapeDtypeStruct((4,) + g.shape[1:], g.dtype) for g in gs],
        compiler_params=_params(("arbitrary",)),
    )(c_arr, *gs, *r1s)


def _adamw_math(w, g, m, v):
    m = ADAM_B1 * m + (1.0 - ADAM_B1) * g
    v = ADAM_B2 * v + (1.0 - ADAM_B2) * (g * g)
    m_hat = m / (1.0 - ADAM_B1 ** ADAM_STEP)
    v_hat = v / (1.0 - ADAM_B2 ** ADAM_STEP)
    return -ADAM_LR * (m_hat / (jnp.sqrt(v_hat) + ADAM_EPS) + ADAM_WD * w), m, v


def _view(name, a):
    return jnp.swapaxes(a, -1, -2) if name in ("w_in", "s5_b_re", "s5_b_im") else a


def _adamw_shards(items, name, steps, chip, jobs=()):
    n = len(items)

    def body(chip_ref, *refs):
        for a in range(n):
            p_ref, r_ref, w_ref, m_ref, v_ref = refs[5 * a:5 * a + 5]
            g = ((p_ref[0].astype(F32) + r_ref[0].astype(F32)) + r_ref[1].astype(F32)) + r_ref[2].astype(F32)
            outs = refs[5 * n + 4 * a:5 * n + 4 * a + 4]
            outs[0][...] = g
            outs[1][...], outs[2][...], outs[3][...] = _adamw_math(w_ref[...], g, m_ref[...], v_ref[...])

    in_specs, out_specs, out_shape, flat = [], [], [], []
    for p, r, w, m, v in items:
        rows, cols = w.shape
        rb = rows // steps
        in_specs += [pl.BlockSpec((1, rb, cols), lambda i, c: (c[0], i, 0)), pl.BlockSpec((3, rb, cols), lambda i, c: (0, i, 0))]
        wblk = pl.BlockSpec((rb, cols), lambda i, c: (i, 0))
        in_specs += [wblk] * 3
        out_specs += [wblk] * 4
        out_shape += [jax.ShapeDtypeStruct(w.shape, F32)] * 4
        flat += [p, r, w, m, v]
    return _call(body, name, (steps,), in_specs, out_specs, out_shape, [], flat, jobs, prefetch=chip)


def _sum_devices(gathered, name):
    def body(gs_ref, g_ref):
        g = gs_ref[0]
        for s in range(1, N_DEV):
            g = g + gs_ref[s]
        g_ref[...] = g

    return pl.pallas_call(body, name=name, out_shape=jax.ShapeDtypeStruct(gathered.shape[1:], F32),
                          in_specs=[_VMEM], out_specs=_VMEM, compiler_params=_params())(gathered)


def _adamw_native(items, name):
    n = len(items)

    def body(*refs):
        for a in range(n):
            g, w, m, v = (refs[4 * a + t][...] for t in range(4))
            refs[4 * n + 3 * a][...], refs[4 * n + 3 * a + 1][...], refs[4 * n + 3 * a + 2][...] = _adamw_math(w, g, m, v)

    return pl.pallas_call(
        body, name=name, out_shape=[jax.ShapeDtypeStruct(it[1].shape, F32) for it in items for _ in range(3)],
        in_specs=[_VMEM] * (4 * n), out_specs=[_VMEM] * (3 * n), compiler_params=_params(),
    )(*[t for it in items for t in it])


SMALL = ["ln_in_g", "ln_in_b", "s5_lambda_re", "s5_lambda_im", "s5_log_dt", "s5_b_re", "s5_b_im", "s5_c_re", "s5_c_im",
         "s5_d", "s5_b_glu", "ret_gn_g", "ret_gn_b", "ln1_g", "ln1_b", "ln2_g", "ln2_b"]
LATE = ["ln_in_g", "ln_in_b", "meta_tokens"]
EARLY = [n for n in SMALL if n not in LATE] + ["s5_w_glu", "loss"]
LANE = 128


def _pack(arrs):
    parts = []
    for a in arrs:
        f = a.reshape(-1)
        parts.append(jnp.pad(f, (0, (-f.shape[0]) % LANE)))
    flat = jnp.concatenate(parts)
    rows = -(-flat.shape[0] // LANE)
    flat = jnp.pad(flat, (0, (-rows % 8) * LANE + rows * LANE - flat.shape[0]))
    return flat.reshape(-1, LANE)


def _unpack(packed, shapes):
    flat = packed.reshape(-1)
    out, off = [], 0
    for s in shapes:
        n = math.prod(s)
        out.append(flat[off:off + n].reshape(s))
        off += n + (-n) % LANE
    return out


def _rope_tables(tp):
    inv_freq = 1.0 / (ROPE_BASE ** (jnp.arange(0, HEAD, 2, dtype=F32) / HEAD))
    blk = (jnp.arange(tp // ROW_BLK, dtype=F32) * ROW_BLK)[:, None, None] * inv_freq
    off = (jnp.arange(ROW_BLK, dtype=F32) - float(PAD))[None, :, None] * inv_freq
    cos = (jnp.cos(blk) * jnp.cos(off) - jnp.sin(blk) * jnp.sin(off)).reshape(tp, HEAD // 2)
    sin = (jnp.sin(blk) * jnp.cos(off) + jnp.cos(blk) * jnp.sin(off)).reshape(tp, HEAD // 2)
    return jnp.concatenate([cos, cos], axis=1), jnp.concatenate([-sin, sin], axis=1)


def _local_step(x2d, tgt, meta, w_int, w_out, w_up, w_down, w_glu, sp, distributed):
    tp = x2d.shape[0] + CHUNK
    row = lambda a: a.reshape(1, -1)
    cos2, sin2 = _rope_tables(tp)
    li_g, li_b = row(sp["ln_in_g"]), row(sp["ln_in_b"])
    l1_g, l1_b, l2_g, l2_b = row(sp["ln1_g"]), row(sp["ln1_b"]), row(sp["ln2_g"]), row(sp["ln2_b"])
    gn_g, gn_b = row(sp["ret_gn_g"]), row(sp["ret_gn_b"])
    lre, lim = row(sp["s5_lambda_re"]), row(sp["s5_lambda_im"])
    ldt = row(jnp.repeat(sp["s5_log_dt"].reshape(-1), S5_P))
    to_t = lambda b: b.reshape(S5_G, S5_P, S5_H).transpose(2, 0, 1).reshape(S5_H, S5_N)
    bre_t, bim_t = to_t(sp["s5_b_re"]), to_t(sp["s5_b_im"])
    to_w = lambda c: jnp.tile(c.reshape(S5_W, S5_P), (1, 2))
    cre_w, cim_w = to_w(sp["s5_c_re"]), to_w(sp["s5_c_im"])

    jobs = (lambda *j: list(j)) if distributed else (lambda *j: [])
    c_arr = jnp.reshape(lax.axis_index("c"), (1,)).astype(jnp.int32) if distributed else None
    (xhat0, rstd0), bg = _ln_in(x2d, meta, jobs(*([_job_gather(w_int), _job_gather(w_glu)] if distributed else [])),
                                gather_meta=distributed)
    if distributed:
        w_int, w_glu = bg[1].reshape(PROJ_W, D_MODEL), bg[2].reshape(S5_W, S5_W)
    s5_small = (lre, lim, ldt, bre_t, bim_t, cre_w, cim_w, row(sp["s5_d"]), w_glu, row(sp["s5_b_glu"]))
    (u, q, k, v, gate), bg = _in_proj(xhat0, li_g, li_b, w_int, cos2, sin2,
                                      jobs(_job_gather(w_out) if distributed else None))
    if distributed:
        w_out = bg[0].reshape(D_MODEL, D_MODEL)
    (ys5, xr, xi), bg = _s5_fwd(u, *s5_small, jobs=jobs(_job_gather(w_up) if distributed else None))
    if distributed:
        w_up = bg[0]
    (o, states), _ = _ret_fwd(q, k, v)
    (ycat, xhat1, rstd1, h1b, pre), bg = _post_up(o, gate, ys5, xhat0, gn_g, gn_b, li_g, li_b, l1_g, l1_b, w_out, w_up,
                                                  jobs(_job_gather(w_down) if distributed else None))
    if distributed:
        w_down = bg[0].reshape(D_FF, D_MODEL)
    dr2, dffb, loss8, dl2g, dl2b = _post_down(pre, xhat1, tgt, l1_g, l1_b, l2_g, l2_b, w_down)
    g_up, g_down, dh1m = _mlp_bwd(h1b, dffb, pre, w_up, w_down)
    (do, dgate, dys5, dh0r, g_out, dl1g, dl1b, dgng, dgnb), bg = _post_bwd(
        dh1m, dr2, xhat1, rstd1, ycat, o, gate, gn_g, gn_b, l1_g, w_out,
        jobs(*([_job_pair(g_up), _job_pair(g_down)] if distributed else [])))
    g_out = g_out.reshape(N_DEV, D_MODEL // N_DEV, D_MODEL)
    if distributed:
        p_up, p_down = _pair_sum([g_up, g_down], bg, c_arr, "pair_sum_mlp")
    (du, dlre, dlim, dldt, dbre_t, dbim_t, dcre, dcim, dd, dwglu, dbglu), bg = _s5_bwd(
        dys5, u, xr, xi, *s5_small,
        jobs=jobs(*([_job_chips(p_up), _job_chips(p_down), _job_pair(g_out)] if distributed else [])))
    if distributed:
        r_up, r_down = bg[0], bg[1]
        (p_out,) = _pair_sum([g_out], bg[2:], c_arr, "pair_sum_out")
    from_t = lambda t: t.reshape(S5_H, S5_G, S5_P).transpose(1, 0, 2)
    small = {
        "s5_lambda_re": dlre, "s5_lambda_im": dlim, "s5_log_dt": dldt[:, :S5_G],
        "s5_b_re": from_t(dbre_t), "s5_b_im": from_t(dbim_t), "s5_c_re": dcre, "s5_c_im": dcim, "s5_d": dd,
        "s5_b_glu": dbglu, "ret_gn_g": dgng, "ret_gn_b": dgnb, "ln1_g": dl1g, "ln1_b": dl1b, "ln2_g": dl2g, "ln2_b": dl2b,
        "s5_w_glu": dwglu, "loss": loss8[0:1, 0:1]}
    early_pack = _pack([small[n] for n in EARLY])
    (dq, dk, dv), bg = _ret_bwd(q, k, v, do, states, cos2, sin2,
                                jobs(*([_job_chips(p_out), _job_gather(early_pack)] if distributed else [])))
    g_int = _in_w_grad(du, dq, dk, dv, dgate, xhat0, li_g, li_b).reshape(N_DEV, PROJ_W // N_DEV, D_MODEL)
    after = jnp.zeros((8, LANE), F32)
    if distributed:
        (r1_in,) = _exchange([_job_pair(g_int)], "exchange_pair_in")
        (p_in,) = _pair_sum([g_int], [r1_in], c_arr, "pair_sum_in")
        sems, p_thru, land_thru, after = _chips_start(p_in, "chips_in_start")
    grad_x, dmeta, dlig, dlib = _in_bwd(du, dq, dk, dv, dgate, dh0r, xhat0, rstd0, li_g, w_int, after)
    small.update(ln_in_g=dlig, ln_in_b=dlib, meta_tokens=dmeta)
    if distributed:
        p_in, r_in = _chips_wait(sems, p_thru, land_thru, dlig, "chips_in_wait")
        big = dict(chip_sums=[p_in, p_out, p_up, p_down], received=[r_in, bg[0], r_up, r_down], early=bg[1])
    else:
        big = dict(partials=[g_int, g_out, g_up, g_down])
    return grad_x, big, small


def kernel(x, meta_tokens, ln_in_g, ln_in_b, w_in, s5_lambda_re, s5_lambda_im, s5_log_dt, s5_b_re, s5_b_im, s5_c_re, s5_c_im, s5_d, s5_w_glu, s5_b_glu, ret_gn_g, ret_gn_b, w_out, ln1_g, ln1_b, w_up, w_down, ln2_g, ln2_b, loss_target, m_meta_tokens, m_ln_in_g, m_ln_in_b, m_w_in, m_s5_lambda_re, m_s5_lambda_im, m_s5_log_dt, m_s5_b_re, m_s5_b_im, m_s5_c_re, m_s5_c_im, m_s5_d, m_s5_w_glu, m_s5_b_glu, m_ret_gn_g, m_ret_gn_b, m_w_out, m_ln1_g, m_ln1_b, m_w_up, m_w_down, m_ln2_g, m_ln2_b, v_meta_tokens, v_ln_in_g, v_ln_in_b, v_w_in, v_s5_lambda_re, v_s5_lambda_im, v_s5_log_dt, v_s5_b_re, v_s5_b_im, v_s5_c_re, v_s5_c_im, v_s5_d, v_s5_w_glu, v_s5_b_glu, v_ret_gn_g, v_ret_gn_b, v_w_out, v_ln1_g, v_ln1_b, v_w_up, v_w_down, v_ln2_g, v_ln2_b):
    args = dict(locals())
    names = ["meta_tokens", "ln_in_g", "ln_in_b", "w_in", "s5_lambda_re", "s5_lambda_im", "s5_log_dt", "s5_b_re", "s5_b_im",
             "s5_c_re", "s5_c_im", "s5_d", "s5_w_glu", "s5_b_glu", "ret_gn_g", "ret_gn_b", "w_out", "ln1_g", "ln1_b",
             "w_up", "w_down", "ln2_g", "ln2_b"]
    ax, ay, ac = _place()
    me = 4 * ax + 2 * ay + ac

    sp = {n: args[n] for n in SMALL}
    grad_x, big, small = _local_step(x[0], loss_target[0], meta_tokens, w_in[0].T.astype(MM), w_out[0].astype(MM),
                                   w_up[0].astype(MM), w_down[0].astype(MM), s5_w_glu[0].astype(MM), sp, True)

    j_arr = jnp.reshape(2 * ax + ay, (1,)).astype(jnp.int32)
    two_d = lambda a: a.reshape(a.shape[-2:])
    item = lambda n, p, r: (p, r, *(two_d(_view(n, a)) for a in (args[n], args["m_" + n], args["v_" + n])))
    late_pack = _pack([small[n] for n in LATE])
    mlp = ("w_out", "w_up", "w_down")
    res, (late_all,) = _adamw_shards(
        [item(n, p, r) for n, p, r in zip(mlp, big["chip_sums"][1:], big["received"][1:])], "adamw_mlp", 8, j_arr,
        [_job_gather(late_pack)])
    res_in, _ = _adamw_shards([item("w_in", big["chip_sums"][0], big["received"][0])], "adamw_in", 2, j_arr)
    upd = {"w_in": res_in}
    for idx, n in enumerate(mlp):
        upd[n] = res[4 * idx:4 * idx + 4]
    shard_grads = {n: upd[n][0] for n in upd}

    early_shapes = [_view(n, args[n]).shape for n in EARLY[:-2]] + [(S5_W, S5_W), (1,)]
    late_shapes = [args["ln_in_g"].shape, args["ln_in_b"].shape, (N_META, D_MODEL)]
    g_small = dict(zip(EARLY, _unpack(_sum_devices(big["early"], "sum_small_early"), early_shapes)))
    g_small.update(zip(LATE, _unpack(_sum_devices(late_all, "sum_small_late"), late_shapes)))
    loss = g_small["loss"].reshape(())

    shard_grads["meta_tokens"] = lax.dynamic_slice(g_small["meta_tokens"], (0, me * (D_MODEL // N_DEV)),
                                                   (N_META, D_MODEL // N_DEV))
    shard_grads["s5_w_glu"] = lax.dynamic_slice(g_small["s5_w_glu"], (me * (S5_W // N_DEV), 0),
                                                (S5_W // N_DEV, S5_W))[None]
    natives = SMALL + ["meta_tokens", "s5_w_glu"]
    res2 = _adamw_native([(shard_grads[n] if n in shard_grads else g_small[n], *(_view(n, args[p + n]) for p in ("", "m_", "v_")))
                          for n in natives], "adamw_small")
    for idx, n in enumerate(natives):
        upd[n] = [shard_grads[n] if n in shard_grads else g_small[n]] + list(res2[3 * idx:3 * idx + 3])

    grads, deltas, new_m, new_v = ([_view(n, upd[n][t]).reshape(args[n].shape) for n in names] for t in range(4))
    return (loss, grad_x[None], *grads, *deltas, *new_m, *new_v)
```

```python
import math

import jax
import jax.numpy as jnp
from jax import lax
from jax.experimental import pallas as pl
from jax.experimental.pallas import tpu as pltpu

F32 = jnp.float32
MM = jnp.bfloat16

D_MODEL = 1024
N_META = 16
CHUNK = 128
PAD = CHUNK - N_META
S5_W, S5_G, S5_H, S5_P = 256, 16, 16, 64
S5_N = S5_G * S5_P
RET_W, RET_H, HEAD = 768, 6, 128
D_FF = 4096
PROJ_W = S5_W + 4 * RET_W
N_DEV = 8
FF_BLK = D_FF // N_DEV
ROW_BLK = 384
MLP_ROWS = 1408
PROJ_ROWS = 704
ALPHA = 2.0 ** 0.25
LN_EPS = 1e-5
GN_EPS = 1e-5
ROPE_BASE = 10000.0
GELU_C = math.sqrt(2.0 / math.pi)
GELU_A = 0.044715
ADAM_LR, ADAM_B1, ADAM_B2, ADAM_EPS, ADAM_WD, ADAM_STEP = 0.001, 0.9, 0.999, 1e-08, 0.01, 10
VMEM_LIMIT = 60 * 1024 * 1024

_VMEM = pl.BlockSpec(memory_space=pltpu.VMEM)
_ANY = pl.BlockSpec(memory_space=pl.ANY)
_MESH = pl.DeviceIdType.MESH


def _params(sem=None):
    return pltpu.CompilerParams(dimension_semantics=sem, vmem_limit_bytes=VMEM_LIMIT)


def _dot(a, b):
    return jnp.dot(a.astype(MM), b.astype(MM), preferred_element_type=F32)


def _dot_nt(a, b):
    return lax.dot_general(a.astype(MM), b.astype(MM), (((1,), (1,)), ((), ())), preferred_element_type=F32)


def _dot_tn(a, b):
    return lax.dot_general(a.astype(MM), b.astype(MM), (((0,), (0,)), ((), ())), preferred_element_type=F32)


def _split3(a):
    hi = a.astype(jnp.bfloat16)
    r1 = a - hi.astype(F32)
    mid = r1.astype(jnp.bfloat16)
    lo = (r1 - mid.astype(F32)).astype(jnp.bfloat16)
    return hi, mid, lo


def _dot_sel_rhs(a, sel):
    s = sel.astype(jnp.bfloat16)
    return sum(jnp.dot(p, s, preferred_element_type=F32) for p in _split3(a))


def _dot_sel_lhs(sel, b):
    s = sel.astype(jnp.bfloat16)
    return sum(jnp.dot(s, p, preferred_element_type=F32) for p in _split3(b))


def _ln_fwd(r, eps):
    mu = jnp.mean(r, axis=-1, keepdims=True)
    xc = r - mu
    var = jnp.mean(xc * xc, axis=-1, keepdims=True)
    rstd = lax.rsqrt(var + eps)
    return xc * rstd, rstd


def _ln_bwd(dxhat, xhat, rstd):
    m1 = jnp.mean(dxhat, axis=-1, keepdims=True)
    m2 = jnp.mean(dxhat * xhat, axis=-1, keepdims=True)
    return rstd * (dxhat - m1 - xhat * m2)


def _colsum(a):
    return jnp.sum(a, axis=0, keepdims=True)


def _shift3(n_in, block=lambda i: i):
    return [pl.BlockSpec((CHUNK, D_MODEL), (lambda i, j=j: (jnp.clip(3 * block(i) - 1 + j, 0, n_in - 1), 0)))
            for j in range(3)]


def _ln_in(x2d, meta, jobs=(), gather_meta=False):
    seq = x2d.shape[0]
    tp = seq + CHUNK
    R = ROW_BLK
    nb = tp // R
    shard_w = D_MODEL // N_DEV

    def body(xa, xb, xc, meta_ref, xhat_ref, rstd_ref, raw_ref, *gathered):
        raw_ref[0:CHUNK, :] = xa[...]
        raw_ref[CHUNK:2 * CHUNK, :] = xb[...]
        raw_ref[2 * CHUNK:3 * CHUNK, :] = xc[...]

        @pl.when(pl.program_id(0) == nb - 1)
        def _():
            raw_ref[0:PAD, :] = jnp.zeros((PAD, D_MODEL), F32)
            if gather_meta:
                for d in range(N_DEV):
                    pltpu.sync_copy(gathered[0].at[d], raw_ref.at[PAD:CHUNK, d * shard_w:(d + 1) * shard_w])
            else:
                raw_ref[PAD:CHUNK, :] = meta_ref[...]

        xhat_ref[...], rstd_ref[...] = _ln_fwd(raw_ref[...], LN_EPS)

    row = lambda w: pl.BlockSpec((R, w), lambda i: (nb - 1 - i, 0))
    jobs = ([_job_gather(meta)] if gather_meta else []) + list(jobs)
    return _call(
        body, "ln_in", (nb,),
        _shift3(seq // CHUNK, lambda i: nb - 1 - i) + [pl.BlockSpec(meta.shape, lambda i: (0, 0))],
        [row(D_MODEL), row(1)], [jax.ShapeDtypeStruct((tp, D_MODEL), F32), jax.ShapeDtypeStruct((tp, 1), F32)],
        [pltpu.VMEM((R, D_MODEL), F32)], (x2d, x2d, x2d, meta), jobs, early=1 if gather_meta else 0)


def _in_proj(xhat0, ln_g, ln_b, w_int, cos2, sin2, jobs=()):
    tp = xhat0.shape[0]
    R = PROJ_ROWS if tp % PROJ_ROWS == 0 else ROW_BLK

    def body(xh_ref, g_ref, b_ref, w_ref, cos_ref, sin_ref, u_ref, q_ref, k_ref, v_ref, gate_ref):
        hb = (xh_ref[...] * g_ref[...] + b_ref[...]).astype(MM)
        valid = (pl.program_id(0) * R + lax.broadcasted_iota(jnp.int32, (R, 1), 0)) >= PAD

        def seg(lo, hi):
            return jnp.where(valid, _dot_nt(hb, w_ref[lo:hi, :]), 0.0)

        u_ref[...] = seg(0, S5_W)
        cos = cos_ref[...]
        sin = sin_ref[...]
        q = seg(S5_W, S5_W + RET_W)
        k = seg(S5_W + RET_W, S5_W + 2 * RET_W)
        for h in range(RET_H):
            sl = slice(h * HEAD, (h + 1) * HEAD)
            qh = q[:, sl]
            kh = k[:, sl]
            q_ref[:, sl] = (qh * cos + pltpu.roll(qh, HEAD // 2, 1) * sin).astype(q_ref.dtype)
            k_ref[:, sl] = ((kh * cos + pltpu.roll(kh, HEAD // 2, 1) * sin) * (HEAD ** -0.5)).astype(k_ref.dtype)
        v_ref[...] = seg(S5_W + 2 * RET_W, S5_W + 3 * RET_W).astype(v_ref.dtype)
        gate_ref[...] = seg(S5_W + 3 * RET_W, PROJ_W)

    def rows(w, dt):
        return pl.BlockSpec((R, w), lambda i: (i, 0)), jax.ShapeDtypeStruct((tp, w), dt)

    outs = [rows(S5_W, F32), rows(RET_W, MM), rows(RET_W, MM), rows(RET_W, MM), rows(RET_W, F32)]
    full = lambda s: pl.BlockSpec(s, lambda i: (0,) * len(s))
    return _call(
        body, "in_proj", (tp // R,),
        [pl.BlockSpec((R, D_MODEL), lambda i: (i, 0)), full((1, D_MODEL)), full((1, D_MODEL)), _VMEM,
         pl.BlockSpec((R, HEAD), lambda i: (i, 0)), pl.BlockSpec((R, HEAD), lambda i: (i, 0))],
        [o[0] for o in outs], [o[1] for o in outs], [], (xhat0, ln_g, ln_b, w_int, cos2, sin2), jobs)


def _s5_disc(lre, lim, ldt, bre_t, bim_t):
    dt = jnp.exp(ldt)
    mag = jnp.exp(lre * dt)
    ang = lim * dt
    lbr = mag * jnp.cos(ang)
    lbi = mag * jnp.sin(ang)
    den = lre * lre + lim * lim
    nr = lbr - 1.0
    qr = (nr * lre + lbi * lim) / den
    qi = (lbi * lre - nr * lim) / den
    return lbr, lbi, qr * bre_t - qi * bim_t, qr * bim_t + qi * bre_t


def _s5_tables(lbr, lbi, reverse):
    if reverse:
        lbi = -lbi
    pw = [(lbr, lbi)]
    for _ in range(7):
        r, i = pw[-1]
        pw.append((r * lbr - i * lbi, r * lbi + i * lbr))
    row = lax.broadcasted_iota(jnp.int32, (8, S5_N), 0)
    tabs = []
    for k in range(3):
        sh = 2 ** k
        mask = (row < 8 - sh) if reverse else (row >= sh)
        ar, ai = pw[sh - 1]
        tabs.append((jnp.where(mask, ar, 0.0), jnp.where(mask, ai, 0.0)))
    pr = jnp.zeros((8, S5_N), F32)
    pi = jnp.zeros((8, S5_N), F32)
    for i in range(8):
        ar, ai = pw[7 - i] if reverse else pw[i]
        pr = jnp.where(row == i, ar, pr)
        pi = jnp.where(row == i, ai, pi)
    tabs.append((pr, pi))
    return tabs


def _store_tables(tab_ref, tabs):
    for k, (r, i) in enumerate(tabs):
        tab_ref[2 * k] = r
        tab_ref[2 * k + 1] = i


def _bd_mask():
    r = lax.broadcasted_iota(jnp.int32, (S5_W, S5_N), 0)
    c = lax.broadcasted_iota(jnp.int32, (S5_W, S5_N), 1)
    return jnp.right_shift(r, 4) == jnp.right_shift(c, 6)


def _s5_block_diag(bbr_t, bbi_t, cre_w, cim_w):
    mask = _bd_mask()
    bd = lambda t: jnp.where(mask, t, 0.0)
    return (bd(jnp.tile(bbr_t, (S5_G, 1))), bd(jnp.tile(bbi_t, (S5_G, 1))),
            bd(jnp.tile(cre_w, (1, S5_N // HEAD))), bd(jnp.tile(cim_w, (1, S5_N // HEAD))))


def _scan8(xr, xi, tab_ref, lanes, reverse):
    for k in range(3):
        sh = (8 - 2 ** k) if reverse else 2 ** k
        sr = pltpu.roll(xr, sh, 0)
        si = pltpu.roll(xi, sh, 0)
        mr = tab_ref[2 * k, :, lanes]
        mi = tab_ref[2 * k + 1, :, lanes]
        xr, xi = xr + (mr * sr - mi * si), xi + (mr * si + mi * sr)
    return xr, xi


S5_LANES = 512


def _gelu(y):
    t = jnp.tanh(GELU_C * (y + GELU_A * y * y * y))
    return 0.5 * y * (1.0 + t), t


def _s5_fwd(u, lre, lim, ldt, bre_t, bim_t, cre_w, cim_w, d_row, w_glu, b_glu, jobs=()):
    tp = u.shape[0]
    R = ROW_BLK

    def body(u_ref, lre_ref, lim_ref, ldt_ref, bre_ref, bim_ref, cre_ref, cim_ref, d_ref, wg_ref, bg_ref,
             y_ref, xr_ref, xi_ref, bbd_r, bbd_i, cbd_r, cbd_i, tab_ref, car_r, car_i):
        @pl.when(pl.program_id(0) == 0)
        def _():
            lbr, lbi, bbr, bbi = _s5_disc(lre_ref[...], lim_ref[...], ldt_ref[...], bre_ref[...], bim_ref[...])
            br, bi, cr, ci = _s5_block_diag(bbr, bbi, cre_ref[...], cim_ref[...])
            bbd_r[...] = br.astype(MM)
            bbd_i[...] = bi.astype(MM)
            cbd_r[...] = cr.astype(MM)
            cbd_i[...] = ci.astype(MM)
            _store_tables(tab_ref, _s5_tables(lbr, lbi, False))
            car_r[...] = jnp.zeros_like(car_r)
            car_i[...] = jnp.zeros_like(car_i)

        u = u_ref[...]
        ub = u.astype(MM)
        xr_ref[...] = jnp.dot(ub, bbd_r[...], preferred_element_type=F32)
        xi_ref[...] = jnp.dot(ub, bbd_i[...], preferred_element_type=F32)
        for j in range(S5_N // S5_LANES):
            lanes = pl.ds(j * S5_LANES, S5_LANES)
            pr = tab_ref[6, :, lanes]
            pi = tab_ref[7, :, lanes]

            def step(g, carry):
                cr, ci = carry
                rows = pl.ds(pl.multiple_of(g * 8, 8), 8)
                xr, xi = _scan8(xr_ref[rows, lanes], xi_ref[rows, lanes], tab_ref, lanes, False)
                br = jnp.broadcast_to(cr[7:8, :], cr.shape)
                bi = jnp.broadcast_to(ci[7:8, :], ci.shape)
                xr = xr + (pr * br - pi * bi)
                xi = xi + (pr * bi + pi * br)
                xr_ref[rows, lanes] = xr
                xi_ref[rows, lanes] = xi
                return xr, xi

            cr, ci = lax.fori_loop(0, R // 8, step, (car_r[:, lanes], car_i[:, lanes]), unroll=2)
            car_r[:, lanes] = cr
            car_i[:, lanes] = ci
        y = _dot_nt(xr_ref[...], cbd_r[...]) - _dot_nt(xi_ref[...], cbd_i[...]) + d_ref[...] * u
        yg, _ = _gelu(y)
        z = _dot(yg, wg_ref[...]) + bg_ref[...]
        y_ref[...] = yg * jax.nn.sigmoid(z)

    full = lambda a: pl.BlockSpec(a.shape, lambda i: (0,) * a.ndim)
    small = [lre, lim, ldt, bre_t, bim_t, cre_w, cim_w, d_row, w_glu, b_glu]
    return _call(
        body, "s5_fwd", (tp // R,),
        [pl.BlockSpec((R, S5_W), lambda i: (i, 0))] + [full(a) for a in small],
        [pl.BlockSpec((R, S5_W), lambda i: (i, 0)), pl.BlockSpec((R, S5_N), lambda i: (i, 0)),
         pl.BlockSpec((R, S5_N), lambda i: (i, 0))],
        [jax.ShapeDtypeStruct((tp, S5_W), F32), jax.ShapeDtypeStruct((tp, S5_N), F32),
         jax.ShapeDtypeStruct((tp, S5_N), F32)],
        [pltpu.VMEM((S5_W, S5_N), MM)] * 4 + [pltpu.VMEM((8, 8, S5_N), F32), pltpu.VMEM((8, S5_N), F32),
                                              pltpu.VMEM((8, S5_N), F32)],
        (u, *small), jobs)


def _s5_bwd(dy_out, u, xr, xi, lre, lim, ldt, bre_t, bim_t, cre_w, cim_w, d_row, w_glu, b_glu, after, jobs=()):
    tp = u.shape[0]
    R = ROW_BLK
    nb = tp // R

    def body(dyo_ref, u_ref, xr_ref, xi_ref, xpr_ref, xpi_ref,
             lre_ref, lim_ref, ldt_ref, bre_ref, bim_ref, cre_ref, cim_ref, d_ref, wg_ref, bg_ref, after_ref,
             du_ref, dlre_ref, dlim_ref, dldt_ref, dbre_ref, dbim_ref, dcre_ref, dcim_ref, dd_ref, dwg_ref, dbg_ref,
             bbd_r, bbd_i, cbd_r, cbd_i, tab_ref, car_r, car_i, gr_ref, gi_ref, xer_ref, xei_ref,
             abr, abi, acr, aci, adr, adi):
        i = pl.program_id(0)

        @pl.when(i == 0)
        def _():
            lbr, lbi, bbr, bbi = _s5_disc(lre_ref[...], lim_ref[...], ldt_ref[...], bre_ref[...], bim_ref[...])
            br, bi, cr, ci = _s5_block_diag(bbr, bbi, cre_ref[...], cim_ref[...])
            bbd_r[...] = br.astype(MM)
            bbd_i[...] = bi.astype(MM)
            cbd_r[...] = cr.astype(MM)
            cbd_i[...] = ci.astype(MM)
            _store_tables(tab_ref, _s5_tables(lbr, lbi, True))
            for ref in (car_r, car_i, abr, abi, acr, aci, adr, adi, dd_ref, dwg_ref, dbg_ref):
                ref[...] = jnp.zeros_like(ref)

        u = u_ref[...]
        xrv = xr_ref[...]
        xiv = xi_ref[...]
        y = _dot_nt(xrv, cbd_r[...]) - _dot_nt(xiv, cbd_i[...]) + d_ref[...] * u
        yg, t = _gelu(y)
        z = _dot(yg, wg_ref[...]) + bg_ref[...]
        s = jax.nn.sigmoid(z)
        dout = dyo_ref[...]
        dz = dout * yg * s * (1.0 - s)
        dyg = dout * s + _dot_nt(dz, wg_ref[...])
        dwg_ref[...] += _dot_tn(yg, dz)
        dbg_ref[...] += _colsum(dz)
        dy = dyg * (0.5 * (1.0 + t) + 0.5 * y * (1.0 - t * t) * GELU_C * (1.0 + 3.0 * GELU_A * y * y))
        dd_ref[...] += _colsum(dy * u)
        acr[...] += _dot_tn(dy, xrv)
        aci[...] -= _dot_tn(dy, xiv)
        gr_ref[...] = _dot(dy, cbd_r[...])
        gi_ref[...] = -_dot(dy, cbd_i[...])
        has_prev = (i < nb - 1).astype(F32)
        xer_ref[0:8, :] = xpr_ref[...] * has_prev
        xei_ref[0:8, :] = xpi_ref[...] * has_prev
        xer_ref[8:R + 8, :] = xrv
        xei_ref[8:R + 8, :] = xiv
        row = lax.broadcasted_iota(jnp.int32, (8, S5_LANES), 0)
        for j in range(S5_N // S5_LANES):
            lanes = pl.ds(j * S5_LANES, S5_LANES)
            pr = tab_ref[6, :, lanes]
            pi = tab_ref[7, :, lanes]

            def step(n, carry):
                cr, ci, sar, sai = carry
                g = R // 8 - 1 - n
                r0 = pl.multiple_of(g * 8, 8)
                rows = pl.ds(r0, 8)
                gr, gi = _scan8(gr_ref[rows, lanes], gi_ref[rows, lanes], tab_ref, lanes, True)
                br = jnp.broadcast_to(cr[0:1, :], cr.shape)
                bi = jnp.broadcast_to(ci[0:1, :], ci.shape)
                gr = gr + (pr * br - pi * bi)
                gi = gi + (pr * bi + pi * br)
                gr_ref[rows, lanes] = gr
                gi_ref[rows, lanes] = gi
                last = row == 7
                xpr = pltpu.roll(jnp.where(last, xer_ref[rows, lanes], xer_ref[pl.ds(r0 + 8, 8), lanes]), 1, 0)
                xpi = pltpu.roll(jnp.where(last, xei_ref[rows, lanes], xei_ref[pl.ds(r0 + 8, 8), lanes]), 1, 0)
                return gr, gi, sar + (gr * xpr + gi * xpi), sai + (gi * xpr - gr * xpi)

            cr, ci, sar, sai = lax.fori_loop(
                0, R // 8, step, (car_r[:, lanes], car_i[:, lanes], adr[:, lanes], adi[:, lanes]), unroll=2)
            car_r[:, lanes] = cr
            car_i[:, lanes] = ci
            adr[:, lanes] = sar
            adi[:, lanes] = sai
        grv = gr_ref[...]
        giv = gi_ref[...]
        du_ref[...] = (dy * d_ref[...] + _dot_nt(grv, bbd_r[...]) + _dot_nt(giv, bbd_i[...])).astype(du_ref.dtype)
        abr[...] += _dot_tn(u, grv)
        abi[...] += _dot_tn(u, giv)

        @pl.when(i == nb - 1)
        def _():
            mask = _bd_mask()
            r16 = lax.broadcasted_iota(jnp.int32, (S5_H, S5_W), 1)
            h16 = lax.broadcasted_iota(jnp.int32, (S5_H, S5_W), 0)
            fold_b = jnp.bitwise_and(r16, S5_H - 1) == h16
            c64 = lax.broadcasted_iota(jnp.int32, (S5_N, S5_P), 0)
            p64 = lax.broadcasted_iota(jnp.int32, (S5_N, S5_P), 1)
            fold_c = jnp.bitwise_and(c64, S5_P - 1) == p64
            dbbr = _dot_sel_lhs(fold_b, jnp.where(mask, abr[...], 0.0))
            dbbi = _dot_sel_lhs(fold_b, jnp.where(mask, abi[...], 0.0))
            dcre_ref[...] = _dot_sel_rhs(jnp.where(mask, acr[...], 0.0), fold_c)
            dcim_ref[...] = _dot_sel_rhs(jnp.where(mask, aci[...], 0.0), fold_c)
            dlbr = _colsum(adr[...])
            dlbi = _colsum(adi[...])
            _, vjp = jax.vjp(_s5_disc, lre_ref[...], lim_ref[...], ldt_ref[...], bre_ref[...], bim_ref[...])
            dlre, dlim, dldt, dbre, dbim = vjp((dlbr, dlbi, dbbr, dbbi))
            dlre_ref[...] = dlre
            dlim_ref[...] = dlim
            dbre_ref[...] = dbre
            dbim_ref[...] = dbim
            gsel = jnp.right_shift(lax.broadcasted_iota(jnp.int32, (S5_N, HEAD), 0), 6) == \
                lax.broadcasted_iota(jnp.int32, (S5_N, HEAD), 1)
            dldt_ref[...] = _dot_sel_rhs(dldt, gsel)

    full = lambda a: pl.BlockSpec(a.shape, lambda i: (0,) * a.ndim)
    rev = lambda w: pl.BlockSpec((R, w), lambda i: (nb - 1 - i, 0))
    prev8 = pl.BlockSpec((8, S5_N), lambda i: (jnp.maximum((nb - 1 - i) * (R // 8) - 1, 0), 0))
    small = [lre, lim, ldt, bre_t, bim_t, cre_w, cim_w, d_row, w_glu, b_glu]
    outs = [((tp, S5_W), rev(S5_W))] + [
        (s, pl.BlockSpec(s, lambda i: (0, 0))) for s in
        [(1, S5_N), (1, S5_N), (1, HEAD), (S5_H, S5_N), (S5_H, S5_N), (S5_W, S5_P), (S5_W, S5_P),
         (1, S5_W), (S5_W, S5_W), (1, S5_W)]]
    return _call(
        body, "s5_bwd", (nb,),
        [rev(S5_W), rev(S5_W), rev(S5_N), rev(S5_N), prev8, prev8] + [full(a) for a in small + [after]],
        [o[1] for o in outs], [jax.ShapeDtypeStruct(o[0], MM if n == 0 else F32) for n, o in enumerate(outs)],
        [pltpu.VMEM((S5_W, S5_N), MM)] * 4 + [
            pltpu.VMEM((8, 8, S5_N), F32), pltpu.VMEM((8, S5_N), F32), pltpu.VMEM((8, S5_N), F32),
            pltpu.VMEM((R, S5_N), F32), pltpu.VMEM((R, S5_N), F32),
            pltpu.VMEM((R + 8, S5_N), F32), pltpu.VMEM((R + 8, S5_N), F32)] + [pltpu.VMEM((S5_W, S5_N), F32)] * 4 + [
            pltpu.VMEM((8, S5_N), F32), pltpu.VMEM((8, S5_N), F32)],
        (dy_out, u, xr, xi, xr, xi, *small, after), jobs)


RET_CHUNK = ROW_BLK
LOG_GAMMA = [math.log1p(-2.0 ** (-5 - h)) for h in range(RET_H)]
GAMMA_CHUNK = [math.exp(RET_CHUNK * lg) for lg in LOG_GAMMA]
_DECAY_SCRATCH = [pltpu.VMEM((RET_H, RET_CHUNK, RET_CHUNK), F32), pltpu.VMEM((RET_H, RET_CHUNK, HEAD), F32),
                  pltpu.VMEM((RET_H, RET_CHUNK, HEAD), F32)]


def _fill_decay(dm_ref, ze_ref, xi_ref):
    C = RET_CHUNK
    diff = (lax.broadcasted_iota(jnp.int32, (C, C), 0) - lax.broadcasted_iota(jnp.int32, (C, C), 1)).astype(F32)
    r = lax.broadcasted_iota(jnp.int32, (C, HEAD), 0).astype(F32)
    for h, lg in enumerate(LOG_GAMMA):
        dm_ref[h] = jnp.where(diff >= 0.0, jnp.exp(jnp.maximum(diff, 0.0) * lg), 0.0)
        ze_ref[h] = jnp.exp((C - 1.0 - r) * lg)
        xi_ref[h] = jnp.exp((r + 1.0) * lg)


def _ret_fwd(q, k, v, jobs=()):
    tp = q.shape[0]
    C = RET_CHUNK
    nc = tp // C

    def body(q_ref, k_ref, v_ref, o_ref, st_ref, s_ref, dm_ref, ze_ref, xi_ref):
        @pl.when(pl.program_id(0) == 0)
        def _():
            s_ref[...] = jnp.zeros_like(s_ref)
            _fill_decay(dm_ref, ze_ref, xi_ref)

        for h in range(RET_H):
            sl = slice(h * HEAD, (h + 1) * HEAD)
            qh, kh, vh = q_ref[:, sl], k_ref[:, sl], v_ref[:, sl]
            sh = s_ref[h]
            st_ref[0, sl, :] = sh
            scores = _dot_nt(qh, kh) * dm_ref[h]
            o_ref[:, sl] = _dot(scores, vh) + _dot(qh, sh) * xi_ref[h]
            s_ref[h] = GAMMA_CHUNK[h] * sh + _dot_tn(kh.astype(F32) * ze_ref[h], vh)

    blk = pl.BlockSpec((C, RET_W), lambda c: (c, 0))
    return _call(
        body, "ret_fwd", (nc,), [blk, blk, blk], [blk, pl.BlockSpec((1, RET_W, HEAD), lambda c: (c, 0, 0))],
        [jax.ShapeDtypeStruct((tp, RET_W), F32), jax.ShapeDtypeStruct((nc, RET_W, HEAD), F32)],
        [pltpu.VMEM((RET_H, HEAD, HEAD), F32)] + _DECAY_SCRATCH, (q, k, v), jobs)


def _ret_bwd(q, k, v, do, states, cos2, sin2, jobs=()):
    tp = q.shape[0]
    C = RET_CHUNK
    nc = tp // C

    def body(q_ref, k_ref, v_ref, do_ref, st_ref, cos_ref, sin_ref,
             dq_ref, dk_ref, dv_ref, ds_ref, dm_ref, ze_ref, xi_ref):
        @pl.when(pl.program_id(0) == 0)
        def _():
            ds_ref[...] = jnp.zeros_like(ds_ref)
            _fill_decay(dm_ref, ze_ref, xi_ref)

        cos = cos_ref[...]
        sin = sin_ref[...]
        for h in range(RET_H):
            sl = slice(h * HEAD, (h + 1) * HEAD)
            qh, kh, vh = q_ref[:, sl], k_ref[:, sl], v_ref[:, sl]
            dmh = dm_ref[h]
            sh = st_ref[0, sl, :]
            dsn = ds_ref[h]
            doh = do_ref[:, sl]
            dox = doh * xi_ref[h]
            a = _dot_nt(qh, kh) * dmh
            dqk = _dot_nt(doh, vh) * dmh
            kz = kh.astype(F32) * ze_ref[h]
            dv_ref[:, sl] = (_dot_tn(a, doh) + _dot(kz, dsn)).astype(dv_ref.dtype)
            dqr = _dot(dqk, kh) + _dot_nt(dox, sh)
            dkr = _dot_tn(dqk, qh) + ze_ref[h] * _dot_nt(vh, dsn)
            ds_ref[h] = GAMMA_CHUNK[h] * dsn + _dot_tn(qh, dox)
            dq_ref[:, sl] = (dqr * cos - pltpu.roll(dqr, HEAD // 2, 1) * sin).astype(dq_ref.dtype)
            dk_ref[:, sl] = ((dkr * cos - pltpu.roll(dkr, HEAD // 2, 1) * sin) * (HEAD ** -0.5)).astype(dk_ref.dtype)

    blk = pl.BlockSpec((C, RET_W), lambda c: (nc - 1 - c, 0))
    tab = pl.BlockSpec((C, HEAD), lambda c: (nc - 1 - c, 0))
    return _call(
        body, "ret_bwd", (nc,),
        [blk, blk, blk, blk, pl.BlockSpec((1, RET_W, HEAD), lambda c: (nc - 1 - c, 0, 0)), tab, tab],
        [blk, blk, blk], [jax.ShapeDtypeStruct((tp, RET_W), MM)] * 3,
        [pltpu.VMEM((RET_H, HEAD, HEAD), F32)] + _DECAY_SCRATCH, (q, k, v, do, states, cos2, sin2), jobs)


def _gn_gate(o, gate, gn_g, gn_b):
    xhat, rstd = _ln_fwd(o, GN_EPS)
    on = xhat * gn_g + gn_b
    s = jax.nn.sigmoid(gate)
    return gate * s * on, xhat, rstd, on, s


def _post_up(o, gate, ys5, xhat0, gn_g, gn_b, li_g, li_b, l1_g, l1_b, w_out, w_up, jobs=()):
    tp = o.shape[0]
    R = ROW_BLK

    def body(o_ref, g_ref, ys_ref, xh0_ref, gng, gnb, lig, lib, l1g, l1b, wo_ref, wu_ref,
             ycat_ref, xh1_ref, rstd1_ref, h1b_ref, pre_ref):
        ycat_ref[:, 0:S5_W] = ys_ref[...].astype(ycat_ref.dtype)
        for h in range(RET_H):
            sl = slice(h * HEAD, (h + 1) * HEAD)
            yret = _gn_gate(o_ref[:, sl], g_ref[:, sl], gng[:, sl], gnb[:, sl])[0]
            ycat_ref[:, S5_W + h * HEAD:S5_W + (h + 1) * HEAD] = yret.astype(ycat_ref.dtype)
        mixed = _dot(ycat_ref[...], wo_ref[...])
        h0 = xh0_ref[...] * lig[...] + lib[...]
        xh1, rstd1 = _ln_fwd(ALPHA * h0 + mixed, LN_EPS)
        xh1_ref[...] = xh1
        rstd1_ref[...] = rstd1
        h1b = (xh1 * l1g[...] + l1b[...]).astype(MM)
        h1b_ref[...] = h1b
        for d in range(N_DEV):
            pre_ref[:, d * FF_BLK:(d + 1) * FF_BLK] = jnp.maximum(_dot(h1b, wu_ref[d]), 0.0)

    row = lambda w: pl.BlockSpec((R, w), lambda i: (i, 0))
    full = lambda a: pl.BlockSpec(a.shape, lambda i: (0,) * a.ndim)
    vecs = [gn_g, gn_b, li_g, li_b, l1_g, l1_b]
    outs = [(row(D_MODEL), jax.ShapeDtypeStruct((tp, D_MODEL), MM)), (row(D_MODEL), jax.ShapeDtypeStruct((tp, D_MODEL), F32)),
            (row(1), jax.ShapeDtypeStruct((tp, 1), F32)), (row(D_MODEL), jax.ShapeDtypeStruct((tp, D_MODEL), MM)),
            (row(D_FF), jax.ShapeDtypeStruct((tp, D_FF), F32))]
    return _call(
        body, "post_up", (tp // R,),
        [row(RET_W), row(RET_W), row(S5_W), row(D_MODEL)] + [full(a) for a in vecs] + [_VMEM, _VMEM],
        [o[0] for o in outs], [o[1] for o in outs], [], (o, gate, ys5, xhat0, *vecs, w_out, w_up), jobs)


def _post_down(pre, xhat1, tgt, l1_g, l1_b, l2_g, l2_b, w_down):
    tp = pre.shape[0]
    seq = tgt.shape[0]
    R = ROW_BLK

    def body(pre_ref, xh1_ref, ta, tb, tc, l1g, l1b, l2g, l2b, wd_ref,
             dr2_ref, dffb_ref, loss_ref, dl2g_ref, dl2b_ref, tgt_ref):
        i = pl.program_id(0)

        @pl.when(i == 0)
        def _():
            for ref in (loss_ref, dl2g_ref, dl2b_ref):
                ref[...] = jnp.zeros_like(ref)

        tgt_ref[0:CHUNK, :] = ta[...]
        tgt_ref[CHUNK:2 * CHUNK, :] = tb[...]
        tgt_ref[2 * CHUNK:3 * CHUNK, :] = tc[...]
        ff = jnp.zeros((R, D_MODEL), F32)
        for d in range(N_DEV):
            pre = pre_ref[:, d * FF_BLK:(d + 1) * FF_BLK]
            ff = ff + _dot(pre * pre, wd_ref[d * FF_BLK:(d + 1) * FF_BLK, :])
        h1 = xh1_ref[...] * l1g[...] + l1b[...]
        xh2, rstd2 = _ln_fwd(ALPHA * h1 + ff, LN_EPS)
        h2 = xh2 * l2g[...] + l2b[...]
        valid = (i * R + lax.broadcasted_iota(jnp.int32, (R, 1), 0)) >= CHUNK
        err = jnp.where(valid, h2 - tgt_ref[...], 0.0)
        loss_ref[...] += 0.5 * jnp.sum(err * err) / D_MODEL
        dh2 = err * (1.0 / D_MODEL)
        dl2g_ref[...] += _colsum(dh2 * xh2)
        dl2b_ref[...] += _colsum(dh2)
        dr2 = _ln_bwd(dh2 * l2g[...], xh2, rstd2)
        dr2_ref[...] = dr2
        dffb_ref[...] = dr2.astype(MM)

    row = lambda w: pl.BlockSpec((R, w), lambda i: (i, 0))
    full = lambda a: pl.BlockSpec(a.shape, lambda i: (0,) * a.ndim)
    vecs = [l1_g, l1_b, l2_g, l2_b]
    acc = lambda s: (pl.BlockSpec(s, lambda i: (0, 0)), jax.ShapeDtypeStruct(s, F32))
    outs = [(row(D_MODEL), jax.ShapeDtypeStruct((tp, D_MODEL), F32)), (row(D_MODEL), jax.ShapeDtypeStruct((tp, D_MODEL), MM)),
            acc((8, HEAD)), acc((1, D_MODEL)), acc((1, D_MODEL))]
    return pl.pallas_call(
        body, name="post_down", grid=(tp // R,),
        in_specs=[row(D_FF), row(D_MODEL)] + _shift3(seq // CHUNK) + [full(a) for a in vecs] + [_VMEM],
        out_specs=[o[0] for o in outs], out_shape=[o[1] for o in outs],
        scratch_shapes=[pltpu.VMEM((R, D_MODEL), F32)],
        compiler_params=_params(("arbitrary",)),
    )(pre, xhat1, tgt, tgt, tgt, *vecs, w_down)


def _mlp_bwd(h1b, dffb, pre, w_up, w_down):
    tp = h1b.shape[0]
    R = MLP_ROWS if tp % MLP_ROWS == 0 else ROW_BLK
    nr = tp // R

    def body(h_ref, df_ref, pre_ref, wu_ref, wd_ref, gup_ref, gdn_ref, dh1_ref, aup, adn):
        d = pl.program_id(0)
        r = pl.program_id(1)

        @pl.when(r == 0)
        def _():
            aup[...] = jnp.zeros_like(aup)
            adn[...] = jnp.zeros_like(adn)

        h = h_ref[...]
        df = df_ref[...]
        wu = wu_ref[0]
        wd = wd_ref[0]
        pre = pre_ref[...]
        dpre = (_dot_nt(df, wd) * (2.0 * pre)).astype(MM)

        aup[...] += _dot_tn(h, dpre)
        adn[...] += _dot_tn(pre * pre, df)
        contrib = _dot_nt(dpre, wu)
        rows = pl.ds(pl.multiple_of(r * R, 64), R)

        @pl.when(d == 0)
        def _():
            dh1_ref[rows, :] = contrib

        @pl.when(d > 0)
        def _():
            dh1_ref[rows, :] += contrib

        @pl.when(r == nr - 1)
        def _():
            gup_ref[0] = aup[...].astype(gup_ref.dtype)
            gdn_ref[0] = adn[...].astype(gdn_ref.dtype)

    return pl.pallas_call(
        body, name="mlp_bwd", grid=(N_DEV, nr),
        in_specs=[pl.BlockSpec((R, D_MODEL), lambda d, r: (r, 0)), pl.BlockSpec((R, D_MODEL), lambda d, r: (r, 0)),
                  pl.BlockSpec((R, FF_BLK), lambda d, r: (r, d)),
                  pl.BlockSpec((1, D_MODEL, FF_BLK), lambda d, r: (d, 0, 0)),
                  pl.BlockSpec((1, FF_BLK, D_MODEL), lambda d, r: (d, 0, 0))],
        out_specs=[pl.BlockSpec((1, D_MODEL, FF_BLK), lambda d, r: (d, 0, 0)),
                   pl.BlockSpec((1, FF_BLK, D_MODEL), lambda d, r: (d, 0, 0)), _VMEM],
        out_shape=[jax.ShapeDtypeStruct((N_DEV, D_MODEL, FF_BLK), MM), jax.ShapeDtypeStruct((N_DEV, FF_BLK, D_MODEL), MM),
                   jax.ShapeDtypeStruct((tp, D_MODEL), F32)],
        scratch_shapes=[pltpu.VMEM((D_MODEL, FF_BLK), F32), pltpu.VMEM((FF_BLK, D_MODEL), F32)],
        compiler_params=_params(("arbitrary", "arbitrary")),
    )(h1b, dffb, pre, w_up, w_down.reshape(N_DEV, FF_BLK, D_MODEL))


def _post_bwd(dh1m, dr2, xhat1, rstd1, ycat, o, gate, gn_g, gn_b, l1_g, w_out, jobs=()):
    tp = o.shape[0]
    R = ROW_BLK
    nb = tp // R

    def body(dm_ref, dr2_ref, xh1_ref, rs1_ref, yc_ref, o_ref, g_ref, gng, gnb, l1g, wo_ref,
             do_ref, dg_ref, dys_ref, dh0_ref, gwo_ref, dl1g_ref, dl1b_ref, dgng_ref, dgnb_ref, awo):
        i = pl.program_id(0)

        @pl.when(i == 0)
        def _():
            for ref in (awo, dl1g_ref, dl1b_ref, dgng_ref, dgnb_ref):
                ref[...] = jnp.zeros_like(ref)

        dh1 = dm_ref[...] + ALPHA * dr2_ref[...]
        xh1 = xh1_ref[...]
        dl1g_ref[...] += _colsum(dh1 * xh1)
        dl1b_ref[...] += _colsum(dh1)
        dr1 = _ln_bwd(dh1 * l1g[...], xh1, rs1_ref[...])
        dh0_ref[...] = ALPHA * dr1
        dmix = dr1.astype(MM)
        awo[...] += _dot_tn(yc_ref[...], dmix)
        dyc = _dot_nt(dmix, wo_ref[...])
        dys_ref[...] = dyc[:, 0:S5_W]
        for h in range(RET_H):
            sl = slice(h * HEAD, (h + 1) * HEAD)
            gt = g_ref[:, sl]
            _, xhat, rstd, on, s = _gn_gate(o_ref[:, sl], gt, gng[:, sl], gnb[:, sl])
            dyr = dyc[:, S5_W + h * HEAD:S5_W + (h + 1) * HEAD]
            dg_ref[:, sl] = (dyr * on * (s * (1.0 + gt * (1.0 - s)))).astype(dg_ref.dtype)
            don = dyr * gt * s
            dgng_ref[:, sl] += _colsum(don * xhat)
            dgnb_ref[:, sl] += _colsum(don)
            do_ref[:, sl] = _ln_bwd(don * gng[:, sl], xhat, rstd)

        @pl.when(i == nb - 1)
        def _():
            gwo_ref[...] = awo[...].astype(gwo_ref.dtype)

    row = lambda w: pl.BlockSpec((R, w), lambda i: (i, 0))
    full = lambda a: pl.BlockSpec(a.shape, lambda i: (0,) * a.ndim)
    acc = lambda s, dt=F32: (pl.BlockSpec(s, lambda i: (0, 0)), jax.ShapeDtypeStruct(s, dt))
    outs = [(row(RET_W), jax.ShapeDtypeStruct((tp, RET_W), F32)), (row(RET_W), jax.ShapeDtypeStruct((tp, RET_W), MM)),
            (row(S5_W), jax.ShapeDtypeStruct((tp, S5_W), F32)), (row(D_MODEL), jax.ShapeDtypeStruct((tp, D_MODEL), F32)),
            acc((D_MODEL, D_MODEL), MM), acc((1, D_MODEL)), acc((1, D_MODEL)), acc((1, RET_W)), acc((1, RET_W))]
    return _call(
        body, "post_bwd", (nb,),
        [row(D_MODEL), row(D_MODEL), row(D_MODEL), row(1), row(D_MODEL), row(RET_W), row(RET_W),
         full(gn_g), full(gn_b), full(l1_g), _VMEM],
        [o[0] for o in outs], [o[1] for o in outs],
        [pltpu.VMEM((D_MODEL, D_MODEL), F32)],
        (dh1m, dr2, xhat1, rstd1, ycat, o, gate, gn_g, gn_b, l1_g, w_out), jobs)


_PROJ_SEGS = [(0, S5_W)] + [(S5_W + n * RET_W, S5_W + (n + 1) * RET_W) for n in range(4)]


def _in_w_grad(du, dq, dk, dv, dg, xhat0, li_g, li_b):
    tp = du.shape[0]
    R = PROJ_ROWS if tp % PROJ_ROWS == 0 else ROW_BLK
    nb = tp // R

    def body(du_ref, dq_ref, dk_ref, dv_ref, dg_ref, xh_ref, lig, lib, gw_ref, aw):
        i = pl.program_id(0)

        @pl.when(i == 0)
        def _():
            aw[...] = jnp.zeros_like(aw)

        valid = (i * R + lax.broadcasted_iota(jnp.int32, (R, 1), 0)) >= PAD
        hb = (xh_ref[...] * lig[...] + lib[...]).astype(MM)
        for (lo, hi), ref in zip(_PROJ_SEGS, (du_ref, dq_ref, dk_ref, dv_ref, dg_ref)):
            aw[lo:hi, :] += _dot_tn(jnp.where(valid, ref[...], 0.0).astype(MM), hb)

        @pl.when(i == nb - 1)
        def _():
            gw_ref[...] = aw[...].astype(gw_ref.dtype)

    row = lambda w: pl.BlockSpec((R, w), lambda i: (i, 0))
    full = lambda a: pl.BlockSpec(a.shape, lambda i: (0,) * a.ndim)
    (gw,), _ = _call(
        body, "in_w_grad", (nb,),
        [row(S5_W), row(RET_W), row(RET_W), row(RET_W), row(RET_W), row(D_MODEL), full(li_g), full(li_b)],
        [pl.BlockSpec((PROJ_W, D_MODEL), lambda i: (0, 0))], [jax.ShapeDtypeStruct((PROJ_W, D_MODEL), MM)],
        [pltpu.VMEM((PROJ_W, D_MODEL), F32)], (du, dq, dk, dv, dg, xhat0, li_g, li_b))
    return gw


def _in_bwd(du, dq, dk, dv, dg, dh0r, xhat0, rstd0, li_g, w_int, after):
    tp = du.shape[0]
    R = PROJ_ROWS if tp % PROJ_ROWS == 0 else ROW_BLK
    nb = tp // R
    segs = _PROJ_SEGS

    def body(du_ref, dq_ref, dk_ref, dv_ref, dg_ref, dh0r_ref, xh_ref, rs_ref, lig, w_ref, after_ref,
             gx_ref, dmeta_ref, dlg_ref, dlb_ref, stage, out_sems):
        i = pl.program_id(0)
        slot = i % 2

        def to_gx(step_slot, first):
            if first:
                return pltpu.make_async_copy(stage.at[0, CHUNK:R, :], gx_ref.at[0:R - CHUNK, :], out_sems.at[0])
            return pltpu.make_async_copy(stage.at[step_slot], gx_ref.at[pl.ds(i * R - CHUNK, R), :], out_sems.at[step_slot])

        @pl.when(i == 0)
        def _():
            for ref in (dlg_ref, dlb_ref):
                ref[...] = jnp.zeros_like(ref)

        @pl.when(i >= 3)
        def _():
            to_gx(slot, False).wait()

        valid = (i * R + lax.broadcasted_iota(jnp.int32, (R, 1), 0)) >= PAD
        xh = xh_ref[...]
        dh0 = dh0r_ref[...]
        for (lo, hi), ref in zip(segs, (du_ref, dq_ref, dk_ref, dv_ref, dg_ref)):
            dh0 = dh0 + _dot(jnp.where(valid, ref[...], 0.0).astype(MM), w_ref[lo:hi, :])
        dlg_ref[...] += _colsum(dh0 * xh)
        dlb_ref[...] += _colsum(dh0)
        draw = _ln_bwd(dh0 * lig[...], xh, rs_ref[...])
        stage[slot] = draw

        @pl.when(i == 0)
        def _():
            dmeta_ref[...] = draw[PAD:CHUNK, :]
            first = to_gx(0, True)
            first.start()
            first.wait()

        @pl.when(i > 0)
        def _():
            to_gx(slot, False).start()

        @pl.when(i == nb - 1)
        def _():
            for back in (1, 0):
                if nb - 1 - back >= 1:
                    to_gx((nb - 1 - back) % 2, False).wait()

    row = lambda w: pl.BlockSpec((R, w), lambda i: (i, 0))
    full = lambda a: pl.BlockSpec(a.shape, lambda i: (0,) * a.ndim)
    acc = lambda s, dt=F32: (pl.BlockSpec(s, lambda i: (0, 0)), jax.ShapeDtypeStruct(s, dt))
    outs = [(_ANY, jax.ShapeDtypeStruct((tp - CHUNK, D_MODEL), F32)), acc((N_META, D_MODEL)),
            acc((1, D_MODEL)), acc((1, D_MODEL))]
    return _call(
        body, "in_bwd", (nb,),
        [row(S5_W), row(RET_W), row(RET_W), row(RET_W), row(RET_W), row(D_MODEL), row(D_MODEL), row(1),
         full(li_g), _VMEM, full(after)],
        [o[0] for o in outs], [o[1] for o in outs],
        [pltpu.VMEM((2, R, D_MODEL), F32), pltpu.SemaphoreType.DMA((2,))],
        (du, dq, dk, dv, dg, dh0r, xhat0, rstd0, li_g, w_int, after))[0]


def _place():
    return lax.axis_index("x"), lax.axis_index("y"), lax.axis_index("c")


def _dma_sems(n):
    return pltpu.SemaphoreType.DMA((n,))


def _job_gather(shard):
    def parts(ins, outs, sems):
        (src,), (out,), (send_sems, recv_sems, local_sem) = ins, outs, sems
        x, y, c = _place()
        north = c == 1
        me, sib = (x, y, c), (x, y, 1 - c)
        xn, yn, dg = (1 - x, y, c), (x, 1 - y, c), (1 - x, 1 - y, c)
        relay_from = (jnp.where(north, 1 - x, x), jnp.where(north, y, 1 - y), c)
        relay_to = (jnp.where(north, x, 1 - x), jnp.where(north, 1 - y, y), c)

        def slot(dev):
            return out.at[4 * dev[0] + 2 * dev[1] + dev[2]]

        def copy(k, block, to, from_input=False):
            return pltpu.make_async_remote_copy(
                src_ref=src if from_input else slot(block), dst_ref=slot(block),
                send_sem=send_sems.at[k], recv_sem=recv_sems.at[k], device_id=to, device_id_type=_MESH)

        mine = lambda: pltpu.make_async_copy(src, slot(me), local_sem.at[0])
        first = lambda: [copy(0, me, sib, True), copy(1, me, xn, True), copy(2, me, yn, True)]
        relayed = lambda: [copy(3, relay_from, relay_to), copy(4, xn, sib), copy(5, yn, sib)]
        return me, sib, xn, yn, dg, copy, mine, first, relayed

    def start(ins, outs, sems):
        mine, first = parts(ins, outs, sems)[6:8]
        mine().start()
        for cp in first():
            cp.start()

    def relay(ins, outs, sems):
        me, sib, xn, yn, dg, copy, mine, first, relayed = parts(ins, outs, sems)
        copy(1, xn, me).wait_recv()
        copy(2, yn, me).wait_recv()
        for cp in relayed():
            cp.start()

    def finish(ins, outs, sems):
        me, sib, xn, yn, dg, copy, mine, first, relayed = parts(ins, outs, sems)
        other = 1 - me[2]
        copy(3, dg, me).wait_recv()
        last = copy(6, dg, sib)
        last.start()
        copy(0, sib, me).wait_recv()
        for k, chip in ((4, xn), (5, yn), (6, dg)):
            copy(k, (chip[0], chip[1], other), me).wait_recv()
        for cp in first() + relayed() + [last]:
            cp.wait_send()
        mine().wait()

    return dict(ins=[shard], outs=[jax.ShapeDtypeStruct((N_DEV,) + shard.shape, shard.dtype)],
                sems=[_dma_sems(7), _dma_sems(7), _dma_sems(1)], start=start, middle=relay, finish=finish)


def _job_pair(g):
    def copies(ins, outs, sems):
        x, y, c = _place()
        return [pltpu.make_async_remote_copy(
            src_ref=ins[0].at[2 * j + (1 - c)], dst_ref=outs[0].at[j], send_sem=sems[0].at[j], recv_sem=sems[1].at[j],
            device_id=(x, y, 1 - c), device_id_type=_MESH) for j in range(4)]

    def start(ins, outs, sems):
        for cp in copies(ins, outs, sems):
            cp.start()

    def finish(ins, outs, sems):
        for cp in copies(ins, outs, sems):
            cp.wait()

    return dict(ins=[g], outs=[jax.ShapeDtypeStruct((4,) + g.shape[1:], g.dtype)], sems=[_dma_sems(4), _dma_sems(4)],
                start=start, finish=finish)


def _job_chips(p):
    def copies(ins, outs, sems):
        x, y, c = _place()
        chips = [(1 - x, y), (x, 1 - y), (1 - x, 1 - y)]
        return [pltpu.make_async_remote_copy(
            src_ref=ins[0].at[2 * chip[0] + chip[1]], dst_ref=outs[0].at[k], send_sem=sems[0].at[k],
            recv_sem=sems[1].at[k], device_id=(*chip, c), device_id_type=_MESH) for k, chip in enumerate(chips)]

    def start(ins, outs, sems):
        for cp in copies(ins, outs, sems):
            cp.start()

    def finish(ins, outs, sems):
        for cp in copies(ins, outs, sems):
            cp.wait()

    return dict(ins=[p], outs=[jax.ShapeDtypeStruct((3,) + p.shape[1:], p.dtype)], sems=[_dma_sems(3), _dma_sems(3)],
                start=start, finish=finish)


_HBM = pl.BlockSpec(memory_space=pltpu.HBM)
_SEM = pl.BlockSpec(memory_space=pltpu.SEMAPHORE)
_ORDERED = pltpu.CompilerParams(has_side_effects=pltpu.SideEffectType.DATAFLOW_SIDE_EFFECTING)


def _chip_copies(p_ref, land_ref, sems):
    x, y, c = _place()
    chips = [(1 - x, y), (x, 1 - y), (1 - x, 1 - y)]
    return [pltpu.make_async_remote_copy(
        src_ref=p_ref.at[2 * chip[0] + chip[1]], dst_ref=land_ref.at[k], send_sem=sems[k], recv_sem=sems[3 + k],
        device_id=(*chip, c), device_id_type=_MESH) for k, chip in enumerate(chips)]


def _chips_start(ps, name):
    n = len(ps)

    def body(*refs):
        sems = refs[2 * n:8 * n]
        for a in range(n):
            for cp in _chip_copies(refs[a], refs[n + a], sems[6 * a:6 * a + 6]):
                cp.start()
        refs[-1][...] = jnp.zeros_like(refs[-1])

    lands = [(3,) + p.shape[1:] for p in ps]
    hbm = lambda arr: pltpu.with_memory_space_constraint(arr, pltpu.HBM)
    outs = pl.pallas_call(
        body, name=name,
        out_shape=(*[pltpu.SemaphoreType.DMA(())] * (6 * n), *[pltpu.HBM(p.shape, p.dtype) for p in ps],
                   *[pltpu.HBM(s, p.dtype) for s, p in zip(lands, ps)], jax.ShapeDtypeStruct((8, LANE), F32)),
        in_specs=[_HBM] * (2 * n), out_specs=(*[_SEM] * (6 * n), *[_HBM] * (2 * n), _VMEM),
        input_output_aliases={a: 6 * n + a for a in range(2 * n)}, compiler_params=_ORDERED,
    )(*[hbm(p) for p in ps], *[hbm(lax.empty(s, p.dtype)) for s, p in zip(lands, ps)])
    return (list(outs[:6 * n]), list(outs[6 * n:7 * n]), list(outs[7 * n:8 * n])), outs[8 * n]


def _chips_wait(started, after, name):
    sems, thrus, lands = started
    n = len(thrus)

    def body(*refs):
        for a in range(n):
            for cp in _chip_copies(refs[a], refs[n + a], refs[2 * n + 6 * a:2 * n + 6 * a + 6]):
                cp.wait_send()
                cp.wait_recv()

    outs = pl.pallas_call(
        body, name=name, out_shape=[pltpu.HBM(t.shape, t.dtype) for t in thrus + lands],
        in_specs=(*[_HBM] * (2 * n), *[_SEM] * (6 * n), _ANY), out_specs=[_HBM] * (2 * n),
        input_output_aliases={a: a for a in range(2 * n)}, compiler_params=_ORDERED,
    )(*thrus, *lands, *sems, after)
    return list(outs[:n]), list(outs[n:])


def _split_job_refs(jobs, ins, outs, sems):
    res, a, b, c = [], 0, 0, 0
    for job in jobs:
        na, nb, nc = len(job["ins"]), len(job["outs"]), len(job["sems"])
        res.append((ins[a:a + na], outs[b:b + nb], sems[c:c + nc]))
        a, b, c = a + na, b + nb, c + nc
    return res


def _call(body, name, grid, in_specs, out_specs, out_shape, scratch, args, jobs=(), prefetch=None, early=0):
    jobs = list(jobs)
    n_in, n_out, n_scr = len(in_specs), len(out_specs), len(scratch)
    j_in = [a for job in jobs for a in job["ins"]]
    j_out = [o for job in jobs for o in job["outs"]]
    j_scr = [s for job in jobs for s in job["sems"]]
    nsteps = grid[0]
    n_pre = 0 if prefetch is None else 1

    def wrapped(*refs):
        pre, refs = refs[:n_pre], refs[n_pre:]
        ins, jins = refs[:n_in], refs[n_in:n_in + len(j_in)]
        refs = refs[n_in + len(j_in):]
        outs, jouts = refs[:n_out], refs[n_out:n_out + len(j_out)]
        refs = refs[n_out + len(j_out):]
        scr, jscr = refs[:n_scr], refs[n_scr:]
        per_job = _split_job_refs(jobs, jins, jouts, jscr)

        def middle():
            for job, r in zip(jobs, per_job):
                if "middle" in job:
                    job["middle"](*r)

        @pl.when(pl.program_id(0) == 0)
        def _():
            for job, r in zip(jobs, per_job):
                job["start"](*r)

        if nsteps >= 3:
            pl.when(pl.program_id(0) == nsteps // 2)(middle)

        if early:
            @pl.when(pl.program_id(0) == nsteps - 1)
            def _():
                for job, r in zip(jobs[:early], per_job[:early]):
                    job["finish"](*r)

        body(*pre, *ins, *outs, *scr, *[o for r in per_job[:early] for o in r[1]])

        @pl.when(pl.program_id(0) == nsteps - 1)
        def _():
            if nsteps < 3:
                middle()
            for job, r in zip(jobs[early:], per_job[early:]):
                job["finish"](*r)

    specs = dict(in_specs=list(in_specs) + [_ANY] * len(j_in), out_specs=list(out_specs) + [_ANY] * len(j_out),
                 scratch_shapes=list(scratch) + j_scr)
    if n_pre:
        specs = dict(grid_spec=pltpu.PrefetchScalarGridSpec(num_scalar_prefetch=1, grid=grid, **specs))
    else:
        specs["grid"] = grid
    res = pl.pallas_call(
        wrapped if jobs else body, name=name, out_shape=list(out_shape) + j_out,
        compiler_params=_params(("arbitrary",) * len(grid)), **specs,
    )(*([prefetch] if n_pre else []), *args, *j_in)
    return list(res[:n_out]), list(res[n_out:])


def _exchange(jobs, name):
    j_in = [a for job in jobs for a in job["ins"]]
    j_out = [o for job in jobs for o in job["outs"]]
    j_scr = [s for job in jobs for s in job["sems"]]

    def body(*refs):
        per_job = _split_job_refs(jobs, refs[:len(j_in)], refs[len(j_in):len(j_in) + len(j_out)],
                                  refs[len(j_in) + len(j_out):])
        for phase in ("start", "middle", "finish"):
            for job, r in zip(jobs, per_job):
                if phase in job:
                    job[phase](*r)

    return pl.pallas_call(body, name=name, out_shape=j_out, in_specs=[_ANY] * len(j_in), out_specs=[_ANY] * len(j_out),
                          scratch_shapes=j_scr)(*j_in)


def _pair_sum(gs, r1s, c_arr, name):
    n = len(gs)

    def body(c_ref, *refs):
        for a in range(n):
            refs[2 * n + a][...] = (refs[a][...].astype(F32) + refs[n + a][...].astype(F32)).astype(refs[2 * n + a].dtype)

    def blk(g, own):
        s = g.shape[1:]
        if own:
            return pl.BlockSpec((1,) + s, lambda j, c_ref: (2 * j + c_ref[0],) + (0,) * len(s))
        return pl.BlockSpec((1,) + s, lambda j, c_ref: (j,) + (0,) * len(s))

    return pl.pallas_call(
        body, name=name,
        grid_spec=pltpu.PrefetchScalarGridSpec(
            num_scalar_prefetch=1, grid=(4,),
            in_specs=[blk(g, True) for g in gs] + [blk(g, False) for g in gs],
            out_specs=[blk(g, False) for g in gs]),
        out_shape=[jax.ShapeDtypeStruct((4,) + g.shape[1:], g.dtype) for g in gs],
        compiler_params=_params(("arbitrary",)),
    )(c_arr, *gs, *r1s)


def _adamw_math(w, g, m, v):
    m = ADAM_B1 * m + (1.0 - ADAM_B1) * g
    v = ADAM_B2 * v + (1.0 - ADAM_B2) * (g * g)
    m_hat = m / (1.0 - ADAM_B1 ** ADAM_STEP)
    v_hat = v / (1.0 - ADAM_B2 ** ADAM_STEP)
    return -ADAM_LR * (m_hat / (jnp.sqrt(v_hat) + ADAM_EPS) + ADAM_WD * w), m, v


def _view(name, a):
    return jnp.swapaxes(a, -1, -2) if name in ("w_in", "s5_b_re", "s5_b_im") else a


def _adamw_shards(items, name, steps, chip, jobs=()):
    n = len(items)

    def body(chip_ref, *refs):
        for a in range(n):
            p_ref, r_ref, w_ref, m_ref, v_ref = refs[5 * a:5 * a + 5]
            g = ((p_ref[0].astype(F32) + r_ref[0].astype(F32)) + r_ref[1].astype(F32)) + r_ref[2].astype(F32)
            outs = refs[5 * n + 4 * a:5 * n + 4 * a + 4]
            outs[0][...] = g
            outs[1][...], outs[2][...], outs[3][...] = _adamw_math(w_ref[...], g, m_ref[...], v_ref[...])

    in_specs, out_specs, out_shape, flat = [], [], [], []
    for p, r, w, m, v in items:
        rows, cols = w.shape
        rb = rows // steps
        in_specs += [pl.BlockSpec((1, rb, cols), lambda i, c: (c[0], i, 0)), pl.BlockSpec((3, rb, cols), lambda i, c: (0, i, 0))]
        wblk = pl.BlockSpec((rb, cols), lambda i, c: (i, 0))
        in_specs += [wblk] * 3
        out_specs += [wblk] * 4
        out_shape += [jax.ShapeDtypeStruct(w.shape, F32)] * 4
        flat += [p, r, w, m, v]
    return _call(body, name, (steps,), in_specs, out_specs, out_shape, [], flat, jobs, prefetch=chip)


def _sum_devices(gathered, name):
    def body(gs_ref, g_ref):
        g = gs_ref[0]
        for s in range(1, N_DEV):
            g = g + gs_ref[s]
        g_ref[...] = g

    return pl.pallas_call(body, name=name, out_shape=jax.ShapeDtypeStruct(gathered.shape[1:], F32),
                          in_specs=[_VMEM], out_specs=_VMEM, compiler_params=_params())(gathered)


def _adamw_native(items, name):
    n = len(items)

    def body(*refs):
        for a in range(n):
            g, w, m, v = (refs[4 * a + t][...] for t in range(4))
            refs[4 * n + 3 * a][...], refs[4 * n + 3 * a + 1][...], refs[4 * n + 3 * a + 2][...] = _adamw_math(w, g, m, v)

    return pl.pallas_call(
        body, name=name, out_shape=[jax.ShapeDtypeStruct(it[1].shape, F32) for it in items for _ in range(3)],
        in_specs=[_VMEM] * (4 * n), out_specs=[_VMEM] * (3 * n), compiler_params=_params(),
    )(*[t for it in items for t in it])


SMALL = ["ln_in_g", "ln_in_b", "s5_lambda_re", "s5_lambda_im", "s5_log_dt", "s5_b_re", "s5_b_im", "s5_c_re", "s5_c_im",
         "s5_d", "s5_b_glu", "ret_gn_g", "ret_gn_b", "ln1_g", "ln1_b", "ln2_g", "ln2_b"]
LATE = ["ln_in_g", "ln_in_b", "meta_tokens"]
EARLY = [n for n in SMALL if n not in LATE] + ["s5_w_glu", "loss"]
LANE = 128


def _pack(arrs):
    parts = []
    for a in arrs:
        f = a.reshape(-1)
        parts.append(jnp.pad(f, (0, (-f.shape[0]) % LANE)))
    flat = jnp.concatenate(parts)
    rows = -(-flat.shape[0] // LANE)
    flat = jnp.pad(flat, (0, (-rows % 8) * LANE + rows * LANE - flat.shape[0]))
    return flat.reshape(-1, LANE)


def _unpack(packed, shapes):
    flat = packed.reshape(-1)
    out, off = [], 0
    for s in shapes:
        n = math.prod(s)
        out.append(flat[off:off + n].reshape(s))
        off += n + (-n) % LANE
    return out


def _rope_tables(tp):
    inv_freq = 1.0 / (ROPE_BASE ** (jnp.arange(0, HEAD, 2, dtype=F32) / HEAD))
    blk = (jnp.arange(tp // ROW_BLK, dtype=F32) * ROW_BLK)[:, None, None] * inv_freq
    off = (jnp.arange(ROW_BLK, dtype=F32) - float(PAD))[None, :, None] * inv_freq
    cos = (jnp.cos(blk) * jnp.cos(off) - jnp.sin(blk) * jnp.sin(off)).reshape(tp, HEAD // 2)
    sin = (jnp.sin(blk) * jnp.cos(off) + jnp.cos(blk) * jnp.sin(off)).reshape(tp, HEAD // 2)
    return jnp.concatenate([cos, cos], axis=1), jnp.concatenate([-sin, sin], axis=1)


def _local_step(x2d, tgt, meta, w_int, w_out, w_up, w_down, w_glu, sp, distributed):
    tp = x2d.shape[0] + CHUNK
    row = lambda a: a.reshape(1, -1)
    cos2, sin2 = _rope_tables(tp)
    li_g, li_b = row(sp["ln_in_g"]), row(sp["ln_in_b"])
    l1_g, l1_b, l2_g, l2_b = row(sp["ln1_g"]), row(sp["ln1_b"]), row(sp["ln2_g"]), row(sp["ln2_b"])
    gn_g, gn_b = row(sp["ret_gn_g"]), row(sp["ret_gn_b"])
    lre, lim = row(sp["s5_lambda_re"]), row(sp["s5_lambda_im"])
    ldt = row(jnp.repeat(sp["s5_log_dt"].reshape(-1), S5_P))
    to_t = lambda b: b.reshape(S5_G, S5_P, S5_H).transpose(2, 0, 1).reshape(S5_H, S5_N)
    bre_t, bim_t = to_t(sp["s5_b_re"]), to_t(sp["s5_b_im"])
    to_w = lambda c: jnp.tile(c.reshape(S5_W, S5_P), (1, 2))
    cre_w, cim_w = to_w(sp["s5_c_re"]), to_w(sp["s5_c_im"])

    jobs = (lambda *j: list(j)) if distributed else (lambda *j: [])
    c_arr = jnp.reshape(lax.axis_index("c"), (1,)).astype(jnp.int32) if distributed else None
    (xhat0, rstd0), bg = _ln_in(x2d, meta, jobs(*([_job_gather(w_int), _job_gather(w_glu)] if distributed else [])),
                                gather_meta=distributed)
    if distributed:
        w_int, w_glu = bg[1].reshape(PROJ_W, D_MODEL), bg[2].reshape(S5_W, S5_W)
    s5_small = (lre, lim, ldt, bre_t, bim_t, cre_w, cim_w, row(sp["s5_d"]), w_glu, row(sp["s5_b_glu"]))
    (u, q, k, v, gate), bg = _in_proj(xhat0, li_g, li_b, w_int, cos2, sin2,
                                      jobs(_job_gather(w_out) if distributed else None))
    if distributed:
        w_out = bg[0].reshape(D_MODEL, D_MODEL)
    (ys5, xr, xi), bg = _s5_fwd(u, *s5_small, jobs=jobs(_job_gather(w_up) if distributed else None))
    if distributed:
        w_up = bg[0]
    (o, states), _ = _ret_fwd(q, k, v)
    (ycat, xhat1, rstd1, h1b, pre), bg = _post_up(o, gate, ys5, xhat0, gn_g, gn_b, li_g, li_b, l1_g, l1_b, w_out, w_up,
                                                  jobs(_job_gather(w_down) if distributed else None))
    if distributed:
        w_down = bg[0].reshape(D_FF, D_MODEL)
    dr2, dffb, loss8, dl2g, dl2b = _post_down(pre, xhat1, tgt, l1_g, l1_b, l2_g, l2_b, w_down)
    g_up, g_down, dh1m = _mlp_bwd(h1b, dffb, pre, w_up, w_down)
    (do, dgate, dys5, dh0r, g_out, dl1g, dl1b, dgng, dgnb), bg = _post_bwd(
        dh1m, dr2, xhat1, rstd1, ycat, o, gate, gn_g, gn_b, l1_g, w_out,
        jobs(*([_job_pair(g_up), _job_pair(g_down)] if distributed else [])))
    g_out = g_out.reshape(N_DEV, D_MODEL // N_DEV, D_MODEL)
    after = jnp.zeros((8, LANE), F32)
    if distributed:
        p_up, p_down = _pair_sum([g_up, g_down], bg, c_arr, "pair_sum_mlp")
        started_mlp, after = _chips_start([p_up, p_down], "chips_mlp_start")
    (du, dlre, dlim, dldt, dbre_t, dbim_t, dcre, dcim, dd, dwglu, dbglu), bg = _s5_bwd(
        dys5, u, xr, xi, *s5_small, after, jobs=jobs(_job_pair(g_out) if distributed else None))
    if distributed:
        (p_out,) = _pair_sum([g_out], bg, c_arr, "pair_sum_out")
    from_t = lambda t: t.reshape(S5_H, S5_G, S5_P).transpose(1, 0, 2)
    small = {
        "s5_lambda_re": dlre, "s5_lambda_im": dlim, "s5_log_dt": dldt[:, :S5_G],
        "s5_b_re": from_t(dbre_t), "s5_b_im": from_t(dbim_t), "s5_c_re": dcre, "s5_c_im": dcim, "s5_d": dd,
        "s5_b_glu": dbglu, "ret_gn_g": dgng, "ret_gn_b": dgnb, "ln1_g": dl1g, "ln1_b": dl1b, "ln2_g": dl2g, "ln2_b": dl2b,
        "s5_w_glu": dwglu, "loss": loss8[0:1, 0:1]}
    early_pack = _pack([small[n] for n in EARLY])
    (dq, dk, dv), bg = _ret_bwd(q, k, v, do, states, cos2, sin2,
                                jobs(*([_job_chips(p_out), _job_gather(early_pack)] if distributed else [])))
    g_int = _in_w_grad(du, dq, dk, dv, dgate, xhat0, li_g, li_b).reshape(N_DEV, PROJ_W // N_DEV, D_MODEL)
    after = jnp.zeros((8, LANE), F32)
    if distributed:
        (r1_in,) = _exchange([_job_pair(g_int)], "exchange_pair_in")
        (p_in,) = _pair_sum([g_int], [r1_in], c_arr, "pair_sum_in")
        (p_up, p_down), (r_up, r_down) = _chips_wait(started_mlp, p_in, "chips_mlp_wait")
        started_in, after = _chips_start([p_in], "chips_in_start")
    grad_x, dmeta, dlig, dlib = _in_bwd(du, dq, dk, dv, dgate, dh0r, xhat0, rstd0, li_g, w_int, after)
    small.update(ln_in_g=dlig, ln_in_b=dlib, meta_tokens=dmeta)
    if distributed:
        (p_in,), (r_in,) = _chips_wait(started_in, dlig, "chips_in_wait")
        big = dict(chip_sums=[p_in, p_out, p_up, p_down], received=[r_in, bg[0], r_up, r_down], early=bg[1])
    else:
        big = dict(partials=[g_int, g_out, g_up, g_down])
    return grad_x, big, small


def kernel(x, meta_tokens, ln_in_g, ln_in_b, w_in, s5_lambda_re, s5_lambda_im, s5_log_dt, s5_b_re, s5_b_im, s5_c_re, s5_c_im, s5_d, s5_w_glu, s5_b_glu, ret_gn_g, ret_gn_b, w_out, ln1_g, ln1_b, w_up, w_down, ln2_g, ln2_b, loss_target, m_meta_tokens, m_ln_in_g, m_ln_in_b, m_w_in, m_s5_lambda_re, m_s5_lambda_im, m_s5_log_dt, m_s5_b_re, m_s5_b_im, m_s5_c_re, m_s5_c_im, m_s5_d, m_s5_w_glu, m_s5_b_glu, m_ret_gn_g, m_ret_gn_b, m_w_out, m_ln1_g, m_ln1_b, m_w_up, m_w_down, m_ln2_g, m_ln2_b, v_meta_tokens, v_ln_in_g, v_ln_in_b, v_w_in, v_s5_lambda_re, v_s5_lambda_im, v_s5_log_dt, v_s5_b_re, v_s5_b_im, v_s5_c_re, v_s5_c_im, v_s5_d, v_s5_w_glu, v_s5_b_glu, v_ret_gn_g, v_ret_gn_b, v_w_out, v_ln1_g, v_ln1_b, v_w_up, v_w_down, v_ln2_g, v_ln2_b):
    args = dict(locals())
    names = ["meta_tokens", "ln_in_g", "ln_in_b", "w_in", "s5_lambda_re", "s5_lambda_im", "s5_log_dt", "s5_b_re", "s5_b_im",
             "s5_c_re", "s5_c_im", "s5_d", "s5_w_glu", "s5_b_glu", "ret_gn_g", "ret_gn_b", "w_out", "ln1_g", "ln1_b",
             "w_up", "w_down", "ln2_g", "ln2_b"]
    ax, ay, ac = _place()
    me = 4 * ax + 2 * ay + ac

    sp = {n: args[n] for n in SMALL}
    grad_x, big, small = _local_step(x[0], loss_target[0], meta_tokens, w_in[0].T.astype(MM), w_out[0].astype(MM),
                                   w_up[0].astype(MM), w_down[0].astype(MM), s5_w_glu[0].astype(MM), sp, True)

    j_arr = jnp.reshape(2 * ax + ay, (1,)).astype(jnp.int32)
    two_d = lambda a: a.reshape(a.shape[-2:])
    item = lambda n, p, r: (p, r, *(two_d(_view(n, a)) for a in (args[n], args["m_" + n], args["v_" + n])))
    late_pack = _pack([small[n] for n in LATE])
    mlp = ("w_out", "w_up", "w_down")
    res, (late_all,) = _adamw_shards(
        [item(n, p, r) for n, p, r in zip(mlp, big["chip_sums"][1:], big["received"][1:])], "adamw_mlp", 8, j_arr,
        [_job_gather(late_pack)])
    res_in, _ = _adamw_shards([item("w_in", big["chip_sums"][0], big["received"][0])], "adamw_in", 2, j_arr)
    upd = {"w_in": res_in}
    for idx, n in enumerate(mlp):
        upd[n] = res[4 * idx:4 * idx + 4]
    shard_grads = {n: upd[n][0] for n in upd}

    early_shapes = [_view(n, args[n]).shape for n in EARLY[:-2]] + [(S5_W, S5_W), (1,)]
    late_shapes = [args["ln_in_g"].shape, args["ln_in_b"].shape, (N_META, D_MODEL)]
    g_small = dict(zip(EARLY, _unpack(_sum_devices(big["early"], "sum_small_early"), early_shapes)))
    g_small.update(zip(LATE, _unpack(_sum_devices(late_all, "sum_small_late"), late_shapes)))
    loss = g_small["loss"].reshape(())

    shard_grads["meta_tokens"] = lax.dynamic_slice(g_small["meta_tokens"], (0, me * (D_MODEL // N_DEV)),
                                                   (N_META, D_MODEL // N_DEV))
    shard_grads["s5_w_glu"] = lax.dynamic_slice(g_small["s5_w_glu"], (me * (S5_W // N_DEV), 0),
                                                (S5_W // N_DEV, S5_W))[None]
    natives = SMALL + ["meta_tokens", "s5_w_glu"]
    res2 = _adamw_native([(shard_grads[n] if n in shard_grads else g_small[n], *(_view(n, args[p + n]) for p in ("", "m_", "v_")))
                          for n in natives], "adamw_small")
    for idx, n in enumerate(natives):
        upd[n] = [shard_grads[n] if n in shard_grads else g_small[n]] + list(res2[3 * idx:3 * idx + 3])

    grads, deltas, new_m, new_v = ([_view(n, upd[n][t]).reshape(args[n].shape) for n in names] for t in range(4))
    return (loss, grad_x[None], *grads, *deltas, *new_m, *new_v)
```

```python
import math

import jax
import jax.numpy as jnp
from jax import lax
from jax.experimental import pallas as pl
from jax.experimental.pallas import tpu as pltpu

F32 = jnp.float32
MM = jnp.bfloat16

D_MODEL = 1024
N_META = 16
CHUNK = 128
PAD = CHUNK - N_META
S5_W, S5_G, S5_H, S5_P = 256, 16, 16, 64
S5_N = S5_G * S5_P
RET_W, RET_H, HEAD = 768, 6, 128
D_FF = 4096
PROJ_W = S5_W + 4 * RET_W
N_DEV = 8
FF_BLK = D_FF // N_DEV
ROW_BLK = 384
MLP_ROWS = 1408
PROJ_ROWS = 704
ALPHA = 2.0 ** 0.25
LN_EPS = 1e-5
GN_EPS = 1e-5
ROPE_BASE = 10000.0
GELU_C = math.sqrt(2.0 / math.pi)
GELU_A = 0.044715
ADAM_LR, ADAM_B1, ADAM_B2, ADAM_EPS, ADAM_WD, ADAM_STEP = 0.001, 0.9, 0.999, 1e-08, 0.01, 10
VMEM_LIMIT = 60 * 1024 * 1024

_VMEM = pl.BlockSpec(memory_space=pltpu.VMEM)
_ANY = pl.BlockSpec(memory_space=pl.ANY)
_MESH = pl.DeviceIdType.MESH


def _params(sem=None):
    return pltpu.CompilerParams(dimension_semantics=sem, vmem_limit_bytes=VMEM_LIMIT)


def _dot(a, b):
    return jnp.dot(a.astype(MM), b.astype(MM), preferred_element_type=F32)


def _dot_nt(a, b):
    return lax.dot_general(a.astype(MM), b.astype(MM), (((1,), (1,)), ((), ())), preferred_element_type=F32)


def _dot_tn(a, b):
    return lax.dot_general(a.astype(MM), b.astype(MM), (((0,), (0,)), ((), ())), preferred_element_type=F32)


def _split3(a):
    hi = a.astype(jnp.bfloat16)
    r1 = a - hi.astype(F32)
    mid = r1.astype(jnp.bfloat16)
    lo = (r1 - mid.astype(F32)).astype(jnp.bfloat16)
    return hi, mid, lo


def _dot_sel_rhs(a, sel):
    s = sel.astype(jnp.bfloat16)
    return sum(jnp.dot(p, s, preferred_element_type=F32) for p in _split3(a))


def _dot_sel_lhs(sel, b):
    s = sel.astype(jnp.bfloat16)
    return sum(jnp.dot(s, p, preferred_element_type=F32) for p in _split3(b))


def _ln_fwd(r, eps):
    mu = jnp.mean(r, axis=-1, keepdims=True)
    xc = r - mu
    var = jnp.mean(xc * xc, axis=-1, keepdims=True)
    rstd = lax.rsqrt(var + eps)
    return xc * rstd, rstd


def _ln_bwd(dxhat, xhat, rstd):
    m1 = jnp.mean(dxhat, axis=-1, keepdims=True)
    m2 = jnp.mean(dxhat * xhat, axis=-1, keepdims=True)
    return rstd * (dxhat - m1 - xhat * m2)


def _colsum(a):
    return jnp.sum(a, axis=0, keepdims=True)


def _shift3(n_in, block=lambda i: i):
    return [pl.BlockSpec((CHUNK, D_MODEL), (lambda i, j=j: (jnp.clip(3 * block(i) - 1 + j, 0, n_in - 1), 0)))
            for j in range(3)]


def _ln_in(x2d, meta, jobs=(), gather_meta=False):
    seq = x2d.shape[0]
    tp = seq + CHUNK
    R = ROW_BLK
    nb = tp // R
    shard_w = D_MODEL // N_DEV

    def body(xa, xb, xc, meta_ref, xhat_ref, rstd_ref, raw_ref, *gathered):
        raw_ref[0:CHUNK, :] = xa[...]
        raw_ref[CHUNK:2 * CHUNK, :] = xb[...]
        raw_ref[2 * CHUNK:3 * CHUNK, :] = xc[...]

        @pl.when(pl.program_id(0) == nb - 1)
        def _():
            raw_ref[0:PAD, :] = jnp.zeros((PAD, D_MODEL), F32)
            if gather_meta:
                for d in range(N_DEV):
                    pltpu.sync_copy(gathered[0].at[d], raw_ref.at[PAD:CHUNK, d * shard_w:(d + 1) * shard_w])
            else:
                raw_ref[PAD:CHUNK, :] = meta_ref[...]

        xhat_ref[...], rstd_ref[...] = _ln_fwd(raw_ref[...], LN_EPS)

    row = lambda w: pl.BlockSpec((R, w), lambda i: (nb - 1 - i, 0))
    jobs = ([_job_gather(meta)] if gather_meta else []) + list(jobs)
    return _call(
        body, "ln_in", (nb,),
        _shift3(seq // CHUNK, lambda i: nb - 1 - i) + [pl.BlockSpec(meta.shape, lambda i: (0, 0))],
        [row(D_MODEL), row(1)], [jax.ShapeDtypeStruct((tp, D_MODEL), F32), jax.ShapeDtypeStruct((tp, 1), F32)],
        [pltpu.VMEM((R, D_MODEL), F32)], (x2d, x2d, x2d, meta), jobs, early=1 if gather_meta else 0)


def _in_proj(xhat0, ln_g, ln_b, w_int, cos2, sin2, jobs=()):
    tp = xhat0.shape[0]
    R = PROJ_ROWS if tp % PROJ_ROWS == 0 else ROW_BLK

    def body(xh_ref, g_ref, b_ref, w_ref, cos_ref, sin_ref, u_ref, q_ref, k_ref, v_ref, gate_ref):
        hb = (xh_ref[...] * g_ref[...] + b_ref[...]).astype(MM)
        valid = (pl.program_id(0) * R + lax.broadcasted_iota(jnp.int32, (R, 1), 0)) >= PAD

        def seg(lo, hi):
            return jnp.where(valid, _dot_nt(hb, w_ref[lo:hi, :]), 0.0)

        u_ref[...] = seg(0, S5_W)
        cos = cos_ref[...]
        sin = sin_ref[...]
        q = seg(S5_W, S5_W + RET_W)
        k = seg(S5_W + RET_W, S5_W + 2 * RET_W)
        for h in range(RET_H):
            sl = slice(h * HEAD, (h + 1) * HEAD)
            qh = q[:, sl]
            kh = k[:, sl]
            q_ref[:, sl] = (qh * cos + pltpu.roll(qh, HEAD // 2, 1) * sin).astype(q_ref.dtype)
            k_ref[:, sl] = ((kh * cos + pltpu.roll(kh, HEAD // 2, 1) * sin) * (HEAD ** -0.5)).astype(k_ref.dtype)
        v_ref[...] = seg(S5_W + 2 * RET_W, S5_W + 3 * RET_W).astype(v_ref.dtype)
        gate_ref[...] = seg(S5_W + 3 * RET_W, PROJ_W)

    def rows(w, dt):
        return pl.BlockSpec((R, w), lambda i: (i, 0)), jax.ShapeDtypeStruct((tp, w), dt)

    outs = [rows(S5_W, F32), rows(RET_W, MM), rows(RET_W, MM), rows(RET_W, MM), rows(RET_W, F32)]
    full = lambda s: pl.BlockSpec(s, lambda i: (0,) * len(s))
    return _call(
        body, "in_proj", (tp // R,),
        [pl.BlockSpec((R, D_MODEL), lambda i: (i, 0)), full((1, D_MODEL)), full((1, D_MODEL)), _VMEM,
         pl.BlockSpec((R, HEAD), lambda i: (i, 0)), pl.BlockSpec((R, HEAD), lambda i: (i, 0))],
        [o[0] for o in outs], [o[1] for o in outs], [], (xhat0, ln_g, ln_b, w_int, cos2, sin2), jobs)


def _s5_disc(lre, lim, ldt, bre_t, bim_t):
    dt = jnp.exp(ldt)
    mag = jnp.exp(lre * dt)
    ang = lim * dt
    lbr = mag * jnp.cos(ang)
    lbi = mag * jnp.sin(ang)
    den = lre * lre + lim * lim
    nr = lbr - 1.0
    qr = (nr * lre + lbi * lim) / den
    qi = (lbi * lre - nr * lim) / den
    return lbr, lbi, qr * bre_t - qi * bim_t, qr * bim_t + qi * bre_t


def _s5_tables(lbr, lbi, reverse):
    if reverse:
        lbi = -lbi
    pw = [(lbr, lbi)]
    for _ in range(7):
        r, i = pw[-1]
        pw.append((r * lbr - i * lbi, r * lbi + i * lbr))
    row = lax.broadcasted_iota(jnp.int32, (8, S5_N), 0)
    tabs = []
    for k in range(3):
        sh = 2 ** k
        mask = (row < 8 - sh) if reverse else (row >= sh)
        ar, ai = pw[sh - 1]
        tabs.append((jnp.where(mask, ar, 0.0), jnp.where(mask, ai, 0.0)))
    pr = jnp.zeros((8, S5_N), F32)
    pi = jnp.zeros((8, S5_N), F32)
    for i in range(8):
        ar, ai = pw[7 - i] if reverse else pw[i]
        pr = jnp.where(row == i, ar, pr)
        pi = jnp.where(row == i, ai, pi)
    tabs.append((pr, pi))
    return tabs


def _store_tables(tab_ref, tabs):
    for k, (r, i) in enumerate(tabs):
        tab_ref[2 * k] = r
        tab_ref[2 * k + 1] = i


def _bd_mask():
    r = lax.broadcasted_iota(jnp.int32, (S5_W, S5_N), 0)
    c = lax.broadcasted_iota(jnp.int32, (S5_W, S5_N), 1)
    return jnp.right_shift(r, 4) == jnp.right_shift(c, 6)


def _s5_block_diag(bbr_t, bbi_t, cre_w, cim_w):
    mask = _bd_mask()
    bd = lambda t: jnp.where(mask, t, 0.0)
    return (bd(jnp.tile(bbr_t, (S5_G, 1))), bd(jnp.tile(bbi_t, (S5_G, 1))),
            bd(jnp.tile(cre_w, (1, S5_N // HEAD))), bd(jnp.tile(cim_w, (1, S5_N // HEAD))))


def _scan8(xr, xi, tab_ref, lanes, reverse):
    for k in range(3):
        sh = (8 - 2 ** k) if reverse else 2 ** k
        sr = pltpu.roll(xr, sh, 0)
        si = pltpu.roll(xi, sh, 0)
        mr = tab_ref[2 * k, :, lanes]
        mi = tab_ref[2 * k + 1, :, lanes]
        xr, xi = xr + (mr * sr - mi * si), xi + (mr * si + mi * sr)
    return xr, xi


S5_LANES = 512


def _gelu(y):
    t = jnp.tanh(GELU_C * (y + GELU_A * y * y * y))
    return 0.5 * y * (1.0 + t), t


def _s5_fwd(u, lre, lim, ldt, bre_t, bim_t, cre_w, cim_w, d_row, w_glu, b_glu, jobs=()):
    tp = u.shape[0]
    R = ROW_BLK

    def body(u_ref, lre_ref, lim_ref, ldt_ref, bre_ref, bim_ref, cre_ref, cim_ref, d_ref, wg_ref, bg_ref,
             y_ref, xr_ref, xi_ref, bbd_r, bbd_i, cbd_r, cbd_i, tab_ref, car_r, car_i):
        @pl.when(pl.program_id(0) == 0)
        def _():
            lbr, lbi, bbr, bbi = _s5_disc(lre_ref[...], lim_ref[...], ldt_ref[...], bre_ref[...], bim_ref[...])
            br, bi, cr, ci = _s5_block_diag(bbr, bbi, cre_ref[...], cim_ref[...])
            bbd_r[...] = br.astype(MM)
            bbd_i[...] = bi.astype(MM)
            cbd_r[...] = cr.astype(MM)
            cbd_i[...] = ci.astype(MM)
            _store_tables(tab_ref, _s5_tables(lbr, lbi, False))
            car_r[...] = jnp.zeros_like(car_r)
            car_i[...] = jnp.zeros_like(car_i)

        u = u_ref[...]
        ub = u.astype(MM)
        xr_ref[...] = jnp.dot(ub, bbd_r[...], preferred_element_type=F32)
        xi_ref[...] = jnp.dot(ub, bbd_i[...], preferred_element_type=F32)
        for j in range(S5_N // S5_LANES):
            lanes = pl.ds(j * S5_LANES, S5_LANES)
            pr = tab_ref[6, :, lanes]
            pi = tab_ref[7, :, lanes]

            def step(g, carry):
                cr, ci = carry
                rows = pl.ds(pl.multiple_of(g * 8, 8), 8)
                xr, xi = _scan8(xr_ref[rows, lanes], xi_ref[rows, lanes], tab_ref, lanes, False)
                br = jnp.broadcast_to(cr[7:8, :], cr.shape)
                bi = jnp.broadcast_to(ci[7:8, :], ci.shape)
                xr = xr + (pr * br - pi * bi)
                xi = xi + (pr * bi + pi * br)
                xr_ref[rows, lanes] = xr
                xi_ref[rows, lanes] = xi
                return xr, xi

            cr, ci = lax.fori_loop(0, R // 8, step, (car_r[:, lanes], car_i[:, lanes]), unroll=2)
            car_r[:, lanes] = cr
            car_i[:, lanes] = ci
        y = _dot_nt(xr_ref[...], cbd_r[...]) - _dot_nt(xi_ref[...], cbd_i[...]) + d_ref[...] * u
        yg, _ = _gelu(y)
        z = _dot(yg, wg_ref[...]) + bg_ref[...]
        y_ref[...] = yg * jax.nn.sigmoid(z)

    full = lambda a: pl.BlockSpec(a.shape, lambda i: (0,) * a.ndim)
    small = [lre, lim, ldt, bre_t, bim_t, cre_w, cim_w, d_row, w_glu, b_glu]
    return _call(
        body, "s5_fwd", (tp // R,),
        [pl.BlockSpec((R, S5_W), lambda i: (i, 0))] + [full(a) for a in small],
        [pl.BlockSpec((R, S5_W), lambda i: (i, 0)), pl.BlockSpec((R, S5_N), lambda i: (i, 0)),
         pl.BlockSpec((R, S5_N), lambda i: (i, 0))],
        [jax.ShapeDtypeStruct((tp, S5_W), F32), jax.ShapeDtypeStruct((tp, S5_N), F32),
         jax.ShapeDtypeStruct((tp, S5_N), F32)],
        [pltpu.VMEM((S5_W, S5_N), MM)] * 4 + [pltpu.VMEM((8, 8, S5_N), F32), pltpu.VMEM((8, S5_N), F32),
                                              pltpu.VMEM((8, S5_N), F32)],
        (u, *small), jobs)


def _s5_bwd(dy_out, u, xr, xi, lre, lim, ldt, bre_t, bim_t, cre_w, cim_w, d_row, w_glu, b_glu, after, jobs=()):
    tp = u.shape[0]
    R = ROW_BLK
    nb = tp // R

    def body(dyo_ref, u_ref, xr_ref, xi_ref, xpr_ref, xpi_ref,
             lre_ref, lim_ref, ldt_ref, bre_ref, bim_ref, cre_ref, cim_ref, d_ref, wg_ref, bg_ref, after_ref,
             du_ref, dlre_ref, dlim_ref, dldt_ref, dbre_ref, dbim_ref, dcre_ref, dcim_ref, dd_ref, dwg_ref, dbg_ref,
             bbd_r, bbd_i, cbd_r, cbd_i, tab_ref, car_r, car_i, gr_ref, gi_ref, xer_ref, xei_ref,
             abr, abi, acr, aci, adr, adi):
        i = pl.program_id(0)

        @pl.when(i == 0)
        def _():
            lbr, lbi, bbr, bbi = _s5_disc(lre_ref[...], lim_ref[...], ldt_ref[...], bre_ref[...], bim_ref[...])
            br, bi, cr, ci = _s5_block_diag(bbr, bbi, cre_ref[...], cim_ref[...])
            bbd_r[...] = br.astype(MM)
            bbd_i[...] = bi.astype(MM)
            cbd_r[...] = cr.astype(MM)
            cbd_i[...] = ci.astype(MM)
            _store_tables(tab_ref, _s5_tables(lbr, lbi, True))
            for ref in (car_r, car_i, abr, abi, acr, aci, adr, adi, dd_ref, dwg_ref, dbg_ref):
                ref[...] = jnp.zeros_like(ref)

        u = u_ref[...]
        xrv = xr_ref[...]
        xiv = xi_ref[...]
        y = _dot_nt(xrv, cbd_r[...]) - _dot_nt(xiv, cbd_i[...]) + d_ref[...] * u
        yg, t = _gelu(y)
        z = _dot(yg, wg_ref[...]) + bg_ref[...]
        s = jax.nn.sigmoid(z)
        dout = dyo_ref[...]
        dz = dout * yg * s * (1.0 - s)
        dyg = dout * s + _dot_nt(dz, wg_ref[...])
        dwg_ref[...] += _dot_tn(yg, dz)
        dbg_ref[...] += _colsum(dz)
        dy = dyg * (0.5 * (1.0 + t) + 0.5 * y * (1.0 - t * t) * GELU_C * (1.0 + 3.0 * GELU_A * y * y))
        dd_ref[...] += _colsum(dy * u)
        acr[...] += _dot_tn(dy, xrv)
        aci[...] -= _dot_tn(dy, xiv)
        gr_ref[...] = _dot(dy, cbd_r[...])
        gi_ref[...] = -_dot(dy, cbd_i[...])
        has_prev = (i < nb - 1).astype(F32)
        xer_ref[0:8, :] = xpr_ref[...] * has_prev
        xei_ref[0:8, :] = xpi_ref[...] * has_prev
        xer_ref[8:R + 8, :] = xrv
        xei_ref[8:R + 8, :] = xiv
        row = lax.broadcasted_iota(jnp.int32, (8, S5_LANES), 0)
        for j in range(S5_N // S5_LANES):
            lanes = pl.ds(j * S5_LANES, S5_LANES)
            pr = tab_ref[6, :, lanes]
            pi = tab_ref[7, :, lanes]

            def step(n, carry):
                cr, ci, sar, sai = carry
                g = R // 8 - 1 - n
                r0 = pl.multiple_of(g * 8, 8)
                rows = pl.ds(r0, 8)
                gr, gi = _scan8(gr_ref[rows, lanes], gi_ref[rows, lanes], tab_ref, lanes, True)
                br = jnp.broadcast_to(cr[0:1, :], cr.shape)
                bi = jnp.broadcast_to(ci[0:1, :], ci.shape)
                gr = gr + (pr * br - pi * bi)
                gi = gi + (pr * bi + pi * br)
                gr_ref[rows, lanes] = gr
                gi_ref[rows, lanes] = gi
                last = row == 7
                xpr = pltpu.roll(jnp.where(last, xer_ref[rows, lanes], xer_ref[pl.ds(r0 + 8, 8), lanes]), 1, 0)
                xpi = pltpu.roll(jnp.where(last, xei_ref[rows, lanes], xei_ref[pl.ds(r0 + 8, 8), lanes]), 1, 0)
                return gr, gi, sar + (gr * xpr + gi * xpi), sai + (gi * xpr - gr * xpi)

            cr, ci, sar, sai = lax.fori_loop(
                0, R // 8, step, (car_r[:, lanes], car_i[:, lanes], adr[:, lanes], adi[:, lanes]), unroll=2)
            car_r[:, lanes] = cr
            car_i[:, lanes] = ci
            adr[:, lanes] = sar
            adi[:, lanes] = sai
        grv = gr_ref[...]
        giv = gi_ref[...]
        du_ref[...] = (dy * d_ref[...] + _dot_nt(grv, bbd_r[...]) + _dot_nt(giv, bbd_i[...])).astype(du_ref.dtype)
        abr[...] += _dot_tn(u, grv)
        abi[...] += _dot_tn(u, giv)

        @pl.when(i == nb - 1)
        def _():
            mask = _bd_mask()
            r16 = lax.broadcasted_iota(jnp.int32, (S5_H, S5_W), 1)
            h16 = lax.broadcasted_iota(jnp.int32, (S5_H, S5_W), 0)
            fold_b = jnp.bitwise_and(r16, S5_H - 1) == h16
            c64 = lax.broadcasted_iota(jnp.int32, (S5_N, S5_P), 0)
            p64 = lax.broadcasted_iota(jnp.int32, (S5_N, S5_P), 1)
            fold_c = jnp.bitwise_and(c64, S5_P - 1) == p64
            dbbr = _dot_sel_lhs(fold_b, jnp.where(mask, abr[...], 0.0))
            dbbi = _dot_sel_lhs(fold_b, jnp.where(mask, abi[...], 0.0))
            dcre_ref[...] = _dot_sel_rhs(jnp.where(mask, acr[...], 0.0), fold_c)
            dcim_ref[...] = _dot_sel_rhs(jnp.where(mask, aci[...], 0.0), fold_c)
            dlbr = _colsum(adr[...])
            dlbi = _colsum(adi[...])
            _, vjp = jax.vjp(_s5_disc, lre_ref[...], lim_ref[...], ldt_ref[...], bre_ref[...], bim_ref[...])
            dlre, dlim, dldt, dbre, dbim = vjp((dlbr, dlbi, dbbr, dbbi))
            dlre_ref[...] = dlre
            dlim_ref[...] = dlim
            dbre_ref[...] = dbre
            dbim_ref[...] = dbim
            gsel = jnp.right_shift(lax.broadcasted_iota(jnp.int32, (S5_N, HEAD), 0), 6) == \
                lax.broadcasted_iota(jnp.int32, (S5_N, HEAD), 1)
            dldt_ref[...] = _dot_sel_rhs(dldt, gsel)

    full = lambda a: pl.BlockSpec(a.shape, lambda i: (0,) * a.ndim)
    rev = lambda w: pl.BlockSpec((R, w), lambda i: (nb - 1 - i, 0))
    prev8 = pl.BlockSpec((8, S5_N), lambda i: (jnp.maximum((nb - 1 - i) * (R // 8) - 1, 0), 0))
    small = [lre, lim, ldt, bre_t, bim_t, cre_w, cim_w, d_row, w_glu, b_glu]
    outs = [((tp, S5_W), rev(S5_W))] + [
        (s, pl.BlockSpec(s, lambda i: (0, 0))) for s in
        [(1, S5_N), (1, S5_N), (1, HEAD), (S5_H, S5_N), (S5_H, S5_N), (S5_W, S5_P), (S5_W, S5_P),
         (1, S5_W), (S5_W, S5_W), (1, S5_W)]]
    return _call(
        body, "s5_bwd", (nb,),
        [rev(S5_W), rev(S5_W), rev(S5_N), rev(S5_N), prev8, prev8] + [full(a) for a in small + [after]],
        [o[1] for o in outs], [jax.ShapeDtypeStruct(o[0], MM if n == 0 else F32) for n, o in enumerate(outs)],
        [pltpu.VMEM((S5_W, S5_N), MM)] * 4 + [
            pltpu.VMEM((8, 8, S5_N), F32), pltpu.VMEM((8, S5_N), F32), pltpu.VMEM((8, S5_N), F32),
            pltpu.VMEM((R, S5_N), F32), pltpu.VMEM((R, S5_N), F32),
            pltpu.VMEM((R + 8, S5_N), F32), pltpu.VMEM((R + 8, S5_N), F32)] + [pltpu.VMEM((S5_W, S5_N), F32)] * 4 + [
            pltpu.VMEM((8, S5_N), F32), pltpu.VMEM((8, S5_N), F32)],
        (dy_out, u, xr, xi, xr, xi, *small, after), jobs)


RET_CHUNK = ROW_BLK
LOG_GAMMA = [math.log1p(-2.0 ** (-5 - h)) for h in range(RET_H)]
GAMMA_CHUNK = [math.exp(RET_CHUNK * lg) for lg in LOG_GAMMA]
_DECAY_SCRATCH = [pltpu.VMEM((RET_H, RET_CHUNK, RET_CHUNK), F32), pltpu.VMEM((RET_H, RET_CHUNK, HEAD), F32),
                  pltpu.VMEM((RET_H, RET_CHUNK, HEAD), F32)]


def _fill_decay(dm_ref, ze_ref, xi_ref):
    C = RET_CHUNK
    diff = (lax.broadcasted_iota(jnp.int32, (C, C), 0) - lax.broadcasted_iota(jnp.int32, (C, C), 1)).astype(F32)
    r = lax.broadcasted_iota(jnp.int32, (C, HEAD), 0).astype(F32)
    for h, lg in enumerate(LOG_GAMMA):
        dm_ref[h] = jnp.where(diff >= 0.0, jnp.exp(jnp.maximum(diff, 0.0) * lg), 0.0)
        ze_ref[h] = jnp.exp((C - 1.0 - r) * lg)
        xi_ref[h] = jnp.exp((r + 1.0) * lg)


def _ret_fwd(q, k, v, jobs=()):
    tp = q.shape[0]
    C = RET_CHUNK
    nc = tp // C

    def body(q_ref, k_ref, v_ref, o_ref, st_ref, s_ref, dm_ref, ze_ref, xi_ref):
        @pl.when(pl.program_id(0) == 0)
        def _():
            s_ref[...] = jnp.zeros_like(s_ref)
            _fill_decay(dm_ref, ze_ref, xi_ref)

        for h in range(RET_H):
            sl = slice(h * HEAD, (h + 1) * HEAD)
            qh, kh, vh = q_ref[:, sl], k_ref[:, sl], v_ref[:, sl]
            sh = s_ref[h]
            st_ref[0, sl, :] = sh
            scores = _dot_nt(qh, kh) * dm_ref[h]
            o_ref[:, sl] = _dot(scores, vh) + _dot(qh, sh) * xi_ref[h]
            s_ref[h] = GAMMA_CHUNK[h] * sh + _dot_tn(kh.astype(F32) * ze_ref[h], vh)

    blk = pl.BlockSpec((C, RET_W), lambda c: (c, 0))
    return _call(
        body, "ret_fwd", (nc,), [blk, blk, blk], [blk, pl.BlockSpec((1, RET_W, HEAD), lambda c: (c, 0, 0))],
        [jax.ShapeDtypeStruct((tp, RET_W), F32), jax.ShapeDtypeStruct((nc, RET_W, HEAD), F32)],
        [pltpu.VMEM((RET_H, HEAD, HEAD), F32)] + _DECAY_SCRATCH, (q, k, v), jobs)


def _ret_bwd(q, k, v, do, states, cos2, sin2, jobs=()):
    tp = q.shape[0]
    C = RET_CHUNK
    nc = tp // C

    def body(q_ref, k_ref, v_ref, do_ref, st_ref, cos_ref, sin_ref,
             dq_ref, dk_ref, dv_ref, ds_ref, dm_ref, ze_ref, xi_ref):
        @pl.when(pl.program_id(0) == 0)
        def _():
            ds_ref[...] = jnp.zeros_like(ds_ref)
            _fill_decay(dm_ref, ze_ref, xi_ref)

        cos = cos_ref[...]
        sin = sin_ref[...]
        for h in range(RET_H):
            sl = slice(h * HEAD, (h + 1) * HEAD)
            qh, kh, vh = q_ref[:, sl], k_ref[:, sl], v_ref[:, sl]
            dmh = dm_ref[h]
            sh = st_ref[0, sl, :]
            dsn = ds_ref[h]
            doh = do_ref[:, sl]
            dox = doh * xi_ref[h]
            a = _dot_nt(qh, kh) * dmh
            dqk = _dot_nt(doh, vh) * dmh
            kz = kh.astype(F32) * ze_ref[h]
            dv_ref[:, sl] = (_dot_tn(a, doh) + _dot(kz, dsn)).astype(dv_ref.dtype)
            dqr = _dot(dqk, kh) + _dot_nt(dox, sh)
            dkr = _dot_tn(dqk, qh) + ze_ref[h] * _dot_nt(vh, dsn)
            ds_ref[h] = GAMMA_CHUNK[h] * dsn + _dot_tn(qh, dox)
            dq_ref[:, sl] = (dqr * cos - pltpu.roll(dqr, HEAD // 2, 1) * sin).astype(dq_ref.dtype)
            dk_ref[:, sl] = ((dkr * cos - pltpu.roll(dkr, HEAD // 2, 1) * sin) * (HEAD ** -0.5)).astype(dk_ref.dtype)

    blk = pl.BlockSpec((C, RET_W), lambda c: (nc - 1 - c, 0))
    tab = pl.BlockSpec((C, HEAD), lambda c: (nc - 1 - c, 0))
    return _call(
        body, "ret_bwd", (nc,),
        [blk, blk, blk, blk, pl.BlockSpec((1, RET_W, HEAD), lambda c: (nc - 1 - c, 0, 0)), tab, tab],
        [blk, blk, blk], [jax.ShapeDtypeStruct((tp, RET_W), MM)] * 3,
        [pltpu.VMEM((RET_H, HEAD, HEAD), F32)] + _DECAY_SCRATCH, (q, k, v, do, states, cos2, sin2), jobs)


def _gn_gate(o, gate, gn_g, gn_b):
    xhat, rstd = _ln_fwd(o, GN_EPS)
    on = xhat * gn_g + gn_b
    s = jax.nn.sigmoid(gate)
    return gate * s * on, xhat, rstd, on, s


def _post_up(o, gate, ys5, xhat0, gn_g, gn_b, li_g, li_b, l1_g, l1_b, w_out, w_up, jobs=()):
    tp = o.shape[0]
    R = ROW_BLK

    def body(o_ref, g_ref, ys_ref, xh0_ref, gng, gnb, lig, lib, l1g, l1b, wo_ref, wu_ref,
             ycat_ref, xh1_ref, rstd1_ref, h1b_ref, pre_ref):
        ycat_ref[:, 0:S5_W] = ys_ref[...].astype(ycat_ref.dtype)
        for h in range(RET_H):
            sl = slice(h * HEAD, (h + 1) * HEAD)
            yret = _gn_gate(o_ref[:, sl], g_ref[:, sl], gng[:, sl], gnb[:, sl])[0]
            ycat_ref[:, S5_W + h * HEAD:S5_W + (h + 1) * HEAD] = yret.astype(ycat_ref.dtype)
        mixed = _dot(ycat_ref[...], wo_ref[...])
        h0 = xh0_ref[...] * lig[...] + lib[...]
        xh1, rstd1 = _ln_fwd(ALPHA * h0 + mixed, LN_EPS)
        xh1_ref[...] = xh1
        rstd1_ref[...] = rstd1
        h1b = (xh1 * l1g[...] + l1b[...]).astype(MM)
        h1b_ref[...] = h1b
        for d in range(N_DEV):
            pre_ref[:, d * FF_BLK:(d + 1) * FF_BLK] = jnp.maximum(_dot(h1b, wu_ref[d]), 0.0)

    row = lambda w: pl.BlockSpec((R, w), lambda i: (i, 0))
    full = lambda a: pl.BlockSpec(a.shape, lambda i: (0,) * a.ndim)
    vecs = [gn_g, gn_b, li_g, li_b, l1_g, l1_b]
    outs = [(row(D_MODEL), jax.ShapeDtypeStruct((tp, D_MODEL), MM)), (row(D_MODEL), jax.ShapeDtypeStruct((tp, D_MODEL), F32)),
            (row(1), jax.ShapeDtypeStruct((tp, 1), F32)), (row(D_MODEL), jax.ShapeDtypeStruct((tp, D_MODEL), MM)),
            (row(D_FF), jax.ShapeDtypeStruct((tp, D_FF), F32))]
    return _call(
        body, "post_up", (tp // R,),
        [row(RET_W), row(RET_W), row(S5_W), row(D_MODEL)] + [full(a) for a in vecs] + [_VMEM, _VMEM],
        [o[0] for o in outs], [o[1] for o in outs], [], (o, gate, ys5, xhat0, *vecs, w_out, w_up), jobs)


def _post_down(pre, xhat1, tgt, l1_g, l1_b, l2_g, l2_b, w_down):
    tp = pre.shape[0]
    seq = tgt.shape[0]
    R = ROW_BLK

    def body(pre_ref, xh1_ref, ta, tb, tc, l1g, l1b, l2g, l2b, wd_ref,
             dr2_ref, dffb_ref, loss_ref, dl2g_ref, dl2b_ref, tgt_ref):
        i = pl.program_id(0)

        @pl.when(i == 0)
        def _():
            for ref in (loss_ref, dl2g_ref, dl2b_ref):
                ref[...] = jnp.zeros_like(ref)

        tgt_ref[0:CHUNK, :] = ta[...]
        tgt_ref[CHUNK:2 * CHUNK, :] = tb[...]
        tgt_ref[2 * CHUNK:3 * CHUNK, :] = tc[...]
        ff = jnp.zeros((R, D_MODEL), F32)
        for d in range(N_DEV):
            pre = pre_ref[:, d * FF_BLK:(d + 1) * FF_BLK]
            ff = ff + _dot(pre * pre, wd_ref[d * FF_BLK:(d + 1) * FF_BLK, :])
        h1 = xh1_ref[...] * l1g[...] + l1b[...]
        xh2, rstd2 = _ln_fwd(ALPHA * h1 + ff, LN_EPS)
        h2 = xh2 * l2g[...] + l2b[...]
        valid = (i * R + lax.broadcasted_iota(jnp.int32, (R, 1), 0)) >= CHUNK
        err = jnp.where(valid, h2 - tgt_ref[...], 0.0)
        loss_ref[...] += 0.5 * jnp.sum(err * err) / D_MODEL
        dh2 = err * (1.0 / D_MODEL)
        dl2g_ref[...] += _colsum(dh2 * xh2)
        dl2b_ref[...] += _colsum(dh2)
        dr2 = _ln_bwd(dh2 * l2g[...], xh2, rstd2)
        dr2_ref[...] = dr2
        dffb_ref[...] = dr2.astype(MM)

    row = lambda w: pl.BlockSpec((R, w), lambda i: (i, 0))
    full = lambda a: pl.BlockSpec(a.shape, lambda i: (0,) * a.ndim)
    vecs = [l1_g, l1_b, l2_g, l2_b]
    acc = lambda s: (pl.BlockSpec(s, lambda i: (0, 0)), jax.ShapeDtypeStruct(s, F32))
    outs = [(row(D_MODEL), jax.ShapeDtypeStruct((tp, D_MODEL), F32)), (row(D_MODEL), jax.ShapeDtypeStruct((tp, D_MODEL), MM)),
            acc((8, HEAD)), acc((1, D_MODEL)), acc((1, D_MODEL))]
    return pl.pallas_call(
        body, name="post_down", grid=(tp // R,),
        in_specs=[row(D_FF), row(D_MODEL)] + _shift3(seq // CHUNK) + [full(a) for a in vecs] + [_VMEM],
        out_specs=[o[0] for o in outs], out_shape=[o[1] for o in outs],
        scratch_shapes=[pltpu.VMEM((R, D_MODEL), F32)],
        compiler_params=_params(("arbitrary",)),
    )(pre, xhat1, tgt, tgt, tgt, *vecs, w_down)


def _mlp_bwd(h1b, dffb, pre, w_up, w_down):
    tp = h1b.shape[0]
    R = MLP_ROWS if tp % MLP_ROWS == 0 else ROW_BLK
    nr = tp // R

    def body(h_ref, df_ref, pre_ref, wu_ref, wd_ref, gup_ref, gdn_ref, dh1_ref, aup, adn):
        d = pl.program_id(0)
        r = pl.program_id(1)

        @pl.when(r == 0)
        def _():
            aup[...] = jnp.zeros_like(aup)
            adn[...] = jnp.zeros_like(adn)

        h = h_ref[...]
        df = df_ref[...]
        wu = wu_ref[0]
        wd = wd_ref[0]
        pre = pre_ref[...]
        dpre = (_dot_nt(df, wd) * (2.0 * pre)).astype(MM)

        aup[...] += _dot_tn(h, dpre)
        adn[...] += _dot_tn(pre * pre, df)
        contrib = _dot_nt(dpre, wu)
        rows = pl.ds(pl.multiple_of(r * R, 64), R)

        @pl.when(d == 0)
        def _():
            dh1_ref[rows, :] = contrib

        @pl.when(d > 0)
        def _():
            dh1_ref[rows, :] += contrib

        @pl.when(r == nr - 1)
        def _():
            gup_ref[0] = aup[...].astype(gup_ref.dtype)
            gdn_ref[0] = adn[...].astype(gdn_ref.dtype)

    return pl.pallas_call(
        body, name="mlp_bwd", grid=(N_DEV, nr),
        in_specs=[pl.BlockSpec((R, D_MODEL), lambda d, r: (r, 0)), pl.BlockSpec((R, D_MODEL), lambda d, r: (r, 0)),
                  pl.BlockSpec((R, FF_BLK), lambda d, r: (r, d)),
                  pl.BlockSpec((1, D_MODEL, FF_BLK), lambda d, r: (d, 0, 0)),
                  pl.BlockSpec((1, FF_BLK, D_MODEL), lambda d, r: (d, 0, 0))],
        out_specs=[pl.BlockSpec((1, D_MODEL, FF_BLK), lambda d, r: (d, 0, 0)),
                   pl.BlockSpec((1, FF_BLK, D_MODEL), lambda d, r: (d, 0, 0)), _VMEM],
        out_shape=[jax.ShapeDtypeStruct((N_DEV, D_MODEL, FF_BLK), MM), jax.ShapeDtypeStruct((N_DEV, FF_BLK, D_MODEL), MM),
                   jax.ShapeDtypeStruct((tp, D_MODEL), F32)],
        scratch_shapes=[pltpu.VMEM((D_MODEL, FF_BLK), F32), pltpu.VMEM((FF_BLK, D_MODEL), F32)],
        compiler_params=_params(("arbitrary", "arbitrary")),
    )(h1b, dffb, pre, w_up, w_down.reshape(N_DEV, FF_BLK, D_MODEL))


def _post_bwd(dh1m, dr2, xhat1, rstd1, ycat, o, gate, gn_g, gn_b, l1_g, w_out, jobs=()):
    tp = o.shape[0]
    R = ROW_BLK
    nb = tp // R

    def body(dm_ref, dr2_ref, xh1_ref, rs1_ref, yc_ref, o_ref, g_ref, gng, gnb, l1g, wo_ref,
             do_ref, dg_ref, dys_ref, dh0_ref, gwo_ref, dl1g_ref, dl1b_ref, dgng_ref, dgnb_ref, awo):
        i = pl.program_id(0)

        @pl.when(i == 0)
        def _():
            for ref in (awo, dl1g_ref, dl1b_ref, dgng_ref, dgnb_ref):
                ref[...] = jnp.zeros_like(ref)

        dh1 = dm_ref[...] + ALPHA * dr2_ref[...]
        xh1 = xh1_ref[...]
        dl1g_ref[...] += _colsum(dh1 * xh1)
        dl1b_ref[...] += _colsum(dh1)
        dr1 = _ln_bwd(dh1 * l1g[...], xh1, rs1_ref[...])
        dh0_ref[...] = ALPHA * dr1
        dmix = dr1.astype(MM)
        awo[...] += _dot_tn(yc_ref[...], dmix)
        dyc = _dot_nt(dmix, wo_ref[...])
        dys_ref[...] = dyc[:, 0:S5_W]
        for h in range(RET_H):
            sl = slice(h * HEAD, (h + 1) * HEAD)
            gt = g_ref[:, sl]
            _, xhat, rstd, on, s = _gn_gate(o_ref[:, sl], gt, gng[:, sl], gnb[:, sl])
            dyr = dyc[:, S5_W + h * HEAD:S5_W + (h + 1) * HEAD]
            dg_ref[:, sl] = (dyr * on * (s * (1.0 + gt * (1.0 - s)))).astype(dg_ref.dtype)
            don = dyr * gt * s
            dgng_ref[:, sl] += _colsum(don * xhat)
            dgnb_ref[:, sl] += _colsum(don)
            do_ref[:, sl] = _ln_bwd(don * gng[:, sl], xhat, rstd)

        @pl.when(i == nb - 1)
        def _():
            gwo_ref[...] = awo[...].astype(gwo_ref.dtype)

    row = lambda w: pl.BlockSpec((R, w), lambda i: (i, 0))
    full = lambda a: pl.BlockSpec(a.shape, lambda i: (0,) * a.ndim)
    acc = lambda s, dt=F32: (pl.BlockSpec(s, lambda i: (0, 0)), jax.ShapeDtypeStruct(s, dt))
    outs = [(row(RET_W), jax.ShapeDtypeStruct((tp, RET_W), F32)), (row(RET_W), jax.ShapeDtypeStruct((tp, RET_W), MM)),
            (row(S5_W), jax.ShapeDtypeStruct((tp, S5_W), F32)), (row(D_MODEL), jax.ShapeDtypeStruct((tp, D_MODEL), F32)),
            acc((D_MODEL, D_MODEL), MM), acc((1, D_MODEL)), acc((1, D_MODEL)), acc((1, RET_W)), acc((1, RET_W))]
    return _call(
        body, "post_bwd", (nb,),
        [row(D_MODEL), row(D_MODEL), row(D_MODEL), row(1), row(D_MODEL), row(RET_W), row(RET_W),
         full(gn_g), full(gn_b), full(l1_g), _VMEM],
        [o[0] for o in outs], [o[1] for o in outs],
        [pltpu.VMEM((D_MODEL, D_MODEL), F32)],
        (dh1m, dr2, xhat1, rstd1, ycat, o, gate, gn_g, gn_b, l1_g, w_out), jobs)


_PROJ_SEGS = [(0, S5_W)] + [(S5_W + n * RET_W, S5_W + (n + 1) * RET_W) for n in range(4)]


def _in_w_grad(du, dq, dk, dv, dg, xhat0, li_g, li_b):
    tp = du.shape[0]
    R = PROJ_ROWS if tp % PROJ_ROWS == 0 else ROW_BLK
    nb = tp // R

    def body(du_ref, dq_ref, dk_ref, dv_ref, dg_ref, xh_ref, lig, lib, gw_ref, aw):
        i = pl.program_id(0)

        @pl.when(i == 0)
        def _():
            aw[...] = jnp.zeros_like(aw)

        valid = (i * R + lax.broadcasted_iota(jnp.int32, (R, 1), 0)) >= PAD
        hb = (xh_ref[...] * lig[...] + lib[...]).astype(MM)
        for (lo, hi), ref in zip(_PROJ_SEGS, (du_ref, dq_ref, dk_ref, dv_ref, dg_ref)):
            aw[lo:hi, :] += _dot_tn(jnp.where(valid, ref[...], 0.0).astype(MM), hb)

        @pl.when(i == nb - 1)
        def _():
            gw_ref[...] = aw[...].astype(gw_ref.dtype)

    row = lambda w: pl.BlockSpec((R, w), lambda i: (i, 0))
    full = lambda a: pl.BlockSpec(a.shape, lambda i: (0,) * a.ndim)
    (gw,), _ = _call(
        body, "in_w_grad", (nb,),
        [row(S5_W), row(RET_W), row(RET_W), row(RET_W), row(RET_W), row(D_MODEL), full(li_g), full(li_b)],
        [pl.BlockSpec((PROJ_W, D_MODEL), lambda i: (0, 0))], [jax.ShapeDtypeStruct((PROJ_W, D_MODEL), MM)],
        [pltpu.VMEM((PROJ_W, D_MODEL), F32)], (du, dq, dk, dv, dg, xhat0, li_g, li_b))
    return gw


def _in_bwd(du, dq, dk, dv, dg, dh0r, xhat0, rstd0, li_g, w_int, after):
    tp = du.shape[0]
    R = PROJ_ROWS if tp % PROJ_ROWS == 0 else ROW_BLK
    nb = tp // R
    segs = _PROJ_SEGS

    def body(du_ref, dq_ref, dk_ref, dv_ref, dg_ref, dh0r_ref, xh_ref, rs_ref, lig, w_ref, after_ref,
             gx_ref, dmeta_ref, dlg_ref, dlb_ref, stage, out_sems):
        i = pl.program_id(0)
        slot = i % 2

        def to_gx(step_slot, first):
            if first:
                return pltpu.make_async_copy(stage.at[0, CHUNK:R, :], gx_ref.at[0:R - CHUNK, :], out_sems.at[0])
            return pltpu.make_async_copy(stage.at[step_slot], gx_ref.at[pl.ds(i * R - CHUNK, R), :], out_sems.at[step_slot])

        @pl.when(i == 0)
        def _():
            for ref in (dlg_ref, dlb_ref):
                ref[...] = jnp.zeros_like(ref)

        @pl.when(i >= 3)
        def _():
            to_gx(slot, False).wait()

        valid = (i * R + lax.broadcasted_iota(jnp.int32, (R, 1), 0)) >= PAD
        xh = xh_ref[...]
        dh0 = dh0r_ref[...]
        for (lo, hi), ref in zip(segs, (du_ref, dq_ref, dk_ref, dv_ref, dg_ref)):
            dh0 = dh0 + _dot(jnp.where(valid, ref[...], 0.0).astype(MM), w_ref[lo:hi, :])
        dlg_ref[...] += _colsum(dh0 * xh)
        dlb_ref[...] += _colsum(dh0)
        draw = _ln_bwd(dh0 * lig[...], xh, rs_ref[...])
        stage[slot] = draw

        @pl.when(i == 0)
        def _():
            dmeta_ref[...] = draw[PAD:CHUNK, :]
            first = to_gx(0, True)
            first.start()
            first.wait()

        @pl.when(i > 0)
        def _():
            to_gx(slot, False).start()

        @pl.when(i == nb - 1)
        def _():
            for back in (1, 0):
                if nb - 1 - back >= 1:
                    to_gx((nb - 1 - back) % 2, False).wait()

    row = lambda w: pl.BlockSpec((R, w), lambda i: (i, 0))
    full = lambda a: pl.BlockSpec(a.shape, lambda i: (0,) * a.ndim)
    acc = lambda s, dt=F32: (pl.BlockSpec(s, lambda i: (0, 0)), jax.ShapeDtypeStruct(s, dt))
    outs = [(_ANY, jax.ShapeDtypeStruct((tp - CHUNK, D_MODEL), F32)), acc((N_META, D_MODEL)),
            acc((1, D_MODEL)), acc((1, D_MODEL))]
    return _call(
        body, "in_bwd", (nb,),
        [row(S5_W), row(RET_W), row(RET_W), row(RET_W), row(RET_W), row(D_MODEL), row(D_MODEL), row(1),
         full(li_g), _VMEM, full(after)],
        [o[0] for o in outs], [o[1] for o in outs],
        [pltpu.VMEM((2, R, D_MODEL), F32), pltpu.SemaphoreType.DMA((2,))],
        (du, dq, dk, dv, dg, dh0r, xhat0, rstd0, li_g, w_int, after))[0]


def _place():
    return lax.axis_index("x"), lax.axis_index("y"), lax.axis_index("c")


def _dma_sems(n):
    return pltpu.SemaphoreType.DMA((n,))


def _job_gather(shard):
    def parts(ins, outs, sems):
        (src,), (out,), (send_sems, recv_sems, local_sem) = ins, outs, sems
        x, y, c = _place()
        north = c == 1
        me, sib = (x, y, c), (x, y, 1 - c)
        xn, yn, dg = (1 - x, y, c), (x, 1 - y, c), (1 - x, 1 - y, c)
        relay_from = (jnp.where(north, 1 - x, x), jnp.where(north, y, 1 - y), c)
        relay_to = (jnp.where(north, x, 1 - x), jnp.where(north, 1 - y, y), c)

        def slot(dev):
            return out.at[4 * dev[0] + 2 * dev[1] + dev[2]]

        def copy(k, block, to, from_input=False):
            return pltpu.make_async_remote_copy(
                src_ref=src if from_input else slot(block), dst_ref=slot(block),
                send_sem=send_sems.at[k], recv_sem=recv_sems.at[k], device_id=to, device_id_type=_MESH)

        mine = lambda: pltpu.make_async_copy(src, slot(me), local_sem.at[0])
        first = lambda: [copy(0, me, sib, True), copy(1, me, xn, True), copy(2, me, yn, True)]
        relayed = lambda: [copy(3, relay_from, relay_to), copy(4, xn, sib), copy(5, yn, sib)]
        return me, sib, xn, yn, dg, copy, mine, first, relayed

    def start(ins, outs, sems):
        mine, first = parts(ins, outs, sems)[6:8]
        mine().start()
        for cp in first():
            cp.start()

    def relay(ins, outs, sems):
        me, sib, xn, yn, dg, copy, mine, first, relayed = parts(ins, outs, sems)
        copy(1, xn, me).wait_recv()
        copy(2, yn, me).wait_recv()
        for cp in relayed():
            cp.start()

    def finish(ins, outs, sems):
        me, sib, xn, yn, dg, copy, mine, first, relayed = parts(ins, outs, sems)
        other = 1 - me[2]
        copy(3, dg, me).wait_recv()
        last = copy(6, dg, sib)
        last.start()
        copy(0, sib, me).wait_recv()
        for k, chip in ((4, xn), (5, yn), (6, dg)):
            copy(k, (chip[0], chip[1], other), me).wait_recv()
        for cp in first() + relayed() + [last]:
            cp.wait_send()
        mine().wait()

    return dict(ins=[shard], outs=[jax.ShapeDtypeStruct((N_DEV,) + shard.shape, shard.dtype)],
                sems=[_dma_sems(7), _dma_sems(7), _dma_sems(1)], start=start, middle=relay, finish=finish)


def _job_pair(g):
    def copies(ins, outs, sems):
        x, y, c = _place()
        return [pltpu.make_async_remote_copy(
            src_ref=ins[0].at[2 * j + (1 - c)], dst_ref=outs[0].at[j], send_sem=sems[0].at[j], recv_sem=sems[1].at[j],
            device_id=(x, y, 1 - c), device_id_type=_MESH) for j in range(4)]

    def start(ins, outs, sems):
        for cp in copies(ins, outs, sems):
            cp.start()

    def finish(ins, outs, sems):
        for cp in copies(ins, outs, sems):
            cp.wait()

    return dict(ins=[g], outs=[jax.ShapeDtypeStruct((4,) + g.shape[1:], g.dtype)], sems=[_dma_sems(4), _dma_sems(4)],
                start=start, finish=finish)


def _job_chips(p):
    def copies(ins, outs, sems):
        x, y, c = _place()
        chips = [(1 - x, y), (x, 1 - y), (1 - x, 1 - y)]
        return [pltpu.make_async_remote_copy(
            src_ref=ins[0].at[2 * chip[0] + chip[1]], dst_ref=outs[0].at[k], send_sem=sems[0].at[k],
            recv_sem=sems[1].at[k], device_id=(*chip, c), device_id_type=_MESH) for k, chip in enumerate(chips)]

    def start(ins, outs, sems):
        for cp in copies(ins, outs, sems):
            cp.start()

    def finish(ins, outs, sems):
        for cp in copies(ins, outs, sems):
            cp.wait()

    return dict(ins=[p], outs=[jax.ShapeDtypeStruct((3,) + p.shape[1:], p.dtype)], sems=[_dma_sems(3), _dma_sems(3)],
                start=start, finish=finish)


_HBM = pl.BlockSpec(memory_space=pltpu.HBM)
_SEM = pl.BlockSpec(memory_space=pltpu.SEMAPHORE)
_ORDERED = pltpu.CompilerParams(has_side_effects=pltpu.SideEffectType.DATAFLOW_SIDE_EFFECTING)


def _chip_copies(p_ref, land_ref, sems):
    x, y, c = _place()
    chips = [(1 - x, y), (x, 1 - y), (1 - x, 1 - y)]
    return [pltpu.make_async_remote_copy(
        src_ref=p_ref.at[2 * chip[0] + chip[1]], dst_ref=land_ref.at[k], send_sem=sems[k], recv_sem=sems[3 + k],
        device_id=(*chip, c), device_id_type=_MESH) for k, chip in enumerate(chips)]


def _chips_start(ps, name):
    n = len(ps)

    def body(*refs):
        sems = refs[2 * n:8 * n]
        for a in range(n):
            for cp in _chip_copies(refs[a], refs[n + a], sems[6 * a:6 * a + 6]):
                cp.start()
        refs[-1][...] = jnp.zeros_like(refs[-1])

    lands = [(3,) + p.shape[1:] for p in ps]
    hbm = lambda arr: pltpu.with_memory_space_constraint(arr, pltpu.HBM)
    outs = pl.pallas_call(
        body, name=name,
        out_shape=(*[pltpu.SemaphoreType.DMA(())] * (6 * n), *[pltpu.HBM(p.shape, p.dtype) for p in ps],
                   *[pltpu.HBM(s, p.dtype) for s, p in zip(lands, ps)], jax.ShapeDtypeStruct((8, LANE), F32)),
        in_specs=[_HBM] * (2 * n), out_specs=(*[_SEM] * (6 * n), *[_HBM] * (2 * n), _VMEM),
        input_output_aliases={a: 6 * n + a for a in range(2 * n)}, compiler_params=_ORDERED,
    )(*[hbm(p) for p in ps], *[hbm(lax.empty(s, p.dtype)) for s, p in zip(lands, ps)])
    return (list(outs[:6 * n]), list(outs[6 * n:7 * n]), list(outs[7 * n:8 * n])), outs[8 * n]


def _chips_wait(started, after, name):
    sems, thrus, lands = started
    n = len(thrus)

    def body(*refs):
        for a in range(n):
            for cp in _chip_copies(refs[a], refs[n + a], refs[2 * n + 6 * a:2 * n + 6 * a + 6]):
                cp.wait_send()
                cp.wait_recv()

    outs = pl.pallas_call(
        body, name=name, out_shape=[pltpu.HBM(t.shape, t.dtype) for t in thrus + lands],
        in_specs=(*[_HBM] * (2 * n), *[_SEM] * (6 * n), _ANY), out_specs=[_HBM] * (2 * n),
        input_output_aliases={a: a for a in range(2 * n)}, compiler_params=_ORDERED,
    )(*thrus, *lands, *sems, after)
    return list(outs[:n]), list(outs[n:])


def _split_job_refs(jobs, ins, outs, sems):
    res, a, b, c = [], 0, 0, 0
    for job in jobs:
        na, nb, nc = len(job["ins"]), len(job["outs"]), len(job["sems"])
        res.append((ins[a:a + na], outs[b:b + nb], sems[c:c + nc]))
        a, b, c = a + na, b + nb, c + nc
    return res


def _call(body, name, grid, in_specs, out_specs, out_shape, scratch, args, jobs=(), prefetch=None, early=0):
    jobs = list(jobs)
    n_in, n_out, n_scr = len(in_specs), len(out_specs), len(scratch)
    j_in = [a for job in jobs for a in job["ins"]]
    j_out = [o for job in jobs for o in job["outs"]]
    j_scr = [s for job in jobs for s in job["sems"]]
    nsteps = grid[0]
    n_pre = 0 if prefetch is None else 1

    def wrapped(*refs):
        pre, refs = refs[:n_pre], refs[n_pre:]
        ins, jins = refs[:n_in], refs[n_in:n_in + len(j_in)]
        refs = refs[n_in + len(j_in):]
        outs, jouts = refs[:n_out], refs[n_out:n_out + len(j_out)]
        refs = refs[n_out + len(j_out):]
        scr, jscr = refs[:n_scr], refs[n_scr:]
        per_job = _split_job_refs(jobs, jins, jouts, jscr)

        def middle():
            for job, r in zip(jobs, per_job):
                if "middle" in job:
                    job["middle"](*r)

        @pl.when(pl.program_id(0) == 0)
        def _():
            for job, r in zip(jobs, per_job):
                job["start"](*r)

        if nsteps >= 3:
            pl.when(pl.program_id(0) == nsteps // 2)(middle)

        if early:
            @pl.when(pl.program_id(0) == nsteps - 1)
            def _():
                for job, r in zip(jobs[:early], per_job[:early]):
                    job["finish"](*r)

        body(*pre, *ins, *outs, *scr, *[o for r in per_job[:early] for o in r[1]])

        @pl.when(pl.program_id(0) == nsteps - 1)
        def _():
            if nsteps < 3:
                middle()
            for job, r in zip(jobs[early:], per_job[early:]):
                job["finish"](*r)

    specs = dict(in_specs=list(in_specs) + [_ANY] * len(j_in), out_specs=list(out_specs) + [_ANY] * len(j_out),
                 scratch_shapes=list(scratch) + j_scr)
    if n_pre:
        specs = dict(grid_spec=pltpu.PrefetchScalarGridSpec(num_scalar_prefetch=1, grid=grid, **specs))
    else:
        specs["grid"] = grid
    res = pl.pallas_call(
        wrapped if jobs else body, name=name, out_shape=list(out_shape) + j_out,
        compiler_params=_params(("arbitrary",) * len(grid)), **specs,
    )(*([prefetch] if n_pre else []), *args, *j_in)
    return list(res[:n_out]), list(res[n_out:])


def _exchange(jobs, name):
    j_in = [a for job in jobs for a in job["ins"]]
    j_out = [o for job in jobs for o in job["outs"]]
    j_scr = [s for job in jobs for s in job["sems"]]

    def body(*refs):
        per_job = _split_job_refs(jobs, refs[:len(j_in)], refs[len(j_in):len(j_in) + len(j_out)],
                                  refs[len(j_in) + len(j_out):])
        for phase in ("start", "middle", "finish"):
            for job, r in zip(jobs, per_job):
                if phase in job:
                    job[phase](*r)

    return pl.pallas_call(body, name=name, out_shape=j_out, in_specs=[_ANY] * len(j_in), out_specs=[_ANY] * len(j_out),
                          scratch_shapes=j_scr)(*j_in)


def _pair_sum(gs, r1s, c_arr, name):
    n = len(gs)

    def body(c_ref, *refs):
        for a in range(n):
            refs[2 * n + a][...] = (refs[a][...].astype(F32) + refs[n + a][...].astype(F32)).astype(refs[2 * n + a].dtype)

    def blk(g, own):
        s = g.shape[1:]
        if own:
            return pl.BlockSpec((1,) + s, lambda j, c_ref: (2 * j + c_ref[0],) + (0,) * len(s))
        return pl.BlockSpec((1,) + s, lambda j, c_ref: (j,) + (0,) * len(s))

    return pl.pallas_call(
        body, name=name,
        grid_spec=pltpu.PrefetchScalarGridSpec(
            num_scalar_prefetch=1, grid=(4,),
            in_specs=[blk(g, True) for g in gs] + [blk(g, False) for g in gs],
            out_specs=[blk(g, False) for g in gs]),
        out_shape=[jax.ShapeDtypeStruct((4,) + g.shape[1:], g.dtype) for g in gs],
        compiler_params=_params(("arbitrary",)),
    )(c_arr, *gs, *r1s)


def _adamw_math(w, g, m, v):
    m = ADAM_B1 * m + (1.0 - ADAM_B1) * g
    v = ADAM_B2 * v + (1.0 - ADAM_B2) * (g * g)
    m_hat = m / (1.0 - ADAM_B1 ** ADAM_STEP)
    v_hat = v / (1.0 - ADAM_B2 ** ADAM_STEP)
    return -ADAM_LR * (m_hat / (jnp.sqrt(v_hat) + ADAM_EPS) + ADAM_WD * w), m, v


def _view(name, a):
    return jnp.swapaxes(a, -1, -2) if name in ("w_in", "s5_b_re", "s5_b_im") else a


def _adamw_shards(items, name, steps, chip, jobs=()):
    n = len(items)

    def body(chip_ref, *refs):
        for a in range(n):
            p_ref, r_ref, w_ref, m_ref, v_ref = refs[5 * a:5 * a + 5]
            g = ((p_ref[0].astype(F32) + r_ref[0].astype(F32)) + r_ref[1].astype(F32)) + r_ref[2].astype(F32)
            outs = refs[5 * n + 4 * a:5 * n + 4 * a + 4]
            outs[0][...] = g
            outs[1][...], outs[2][...], outs[3][...] = _adamw_math(w_ref[...], g, m_ref[...], v_ref[...])

    in_specs, out_specs, out_shape, flat = [], [], [], []
    for p, r, w, m, v in items:
        rows, cols = w.shape
        rb = rows // steps
        in_specs += [pl.BlockSpec((1, rb, cols), lambda i, c: (c[0], i, 0)), pl.BlockSpec((3, rb, cols), lambda i, c: (0, i, 0))]
        wblk = pl.BlockSpec((rb, cols), lambda i, c: (i, 0))
        in_specs += [wblk] * 3
        out_specs += [wblk] * 4
        out_shape += [jax.ShapeDtypeStruct(w.shape, F32)] * 4
        flat += [p, r, w, m, v]
    return _call(body, name, (steps,), in_specs, out_specs, out_shape, [], flat, jobs, prefetch=chip)


def _sum_devices(gathered, name):
    def body(gs_ref, g_ref):
        g = gs_ref[0]
        for s in range(1, N_DEV):
            g = g + gs_ref[s]
        g_ref[...] = g

    return pl.pallas_call(body, name=name, out_shape=jax.ShapeDtypeStruct(gathered.shape[1:], F32),
                          in_specs=[_VMEM], out_specs=_VMEM, compiler_params=_params())(gathered)


def _adamw_native(items, name):
    n = len(items)

    def body(*refs):
        for a in range(n):
            g, w, m, v = (refs[4 * a + t][...] for t in range(4))
            refs[4 * n + 3 * a][...], refs[4 * n + 3 * a + 1][...], refs[4 * n + 3 * a + 2][...] = _adamw_math(w, g, m, v)

    return pl.pallas_call(
        body, name=name, out_shape=[jax.ShapeDtypeStruct(it[1].shape, F32) for it in items for _ in range(3)],
        in_specs=[_VMEM] * (4 * n), out_specs=[_VMEM] * (3 * n), compiler_params=_params(),
    )(*[t for it in items for t in it])


SMALL = ["ln_in_g", "ln_in_b", "s5_lambda_re", "s5_lambda_im", "s5_log_dt", "s5_b_re", "s5_b_im", "s5_c_re", "s5_c_im",
         "s5_d", "s5_b_glu", "ret_gn_g", "ret_gn_b", "ln1_g", "ln1_b", "ln2_g", "ln2_b"]
LATE = ["ln_in_g", "ln_in_b", "meta_tokens"]
EARLY = [n for n in SMALL if n not in LATE] + ["s5_w_glu", "loss"]
LANE = 128


def _pack(arrs):
    parts = []
    for a in arrs:
        f = a.reshape(-1)
        parts.append(jnp.pad(f, (0, (-f.shape[0]) % LANE)))
    flat = jnp.concatenate(parts)
    rows = -(-flat.shape[0] // LANE)
    flat = jnp.pad(flat, (0, (-rows % 8) * LANE + rows * LANE - flat.shape[0]))
    return flat.reshape(-1, LANE)


def _unpack(packed, shapes):
    flat = packed.reshape(-1)
    out, off = [], 0
    for s in shapes:
        n = math.prod(s)
        out.append(flat[off:off + n].reshape(s))
        off += n + (-n) % LANE
    return out


def _rope_tables(tp):
    inv_freq = 1.0 / (ROPE_BASE ** (jnp.arange(0, HEAD, 2, dtype=F32) / HEAD))
    blk = (jnp.arange(tp // ROW_BLK, dtype=F32) * ROW_BLK)[:, None, None] * inv_freq
    off = (jnp.arange(ROW_BLK, dtype=F32) - float(PAD))[None, :, None] * inv_freq
    cos = (jnp.cos(blk) * jnp.cos(off) - jnp.sin(blk) * jnp.sin(off)).reshape(tp, HEAD // 2)
    sin = (jnp.sin(blk) * jnp.cos(off) + jnp.cos(blk) * jnp.sin(off)).reshape(tp, HEAD // 2)
    return jnp.concatenate([cos, cos], axis=1), jnp.concatenate([-sin, sin], axis=1)


def _local_step(x2d, tgt, meta, w_int, w_out, w_up, w_down, w_glu, sp, distributed):
    tp = x2d.shape[0] + CHUNK
    row = lambda a: a.reshape(1, -1)
    cos2, sin2 = _rope_tables(tp)
    li_g, li_b = row(sp["ln_in_g"]), row(sp["ln_in_b"])
    l1_g, l1_b, l2_g, l2_b = row(sp["ln1_g"]), row(sp["ln1_b"]), row(sp["ln2_g"]), row(sp["ln2_b"])
    gn_g, gn_b = row(sp["ret_gn_g"]), row(sp["ret_gn_b"])
    lre, lim = row(sp["s5_lambda_re"]), row(sp["s5_lambda_im"])
    ldt = row(jnp.repeat(sp["s5_log_dt"].reshape(-1), S5_P))
    to_t = lambda b: b.reshape(S5_G, S5_P, S5_H).transpose(2, 0, 1).reshape(S5_H, S5_N)
    bre_t, bim_t = to_t(sp["s5_b_re"]), to_t(sp["s5_b_im"])
    to_w = lambda c: jnp.tile(c.reshape(S5_W, S5_P), (1, 2))
    cre_w, cim_w = to_w(sp["s5_c_re"]), to_w(sp["s5_c_im"])

    jobs = (lambda *j: list(j)) if distributed else (lambda *j: [])
    c_arr = jnp.reshape(lax.axis_index("c"), (1,)).astype(jnp.int32) if distributed else None
    (xhat0, rstd0), bg = _ln_in(x2d, meta, jobs(*([_job_gather(w_int), _job_gather(w_glu)] if distributed else [])),
                                gather_meta=distributed)
    if distributed:
        w_int, w_glu = bg[1].reshape(PROJ_W, D_MODEL), bg[2].reshape(S5_W, S5_W)
    s5_small = (lre, lim, ldt, bre_t, bim_t, cre_w, cim_w, row(sp["s5_d"]), w_glu, row(sp["s5_b_glu"]))
    (u, q, k, v, gate), bg = _in_proj(xhat0, li_g, li_b, w_int, cos2, sin2,
                                      jobs(_job_gather(w_out) if distributed else None))
    if distributed:
        w_out = bg[0].reshape(D_MODEL, D_MODEL)
    (ys5, xr, xi), bg = _s5_fwd(u, *s5_small, jobs=jobs(_job_gather(w_up) if distributed else None))
    if distributed:
        w_up = bg[0]
    (o, states), _ = _ret_fwd(q, k, v)
    (ycat, xhat1, rstd1, h1b, pre), bg = _post_up(o, gate, ys5, xhat0, gn_g, gn_b, li_g, li_b, l1_g, l1_b, w_out, w_up,
                                                  jobs(_job_gather(w_down) if distributed else None))
    if distributed:
        w_down = bg[0].reshape(D_FF, D_MODEL)
    dr2, dffb, loss8, dl2g, dl2b = _post_down(pre, xhat1, tgt, l1_g, l1_b, l2_g, l2_b, w_down)
    g_up, g_down, dh1m = _mlp_bwd(h1b, dffb, pre, w_up, w_down)
    (do, dgate, dys5, dh0r, g_out, dl1g, dl1b, dgng, dgnb), bg = _post_bwd(
        dh1m, dr2, xhat1, rstd1, ycat, o, gate, gn_g, gn_b, l1_g, w_out,
        jobs(*([_job_pair(g_up), _job_pair(g_down)] if distributed else [])))
    g_out = g_out.reshape(N_DEV, D_MODEL // N_DEV, D_MODEL)
    after = jnp.zeros((8, LANE), F32)
    if distributed:
        p_up, p_down = _pair_sum([g_up, g_down], bg, c_arr, "pair_sum_mlp")
        started_mlp, after = _chips_start([p_up, p_down], "chips_mlp_start")
    (du, dlre, dlim, dldt, dbre_t, dbim_t, dcre, dcim, dd, dwglu, dbglu), bg = _s5_bwd(
        dys5, u, xr, xi, *s5_small, after, jobs=jobs(_job_pair(g_out) if distributed else None))
    if distributed:
        (p_out,) = _pair_sum([g_out], bg, c_arr, "pair_sum_out")
    from_t = lambda t: t.reshape(S5_H, S5_G, S5_P).transpose(1, 0, 2)
    small = {
        "s5_lambda_re": dlre, "s5_lambda_im": dlim, "s5_log_dt": dldt[:, :S5_G],
        "s5_b_re": from_t(dbre_t), "s5_b_im": from_t(dbim_t), "s5_c_re": dcre, "s5_c_im": dcim, "s5_d": dd,
        "s5_b_glu": dbglu, "ret_gn_g": dgng, "ret_gn_b": dgnb, "ln1_g": dl1g, "ln1_b": dl1b, "ln2_g": dl2g, "ln2_b": dl2b,
        "s5_w_glu": dwglu, "loss": loss8[0:1, 0:1]}
    early_pack = _pack([small[n] for n in EARLY])
    (dq, dk, dv), bg = _ret_bwd(q, k, v, do, states, cos2, sin2,
                                jobs(*([_job_chips(p_out), _job_gather(early_pack)] if distributed else [])))
    g_int = _in_w_grad(du, dq, dk, dv, dgate, xhat0, li_g, li_b).reshape(N_DEV, PROJ_W // N_DEV, D_MODEL)
    after = jnp.zeros((8, LANE), F32)
    if distributed:
        (r1_in,) = _exchange([_job_pair(g_int)], "exchange_pair_in")
        (p_in,) = _pair_sum([g_int], [r1_in], c_arr, "pair_sum_in")
        (p_up, p_down), (r_up, r_down) = _chips_wait(started_mlp, p_in, "chips_mlp_wait")
        started_in, after = _chips_start([p_in], "chips_in_start")
    grad_x, dmeta, dlig, dlib = _in_bwd(du, dq, dk, dv, dgate, dh0r, xhat0, rstd0, li_g, w_int, after)
    small.update(ln_in_g=dlig, ln_in_b=dlib, meta_tokens=dmeta)
    if distributed:
        (p_in,), (r_in,) = _chips_wait(started_in, dlig, "chips_in_wait")
        big = dict(chip_sums=[p_in, p_out, p_up, p_down], received=[r_in, bg[0], r_up, r_down], early=bg[1])
    else:
        big = dict(partials=[g_int, g_out, g_up, g_down])
    return grad_x, big, small


def kernel(x, meta_tokens, ln_in_g, ln_in_b, w_in, s5_lambda_re, s5_lambda_im, s5_log_dt, s5_b_re, s5_b_im, s5_c_re, s5_c_im, s5_d, s5_w_glu, s5_b_glu, ret_gn_g, ret_gn_b, w_out, ln1_g, ln1_b, w_up, w_down, ln2_g, ln2_b, loss_target, m_meta_tokens, m_ln_in_g, m_ln_in_b, m_w_in, m_s5_lambda_re, m_s5_lambda_im, m_s5_log_dt, m_s5_b_re, m_s5_b_im, m_s5_c_re, m_s5_c_im, m_s5_d, m_s5_w_glu, m_s5_b_glu, m_ret_gn_g, m_ret_gn_b, m_w_out, m_ln1_g, m_ln1_b, m_w_up, m_w_down, m_ln2_g, m_ln2_b, v_meta_tokens, v_ln_in_g, v_ln_in_b, v_w_in, v_s5_lambda_re, v_s5_lambda_im, v_s5_log_dt, v_s5_b_re, v_s5_b_im, v_s5_c_re, v_s5_c_im, v_s5_d, v_s5_w_glu, v_s5_b_glu, v_ret_gn_g, v_ret_gn_b, v_w_out, v_ln1_g, v_ln1_b, v_w_up, v_w_down, v_ln2_g, v_ln2_b):
    args = dict(locals())
    names = ["meta_tokens", "ln_in_g", "ln_in_b", "w_in", "s5_lambda_re", "s5_lambda_im", "s5_log_dt", "s5_b_re", "s5_b_im",
             "s5_c_re", "s5_c_im", "s5_d", "s5_w_glu", "s5_b_glu", "ret_gn_g", "ret_gn_b", "w_out", "ln1_g", "ln1_b",
             "w_up", "w_down", "ln2_g", "ln2_b"]
    ax, ay, ac = _place()
    me = 4 * ax + 2 * ay + ac

    sp = {n: args[n] for n in SMALL}
    grad_x, big, small = _local_step(x[0], loss_target[0], meta_tokens, w_in[0].T.astype(MM), w_out[0].astype(MM),
                                   w_up[0].astype(MM), w_down[0].astype(MM), s5_w_glu[0].astype(MM), sp, True)

    j_arr = jnp.reshape(2 * ax + ay, (1,)).astype(jnp.int32)
    two_d = lambda a: a.reshape(a.shape[-2:])
    item = lambda n, p, r: (p, r, *(two_d(_view(n, a)) for a in (args[n], args["m_" + n], args["v_" + n])))
    late_pack = _pack([small[n] for n in LATE])
    mlp = ("w_out", "w_up", "w_down")
    (late_all,) = _exchange([_job_gather(late_pack)], "gather_small_late")
    res, _ = _adamw_shards(
        [item(n, p, r) for n, p, r in zip(mlp, big["chip_sums"][1:], big["received"][1:])], "adamw_mlp", 8, j_arr)
    res_in, _ = _adamw_shards([item("w_in", big["chip_sums"][0], big["received"][0])], "adamw_in", 2, j_arr)
    upd = {"w_in": res_in}
    for idx, n in enumerate(mlp):
        upd[n] = res[4 * idx:4 * idx + 4]
    shard_grads = {n: upd[n][0] for n in upd}

    early_shapes = [_view(n, args[n]).shape for n in EARLY[:-2]] + [(S5_W, S5_W), (1,)]
    late_shapes = [args["ln_in_g"].shape, args["ln_in_b"].shape, (N_META, D_MODEL)]
    g_small = dict(zip(EARLY, _unpack(_sum_devices(big["early"], "sum_small_early"), early_shapes)))
    g_small.update(zip(LATE, _unpack(_sum_devices(late_all, "sum_small_late"), late_shapes)))
    loss = g_small["loss"].reshape(())

    shard_grads["meta_tokens"] = lax.dynamic_slice(g_small["meta_tokens"], (0, me * (D_MODEL // N_DEV)),
                                                   (N_META, D_MODEL // N_DEV))
    shard_grads["s5_w_glu"] = lax.dynamic_slice(g_small["s5_w_glu"], (me * (S5_W // N_DEV), 0),
                                                (S5_W // N_DEV, S5_W))[None]
    natives = SMALL + ["meta_tokens", "s5_w_glu"]
    res2 = _adamw_native([(shard_grads[n] if n in shard_grads else g_small[n], *(_view(n, args[p + n]) for p in ("", "m_", "v_")))
                          for n in natives], "adamw_small")
    for idx, n in enumerate(natives):
        upd[n] = [shard_grads[n] if n in shard_grads else g_small[n]] + list(res2[3 * idx:3 * idx + 3])

    grads, deltas, new_m, new_v = ([_view(n, upd[n][t]).reshape(args[n].shape) for n in names] for t in range(4))
    return (loss, grad_x[None], *grads, *deltas, *new_m, *new_v)
```

```python
import math

import jax
import jax.numpy as jnp
from jax import lax
from jax.experimental import pallas as pl
from jax.experimental.pallas import tpu as pltpu

F32 = jnp.float32
MM = jnp.bfloat16

D_MODEL = 1024
N_META = 16
CHUNK = 128
PAD = CHUNK - N_META
S5_W, S5_G, S5_H, S5_P = 256, 16, 16, 64
S5_N = S5_G * S5_P
RET_W, RET_H, HEAD = 768, 6, 128
D_FF = 4096
PROJ_W = S5_W + 4 * RET_W
N_DEV = 8
FF_BLK = D_FF // N_DEV
ROW_BLK = 384
MLP_ROWS = 1408
PROJ_ROWS = 704
ALPHA = 2.0 ** 0.25
LN_EPS = 1e-5
GN_EPS = 1e-5
ROPE_BASE = 10000.0
GELU_C = math.sqrt(2.0 / math.pi)
GELU_A = 0.044715
ADAM_LR, ADAM_B1, ADAM_B2, ADAM_EPS, ADAM_WD, ADAM_STEP = 0.001, 0.9, 0.999, 1e-08, 0.01, 10
VMEM_LIMIT = 60 * 1024 * 1024

_VMEM = pl.BlockSpec(memory_space=pltpu.VMEM)
_ANY = pl.BlockSpec(memory_space=pl.ANY)
_MESH = pl.DeviceIdType.MESH


def _params(sem=None):
    return pltpu.CompilerParams(dimension_semantics=sem, vmem_limit_bytes=VMEM_LIMIT)


def _dot(a, b):
    return jnp.dot(a.astype(MM), b.astype(MM), preferred_element_type=F32)


def _dot_nt(a, b):
    return lax.dot_general(a.astype(MM), b.astype(MM), (((1,), (1,)), ((), ())), preferred_element_type=F32)


def _dot_tn(a, b):
    return lax.dot_general(a.astype(MM), b.astype(MM), (((0,), (0,)), ((), ())), preferred_element_type=F32)


def _split3(a):
    hi = a.astype(jnp.bfloat16)
    r1 = a - hi.astype(F32)
    mid = r1.astype(jnp.bfloat16)
    lo = (r1 - mid.astype(F32)).astype(jnp.bfloat16)
    return hi, mid, lo


def _dot_sel_rhs(a, sel):
    s = sel.astype(jnp.bfloat16)
    return sum(jnp.dot(p, s, preferred_element_type=F32) for p in _split3(a))


def _dot_sel_lhs(sel, b):
    s = sel.astype(jnp.bfloat16)
    return sum(jnp.dot(s, p, preferred_element_type=F32) for p in _split3(b))


def _ln_fwd(r, eps):
    mu = jnp.mean(r, axis=-1, keepdims=True)
    xc = r - mu
    var = jnp.mean(xc * xc, axis=-1, keepdims=True)
    rstd = lax.rsqrt(var + eps)
    return xc * rstd, rstd


def _ln_bwd(dxhat, xhat, rstd):
    m1 = jnp.mean(dxhat, axis=-1, keepdims=True)
    m2 = jnp.mean(dxhat * xhat, axis=-1, keepdims=True)
    return rstd * (dxhat - m1 - xhat * m2)


def _colsum(a):
    return jnp.sum(a, axis=0, keepdims=True)


def _shift3(n_in, block=lambda i: i):
    return [pl.BlockSpec((CHUNK, D_MODEL), (lambda i, j=j: (jnp.clip(3 * block(i) - 1 + j, 0, n_in - 1), 0)))
            for j in range(3)]


def _ln_in(x2d, meta, jobs=(), gather_meta=False):
    seq = x2d.shape[0]
    tp = seq + CHUNK
    R = ROW_BLK
    nb = tp // R
    shard_w = D_MODEL // N_DEV

    def body(xa, xb, xc, meta_ref, xhat_ref, rstd_ref, raw_ref, *gathered):
        raw_ref[0:CHUNK, :] = xa[...]
        raw_ref[CHUNK:2 * CHUNK, :] = xb[...]
        raw_ref[2 * CHUNK:3 * CHUNK, :] = xc[...]

        @pl.when(pl.program_id(0) == nb - 1)
        def _():
            raw_ref[0:PAD, :] = jnp.zeros((PAD, D_MODEL), F32)
            if gather_meta:
                for d in range(N_DEV):
                    pltpu.sync_copy(gathered[0].at[d], raw_ref.at[PAD:CHUNK, d * shard_w:(d + 1) * shard_w])
            else:
                raw_ref[PAD:CHUNK, :] = meta_ref[...]

        xhat_ref[...], rstd_ref[...] = _ln_fwd(raw_ref[...], LN_EPS)

    row = lambda w: pl.BlockSpec((R, w), lambda i: (nb - 1 - i, 0))
    jobs = ([_job_gather(meta)] if gather_meta else []) + list(jobs)
    return _call(
        body, "ln_in", (nb,),
        _shift3(seq // CHUNK, lambda i: nb - 1 - i) + [pl.BlockSpec(meta.shape, lambda i: (0, 0))],
        [row(D_MODEL), row(1)], [jax.ShapeDtypeStruct((tp, D_MODEL), F32), jax.ShapeDtypeStruct((tp, 1), F32)],
        [pltpu.VMEM((R, D_MODEL), F32)], (x2d, x2d, x2d, meta), jobs, early=1 if gather_meta else 0)


def _in_proj(xhat0, ln_g, ln_b, w_int, cos2, sin2, jobs=()):
    tp = xhat0.shape[0]
    R = PROJ_ROWS if tp % PROJ_ROWS == 0 else ROW_BLK

    def body(xh_ref, g_ref, b_ref, w_ref, cos_ref, sin_ref, u_ref, q_ref, k_ref, v_ref, gate_ref):
        hb = (xh_ref[...] * g_ref[...] + b_ref[...]).astype(MM)
        valid = (pl.program_id(0) * R + lax.broadcasted_iota(jnp.int32, (R, 1), 0)) >= PAD

        def seg(lo, hi):
            return jnp.where(valid, _dot_nt(hb, w_ref[lo:hi, :]), 0.0)

        u_ref[...] = seg(0, S5_W)
        cos = cos_ref[...]
        sin = sin_ref[...]
        q = seg(S5_W, S5_W + RET_W)
        k = seg(S5_W + RET_W, S5_W + 2 * RET_W)
        for h in range(RET_H):
            sl = slice(h * HEAD, (h + 1) * HEAD)
            qh = q[:, sl]
            kh = k[:, sl]
            q_ref[:, sl] = (qh * cos + pltpu.roll(qh, HEAD // 2, 1) * sin).astype(q_ref.dtype)
            k_ref[:, sl] = ((kh * cos + pltpu.roll(kh, HEAD // 2, 1) * sin) * (HEAD ** -0.5)).astype(k_ref.dtype)
        v_ref[...] = seg(S5_W + 2 * RET_W, S5_W + 3 * RET_W).astype(v_ref.dtype)
        gate_ref[...] = seg(S5_W + 3 * RET_W, PROJ_W)

    def rows(w, dt):
        return pl.BlockSpec((R, w), lambda i: (i, 0)), jax.ShapeDtypeStruct((tp, w), dt)

    outs = [rows(S5_W, F32), rows(RET_W, MM), rows(RET_W, MM), rows(RET_W, MM), rows(RET_W, F32)]
    full = lambda s: pl.BlockSpec(s, lambda i: (0,) * len(s))
    return _call(
        body, "in_proj", (tp // R,),
        [pl.BlockSpec((R, D_MODEL), lambda i: (i, 0)), full((1, D_MODEL)), full((1, D_MODEL)), _VMEM,
         pl.BlockSpec((R, HEAD), lambda i: (i, 0)), pl.BlockSpec((R, HEAD), lambda i: (i, 0))],
        [o[0] for o in outs], [o[1] for o in outs], [], (xhat0, ln_g, ln_b, w_int, cos2, sin2), jobs)


def _s5_disc(lre, lim, ldt, bre_t, bim_t):
    dt = jnp.exp(ldt)
    mag = jnp.exp(lre * dt)
    ang = lim * dt
    lbr = mag * jnp.cos(ang)
    lbi = mag * jnp.sin(ang)
    den = lre * lre + lim * lim
    nr = lbr - 1.0
    qr = (nr * lre + lbi * lim) / den
    qi = (lbi * lre - nr * lim) / den
    return lbr, lbi, qr * bre_t - qi * bim_t, qr * bim_t + qi * bre_t


def _s5_tables(lbr, lbi, reverse):
    if reverse:
        lbi = -lbi
    pw = [(lbr, lbi)]
    for _ in range(7):
        r, i = pw[-1]
        pw.append((r * lbr - i * lbi, r * lbi + i * lbr))
    row = lax.broadcasted_iota(jnp.int32, (8, S5_N), 0)
    tabs = []
    for k in range(3):
        sh = 2 ** k
        mask = (row < 8 - sh) if reverse else (row >= sh)
        ar, ai = pw[sh - 1]
        tabs.append((jnp.where(mask, ar, 0.0), jnp.where(mask, ai, 0.0)))
    pr = jnp.zeros((8, S5_N), F32)
    pi = jnp.zeros((8, S5_N), F32)
    for i in range(8):
        ar, ai = pw[7 - i] if reverse else pw[i]
        pr = jnp.where(row == i, ar, pr)
        pi = jnp.where(row == i, ai, pi)
    tabs.append((pr, pi))
    return tabs


def _store_tables(tab_ref, tabs):
    for k, (r, i) in enumerate(tabs):
        tab_ref[2 * k] = r
        tab_ref[2 * k + 1] = i


def _bd_mask():
    r = lax.broadcasted_iota(jnp.int32, (S5_W, S5_N), 0)
    c = lax.broadcasted_iota(jnp.int32, (S5_W, S5_N), 1)
    return jnp.right_shift(r, 4) == jnp.right_shift(c, 6)


def _s5_block_diag(bbr_t, bbi_t, cre_w, cim_w):
    mask = _bd_mask()
    bd = lambda t: jnp.where(mask, t, 0.0)
    return (bd(jnp.tile(bbr_t, (S5_G, 1))), bd(jnp.tile(bbi_t, (S5_G, 1))),
            bd(jnp.tile(cre_w, (1, S5_N // HEAD))), bd(jnp.tile(cim_w, (1, S5_N // HEAD))))


def _scan8(xr, xi, tab_ref, lanes, reverse):
    for k in range(3):
        sh = (8 - 2 ** k) if reverse else 2 ** k
        sr = pltpu.roll(xr, sh, 0)
        si = pltpu.roll(xi, sh, 0)
        mr = tab_ref[2 * k, :, lanes]
        mi = tab_ref[2 * k + 1, :, lanes]
        xr, xi = xr + (mr * sr - mi * si), xi + (mr * si + mi * sr)
    return xr, xi


S5_LANES = 512


def _gelu(y):
    t = jnp.tanh(GELU_C * (y + GELU_A * y * y * y))
    return 0.5 * y * (1.0 + t), t


def _s5_fwd(u, lre, lim, ldt, bre_t, bim_t, cre_w, cim_w, d_row, w_glu, b_glu, jobs=()):
    tp = u.shape[0]
    R = ROW_BLK

    def body(u_ref, lre_ref, lim_ref, ldt_ref, bre_ref, bim_ref, cre_ref, cim_ref, d_ref, wg_ref, bg_ref,
             y_ref, xr_ref, xi_ref, bbd_r, bbd_i, cbd_r, cbd_i, tab_ref, car_r, car_i):
        @pl.when(pl.program_id(0) == 0)
        def _():
            lbr, lbi, bbr, bbi = _s5_disc(lre_ref[...], lim_ref[...], ldt_ref[...], bre_ref[...], bim_ref[...])
            br, bi, cr, ci = _s5_block_diag(bbr, bbi, cre_ref[...], cim_ref[...])
            bbd_r[...] = br.astype(MM)
            bbd_i[...] = bi.astype(MM)
            cbd_r[...] = cr.astype(MM)
            cbd_i[...] = ci.astype(MM)
            _store_tables(tab_ref, _s5_tables(lbr, lbi, False))
            car_r[...] = jnp.zeros_like(car_r)
            car_i[...] = jnp.zeros_like(car_i)

        u = u_ref[...]
        ub = u.astype(MM)
        xr_ref[...] = jnp.dot(ub, bbd_r[...], preferred_element_type=F32)
        xi_ref[...] = jnp.dot(ub, bbd_i[...], preferred_element_type=F32)
        for j in range(S5_N // S5_LANES):
            lanes = pl.ds(j * S5_LANES, S5_LANES)
            pr = tab_ref[6, :, lanes]
            pi = tab_ref[7, :, lanes]

            def step(g, carry):
                cr, ci = carry
                rows = pl.ds(pl.multiple_of(g * 8, 8), 8)
                xr, xi = _scan8(xr_ref[rows, lanes], xi_ref[rows, lanes], tab_ref, lanes, False)
                br = jnp.broadcast_to(cr[7:8, :], cr.shape)
                bi = jnp.broadcast_to(ci[7:8, :], ci.shape)
                xr = xr + (pr * br - pi * bi)
                xi = xi + (pr * bi + pi * br)
                xr_ref[rows, lanes] = xr
                xi_ref[rows, lanes] = xi
                return xr, xi

            cr, ci = lax.fori_loop(0, R // 8, step, (car_r[:, lanes], car_i[:, lanes]), unroll=2)
            car_r[:, lanes] = cr
            car_i[:, lanes] = ci
        y = _dot_nt(xr_ref[...], cbd_r[...]) - _dot_nt(xi_ref[...], cbd_i[...]) + d_ref[...] * u
        yg, _ = _gelu(y)
        z = _dot(yg, wg_ref[...]) + bg_ref[...]
        y_ref[...] = yg * jax.nn.sigmoid(z)

    full = lambda a: pl.BlockSpec(a.shape, lambda i: (0,) * a.ndim)
    small = [lre, lim, ldt, bre_t, bim_t, cre_w, cim_w, d_row, w_glu, b_glu]
    return _call(
        body, "s5_fwd", (tp // R,),
        [pl.BlockSpec((R, S5_W), lambda i: (i, 0))] + [full(a) for a in small],
        [pl.BlockSpec((R, S5_W), lambda i: (i, 0)), pl.BlockSpec((R, S5_N), lambda i: (i, 0)),
         pl.BlockSpec((R, S5_N), lambda i: (i, 0))],
        [jax.ShapeDtypeStruct((tp, S5_W), F32), jax.ShapeDtypeStruct((tp, S5_N), F32),
         jax.ShapeDtypeStruct((tp, S5_N), F32)],
        [pltpu.VMEM((S5_W, S5_N), MM)] * 4 + [pltpu.VMEM((8, 8, S5_N), F32), pltpu.VMEM((8, S5_N), F32),
                                              pltpu.VMEM((8, S5_N), F32)],
        (u, *small), jobs)


def _s5_bwd(dy_out, u, xr, xi, lre, lim, ldt, bre_t, bim_t, cre_w, cim_w, d_row, w_glu, b_glu, after, jobs=()):
    tp = u.shape[0]
    R = ROW_BLK
    nb = tp // R

    def body(dyo_ref, u_ref, xr_ref, xi_ref, xpr_ref, xpi_ref,
             lre_ref, lim_ref, ldt_ref, bre_ref, bim_ref, cre_ref, cim_ref, d_ref, wg_ref, bg_ref, after_ref,
             du_ref, dlre_ref, dlim_ref, dldt_ref, dbre_ref, dbim_ref, dcre_ref, dcim_ref, dd_ref, dwg_ref, dbg_ref,
             bbd_r, bbd_i, cbd_r, cbd_i, tab_ref, car_r, car_i, gr_ref, gi_ref, xer_ref, xei_ref,
             abr, abi, acr, aci, adr, adi):
        i = pl.program_id(0)

        @pl.when(i == 0)
        def _():
            lbr, lbi, bbr, bbi = _s5_disc(lre_ref[...], lim_ref[...], ldt_ref[...], bre_ref[...], bim_ref[...])
            br, bi, cr, ci = _s5_block_diag(bbr, bbi, cre_ref[...], cim_ref[...])
            bbd_r[...] = br.astype(MM)
            bbd_i[...] = bi.astype(MM)
            cbd_r[...] = cr.astype(MM)
            cbd_i[...] = ci.astype(MM)
            _store_tables(tab_ref, _s5_tables(lbr, lbi, True))
            for ref in (car_r, car_i, abr, abi, acr, aci, adr, adi, dd_ref, dwg_ref, dbg_ref):
                ref[...] = jnp.zeros_like(ref)

        u = u_ref[...]
        xrv = xr_ref[...]
        xiv = xi_ref[...]
        y = _dot_nt(xrv, cbd_r[...]) - _dot_nt(xiv, cbd_i[...]) + d_ref[...] * u
        yg, t = _gelu(y)
        z = _dot(yg, wg_ref[...]) + bg_ref[...]
        s = jax.nn.sigmoid(z)
        dout = dyo_ref[...]
        dz = dout * yg * s * (1.0 - s)
        dyg = dout * s + _dot_nt(dz, wg_ref[...])
        dwg_ref[...] += _dot_tn(yg, dz)
        dbg_ref[...] += _colsum(dz)
        dy = dyg * (0.5 * (1.0 + t) + 0.5 * y * (1.0 - t * t) * GELU_C * (1.0 + 3.0 * GELU_A * y * y))
        dd_ref[...] += _colsum(dy * u)
        acr[...] += _dot_tn(dy, xrv)
        aci[...] -= _dot_tn(dy, xiv)
        gr_ref[...] = _dot(dy, cbd_r[...])
        gi_ref[...] = -_dot(dy, cbd_i[...])
        has_prev = (i < nb - 1).astype(F32)
        xer_ref[0:8, :] = xpr_ref[...] * has_prev
        xei_ref[0:8, :] = xpi_ref[...] * has_prev
        xer_ref[8:R + 8, :] = xrv
        xei_ref[8:R + 8, :] = xiv
        row = lax.broadcasted_iota(jnp.int32, (8, S5_LANES), 0)
        for j in range(S5_N // S5_LANES):
            lanes = pl.ds(j * S5_LANES, S5_LANES)
            pr = tab_ref[6, :, lanes]
            pi = tab_ref[7, :, lanes]

            def step(n, carry):
                cr, ci, sar, sai = carry
                g = R // 8 - 1 - n
                r0 = pl.multiple_of(g * 8, 8)
                rows = pl.ds(r0, 8)
                gr, gi = _scan8(gr_ref[rows, lanes], gi_ref[rows, lanes], tab_ref, lanes, True)
                br = jnp.broadcast_to(cr[0:1, :], cr.shape)
                bi = jnp.broadcast_to(ci[0:1, :], ci.shape)
                gr = gr + (pr * br - pi * bi)
                gi = gi + (pr * bi + pi * br)
                gr_ref[rows, lanes] = gr
                gi_ref[rows, lanes] = gi
                last = row == 7
                xpr = pltpu.roll(jnp.where(last, xer_ref[rows, lanes], xer_ref[pl.ds(r0 + 8, 8), lanes]), 1, 0)
                xpi = pltpu.roll(jnp.where(last, xei_ref[rows, lanes], xei_ref[pl.ds(r0 + 8, 8), lanes]), 1, 0)
                return gr, gi, sar + (gr * xpr + gi * xpi), sai + (gi * xpr - gr * xpi)

            cr, ci, sar, sai = lax.fori_loop(
                0, R // 8, step, (car_r[:, lanes], car_i[:, lanes], adr[:, lanes], adi[:, lanes]), unroll=2)
            car_r[:, lanes] = cr
            car_i[:, lanes] = ci
            adr[:, lanes] = sar
            adi[:, lanes] = sai
        grv = gr_ref[...]
        giv = gi_ref[...]
        du_ref[...] = (dy * d_ref[...] + _dot_nt(grv, bbd_r[...]) + _dot_nt(giv, bbd_i[...])).astype(du_ref.dtype)
        abr[...] += _dot_tn(u, grv)
        abi[...] += _dot_tn(u, giv)

        @pl.when(i == nb - 1)
        def _():
            mask = _bd_mask()
            r16 = lax.broadcasted_iota(jnp.int32, (S5_H, S5_W), 1)
            h16 = lax.broadcasted_iota(jnp.int32, (S5_H, S5_W), 0)
            fold_b = jnp.bitwise_and(r16, S5_H - 1) == h16
            c64 = lax.broadcasted_iota(jnp.int32, (S5_N, S5_P), 0)
            p64 = lax.broadcasted_iota(jnp.int32, (S5_N, S5_P), 1)
            fold_c = jnp.bitwise_and(c64, S5_P - 1) == p64
            dbbr = _dot_sel_lhs(fold_b, jnp.where(mask, abr[...], 0.0))
            dbbi = _dot_sel_lhs(fold_b, jnp.where(mask, abi[...], 0.0))
            dcre_ref[...] = _dot_sel_rhs(jnp.where(mask, acr[...], 0.0), fold_c)
            dcim_ref[...] = _dot_sel_rhs(jnp.where(mask, aci[...], 0.0), fold_c)
            dlbr = _colsum(adr[...])
            dlbi = _colsum(adi[...])
            _, vjp = jax.vjp(_s5_disc, lre_ref[...], lim_ref[...], ldt_ref[...], bre_ref[...], bim_ref[...])
            dlre, dlim, dldt, dbre, dbim = vjp((dlbr, dlbi, dbbr, dbbi))
            dlre_ref[...] = dlre
            dlim_ref[...] = dlim
            dbre_ref[...] = dbre
            dbim_ref[...] = dbim
            gsel = jnp.right_shift(lax.broadcasted_iota(jnp.int32, (S5_N, HEAD), 0), 6) == \
                lax.broadcasted_iota(jnp.int32, (S5_N, HEAD), 1)
            dldt_ref[...] = _dot_sel_rhs(dldt, gsel)

    full = lambda a: pl.BlockSpec(a.shape, lambda i: (0,) * a.ndim)
    rev = lambda w: pl.BlockSpec((R, w), lambda i: (nb - 1 - i, 0))
    prev8 = pl.BlockSpec((8, S5_N), lambda i: (jnp.maximum((nb - 1 - i) * (R // 8) - 1, 0), 0))
    small = [lre, lim, ldt, bre_t, bim_t, cre_w, cim_w, d_row, w_glu, b_glu]
    outs = [((tp, S5_W), rev(S5_W))] + [
        (s, pl.BlockSpec(s, lambda i: (0, 0))) for s in
        [(1, S5_N), (1, S5_N), (1, HEAD), (S5_H, S5_N), (S5_H, S5_N), (S5_W, S5_P), (S5_W, S5_P),
         (1, S5_W), (S5_W, S5_W), (1, S5_W)]]
    return _call(
        body, "s5_bwd", (nb,),
        [rev(S5_W), rev(S5_W), rev(S5_N), rev(S5_N), prev8, prev8] + [full(a) for a in small + [after]],
        [o[1] for o in outs], [jax.ShapeDtypeStruct(o[0], MM if n == 0 else F32) for n, o in enumerate(outs)],
        [pltpu.VMEM((S5_W, S5_N), MM)] * 4 + [
            pltpu.VMEM((8, 8, S5_N), F32), pltpu.VMEM((8, S5_N), F32), pltpu.VMEM((8, S5_N), F32),
            pltpu.VMEM((R, S5_N), F32), pltpu.VMEM((R, S5_N), F32),
            pltpu.VMEM((R + 8, S5_N), F32), pltpu.VMEM((R + 8, S5_N), F32)] + [pltpu.VMEM((S5_W, S5_N), F32)] * 4 + [
            pltpu.VMEM((8, S5_N), F32), pltpu.VMEM((8, S5_N), F32)],
        (dy_out, u, xr, xi, xr, xi, *small, after), jobs)


RET_CHUNK = ROW_BLK
LOG_GAMMA = [math.log1p(-2.0 ** (-5 - h)) for h in range(RET_H)]
GAMMA_CHUNK = [math.exp(RET_CHUNK * lg) for lg in LOG_GAMMA]
_DECAY_SCRATCH = [pltpu.VMEM((RET_H, RET_CHUNK, RET_CHUNK), F32), pltpu.VMEM((RET_H, RET_CHUNK, HEAD), F32),
                  pltpu.VMEM((RET_H, RET_CHUNK, HEAD), F32)]


def _fill_decay(dm_ref, ze_ref, xi_ref):
    C = RET_CHUNK
    diff = (lax.broadcasted_iota(jnp.int32, (C, C), 0) - lax.broadcasted_iota(jnp.int32, (C, C), 1)).astype(F32)
    r = lax.broadcasted_iota(jnp.int32, (C, HEAD), 0).astype(F32)
    for h, lg in enumerate(LOG_GAMMA):
        dm_ref[h] = jnp.where(diff >= 0.0, jnp.exp(jnp.maximum(diff, 0.0) * lg), 0.0)
        ze_ref[h] = jnp.exp((C - 1.0 - r) * lg)
        xi_ref[h] = jnp.exp((r + 1.0) * lg)


def _ret_fwd(q, k, v, jobs=()):
    tp = q.shape[0]
    C = RET_CHUNK
    nc = tp // C

    def body(q_ref, k_ref, v_ref, o_ref, st_ref, s_ref, dm_ref, ze_ref, xi_ref):
        @pl.when(pl.program_id(0) == 0)
        def _():
            s_ref[...] = jnp.zeros_like(s_ref)
            _fill_decay(dm_ref, ze_ref, xi_ref)

        for h in range(RET_H):
            sl = slice(h * HEAD, (h + 1) * HEAD)
            qh, kh, vh = q_ref[:, sl], k_ref[:, sl], v_ref[:, sl]
            sh = s_ref[h]
            st_ref[0, sl, :] = sh
            scores = _dot_nt(qh, kh) * dm_ref[h]
            o_ref[:, sl] = _dot(scores, vh) + _dot(qh, sh) * xi_ref[h]
            s_ref[h] = GAMMA_CHUNK[h] * sh + _dot_tn(kh.astype(F32) * ze_ref[h], vh)

    blk = pl.BlockSpec((C, RET_W), lambda c: (c, 0))
    return _call(
        body, "ret_fwd", (nc,), [blk, blk, blk], [blk, pl.BlockSpec((1, RET_W, HEAD), lambda c: (c, 0, 0))],
        [jax.ShapeDtypeStruct((tp, RET_W), F32), jax.ShapeDtypeStruct((nc, RET_W, HEAD), F32)],
        [pltpu.VMEM((RET_H, HEAD, HEAD), F32)] + _DECAY_SCRATCH, (q, k, v), jobs)


def _ret_bwd(q, k, v, do, states, cos2, sin2, jobs=()):
    tp = q.shape[0]
    C = RET_CHUNK
    nc = tp // C

    def body(q_ref, k_ref, v_ref, do_ref, st_ref, cos_ref, sin_ref,
             dq_ref, dk_ref, dv_ref, ds_ref, dm_ref, ze_ref, xi_ref):
        @pl.when(pl.program_id(0) == 0)
        def _():
            ds_ref[...] = jnp.zeros_like(ds_ref)
            _fill_decay(dm_ref, ze_ref, xi_ref)

        cos = cos_ref[...]
        sin = sin_ref[...]
        for h in range(RET_H):
            sl = slice(h * HEAD, (h + 1) * HEAD)
            qh, kh, vh = q_ref[:, sl], k_ref[:, sl], v_ref[:, sl]
            dmh = dm_ref[h]
            sh = st_ref[0, sl, :]
            dsn = ds_ref[h]
            doh = do_ref[:, sl]
            dox = doh * xi_ref[h]
            a = _dot_nt(qh, kh) * dmh
            dqk = _dot_nt(doh, vh) * dmh
            kz = kh.astype(F32) * ze_ref[h]
            dv_ref[:, sl] = (_dot_tn(a, doh) + _dot(kz, dsn)).astype(dv_ref.dtype)
            dqr = _dot(dqk, kh) + _dot_nt(dox, sh)
            dkr = _dot_tn(dqk, qh) + ze_ref[h] * _dot_nt(vh, dsn)
            ds_ref[h] = GAMMA_CHUNK[h] * dsn + _dot_tn(qh, dox)
            dq_ref[:, sl] = (dqr * cos - pltpu.roll(dqr, HEAD // 2, 1) * sin).astype(dq_ref.dtype)
            dk_ref[:, sl] = ((dkr * cos - pltpu.roll(dkr, HEAD // 2, 1) * sin) * (HEAD ** -0.5)).astype(dk_ref.dtype)

    blk = pl.BlockSpec((C, RET_W), lambda c: (nc - 1 - c, 0))
    tab = pl.BlockSpec((C, HEAD), lambda c: (nc - 1 - c, 0))
    return _call(
        body, "ret_bwd", (nc,),
        [blk, blk, blk, blk, pl.BlockSpec((1, RET_W, HEAD), lambda c: (nc - 1 - c, 0, 0)), tab, tab],
        [blk, blk, blk], [jax.ShapeDtypeStruct((tp, RET_W), MM)] * 3,
        [pltpu.VMEM((RET_H, HEAD, HEAD), F32)] + _DECAY_SCRATCH, (q, k, v, do, states, cos2, sin2), jobs)


def _gn_gate(o, gate, gn_g, gn_b):
    xhat, rstd = _ln_fwd(o, GN_EPS)
    on = xhat * gn_g + gn_b
    s = jax.nn.sigmoid(gate)
    return gate * s * on, xhat, rstd, on, s


def _post_up(o, gate, ys5, xhat0, gn_g, gn_b, li_g, li_b, l1_g, l1_b, w_out, w_up, jobs=()):
    tp = o.shape[0]
    R = ROW_BLK

    def body(o_ref, g_ref, ys_ref, xh0_ref, gng, gnb, lig, lib, l1g, l1b, wo_ref, wu_ref,
             ycat_ref, xh1_ref, rstd1_ref, h1b_ref, pre_ref):
        ycat_ref[:, 0:S5_W] = ys_ref[...].astype(ycat_ref.dtype)
        for h in range(RET_H):
            sl = slice(h * HEAD, (h + 1) * HEAD)
            yret = _gn_gate(o_ref[:, sl], g_ref[:, sl], gng[:, sl], gnb[:, sl])[0]
            ycat_ref[:, S5_W + h * HEAD:S5_W + (h + 1) * HEAD] = yret.astype(ycat_ref.dtype)
        mixed = _dot(ycat_ref[...], wo_ref[...])
        h0 = xh0_ref[...] * lig[...] + lib[...]
        xh1, rstd1 = _ln_fwd(ALPHA * h0 + mixed, LN_EPS)
        xh1_ref[...] = xh1
        rstd1_ref[...] = rstd1
        h1b = (xh1 * l1g[...] + l1b[...]).astype(MM)
        h1b_ref[...] = h1b
        for d in range(N_DEV):
            pre_ref[:, d * FF_BLK:(d + 1) * FF_BLK] = jnp.maximum(_dot(h1b, wu_ref[d]), 0.0)

    row = lambda w: pl.BlockSpec((R, w), lambda i: (i, 0))
    full = lambda a: pl.BlockSpec(a.shape, lambda i: (0,) * a.ndim)
    vecs = [gn_g, gn_b, li_g, li_b, l1_g, l1_b]
    outs = [(row(D_MODEL), jax.ShapeDtypeStruct((tp, D_MODEL), MM)), (row(D_MODEL), jax.ShapeDtypeStruct((tp, D_MODEL), F32)),
            (row(1), jax.ShapeDtypeStruct((tp, 1), F32)), (row(D_MODEL), jax.ShapeDtypeStruct((tp, D_MODEL), MM)),
            (row(D_FF), jax.ShapeDtypeStruct((tp, D_FF), F32))]
    return _call(
        body, "post_up", (tp // R,),
        [row(RET_W), row(RET_W), row(S5_W), row(D_MODEL)] + [full(a) for a in vecs] + [_VMEM, _VMEM],
        [o[0] for o in outs], [o[1] for o in outs], [], (o, gate, ys5, xhat0, *vecs, w_out, w_up), jobs)


def _post_down(pre, xhat1, tgt, l1_g, l1_b, l2_g, l2_b, w_down):
    tp = pre.shape[0]
    seq = tgt.shape[0]
    R = ROW_BLK

    def body(pre_ref, xh1_ref, ta, tb, tc, l1g, l1b, l2g, l2b, wd_ref,
             dr2_ref, dffb_ref, loss_ref, dl2g_ref, dl2b_ref, tgt_ref):
        i = pl.program_id(0)

        @pl.when(i == 0)
        def _():
            for ref in (loss_ref, dl2g_ref, dl2b_ref):
                ref[...] = jnp.zeros_like(ref)

        tgt_ref[0:CHUNK, :] = ta[...]
        tgt_ref[CHUNK:2 * CHUNK, :] = tb[...]
        tgt_ref[2 * CHUNK:3 * CHUNK, :] = tc[...]
        ff = jnp.zeros((R, D_MODEL), F32)
        for d in range(N_DEV):
            pre = pre_ref[:, d * FF_BLK:(d + 1) * FF_BLK]
            ff = ff + _dot(pre * pre, wd_ref[d * FF_BLK:(d + 1) * FF_BLK, :])
        h1 = xh1_ref[...] * l1g[...] + l1b[...]
        xh2, rstd2 = _ln_fwd(ALPHA * h1 + ff, LN_EPS)
        h2 = xh2 * l2g[...] + l2b[...]
        valid = (i * R + lax.broadcasted_iota(jnp.int32, (R, 1), 0)) >= CHUNK
        err = jnp.where(valid, h2 - tgt_ref[...], 0.0)
        loss_ref[...] += 0.5 * jnp.sum(err * err) / D_MODEL
        dh2 = err * (1.0 / D_MODEL)
        dl2g_ref[...] += _colsum(dh2 * xh2)
        dl2b_ref[...] += _colsum(dh2)
        dr2 = _ln_bwd(dh2 * l2g[...], xh2, rstd2)
        dr2_ref[...] = dr2
        dffb_ref[...] = dr2.astype(MM)

    row = lambda w: pl.BlockSpec((R, w), lambda i: (i, 0))
    full = lambda a: pl.BlockSpec(a.shape, lambda i: (0,) * a.ndim)
    vecs = [l1_g, l1_b, l2_g, l2_b]
    acc = lambda s: (pl.BlockSpec(s, lambda i: (0, 0)), jax.ShapeDtypeStruct(s, F32))
    outs = [(row(D_MODEL), jax.ShapeDtypeStruct((tp, D_MODEL), F32)), (row(D_MODEL), jax.ShapeDtypeStruct((tp, D_MODEL), MM)),
            acc((8, HEAD)), acc((1, D_MODEL)), acc((1, D_MODEL))]
    return pl.pallas_call(
        body, name="post_down", grid=(tp // R,),
        in_specs=[row(D_FF), row(D_MODEL)] + _shift3(seq // CHUNK) + [full(a) for a in vecs] + [_VMEM],
        out_specs=[o[0] for o in outs], out_shape=[o[1] for o in outs],
        scratch_shapes=[pltpu.VMEM((R, D_MODEL), F32)],
        compiler_params=_params(("arbitrary",)),
    )(pre, xhat1, tgt, tgt, tgt, *vecs, w_down)


def _mlp_bwd(h1b, dffb, pre, w_up, w_down):
    tp = h1b.shape[0]
    R = MLP_ROWS if tp % MLP_ROWS == 0 else ROW_BLK
    nr = tp // R

    def body(h_ref, df_ref, pre_ref, wu_ref, wd_ref, gup_ref, gdn_ref, dh1_ref, aup, adn):
        d = pl.program_id(0)
        r = pl.program_id(1)

        @pl.when(r == 0)
        def _():
            aup[...] = jnp.zeros_like(aup)
            adn[...] = jnp.zeros_like(adn)

        h = h_ref[...]
        df = df_ref[...]
        wu = wu_ref[0]
        wd = wd_ref[0]
        pre = pre_ref[...]
        dpre = (_dot_nt(df, wd) * (2.0 * pre)).astype(MM)

        aup[...] += _dot_tn(h, dpre)
        adn[...] += _dot_tn(pre * pre, df)
        contrib = _dot_nt(dpre, wu)
        rows = pl.ds(pl.multiple_of(r * R, 64), R)

        @pl.when(d == 0)
        def _():
            dh1_ref[rows, :] = contrib

        @pl.when(d > 0)
        def _():
            dh1_ref[rows, :] += contrib

        @pl.when(r == nr - 1)
        def _():
            gup_ref[0] = aup[...].astype(gup_ref.dtype)
            gdn_ref[0] = adn[...].astype(gdn_ref.dtype)

    return pl.pallas_call(
        body, name="mlp_bwd", grid=(N_DEV, nr),
        in_specs=[pl.BlockSpec((R, D_MODEL), lambda d, r: (r, 0)), pl.BlockSpec((R, D_MODEL), lambda d, r: (r, 0)),
                  pl.BlockSpec((R, FF_BLK), lambda d, r: (r, d)),
                  pl.BlockSpec((1, D_MODEL, FF_BLK), lambda d, r: (d, 0, 0)),
                  pl.BlockSpec((1, FF_BLK, D_MODEL), lambda d, r: (d, 0, 0))],
        out_specs=[pl.BlockSpec((1, D_MODEL, FF_BLK), lambda d, r: (d, 0, 0)),
                   pl.BlockSpec((1, FF_BLK, D_MODEL), lambda d, r: (d, 0, 0)), _VMEM],
        out_shape=[jax.ShapeDtypeStruct((N_DEV, D_MODEL, FF_BLK), MM), jax.ShapeDtypeStruct((N_DEV, FF_BLK, D_MODEL), MM),
                   jax.ShapeDtypeStruct((tp, D_MODEL), F32)],
        scratch_shapes=[pltpu.VMEM((D_MODEL, FF_BLK), F32), pltpu.VMEM((FF_BLK, D_MODEL), F32)],
        compiler_params=_params(("arbitrary", "arbitrary")),
    )(h1b, dffb, pre, w_up, w_down.reshape(N_DEV, FF_BLK, D_MODEL))


def _post_bwd(dh1m, dr2, xhat1, rstd1, ycat, o, gate, gn_g, gn_b, l1_g, w_out, jobs=()):
    tp = o.shape[0]
    R = ROW_BLK
    nb = tp // R

    def body(dm_ref, dr2_ref, xh1_ref, rs1_ref, yc_ref, o_ref, g_ref, gng, gnb, l1g, wo_ref,
             do_ref, dg_ref, dys_ref, dh0_ref, gwo_ref, dl1g_ref, dl1b_ref, dgng_ref, dgnb_ref, awo):
        i = pl.program_id(0)

        @pl.when(i == 0)
        def _():
            for ref in (awo, dl1g_ref, dl1b_ref, dgng_ref, dgnb_ref):
                ref[...] = jnp.zeros_like(ref)

        dh1 = dm_ref[...] + ALPHA * dr2_ref[...]
        xh1 = xh1_ref[...]
        dl1g_ref[...] += _colsum(dh1 * xh1)
        dl1b_ref[...] += _colsum(dh1)
        dr1 = _ln_bwd(dh1 * l1g[...], xh1, rs1_ref[...])
        dh0_ref[...] = ALPHA * dr1
        dmix = dr1.astype(MM)
        awo[...] += _dot_tn(yc_ref[...], dmix)
        dyc = _dot_nt(dmix, wo_ref[...])
        dys_ref[...] = dyc[:, 0:S5_W]
        for h in range(RET_H):
            sl = slice(h * HEAD, (h + 1) * HEAD)
            gt = g_ref[:, sl]
            _, xhat, rstd, on, s = _gn_gate(o_ref[:, sl], gt, gng[:, sl], gnb[:, sl])
            dyr = dyc[:, S5_W + h * HEAD:S5_W + (h + 1) * HEAD]
            dg_ref[:, sl] = (dyr * on * (s * (1.0 + gt * (1.0 - s)))).astype(dg_ref.dtype)
            don = dyr * gt * s
            dgng_ref[:, sl] += _colsum(don * xhat)
            dgnb_ref[:, sl] += _colsum(don)
            do_ref[:, sl] = _ln_bwd(don * gng[:, sl], xhat, rstd)

        @pl.when(i == nb - 1)
        def _():
            gwo_ref[...] = awo[...].astype(gwo_ref.dtype)

    row = lambda w: pl.BlockSpec((R, w), lambda i: (i, 0))
    full = lambda a: pl.BlockSpec(a.shape, lambda i: (0,) * a.ndim)
    acc = lambda s, dt=F32: (pl.BlockSpec(s, lambda i: (0, 0)), jax.ShapeDtypeStruct(s, dt))
    outs = [(row(RET_W), jax.ShapeDtypeStruct((tp, RET_W), F32)), (row(RET_W), jax.ShapeDtypeStruct((tp, RET_W), MM)),
            (row(S5_W), jax.ShapeDtypeStruct((tp, S5_W), F32)), (row(D_MODEL), jax.ShapeDtypeStruct((tp, D_MODEL), F32)),
            acc((D_MODEL, D_MODEL), MM), acc((1, D_MODEL)), acc((1, D_MODEL)), acc((1, RET_W)), acc((1, RET_W))]
    return _call(
        body, "post_bwd", (nb,),
        [row(D_MODEL), row(D_MODEL), row(D_MODEL), row(1), row(D_MODEL), row(RET_W), row(RET_W),
         full(gn_g), full(gn_b), full(l1_g), _VMEM],
        [o[0] for o in outs], [o[1] for o in outs],
        [pltpu.VMEM((D_MODEL, D_MODEL), F32)],
        (dh1m, dr2, xhat1, rstd1, ycat, o, gate, gn_g, gn_b, l1_g, w_out), jobs)


_PROJ_SEGS = [(0, S5_W)] + [(S5_W + n * RET_W, S5_W + (n + 1) * RET_W) for n in range(4)]


def _in_w_grad(du, dq, dk, dv, dg, xhat0, li_g, li_b):
    tp = du.shape[0]
    R = PROJ_ROWS if tp % PROJ_ROWS == 0 else ROW_BLK
    nb = tp // R

    def body(du_ref, dq_ref, dk_ref, dv_ref, dg_ref, xh_ref, lig, lib, gw_ref, aw):
        i = pl.program_id(0)

        @pl.when(i == 0)
        def _():
            aw[...] = jnp.zeros_like(aw)

        valid = (i * R + lax.broadcasted_iota(jnp.int32, (R, 1), 0)) >= PAD
        hb = (xh_ref[...] * lig[...] + lib[...]).astype(MM)
        for (lo, hi), ref in zip(_PROJ_SEGS, (du_ref, dq_ref, dk_ref, dv_ref, dg_ref)):
            aw[lo:hi, :] += _dot_tn(jnp.where(valid, ref[...], 0.0).astype(MM), hb)

        @pl.when(i == nb - 1)
        def _():
            gw_ref[...] = aw[...].astype(gw_ref.dtype)

    row = lambda w: pl.BlockSpec((R, w), lambda i: (i, 0))
    full = lambda a: pl.BlockSpec(a.shape, lambda i: (0,) * a.ndim)
    (gw,), _ = _call(
        body, "in_w_grad", (nb,),
        [row(S5_W), row(RET_W), row(RET_W), row(RET_W), row(RET_W), row(D_MODEL), full(li_g), full(li_b)],
        [pl.BlockSpec((PROJ_W, D_MODEL), lambda i: (0, 0))], [jax.ShapeDtypeStruct((PROJ_W, D_MODEL), MM)],
        [pltpu.VMEM((PROJ_W, D_MODEL), F32)], (du, dq, dk, dv, dg, xhat0, li_g, li_b))
    return gw


def _in_bwd(du, dq, dk, dv, dg, dh0r, xhat0, rstd0, li_g, w_int, after):
    tp = du.shape[0]
    R = PROJ_ROWS if tp % PROJ_ROWS == 0 else ROW_BLK
    nb = tp // R
    segs = _PROJ_SEGS

    def body(du_ref, dq_ref, dk_ref, dv_ref, dg_ref, dh0r_ref, xh_ref, rs_ref, lig, w_ref, after_ref,
             gx_ref, dmeta_ref, dlg_ref, dlb_ref, stage, out_sems):
        i = pl.program_id(0)
        slot = i % 2

        def to_gx(step_slot, first):
            if first:
                return pltpu.make_async_copy(stage.at[0, CHUNK:R, :], gx_ref.at[0:R - CHUNK, :], out_sems.at[0])
            return pltpu.make_async_copy(stage.at[step_slot], gx_ref.at[pl.ds(i * R - CHUNK, R), :], out_sems.at[step_slot])

        @pl.when(i == 0)
        def _():
            for ref in (dlg_ref, dlb_ref):
                ref[...] = jnp.zeros_like(ref)

        @pl.when(i >= 3)
        def _():
            to_gx(slot, False).wait()

        valid = (i * R + lax.broadcasted_iota(jnp.int32, (R, 1), 0)) >= PAD
        xh = xh_ref[...]
        dh0 = dh0r_ref[...]
        for (lo, hi), ref in zip(segs, (du_ref, dq_ref, dk_ref, dv_ref, dg_ref)):
            dh0 = dh0 + _dot(jnp.where(valid, ref[...], 0.0).astype(MM), w_ref[lo:hi, :])
        dlg_ref[...] += _colsum(dh0 * xh)
        dlb_ref[...] += _colsum(dh0)
        draw = _ln_bwd(dh0 * lig[...], xh, rs_ref[...])
        stage[slot] = draw

        @pl.when(i == 0)
        def _():
            dmeta_ref[...] = draw[PAD:CHUNK, :]
            first = to_gx(0, True)
            first.start()
            first.wait()

        @pl.when(i > 0)
        def _():
            to_gx(slot, False).start()

        @pl.when(i == nb - 1)
        def _():
            for back in (1, 0):
                if nb - 1 - back >= 1:
                    to_gx((nb - 1 - back) % 2, False).wait()

    row = lambda w: pl.BlockSpec((R, w), lambda i: (i, 0))
    full = lambda a: pl.BlockSpec(a.shape, lambda i: (0,) * a.ndim)
    acc = lambda s, dt=F32: (pl.BlockSpec(s, lambda i: (0, 0)), jax.ShapeDtypeStruct(s, dt))
    outs = [(_ANY, jax.ShapeDtypeStruct((tp - CHUNK, D_MODEL), F32)), acc((N_META, D_MODEL)),
            acc((1, D_MODEL)), acc((1, D_MODEL))]
    return _call(
        body, "in_bwd", (nb,),
        [row(S5_W), row(RET_W), row(RET_W), row(RET_W), row(RET_W), row(D_MODEL), row(D_MODEL), row(1),
         full(li_g), _VMEM, full(after)],
        [o[0] for o in outs], [o[1] for o in outs],
        [pltpu.VMEM((2, R, D_MODEL), F32), pltpu.SemaphoreType.DMA((2,))],
        (du, dq, dk, dv, dg, dh0r, xhat0, rstd0, li_g, w_int, after))[0]


def _place():
    return lax.axis_index("x"), lax.axis_index("y"), lax.axis_index("c")


def _dma_sems(n):
    return pltpu.SemaphoreType.DMA((n,))


def _job_gather(shard):
    def parts(ins, outs, sems):
        (src,), (out,), (send_sems, recv_sems, local_sem) = ins, outs, sems
        x, y, c = _place()
        north = c == 1
        me, sib = (x, y, c), (x, y, 1 - c)
        xn, yn, dg = (1 - x, y, c), (x, 1 - y, c), (1 - x, 1 - y, c)
        relay_from = (jnp.where(north, 1 - x, x), jnp.where(north, y, 1 - y), c)
        relay_to = (jnp.where(north, x, 1 - x), jnp.where(north, 1 - y, y), c)

        def slot(dev):
            return out.at[4 * dev[0] + 2 * dev[1] + dev[2]]

        def copy(k, block, to, from_input=False):
            return pltpu.make_async_remote_copy(
                src_ref=src if from_input else slot(block), dst_ref=slot(block),
                send_sem=send_sems.at[k], recv_sem=recv_sems.at[k], device_id=to, device_id_type=_MESH)

        mine = lambda: pltpu.make_async_copy(src, slot(me), local_sem.at[0])
        first = lambda: [copy(0, me, sib, True), copy(1, me, xn, True), copy(2, me, yn, True)]
        relayed = lambda: [copy(3, relay_from, relay_to), copy(4, xn, sib), copy(5, yn, sib)]
        return me, sib, xn, yn, dg, copy, mine, first, relayed

    def start(ins, outs, sems):
        mine, first = parts(ins, outs, sems)[6:8]
        mine().start()
        for cp in first():
            cp.start()

    def relay(ins, outs, sems):
        me, sib, xn, yn, dg, copy, mine, first, relayed = parts(ins, outs, sems)
        copy(1, xn, me).wait_recv()
        copy(2, yn, me).wait_recv()
        for cp in relayed():
            cp.start()

    def finish(ins, outs, sems):
        me, sib, xn, yn, dg, copy, mine, first, relayed = parts(ins, outs, sems)
        other = 1 - me[2]
        copy(3, dg, me).wait_recv()
        last = copy(6, dg, sib)
        last.start()
        copy(0, sib, me).wait_recv()
        for k, chip in ((4, xn), (5, yn), (6, dg)):
            copy(k, (chip[0], chip[1], other), me).wait_recv()
        for cp in first() + relayed() + [last]:
            cp.wait_send()
        mine().wait()

    return dict(ins=[shard], outs=[jax.ShapeDtypeStruct((N_DEV,) + shard.shape, shard.dtype)],
                sems=[_dma_sems(7), _dma_sems(7), _dma_sems(1)], start=start, middle=relay, finish=finish)


def _job_pair(g):
    def copies(ins, outs, sems):
        x, y, c = _place()
        return [pltpu.make_async_remote_copy(
            src_ref=ins[0].at[2 * j + (1 - c)], dst_ref=outs[0].at[j], send_sem=sems[0].at[j], recv_sem=sems[1].at[j],
            device_id=(x, y, 1 - c), device_id_type=_MESH) for j in range(4)]

    def start(ins, outs, sems):
        for cp in copies(ins, outs, sems):
            cp.start()

    def finish(ins, outs, sems):
        for cp in copies(ins, outs, sems):
            cp.wait()

    return dict(ins=[g], outs=[jax.ShapeDtypeStruct((4,) + g.shape[1:], g.dtype)], sems=[_dma_sems(4), _dma_sems(4)],
                start=start, finish=finish)


def _job_chips(p):
    def copies(ins, outs, sems):
        x, y, c = _place()
        chips = [(1 - x, y), (x, 1 - y), (1 - x, 1 - y)]
        return [pltpu.make_async_remote_copy(
            src_ref=ins[0].at[2 * chip[0] + chip[1]], dst_ref=outs[0].at[k], send_sem=sems[0].at[k],
            recv_sem=sems[1].at[k], device_id=(*chip, c), device_id_type=_MESH) for k, chip in enumerate(chips)]

    def start(ins, outs, sems):
        for cp in copies(ins, outs, sems):
            cp.start()

    def finish(ins, outs, sems):
        for cp in copies(ins, outs, sems):
            cp.wait()

    return dict(ins=[p], outs=[jax.ShapeDtypeStruct((3,) + p.shape[1:], p.dtype)], sems=[_dma_sems(3), _dma_sems(3)],
                start=start, finish=finish)


_HBM = pl.BlockSpec(memory_space=pltpu.HBM)
_SEM = pl.BlockSpec(memory_space=pltpu.SEMAPHORE)
_ORDERED = pltpu.CompilerParams(has_side_effects=pltpu.SideEffectType.DATAFLOW_SIDE_EFFECTING)


def _chip_copies(p_ref, land_ref, sems):
    x, y, c = _place()
    chips = [(1 - x, y), (x, 1 - y), (1 - x, 1 - y)]
    return [pltpu.make_async_remote_copy(
        src_ref=p_ref.at[2 * chip[0] + chip[1]], dst_ref=land_ref.at[k], send_sem=sems[k], recv_sem=sems[3 + k],
        device_id=(*chip, c), device_id_type=_MESH) for k, chip in enumerate(chips)]


def _chips_start(ps, name):
    n = len(ps)

    def body(*refs):
        sems = refs[2 * n:8 * n]
        for a in range(n):
            for cp in _chip_copies(refs[a], refs[n + a], sems[6 * a:6 * a + 6]):
                cp.start()
        refs[-1][...] = jnp.zeros_like(refs[-1])

    lands = [(3,) + p.shape[1:] for p in ps]
    hbm = lambda arr: pltpu.with_memory_space_constraint(arr, pltpu.HBM)
    outs = pl.pallas_call(
        body, name=name,
        out_shape=(*[pltpu.SemaphoreType.DMA(())] * (6 * n), *[pltpu.HBM(p.shape, p.dtype) for p in ps],
                   *[pltpu.HBM(s, p.dtype) for s, p in zip(lands, ps)], jax.ShapeDtypeStruct((8, LANE), F32)),
        in_specs=[_HBM] * (2 * n), out_specs=(*[_SEM] * (6 * n), *[_HBM] * (2 * n), _VMEM),
        input_output_aliases={a: 6 * n + a for a in range(2 * n)}, compiler_params=_ORDERED,
    )(*[hbm(p) for p in ps], *[hbm(lax.empty(s, p.dtype)) for s, p in zip(lands, ps)])
    return (list(outs[:6 * n]), list(outs[6 * n:7 * n]), list(outs[7 * n:8 * n])), outs[8 * n]


def _chips_wait(started, after, name):
    sems, thrus, lands = started
    n = len(thrus)

    def body(*refs):
        for a in range(n):
            for cp in _chip_copies(refs[a], refs[n + a], refs[2 * n + 6 * a:2 * n + 6 * a + 6]):
                cp.wait_send()
                cp.wait_recv()

    outs = pl.pallas_call(
        body, name=name, out_shape=[pltpu.HBM(t.shape, t.dtype) for t in thrus + lands],
        in_specs=(*[_HBM] * (2 * n), *[_SEM] * (6 * n), _ANY), out_specs=[_HBM] * (2 * n),
        input_output_aliases={a: a for a in range(2 * n)}, compiler_params=_ORDERED,
    )(*thrus, *lands, *sems, after)
    return list(outs[:n]), list(outs[n:])


def _split_job_refs(jobs, ins, outs, sems):
    res, a, b, c = [], 0, 0, 0
    for job in jobs:
        na, nb, nc = len(job["ins"]), len(job["outs"]), len(job["sems"])
        res.append((ins[a:a + na], outs[b:b + nb], sems[c:c + nc]))
        a, b, c = a + na, b + nb, c + nc
    return res


def _call(body, name, grid, in_specs, out_specs, out_shape, scratch, args, jobs=(), prefetch=None, early=0):
    jobs = list(jobs)
    n_in, n_out, n_scr = len(in_specs), len(out_specs), len(scratch)
    j_in = [a for job in jobs for a in job["ins"]]
    j_out = [o for job in jobs for o in job["outs"]]
    j_scr = [s for job in jobs for s in job["sems"]]
    nsteps = grid[0]
    n_pre = 0 if prefetch is None else 1

    def wrapped(*refs):
        pre, refs = refs[:n_pre], refs[n_pre:]
        ins, jins = refs[:n_in], refs[n_in:n_in + len(j_in)]
        refs = refs[n_in + len(j_in):]
        outs, jouts = refs[:n_out], refs[n_out:n_out + len(j_out)]
        refs = refs[n_out + len(j_out):]
        scr, jscr = refs[:n_scr], refs[n_scr:]
        per_job = _split_job_refs(jobs, jins, jouts, jscr)

        def middle():
            for job, r in zip(jobs, per_job):
                if "middle" in job:
                    job["middle"](*r)

        @pl.when(pl.program_id(0) == 0)
        def _():
            for job, r in zip(jobs, per_job):
                job["start"](*r)

        if nsteps >= 3:
            pl.when(pl.program_id(0) == nsteps // 2)(middle)

        if early:
            @pl.when(pl.program_id(0) == nsteps - 1)
            def _():
                for job, r in zip(jobs[:early], per_job[:early]):
                    job["finish"](*r)

        body(*pre, *ins, *outs, *scr, *[o for r in per_job[:early] for o in r[1]])

        @pl.when(pl.program_id(0) == nsteps - 1)
        def _():
            if nsteps < 3:
                middle()
            for job, r in zip(jobs[early:], per_job[early:]):
                job["finish"](*r)

    specs = dict(in_specs=list(in_specs) + [_ANY] * len(j_in), out_specs=list(out_specs) + [_ANY] * len(j_out),
                 scratch_shapes=list(scratch) + j_scr)
    if n_pre:
        specs = dict(grid_spec=pltpu.PrefetchScalarGridSpec(num_scalar_prefetch=1, grid=grid, **specs))
    else:
        specs["grid"] = grid
    res = pl.pallas_call(
        wrapped if jobs else body, name=name, out_shape=list(out_shape) + j_out,
        compiler_params=_params(("arbitrary",) * len(grid)), **specs,
    )(*([prefetch] if n_pre else []), *args, *j_in)
    return list(res[:n_out]), list(res[n_out:])


def _exchange(jobs, name):
    j_in = [a for job in jobs for a in job["ins"]]
    j_out = [o for job in jobs for o in job["outs"]]
    j_scr = [s for job in jobs for s in job["sems"]]

    def body(*refs):
        per_job = _split_job_refs(jobs, refs[:len(j_in)], refs[len(j_in):len(j_in) + len(j_out)],
                                  refs[len(j_in) + len(j_out):])
        for phase in ("start", "middle", "finish"):
            for job, r in zip(jobs, per_job):
                if phase in job:
                    job[phase](*r)

    return pl.pallas_call(body, name=name, out_shape=j_out, in_specs=[_ANY] * len(j_in), out_specs=[_ANY] * len(j_out),
                          scratch_shapes=j_scr)(*j_in)


def _pair_sum(gs, r1s, c_arr, name):
    n = len(gs)

    def body(c_ref, *refs):
        for a in range(n):
            refs[2 * n + a][...] = (refs[a][...].astype(F32) + refs[n + a][...].astype(F32)).astype(refs[2 * n + a].dtype)

    def blk(g, own):
        s = g.shape[1:]
        if own:
            return pl.BlockSpec((1,) + s, lambda j, c_ref: (2 * j + c_ref[0],) + (0,) * len(s))
        return pl.BlockSpec((1,) + s, lambda j, c_ref: (j,) + (0,) * len(s))

    return pl.pallas_call(
        body, name=name,
        grid_spec=pltpu.PrefetchScalarGridSpec(
            num_scalar_prefetch=1, grid=(4,),
            in_specs=[blk(g, True) for g in gs] + [blk(g, False) for g in gs],
            out_specs=[blk(g, False) for g in gs]),
        out_shape=[jax.ShapeDtypeStruct((4,) + g.shape[1:], g.dtype) for g in gs],
        compiler_params=_params(("arbitrary",)),
    )(c_arr, *gs, *r1s)


def _adamw_math(w, g, m, v):
    m = ADAM_B1 * m + (1.0 - ADAM_B1) * g
    v = ADAM_B2 * v + (1.0 - ADAM_B2) * (g * g)
    m_hat = m / (1.0 - ADAM_B1 ** ADAM_STEP)
    v_hat = v / (1.0 - ADAM_B2 ** ADAM_STEP)
    return -ADAM_LR * (m_hat / (jnp.sqrt(v_hat) + ADAM_EPS) + ADAM_WD * w), m, v


def _view(name, a):
    return jnp.swapaxes(a, -1, -2) if name in ("w_in", "s5_b_re", "s5_b_im") else a


def _adamw_shards(items, name, steps, chip, jobs=()):
    n = len(items)

    def body(chip_ref, *refs):
        for a in range(n):
            p_ref, r_ref, w_ref, m_ref, v_ref = refs[5 * a:5 * a + 5]
            g = ((p_ref[0].astype(F32) + r_ref[0].astype(F32)) + r_ref[1].astype(F32)) + r_ref[2].astype(F32)
            outs = refs[5 * n + 4 * a:5 * n + 4 * a + 4]
            outs[0][...] = g
            outs[1][...], outs[2][...], outs[3][...] = _adamw_math(w_ref[...], g, m_ref[...], v_ref[...])

    in_specs, out_specs, out_shape, flat = [], [], [], []
    for p, r, w, m, v in items:
        rows, cols = w.shape
        rb = rows // steps
        in_specs += [pl.BlockSpec((1, rb, cols), lambda i, c: (c[0], i, 0)), pl.BlockSpec((3, rb, cols), lambda i, c: (0, i, 0))]
        wblk = pl.BlockSpec((rb, cols), lambda i, c: (i, 0))
        in_specs += [wblk] * 3
        out_specs += [wblk] * 4
        out_shape += [jax.ShapeDtypeStruct(w.shape, F32)] * 4
        flat += [p, r, w, m, v]
    return _call(body, name, (steps,), in_specs, out_specs, out_shape, [], flat, jobs, prefetch=chip)


def _sum_devices(gathered, name):
    def body(gs_ref, g_ref):
        g = gs_ref[0]
        for s in range(1, N_DEV):
            g = g + gs_ref[s]
        g_ref[...] = g

    return pl.pallas_call(body, name=name, out_shape=jax.ShapeDtypeStruct(gathered.shape[1:], F32),
                          in_specs=[_VMEM], out_specs=_VMEM, compiler_params=_params())(gathered)


def _adamw_native(items, name):
    n = len(items)

    def body(*refs):
        for a in range(n):
            g, w, m, v = (refs[4 * a + t][...] for t in range(4))
            refs[4 * n + 3 * a][...], refs[4 * n + 3 * a + 1][...], refs[4 * n + 3 * a + 2][...] = _adamw_math(w, g, m, v)

    return pl.pallas_call(
        body, name=name, out_shape=[jax.ShapeDtypeStruct(it[1].shape, F32) for it in items for _ in range(3)],
        in_specs=[_VMEM] * (4 * n), out_specs=[_VMEM] * (3 * n), compiler_params=_params(),
    )(*[t for it in items for t in it])


SMALL = ["ln_in_g", "ln_in_b", "s5_lambda_re", "s5_lambda_im", "s5_log_dt", "s5_b_re", "s5_b_im", "s5_c_re", "s5_c_im",
         "s5_d", "s5_b_glu", "ret_gn_g", "ret_gn_b", "ln1_g", "ln1_b", "ln2_g", "ln2_b"]
LATE = ["ln_in_g", "ln_in_b", "meta_tokens"]
EARLY = [n for n in SMALL if n not in LATE] + ["s5_w_glu", "loss"]
LANE = 128


def _pack(arrs):
    parts = []
    for a in arrs:
        f = a.reshape(-1)
        parts.append(jnp.pad(f, (0, (-f.shape[0]) % LANE)))
    flat = jnp.concatenate(parts)
    rows = -(-flat.shape[0] // LANE)
    flat = jnp.pad(flat, (0, (-rows % 8) * LANE + rows * LANE - flat.shape[0]))
    return flat.reshape(-1, LANE)


def _unpack(packed, shapes):
    flat = packed.reshape(-1)
    out, off = [], 0
    for s in shapes:
        n = math.prod(s)
        out.append(flat[off:off + n].reshape(s))
        off += n + (-n) % LANE
    return out


def _rope_tables(tp):
    inv_freq = 1.0 / (ROPE_BASE ** (jnp.arange(0, HEAD, 2, dtype=F32) / HEAD))
    blk = (jnp.arange(tp // ROW_BLK, dtype=F32) * ROW_BLK)[:, None, None] * inv_freq
    off = (jnp.arange(ROW_BLK, dtype=F32) - float(PAD))[None, :, None] * inv_freq
    cos = (jnp.cos(blk) * jnp.cos(off) - jnp.sin(blk) * jnp.sin(off)).reshape(tp, HEAD // 2)
    sin = (jnp.sin(blk) * jnp.cos(off) + jnp.cos(blk) * jnp.sin(off)).reshape(tp, HEAD // 2)
    return jnp.concatenate([cos, cos], axis=1), jnp.concatenate([-sin, sin], axis=1)


def _local_step(x2d, tgt, meta, w_int, w_out, w_up, w_down, w_glu, sp, distributed):
    tp = x2d.shape[0] + CHUNK
    row = lambda a: a.reshape(1, -1)
    cos2, sin2 = _rope_tables(tp)
    li_g, li_b = row(sp["ln_in_g"]), row(sp["ln_in_b"])
    l1_g, l1_b, l2_g, l2_b = row(sp["ln1_g"]), row(sp["ln1_b"]), row(sp["ln2_g"]), row(sp["ln2_b"])
    gn_g, gn_b = row(sp["ret_gn_g"]), row(sp["ret_gn_b"])
    lre, lim = row(sp["s5_lambda_re"]), row(sp["s5_lambda_im"])
    ldt = row(jnp.repeat(sp["s5_log_dt"].reshape(-1), S5_P))
    to_t = lambda b: b.reshape(S5_G, S5_P, S5_H).transpose(2, 0, 1).reshape(S5_H, S5_N)
    bre_t, bim_t = to_t(sp["s5_b_re"]), to_t(sp["s5_b_im"])
    to_w = lambda c: jnp.tile(c.reshape(S5_W, S5_P), (1, 2))
    cre_w, cim_w = to_w(sp["s5_c_re"]), to_w(sp["s5_c_im"])

    jobs = (lambda *j: list(j)) if distributed else (lambda *j: [])
    c_arr = jnp.reshape(lax.axis_index("c"), (1,)).astype(jnp.int32) if distributed else None
    (xhat0, rstd0), bg = _ln_in(x2d, meta, jobs(*([_job_gather(w_int), _job_gather(w_glu)] if distributed else [])),
                                gather_meta=distributed)
    if distributed:
        w_int, w_glu = bg[1].reshape(PROJ_W, D_MODEL), bg[2].reshape(S5_W, S5_W)
    s5_small = (lre, lim, ldt, bre_t, bim_t, cre_w, cim_w, row(sp["s5_d"]), w_glu, row(sp["s5_b_glu"]))
    (u, q, k, v, gate), bg = _in_proj(xhat0, li_g, li_b, w_int, cos2, sin2,
                                      jobs(_job_gather(w_out) if distributed else None))
    if distributed:
        w_out = bg[0].reshape(D_MODEL, D_MODEL)
    (ys5, xr, xi), bg = _s5_fwd(u, *s5_small, jobs=jobs(_job_gather(w_up) if distributed else None))
    if distributed:
        w_up = bg[0]
    (o, states), _ = _ret_fwd(q, k, v)
    (ycat, xhat1, rstd1, h1b, pre), bg = _post_up(o, gate, ys5, xhat0, gn_g, gn_b, li_g, li_b, l1_g, l1_b, w_out, w_up,
                                                  jobs(_job_gather(w_down) if distributed else None))
    if distributed:
        w_down = bg[0].reshape(D_FF, D_MODEL)
    dr2, dffb, loss8, dl2g, dl2b = _post_down(pre, xhat1, tgt, l1_g, l1_b, l2_g, l2_b, w_down)
    g_up, g_down, dh1m = _mlp_bwd(h1b, dffb, pre, w_up, w_down)
    (do, dgate, dys5, dh0r, g_out, dl1g, dl1b, dgng, dgnb), bg = _post_bwd(
        dh1m, dr2, xhat1, rstd1, ycat, o, gate, gn_g, gn_b, l1_g, w_out,
        jobs(*([_job_pair(g_up), _job_pair(g_down)] if distributed else [])))
    g_out = g_out.reshape(N_DEV, D_MODEL // N_DEV, D_MODEL)
    after = jnp.zeros((8, LANE), F32)
    if distributed:
        p_up, p_down = _pair_sum([g_up, g_down], bg, c_arr, "pair_sum_mlp")
        started_mlp, after = _chips_start([p_up, p_down], "chips_mlp_start")
    (du, dlre, dlim, dldt, dbre_t, dbim_t, dcre, dcim, dd, dwglu, dbglu), bg = _s5_bwd(
        dys5, u, xr, xi, *s5_small, after, jobs=jobs(_job_pair(g_out) if distributed else None))
    if distributed:
        (p_out,) = _pair_sum([g_out], bg, c_arr, "pair_sum_out")
    from_t = lambda t: t.reshape(S5_H, S5_G, S5_P).transpose(1, 0, 2)
    small = {
        "s5_lambda_re": dlre, "s5_lambda_im": dlim, "s5_log_dt": dldt[:, :S5_G],
        "s5_b_re": from_t(dbre_t), "s5_b_im": from_t(dbim_t), "s5_c_re": dcre, "s5_c_im": dcim, "s5_d": dd,
        "s5_b_glu": dbglu, "ret_gn_g": dgng, "ret_gn_b": dgnb, "ln1_g": dl1g, "ln1_b": dl1b, "ln2_g": dl2g, "ln2_b": dl2b,
        "s5_w_glu": dwglu, "loss": loss8[0:1, 0:1]}
    early_pack = _pack([small[n] for n in EARLY])
    (dq, dk, dv), bg = _ret_bwd(q, k, v, do, states, cos2, sin2,
                                jobs(*([_job_chips(p_out), _job_gather(early_pack)] if distributed else [])))
    g_int = _in_w_grad(du, dq, dk, dv, dgate, xhat0, li_g, li_b).reshape(N_DEV, PROJ_W // N_DEV, D_MODEL)
    after = jnp.zeros((8, LANE), F32)
    if distributed:
        (r1_in,) = _exchange([_job_pair(g_int)], "exchange_pair_in")
        (p_in,) = _pair_sum([g_int], [r1_in], c_arr, "pair_sum_in")
        (p_up, p_down), (r_up, r_down) = _chips_wait(started_mlp, p_in, "chips_mlp_wait")
        started_in, after = _chips_start([p_in], "chips_in_start")
    grad_x, dmeta, dlig, dlib = _in_bwd(du, dq, dk, dv, dgate, dh0r, xhat0, rstd0, li_g, w_int, after)
    if distributed:
        p_out, dlig = lax.optimization_barrier((p_out, dlig))
        big = dict(chip_sums=[p_out, p_up, p_down], received=[bg[0], r_up, r_down], early=bg[1], started_in=started_in)
    else:
        big = dict(partials=[g_int, g_out, g_up, g_down])
    small.update(ln_in_g=dlig, ln_in_b=dlib, meta_tokens=dmeta)
    return grad_x, big, small


def kernel(x, meta_tokens, ln_in_g, ln_in_b, w_in, s5_lambda_re, s5_lambda_im, s5_log_dt, s5_b_re, s5_b_im, s5_c_re, s5_c_im, s5_d, s5_w_glu, s5_b_glu, ret_gn_g, ret_gn_b, w_out, ln1_g, ln1_b, w_up, w_down, ln2_g, ln2_b, loss_target, m_meta_tokens, m_ln_in_g, m_ln_in_b, m_w_in, m_s5_lambda_re, m_s5_lambda_im, m_s5_log_dt, m_s5_b_re, m_s5_b_im, m_s5_c_re, m_s5_c_im, m_s5_d, m_s5_w_glu, m_s5_b_glu, m_ret_gn_g, m_ret_gn_b, m_w_out, m_ln1_g, m_ln1_b, m_w_up, m_w_down, m_ln2_g, m_ln2_b, v_meta_tokens, v_ln_in_g, v_ln_in_b, v_w_in, v_s5_lambda_re, v_s5_lambda_im, v_s5_log_dt, v_s5_b_re, v_s5_b_im, v_s5_c_re, v_s5_c_im, v_s5_d, v_s5_w_glu, v_s5_b_glu, v_ret_gn_g, v_ret_gn_b, v_w_out, v_ln1_g, v_ln1_b, v_w_up, v_w_down, v_ln2_g, v_ln2_b):
    args = dict(locals())
    names = ["meta_tokens", "ln_in_g", "ln_in_b", "w_in", "s5_lambda_re", "s5_lambda_im", "s5_log_dt", "s5_b_re", "s5_b_im",
             "s5_c_re", "s5_c_im", "s5_d", "s5_w_glu", "s5_b_glu", "ret_gn_g", "ret_gn_b", "w_out", "ln1_g", "ln1_b",
             "w_up", "w_down", "ln2_g", "ln2_b"]
    ax, ay, ac = _place()
    me = 4 * ax + 2 * ay + ac

    sp = {n: args[n] for n in SMALL}
    grad_x, big, small = _local_step(x[0], loss_target[0], meta_tokens, w_in[0].T.astype(MM), w_out[0].astype(MM),
                                   w_up[0].astype(MM), w_down[0].astype(MM), s5_w_glu[0].astype(MM), sp, True)

    j_arr = jnp.reshape(2 * ax + ay, (1,)).astype(jnp.int32)
    two_d = lambda a: a.reshape(a.shape[-2:])
    item = lambda n, p, r: (p, r, *(two_d(_view(n, a)) for a in (args[n], args["m_" + n], args["v_" + n])))
    late_pack = _pack([small[n] for n in LATE])
    mlp = ("w_out", "w_up", "w_down")
    res, _ = _adamw_shards(
        [item(n, p, r) for n, p, r in zip(mlp, big["chip_sums"], big["received"])], "adamw_mlp", 8, j_arr)
    (p_in,), (r_in,) = _chips_wait(big["started_in"], res[0], "chips_in_wait")
    late_pack, r_in = lax.optimization_barrier((late_pack, r_in))
    (late_all,) = _exchange([_job_gather(late_pack)], "gather_small_late")
    res_in, _ = _adamw_shards([item("w_in", p_in, r_in)], "adamw_in", 2, j_arr)
    upd = {"w_in": res_in}
    for idx, n in enumerate(mlp):
        upd[n] = res[4 * idx:4 * idx + 4]
    shard_grads = {n: upd[n][0] for n in upd}

    early_shapes = [_view(n, args[n]).shape for n in EARLY[:-2]] + [(S5_W, S5_W), (1,)]
    late_shapes = [args["ln_in_g"].shape, args["ln_in_b"].shape, (N_META, D_MODEL)]
    g_small = dict(zip(EARLY, _unpack(_sum_devices(big["early"], "sum_small_early"), early_shapes)))
    g_small.update(zip(LATE, _unpack(_sum_devices(late_all, "sum_small_late"), late_shapes)))
    loss = g_small["loss"].reshape(())

    shard_grads["meta_tokens"] = lax.dynamic_slice(g_small["meta_tokens"], (0, me * (D_MODEL // N_DEV)),
                                                   (N_META, D_MODEL // N_DEV))
    shard_grads["s5_w_glu"] = lax.dynamic_slice(g_small["s5_w_glu"], (me * (S5_W // N_DEV), 0),
                                                (S5_W // N_DEV, S5_W))[None]
    natives = SMALL + ["meta_tokens", "s5_w_glu"]
    res2 = _adamw_native([(shard_grads[n] if n in shard_grads else g_small[n], *(_view(n, args[p + n]) for p in ("", "m_", "v_")))
                          for n in natives], "adamw_small")
    for idx, n in enumerate(natives):
        upd[n] = [shard_grads[n] if n in shard_grads else g_small[n]] + list(res2[3 * idx:3 * idx + 3])

    grads, deltas, new_m, new_v = ([_view(n, upd[n][t]).reshape(args[n].shape) for n in names] for t in range(4))
    return (loss, grad_x[None], *grads, *deltas, *new_m, *new_v)
```

```python
import math

import jax
import jax.numpy as jnp
from jax import lax
from jax.experimental import pallas as pl
from jax.experimental.pallas import tpu as pltpu

F32 = jnp.float32
MM = jnp.bfloat16

D_MODEL = 1024
N_META = 16
CHUNK = 128
PAD = CHUNK - N_META
S5_W, S5_G, S5_H, S5_P = 256, 16, 16, 64
S5_N = S5_G * S5_P
RET_W, RET_H, HEAD = 768, 6, 128
D_FF = 4096
PROJ_W = S5_W + 4 * RET_W
N_DEV = 8
FF_BLK = D_FF // N_DEV
ROW_BLK = 384
MLP_ROWS = 1408
PROJ_ROWS = 704
ALPHA = 2.0 ** 0.25
LN_EPS = 1e-5
GN_EPS = 1e-5
ROPE_BASE = 10000.0
GELU_C = math.sqrt(2.0 / math.pi)
GELU_A = 0.044715
ADAM_LR, ADAM_B1, ADAM_B2, ADAM_EPS, ADAM_WD, ADAM_STEP = 0.001, 0.9, 0.999, 1e-08, 0.01, 10
VMEM_LIMIT = 60 * 1024 * 1024

_VMEM = pl.BlockSpec(memory_space=pltpu.VMEM)
_ANY = pl.BlockSpec(memory_space=pl.ANY)
_MESH = pl.DeviceIdType.MESH


def _params(sem=None):
    return pltpu.CompilerParams(dimension_semantics=sem, vmem_limit_bytes=VMEM_LIMIT)


def _dot(a, b):
    return jnp.dot(a.astype(MM), b.astype(MM), preferred_element_type=F32)


def _dot_nt(a, b):
    return lax.dot_general(a.astype(MM), b.astype(MM), (((1,), (1,)), ((), ())), preferred_element_type=F32)


def _dot_tn(a, b):
    return lax.dot_general(a.astype(MM), b.astype(MM), (((0,), (0,)), ((), ())), preferred_element_type=F32)


def _split3(a):
    hi = a.astype(jnp.bfloat16)
    r1 = a - hi.astype(F32)
    mid = r1.astype(jnp.bfloat16)
    lo = (r1 - mid.astype(F32)).astype(jnp.bfloat16)
    return hi, mid, lo


def _dot_sel_rhs(a, sel):
    s = sel.astype(jnp.bfloat16)
    return sum(jnp.dot(p, s, preferred_element_type=F32) for p in _split3(a))


def _dot_sel_lhs(sel, b):
    s = sel.astype(jnp.bfloat16)
    return sum(jnp.dot(s, p, preferred_element_type=F32) for p in _split3(b))


def _ln_fwd(r, eps):
    mu = jnp.mean(r, axis=-1, keepdims=True)
    xc = r - mu
    var = jnp.mean(xc * xc, axis=-1, keepdims=True)
    rstd = lax.rsqrt(var + eps)
    return xc * rstd, rstd


def _ln_bwd(dxhat, xhat, rstd):
    m1 = jnp.mean(dxhat, axis=-1, keepdims=True)
    m2 = jnp.mean(dxhat * xhat, axis=-1, keepdims=True)
    return rstd * (dxhat - m1 - xhat * m2)


def _colsum(a):
    return jnp.sum(a, axis=0, keepdims=True)


def _shift3(n_in, block=lambda i: i):
    return [pl.BlockSpec((CHUNK, D_MODEL), (lambda i, j=j: (jnp.clip(3 * block(i) - 1 + j, 0, n_in - 1), 0)))
            for j in range(3)]


def _ln_in(x2d, meta, jobs=(), gather_meta=False):
    seq = x2d.shape[0]
    tp = seq + CHUNK
    R = ROW_BLK
    nb = tp // R
    shard_w = D_MODEL // N_DEV

    def body(xa, xb, xc, meta_ref, xhat_ref, rstd_ref, raw_ref, *gathered):
        raw_ref[0:CHUNK, :] = xa[...]
        raw_ref[CHUNK:2 * CHUNK, :] = xb[...]
        raw_ref[2 * CHUNK:3 * CHUNK, :] = xc[...]

        @pl.when(pl.program_id(0) == nb - 1)
        def _():
            raw_ref[0:PAD, :] = jnp.zeros((PAD, D_MODEL), F32)
            if gather_meta:
                for d in range(N_DEV):
                    pltpu.sync_copy(gathered[0].at[d], raw_ref.at[PAD:CHUNK, d * shard_w:(d + 1) * shard_w])
            else:
                raw_ref[PAD:CHUNK, :] = meta_ref[...]

        xhat_ref[...], rstd_ref[...] = _ln_fwd(raw_ref[...], LN_EPS)

    row = lambda w: pl.BlockSpec((R, w), lambda i: (nb - 1 - i, 0))
    jobs = ([_job_gather(meta)] if gather_meta else []) + list(jobs)
    return _call(
        body, "ln_in", (nb,),
        _shift3(seq // CHUNK, lambda i: nb - 1 - i) + [pl.BlockSpec(meta.shape, lambda i: (0, 0))],
        [row(D_MODEL), row(1)], [jax.ShapeDtypeStruct((tp, D_MODEL), F32), jax.ShapeDtypeStruct((tp, 1), F32)],
        [pltpu.VMEM((R, D_MODEL), F32)], (x2d, x2d, x2d, meta), jobs, early=1 if gather_meta else 0)


def _in_proj(xhat0, ln_g, ln_b, w_int, cos2, sin2, jobs=()):
    tp = xhat0.shape[0]
    R = PROJ_ROWS if tp % PROJ_ROWS == 0 else ROW_BLK

    def body(xh_ref, g_ref, b_ref, w_ref, cos_ref, sin_ref, u_ref, q_ref, k_ref, v_ref, gate_ref):
        hb = (xh_ref[...] * g_ref[...] + b_ref[...]).astype(MM)
        valid = (pl.program_id(0) * R + lax.broadcasted_iota(jnp.int32, (R, 1), 0)) >= PAD

        def seg(lo, hi):
            return jnp.where(valid, _dot_nt(hb, w_ref[lo:hi, :]), 0.0)

        u_ref[...] = seg(0, S5_W)
        cos = cos_ref[...]
        sin = sin_ref[...]
        q = seg(S5_W, S5_W + RET_W)
        k = seg(S5_W + RET_W, S5_W + 2 * RET_W)
        for h in range(RET_H):
            sl = slice(h * HEAD, (h + 1) * HEAD)
            qh = q[:, sl]
            kh = k[:, sl]
            q_ref[:, sl] = (qh * cos + pltpu.roll(qh, HEAD // 2, 1) * sin).astype(q_ref.dtype)
            k_ref[:, sl] = ((kh * cos + pltpu.roll(kh, HEAD // 2, 1) * sin) * (HEAD ** -0.5)).astype(k_ref.dtype)
        v_ref[...] = seg(S5_W + 2 * RET_W, S5_W + 3 * RET_W).astype(v_ref.dtype)
        gate_ref[...] = seg(S5_W + 3 * RET_W, PROJ_W)

    def rows(w, dt):
        return pl.BlockSpec((R, w), lambda i: (i, 0)), jax.ShapeDtypeStruct((tp, w), dt)

    outs = [rows(S5_W, F32), rows(RET_W, MM), rows(RET_W, MM), rows(RET_W, MM), rows(RET_W, F32)]
    full = lambda s: pl.BlockSpec(s, lambda i: (0,) * len(s))
    return _call(
        body, "in_proj", (tp // R,),
        [pl.BlockSpec((R, D_MODEL), lambda i: (i, 0)), full((1, D_MODEL)), full((1, D_MODEL)), _VMEM,
         pl.BlockSpec((R, HEAD), lambda i: (i, 0)), pl.BlockSpec((R, HEAD), lambda i: (i, 0))],
        [o[0] for o in outs], [o[1] for o in outs], [], (xhat0, ln_g, ln_b, w_int, cos2, sin2), jobs)


def _s5_disc(lre, lim, ldt, bre_t, bim_t):
    dt = jnp.exp(ldt)
    mag = jnp.exp(lre * dt)
    ang = lim * dt
    lbr = mag * jnp.cos(ang)
    lbi = mag * jnp.sin(ang)
    den = lre * lre + lim * lim
    nr = lbr - 1.0
    qr = (nr * lre + lbi * lim) / den
    qi = (lbi * lre - nr * lim) / den
    return lbr, lbi, qr * bre_t - qi * bim_t, qr * bim_t + qi * bre_t


def _s5_tables(lbr, lbi, reverse):
    if reverse:
        lbi = -lbi
    pw = [(lbr, lbi)]
    for _ in range(7):
        r, i = pw[-1]
        pw.append((r * lbr - i * lbi, r * lbi + i * lbr))
    row = lax.broadcasted_iota(jnp.int32, (8, S5_N), 0)
    tabs = []
    for k in range(3):
        sh = 2 ** k
        mask = (row < 8 - sh) if reverse else (row >= sh)
        ar, ai = pw[sh - 1]
        tabs.append((jnp.where(mask, ar, 0.0), jnp.where(mask, ai, 0.0)))
    pr = jnp.zeros((8, S5_N), F32)
    pi = jnp.zeros((8, S5_N), F32)
    for i in range(8):
        ar, ai = pw[7 - i] if reverse else pw[i]
        pr = jnp.where(row == i, ar, pr)
        pi = jnp.where(row == i, ai, pi)
    tabs.append((pr, pi))
    return tabs


def _store_tables(tab_ref, tabs):
    for k, (r, i) in enumerate(tabs):
        tab_ref[2 * k] = r
        tab_ref[2 * k + 1] = i


def _bd_mask():
    r = lax.broadcasted_iota(jnp.int32, (S5_W, S5_N), 0)
    c = lax.broadcasted_iota(jnp.int32, (S5_W, S5_N), 1)
    return jnp.right_shift(r, 4) == jnp.right_shift(c, 6)


def _s5_block_diag(bbr_t, bbi_t, cre_w, cim_w):
    mask = _bd_mask()
    bd = lambda t: jnp.where(mask, t, 0.0)
    return (bd(jnp.tile(bbr_t, (S5_G, 1))), bd(jnp.tile(bbi_t, (S5_G, 1))),
            bd(jnp.tile(cre_w, (1, S5_N // HEAD))), bd(jnp.tile(cim_w, (1, S5_N // HEAD))))


def _scan8(xr, xi, tab_ref, lanes, reverse):
    for k in range(3):
        sh = (8 - 2 ** k) if reverse else 2 ** k
        sr = pltpu.roll(xr, sh, 0)
        si = pltpu.roll(xi, sh, 0)
        mr = tab_ref[2 * k, :, lanes]
        mi = tab_ref[2 * k + 1, :, lanes]
        xr, xi = xr + (mr * sr - mi * si), xi + (mr * si + mi * sr)
    return xr, xi


S5_LANES = 512


def _gelu(y):
    t = jnp.tanh(GELU_C * (y + GELU_A * y * y * y))
    return 0.5 * y * (1.0 + t), t


def _s5_fwd(u, lre, lim, ldt, bre_t, bim_t, cre_w, cim_w, d_row, w_glu, b_glu, jobs=()):
    tp = u.shape[0]
    R = ROW_BLK

    def body(u_ref, lre_ref, lim_ref, ldt_ref, bre_ref, bim_ref, cre_ref, cim_ref, d_ref, wg_ref, bg_ref,
             y_ref, xr_ref, xi_ref, bbd_r, bbd_i, cbd_r, cbd_i, tab_ref, car_r, car_i):
        @pl.when(pl.program_id(0) == 0)
        def _():
            lbr, lbi, bbr, bbi = _s5_disc(lre_ref[...], lim_ref[...], ldt_ref[...], bre_ref[...], bim_ref[...])
            br, bi, cr, ci = _s5_block_diag(bbr, bbi, cre_ref[...], cim_ref[...])
            bbd_r[...] = br.astype(MM)
            bbd_i[...] = bi.astype(MM)
            cbd_r[...] = cr.astype(MM)
            cbd_i[...] = ci.astype(MM)
            _store_tables(tab_ref, _s5_tables(lbr, lbi, False))
            car_r[...] = jnp.zeros_like(car_r)
            car_i[...] = jnp.zeros_like(car_i)

        u = u_ref[...]
        ub = u.astype(MM)
        xr_ref[...] = jnp.dot(ub, bbd_r[...], preferred_element_type=F32)
        xi_ref[...] = jnp.dot(ub, bbd_i[...], preferred_element_type=F32)
        for j in range(S5_N // S5_LANES):
            lanes = pl.ds(j * S5_LANES, S5_LANES)
            pr = tab_ref[6, :, lanes]
            pi = tab_ref[7, :, lanes]

            def step(g, carry):
                cr, ci = carry
                rows = pl.ds(pl.multiple_of(g * 8, 8), 8)
                xr, xi = _scan8(xr_ref[rows, lanes], xi_ref[rows, lanes], tab_ref, lanes, False)
                br = jnp.broadcast_to(cr[7:8, :], cr.shape)
                bi = jnp.broadcast_to(ci[7:8, :], ci.shape)
                xr = xr + (pr * br - pi * bi)
                xi = xi + (pr * bi + pi * br)
                xr_ref[rows, lanes] = xr
                xi_ref[rows, lanes] = xi
                return xr, xi

            cr, ci = lax.fori_loop(0, R // 8, step, (car_r[:, lanes], car_i[:, lanes]), unroll=2)
            car_r[:, lanes] = cr
            car_i[:, lanes] = ci
        y = _dot_nt(xr_ref[...], cbd_r[...]) - _dot_nt(xi_ref[...], cbd_i[...]) + d_ref[...] * u
        yg, _ = _gelu(y)
        z = _dot(yg, wg_ref[...]) + bg_ref[...]
        y_ref[...] = yg * jax.nn.sigmoid(z)

    full = lambda a: pl.BlockSpec(a.shape, lambda i: (0,) * a.ndim)
    small = [lre, lim, ldt, bre_t, bim_t, cre_w, cim_w, d_row, w_glu, b_glu]
    return _call(
        body, "s5_fwd", (tp // R,),
        [pl.BlockSpec((R, S5_W), lambda i: (i, 0))] + [full(a) for a in small],
        [pl.BlockSpec((R, S5_W), lambda i: (i, 0)), pl.BlockSpec((R, S5_N), lambda i: (i, 0)),
         pl.BlockSpec((R, S5_N), lambda i: (i, 0))],
        [jax.ShapeDtypeStruct((tp, S5_W), F32), jax.ShapeDtypeStruct((tp, S5_N), F32),
         jax.ShapeDtypeStruct((tp, S5_N), F32)],
        [pltpu.VMEM((S5_W, S5_N), MM)] * 4 + [pltpu.VMEM((8, 8, S5_N), F32), pltpu.VMEM((8, S5_N), F32),
                                              pltpu.VMEM((8, S5_N), F32)],
        (u, *small), jobs)


def _s5_bwd(dy_out, u, xr, xi, lre, lim, ldt, bre_t, bim_t, cre_w, cim_w, d_row, w_glu, b_glu, after, jobs=()):
    tp = u.shape[0]
    R = ROW_BLK
    nb = tp // R

    def body(dyo_ref, u_ref, xr_ref, xi_ref, xpr_ref, xpi_ref,
             lre_ref, lim_ref, ldt_ref, bre_ref, bim_ref, cre_ref, cim_ref, d_ref, wg_ref, bg_ref, after_ref,
             du_ref, dlre_ref, dlim_ref, dldt_ref, dbre_ref, dbim_ref, dcre_ref, dcim_ref, dd_ref, dwg_ref, dbg_ref,
             bbd_r, bbd_i, cbd_r, cbd_i, tab_ref, car_r, car_i, gr_ref, gi_ref, xer_ref, xei_ref,
             abr, abi, acr, aci, adr, adi):
        i = pl.program_id(0)

        @pl.when(i == 0)
        def _():
            lbr, lbi, bbr, bbi = _s5_disc(lre_ref[...], lim_ref[...], ldt_ref[...], bre_ref[...], bim_ref[...])
            br, bi, cr, ci = _s5_block_diag(bbr, bbi, cre_ref[...], cim_ref[...])
            bbd_r[...] = br.astype(MM)
            bbd_i[...] = bi.astype(MM)
            cbd_r[...] = cr.astype(MM)
            cbd_i[...] = ci.astype(MM)
            _store_tables(tab_ref, _s5_tables(lbr, lbi, True))
            for ref in (car_r, car_i, abr, abi, acr, aci, adr, adi, dd_ref, dwg_ref, dbg_ref):
                ref[...] = jnp.zeros_like(ref)

        u = u_ref[...]
        xrv = xr_ref[...]
        xiv = xi_ref[...]
        y = _dot_nt(xrv, cbd_r[...]) - _dot_nt(xiv, cbd_i[...]) + d_ref[...] * u
        yg, t = _gelu(y)
        z = _dot(yg, wg_ref[...]) + bg_ref[...]
        s = jax.nn.sigmoid(z)
        dout = dyo_ref[...]
        dz = dout * yg * s * (1.0 - s)
        dyg = dout * s + _dot_nt(dz, wg_ref[...])
        dwg_ref[...] += _dot_tn(yg, dz)
        dbg_ref[...] += _colsum(dz)
        dy = dyg * (0.5 * (1.0 + t) + 0.5 * y * (1.0 - t * t) * GELU_C * (1.0 + 3.0 * GELU_A * y * y))
        dd_ref[...] += _colsum(dy * u)
        acr[...] += _dot_tn(dy, xrv)
        aci[...] -= _dot_tn(dy, xiv)
        gr_ref[...] = _dot(dy, cbd_r[...])
        gi_ref[...] = -_dot(dy, cbd_i[...])
        has_prev = (i < nb - 1).astype(F32)
        xer_ref[0:8, :] = xpr_ref[...] * has_prev
        xei_ref[0:8, :] = xpi_ref[...] * has_prev
        xer_ref[8:R + 8, :] = xrv
        xei_ref[8:R + 8, :] = xiv
        row = lax.broadcasted_iota(jnp.int32, (8, S5_LANES), 0)
        for j in range(S5_N // S5_LANES):
            lanes = pl.ds(j * S5_LANES, S5_LANES)
            pr = tab_ref[6, :, lanes]
            pi = tab_ref[7, :, lanes]

            def step(n, carry):
                cr, ci, sar, sai = carry
                g = R // 8 - 1 - n
                r0 = pl.multiple_of(g * 8, 8)
                rows = pl.ds(r0, 8)
                gr, gi = _scan8(gr_ref[rows, lanes], gi_ref[rows, lanes], tab_ref, lanes, True)
                br = jnp.broadcast_to(cr[0:1, :], cr.shape)
                bi = jnp.broadcast_to(ci[0:1, :], ci.shape)
                gr = gr + (pr * br - pi * bi)
                gi = gi + (pr * bi + pi * br)
                gr_ref[rows, lanes] = gr
                gi_ref[rows, lanes] = gi
                last = row == 7
                xpr = pltpu.roll(jnp.where(last, xer_ref[rows, lanes], xer_ref[pl.ds(r0 + 8, 8), lanes]), 1, 0)
                xpi = pltpu.roll(jnp.where(last, xei_ref[rows, lanes], xei_ref[pl.ds(r0 + 8, 8), lanes]), 1, 0)
                return gr, gi, sar + (gr * xpr + gi * xpi), sai + (gi * xpr - gr * xpi)

            cr, ci, sar, sai = lax.fori_loop(
                0, R // 8, step, (car_r[:, lanes], car_i[:, lanes], adr[:, lanes], adi[:, lanes]), unroll=2)
            car_r[:, lanes] = cr
            car_i[:, lanes] = ci
            adr[:, lanes] = sar
            adi[:, lanes] = sai
        grv = gr_ref[...]
        giv = gi_ref[...]
        du_ref[...] = (dy * d_ref[...] + _dot_nt(grv, bbd_r[...]) + _dot_nt(giv, bbd_i[...])).astype(du_ref.dtype)
        abr[...] += _dot_tn(u, grv)
        abi[...] += _dot_tn(u, giv)

        @pl.when(i == nb - 1)
        def _():
            mask = _bd_mask()
            r16 = lax.broadcasted_iota(jnp.int32, (S5_H, S5_W), 1)
            h16 = lax.broadcasted_iota(jnp.int32, (S5_H, S5_W), 0)
            fold_b = jnp.bitwise_and(r16, S5_H - 1) == h16
            c64 = lax.broadcasted_iota(jnp.int32, (S5_N, S5_P), 0)
            p64 = lax.broadcasted_iota(jnp.int32, (S5_N, S5_P), 1)
            fold_c = jnp.bitwise_and(c64, S5_P - 1) == p64
            dbbr = _dot_sel_lhs(fold_b, jnp.where(mask, abr[...], 0.0))
            dbbi = _dot_sel_lhs(fold_b, jnp.where(mask, abi[...], 0.0))
            dcre_ref[...] = _dot_sel_rhs(jnp.where(mask, acr[...], 0.0), fold_c)
            dcim_ref[...] = _dot_sel_rhs(jnp.where(mask, aci[...], 0.0), fold_c)
            dlbr = _colsum(adr[...])
            dlbi = _colsum(adi[...])
            _, vjp = jax.vjp(_s5_disc, lre_ref[...], lim_ref[...], ldt_ref[...], bre_ref[...], bim_ref[...])
            dlre, dlim, dldt, dbre, dbim = vjp((dlbr, dlbi, dbbr, dbbi))
            dlre_ref[...] = dlre
            dlim_ref[...] = dlim
            dbre_ref[...] = dbre
            dbim_ref[...] = dbim
            gsel = jnp.right_shift(lax.broadcasted_iota(jnp.int32, (S5_N, HEAD), 0), 6) == \
                lax.broadcasted_iota(jnp.int32, (S5_N, HEAD), 1)
            dldt_ref[...] = _dot_sel_rhs(dldt, gsel)

    full = lambda a: pl.BlockSpec(a.shape, lambda i: (0,) * a.ndim)
    rev = lambda w: pl.BlockSpec((R, w), lambda i: (nb - 1 - i, 0))
    prev8 = pl.BlockSpec((8, S5_N), lambda i: (jnp.maximum((nb - 1 - i) * (R // 8) - 1, 0), 0))
    small = [lre, lim, ldt, bre_t, bim_t, cre_w, cim_w, d_row, w_glu, b_glu]
    outs = [((tp, S5_W), rev(S5_W))] + [
        (s, pl.BlockSpec(s, lambda i: (0, 0))) for s in
        [(1, S5_N), (1, S5_N), (1, HEAD), (S5_H, S5_N), (S5_H, S5_N), (S5_W, S5_P), (S5_W, S5_P),
         (1, S5_W), (S5_W, S5_W), (1, S5_W)]]
    return _call(
        body, "s5_bwd", (nb,),
        [rev(S5_W), rev(S5_W), rev(S5_N), rev(S5_N), prev8, prev8] + [full(a) for a in small + [after]],
        [o[1] for o in outs], [jax.ShapeDtypeStruct(o[0], MM if n == 0 else F32) for n, o in enumerate(outs)],
        [pltpu.VMEM((S5_W, S5_N), MM)] * 4 + [
            pltpu.VMEM((8, 8, S5_N), F32), pltpu.VMEM((8, S5_N), F32), pltpu.VMEM((8, S5_N), F32),
            pltpu.VMEM((R, S5_N), F32), pltpu.VMEM((R, S5_N), F32),
            pltpu.VMEM((R + 8, S5_N), F32), pltpu.VMEM((R + 8, S5_N), F32)] + [pltpu.VMEM((S5_W, S5_N), F32)] * 4 + [
            pltpu.VMEM((8, S5_N), F32), pltpu.VMEM((8, S5_N), F32)],
        (dy_out, u, xr, xi, xr, xi, *small, after), jobs)


RET_CHUNK = ROW_BLK
LOG_GAMMA = [math.log1p(-2.0 ** (-5 - h)) for h in range(RET_H)]
GAMMA_CHUNK = [math.exp(RET_CHUNK * lg) for lg in LOG_GAMMA]
_DECAY_SCRATCH = [pltpu.VMEM((RET_H, RET_CHUNK, RET_CHUNK), F32), pltpu.VMEM((RET_H, RET_CHUNK, HEAD), F32),
                  pltpu.VMEM((RET_H, RET_CHUNK, HEAD), F32)]


def _fill_decay(dm_ref, ze_ref, xi_ref):
    C = RET_CHUNK
    diff = (lax.broadcasted_iota(jnp.int32, (C, C), 0) - lax.broadcasted_iota(jnp.int32, (C, C), 1)).astype(F32)
    r = lax.broadcasted_iota(jnp.int32, (C, HEAD), 0).astype(F32)
    for h, lg in enumerate(LOG_GAMMA):
        dm_ref[h] = jnp.where(diff >= 0.0, jnp.exp(jnp.maximum(diff, 0.0) * lg), 0.0)
        ze_ref[h] = jnp.exp((C - 1.0 - r) * lg)
        xi_ref[h] = jnp.exp((r + 1.0) * lg)


def _ret_fwd(q, k, v, jobs=()):
    tp = q.shape[0]
    C = RET_CHUNK
    nc = tp // C

    def body(q_ref, k_ref, v_ref, o_ref, st_ref, s_ref, dm_ref, ze_ref, xi_ref):
        @pl.when(pl.program_id(0) == 0)
        def _():
            s_ref[...] = jnp.zeros_like(s_ref)
            _fill_decay(dm_ref, ze_ref, xi_ref)

        for h in range(RET_H):
            sl = slice(h * HEAD, (h + 1) * HEAD)
            qh, kh, vh = q_ref[:, sl], k_ref[:, sl], v_ref[:, sl]
            sh = s_ref[h]
            st_ref[0, sl, :] = sh
            scores = _dot_nt(qh, kh) * dm_ref[h]
            o_ref[:, sl] = _dot(scores, vh) + _dot(qh, sh) * xi_ref[h]
            s_ref[h] = GAMMA_CHUNK[h] * sh + _dot_tn(kh.astype(F32) * ze_ref[h], vh)

    blk = pl.BlockSpec((C, RET_W), lambda c: (c, 0))
    return _call(
        body, "ret_fwd", (nc,), [blk, blk, blk], [blk, pl.BlockSpec((1, RET_W, HEAD), lambda c: (c, 0, 0))],
        [jax.ShapeDtypeStruct((tp, RET_W), F32), jax.ShapeDtypeStruct((nc, RET_W, HEAD), F32)],
        [pltpu.VMEM((RET_H, HEAD, HEAD), F32)] + _DECAY_SCRATCH, (q, k, v), jobs)


def _ret_bwd(q, k, v, do, states, cos2, sin2, jobs=()):
    tp = q.shape[0]
    C = RET_CHUNK
    nc = tp // C

    def body(q_ref, k_ref, v_ref, do_ref, st_ref, cos_ref, sin_ref,
             dq_ref, dk_ref, dv_ref, ds_ref, dm_ref, ze_ref, xi_ref):
        @pl.when(pl.program_id(0) == 0)
        def _():
            ds_ref[...] = jnp.zeros_like(ds_ref)
            _fill_decay(dm_ref, ze_ref, xi_ref)

        cos = cos_ref[...]
        sin = sin_ref[...]
        for h in range(RET_H):
            sl = slice(h * HEAD, (h + 1) * HEAD)
            qh, kh, vh = q_ref[:, sl], k_ref[:, sl], v_ref[:, sl]
            dmh = dm_ref[h]
            sh = st_ref[0, sl, :]
            dsn = ds_ref[h]
            doh = do_ref[:, sl]
            dox = doh * xi_ref[h]
            a = _dot_nt(qh, kh) * dmh
            dqk = _dot_nt(doh, vh) * dmh
            kz = kh.astype(F32) * ze_ref[h]
            dv_ref[:, sl] = (_dot_tn(a, doh) + _dot(kz, dsn)).astype(dv_ref.dtype)
            dqr = _dot(dqk, kh) + _dot_nt(dox, sh)
            dkr = _dot_tn(dqk, qh) + ze_ref[h] * _dot_nt(vh, dsn)
            ds_ref[h] = GAMMA_CHUNK[h] * dsn + _dot_tn(qh, dox)
            dq_ref[:, sl] = (dqr * cos - pltpu.roll(dqr, HEAD // 2, 1) * sin).astype(dq_ref.dtype)
            dk_ref[:, sl] = ((dkr * cos - pltpu.roll(dkr, HEAD // 2, 1) * sin) * (HEAD ** -0.5)).astype(dk_ref.dtype)

    blk = pl.BlockSpec((C, RET_W), lambda c: (nc - 1 - c, 0))
    tab = pl.BlockSpec((C, HEAD), lambda c: (nc - 1 - c, 0))
    return _call(
        body, "ret_bwd", (nc,),
        [blk, blk, blk, blk, pl.BlockSpec((1, RET_W, HEAD), lambda c: (nc - 1 - c, 0, 0)), tab, tab],
        [blk, blk, blk], [jax.ShapeDtypeStruct((tp, RET_W), MM)] * 3,
        [pltpu.VMEM((RET_H, HEAD, HEAD), F32)] + _DECAY_SCRATCH, (q, k, v, do, states, cos2, sin2), jobs)


def _gn_gate(o, gate, gn_g, gn_b):
    xhat, rstd = _ln_fwd(o, GN_EPS)
    on = xhat * gn_g + gn_b
    s = jax.nn.sigmoid(gate)
    return gate * s * on, xhat, rstd, on, s


def _post_up(o, gate, ys5, xhat0, gn_g, gn_b, li_g, li_b, l1_g, l1_b, w_out, w_up, jobs=()):
    tp = o.shape[0]
    R = ROW_BLK

    def body(o_ref, g_ref, ys_ref, xh0_ref, gng, gnb, lig, lib, l1g, l1b, wo_ref, wu_ref,
             ycat_ref, xh1_ref, rstd1_ref, h1b_ref, pre_ref):
        ycat_ref[:, 0:S5_W] = ys_ref[...].astype(ycat_ref.dtype)
        for h in range(RET_H):
            sl = slice(h * HEAD, (h + 1) * HEAD)
            yret = _gn_gate(o_ref[:, sl], g_ref[:, sl], gng[:, sl], gnb[:, sl])[0]
            ycat_ref[:, S5_W + h * HEAD:S5_W + (h + 1) * HEAD] = yret.astype(ycat_ref.dtype)
        mixed = _dot(ycat_ref[...], wo_ref[...])
        h0 = xh0_ref[...] * lig[...] + lib[...]
        xh1, rstd1 = _ln_fwd(ALPHA * h0 + mixed, LN_EPS)
        xh1_ref[...] = xh1
        rstd1_ref[...] = rstd1
        h1b = (xh1 * l1g[...] + l1b[...]).astype(MM)
        h1b_ref[...] = h1b
        for d in range(N_DEV):
            pre_ref[:, d * FF_BLK:(d + 1) * FF_BLK] = jnp.maximum(_dot(h1b, wu_ref[d]), 0.0)

    row = lambda w: pl.BlockSpec((R, w), lambda i: (i, 0))
    full = lambda a: pl.BlockSpec(a.shape, lambda i: (0,) * a.ndim)
    vecs = [gn_g, gn_b, li_g, li_b, l1_g, l1_b]
    outs = [(row(D_MODEL), jax.ShapeDtypeStruct((tp, D_MODEL), MM)), (row(D_MODEL), jax.ShapeDtypeStruct((tp, D_MODEL), F32)),
            (row(1), jax.ShapeDtypeStruct((tp, 1), F32)), (row(D_MODEL), jax.ShapeDtypeStruct((tp, D_MODEL), MM)),
            (row(D_FF), jax.ShapeDtypeStruct((tp, D_FF), F32))]
    return _call(
        body, "post_up", (tp // R,),
        [row(RET_W), row(RET_W), row(S5_W), row(D_MODEL)] + [full(a) for a in vecs] + [_VMEM, _VMEM],
        [o[0] for o in outs], [o[1] for o in outs], [], (o, gate, ys5, xhat0, *vecs, w_out, w_up), jobs)


def _post_down(pre, xhat1, tgt, l1_g, l1_b, l2_g, l2_b, w_down):
    tp = pre.shape[0]
    seq = tgt.shape[0]
    R = ROW_BLK

    def body(pre_ref, xh1_ref, ta, tb, tc, l1g, l1b, l2g, l2b, wd_ref,
             dr2_ref, dffb_ref, loss_ref, dl2g_ref, dl2b_ref, tgt_ref):
        i = pl.program_id(0)

        @pl.when(i == 0)
        def _():
            for ref in (loss_ref, dl2g_ref, dl2b_ref):
                ref[...] = jnp.zeros_like(ref)

        tgt_ref[0:CHUNK, :] = ta[...]
        tgt_ref[CHUNK:2 * CHUNK, :] = tb[...]
        tgt_ref[2 * CHUNK:3 * CHUNK, :] = tc[...]
        ff = jnp.zeros((R, D_MODEL), F32)
        for d in range(N_DEV):
            pre = pre_ref[:, d * FF_BLK:(d + 1) * FF_BLK]
            ff = ff + _dot(pre * pre, wd_ref[d * FF_BLK:(d + 1) * FF_BLK, :])
        h1 = xh1_ref[...] * l1g[...] + l1b[...]
        xh2, rstd2 = _ln_fwd(ALPHA * h1 + ff, LN_EPS)
        h2 = xh2 * l2g[...] + l2b[...]
        valid = (i * R + lax.broadcasted_iota(jnp.int32, (R, 1), 0)) >= CHUNK
        err = jnp.where(valid, h2 - tgt_ref[...], 0.0)
        loss_ref[...] += 0.5 * jnp.sum(err * err) / D_MODEL
        dh2 = err * (1.0 / D_MODEL)
        dl2g_ref[...] += _colsum(dh2 * xh2)
        dl2b_ref[...] += _colsum(dh2)
        dr2 = _ln_bwd(dh2 * l2g[...], xh2, rstd2)
        dr2_ref[...] = dr2
        dffb_ref[...] = dr2.astype(MM)

    row = lambda w: pl.BlockSpec((R, w), lambda i: (i, 0))
    full = lambda a: pl.BlockSpec(a.shape, lambda i: (0,) * a.ndim)
    vecs = [l1_g, l1_b, l2_g, l2_b]
    acc = lambda s: (pl.BlockSpec(s, lambda i: (0, 0)), jax.ShapeDtypeStruct(s, F32))
    outs = [(row(D_MODEL), jax.ShapeDtypeStruct((tp, D_MODEL), F32)), (row(D_MODEL), jax.ShapeDtypeStruct((tp, D_MODEL), MM)),
            acc((8, HEAD)), acc((1, D_MODEL)), acc((1, D_MODEL))]
    return pl.pallas_call(
        body, name="post_down", grid=(tp // R,),
        in_specs=[row(D_FF), row(D_MODEL)] + _shift3(seq // CHUNK) + [full(a) for a in vecs] + [_VMEM],
        out_specs=[o[0] for o in outs], out_shape=[o[1] for o in outs],
        scratch_shapes=[pltpu.VMEM((R, D_MODEL), F32)],
        compiler_params=_params(("arbitrary",)),
    )(pre, xhat1, tgt, tgt, tgt, *vecs, w_down)


def _mlp_bwd(h1b, dffb, pre, w_up, w_down):
    tp = h1b.shape[0]
    R = MLP_ROWS if tp % MLP_ROWS == 0 else ROW_BLK
    nr = tp // R

    def body(h_ref, df_ref, pre_ref, wu_ref, wd_ref, gup_ref, gdn_ref, dh1_ref, aup, adn):
        d = pl.program_id(0)
        r = pl.program_id(1)

        @pl.when(r == 0)
        def _():
            aup[...] = jnp.zeros_like(aup)
            adn[...] = jnp.zeros_like(adn)

        h = h_ref[...]
        df = df_ref[...]
        wu = wu_ref[0]
        wd = wd_ref[0]
        pre = pre_ref[...]
        dpre = (_dot_nt(df, wd) * (2.0 * pre)).astype(MM)

        aup[...] += _dot_tn(h, dpre)
        adn[...] += _dot_tn(pre * pre, df)
        contrib = _dot_nt(dpre, wu)
        rows = pl.ds(pl.multiple_of(r * R, 64), R)

        @pl.when(d == 0)
        def _():
            dh1_ref[rows, :] = contrib

        @pl.when(d > 0)
        def _():
            dh1_ref[rows, :] += contrib

        @pl.when(r == nr - 1)
        def _():
            gup_ref[0] = aup[...].astype(gup_ref.dtype)
            gdn_ref[0] = adn[...].astype(gdn_ref.dtype)

    return pl.pallas_call(
        body, name="mlp_bwd", grid=(N_DEV, nr),
        in_specs=[pl.BlockSpec((R, D_MODEL), lambda d, r: (r, 0)), pl.BlockSpec((R, D_MODEL), lambda d, r: (r, 0)),
                  pl.BlockSpec((R, FF_BLK), lambda d, r: (r, d)),
                  pl.BlockSpec((1, D_MODEL, FF_BLK), lambda d, r: (d, 0, 0)),
                  pl.BlockSpec((1, FF_BLK, D_MODEL), lambda d, r: (d, 0, 0))],
        out_specs=[pl.BlockSpec((1, D_MODEL, FF_BLK), lambda d, r: (d, 0, 0)),
                   pl.BlockSpec((1, FF_BLK, D_MODEL), lambda d, r: (d, 0, 0)), _VMEM],
        out_shape=[jax.ShapeDtypeStruct((N_DEV, D_MODEL, FF_BLK), MM), jax.ShapeDtypeStruct((N_DEV, FF_BLK, D_MODEL), MM),
                   jax.ShapeDtypeStruct((tp, D_MODEL), F32)],
        scratch_shapes=[pltpu.VMEM((D_MODEL, FF_BLK), F32), pltpu.VMEM((FF_BLK, D_MODEL), F32)],
        compiler_params=_params(("arbitrary", "arbitrary")),
    )(h1b, dffb, pre, w_up, w_down.reshape(N_DEV, FF_BLK, D_MODEL))


def _post_bwd(dh1m, dr2, xhat1, rstd1, ycat, o, gate, gn_g, gn_b, l1_g, w_out, jobs=()):
    tp = o.shape[0]
    R = ROW_BLK
    nb = tp // R

    def body(dm_ref, dr2_ref, xh1_ref, rs1_ref, yc_ref, o_ref, g_ref, gng, gnb, l1g, wo_ref,
             do_ref, dg_ref, dys_ref, dh0_ref, gwo_ref, dl1g_ref, dl1b_ref, dgng_ref, dgnb_ref, awo):
        i = pl.program_id(0)

        @pl.when(i == 0)
        def _():
            for ref in (awo, dl1g_ref, dl1b_ref, dgng_ref, dgnb_ref):
                ref[...] = jnp.zeros_like(ref)

        dh1 = dm_ref[...] + ALPHA * dr2_ref[...]
        xh1 = xh1_ref[...]
        dl1g_ref[...] += _colsum(dh1 * xh1)
        dl1b_ref[...] += _colsum(dh1)
        dr1 = _ln_bwd(dh1 * l1g[...], xh1, rs1_ref[...])
        dh0_ref[...] = ALPHA * dr1
        dmix = dr1.astype(MM)
        awo[...] += _dot_tn(yc_ref[...], dmix)
        dyc = _dot_nt(dmix, wo_ref[...])
        dys_ref[...] = dyc[:, 0:S5_W]
        for h in range(RET_H):
            sl = slice(h * HEAD, (h + 1) * HEAD)
            gt = g_ref[:, sl]
            _, xhat, rstd, on, s = _gn_gate(o_ref[:, sl], gt, gng[:, sl], gnb[:, sl])
            dyr = dyc[:, S5_W + h * HEAD:S5_W + (h + 1) * HEAD]
            dg_ref[:, sl] = (dyr * on * (s * (1.0 + gt * (1.0 - s)))).astype(dg_ref.dtype)
            don = dyr * gt * s
            dgng_ref[:, sl] += _colsum(don * xhat)
            dgnb_ref[:, sl] += _colsum(don)
            do_ref[:, sl] = _ln_bwd(don * gng[:, sl], xhat, rstd)

        @pl.when(i == nb - 1)
        def _():
            gwo_ref[...] = awo[...].astype(gwo_ref.dtype)

    row = lambda w: pl.BlockSpec((R, w), lambda i: (i, 0))
    full = lambda a: pl.BlockSpec(a.shape, lambda i: (0,) * a.ndim)
    acc = lambda s, dt=F32: (pl.BlockSpec(s, lambda i: (0, 0)), jax.ShapeDtypeStruct(s, dt))
    outs = [(row(RET_W), jax.ShapeDtypeStruct((tp, RET_W), F32)), (row(RET_W), jax.ShapeDtypeStruct((tp, RET_W), MM)),
            (row(S5_W), jax.ShapeDtypeStruct((tp, S5_W), F32)), (row(D_MODEL), jax.ShapeDtypeStruct((tp, D_MODEL), F32)),
            acc((D_MODEL, D_MODEL), MM), acc((1, D_MODEL)), acc((1, D_MODEL)), acc((1, RET_W)), acc((1, RET_W))]
    return _call(
        body, "post_bwd", (nb,),
        [row(D_MODEL), row(D_MODEL), row(D_MODEL), row(1), row(D_MODEL), row(RET_W), row(RET_W),
         full(gn_g), full(gn_b), full(l1_g), _VMEM],
        [o[0] for o in outs], [o[1] for o in outs],
        [pltpu.VMEM((D_MODEL, D_MODEL), F32)],
        (dh1m, dr2, xhat1, rstd1, ycat, o, gate, gn_g, gn_b, l1_g, w_out), jobs)


_PROJ_SEGS = [(0, S5_W)] + [(S5_W + n * RET_W, S5_W + (n + 1) * RET_W) for n in range(4)]


def _in_w_grad(du, dq, dk, dv, dg, xhat0, li_g, li_b):
    tp = du.shape[0]
    R = PROJ_ROWS if tp % PROJ_ROWS == 0 else ROW_BLK
    nb = tp // R

    def body(du_ref, dq_ref, dk_ref, dv_ref, dg_ref, xh_ref, lig, lib, gw_ref, aw):
        i = pl.program_id(0)

        @pl.when(i == 0)
        def _():
            aw[...] = jnp.zeros_like(aw)

        valid = (i * R + lax.broadcasted_iota(jnp.int32, (R, 1), 0)) >= PAD
        hb = (xh_ref[...] * lig[...] + lib[...]).astype(MM)
        for (lo, hi), ref in zip(_PROJ_SEGS, (du_ref, dq_ref, dk_ref, dv_ref, dg_ref)):
            aw[lo:hi, :] += _dot_tn(jnp.where(valid, ref[...], 0.0).astype(MM), hb)

        @pl.when(i == nb - 1)
        def _():
            gw_ref[...] = aw[...].astype(gw_ref.dtype)

    row = lambda w: pl.BlockSpec((R, w), lambda i: (i, 0))
    full = lambda a: pl.BlockSpec(a.shape, lambda i: (0,) * a.ndim)
    (gw,), _ = _call(
        body, "in_w_grad", (nb,),
        [row(S5_W), row(RET_W), row(RET_W), row(RET_W), row(RET_W), row(D_MODEL), full(li_g), full(li_b)],
        [pl.BlockSpec((PROJ_W, D_MODEL), lambda i: (0, 0))], [jax.ShapeDtypeStruct((PROJ_W, D_MODEL), MM)],
        [pltpu.VMEM((PROJ_W, D_MODEL), F32)], (du, dq, dk, dv, dg, xhat0, li_g, li_b))
    return gw


def _in_bwd(du, dq, dk, dv, dg, dh0r, xhat0, rstd0, li_g, w_int, after):
    tp = du.shape[0]
    R = PROJ_ROWS if tp % PROJ_ROWS == 0 else ROW_BLK
    nb = tp // R
    segs = _PROJ_SEGS

    def body(du_ref, dq_ref, dk_ref, dv_ref, dg_ref, dh0r_ref, xh_ref, rs_ref, lig, w_ref, after_ref,
             gx_ref, dmeta_ref, dlg_ref, dlb_ref, stage, out_sems):
        i = pl.program_id(0)
        slot = i % 2

        def to_gx(step_slot, first):
            if first:
                return pltpu.make_async_copy(stage.at[0, CHUNK:R, :], gx_ref.at[0:R - CHUNK, :], out_sems.at[0])
            return pltpu.make_async_copy(stage.at[step_slot], gx_ref.at[pl.ds(i * R - CHUNK, R), :], out_sems.at[step_slot])

        @pl.when(i == 0)
        def _():
            for ref in (dlg_ref, dlb_ref):
                ref[...] = jnp.zeros_like(ref)

        @pl.when(i >= 3)
        def _():
            to_gx(slot, False).wait()

        valid = (i * R + lax.broadcasted_iota(jnp.int32, (R, 1), 0)) >= PAD
        xh = xh_ref[...]
        dh0 = dh0r_ref[...]
        for (lo, hi), ref in zip(segs, (du_ref, dq_ref, dk_ref, dv_ref, dg_ref)):
            dh0 = dh0 + _dot(jnp.where(valid, ref[...], 0.0).astype(MM), w_ref[lo:hi, :])
        dlg_ref[...] += _colsum(dh0 * xh)
        dlb_ref[...] += _colsum(dh0)
        draw = _ln_bwd(dh0 * lig[...], xh, rs_ref[...])
        stage[slot] = draw

        @pl.when(i == 0)
        def _():
            dmeta_ref[...] = draw[PAD:CHUNK, :]
            first = to_gx(0, True)
            first.start()
            first.wait()

        @pl.when(i > 0)
        def _():
            to_gx(slot, False).start()

        @pl.when(i == nb - 1)
        def _():
            for back in (1, 0):
                if nb - 1 - back >= 1:
                    to_gx((nb - 1 - back) % 2, False).wait()

    row = lambda w: pl.BlockSpec((R, w), lambda i: (i, 0))
    full = lambda a: pl.BlockSpec(a.shape, lambda i: (0,) * a.ndim)
    acc = lambda s, dt=F32: (pl.BlockSpec(s, lambda i: (0, 0)), jax.ShapeDtypeStruct(s, dt))
    outs = [(_ANY, jax.ShapeDtypeStruct((tp - CHUNK, D_MODEL), F32)), acc((N_META, D_MODEL)),
            acc((1, D_MODEL)), acc((1, D_MODEL))]
    return _call(
        body, "in_bwd", (nb,),
        [row(S5_W), row(RET_W), row(RET_W), row(RET_W), row(RET_W), row(D_MODEL), row(D_MODEL), row(1),
         full(li_g), _VMEM, full(after)],
        [o[0] for o in outs], [o[1] for o in outs],
        [pltpu.VMEM((2, R, D_MODEL), F32), pltpu.SemaphoreType.DMA((2,))],
        (du, dq, dk, dv, dg, dh0r, xhat0, rstd0, li_g, w_int, after))[0]


def _place():
    return lax.axis_index("x"), lax.axis_index("y"), lax.axis_index("c")


def _dma_sems(n):
    return pltpu.SemaphoreType.DMA((n,))


def _job_gather(shard):
    def parts(ins, outs, sems):
        (src,), (out,), (send_sems, recv_sems, local_sem) = ins, outs, sems
        x, y, c = _place()
        north = c == 1
        me, sib = (x, y, c), (x, y, 1 - c)
        xn, yn, dg = (1 - x, y, c), (x, 1 - y, c), (1 - x, 1 - y, c)
        relay_from = (jnp.where(north, 1 - x, x), jnp.where(north, y, 1 - y), c)
        relay_to = (jnp.where(north, x, 1 - x), jnp.where(north, 1 - y, y), c)

        def slot(dev):
            return out.at[4 * dev[0] + 2 * dev[1] + dev[2]]

        def copy(k, block, to, from_input=False):
            return pltpu.make_async_remote_copy(
                src_ref=src if from_input else slot(block), dst_ref=slot(block),
                send_sem=send_sems.at[k], recv_sem=recv_sems.at[k], device_id=to, device_id_type=_MESH)

        mine = lambda: pltpu.make_async_copy(src, slot(me), local_sem.at[0])
        first = lambda: [copy(0, me, sib, True), copy(1, me, xn, True), copy(2, me, yn, True)]
        relayed = lambda: [copy(3, relay_from, relay_to), copy(4, xn, sib), copy(5, yn, sib)]
        return me, sib, xn, yn, dg, copy, mine, first, relayed

    def start(ins, outs, sems):
        mine, first = parts(ins, outs, sems)[6:8]
        mine().start()
        for cp in first():
            cp.start()

    def relay(ins, outs, sems):
        me, sib, xn, yn, dg, copy, mine, first, relayed = parts(ins, outs, sems)
        copy(1, xn, me).wait_recv()
        copy(2, yn, me).wait_recv()
        for cp in relayed():
            cp.start()

    def finish(ins, outs, sems):
        me, sib, xn, yn, dg, copy, mine, first, relayed = parts(ins, outs, sems)
        other = 1 - me[2]
        copy(3, dg, me).wait_recv()
        last = copy(6, dg, sib)
        last.start()
        copy(0, sib, me).wait_recv()
        for k, chip in ((4, xn), (5, yn), (6, dg)):
            copy(k, (chip[0], chip[1], other), me).wait_recv()
        for cp in first() + relayed() + [last]:
            cp.wait_send()
        mine().wait()

    return dict(ins=[shard], outs=[jax.ShapeDtypeStruct((N_DEV,) + shard.shape, shard.dtype)],
                sems=[_dma_sems(7), _dma_sems(7), _dma_sems(1)], start=start, middle=relay, finish=finish)


def _job_gather_direct(shard):
    def copies(ins, outs, sems):
        (src,), (out,), (send_sems, recv_sems, local_sem) = ins, outs, sems
        x, y, c = _place()
        flip = lambda a, bit: 1 - a if bit else a
        slot = out.at[4 * x + 2 * y + c]
        return [pltpu.make_async_copy(src, slot, local_sem.at[0])] + [pltpu.make_async_remote_copy(
            src_ref=src, dst_ref=slot, send_sem=send_sems.at[k - 1], recv_sem=recv_sems.at[k - 1],
            device_id=(flip(x, k & 4), flip(y, k & 2), flip(c, k & 1)), device_id_type=_MESH) for k in range(1, N_DEV)]

    def start(ins, outs, sems):
        for cp in copies(ins, outs, sems):
            cp.start()

    def finish(ins, outs, sems):
        for cp in copies(ins, outs, sems):
            cp.wait()

    return dict(ins=[shard], outs=[jax.ShapeDtypeStruct((N_DEV,) + shard.shape, shard.dtype)],
                sems=[_dma_sems(N_DEV - 1), _dma_sems(N_DEV - 1), _dma_sems(1)], start=start, finish=finish)


def _job_pair(g):
    def copies(ins, outs, sems):
        x, y, c = _place()
        return [pltpu.make_async_remote_copy(
            src_ref=ins[0].at[2 * j + (1 - c)], dst_ref=outs[0].at[j], send_sem=sems[0].at[j], recv_sem=sems[1].at[j],
            device_id=(x, y, 1 - c), device_id_type=_MESH) for j in range(4)]

    def start(ins, outs, sems):
        for cp in copies(ins, outs, sems):
            cp.start()

    def finish(ins, outs, sems):
        for cp in copies(ins, outs, sems):
            cp.wait()

    return dict(ins=[g], outs=[jax.ShapeDtypeStruct((4,) + g.shape[1:], g.dtype)], sems=[_dma_sems(4), _dma_sems(4)],
                start=start, finish=finish)


def _job_chips(p):
    def copies(ins, outs, sems):
        x, y, c = _place()
        chips = [(1 - x, y), (x, 1 - y), (1 - x, 1 - y)]
        return [pltpu.make_async_remote_copy(
            src_ref=ins[0].at[2 * chip[0] + chip[1]], dst_ref=outs[0].at[k], send_sem=sems[0].at[k],
            recv_sem=sems[1].at[k], device_id=(*chip, c), device_id_type=_MESH) for k, chip in enumerate(chips)]

    def start(ins, outs, sems):
        for cp in copies(ins, outs, sems):
            cp.start()

    def finish(ins, outs, sems):
        for cp in copies(ins, outs, sems):
            cp.wait()

    return dict(ins=[p], outs=[jax.ShapeDtypeStruct((3,) + p.shape[1:], p.dtype)], sems=[_dma_sems(3), _dma_sems(3)],
                start=start, finish=finish)


_HBM = pl.BlockSpec(memory_space=pltpu.HBM)
_SEM = pl.BlockSpec(memory_space=pltpu.SEMAPHORE)
_ORDERED = pltpu.CompilerParams(has_side_effects=pltpu.SideEffectType.DATAFLOW_SIDE_EFFECTING)


def _chip_copies(p_ref, land_ref, sems):
    x, y, c = _place()
    chips = [(1 - x, y), (x, 1 - y), (1 - x, 1 - y)]
    return [pltpu.make_async_remote_copy(
        src_ref=p_ref.at[2 * chip[0] + chip[1]], dst_ref=land_ref.at[k], send_sem=sems[k], recv_sem=sems[3 + k],
        device_id=(*chip, c), device_id_type=_MESH) for k, chip in enumerate(chips)]


def _chips_start(ps, name):
    n = len(ps)

    def body(*refs):
        sems = refs[2 * n:8 * n]
        for a in range(n):
            for cp in _chip_copies(refs[a], refs[n + a], sems[6 * a:6 * a + 6]):
                cp.start()
        refs[-1][...] = jnp.zeros_like(refs[-1])

    lands = [(3,) + p.shape[1:] for p in ps]
    hbm = lambda arr: pltpu.with_memory_space_constraint(arr, pltpu.HBM)
    outs = pl.pallas_call(
        body, name=name,
        out_shape=(*[pltpu.SemaphoreType.DMA(())] * (6 * n), *[pltpu.HBM(p.shape, p.dtype) for p in ps],
                   *[pltpu.HBM(s, p.dtype) for s, p in zip(lands, ps)], jax.ShapeDtypeStruct((8, LANE), F32)),
        in_specs=[_HBM] * (2 * n), out_specs=(*[_SEM] * (6 * n), *[_HBM] * (2 * n), _VMEM),
        input_output_aliases={a: 6 * n + a for a in range(2 * n)}, compiler_params=_ORDERED,
    )(*[hbm(p) for p in ps], *[hbm(lax.empty(s, p.dtype)) for s, p in zip(lands, ps)])
    return (list(outs[:6 * n]), list(outs[6 * n:7 * n]), list(outs[7 * n:8 * n])), outs[8 * n]


def _chips_wait(started, after, name):
    sems, thrus, lands = started
    n = len(thrus)

    def body(*refs):
        for a in range(n):
            for cp in _chip_copies(refs[a], refs[n + a], refs[2 * n + 6 * a:2 * n + 6 * a + 6]):
                cp.wait_send()
                cp.wait_recv()

    outs = pl.pallas_call(
        body, name=name, out_shape=[pltpu.HBM(t.shape, t.dtype) for t in thrus + lands],
        in_specs=(*[_HBM] * (2 * n), *[_SEM] * (6 * n), _ANY), out_specs=[_HBM] * (2 * n),
        input_output_aliases={a: a for a in range(2 * n)}, compiler_params=_ORDERED,
    )(*thrus, *lands, *sems, after)
    return list(outs[:n]), list(outs[n:])


def _split_job_refs(jobs, ins, outs, sems):
    res, a, b, c = [], 0, 0, 0
    for job in jobs:
        na, nb, nc = len(job["ins"]), len(job["outs"]), len(job["sems"])
        res.append((ins[a:a + na], outs[b:b + nb], sems[c:c + nc]))
        a, b, c = a + na, b + nb, c + nc
    return res


def _call(body, name, grid, in_specs, out_specs, out_shape, scratch, args, jobs=(), prefetch=None, early=0):
    jobs = list(jobs)
    n_in, n_out, n_scr = len(in_specs), len(out_specs), len(scratch)
    j_in = [a for job in jobs for a in job["ins"]]
    j_out = [o for job in jobs for o in job["outs"]]
    j_scr = [s for job in jobs for s in job["sems"]]
    nsteps = grid[0]
    n_pre = 0 if prefetch is None else 1

    def wrapped(*refs):
        pre, refs = refs[:n_pre], refs[n_pre:]
        ins, jins = refs[:n_in], refs[n_in:n_in + len(j_in)]
        refs = refs[n_in + len(j_in):]
        outs, jouts = refs[:n_out], refs[n_out:n_out + len(j_out)]
        refs = refs[n_out + len(j_out):]
        scr, jscr = refs[:n_scr], refs[n_scr:]
        per_job = _split_job_refs(jobs, jins, jouts, jscr)

        def middle():
            for job, r in zip(jobs, per_job):
                if "middle" in job:
                    job["middle"](*r)

        @pl.when(pl.program_id(0) == 0)
        def _():
            for job, r in zip(jobs, per_job):
                job["start"](*r)

        if nsteps >= 3:
            pl.when(pl.program_id(0) == nsteps // 2)(middle)

        if early:
            @pl.when(pl.program_id(0) == nsteps - 1)
            def _():
                for job, r in zip(jobs[:early], per_job[:early]):
                    job["finish"](*r)

        body(*pre, *ins, *outs, *scr, *[o for r in per_job[:early] for o in r[1]])

        @pl.when(pl.program_id(0) == nsteps - 1)
        def _():
            if nsteps < 3:
                middle()
            for job, r in zip(jobs[early:], per_job[early:]):
                job["finish"](*r)

    specs = dict(in_specs=list(in_specs) + [_ANY] * len(j_in), out_specs=list(out_specs) + [_ANY] * len(j_out),
                 scratch_shapes=list(scratch) + j_scr)
    if n_pre:
        specs = dict(grid_spec=pltpu.PrefetchScalarGridSpec(num_scalar_prefetch=1, grid=grid, **specs))
    else:
        specs["grid"] = grid
    res = pl.pallas_call(
        wrapped if jobs else body, name=name, out_shape=list(out_shape) + j_out,
        compiler_params=_params(("arbitrary",) * len(grid)), **specs,
    )(*([prefetch] if n_pre else []), *args, *j_in)
    return list(res[:n_out]), list(res[n_out:])


def _exchange(jobs, name):
    j_in = [a for job in jobs for a in job["ins"]]
    j_out = [o for job in jobs for o in job["outs"]]
    j_scr = [s for job in jobs for s in job["sems"]]

    def body(*refs):
        per_job = _split_job_refs(jobs, refs[:len(j_in)], refs[len(j_in):len(j_in) + len(j_out)],
                                  refs[len(j_in) + len(j_out):])
        for phase in ("start", "middle", "finish"):
            for job, r in zip(jobs, per_job):
                if phase in job:
                    job[phase](*r)

    return pl.pallas_call(body, name=name, out_shape=j_out, in_specs=[_ANY] * len(j_in), out_specs=[_ANY] * len(j_out),
                          scratch_shapes=j_scr)(*j_in)


def _pair_sum(gs, r1s, c_arr, name):
    n = len(gs)

    def body(c_ref, *refs):
        for a in range(n):
            refs[2 * n + a][...] = (refs[a][...].astype(F32) + refs[n + a][...].astype(F32)).astype(refs[2 * n + a].dtype)

    def blk(g, own):
        s = g.shape[1:]
        if own:
            return pl.BlockSpec((1,) + s, lambda j, c_ref: (2 * j + c_ref[0],) + (0,) * len(s))
        return pl.BlockSpec((1,) + s, lambda j, c_ref: (j,) + (0,) * len(s))

    return pl.pallas_call(
        body, name=name,
        grid_spec=pltpu.PrefetchScalarGridSpec(
            num_scalar_prefetch=1, grid=(4,),
            in_specs=[blk(g, True) for g in gs] + [blk(g, False) for g in gs],
            out_specs=[blk(g, False) for g in gs]),
        out_shape=[jax.ShapeDtypeStruct((4,) + g.shape[1:], g.dtype) for g in gs],
        compiler_params=_params(("arbitrary",)),
    )(c_arr, *gs, *r1s)


def _adamw_math(w, g, m, v):
    m = ADAM_B1 * m + (1.0 - ADAM_B1) * g
    v = ADAM_B2 * v + (1.0 - ADAM_B2) * (g * g)
    m_hat = m / (1.0 - ADAM_B1 ** ADAM_STEP)
    v_hat = v / (1.0 - ADAM_B2 ** ADAM_STEP)
    return -ADAM_LR * (m_hat / (jnp.sqrt(v_hat) + ADAM_EPS) + ADAM_WD * w), m, v


def _view(name, a):
    return jnp.swapaxes(a, -1, -2) if name in ("w_in", "s5_b_re", "s5_b_im") else a


def _adamw_shards(items, name, steps, chip, jobs=()):
    n = len(items)

    def body(chip_ref, *refs):
        for a in range(n):
            p_ref, r_ref, w_ref, m_ref, v_ref = refs[5 * a:5 * a + 5]
            g = ((p_ref[0].astype(F32) + r_ref[0].astype(F32)) + r_ref[1].astype(F32)) + r_ref[2].astype(F32)
            outs = refs[5 * n + 4 * a:5 * n + 4 * a + 4]
            outs[0][...] = g
            outs[1][...], outs[2][...], outs[3][...] = _adamw_math(w_ref[...], g, m_ref[...], v_ref[...])

    in_specs, out_specs, out_shape, flat = [], [], [], []
    for p, r, w, m, v in items:
        rows, cols = w.shape
        rb = rows // steps
        in_specs += [pl.BlockSpec((1, rb, cols), lambda i, c: (c[0], i, 0)), pl.BlockSpec((3, rb, cols), lambda i, c: (0, i, 0))]
        wblk = pl.BlockSpec((rb, cols), lambda i, c: (i, 0))
        in_specs += [wblk] * 3
        out_specs += [wblk] * 4
        out_shape += [jax.ShapeDtypeStruct(w.shape, F32)] * 4
        flat += [p, r, w, m, v]
    return _call(body, name, (steps,), in_specs, out_specs, out_shape, [], flat, jobs, prefetch=chip)


def _sum_devices(gathered, name):
    def body(gs_ref, g_ref):
        g = gs_ref[0]
        for s in range(1, N_DEV):
            g = g + gs_ref[s]
        g_ref[...] = g

    return pl.pallas_call(body, name=name, out_shape=jax.ShapeDtypeStruct(gathered.shape[1:], F32),
                          in_specs=[_VMEM], out_specs=_VMEM, compiler_params=_params())(gathered)


def _adamw_native(items, name):
    n = len(items)

    def body(*refs):
        for a in range(n):
            g, w, m, v = (refs[4 * a + t][...] for t in range(4))
            refs[4 * n + 3 * a][...], refs[4 * n + 3 * a + 1][...], refs[4 * n + 3 * a + 2][...] = _adamw_math(w, g, m, v)

    return pl.pallas_call(
        body, name=name, out_shape=[jax.ShapeDtypeStruct(it[1].shape, F32) for it in items for _ in range(3)],
        in_specs=[_VMEM] * (4 * n), out_specs=[_VMEM] * (3 * n), compiler_params=_params(),
    )(*[t for it in items for t in it])


SMALL = ["ln_in_g", "ln_in_b", "s5_lambda_re", "s5_lambda_im", "s5_log_dt", "s5_b_re", "s5_b_im", "s5_c_re", "s5_c_im",
         "s5_d", "s5_b_glu", "ret_gn_g", "ret_gn_b", "ln1_g", "ln1_b", "ln2_g", "ln2_b"]
LATE = ["ln_in_g", "ln_in_b", "meta_tokens"]
EARLY = [n for n in SMALL if n not in LATE] + ["s5_w_glu", "loss"]
LANE = 128


def _pack(arrs):
    parts = []
    for a in arrs:
        f = a.reshape(-1)
        parts.append(jnp.pad(f, (0, (-f.shape[0]) % LANE)))
    flat = jnp.concatenate(parts)
    rows = -(-flat.shape[0] // LANE)
    flat = jnp.pad(flat, (0, (-rows % 8) * LANE + rows * LANE - flat.shape[0]))
    return flat.reshape(-1, LANE)


def _unpack(packed, shapes):
    flat = packed.reshape(-1)
    out, off = [], 0
    for s in shapes:
        n = math.prod(s)
        out.append(flat[off:off + n].reshape(s))
        off += n + (-n) % LANE
    return out


def _rope_tables(tp):
    inv_freq = 1.0 / (ROPE_BASE ** (jnp.arange(0, HEAD, 2, dtype=F32) / HEAD))
    blk = (jnp.arange(tp // ROW_BLK, dtype=F32) * ROW_BLK)[:, None, None] * inv_freq
    off = (jnp.arange(ROW_BLK, dtype=F32) - float(PAD))[None, :, None] * inv_freq
    cos = (jnp.cos(blk) * jnp.cos(off) - jnp.sin(blk) * jnp.sin(off)).reshape(tp, HEAD // 2)
    sin = (jnp.sin(blk) * jnp.cos(off) + jnp.cos(blk) * jnp.sin(off)).reshape(tp, HEAD // 2)
    return jnp.concatenate([cos, cos], axis=1), jnp.concatenate([-sin, sin], axis=1)


def _local_step(x2d, tgt, meta, w_int, w_out, w_up, w_down, w_glu, sp, distributed):
    tp = x2d.shape[0] + CHUNK
    row = lambda a: a.reshape(1, -1)
    cos2, sin2 = _rope_tables(tp)
    li_g, li_b = row(sp["ln_in_g"]), row(sp["ln_in_b"])
    l1_g, l1_b, l2_g, l2_b = row(sp["ln1_g"]), row(sp["ln1_b"]), row(sp["ln2_g"]), row(sp["ln2_b"])
    gn_g, gn_b = row(sp["ret_gn_g"]), row(sp["ret_gn_b"])
    lre, lim = row(sp["s5_lambda_re"]), row(sp["s5_lambda_im"])
    ldt = row(jnp.repeat(sp["s5_log_dt"].reshape(-1), S5_P))
    to_t = lambda b: b.reshape(S5_G, S5_P, S5_H).transpose(2, 0, 1).reshape(S5_H, S5_N)
    bre_t, bim_t = to_t(sp["s5_b_re"]), to_t(sp["s5_b_im"])
    to_w = lambda c: jnp.tile(c.reshape(S5_W, S5_P), (1, 2))
    cre_w, cim_w = to_w(sp["s5_c_re"]), to_w(sp["s5_c_im"])

    jobs = (lambda *j: list(j)) if distributed else (lambda *j: [])
    c_arr = jnp.reshape(lax.axis_index("c"), (1,)).astype(jnp.int32) if distributed else None
    (xhat0, rstd0), bg = _ln_in(x2d, meta, jobs(*([_job_gather(w_int), _job_gather(w_glu)] if distributed else [])),
                                gather_meta=distributed)
    if distributed:
        w_int, w_glu = bg[1].reshape(PROJ_W, D_MODEL), bg[2].reshape(S5_W, S5_W)
    s5_small = (lre, lim, ldt, bre_t, bim_t, cre_w, cim_w, row(sp["s5_d"]), w_glu, row(sp["s5_b_glu"]))
    (u, q, k, v, gate), bg = _in_proj(xhat0, li_g, li_b, w_int, cos2, sin2,
                                      jobs(_job_gather(w_out) if distributed else None))
    if distributed:
        w_out = bg[0].reshape(D_MODEL, D_MODEL)
    (ys5, xr, xi), bg = _s5_fwd(u, *s5_small, jobs=jobs(_job_gather(w_up) if distributed else None))
    if distributed:
        w_up = bg[0]
    (o, states), _ = _ret_fwd(q, k, v)
    (ycat, xhat1, rstd1, h1b, pre), bg = _post_up(o, gate, ys5, xhat0, gn_g, gn_b, li_g, li_b, l1_g, l1_b, w_out, w_up,
                                                  jobs(_job_gather(w_down) if distributed else None))
    if distributed:
        w_down = bg[0].reshape(D_FF, D_MODEL)
    dr2, dffb, loss8, dl2g, dl2b = _post_down(pre, xhat1, tgt, l1_g, l1_b, l2_g, l2_b, w_down)
    g_up, g_down, dh1m = _mlp_bwd(h1b, dffb, pre, w_up, w_down)
    (do, dgate, dys5, dh0r, g_out, dl1g, dl1b, dgng, dgnb), bg = _post_bwd(
        dh1m, dr2, xhat1, rstd1, ycat, o, gate, gn_g, gn_b, l1_g, w_out,
        jobs(*([_job_pair(g_up), _job_pair(g_down)] if distributed else [])))
    g_out = g_out.reshape(N_DEV, D_MODEL // N_DEV, D_MODEL)
    after = jnp.zeros((8, LANE), F32)
    if distributed:
        p_up, p_down = _pair_sum([g_up, g_down], bg, c_arr, "pair_sum_mlp")
        started_mlp, after = _chips_start([p_up, p_down], "chips_mlp_start")
    (du, dlre, dlim, dldt, dbre_t, dbim_t, dcre, dcim, dd, dwglu, dbglu), bg = _s5_bwd(
        dys5, u, xr, xi, *s5_small, after, jobs=jobs(_job_pair(g_out) if distributed else None))
    if distributed:
        (p_out,) = _pair_sum([g_out], bg, c_arr, "pair_sum_out")
    from_t = lambda t: t.reshape(S5_H, S5_G, S5_P).transpose(1, 0, 2)
    small = {
        "s5_lambda_re": dlre, "s5_lambda_im": dlim, "s5_log_dt": dldt[:, :S5_G],
        "s5_b_re": from_t(dbre_t), "s5_b_im": from_t(dbim_t), "s5_c_re": dcre, "s5_c_im": dcim, "s5_d": dd,
        "s5_b_glu": dbglu, "ret_gn_g": dgng, "ret_gn_b": dgnb, "ln1_g": dl1g, "ln1_b": dl1b, "ln2_g": dl2g, "ln2_b": dl2b,
        "s5_w_glu": dwglu, "loss": loss8[0:1, 0:1]}
    early_pack = _pack([small[n] for n in EARLY])
    (dq, dk, dv), bg = _ret_bwd(q, k, v, do, states, cos2, sin2,
                                jobs(*([_job_chips(p_out), _job_gather(early_pack)] if distributed else [])))
    g_int = _in_w_grad(du, dq, dk, dv, dgate, xhat0, li_g, li_b).reshape(N_DEV, PROJ_W // N_DEV, D_MODEL)
    after = jnp.zeros((8, LANE), F32)
    if distributed:
        (r1_in,) = _exchange([_job_pair(g_int)], "exchange_pair_in")
        (p_in,) = _pair_sum([g_int], [r1_in], c_arr, "pair_sum_in")
        (p_up, p_down), (r_up, r_down) = _chips_wait(started_mlp, p_in, "chips_mlp_wait")
        started_in, after = _chips_start([p_in], "chips_in_start")
    grad_x, dmeta, dlig, dlib = _in_bwd(du, dq, dk, dv, dgate, dh0r, xhat0, rstd0, li_g, w_int, after)
    small.update(ln_in_g=dlig, ln_in_b=dlib, meta_tokens=dmeta)
    if distributed:
        (p_in,), (r_in,) = _chips_wait(started_in, dlig, "chips_in_wait")
        big = dict(chip_sums=[p_in, p_out, p_up, p_down], received=[r_in, bg[0], r_up, r_down], early=bg[1])
    else:
        big = dict(partials=[g_int, g_out, g_up, g_down])
    return grad_x, big, small


def kernel(x, meta_tokens, ln_in_g, ln_in_b, w_in, s5_lambda_re, s5_lambda_im, s5_log_dt, s5_b_re, s5_b_im, s5_c_re, s5_c_im, s5_d, s5_w_glu, s5_b_glu, ret_gn_g, ret_gn_b, w_out, ln1_g, ln1_b, w_up, w_down, ln2_g, ln2_b, loss_target, m_meta_tokens, m_ln_in_g, m_ln_in_b, m_w_in, m_s5_lambda_re, m_s5_lambda_im, m_s5_log_dt, m_s5_b_re, m_s5_b_im, m_s5_c_re, m_s5_c_im, m_s5_d, m_s5_w_glu, m_s5_b_glu, m_ret_gn_g, m_ret_gn_b, m_w_out, m_ln1_g, m_ln1_b, m_w_up, m_w_down, m_ln2_g, m_ln2_b, v_meta_tokens, v_ln_in_g, v_ln_in_b, v_w_in, v_s5_lambda_re, v_s5_lambda_im, v_s5_log_dt, v_s5_b_re, v_s5_b_im, v_s5_c_re, v_s5_c_im, v_s5_d, v_s5_w_glu, v_s5_b_glu, v_ret_gn_g, v_ret_gn_b, v_w_out, v_ln1_g, v_ln1_b, v_w_up, v_w_down, v_ln2_g, v_ln2_b):
    args = dict(locals())
    names = ["meta_tokens", "ln_in_g", "ln_in_b", "w_in", "s5_lambda_re", "s5_lambda_im", "s5_log_dt", "s5_b_re", "s5_b_im",
             "s5_c_re", "s5_c_im", "s5_d", "s5_w_glu", "s5_b_glu", "ret_gn_g", "ret_gn_b", "w_out", "ln1_g", "ln1_b",
             "w_up", "w_down", "ln2_g", "ln2_b"]
    ax, ay, ac = _place()
    me = 4 * ax + 2 * ay + ac

    sp = {n: args[n] for n in SMALL}
    grad_x, big, small = _local_step(x[0], loss_target[0], meta_tokens, w_in[0].T.astype(MM), w_out[0].astype(MM),
                                   w_up[0].astype(MM), w_down[0].astype(MM), s5_w_glu[0].astype(MM), sp, True)

    j_arr = jnp.reshape(2 * ax + ay, (1,)).astype(jnp.int32)
    two_d = lambda a: a.reshape(a.shape[-2:])
    item = lambda n, p, r: (p, r, *(two_d(_view(n, a)) for a in (args[n], args["m_" + n], args["v_" + n])))
    late_pack = _pack([small[n] for n in LATE])
    mlp = ("w_out", "w_up", "w_down")
    (late_all,) = _exchange([_job_gather_direct(late_pack)], "gather_small_late")
    res, _ = _adamw_shards(
        [item(n, p, r) for n, p, r in zip(mlp, big["chip_sums"][1:], big["received"][1:])], "adamw_mlp", 8, j_arr)
    res_in, _ = _adamw_shards([item("w_in", big["chip_sums"][0], big["received"][0])], "adamw_in", 2, j_arr)
    upd = {"w_in": res_in}
    for idx, n in enumerate(mlp):
        upd[n] = res[4 * idx:4 * idx + 4]
    shard_grads = {n: upd[n][0] for n in upd}

    early_shapes = [_view(n, args[n]).shape for n in EARLY[:-2]] + [(S5_W, S5_W), (1,)]
    late_shapes = [args["ln_in_g"].shape, args["ln_in_b"].shape, (N_META, D_MODEL)]
    g_small = dict(zip(EARLY, _unpack(_sum_devices(big["early"], "sum_small_early"), early_shapes)))
    g_small.update(zip(LATE, _unpack(_sum_devices(late_all, "sum_small_late"), late_shapes)))
    loss = g_small["loss"].reshape(())

    shard_grads["meta_tokens"] = lax.dynamic_slice(g_small["meta_tokens"], (0, me * (D_MODEL // N_DEV)),
                                                   (N_META, D_MODEL // N_DEV))
    shard_grads["s5_w_glu"] = lax.dynamic_slice(g_small["s5_w_glu"], (me * (S5_W // N_DEV), 0),
                                                (S5_W // N_DEV, S5_W))[None]
    natives = SMALL + ["meta_tokens", "s5_w_glu"]
    res2 = _adamw_native([(shard_grads[n] if n in shard_grads else g_small[n], *(_view(n, args[p + n]) for p in ("", "m_", "v_")))
                          for n in natives], "adamw_small")
    for idx, n in enumerate(natives):
        upd[n] = [shard_grads[n] if n in shard_grads else g_small[n]] + list(res2[3 * idx:3 * idx + 3])

    grads, deltas, new_m, new_v = ([_view(n, upd[n][t]).reshape(args[n].shape) for n in names] for t in range(4))
    return (loss, grad_x[None], *grads, *deltas, *new_m, *new_v)
```

```python
import math

import jax
import jax.numpy as jnp
from jax import lax
from jax.experimental import pallas as pl
from jax.experimental.pallas import tpu as pltpu

F32 = jnp.float32
MM = jnp.bfloat16

D_MODEL = 1024
N_META = 16
CHUNK = 128
PAD = CHUNK - N_META
S5_W, S5_G, S5_H, S5_P = 256, 16, 16, 64
S5_N = S5_G * S5_P
RET_W, RET_H, HEAD = 768, 6, 128
D_FF = 4096
PROJ_W = S5_W + 4 * RET_W
N_DEV = 8
FF_BLK = D_FF // N_DEV
ROW_BLK = 384
MLP_ROWS = 1408
PROJ_ROWS = 704
ALPHA = 2.0 ** 0.25
LN_EPS = 1e-5
GN_EPS = 1e-5
ROPE_BASE = 10000.0
GELU_C = math.sqrt(2.0 / math.pi)
GELU_A = 0.044715
ADAM_LR, ADAM_B1, ADAM_B2, ADAM_EPS, ADAM_WD, ADAM_STEP = 0.001, 0.9, 0.999, 1e-08, 0.01, 10
VMEM_LIMIT = 60 * 1024 * 1024

_VMEM = pl.BlockSpec(memory_space=pltpu.VMEM)
_ANY = pl.BlockSpec(memory_space=pl.ANY)
_MESH = pl.DeviceIdType.MESH


def _params(sem=None):
    return pltpu.CompilerParams(dimension_semantics=sem, vmem_limit_bytes=VMEM_LIMIT)


def _dot(a, b):
    return jnp.dot(a.astype(MM), b.astype(MM), preferred_element_type=F32)


def _dot_nt(a, b):
    return lax.dot_general(a.astype(MM), b.astype(MM), (((1,), (1,)), ((), ())), preferred_element_type=F32)


def _dot_tn(a, b):
    return lax.dot_general(a.astype(MM), b.astype(MM), (((0,), (0,)), ((), ())), preferred_element_type=F32)


def _split3(a):
    hi = a.astype(jnp.bfloat16)
    r1 = a - hi.astype(F32)
    mid = r1.astype(jnp.bfloat16)
    lo = (r1 - mid.astype(F32)).astype(jnp.bfloat16)
    return hi, mid, lo


def _dot_sel_rhs(a, sel):
    s = sel.astype(jnp.bfloat16)
    return sum(jnp.dot(p, s, preferred_element_type=F32) for p in _split3(a))


def _dot_sel_lhs(sel, b):
    s = sel.astype(jnp.bfloat16)
    return sum(jnp.dot(s, p, preferred_element_type=F32) for p in _split3(b))


def _ln_fwd(r, eps):
    mu = jnp.mean(r, axis=-1, keepdims=True)
    xc = r - mu
    var = jnp.mean(xc * xc, axis=-1, keepdims=True)
    rstd = lax.rsqrt(var + eps)
    return xc * rstd, rstd


def _ln_bwd(dxhat, xhat, rstd):
    m1 = jnp.mean(dxhat, axis=-1, keepdims=True)
    m2 = jnp.mean(dxhat * xhat, axis=-1, keepdims=True)
    return rstd * (dxhat - m1 - xhat * m2)


def _colsum(a):
    return jnp.sum(a, axis=0, keepdims=True)


def _shift3(n_in, block=lambda i: i):
    return [pl.BlockSpec((CHUNK, D_MODEL), (lambda i, j=j: (jnp.clip(3 * block(i) - 1 + j, 0, n_in - 1), 0)))
            for j in range(3)]


def _ln_in(x2d, meta, jobs=(), gather_meta=False):
    seq = x2d.shape[0]
    tp = seq + CHUNK
    R = ROW_BLK
    nb = tp // R
    shard_w = D_MODEL // N_DEV

    def body(xa, xb, xc, meta_ref, xhat_ref, rstd_ref, raw_ref, *gathered):
        raw_ref[0:CHUNK, :] = xa[...]
        raw_ref[CHUNK:2 * CHUNK, :] = xb[...]
        raw_ref[2 * CHUNK:3 * CHUNK, :] = xc[...]

        @pl.when(pl.program_id(0) == nb - 1)
        def _():
            raw_ref[0:PAD, :] = jnp.zeros((PAD, D_MODEL), F32)
            if gather_meta:
                for d in range(N_DEV):
                    pltpu.sync_copy(gathered[0].at[d], raw_ref.at[PAD:CHUNK, d * shard_w:(d + 1) * shard_w])
            else:
                raw_ref[PAD:CHUNK, :] = meta_ref[...]

        xhat_ref[...], rstd_ref[...] = _ln_fwd(raw_ref[...], LN_EPS)

    row = lambda w: pl.BlockSpec((R, w), lambda i: (nb - 1 - i, 0))
    jobs = ([_job_gather(meta)] if gather_meta else []) + list(jobs)
    return _call(
        body, "ln_in", (nb,),
        _shift3(seq // CHUNK, lambda i: nb - 1 - i) + [pl.BlockSpec(meta.shape, lambda i: (0, 0))],
        [row(D_MODEL), row(1)], [jax.ShapeDtypeStruct((tp, D_MODEL), F32), jax.ShapeDtypeStruct((tp, 1), F32)],
        [pltpu.VMEM((R, D_MODEL), F32)], (x2d, x2d, x2d, meta), jobs, early=1 if gather_meta else 0)


def _in_proj(xhat0, ln_g, ln_b, w_int, cos2, sin2, jobs=()):
    tp = xhat0.shape[0]
    R = PROJ_ROWS if tp % PROJ_ROWS == 0 else ROW_BLK

    def body(xh_ref, g_ref, b_ref, w_ref, cos_ref, sin_ref, u_ref, q_ref, k_ref, v_ref, gate_ref):
        hb = (xh_ref[...] * g_ref[...] + b_ref[...]).astype(MM)
        valid = (pl.program_id(0) * R + lax.broadcasted_iota(jnp.int32, (R, 1), 0)) >= PAD

        def seg(lo, hi):
            return jnp.where(valid, _dot_nt(hb, w_ref[lo:hi, :]), 0.0)

        u_ref[...] = seg(0, S5_W)
        cos = cos_ref[...]
        sin = sin_ref[...]
        q = seg(S5_W, S5_W + RET_W)
        k = seg(S5_W + RET_W, S5_W + 2 * RET_W)
        for h in range(RET_H):
            sl = slice(h * HEAD, (h + 1) * HEAD)
            qh = q[:, sl]
            kh = k[:, sl]
            q_ref[:, sl] = (qh * cos + pltpu.roll(qh, HEAD // 2, 1) * sin).astype(q_ref.dtype)
            k_ref[:, sl] = ((kh * cos + pltpu.roll(kh, HEAD // 2, 1) * sin) * (HEAD ** -0.5)).astype(k_ref.dtype)
        v_ref[...] = seg(S5_W + 2 * RET_W, S5_W + 3 * RET_W).astype(v_ref.dtype)
        gate_ref[...] = seg(S5_W + 3 * RET_W, PROJ_W)

    def rows(w, dt):
        return pl.BlockSpec((R, w), lambda i: (i, 0)), jax.ShapeDtypeStruct((tp, w), dt)

    outs = [rows(S5_W, F32), rows(RET_W, MM), rows(RET_W, MM), rows(RET_W, MM), rows(RET_W, F32)]
    full = lambda s: pl.BlockSpec(s, lambda i: (0,) * len(s))
    return _call(
        body, "in_proj", (tp // R,),
        [pl.BlockSpec((R, D_MODEL), lambda i: (i, 0)), full((1, D_MODEL)), full((1, D_MODEL)), _VMEM,
         pl.BlockSpec((R, HEAD), lambda i: (i, 0)), pl.BlockSpec((R, HEAD), lambda i: (i, 0))],
        [o[0] for o in outs], [o[1] for o in outs], [], (xhat0, ln_g, ln_b, w_int, cos2, sin2), jobs)


def _s5_disc(lre, lim, ldt, bre_t, bim_t):
    dt = jnp.exp(ldt)
    mag = jnp.exp(lre * dt)
    ang = lim * dt
    lbr = mag * jnp.cos(ang)
    lbi = mag * jnp.sin(ang)
    den = lre * lre + lim * lim
    nr = lbr - 1.0
    qr = (nr * lre + lbi * lim) / den
    qi = (lbi * lre - nr * lim) / den
    return lbr, lbi, qr * bre_t - qi * bim_t, qr * bim_t + qi * bre_t


def _s5_tables(lbr, lbi, reverse):
    if reverse:
        lbi = -lbi
    pw = [(lbr, lbi)]
    for _ in range(7):
        r, i = pw[-1]
        pw.append((r * lbr - i * lbi, r * lbi + i * lbr))
    row = lax.broadcasted_iota(jnp.int32, (8, S5_N), 0)
    tabs = []
    for k in range(3):
        sh = 2 ** k
        mask = (row < 8 - sh) if reverse else (row >= sh)
        ar, ai = pw[sh - 1]
        tabs.append((jnp.where(mask, ar, 0.0), jnp.where(mask, ai, 0.0)))
    pr = jnp.zeros((8, S5_N), F32)
    pi = jnp.zeros((8, S5_N), F32)
    for i in range(8):
        ar, ai = pw[7 - i] if reverse else pw[i]
        pr = jnp.where(row == i, ar, pr)
        pi = jnp.where(row == i, ai, pi)
    tabs.append((pr, pi))
    return tabs


def _store_tables(tab_ref, tabs):
    for k, (r, i) in enumerate(tabs):
        tab_ref[2 * k] = r
        tab_ref[2 * k + 1] = i


def _bd_mask():
    r = lax.broadcasted_iota(jnp.int32, (S5_W, S5_N), 0)
    c = lax.broadcasted_iota(jnp.int32, (S5_W, S5_N), 1)
    return jnp.right_shift(r, 4) == jnp.right_shift(c, 6)


def _s5_block_diag(bbr_t, bbi_t, cre_w, cim_w):
    mask = _bd_mask()
    bd = lambda t: jnp.where(mask, t, 0.0)
    return (bd(jnp.tile(bbr_t, (S5_G, 1))), bd(jnp.tile(bbi_t, (S5_G, 1))),
            bd(jnp.tile(cre_w, (1, S5_N // HEAD))), bd(jnp.tile(cim_w, (1, S5_N // HEAD))))


def _scan8(xr, xi, tab_ref, lanes, reverse):
    for k in range(3):
        sh = (8 - 2 ** k) if reverse else 2 ** k
        sr = pltpu.roll(xr, sh, 0)
        si = pltpu.roll(xi, sh, 0)
        mr = tab_ref[2 * k, :, lanes]
        mi = tab_ref[2 * k + 1, :, lanes]
        xr, xi = xr + (mr * sr - mi * si), xi + (mr * si + mi * sr)
    return xr, xi


S5_LANES = 512


def _gelu(y):
    t = jnp.tanh(GELU_C * (y + GELU_A * y * y * y))
    return 0.5 * y * (1.0 + t), t


def _s5_fwd(u, lre, lim, ldt, bre_t, bim_t, cre_w, cim_w, d_row, w_glu, b_glu, jobs=()):
    tp = u.shape[0]
    R = ROW_BLK

    def body(u_ref, lre_ref, lim_ref, ldt_ref, bre_ref, bim_ref, cre_ref, cim_ref, d_ref, wg_ref, bg_ref,
             y_ref, xr_ref, xi_ref, bbd_r, bbd_i, cbd_r, cbd_i, tab_ref, car_r, car_i):
        @pl.when(pl.program_id(0) == 0)
        def _():
            lbr, lbi, bbr, bbi = _s5_disc(lre_ref[...], lim_ref[...], ldt_ref[...], bre_ref[...], bim_ref[...])
            br, bi, cr, ci = _s5_block_diag(bbr, bbi, cre_ref[...], cim_ref[...])
            bbd_r[...] = br.astype(MM)
            bbd_i[...] = bi.astype(MM)
            cbd_r[...] = cr.astype(MM)
            cbd_i[...] = ci.astype(MM)
            _store_tables(tab_ref, _s5_tables(lbr, lbi, False))
            car_r[...] = jnp.zeros_like(car_r)
            car_i[...] = jnp.zeros_like(car_i)

        u = u_ref[...]
        ub = u.astype(MM)
        xr_ref[...] = jnp.dot(ub, bbd_r[...], preferred_element_type=F32)
        xi_ref[...] = jnp.dot(ub, bbd_i[...], preferred_element_type=F32)
        for j in range(S5_N // S5_LANES):
            lanes = pl.ds(j * S5_LANES, S5_LANES)
            pr = tab_ref[6, :, lanes]
            pi = tab_ref[7, :, lanes]

            def step(g, carry):
                cr, ci = carry
                rows = pl.ds(pl.multiple_of(g * 8, 8), 8)
                xr, xi = _scan8(xr_ref[rows, lanes], xi_ref[rows, lanes], tab_ref, lanes, False)
                br = jnp.broadcast_to(cr[7:8, :], cr.shape)
                bi = jnp.broadcast_to(ci[7:8, :], ci.shape)
                xr = xr + (pr * br - pi * bi)
                xi = xi + (pr * bi + pi * br)
                xr_ref[rows, lanes] = xr
                xi_ref[rows, lanes] = xi
                return xr, xi

            cr, ci = lax.fori_loop(0, R // 8, step, (car_r[:, lanes], car_i[:, lanes]), unroll=2)
            car_r[:, lanes] = cr
            car_i[:, lanes] = ci
        y = _dot_nt(xr_ref[...], cbd_r[...]) - _dot_nt(xi_ref[...], cbd_i[...]) + d_ref[...] * u
        yg, _ = _gelu(y)
        z = _dot(yg, wg_ref[...]) + bg_ref[...]
        y_ref[...] = yg * jax.nn.sigmoid(z)

    full = lambda a: pl.BlockSpec(a.shape, lambda i: (0,) * a.ndim)
    small = [lre, lim, ldt, bre_t, bim_t, cre_w, cim_w, d_row, w_glu, b_glu]
    return _call(
        body, "s5_fwd", (tp // R,),
        [pl.BlockSpec((R, S5_W), lambda i: (i, 0))] + [full(a) for a in small],
        [pl.BlockSpec((R, S5_W), lambda i: (i, 0)), pl.BlockSpec((R, S5_N), lambda i: (i, 0)),
         pl.BlockSpec((R, S5_N), lambda i: (i, 0))],
        [jax.ShapeDtypeStruct((tp, S5_W), F32), jax.ShapeDtypeStruct((tp, S5_N), F32),
         jax.ShapeDtypeStruct((tp, S5_N), F32)],
        [pltpu.VMEM((S5_W, S5_N), MM)] * 4 + [pltpu.VMEM((8, 8, S5_N), F32), pltpu.VMEM((8, S5_N), F32),
                                              pltpu.VMEM((8, S5_N), F32)],
        (u, *small), jobs)


def _s5_bwd(dy_out, u, xr, xi, lre, lim, ldt, bre_t, bim_t, cre_w, cim_w, d_row, w_glu, b_glu, after, jobs=()):
    tp = u.shape[0]
    R = ROW_BLK
    nb = tp // R

    def body(dyo_ref, u_ref, xr_ref, xi_ref, xpr_ref, xpi_ref,
             lre_ref, lim_ref, ldt_ref, bre_ref, bim_ref, cre_ref, cim_ref, d_ref, wg_ref, bg_ref, after_ref,
             du_ref, dlre_ref, dlim_ref, dldt_ref, dbre_ref, dbim_ref, dcre_ref, dcim_ref, dd_ref, dwg_ref, dbg_ref,
             bbd_r, bbd_i, cbd_r, cbd_i, tab_ref, car_r, car_i, gr_ref, gi_ref, xer_ref, xei_ref,
             abr, abi, acr, aci, adr, adi):
        i = pl.program_id(0)

        @pl.when(i == 0)
        def _():
            lbr, lbi, bbr, bbi = _s5_disc(lre_ref[...], lim_ref[...], ldt_ref[...], bre_ref[...], bim_ref[...])
            br, bi, cr, ci = _s5_block_diag(bbr, bbi, cre_ref[...], cim_ref[...])
            bbd_r[...] = br.astype(MM)
            bbd_i[...] = bi.astype(MM)
            cbd_r[...] = cr.astype(MM)
            cbd_i[...] = ci.astype(MM)
            _store_tables(tab_ref, _s5_tables(lbr, lbi, True))
            for ref in (car_r, car_i, abr, abi, acr, aci, adr, adi, dd_ref, dwg_ref, dbg_ref):
                ref[...] = jnp.zeros_like(ref)

        u = u_ref[...]
        xrv = xr_ref[...]
        xiv = xi_ref[...]
        y = _dot_nt(xrv, cbd_r[...]) - _dot_nt(xiv, cbd_i[...]) + d_ref[...] * u
        yg, t = _gelu(y)
        z = _dot(yg, wg_ref[...]) + bg_ref[...]
        s = jax.nn.sigmoid(z)
        dout = dyo_ref[...]
        dz = dout * yg * s * (1.0 - s)
        dyg = dout * s + _dot_nt(dz, wg_ref[...])
        dwg_ref[...] += _dot_tn(yg, dz)
        dbg_ref[...] += _colsum(dz)
        dy = dyg * (0.5 * (1.0 + t) + 0.5 * y * (1.0 - t * t) * GELU_C * (1.0 + 3.0 * GELU_A * y * y))
        dd_ref[...] += _colsum(dy * u)
        acr[...] += _dot_tn(dy, xrv)
        aci[...] -= _dot_tn(dy, xiv)
        gr_ref[...] = _dot(dy, cbd_r[...])
        gi_ref[...] = -_dot(dy, cbd_i[...])
        has_prev = (i < nb - 1).astype(F32)
        xer_ref[0:8, :] = xpr_ref[...] * has_prev
        xei_ref[0:8, :] = xpi_ref[...] * has_prev
        xer_ref[8:R + 8, :] = xrv
        xei_ref[8:R + 8, :] = xiv
        row = lax.broadcasted_iota(jnp.int32, (8, S5_LANES), 0)
        for j in range(S5_N // S5_LANES):
            lanes = pl.ds(j * S5_LANES, S5_LANES)
            pr = tab_ref[6, :, lanes]
            pi = tab_ref[7, :, lanes]

            def step(n, carry):
                cr, ci, sar, sai = carry
                g = R // 8 - 1 - n
                r0 = pl.multiple_of(g * 8, 8)
                rows = pl.ds(r0, 8)
                gr, gi = _scan8(gr_ref[rows, lanes], gi_ref[rows, lanes], tab_ref, lanes, True)
                br = jnp.broadcast_to(cr[0:1, :], cr.shape)
                bi = jnp.broadcast_to(ci[0:1, :], ci.shape)
                gr = gr + (pr * br - pi * bi)
                gi = gi + (pr * bi + pi * br)
                gr_ref[rows, lanes] = gr
                gi_ref[rows, lanes] = gi
                last = row == 7
                xpr = pltpu.roll(jnp.where(last, xer_ref[rows, lanes], xer_ref[pl.ds(r0 + 8, 8), lanes]), 1, 0)
                xpi = pltpu.roll(jnp.where(last, xei_ref[rows, lanes], xei_ref[pl.ds(r0 + 8, 8), lanes]), 1, 0)
                return gr, gi, sar + (gr * xpr + gi * xpi), sai + (gi * xpr - gr * xpi)

            cr, ci, sar, sai = lax.fori_loop(
                0, R // 8, step, (car_r[:, lanes], car_i[:, lanes], adr[:, lanes], adi[:, lanes]), unroll=2)
            car_r[:, lanes] = cr
            car_i[:, lanes] = ci
            adr[:, lanes] = sar
            adi[:, lanes] = sai
        grv = gr_ref[...]
        giv = gi_ref[...]
        du_ref[...] = (dy * d_ref[...] + _dot_nt(grv, bbd_r[...]) + _dot_nt(giv, bbd_i[...])).astype(du_ref.dtype)
        abr[...] += _dot_tn(u, grv)
        abi[...] += _dot_tn(u, giv)

        @pl.when(i == nb - 1)
        def _():
            mask = _bd_mask()
            r16 = lax.broadcasted_iota(jnp.int32, (S5_H, S5_W), 1)
            h16 = lax.broadcasted_iota(jnp.int32, (S5_H, S5_W), 0)
            fold_b = jnp.bitwise_and(r16, S5_H - 1) == h16
            c64 = lax.broadcasted_iota(jnp.int32, (S5_N, S5_P), 0)
            p64 = lax.broadcasted_iota(jnp.int32, (S5_N, S5_P), 1)
            fold_c = jnp.bitwise_and(c64, S5_P - 1) == p64
            dbbr = _dot_sel_lhs(fold_b, jnp.where(mask, abr[...], 0.0))
            dbbi = _dot_sel_lhs(fold_b, jnp.where(mask, abi[...], 0.0))
            dcre_ref[...] = _dot_sel_rhs(jnp.where(mask, acr[...], 0.0), fold_c)
            dcim_ref[...] = _dot_sel_rhs(jnp.where(mask, aci[...], 0.0), fold_c)
            dlbr = _colsum(adr[...])
            dlbi = _colsum(adi[...])
            _, vjp = jax.vjp(_s5_disc, lre_ref[...], lim_ref[...], ldt_ref[...], bre_ref[...], bim_ref[...])
            dlre, dlim, dldt, dbre, dbim = vjp((dlbr, dlbi, dbbr, dbbi))
            dlre_ref[...] = dlre
            dlim_ref[...] = dlim
            dbre_ref[...] = dbre
            dbim_ref[...] = dbim
            gsel = jnp.right_shift(lax.broadcasted_iota(jnp.int32, (S5_N, HEAD), 0), 6) == \
                lax.broadcasted_iota(jnp.int32, (S5_N, HEAD), 1)
            dldt_ref[...] = _dot_sel_rhs(dldt, gsel)

    full = lambda a: pl.BlockSpec(a.shape, lambda i: (0,) * a.ndim)
    rev = lambda w: pl.BlockSpec((R, w), lambda i: (nb - 1 - i, 0))
    prev8 = pl.BlockSpec((8, S5_N), lambda i: (jnp.maximum((nb - 1 - i) * (R // 8) - 1, 0), 0))
    small = [lre, lim, ldt, bre_t, bim_t, cre_w, cim_w, d_row, w_glu, b_glu]
    outs = [((tp, S5_W), rev(S5_W))] + [
        (s, pl.BlockSpec(s, lambda i: (0, 0))) for s in
        [(1, S5_N), (1, S5_N), (1, HEAD), (S5_H, S5_N), (S5_H, S5_N), (S5_W, S5_P), (S5_W, S5_P),
         (1, S5_W), (S5_W, S5_W), (1, S5_W)]]
    return _call(
        body, "s5_bwd", (nb,),
        [rev(S5_W), rev(S5_W), rev(S5_N), rev(S5_N), prev8, prev8] + [full(a) for a in small + [after]],
        [o[1] for o in outs], [jax.ShapeDtypeStruct(o[0], MM if n == 0 else F32) for n, o in enumerate(outs)],
        [pltpu.VMEM((S5_W, S5_N), MM)] * 4 + [
            pltpu.VMEM((8, 8, S5_N), F32), pltpu.VMEM((8, S5_N), F32), pltpu.VMEM((8, S5_N), F32),
            pltpu.VMEM((R, S5_N), F32), pltpu.VMEM((R, S5_N), F32),
            pltpu.VMEM((R + 8, S5_N), F32), pltpu.VMEM((R + 8, S5_N), F32)] + [pltpu.VMEM((S5_W, S5_N), F32)] * 4 + [
            pltpu.VMEM((8, S5_N), F32), pltpu.VMEM((8, S5_N), F32)],
        (dy_out, u, xr, xi, xr, xi, *small, after), jobs)


RET_CHUNK = ROW_BLK
LOG_GAMMA = [math.log1p(-2.0 ** (-5 - h)) for h in range(RET_H)]
GAMMA_CHUNK = [math.exp(RET_CHUNK * lg) for lg in LOG_GAMMA]
_DECAY_SCRATCH = [pltpu.VMEM((RET_H, RET_CHUNK, RET_CHUNK), F32), pltpu.VMEM((RET_H, RET_CHUNK, HEAD), F32),
                  pltpu.VMEM((RET_H, RET_CHUNK, HEAD), F32)]


def _fill_decay(dm_ref, ze_ref, xi_ref):
    C = RET_CHUNK
    diff = (lax.broadcasted_iota(jnp.int32, (C, C), 0) - lax.broadcasted_iota(jnp.int32, (C, C), 1)).astype(F32)
    r = lax.broadcasted_iota(jnp.int32, (C, HEAD), 0).astype(F32)
    for h, lg in enumerate(LOG_GAMMA):
        dm_ref[h] = jnp.where(diff >= 0.0, jnp.exp(jnp.maximum(diff, 0.0) * lg), 0.0)
        ze_ref[h] = jnp.exp((C - 1.0 - r) * lg)
        xi_ref[h] = jnp.exp((r + 1.0) * lg)


def _ret_fwd(q, k, v, jobs=()):
    tp = q.shape[0]
    C = RET_CHUNK
    nc = tp // C

    def body(q_ref, k_ref, v_ref, o_ref, st_ref, s_ref, dm_ref, ze_ref, xi_ref):
        @pl.when(pl.program_id(0) == 0)
        def _():
            s_ref[...] = jnp.zeros_like(s_ref)
            _fill_decay(dm_ref, ze_ref, xi_ref)

        for h in range(RET_H):
            sl = slice(h * HEAD, (h + 1) * HEAD)
            qh, kh, vh = q_ref[:, sl], k_ref[:, sl], v_ref[:, sl]
            sh = s_ref[h]
            st_ref[0, sl, :] = sh
            scores = _dot_nt(qh, kh) * dm_ref[h]
            o_ref[:, sl] = _dot(scores, vh) + _dot(qh, sh) * xi_ref[h]
            s_ref[h] = GAMMA_CHUNK[h] * sh + _dot_tn(kh.astype(F32) * ze_ref[h], vh)

    blk = pl.BlockSpec((C, RET_W), lambda c: (c, 0))
    return _call(
        body, "ret_fwd", (nc,), [blk, blk, blk], [blk, pl.BlockSpec((1, RET_W, HEAD), lambda c: (c, 0, 0))],
        [jax.ShapeDtypeStruct((tp, RET_W), F32), jax.ShapeDtypeStruct((nc, RET_W, HEAD), F32)],
        [pltpu.VMEM((RET_H, HEAD, HEAD), F32)] + _DECAY_SCRATCH, (q, k, v), jobs)


def _ret_bwd(q, k, v, do, states, cos2, sin2, jobs=()):
    tp = q.shape[0]
    C = RET_CHUNK
    nc = tp // C

    def body(q_ref, k_ref, v_ref, do_ref, st_ref, cos_ref, sin_ref,
             dq_ref, dk_ref, dv_ref, ds_ref, dm_ref, ze_ref, xi_ref):
        @pl.when(pl.program_id(0) == 0)
        def _():
            ds_ref[...] = jnp.zeros_like(ds_ref)
            _fill_decay(dm_ref, ze_ref, xi_ref)

        cos = cos_ref[...]
        sin = sin_ref[...]
        for h in range(RET_H):
            sl = slice(h * HEAD, (h + 1) * HEAD)
            qh, kh, vh = q_ref[:, sl], k_ref[:, sl], v_ref[:, sl]
            dmh = dm_ref[h]
            sh = st_ref[0, sl, :]
            dsn = ds_ref[h]
            doh = do_ref[:, sl]
            dox = doh * xi_ref[h]
            a = _dot_nt(qh, kh) * dmh
            dqk = _dot_nt(doh, vh) * dmh
            kz = kh.astype(F32) * ze_ref[h]
            dv_ref[:, sl] = (_dot_tn(a, doh) + _dot(kz, dsn)).astype(dv_ref.dtype)
            dqr = _dot(dqk, kh) + _dot_nt(dox, sh)
            dkr = _dot_tn(dqk, qh) + ze_ref[h] * _dot_nt(vh, dsn)
            ds_ref[h] = GAMMA_CHUNK[h] * dsn + _dot_tn(qh, dox)
            dq_ref[:, sl] = (dqr * cos - pltpu.roll(dqr, HEAD // 2, 1) * sin).astype(dq_ref.dtype)
            dk_ref[:, sl] = ((dkr * cos - pltpu.roll(dkr, HEAD // 2, 1) * sin) * (HEAD ** -0.5)).astype(dk_ref.dtype)

    blk = pl.BlockSpec((C, RET_W), lambda c: (nc - 1 - c, 0))
    tab = pl.BlockSpec((C, HEAD), lambda c: (nc - 1 - c, 0))
    return _call(
        body, "ret_bwd", (nc,),
        [blk, blk, blk, blk, pl.BlockSpec((1, RET_W, HEAD), lambda c: (nc - 1 - c, 0, 0)), tab, tab],
        [blk, blk, blk], [jax.ShapeDtypeStruct((tp, RET_W), MM)] * 3,
        [pltpu.VMEM((RET_H, HEAD, HEAD), F32)] + _DECAY_SCRATCH, (q, k, v, do, states, cos2, sin2), jobs)


def _gn_gate(o, gate, gn_g, gn_b):
    xhat, rstd = _ln_fwd(o, GN_EPS)
    on = xhat * gn_g + gn_b
    s = jax.nn.sigmoid(gate)
    return gate * s * on, xhat, rstd, on, s


def _post_up(o, gate, ys5, xhat0, gn_g, gn_b, li_g, li_b, l1_g, l1_b, w_out, w_up, jobs=()):
    tp = o.shape[0]
    R = ROW_BLK

    def body(o_ref, g_ref, ys_ref, xh0_ref, gng, gnb, lig, lib, l1g, l1b, wo_ref, wu_ref,
             ycat_ref, xh1_ref, rstd1_ref, h1b_ref, pre_ref):
        ycat_ref[:, 0:S5_W] = ys_ref[...].astype(ycat_ref.dtype)
        for h in range(RET_H):
            sl = slice(h * HEAD, (h + 1) * HEAD)
            yret = _gn_gate(o_ref[:, sl], g_ref[:, sl], gng[:, sl], gnb[:, sl])[0]
            ycat_ref[:, S5_W + h * HEAD:S5_W + (h + 1) * HEAD] = yret.astype(ycat_ref.dtype)
        mixed = _dot(ycat_ref[...], wo_ref[...])
        h0 = xh0_ref[...] * lig[...] + lib[...]
        xh1, rstd1 = _ln_fwd(ALPHA * h0 + mixed, LN_EPS)
        xh1_ref[...] = xh1
        rstd1_ref[...] = rstd1
        h1b = (xh1 * l1g[...] + l1b[...]).astype(MM)
        h1b_ref[...] = h1b
        for d in range(N_DEV):
            pre_ref[:, d * FF_BLK:(d + 1) * FF_BLK] = jnp.maximum(_dot(h1b, wu_ref[d]), 0.0)

    row = lambda w: pl.BlockSpec((R, w), lambda i: (i, 0))
    full = lambda a: pl.BlockSpec(a.shape, lambda i: (0,) * a.ndim)
    vecs = [gn_g, gn_b, li_g, li_b, l1_g, l1_b]
    outs = [(row(D_MODEL), jax.ShapeDtypeStruct((tp, D_MODEL), MM)), (row(D_MODEL), jax.ShapeDtypeStruct((tp, D_MODEL), F32)),
            (row(1), jax.ShapeDtypeStruct((tp, 1), F32)), (row(D_MODEL), jax.ShapeDtypeStruct((tp, D_MODEL), MM)),
            (row(D_FF), jax.ShapeDtypeStruct((tp, D_FF), F32))]
    return _call(
        body, "post_up", (tp // R,),
        [row(RET_W), row(RET_W), row(S5_W), row(D_MODEL)] + [full(a) for a in vecs] + [_VMEM, _VMEM],
        [o[0] for o in outs], [o[1] for o in outs], [], (o, gate, ys5, xhat0, *vecs, w_out, w_up), jobs)


def _post_down(pre, xhat1, tgt, l1_g, l1_b, l2_g, l2_b, w_down):
    tp = pre.shape[0]
    seq = tgt.shape[0]
    R = ROW_BLK

    def body(pre_ref, xh1_ref, ta, tb, tc, l1g, l1b, l2g, l2b, wd_ref,
             dr2_ref, dffb_ref, loss_ref, dl2g_ref, dl2b_ref, tgt_ref):
        i = pl.program_id(0)

        @pl.when(i == 0)
        def _():
            for ref in (loss_ref, dl2g_ref, dl2b_ref):
                ref[...] = jnp.zeros_like(ref)

        tgt_ref[0:CHUNK, :] = ta[...]
        tgt_ref[CHUNK:2 * CHUNK, :] = tb[...]
        tgt_ref[2 * CHUNK:3 * CHUNK, :] = tc[...]
        ff = jnp.zeros((R, D_MODEL), F32)
        for d in range(N_DEV):
            pre = pre_ref[:, d * FF_BLK:(d + 1) * FF_BLK]
            ff = ff + _dot(pre * pre, wd_ref[d * FF_BLK:(d + 1) * FF_BLK, :])
        h1 = xh1_ref[...] * l1g[...] + l1b[...]
        xh2, rstd2 = _ln_fwd(ALPHA * h1 + ff, LN_EPS)
        h2 = xh2 * l2g[...] + l2b[...]
        valid = (i * R + lax.broadcasted_iota(jnp.int32, (R, 1), 0)) >= CHUNK
        err = jnp.where(valid, h2 - tgt_ref[...], 0.0)
        loss_ref[...] += 0.5 * jnp.sum(err * err) / D_MODEL
        dh2 = err * (1.0 / D_MODEL)
        dl2g_ref[...] += _colsum(dh2 * xh2)
        dl2b_ref[...] += _colsum(dh2)
        dr2 = _ln_bwd(dh2 * l2g[...], xh2, rstd2)
        dr2_ref[...] = dr2
        dffb_ref[...] = dr2.astype(MM)

    row = lambda w: pl.BlockSpec((R, w), lambda i: (i, 0))
    full = lambda a: pl.BlockSpec(a.shape, lambda i: (0,) * a.ndim)
    vecs = [l1_g, l1_b, l2_g, l2_b]
    acc = lambda s: (pl.BlockSpec(s, lambda i: (0, 0)), jax.ShapeDtypeStruct(s, F32))
    outs = [(row(D_MODEL), jax.ShapeDtypeStruct((tp, D_MODEL), F32)), (row(D_MODEL), jax.ShapeDtypeStruct((tp, D_MODEL), MM)),
            acc((8, HEAD)), acc((1, D_MODEL)), acc((1, D_MODEL))]
    return pl.pallas_call(
        body, name="post_down", grid=(tp // R,),
        in_specs=[row(D_FF), row(D_MODEL)] + _shift3(seq // CHUNK) + [full(a) for a in vecs] + [_VMEM],
        out_specs=[o[0] for o in outs], out_shape=[o[1] for o in outs],
        scratch_shapes=[pltpu.VMEM((R, D_MODEL), F32)],
        compiler_params=_params(("arbitrary",)),
    )(pre, xhat1, tgt, tgt, tgt, *vecs, w_down)


def _mlp_bwd(h1b, dffb, pre, w_up, w_down):
    tp = h1b.shape[0]
    R = MLP_ROWS if tp % MLP_ROWS == 0 else ROW_BLK
    nr = tp // R

    def body(h_ref, df_ref, pre_ref, wu_ref, wd_ref, gup_ref, gdn_ref, dh1_ref, aup, adn):
        d = pl.program_id(0)
        r = pl.program_id(1)

        @pl.when(r == 0)
        def _():
            aup[...] = jnp.zeros_like(aup)
            adn[...] = jnp.zeros_like(adn)

        h = h_ref[...]
        df = df_ref[...]
        wu = wu_ref[0]
        wd = wd_ref[0]
        pre = pre_ref[...]
        dpre = (_dot_nt(df, wd) * (2.0 * pre)).astype(MM)

        aup[...] += _dot_tn(h, dpre)
        adn[...] += _dot_tn(pre * pre, df)
        contrib = _dot_nt(dpre, wu)
        rows = pl.ds(pl.multiple_of(r * R, 64), R)

        @pl.when(d == 0)
        def _():
            dh1_ref[rows, :] = contrib

        @pl.when(d > 0)
        def _():
            dh1_ref[rows, :] += contrib

        @pl.when(r == nr - 1)
        def _():
            gup_ref[0] = aup[...].astype(gup_ref.dtype)
            gdn_ref[0] = adn[...].astype(gdn_ref.dtype)

    return pl.pallas_call(
        body, name="mlp_bwd", grid=(N_DEV, nr),
        in_specs=[pl.BlockSpec((R, D_MODEL), lambda d, r: (r, 0)), pl.BlockSpec((R, D_MODEL), lambda d, r: (r, 0)),
                  pl.BlockSpec((R, FF_BLK), lambda d, r: (r, d)),
                  pl.BlockSpec((1, D_MODEL, FF_BLK), lambda d, r: (d, 0, 0)),
                  pl.BlockSpec((1, FF_BLK, D_MODEL), lambda d, r: (d, 0, 0))],
        out_specs=[pl.BlockSpec((1, D_MODEL, FF_BLK), lambda d, r: (d, 0, 0)),
                   pl.BlockSpec((1, FF_BLK, D_MODEL), lambda d, r: (d, 0, 0)), _VMEM],
        out_shape=[jax.ShapeDtypeStruct((N_DEV, D_MODEL, FF_BLK), MM), jax.ShapeDtypeStruct((N_DEV, FF_BLK, D_MODEL), MM),
                   jax.ShapeDtypeStruct((tp, D_MODEL), F32)],
        scratch_shapes=[pltpu.VMEM((D_MODEL, FF_BLK), F32), pltpu.VMEM((FF_BLK, D_MODEL), F32)],
        compiler_params=_params(("arbitrary", "arbitrary")),
    )(h1b, dffb, pre, w_up, w_down.reshape(N_DEV, FF_BLK, D_MODEL))


def _post_bwd(dh1m, dr2, xhat1, rstd1, ycat, o, gate, gn_g, gn_b, l1_g, w_out, jobs=()):
    tp = o.shape[0]
    R = ROW_BLK
    nb = tp // R

    def body(dm_ref, dr2_ref, xh1_ref, rs1_ref, yc_ref, o_ref, g_ref, gng, gnb, l1g, wo_ref,
             do_ref, dg_ref, dys_ref, dh0_ref, gwo_ref, dl1g_ref, dl1b_ref, dgng_ref, dgnb_ref, awo):
        i = pl.program_id(0)

        @pl.when(i == 0)
        def _():
            for ref in (awo, dl1g_ref, dl1b_ref, dgng_ref, dgnb_ref):
                ref[...] = jnp.zeros_like(ref)

        dh1 = dm_ref[...] + ALPHA * dr2_ref[...]
        xh1 = xh1_ref[...]
        dl1g_ref[...] += _colsum(dh1 * xh1)
        dl1b_ref[...] += _colsum(dh1)
        dr1 = _ln_bwd(dh1 * l1g[...], xh1, rs1_ref[...])
        dh0_ref[...] = ALPHA * dr1
        dmix = dr1.astype(MM)
        awo[...] += _dot_tn(yc_ref[...], dmix)
        dyc = _dot_nt(dmix, wo_ref[...])
        dys_ref[...] = dyc[:, 0:S5_W]
        for h in range(RET_H):
            sl = slice(h * HEAD, (h + 1) * HEAD)
            gt = g_ref[:, sl]
            _, xhat, rstd, on, s = _gn_gate(o_ref[:, sl], gt, gng[:, sl], gnb[:, sl])
            dyr = dyc[:, S5_W + h * HEAD:S5_W + (h + 1) * HEAD]
            dg_ref[:, sl] = (dyr * on * (s * (1.0 + gt * (1.0 - s)))).astype(dg_ref.dtype)
            don = dyr * gt * s
            dgng_ref[:, sl] += _colsum(don * xhat)
            dgnb_ref[:, sl] += _colsum(don)
            do_ref[:, sl] = _ln_bwd(don * gng[:, sl], xhat, rstd)

        @pl.when(i == nb - 1)
        def _():
            gwo_ref[...] = awo[...].astype(gwo_ref.dtype)

    row = lambda w: pl.BlockSpec((R, w), lambda i: (i, 0))
    full = lambda a: pl.BlockSpec(a.shape, lambda i: (0,) * a.ndim)
    acc = lambda s, dt=F32: (pl.BlockSpec(s, lambda i: (0, 0)), jax.ShapeDtypeStruct(s, dt))
    outs = [(row(RET_W), jax.ShapeDtypeStruct((tp, RET_W), F32)), (row(RET_W), jax.ShapeDtypeStruct((tp, RET_W), MM)),
            (row(S5_W), jax.ShapeDtypeStruct((tp, S5_W), F32)), (row(D_MODEL), jax.ShapeDtypeStruct((tp, D_MODEL), F32)),
            acc((D_MODEL, D_MODEL), MM), acc((1, D_MODEL)), acc((1, D_MODEL)), acc((1, RET_W)), acc((1, RET_W))]
    return _call(
        body, "post_bwd", (nb,),
        [row(D_MODEL), row(D_MODEL), row(D_MODEL), row(1), row(D_MODEL), row(RET_W), row(RET_W),
         full(gn_g), full(gn_b), full(l1_g), _VMEM],
        [o[0] for o in outs], [o[1] for o in outs],
        [pltpu.VMEM((D_MODEL, D_MODEL), F32)],
        (dh1m, dr2, xhat1, rstd1, ycat, o, gate, gn_g, gn_b, l1_g, w_out), jobs)


_PROJ_SEGS = [(0, S5_W)] + [(S5_W + n * RET_W, S5_W + (n + 1) * RET_W) for n in range(4)]


def _in_w_grad(du, dq, dk, dv, dg, xhat0, li_g, li_b):
    tp = du.shape[0]
    R = PROJ_ROWS if tp % PROJ_ROWS == 0 else ROW_BLK
    nb = tp // R

    def body(du_ref, dq_ref, dk_ref, dv_ref, dg_ref, xh_ref, lig, lib, gw_ref, aw):
        i = pl.program_id(0)

        @pl.when(i == 0)
        def _():
            aw[...] = jnp.zeros_like(aw)

        valid = (i * R + lax.broadcasted_iota(jnp.int32, (R, 1), 0)) >= PAD
        hb = (xh_ref[...] * lig[...] + lib[...]).astype(MM)
        for (lo, hi), ref in zip(_PROJ_SEGS, (du_ref, dq_ref, dk_ref, dv_ref, dg_ref)):
            aw[lo:hi, :] += _dot_tn(jnp.where(valid, ref[...], 0.0).astype(MM), hb)

        @pl.when(i == nb - 1)
        def _():
            gw_ref[...] = aw[...].astype(gw_ref.dtype)

    row = lambda w: pl.BlockSpec((R, w), lambda i: (i, 0))
    full = lambda a: pl.BlockSpec(a.shape, lambda i: (0,) * a.ndim)
    (gw,), _ = _call(
        body, "in_w_grad", (nb,),
        [row(S5_W), row(RET_W), row(RET_W), row(RET_W), row(RET_W), row(D_MODEL), full(li_g), full(li_b)],
        [pl.BlockSpec((PROJ_W, D_MODEL), lambda i: (0, 0))], [jax.ShapeDtypeStruct((PROJ_W, D_MODEL), MM)],
        [pltpu.VMEM((PROJ_W, D_MODEL), F32)], (du, dq, dk, dv, dg, xhat0, li_g, li_b))
    return gw


def _in_bwd(du, dq, dk, dv, dg, dh0r, xhat0, rstd0, li_g, w_int, after):
    tp = du.shape[0]
    R = PROJ_ROWS if tp % PROJ_ROWS == 0 else ROW_BLK
    nb = tp // R
    segs = _PROJ_SEGS

    def body(du_ref, dq_ref, dk_ref, dv_ref, dg_ref, dh0r_ref, xh_ref, rs_ref, lig, w_ref, after_ref,
             gx_ref, dmeta_ref, dlg_ref, dlb_ref, stage, out_sems):
        i = pl.program_id(0)
        slot = i % 2

        def to_gx(step_slot, first):
            if first:
                return pltpu.make_async_copy(stage.at[0, CHUNK:R, :], gx_ref.at[0:R - CHUNK, :], out_sems.at[0])
            return pltpu.make_async_copy(stage.at[step_slot], gx_ref.at[pl.ds(i * R - CHUNK, R), :], out_sems.at[step_slot])

        @pl.when(i == 0)
        def _():
            for ref in (dlg_ref, dlb_ref):
                ref[...] = jnp.zeros_like(ref)

        @pl.when(i >= 3)
        def _():
            to_gx(slot, False).wait()

        valid = (i * R + lax.broadcasted_iota(jnp.int32, (R, 1), 0)) >= PAD
        xh = xh_ref[...]
        dh0 = dh0r_ref[...]
        for (lo, hi), ref in zip(segs, (du_ref, dq_ref, dk_ref, dv_ref, dg_ref)):
            dh0 = dh0 + _dot(jnp.where(valid, ref[...], 0.0).astype(MM), w_ref[lo:hi, :])
        dlg_ref[...] += _colsum(dh0 * xh)
        dlb_ref[...] += _colsum(dh0)
        draw = _ln_bwd(dh0 * lig[...], xh, rs_ref[...])
        stage[slot] = draw

        @pl.when(i == 0)
        def _():
            dmeta_ref[...] = draw[PAD:CHUNK, :]
            first = to_gx(0, True)
            first.start()
            first.wait()

        @pl.when(i > 0)
        def _():
            to_gx(slot, False).start()

        @pl.when(i == nb - 1)
        def _():
            for back in (1, 0):
                if nb - 1 - back >= 1:
                    to_gx((nb - 1 - back) % 2, False).wait()

    row = lambda w: pl.BlockSpec((R, w), lambda i: (i, 0))
    full = lambda a: pl.BlockSpec(a.shape, lambda i: (0,) * a.ndim)
    acc = lambda s, dt=F32: (pl.BlockSpec(s, lambda i: (0, 0)), jax.ShapeDtypeStruct(s, dt))
    outs = [(_ANY, jax.ShapeDtypeStruct((tp - CHUNK, D_MODEL), F32)), acc((N_META, D_MODEL)),
            acc((1, D_MODEL)), acc((1, D_MODEL))]
    return _call(
        body, "in_bwd", (nb,),
        [row(S5_W), row(RET_W), row(RET_W), row(RET_W), row(RET_W), row(D_MODEL), row(D_MODEL), row(1),
         full(li_g), _VMEM, full(after)],
        [o[0] for o in outs], [o[1] for o in outs],
        [pltpu.VMEM((2, R, D_MODEL), F32), pltpu.SemaphoreType.DMA((2,))],
        (du, dq, dk, dv, dg, dh0r, xhat0, rstd0, li_g, w_int, after))[0]


def _place():
    return lax.axis_index("x"), lax.axis_index("y"), lax.axis_index("c")


def _dma_sems(n):
    return pltpu.SemaphoreType.DMA((n,))


def _job_gather(shard):
    def parts(ins, outs, sems):
        (src,), (out,), (send_sems, recv_sems, local_sem) = ins, outs, sems
        x, y, c = _place()
        north = c == 1
        me, sib = (x, y, c), (x, y, 1 - c)
        xn, yn, dg = (1 - x, y, c), (x, 1 - y, c), (1 - x, 1 - y, c)
        relay_from = (jnp.where(north, 1 - x, x), jnp.where(north, y, 1 - y), c)
        relay_to = (jnp.where(north, x, 1 - x), jnp.where(north, 1 - y, y), c)

        def slot(dev):
            return out.at[4 * dev[0] + 2 * dev[1] + dev[2]]

        def copy(k, block, to, from_input=False):
            return pltpu.make_async_remote_copy(
                src_ref=src if from_input else slot(block), dst_ref=slot(block),
                send_sem=send_sems.at[k], recv_sem=recv_sems.at[k], device_id=to, device_id_type=_MESH)

        mine = lambda: pltpu.make_async_copy(src, slot(me), local_sem.at[0])
        first = lambda: [copy(0, me, sib, True), copy(1, me, xn, True), copy(2, me, yn, True)]
        relayed = lambda: [copy(3, relay_from, relay_to), copy(4, xn, sib), copy(5, yn, sib)]
        return me, sib, xn, yn, dg, copy, mine, first, relayed

    def start(ins, outs, sems):
        mine, first = parts(ins, outs, sems)[6:8]
        mine().start()
        for cp in first():
            cp.start()

    def relay(ins, outs, sems):
        me, sib, xn, yn, dg, copy, mine, first, relayed = parts(ins, outs, sems)
        copy(1, xn, me).wait_recv()
        copy(2, yn, me).wait_recv()
        for cp in relayed():
            cp.start()

    def finish(ins, outs, sems):
        me, sib, xn, yn, dg, copy, mine, first, relayed = parts(ins, outs, sems)
        other = 1 - me[2]
        copy(3, dg, me).wait_recv()
        last = copy(6, dg, sib)
        last.start()
        copy(0, sib, me).wait_recv()
        for k, chip in ((4, xn), (5, yn), (6, dg)):
            copy(k, (chip[0], chip[1], other), me).wait_recv()
        for cp in first() + relayed() + [last]:
            cp.wait_send()
        mine().wait()

    return dict(ins=[shard], outs=[jax.ShapeDtypeStruct((N_DEV,) + shard.shape, shard.dtype)],
                sems=[_dma_sems(7), _dma_sems(7), _dma_sems(1)], start=start, middle=relay, finish=finish)


def _job_gather_direct(shard):
    def copies(ins, outs, sems):
        (src,), (out,), (send_sems, recv_sems, local_sem) = ins, outs, sems
        x, y, c = _place()
        flip = lambda a, bit: 1 - a if bit else a
        slot = out.at[4 * x + 2 * y + c]
        return [pltpu.make_async_copy(src, slot, local_sem.at[0])] + [pltpu.make_async_remote_copy(
            src_ref=src, dst_ref=slot, send_sem=send_sems.at[k - 1], recv_sem=recv_sems.at[k - 1],
            device_id=(flip(x, k & 4), flip(y, k & 2), flip(c, k & 1)), device_id_type=_MESH) for k in range(1, N_DEV)]

    def start(ins, outs, sems):
        for cp in copies(ins, outs, sems):
            cp.start()

    def finish(ins, outs, sems):
        for cp in copies(ins, outs, sems):
            cp.wait()

    return dict(ins=[shard], outs=[jax.ShapeDtypeStruct((N_DEV,) + shard.shape, shard.dtype)],
                sems=[_dma_sems(N_DEV - 1), _dma_sems(N_DEV - 1), _dma_sems(1)], start=start, finish=finish)


def _job_pair(g):
    def copies(ins, outs, sems):
        x, y, c = _place()
        return [pltpu.make_async_remote_copy(
            src_ref=ins[0].at[2 * j + (1 - c)], dst_ref=outs[0].at[j], send_sem=sems[0].at[j], recv_sem=sems[1].at[j],
            device_id=(x, y, 1 - c), device_id_type=_MESH) for j in range(4)]

    def start(ins, outs, sems):
        for cp in copies(ins, outs, sems):
            cp.start()

    def finish(ins, outs, sems):
        for cp in copies(ins, outs, sems):
            cp.wait()

    return dict(ins=[g], outs=[jax.ShapeDtypeStruct((4,) + g.shape[1:], g.dtype)], sems=[_dma_sems(4), _dma_sems(4)],
                start=start, finish=finish)


def _job_chips(p):
    def copies(ins, outs, sems):
        x, y, c = _place()
        chips = [(1 - x, y), (x, 1 - y), (1 - x, 1 - y)]
        return [pltpu.make_async_remote_copy(
            src_ref=ins[0].at[2 * chip[0] + chip[1]], dst_ref=outs[0].at[k], send_sem=sems[0].at[k],
            recv_sem=sems[1].at[k], device_id=(*chip, c), device_id_type=_MESH) for k, chip in enumerate(chips)]

    def start(ins, outs, sems):
        for cp in copies(ins, outs, sems):
            cp.start()

    def finish(ins, outs, sems):
        for cp in copies(ins, outs, sems):
            cp.wait()

    return dict(ins=[p], outs=[jax.ShapeDtypeStruct((3,) + p.shape[1:], p.dtype)], sems=[_dma_sems(3), _dma_sems(3)],
                start=start, finish=finish)


_HBM = pl.BlockSpec(memory_space=pltpu.HBM)
_SEM = pl.BlockSpec(memory_space=pltpu.SEMAPHORE)
_ORDERED = pltpu.CompilerParams(has_side_effects=pltpu.SideEffectType.DATAFLOW_SIDE_EFFECTING)


def _chip_copies(p_ref, land_ref, sems):
    x, y, c = _place()
    chips = [(1 - x, y), (x, 1 - y), (1 - x, 1 - y)]
    return [pltpu.make_async_remote_copy(
        src_ref=p_ref.at[2 * chip[0] + chip[1]], dst_ref=land_ref.at[k], send_sem=sems[k], recv_sem=sems[3 + k],
        device_id=(*chip, c), device_id_type=_MESH) for k, chip in enumerate(chips)]


def _chips_start(ps, name):
    n = len(ps)

    def body(*refs):
        sems = refs[2 * n:8 * n]
        for a in range(n):
            for cp in _chip_copies(refs[a], refs[n + a], sems[6 * a:6 * a + 6]):
                cp.start()
        refs[-1][...] = jnp.zeros_like(refs[-1])

    lands = [(3,) + p.shape[1:] for p in ps]
    hbm = lambda arr: pltpu.with_memory_space_constraint(arr, pltpu.HBM)
    outs = pl.pallas_call(
        body, name=name,
        out_shape=(*[pltpu.SemaphoreType.DMA(())] * (6 * n), *[pltpu.HBM(p.shape, p.dtype) for p in ps],
                   *[pltpu.HBM(s, p.dtype) for s, p in zip(lands, ps)], jax.ShapeDtypeStruct((8, LANE), F32)),
        in_specs=[_HBM] * (2 * n), out_specs=(*[_SEM] * (6 * n), *[_HBM] * (2 * n), _VMEM),
        input_output_aliases={a: 6 * n + a for a in range(2 * n)}, compiler_params=_ORDERED,
    )(*[hbm(p) for p in ps], *[hbm(lax.empty(s, p.dtype)) for s, p in zip(lands, ps)])
    return (list(outs[:6 * n]), list(outs[6 * n:7 * n]), list(outs[7 * n:8 * n])), outs[8 * n]


def _chips_wait(started, after, name):
    sems, thrus, lands = started
    n = len(thrus)

    def body(*refs):
        for a in range(n):
            for cp in _chip_copies(refs[a], refs[n + a], refs[2 * n + 6 * a:2 * n + 6 * a + 6]):
                cp.wait_send()
                cp.wait_recv()

    outs = pl.pallas_call(
        body, name=name, out_shape=[pltpu.HBM(t.shape, t.dtype) for t in thrus + lands],
        in_specs=(*[_HBM] * (2 * n), *[_SEM] * (6 * n), _ANY), out_specs=[_HBM] * (2 * n),
        input_output_aliases={a: a for a in range(2 * n)}, compiler_params=_ORDERED,
    )(*thrus, *lands, *sems, after)
    return list(outs[:n]), list(outs[n:])


def _split_job_refs(jobs, ins, outs, sems):
    res, a, b, c = [], 0, 0, 0
    for job in jobs:
        na, nb, nc = len(job["ins"]), len(job["outs"]), len(job["sems"])
        res.append((ins[a:a + na], outs[b:b + nb], sems[c:c + nc]))
        a, b, c = a + na, b + nb, c + nc
    return res


def _call(body, name, grid, in_specs, out_specs, out_shape, scratch, args, jobs=(), prefetch=None, early=0):
    jobs = list(jobs)
    n_in, n_out, n_scr = len(in_specs), len(out_specs), len(scratch)
    j_in = [a for job in jobs for a in job["ins"]]
    j_out = [o for job in jobs for o in job["outs"]]
    j_scr = [s for job in jobs for s in job["sems"]]
    nsteps = grid[0]
    n_pre = 0 if prefetch is None else 1

    def wrapped(*refs):
        pre, refs = refs[:n_pre], refs[n_pre:]
        ins, jins = refs[:n_in], refs[n_in:n_in + len(j_in)]
        refs = refs[n_in + len(j_in):]
        outs, jouts = refs[:n_out], refs[n_out:n_out + len(j_out)]
        refs = refs[n_out + len(j_out):]
        scr, jscr = refs[:n_scr], refs[n_scr:]
        per_job = _split_job_refs(jobs, jins, jouts, jscr)

        def middle():
            for job, r in zip(jobs, per_job):
                if "middle" in job:
                    job["middle"](*r)

        @pl.when(pl.program_id(0) == 0)
        def _():
            for job, r in zip(jobs, per_job):
                job["start"](*r)

        if nsteps >= 3:
            pl.when(pl.program_id(0) == (2 * nsteps) // 3)(middle)

        if early:
            @pl.when(pl.program_id(0) == nsteps - 1)
            def _():
                for job, r in zip(jobs[:early], per_job[:early]):
                    job["finish"](*r)

        body(*pre, *ins, *outs, *scr, *[o for r in per_job[:early] for o in r[1]])

        @pl.when(pl.program_id(0) == nsteps - 1)
        def _():
            if nsteps < 3:
                middle()
            for job, r in zip(jobs[early:], per_job[early:]):
                job["finish"](*r)

    specs = dict(in_specs=list(in_specs) + [_ANY] * len(j_in), out_specs=list(out_specs) + [_ANY] * len(j_out),
                 scratch_shapes=list(scratch) + j_scr)
    if n_pre:
        specs = dict(grid_spec=pltpu.PrefetchScalarGridSpec(num_scalar_prefetch=1, grid=grid, **specs))
    else:
        specs["grid"] = grid
    res = pl.pallas_call(
        wrapped if jobs else body, name=name, out_shape=list(out_shape) + j_out,
        compiler_params=_params(("arbitrary",) * len(grid)), **specs,
    )(*([prefetch] if n_pre else []), *args, *j_in)
    return list(res[:n_out]), list(res[n_out:])


def _exchange(jobs, name):
    j_in = [a for job in jobs for a in job["ins"]]
    j_out = [o for job in jobs for o in job["outs"]]
    j_scr = [s for job in jobs for s in job["sems"]]

    def body(*refs):
        per_job = _split_job_refs(jobs, refs[:len(j_in)], refs[len(j_in):len(j_in) + len(j_out)],
                                  refs[len(j_in) + len(j_out):])
        for phase in ("start", "middle", "finish"):
            for job, r in zip(jobs, per_job):
                if phase in job:
                    job[phase](*r)

    return pl.pallas_call(body, name=name, out_shape=j_out, in_specs=[_ANY] * len(j_in), out_specs=[_ANY] * len(j_out),
                          scratch_shapes=j_scr)(*j_in)


def _pair_sum(gs, r1s, c_arr, name):
    n = len(gs)

    def body(c_ref, *refs):
        for a in range(n):
            refs[2 * n + a][...] = (refs[a][...].astype(F32) + refs[n + a][...].astype(F32)).astype(refs[2 * n + a].dtype)

    def blk(g, own):
        s = g.shape[1:]
        if own:
            return pl.BlockSpec((1,) + s, lambda j, c_ref: (2 * j + c_ref[0],) + (0,) * len(s))
        return pl.BlockSpec((1,) + s, lambda j, c_ref: (j,) + (0,) * len(s))

    return pl.pallas_call(
        body, name=name,
        grid_spec=pltpu.PrefetchScalarGridSpec(
            num_scalar_prefetch=1, grid=(4,),
            in_specs=[blk(g, True) for g in gs] + [blk(g, False) for g in gs],
            out_specs=[blk(g, False) for g in gs]),
        out_shape=[jax.ShapeDtypeStruct((4,) + g.shape[1:], g.dtype) for g in gs],
        compiler_params=_params(("arbitrary",)),
    )(c_arr, *gs, *r1s)


def _adamw_math(w, g, m, v):
    m = ADAM_B1 * m + (1.0 - ADAM_B1) * g
    v = ADAM_B2 * v + (1.0 - ADAM_B2) * (g * g)
    m_hat = m / (1.0 - ADAM_B1 ** ADAM_STEP)
    v_hat = v / (1.0 - ADAM_B2 ** ADAM_STEP)
    return -ADAM_LR * (m_hat / (jnp.sqrt(v_hat) + ADAM_EPS) + ADAM_WD * w), m, v


def _view(name, a):
    return jnp.swapaxes(a, -1, -2) if name in ("w_in", "s5_b_re", "s5_b_im") else a


def _adamw_shards(items, name, steps, chip, jobs=()):
    n = len(items)

    def body(chip_ref, *refs):
        for a in range(n):
            p_ref, r_ref, w_ref, m_ref, v_ref = refs[5 * a:5 * a + 5]
            g = ((p_ref[0].astype(F32) + r_ref[0].astype(F32)) + r_ref[1].astype(F32)) + r_ref[2].astype(F32)
            outs = refs[5 * n + 4 * a:5 * n + 4 * a + 4]
            outs[0][...] = g
            outs[1][...], outs[2][...], outs[3][...] = _adamw_math(w_ref[...], g, m_ref[...], v_ref[...])

    in_specs, out_specs, out_shape, flat = [], [], [], []
    for p, r, w, m, v in items:
        rows, cols = w.shape
        rb = rows // steps
        in_specs += [pl.BlockSpec((1, rb, cols), lambda i, c: (c[0], i, 0)), pl.BlockSpec((3, rb, cols), lambda i, c: (0, i, 0))]
        wblk = pl.BlockSpec((rb, cols), lambda i, c: (i, 0))
        in_specs += [wblk] * 3
        out_specs += [wblk] * 4
        out_shape += [jax.ShapeDtypeStruct(w.shape, F32)] * 4
        flat += [p, r, w, m, v]
    return _call(body, name, (steps,), in_specs, out_specs, out_shape, [], flat, jobs, prefetch=chip)


def _sum_devices(gathered, name):
    def body(gs_ref, g_ref):
        g = gs_ref[0]
        for s in range(1, N_DEV):
            g = g + gs_ref[s]
        g_ref[...] = g

    return pl.pallas_call(body, name=name, out_shape=jax.ShapeDtypeStruct(gathered.shape[1:], F32),
                          in_specs=[_VMEM], out_specs=_VMEM, compiler_params=_params())(gathered)


def _adamw_native(items, name):
    n = len(items)

    def body(*refs):
        for a in range(n):
            g, w, m, v = (refs[4 * a + t][...] for t in range(4))
            refs[4 * n + 3 * a][...], refs[4 * n + 3 * a + 1][...], refs[4 * n + 3 * a + 2][...] = _adamw_math(w, g, m, v)

    return pl.pallas_call(
        body, name=name, out_shape=[jax.ShapeDtypeStruct(it[1].shape, F32) for it in items for _ in range(3)],
        in_specs=[_VMEM] * (4 * n), out_specs=[_VMEM] * (3 * n), compiler_params=_params(),
    )(*[t for it in items for t in it])


SMALL = ["ln_in_g", "ln_in_b", "s5_lambda_re", "s5_lambda_im", "s5_log_dt", "s5_b_re", "s5_b_im", "s5_c_re", "s5_c_im",
         "s5_d", "s5_b_glu", "ret_gn_g", "ret_gn_b", "ln1_g", "ln1_b", "ln2_g", "ln2_b"]
LATE = ["ln_in_g", "ln_in_b", "meta_tokens"]
EARLY = [n for n in SMALL if n not in LATE] + ["s5_w_glu", "loss"]
LANE = 128


def _pack(arrs):
    parts = []
    for a in arrs:
        f = a.reshape(-1)
        parts.append(jnp.pad(f, (0, (-f.shape[0]) % LANE)))
    flat = jnp.concatenate(parts)
    rows = -(-flat.shape[0] // LANE)
    flat = jnp.pad(flat, (0, (-rows % 8) * LANE + rows * LANE - flat.shape[0]))
    return flat.reshape(-1, LANE)


def _unpack(packed, shapes):
    flat = packed.reshape(-1)
    out, off = [], 0
    for s in shapes:
        n = math.prod(s)
        out.append(flat[off:off + n].reshape(s))
        off += n + (-n) % LANE
    return out


def _rope_tables(tp):
    inv_freq = 1.0 / (ROPE_BASE ** (jnp.arange(0, HEAD, 2, dtype=F32) / HEAD))
    blk = (jnp.arange(tp // ROW_BLK, dtype=F32) * ROW_BLK)[:, None, None] * inv_freq
    off = (jnp.arange(ROW_BLK, dtype=F32) - float(PAD))[None, :, None] * inv_freq
    cos = (jnp.cos(blk) * jnp.cos(off) - jnp.sin(blk) * jnp.sin(off)).reshape(tp, HEAD // 2)
    sin = (jnp.sin(blk) * jnp.cos(off) + jnp.cos(blk) * jnp.sin(off)).reshape(tp, HEAD // 2)
    return jnp.concatenate([cos, cos], axis=1), jnp.concatenate([-sin, sin], axis=1)


def _local_step(x2d, tgt, meta, w_int, w_out, w_up, w_down, w_glu, sp, distributed):
    tp = x2d.shape[0] + CHUNK
    row = lambda a: a.reshape(1, -1)
    cos2, sin2 = _rope_tables(tp)
    li_g, li_b = row(sp["ln_in_g"]), row(sp["ln_in_b"])
    l1_g, l1_b, l2_g, l2_b = row(sp["ln1_g"]), row(sp["ln1_b"]), row(sp["ln2_g"]), row(sp["ln2_b"])
    gn_g, gn_b = row(sp["ret_gn_g"]), row(sp["ret_gn_b"])
    lre, lim = row(sp["s5_lambda_re"]), row(sp["s5_lambda_im"])
    ldt = row(jnp.repeat(sp["s5_log_dt"].reshape(-1), S5_P))
    to_t = lambda b: b.reshape(S5_G, S5_P, S5_H).transpose(2, 0, 1).reshape(S5_H, S5_N)
    bre_t, bim_t = to_t(sp["s5_b_re"]), to_t(sp["s5_b_im"])
    to_w = lambda c: jnp.tile(c.reshape(S5_W, S5_P), (1, 2))
    cre_w, cim_w = to_w(sp["s5_c_re"]), to_w(sp["s5_c_im"])

    jobs = (lambda *j: list(j)) if distributed else (lambda *j: [])
    c_arr = jnp.reshape(lax.axis_index("c"), (1,)).astype(jnp.int32) if distributed else None
    (xhat0, rstd0), bg = _ln_in(x2d, meta, jobs(*([_job_gather(w_int), _job_gather(w_glu)] if distributed else [])),
                                gather_meta=distributed)
    if distributed:
        w_int, w_glu = bg[1].reshape(PROJ_W, D_MODEL), bg[2].reshape(S5_W, S5_W)
    s5_small = (lre, lim, ldt, bre_t, bim_t, cre_w, cim_w, row(sp["s5_d"]), w_glu, row(sp["s5_b_glu"]))
    (u, q, k, v, gate), bg = _in_proj(xhat0, li_g, li_b, w_int, cos2, sin2,
                                      jobs(_job_gather(w_out) if distributed else None))
    if distributed:
        w_out = bg[0].reshape(D_MODEL, D_MODEL)
    (ys5, xr, xi), bg = _s5_fwd(u, *s5_small, jobs=jobs(_job_gather(w_up) if distributed else None))
    if distributed:
        w_up = bg[0]
    (o, states), _ = _ret_fwd(q, k, v)
    (ycat, xhat1, rstd1, h1b, pre), bg = _post_up(o, gate, ys5, xhat0, gn_g, gn_b, li_g, li_b, l1_g, l1_b, w_out, w_up,
                                                  jobs(_job_gather(w_down) if distributed else None))
    if distributed:
        w_down = bg[0].reshape(D_FF, D_MODEL)
    dr2, dffb, loss8, dl2g, dl2b = _post_down(pre, xhat1, tgt, l1_g, l1_b, l2_g, l2_b, w_down)
    g_up, g_down, dh1m = _mlp_bwd(h1b, dffb, pre, w_up, w_down)
    (do, dgate, dys5, dh0r, g_out, dl1g, dl1b, dgng, dgnb), bg = _post_bwd(
        dh1m, dr2, xhat1, rstd1, ycat, o, gate, gn_g, gn_b, l1_g, w_out,
        jobs(*([_job_pair(g_up), _job_pair(g_down)] if distributed else [])))
    g_out = g_out.reshape(N_DEV, D_MODEL // N_DEV, D_MODEL)
    after = jnp.zeros((8, LANE), F32)
    if distributed:
        p_up, p_down = _pair_sum([g_up, g_down], bg, c_arr, "pair_sum_mlp")
        started_mlp, after = _chips_start([p_up, p_down], "chips_mlp_start")
    (du, dlre, dlim, dldt, dbre_t, dbim_t, dcre, dcim, dd, dwglu, dbglu), bg = _s5_bwd(
        dys5, u, xr, xi, *s5_small, after, jobs=jobs(_job_pair(g_out) if distributed else None))
    if distributed:
        (p_out,) = _pair_sum([g_out], bg, c_arr, "pair_sum_out")
    from_t = lambda t: t.reshape(S5_H, S5_G, S5_P).transpose(1, 0, 2)
    small = {
        "s5_lambda_re": dlre, "s5_lambda_im": dlim, "s5_log_dt": dldt[:, :S5_G],
        "s5_b_re": from_t(dbre_t), "s5_b_im": from_t(dbim_t), "s5_c_re": dcre, "s5_c_im": dcim, "s5_d": dd,
        "s5_b_glu": dbglu, "ret_gn_g": dgng, "ret_gn_b": dgnb, "ln1_g": dl1g, "ln1_b": dl1b, "ln2_g": dl2g, "ln2_b": dl2b,
        "s5_w_glu": dwglu, "loss": loss8[0:1, 0:1]}
    early_pack = _pack([small[n] for n in EARLY])
    (dq, dk, dv), bg = _ret_bwd(q, k, v, do, states, cos2, sin2,
                                jobs(*([_job_chips(p_out), _job_gather(early_pack)] if distributed else [])))
    g_int = _in_w_grad(du, dq, dk, dv, dgate, xhat0, li_g, li_b).reshape(N_DEV, PROJ_W // N_DEV, D_MODEL)
    after = jnp.zeros((8, LANE), F32)
    if distributed:
        (r1_in,) = _exchange([_job_pair(g_int)], "exchange_pair_in")
        (p_in,) = _pair_sum([g_int], [r1_in], c_arr, "pair_sum_in")
        (p_up, p_down), (r_up, r_down) = _chips_wait(started_mlp, p_in, "chips_mlp_wait")
        started_in, after = _chips_start([p_in], "chips_in_start")
    grad_x, dmeta, dlig, dlib = _in_bwd(du, dq, dk, dv, dgate, dh0r, xhat0, rstd0, li_g, w_int, after)
    small.update(ln_in_g=dlig, ln_in_b=dlib, meta_tokens=dmeta)
    if distributed:
        (p_in,), (r_in,) = _chips_wait(started_in, dlig, "chips_in_wait")
        big = dict(chip_sums=[p_in, p_out, p_up, p_down], received=[r_in, bg[0], r_up, r_down], early=bg[1])
    else:
        big = dict(partials=[g_int, g_out, g_up, g_down])
    return grad_x, big, small


def kernel(x, meta_tokens, ln_in_g, ln_in_b, w_in, s5_lambda_re, s5_lambda_im, s5_log_dt, s5_b_re, s5_b_im, s5_c_re, s5_c_im, s5_d, s5_w_glu, s5_b_glu, ret_gn_g, ret_gn_b, w_out, ln1_g, ln1_b, w_up, w_down, ln2_g, ln2_b, loss_target, m_meta_tokens, m_ln_in_g, m_ln_in_b, m_w_in, m_s5_lambda_re, m_s5_lambda_im, m_s5_log_dt, m_s5_b_re, m_s5_b_im, m_s5_c_re, m_s5_c_im, m_s5_d, m_s5_w_glu, m_s5_b_glu, m_ret_gn_g, m_ret_gn_b, m_w_out, m_ln1_g, m_ln1_b, m_w_up, m_w_down, m_ln2_g, m_ln2_b, v_meta_tokens, v_ln_in_g, v_ln_in_b, v_w_in, v_s5_lambda_re, v_s5_lambda_im, v_s5_log_dt, v_s5_b_re, v_s5_b_im, v_s5_c_re, v_s5_c_im, v_s5_d, v_s5_w_glu, v_s5_b_glu, v_ret_gn_g, v_ret_gn_b, v_w_out, v_ln1_g, v_ln1_b, v_w_up, v_w_down, v_ln2_g, v_ln2_b):
    args = dict(locals())
    names = ["meta_tokens", "ln_in_g", "ln_in_b", "w_in", "s5_lambda_re", "s5_lambda_im", "s5_log_dt", "s5_b_re", "s5_b_im",
             "s5_c_re", "s5_c_im", "s5_d", "s5_w_glu", "s5_b_glu", "ret_gn_g", "ret_gn_b", "w_out", "ln1_g", "ln1_b",
             "w_up", "w_down", "ln2_g", "ln2_b"]
    ax, ay, ac = _place()
    me = 4 * ax + 2 * ay + ac

    sp = {n: args[n] for n in SMALL}
    grad_x, big, small = _local_step(x[0], loss_target[0], meta_tokens, w_in[0].T.astype(MM), w_out[0].astype(MM),
                                   w_up[0].astype(MM), w_down[0].astype(MM), s5_w_glu[0].astype(MM), sp, True)

    j_arr = jnp.reshape(2 * ax + ay, (1,)).astype(jnp.int32)
    two_d = lambda a: a.reshape(a.shape[-2:])
    item = lambda n, p, r: (p, r, *(two_d(_view(n, a)) for a in (args[n], args["m_" + n], args["v_" + n])))
    late_pack = _pack([small[n] for n in LATE])
    mlp = ("w_out", "w_up", "w_down")
    (late_all,) = _exchange([_job_gather_direct(late_pack)], "gather_small_late")
    res, _ = _adamw_shards(
        [item(n, p, r) for n, p, r in zip(mlp, big["chip_sums"][1:], big["received"][1:])], "adamw_mlp", 8, j_arr)
    res_in, _ = _adamw_shards([item("w_in", big["chip_sums"][0], big["received"][0])], "adamw_in", 2, j_arr)
    upd = {"w_in": res_in}
    for idx, n in enumerate(mlp):
        upd[n] = res[4 * idx:4 * idx + 4]
    shard_grads = {n: upd[n][0] for n in upd}

    early_shapes = [_view(n, args[n]).shape for n in EARLY[:-2]] + [(S5_W, S5_W), (1,)]
    late_shapes = [args["ln_in_g"].shape, args["ln_in_b"].shape, (N_META, D_MODEL)]
    g_small = dict(zip(EARLY, _unpack(_sum_devices(big["early"], "sum_small_early"), early_shapes)))
    g_small.update(zip(LATE, _unpack(_sum_devices(late_all, "sum_small_late"), late_shapes)))
    loss = g_small["loss"].reshape(())

    shard_grads["meta_tokens"] = lax.dynamic_slice(g_small["meta_tokens"], (0, me * (D_MODEL // N_DEV)),
                                                   (N_META, D_MODEL // N_DEV))
    shard_grads["s5_w_glu"] = lax.dynamic_slice(g_small["s5_w_glu"], (me * (S5_W // N_DEV), 0),
                                                (S5_W // N_DEV, S5_W))[None]
    natives = SMALL + ["meta_tokens", "s5_w_glu"]
    res2 = _adamw_native([(shard_grads[n] if n in shard_grads else g_small[n], *(_view(n, args[p + n]) for p in ("", "m_", "v_")))
                          for n in natives], "adamw_small")
    for idx, n in enumerate(natives):
        upd[n] = [shard_grads[n] if n in shard_grads else g_small[n]] + list(res2[3 * idx:3 * idx + 3])

    grads, deltas, new_m, new_v = ([_view(n, upd[n][t]).reshape(args[n].shape) for n in names] for t in range(4))
    return (loss, grad_x[None], *grads, *deltas, *new_m, *new_v)
```

```python
import math

import jax
import jax.numpy as jnp
from jax import lax
from jax.experimental import pallas as pl
from jax.experimental.pallas import tpu as pltpu

F32 = jnp.float32
MM = jnp.bfloat16

D_MODEL = 1024
N_META = 16
CHUNK = 128
PAD = CHUNK - N_META
S5_W, S5_G, S5_H, S5_P = 256, 16, 16, 64
S5_N = S5_G * S5_P
RET_W, RET_H, HEAD = 768, 6, 128
D_FF = 4096
PROJ_W = S5_W + 4 * RET_W
N_DEV = 8
FF_BLK = D_FF // N_DEV
ROW_BLK = 384
MLP_ROWS = 1408
PROJ_ROWS = 704
ALPHA = 2.0 ** 0.25
LN_EPS = 1e-5
GN_EPS = 1e-5
ROPE_BASE = 10000.0
GELU_C = math.sqrt(2.0 / math.pi)
GELU_A = 0.044715
ADAM_LR, ADAM_B1, ADAM_B2, ADAM_EPS, ADAM_WD, ADAM_STEP = 0.001, 0.9, 0.999, 1e-08, 0.01, 10
VMEM_LIMIT = 60 * 1024 * 1024

_VMEM = pl.BlockSpec(memory_space=pltpu.VMEM)
_ANY = pl.BlockSpec(memory_space=pl.ANY)
_MESH = pl.DeviceIdType.MESH


def _params(sem=None):
    return pltpu.CompilerParams(dimension_semantics=sem, vmem_limit_bytes=VMEM_LIMIT)


def _dot(a, b):
    return jnp.dot(a.astype(MM), b.astype(MM), preferred_element_type=F32)


def _dot_nt(a, b):
    return lax.dot_general(a.astype(MM), b.astype(MM), (((1,), (1,)), ((), ())), preferred_element_type=F32)


def _dot_tn(a, b):
    return lax.dot_general(a.astype(MM), b.astype(MM), (((0,), (0,)), ((), ())), preferred_element_type=F32)


def _split3(a):
    hi = a.astype(jnp.bfloat16)
    r1 = a - hi.astype(F32)
    mid = r1.astype(jnp.bfloat16)
    lo = (r1 - mid.astype(F32)).astype(jnp.bfloat16)
    return hi, mid, lo


def _dot_sel_rhs(a, sel):
    s = sel.astype(jnp.bfloat16)
    return sum(jnp.dot(p, s, preferred_element_type=F32) for p in _split3(a))


def _dot_sel_lhs(sel, b):
    s = sel.astype(jnp.bfloat16)
    return sum(jnp.dot(s, p, preferred_element_type=F32) for p in _split3(b))


def _ln_fwd(r, eps):
    mu = jnp.mean(r, axis=-1, keepdims=True)
    xc = r - mu
    var = jnp.mean(xc * xc, axis=-1, keepdims=True)
    rstd = lax.rsqrt(var + eps)
    return xc * rstd, rstd


def _ln_bwd(dxhat, xhat, rstd):
    m1 = jnp.mean(dxhat, axis=-1, keepdims=True)
    m2 = jnp.mean(dxhat * xhat, axis=-1, keepdims=True)
    return rstd * (dxhat - m1 - xhat * m2)


def _colsum(a):
    return jnp.sum(a, axis=0, keepdims=True)


def _shift3(n_in, block=lambda i: i):
    return [pl.BlockSpec((CHUNK, D_MODEL), (lambda i, j=j: (jnp.clip(3 * block(i) - 1 + j, 0, n_in - 1), 0)))
            for j in range(3)]


def _ln_in(x2d, meta, jobs=(), gather_meta=False):
    seq = x2d.shape[0]
    tp = seq + CHUNK
    R = ROW_BLK
    nb = tp // R
    shard_w = D_MODEL // N_DEV

    def body(xa, xb, xc, meta_ref, xhat_ref, rstd_ref, raw_ref, *gathered):
        raw_ref[0:CHUNK, :] = xa[...]
        raw_ref[CHUNK:2 * CHUNK, :] = xb[...]
        raw_ref[2 * CHUNK:3 * CHUNK, :] = xc[...]

        @pl.when(pl.program_id(0) == nb - 1)
        def _():
            raw_ref[0:PAD, :] = jnp.zeros((PAD, D_MODEL), F32)
            if gather_meta:
                for d in range(N_DEV):
                    pltpu.sync_copy(gathered[0].at[d], raw_ref.at[PAD:CHUNK, d * shard_w:(d + 1) * shard_w])
            else:
                raw_ref[PAD:CHUNK, :] = meta_ref[...]

        xhat_ref[...], rstd_ref[...] = _ln_fwd(raw_ref[...], LN_EPS)

    row = lambda w: pl.BlockSpec((R, w), lambda i: (nb - 1 - i, 0))
    jobs = ([_job_gather(meta)] if gather_meta else []) + list(jobs)
    return _call(
        body, "ln_in", (nb,),
        _shift3(seq // CHUNK, lambda i: nb - 1 - i) + [pl.BlockSpec(meta.shape, lambda i: (0, 0))],
        [row(D_MODEL), row(1)], [jax.ShapeDtypeStruct((tp, D_MODEL), F32), jax.ShapeDtypeStruct((tp, 1), F32)],
        [pltpu.VMEM((R, D_MODEL), F32)], (x2d, x2d, x2d, meta), jobs, early=1 if gather_meta else 0)


def _in_proj(xhat0, ln_g, ln_b, w_int, cos2, sin2, jobs=()):
    tp = xhat0.shape[0]
    R = PROJ_ROWS if tp % PROJ_ROWS == 0 else ROW_BLK

    def body(xh_ref, g_ref, b_ref, w_ref, cos_ref, sin_ref, u_ref, q_ref, k_ref, v_ref, gate_ref):
        hb = (xh_ref[...] * g_ref[...] + b_ref[...]).astype(MM)
        valid = (pl.program_id(0) * R + lax.broadcasted_iota(jnp.int32, (R, 1), 0)) >= PAD

        def seg(lo, hi):
            return jnp.where(valid, _dot_nt(hb, w_ref[lo:hi, :]), 0.0)

        u_ref[...] = seg(0, S5_W)
        cos = cos_ref[...]
        sin = sin_ref[...]
        q = seg(S5_W, S5_W + RET_W)
        k = seg(S5_W + RET_W, S5_W + 2 * RET_W)
        for h in range(RET_H):
            sl = slice(h * HEAD, (h + 1) * HEAD)
            qh = q[:, sl]
            kh = k[:, sl]
            q_ref[:, sl] = (qh * cos + pltpu.roll(qh, HEAD // 2, 1) * sin).astype(q_ref.dtype)
            k_ref[:, sl] = ((kh * cos + pltpu.roll(kh, HEAD // 2, 1) * sin) * (HEAD ** -0.5)).astype(k_ref.dtype)
        v_ref[...] = seg(S5_W + 2 * RET_W, S5_W + 3 * RET_W).astype(v_ref.dtype)
        gate_ref[...] = seg(S5_W + 3 * RET_W, PROJ_W)

    def rows(w, dt):
        return pl.BlockSpec((R, w), lambda i: (i, 0)), jax.ShapeDtypeStruct((tp, w), dt)

    outs = [rows(S5_W, F32), rows(RET_W, MM), rows(RET_W, MM), rows(RET_W, MM), rows(RET_W, F32)]
    full = lambda s: pl.BlockSpec(s, lambda i: (0,) * len(s))
    return _call(
        body, "in_proj", (tp // R,),
        [pl.BlockSpec((R, D_MODEL), lambda i: (i, 0)), full((1, D_MODEL)), full((1, D_MODEL)), _VMEM,
         pl.BlockSpec((R, HEAD), lambda i: (i, 0)), pl.BlockSpec((R, HEAD), lambda i: (i, 0))],
        [o[0] for o in outs], [o[1] for o in outs], [], (xhat0, ln_g, ln_b, w_int, cos2, sin2), jobs)


def _s5_disc(lre, lim, ldt, bre_t, bim_t):
    dt = jnp.exp(ldt)
    mag = jnp.exp(lre * dt)
    ang = lim * dt
    lbr = mag * jnp.cos(ang)
    lbi = mag * jnp.sin(ang)
    den = lre * lre + lim * lim
    nr = lbr - 1.0
    qr = (nr * lre + lbi * lim) / den
    qi = (lbi * lre - nr * lim) / den
    return lbr, lbi, qr * bre_t - qi * bim_t, qr * bim_t + qi * bre_t


def _s5_tables(lbr, lbi, reverse):
    if reverse:
        lbi = -lbi
    pw = [(lbr, lbi)]
    for _ in range(7):
        r, i = pw[-1]
        pw.append((r * lbr - i * lbi, r * lbi + i * lbr))
    row = lax.broadcasted_iota(jnp.int32, (8, S5_N), 0)
    tabs = []
    for k in range(3):
        sh = 2 ** k
        mask = (row < 8 - sh) if reverse else (row >= sh)
        ar, ai = pw[sh - 1]
        tabs.append((jnp.where(mask, ar, 0.0), jnp.where(mask, ai, 0.0)))
    pr = jnp.zeros((8, S5_N), F32)
    pi = jnp.zeros((8, S5_N), F32)
    for i in range(8):
        ar, ai = pw[7 - i] if reverse else pw[i]
        pr = jnp.where(row == i, ar, pr)
        pi = jnp.where(row == i, ai, pi)
    tabs.append((pr, pi))
    return tabs


def _store_tables(tab_ref, tabs):
    for k, (r, i) in enumerate(tabs):
        tab_ref[2 * k] = r
        tab_ref[2 * k + 1] = i


def _bd_mask():
    r = lax.broadcasted_iota(jnp.int32, (S5_W, S5_N), 0)
    c = lax.broadcasted_iota(jnp.int32, (S5_W, S5_N), 1)
    return jnp.right_shift(r, 4) == jnp.right_shift(c, 6)


def _s5_block_diag(bbr_t, bbi_t, cre_w, cim_w):
    mask = _bd_mask()
    bd = lambda t: jnp.where(mask, t, 0.0)
    return (bd(jnp.tile(bbr_t, (S5_G, 1))), bd(jnp.tile(bbi_t, (S5_G, 1))),
            bd(jnp.tile(cre_w, (1, S5_N // HEAD))), bd(jnp.tile(cim_w, (1, S5_N // HEAD))))


def _scan8(xr, xi, tab_ref, lanes, reverse):
    for k in range(3):
        sh = (8 - 2 ** k) if reverse else 2 ** k
        sr = pltpu.roll(xr, sh, 0)
        si = pltpu.roll(xi, sh, 0)
        mr = tab_ref[2 * k, :, lanes]
        mi = tab_ref[2 * k + 1, :, lanes]
        xr, xi = xr + (mr * sr - mi * si), xi + (mr * si + mi * sr)
    return xr, xi


S5_LANES = 512


def _gelu(y):
    t = jnp.tanh(GELU_C * (y + GELU_A * y * y * y))
    return 0.5 * y * (1.0 + t), t


def _s5_fwd(u, lre, lim, ldt, bre_t, bim_t, cre_w, cim_w, d_row, w_glu, b_glu, jobs=()):
    tp = u.shape[0]
    R = ROW_BLK

    def body(u_ref, lre_ref, lim_ref, ldt_ref, bre_ref, bim_ref, cre_ref, cim_ref, d_ref, wg_ref, bg_ref,
             y_ref, xr_ref, xi_ref, bbd_r, bbd_i, cbd_r, cbd_i, tab_ref, car_r, car_i):
        @pl.when(pl.program_id(0) == 0)
        def _():
            lbr, lbi, bbr, bbi = _s5_disc(lre_ref[...], lim_ref[...], ldt_ref[...], bre_ref[...], bim_ref[...])
            br, bi, cr, ci = _s5_block_diag(bbr, bbi, cre_ref[...], cim_ref[...])
            bbd_r[...] = br.astype(MM)
            bbd_i[...] = bi.astype(MM)
            cbd_r[...] = cr.astype(MM)
            cbd_i[...] = ci.astype(MM)
            _store_tables(tab_ref, _s5_tables(lbr, lbi, False))
            car_r[...] = jnp.zeros_like(car_r)
            car_i[...] = jnp.zeros_like(car_i)

        u = u_ref[...]
        ub = u.astype(MM)
        xr_ref[...] = jnp.dot(ub, bbd_r[...], preferred_element_type=F32)
        xi_ref[...] = jnp.dot(ub, bbd_i[...], preferred_element_type=F32)
        for j in range(S5_N // S5_LANES):
            lanes = pl.ds(j * S5_LANES, S5_LANES)
            pr = tab_ref[6, :, lanes]
            pi = tab_ref[7, :, lanes]

            def step(g, carry):
                cr, ci = carry
                rows = pl.ds(pl.multiple_of(g * 8, 8), 8)
                xr, xi = _scan8(xr_ref[rows, lanes], xi_ref[rows, lanes], tab_ref, lanes, False)
                br = jnp.broadcast_to(cr[7:8, :], cr.shape)
                bi = jnp.broadcast_to(ci[7:8, :], ci.shape)
                xr = xr + (pr * br - pi * bi)
                xi = xi + (pr * bi + pi * br)
                xr_ref[rows, lanes] = xr
                xi_ref[rows, lanes] = xi
                return xr, xi

            cr, ci = lax.fori_loop(0, R // 8, step, (car_r[:, lanes], car_i[:, lanes]), unroll=2)
            car_r[:, lanes] = cr
            car_i[:, lanes] = ci
        y = _dot_nt(xr_ref[...], cbd_r[...]) - _dot_nt(xi_ref[...], cbd_i[...]) + d_ref[...] * u
        yg, _ = _gelu(y)
        z = _dot(yg, wg_ref[...]) + bg_ref[...]
        y_ref[...] = yg * jax.nn.sigmoid(z)

    full = lambda a: pl.BlockSpec(a.shape, lambda i: (0,) * a.ndim)
    small = [lre, lim, ldt, bre_t, bim_t, cre_w, cim_w, d_row, w_glu, b_glu]
    return _call(
        body, "s5_fwd", (tp // R,),
        [pl.BlockSpec((R, S5_W), lambda i: (i, 0))] + [full(a) for a in small],
        [pl.BlockSpec((R, S5_W), lambda i: (i, 0)), pl.BlockSpec((R, S5_N), lambda i: (i, 0)),
         pl.BlockSpec((R, S5_N), lambda i: (i, 0))],
        [jax.ShapeDtypeStruct((tp, S5_W), F32), jax.ShapeDtypeStruct((tp, S5_N), F32),
         jax.ShapeDtypeStruct((tp, S5_N), F32)],
        [pltpu.VMEM((S5_W, S5_N), MM)] * 4 + [pltpu.VMEM((8, 8, S5_N), F32), pltpu.VMEM((8, S5_N), F32),
                                              pltpu.VMEM((8, S5_N), F32)],
        (u, *small), jobs)


def _s5_bwd(dy_out, u, xr, xi, lre, lim, ldt, bre_t, bim_t, cre_w, cim_w, d_row, w_glu, b_glu, after, jobs=()):
    tp = u.shape[0]
    R = ROW_BLK
    nb = tp // R

    def body(dyo_ref, u_ref, xr_ref, xi_ref, xpr_ref, xpi_ref,
             lre_ref, lim_ref, ldt_ref, bre_ref, bim_ref, cre_ref, cim_ref, d_ref, wg_ref, bg_ref, after_ref,
             du_ref, dlre_ref, dlim_ref, dldt_ref, dbre_ref, dbim_ref, dcre_ref, dcim_ref, dd_ref, dwg_ref, dbg_ref,
             bbd_r, bbd_i, cbd_r, cbd_i, tab_ref, car_r, car_i, gr_ref, gi_ref, xer_ref, xei_ref,
             abr, abi, acr, aci, adr, adi):
        i = pl.program_id(0)

        @pl.when(i == 0)
        def _():
            lbr, lbi, bbr, bbi = _s5_disc(lre_ref[...], lim_ref[...], ldt_ref[...], bre_ref[...], bim_ref[...])
            br, bi, cr, ci = _s5_block_diag(bbr, bbi, cre_ref[...], cim_ref[...])
            bbd_r[...] = br.astype(MM)
            bbd_i[...] = bi.astype(MM)
            cbd_r[...] = cr.astype(MM)
            cbd_i[...] = ci.astype(MM)
            _store_tables(tab_ref, _s5_tables(lbr, lbi, True))
            for ref in (car_r, car_i, abr, abi, acr, aci, adr, adi, dd_ref, dwg_ref, dbg_ref):
                ref[...] = jnp.zeros_like(ref)

        u = u_ref[...]
        xrv = xr_ref[...]
        xiv = xi_ref[...]
        y = _dot_nt(xrv, cbd_r[...]) - _dot_nt(xiv, cbd_i[...]) + d_ref[...] * u
        yg, t = _gelu(y)
        z = _dot(yg, wg_ref[...]) + bg_ref[...]
        s = jax.nn.sigmoid(z)
        dout = dyo_ref[...]
        dz = dout * yg * s * (1.0 - s)
        dyg = dout * s + _dot_nt(dz, wg_ref[...])
        dwg_ref[...] += _dot_tn(yg, dz)
        dbg_ref[...] += _colsum(dz)
        dy = dyg * (0.5 * (1.0 + t) + 0.5 * y * (1.0 - t * t) * GELU_C * (1.0 + 3.0 * GELU_A * y * y))
        dd_ref[...] += _colsum(dy * u)
        acr[...] += _dot_tn(dy, xrv)
        aci[...] -= _dot_tn(dy, xiv)
        gr_ref[...] = _dot(dy, cbd_r[...])
        gi_ref[...] = -_dot(dy, cbd_i[...])
        has_prev = (i < nb - 1).astype(F32)
        xer_ref[0:8, :] = xpr_ref[...] * has_prev
        xei_ref[0:8, :] = xpi_ref[...] * has_prev
        xer_ref[8:R + 8, :] = xrv
        xei_ref[8:R + 8, :] = xiv
        row = lax.broadcasted_iota(jnp.int32, (8, S5_LANES), 0)
        for j in range(S5_N // S5_LANES):
            lanes = pl.ds(j * S5_LANES, S5_LANES)
            pr = tab_ref[6, :, lanes]
            pi = tab_ref[7, :, lanes]

            def step(n, carry):
                cr, ci, sar, sai = carry
                g = R // 8 - 1 - n
                r0 = pl.multiple_of(g * 8, 8)
                rows = pl.ds(r0, 8)
                gr, gi = _scan8(gr_ref[rows, lanes], gi_ref[rows, lanes], tab_ref, lanes, True)
                br = jnp.broadcast_to(cr[0:1, :], cr.shape)
                bi = jnp.broadcast_to(ci[0:1, :], ci.shape)
                gr = gr + (pr * br - pi * bi)
                gi = gi + (pr * bi + pi * br)
                gr_ref[rows, lanes] = gr
                gi_ref[rows, lanes] = gi
                last = row == 7
                xpr = pltpu.roll(jnp.where(last, xer_ref[rows, lanes], xer_ref[pl.ds(r0 + 8, 8), lanes]), 1, 0)
                xpi = pltpu.roll(jnp.where(last, xei_ref[rows, lanes], xei_ref[pl.ds(r0 + 8, 8), lanes]), 1, 0)
                return gr, gi, sar + (gr * xpr + gi * xpi), sai + (gi * xpr - gr * xpi)

            cr, ci, sar, sai = lax.fori_loop(
                0, R // 8, step, (car_r[:, lanes], car_i[:, lanes], adr[:, lanes], adi[:, lanes]), unroll=2)
            car_r[:, lanes] = cr
            car_i[:, lanes] = ci
            adr[:, lanes] = sar
            adi[:, lanes] = sai
        grv = gr_ref[...]
        giv = gi_ref[...]
        du_ref[...] = (dy * d_ref[...] + _dot_nt(grv, bbd_r[...]) + _dot_nt(giv, bbd_i[...])).astype(du_ref.dtype)
        abr[...] += _dot_tn(u, grv)
        abi[...] += _dot_tn(u, giv)

        @pl.when(i == nb - 1)
        def _():
            mask = _bd_mask()
            r16 = lax.broadcasted_iota(jnp.int32, (S5_H, S5_W), 1)
            h16 = lax.broadcasted_iota(jnp.int32, (S5_H, S5_W), 0)
            fold_b = jnp.bitwise_and(r16, S5_H - 1) == h16
            c64 = lax.broadcasted_iota(jnp.int32, (S5_N, S5_P), 0)
            p64 = lax.broadcasted_iota(jnp.int32, (S5_N, S5_P), 1)
            fold_c = jnp.bitwise_and(c64, S5_P - 1) == p64
            dbbr = _dot_sel_lhs(fold_b, jnp.where(mask, abr[...], 0.0))
            dbbi = _dot_sel_lhs(fold_b, jnp.where(mask, abi[...], 0.0))
            dcre_ref[...] = _dot_sel_rhs(jnp.where(mask, acr[...], 0.0), fold_c)
            dcim_ref[...] = _dot_sel_rhs(jnp.where(mask, aci[...], 0.0), fold_c)
            dlbr = _colsum(adr[...])
            dlbi = _colsum(adi[...])
            _, vjp = jax.vjp(_s5_disc, lre_ref[...], lim_ref[...], ldt_ref[...], bre_ref[...], bim_ref[...])
            dlre, dlim, dldt, dbre, dbim = vjp((dlbr, dlbi, dbbr, dbbi))
            dlre_ref[...] = dlre
            dlim_ref[...] = dlim
            dbre_ref[...] = dbre
            dbim_ref[...] = dbim
            gsel = jnp.right_shift(lax.broadcasted_iota(jnp.int32, (S5_N, HEAD), 0), 6) == \
                lax.broadcasted_iota(jnp.int32, (S5_N, HEAD), 1)
            dldt_ref[...] = _dot_sel_rhs(dldt, gsel)

    full = lambda a: pl.BlockSpec(a.shape, lambda i: (0,) * a.ndim)
    rev = lambda w: pl.BlockSpec((R, w), lambda i: (nb - 1 - i, 0))
    prev8 = pl.BlockSpec((8, S5_N), lambda i: (jnp.maximum((nb - 1 - i) * (R // 8) - 1, 0), 0))
    small = [lre, lim, ldt, bre_t, bim_t, cre_w, cim_w, d_row, w_glu, b_glu]
    outs = [((tp, S5_W), rev(S5_W))] + [
        (s, pl.BlockSpec(s, lambda i: (0, 0))) for s in
        [(1, S5_N), (1, S5_N), (1, HEAD), (S5_H, S5_N), (S5_H, S5_N), (S5_W, S5_P), (S5_W, S5_P),
         (1, S5_W), (S5_W, S5_W), (1, S5_W)]]
    return _call(
        body, "s5_bwd", (nb,),
        [rev(S5_W), rev(S5_W), rev(S5_N), rev(S5_N), prev8, prev8] + [full(a) for a in small + [after]],
        [o[1] for o in outs], [jax.ShapeDtypeStruct(o[0], MM if n == 0 else F32) for n, o in enumerate(outs)],
        [pltpu.VMEM((S5_W, S5_N), MM)] * 4 + [
            pltpu.VMEM((8, 8, S5_N), F32), pltpu.VMEM((8, S5_N), F32), pltpu.VMEM((8, S5_N), F32),
            pltpu.VMEM((R, S5_N), F32), pltpu.VMEM((R, S5_N), F32),
            pltpu.VMEM((R + 8, S5_N), F32), pltpu.VMEM((R + 8, S5_N), F32)] + [pltpu.VMEM((S5_W, S5_N), F32)] * 4 + [
            pltpu.VMEM((8, S5_N), F32), pltpu.VMEM((8, S5_N), F32)],
        (dy_out, u, xr, xi, xr, xi, *small, after), jobs)


RET_CHUNK = ROW_BLK
LOG_GAMMA = [math.log1p(-2.0 ** (-5 - h)) for h in range(RET_H)]
GAMMA_CHUNK = [math.exp(RET_CHUNK * lg) for lg in LOG_GAMMA]
_DECAY_SCRATCH = [pltpu.VMEM((RET_H, RET_CHUNK, RET_CHUNK), F32), pltpu.VMEM((RET_H, RET_CHUNK, HEAD), F32),
                  pltpu.VMEM((RET_H, RET_CHUNK, HEAD), F32)]


def _fill_decay(dm_ref, ze_ref, xi_ref):
    C = RET_CHUNK
    diff = (lax.broadcasted_iota(jnp.int32, (C, C), 0) - lax.broadcasted_iota(jnp.int32, (C, C), 1)).astype(F32)
    r = lax.broadcasted_iota(jnp.int32, (C, HEAD), 0).astype(F32)
    for h, lg in enumerate(LOG_GAMMA):
        dm_ref[h] = jnp.where(diff >= 0.0, jnp.exp(jnp.maximum(diff, 0.0) * lg), 0.0)
        ze_ref[h] = jnp.exp((C - 1.0 - r) * lg)
        xi_ref[h] = jnp.exp((r + 1.0) * lg)


def _ret_fwd(q, k, v, jobs=()):
    tp = q.shape[0]
    C = RET_CHUNK
    nc = tp // C

    def body(q_ref, k_ref, v_ref, o_ref, st_ref, s_ref, dm_ref, ze_ref, xi_ref):
        @pl.when(pl.program_id(0) == 0)
        def _():
            s_ref[...] = jnp.zeros_like(s_ref)
            _fill_decay(dm_ref, ze_ref, xi_ref)

        for h in range(RET_H):
            sl = slice(h * HEAD, (h + 1) * HEAD)
            qh, kh, vh = q_ref[:, sl], k_ref[:, sl], v_ref[:, sl]
            sh = s_ref[h]
            st_ref[0, sl, :] = sh
            scores = _dot_nt(qh, kh) * dm_ref[h]
            o_ref[:, sl] = _dot(scores, vh) + _dot(qh, sh) * xi_ref[h]
            s_ref[h] = GAMMA_CHUNK[h] * sh + _dot_tn(kh.astype(F32) * ze_ref[h], vh)

    blk = pl.BlockSpec((C, RET_W), lambda c: (c, 0))
    return _call(
        body, "ret_fwd", (nc,), [blk, blk, blk], [blk, pl.BlockSpec((1, RET_W, HEAD), lambda c: (c, 0, 0))],
        [jax.ShapeDtypeStruct((tp, RET_W), F32), jax.ShapeDtypeStruct((nc, RET_W, HEAD), F32)],
        [pltpu.VMEM((RET_H, HEAD, HEAD), F32)] + _DECAY_SCRATCH, (q, k, v), jobs)


def _ret_bwd(q, k, v, do, states, cos2, sin2, jobs=()):
    tp = q.shape[0]
    C = RET_CHUNK
    nc = tp // C

    def body(q_ref, k_ref, v_ref, do_ref, st_ref, cos_ref, sin_ref,
             dq_ref, dk_ref, dv_ref, ds_ref, dm_ref, ze_ref, xi_ref):
        @pl.when(pl.program_id(0) == 0)
        def _():
            ds_ref[...] = jnp.zeros_like(ds_ref)
            _fill_decay(dm_ref, ze_ref, xi_ref)

        cos = cos_ref[...]
        sin = sin_ref[...]
        for h in range(RET_H):
            sl = slice(h * HEAD, (h + 1) * HEAD)
            qh, kh, vh = q_ref[:, sl], k_ref[:, sl], v_ref[:, sl]
            dmh = dm_ref[h]
            sh = st_ref[0, sl, :]
            dsn = ds_ref[h]
            doh = do_ref[:, sl]
            dox = doh * xi_ref[h]
            a = _dot_nt(qh, kh) * dmh
            dqk = _dot_nt(doh, vh) * dmh
            kz = kh.astype(F32) * ze_ref[h]
            dv_ref[:, sl] = (_dot_tn(a, doh) + _dot(kz, dsn)).astype(dv_ref.dtype)
            dqr = _dot(dqk, kh) + _dot_nt(dox, sh)
            dkr = _dot_tn(dqk, qh) + ze_ref[h] * _dot_nt(vh, dsn)
            ds_ref[h] = GAMMA_CHUNK[h] * dsn + _dot_tn(qh, dox)
            dq_ref[:, sl] = (dqr * cos - pltpu.roll(dqr, HEAD // 2, 1) * sin).astype(dq_ref.dtype)
            dk_ref[:, sl] = ((dkr * cos - pltpu.roll(dkr, HEAD // 2, 1) * sin) * (HEAD ** -0.5)).astype(dk_ref.dtype)

    blk = pl.BlockSpec((C, RET_W), lambda c: (nc - 1 - c, 0))
    tab = pl.BlockSpec((C, HEAD), lambda c: (nc - 1 - c, 0))
    return _call(
        body, "ret_bwd", (nc,),
        [blk, blk, blk, blk, pl.BlockSpec((1, RET_W, HEAD), lambda c: (nc - 1 - c, 0, 0)), tab, tab],
        [blk, blk, blk], [jax.ShapeDtypeStruct((tp, RET_W), MM)] * 3,
        [pltpu.VMEM((RET_H, HEAD, HEAD), F32)] + _DECAY_SCRATCH, (q, k, v, do, states, cos2, sin2), jobs)


def _gn_gate(o, gate, gn_g, gn_b):
    xhat, rstd = _ln_fwd(o, GN_EPS)
    on = xhat * gn_g + gn_b
    s = jax.nn.sigmoid(gate)
    return gate * s * on, xhat, rstd, on, s


def _post_up(o, gate, ys5, xhat0, gn_g, gn_b, li_g, li_b, l1_g, l1_b, w_out, w_up, jobs=()):
    tp = o.shape[0]
    R = ROW_BLK

    def body(o_ref, g_ref, ys_ref, xh0_ref, gng, gnb, lig, lib, l1g, l1b, wo_ref, wu_ref,
             ycat_ref, xh1_ref, rstd1_ref, h1b_ref, pre_ref):
        ycat_ref[:, 0:S5_W] = ys_ref[...].astype(ycat_ref.dtype)
        for h in range(RET_H):
            sl = slice(h * HEAD, (h + 1) * HEAD)
            yret = _gn_gate(o_ref[:, sl], g_ref[:, sl], gng[:, sl], gnb[:, sl])[0]
            ycat_ref[:, S5_W + h * HEAD:S5_W + (h + 1) * HEAD] = yret.astype(ycat_ref.dtype)
        mixed = _dot(ycat_ref[...], wo_ref[...])
        h0 = xh0_ref[...] * lig[...] + lib[...]
        xh1, rstd1 = _ln_fwd(ALPHA * h0 + mixed, LN_EPS)
        xh1_ref[...] = xh1
        rstd1_ref[...] = rstd1
        h1b = (xh1 * l1g[...] + l1b[...]).astype(MM)
        h1b_ref[...] = h1b
        for d in range(N_DEV):
            pre_ref[:, d * FF_BLK:(d + 1) * FF_BLK] = jnp.maximum(_dot(h1b, wu_ref[d]), 0.0)

    row = lambda w: pl.BlockSpec((R, w), lambda i: (i, 0))
    full = lambda a: pl.BlockSpec(a.shape, lambda i: (0,) * a.ndim)
    vecs = [gn_g, gn_b, li_g, li_b, l1_g, l1_b]
    outs = [(row(D_MODEL), jax.ShapeDtypeStruct((tp, D_MODEL), MM)), (row(D_MODEL), jax.ShapeDtypeStruct((tp, D_MODEL), F32)),
            (row(1), jax.ShapeDtypeStruct((tp, 1), F32)), (row(D_MODEL), jax.ShapeDtypeStruct((tp, D_MODEL), MM)),
            (row(D_FF), jax.ShapeDtypeStruct((tp, D_FF), F32))]
    return _call(
        body, "post_up", (tp // R,),
        [row(RET_W), row(RET_W), row(S5_W), row(D_MODEL)] + [full(a) for a in vecs] + [_VMEM, _VMEM],
        [o[0] for o in outs], [o[1] for o in outs], [], (o, gate, ys5, xhat0, *vecs, w_out, w_up), jobs)


def _post_down(pre, xhat1, tgt, l1_g, l1_b, l2_g, l2_b, w_down):
    tp = pre.shape[0]
    seq = tgt.shape[0]
    R = ROW_BLK

    def body(pre_ref, xh1_ref, ta, tb, tc, l1g, l1b, l2g, l2b, wd_ref,
             dr2_ref, dffb_ref, loss_ref, dl2g_ref, dl2b_ref, tgt_ref):
        i = pl.program_id(0)

        @pl.when(i == 0)
        def _():
            for ref in (loss_ref, dl2g_ref, dl2b_ref):
                ref[...] = jnp.zeros_like(ref)

        tgt_ref[0:CHUNK, :] = ta[...]
        tgt_ref[CHUNK:2 * CHUNK, :] = tb[...]
        tgt_ref[2 * CHUNK:3 * CHUNK, :] = tc[...]
        ff = jnp.zeros((R, D_MODEL), F32)
        for d in range(N_DEV):
            pre = pre_ref[:, d * FF_BLK:(d + 1) * FF_BLK]
            ff = ff + _dot(pre * pre, wd_ref[d * FF_BLK:(d + 1) * FF_BLK, :])
        h1 = xh1_ref[...] * l1g[...] + l1b[...]
        xh2, rstd2 = _ln_fwd(ALPHA * h1 + ff, LN_EPS)
        h2 = xh2 * l2g[...] + l2b[...]
        valid = (i * R + lax.broadcasted_iota(jnp.int32, (R, 1), 0)) >= CHUNK
        err = jnp.where(valid, h2 - tgt_ref[...], 0.0)
        loss_ref[...] += 0.5 * jnp.sum(err * err) / D_MODEL
        dh2 = err * (1.0 / D_MODEL)
        dl2g_ref[...] += _colsum(dh2 * xh2)
        dl2b_ref[...] += _colsum(dh2)
        dr2 = _ln_bwd(dh2 * l2g[...], xh2, rstd2)
        dr2_ref[...] = dr2
        dffb_ref[...] = dr2.astype(MM)

    row = lambda w: pl.BlockSpec((R, w), lambda i: (i, 0))
    full = lambda a: pl.BlockSpec(a.shape, lambda i: (0,) * a.ndim)
    vecs = [l1_g, l1_b, l2_g, l2_b]
    acc = lambda s: (pl.BlockSpec(s, lambda i: (0, 0)), jax.ShapeDtypeStruct(s, F32))
    outs = [(row(D_MODEL), jax.ShapeDtypeStruct((tp, D_MODEL), F32)), (row(D_MODEL), jax.ShapeDtypeStruct((tp, D_MODEL), MM)),
            acc((8, HEAD)), acc((1, D_MODEL)), acc((1, D_MODEL))]
    return pl.pallas_call(
        body, name="post_down", grid=(tp // R,),
        in_specs=[row(D_FF), row(D_MODEL)] + _shift3(seq // CHUNK) + [full(a) for a in vecs] + [_VMEM],
        out_specs=[o[0] for o in outs], out_shape=[o[1] for o in outs],
        scratch_shapes=[pltpu.VMEM((R, D_MODEL), F32)],
        compiler_params=_params(("arbitrary",)),
    )(pre, xhat1, tgt, tgt, tgt, *vecs, w_down)


def _mlp_bwd(h1b, dffb, pre, w_up, w_down):
    tp = h1b.shape[0]
    R = MLP_ROWS if tp % MLP_ROWS == 0 else ROW_BLK
    nr = tp // R

    def body(h_ref, df_ref, pre_ref, wu_ref, wd_ref, gup_ref, gdn_ref, dh1_ref, aup, adn):
        d = pl.program_id(0)
        r = pl.program_id(1)

        @pl.when(r == 0)
        def _():
            aup[...] = jnp.zeros_like(aup)
            adn[...] = jnp.zeros_like(adn)

        h = h_ref[...]
        df = df_ref[...]
        wu = wu_ref[0]
        wd = wd_ref[0]
        pre = pre_ref[...]
        dpre = (_dot_nt(df, wd) * (2.0 * pre)).astype(MM)

        aup[...] += _dot_tn(h, dpre)
        adn[...] += _dot_tn(pre * pre, df)
        contrib = _dot_nt(dpre, wu)
        rows = pl.ds(pl.multiple_of(r * R, 64), R)

        @pl.when(d == 0)
        def _():
            dh1_ref[rows, :] = contrib

        @pl.when(d > 0)
        def _():
            dh1_ref[rows, :] += contrib

        @pl.when(r == nr - 1)
        def _():
            gup_ref[0] = aup[...].astype(gup_ref.dtype)
            gdn_ref[0] = adn[...].astype(gdn_ref.dtype)

    return pl.pallas_call(
        body, name="mlp_bwd", grid=(N_DEV, nr),
        in_specs=[pl.BlockSpec((R, D_MODEL), lambda d, r: (r, 0)), pl.BlockSpec((R, D_MODEL), lambda d, r: (r, 0)),
                  pl.BlockSpec((R, FF_BLK), lambda d, r: (r, d)),
                  pl.BlockSpec((1, D_MODEL, FF_BLK), lambda d, r: (d, 0, 0)),
                  pl.BlockSpec((1, FF_BLK, D_MODEL), lambda d, r: (d, 0, 0))],
        out_specs=[pl.BlockSpec((1, D_MODEL, FF_BLK), lambda d, r: (d, 0, 0)),
                   pl.BlockSpec((1, FF_BLK, D_MODEL), lambda d, r: (d, 0, 0)), _VMEM],
        out_shape=[jax.ShapeDtypeStruct((N_DEV, D_MODEL, FF_BLK), MM), jax.ShapeDtypeStruct((N_DEV, FF_BLK, D_MODEL), MM),
                   jax.ShapeDtypeStruct((tp, D_MODEL), F32)],
        scratch_shapes=[pltpu.VMEM((D_MODEL, FF_BLK), F32), pltpu.VMEM((FF_BLK, D_MODEL), F32)],
        compiler_params=_params(("arbitrary", "arbitrary")),
    )(h1b, dffb, pre, w_up, w_down.reshape(N_DEV, FF_BLK, D_MODEL))


def _post_bwd(dh1m, dr2, xhat1, rstd1, ycat, o, gate, gn_g, gn_b, l1_g, w_out, jobs=()):
    tp = o.shape[0]
    R = ROW_BLK
    nb = tp // R

    def body(dm_ref, dr2_ref, xh1_ref, rs1_ref, yc_ref, o_ref, g_ref, gng, gnb, l1g, wo_ref,
             do_ref, dg_ref, dys_ref, dh0_ref, gwo_ref, dl1g_ref, dl1b_ref, dgng_ref, dgnb_ref, awo):
        i = pl.program_id(0)

        @pl.when(i == 0)
        def _():
            for ref in (awo, dl1g_ref, dl1b_ref, dgng_ref, dgnb_ref):
                ref[...] = jnp.zeros_like(ref)

        dh1 = dm_ref[...] + ALPHA * dr2_ref[...]
        xh1 = xh1_ref[...]
        dl1g_ref[...] += _colsum(dh1 * xh1)
        dl1b_ref[...] += _colsum(dh1)
        dr1 = _ln_bwd(dh1 * l1g[...], xh1, rs1_ref[...])
        dh0_ref[...] = ALPHA * dr1
        dmix = dr1.astype(MM)
        awo[...] += _dot_tn(yc_ref[...], dmix)
        dyc = _dot_nt(dmix, wo_ref[...])
        dys_ref[...] = dyc[:, 0:S5_W]
        for h in range(RET_H):
            sl = slice(h * HEAD, (h + 1) * HEAD)
            gt = g_ref[:, sl]
            _, xhat, rstd, on, s = _gn_gate(o_ref[:, sl], gt, gng[:, sl], gnb[:, sl])
            dyr = dyc[:, S5_W + h * HEAD:S5_W + (h + 1) * HEAD]
            dg_ref[:, sl] = (dyr * on * (s * (1.0 + gt * (1.0 - s)))).astype(dg_ref.dtype)
            don = dyr * gt * s
            dgng_ref[:, sl] += _colsum(don * xhat)
            dgnb_ref[:, sl] += _colsum(don)
            do_ref[:, sl] = _ln_bwd(don * gng[:, sl], xhat, rstd)

        @pl.when(i == nb - 1)
        def _():
            gwo_ref[...] = awo[...].astype(gwo_ref.dtype)

    row = lambda w: pl.BlockSpec((R, w), lambda i: (i, 0))
    full = lambda a: pl.BlockSpec(a.shape, lambda i: (0,) * a.ndim)
    acc = lambda s, dt=F32: (pl.BlockSpec(s, lambda i: (0, 0)), jax.ShapeDtypeStruct(s, dt))
    outs = [(row(RET_W), jax.ShapeDtypeStruct((tp, RET_W), F32)), (row(RET_W), jax.ShapeDtypeStruct((tp, RET_W), MM)),
            (row(S5_W), jax.ShapeDtypeStruct((tp, S5_W), F32)), (row(D_MODEL), jax.ShapeDtypeStruct((tp, D_MODEL), F32)),
            acc((D_MODEL, D_MODEL), MM), acc((1, D_MODEL)), acc((1, D_MODEL)), acc((1, RET_W)), acc((1, RET_W))]
    return _call(
        body, "post_bwd", (nb,),
        [row(D_MODEL), row(D_MODEL), row(D_MODEL), row(1), row(D_MODEL), row(RET_W), row(RET_W),
         full(gn_g), full(gn_b), full(l1_g), _VMEM],
        [o[0] for o in outs], [o[1] for o in outs],
        [pltpu.VMEM((D_MODEL, D_MODEL), F32)],
        (dh1m, dr2, xhat1, rstd1, ycat, o, gate, gn_g, gn_b, l1_g, w_out), jobs)


_PROJ_SEGS = [(0, S5_W)] + [(S5_W + n * RET_W, S5_W + (n + 1) * RET_W) for n in range(4)]


def _in_w_grad(du, dq, dk, dv, dg, xhat0, li_g, li_b):
    tp = du.shape[0]
    R = PROJ_ROWS if tp % PROJ_ROWS == 0 else ROW_BLK
    nb = tp // R

    def body(du_ref, dq_ref, dk_ref, dv_ref, dg_ref, xh_ref, lig, lib, gw_ref, aw):
        i = pl.program_id(0)

        @pl.when(i == 0)
        def _():
            aw[...] = jnp.zeros_like(aw)

        valid = (i * R + lax.broadcasted_iota(jnp.int32, (R, 1), 0)) >= PAD
        hb = (xh_ref[...] * lig[...] + lib[...]).astype(MM)
        for (lo, hi), ref in zip(_PROJ_SEGS, (du_ref, dq_ref, dk_ref, dv_ref, dg_ref)):
            aw[lo:hi, :] += _dot_tn(jnp.where(valid, ref[...], 0.0).astype(MM), hb)

        @pl.when(i == nb - 1)
        def _():
            gw_ref[...] = aw[...].astype(gw_ref.dtype)

    row = lambda w: pl.BlockSpec((R, w), lambda i: (i, 0))
    full = lambda a: pl.BlockSpec(a.shape, lambda i: (0,) * a.ndim)
    (gw,), _ = _call(
        body, "in_w_grad", (nb,),
        [row(S5_W), row(RET_W), row(RET_W), row(RET_W), row(RET_W), row(D_MODEL), full(li_g), full(li_b)],
        [pl.BlockSpec((PROJ_W, D_MODEL), lambda i: (0, 0))], [jax.ShapeDtypeStruct((PROJ_W, D_MODEL), MM)],
        [pltpu.VMEM((PROJ_W, D_MODEL), F32)], (du, dq, dk, dv, dg, xhat0, li_g, li_b))
    return gw


def _in_bwd(du, dq, dk, dv, dg, dh0r, xhat0, rstd0, li_g, w_int, after):
    tp = du.shape[0]
    R = PROJ_ROWS if tp % PROJ_ROWS == 0 else ROW_BLK
    nb = tp // R
    segs = _PROJ_SEGS

    def body(du_ref, dq_ref, dk_ref, dv_ref, dg_ref, dh0r_ref, xh_ref, rs_ref, lig, w_ref, after_ref,
             gx_ref, dmeta_ref, dlg_ref, dlb_ref, stage, out_sems):
        i = pl.program_id(0)
        slot = i % 2

        def to_gx(step_slot, first):
            if first:
                return pltpu.make_async_copy(stage.at[0, CHUNK:R, :], gx_ref.at[0:R - CHUNK, :], out_sems.at[0])
            return pltpu.make_async_copy(stage.at[step_slot], gx_ref.at[pl.ds(i * R - CHUNK, R), :], out_sems.at[step_slot])

        @pl.when(i == 0)
        def _():
            for ref in (dlg_ref, dlb_ref):
                ref[...] = jnp.zeros_like(ref)

        @pl.when(i >= 3)
        def _():
            to_gx(slot, False).wait()

        valid = (i * R + lax.broadcasted_iota(jnp.int32, (R, 1), 0)) >= PAD
        xh = xh_ref[...]
        dh0 = dh0r_ref[...]
        for (lo, hi), ref in zip(segs, (du_ref, dq_ref, dk_ref, dv_ref, dg_ref)):
            dh0 = dh0 + _dot(jnp.where(valid, ref[...], 0.0).astype(MM), w_ref[lo:hi, :])
        dlg_ref[...] += _colsum(dh0 * xh)
        dlb_ref[...] += _colsum(dh0)
        draw = _ln_bwd(dh0 * lig[...], xh, rs_ref[...])
        stage[slot] = draw

        @pl.when(i == 0)
        def _():
            dmeta_ref[...] = draw[PAD:CHUNK, :]
            first = to_gx(0, True)
            first.start()
            first.wait()

        @pl.when(i > 0)
        def _():
            to_gx(slot, False).start()

        @pl.when(i == nb - 1)
        def _():
            for back in (1, 0):
                if nb - 1 - back >= 1:
                    to_gx((nb - 1 - back) % 2, False).wait()

    row = lambda w: pl.BlockSpec((R, w), lambda i: (i, 0))
    full = lambda a: pl.BlockSpec(a.shape, lambda i: (0,) * a.ndim)
    acc = lambda s, dt=F32: (pl.BlockSpec(s, lambda i: (0, 0)), jax.ShapeDtypeStruct(s, dt))
    outs = [(_ANY, jax.ShapeDtypeStruct((tp - CHUNK, D_MODEL), F32)), acc((N_META, D_MODEL)),
            acc((1, D_MODEL)), acc((1, D_MODEL))]
    return _call(
        body, "in_bwd", (nb,),
        [row(S5_W), row(RET_W), row(RET_W), row(RET_W), row(RET_W), row(D_MODEL), row(D_MODEL), row(1),
         full(li_g), _VMEM, full(after)],
        [o[0] for o in outs], [o[1] for o in outs],
        [pltpu.VMEM((2, R, D_MODEL), F32), pltpu.SemaphoreType.DMA((2,))],
        (du, dq, dk, dv, dg, dh0r, xhat0, rstd0, li_g, w_int, after))[0]


def _place():
    return lax.axis_index("x"), lax.axis_index("y"), lax.axis_index("c")


def _dma_sems(n):
    return pltpu.SemaphoreType.DMA((n,))


def _job_gather(shard):
    def parts(ins, outs, sems):
        (src,), (out,), (send_sems, recv_sems, local_sem) = ins, outs, sems
        x, y, c = _place()
        north = c == 1
        me, sib = (x, y, c), (x, y, 1 - c)
        xn, yn, dg = (1 - x, y, c), (x, 1 - y, c), (1 - x, 1 - y, c)
        relay_from = (jnp.where(north, 1 - x, x), jnp.where(north, y, 1 - y), c)
        relay_to = (jnp.where(north, x, 1 - x), jnp.where(north, 1 - y, y), c)

        def slot(dev):
            return out.at[4 * dev[0] + 2 * dev[1] + dev[2]]

        def copy(k, block, to, from_input=False):
            return pltpu.make_async_remote_copy(
                src_ref=src if from_input else slot(block), dst_ref=slot(block),
                send_sem=send_sems.at[k], recv_sem=recv_sems.at[k], device_id=to, device_id_type=_MESH)

        mine = lambda: pltpu.make_async_copy(src, slot(me), local_sem.at[0])
        first = lambda: [copy(0, me, sib, True), copy(1, me, xn, True), copy(2, me, yn, True)]
        relayed = lambda: [copy(3, relay_from, relay_to), copy(4, xn, sib), copy(5, yn, sib)]
        return me, sib, xn, yn, dg, copy, mine, first, relayed

    def start(ins, outs, sems):
        mine, first = parts(ins, outs, sems)[6:8]
        mine().start()
        for cp in first():
            cp.start()

    def relay(ins, outs, sems):
        me, sib, xn, yn, dg, copy, mine, first, relayed = parts(ins, outs, sems)
        copy(1, xn, me).wait_recv()
        copy(2, yn, me).wait_recv()
        for cp in relayed():
            cp.start()

    def finish(ins, outs, sems):
        me, sib, xn, yn, dg, copy, mine, first, relayed = parts(ins, outs, sems)
        other = 1 - me[2]
        copy(3, dg, me).wait_recv()
        last = copy(6, dg, sib)
        last.start()
        copy(0, sib, me).wait_recv()
        for k, chip in ((4, xn), (5, yn), (6, dg)):
            copy(k, (chip[0], chip[1], other), me).wait_recv()
        for cp in first() + relayed() + [last]:
            cp.wait_send()
        mine().wait()

    return dict(ins=[shard], outs=[jax.ShapeDtypeStruct((N_DEV,) + shard.shape, shard.dtype)],
                sems=[_dma_sems(7), _dma_sems(7), _dma_sems(1)], start=start, middle=relay, finish=finish)


def _job_gather_direct(shard):
    def copies(ins, outs, sems):
        (src,), (out,), (send_sems, recv_sems, local_sem) = ins, outs, sems
        x, y, c = _place()
        flip = lambda a, bit: 1 - a if bit else a
        slot = out.at[4 * x + 2 * y + c]
        return [pltpu.make_async_copy(src, slot, local_sem.at[0])] + [pltpu.make_async_remote_copy(
            src_ref=src, dst_ref=slot, send_sem=send_sems.at[k - 1], recv_sem=recv_sems.at[k - 1],
            device_id=(flip(x, k & 4), flip(y, k & 2), flip(c, k & 1)), device_id_type=_MESH) for k in range(1, N_DEV)]

    def start(ins, outs, sems):
        for cp in copies(ins, outs, sems):
            cp.start()

    def finish(ins, outs, sems):
        for cp in copies(ins, outs, sems):
            cp.wait()

    return dict(ins=[shard], outs=[jax.ShapeDtypeStruct((N_DEV,) + shard.shape, shard.dtype)],
                sems=[_dma_sems(N_DEV - 1), _dma_sems(N_DEV - 1), _dma_sems(1)], start=start, finish=finish)


def _job_pair(g):
    def copies(ins, outs, sems):
        x, y, c = _place()
        return [pltpu.make_async_remote_copy(
            src_ref=ins[0].at[2 * j + (1 - c)], dst_ref=outs[0].at[j], send_sem=sems[0].at[j], recv_sem=sems[1].at[j],
            device_id=(x, y, 1 - c), device_id_type=_MESH) for j in range(4)]

    def start(ins, outs, sems):
        for cp in copies(ins, outs, sems):
            cp.start()

    def finish(ins, outs, sems):
        for cp in copies(ins, outs, sems):
            cp.wait()

    return dict(ins=[g], outs=[jax.ShapeDtypeStruct((4,) + g.shape[1:], g.dtype)], sems=[_dma_sems(4), _dma_sems(4)],
                start=start, finish=finish)


def _job_chips(p):
    def copies(ins, outs, sems):
        x, y, c = _place()
        chips = [(1 - x, y), (x, 1 - y), (1 - x, 1 - y)]
        return [pltpu.make_async_remote_copy(
            src_ref=ins[0].at[2 * chip[0] + chip[1]], dst_ref=outs[0].at[k], send_sem=sems[0].at[k],
            recv_sem=sems[1].at[k], device_id=(*chip, c), device_id_type=_MESH) for k, chip in enumerate(chips)]

    def start(ins, outs, sems):
        for cp in copies(ins, outs, sems):
            cp.start()

    def finish(ins, outs, sems):
        for cp in copies(ins, outs, sems):
            cp.wait()

    return dict(ins=[p], outs=[jax.ShapeDtypeStruct((3,) + p.shape[1:], p.dtype)], sems=[_dma_sems(3), _dma_sems(3)],
                start=start, finish=finish)


_HBM = pl.BlockSpec(memory_space=pltpu.HBM)
_SEM = pl.BlockSpec(memory_space=pltpu.SEMAPHORE)
_ORDERED = pltpu.CompilerParams(has_side_effects=pltpu.SideEffectType.DATAFLOW_SIDE_EFFECTING)


def _chip_copies(p_ref, land_ref, sems):
    x, y, c = _place()
    chips = [(1 - x, y), (x, 1 - y), (1 - x, 1 - y)]
    return [pltpu.make_async_remote_copy(
        src_ref=p_ref.at[2 * chip[0] + chip[1]], dst_ref=land_ref.at[k], send_sem=sems[k], recv_sem=sems[3 + k],
        device_id=(*chip, c), device_id_type=_MESH) for k, chip in enumerate(chips)]


def _chips_start(ps, name):
    n = len(ps)

    def body(*refs):
        sems = refs[2 * n:8 * n]
        for a in range(n):
            for cp in _chip_copies(refs[a], refs[n + a], sems[6 * a:6 * a + 6]):
                cp.start()
        refs[-1][...] = jnp.zeros_like(refs[-1])

    lands = [(3,) + p.shape[1:] for p in ps]
    hbm = lambda arr: pltpu.with_memory_space_constraint(arr, pltpu.HBM)
    outs = pl.pallas_call(
        body, name=name,
        out_shape=(*[pltpu.SemaphoreType.DMA(())] * (6 * n), *[pltpu.HBM(p.shape, p.dtype) for p in ps],
                   *[pltpu.HBM(s, p.dtype) for s, p in zip(lands, ps)], jax.ShapeDtypeStruct((8, LANE), F32)),
        in_specs=[_HBM] * (2 * n), out_specs=(*[_SEM] * (6 * n), *[_HBM] * (2 * n), _VMEM),
        input_output_aliases={a: 6 * n + a for a in range(2 * n)}, compiler_params=_ORDERED,
    )(*[hbm(p) for p in ps], *[hbm(lax.empty(s, p.dtype)) for s, p in zip(lands, ps)])
    return (list(outs[:6 * n]), list(outs[6 * n:7 * n]), list(outs[7 * n:8 * n])), outs[8 * n]


def _chips_wait(started, after, name):
    sems, thrus, lands = started
    n = len(thrus)

    def body(*refs):
        for a in range(n):
            for cp in _chip_copies(refs[a], refs[n + a], refs[2 * n + 6 * a:2 * n + 6 * a + 6]):
                cp.wait_send()
                cp.wait_recv()

    outs = pl.pallas_call(
        body, name=name, out_shape=[pltpu.HBM(t.shape, t.dtype) for t in thrus + lands],
        in_specs=(*[_HBM] * (2 * n), *[_SEM] * (6 * n), _ANY), out_specs=[_HBM] * (2 * n),
        input_output_aliases={a: a for a in range(2 * n)}, compiler_params=_ORDERED,
    )(*thrus, *lands, *sems, after)
    return list(outs[:n]), list(outs[n:])


def _split_job_refs(jobs, ins, outs, sems):
    res, a, b, c = [], 0, 0, 0
    for job in jobs:
        na, nb, nc = len(job["ins"]), len(job["outs"]), len(job["sems"])
        res.append((ins[a:a + na], outs[b:b + nb], sems[c:c + nc]))
        a, b, c = a + na, b + nb, c + nc
    return res


def _call(body, name, grid, in_specs, out_specs, out_shape, scratch, args, jobs=(), prefetch=None, early=0):
    jobs = list(jobs)
    n_in, n_out, n_scr = len(in_specs), len(out_specs), len(scratch)
    j_in = [a for job in jobs for a in job["ins"]]
    j_out = [o for job in jobs for o in job["outs"]]
    j_scr = [s for job in jobs for s in job["sems"]]
    nsteps = grid[0]
    n_pre = 0 if prefetch is None else 1

    def wrapped(*refs):
        pre, refs = refs[:n_pre], refs[n_pre:]
        ins, jins = refs[:n_in], refs[n_in:n_in + len(j_in)]
        refs = refs[n_in + len(j_in):]
        outs, jouts = refs[:n_out], refs[n_out:n_out + len(j_out)]
        refs = refs[n_out + len(j_out):]
        scr, jscr = refs[:n_scr], refs[n_scr:]
        per_job = _split_job_refs(jobs, jins, jouts, jscr)

        def middle():
            for job, r in zip(jobs, per_job):
                if "middle" in job:
                    job["middle"](*r)

        @pl.when(pl.program_id(0) == 0)
        def _():
            for job, r in zip(jobs, per_job):
                job["start"](*r)

        if nsteps >= 3:
            pl.when(pl.program_id(0) == (2 * nsteps) // 3)(middle)

        if early:
            @pl.when(pl.program_id(0) == nsteps - 1)
            def _():
                for job, r in zip(jobs[:early], per_job[:early]):
                    job["finish"](*r)

        body(*pre, *ins, *outs, *scr, *[o for r in per_job[:early] for o in r[1]])

        @pl.when(pl.program_id(0) == nsteps - 1)
        def _():
            if nsteps < 3:
                middle()
            for job, r in zip(jobs[early:], per_job[early:]):
                job["finish"](*r)

    specs = dict(in_specs=list(in_specs) + [_ANY] * len(j_in), out_specs=list(out_specs) + [_ANY] * len(j_out),
                 scratch_shapes=list(scratch) + j_scr)
    if n_pre:
        specs = dict(grid_spec=pltpu.PrefetchScalarGridSpec(num_scalar_prefetch=1, grid=grid, **specs))
    else:
        specs["grid"] = grid
    res = pl.pallas_call(
        wrapped if jobs else body, name=name, out_shape=list(out_shape) + j_out,
        compiler_params=_params(("arbitrary",) * len(grid)), **specs,
    )(*([prefetch] if n_pre else []), *args, *j_in)
    return list(res[:n_out]), list(res[n_out:])


def _exchange(jobs, name):
    j_in = [a for job in jobs for a in job["ins"]]
    j_out = [o for job in jobs for o in job["outs"]]
    j_scr = [s for job in jobs for s in job["sems"]]

    def body(*refs):
        per_job = _split_job_refs(jobs, refs[:len(j_in)], refs[len(j_in):len(j_in) + len(j_out)],
                                  refs[len(j_in) + len(j_out):])
        for phase in ("start", "middle", "finish"):
            for job, r in zip(jobs, per_job):
                if phase in job:
                    job[phase](*r)

    return pl.pallas_call(body, name=name, out_shape=j_out, in_specs=[_ANY] * len(j_in), out_specs=[_ANY] * len(j_out),
                          scratch_shapes=j_scr)(*j_in)


def _pair_sum(gs, r1s, c_arr, name):
    n = len(gs)

    def body(c_ref, *refs):
        for a in range(n):
            refs[2 * n + a][...] = (refs[a][...].astype(F32) + refs[n + a][...].astype(F32)).astype(refs[2 * n + a].dtype)

    def blk(g, own):
        s = g.shape[1:]
        if own:
            return pl.BlockSpec((1,) + s, lambda j, c_ref: (2 * j + c_ref[0],) + (0,) * len(s))
        return pl.BlockSpec((1,) + s, lambda j, c_ref: (j,) + (0,) * len(s))

    return pl.pallas_call(
        body, name=name,
        grid_spec=pltpu.PrefetchScalarGridSpec(
            num_scalar_prefetch=1, grid=(4,),
            in_specs=[blk(g, True) for g in gs] + [blk(g, False) for g in gs],
            out_specs=[blk(g, False) for g in gs]),
        out_shape=[jax.ShapeDtypeStruct((4,) + g.shape[1:], g.dtype) for g in gs],
        compiler_params=_params(("arbitrary",)),
    )(c_arr, *gs, *r1s)


def _adamw_math(w, g, m, v):
    m = ADAM_B1 * m + (1.0 - ADAM_B1) * g
    v = ADAM_B2 * v + (1.0 - ADAM_B2) * (g * g)
    m_hat = m / (1.0 - ADAM_B1 ** ADAM_STEP)
    v_hat = v / (1.0 - ADAM_B2 ** ADAM_STEP)
    return -ADAM_LR * (m_hat / (jnp.sqrt(v_hat) + ADAM_EPS) + ADAM_WD * w), m, v


def _view(name, a):
    return jnp.swapaxes(a, -1, -2) if name in ("w_in", "s5_b_re", "s5_b_im") else a


def _adamw_shards(items, name, steps, chip, jobs=()):
    n = len(items)

    def body(chip_ref, *refs):
        for a in range(n):
            p_ref, r_ref, w_ref, m_ref, v_ref = refs[5 * a:5 * a + 5]
            g = ((p_ref[0].astype(F32) + r_ref[0].astype(F32)) + r_ref[1].astype(F32)) + r_ref[2].astype(F32)
            outs = refs[5 * n + 4 * a:5 * n + 4 * a + 4]
            outs[0][...] = g
            outs[1][...], outs[2][...], outs[3][...] = _adamw_math(w_ref[...], g, m_ref[...], v_ref[...])

    in_specs, out_specs, out_shape, flat = [], [], [], []
    for p, r, w, m, v in items:
        rows, cols = w.shape
        rb = rows // steps
        in_specs += [pl.BlockSpec((1, rb, cols), lambda i, c: (c[0], i, 0)), pl.BlockSpec((3, rb, cols), lambda i, c: (0, i, 0))]
        wblk = pl.BlockSpec((rb, cols), lambda i, c: (i, 0))
        in_specs += [wblk] * 3
        out_specs += [wblk] * 4
        out_shape += [jax.ShapeDtypeStruct(w.shape, F32)] * 4
        flat += [p, r, w, m, v]
    return _call(body, name, (steps,), in_specs, out_specs, out_shape, [], flat, jobs, prefetch=chip)


def _sum_devices(gathered, name):
    def body(gs_ref, g_ref):
        g = gs_ref[0]
        for s in range(1, N_DEV):
            g = g + gs_ref[s]
        g_ref[...] = g

    return pl.pallas_call(body, name=name, out_shape=jax.ShapeDtypeStruct(gathered.shape[1:], F32),
                          in_specs=[_VMEM], out_specs=_VMEM, compiler_params=_params())(gathered)


def _adamw_native(items, name):
    n = len(items)

    def body(*refs):
        for a in range(n):
            g, w, m, v = (refs[4 * a + t][...] for t in range(4))
            refs[4 * n + 3 * a][...], refs[4 * n + 3 * a + 1][...], refs[4 * n + 3 * a + 2][...] = _adamw_math(w, g, m, v)

    return pl.pallas_call(
        body, name=name, out_shape=[jax.ShapeDtypeStruct(it[1].shape, F32) for it in items for _ in range(3)],
        in_specs=[_VMEM] * (4 * n), out_specs=[_VMEM] * (3 * n), compiler_params=_params(),
    )(*[t for it in items for t in it])


SMALL = ["ln_in_g", "ln_in_b", "s5_lambda_re", "s5_lambda_im", "s5_log_dt", "s5_b_re", "s5_b_im", "s5_c_re", "s5_c_im",
         "s5_d", "s5_b_glu", "ret_gn_g", "ret_gn_b", "ln1_g", "ln1_b", "ln2_g", "ln2_b"]
LATE = ["ln_in_g", "ln_in_b", "meta_tokens"]
EARLY = [n for n in SMALL if n not in LATE] + ["s5_w_glu", "loss"]
LANE = 128


def _pack(arrs):
    parts = []
    for a in arrs:
        f = a.reshape(-1)
        parts.append(jnp.pad(f, (0, (-f.shape[0]) % LANE)))
    flat = jnp.concatenate(parts)
    rows = -(-flat.shape[0] // LANE)
    flat = jnp.pad(flat, (0, (-rows % 8) * LANE + rows * LANE - flat.shape[0]))
    return flat.reshape(-1, LANE)


def _unpack(packed, shapes):
    flat = packed.reshape(-1)
    out, off = [], 0
    for s in shapes:
        n = math.prod(s)
        out.append(flat[off:off + n].reshape(s))
        off += n + (-n) % LANE
    return out


def _rope_tables(tp):
    inv_freq = 1.0 / (ROPE_BASE ** (jnp.arange(0, HEAD, 2, dtype=F32) / HEAD))
    blk = (jnp.arange(tp // ROW_BLK, dtype=F32) * ROW_BLK)[:, None, None] * inv_freq
    off = (jnp.arange(ROW_BLK, dtype=F32) - float(PAD))[None, :, None] * inv_freq
    cos = (jnp.cos(blk) * jnp.cos(off) - jnp.sin(blk) * jnp.sin(off)).reshape(tp, HEAD // 2)
    sin = (jnp.sin(blk) * jnp.cos(off) + jnp.cos(blk) * jnp.sin(off)).reshape(tp, HEAD // 2)
    return jnp.concatenate([cos, cos], axis=1), jnp.concatenate([-sin, sin], axis=1)


def _local_step(x2d, tgt, meta, w_int, w_out, w_up, w_down, w_glu, sp, distributed):
    tp = x2d.shape[0] + CHUNK
    row = lambda a: a.reshape(1, -1)
    cos2, sin2 = _rope_tables(tp)
    li_g, li_b = row(sp["ln_in_g"]), row(sp["ln_in_b"])
    l1_g, l1_b, l2_g, l2_b = row(sp["ln1_g"]), row(sp["ln1_b"]), row(sp["ln2_g"]), row(sp["ln2_b"])
    gn_g, gn_b = row(sp["ret_gn_g"]), row(sp["ret_gn_b"])
    lre, lim = row(sp["s5_lambda_re"]), row(sp["s5_lambda_im"])
    ldt = row(jnp.repeat(sp["s5_log_dt"].reshape(-1), S5_P))
    to_t = lambda b: b.reshape(S5_G, S5_P, S5_H).transpose(2, 0, 1).reshape(S5_H, S5_N)
    bre_t, bim_t = to_t(sp["s5_b_re"]), to_t(sp["s5_b_im"])
    to_w = lambda c: jnp.tile(c.reshape(S5_W, S5_P), (1, 2))
    cre_w, cim_w = to_w(sp["s5_c_re"]), to_w(sp["s5_c_im"])

    jobs = (lambda *j: list(j)) if distributed else (lambda *j: [])
    c_arr = jnp.reshape(lax.axis_index("c"), (1,)).astype(jnp.int32) if distributed else None
    (xhat0, rstd0), bg = _ln_in(x2d, meta, jobs(*([_job_gather(w_int), _job_gather(w_glu)] if distributed else [])),
                                gather_meta=distributed)
    if distributed:
        w_int, w_glu = bg[1].reshape(PROJ_W, D_MODEL), bg[2].reshape(S5_W, S5_W)
    s5_small = (lre, lim, ldt, bre_t, bim_t, cre_w, cim_w, row(sp["s5_d"]), w_glu, row(sp["s5_b_glu"]))
    (u, q, k, v, gate), bg = _in_proj(xhat0, li_g, li_b, w_int, cos2, sin2,
                                      jobs(_job_gather(w_up) if distributed else None))
    if distributed:
        w_up = bg[0]
    (ys5, xr, xi), bg = _s5_fwd(u, *s5_small, jobs=jobs(_job_gather(w_down) if distributed else None))
    if distributed:
        w_down = bg[0].reshape(D_FF, D_MODEL)
    (o, states), bg = _ret_fwd(q, k, v, jobs(_job_gather(w_out) if distributed else None))
    if distributed:
        w_out = bg[0].reshape(D_MODEL, D_MODEL)
    (ycat, xhat1, rstd1, h1b, pre), _ = _post_up(o, gate, ys5, xhat0, gn_g, gn_b, li_g, li_b, l1_g, l1_b, w_out, w_up)
    dr2, dffb, loss8, dl2g, dl2b = _post_down(pre, xhat1, tgt, l1_g, l1_b, l2_g, l2_b, w_down)
    g_up, g_down, dh1m = _mlp_bwd(h1b, dffb, pre, w_up, w_down)
    (do, dgate, dys5, dh0r, g_out, dl1g, dl1b, dgng, dgnb), bg = _post_bwd(
        dh1m, dr2, xhat1, rstd1, ycat, o, gate, gn_g, gn_b, l1_g, w_out,
        jobs(*([_job_pair(g_up), _job_pair(g_down)] if distributed else [])))
    g_out = g_out.reshape(N_DEV, D_MODEL // N_DEV, D_MODEL)
    after = jnp.zeros((8, LANE), F32)
    if distributed:
        p_up, p_down = _pair_sum([g_up, g_down], bg, c_arr, "pair_sum_mlp")
        started_mlp, after = _chips_start([p_up, p_down], "chips_mlp_start")
    (du, dlre, dlim, dldt, dbre_t, dbim_t, dcre, dcim, dd, dwglu, dbglu), bg = _s5_bwd(
        dys5, u, xr, xi, *s5_small, after, jobs=jobs(_job_pair(g_out) if distributed else None))
    if distributed:
        (p_out,) = _pair_sum([g_out], bg, c_arr, "pair_sum_out")
    from_t = lambda t: t.reshape(S5_H, S5_G, S5_P).transpose(1, 0, 2)
    small = {
        "s5_lambda_re": dlre, "s5_lambda_im": dlim, "s5_log_dt": dldt[:, :S5_G],
        "s5_b_re": from_t(dbre_t), "s5_b_im": from_t(dbim_t), "s5_c_re": dcre, "s5_c_im": dcim, "s5_d": dd,
        "s5_b_glu": dbglu, "ret_gn_g": dgng, "ret_gn_b": dgnb, "ln1_g": dl1g, "ln1_b": dl1b, "ln2_g": dl2g, "ln2_b": dl2b,
        "s5_w_glu": dwglu, "loss": loss8[0:1, 0:1]}
    early_pack = _pack([small[n] for n in EARLY])
    (dq, dk, dv), bg = _ret_bwd(q, k, v, do, states, cos2, sin2,
                                jobs(*([_job_chips(p_out), _job_gather(early_pack)] if distributed else [])))
    g_int = _in_w_grad(du, dq, dk, dv, dgate, xhat0, li_g, li_b).reshape(N_DEV, PROJ_W // N_DEV, D_MODEL)
    after = jnp.zeros((8, LANE), F32)
    if distributed:
        (r1_in,) = _exchange([_job_pair(g_int)], "exchange_pair_in")
        (p_in,) = _pair_sum([g_int], [r1_in], c_arr, "pair_sum_in")
        (p_up, p_down), (r_up, r_down) = _chips_wait(started_mlp, p_in, "chips_mlp_wait")
        started_in, after = _chips_start([p_in], "chips_in_start")
    grad_x, dmeta, dlig, dlib = _in_bwd(du, dq, dk, dv, dgate, dh0r, xhat0, rstd0, li_g, w_int, after)
    small.update(ln_in_g=dlig, ln_in_b=dlib, meta_tokens=dmeta)
    if distributed:
        (p_in,), (r_in,) = _chips_wait(started_in, dlig, "chips_in_wait")
        big = dict(chip_sums=[p_in, p_out, p_up, p_down], received=[r_in, bg[0], r_up, r_down], early=bg[1])
    else:
        big = dict(partials=[g_int, g_out, g_up, g_down])
    return grad_x, big, small


def kernel(x, meta_tokens, ln_in_g, ln_in_b, w_in, s5_lambda_re, s5_lambda_im, s5_log_dt, s5_b_re, s5_b_im, s5_c_re, s5_c_im, s5_d, s5_w_glu, s5_b_glu, ret_gn_g, ret_gn_b, w_out, ln1_g, ln1_b, w_up, w_down, ln2_g, ln2_b, loss_target, m_meta_tokens, m_ln_in_g, m_ln_in_b, m_w_in, m_s5_lambda_re, m_s5_lambda_im, m_s5_log_dt, m_s5_b_re, m_s5_b_im, m_s5_c_re, m_s5_c_im, m_s5_d, m_s5_w_glu, m_s5_b_glu, m_ret_gn_g, m_ret_gn_b, m_w_out, m_ln1_g, m_ln1_b, m_w_up, m_w_down, m_ln2_g, m_ln2_b, v_meta_tokens, v_ln_in_g, v_ln_in_b, v_w_in, v_s5_lambda_re, v_s5_lambda_im, v_s5_log_dt, v_s5_b_re, v_s5_b_im, v_s5_c_re, v_s5_c_im, v_s5_d, v_s5_w_glu, v_s5_b_glu, v_ret_gn_g, v_ret_gn_b, v_w_out, v_ln1_g, v_ln1_b, v_w_up, v_w_down, v_ln2_g, v_ln2_b):
    args = dict(locals())
    names = ["meta_tokens", "ln_in_g", "ln_in_b", "w_in", "s5_lambda_re", "s5_lambda_im", "s5_log_dt", "s5_b_re", "s5_b_im",
             "s5_c_re", "s5_c_im", "s5_d", "s5_w_glu", "s5_b_glu", "ret_gn_g", "ret_gn_b", "w_out", "ln1_g", "ln1_b",
             "w_up", "w_down", "ln2_g", "ln2_b"]
    ax, ay, ac = _place()
    me = 4 * ax + 2 * ay + ac

    sp = {n: args[n] for n in SMALL}
    grad_x, big, small = _local_step(x[0], loss_target[0], meta_tokens, w_in[0].T.astype(MM), w_out[0].astype(MM),
                                   w_up[0].astype(MM), w_down[0].astype(MM), s5_w_glu[0].astype(MM), sp, True)

    j_arr = jnp.reshape(2 * ax + ay, (1,)).astype(jnp.int32)
    two_d = lambda a: a.reshape(a.shape[-2:])
    item = lambda n, p, r: (p, r, *(two_d(_view(n, a)) for a in (args[n], args["m_" + n], args["v_" + n])))
    late_pack = _pack([small[n] for n in LATE])
    mlp = ("w_out", "w_up", "w_down")
    (late_all,) = _exchange([_job_gather_direct(late_pack)], "gather_small_late")
    res, _ = _adamw_shards(
        [item(n, p, r) for n, p, r in zip(mlp, big["chip_sums"][1:], big["received"][1:])], "adamw_mlp", 8, j_arr)
    res_in, _ = _adamw_shards([item("w_in", big["chip_sums"][0], big["received"][0])], "adamw_in", 2, j_arr)
    upd = {"w_in": res_in}
    for idx, n in enumerate(mlp):
        upd[n] = res[4 * idx:4 * idx + 4]
    shard_grads = {n: upd[n][0] for n in upd}

    early_shapes = [_view(n, args[n]).shape for n in EARLY[:-2]] + [(S5_W, S5_W), (1,)]
    late_shapes = [args["ln_in_g"].shape, args["ln_in_b"].shape, (N_META, D_MODEL)]
    g_small = dict(zip(EARLY, _unpack(_sum_devices(big["early"], "sum_small_early"), early_shapes)))
    g_small.update(zip(LATE, _unpack(_sum_devices(late_all, "sum_small_late"), late_shapes)))
    loss = g_small["loss"].reshape(())

    shard_grads["meta_tokens"] = lax.dynamic_slice(g_small["meta_tokens"], (0, me * (D_MODEL // N_DEV)),
                                                   (N_META, D_MODEL // N_DEV))
    shard_grads["s5_w_glu"] = lax.dynamic_slice(g_small["s5_w_glu"], (me * (S5_W // N_DEV), 0),
                                                (S5_W // N_DEV, S5_W))[None]
    natives = SMALL + ["meta_tokens", "s5_w_glu"]
    res2 = _adamw_native([(shard_grads[n] if n in shard_grads else g_small[n], *(_view(n, args[p + n]) for p in ("", "m_", "v_")))
                          for n in natives], "adamw_small")
    for idx, n in enumerate(natives):
        upd[n] = [shard_grads[n] if n in shard_grads else g_small[n]] + list(res2[3 * idx:3 * idx + 3])

    grads, deltas, new_m, new_v = ([_view(n, upd[n][t]).reshape(args[n].shape) for n in names] for t in range(4))
    return (loss, grad_x[None], *grads, *deltas, *new_m, *new_v)
```

```python
import math

import jax
import jax.numpy as jnp
from jax import lax
from jax.experimental import pallas as pl
from jax.experimental.pallas import tpu as pltpu

F32 = jnp.float32
MM = jnp.bfloat16

D_MODEL = 1024
N_META = 16
CHUNK = 128
PAD = CHUNK - N_META
S5_W, S5_G, S5_H, S5_P = 256, 16, 16, 64
S5_N = S5_G * S5_P
RET_W, RET_H, HEAD = 768, 6, 128
D_FF = 4096
PROJ_W = S5_W + 4 * RET_W
N_DEV = 8
FF_BLK = D_FF // N_DEV
ROW_BLK = 384
MLP_ROWS = 1408
PROJ_ROWS = 704
ALPHA = 2.0 ** 0.25
LN_EPS = 1e-5
GN_EPS = 1e-5
ROPE_BASE = 10000.0
GELU_C = math.sqrt(2.0 / math.pi)
GELU_A = 0.044715
ADAM_LR, ADAM_B1, ADAM_B2, ADAM_EPS, ADAM_WD, ADAM_STEP = 0.001, 0.9, 0.999, 1e-08, 0.01, 10
VMEM_LIMIT = 60 * 1024 * 1024

_VMEM = pl.BlockSpec(memory_space=pltpu.VMEM)
_ANY = pl.BlockSpec(memory_space=pl.ANY)
_MESH = pl.DeviceIdType.MESH


def _params(sem=None):
    return pltpu.CompilerParams(dimension_semantics=sem, vmem_limit_bytes=VMEM_LIMIT)


def _dot(a, b):
    return jnp.dot(a.astype(MM), b.astype(MM), preferred_element_type=F32)


def _dot_nt(a, b):
    return lax.dot_general(a.astype(MM), b.astype(MM), (((1,), (1,)), ((), ())), preferred_element_type=F32)


def _dot_tn(a, b):
    return lax.dot_general(a.astype(MM), b.astype(MM), (((0,), (0,)), ((), ())), preferred_element_type=F32)


def _split3(a):
    hi = a.astype(jnp.bfloat16)
    r1 = a - hi.astype(F32)
    mid = r1.astype(jnp.bfloat16)
    lo = (r1 - mid.astype(F32)).astype(jnp.bfloat16)
    return hi, mid, lo


def _dot_sel_rhs(a, sel):
    s = sel.astype(jnp.bfloat16)
    return sum(jnp.dot(p, s, preferred_element_type=F32) for p in _split3(a))


def _dot_sel_lhs(sel, b):
    s = sel.astype(jnp.bfloat16)
    return sum(jnp.dot(s, p, preferred_element_type=F32) for p in _split3(b))


def _ln_fwd(r, eps):
    mu = jnp.mean(r, axis=-1, keepdims=True)
    xc = r - mu
    var = jnp.mean(xc * xc, axis=-1, keepdims=True)
    rstd = lax.rsqrt(var + eps)
    return xc * rstd, rstd


def _ln_bwd(dxhat, xhat, rstd):
    m1 = jnp.mean(dxhat, axis=-1, keepdims=True)
    m2 = jnp.mean(dxhat * xhat, axis=-1, keepdims=True)
    return rstd * (dxhat - m1 - xhat * m2)


def _colsum(a):
    return jnp.sum(a, axis=0, keepdims=True)


def _shift3(n_in, block=lambda i: i):
    return [pl.BlockSpec((CHUNK, D_MODEL), (lambda i, j=j: (jnp.clip(3 * block(i) - 1 + j, 0, n_in - 1), 0)))
            for j in range(3)]


def _ln_in(x2d, meta, jobs=(), gather_meta=False):
    seq = x2d.shape[0]
    tp = seq + CHUNK
    R = ROW_BLK
    nb = tp // R
    shard_w = D_MODEL // N_DEV

    def body(xa, xb, xc, meta_ref, xhat_ref, rstd_ref, raw_ref, *gathered):
        raw_ref[0:CHUNK, :] = xa[...]
        raw_ref[CHUNK:2 * CHUNK, :] = xb[...]
        raw_ref[2 * CHUNK:3 * CHUNK, :] = xc[...]

        @pl.when(pl.program_id(0) == nb - 1)
        def _():
            raw_ref[0:PAD, :] = jnp.zeros((PAD, D_MODEL), F32)
            if gather_meta:
                for d in range(N_DEV):
                    pltpu.sync_copy(gathered[0].at[d], raw_ref.at[PAD:CHUNK, d * shard_w:(d + 1) * shard_w])
            else:
                raw_ref[PAD:CHUNK, :] = meta_ref[...]

        xhat_ref[...], rstd_ref[...] = _ln_fwd(raw_ref[...], LN_EPS)

    row = lambda w: pl.BlockSpec((R, w), lambda i: (nb - 1 - i, 0))
    jobs = ([_job_gather(meta)] if gather_meta else []) + list(jobs)
    return _call(
        body, "ln_in", (nb,),
        _shift3(seq // CHUNK, lambda i: nb - 1 - i) + [pl.BlockSpec(meta.shape, lambda i: (0, 0))],
        [row(D_MODEL), row(1)], [jax.ShapeDtypeStruct((tp, D_MODEL), F32), jax.ShapeDtypeStruct((tp, 1), F32)],
        [pltpu.VMEM((R, D_MODEL), F32)], (x2d, x2d, x2d, meta), jobs, early=1 if gather_meta else 0)


def _in_proj(xhat0, ln_g, ln_b, w_int, cos2, sin2, jobs=()):
    tp = xhat0.shape[0]
    R = PROJ_ROWS if tp % PROJ_ROWS == 0 else ROW_BLK

    def body(xh_ref, g_ref, b_ref, w_ref, cos_ref, sin_ref, u_ref, q_ref, k_ref, v_ref, gate_ref):
        hb = (xh_ref[...] * g_ref[...] + b_ref[...]).astype(MM)
        valid = (pl.program_id(0) * R + lax.broadcasted_iota(jnp.int32, (R, 1), 0)) >= PAD

        def seg(lo, hi):
            return jnp.where(valid, _dot_nt(hb, w_ref[lo:hi, :]), 0.0)

        u_ref[...] = seg(0, S5_W)
        cos = cos_ref[...]
        sin = sin_ref[...]
        q = seg(S5_W, S5_W + RET_W)
        k = seg(S5_W + RET_W, S5_W + 2 * RET_W)
        for h in range(RET_H):
            sl = slice(h * HEAD, (h + 1) * HEAD)
            qh = q[:, sl]
            kh = k[:, sl]
            q_ref[:, sl] = (qh * cos + pltpu.roll(qh, HEAD // 2, 1) * sin).astype(q_ref.dtype)
            k_ref[:, sl] = ((kh * cos + pltpu.roll(kh, HEAD // 2, 1) * sin) * (HEAD ** -0.5)).astype(k_ref.dtype)
        v_ref[...] = seg(S5_W + 2 * RET_W, S5_W + 3 * RET_W).astype(v_ref.dtype)
        gate_ref[...] = seg(S5_W + 3 * RET_W, PROJ_W)

    def rows(w, dt):
        return pl.BlockSpec((R, w), lambda i: (i, 0)), jax.ShapeDtypeStruct((tp, w), dt)

    outs = [rows(S5_W, F32), rows(RET_W, MM), rows(RET_W, MM), rows(RET_W, MM), rows(RET_W, F32)]
    full = lambda s: pl.BlockSpec(s, lambda i: (0,) * len(s))
    return _call(
        body, "in_proj", (tp // R,),
        [pl.BlockSpec((R, D_MODEL), lambda i: (i, 0)), full((1, D_MODEL)), full((1, D_MODEL)), _VMEM,
         pl.BlockSpec((R, HEAD), lambda i: (i, 0)), pl.BlockSpec((R, HEAD), lambda i: (i, 0))],
        [o[0] for o in outs], [o[1] for o in outs], [], (xhat0, ln_g, ln_b, w_int, cos2, sin2), jobs)


def _s5_disc(lre, lim, ldt, bre_t, bim_t):
    dt = jnp.exp(ldt)
    mag = jnp.exp(lre * dt)
    ang = lim * dt
    lbr = mag * jnp.cos(ang)
    lbi = mag * jnp.sin(ang)
    den = lre * lre + lim * lim
    nr = lbr - 1.0
    qr = (nr * lre + lbi * lim) / den
    qi = (lbi * lre - nr * lim) / den
    return lbr, lbi, qr * bre_t - qi * bim_t, qr * bim_t + qi * bre_t


def _s5_tables(lbr, lbi, reverse):
    if reverse:
        lbi = -lbi
    pw = [(lbr, lbi)]
    for _ in range(7):
        r, i = pw[-1]
        pw.append((r * lbr - i * lbi, r * lbi + i * lbr))
    row = lax.broadcasted_iota(jnp.int32, (8, S5_N), 0)
    tabs = []
    for k in range(3):
        sh = 2 ** k
        mask = (row < 8 - sh) if reverse else (row >= sh)
        ar, ai = pw[sh - 1]
        tabs.append((jnp.where(mask, ar, 0.0), jnp.where(mask, ai, 0.0)))
    pr = jnp.zeros((8, S5_N), F32)
    pi = jnp.zeros((8, S5_N), F32)
    for i in range(8):
        ar, ai = pw[7 - i] if reverse else pw[i]
        pr = jnp.where(row == i, ar, pr)
        pi = jnp.where(row == i, ai, pi)
    tabs.append((pr, pi))
    return tabs


def _store_tables(tab_ref, tabs):
    for k, (r, i) in enumerate(tabs):
        tab_ref[2 * k] = r
        tab_ref[2 * k + 1] = i


def _bd_mask():
    r = lax.broadcasted_iota(jnp.int32, (S5_W, S5_N), 0)
    c = lax.broadcasted_iota(jnp.int32, (S5_W, S5_N), 1)
    return jnp.right_shift(r, 4) == jnp.right_shift(c, 6)


def _s5_block_diag(bbr_t, bbi_t, cre_w, cim_w):
    mask = _bd_mask()
    bd = lambda t: jnp.where(mask, t, 0.0)
    return (bd(jnp.tile(bbr_t, (S5_G, 1))), bd(jnp.tile(bbi_t, (S5_G, 1))),
            bd(jnp.tile(cre_w, (1, S5_N // HEAD))), bd(jnp.tile(cim_w, (1, S5_N // HEAD))))


def _scan8(xr, xi, tab_ref, lanes, reverse):
    for k in range(3):
        sh = (8 - 2 ** k) if reverse else 2 ** k
        sr = pltpu.roll(xr, sh, 0)
        si = pltpu.roll(xi, sh, 0)
        mr = tab_ref[2 * k, :, lanes]
        mi = tab_ref[2 * k + 1, :, lanes]
        xr, xi = xr + (mr * sr - mi * si), xi + (mr * si + mi * sr)
    return xr, xi


S5_LANES = 512


def _gelu(y):
    t = jnp.tanh(GELU_C * (y + GELU_A * y * y * y))
    return 0.5 * y * (1.0 + t), t


def _s5_fwd(u, lre, lim, ldt, bre_t, bim_t, cre_w, cim_w, d_row, w_glu, b_glu, jobs=()):
    tp = u.shape[0]
    R = ROW_BLK

    def body(u_ref, lre_ref, lim_ref, ldt_ref, bre_ref, bim_ref, cre_ref, cim_ref, d_ref, wg_ref, bg_ref,
             y_ref, xr_ref, xi_ref, bbd_r, bbd_i, cbd_r, cbd_i, tab_ref, car_r, car_i):
        @pl.when(pl.program_id(0) == 0)
        def _():
            lbr, lbi, bbr, bbi = _s5_disc(lre_ref[...], lim_ref[...], ldt_ref[...], bre_ref[...], bim_ref[...])
            br, bi, cr, ci = _s5_block_diag(bbr, bbi, cre_ref[...], cim_ref[...])
            bbd_r[...] = br.astype(MM)
            bbd_i[...] = bi.astype(MM)
            cbd_r[...] = cr.astype(MM)
            cbd_i[...] = ci.astype(MM)
            _store_tables(tab_ref, _s5_tables(lbr, lbi, False))
            car_r[...] = jnp.zeros_like(car_r)
            car_i[...] = jnp.zeros_like(car_i)

        u = u_ref[...]
        ub = u.astype(MM)
        xr_ref[...] = jnp.dot(ub, bbd_r[...], preferred_element_type=F32)
        xi_ref[...] = jnp.dot(ub, bbd_i[...], preferred_element_type=F32)
        for j in range(S5_N // S5_LANES):
            lanes = pl.ds(j * S5_LANES, S5_LANES)
            pr = tab_ref[6, :, lanes]
            pi = tab_ref[7, :, lanes]

            def step(g, carry):
                cr, ci = carry
                rows = pl.ds(pl.multiple_of(g * 8, 8), 8)
                xr, xi = _scan8(xr_ref[rows, lanes], xi_ref[rows, lanes], tab_ref, lanes, False)
                br = jnp.broadcast_to(cr[7:8, :], cr.shape)
                bi = jnp.broadcast_to(ci[7:8, :], ci.shape)
                xr = xr + (pr * br - pi * bi)
                xi = xi + (pr * bi + pi * br)
                xr_ref[rows, lanes] = xr
                xi_ref[rows, lanes] = xi
                return xr, xi

            cr, ci = lax.fori_loop(0, R // 8, step, (car_r[:, lanes], car_i[:, lanes]), unroll=2)
            car_r[:, lanes] = cr
            car_i[:, lanes] = ci
        y = _dot_nt(xr_ref[...], cbd_r[...]) - _dot_nt(xi_ref[...], cbd_i[...]) + d_ref[...] * u
        yg, _ = _gelu(y)
        z = _dot(yg, wg_ref[...]) + bg_ref[...]
        y_ref[...] = yg * jax.nn.sigmoid(z)

    full = lambda a: pl.BlockSpec(a.shape, lambda i: (0,) * a.ndim)
    small = [lre, lim, ldt, bre_t, bim_t, cre_w, cim_w, d_row, w_glu, b_glu]
    return _call(
        body, "s5_fwd", (tp // R,),
        [pl.BlockSpec((R, S5_W), lambda i: (i, 0))] + [full(a) for a in small],
        [pl.BlockSpec((R, S5_W), lambda i: (i, 0)), pl.BlockSpec((R, S5_N), lambda i: (i, 0)),
         pl.BlockSpec((R, S5_N), lambda i: (i, 0))],
        [jax.ShapeDtypeStruct((tp, S5_W), F32), jax.ShapeDtypeStruct((tp, S5_N), F32),
         jax.ShapeDtypeStruct((tp, S5_N), F32)],
        [pltpu.VMEM((S5_W, S5_N), MM)] * 4 + [pltpu.VMEM((8, 8, S5_N), F32), pltpu.VMEM((8, S5_N), F32),
                                              pltpu.VMEM((8, S5_N), F32)],
        (u, *small), jobs)


def _s5_bwd(dy_out, u, xr, xi, lre, lim, ldt, bre_t, bim_t, cre_w, cim_w, d_row, w_glu, b_glu, after, jobs=()):
    tp = u.shape[0]
    R = ROW_BLK
    nb = tp // R

    def body(dyo_ref, u_ref, xr_ref, xi_ref, xpr_ref, xpi_ref,
             lre_ref, lim_ref, ldt_ref, bre_ref, bim_ref, cre_ref, cim_ref, d_ref, wg_ref, bg_ref, after_ref,
             du_ref, dlre_ref, dlim_ref, dldt_ref, dbre_ref, dbim_ref, dcre_ref, dcim_ref, dd_ref, dwg_ref, dbg_ref,
             bbd_r, bbd_i, cbd_r, cbd_i, tab_ref, car_r, car_i, gr_ref, gi_ref, xer_ref, xei_ref,
             abr, abi, acr, aci, adr, adi):
        i = pl.program_id(0)

        @pl.when(i == 0)
        def _():
            lbr, lbi, bbr, bbi = _s5_disc(lre_ref[...], lim_ref[...], ldt_ref[...], bre_ref[...], bim_ref[...])
            br, bi, cr, ci = _s5_block_diag(bbr, bbi, cre_ref[...], cim_ref[...])
            bbd_r[...] = br.astype(MM)
            bbd_i[...] = bi.astype(MM)
            cbd_r[...] = cr.astype(MM)
            cbd_i[...] = ci.astype(MM)
            _store_tables(tab_ref, _s5_tables(lbr, lbi, True))
            for ref in (car_r, car_i, abr, abi, acr, aci, adr, adi, dd_ref, dwg_ref, dbg_ref):
                ref[...] = jnp.zeros_like(ref)

        u = u_ref[...]
        xrv = xr_ref[...]
        xiv = xi_ref[...]
        y = _dot_nt(xrv, cbd_r[...]) - _dot_nt(xiv, cbd_i[...]) + d_ref[...] * u
        yg, t = _gelu(y)
        z = _dot(yg, wg_ref[...]) + bg_ref[...]
        s = jax.nn.sigmoid(z)
        dout = dyo_ref[...]
        dz = dout * yg * s * (1.0 - s)
        dyg = dout * s + _dot_nt(dz, wg_ref[...])
        dwg_ref[...] += _dot_tn(yg, dz)
        dbg_ref[...] += _colsum(dz)
        dy = dyg * (0.5 * (1.0 + t) + 0.5 * y * (1.0 - t * t) * GELU_C * (1.0 + 3.0 * GELU_A * y * y))
        dd_ref[...] += _colsum(dy * u)
        acr[...] += _dot_tn(dy, xrv)
        aci[...] -= _dot_tn(dy, xiv)
        gr_ref[...] = _dot(dy, cbd_r[...])
        gi_ref[...] = -_dot(dy, cbd_i[...])
        has_prev = (i < nb - 1).astype(F32)
        xer_ref[0:8, :] = xpr_ref[...] * has_prev
        xei_ref[0:8, :] = xpi_ref[...] * has_prev
        xer_ref[8:R + 8, :] = xrv
        xei_ref[8:R + 8, :] = xiv
        row = lax.broadcasted_iota(jnp.int32, (8, S5_LANES), 0)
        for j in range(S5_N // S5_LANES):
            lanes = pl.ds(j * S5_LANES, S5_LANES)
            pr = tab_ref[6, :, lanes]
            pi = tab_ref[7, :, lanes]

            def step(n, carry):
                cr, ci, sar, sai = carry
                g = R // 8 - 1 - n
                r0 = pl.multiple_of(g * 8, 8)
                rows = pl.ds(r0, 8)
                gr, gi = _scan8(gr_ref[rows, lanes], gi_ref[rows, lanes], tab_ref, lanes, True)
                br = jnp.broadcast_to(cr[0:1, :], cr.shape)
                bi = jnp.broadcast_to(ci[0:1, :], ci.shape)
                gr = gr + (pr * br - pi * bi)
                gi = gi + (pr * bi + pi * br)
                gr_ref[rows, lanes] = gr
                gi_ref[rows, lanes] = gi
                last = row == 7
                xpr = pltpu.roll(jnp.where(last, xer_ref[rows, lanes], xer_ref[pl.ds(r0 + 8, 8), lanes]), 1, 0)
                xpi = pltpu.roll(jnp.where(last, xei_ref[rows, lanes], xei_ref[pl.ds(r0 + 8, 8), lanes]), 1, 0)
                return gr, gi, sar + (gr * xpr + gi * xpi), sai + (gi * xpr - gr * xpi)

            cr, ci, sar, sai = lax.fori_loop(
                0, R // 8, step, (car_r[:, lanes], car_i[:, lanes], adr[:, lanes], adi[:, lanes]), unroll=2)
            car_r[:, lanes] = cr
            car_i[:, lanes] = ci
            adr[:, lanes] = sar
            adi[:, lanes] = sai
        grv = gr_ref[...]
        giv = gi_ref[...]
        du_ref[...] = (dy * d_ref[...] + _dot_nt(grv, bbd_r[...]) + _dot_nt(giv, bbd_i[...])).astype(du_ref.dtype)
        abr[...] += _dot_tn(u, grv)
        abi[...] += _dot_tn(u, giv)

        @pl.when(i == nb - 1)
        def _():
            mask = _bd_mask()
            r16 = lax.broadcasted_iota(jnp.int32, (S5_H, S5_W), 1)
            h16 = lax.broadcasted_iota(jnp.int32, (S5_H, S5_W), 0)
            fold_b = jnp.bitwise_and(r16, S5_H - 1) == h16
            c64 = lax.broadcasted_iota(jnp.int32, (S5_N, S5_P), 0)
            p64 = lax.broadcasted_iota(jnp.int32, (S5_N, S5_P), 1)
            fold_c = jnp.bitwise_and(c64, S5_P - 1) == p64
            dbbr = _dot_sel_lhs(fold_b, jnp.where(mask, abr[...], 0.0))
            dbbi = _dot_sel_lhs(fold_b, jnp.where(mask, abi[...], 0.0))
            dcre_ref[...] = _dot_sel_rhs(jnp.where(mask, acr[...], 0.0), fold_c)
            dcim_ref[...] = _dot_sel_rhs(jnp.where(mask, aci[...], 0.0), fold_c)
            dlbr = _colsum(adr[...])
            dlbi = _colsum(adi[...])
            _, vjp = jax.vjp(_s5_disc, lre_ref[...], lim_ref[...], ldt_ref[...], bre_ref[...], bim_ref[...])
            dlre, dlim, dldt, dbre, dbim = vjp((dlbr, dlbi, dbbr, dbbi))
            dlre_ref[...] = dlre
            dlim_ref[...] = dlim
            dbre_ref[...] = dbre
            dbim_ref[...] = dbim
            gsel = jnp.right_shift(lax.broadcasted_iota(jnp.int32, (S5_N, HEAD), 0), 6) == \
                lax.broadcasted_iota(jnp.int32, (S5_N, HEAD), 1)
            dldt_ref[...] = _dot_sel_rhs(dldt, gsel)

    full = lambda a: pl.BlockSpec(a.shape, lambda i: (0,) * a.ndim)
    rev = lambda w: pl.BlockSpec((R, w), lambda i: (nb - 1 - i, 0))
    prev8 = pl.BlockSpec((8, S5_N), lambda i: (jnp.maximum((nb - 1 - i) * (R // 8) - 1, 0), 0))
    small = [lre, lim, ldt, bre_t, bim_t, cre_w, cim_w, d_row, w_glu, b_glu]
    outs = [((tp, S5_W), rev(S5_W))] + [
        (s, pl.BlockSpec(s, lambda i: (0, 0))) for s in
        [(1, S5_N), (1, S5_N), (1, HEAD), (S5_H, S5_N), (S5_H, S5_N), (S5_W, S5_P), (S5_W, S5_P),
         (1, S5_W), (S5_W, S5_W), (1, S5_W)]]
    return _call(
        body, "s5_bwd", (nb,),
        [rev(S5_W), rev(S5_W), rev(S5_N), rev(S5_N), prev8, prev8] + [full(a) for a in small + [after]],
        [o[1] for o in outs], [jax.ShapeDtypeStruct(o[0], MM if n == 0 else F32) for n, o in enumerate(outs)],
        [pltpu.VMEM((S5_W, S5_N), MM)] * 4 + [
            pltpu.VMEM((8, 8, S5_N), F32), pltpu.VMEM((8, S5_N), F32), pltpu.VMEM((8, S5_N), F32),
            pltpu.VMEM((R, S5_N), F32), pltpu.VMEM((R, S5_N), F32),
            pltpu.VMEM((R + 8, S5_N), F32), pltpu.VMEM((R + 8, S5_N), F32)] + [pltpu.VMEM((S5_W, S5_N), F32)] * 4 + [
            pltpu.VMEM((8, S5_N), F32), pltpu.VMEM((8, S5_N), F32)],
        (dy_out, u, xr, xi, xr, xi, *small, after), jobs)


RET_CHUNK = ROW_BLK
LOG_GAMMA = [math.log1p(-2.0 ** (-5 - h)) for h in range(RET_H)]
GAMMA_CHUNK = [math.exp(RET_CHUNK * lg) for lg in LOG_GAMMA]
_DECAY_SCRATCH = [pltpu.VMEM((RET_H, RET_CHUNK, RET_CHUNK), F32), pltpu.VMEM((RET_H, RET_CHUNK, HEAD), F32),
                  pltpu.VMEM((RET_H, RET_CHUNK, HEAD), F32)]


def _fill_decay(dm_ref, ze_ref, xi_ref):
    C = RET_CHUNK
    diff = (lax.broadcasted_iota(jnp.int32, (C, C), 0) - lax.broadcasted_iota(jnp.int32, (C, C), 1)).astype(F32)
    r = lax.broadcasted_iota(jnp.int32, (C, HEAD), 0).astype(F32)
    for h, lg in enumerate(LOG_GAMMA):
        dm_ref[h] = jnp.where(diff >= 0.0, jnp.exp(jnp.maximum(diff, 0.0) * lg), 0.0)
        ze_ref[h] = jnp.exp((C - 1.0 - r) * lg)
        xi_ref[h] = jnp.exp((r + 1.0) * lg)


def _ret_fwd(q, k, v, jobs=()):
    tp = q.shape[0]
    C = RET_CHUNK
    nc = tp // C

    def body(q_ref, k_ref, v_ref, o_ref, st_ref, s_ref, dm_ref, ze_ref, xi_ref):
        @pl.when(pl.program_id(0) == 0)
        def _():
            s_ref[...] = jnp.zeros_like(s_ref)
            _fill_decay(dm_ref, ze_ref, xi_ref)

        for h in range(RET_H):
            sl = slice(h * HEAD, (h + 1) * HEAD)
            qh, kh, vh = q_ref[:, sl], k_ref[:, sl], v_ref[:, sl]
            sh = s_ref[h]
            st_ref[0, sl, :] = sh
            scores = _dot_nt(qh, kh) * dm_ref[h]
            o_ref[:, sl] = _dot(scores, vh) + _dot(qh, sh) * xi_ref[h]
            s_ref[h] = GAMMA_CHUNK[h] * sh + _dot_tn(kh.astype(F32) * ze_ref[h], vh)

    blk = pl.BlockSpec((C, RET_W), lambda c: (c, 0))
    return _call(
        body, "ret_fwd", (nc,), [blk, blk, blk], [blk, pl.BlockSpec((1, RET_W, HEAD), lambda c: (c, 0, 0))],
        [jax.ShapeDtypeStruct((tp, RET_W), F32), jax.ShapeDtypeStruct((nc, RET_W, HEAD), F32)],
        [pltpu.VMEM((RET_H, HEAD, HEAD), F32)] + _DECAY_SCRATCH, (q, k, v), jobs)


def _ret_bwd(q, k, v, do, states, cos2, sin2, jobs=()):
    tp = q.shape[0]
    C = RET_CHUNK
    nc = tp // C

    def body(q_ref, k_ref, v_ref, do_ref, st_ref, cos_ref, sin_ref,
             dq_ref, dk_ref, dv_ref, ds_ref, dm_ref, ze_ref, xi_ref):
        @pl.when(pl.program_id(0) == 0)
        def _():
            ds_ref[...] = jnp.zeros_like(ds_ref)
            _fill_decay(dm_ref, ze_ref, xi_ref)

        cos = cos_ref[...]
        sin = sin_ref[...]
        for h in range(RET_H):
            sl = slice(h * HEAD, (h + 1) * HEAD)
            qh, kh, vh = q_ref[:, sl], k_ref[:, sl], v_ref[:, sl]
            dmh = dm_ref[h]
            sh = st_ref[0, sl, :]
            dsn = ds_ref[h]
            doh = do_ref[:, sl]
            dox = doh * xi_ref[h]
            a = _dot_nt(qh, kh) * dmh
            dqk = _dot_nt(doh, vh) * dmh
            kz = kh.astype(F32) * ze_ref[h]
            dv_ref[:, sl] = (_dot_tn(a, doh) + _dot(kz, dsn)).astype(dv_ref.dtype)
            dqr = _dot(dqk, kh) + _dot_nt(dox, sh)
            dkr = _dot_tn(dqk, qh) + ze_ref[h] * _dot_nt(vh, dsn)
            ds_ref[h] = GAMMA_CHUNK[h] * dsn + _dot_tn(qh, dox)
            dq_ref[:, sl] = (dqr * cos - pltpu.roll(dqr, HEAD // 2, 1) * sin).astype(dq_ref.dtype)
            dk_ref[:, sl] = ((dkr * cos - pltpu.roll(dkr, HEAD // 2, 1) * sin) * (HEAD ** -0.5)).astype(dk_ref.dtype)

    blk = pl.BlockSpec((C, RET_W), lambda c: (nc - 1 - c, 0))
    tab = pl.BlockSpec((C, HEAD), lambda c: (nc - 1 - c, 0))
    return _call(
        body, "ret_bwd", (nc,),
        [blk, blk, blk, blk, pl.BlockSpec((1, RET_W, HEAD), lambda c: (nc - 1 - c, 0, 0)), tab, tab],
        [blk, blk, blk], [jax.ShapeDtypeStruct((tp, RET_W), MM)] * 3,
        [pltpu.VMEM((RET_H, HEAD, HEAD), F32)] + _DECAY_SCRATCH, (q, k, v, do, states, cos2, sin2), jobs)


def _gn_gate(o, gate, gn_g, gn_b):
    xhat, rstd = _ln_fwd(o, GN_EPS)
    on = xhat * gn_g + gn_b
    s = jax.nn.sigmoid(gate)
    return gate * s * on, xhat, rstd, on, s


def _post_up(o, gate, ys5, xhat0, gn_g, gn_b, li_g, li_b, l1_g, l1_b, w_out, w_up, jobs=()):
    tp = o.shape[0]
    R = ROW_BLK

    def body(o_ref, g_ref, ys_ref, xh0_ref, gng, gnb, lig, lib, l1g, l1b, wo_ref, wu_ref,
             ycat_ref, xh1_ref, rstd1_ref, h1b_ref, pre_ref):
        ycat_ref[:, 0:S5_W] = ys_ref[...].astype(ycat_ref.dtype)
        for h in range(RET_H):
            sl = slice(h * HEAD, (h + 1) * HEAD)
            yret = _gn_gate(o_ref[:, sl], g_ref[:, sl], gng[:, sl], gnb[:, sl])[0]
            ycat_ref[:, S5_W + h * HEAD:S5_W + (h + 1) * HEAD] = yret.astype(ycat_ref.dtype)
        mixed = _dot(ycat_ref[...], wo_ref[...])
        h0 = xh0_ref[...] * lig[...] + lib[...]
        xh1, rstd1 = _ln_fwd(ALPHA * h0 + mixed, LN_EPS)
        xh1_ref[...] = xh1
        rstd1_ref[...] = rstd1
        h1b = (xh1 * l1g[...] + l1b[...]).astype(MM)
        h1b_ref[...] = h1b
        for d in range(N_DEV):
            pre_ref[:, d * FF_BLK:(d + 1) * FF_BLK] = jnp.maximum(_dot(h1b, wu_ref[d]), 0.0)

    row = lambda w: pl.BlockSpec((R, w), lambda i: (i, 0))
    full = lambda a: pl.BlockSpec(a.shape, lambda i: (0,) * a.ndim)
    vecs = [gn_g, gn_b, li_g, li_b, l1_g, l1_b]
    outs = [(row(D_MODEL), jax.ShapeDtypeStruct((tp, D_MODEL), MM)), (row(D_MODEL), jax.ShapeDtypeStruct((tp, D_MODEL), F32)),
            (row(1), jax.ShapeDtypeStruct((tp, 1), F32)), (row(D_MODEL), jax.ShapeDtypeStruct((tp, D_MODEL), MM)),
            (row(D_FF), jax.ShapeDtypeStruct((tp, D_FF), F32))]
    return _call(
        body, "post_up", (tp // R,),
        [row(RET_W), row(RET_W), row(S5_W), row(D_MODEL)] + [full(a) for a in vecs] + [_VMEM, _VMEM],
        [o[0] for o in outs], [o[1] for o in outs], [], (o, gate, ys5, xhat0, *vecs, w_out, w_up), jobs)


def _post_down(pre, xhat1, tgt, l1_g, l1_b, l2_g, l2_b, w_down):
    tp = pre.shape[0]
    seq = tgt.shape[0]
    R = ROW_BLK

    def body(pre_ref, xh1_ref, ta, tb, tc, l1g, l1b, l2g, l2b, wd_ref,
             dr2_ref, dffb_ref, loss_ref, dl2g_ref, dl2b_ref, tgt_ref):
        i = pl.program_id(0)

        @pl.when(i == 0)
        def _():
            for ref in (loss_ref, dl2g_ref, dl2b_ref):
                ref[...] = jnp.zeros_like(ref)

        tgt_ref[0:CHUNK, :] = ta[...]
        tgt_ref[CHUNK:2 * CHUNK, :] = tb[...]
        tgt_ref[2 * CHUNK:3 * CHUNK, :] = tc[...]
        ff = jnp.zeros((R, D_MODEL), F32)
        for d in range(N_DEV):
            pre = pre_ref[:, d * FF_BLK:(d + 1) * FF_BLK]
            ff = ff + _dot(pre * pre, wd_ref[d * FF_BLK:(d + 1) * FF_BLK, :])
        h1 = xh1_ref[...] * l1g[...] + l1b[...]
        xh2, rstd2 = _ln_fwd(ALPHA * h1 + ff, LN_EPS)
        h2 = xh2 * l2g[...] + l2b[...]
        valid = (i * R + lax.broadcasted_iota(jnp.int32, (R, 1), 0)) >= CHUNK
        err = jnp.where(valid, h2 - tgt_ref[...], 0.0)
        loss_ref[...] += 0.5 * jnp.sum(err * err) / D_MODEL
        dh2 = err * (1.0 / D_MODEL)
        dl2g_ref[...] += _colsum(dh2 * xh2)
        dl2b_ref[...] += _colsum(dh2)
        dr2 = _ln_bwd(dh2 * l2g[...], xh2, rstd2)
        dr2_ref[...] = dr2
        dffb_ref[...] = dr2.astype(MM)

    row = lambda w: pl.BlockSpec((R, w), lambda i: (i, 0))
    full = lambda a: pl.BlockSpec(a.shape, lambda i: (0,) * a.ndim)
    vecs = [l1_g, l1_b, l2_g, l2_b]
    acc = lambda s: (pl.BlockSpec(s, lambda i: (0, 0)), jax.ShapeDtypeStruct(s, F32))
    outs = [(row(D_MODEL), jax.ShapeDtypeStruct((tp, D_MODEL), F32)), (row(D_MODEL), jax.ShapeDtypeStruct((tp, D_MODEL), MM)),
            acc((8, HEAD)), acc((1, D_MODEL)), acc((1, D_MODEL))]
    return pl.pallas_call(
        body, name="post_down", grid=(tp // R,),
        in_specs=[row(D_FF), row(D_MODEL)] + _shift3(seq // CHUNK) + [full(a) for a in vecs] + [_VMEM],
        out_specs=[o[0] for o in outs], out_shape=[o[1] for o in outs],
        scratch_shapes=[pltpu.VMEM((R, D_MODEL), F32)],
        compiler_params=_params(("arbitrary",)),
    )(pre, xhat1, tgt, tgt, tgt, *vecs, w_down)


def _mlp_bwd(h1b, dffb, pre, w_up, w_down):
    tp = h1b.shape[0]
    R = MLP_ROWS if tp % MLP_ROWS == 0 else ROW_BLK
    nr = tp // R

    def body(h_ref, df_ref, pre_ref, wu_ref, wd_ref, gup_ref, gdn_ref, dh1_ref, aup, adn):
        d = pl.program_id(0)
        r = pl.program_id(1)

        @pl.when(r == 0)
        def _():
            aup[...] = jnp.zeros_like(aup)
            adn[...] = jnp.zeros_like(adn)

        h = h_ref[...]
        df = df_ref[...]
        wu = wu_ref[0]
        wd = wd_ref[0]
        pre = pre_ref[...]
        dpre = (_dot_nt(df, wd) * (2.0 * pre)).astype(MM)

        aup[...] += _dot_tn(h, dpre)
        adn[...] += _dot_tn(pre * pre, df)
        contrib = _dot_nt(dpre, wu)
        rows = pl.ds(pl.multiple_of(r * R, 64), R)

        @pl.when(d == 0)
        def _():
            dh1_ref[rows, :] = contrib

        @pl.when(d > 0)
        def _():
            dh1_ref[rows, :] += contrib

        @pl.when(r == nr - 1)
        def _():
            gup_ref[0] = aup[...].astype(gup_ref.dtype)
            gdn_ref[0] = adn[...].astype(gdn_ref.dtype)

    return pl.pallas_call(
        body, name="mlp_bwd", grid=(N_DEV, nr),
        in_specs=[pl.BlockSpec((R, D_MODEL), lambda d, r: (r, 0)), pl.BlockSpec((R, D_MODEL), lambda d, r: (r, 0)),
                  pl.BlockSpec((R, FF_BLK), lambda d, r: (r, d)),
                  pl.BlockSpec((1, D_MODEL, FF_BLK), lambda d, r: (d, 0, 0)),
                  pl.BlockSpec((1, FF_BLK, D_MODEL), lambda d, r: (d, 0, 0))],
        out_specs=[pl.BlockSpec((1, D_MODEL, FF_BLK), lambda d, r: (d, 0, 0)),
                   pl.BlockSpec((1, FF_BLK, D_MODEL), lambda d, r: (d, 0, 0)), _VMEM],
        out_shape=[jax.ShapeDtypeStruct((N_DEV, D_MODEL, FF_BLK), MM), jax.ShapeDtypeStruct((N_DEV, FF_BLK, D_MODEL), MM),
                   jax.ShapeDtypeStruct((tp, D_MODEL), F32)],
        scratch_shapes=[pltpu.VMEM((D_MODEL, FF_BLK), F32), pltpu.VMEM((FF_BLK, D_MODEL), F32)],
        compiler_params=_params(("arbitrary", "arbitrary")),
    )(h1b, dffb, pre, w_up, w_down.reshape(N_DEV, FF_BLK, D_MODEL))


def _post_bwd(dh1m, dr2, xhat1, rstd1, ycat, o, gate, gn_g, gn_b, l1_g, w_out, jobs=()):
    tp = o.shape[0]
    R = ROW_BLK
    nb = tp // R

    def body(dm_ref, dr2_ref, xh1_ref, rs1_ref, yc_ref, o_ref, g_ref, gng, gnb, l1g, wo_ref,
             do_ref, dg_ref, dys_ref, dh0_ref, gwo_ref, dl1g_ref, dl1b_ref, dgng_ref, dgnb_ref, awo):
        i = pl.program_id(0)

        @pl.when(i == 0)
        def _():
            for ref in (awo, dl1g_ref, dl1b_ref, dgng_ref, dgnb_ref):
                ref[...] = jnp.zeros_like(ref)

        dh1 = dm_ref[...] + ALPHA * dr2_ref[...]
        xh1 = xh1_ref[...]
        dl1g_ref[...] += _colsum(dh1 * xh1)
        dl1b_ref[...] += _colsum(dh1)
        dr1 = _ln_bwd(dh1 * l1g[...], xh1, rs1_ref[...])
        dh0_ref[...] = ALPHA * dr1
        dmix = dr1.astype(MM)
        awo[...] += _dot_tn(yc_ref[...], dmix)
        dyc = _dot_nt(dmix, wo_ref[...])
        dys_ref[...] = dyc[:, 0:S5_W]
        for h in range(RET_H):
            sl = slice(h * HEAD, (h + 1) * HEAD)
            gt = g_ref[:, sl]
            _, xhat, rstd, on, s = _gn_gate(o_ref[:, sl], gt, gng[:, sl], gnb[:, sl])
            dyr = dyc[:, S5_W + h * HEAD:S5_W + (h + 1) * HEAD]
            dg_ref[:, sl] = (dyr * on * (s * (1.0 + gt * (1.0 - s)))).astype(dg_ref.dtype)
            don = dyr * gt * s
            dgng_ref[:, sl] += _colsum(don * xhat)
            dgnb_ref[:, sl] += _colsum(don)
            do_ref[:, sl] = _ln_bwd(don * gng[:, sl], xhat, rstd)

        @pl.when(i == nb - 1)
        def _():
            gwo_ref[...] = awo[...].astype(gwo_ref.dtype)

    row = lambda w: pl.BlockSpec((R, w), lambda i: (i, 0))
    full = lambda a: pl.BlockSpec(a.shape, lambda i: (0,) * a.ndim)
    acc = lambda s, dt=F32: (pl.BlockSpec(s, lambda i: (0, 0)), jax.ShapeDtypeStruct(s, dt))
    outs = [(row(RET_W), jax.ShapeDtypeStruct((tp, RET_W), F32)), (row(RET_W), jax.ShapeDtypeStruct((tp, RET_W), MM)),
            (row(S5_W), jax.ShapeDtypeStruct((tp, S5_W), F32)), (row(D_MODEL), jax.ShapeDtypeStruct((tp, D_MODEL), F32)),
            acc((D_MODEL, D_MODEL), MM), acc((1, D_MODEL)), acc((1, D_MODEL)), acc((1, RET_W)), acc((1, RET_W))]
    return _call(
        body, "post_bwd", (nb,),
        [row(D_MODEL), row(D_MODEL), row(D_MODEL), row(1), row(D_MODEL), row(RET_W), row(RET_W),
         full(gn_g), full(gn_b), full(l1_g), _VMEM],
        [o[0] for o in outs], [o[1] for o in outs],
        [pltpu.VMEM((D_MODEL, D_MODEL), F32)],
        (dh1m, dr2, xhat1, rstd1, ycat, o, gate, gn_g, gn_b, l1_g, w_out), jobs)


_PROJ_SEGS = [(0, S5_W)] + [(S5_W + n * RET_W, S5_W + (n + 1) * RET_W) for n in range(4)]


def _in_w_grad(du, dq, dk, dv, dg, xhat0, li_g, li_b):
    tp = du.shape[0]
    R = PROJ_ROWS if tp % PROJ_ROWS == 0 else ROW_BLK
    nb = tp // R

    def body(du_ref, dq_ref, dk_ref, dv_ref, dg_ref, xh_ref, lig, lib, gw_ref, aw):
        i = pl.program_id(0)

        @pl.when(i == 0)
        def _():
            aw[...] = jnp.zeros_like(aw)

        valid = (i * R + lax.broadcasted_iota(jnp.int32, (R, 1), 0)) >= PAD
        hb = (xh_ref[...] * lig[...] + lib[...]).astype(MM)
        for (lo, hi), ref in zip(_PROJ_SEGS, (du_ref, dq_ref, dk_ref, dv_ref, dg_ref)):
            aw[lo:hi, :] += _dot_tn(jnp.where(valid, ref[...], 0.0).astype(MM), hb)

        @pl.when(i == nb - 1)
        def _():
            gw_ref[...] = aw[...].astype(gw_ref.dtype)

    row = lambda w: pl.BlockSpec((R, w), lambda i: (i, 0))
    full = lambda a: pl.BlockSpec(a.shape, lambda i: (0,) * a.ndim)
    (gw,), _ = _call(
        body, "in_w_grad", (nb,),
        [row(S5_W), row(RET_W), row(RET_W), row(RET_W), row(RET_W), row(D_MODEL), full(li_g), full(li_b)],
        [pl.BlockSpec((PROJ_W, D_MODEL), lambda i: (0, 0))], [jax.ShapeDtypeStruct((PROJ_W, D_MODEL), MM)],
        [pltpu.VMEM((PROJ_W, D_MODEL), F32)], (du, dq, dk, dv, dg, xhat0, li_g, li_b))
    return gw


def _in_bwd(du, dq, dk, dv, dg, dh0r, xhat0, rstd0, li_g, w_int, after):
    tp = du.shape[0]
    R = PROJ_ROWS if tp % PROJ_ROWS == 0 else ROW_BLK
    nb = tp // R
    segs = _PROJ_SEGS

    def body(du_ref, dq_ref, dk_ref, dv_ref, dg_ref, dh0r_ref, xh_ref, rs_ref, lig, w_ref, after_ref,
             gx_ref, dmeta_ref, dlg_ref, dlb_ref, stage, out_sems):
        i = pl.program_id(0)
        slot = i % 2

        def to_gx(step_slot, first):
            if first:
                return pltpu.make_async_copy(stage.at[0, CHUNK:R, :], gx_ref.at[0:R - CHUNK, :], out_sems.at[0])
            return pltpu.make_async_copy(stage.at[step_slot], gx_ref.at[pl.ds(i * R - CHUNK, R), :], out_sems.at[step_slot])

        @pl.when(i == 0)
        def _():
            for ref in (dlg_ref, dlb_ref):
                ref[...] = jnp.zeros_like(ref)

        @pl.when(i >= 3)
        def _():
            to_gx(slot, False).wait()

        valid = (i * R + lax.broadcasted_iota(jnp.int32, (R, 1), 0)) >= PAD
        xh = xh_ref[...]
        dh0 = dh0r_ref[...]
        for (lo, hi), ref in zip(segs, (du_ref, dq_ref, dk_ref, dv_ref, dg_ref)):
            dh0 = dh0 + _dot(jnp.where(valid, ref[...], 0.0).astype(MM), w_ref[lo:hi, :])
        dlg_ref[...] += _colsum(dh0 * xh)
        dlb_ref[...] += _colsum(dh0)
        draw = _ln_bwd(dh0 * lig[...], xh, rs_ref[...])
        stage[slot] = draw

        @pl.when(i == 0)
        def _():
            dmeta_ref[...] = draw[PAD:CHUNK, :]
            first = to_gx(0, True)
            first.start()
            first.wait()

        @pl.when(i > 0)
        def _():
            to_gx(slot, False).start()

        @pl.when(i == nb - 1)
        def _():
            for back in (1, 0):
                if nb - 1 - back >= 1:
                    to_gx((nb - 1 - back) % 2, False).wait()

    row = lambda w: pl.BlockSpec((R, w), lambda i: (i, 0))
    full = lambda a: pl.BlockSpec(a.shape, lambda i: (0,) * a.ndim)
    acc = lambda s, dt=F32: (pl.BlockSpec(s, lambda i: (0, 0)), jax.ShapeDtypeStruct(s, dt))
    outs = [(_ANY, jax.ShapeDtypeStruct((tp - CHUNK, D_MODEL), F32)), acc((N_META, D_MODEL)),
            acc((1, D_MODEL)), acc((1, D_MODEL))]
    return _call(
        body, "in_bwd", (nb,),
        [row(S5_W), row(RET_W), row(RET_W), row(RET_W), row(RET_W), row(D_MODEL), row(D_MODEL), row(1),
         full(li_g), _VMEM, full(after)],
        [o[0] for o in outs], [o[1] for o in outs],
        [pltpu.VMEM((2, R, D_MODEL), F32), pltpu.SemaphoreType.DMA((2,))],
        (du, dq, dk, dv, dg, dh0r, xhat0, rstd0, li_g, w_int, after))[0]


def _place():
    return lax.axis_index("x"), lax.axis_index("y"), lax.axis_index("c")


def _dma_sems(n):
    return pltpu.SemaphoreType.DMA((n,))


def _job_gather(shard):
    def parts(ins, outs, sems):
        (src,), (out,), (send_sems, recv_sems, local_sem) = ins, outs, sems
        x, y, c = _place()
        north = c == 1
        me, sib = (x, y, c), (x, y, 1 - c)
        xn, yn, dg = (1 - x, y, c), (x, 1 - y, c), (1 - x, 1 - y, c)
        relay_from = (jnp.where(north, 1 - x, x), jnp.where(north, y, 1 - y), c)
        relay_to = (jnp.where(north, x, 1 - x), jnp.where(north, 1 - y, y), c)

        def slot(dev):
            return out.at[4 * dev[0] + 2 * dev[1] + dev[2]]

        def copy(k, block, to, from_input=False):
            return pltpu.make_async_remote_copy(
                src_ref=src if from_input else slot(block), dst_ref=slot(block),
                send_sem=send_sems.at[k], recv_sem=recv_sems.at[k], device_id=to, device_id_type=_MESH)

        mine = lambda: pltpu.make_async_copy(src, slot(me), local_sem.at[0])
        first = lambda: [copy(0, me, sib, True), copy(1, me, xn, True), copy(2, me, yn, True)]
        relayed = lambda: [copy(3, relay_from, relay_to), copy(4, xn, sib), copy(5, yn, sib)]
        return me, sib, xn, yn, dg, copy, mine, first, relayed

    def start(ins, outs, sems):
        mine, first = parts(ins, outs, sems)[6:8]
        mine().start()
        for cp in first():
            cp.start()

    def relay(ins, outs, sems):
        me, sib, xn, yn, dg, copy, mine, first, relayed = parts(ins, outs, sems)
        copy(1, xn, me).wait_recv()
        copy(2, yn, me).wait_recv()
        for cp in relayed():
            cp.start()

    def finish(ins, outs, sems):
        me, sib, xn, yn, dg, copy, mine, first, relayed = parts(ins, outs, sems)
        other = 1 - me[2]
        copy(3, dg, me).wait_recv()
        last = copy(6, dg, sib)
        last.start()
        copy(0, sib, me).wait_recv()
        for k, chip in ((4, xn), (5, yn), (6, dg)):
            copy(k, (chip[0], chip[1], other), me).wait_recv()
        for cp in first() + relayed() + [last]:
            cp.wait_send()
        mine().wait()

    return dict(ins=[shard], outs=[jax.ShapeDtypeStruct((N_DEV,) + shard.shape, shard.dtype)],
                sems=[_dma_sems(7), _dma_sems(7), _dma_sems(1)], start=start, middle=relay, finish=finish)


def _job_gather_direct(shard):
    def copies(ins, outs, sems):
        (src,), (out,), (send_sems, recv_sems, local_sem) = ins, outs, sems
        x, y, c = _place()
        flip = lambda a, bit: 1 - a if bit else a
        slot = out.at[4 * x + 2 * y + c]
        return [pltpu.make_async_copy(src, slot, local_sem.at[0])] + [pltpu.make_async_remote_copy(
            src_ref=src, dst_ref=slot, send_sem=send_sems.at[k - 1], recv_sem=recv_sems.at[k - 1],
            device_id=(flip(x, k & 4), flip(y, k & 2), flip(c, k & 1)), device_id_type=_MESH) for k in range(1, N_DEV)]

    def start(ins, outs, sems):
        for cp in copies(ins, outs, sems):
            cp.start()

    def finish(ins, outs, sems):
        for cp in copies(ins, outs, sems):
            cp.wait()

    return dict(ins=[shard], outs=[jax.ShapeDtypeStruct((N_DEV,) + shard.shape, shard.dtype)],
                sems=[_dma_sems(N_DEV - 1), _dma_sems(N_DEV - 1), _dma_sems(1)], start=start, finish=finish)


def _job_pair(g):
    def copies(ins, outs, sems):
        x, y, c = _place()
        return [pltpu.make_async_remote_copy(
            src_ref=ins[0].at[2 * j + (1 - c)], dst_ref=outs[0].at[j], send_sem=sems[0].at[j], recv_sem=sems[1].at[j],
            device_id=(x, y, 1 - c), device_id_type=_MESH) for j in range(4)]

    def start(ins, outs, sems):
        for cp in copies(ins, outs, sems):
            cp.start()

    def finish(ins, outs, sems):
        for cp in copies(ins, outs, sems):
            cp.wait()

    return dict(ins=[g], outs=[jax.ShapeDtypeStruct((4,) + g.shape[1:], g.dtype)], sems=[_dma_sems(4), _dma_sems(4)],
                start=start, finish=finish)


def _job_chips(p):
    def copies(ins, outs, sems):
        x, y, c = _place()
        chips = [(1 - x, y), (x, 1 - y), (1 - x, 1 - y)]
        return [pltpu.make_async_remote_copy(
            src_ref=ins[0].at[2 * chip[0] + chip[1]], dst_ref=outs[0].at[k], send_sem=sems[0].at[k],
            recv_sem=sems[1].at[k], device_id=(*chip, c), device_id_type=_MESH) for k, chip in enumerate(chips)]

    def start(ins, outs, sems):
        for cp in copies(ins, outs, sems):
            cp.start()

    def finish(ins, outs, sems):
        for cp in copies(ins, outs, sems):
            cp.wait()

    return dict(ins=[p], outs=[jax.ShapeDtypeStruct((3,) + p.shape[1:], p.dtype)], sems=[_dma_sems(3), _dma_sems(3)],
                start=start, finish=finish)


_HBM = pl.BlockSpec(memory_space=pltpu.HBM)
_SEM = pl.BlockSpec(memory_space=pltpu.SEMAPHORE)
_ORDERED = pltpu.CompilerParams(has_side_effects=pltpu.SideEffectType.DATAFLOW_SIDE_EFFECTING)


def _chip_copies(p_ref, land_ref, sems):
    x, y, c = _place()
    chips = [(1 - x, y), (x, 1 - y), (1 - x, 1 - y)]
    return [pltpu.make_async_remote_copy(
        src_ref=p_ref.at[2 * chip[0] + chip[1]], dst_ref=land_ref.at[k], send_sem=sems[k], recv_sem=sems[3 + k],
        device_id=(*chip, c), device_id_type=_MESH) for k, chip in enumerate(chips)]


def _chips_start(ps, name):
    n = len(ps)

    def body(*refs):
        sems = refs[2 * n:8 * n]
        for a in range(n):
            for cp in _chip_copies(refs[a], refs[n + a], sems[6 * a:6 * a + 6]):
                cp.start()
        refs[-1][...] = jnp.zeros_like(refs[-1])

    lands = [(3,) + p.shape[1:] for p in ps]
    hbm = lambda arr: pltpu.with_memory_space_constraint(arr, pltpu.HBM)
    outs = pl.pallas_call(
        body, name=name,
        out_shape=(*[pltpu.SemaphoreType.DMA(())] * (6 * n), *[pltpu.HBM(p.shape, p.dtype) for p in ps],
                   *[pltpu.HBM(s, p.dtype) for s, p in zip(lands, ps)], jax.ShapeDtypeStruct((8, LANE), F32)),
        in_specs=[_HBM] * (2 * n), out_specs=(*[_SEM] * (6 * n), *[_HBM] * (2 * n), _VMEM),
        input_output_aliases={a: 6 * n + a for a in range(2 * n)}, compiler_params=_ORDERED,
    )(*[hbm(p) for p in ps], *[hbm(lax.empty(s, p.dtype)) for s, p in zip(lands, ps)])
    return (list(outs[:6 * n]), list(outs[6 * n:7 * n]), list(outs[7 * n:8 * n])), outs[8 * n]


def _chips_wait(started, after, name):
    sems, thrus, lands = started
    n = len(thrus)

    def body(*refs):
        for a in range(n):
            for cp in _chip_copies(refs[a], refs[n + a], refs[2 * n + 6 * a:2 * n + 6 * a + 6]):
                cp.wait_send()
                cp.wait_recv()

    outs = pl.pallas_call(
        body, name=name, out_shape=[pltpu.HBM(t.shape, t.dtype) for t in thrus + lands],
        in_specs=(*[_HBM] * (2 * n), *[_SEM] * (6 * n), _ANY), out_specs=[_HBM] * (2 * n),
        input_output_aliases={a: a for a in range(2 * n)}, compiler_params=_ORDERED,
    )(*thrus, *lands, *sems, after)
    return list(outs[:n]), list(outs[n:])


def _split_job_refs(jobs, ins, outs, sems):
    res, a, b, c = [], 0, 0, 0
    for job in jobs:
        na, nb, nc = len(job["ins"]), len(job["outs"]), len(job["sems"])
        res.append((ins[a:a + na], outs[b:b + nb], sems[c:c + nc]))
        a, b, c = a + na, b + nb, c + nc
    return res


def _call(body, name, grid, in_specs, out_specs, out_shape, scratch, args, jobs=(), prefetch=None, early=0):
    jobs = list(jobs)
    n_in, n_out, n_scr = len(in_specs), len(out_specs), len(scratch)
    j_in = [a for job in jobs for a in job["ins"]]
    j_out = [o for job in jobs for o in job["outs"]]
    j_scr = [s for job in jobs for s in job["sems"]]
    nsteps = grid[0]
    n_pre = 0 if prefetch is None else 1

    def wrapped(*refs):
        pre, refs = refs[:n_pre], refs[n_pre:]
        ins, jins = refs[:n_in], refs[n_in:n_in + len(j_in)]
        refs = refs[n_in + len(j_in):]
        outs, jouts = refs[:n_out], refs[n_out:n_out + len(j_out)]
        refs = refs[n_out + len(j_out):]
        scr, jscr = refs[:n_scr], refs[n_scr:]
        per_job = _split_job_refs(jobs, jins, jouts, jscr)

        def middle():
            for job, r in zip(jobs, per_job):
                if "middle" in job:
                    job["middle"](*r)

        @pl.when(pl.program_id(0) == 0)
        def _():
            for job, r in zip(jobs, per_job):
                job["start"](*r)

        if nsteps >= 3:
            pl.when(pl.program_id(0) == (2 * nsteps) // 3)(middle)

        if early:
            @pl.when(pl.program_id(0) == nsteps - 1)
            def _():
                for job, r in zip(jobs[:early], per_job[:early]):
                    job["finish"](*r)

        body(*pre, *ins, *outs, *scr, *[o for r in per_job[:early] for o in r[1]])

        @pl.when(pl.program_id(0) == nsteps - 1)
        def _():
            if nsteps < 3:
                middle()
            for job, r in zip(jobs[early:], per_job[early:]):
                job["finish"](*r)

    specs = dict(in_specs=list(in_specs) + [_ANY] * len(j_in), out_specs=list(out_specs) + [_ANY] * len(j_out),
                 scratch_shapes=list(scratch) + j_scr)
    if n_pre:
        specs = dict(grid_spec=pltpu.PrefetchScalarGridSpec(num_scalar_prefetch=1, grid=grid, **specs))
    else:
        specs["grid"] = grid
    res = pl.pallas_call(
        wrapped if jobs else body, name=name, out_shape=list(out_shape) + j_out,
        compiler_params=_params(("arbitrary",) * len(grid)), **specs,
    )(*([prefetch] if n_pre else []), *args, *j_in)
    return list(res[:n_out]), list(res[n_out:])


def _exchange(jobs, name):
    j_in = [a for job in jobs for a in job["ins"]]
    j_out = [o for job in jobs for o in job["outs"]]
    j_scr = [s for job in jobs for s in job["sems"]]

    def body(*refs):
        per_job = _split_job_refs(jobs, refs[:len(j_in)], refs[len(j_in):len(j_in) + len(j_out)],
                                  refs[len(j_in) + len(j_out):])
        for phase in ("start", "middle", "finish"):
            for job, r in zip(jobs, per_job):
                if phase in job:
                    job[phase](*r)

    return pl.pallas_call(body, name=name, out_shape=j_out, in_specs=[_ANY] * len(j_in), out_specs=[_ANY] * len(j_out),
                          scratch_shapes=j_scr)(*j_in)


def _pair_sum(gs, r1s, c_arr, name):
    n = len(gs)

    def body(c_ref, *refs):
        for a in range(n):
            refs[2 * n + a][...] = (refs[a][...].astype(F32) + refs[n + a][...].astype(F32)).astype(refs[2 * n + a].dtype)

    def blk(g, own):
        s = g.shape[1:]
        if own:
            return pl.BlockSpec((1,) + s, lambda j, c_ref: (2 * j + c_ref[0],) + (0,) * len(s))
        return pl.BlockSpec((1,) + s, lambda j, c_ref: (j,) + (0,) * len(s))

    return pl.pallas_call(
        body, name=name,
        grid_spec=pltpu.PrefetchScalarGridSpec(
            num_scalar_prefetch=1, grid=(4,),
            in_specs=[blk(g, True) for g in gs] + [blk(g, False) for g in gs],
            out_specs=[blk(g, False) for g in gs]),
        out_shape=[jax.ShapeDtypeStruct((4,) + g.shape[1:], g.dtype) for g in gs],
        compiler_params=_params(("arbitrary",)),
    )(c_arr, *gs, *r1s)


def _adamw_math(w, g, m, v):
    m = ADAM_B1 * m + (1.0 - ADAM_B1) * g
    v = ADAM_B2 * v + (1.0 - ADAM_B2) * (g * g)
    m_hat = m / (1.0 - ADAM_B1 ** ADAM_STEP)
    v_hat = v / (1.0 - ADAM_B2 ** ADAM_STEP)
    return -ADAM_LR * (m_hat / (jnp.sqrt(v_hat) + ADAM_EPS) + ADAM_WD * w), m, v


def _view(name, a):
    return jnp.swapaxes(a, -1, -2) if name in ("w_in", "s5_b_re", "s5_b_im") else a


def _adamw_shards(items, name, steps, chip, jobs=()):
    n = len(items)

    def body(chip_ref, *refs):
        for a in range(n):
            p_ref, r_ref, w_ref, m_ref, v_ref = refs[5 * a:5 * a + 5]
            g = ((p_ref[0].astype(F32) + r_ref[0].astype(F32)) + r_ref[1].astype(F32)) + r_ref[2].astype(F32)
            outs = refs[5 * n + 4 * a:5 * n + 4 * a + 4]
            outs[0][...] = g
            outs[1][...], outs[2][...], outs[3][...] = _adamw_math(w_ref[...], g, m_ref[...], v_ref[...])

    in_specs, out_specs, out_shape, flat = [], [], [], []
    for p, r, w, m, v in items:
        rows, cols = w.shape
        rb = rows // steps
        in_specs += [pl.BlockSpec((1, rb, cols), lambda i, c: (c[0], i, 0)), pl.BlockSpec((3, rb, cols), lambda i, c: (0, i, 0))]
        wblk = pl.BlockSpec((rb, cols), lambda i, c: (i, 0))
        in_specs += [wblk] * 3
        out_specs += [wblk] * 4
        out_shape += [jax.ShapeDtypeStruct(w.shape, F32)] * 4
        flat += [p, r, w, m, v]
    return _call(body, name, (steps,), in_specs, out_specs, out_shape, [], flat, jobs, prefetch=chip)


def _sum_devices(gathered, name):
    def body(gs_ref, g_ref):
        g = gs_ref[0]
        for s in range(1, N_DEV):
            g = g + gs_ref[s]
        g_ref[...] = g

    return pl.pallas_call(body, name=name, out_shape=jax.ShapeDtypeStruct(gathered.shape[1:], F32),
                          in_specs=[_VMEM], out_specs=_VMEM, compiler_params=_params())(gathered)


def _adamw_native(items, name):
    n = len(items)

    def body(*refs):
        for a in range(n):
            g, w, m, v = (refs[4 * a + t][...] for t in range(4))
            refs[4 * n + 3 * a][...], refs[4 * n + 3 * a + 1][...], refs[4 * n + 3 * a + 2][...] = _adamw_math(w, g, m, v)

    return pl.pallas_call(
        body, name=name, out_shape=[jax.ShapeDtypeStruct(it[1].shape, F32) for it in items for _ in range(3)],
        in_specs=[_VMEM] * (4 * n), out_specs=[_VMEM] * (3 * n), compiler_params=_params(),
    )(*[t for it in items for t in it])


SMALL = ["ln_in_g", "ln_in_b", "s5_lambda_re", "s5_lambda_im", "s5_log_dt", "s5_b_re", "s5_b_im", "s5_c_re", "s5_c_im",
         "s5_d", "s5_b_glu", "ret_gn_g", "ret_gn_b", "ln1_g", "ln1_b", "ln2_g", "ln2_b"]
LATE = ["ln_in_g", "ln_in_b", "meta_tokens"]
EARLY = [n for n in SMALL if n not in LATE] + ["s5_w_glu", "loss"]
LANE = 128


def _pack(arrs):
    parts = []
    for a in arrs:
        f = a.reshape(-1)
        parts.append(jnp.pad(f, (0, (-f.shape[0]) % LANE)))
    flat = jnp.concatenate(parts)
    rows = -(-flat.shape[0] // LANE)
    flat = jnp.pad(flat, (0, (-rows % 8) * LANE + rows * LANE - flat.shape[0]))
    return flat.reshape(-1, LANE)


def _unpack(packed, shapes):
    flat = packed.reshape(-1)
    out, off = [], 0
    for s in shapes:
        n = math.prod(s)
        out.append(flat[off:off + n].reshape(s))
        off += n + (-n) % LANE
    return out


def _rope_tables(tp):
    inv_freq = 1.0 / (ROPE_BASE ** (jnp.arange(0, HEAD, 2, dtype=F32) / HEAD))
    blk = (jnp.arange(tp // ROW_BLK, dtype=F32) * ROW_BLK)[:, None, None] * inv_freq
    off = (jnp.arange(ROW_BLK, dtype=F32) - float(PAD))[None, :, None] * inv_freq
    cos = (jnp.cos(blk) * jnp.cos(off) - jnp.sin(blk) * jnp.sin(off)).reshape(tp, HEAD // 2)
    sin = (jnp.sin(blk) * jnp.cos(off) + jnp.cos(blk) * jnp.sin(off)).reshape(tp, HEAD // 2)
    return jnp.concatenate([cos, cos], axis=1), jnp.concatenate([-sin, sin], axis=1)


def _local_step(x2d, tgt, meta, w_int, w_out, w_up, w_down, w_glu, sp, distributed):
    tp = x2d.shape[0] + CHUNK
    row = lambda a: a.reshape(1, -1)
    cos2, sin2 = _rope_tables(tp)
    li_g, li_b = row(sp["ln_in_g"]), row(sp["ln_in_b"])
    l1_g, l1_b, l2_g, l2_b = row(sp["ln1_g"]), row(sp["ln1_b"]), row(sp["ln2_g"]), row(sp["ln2_b"])
    gn_g, gn_b = row(sp["ret_gn_g"]), row(sp["ret_gn_b"])
    lre, lim = row(sp["s5_lambda_re"]), row(sp["s5_lambda_im"])
    ldt = row(jnp.repeat(sp["s5_log_dt"].reshape(-1), S5_P))
    to_t = lambda b: b.reshape(S5_G, S5_P, S5_H).transpose(2, 0, 1).reshape(S5_H, S5_N)
    bre_t, bim_t = to_t(sp["s5_b_re"]), to_t(sp["s5_b_im"])
    to_w = lambda c: jnp.tile(c.reshape(S5_W, S5_P), (1, 2))
    cre_w, cim_w = to_w(sp["s5_c_re"]), to_w(sp["s5_c_im"])

    jobs = (lambda *j: list(j)) if distributed else (lambda *j: [])
    c_arr = jnp.reshape(lax.axis_index("c"), (1,)).astype(jnp.int32) if distributed else None
    (xhat0, rstd0), bg = _ln_in(x2d, meta, jobs(_job_gather(w_int) if distributed else None), gather_meta=distributed)
    if distributed:
        w_int = bg[1].reshape(PROJ_W, D_MODEL)
    (u, q, k, v, gate), bg = _in_proj(xhat0, li_g, li_b, w_int, cos2, sin2,
                                      jobs(*([_job_gather(w_out), _job_gather(w_glu)] if distributed else [])))
    if distributed:
        w_out, w_glu = bg[0].reshape(D_MODEL, D_MODEL), bg[1].reshape(S5_W, S5_W)
    s5_small = (lre, lim, ldt, bre_t, bim_t, cre_w, cim_w, row(sp["s5_d"]), w_glu, row(sp["s5_b_glu"]))
    (ys5, xr, xi), bg = _s5_fwd(u, *s5_small, jobs=jobs(_job_gather(w_up) if distributed else None))
    if distributed:
        w_up = bg[0]
    (o, states), _ = _ret_fwd(q, k, v)
    (ycat, xhat1, rstd1, h1b, pre), bg = _post_up(o, gate, ys5, xhat0, gn_g, gn_b, li_g, li_b, l1_g, l1_b, w_out, w_up,
                                                  jobs(_job_gather(w_down) if distributed else None))
    if distributed:
        w_down = bg[0].reshape(D_FF, D_MODEL)
    dr2, dffb, loss8, dl2g, dl2b = _post_down(pre, xhat1, tgt, l1_g, l1_b, l2_g, l2_b, w_down)
    g_up, g_down, dh1m = _mlp_bwd(h1b, dffb, pre, w_up, w_down)
    (do, dgate, dys5, dh0r, g_out, dl1g, dl1b, dgng, dgnb), bg = _post_bwd(
        dh1m, dr2, xhat1, rstd1, ycat, o, gate, gn_g, gn_b, l1_g, w_out,
        jobs(*([_job_pair(g_up), _job_pair(g_down)] if distributed else [])))
    g_out = g_out.reshape(N_DEV, D_MODEL // N_DEV, D_MODEL)
    after = jnp.zeros((8, LANE), F32)
    if distributed:
        p_up, p_down = _pair_sum([g_up, g_down], bg, c_arr, "pair_sum_mlp")
        started_mlp, after = _chips_start([p_up, p_down], "chips_mlp_start")
    (du, dlre, dlim, dldt, dbre_t, dbim_t, dcre, dcim, dd, dwglu, dbglu), bg = _s5_bwd(
        dys5, u, xr, xi, *s5_small, after, jobs=jobs(_job_pair(g_out) if distributed else None))
    if distributed:
        (p_out,) = _pair_sum([g_out], bg, c_arr, "pair_sum_out")
    from_t = lambda t: t.reshape(S5_H, S5_G, S5_P).transpose(1, 0, 2)
    small = {
        "s5_lambda_re": dlre, "s5_lambda_im": dlim, "s5_log_dt": dldt[:, :S5_G],
        "s5_b_re": from_t(dbre_t), "s5_b_im": from_t(dbim_t), "s5_c_re": dcre, "s5_c_im": dcim, "s5_d": dd,
        "s5_b_glu": dbglu, "ret_gn_g": dgng, "ret_gn_b": dgnb, "ln1_g": dl1g, "ln1_b": dl1b, "ln2_g": dl2g, "ln2_b": dl2b,
        "s5_w_glu": dwglu, "loss": loss8[0:1, 0:1]}
    early_pack = _pack([small[n] for n in EARLY])
    (dq, dk, dv), bg = _ret_bwd(q, k, v, do, states, cos2, sin2,
                                jobs(*([_job_chips(p_out), _job_gather(early_pack)] if distributed else [])))
    g_int = _in_w_grad(du, dq, dk, dv, dgate, xhat0, li_g, li_b).reshape(N_DEV, PROJ_W // N_DEV, D_MODEL)
    after = jnp.zeros((8, LANE), F32)
    if distributed:
        (r1_in,) = _exchange([_job_pair(g_int)], "exchange_pair_in")
        (p_in,) = _pair_sum([g_int], [r1_in], c_arr, "pair_sum_in")
        (p_up, p_down), (r_up, r_down) = _chips_wait(started_mlp, p_in, "chips_mlp_wait")
        started_in, after = _chips_start([p_in], "chips_in_start")
    grad_x, dmeta, dlig, dlib = _in_bwd(du, dq, dk, dv, dgate, dh0r, xhat0, rstd0, li_g, w_int, after)
    small.update(ln_in_g=dlig, ln_in_b=dlib, meta_tokens=dmeta)
    if distributed:
        (p_in,), (r_in,) = _chips_wait(started_in, dlig, "chips_in_wait")
        big = dict(chip_sums=[p_in, p_out, p_up, p_down], received=[r_in, bg[0], r_up, r_down], early=bg[1])
    else:
        big = dict(partials=[g_int, g_out, g_up, g_down])
    return grad_x, big, small


def kernel(x, meta_tokens, ln_in_g, ln_in_b, w_in, s5_lambda_re, s5_lambda_im, s5_log_dt, s5_b_re, s5_b_im, s5_c_re, s5_c_im, s5_d, s5_w_glu, s5_b_glu, ret_gn_g, ret_gn_b, w_out, ln1_g, ln1_b, w_up, w_down, ln2_g, ln2_b, loss_target, m_meta_tokens, m_ln_in_g, m_ln_in_b, m_w_in, m_s5_lambda_re, m_s5_lambda_im, m_s5_log_dt, m_s5_b_re, m_s5_b_im, m_s5_c_re, m_s5_c_im, m_s5_d, m_s5_w_glu, m_s5_b_glu, m_ret_gn_g, m_ret_gn_b, m_w_out, m_ln1_g, m_ln1_b, m_w_up, m_w_down, m_ln2_g, m_ln2_b, v_meta_tokens, v_ln_in_g, v_ln_in_b, v_w_in, v_s5_lambda_re, v_s5_lambda_im, v_s5_log_dt, v_s5_b_re, v_s5_b_im, v_s5_c_re, v_s5_c_im, v_s5_d, v_s5_w_glu, v_s5_b_glu, v_ret_gn_g, v_ret_gn_b, v_w_out, v_ln1_g, v_ln1_b, v_w_up, v_w_down, v_ln2_g, v_ln2_b):
    args = dict(locals())
    names = ["meta_tokens", "ln_in_g", "ln_in_b", "w_in", "s5_lambda_re", "s5_lambda_im", "s5_log_dt", "s5_b_re", "s5_b_im",
             "s5_c_re", "s5_c_im", "s5_d", "s5_w_glu", "s5_b_glu", "ret_gn_g", "ret_gn_b", "w_out", "ln1_g", "ln1_b",
             "w_up", "w_down", "ln2_g", "ln2_b"]
    ax, ay, ac = _place()
    me = 4 * ax + 2 * ay + ac

    sp = {n: args[n] for n in SMALL}
    grad_x, big, small = _local_step(x[0], loss_target[0], meta_tokens, w_in[0].T.astype(MM), w_out[0].astype(MM),
                                   w_up[0].astype(MM), w_down[0].astype(MM), s5_w_glu[0].astype(MM), sp, True)

    j_arr = jnp.reshape(2 * ax + ay, (1,)).astype(jnp.int32)
    two_d = lambda a: a.reshape(a.shape[-2:])
    item = lambda n, p, r: (p, r, *(two_d(_view(n, a)) for a in (args[n], args["m_" + n], args["v_" + n])))
    late_pack = _pack([small[n] for n in LATE])
    mlp = ("w_out", "w_up", "w_down")
    (late_all,) = _exchange([_job_gather_direct(late_pack)], "gather_small_late")
    res, _ = _adamw_shards(
        [item(n, p, r) for n, p, r in zip(mlp, big["chip_sums"][1:], big["received"][1:])], "adamw_mlp", 8, j_arr)
    res_in, _ = _adamw_shards([item("w_in", big["chip_sums"][0], big["received"][0])], "adamw_in", 2, j_arr)
    upd = {"w_in": res_in}
    for idx, n in enumerate(mlp):
        upd[n] = res[4 * idx:4 * idx + 4]
    shard_grads = {n: upd[n][0] for n in upd}

    early_shapes = [_view(n, args[n]).shape for n in EARLY[:-2]] + [(S5_W, S5_W), (1,)]
    late_shapes = [args["ln_in_g"].shape, args["ln_in_b"].shape, (N_META, D_MODEL)]
    g_small = dict(zip(EARLY, _unpack(_sum_devices(big["early"], "sum_small_early"), early_shapes)))
    g_small.update(zip(LATE, _unpack(_sum_devices(late_all, "sum_small_late"), late_shapes)))
    loss = g_small["loss"].reshape(())

    shard_grads["meta_tokens"] = lax.dynamic_slice(g_small["meta_tokens"], (0, me * (D_MODEL // N_DEV)),
                                                   (N_META, D_MODEL // N_DEV))
    shard_grads["s5_w_glu"] = lax.dynamic_slice(g_small["s5_w_glu"], (me * (S5_W // N_DEV), 0),
                                                (S5_W // N_DEV, S5_W))[None]
    natives = SMALL + ["meta_tokens", "s5_w_glu"]
    res2 = _adamw_native([(shard_grads[n] if n in shard_grads else g_small[n], *(_view(n, args[p + n]) for p in ("", "m_", "v_")))
                          for n in natives], "adamw_small")
    for idx, n in enumerate(natives):
        upd[n] = [shard_grads[n] if n in shard_grads else g_small[n]] + list(res2[3 * idx:3 * idx + 3])

    grads, deltas, new_m, new_v = ([_view(n, upd[n][t]).reshape(args[n].shape) for n in names] for t in range(4))
    return (loss, grad_x[None], *grads, *deltas, *new_m, *new_v)
```
